```python
import math
import jax, jax.numpy as jnp
from jax import lax
import numpy as np

D_MODEL = 2048
BATCH = 8
SEQ = 2048
DEPTH = 1

D_MIX = D_MODEL
HEAD_DIM = 64
D_ATTN = D_MIX // 2
N_Q_HEADS = D_ATTN // HEAD_DIM
N_KV_HEADS = 2
Q_PER_KV = N_Q_HEADS // N_KV_HEADS
D_KV = N_KV_HEADS * HEAD_DIM
WINDOW = 128
BLOCK = WINDOW
D_SGU = D_MIX - D_ATTN
SGU_GROUPS = 8
SGU_GROUP_DIM = D_SGU // SGU_GROUPS
CHUNK = 128
D_IN = D_ATTN + 2 * D_KV + D_ATTN + 3 * D_SGU
EPS = 1e-6

kernel_name = "hybrid_swa_sink_gmlp_parallel_heads"


def rms_norm(x, g):
    xf = x.astype(jnp.float32)
    y = xf * lax.rsqrt(jnp.mean(xf * xf, axis=-1, keepdims=True) + EPS)
    return (y * g.astype(jnp.float32)).astype(x.dtype)


def layer_norm(x, g, b):
    xf = x.astype(jnp.float32)
    mu = jnp.mean(xf, axis=-1, keepdims=True)
    var = jnp.mean(jnp.square(xf - mu), axis=-1, keepdims=True)
    y = (xf - mu) * lax.rsqrt(var + EPS)
    return (y * g.astype(jnp.float32) + b.astype(jnp.float32)).astype(x.dtype)


def sliding_window_sink_attention(q, k, v, sinks):
    B, S = q.shape[0], q.shape[1]
    nb = S // BLOCK
    qb = q.reshape(B, nb, BLOCK, N_KV_HEADS, Q_PER_KV, HEAD_DIM)
    kb = k.reshape(B, nb, BLOCK, N_KV_HEADS, HEAD_DIM)
    vb = v.reshape(B, nb, BLOCK, N_KV_HEADS, HEAD_DIM)
    pad = ((0, 0), (1, 0), (0, 0), (0, 0), (0, 0))
    k_ext = jnp.concatenate([jnp.pad(kb, pad)[:, :-1], kb], axis=2)
    v_ext = jnp.concatenate([jnp.pad(vb, pad)[:, :-1], vb], axis=2)
    scale = 1.0 / math.sqrt(HEAD_DIM)
    scores = jnp.einsum('bnqhgd,bnshd->bnhgqs', qb, k_ext).astype(jnp.float32) * scale
    qpos = jnp.arange(BLOCK)[:, None] + BLOCK
    kpos = jnp.arange(2 * BLOCK)[None, :]
    dist = qpos - kpos
    band = (dist >= 0) & (dist < WINDOW)
    has_prev = (jnp.arange(nb) > 0)[:, None, None] | (kpos >= BLOCK)[None]
    valid = band[None] & has_prev
    scores = jnp.where(valid[None, :, None, None], scores, -jnp.inf)
    sink = sinks.astype(jnp.float32).reshape(N_KV_HEADS, Q_PER_KV)
    sink = jnp.broadcast_to(sink[None, None, :, :, None, None], scores.shape[:-1] + (1,))
    probs = jax.nn.softmax(jnp.concatenate([scores, sink], axis=-1), axis=-1)[..., :-1]
    out = jnp.einsum('bnhgqs,bnshd->bnqhgd', probs.astype(v.dtype), v_ext)
    return out.reshape(B, S, N_Q_HEADS * HEAD_DIM)


def chunked_spatial_gating(u, v, ln_g, ln_b, w_s, b_s):
    B, S = u.shape[0], u.shape[1]
    nc = S // CHUNK
    vn = layer_norm(v, ln_g, ln_b).reshape(B, nc, CHUNK, SGU_GROUPS, SGU_GROUP_DIM)
    causal = jnp.tril(jnp.ones((CHUNK, CHUNK), dtype=w_s.dtype))
    w = w_s * causal[None]
    mixed = jnp.einsum('gts,bnsgc->bntgc', w, vn) + b_s.T[None, None, :, :, None]
    return u * mixed.reshape(B, S, D_SGU)


def _fwd_setup_inputs(seed: int = 0) -> dict:
    key = jax.random.key(seed)
    ks = jax.random.split(key, 14)
    f32 = jnp.float32
    x = jax.random.normal(ks[0], (BATCH, SEQ, D_MODEL), f32)
    c = jax.random.normal(ks[1], (BATCH, D_MODEL), f32)
    norm_g = 1.0 + 0.05 * jax.random.normal(ks[2], (DEPTH, D_MODEL), f32)
    w_ada = 0.5 * D_MODEL ** -0.5 * jax.random.normal(ks[3], (DEPTH, D_MODEL, 3 * D_MODEL), f32)
    b_ada = 0.02 * jax.random.normal(ks[4], (DEPTH, 3 * D_MODEL), f32)
    w_in = D_MODEL ** -0.5 * jax.random.normal(ks[5], (DEPTH, D_MODEL, D_IN), f32)
    attn_sinks = 0.5 * jax.random.normal(ks[6], (DEPTH, N_Q_HEADS), f32)
    sgu_ln_g = 1.0 + 0.05 * jax.random.normal(ks[7], (DEPTH, D_SGU), f32)
    sgu_ln_b = 0.02 * jax.random.normal(ks[8], (DEPTH, D_SGU), f32)
    sgu_w = CHUNK ** -0.5 * jax.random.normal(ks[9], (DEPTH, SGU_GROUPS, CHUNK, CHUNK), f32)
    sgu_b = 1.0 + 0.1 * jax.random.normal(ks[10], (DEPTH, SGU_GROUPS, CHUNK), f32)
    w_out = D_MIX ** -0.5 * jax.random.normal(ks[11], (DEPTH, D_MIX, D_MODEL), f32)
    final_g = 1.0 + 0.05 * jax.random.normal(ks[12], (D_MODEL,), f32)
    return {"x": x, "c": c, "norm_g": norm_g, "w_ada": w_ada, "b_ada": b_ada,
            "w_in": w_in, "attn_sinks": attn_sinks, "sgu_ln_g": sgu_ln_g,
            "sgu_ln_b": sgu_ln_b, "sgu_w": sgu_w, "sgu_b": sgu_b, "w_out": w_out,
            "final_g": final_g}


def _fwd_reference(x, c, norm_g, w_ada, b_ada, w_in, attn_sinks, sgu_ln_g, sgu_ln_b,
              sgu_w, sgu_b, w_out, final_g):
    B, S = x.shape[0], x.shape[1]
    splits = np.cumsum([D_ATTN, D_KV, D_KV, D_ATTN, D_SGU, D_SGU])
    c_act = jax.nn.silu(c)
    for l in range(DEPTH):
        mod = c_act @ w_ada[l] + b_ada[l]
        shift, scale, gate = jnp.split(mod, 3, axis=-1)
        h = rms_norm(x, norm_g[l]) * (1.0 + scale[:, None, :]) + shift[:, None, :]
        z = h @ w_in[l]
        q, k, v, g_attn, u, v_s, g_sgu = jnp.split(z, splits, axis=-1)
        attn = sliding_window_sink_attention(
            q.reshape(B, S, N_Q_HEADS, HEAD_DIM),
            k.reshape(B, S, N_KV_HEADS, HEAD_DIM),
            v.reshape(B, S, N_KV_HEADS, HEAD_DIM),
            attn_sinks[l]) * jax.nn.silu(g_attn)
        sgu = chunked_spatial_gating(u, v_s, sgu_ln_g[l], sgu_ln_b[l],
                                     sgu_w[l], sgu_b[l]) * jax.nn.silu(g_sgu)
        y = jnp.concatenate([attn, sgu], axis=-1) @ w_out[l]
        x = x + gate[:, None, :] * y
    return rms_norm(x, final_g)


import jax as _jax
import jax.numpy as _jnp

TWIN_FORMAT = 'train_step'
FWD_PARAMS = ['x', 'c', 'norm_g', 'w_ada', 'b_ada', 'w_in', 'attn_sinks', 'sgu_ln_g', 'sgu_ln_b', 'sgu_w', 'sgu_b', 'w_out', 'final_g']
TWIN_WEIGHTS = ['norm_g', 'w_ada', 'b_ada', 'w_in', 'attn_sinks', 'sgu_ln_g', 'sgu_ln_b', 'sgu_w', 'sgu_b', 'w_out', 'final_g']
TWIN_DIFF_INPUT = 'x'
TWIN_INPUTS = ['x', 'c', 'norm_g', 'w_ada', 'b_ada', 'w_in', 'attn_sinks', 'sgu_ln_g', 'sgu_ln_b', 'sgu_w', 'sgu_b', 'w_out', 'final_g', 'loss_target', 'm_norm_g', 'm_w_ada', 'm_b_ada', 'm_w_in', 'm_attn_sinks', 'm_sgu_ln_g', 'm_sgu_ln_b', 'm_sgu_w', 'm_sgu_b', 'm_w_out', 'm_final_g', 'v_norm_g', 'v_w_ada', 'v_b_ada', 'v_w_in', 'v_attn_sinks', 'v_sgu_ln_g', 'v_sgu_ln_b', 'v_sgu_w', 'v_sgu_b', 'v_w_out', 'v_final_g']
TWIN_OUTPUTS = ['loss', 'grad_x', 'grad_norm_g', 'grad_w_ada', 'grad_b_ada', 'grad_w_in', 'grad_attn_sinks', 'grad_sgu_ln_g', 'grad_sgu_ln_b', 'grad_sgu_w', 'grad_sgu_b', 'grad_w_out', 'grad_final_g', 'delta_norm_g', 'delta_w_ada', 'delta_b_ada', 'delta_w_in', 'delta_attn_sinks', 'delta_sgu_ln_g', 'delta_sgu_ln_b', 'delta_sgu_w', 'delta_sgu_b', 'delta_w_out', 'delta_final_g', 'new_m_norm_g', 'new_m_w_ada', 'new_m_b_ada', 'new_m_w_in', 'new_m_attn_sinks', 'new_m_sgu_ln_g', 'new_m_sgu_ln_b', 'new_m_sgu_w', 'new_m_sgu_b', 'new_m_w_out', 'new_m_final_g', 'new_v_norm_g', 'new_v_w_ada', 'new_v_b_ada', 'new_v_w_in', 'new_v_attn_sinks', 'new_v_sgu_ln_g', 'new_v_sgu_ln_b', 'new_v_sgu_w', 'new_v_sgu_b', 'new_v_w_out', 'new_v_final_g']
TWIN_LEAF_KINDS = {'loss': 'loss', 'grad_x': 'grad_x', 'grad_norm_g': 'grad_w', 'grad_w_ada': 'grad_w', 'grad_b_ada': 'grad_w', 'grad_w_in': 'grad_w', 'grad_attn_sinks': 'grad_w', 'grad_sgu_ln_g': 'grad_w', 'grad_sgu_ln_b': 'grad_w', 'grad_sgu_w': 'grad_w', 'grad_sgu_b': 'grad_w', 'grad_w_out': 'grad_w', 'grad_final_g': 'grad_w', 'delta_norm_g': 'delta_w', 'delta_w_ada': 'delta_w', 'delta_b_ada': 'delta_w', 'delta_w_in': 'delta_w', 'delta_attn_sinks': 'delta_w', 'delta_sgu_ln_g': 'delta_w', 'delta_sgu_ln_b': 'delta_w', 'delta_sgu_w': 'delta_w', 'delta_sgu_b': 'delta_w', 'delta_w_out': 'delta_w', 'delta_final_g': 'delta_w', 'new_m_norm_g': 'new_m', 'new_m_w_ada': 'new_m', 'new_m_b_ada': 'new_m', 'new_m_w_in': 'new_m', 'new_m_attn_sinks': 'new_m', 'new_m_sgu_ln_g': 'new_m', 'new_m_sgu_ln_b': 'new_m', 'new_m_sgu_w': 'new_m', 'new_m_sgu_b': 'new_m', 'new_m_w_out': 'new_m', 'new_m_final_g': 'new_m', 'new_v_norm_g': 'new_v', 'new_v_w_ada': 'new_v', 'new_v_b_ada': 'new_v', 'new_v_w_in': 'new_v', 'new_v_attn_sinks': 'new_v', 'new_v_sgu_ln_g': 'new_v', 'new_v_sgu_ln_b': 'new_v', 'new_v_sgu_w': 'new_v', 'new_v_sgu_b': 'new_v', 'new_v_w_out': 'new_v', 'new_v_final_g': 'new_v'}


def _forward(args):
    return _fwd_reference(*[args[k] for k in FWD_PARAMS])


def _output_shape():
    out = _jax.eval_shape(lambda: _forward(_fwd_setup_inputs(0)))
    return out.shape, out.dtype

N_MICROBATCH = 1
ADAM_LR = 0.001
ADAM_B1 = 0.9
ADAM_B2 = 0.999
ADAM_EPS = 1e-08
ADAM_WD = 0.01
ADAM_STEP = 10
PER_EXAMPLE_BATCH_AXIS = {'x': 0, 'c': 0, 'loss_target': 0}
SHARED_INPUTS = []
_WEIGHT_DTYPES = {'norm_g': _jnp.float32, 'w_ada': _jnp.float32, 'b_ada': _jnp.float32, 'w_in': _jnp.float32, 'attn_sinks': _jnp.float32, 'sgu_ln_g': _jnp.float32, 'sgu_ln_b': _jnp.float32, 'sgu_w': _jnp.float32, 'sgu_b': _jnp.float32, 'w_out': _jnp.float32, 'final_g': _jnp.float32}
MOMENT_SCALE = {'norm_g': 1.818146e-02, 'w_ada': 2.027679e-02, 'b_ada': 3.572530e-02, 'w_in': 1.162140e-02, 'attn_sinks': 2.339115e-03, 'sgu_ln_g': 9.417063e-03, 'sgu_ln_b': 9.981602e-03, 'sgu_w': 9.519776e-03, 'sgu_b': 1.380077e-02, 'w_out': 1.209260e-02, 'final_g': 8.022642e+00}


def _to_microbatches(a, axis):
    t = _jnp.moveaxis(a, axis, 0)
    t = t.reshape((N_MICROBATCH, t.shape[0] // N_MICROBATCH) + t.shape[1:])
    return _jnp.moveaxis(t, 1, axis + 1)


def setup_inputs(seed: int = 0) -> dict:
    inp = _fwd_setup_inputs(seed)
    key = _jax.random.fold_in(_jax.random.key(seed), 7919)
    shape, _ = _output_shape()
    out = dict(inp)
    out["loss_target"] = _jax.random.normal(_jax.random.fold_in(key, 0), shape, _jnp.float32)
    for i, name in enumerate(TWIN_WEIGHTS):
        w = inp[name].astype(_jnp.float32)
        if MOMENT_SCALE is None:
            s = _jnp.sqrt(_jnp.mean(_jnp.square(w)) + 1e-30)
        else:
            s = MOMENT_SCALE[name]
        km, kv = _jax.random.split(_jax.random.fold_in(key, i + 1))
        out[name] = w
        out["m_" + name] = s * _jax.random.normal(km, w.shape, _jnp.float32)
        out["v_" + name] = (s * s) * _jax.random.uniform(kv, w.shape, _jnp.float32, 0.5, 1.5)
    if N_MICROBATCH > 1:
        for name, axis in PER_EXAMPLE_BATCH_AXIS.items():
            out[name] = _to_microbatches(out[name], axis)
    return {'x': out['x'], 'c': out['c'], 'norm_g': out['norm_g'], 'w_ada': out['w_ada'], 'b_ada': out['b_ada'], 'w_in': out['w_in'], 'attn_sinks': out['attn_sinks'], 'sgu_ln_g': out['sgu_ln_g'], 'sgu_ln_b': out['sgu_ln_b'], 'sgu_w': out['sgu_w'], 'sgu_b': out['sgu_b'], 'w_out': out['w_out'], 'final_g': out['final_g'], 'loss_target': out['loss_target'], 'm_norm_g': out['m_norm_g'], 'm_w_ada': out['m_w_ada'], 'm_b_ada': out['m_b_ada'], 'm_w_in': out['m_w_in'], 'm_attn_sinks': out['m_attn_sinks'], 'm_sgu_ln_g': out['m_sgu_ln_g'], 'm_sgu_ln_b': out['m_sgu_ln_b'], 'm_sgu_w': out['m_sgu_w'], 'm_sgu_b': out['m_sgu_b'], 'm_w_out': out['m_w_out'], 'm_final_g': out['m_final_g'], 'v_norm_g': out['v_norm_g'], 'v_w_ada': out['v_w_ada'], 'v_b_ada': out['v_b_ada'], 'v_w_in': out['v_w_in'], 'v_attn_sinks': out['v_attn_sinks'], 'v_sgu_ln_g': out['v_sgu_ln_g'], 'v_sgu_ln_b': out['v_sgu_ln_b'], 'v_sgu_w': out['v_sgu_w'], 'v_sgu_b': out['v_sgu_b'], 'v_w_out': out['v_w_out'], 'v_final_g': out['v_final_g']}


def _loss(weights, diff, rest, loss_target):
    with _jax.named_scope("forward"):
        args = {**rest, TWIN_DIFF_INPUT: diff, **{k: w.astype(_WEIGHT_DTYPES[k]) for k, w in weights.items()}}
        y = _forward(args)
    with _jax.named_scope("loss_head"):
        err = _jnp.square(y.astype(_jnp.float32) - loss_target)
        return 0.5 * _jnp.sum(_jnp.mean(err, axis=-1)) if err.ndim else 0.5 * err


def _adamw(w, g, m, v):
    m = ADAM_B1 * m + (1.0 - ADAM_B1) * g
    v = ADAM_B2 * v + (1.0 - ADAM_B2) * _jnp.square(g)
    m_hat = m / (1.0 - ADAM_B1 ** ADAM_STEP)
    v_hat = v / (1.0 - ADAM_B2 ** ADAM_STEP)
    delta = -ADAM_LR * (m_hat / (_jnp.sqrt(v_hat) + ADAM_EPS) + ADAM_WD * w)
    return delta, m, v


def reference(x, c, norm_g, w_ada, b_ada, w_in, attn_sinks, sgu_ln_g, sgu_ln_b, sgu_w, sgu_b, w_out, final_g, loss_target, m_norm_g, m_w_ada, m_b_ada, m_w_in, m_attn_sinks, m_sgu_ln_g, m_sgu_ln_b, m_sgu_w, m_sgu_b, m_w_out, m_final_g, v_norm_g, v_w_ada, v_b_ada, v_w_in, v_attn_sinks, v_sgu_ln_g, v_sgu_ln_b, v_sgu_w, v_sgu_b, v_w_out, v_final_g):
    given = dict(x=x, c=c, norm_g=norm_g, w_ada=w_ada, b_ada=b_ada, w_in=w_in, attn_sinks=attn_sinks, sgu_ln_g=sgu_ln_g, sgu_ln_b=sgu_ln_b, sgu_w=sgu_w, sgu_b=sgu_b, w_out=w_out, final_g=final_g, loss_target=loss_target, m_norm_g=m_norm_g, m_w_ada=m_w_ada, m_b_ada=m_b_ada, m_w_in=m_w_in, m_attn_sinks=m_attn_sinks, m_sgu_ln_g=m_sgu_ln_g, m_sgu_ln_b=m_sgu_ln_b, m_sgu_w=m_sgu_w, m_sgu_b=m_sgu_b, m_w_out=m_w_out, m_final_g=m_final_g, v_norm_g=v_norm_g, v_w_ada=v_w_ada, v_b_ada=v_b_ada, v_w_in=v_w_in, v_attn_sinks=v_attn_sinks, v_sgu_ln_g=v_sgu_ln_g, v_sgu_ln_b=v_sgu_ln_b, v_sgu_w=v_sgu_w, v_sgu_b=v_sgu_b, v_w_out=v_w_out, v_final_g=v_final_g)
    weights = {n: given[n] for n in TWIN_WEIGHTS}
    shared = {n: given[n] for n in SHARED_INPUTS}
    per_example = {n: given[n] for n in ['x', 'c']}
    grad_fn = _jax.value_and_grad(_loss, argnums=(0, 1))

    def one_microbatch(ex, loss_target):
        ex = dict(ex)
        diff = ex.pop(TWIN_DIFF_INPUT)
        return grad_fn(weights, diff, {**shared, **ex}, loss_target)

    if N_MICROBATCH == 1:
        loss, (grad_w, grad_x) = one_microbatch(per_example, given["loss_target"])
    else:
        def body(carry, xs):
            loss_sum, grad_sum = carry
            l_k, (gw_k, gx_k) = one_microbatch(xs[0], xs[1])
            with _jax.named_scope("update"):
                return (loss_sum + l_k, _jax.tree.map(_jnp.add, grad_sum, gw_k)), gx_k

        init = (_jnp.zeros((), _jnp.float32), _jax.tree.map(_jnp.zeros_like, weights))
        (loss, grad_w), grad_x = _jax.lax.scan(body, init, (per_example, given["loss_target"]))
    with _jax.named_scope("update"):
        delta_w, new_m, new_v = {}, {}, {}
        for n in TWIN_WEIGHTS:
            delta_w[n], new_m[n], new_v[n] = _adamw(weights[n], grad_w[n], given["m_" + n], given["v_" + n])
    return (loss, grad_x, *[grad_w[n] for n in TWIN_WEIGHTS], *[delta_w[n] for n in TWIN_WEIGHTS],
            *[new_m[n] for n in TWIN_WEIGHTS], *[new_v[n] for n in TWIN_WEIGHTS])
```

```python
import functools

import jax
import jax.numpy as jnp
from jax import lax
from jax.experimental import pallas as pl
from jax.experimental.pallas import tpu as pltpu

F32 = jnp.float32
BF16 = jnp.bfloat16
MESH = pl.DeviceIdType.MESH

N_DEV = 8
D_MODEL = 2048
HEAD_DIM = 64
D_ATTN = 1024
N_Q_HEADS = 16
D_KV = 128
BLOCK = 128
D_SGU = 1024
SGU_GROUPS = 8
D_IN = 5376
W_IN_SHARD = D_IN // N_DEV
W_OUT_SHARD = D_MODEL // N_DEV
W_ADA_SHARD = 3 * D_MODEL // N_DEV
EPS = 1e-6
ATTN_SCALE = 0.125

ADAM_LR = 0.001
ADAM_B1 = 0.9
ADAM_B2 = 0.999
ADAM_EPS = 1e-08
ADAM_WD = 0.01
ADAM_STEP = 10

SEG_Q, SEG_GA, SEG_U, SEG_VS, SEG_GS, SEG_KV = 0, 1024, 2048, 3072, 4096, 5120
_REF_PIECES = ((0, 1024), (1280, 2304), (2304, 3328), (3328, 4352), (4352, 5376), (1024, 1280))

VMEM_LIMIT = 56 * 1024 * 1024


def _params(**kw):
    return pltpu.CompilerParams(vmem_limit_bytes=VMEM_LIMIT, **kw)


def _sigmoid(x):
    return 1.0 / (1.0 + jnp.exp(-x))


def _place():
    return lax.axis_index("x"), lax.axis_index("y"), lax.axis_index("c")


def _all_gather(shards, name, in_vmem):
    n = len(shards)

    def body(*refs):
        ins, outs = refs[:n], refs[n:2 * n]
        send_sems, recv_sems, local_sems = refs[2 * n:]
        x, y, c = _place()
        me, sibling = (x, y, c), (x, y, 1 - c)
        chips = [(1 - x, y), (x, 1 - y), (1 - x, 1 - y)]
        first, passed, mine = [], [], []
        for a in range(n):
            out_ref = outs[a]

            def slot(px, py, pc, out_ref=out_ref):
                return out_ref.at[4 * px + 2 * py + pc]

            def copy(k, block, to, src=None, a=a, slot=slot):
                return pltpu.make_async_remote_copy(
                    src_ref=slot(*block) if src is None else src, dst_ref=slot(*block),
                    send_sem=send_sems.at[a, k], recv_sem=recv_sems.at[a, k],
                    device_id=to, device_id_type=MESH)

            own = pltpu.make_async_copy(ins[a], slot(*me), local_sems.at[a])
            own.start()
            mine.append(own)
            mine_out = [copy(0, me, sibling, src=ins[a])]
            mine_out += [copy(1 + j, me, (*chip, c), src=ins[a]) for j, chip in enumerate(chips)]
            for cp in mine_out:
                cp.start()
            first += mine_out
            passed.append([copy(4 + j, (*chip, c), sibling) for j, chip in enumerate(chips)])
        for j, chip in enumerate(chips):
            for a in range(n):
                out_ref = outs[a]
                blk = out_ref.at[4 * chip[0] + 2 * chip[1] + c]
                pltpu.make_async_remote_copy(
                    src_ref=blk, dst_ref=blk, send_sem=send_sems.at[a, 1 + j], recv_sem=recv_sems.at[a, 1 + j],
                    device_id=me, device_id_type=MESH).wait_recv()
                passed[a][j].start()
        for a in range(n):
            out_ref = outs[a]
            blk = out_ref.at[4 * x + 2 * y + (1 - c)]
            pltpu.make_async_remote_copy(
                src_ref=blk, dst_ref=blk, send_sem=send_sems.at[a, 0], recv_sem=recv_sems.at[a, 0],
                device_id=me, device_id_type=MESH).wait_recv()
            for j, chip in enumerate(chips):
                blk = out_ref.at[4 * chip[0] + 2 * chip[1] + (1 - c)]
                pltpu.make_async_remote_copy(
                    src_ref=blk, dst_ref=blk, send_sem=send_sems.at[a, 4 + j], recv_sem=recv_sems.at[a, 4 + j],
                    device_id=me, device_id_type=MESH).wait_recv()
        for cp in first:
            cp.wait_send()
        for a in range(n):
            for cp in passed[a]:
                cp.wait_send()
        for cp in mine:
            cp.wait()

    space = pltpu.VMEM if in_vmem else pl.ANY
    spec = pl.BlockSpec(memory_space=space)
    return pl.pallas_call(
        body, name=name,
        out_shape=tuple(jax.ShapeDtypeStruct((N_DEV,) + s.shape, s.dtype) for s in shards),
        in_specs=[spec] * n, out_specs=tuple([spec] * n),
        scratch_shapes=[pltpu.SemaphoreType.DMA((n, 7)), pltpu.SemaphoreType.DMA((n, 7)),
                        pltpu.SemaphoreType.DMA((n,))],
        compiler_params=_params(),
    )(*shards)


def _pair_exchange(sends, name):
    n = len(sends)

    def body(*refs):
        ins, outs = refs[:n], refs[n:2 * n]
        send_sems, recv_sems = refs[2 * n:]
        x, y, c = _place()
        copies = []
        for a in range(n):
            for m in range(4):
                cp = pltpu.make_async_remote_copy(
                    src_ref=ins[a].at[m], dst_ref=outs[a].at[m],
                    send_sem=send_sems.at[a, m], recv_sem=recv_sems.at[a, m],
                    device_id=(x, y, 1 - c), device_id_type=MESH)
                cp.start()
                copies.append(cp)
        for cp in copies:
            cp.wait()

    spec = pl.BlockSpec(memory_space=pl.ANY)
    return pl.pallas_call(
        body, name=name,
        out_shape=tuple(jax.ShapeDtypeStruct(s.shape, s.dtype) for s in sends),
        in_specs=[spec] * n, out_specs=tuple([spec] * n),
        scratch_shapes=[pltpu.SemaphoreType.DMA((n, 4)), pltpu.SemaphoreType.DMA((n, 4))],
        compiler_params=_params(),
    )(*sends)


def _chip_exchange(sends, name):
    n = len(sends)

    def body(*refs):
        ins, outs = refs[:n], refs[n:2 * n]
        send_sems, recv_sems, local_sems = refs[2 * n:]
        x, y, c = _place()
        my_chip = 2 * x + y
        chips = [(1 - x, y), (x, 1 - y), (1 - x, 1 - y)]
        copies, own = [], []
        for a in range(n):
            cp = pltpu.make_async_copy(ins[a].at[my_chip], outs[a].at[my_chip], local_sems.at[a])
            cp.start()
            own.append(cp)
            for k, chip in enumerate(chips):
                cp = pltpu.make_async_remote_copy(
                    src_ref=ins[a].at[2 * chip[0] + chip[1]], dst_ref=outs[a].at[my_chip],
                    send_sem=send_sems.at[a, k], recv_sem=recv_sems.at[a, k],
                    device_id=(*chip, c), device_id_type=MESH)
                cp.start()
                copies.append((cp, a, k, chip))
        for cp, a, k, chip in copies:
            cp.wait_send()
            landed = outs[a].at[2 * chip[0] + chip[1]]
            pltpu.make_async_remote_copy(
                src_ref=landed, dst_ref=landed, send_sem=send_sems.at[a, k], recv_sem=recv_sems.at[a, k],
                device_id=(x, y, c), device_id_type=MESH).wait_recv()
        for cp in own:
            cp.wait()

    spec = pl.BlockSpec(memory_space=pl.ANY)
    return pl.pallas_call(
        body, name=name,
        out_shape=tuple(jax.ShapeDtypeStruct(s.shape, s.dtype) for s in sends),
        in_specs=[spec] * n, out_specs=tuple([spec] * n),
        scratch_shapes=[pltpu.SemaphoreType.DMA((n, 3)), pltpu.SemaphoreType.DMA((n, 3)),
                        pltpu.SemaphoreType.DMA((n,))],
        compiler_params=_params(),
    )(*sends)


def _matmul(a, b, dims, out_dtype, tm, tn, name):
    if dims == "nn":
        (m, k), n = a.shape, b.shape[1]
        a_spec = pl.BlockSpec((tm, k), lambda i, j: (i, 0))
        b_spec = pl.BlockSpec((k, tn), lambda i, j: (0, j))
        contract = ((1,), (0,))
    elif dims == "nt":
        (m, k), n = a.shape, b.shape[0]
        a_spec = pl.BlockSpec((tm, k), lambda i, j: (i, 0))
        b_spec = pl.BlockSpec((tn, k), lambda i, j: (j, 0))
        contract = ((1,), (1,))
    else:
        (k, m), n = a.shape, b.shape[1]
        a_spec = pl.BlockSpec((k, tm), lambda i, j: (0, i))
        b_spec = pl.BlockSpec((k, tn), lambda i, j: (0, j))
        contract = ((0,), (0,))
    assert m % tm == 0 and n % tn == 0 and a.dtype == BF16 and b.dtype == BF16

    def body(a_ref, b_ref, o_ref):
        o_ref[...] = lax.dot_general(a_ref[...], b_ref[...], (contract, ((), ())),
                                     preferred_element_type=F32).astype(out_dtype)

    return pl.pallas_call(
        body, name=name, grid=(m // tm, n // tn),
        in_specs=[a_spec, b_spec], out_specs=pl.BlockSpec((tm, tn), lambda i, j: (i, j)),
        out_shape=jax.ShapeDtypeStruct((m, n), out_dtype),
        compiler_params=_params(dimension_semantics=("arbitrary", "arbitrary")),
    )(a, b)


def _modulation(c_all, w_ada, b_ada_mine):
    def body(c_ref, w_ref, b_ref, act_ref, mod_ref):
        cv = c_ref[...]
        act = cv * _sigmoid(cv)
        act_ref[...] = act
        mod_ref[...] = jnp.dot(act.astype(BF16), w_ref[...].astype(BF16), preferred_element_type=F32) + b_ref[...]

    return pl.pallas_call(
        body, name="modulation",
        out_shape=(jax.ShapeDtypeStruct(c_all.shape, F32), jax.ShapeDtypeStruct((N_DEV, W_ADA_SHARD), F32)),
        compiler_params=_params(),
    )(c_all, w_ada, b_ada_mine)


def _modulated_norm(x, norm_g, scale, shift, tm=256):
    t, d = x.shape

    def body(x_ref, g_ref, sc_ref, sh_ref, h_ref):
        xv = x_ref[...]
        r = lax.rsqrt(jnp.mean(xv * xv, axis=-1, keepdims=True) + EPS)
        h = (xv * r) * g_ref[...] * (1.0 + sc_ref[...]) + sh_ref[...]
        h_ref[...] = h.astype(BF16)

    row = pl.BlockSpec((1, d), lambda i: (0, 0))
    return pl.pallas_call(
        body, name="modulated_norm", grid=(t // tm,),
        in_specs=[pl.BlockSpec((tm, d), lambda i: (i, 0)), row, row, row],
        out_specs=pl.BlockSpec((tm, d), lambda i: (i, 0)),
        out_shape=jax.ShapeDtypeStruct((t, d), BF16),
        compiler_params=_params(dimension_semantics=("arbitrary",)),
    )(x, norm_g, scale, shift)


def _lane_lo():
    return lax.broadcasted_iota(jnp.int32, (BLOCK, 128), 1) < HEAD_DIM


def _window_mask(block_index):
    t = lax.broadcasted_iota(jnp.int32, (BLOCK, 2 * BLOCK), 0)
    s = lax.broadcasted_iota(jnp.int32, (BLOCK, 2 * BLOCK), 1)
    return ((s < BLOCK) & (s > t) & (block_index > 0)) | ((s >= BLOCK) & ((s - BLOCK) <= t))


def _head_stack(pair_blocks, lo):
    rows = []
    for blk in pair_blocks:
        rows.append(jnp.where(lo, blk, 0.0))
        rows.append(jnp.where(lo, pltpu.roll(blk, HEAD_DIM, axis=1), 0.0))
    return jnp.concatenate(rows, axis=0)


def _head_unstack(stack, p, lo):
    even = stack[(2 * p) * BLOCK:(2 * p + 1) * BLOCK]
    odd = stack[(2 * p + 1) * BLOCK:(2 * p + 2) * BLOCK]
    return jnp.where(lo, even, pltpu.roll(odd, HEAD_DIM, axis=1))


def _kv_group(kk, vv, g):
    if g == 0:
        return kk, vv
    return pltpu.roll(kk, HEAD_DIM, axis=1), pltpu.roll(vv, HEAD_DIM, axis=1)


def _softmax_with_sink(scores, valid, sink):
    s3 = jnp.where(valid[None], scores.reshape(8, BLOCK, 2 * BLOCK), -jnp.inf).reshape(8 * BLOCK, 2 * BLOCK)
    m = jnp.maximum(jnp.max(s3, axis=-1, keepdims=True), sink)
    e = jnp.exp(s3 - m)
    es = jnp.exp(sink - m)
    inv = 1.0 / (jnp.sum(e, axis=-1, keepdims=True) + es)
    return e * inv, es * inv


def _dot_nt(a, b):
    return lax.dot_general(a, b, (((1,), (1,)), ((), ())), preferred_element_type=F32)


def _dot_tn(a, b):
    return lax.dot_general(a, b, (((0,), (0,)), ((), ())), preferred_element_type=F32)


def _layer_norm_fwd(v):
    mu = jnp.mean(v, axis=-1, keepdims=True)
    xc = v - mu
    rstd = lax.rsqrt(jnp.mean(xc * xc, axis=-1, keepdims=True) + EPS)
    return xc * rstd, rstd


def _tril(transposed=False):
    t = lax.broadcasted_iota(jnp.int32, (BLOCK, BLOCK), 0)
    s = lax.broadcasted_iota(jnp.int32, (BLOCK, BLOCK), 1)
    return s >= t if transposed else t >= s


def _z_specs(index):
    wide = [pl.BlockSpec((BLOCK, 1024), functools.partial(lambda i, s: (index(i), s), s=s)) for s in range(5)]
    kv = pl.BlockSpec((BLOCK, 2 * D_KV), lambda i: (index(i), SEG_KV // (2 * D_KV)))
    kv_prev = pl.BlockSpec((BLOCK, 2 * D_KV), lambda i: (jnp.maximum(index(i) - 1, 0), SEG_KV // (2 * D_KV)))
    return wide + [kv, kv_prev]


def _const_spec(shape):
    return pl.BlockSpec(shape, lambda i: (0,) * len(shape))


def _mixer_fwd(z, sink_rows, ln_g, ln_b, sgu_w, sgu_bt):
    t = z.shape[0]

    def body(q_ref, ga_ref, u_ref, vs_ref, gs_ref, kvc_ref, kvp_ref, sink_ref, lng_ref, lnb_ref, w_ref, bt_ref, a_ref):
        lo = _lane_lo()
        valid = _window_mask(pl.program_id(0))
        kvp, kvc = kvp_ref[...], kvc_ref[...]
        kk = jnp.concatenate([kvp[:, :D_KV], kvc[:, :D_KV]], axis=0)
        vv = jnp.concatenate([kvp[:, D_KV:], kvc[:, D_KV:]], axis=0)
        for g in range(2):
            kg, vg = _kv_group(kk, vv, g)
            q = _head_stack([q_ref[:, (4 * g + p) * 128:(4 * g + p + 1) * 128] for p in range(4)], lo)
            scores = _dot_nt(q.astype(BF16), kg.astype(BF16)) * ATTN_SCALE
            prob, _ = _softmax_with_sink(scores, valid, sink_ref[g])
            o = jnp.dot(prob.astype(BF16), vg.astype(BF16), preferred_element_type=F32)
            for p in range(4):
                cols = slice((4 * g + p) * 128, (4 * g + p + 1) * 128)
                gate = ga_ref[:, cols]
                a_ref[:, cols] = (_head_unstack(o, p, lo) * (gate * _sigmoid(gate))).astype(BF16)

        vhat, _ = _layer_norm_fwd(vs_ref[...])
        vn = vhat * lng_ref[...] + lnb_ref[...]
        tril = _tril()
        for g in range(SGU_GROUPS):
            cols = slice(g * 128, (g + 1) * 128)
            wm = jnp.where(tril, w_ref[g], 0.0).astype(BF16)
            mixed = jnp.dot(wm, vn[:, cols].astype(BF16), preferred_element_type=F32) + bt_ref[:, g:g + 1]
            gate = gs_ref[:, cols]
            a_ref[:, D_ATTN + g * 128:D_ATTN + (g + 1) * 128] = (
                (u_ref[:, cols] * mixed) * (gate * _sigmoid(gate))).astype(BF16)

    return pl.pallas_call(
        body, name="mixer_fwd", grid=(t // BLOCK,),
        in_specs=_z_specs(lambda i: i) + [
            _const_spec((2, 8 * BLOCK, 1)), _const_spec((1, D_SGU)), _const_spec((1, D_SGU)),
            _const_spec((SGU_GROUPS, BLOCK, BLOCK)), _const_spec((BLOCK, SGU_GROUPS))],
        out_specs=pl.BlockSpec((BLOCK, D_MODEL), lambda i: (i, 0)),
        out_shape=jax.ShapeDtypeStruct((t, D_MODEL), BF16),
        compiler_params=_params(dimension_semantics=("arbitrary",)),
    )(z, z, z, z, z, z, z, sink_rows, ln_g, ln_b, sgu_w, sgu_bt)


def _mixer_bwd(z, da, sink_rows, ln_g, ln_b, sgu_w, sgu_wt, sgu_bt):
    t = z.shape[0]
    nb = t // BLOCK

    def body(q_ref, ga_ref, u_ref, vs_ref, gs_ref, kvc_ref, kvp_ref, daa_ref, das_ref, sink_ref, lng_ref, lnb_ref,
             w_ref, wt_ref, bt_ref, dz_ref, dsink_ref, dw_ref, dbt_ref, dlng_ref, dlnb_ref, carry_ref):
        step = pl.program_id(0)

        @pl.when(step == 0)
        def _():
            carry_ref[...] = jnp.zeros_like(carry_ref)
            dsink_ref[...] = jnp.zeros_like(dsink_ref)
            dw_ref[...] = jnp.zeros_like(dw_ref)
            dbt_ref[...] = jnp.zeros_like(dbt_ref)
            dlng_ref[...] = jnp.zeros_like(dlng_ref)
            dlnb_ref[...] = jnp.zeros_like(dlnb_ref)

        lo = _lane_lo()
        valid = _window_mask(nb - 1 - step)
        kvp, kvc = kvp_ref[...], kvc_ref[...]
        kk = jnp.concatenate([kvp[:, :D_KV], kvc[:, :D_KV]], axis=0)
        vv = jnp.concatenate([kvp[:, D_KV:], kvc[:, D_KV:]], axis=0)
        dkk = jnp.zeros((2 * BLOCK, D_KV), F32)
        dvv = jnp.zeros((2 * BLOCK, D_KV), F32)
        for g in range(2):
            kg, vg = _kv_group(kk, vv, g)
            kg, vg = kg.astype(BF16), vg.astype(BF16)
            pairs = [slice((4 * g + p) * 128, (4 * g + p + 1) * 128) for p in range(4)]
            q = _head_stack([q_ref[:, cols] for cols in pairs], lo).astype(BF16)
            scores = _dot_nt(q, kg) * ATTN_SCALE
            prob, sink_prob = _softmax_with_sink(scores, valid, sink_ref[g])
            prob_b = prob.astype(BF16)
            o = jnp.dot(prob_b, vg, preferred_element_type=F32)
            gates = [ga_ref[:, cols] for cols in pairs]
            sig = [_sigmoid(gt) for gt in gates]
            d_attn = [daa_ref[:, cols] for cols in pairs]
            d_o = _head_stack([d_attn[p] * (gates[p] * sig[p]) for p in range(4)], lo).astype(BF16)
            d_prob = _dot_nt(d_o, vg)
            delta = jnp.sum(prob * d_prob, axis=-1, keepdims=True)
            d_scores = (prob * (d_prob - delta) * ATTN_SCALE).astype(BF16)
            d_sink = -jnp.sum((sink_prob * delta).reshape(8, BLOCK, 1), axis=1)
            dsink_ref[8 * g:8 * g + 8, :] += jnp.broadcast_to(d_sink, (8, 128))
            d_q = jnp.dot(d_scores, kg, preferred_element_type=F32)
            d_kg = _dot_tn(d_scores, q)
            d_vg = _dot_tn(prob_b, d_o)
            if g == 1:
                d_kg, d_vg = pltpu.roll(d_kg, HEAD_DIM, axis=1), pltpu.roll(d_vg, HEAD_DIM, axis=1)
            dkk, dvv = dkk + d_kg, dvv + d_vg
            for p in range(4):
                cols = pairs[p]
                dz_ref[:, cols] = _head_unstack(d_q, p, lo).astype(BF16)
                d_silu = sig[p] * (1.0 + gates[p] * (1.0 - sig[p]))
                dz_ref[:, SEG_GA + cols.start:SEG_GA + cols.stop] = (
                    d_attn[p] * _head_unstack(o, p, lo) * d_silu).astype(BF16)
        d_kv = jnp.concatenate([dkk, dvv], axis=1)
        dz_ref[:, SEG_KV:SEG_KV + 2 * D_KV] = (d_kv[BLOCK:] + carry_ref[...]).astype(BF16)
        carry_ref[...] = d_kv[:BLOCK]

        vhat, rstd = _layer_norm_fwd(vs_ref[...])
        lng = lng_ref[...]
        vn = vhat * lng + lnb_ref[...]
        tril, triu = _tril(), _tril(transposed=True)
        lane = lax.broadcasted_iota(jnp.int32, (BLOCK, 128), 1)
        d_bt = jnp.zeros((BLOCK, 128), F32)
        d_vn = []
        for g in range(SGU_GROUPS):
            cols = slice(g * 128, (g + 1) * 128)
            wm = jnp.where(tril, w_ref[g], 0.0).astype(BF16)
            wmt = jnp.where(triu, wt_ref[g], 0.0).astype(BF16)
            vn_g = vn[:, cols].astype(BF16)
            mixed = jnp.dot(wm, vn_g, preferred_element_type=F32) + bt_ref[:, g:g + 1]
            gate, u, d_out = gs_ref[:, cols], u_ref[:, cols], das_ref[:, cols]
            sg = _sigmoid(gate)
            silu = gate * sg
            d_um = d_out * silu
            dz_ref[:, SEG_U + g * 128:SEG_U + (g + 1) * 128] = (d_um * mixed).astype(BF16)
            dz_ref[:, SEG_GS + g * 128:SEG_GS + (g + 1) * 128] = (
                d_out * (u * mixed) * (sg * (1.0 + gate * (1.0 - sg)))).astype(BF16)
            d_mixed = d_um * u
            d_mixed_b = d_mixed.astype(BF16)
            dw_ref[g] += jnp.where(tril, _dot_nt(d_mixed_b, vn_g), 0.0)
            d_bt = d_bt + jnp.where(lane == g, jnp.sum(d_mixed, axis=-1, keepdims=True), 0.0)
            d_vn.append(jnp.dot(wmt, d_mixed_b, preferred_element_type=F32))
        dbt_ref[...] += d_bt
        d_vn = jnp.concatenate(d_vn, axis=1)
        dlng_ref[...] += jnp.sum(d_vn * vhat, axis=0, keepdims=True)
        dlnb_ref[...] += jnp.sum(d_vn, axis=0, keepdims=True)
        d_vhat = d_vn * lng
        d_v = rstd * (d_vhat - jnp.mean(d_vhat, axis=-1, keepdims=True)
                      - vhat * jnp.mean(d_vhat * vhat, axis=-1, keepdims=True))
        dz_ref[:, SEG_VS:SEG_VS + D_SGU] = d_v.astype(BF16)

    rev = lambda i: nb - 1 - i
    return pl.pallas_call(
        body, name="mixer_bwd", grid=(nb,),
        in_specs=_z_specs(rev) + [
            pl.BlockSpec((BLOCK, 1024), lambda i: (rev(i), 0)), pl.BlockSpec((BLOCK, 1024), lambda i: (rev(i), 1)),
            _const_spec((2, 8 * BLOCK, 1)), _const_spec((1, D_SGU)), _const_spec((1, D_SGU)),
            _const_spec((SGU_GROUPS, BLOCK, BLOCK)), _const_spec((SGU_GROUPS, BLOCK, BLOCK)),
            _const_spec((BLOCK, SGU_GROUPS))],
        out_specs=(pl.BlockSpec((BLOCK, D_IN), lambda i: (rev(i), 0)), _const_spec((N_Q_HEADS, 128)),
                   _const_spec((SGU_GROUPS, BLOCK, BLOCK)), _const_spec((BLOCK, 128)),
                   _const_spec((1, D_SGU)), _const_spec((1, D_SGU))),
        out_shape=(jax.ShapeDtypeStruct((t, D_IN), BF16), jax.ShapeDtypeStruct((N_Q_HEADS, 128), F32),
                   jax.ShapeDtypeStruct((SGU_GROUPS, BLOCK, BLOCK), F32), jax.ShapeDtypeStruct((BLOCK, 128), F32),
                   jax.ShapeDtypeStruct((1, D_SGU), F32), jax.ShapeDtypeStruct((1, D_SGU), F32)),
        scratch_shapes=[pltpu.VMEM((BLOCK, 2 * D_KV), F32)],
        compiler_params=_params(dimension_semantics=("arbitrary",)),
    )(z, z, z, z, z, z, z, da, da, sink_rows, ln_g, ln_b, sgu_w, sgu_wt, sgu_bt)


def _head(y, x, target, gate, final_g, tm=256):
    t, d = x.shape

    def body(y_ref, x_ref, tg_ref, gate_ref, fg_ref, dx2_ref, dy_ref, loss_ref, dfg_ref, dgate_ref):
        @pl.when(pl.program_id(0) == 0)
        def _():
            loss_ref[...] = jnp.zeros_like(loss_ref)
            dfg_ref[...] = jnp.zeros_like(dfg_ref)
            dgate_ref[...] = jnp.zeros_like(dgate_ref)

        yv, gate, fg = y_ref[...], gate_ref[...], fg_ref[...]
        x2 = x_ref[...] + gate * yv
        r2 = lax.rsqrt(jnp.mean(x2 * x2, axis=-1, keepdims=True) + EPS)
        nrm = x2 * r2
        err = nrm * fg - tg_ref[...]
        loss_ref[...] += 0.5 * jnp.sum(jnp.mean(err * err, axis=-1, keepdims=True), axis=0, keepdims=True)
        d_out = err * (1.0 / d)
        dfg_ref[...] += jnp.sum(d_out * nrm, axis=0, keepdims=True)
        d_nrm = d_out * fg
        dx2 = r2 * (d_nrm - nrm * jnp.mean(d_nrm * nrm, axis=-1, keepdims=True))
        dx2_ref[...] = dx2
        dgate_ref[...] += jnp.sum(dx2 * yv, axis=0, keepdims=True)
        dy_ref[...] = (dx2 * gate).astype(BF16)

    blk = pl.BlockSpec((tm, d), lambda i: (i, 0))
    row = _const_spec((1, d))
    return pl.pallas_call(
        body, name="head", grid=(t // tm,),
        in_specs=[blk, blk, blk, row, row],
        out_specs=(blk, blk, _const_spec((1, 128)), row, row),
        out_shape=(jax.ShapeDtypeStruct((t, d), F32), jax.ShapeDtypeStruct((t, d), BF16),
                   jax.ShapeDtypeStruct((1, 128), F32), jax.ShapeDtypeStruct((1, d), F32),
                   jax.ShapeDtypeStruct((1, d), F32)),
        compiler_params=_params(dimension_semantics=("arbitrary",)),
    )(y, x, target, gate, final_g)


def _modulated_norm_bwd(dh, x, dx2, norm_g, scale, tm=256):
    t, d = x.shape

    def body(dh_ref, x_ref, dx2_ref, g_ref, sc_ref, gx_ref, dshift_ref, dscale_ref, dg_ref):
        @pl.when(pl.program_id(0) == 0)
        def _():
            dshift_ref[...] = jnp.zeros_like(dshift_ref)
            dscale_ref[...] = jnp.zeros_like(dscale_ref)
            dg_ref[...] = jnp.zeros_like(dg_ref)

        dh, xv, g = dh_ref[...], x_ref[...], g_ref[...]
        one_plus = 1.0 + sc_ref[...]
        r = lax.rsqrt(jnp.mean(xv * xv, axis=-1, keepdims=True) + EPS)
        xn = xv * r
        dshift_ref[...] += jnp.sum(dh, axis=0, keepdims=True)
        dscale_ref[...] += jnp.sum(dh * (xn * g), axis=0, keepdims=True)
        d_y = dh * one_plus
        dg_ref[...] += jnp.sum(d_y * xn, axis=0, keepdims=True)
        d_xn = d_y * g
        gx_ref[...] = dx2_ref[...] + r * (d_xn - xn * jnp.mean(d_xn * xn, axis=-1, keepdims=True))

    blk = pl.BlockSpec((tm, d), lambda i: (i, 0))
    row = _const_spec((1, d))
    return pl.pallas_call(
        body, name="modulated_norm_bwd", grid=(t // tm,),
        in_specs=[blk, blk, blk, row, row], out_specs=(blk, row, row, row),
        out_shape=(jax.ShapeDtypeStruct((t, d), F32),) + (jax.ShapeDtypeStruct((1, d), F32),) * 3,
        compiler_params=_params(dimension_semantics=("arbitrary",)),
    )(dh, x, dx2, norm_g, scale)


def _pair_sum(keep, got, name, tr):
    _, r, c = keep.shape

    def body(a_ref, b_ref, o_ref):
        o_ref[...] = (a_ref[...].astype(F32) + b_ref[...].astype(F32)).astype(BF16)

    blk = pl.BlockSpec((1, tr, c), lambda m, i: (m, i, 0))
    return pl.pallas_call(
        body, name=name, grid=(4, r // tr), in_specs=[blk, blk], out_specs=blk,
        out_shape=jax.ShapeDtypeStruct(keep.shape, BF16),
        compiler_params=_params(dimension_semantics=("arbitrary", "arbitrary")),
    )(keep, got)


def _adamw(w, g, m, v):
    m = ADAM_B1 * m + (1.0 - ADAM_B1) * g
    v = ADAM_B2 * v + (1.0 - ADAM_B2) * (g * g)
    m_hat = m / (1.0 - ADAM_B1 ** ADAM_STEP)
    v_hat = v / (1.0 - ADAM_B2 ** ADAM_STEP)
    delta = -ADAM_LR * (m_hat / (jnp.sqrt(v_hat) + ADAM_EPS) + ADAM_WD * w)
    return delta, m, v


def _adam_from_parts(parts, w, m, v, name, tr):
    n, r, c = parts.shape

    def body(p_ref, w_ref, m_ref, v_ref, g_ref, d_ref, nm_ref, nv_ref):
        g = p_ref[0].astype(F32)
        for k in range(1, n):
            g = g + p_ref[k].astype(F32)
        g_ref[...] = g
        d_ref[...], nm_ref[...], nv_ref[...] = _adamw(w_ref[...], g, m_ref[...], v_ref[...])

    blk = pl.BlockSpec((tr, c), lambda i: (i, 0))
    return pl.pallas_call(
        body, name=name, grid=(r // tr,),
        in_specs=[pl.BlockSpec((n, tr, c), lambda i: (0, i, 0)), blk, blk, blk], out_specs=(blk,) * 4,
        out_shape=(jax.ShapeDtypeStruct((r, c), F32),) * 4,
        compiler_params=_params(dimension_semantics=("arbitrary",)),
    )(parts, w, m, v)


def _adam_w_ada(act_t, dmod_mine, w, m, v, tr=256):
    r, c = w.shape

    def body(a_ref, dm_ref, w_ref, m_ref, v_ref, g_ref, d_ref, nm_ref, nv_ref):
        g = jnp.dot(a_ref[...].astype(BF16), dm_ref[...].astype(BF16), preferred_element_type=F32)
        g_ref[...] = g
        d_ref[...], nm_ref[...], nv_ref[...] = _adamw(w_ref[...], g, m_ref[...], v_ref[...])

    blk = pl.BlockSpec((tr, c), lambda i: (i, 0))
    return pl.pallas_call(
        body, name="adam_w_ada", grid=(r // tr,),
        in_specs=[pl.BlockSpec((tr, N_DEV), lambda i: (i, 0)), _const_spec((N_DEV, c)), blk, blk, blk],
        out_specs=(blk,) * 4, out_shape=(jax.ShapeDtypeStruct((r, c), F32),) * 4,
        compiler_params=_params(dimension_semantics=("arbitrary",)),
    )(act_t, dmod_mine, w, m, v)


_SMALL = (("b_ada", 6144), ("norm_g", 2048), ("final_g", 2048), ("sgu_ln_g", 1024), ("sgu_ln_b", 1024),
          ("sgu_b", 1024), ("attn_sinks", 16), ("loss", 1), ("sgu_w", 131072))


def _rows_of(size):
    return -(-size // 1024) * 8


def _pack(values):
    pieces = []
    for name, size in _SMALL:
        flat = values[name].reshape(-1).astype(F32)
        flat = jnp.pad(flat, (0, _rows_of(size) * 128 - size))
        pieces.append(flat.reshape(-1, 128))
    return jnp.concatenate(pieces, axis=0)


def _unpack(packed, shapes):
    out, row = {}, 0
    for name, size in _SMALL:
        rows = _rows_of(size)
        if name in shapes:
            out[name] = packed[row:row + rows].reshape(-1)[:size].reshape(shapes[name])
        row += rows
    return out


def kernel(x, c, norm_g, w_ada, b_ada, w_in, attn_sinks, sgu_ln_g, sgu_ln_b, sgu_w, sgu_b, w_out, final_g, loss_target, m_norm_g, m_w_ada, m_b_ada, m_w_in, m_attn_sinks, m_sgu_ln_g, m_sgu_ln_b, m_sgu_w, m_sgu_b, m_w_out, m_final_g, v_norm_g, v_w_ada, v_b_ada, v_w_in, v_attn_sinks, v_sgu_ln_g, v_sgu_ln_b, v_sgu_w, v_sgu_b, v_w_out, v_final_g):
    xi, yi, ci = _place()
    me = 4 * xi + 2 * yi + ci
    x2d, target = x[0], loss_target[0]
    t = x2d.shape[0]

    c_all = _all_gather([c.reshape(8, 256)], "gather_c", True)[0].reshape(N_DEV, D_MODEL)
    b_mine = lax.dynamic_slice(b_ada, (0, me * W_ADA_SHARD), (1, W_ADA_SHARD))
    c_act, mod_part = _modulation(c_all, w_ada[0], b_mine)
    mod_all = _all_gather([mod_part], "gather_mod", True)[0]
    mod = lax.dynamic_index_in_dim(mod_all, me, axis=1, keepdims=False).reshape(1, 3 * D_MODEL)
    shift, scale, gate = mod[:, :D_MODEL], mod[:, D_MODEL:2 * D_MODEL], mod[:, 2 * D_MODEL:]

    w_in_all, w_out_all = _all_gather([w_in[0].astype(BF16), w_out[0].astype(BF16)], "gather_w", False)
    w_in_ref = jnp.transpose(w_in_all, (1, 0, 2)).reshape(D_MODEL, D_IN)
    w_in_loc = jnp.concatenate([w_in_ref[:, a:b] for a, b in _REF_PIECES], axis=1)
    w_out_full = w_out_all.reshape(D_MODEL, D_MODEL)

    h = _modulated_norm(x2d, norm_g, scale, shift)
    z = _matmul(h, w_in_loc, "nn", F32, t, 768, "z_proj")
    sink_rows = jnp.repeat(attn_sinks.reshape(N_Q_HEADS), BLOCK).reshape(2, 8 * BLOCK, 1)
    sgu_bt = sgu_b[0].T
    a = _mixer_fwd(z, sink_rows, sgu_ln_g, sgu_ln_b, sgu_w[0], sgu_bt)
    y = _matmul(a, w_out_full, "nn", F32, min(t, 1024), 1024, "out_proj")
    dx2, dy, loss_part, d_final_g, d_gate = _head(y, x2d, target, gate, final_g.reshape(1, D_MODEL))

    da = _matmul(dy, w_out_full, "nt", F32, min(t, 1024), 1024, "out_proj_bwd")
    dw_out = _matmul(a, dy, "tn", BF16, 1024, 1024, "w_out_grad")
    dz, d_sinks, d_sgu_w, d_sgu_bt, d_ln_g, d_ln_b = _mixer_bwd(
        z, da, sink_rows, sgu_ln_g, sgu_ln_b, sgu_w[0], jnp.swapaxes(sgu_w[0], 1, 2), sgu_bt)
    dh = _matmul(dz, w_in_loc, "nt", F32, min(t, 1024), 512, "z_proj_bwd")
    dw_in_loc = _matmul(h, dz, "tn", BF16, D_MODEL, 768, "w_in_grad")
    grad_x, d_shift, d_scale, d_norm_g = _modulated_norm_bwd(dh, x2d, dx2, norm_g, scale)

    starts = [sum(b - a for a, b in _REF_PIECES[:k]) for k in range(len(_REF_PIECES))]
    ref_order = sorted(range(len(_REF_PIECES)), key=lambda k: _REF_PIECES[k][0])
    dw_in_ref = jnp.concatenate(
        [dw_in_loc[:, starts[k]:starts[k] + _REF_PIECES[k][1] - _REF_PIECES[k][0]] for k in ref_order], axis=1)
    dw_in_blocks = jnp.transpose(dw_in_ref.reshape(D_MODEL, 4, 2, W_IN_SHARD), (2, 1, 0, 3))
    dw_out_blocks = jnp.transpose(dw_out.reshape(4, 2, W_OUT_SHARD, D_MODEL), (1, 0, 2, 3))
    keep = [lax.dynamic_index_in_dim(b, ci, 0, keepdims=False) for b in (dw_in_blocks, dw_out_blocks)]
    send = [lax.dynamic_index_in_dim(b, 1 - ci, 0, keepdims=False) for b in (dw_in_blocks, dw_out_blocks)]
    got = _pair_exchange(send, "grad_pair_exchange")
    pair_in = _pair_sum(keep[0], got[0], "grad_pair_sum_in", 512)
    pair_out = _pair_sum(keep[1], got[1], "grad_pair_sum_out", W_OUT_SHARD)
    parts_in, parts_out = _chip_exchange([pair_in, pair_out], "grad_chip_exchange")

    g_w_in, d_w_in, nm_w_in, nv_w_in = _adam_from_parts(parts_in, w_in[0], m_w_in[0], v_w_in[0], "adam_w_in", 256)
    g_w_out, d_w_out, nm_w_out, nv_w_out = _adam_from_parts(
        parts_out, w_out[0], m_w_out[0], v_w_out[0], "adam_w_out", W_OUT_SHARD)

    partial = _pack({
        "b_ada": jnp.concatenate([d_shift, d_scale, d_gate], axis=1), "norm_g": d_norm_g, "final_g": d_final_g,
        "sgu_ln_g": d_ln_g, "sgu_ln_b": d_ln_b, "sgu_b": d_sgu_bt[:, :SGU_GROUPS].T, "attn_sinks": d_sinks[:, 0],
        "loss": loss_part[:, 0], "sgu_w": d_sgu_w})
    partial_all = _all_gather([partial], "gather_small", True)[0]
    small = {"b_ada": b_ada, "norm_g": norm_g, "final_g": final_g, "sgu_ln_g": sgu_ln_g, "sgu_ln_b": sgu_ln_b,
             "sgu_b": sgu_b, "attn_sinks": attn_sinks, "loss": jnp.zeros((1,), F32), "sgu_w": sgu_w}
    small_m = {"b_ada": m_b_ada, "norm_g": m_norm_g, "final_g": m_final_g, "sgu_ln_g": m_sgu_ln_g,
               "sgu_ln_b": m_sgu_ln_b, "sgu_b": m_sgu_b, "attn_sinks": m_attn_sinks, "loss": jnp.zeros((1,), F32),
               "sgu_w": m_sgu_w}
    small_v = {"b_ada": v_b_ada, "norm_g": v_norm_g, "final_g": v_final_g, "sgu_ln_g": v_sgu_ln_g,
               "sgu_ln_b": v_sgu_ln_b, "sgu_b": v_sgu_b, "attn_sinks": v_attn_sinks, "loss": jnp.ones((1,), F32),
               "sgu_w": v_sgu_w}
    rows = partial.shape[0]
    packed = _adam_from_parts(partial_all, _pack(small), _pack(small_m), _pack(small_v), "adam_small", rows)
    shapes = {k: v.shape for k, v in small.items() if k != "loss"}
    g_s, d_s, nm_s, nv_s = (_unpack(p, shapes) for p in packed)
    loss = packed[0][sum(_rows_of(s) for n, s in _SMALL[:7]), 0]

    dmod_all = partial_all[:, :_rows_of(6144)].reshape(N_DEV, 3 * D_MODEL)
    dmod_mine = lax.dynamic_slice(dmod_all, (0, me * W_ADA_SHARD), (N_DEV, W_ADA_SHARD))
    g_w_ada, d_w_ada, nm_w_ada, nv_w_ada = _adam_w_ada(c_act.T, dmod_mine, w_ada[0], m_w_ada[0], v_w_ada[0])

    big = {"w_ada": (g_w_ada, d_w_ada, nm_w_ada, nv_w_ada), "w_in": (g_w_in, d_w_in, nm_w_in, nv_w_in),
           "w_out": (g_w_out, d_w_out, nm_w_out, nv_w_out)}
    order = ["norm_g", "w_ada", "b_ada", "w_in", "attn_sinks", "sgu_ln_g", "sgu_ln_b", "sgu_w", "sgu_b", "w_out",
             "final_g"]
    outs = [loss, grad_x[None]]
    for k, per_small in enumerate((g_s, d_s, nm_s, nv_s)):
        for name in order:
            outs.append(big[name][k][None] if name in big else per_small[name])
    return tuple(outs)
```

```python
import jax
import jax.numpy as jnp
from jax import lax
from jax.experimental import pallas as pl
from jax.experimental.pallas import tpu as pltpu

F32 = jnp.float32
BF16 = jnp.bfloat16
MESH = pl.DeviceIdType.MESH

N_DEV = 8
D_MODEL = 2048
HEAD_DIM = 64
D_ATTN = 1024
N_Q_HEADS = 16
D_KV = 128
BLOCK = 128
D_SGU = 1024
SGU_GROUPS = 8
D_IN = 5376
W_IN_SHARD = D_IN // N_DEV
W_OUT_SHARD = D_MODEL // N_DEV
W_ADA_SHARD = 3 * D_MODEL // N_DEV
EPS = 1e-6
ATTN_SCALE = 0.125

ADAM_LR = 0.001
ADAM_B1 = 0.9
ADAM_B2 = 0.999
ADAM_EPS = 1e-08
ADAM_WD = 0.01
ADAM_STEP = 10

SEG_Q, SEG_KV, SEG_GA, SEG_U, SEG_VS, SEG_GS = 0, 1024, 1280, 2304, 3328, 4352

VMEM_LIMIT = 56 * 1024 * 1024

ROW_SHIFT, ROW_SCALE, ROW_GATE, ROW_NORM_G, ROW_FINAL_G, ROW_LN, ROW_MISC, ROW_SGU_B = 0, 1, 2, 3, 4, 5, 6, 8
SMALL_ROWS = 16


def _params(**kw):
    return pltpu.CompilerParams(vmem_limit_bytes=VMEM_LIMIT, **kw)


def _sigmoid(x):
    return 0.5 * (jnp.tanh(0.5 * x) + 1.0)


def _place():
    return lax.axis_index("x"), lax.axis_index("y"), lax.axis_index("c")


def _all_gather(shards, name, in_vmem):
    n = len(shards)

    def body(*refs):
        ins, outs = refs[:n], refs[n:2 * n]
        send_sems, recv_sems, local_sems = refs[2 * n:]
        x, y, c = _place()
        me, sibling = (x, y, c), (x, y, 1 - c)
        chips = [(1 - x, y), (x, 1 - y), (1 - x, 1 - y)]
        first, passed, mine = [], [], []
        for a in range(n):
            out_ref = outs[a]

            def slot(px, py, pc, out_ref=out_ref):
                return out_ref.at[4 * px + 2 * py + pc]

            def copy(k, block, to, src=None, a=a, slot=slot):
                return pltpu.make_async_remote_copy(
                    src_ref=slot(*block) if src is None else src, dst_ref=slot(*block),
                    send_sem=send_sems.at[a, k], recv_sem=recv_sems.at[a, k],
                    device_id=to, device_id_type=MESH)

            own = pltpu.make_async_copy(ins[a], slot(*me), local_sems.at[a])
            own.start()
            mine.append(own)
            mine_out = [copy(0, me, sibling, src=ins[a])]
            mine_out += [copy(1 + j, me, (*chip, c), src=ins[a]) for j, chip in enumerate(chips)]
            for cp in mine_out:
                cp.start()
            first += mine_out
            passed.append([copy(4 + j, (*chip, c), sibling) for j, chip in enumerate(chips)])
        for j, chip in enumerate(chips):
            for a in range(n):
                out_ref = outs[a]
                blk = out_ref.at[4 * chip[0] + 2 * chip[1] + c]
                pltpu.make_async_remote_copy(
                    src_ref=blk, dst_ref=blk, send_sem=send_sems.at[a, 1 + j], recv_sem=recv_sems.at[a, 1 + j],
                    device_id=me, device_id_type=MESH).wait_recv()
                passed[a][j].start()
        for a in range(n):
            out_ref = outs[a]
            blk = out_ref.at[4 * x + 2 * y + (1 - c)]
            pltpu.make_async_remote_copy(
                src_ref=blk, dst_ref=blk, send_sem=send_sems.at[a, 0], recv_sem=recv_sems.at[a, 0],
                device_id=me, device_id_type=MESH).wait_recv()
            for j, chip in enumerate(chips):
                blk = out_ref.at[4 * chip[0] + 2 * chip[1] + (1 - c)]
                pltpu.make_async_remote_copy(
                    src_ref=blk, dst_ref=blk, send_sem=send_sems.at[a, 4 + j], recv_sem=recv_sems.at[a, 4 + j],
                    device_id=me, device_id_type=MESH).wait_recv()
        for cp in first:
            cp.wait_send()
        for a in range(n):
            for cp in passed[a]:
                cp.wait_send()
        for cp in mine:
            cp.wait()

    space = pltpu.VMEM if in_vmem else pl.ANY
    spec = pl.BlockSpec(memory_space=space)
    return pl.pallas_call(
        body, name=name,
        out_shape=tuple(jax.ShapeDtypeStruct((N_DEV,) + s.shape, s.dtype) for s in shards),
        in_specs=[spec] * n, out_specs=tuple([spec] * n),
        scratch_shapes=[pltpu.SemaphoreType.DMA((n, 7)), pltpu.SemaphoreType.DMA((n, 7)),
                        pltpu.SemaphoreType.DMA((n,))],
        compiler_params=_params(),
    )(*shards)


def _pair_exchange(sends, name):
    n = len(sends)

    def body(*refs):
        ins, outs = refs[:n], refs[n:2 * n]
        send_sems, recv_sems = refs[2 * n:]
        x, y, c = _place()
        copies = []
        for a in range(n):
            for m in range(4):
                cp = pltpu.make_async_remote_copy(
                    src_ref=ins[a].at[m, 1 - c], dst_ref=outs[a].at[m],
                    send_sem=send_sems.at[a, m], recv_sem=recv_sems.at[a, m],
                    device_id=(x, y, 1 - c), device_id_type=MESH)
                cp.start()
                copies.append(cp)
        for cp in copies:
            cp.wait()

    spec = pl.BlockSpec(memory_space=pl.ANY)
    return pl.pallas_call(
        body, name=name,
        out_shape=tuple(jax.ShapeDtypeStruct((4,) + s.shape[2:], s.dtype) for s in sends),
        in_specs=[spec] * n, out_specs=tuple([spec] * n),
        scratch_shapes=[pltpu.SemaphoreType.DMA((n, 4)), pltpu.SemaphoreType.DMA((n, 4))],
        compiler_params=_params(),
    )(*sends)


def _chip_exchange(sends, name):
    n = len(sends)

    def body(*refs):
        ins, outs = refs[:n], refs[n:2 * n]
        send_sems, recv_sems, local_sems = refs[2 * n:]
        x, y, c = _place()
        my_chip = 2 * x + y
        chips = [(1 - x, y), (x, 1 - y), (1 - x, 1 - y)]
        copies, own = [], []
        for a in range(n):
            cp = pltpu.make_async_copy(ins[a].at[my_chip], outs[a].at[my_chip], local_sems.at[a])
            cp.start()
            own.append(cp)
            for k, chip in enumerate(chips):
                cp = pltpu.make_async_remote_copy(
                    src_ref=ins[a].at[2 * chip[0] + chip[1]], dst_ref=outs[a].at[my_chip],
                    send_sem=send_sems.at[a, k], recv_sem=recv_sems.at[a, k],
                    device_id=(*chip, c), device_id_type=MESH)
                cp.start()
                copies.append((cp, a, k, chip))
        for cp, a, k, chip in copies:
            cp.wait_send()
            landed = outs[a].at[2 * chip[0] + chip[1]]
            pltpu.make_async_remote_copy(
                src_ref=landed, dst_ref=landed, send_sem=send_sems.at[a, k], recv_sem=recv_sems.at[a, k],
                device_id=(x, y, c), device_id_type=MESH).wait_recv()
        for cp in own:
            cp.wait()

    spec = pl.BlockSpec(memory_space=pl.ANY)
    return pl.pallas_call(
        body, name=name,
        out_shape=tuple(jax.ShapeDtypeStruct(s.shape, s.dtype) for s in sends),
        in_specs=[spec] * n, out_specs=tuple([spec] * n),
        scratch_shapes=[pltpu.SemaphoreType.DMA((n, 3)), pltpu.SemaphoreType.DMA((n, 3)),
                        pltpu.SemaphoreType.DMA((n,))],
        compiler_params=_params(),
    )(*sends)


def _matmul(a, b, dims, out_dtype, tm, tn, name):
    if dims == "nn":
        (m, k), n = a.shape, b.shape[1]
        a_spec = pl.BlockSpec((tm, k), lambda i, j: (i, 0))
        b_spec = pl.BlockSpec((k, tn), lambda i, j: (0, j))
        contract = ((1,), (0,))
    elif dims == "nt":
        (m, k), n = a.shape, b.shape[0]
        a_spec = pl.BlockSpec((tm, k), lambda i, j: (i, 0))
        b_spec = pl.BlockSpec((tn, k), lambda i, j: (j, 0))
        contract = ((1,), (1,))
    else:
        (k, m), n = a.shape, b.shape[1]
        a_spec = pl.BlockSpec((k, tm), lambda i, j: (0, i))
        b_spec = pl.BlockSpec((k, tn), lambda i, j: (0, j))
        contract = ((0,), (0,))
    assert m % tm == 0 and n % tn == 0 and a.dtype == BF16 and b.dtype == BF16

    def body(a_ref, b_ref, o_ref):
        o_ref[...] = lax.dot_general(a_ref[...], b_ref[...], (contract, ((), ())),
                                     preferred_element_type=F32).astype(out_dtype)

    return pl.pallas_call(
        body, name=name, grid=(m // tm, n // tn),
        in_specs=[a_spec, b_spec], out_specs=pl.BlockSpec((tm, tn), lambda i, j: (i, j)),
        out_shape=jax.ShapeDtypeStruct((m, n), out_dtype),
        compiler_params=_params(dimension_semantics=("arbitrary", "arbitrary")),
    )(a, b)


def _modulation(c_all, w_ada, b_ada_mine):
    def body(c_ref, w_ref, b_ref, act_ref, mod_ref):
        cv = c_ref[...]
        act = cv * _sigmoid(cv)
        act_ref[...] = act
        mod_ref[...] = jnp.dot(act.astype(BF16), w_ref[...].astype(BF16), preferred_element_type=F32) + b_ref[...]

    return pl.pallas_call(
        body, name="modulation",
        out_shape=(jax.ShapeDtypeStruct(c_all.shape, F32), jax.ShapeDtypeStruct((N_DEV, W_ADA_SHARD), F32)),
        compiler_params=_params(),
    )(c_all, w_ada, b_ada_mine)


def _modulated_norm(x, norm_g, scale, shift, tm=256):
    t, d = x.shape

    def body(x_ref, g_ref, sc_ref, sh_ref, h_ref):
        xv = x_ref[...]
        r = lax.rsqrt(jnp.mean(xv * xv, axis=-1, keepdims=True) + EPS)
        h = (xv * r) * g_ref[...] * (1.0 + sc_ref[...]) + sh_ref[...]
        h_ref[...] = h.astype(BF16)

    row = pl.BlockSpec((1, d), lambda i: (0, 0))
    return pl.pallas_call(
        body, name="modulated_norm", grid=(t // tm,),
        in_specs=[pl.BlockSpec((tm, d), lambda i: (i, 0)), row, row, row],
        out_specs=pl.BlockSpec((tm, d), lambda i: (i, 0)),
        out_shape=jax.ShapeDtypeStruct((t, d), BF16),
        compiler_params=_params(dimension_semantics=("arbitrary",)),
    )(x, norm_g, scale, shift)


def _window_bias(block_index):
    s = lax.broadcasted_iota(jnp.int32, (2 * BLOCK, BLOCK), 0)
    t = lax.broadcasted_iota(jnp.int32, (2 * BLOCK, BLOCK), 1)
    valid = ((s < BLOCK) & (s > t) & (block_index > 0)) | ((s >= BLOCK) & ((s - BLOCK) <= t))
    bias = jnp.where(valid, 0.0, -jnp.inf).astype(F32)
    return jnp.concatenate([bias] * 8, axis=1)


def _heads_t(pair_blocks, g):
    top = lax.broadcasted_iota(jnp.int32, (BLOCK, BLOCK), 0) < HEAD_DIM
    zeros = jnp.zeros((HEAD_DIM, BLOCK), F32)
    tiles = []
    for blk in pair_blocks:
        tp = blk.T
        if g == 0:
            tiles += [jnp.where(top, tp, 0.0), jnp.concatenate([tp[HEAD_DIM:], zeros], axis=0)]
        else:
            tiles += [jnp.concatenate([zeros, tp[:HEAD_DIM]], axis=0), jnp.where(top, 0.0, tp)]
    return jnp.concatenate(tiles, axis=1)


def _pair_block(xt, p, g):
    r0 = HEAD_DIM * g
    even = xt[r0:r0 + HEAD_DIM, (2 * p) * BLOCK:(2 * p + 1) * BLOCK]
    odd = xt[r0:r0 + HEAD_DIM, (2 * p + 1) * BLOCK:(2 * p + 2) * BLOCK]
    return jnp.concatenate([even, odd], axis=0).T


def _softmax_t(scores_t, bias, sink):
    st = scores_t * ATTN_SCALE + bias
    m = jnp.maximum(jnp.max(st, axis=0, keepdims=True), sink)
    e = jnp.exp(st - m)
    es = jnp.exp(sink - m)
    inv = 1.0 / (jnp.sum(e, axis=0, keepdims=True) + es)
    return e * inv, es * inv


def _dot(a, b):
    return jnp.dot(a, b, preferred_element_type=F32)


def _dot_nt(a, b):
    return lax.dot_general(a, b, (((1,), (1,)), ((), ())), preferred_element_type=F32)


def _layer_norm_fwd(v):
    mu = jnp.mean(v, axis=-1, keepdims=True)
    xc = v - mu
    rstd = lax.rsqrt(jnp.mean(xc * xc, axis=-1, keepdims=True) + EPS)
    return xc * rstd, rstd


def _tril(transposed=False):
    t = lax.broadcasted_iota(jnp.int32, (BLOCK, BLOCK), 0)
    s = lax.broadcasted_iota(jnp.int32, (BLOCK, BLOCK), 1)
    return s >= t if transposed else t >= s


def _const_spec(shape):
    return pl.BlockSpec(shape, lambda i: (0,) * len(shape))


def _kv_prev_spec(index):
    return pl.BlockSpec((BLOCK, 2 * D_KV), lambda i: (jnp.maximum(index(i) - 1, 0), SEG_KV // (2 * D_KV)))


def _keys_values(z_ref, kvp_ref):
    kvp, kvc = kvp_ref[...], z_ref[:, SEG_KV:SEG_KV + 2 * D_KV]
    kk = jnp.concatenate([kvp[:, :D_KV], kvc[:, :D_KV]], axis=0)
    vv = jnp.concatenate([kvp[:, D_KV:], kvc[:, D_KV:]], axis=0)
    return kk, vv


def _pair_cols(g, p, base=0):
    return slice(base + (4 * g + p) * 128, base + (4 * g + p + 1) * 128)


def _mixer_fwd(z, sink_rows, ln_g, ln_b, sgu_w, sgu_bt):
    t = z.shape[0]

    def body(z_ref, kvp_ref, sink_ref, lng_ref, lnb_ref, w_ref, bt_ref, a_ref):
        bias = _window_bias(pl.program_id(0))
        kk, vv = _keys_values(z_ref, kvp_ref)
        kk_b, vvt_b = kk.astype(BF16), vv.T.astype(BF16)
        for g in range(2):
            qt = _heads_t([z_ref[:, _pair_cols(g, p, SEG_Q)] for p in range(4)], g).astype(BF16)
            prob, _ = _softmax_t(_dot(kk_b, qt), bias, sink_ref[g])
            ot = _dot(vvt_b, prob.astype(BF16))
            for p in range(4):
                gate = z_ref[:, _pair_cols(g, p, SEG_GA)]
                a_ref[:, _pair_cols(g, p)] = (_pair_block(ot, p, g) * (gate * _sigmoid(gate))).astype(BF16)

        vhat, _ = _layer_norm_fwd(z_ref[:, SEG_VS:SEG_VS + D_SGU])
        vn = vhat * lng_ref[...] + lnb_ref[...]
        tril = _tril()
        for g in range(SGU_GROUPS):
            cols = slice(g * 128, (g + 1) * 128)
            wm = jnp.where(tril, w_ref[g], 0.0).astype(BF16)
            mixed = _dot(wm, vn[:, cols].astype(BF16)) + bt_ref[:, g:g + 1]
            gate = z_ref[:, SEG_GS + g * 128:SEG_GS + (g + 1) * 128]
            a_ref[:, D_ATTN + g * 128:D_ATTN + (g + 1) * 128] = (
                (z_ref[:, SEG_U + g * 128:SEG_U + (g + 1) * 128] * mixed) * (gate * _sigmoid(gate))).astype(BF16)

    return pl.pallas_call(
        body, name="mixer_fwd", grid=(t // BLOCK,),
        in_specs=[pl.BlockSpec((BLOCK, D_IN), lambda i: (i, 0)), _kv_prev_spec(lambda i: i),
                  _const_spec((2, 1, 8 * BLOCK)), _const_spec((1, D_SGU)), _const_spec((1, D_SGU)),
                  _const_spec((SGU_GROUPS, BLOCK, BLOCK)), _const_spec((BLOCK, SGU_GROUPS))],
        out_specs=pl.BlockSpec((BLOCK, D_MODEL), lambda i: (i, 0)),
        out_shape=jax.ShapeDtypeStruct((t, D_MODEL), BF16),
        compiler_params=_params(dimension_semantics=("arbitrary",)),
    )(z, z, sink_rows, ln_g, ln_b, sgu_w, sgu_bt)


def _mixer_bwd(z, da, sink_rows, ln_g, ln_b, sgu_w, sgu_wt, sgu_bt):
    t = z.shape[0]
    nb = t // BLOCK

    def body(z_ref, kvp_ref, da_ref, sink_ref, lng_ref, lnb_ref, w_ref, wt_ref, bt_ref,
             dz_ref, dsink_ref, dw_ref, db_ref, dlng_ref, dlnb_ref, carry_ref, dsink_acc, dbt_acc):
        step = pl.program_id(0)

        @pl.when(step == 0)
        def _():
            carry_ref[...] = jnp.zeros_like(carry_ref)
            dsink_acc[...] = jnp.zeros_like(dsink_acc)
            dbt_acc[...] = jnp.zeros_like(dbt_acc)
            dw_ref[...] = jnp.zeros_like(dw_ref)
            dlng_ref[...] = jnp.zeros_like(dlng_ref)
            dlnb_ref[...] = jnp.zeros_like(dlnb_ref)

        bias = _window_bias(nb - 1 - step)
        kk, vv = _keys_values(z_ref, kvp_ref)
        kk_b, vv_b = kk.astype(BF16), vv.astype(BF16)
        kkt_b, vvt_b = kk.T.astype(BF16), vv.T.astype(BF16)
        dkk = jnp.zeros((2 * BLOCK, D_KV), F32)
        dvv = jnp.zeros((2 * BLOCK, D_KV), F32)
        for g in range(2):
            qt = _heads_t([z_ref[:, _pair_cols(g, p, SEG_Q)] for p in range(4)], g).astype(BF16)
            prob, sink_prob = _softmax_t(_dot(kk_b, qt), bias, sink_ref[g])
            prob_b = prob.astype(BF16)
            ot = _dot(vvt_b, prob_b)
            gates = [z_ref[:, _pair_cols(g, p, SEG_GA)] for p in range(4)]
            sig = [_sigmoid(gt) for gt in gates]
            d_attn = [da_ref[:, _pair_cols(g, p)] for p in range(4)]
            d_ot = _heads_t([d_attn[p] * (gates[p] * sig[p]) for p in range(4)], g).astype(BF16)
            d_prob = _dot(vv_b, d_ot)
            delta = jnp.sum(prob * d_prob, axis=0, keepdims=True)
            d_scores = (prob * (d_prob - delta) * ATTN_SCALE).astype(BF16)
            dsink_acc[g] -= sink_prob * delta
            d_qt = _dot(kkt_b, d_scores)
            dkk = dkk + _dot_nt(d_scores, qt)
            dvv = dvv + _dot_nt(prob_b, d_ot)
            for p in range(4):
                dz_ref[:, _pair_cols(g, p, SEG_Q)] = _pair_block(d_qt, p, g).astype(BF16)
                d_silu = sig[p] * (1.0 + gates[p] * (1.0 - sig[p]))
                dz_ref[:, _pair_cols(g, p, SEG_GA)] = (d_attn[p] * _pair_block(ot, p, g) * d_silu).astype(BF16)
        d_kv = jnp.concatenate([dkk, dvv], axis=1)
        dz_ref[:, SEG_KV:SEG_KV + 2 * D_KV] = (d_kv[BLOCK:] + carry_ref[...]).astype(BF16)
        carry_ref[...] = d_kv[:BLOCK]

        vhat, rstd = _layer_norm_fwd(z_ref[:, SEG_VS:SEG_VS + D_SGU])
        lng = lng_ref[...]
        vn = vhat * lng + lnb_ref[...]
        tril, triu = _tril(), _tril(transposed=True)
        lane = lax.broadcasted_iota(jnp.int32, (BLOCK, 128), 1)
        d_bt = jnp.zeros((BLOCK, 128), F32)
        d_vn = []
        for g in range(SGU_GROUPS):
            cols = slice(g * 128, (g + 1) * 128)
            wm = jnp.where(tril, w_ref[g], 0.0).astype(BF16)
            wmt = jnp.where(triu, wt_ref[g], 0.0).astype(BF16)
            vn_g = vn[:, cols].astype(BF16)
            mixed = _dot(wm, vn_g) + bt_ref[:, g:g + 1]
            gate = z_ref[:, SEG_GS + g * 128:SEG_GS + (g + 1) * 128]
            u = z_ref[:, SEG_U + g * 128:SEG_U + (g + 1) * 128]
            d_out = da_ref[:, D_ATTN + g * 128:D_ATTN + (g + 1) * 128]
            sg = _sigmoid(gate)
            d_um = d_out * (gate * sg)
            dz_ref[:, SEG_U + g * 128:SEG_U + (g + 1) * 128] = (d_um * mixed).astype(BF16)
            dz_ref[:, SEG_GS + g * 128:SEG_GS + (g + 1) * 128] = (
                d_out * (u * mixed) * (sg * (1.0 + gate * (1.0 - sg)))).astype(BF16)
            d_mixed = d_um * u
            d_mixed_b = d_mixed.astype(BF16)
            dw_ref[g] += jnp.where(tril, _dot_nt(d_mixed_b, vn_g), 0.0)
            d_bt = d_bt + jnp.where(lane == g, jnp.sum(d_mixed, axis=-1, keepdims=True), 0.0)
            d_vn.append(_dot(wmt, d_mixed_b))
        dbt_acc[...] += d_bt
        d_vn = jnp.concatenate(d_vn, axis=1)
        dlng_ref[...] += jnp.sum(d_vn * vhat, axis=0, keepdims=True)
        dlnb_ref[...] += jnp.sum(d_vn, axis=0, keepdims=True)
        d_vhat = d_vn * lng
        d_v = rstd * (d_vhat - jnp.mean(d_vhat, axis=-1, keepdims=True)
                      - vhat * jnp.mean(d_vhat * vhat, axis=-1, keepdims=True))
        dz_ref[:, SEG_VS:SEG_VS + D_SGU] = d_v.astype(BF16)

        @pl.when(step == nb - 1)
        def _():
            db_ref[...] = dbt_acc[...].T[:SGU_GROUPS]
            lane_row = lax.broadcasted_iota(jnp.int32, (1, 128), 1)
            d_sink = jnp.zeros((1, 128), F32)
            for g in range(2):
                acc = dsink_acc[g]
                for j in range(8):
                    head_sum = jnp.sum(acc[:, j * BLOCK:(j + 1) * BLOCK], axis=-1, keepdims=True)
                    d_sink = d_sink + jnp.where(lane_row == 8 * g + j, head_sum, 0.0)
            dsink_ref[...] = d_sink

    rev = lambda i: nb - 1 - i
    return pl.pallas_call(
        body, name="mixer_bwd", grid=(nb,),
        in_specs=[pl.BlockSpec((BLOCK, D_IN), lambda i: (rev(i), 0)), _kv_prev_spec(rev),
                  pl.BlockSpec((BLOCK, D_MODEL), lambda i: (rev(i), 0)),
                  _const_spec((2, 1, 8 * BLOCK)), _const_spec((1, D_SGU)), _const_spec((1, D_SGU)),
                  _const_spec((SGU_GROUPS, BLOCK, BLOCK)), _const_spec((SGU_GROUPS, BLOCK, BLOCK)),
                  _const_spec((BLOCK, SGU_GROUPS))],
        out_specs=(pl.BlockSpec((BLOCK, D_IN), lambda i: (rev(i), 0)), _const_spec((1, 128)),
                   _const_spec((SGU_GROUPS, BLOCK, BLOCK)), _const_spec((SGU_GROUPS, BLOCK)),
                   _const_spec((1, D_SGU)), _const_spec((1, D_SGU))),
        out_shape=(jax.ShapeDtypeStruct((t, D_IN), BF16), jax.ShapeDtypeStruct((1, 128), F32),
                   jax.ShapeDtypeStruct((SGU_GROUPS, BLOCK, BLOCK), F32), jax.ShapeDtypeStruct((SGU_GROUPS, BLOCK), F32),
                   jax.ShapeDtypeStruct((1, D_SGU), F32), jax.ShapeDtypeStruct((1, D_SGU), F32)),
        scratch_shapes=[pltpu.VMEM((BLOCK, 2 * D_KV), F32), pltpu.VMEM((2, 1, 8 * BLOCK), F32),
                        pltpu.VMEM((BLOCK, 128), F32)],
        compiler_params=_params(dimension_semantics=("arbitrary",)),
    )(z, z, da, sink_rows, ln_g, ln_b, sgu_w, sgu_wt, sgu_bt)


def _head(y, x, target, gate, final_g, tm=256):
    t, d = x.shape

    def body(y_ref, x_ref, tg_ref, gate_ref, fg_ref, dx2_ref, dy_ref, loss_ref, dfg_ref, dgate_ref):
        @pl.when(pl.program_id(0) == 0)
        def _():
            loss_ref[...] = jnp.zeros_like(loss_ref)
            dfg_ref[...] = jnp.zeros_like(dfg_ref)
            dgate_ref[...] = jnp.zeros_like(dgate_ref)

        yv, gate, fg = y_ref[...], gate_ref[...], fg_ref[...]
        x2 = x_ref[...] + gate * yv
        r2 = lax.rsqrt(jnp.mean(x2 * x2, axis=-1, keepdims=True) + EPS)
        nrm = x2 * r2
        err = nrm * fg - tg_ref[...]
        loss_ref[...] += 0.5 * jnp.sum(jnp.mean(err * err, axis=-1, keepdims=True), axis=0, keepdims=True)
        d_out = err * (1.0 / d)
        dfg_ref[...] += jnp.sum(d_out * nrm, axis=0, keepdims=True)
        d_nrm = d_out * fg
        dx2 = r2 * (d_nrm - nrm * jnp.mean(d_nrm * nrm, axis=-1, keepdims=True))
        dx2_ref[...] = dx2
        dgate_ref[...] += jnp.sum(dx2 * yv, axis=0, keepdims=True)
        dy_ref[...] = (dx2 * gate).astype(BF16)

    blk = pl.BlockSpec((tm, d), lambda i: (i, 0))
    row = _const_spec((1, d))
    return pl.pallas_call(
        body, name="head", grid=(t // tm,),
        in_specs=[blk, blk, blk, row, row],
        out_specs=(blk, blk, _const_spec((1, 128)), row, row),
        out_shape=(jax.ShapeDtypeStruct((t, d), F32), jax.ShapeDtypeStruct((t, d), BF16),
                   jax.ShapeDtypeStruct((1, 128), F32), jax.ShapeDtypeStruct((1, d), F32),
                   jax.ShapeDtypeStruct((1, d), F32)),
        compiler_params=_params(dimension_semantics=("arbitrary",)),
    )(y, x, target, gate, final_g)


def _modulated_norm_bwd(dh, x, dx2, norm_g, scale, tm=256):
    t, d = x.shape

    def body(dh_ref, x_ref, dx2_ref, g_ref, sc_ref, gx_ref, dshift_ref, dscale_ref, dg_ref):
        @pl.when(pl.program_id(0) == 0)
        def _():
            dshift_ref[...] = jnp.zeros_like(dshift_ref)
            dscale_ref[...] = jnp.zeros_like(dscale_ref)
            dg_ref[...] = jnp.zeros_like(dg_ref)

        dh, xv, g = dh_ref[...], x_ref[...], g_ref[...]
        one_plus = 1.0 + sc_ref[...]
        r = lax.rsqrt(jnp.mean(xv * xv, axis=-1, keepdims=True) + EPS)
        xn = xv * r
        dshift_ref[...] += jnp.sum(dh, axis=0, keepdims=True)
        dscale_ref[...] += jnp.sum(dh * (xn * g), axis=0, keepdims=True)
        d_y = dh * one_plus
        dg_ref[...] += jnp.sum(d_y * xn, axis=0, keepdims=True)
        d_xn = d_y * g
        gx_ref[...] = dx2_ref[...] + r * (d_xn - xn * jnp.mean(d_xn * xn, axis=-1, keepdims=True))

    blk = pl.BlockSpec((tm, d), lambda i: (i, 0))
    row = _const_spec((1, d))
    return pl.pallas_call(
        body, name="modulated_norm_bwd", grid=(t // tm,),
        in_specs=[blk, blk, blk, row, row], out_specs=(blk, row, row, row),
        out_shape=(jax.ShapeDtypeStruct((t, d), F32),) + (jax.ShapeDtypeStruct((1, d), F32),) * 3,
        compiler_params=_params(dimension_semantics=("arbitrary",)),
    )(dh, x, dx2, norm_g, scale)


def _pair_sum(core, blocks, got, name, tr):
    _, _, r, c = blocks.shape

    def body(core_ref, a_ref, b_ref, o_ref):
        o_ref[...] = (a_ref[...].astype(F32) + b_ref[...].astype(F32)).astype(BF16)

    return pl.pallas_call(
        body, name=name,
        grid_spec=pltpu.PrefetchScalarGridSpec(
            num_scalar_prefetch=1, grid=(4, r // tr),
            in_specs=[pl.BlockSpec((None, None, tr, c), lambda m, i, core_ref: (m, core_ref[0], i, 0)),
                      pl.BlockSpec((None, tr, c), lambda m, i, core_ref: (m, i, 0))],
            out_specs=pl.BlockSpec((None, tr, c), lambda m, i, core_ref: (m, i, 0))),
        out_shape=jax.ShapeDtypeStruct(got.shape, BF16),
        compiler_params=_params(dimension_semantics=("arbitrary", "arbitrary")),
    )(core, blocks, got)


def _adamw(w, g, m, v):
    m = ADAM_B1 * m + (1.0 - ADAM_B1) * g
    v = ADAM_B2 * v + (1.0 - ADAM_B2) * (g * g)
    m_hat = m / (1.0 - ADAM_B1 ** ADAM_STEP)
    v_hat = v / (1.0 - ADAM_B2 ** ADAM_STEP)
    delta = -ADAM_LR * (m_hat / (jnp.sqrt(v_hat) + ADAM_EPS) + ADAM_WD * w)
    return delta, m, v


def _adam_w_out(parts, w, m, v):
    n, r, c = parts.shape

    def body(p_ref, w_ref, m_ref, v_ref, g_ref, d_ref, nm_ref, nv_ref):
        g = p_ref[0].astype(F32)
        for k in range(1, n):
            g = g + p_ref[k].astype(F32)
        g_ref[...] = g
        d_ref[...], nm_ref[...], nv_ref[...] = _adamw(w_ref[...], g, m_ref[...], v_ref[...])

    return pl.pallas_call(
        body, name="adam_w_out", out_shape=(jax.ShapeDtypeStruct((r, c), F32),) * 4,
        compiler_params=_params(),
    )(parts, w, m, v)


def _adam_w_in(parts_t, w, m, v, tr=256):
    n, cols, rows = parts_t.shape
    full, rest = divmod(cols, 128)
    padded = (full + (rest > 0)) * 128

    def body(p_ref, w_ref, m_ref, v_ref, g_ref, d_ref, nm_ref, nv_ref, gt_ref):
        gt = p_ref[0].astype(F32)
        for k in range(1, n):
            gt = gt + p_ref[k].astype(F32)
        for k in range(full):
            gt_ref[:, k * 128:(k + 1) * 128] = gt[k * 128:(k + 1) * 128].T
        if rest:
            last = jnp.concatenate([gt[full * 128:], jnp.zeros((128 - rest, tr), F32)], axis=0)
            gt_ref[:, full * 128:padded] = last.T
        g = gt_ref[:, 0:cols]
        g_ref[...] = g
        d_ref[...], nm_ref[...], nv_ref[...] = _adamw(w_ref[...], g, m_ref[...], v_ref[...])

    blk = pl.BlockSpec((tr, cols), lambda i: (i, 0))
    return pl.pallas_call(
        body, name="adam_w_in", grid=(rows // tr,),
        in_specs=[pl.BlockSpec((n, cols, tr), lambda i: (0, 0, i)), blk, blk, blk], out_specs=(blk,) * 4,
        out_shape=(jax.ShapeDtypeStruct((rows, cols), F32),) * 4,
        scratch_shapes=[pltpu.VMEM((tr, padded), F32)],
        compiler_params=_params(dimension_semantics=("arbitrary",)),
    )(parts_t, w, m, v)


def _adam_w_ada(act_t, dmod_mine, w, m, v, tr=256):
    r, c = w.shape

    def body(a_ref, dm_ref, w_ref, m_ref, v_ref, g_ref, d_ref, nm_ref, nv_ref):
        g = _dot(a_ref[...].astype(BF16), dm_ref[...].astype(BF16))
        g_ref[...] = g
        d_ref[...], nm_ref[...], nv_ref[...] = _adamw(w_ref[...], g, m_ref[...], v_ref[...])

    blk = pl.BlockSpec((tr, c), lambda i: (i, 0))
    return pl.pallas_call(
        body, name="adam_w_ada", grid=(r // tr,),
        in_specs=[pl.BlockSpec((tr, N_DEV), lambda i: (i, 0)), _const_spec((N_DEV, c)), blk, blk, blk],
        out_specs=(blk,) * 4, out_shape=(jax.ShapeDtypeStruct((r, c), F32),) * 4,
        compiler_params=_params(dimension_semantics=("arbitrary",)),
    )(act_t, dmod_mine, w, m, v)


def _pack_small(d_shift, d_scale, d_gate, d_norm_g, d_final_g, d_ln_g, d_ln_b, loss, d_sinks, d_sgu_b):
    def body(shift_ref, scale_ref, gate_ref, ng_ref, fg_ref, lng_ref, lnb_ref, loss_ref, sink_ref, b_ref, o_ref):
        o_ref[...] = jnp.zeros_like(o_ref)
        o_ref[ROW_SHIFT:ROW_SHIFT + 1, :] = shift_ref[...]
        o_ref[ROW_SCALE:ROW_SCALE + 1, :] = scale_ref[...]
        o_ref[ROW_GATE:ROW_GATE + 1, :] = gate_ref[...]
        o_ref[ROW_NORM_G:ROW_NORM_G + 1, :] = ng_ref[...]
        o_ref[ROW_FINAL_G:ROW_FINAL_G + 1, :] = fg_ref[...]
        o_ref[ROW_LN:ROW_LN + 1, 0:D_SGU] = lng_ref[...]
        o_ref[ROW_LN:ROW_LN + 1, D_SGU:2 * D_SGU] = lnb_ref[...]
        o_ref[ROW_MISC:ROW_MISC + 1, 0:128] = loss_ref[...]
        o_ref[ROW_MISC:ROW_MISC + 1, 128:256] = sink_ref[...]
        o_ref[ROW_SGU_B:ROW_SGU_B + SGU_GROUPS, 0:BLOCK] = b_ref[...]

    return pl.pallas_call(
        body, name="pack_small", out_shape=jax.ShapeDtypeStruct((SMALL_ROWS, D_MODEL), F32),
        compiler_params=_params(),
    )(d_shift, d_scale, d_gate, d_norm_g, d_final_g, d_ln_g, d_ln_b, loss, d_sinks, d_sgu_b)


_SMALL_NAMES = ("norm_g", "b_ada", "attn_sinks", "sgu_ln_g", "sgu_ln_b", "sgu_w", "sgu_b", "final_g")


def _adam_small(partials, d_sgu_w_all, weights, moments_m, moments_v):
    names = _SMALL_NAMES
    k = len(names)

    def body(*refs):
        p_ref, sw_ref = refs[0], refs[1]
        w_refs, m_refs, v_refs = refs[2:2 + k], refs[2 + k:2 + 2 * k], refs[2 + 2 * k:2 + 3 * k]
        loss_ref, dmod_ref = refs[2 + 3 * k], refs[3 + 3 * k]
        out_refs = refs[4 + 3 * k:4 + 7 * k]
        sum_ref = refs[4 + 7 * k]
        total = p_ref[0]
        for j in range(1, N_DEV):
            total = total + p_ref[j]
        sum_ref[...] = total
        for j in range(N_DEV):
            for part, row in enumerate((ROW_SHIFT, ROW_SCALE, ROW_GATE)):
                dmod_ref[j:j + 1, part * D_MODEL:(part + 1) * D_MODEL] = p_ref[j, row:row + 1, :]
        loss_ref[...] = sum_ref[ROW_MISC:ROW_MISC + 1, 0:1]
        d_sgu_w = sw_ref[0]
        for j in range(1, N_DEV):
            d_sgu_w = d_sgu_w + sw_ref[j]
        grads = {
            "norm_g": sum_ref[ROW_NORM_G:ROW_NORM_G + 1, :],
            "b_ada": jnp.concatenate([sum_ref[r:r + 1, :] for r in (ROW_SHIFT, ROW_SCALE, ROW_GATE)], axis=1),
            "attn_sinks": sum_ref[ROW_MISC:ROW_MISC + 1, 128:128 + N_Q_HEADS],
            "sgu_ln_g": sum_ref[ROW_LN:ROW_LN + 1, 0:D_SGU],
            "sgu_ln_b": sum_ref[ROW_LN:ROW_LN + 1, D_SGU:2 * D_SGU],
            "sgu_w": d_sgu_w[None],
            "sgu_b": sum_ref[ROW_SGU_B:ROW_SGU_B + SGU_GROUPS, 0:BLOCK][None],
            "final_g": sum_ref[ROW_FINAL_G:ROW_FINAL_G + 1, :],
        }
        for i, name in enumerate(names):
            g = grads[name]
            delta, m, v = _adamw(w_refs[i][...], g, m_refs[i][...], v_refs[i][...])
            out_refs[4 * i][...] = g
            out_refs[4 * i + 1][...] = delta
            out_refs[4 * i + 2][...] = m
            out_refs[4 * i + 3][...] = v

    shapes = [jax.ShapeDtypeStruct((1, 1), F32), jax.ShapeDtypeStruct((N_DEV, 3 * D_MODEL), F32)]
    for name in names:
        shapes += [jax.ShapeDtypeStruct(weights[name].shape, F32)] * 4
    outs = pl.pallas_call(
        body, name="adam_small", out_shape=tuple(shapes),
        scratch_shapes=[pltpu.VMEM((SMALL_ROWS, D_MODEL), F32)],
        compiler_params=_params(),
    )(partials, d_sgu_w_all, *[weights[n] for n in names], *[moments_m[n] for n in names],
      *[moments_v[n] for n in names])
    return outs[0], outs[1], {name: outs[2 + 4 * i:6 + 4 * i] for i, name in enumerate(names)}


def kernel(x, c, norm_g, w_ada, b_ada, w_in, attn_sinks, sgu_ln_g, sgu_ln_b, sgu_w, sgu_b, w_out, final_g, loss_target, m_norm_g, m_w_ada, m_b_ada, m_w_in, m_attn_sinks, m_sgu_ln_g, m_sgu_ln_b, m_sgu_w, m_sgu_b, m_w_out, m_final_g, v_norm_g, v_w_ada, v_b_ada, v_w_in, v_attn_sinks, v_sgu_ln_g, v_sgu_ln_b, v_sgu_w, v_sgu_b, v_w_out, v_final_g):
    xi, yi, ci = _place()
    me = 4 * xi + 2 * yi + ci
    x2d, target = x[0], loss_target[0]
    t = x2d.shape[0]

    c_all = _all_gather([c.reshape(8, 256)], "gather_c", True)[0].reshape(N_DEV, D_MODEL)
    b_mine = lax.dynamic_slice(b_ada, (0, me * W_ADA_SHARD), (1, W_ADA_SHARD))
    c_act, mod_part = _modulation(c_all, w_ada[0], b_mine)
    mod_all = _all_gather([mod_part], "gather_mod", True)[0]
    mod = lax.dynamic_index_in_dim(mod_all, me, axis=1, keepdims=False).reshape(1, 3 * D_MODEL)
    shift, scale, gate = mod[:, :D_MODEL], mod[:, D_MODEL:2 * D_MODEL], mod[:, 2 * D_MODEL:]

    w_in_all, w_out_all = _all_gather([w_in[0].T.astype(BF16), w_out[0].astype(BF16)], "gather_w", False)
    w_in_t = w_in_all.reshape(D_IN, D_MODEL)
    w_out_full = w_out_all.reshape(D_MODEL, D_MODEL)

    h = _modulated_norm(x2d, norm_g, scale, shift)
    z = _matmul(h, w_in_t, "nt", F32, t, 768, "z_proj")
    sink_rows = jnp.repeat(attn_sinks.reshape(N_Q_HEADS), BLOCK).reshape(2, 1, 8 * BLOCK)
    sgu_bt = sgu_b[0].T
    a = _mixer_fwd(z, sink_rows, sgu_ln_g, sgu_ln_b, sgu_w[0], sgu_bt)
    y = _matmul(a, w_out_full, "nn", F32, min(t, 1024), 1024, "out_proj")
    final_g_row = final_g.reshape(1, D_MODEL)
    dx2, dy, loss_part, d_final_g, d_gate = _head(y, x2d, target, gate, final_g_row)

    da = _matmul(dy, w_out_full, "nt", F32, min(t, 1024), 1024, "out_proj_bwd")
    dw_out = _matmul(a, dy, "tn", BF16, 1024, 1024, "w_out_grad")
    dz, d_sinks, d_sgu_w, d_sgu_b, d_ln_g, d_ln_b = _mixer_bwd(
        z, da, sink_rows, sgu_ln_g, sgu_ln_b, sgu_w[0], jnp.swapaxes(sgu_w[0], 1, 2), sgu_bt)
    dw_in_t = _matmul(dz, h, "tn", BF16, 768, D_MODEL, "w_in_grad")
    dh = _matmul(dz, w_in_t, "nn", F32, min(t, 1024), 512, "z_proj_bwd")
    grad_x, d_shift, d_scale, d_norm_g = _modulated_norm_bwd(dh, x2d, dx2, norm_g, scale)

    blocks = [dw_in_t.reshape(4, 2, W_IN_SHARD, D_MODEL), dw_out.reshape(4, 2, W_OUT_SHARD, D_MODEL)]
    got = _pair_exchange(blocks, "grad_pair_exchange")
    core = ci.astype(jnp.int32).reshape(1)
    pair_in = _pair_sum(core, blocks[0], got[0], "grad_pair_sum_in", W_IN_SHARD // 2)
    pair_out = _pair_sum(core, blocks[1], got[1], "grad_pair_sum_out", W_OUT_SHARD)
    parts_in, parts_out = _chip_exchange([pair_in, pair_out], "grad_chip_exchange")
    g_w_in, d_w_in, nm_w_in, nv_w_in = _adam_w_in(parts_in, w_in[0], m_w_in[0], v_w_in[0])
    g_w_out, d_w_out, nm_w_out, nv_w_out = _adam_w_out(parts_out, w_out[0], m_w_out[0], v_w_out[0])

    partial = _pack_small(d_shift, d_scale, d_gate, d_norm_g, d_final_g, d_ln_g, d_ln_b, loss_part, d_sinks, d_sgu_b)
    partial_all, d_sgu_w_all = _all_gather([partial, d_sgu_w], "gather_small", True)
    weights = {"norm_g": norm_g, "b_ada": b_ada, "attn_sinks": attn_sinks, "sgu_ln_g": sgu_ln_g,
               "sgu_ln_b": sgu_ln_b, "sgu_w": sgu_w, "sgu_b": sgu_b, "final_g": final_g_row}
    moments_m = {"norm_g": m_norm_g, "b_ada": m_b_ada, "attn_sinks": m_attn_sinks, "sgu_ln_g": m_sgu_ln_g,
                 "sgu_ln_b": m_sgu_ln_b, "sgu_w": m_sgu_w, "sgu_b": m_sgu_b,
                 "final_g": m_final_g.reshape(1, D_MODEL)}
    moments_v = {"norm_g": v_norm_g, "b_ada": v_b_ada, "attn_sinks": v_attn_sinks, "sgu_ln_g": v_sgu_ln_g,
                 "sgu_ln_b": v_sgu_ln_b, "sgu_w": v_sgu_w, "sgu_b": v_sgu_b,
                 "final_g": v_final_g.reshape(1, D_MODEL)}
    loss, dmod_all, small = _adam_small(partial_all, d_sgu_w_all, weights, moments_m, moments_v)
    small["final_g"] = tuple(o.reshape(D_MODEL) for o in small["final_g"])

    dmod_mine = lax.dynamic_slice(dmod_all, (0, me * W_ADA_SHARD), (N_DEV, W_ADA_SHARD))
    big = {"w_ada": _adam_w_ada(c_act.T, dmod_mine, w_ada[0], m_w_ada[0], v_w_ada[0]),
           "w_in": (g_w_in, d_w_in, nm_w_in, nv_w_in), "w_out": (g_w_out, d_w_out, nm_w_out, nv_w_out)}
    order = ["norm_g", "w_ada", "b_ada", "w_in", "attn_sinks", "sgu_ln_g", "sgu_ln_b", "sgu_w", "sgu_b", "w_out",
             "final_g"]
    outs = [loss.reshape(()), grad_x[None]]
    for k in range(4):
        for name in order:
            outs.append(big[name][k][None] if name in big else small[name][k])
    return tuple(outs)
```

```python
import jax
import jax.numpy as jnp
from jax import lax
from jax.experimental import pallas as pl
from jax.experimental.pallas import tpu as pltpu

F32 = jnp.float32
BF16 = jnp.bfloat16
MESH = pl.DeviceIdType.MESH

N_DEV = 8
D_MODEL = 2048
HEAD_DIM = 64
D_ATTN = 1024
N_Q_HEADS = 16
D_KV = 128
BLOCK = 128
D_SGU = 1024
SGU_GROUPS = 8
D_IN = 5376
W_IN_SHARD = D_IN // N_DEV
W_OUT_SHARD = D_MODEL // N_DEV
W_ADA_SHARD = 3 * D_MODEL // N_DEV
EPS = 1e-6
ATTN_SCALE = 0.125

ADAM_LR = 0.001
ADAM_B1 = 0.9
ADAM_B2 = 0.999
ADAM_EPS = 1e-08
ADAM_WD = 0.01
ADAM_STEP = 10

SEG_Q, SEG_KV, SEG_GA, SEG_U, SEG_VS, SEG_GS = 0, 1024, 1280, 2304, 3328, 4352

VMEM_LIMIT = 56 * 1024 * 1024

ROW_SHIFT, ROW_SCALE, ROW_GATE, ROW_NORM_G, ROW_FINAL_G, ROW_LN, ROW_MISC, ROW_SGU_B = 0, 1, 2, 3, 4, 5, 6, 8
SMALL_ROWS = 16


def _params(**kw):
    return pltpu.CompilerParams(vmem_limit_bytes=VMEM_LIMIT, **kw)


def _sigmoid(x):
    return 0.5 * (jnp.tanh(0.5 * x) + 1.0)


def _place():
    return lax.axis_index("x"), lax.axis_index("y"), lax.axis_index("c")


def _all_gather(shards, name, in_vmem):
    n = len(shards)

    def body(*refs):
        ins, outs = refs[:n], refs[n:2 * n]
        send_sems, recv_sems, local_sems = refs[2 * n:]
        x, y, c = _place()
        me, sibling = (x, y, c), (x, y, 1 - c)
        chips = [(1 - x, y), (x, 1 - y), (1 - x, 1 - y)]
        first, passed, mine = [], [], []
        for a in range(n):
            out_ref = outs[a]

            def slot(px, py, pc, out_ref=out_ref):
                return out_ref.at[4 * px + 2 * py + pc]

            def copy(k, block, to, src=None, a=a, slot=slot):
                return pltpu.make_async_remote_copy(
                    src_ref=slot(*block) if src is None else src, dst_ref=slot(*block),
                    send_sem=send_sems.at[a, k], recv_sem=recv_sems.at[a, k],
                    device_id=to, device_id_type=MESH)

            own = pltpu.make_async_copy(ins[a], slot(*me), local_sems.at[a])
            own.start()
            mine.append(own)
            mine_out = [copy(0, me, sibling, src=ins[a])]
            mine_out += [copy(1 + j, me, (*chip, c), src=ins[a]) for j, chip in enumerate(chips)]
            for cp in mine_out:
                cp.start()
            first += mine_out
            passed.append([copy(4 + j, (*chip, c), sibling) for j, chip in enumerate(chips)])
        for j, chip in enumerate(chips):
            for a in range(n):
                out_ref = outs[a]
                blk = out_ref.at[4 * chip[0] + 2 * chip[1] + c]
                pltpu.make_async_remote_copy(
                    src_ref=blk, dst_ref=blk, send_sem=send_sems.at[a, 1 + j], recv_sem=recv_sems.at[a, 1 + j],
                    device_id=me, device_id_type=MESH).wait_recv()
                passed[a][j].start()
        for a in range(n):
            out_ref = outs[a]
            blk = out_ref.at[4 * x + 2 * y + (1 - c)]
            pltpu.make_async_remote_copy(
                src_ref=blk, dst_ref=blk, send_sem=send_sems.at[a, 0], recv_sem=recv_sems.at[a, 0],
                device_id=me, device_id_type=MESH).wait_recv()
            for j, chip in enumerate(chips):
                blk = out_ref.at[4 * chip[0] + 2 * chip[1] + (1 - c)]
                pltpu.make_async_remote_copy(
                    src_ref=blk, dst_ref=blk, send_sem=send_sems.at[a, 4 + j], recv_sem=recv_sems.at[a, 4 + j],
                    device_id=me, device_id_type=MESH).wait_recv()
        for cp in first:
            cp.wait_send()
        for a in range(n):
            for cp in passed[a]:
                cp.wait_send()
        for cp in mine:
            cp.wait()

    space = pltpu.VMEM if in_vmem else pl.ANY
    spec = pl.BlockSpec(memory_space=space)
    return pl.pallas_call(
        body, name=name,
        out_shape=tuple(jax.ShapeDtypeStruct((N_DEV,) + s.shape, s.dtype) for s in shards),
        in_specs=[spec] * n, out_specs=tuple([spec] * n),
        scratch_shapes=[pltpu.SemaphoreType.DMA((n, 7)), pltpu.SemaphoreType.DMA((n, 7)),
                        pltpu.SemaphoreType.DMA((n,))],
        compiler_params=_params(),
    )(*shards)


def _pair_exchange(sends, name):
    n = len(sends)

    def body(*refs):
        ins, outs = refs[:n], refs[n:2 * n]
        send_sems, recv_sems = refs[2 * n:]
        x, y, c = _place()
        copies = []
        for a in range(n):
            for m in range(4):
                cp = pltpu.make_async_remote_copy(
                    src_ref=ins[a].at[m, 1 - c], dst_ref=outs[a].at[m],
                    send_sem=send_sems.at[a, m], recv_sem=recv_sems.at[a, m],
                    device_id=(x, y, 1 - c), device_id_type=MESH)
                cp.start()
                copies.append(cp)
        for cp in copies:
            cp.wait()

    spec = pl.BlockSpec(memory_space=pl.ANY)
    return pl.pallas_call(
        body, name=name,
        out_shape=tuple(jax.ShapeDtypeStruct((4,) + s.shape[2:], s.dtype) for s in sends),
        in_specs=[spec] * n, out_specs=tuple([spec] * n),
        scratch_shapes=[pltpu.SemaphoreType.DMA((n, 4)), pltpu.SemaphoreType.DMA((n, 4))],
        compiler_params=_params(),
    )(*sends)


_HBM = pl.BlockSpec(memory_space=pltpu.HBM)
_SEM = pl.BlockSpec(memory_space=pltpu.SEMAPHORE)
_EFFECT = pltpu.SideEffectType.DATAFLOW_SIDE_EFFECTING


def _chip_copies(pair_ref, land_ref, send_sems, recv_sems):
    x, y, c = _place()
    chips = [(1 - x, y), (x, 1 - y), (1 - x, 1 - y)]
    return [pltpu.make_async_remote_copy(
        src_ref=pair_ref.at[2 * chip[0] + chip[1]], dst_ref=land_ref.at[k],
        send_sem=send_sems.at[k], recv_sem=recv_sems.at[k],
        device_id=(*chip, c), device_id_type=MESH) for k, chip in enumerate(chips)]


def _chip_exchange_start(pair, name):
    def body(pair_ref, land_ref, send_sems, recv_sems, pair_thru, land_thru, token):
        for cp in _chip_copies(pair_ref, land_ref, send_sems, recv_sems):
            cp.start()
        token[...] = jnp.zeros_like(token)

    land_shape = (3,) + pair.shape[1:]
    return pl.pallas_call(
        body, name=name,
        out_shape=(pltpu.SemaphoreType.DMA((3,)), pltpu.SemaphoreType.DMA((3,)),
                   pltpu.HBM(pair.shape, pair.dtype), pltpu.HBM(land_shape, pair.dtype),
                   jax.ShapeDtypeStruct((8, 128), F32)),
        in_specs=(_HBM, _HBM), out_specs=(_SEM, _SEM, _HBM, _HBM, pl.BlockSpec(memory_space=pltpu.VMEM)),
        input_output_aliases={0: 2, 1: 3},
        compiler_params=pltpu.CompilerParams(has_side_effects=_EFFECT),
    )(pltpu.with_memory_space_constraint(pair, pltpu.HBM),
      pltpu.with_memory_space_constraint(lax.empty(land_shape, pair.dtype), pltpu.HBM))


def _chip_exchange_wait(send_sems, recv_sems, pair_thru, land_thru, after, name):
    def body(pair_ref, land_ref, send_sems, recv_sems, after_ref, pair_out, land_out):
        for cp in _chip_copies(pair_ref, land_ref, send_sems, recv_sems):
            cp.wait_send()
            cp.wait_recv()

    return pl.pallas_call(
        body, name=name,
        out_shape=(pltpu.HBM(pair_thru.shape, pair_thru.dtype), pltpu.HBM(land_thru.shape, land_thru.dtype)),
        in_specs=(_HBM, _HBM, _SEM, _SEM, pl.BlockSpec(memory_space=pl.ANY)), out_specs=(_HBM, _HBM),
        input_output_aliases={0: 0, 1: 1},
        compiler_params=pltpu.CompilerParams(has_side_effects=_EFFECT),
    )(pair_thru, land_thru, send_sems, recv_sems, after)


def _matmul(a, b, dims, out_dtype, tm, tn, name, dep=None):
    if dims == "nn":
        (m, k), n = a.shape, b.shape[1]
        a_spec = pl.BlockSpec((tm, k), lambda i, j: (i, 0))
        b_spec = pl.BlockSpec((k, tn), lambda i, j: (0, j))
        contract = ((1,), (0,))
    elif dims == "nt":
        (m, k), n = a.shape, b.shape[0]
        a_spec = pl.BlockSpec((tm, k), lambda i, j: (i, 0))
        b_spec = pl.BlockSpec((tn, k), lambda i, j: (j, 0))
        contract = ((1,), (1,))
    else:
        (k, m), n = a.shape, b.shape[1]
        a_spec = pl.BlockSpec((k, tm), lambda i, j: (0, i))
        b_spec = pl.BlockSpec((k, tn), lambda i, j: (0, j))
        contract = ((0,), (0,))
    assert m % tm == 0 and n % tn == 0 and a.dtype == BF16 and b.dtype == BF16

    def body(a_ref, b_ref, *rest):
        rest[-1][...] = lax.dot_general(a_ref[...], b_ref[...], (contract, ((), ())),
                                        preferred_element_type=F32).astype(out_dtype)

    deps = [] if dep is None else [dep]
    return pl.pallas_call(
        body, name=name, grid=(m // tm, n // tn),
        in_specs=[a_spec, b_spec] + [pl.BlockSpec((8, 128), lambda i, j: (0, 0))] * len(deps),
        out_specs=pl.BlockSpec((tm, tn), lambda i, j: (i, j)),
        out_shape=jax.ShapeDtypeStruct((m, n), out_dtype),
        compiler_params=_params(dimension_semantics=("arbitrary", "arbitrary")),
    )(a, b, *deps)


def _modulation(c_all, w_ada, b_ada_mine):
    def body(c_ref, w_ref, b_ref, act_ref, mod_ref):
        cv = c_ref[...]
        act = cv * _sigmoid(cv)
        act_ref[...] = act
        mod_ref[...] = jnp.dot(act.astype(BF16), w_ref[...].astype(BF16), preferred_element_type=F32) + b_ref[...]

    return pl.pallas_call(
        body, name="modulation",
        out_shape=(jax.ShapeDtypeStruct(c_all.shape, F32), jax.ShapeDtypeStruct((N_DEV, W_ADA_SHARD), F32)),
        compiler_params=_params(),
    )(c_all, w_ada, b_ada_mine)


def _modulated_norm(x, norm_g, scale, shift, tm=256):
    t, d = x.shape

    def body(x_ref, g_ref, sc_ref, sh_ref, h_ref):
        xv = x_ref[...]
        r = lax.rsqrt(jnp.mean(xv * xv, axis=-1, keepdims=True) + EPS)
        h = (xv * r) * g_ref[...] * (1.0 + sc_ref[...]) + sh_ref[...]
        h_ref[...] = h.astype(BF16)

    row = pl.BlockSpec((1, d), lambda i: (0, 0))
    return pl.pallas_call(
        body, name="modulated_norm", grid=(t // tm,),
        in_specs=[pl.BlockSpec((tm, d), lambda i: (i, 0)), row, row, row],
        out_specs=pl.BlockSpec((tm, d), lambda i: (i, 0)),
        out_shape=jax.ShapeDtypeStruct((t, d), BF16),
        compiler_params=_params(dimension_semantics=("arbitrary",)),
    )(x, norm_g, scale, shift)


def _window_bias(block_index):
    s = lax.broadcasted_iota(jnp.int32, (2 * BLOCK, BLOCK), 0)
    t = lax.broadcasted_iota(jnp.int32, (2 * BLOCK, BLOCK), 1)
    valid = ((s < BLOCK) & (s > t) & (block_index > 0)) | ((s >= BLOCK) & ((s - BLOCK) <= t))
    bias = jnp.where(valid, 0.0, -jnp.inf).astype(F32)
    return jnp.concatenate([bias] * 8, axis=1)


def _heads_t(pair_blocks, g):
    top = lax.broadcasted_iota(jnp.int32, (BLOCK, BLOCK), 0) < HEAD_DIM
    zeros = jnp.zeros((HEAD_DIM, BLOCK), F32)
    tiles = []
    for blk in pair_blocks:
        tp = blk.T
        if g == 0:
            tiles += [jnp.where(top, tp, 0.0), jnp.concatenate([tp[HEAD_DIM:], zeros], axis=0)]
        else:
            tiles += [jnp.concatenate([zeros, tp[:HEAD_DIM]], axis=0), jnp.where(top, 0.0, tp)]
    return jnp.concatenate(tiles, axis=1)


def _pair_block(xt, p, g):
    r0 = HEAD_DIM * g
    even = xt[r0:r0 + HEAD_DIM, (2 * p) * BLOCK:(2 * p + 1) * BLOCK]
    odd = xt[r0:r0 + HEAD_DIM, (2 * p + 1) * BLOCK:(2 * p + 2) * BLOCK]
    return jnp.concatenate([even, odd], axis=0).T


def _softmax_t(scores_t, bias, sink):
    st = scores_t * ATTN_SCALE + bias
    m = jnp.maximum(jnp.max(st, axis=0, keepdims=True), sink)
    e = jnp.exp(st - m)
    es = jnp.exp(sink - m)
    inv = 1.0 / (jnp.sum(e, axis=0, keepdims=True) + es)
    return e * inv, es * inv


def _dot(a, b):
    return jnp.dot(a, b, preferred_element_type=F32)


def _dot_nt(a, b):
    return lax.dot_general(a, b, (((1,), (1,)), ((), ())), preferred_element_type=F32)


def _layer_norm_fwd(v):
    mu = jnp.mean(v, axis=-1, keepdims=True)
    xc = v - mu
    rstd = lax.rsqrt(jnp.mean(xc * xc, axis=-1, keepdims=True) + EPS)
    return xc * rstd, rstd


def _tril(transposed=False):
    t = lax.broadcasted_iota(jnp.int32, (BLOCK, BLOCK), 0)
    s = lax.broadcasted_iota(jnp.int32, (BLOCK, BLOCK), 1)
    return s >= t if transposed else t >= s


def _const_spec(shape):
    return pl.BlockSpec(shape, lambda i: (0,) * len(shape))


def _kv_prev_spec(index):
    return pl.BlockSpec((BLOCK, 2 * D_KV), lambda i: (jnp.maximum(index(i) - 1, 0), SEG_KV // (2 * D_KV)))


def _keys_values(z_ref, kvp_ref):
    kvp, kvc = kvp_ref[...], z_ref[:, SEG_KV:SEG_KV + 2 * D_KV]
    kk = jnp.concatenate([kvp[:, :D_KV], kvc[:, :D_KV]], axis=0)
    vv = jnp.concatenate([kvp[:, D_KV:], kvc[:, D_KV:]], axis=0)
    return kk, vv


def _pair_cols(g, p, base=0):
    return slice(base + (4 * g + p) * 128, base + (4 * g + p + 1) * 128)


def _mixer_fwd(z, sink_rows, ln_g, ln_b, sgu_w, sgu_bt):
    t = z.shape[0]

    def body(z_ref, kvp_ref, sink_ref, lng_ref, lnb_ref, w_ref, bt_ref, a_ref):
        bias = _window_bias(pl.program_id(0))
        kk, vv = _keys_values(z_ref, kvp_ref)
        kk_b, vvt_b = kk.astype(BF16), vv.T.astype(BF16)
        for g in range(2):
            qt = _heads_t([z_ref[:, _pair_cols(g, p, SEG_Q)] for p in range(4)], g).astype(BF16)
            prob, _ = _softmax_t(_dot(kk_b, qt), bias, sink_ref[g])
            ot = _dot(vvt_b, prob.astype(BF16))
            for p in range(4):
                gate = z_ref[:, _pair_cols(g, p, SEG_GA)]
                a_ref[:, _pair_cols(g, p)] = (_pair_block(ot, p, g) * (gate * _sigmoid(gate))).astype(BF16)

        vhat, _ = _layer_norm_fwd(z_ref[:, SEG_VS:SEG_VS + D_SGU])
        vn = vhat * lng_ref[...] + lnb_ref[...]
        tril = _tril()
        for g in range(SGU_GROUPS):
            cols = slice(g * 128, (g + 1) * 128)
            wm = jnp.where(tril, w_ref[g], 0.0).astype(BF16)
            mixed = _dot(wm, vn[:, cols].astype(BF16)) + bt_ref[:, g:g + 1]
            gate = z_ref[:, SEG_GS + g * 128:SEG_GS + (g + 1) * 128]
            a_ref[:, D_ATTN + g * 128:D_ATTN + (g + 1) * 128] = (
                (z_ref[:, SEG_U + g * 128:SEG_U + (g + 1) * 128] * mixed) * (gate * _sigmoid(gate))).astype(BF16)

    return pl.pallas_call(
        body, name="mixer_fwd", grid=(t // BLOCK,),
        in_specs=[pl.BlockSpec((BLOCK, D_IN), lambda i: (i, 0)), _kv_prev_spec(lambda i: i),
                  _const_spec((2, 1, 8 * BLOCK)), _const_spec((1, D_SGU)), _const_spec((1, D_SGU)),
                  _const_spec((SGU_GROUPS, BLOCK, BLOCK)), _const_spec((BLOCK, SGU_GROUPS))],
        out_specs=pl.BlockSpec((BLOCK, D_MODEL), lambda i: (i, 0)),
        out_shape=jax.ShapeDtypeStruct((t, D_MODEL), BF16),
        compiler_params=_params(dimension_semantics=("arbitrary",)),
    )(z, z, sink_rows, ln_g, ln_b, sgu_w, sgu_bt)


def _mixer_bwd(z, da, sink_rows, ln_g, ln_b, sgu_w, sgu_wt, sgu_bt):
    t = z.shape[0]
    nb = t // BLOCK

    def body(z_ref, kvp_ref, da_ref, sink_ref, lng_ref, lnb_ref, w_ref, wt_ref, bt_ref,
             dz_ref, dsink_ref, dw_ref, db_ref, dlng_ref, dlnb_ref, carry_ref, dsink_acc, dbt_acc):
        step = pl.program_id(0)

        @pl.when(step == 0)
        def _():
            carry_ref[...] = jnp.zeros_like(carry_ref)
            dsink_acc[...] = jnp.zeros_like(dsink_acc)
            dbt_acc[...] = jnp.zeros_like(dbt_acc)
            dw_ref[...] = jnp.zeros_like(dw_ref)
            dlng_ref[...] = jnp.zeros_like(dlng_ref)
            dlnb_ref[...] = jnp.zeros_like(dlnb_ref)

        bias = _window_bias(nb - 1 - step)
        kk, vv = _keys_values(z_ref, kvp_ref)
        kk_b, vv_b = kk.astype(BF16), vv.astype(BF16)
        kkt_b, vvt_b = kk.T.astype(BF16), vv.T.astype(BF16)
        dkk = jnp.zeros((2 * BLOCK, D_KV), F32)
        dvv = jnp.zeros((2 * BLOCK, D_KV), F32)
        for g in range(2):
            qt = _heads_t([z_ref[:, _pair_cols(g, p, SEG_Q)] for p in range(4)], g).astype(BF16)
            prob, sink_prob = _softmax_t(_dot(kk_b, qt), bias, sink_ref[g])
            prob_b = prob.astype(BF16)
            ot = _dot(vvt_b, prob_b)
            gates = [z_ref[:, _pair_cols(g, p, SEG_GA)] for p in range(4)]
            sig = [_sigmoid(gt) for gt in gates]
            d_attn = [da_ref[:, _pair_cols(g, p)] for p in range(4)]
            d_ot = _heads_t([d_attn[p] * (gates[p] * sig[p]) for p in range(4)], g).astype(BF16)
            d_prob = _dot(vv_b, d_ot)
            delta = jnp.sum(prob * d_prob, axis=0, keepdims=True)
            d_scores = (prob * (d_prob - delta) * ATTN_SCALE).astype(BF16)
            dsink_acc[g] -= sink_prob * delta
            d_qt = _dot(kkt_b, d_scores)
            dkk = dkk + _dot_nt(d_scores, qt)
            dvv = dvv + _dot_nt(prob_b, d_ot)
            for p in range(4):
                dz_ref[:, _pair_cols(g, p, SEG_Q)] = _pair_block(d_qt, p, g).astype(BF16)
                d_silu = sig[p] * (1.0 + gates[p] * (1.0 - sig[p]))
                dz_ref[:, _pair_cols(g, p, SEG_GA)] = (d_attn[p] * _pair_block(ot, p, g) * d_silu).astype(BF16)
        d_kv = jnp.concatenate([dkk, dvv], axis=1)
        dz_ref[:, SEG_KV:SEG_KV + 2 * D_KV] = (d_kv[BLOCK:] + carry_ref[...]).astype(BF16)
        carry_ref[...] = d_kv[:BLOCK]

        vhat, rstd = _layer_norm_fwd(z_ref[:, SEG_VS:SEG_VS + D_SGU])
        lng = lng_ref[...]
        vn = vhat * lng + lnb_ref[...]
        tril, triu = _tril(), _tril(transposed=True)
        lane = lax.broadcasted_iota(jnp.int32, (BLOCK, 128), 1)
        d_bt = jnp.zeros((BLOCK, 128), F32)
        d_vn = []
        for g in range(SGU_GROUPS):
            cols = slice(g * 128, (g + 1) * 128)
            wm = jnp.where(tril, w_ref[g], 0.0).astype(BF16)
            wmt = jnp.where(triu, wt_ref[g], 0.0).astype(BF16)
            vn_g = vn[:, cols].astype(BF16)
            mixed = _dot(wm, vn_g) + bt_ref[:, g:g + 1]
            gate = z_ref[:, SEG_GS + g * 128:SEG_GS + (g + 1) * 128]
            u = z_ref[:, SEG_U + g * 128:SEG_U + (g + 1) * 128]
            d_out = da_ref[:, D_ATTN + g * 128:D_ATTN + (g + 1) * 128]
            sg = _sigmoid(gate)
            d_um = d_out * (gate * sg)
            dz_ref[:, SEG_U + g * 128:SEG_U + (g + 1) * 128] = (d_um * mixed).astype(BF16)
            dz_ref[:, SEG_GS + g * 128:SEG_GS + (g + 1) * 128] = (
                d_out * (u * mixed) * (sg * (1.0 + gate * (1.0 - sg)))).astype(BF16)
            d_mixed = d_um * u
            d_mixed_b = d_mixed.astype(BF16)
            dw_ref[g] += jnp.where(tril, _dot_nt(d_mixed_b, vn_g), 0.0)
            d_bt = d_bt + jnp.where(lane == g, jnp.sum(d_mixed, axis=-1, keepdims=True), 0.0)
            d_vn.append(_dot(wmt, d_mixed_b))
        dbt_acc[...] += d_bt
        d_vn = jnp.concatenate(d_vn, axis=1)
        dlng_ref[...] += jnp.sum(d_vn * vhat, axis=0, keepdims=True)
        dlnb_ref[...] += jnp.sum(d_vn, axis=0, keepdims=True)
        d_vhat = d_vn * lng
        d_v = rstd * (d_vhat - jnp.mean(d_vhat, axis=-1, keepdims=True)
                      - vhat * jnp.mean(d_vhat * vhat, axis=-1, keepdims=True))
        dz_ref[:, SEG_VS:SEG_VS + D_SGU] = d_v.astype(BF16)

        @pl.when(step == nb - 1)
        def _():
            db_ref[...] = dbt_acc[...].T[:SGU_GROUPS]
            lane_row = lax.broadcasted_iota(jnp.int32, (1, 128), 1)
            d_sink = jnp.zeros((1, 128), F32)
            for g in range(2):
                acc = dsink_acc[g]
                for j in range(8):
                    head_sum = jnp.sum(acc[:, j * BLOCK:(j + 1) * BLOCK], axis=-1, keepdims=True)
                    d_sink = d_sink + jnp.where(lane_row == 8 * g + j, head_sum, 0.0)
            dsink_ref[...] = d_sink

    rev = lambda i: nb - 1 - i
    return pl.pallas_call(
        body, name="mixer_bwd", grid=(nb,),
        in_specs=[pl.BlockSpec((BLOCK, D_IN), lambda i: (rev(i), 0)), _kv_prev_spec(rev),
                  pl.BlockSpec((BLOCK, D_MODEL), lambda i: (rev(i), 0)),
                  _const_spec((2, 1, 8 * BLOCK)), _const_spec((1, D_SGU)), _const_spec((1, D_SGU)),
                  _const_spec((SGU_GROUPS, BLOCK, BLOCK)), _const_spec((SGU_GROUPS, BLOCK, BLOCK)),
                  _const_spec((BLOCK, SGU_GROUPS))],
        out_specs=(pl.BlockSpec((BLOCK, D_IN), lambda i: (rev(i), 0)), _const_spec((1, 128)),
                   _const_spec((SGU_GROUPS, BLOCK, BLOCK)), _const_spec((SGU_GROUPS, BLOCK)),
                   _const_spec((1, D_SGU)), _const_spec((1, D_SGU))),
        out_shape=(jax.ShapeDtypeStruct((t, D_IN), BF16), jax.ShapeDtypeStruct((1, 128), F32),
                   jax.ShapeDtypeStruct((SGU_GROUPS, BLOCK, BLOCK), F32), jax.ShapeDtypeStruct((SGU_GROUPS, BLOCK), F32),
                   jax.ShapeDtypeStruct((1, D_SGU), F32), jax.ShapeDtypeStruct((1, D_SGU), F32)),
        scratch_shapes=[pltpu.VMEM((BLOCK, 2 * D_KV), F32), pltpu.VMEM((2, 1, 8 * BLOCK), F32),
                        pltpu.VMEM((BLOCK, 128), F32)],
        compiler_params=_params(dimension_semantics=("arbitrary",)),
    )(z, z, da, sink_rows, ln_g, ln_b, sgu_w, sgu_wt, sgu_bt)


def _head(y, x, target, gate, final_g, tm=256):
    t, d = x.shape

    def body(y_ref, x_ref, tg_ref, gate_ref, fg_ref, dx2_ref, dy_ref, loss_ref, dfg_ref, dgate_ref):
        @pl.when(pl.program_id(0) == 0)
        def _():
            loss_ref[...] = jnp.zeros_like(loss_ref)
            dfg_ref[...] = jnp.zeros_like(dfg_ref)
            dgate_ref[...] = jnp.zeros_like(dgate_ref)

        yv, gate, fg = y_ref[...], gate_ref[...], fg_ref[...]
        x2 = x_ref[...] + gate * yv
        r2 = lax.rsqrt(jnp.mean(x2 * x2, axis=-1, keepdims=True) + EPS)
        nrm = x2 * r2
        err = nrm * fg - tg_ref[...]
        loss_ref[...] += 0.5 * jnp.sum(jnp.mean(err * err, axis=-1, keepdims=True), axis=0, keepdims=True)
        d_out = err * (1.0 / d)
        dfg_ref[...] += jnp.sum(d_out * nrm, axis=0, keepdims=True)
        d_nrm = d_out * fg
        dx2 = r2 * (d_nrm - nrm * jnp.mean(d_nrm * nrm, axis=-1, keepdims=True))
        dx2_ref[...] = dx2
        dgate_ref[...] += jnp.sum(dx2 * yv, axis=0, keepdims=True)
        dy_ref[...] = (dx2 * gate).astype(BF16)

    blk = pl.BlockSpec((tm, d), lambda i: (i, 0))
    row = _const_spec((1, d))
    return pl.pallas_call(
        body, name="head", grid=(t // tm,),
        in_specs=[blk, blk, blk, row, row],
        out_specs=(blk, blk, _const_spec((1, 128)), row, row),
        out_shape=(jax.ShapeDtypeStruct((t, d), F32), jax.ShapeDtypeStruct((t, d), BF16),
                   jax.ShapeDtypeStruct((1, 128), F32), jax.ShapeDtypeStruct((1, d), F32),
                   jax.ShapeDtypeStruct((1, d), F32)),
        compiler_params=_params(dimension_semantics=("arbitrary",)),
    )(y, x, target, gate, final_g)


def _modulated_norm_bwd(dh, x, dx2, norm_g, scale, tm=256):
    t, d = x.shape

    def body(dh_ref, x_ref, dx2_ref, g_ref, sc_ref, gx_ref, dshift_ref, dscale_ref, dg_ref):
        @pl.when(pl.program_id(0) == 0)
        def _():
            dshift_ref[...] = jnp.zeros_like(dshift_ref)
            dscale_ref[...] = jnp.zeros_like(dscale_ref)
            dg_ref[...] = jnp.zeros_like(dg_ref)

        dh, xv, g = dh_ref[...], x_ref[...], g_ref[...]
        one_plus = 1.0 + sc_ref[...]
        r = lax.rsqrt(jnp.mean(xv * xv, axis=-1, keepdims=True) + EPS)
        xn = xv * r
        dshift_ref[...] += jnp.sum(dh, axis=0, keepdims=True)
        dscale_ref[...] += jnp.sum(dh * (xn * g), axis=0, keepdims=True)
        d_y = dh * one_plus
        dg_ref[...] += jnp.sum(d_y * xn, axis=0, keepdims=True)
        d_xn = d_y * g
        gx_ref[...] = dx2_ref[...] + r * (d_xn - xn * jnp.mean(d_xn * xn, axis=-1, keepdims=True))

    blk = pl.BlockSpec((tm, d), lambda i: (i, 0))
    row = _const_spec((1, d))
    return pl.pallas_call(
        body, name="modulated_norm_bwd", grid=(t // tm,),
        in_specs=[blk, blk, blk, row, row], out_specs=(blk, row, row, row),
        out_shape=(jax.ShapeDtypeStruct((t, d), F32),) + (jax.ShapeDtypeStruct((1, d), F32),) * 3,
        compiler_params=_params(dimension_semantics=("arbitrary",)),
    )(dh, x, dx2, norm_g, scale)


def _pair_sum(core, blocks, got, name, tr):
    _, _, r, c = blocks.shape

    def body(core_ref, a_ref, b_ref, o_ref):
        o_ref[...] = (a_ref[...].astype(F32) + b_ref[...].astype(F32)).astype(BF16)

    return pl.pallas_call(
        body, name=name,
        grid_spec=pltpu.PrefetchScalarGridSpec(
            num_scalar_prefetch=1, grid=(4, r // tr),
            in_specs=[pl.BlockSpec((None, None, tr, c), lambda m, i, core_ref: (m, core_ref[0], i, 0)),
                      pl.BlockSpec((None, tr, c), lambda m, i, core_ref: (m, i, 0))],
            out_specs=pl.BlockSpec((None, tr, c), lambda m, i, core_ref: (m, i, 0))),
        out_shape=jax.ShapeDtypeStruct(got.shape, BF16),
        compiler_params=_params(dimension_semantics=("arbitrary", "arbitrary")),
    )(core, blocks, got)


def _adamw(w, g, m, v):
    m = ADAM_B1 * m + (1.0 - ADAM_B1) * g
    v = ADAM_B2 * v + (1.0 - ADAM_B2) * (g * g)
    m_hat = m / (1.0 - ADAM_B1 ** ADAM_STEP)
    v_hat = v / (1.0 - ADAM_B2 ** ADAM_STEP)
    delta = -ADAM_LR * (m_hat / (jnp.sqrt(v_hat) + ADAM_EPS) + ADAM_WD * w)
    return delta, m, v


def _adam_from_chips(chip, pair, land, w, m, v, name, tc):
    _, r, c = pair.shape

    def body(chip_ref, own_ref, land_ref, w_ref, m_ref, v_ref, g_ref, d_ref, nm_ref, nv_ref):
        g = own_ref[...].astype(F32)
        for k in range(3):
            g = g + land_ref[k].astype(F32)
        g_ref[...] = g
        d_ref[...], nm_ref[...], nv_ref[...] = _adamw(w_ref[...], g, m_ref[...], v_ref[...])

    blk = pl.BlockSpec((r, tc), lambda i, chip_ref: (0, i))
    return pl.pallas_call(
        body, name=name,
        grid_spec=pltpu.PrefetchScalarGridSpec(
            num_scalar_prefetch=1, grid=(c // tc,),
            in_specs=[pl.BlockSpec((None, r, tc), lambda i, chip_ref: (chip_ref[0], 0, i)),
                      pl.BlockSpec((3, r, tc), lambda i, chip_ref: (0, 0, i)), blk, blk, blk],
            out_specs=(blk,) * 4),
        out_shape=(jax.ShapeDtypeStruct((r, c), F32),) * 4,
        compiler_params=_params(dimension_semantics=("arbitrary",)),
    )(chip, pair, land, w, m, v)


def _adam_w_ada(act_t, dmod_mine, w, m, v, tr=256):
    r, c = w.shape

    def body(a_ref, dm_ref, w_ref, m_ref, v_ref, g_ref, d_ref, nm_ref, nv_ref):
        g = _dot(a_ref[...].astype(BF16), dm_ref[...].astype(BF16))
        g_ref[...] = g
        d_ref[...], nm_ref[...], nv_ref[...] = _adamw(w_ref[...], g, m_ref[...], v_ref[...])

    blk = pl.BlockSpec((tr, c), lambda i: (i, 0))
    return pl.pallas_call(
        body, name="adam_w_ada", grid=(r // tr,),
        in_specs=[pl.BlockSpec((tr, N_DEV), lambda i: (i, 0)), _const_spec((N_DEV, c)), blk, blk, blk],
        out_specs=(blk,) * 4, out_shape=(jax.ShapeDtypeStruct((r, c), F32),) * 4,
        compiler_params=_params(dimension_semantics=("arbitrary",)),
    )(act_t, dmod_mine, w, m, v)


def _pack_small(d_shift, d_scale, d_gate, d_norm_g, d_final_g, d_ln_g, d_ln_b, loss, d_sinks, d_sgu_b):
    def body(shift_ref, scale_ref, gate_ref, ng_ref, fg_ref, lng_ref, lnb_ref, loss_ref, sink_ref, b_ref, o_ref):
        o_ref[...] = jnp.zeros_like(o_ref)
        o_ref[ROW_SHIFT:ROW_SHIFT + 1, :] = shift_ref[...]
        o_ref[ROW_SCALE:ROW_SCALE + 1, :] = scale_ref[...]
        o_ref[ROW_GATE:ROW_GATE + 1, :] = gate_ref[...]
        o_ref[ROW_NORM_G:ROW_NORM_G + 1, :] = ng_ref[...]
        o_ref[ROW_FINAL_G:ROW_FINAL_G + 1, :] = fg_ref[...]
        o_ref[ROW_LN:ROW_LN + 1, 0:D_SGU] = lng_ref[...]
        o_ref[ROW_LN:ROW_LN + 1, D_SGU:2 * D_SGU] = lnb_ref[...]
        o_ref[ROW_MISC:ROW_MISC + 1, 0:128] = loss_ref[...]
        o_ref[ROW_MISC:ROW_MISC + 1, 128:256] = sink_ref[...]
        o_ref[ROW_SGU_B:ROW_SGU_B + SGU_GROUPS, 0:BLOCK] = b_ref[...]

    return pl.pallas_call(
        body, name="pack_small", out_shape=jax.ShapeDtypeStruct((SMALL_ROWS, D_MODEL), F32),
        compiler_params=_params(),
    )(d_shift, d_scale, d_gate, d_norm_g, d_final_g, d_ln_g, d_ln_b, loss, d_sinks, d_sgu_b)


_SMALL_NAMES = ("norm_g", "b_ada", "attn_sinks", "sgu_ln_g", "sgu_ln_b", "sgu_w", "sgu_b", "final_g")


def _adam_small(partials, d_sgu_w_all, weights, moments_m, moments_v):
    names = _SMALL_NAMES
    k = len(names)

    def body(*refs):
        p_ref, sw_ref = refs[0], refs[1]
        w_refs, m_refs, v_refs = refs[2:2 + k], refs[2 + k:2 + 2 * k], refs[2 + 2 * k:2 + 3 * k]
        loss_ref, dmod_ref = refs[2 + 3 * k], refs[3 + 3 * k]
        out_refs = refs[4 + 3 * k:4 + 7 * k]
        sum_ref = refs[4 + 7 * k]
        total = p_ref[0]
        for j in range(1, N_DEV):
            total = total + p_ref[j]
        sum_ref[...] = total
        for j in range(N_DEV):
            for part, row in enumerate((ROW_SHIFT, ROW_SCALE, ROW_GATE)):
                dmod_ref[j:j + 1, part * D_MODEL:(part + 1) * D_MODEL] = p_ref[j, row:row + 1, :]
        loss_ref[...] = sum_ref[ROW_MISC:ROW_MISC + 1, 0:1]
        d_sgu_w = sw_ref[0]
        for j in range(1, N_DEV):
            d_sgu_w = d_sgu_w + sw_ref[j]
        grads = {
            "norm_g": sum_ref[ROW_NORM_G:ROW_NORM_G + 1, :],
            "b_ada": jnp.concatenate([sum_ref[r:r + 1, :] for r in (ROW_SHIFT, ROW_SCALE, ROW_GATE)], axis=1),
            "attn_sinks": sum_ref[ROW_MISC:ROW_MISC + 1, 128:128 + N_Q_HEADS],
            "sgu_ln_g": sum_ref[ROW_LN:ROW_LN + 1, 0:D_SGU],
            "sgu_ln_b": sum_ref[ROW_LN:ROW_LN + 1, D_SGU:2 * D_SGU],
            "sgu_w": d_sgu_w[None],
            "sgu_b": sum_ref[ROW_SGU_B:ROW_SGU_B + SGU_GROUPS, 0:BLOCK][None],
            "final_g": sum_ref[ROW_FINAL_G:ROW_FINAL_G + 1, :],
        }
        for i, name in enumerate(names):
            g = grads[name]
            delta, m, v = _adamw(w_refs[i][...], g, m_refs[i][...], v_refs[i][...])
            out_refs[4 * i][...] = g
            out_refs[4 * i + 1][...] = delta
            out_refs[4 * i + 2][...] = m
            out_refs[4 * i + 3][...] = v

    shapes = [jax.ShapeDtypeStruct((1, 1), F32), jax.ShapeDtypeStruct((N_DEV, 3 * D_MODEL), F32)]
    for name in names:
        shapes += [jax.ShapeDtypeStruct(weights[name].shape, F32)] * 4
    outs = pl.pallas_call(
        body, name="adam_small", out_shape=tuple(shapes),
        scratch_shapes=[pltpu.VMEM((SMALL_ROWS, D_MODEL), F32)],
        compiler_params=_params(),
    )(partials, d_sgu_w_all, *[weights[n] for n in names], *[moments_m[n] for n in names],
      *[moments_v[n] for n in names])
    return outs[0], outs[1], {name: outs[2 + 4 * i:6 + 4 * i] for i, name in enumerate(names)}


def kernel(x, c, norm_g, w_ada, b_ada, w_in, attn_sinks, sgu_ln_g, sgu_ln_b, sgu_w, sgu_b, w_out, final_g, loss_target, m_norm_g, m_w_ada, m_b_ada, m_w_in, m_attn_sinks, m_sgu_ln_g, m_sgu_ln_b, m_sgu_w, m_sgu_b, m_w_out, m_final_g, v_norm_g, v_w_ada, v_b_ada, v_w_in, v_attn_sinks, v_sgu_ln_g, v_sgu_ln_b, v_sgu_w, v_sgu_b, v_w_out, v_final_g):
    xi, yi, ci = _place()
    me = 4 * xi + 2 * yi + ci
    x2d, target = x[0], loss_target[0]
    t = x2d.shape[0]

    c_all = _all_gather([c.reshape(8, 256)], "gather_c", True)[0].reshape(N_DEV, D_MODEL)
    b_mine = lax.dynamic_slice(b_ada, (0, me * W_ADA_SHARD), (1, W_ADA_SHARD))
    c_act, mod_part = _modulation(c_all, w_ada[0], b_mine)
    mod_all = _all_gather([mod_part], "gather_mod", True)[0]
    mod = lax.dynamic_index_in_dim(mod_all, me, axis=1, keepdims=False).reshape(1, 3 * D_MODEL)
    shift, scale, gate = mod[:, :D_MODEL], mod[:, D_MODEL:2 * D_MODEL], mod[:, 2 * D_MODEL:]

    w_in_all, w_out_all = _all_gather([w_in[0].T.astype(BF16), w_out[0].astype(BF16)], "gather_w", False)
    w_in_t = w_in_all.reshape(D_IN, D_MODEL)
    w_out_full = w_out_all.reshape(D_MODEL, D_MODEL)

    h = _modulated_norm(x2d, norm_g, scale, shift)
    z = _matmul(h, w_in_t, "nt", F32, t, 768, "z_proj")
    sink_rows = jnp.repeat(attn_sinks.reshape(N_Q_HEADS), BLOCK).reshape(2, 1, 8 * BLOCK)
    sgu_bt = sgu_b[0].T
    a = _mixer_fwd(z, sink_rows, sgu_ln_g, sgu_ln_b, sgu_w[0], sgu_bt)
    y = _matmul(a, w_out_full, "nn", F32, min(t, 1024), 1024, "out_proj")
    final_g_row = final_g.reshape(1, D_MODEL)
    dx2, dy, loss_part, d_final_g, d_gate = _head(y, x2d, target, gate, final_g_row)

    da = _matmul(dy, w_out_full, "nt", F32, min(t, 1024), 1024, "out_proj_bwd")
    core = ci.astype(jnp.int32).reshape(1)
    chip = (2 * xi + yi).astype(jnp.int32).reshape(1)
    dw_out = _matmul(a, dy, "tn", BF16, 1024, 1024, "w_out_grad").reshape(4, 2, W_OUT_SHARD, D_MODEL)
    got_out = _pair_exchange([dw_out], "w_out_grad_pair_exchange")[0]
    pair_out = _pair_sum(core, dw_out, got_out, "w_out_grad_pair_sum", W_OUT_SHARD)
    out_flight = _chip_exchange_start(pair_out, "w_out_grad_chip_start")
    dz, d_sinks, d_sgu_w, d_sgu_b, d_ln_g, d_ln_b = _mixer_bwd(
        z, da, sink_rows + out_flight[4][0, 0], sgu_ln_g, sgu_ln_b, sgu_w[0], jnp.swapaxes(sgu_w[0], 1, 2), sgu_bt)
    dw_in_t = _matmul(dz, h, "tn", BF16, 768, D_MODEL, "w_in_grad").reshape(4, 2, W_IN_SHARD, D_MODEL)
    got_in = _pair_exchange([dw_in_t], "w_in_grad_pair_exchange")[0]
    pair_in = _pair_sum(core, dw_in_t, got_in, "w_in_grad_pair_sum", W_IN_SHARD // 2)
    in_flight = _chip_exchange_start(pair_in, "w_in_grad_chip_start")
    dh = _matmul(dz, w_in_t, "nn", F32, min(t, 1024), 512, "z_proj_bwd", dep=in_flight[4])
    grad_x, d_shift, d_scale, d_norm_g = _modulated_norm_bwd(dh, x2d, dx2, norm_g, scale)

    partial = _pack_small(d_shift, d_scale, d_gate, d_norm_g, d_final_g, d_ln_g, d_ln_b, loss_part, d_sinks, d_sgu_b)
    partial_all, d_sgu_w_all = _all_gather([partial, d_sgu_w], "gather_small", True)
    weights = {"norm_g": norm_g, "b_ada": b_ada, "attn_sinks": attn_sinks, "sgu_ln_g": sgu_ln_g,
               "sgu_ln_b": sgu_ln_b, "sgu_w": sgu_w, "sgu_b": sgu_b, "final_g": final_g_row}
    moments_m = {"norm_g": m_norm_g, "b_ada": m_b_ada, "attn_sinks": m_attn_sinks, "sgu_ln_g": m_sgu_ln_g,
                 "sgu_ln_b": m_sgu_ln_b, "sgu_w": m_sgu_w, "sgu_b": m_sgu_b,
                 "final_g": m_final_g.reshape(1, D_MODEL)}
    moments_v = {"norm_g": v_norm_g, "b_ada": v_b_ada, "attn_sinks": v_attn_sinks, "sgu_ln_g": v_sgu_ln_g,
                 "sgu_ln_b": v_sgu_ln_b, "sgu_w": v_sgu_w, "sgu_b": v_sgu_b,
                 "final_g": v_final_g.reshape(1, D_MODEL)}
    loss, dmod_all, small = _adam_small(partial_all, d_sgu_w_all, weights, moments_m, moments_v)
    small["final_g"] = tuple(o.reshape(D_MODEL) for o in small["final_g"])

    dmod_mine = lax.dynamic_slice(dmod_all, (0, me * W_ADA_SHARD), (N_DEV, W_ADA_SHARD))
    big = {"w_ada": _adam_w_ada(c_act.T, dmod_mine, w_ada[0], m_w_ada[0], v_w_ada[0])}
    pair_out, land_out = _chip_exchange_wait(*out_flight[:4], big["w_ada"][0], "w_out_grad_chip_wait")
    big["w_out"] = _adam_from_chips(chip, pair_out, land_out, w_out[0], m_w_out[0], v_w_out[0], "adam_w_out", 1024)
    pair_in, land_in = _chip_exchange_wait(*in_flight[:4], big["w_out"][0], "w_in_grad_chip_wait")
    big["w_in"] = tuple(o.T for o in _adam_from_chips(
        chip, pair_in, land_in, w_in[0].T, m_w_in[0].T, v_w_in[0].T, "adam_w_in", 256))
    order = ["norm_g", "w_ada", "b_ada", "w_in", "attn_sinks", "sgu_ln_g", "sgu_ln_b", "sgu_w", "sgu_b", "w_out",
             "final_g"]
    outs = [loss.reshape(()), grad_x[None]]
    for k in range(4):
        for name in order:
            outs.append(big[name][k][None] if name in big else small[name][k])
    return tuple(outs)
```

```python
import jax
import jax.numpy as jnp
from jax import lax
from jax.experimental import pallas as pl
from jax.experimental.pallas import tpu as pltpu

F32 = jnp.float32
BF16 = jnp.bfloat16
MESH = pl.DeviceIdType.MESH

N_DEV = 8
D_MODEL = 2048
HEAD_DIM = 64
D_ATTN = 1024
N_Q_HEADS = 16
D_KV = 128
BLOCK = 128
D_SGU = 1024
SGU_GROUPS = 8
D_IN = 5376
W_IN_SHARD = D_IN // N_DEV
W_OUT_SHARD = D_MODEL // N_DEV
W_ADA_SHARD = 3 * D_MODEL // N_DEV
EPS = 1e-6
ATTN_SCALE = 0.125

ADAM_LR = 0.001
ADAM_B1 = 0.9
ADAM_B2 = 0.999
ADAM_EPS = 1e-08
ADAM_WD = 0.01
ADAM_STEP = 10

SEG_Q, SEG_KV, SEG_GA, SEG_U, SEG_VS, SEG_GS = 0, 1024, 1280, 2304, 3328, 4352

VMEM_LIMIT = 56 * 1024 * 1024

ROW_SHIFT, ROW_SCALE, ROW_GATE, ROW_NORM_G, ROW_FINAL_G, ROW_LN, ROW_MISC, ROW_SGU_B = 0, 1, 2, 3, 4, 5, 6, 8
SMALL_ROWS = 16


def _params(**kw):
    return pltpu.CompilerParams(vmem_limit_bytes=VMEM_LIMIT, **kw)


def _sigmoid(x):
    return 0.5 * (jnp.tanh(0.5 * x) + 1.0)


def _place():
    return lax.axis_index("x"), lax.axis_index("y"), lax.axis_index("c")


def _all_gather(shards, name, in_vmem):
    n = len(shards)

    def body(*refs):
        ins, outs = refs[:n], refs[n:2 * n]
        send_sems, recv_sems, local_sems = refs[2 * n:]
        x, y, c = _place()
        me, sibling = (x, y, c), (x, y, 1 - c)
        chips = [(1 - x, y), (x, 1 - y), (1 - x, 1 - y)]
        first, passed, mine = [], [], []
        for a in range(n):
            out_ref = outs[a]

            def slot(px, py, pc, out_ref=out_ref):
                return out_ref.at[4 * px + 2 * py + pc]

            def copy(k, block, to, src=None, a=a, slot=slot):
                return pltpu.make_async_remote_copy(
                    src_ref=slot(*block) if src is None else src, dst_ref=slot(*block),
                    send_sem=send_sems.at[a, k], recv_sem=recv_sems.at[a, k],
                    device_id=to, device_id_type=MESH)

            own = pltpu.make_async_copy(ins[a], slot(*me), local_sems.at[a])
            own.start()
            mine.append(own)
            mine_out = [copy(0, me, sibling, src=ins[a])]
            mine_out += [copy(1 + j, me, (*chip, c), src=ins[a]) for j, chip in enumerate(chips)]
            for cp in mine_out:
                cp.start()
            first += mine_out
            passed.append([copy(4 + j, (*chip, c), sibling) for j, chip in enumerate(chips)])
        for j, chip in enumerate(chips):
            for a in range(n):
                out_ref = outs[a]
                blk = out_ref.at[4 * chip[0] + 2 * chip[1] + c]
                pltpu.make_async_remote_copy(
                    src_ref=blk, dst_ref=blk, send_sem=send_sems.at[a, 1 + j], recv_sem=recv_sems.at[a, 1 + j],
                    device_id=me, device_id_type=MESH).wait_recv()
                passed[a][j].start()
        for a in range(n):
            out_ref = outs[a]
            blk = out_ref.at[4 * x + 2 * y + (1 - c)]
            pltpu.make_async_remote_copy(
                src_ref=blk, dst_ref=blk, send_sem=send_sems.at[a, 0], recv_sem=recv_sems.at[a, 0],
                device_id=me, device_id_type=MESH).wait_recv()
            for j, chip in enumerate(chips):
                blk = out_ref.at[4 * chip[0] + 2 * chip[1] + (1 - c)]
                pltpu.make_async_remote_copy(
                    src_ref=blk, dst_ref=blk, send_sem=send_sems.at[a, 4 + j], recv_sem=recv_sems.at[a, 4 + j],
                    device_id=me, device_id_type=MESH).wait_recv()
        for cp in first:
            cp.wait_send()
        for a in range(n):
            for cp in passed[a]:
                cp.wait_send()
        for cp in mine:
            cp.wait()

    space = pltpu.VMEM if in_vmem else pl.ANY
    spec = pl.BlockSpec(memory_space=space)
    return pl.pallas_call(
        body, name=name,
        out_shape=tuple(jax.ShapeDtypeStruct((N_DEV,) + s.shape, s.dtype) for s in shards),
        in_specs=[spec] * n, out_specs=tuple([spec] * n),
        scratch_shapes=[pltpu.SemaphoreType.DMA((n, 7)), pltpu.SemaphoreType.DMA((n, 7)),
                        pltpu.SemaphoreType.DMA((n,))],
        compiler_params=_params(),
    )(*shards)


def _pair_exchange(sends, name):
    n = len(sends)

    def body(*refs):
        ins, outs = refs[:n], refs[n:2 * n]
        send_sems, recv_sems = refs[2 * n:]
        x, y, c = _place()
        copies = []
        for a in range(n):
            for m in range(4):
                cp = pltpu.make_async_remote_copy(
                    src_ref=ins[a].at[m, 1 - c], dst_ref=outs[a].at[m],
                    send_sem=send_sems.at[a, m], recv_sem=recv_sems.at[a, m],
                    device_id=(x, y, 1 - c), device_id_type=MESH)
                cp.start()
                copies.append(cp)
        for cp in copies:
            cp.wait()

    spec = pl.BlockSpec(memory_space=pl.ANY)
    return pl.pallas_call(
        body, name=name,
        out_shape=tuple(jax.ShapeDtypeStruct((4,) + s.shape[2:], s.dtype) for s in sends),
        in_specs=[spec] * n, out_specs=tuple([spec] * n),
        scratch_shapes=[pltpu.SemaphoreType.DMA((n, 4)), pltpu.SemaphoreType.DMA((n, 4))],
        compiler_params=_params(),
    )(*sends)


_HBM = pl.BlockSpec(memory_space=pltpu.HBM)
_SEM = pl.BlockSpec(memory_space=pltpu.SEMAPHORE)
_EFFECT = pltpu.SideEffectType.DATAFLOW_SIDE_EFFECTING


def _start_copies(bufs, copies, n_copies, after, name):
    nb = len(bufs)

    def body(*refs):
        for cp in copies(refs[:nb], refs[nb + 1], refs[nb + 2]):
            cp.start()
        refs[-1][...] = jnp.zeros_like(refs[-1])

    out = pl.pallas_call(
        body, name=name,
        out_shape=(pltpu.SemaphoreType.DMA((n_copies,)), pltpu.SemaphoreType.DMA((n_copies,)),
                   *[pltpu.HBM(b.shape, b.dtype) for b in bufs], jax.ShapeDtypeStruct((8, 128), F32)),
        in_specs=(_HBM,) * nb + (pl.BlockSpec(memory_space=pl.ANY),),
        out_specs=(_SEM, _SEM) + (_HBM,) * nb + (pl.BlockSpec(memory_space=pltpu.VMEM),),
        input_output_aliases={i: 2 + i for i in range(nb)},
        compiler_params=pltpu.CompilerParams(has_side_effects=_EFFECT),
    )(*[pltpu.with_memory_space_constraint(b, pltpu.HBM) for b in bufs], after)
    return out[0], out[1], list(out[2:2 + nb]), out[-1]


def _wait_copies(flight, copies, after, name):
    send_sems, recv_sems, bufs, _ = flight
    nb = len(bufs)

    def body(*refs):
        for cp in copies(refs[:nb], refs[nb], refs[nb + 1]):
            cp.wait_send()
            cp.wait_recv()

    return pl.pallas_call(
        body, name=name,
        out_shape=tuple(pltpu.HBM(b.shape, b.dtype) for b in bufs),
        in_specs=(_HBM,) * nb + (_SEM, _SEM, pl.BlockSpec(memory_space=pl.ANY)), out_specs=(_HBM,) * nb,
        input_output_aliases={i: i for i in range(nb)},
        compiler_params=pltpu.CompilerParams(has_side_effects=_EFFECT),
    )(*bufs, send_sems, recv_sems, after)


def _chip_copies(refs, send_sems, recv_sems):
    pair_ref, land_ref = refs
    x, y, c = _place()
    chips = [(1 - x, y), (x, 1 - y), (1 - x, 1 - y)]
    return [pltpu.make_async_remote_copy(
        src_ref=pair_ref.at[2 * chip[0] + chip[1]], dst_ref=land_ref.at[k],
        send_sem=send_sems.at[k], recv_sem=recv_sems.at[k],
        device_id=(*chip, c), device_id_type=MESH) for k, chip in enumerate(chips)]


def _own_block_copies(targets):
    def copies(refs, send_sems, recv_sems):
        x, y, c = _place()
        mine = refs[0].at[4 * x + 2 * y + c]
        return [pltpu.make_async_remote_copy(
            src_ref=mine, dst_ref=mine, send_sem=send_sems.at[k], recv_sem=recv_sems.at[k],
            device_id=to, device_id_type=MESH) for k, to in enumerate(targets(x, y, c))]
    return copies


def _my_core_and_sibling(x, y, c):
    return [(x, y, 1 - c), (1 - x, y, c), (x, 1 - y, c), (1 - x, 1 - y, c)]


def _all_others(x, y, c):
    flip = lambda v, f: 1 - v if f else v
    return [(flip(x, r & 4), flip(y, r & 2), flip(c, r & 1)) for r in range(1, N_DEV)]


def _forward_copies(refs, send_sems, recv_sems):
    x, y, c = _place()
    chips = [(1 - x, y), (x, 1 - y), (1 - x, 1 - y)]
    return [pltpu.make_async_remote_copy(
        src_ref=refs[0].at[4 * chip[0] + 2 * chip[1] + c], dst_ref=refs[0].at[4 * chip[0] + 2 * chip[1] + c],
        send_sem=send_sems.at[k], recv_sem=recv_sems.at[k],
        device_id=(x, y, 1 - c), device_id_type=MESH) for k, chip in enumerate(chips)]


def _with_own_slot(block, me):
    return lax.dynamic_update_index_in_dim(lax.empty((N_DEV,) + block.shape, block.dtype), block, me, 0)


def _matmul(a, b, dims, out_dtype, tm, tn, name, dep=None):
    if dims == "nn":
        (m, k), n = a.shape, b.shape[1]
        a_spec = pl.BlockSpec((tm, k), lambda i, j: (i, 0))
        b_spec = pl.BlockSpec((k, tn), lambda i, j: (0, j))
        contract = ((1,), (0,))
    elif dims == "nt":
        (m, k), n = a.shape, b.shape[0]
        a_spec = pl.BlockSpec((tm, k), lambda i, j: (i, 0))
        b_spec = pl.BlockSpec((tn, k), lambda i, j: (j, 0))
        contract = ((1,), (1,))
    else:
        (k, m), n = a.shape, b.shape[1]
        a_spec = pl.BlockSpec((k, tm), lambda i, j: (0, i))
        b_spec = pl.BlockSpec((k, tn), lambda i, j: (0, j))
        contract = ((0,), (0,))
    assert m % tm == 0 and n % tn == 0 and a.dtype == BF16 and b.dtype == BF16

    def body(a_ref, b_ref, *rest):
        rest[-1][...] = lax.dot_general(a_ref[...], b_ref[...], (contract, ((), ())),
                                        preferred_element_type=F32).astype(out_dtype)

    deps = [] if dep is None else [dep]
    return pl.pallas_call(
        body, name=name, grid=(m // tm, n // tn),
        in_specs=[a_spec, b_spec] + [pl.BlockSpec((8, 128), lambda i, j: (0, 0))] * len(deps),
        out_specs=pl.BlockSpec((tm, tn), lambda i, j: (i, j)),
        out_shape=jax.ShapeDtypeStruct((m, n), out_dtype),
        compiler_params=_params(dimension_semantics=("arbitrary", "arbitrary")),
    )(a, b, *deps)


def _modulation(c_all, w_ada, b_ada_mine):
    def body(c_ref, w_ref, b_ref, act_ref, mod_ref):
        cv = c_ref[...]
        act = cv * _sigmoid(cv)
        act_ref[...] = act
        mod_ref[...] = jnp.dot(act.astype(BF16), w_ref[...].astype(BF16), preferred_element_type=F32) + b_ref[...]

    return pl.pallas_call(
        body, name="modulation",
        out_shape=(jax.ShapeDtypeStruct(c_all.shape, F32), jax.ShapeDtypeStruct((N_DEV, W_ADA_SHARD), F32)),
        compiler_params=_params(),
    )(c_all, w_ada, b_ada_mine)


def _modulated_norm(x, norm_g, scale, shift, tm=256):
    t, d = x.shape

    def body(x_ref, g_ref, sc_ref, sh_ref, h_ref):
        xv = x_ref[...]
        r = lax.rsqrt(jnp.mean(xv * xv, axis=-1, keepdims=True) + EPS)
        h = (xv * r) * g_ref[...] * (1.0 + sc_ref[...]) + sh_ref[...]
        h_ref[...] = h.astype(BF16)

    row = pl.BlockSpec((1, d), lambda i: (0, 0))
    return pl.pallas_call(
        body, name="modulated_norm", grid=(t // tm,),
        in_specs=[pl.BlockSpec((tm, d), lambda i: (i, 0)), row, row, row],
        out_specs=pl.BlockSpec((tm, d), lambda i: (i, 0)),
        out_shape=jax.ShapeDtypeStruct((t, d), BF16),
        compiler_params=_params(dimension_semantics=("arbitrary",)),
    )(x, norm_g, scale, shift)


def _window_bias(block_index):
    s = lax.broadcasted_iota(jnp.int32, (2 * BLOCK, BLOCK), 0)
    t = lax.broadcasted_iota(jnp.int32, (2 * BLOCK, BLOCK), 1)
    valid = ((s < BLOCK) & (s > t) & (block_index > 0)) | ((s >= BLOCK) & ((s - BLOCK) <= t))
    bias = jnp.where(valid, 0.0, -jnp.inf).astype(F32)
    return jnp.concatenate([bias] * 8, axis=1)


def _heads_t(pair_blocks, g):
    top = lax.broadcasted_iota(jnp.int32, (BLOCK, BLOCK), 0) < HEAD_DIM
    zeros = jnp.zeros((HEAD_DIM, BLOCK), F32)
    tiles = []
    for blk in pair_blocks:
        tp = blk.T
        if g == 0:
            tiles += [jnp.where(top, tp, 0.0), jnp.concatenate([tp[HEAD_DIM:], zeros], axis=0)]
        else:
            tiles += [jnp.concatenate([zeros, tp[:HEAD_DIM]], axis=0), jnp.where(top, 0.0, tp)]
    return jnp.concatenate(tiles, axis=1)


def _pair_block(xt, p, g):
    r0 = HEAD_DIM * g
    even = xt[r0:r0 + HEAD_DIM, (2 * p) * BLOCK:(2 * p + 1) * BLOCK]
    odd = xt[r0:r0 + HEAD_DIM, (2 * p + 1) * BLOCK:(2 * p + 2) * BLOCK]
    return jnp.concatenate([even, odd], axis=0).T


def _softmax_t(scores_t, bias, sink):
    st = scores_t * ATTN_SCALE + bias
    m = jnp.maximum(jnp.max(st, axis=0, keepdims=True), sink)
    e = jnp.exp(st - m)
    es = jnp.exp(sink - m)
    inv = 1.0 / (jnp.sum(e, axis=0, keepdims=True) + es)
    return e * inv, es * inv


def _dot(a, b):
    return jnp.dot(a, b, preferred_element_type=F32)


def _dot_nt(a, b):
    return lax.dot_general(a, b, (((1,), (1,)), ((), ())), preferred_element_type=F32)


def _layer_norm_fwd(v):
    mu = jnp.mean(v, axis=-1, keepdims=True)
    xc = v - mu
    rstd = lax.rsqrt(jnp.mean(xc * xc, axis=-1, keepdims=True) + EPS)
    return xc * rstd, rstd


def _tril(transposed=False):
    t = lax.broadcasted_iota(jnp.int32, (BLOCK, BLOCK), 0)
    s = lax.broadcasted_iota(jnp.int32, (BLOCK, BLOCK), 1)
    return s >= t if transposed else t >= s


def _const_spec(shape):
    return pl.BlockSpec(shape, lambda i: (0,) * len(shape))


def _kv_prev_spec(index):
    return pl.BlockSpec((BLOCK, 2 * D_KV), lambda i: (jnp.maximum(index(i) - 1, 0), SEG_KV // (2 * D_KV)))


def _keys_values(z_ref, kvp_ref):
    kvp, kvc = kvp_ref[...], z_ref[:, SEG_KV:SEG_KV + 2 * D_KV]
    kk = jnp.concatenate([kvp[:, :D_KV], kvc[:, :D_KV]], axis=0)
    vv = jnp.concatenate([kvp[:, D_KV:], kvc[:, D_KV:]], axis=0)
    return kk, vv


def _pair_cols(g, p, base=0):
    return slice(base + (4 * g + p) * 128, base + (4 * g + p + 1) * 128)


def _mixer_fwd(z, sink_rows, ln_g, ln_b, sgu_w, sgu_bt):
    t = z.shape[0]

    def body(z_ref, kvp_ref, sink_ref, lng_ref, lnb_ref, w_ref, bt_ref, a_ref):
        bias = _window_bias(pl.program_id(0))
        kk, vv = _keys_values(z_ref, kvp_ref)
        kk_b, vvt_b = kk.astype(BF16), vv.T.astype(BF16)
        for g in range(2):
            qt = _heads_t([z_ref[:, _pair_cols(g, p, SEG_Q)] for p in range(4)], g).astype(BF16)
            prob, _ = _softmax_t(_dot(kk_b, qt), bias, sink_ref[g])
            ot = _dot(vvt_b, prob.astype(BF16))
            for p in range(4):
                gate = z_ref[:, _pair_cols(g, p, SEG_GA)]
                a_ref[:, _pair_cols(g, p)] = (_pair_block(ot, p, g) * (gate * _sigmoid(gate))).astype(BF16)

        vhat, _ = _layer_norm_fwd(z_ref[:, SEG_VS:SEG_VS + D_SGU])
        vn = vhat * lng_ref[...] + lnb_ref[...]
        tril = _tril()
        for g in range(SGU_GROUPS):
            cols = slice(g * 128, (g + 1) * 128)
            wm = jnp.where(tril, w_ref[g], 0.0).astype(BF16)
            mixed = _dot(wm, vn[:, cols].astype(BF16)) + bt_ref[:, g:g + 1]
            gate = z_ref[:, SEG_GS + g * 128:SEG_GS + (g + 1) * 128]
            a_ref[:, D_ATTN + g * 128:D_ATTN + (g + 1) * 128] = (
                (z_ref[:, SEG_U + g * 128:SEG_U + (g + 1) * 128] * mixed) * (gate * _sigmoid(gate))).astype(BF16)

    return pl.pallas_call(
        body, name="mixer_fwd", grid=(t // BLOCK,),
        in_specs=[pl.BlockSpec((BLOCK, D_IN), lambda i: (i, 0)), _kv_prev_spec(lambda i: i),
                  _const_spec((2, 1, 8 * BLOCK)), _const_spec((1, D_SGU)), _const_spec((1, D_SGU)),
                  _const_spec((SGU_GROUPS, BLOCK, BLOCK)), _const_spec((BLOCK, SGU_GROUPS))],
        out_specs=pl.BlockSpec((BLOCK, D_MODEL), lambda i: (i, 0)),
        out_shape=jax.ShapeDtypeStruct((t, D_MODEL), BF16),
        compiler_params=_params(dimension_semantics=("arbitrary",)),
    )(z, z, sink_rows, ln_g, ln_b, sgu_w, sgu_bt)


def _mixer_bwd(z, da, sink_rows, ln_g, ln_b, sgu_w, sgu_wt, sgu_bt):
    t = z.shape[0]
    nb = t // BLOCK

    def body(z_ref, kvp_ref, da_ref, sink_ref, lng_ref, lnb_ref, w_ref, wt_ref, bt_ref,
             dz_ref, dsink_ref, dw_ref, db_ref, dlng_ref, dlnb_ref, carry_ref, dsink_acc, dbt_acc):
        step = pl.program_id(0)

        @pl.when(step == 0)
        def _():
            carry_ref[...] = jnp.zeros_like(carry_ref)
            dsink_acc[...] = jnp.zeros_like(dsink_acc)
            dbt_acc[...] = jnp.zeros_like(dbt_acc)
            dw_ref[...] = jnp.zeros_like(dw_ref)
            dlng_ref[...] = jnp.zeros_like(dlng_ref)
            dlnb_ref[...] = jnp.zeros_like(dlnb_ref)

        bias = _window_bias(nb - 1 - step)
        kk, vv = _keys_values(z_ref, kvp_ref)
        kk_b, vv_b = kk.astype(BF16), vv.astype(BF16)
        kkt_b, vvt_b = kk.T.astype(BF16), vv.T.astype(BF16)
        dkk = jnp.zeros((2 * BLOCK, D_KV), F32)
        dvv = jnp.zeros((2 * BLOCK, D_KV), F32)
        for g in range(2):
            qt = _heads_t([z_ref[:, _pair_cols(g, p, SEG_Q)] for p in range(4)], g).astype(BF16)
            prob, sink_prob = _softmax_t(_dot(kk_b, qt), bias, sink_ref[g])
            prob_b = prob.astype(BF16)
            ot = _dot(vvt_b, prob_b)
            gates = [z_ref[:, _pair_cols(g, p, SEG_GA)] for p in range(4)]
            sig = [_sigmoid(gt) for gt in gates]
            d_attn = [da_ref[:, _pair_cols(g, p)] for p in range(4)]
            d_ot = _heads_t([d_attn[p] * (gates[p] * sig[p]) for p in range(4)], g).astype(BF16)
            d_prob = _dot(vv_b, d_ot)
            delta = jnp.sum(prob * d_prob, axis=0, keepdims=True)
            d_scores = (prob * (d_prob - delta) * ATTN_SCALE).astype(BF16)
            dsink_acc[g] -= sink_prob * delta
            d_qt = _dot(kkt_b, d_scores)
            dkk = dkk + _dot_nt(d_scores, qt)
            dvv = dvv + _dot_nt(prob_b, d_ot)
            for p in range(4):
                dz_ref[:, _pair_cols(g, p, SEG_Q)] = _pair_block(d_qt, p, g).astype(BF16)
                d_silu = sig[p] * (1.0 + gates[p] * (1.0 - sig[p]))
                dz_ref[:, _pair_cols(g, p, SEG_GA)] = (d_attn[p] * _pair_block(ot, p, g) * d_silu).astype(BF16)
        d_kv = jnp.concatenate([dkk, dvv], axis=1)
        dz_ref[:, SEG_KV:SEG_KV + 2 * D_KV] = (d_kv[BLOCK:] + carry_ref[...]).astype(BF16)
        carry_ref[...] = d_kv[:BLOCK]

        vhat, rstd = _layer_norm_fwd(z_ref[:, SEG_VS:SEG_VS + D_SGU])
        lng = lng_ref[...]
        vn = vhat * lng + lnb_ref[...]
        tril, triu = _tril(), _tril(transposed=True)
        lane = lax.broadcasted_iota(jnp.int32, (BLOCK, 128), 1)
        d_bt = jnp.zeros((BLOCK, 128), F32)
        d_vn = []
        for g in range(SGU_GROUPS):
            cols = slice(g * 128, (g + 1) * 128)
            wm = jnp.where(tril, w_ref[g], 0.0).astype(BF16)
            wmt = jnp.where(triu, wt_ref[g], 0.0).astype(BF16)
            vn_g = vn[:, cols].astype(BF16)
            mixed = _dot(wm, vn_g) + bt_ref[:, g:g + 1]
            gate = z_ref[:, SEG_GS + g * 128:SEG_GS + (g + 1) * 128]
            u = z_ref[:, SEG_U + g * 128:SEG_U + (g + 1) * 128]
            d_out = da_ref[:, D_ATTN + g * 128:D_ATTN + (g + 1) * 128]
            sg = _sigmoid(gate)
            d_um = d_out * (gate * sg)
            dz_ref[:, SEG_U + g * 128:SEG_U + (g + 1) * 128] = (d_um * mixed).astype(BF16)
            dz_ref[:, SEG_GS + g * 128:SEG_GS + (g + 1) * 128] = (
                d_out * (u * mixed) * (sg * (1.0 + gate * (1.0 - sg)))).astype(BF16)
            d_mixed = d_um * u
            d_mixed_b = d_mixed.astype(BF16)
            dw_ref[g] += jnp.where(tril, _dot_nt(d_mixed_b, vn_g), 0.0)
            d_bt = d_bt + jnp.where(lane == g, jnp.sum(d_mixed, axis=-1, keepdims=True), 0.0)
            d_vn.append(_dot(wmt, d_mixed_b))
        dbt_acc[...] += d_bt
        d_vn = jnp.concatenate(d_vn, axis=1)
        dlng_ref[...] += jnp.sum(d_vn * vhat, axis=0, keepdims=True)
        dlnb_ref[...] += jnp.sum(d_vn, axis=0, keepdims=True)
        d_vhat = d_vn * lng
        d_v = rstd * (d_vhat - jnp.mean(d_vhat, axis=-1, keepdims=True)
                      - vhat * jnp.mean(d_vhat * vhat, axis=-1, keepdims=True))
        dz_ref[:, SEG_VS:SEG_VS + D_SGU] = d_v.astype(BF16)

        @pl.when(step == nb - 1)
        def _():
            db_ref[...] = dbt_acc[...].T[:SGU_GROUPS]
            lane_row = lax.broadcasted_iota(jnp.int32, (1, 128), 1)
            d_sink = jnp.zeros((1, 128), F32)
            for g in range(2):
                acc = dsink_acc[g]
                for j in range(8):
                    head_sum = jnp.sum(acc[:, j * BLOCK:(j + 1) * BLOCK], axis=-1, keepdims=True)
                    d_sink = d_sink + jnp.where(lane_row == 8 * g + j, head_sum, 0.0)
            dsink_ref[...] = d_sink

    rev = lambda i: nb - 1 - i
    return pl.pallas_call(
        body, name="mixer_bwd", grid=(nb,),
        in_specs=[pl.BlockSpec((BLOCK, D_IN), lambda i: (rev(i), 0)), _kv_prev_spec(rev),
                  pl.BlockSpec((BLOCK, D_MODEL), lambda i: (rev(i), 0)),
                  _const_spec((2, 1, 8 * BLOCK)), _const_spec((1, D_SGU)), _const_spec((1, D_SGU)),
                  _const_spec((SGU_GROUPS, BLOCK, BLOCK)), _const_spec((SGU_GROUPS, BLOCK, BLOCK)),
                  _const_spec((BLOCK, SGU_GROUPS))],
        out_specs=(pl.BlockSpec((BLOCK, D_IN), lambda i: (rev(i), 0)), _const_spec((1, 128)),
                   _const_spec((SGU_GROUPS, BLOCK, BLOCK)), _const_spec((SGU_GROUPS, BLOCK)),
                   _const_spec((1, D_SGU)), _const_spec((1, D_SGU))),
        out_shape=(jax.ShapeDtypeStruct((t, D_IN), BF16), jax.ShapeDtypeStruct((1, 128), F32),
                   jax.ShapeDtypeStruct((SGU_GROUPS, BLOCK, BLOCK), F32), jax.ShapeDtypeStruct((SGU_GROUPS, BLOCK), F32),
                   jax.ShapeDtypeStruct((1, D_SGU), F32), jax.ShapeDtypeStruct((1, D_SGU), F32)),
        scratch_shapes=[pltpu.VMEM((BLOCK, 2 * D_KV), F32), pltpu.VMEM((2, 1, 8 * BLOCK), F32),
                        pltpu.VMEM((BLOCK, 128), F32)],
        compiler_params=_params(dimension_semantics=("arbitrary",)),
    )(z, z, da, sink_rows, ln_g, ln_b, sgu_w, sgu_wt, sgu_bt)


def _head(y, x, target, gate, final_g, tm=256):
    t, d = x.shape

    def body(y_ref, x_ref, tg_ref, gate_ref, fg_ref, dx2_ref, dy_ref, loss_ref, dfg_ref, dgate_ref):
        @pl.when(pl.program_id(0) == 0)
        def _():
            loss_ref[...] = jnp.zeros_like(loss_ref)
            dfg_ref[...] = jnp.zeros_like(dfg_ref)
            dgate_ref[...] = jnp.zeros_like(dgate_ref)

        yv, gate, fg = y_ref[...], gate_ref[...], fg_ref[...]
        x2 = x_ref[...] + gate * yv
        r2 = lax.rsqrt(jnp.mean(x2 * x2, axis=-1, keepdims=True) + EPS)
        nrm = x2 * r2
        err = nrm * fg - tg_ref[...]
        loss_ref[...] += 0.5 * jnp.sum(jnp.mean(err * err, axis=-1, keepdims=True), axis=0, keepdims=True)
        d_out = err * (1.0 / d)
        dfg_ref[...] += jnp.sum(d_out * nrm, axis=0, keepdims=True)
        d_nrm = d_out * fg
        dx2 = r2 * (d_nrm - nrm * jnp.mean(d_nrm * nrm, axis=-1, keepdims=True))
        dx2_ref[...] = dx2
        dgate_ref[...] += jnp.sum(dx2 * yv, axis=0, keepdims=True)
        dy_ref[...] = (dx2 * gate).astype(BF16)

    blk = pl.BlockSpec((tm, d), lambda i: (i, 0))
    row = _const_spec((1, d))
    return pl.pallas_call(
        body, name="head", grid=(t // tm,),
        in_specs=[blk, blk, blk, row, row],
        out_specs=(blk, blk, _const_spec((1, 128)), row, row),
        out_shape=(jax.ShapeDtypeStruct((t, d), F32), jax.ShapeDtypeStruct((t, d), BF16),
                   jax.ShapeDtypeStruct((1, 128), F32), jax.ShapeDtypeStruct((1, d), F32),
                   jax.ShapeDtypeStruct((1, d), F32)),
        compiler_params=_params(dimension_semantics=("arbitrary",)),
    )(y, x, target, gate, final_g)


def _modulated_norm_bwd(dh, x, dx2, norm_g, scale, tm=256):
    t, d = x.shape

    def body(dh_ref, x_ref, dx2_ref, g_ref, sc_ref, gx_ref, dshift_ref, dscale_ref, dg_ref):
        @pl.when(pl.program_id(0) == 0)
        def _():
            dshift_ref[...] = jnp.zeros_like(dshift_ref)
            dscale_ref[...] = jnp.zeros_like(dscale_ref)
            dg_ref[...] = jnp.zeros_like(dg_ref)

        dh, xv, g = dh_ref[...], x_ref[...], g_ref[...]
        one_plus = 1.0 + sc_ref[...]
        r = lax.rsqrt(jnp.mean(xv * xv, axis=-1, keepdims=True) + EPS)
        xn = xv * r
        dshift_ref[...] += jnp.sum(dh, axis=0, keepdims=True)
        dscale_ref[...] += jnp.sum(dh * (xn * g), axis=0, keepdims=True)
        d_y = dh * one_plus
        dg_ref[...] += jnp.sum(d_y * xn, axis=0, keepdims=True)
        d_xn = d_y * g
        gx_ref[...] = dx2_ref[...] + r * (d_xn - xn * jnp.mean(d_xn * xn, axis=-1, keepdims=True))

    blk = pl.BlockSpec((tm, d), lambda i: (i, 0))
    row = _const_spec((1, d))
    return pl.pallas_call(
        body, name="modulated_norm_bwd", grid=(t // tm,),
        in_specs=[blk, blk, blk, row, row], out_specs=(blk, row, row, row),
        out_shape=(jax.ShapeDtypeStruct((t, d), F32),) + (jax.ShapeDtypeStruct((1, d), F32),) * 3,
        compiler_params=_params(dimension_semantics=("arbitrary",)),
    )(dh, x, dx2, norm_g, scale)


def _pair_sum(core, blocks, got, name, tr):
    _, _, r, c = blocks.shape

    def body(core_ref, a_ref, b_ref, o_ref):
        o_ref[...] = (a_ref[...].astype(F32) + b_ref[...].astype(F32)).astype(BF16)

    return pl.pallas_call(
        body, name=name,
        grid_spec=pltpu.PrefetchScalarGridSpec(
            num_scalar_prefetch=1, grid=(4, r // tr),
            in_specs=[pl.BlockSpec((None, None, tr, c), lambda m, i, core_ref: (m, core_ref[0], i, 0)),
                      pl.BlockSpec((None, tr, c), lambda m, i, core_ref: (m, i, 0))],
            out_specs=pl.BlockSpec((None, tr, c), lambda m, i, core_ref: (m, i, 0))),
        out_shape=jax.ShapeDtypeStruct(got.shape, BF16),
        compiler_params=_params(dimension_semantics=("arbitrary", "arbitrary")),
    )(core, blocks, got)


def _adamw(w, g, m, v):
    m = ADAM_B1 * m + (1.0 - ADAM_B1) * g
    v = ADAM_B2 * v + (1.0 - ADAM_B2) * (g * g)
    m_hat = m / (1.0 - ADAM_B1 ** ADAM_STEP)
    v_hat = v / (1.0 - ADAM_B2 ** ADAM_STEP)
    delta = -ADAM_LR * (m_hat / (jnp.sqrt(v_hat) + ADAM_EPS) + ADAM_WD * w)
    return delta, m, v


def _adam_from_chips(chip, pair, land, w, m, v, name, tc):
    _, r, c = pair.shape

    def body(chip_ref, own_ref, land_ref, w_ref, m_ref, v_ref, g_ref, d_ref, nm_ref, nv_ref):
        g = own_ref[...].astype(F32)
        for k in range(3):
            g = g + land_ref[k].astype(F32)
        g_ref[...] = g
        d_ref[...], nm_ref[...], nv_ref[...] = _adamw(w_ref[...], g, m_ref[...], v_ref[...])

    blk = pl.BlockSpec((r, tc), lambda i, chip_ref: (0, i))
    return pl.pallas_call(
        body, name=name,
        grid_spec=pltpu.PrefetchScalarGridSpec(
            num_scalar_prefetch=1, grid=(c // tc,),
            in_specs=[pl.BlockSpec((None, r, tc), lambda i, chip_ref: (chip_ref[0], 0, i)),
                      pl.BlockSpec((3, r, tc), lambda i, chip_ref: (0, 0, i)), blk, blk, blk],
            out_specs=(blk,) * 4),
        out_shape=(jax.ShapeDtypeStruct((r, c), F32),) * 4,
        compiler_params=_params(dimension_semantics=("arbitrary",)),
    )(chip, pair, land, w, m, v)


def _adam_w_ada(act_t, dmod_mine, w, m, v, tr=256):
    r, c = w.shape

    def body(a_ref, dm_ref, w_ref, m_ref, v_ref, g_ref, d_ref, nm_ref, nv_ref):
        g = _dot(a_ref[...].astype(BF16), dm_ref[...].astype(BF16))
        g_ref[...] = g
        d_ref[...], nm_ref[...], nv_ref[...] = _adamw(w_ref[...], g, m_ref[...], v_ref[...])

    blk = pl.BlockSpec((tr, c), lambda i: (i, 0))
    return pl.pallas_call(
        body, name="adam_w_ada", grid=(r // tr,),
        in_specs=[pl.BlockSpec((tr, N_DEV), lambda i: (i, 0)), _const_spec((N_DEV, c)), blk, blk, blk],
        out_specs=(blk,) * 4, out_shape=(jax.ShapeDtypeStruct((r, c), F32),) * 4,
        compiler_params=_params(dimension_semantics=("arbitrary",)),
    )(act_t, dmod_mine, w, m, v)


def _pack_small(d_shift, d_scale, d_gate, d_norm_g, d_final_g, d_ln_g, d_ln_b, loss, d_sinks, d_sgu_b):
    def body(shift_ref, scale_ref, gate_ref, ng_ref, fg_ref, lng_ref, lnb_ref, loss_ref, sink_ref, b_ref, o_ref):
        o_ref[...] = jnp.zeros_like(o_ref)
        o_ref[ROW_SHIFT:ROW_SHIFT + 1, :] = shift_ref[...]
        o_ref[ROW_SCALE:ROW_SCALE + 1, :] = scale_ref[...]
        o_ref[ROW_GATE:ROW_GATE + 1, :] = gate_ref[...]
        o_ref[ROW_NORM_G:ROW_NORM_G + 1, :] = ng_ref[...]
        o_ref[ROW_FINAL_G:ROW_FINAL_G + 1, :] = fg_ref[...]
        o_ref[ROW_LN:ROW_LN + 1, 0:D_SGU] = lng_ref[...]
        o_ref[ROW_LN:ROW_LN + 1, D_SGU:2 * D_SGU] = lnb_ref[...]
        o_ref[ROW_MISC:ROW_MISC + 1, 0:128] = loss_ref[...]
        o_ref[ROW_MISC:ROW_MISC + 1, 128:256] = sink_ref[...]
        o_ref[ROW_SGU_B:ROW_SGU_B + SGU_GROUPS, 0:BLOCK] = b_ref[...]

    return pl.pallas_call(
        body, name="pack_small", out_shape=jax.ShapeDtypeStruct((SMALL_ROWS, D_MODEL), F32),
        compiler_params=_params(),
    )(d_shift, d_scale, d_gate, d_norm_g, d_final_g, d_ln_g, d_ln_b, loss, d_sinks, d_sgu_b)


_SMALL_NAMES = ("norm_g", "b_ada", "attn_sinks", "sgu_ln_g", "sgu_ln_b", "sgu_w", "sgu_b", "final_g")


def _adam_small(partials, d_sgu_w_all, weights, moments_m, moments_v):
    names = _SMALL_NAMES
    k = len(names)

    def body(*refs):
        p_ref, sw_ref = refs[0], refs[1]
        w_refs, m_refs, v_refs = refs[2:2 + k], refs[2 + k:2 + 2 * k], refs[2 + 2 * k:2 + 3 * k]
        loss_ref, dmod_ref = refs[2 + 3 * k], refs[3 + 3 * k]
        out_refs = refs[4 + 3 * k:4 + 7 * k]
        sum_ref = refs[4 + 7 * k]
        total = p_ref[0]
        for j in range(1, N_DEV):
            total = total + p_ref[j]
        sum_ref[...] = total
        for j in range(N_DEV):
            for part, row in enumerate((ROW_SHIFT, ROW_SCALE, ROW_GATE)):
                dmod_ref[j:j + 1, part * D_MODEL:(part + 1) * D_MODEL] = p_ref[j, row:row + 1, :]
        loss_ref[...] = sum_ref[ROW_MISC:ROW_MISC + 1, 0:1]
        d_sgu_w = sw_ref[0]
        for j in range(1, N_DEV):
            d_sgu_w = d_sgu_w + sw_ref[j]
        grads = {
            "norm_g": sum_ref[ROW_NORM_G:ROW_NORM_G + 1, :],
            "b_ada": jnp.concatenate([sum_ref[r:r + 1, :] for r in (ROW_SHIFT, ROW_SCALE, ROW_GATE)], axis=1),
            "attn_sinks": sum_ref[ROW_MISC:ROW_MISC + 1, 128:128 + N_Q_HEADS],
            "sgu_ln_g": sum_ref[ROW_LN:ROW_LN + 1, 0:D_SGU],
            "sgu_ln_b": sum_ref[ROW_LN:ROW_LN + 1, D_SGU:2 * D_SGU],
            "sgu_w": d_sgu_w[None],
            "sgu_b": sum_ref[ROW_SGU_B:ROW_SGU_B + SGU_GROUPS, 0:BLOCK][None],
            "final_g": sum_ref[ROW_FINAL_G:ROW_FINAL_G + 1, :],
        }
        for i, name in enumerate(names):
            g = grads[name]
            delta, m, v = _adamw(w_refs[i][...], g, m_refs[i][...], v_refs[i][...])
            out_refs[4 * i][...] = g
            out_refs[4 * i + 1][...] = delta
            out_refs[4 * i + 2][...] = m
            out_refs[4 * i + 3][...] = v

    shapes = [jax.ShapeDtypeStruct((1, 1), F32), jax.ShapeDtypeStruct((N_DEV, 3 * D_MODEL), F32)]
    for name in names:
        shapes += [jax.ShapeDtypeStruct(weights[name].shape, F32)] * 4
    outs = pl.pallas_call(
        body, name="adam_small", out_shape=tuple(shapes),
        scratch_shapes=[pltpu.VMEM((SMALL_ROWS, D_MODEL), F32)],
        compiler_params=_params(),
    )(partials, d_sgu_w_all, *[weights[n] for n in names], *[moments_m[n] for n in names],
      *[moments_v[n] for n in names])
    return outs[0], outs[1], {name: outs[2 + 4 * i:6 + 4 * i] for i, name in enumerate(names)}


def kernel(x, c, norm_g, w_ada, b_ada, w_in, attn_sinks, sgu_ln_g, sgu_ln_b, sgu_w, sgu_b, w_out, final_g, loss_target, m_norm_g, m_w_ada, m_b_ada, m_w_in, m_attn_sinks, m_sgu_ln_g, m_sgu_ln_b, m_sgu_w, m_sgu_b, m_w_out, m_final_g, v_norm_g, v_w_ada, v_b_ada, v_w_in, v_attn_sinks, v_sgu_ln_g, v_sgu_ln_b, v_sgu_w, v_sgu_b, v_w_out, v_final_g):
    xi, yi, ci = _place()
    me = 4 * xi + 2 * yi + ci
    x2d, target = x[0], loss_target[0]
    t = x2d.shape[0]

    c_all = _all_gather([c.reshape(8, 256)], "gather_c", True)[0].reshape(N_DEV, D_MODEL)
    b_mine = lax.dynamic_slice(b_ada, (0, me * W_ADA_SHARD), (1, W_ADA_SHARD))
    c_act, mod_part = _modulation(c_all, w_ada[0], b_mine)
    mod_all = _all_gather([mod_part], "gather_mod", True)[0]
    mod = lax.dynamic_index_in_dim(mod_all, me, axis=1, keepdims=False).reshape(1, 3 * D_MODEL)
    shift, scale, gate = mod[:, :D_MODEL], mod[:, D_MODEL:2 * D_MODEL], mod[:, 2 * D_MODEL:]

    w_in_all = _all_gather([w_in[0].T.astype(BF16)], "gather_w_in", False)[0]
    w_in_t = w_in_all.reshape(D_IN, D_MODEL)
    w_out_flight = _start_copies([_with_own_slot(w_out[0].astype(BF16), me)], _own_block_copies(_my_core_and_sibling),
                                 4, w_in_all, "gather_w_out_start")

    h = _modulated_norm(x2d, norm_g, scale, shift)
    z = _matmul(h, w_in_t, "nt", F32, t, 768, "z_proj", dep=w_out_flight[3])
    w_out_half = _wait_copies(w_out_flight, _own_block_copies(_my_core_and_sibling), z, "gather_w_out_wait")
    w_out_flight = _start_copies(w_out_half, _forward_copies, 3, z, "gather_w_out_forward_start")
    sink_rows = jnp.repeat(attn_sinks.reshape(N_Q_HEADS), BLOCK).reshape(2, 1, 8 * BLOCK)
    sgu_bt = sgu_b[0].T
    a = _mixer_fwd(z, sink_rows + w_out_flight[3][0, 0], sgu_ln_g, sgu_ln_b, sgu_w[0], sgu_bt)
    w_out_all = _wait_copies(w_out_flight, _forward_copies, a, "gather_w_out_forward_wait")[0]
    w_out_full = w_out_all.reshape(D_MODEL, D_MODEL)
    y = _matmul(a, w_out_full, "nn", F32, min(t, 1024), 1024, "out_proj")
    final_g_row = final_g.reshape(1, D_MODEL)
    dx2, dy, loss_part, d_final_g, d_gate = _head(y, x2d, target, gate, final_g_row)

    da = _matmul(dy, w_out_full, "nt", F32, min(t, 1024), 1024, "out_proj_bwd")
    core = ci.astype(jnp.int32).reshape(1)
    chip = (2 * xi + yi).astype(jnp.int32).reshape(1)
    dw_out = _matmul(a, dy, "tn", BF16, 1024, 1024, "w_out_grad").reshape(4, 2, W_OUT_SHARD, D_MODEL)
    got_out = _pair_exchange([dw_out], "w_out_grad_pair_exchange")[0]
    pair_out = _pair_sum(core, dw_out, got_out, "w_out_grad_pair_sum", W_OUT_SHARD)
    out_flight = _start_copies([pair_out, lax.empty((3, W_OUT_SHARD, D_MODEL), BF16)], _chip_copies, 3, core,
                               "w_out_grad_chip_start")
    dz, d_sinks, d_sgu_w, d_sgu_b, d_ln_g, d_ln_b = _mixer_bwd(
        z, da, sink_rows + out_flight[3][0, 0], sgu_ln_g, sgu_ln_b, sgu_w[0], jnp.swapaxes(sgu_w[0], 1, 2), sgu_bt)
    sgu_w_flight = _start_copies([_with_own_slot(d_sgu_w, me)], _own_block_copies(_all_others), N_DEV - 1, core,
                                 "sgu_w_grad_gather_start")
    dw_in_t = _matmul(dz, h, "tn", BF16, 768, D_MODEL, "w_in_grad", dep=sgu_w_flight[3])
    dw_in_t = dw_in_t.reshape(4, 2, W_IN_SHARD, D_MODEL)
    got_in = _pair_exchange([dw_in_t], "w_in_grad_pair_exchange")[0]
    pair_in = _pair_sum(core, dw_in_t, got_in, "w_in_grad_pair_sum", W_IN_SHARD // 2)
    in_flight = _start_copies([pair_in, lax.empty((3, W_IN_SHARD, D_MODEL), BF16)], _chip_copies, 3, core,
                              "w_in_grad_chip_start")
    dh = _matmul(dz, w_in_t, "nn", F32, min(t, 1024), 512, "z_proj_bwd", dep=in_flight[3])
    grad_x, d_shift, d_scale, d_norm_g = _modulated_norm_bwd(dh, x2d, dx2, norm_g, scale)

    partial = _pack_small(d_shift, d_scale, d_gate, d_norm_g, d_final_g, d_ln_g, d_ln_b, loss_part, d_sinks, d_sgu_b)
    partial_all = _all_gather([partial], "gather_small", True)[0]
    d_sgu_w_all = _wait_copies(sgu_w_flight, _own_block_copies(_all_others), partial_all, "sgu_w_grad_gather_wait")[0]
    weights = {"norm_g": norm_g, "b_ada": b_ada, "attn_sinks": attn_sinks, "sgu_ln_g": sgu_ln_g,
               "sgu_ln_b": sgu_ln_b, "sgu_w": sgu_w, "sgu_b": sgu_b, "final_g": final_g_row}
    moments_m = {"norm_g": m_norm_g, "b_ada": m_b_ada, "attn_sinks": m_attn_sinks, "sgu_ln_g": m_sgu_ln_g,
                 "sgu_ln_b": m_sgu_ln_b, "sgu_w": m_sgu_w, "sgu_b": m_sgu_b,
                 "final_g": m_final_g.reshape(1, D_MODEL)}
    moments_v = {"norm_g": v_norm_g, "b_ada": v_b_ada, "attn_sinks": v_attn_sinks, "sgu_ln_g": v_sgu_ln_g,
                 "sgu_ln_b": v_sgu_ln_b, "sgu_w": v_sgu_w, "sgu_b": v_sgu_b,
                 "final_g": v_final_g.reshape(1, D_MODEL)}
    loss, dmod_all, small = _adam_small(partial_all, d_sgu_w_all, weights, moments_m, moments_v)
    small["final_g"] = tuple(o.reshape(D_MODEL) for o in small["final_g"])

    dmod_mine = lax.dynamic_slice(dmod_all, (0, me * W_ADA_SHARD), (N_DEV, W_ADA_SHARD))
    big = {"w_ada": _adam_w_ada(c_act.T, dmod_mine, w_ada[0], m_w_ada[0], v_w_ada[0])}
    pair_out, land_out = _wait_copies(out_flight, _chip_copies, big["w_ada"][0], "w_out_grad_chip_wait")
    big["w_out"] = _adam_from_chips(chip, pair_out, land_out, w_out[0], m_w_out[0], v_w_out[0], "adam_w_out", 1024)
    pair_in, land_in = _wait_copies(in_flight, _chip_copies, big["w_out"][0], "w_in_grad_chip_wait")
    big["w_in"] = tuple(o.T for o in _adam_from_chips(
        chip, pair_in, land_in, w_in[0].T, m_w_in[0].T, v_w_in[0].T, "adam_w_in", 256))
    order = ["norm_g", "w_ada", "b_ada", "w_in", "attn_sinks", "sgu_ln_g", "sgu_ln_b", "sgu_w", "sgu_b", "w_out",
             "final_g"]
    outs = [loss.reshape(()), grad_x[None]]
    for k in range(4):
        for name in order:
            outs.append(big[name][k][None] if name in big else small[name][k])
    return tuple(outs)
```

```python
import jax
import jax.numpy as jnp
from jax import lax
from jax.experimental import pallas as pl
from jax.experimental.pallas import tpu as pltpu

F32 = jnp.float32
BF16 = jnp.bfloat16
MESH = pl.DeviceIdType.MESH

N_DEV = 8
D_MODEL = 2048
HEAD_DIM = 64
D_ATTN = 1024
N_Q_HEADS = 16
D_KV = 128
BLOCK = 128
D_SGU = 1024
SGU_GROUPS = 8
D_IN = 5376
W_IN_SHARD = D_IN // N_DEV
W_OUT_SHARD = D_MODEL // N_DEV
W_ADA_SHARD = 3 * D_MODEL // N_DEV
EPS = 1e-6
ATTN_SCALE = 0.125

ADAM_LR = 0.001
ADAM_B1 = 0.9
ADAM_B2 = 0.999
ADAM_EPS = 1e-08
ADAM_WD = 0.01
ADAM_STEP = 10

SEG_Q, SEG_KV, SEG_GA, SEG_U, SEG_VS, SEG_GS = 0, 1024, 1280, 2304, 3328, 4352

VMEM_LIMIT = 56 * 1024 * 1024

ROW_SHIFT, ROW_SCALE, ROW_GATE, ROW_NORM_G, ROW_FINAL_G, ROW_LN, ROW_MISC, ROW_SGU_B = 0, 1, 2, 3, 4, 5, 6, 8
SMALL_ROWS = 16


def _params(**kw):
    return pltpu.CompilerParams(vmem_limit_bytes=VMEM_LIMIT, **kw)


def _sigmoid(x):
    return 0.5 * (jnp.tanh(0.5 * x) + 1.0)


def _place():
    return lax.axis_index("x"), lax.axis_index("y"), lax.axis_index("c")


def _all_gather(shards, name, in_vmem):
    n = len(shards)

    def body(*refs):
        ins, outs = refs[:n], refs[n:2 * n]
        send_sems, recv_sems, local_sems = refs[2 * n:]
        x, y, c = _place()
        me, sibling = (x, y, c), (x, y, 1 - c)
        chips = [(1 - x, y), (x, 1 - y), (1 - x, 1 - y)]
        first, passed, mine = [], [], []
        for a in range(n):
            out_ref = outs[a]

            def slot(px, py, pc, out_ref=out_ref):
                return out_ref.at[4 * px + 2 * py + pc]

            def copy(k, block, to, src=None, a=a, slot=slot):
                return pltpu.make_async_remote_copy(
                    src_ref=slot(*block) if src is None else src, dst_ref=slot(*block),
                    send_sem=send_sems.at[a, k], recv_sem=recv_sems.at[a, k],
                    device_id=to, device_id_type=MESH)

            own = pltpu.make_async_copy(ins[a], slot(*me), local_sems.at[a])
            own.start()
            mine.append(own)
            mine_out = [copy(0, me, sibling, src=ins[a])]
            mine_out += [copy(1 + j, me, (*chip, c), src=ins[a]) for j, chip in enumerate(chips)]
            for cp in mine_out:
                cp.start()
            first += mine_out
            passed.append([copy(4 + j, (*chip, c), sibling) for j, chip in enumerate(chips)])
        for j, chip in enumerate(chips):
            for a in range(n):
                out_ref = outs[a]
                blk = out_ref.at[4 * chip[0] + 2 * chip[1] + c]
                pltpu.make_async_remote_copy(
                    src_ref=blk, dst_ref=blk, send_sem=send_sems.at[a, 1 + j], recv_sem=recv_sems.at[a, 1 + j],
                    device_id=me, device_id_type=MESH).wait_recv()
                passed[a][j].start()
        for a in range(n):
            out_ref = outs[a]
            blk = out_ref.at[4 * x + 2 * y + (1 - c)]
            pltpu.make_async_remote_copy(
                src_ref=blk, dst_ref=blk, send_sem=send_sems.at[a, 0], recv_sem=recv_sems.at[a, 0],
                device_id=me, device_id_type=MESH).wait_recv()
            for j, chip in enumerate(chips):
                blk = out_ref.at[4 * chip[0] + 2 * chip[1] + (1 - c)]
                pltpu.make_async_remote_copy(
                    src_ref=blk, dst_ref=blk, send_sem=send_sems.at[a, 4 + j], recv_sem=recv_sems.at[a, 4 + j],
                    device_id=me, device_id_type=MESH).wait_recv()
        for cp in first:
            cp.wait_send()
        for a in range(n):
            for cp in passed[a]:
                cp.wait_send()
        for cp in mine:
            cp.wait()

    space = pltpu.VMEM if in_vmem else pl.ANY
    spec = pl.BlockSpec(memory_space=space)
    return pl.pallas_call(
        body, name=name,
        out_shape=tuple(jax.ShapeDtypeStruct((N_DEV,) + s.shape, s.dtype) for s in shards),
        in_specs=[spec] * n, out_specs=tuple([spec] * n),
        scratch_shapes=[pltpu.SemaphoreType.DMA((n, 7)), pltpu.SemaphoreType.DMA((n, 7)),
                        pltpu.SemaphoreType.DMA((n,))],
        compiler_params=_params(),
    )(*shards)


def _pair_exchange(sends, name):
    n = len(sends)

    def body(*refs):
        ins, outs = refs[:n], refs[n:2 * n]
        send_sems, recv_sems = refs[2 * n:]
        x, y, c = _place()
        copies = []
        for a in range(n):
            for m in range(4):
                cp = pltpu.make_async_remote_copy(
                    src_ref=ins[a].at[m, 1 - c], dst_ref=outs[a].at[m],
                    send_sem=send_sems.at[a, m], recv_sem=recv_sems.at[a, m],
                    device_id=(x, y, 1 - c), device_id_type=MESH)
                cp.start()
                copies.append(cp)
        for cp in copies:
            cp.wait()

    spec = pl.BlockSpec(memory_space=pl.ANY)
    return pl.pallas_call(
        body, name=name,
        out_shape=tuple(jax.ShapeDtypeStruct((4,) + s.shape[2:], s.dtype) for s in sends),
        in_specs=[spec] * n, out_specs=tuple([spec] * n),
        scratch_shapes=[pltpu.SemaphoreType.DMA((n, 4)), pltpu.SemaphoreType.DMA((n, 4))],
        compiler_params=_params(),
    )(*sends)


_HBM = pl.BlockSpec(memory_space=pltpu.HBM)
_SEM = pl.BlockSpec(memory_space=pltpu.SEMAPHORE)
_EFFECT = pltpu.SideEffectType.DATAFLOW_SIDE_EFFECTING


def _start_copies(bufs, copies, n_copies, after, name):
    nb = len(bufs)

    def body(*refs):
        for cp in copies(refs[:nb], refs[nb + 1], refs[nb + 2]):
            cp.start()
        refs[-1][...] = jnp.zeros_like(refs[-1])

    out = pl.pallas_call(
        body, name=name,
        out_shape=(pltpu.SemaphoreType.DMA((n_copies,)), pltpu.SemaphoreType.DMA((n_copies,)),
                   *[pltpu.HBM(b.shape, b.dtype) for b in bufs], jax.ShapeDtypeStruct((8, 128), F32)),
        in_specs=(_HBM,) * nb + (pl.BlockSpec(memory_space=pl.ANY),),
        out_specs=(_SEM, _SEM) + (_HBM,) * nb + (pl.BlockSpec(memory_space=pltpu.VMEM),),
        input_output_aliases={i: 2 + i for i in range(nb)},
        compiler_params=pltpu.CompilerParams(has_side_effects=_EFFECT),
    )(*[pltpu.with_memory_space_constraint(b, pltpu.HBM) for b in bufs], after)
    return out[0], out[1], list(out[2:2 + nb]), out[-1]


def _wait_copies(flight, copies, after, name):
    send_sems, recv_sems, bufs, _ = flight
    nb = len(bufs)

    def body(*refs):
        for cp in copies(refs[:nb], refs[nb], refs[nb + 1]):
            cp.wait_send()
            cp.wait_recv()

    return pl.pallas_call(
        body, name=name,
        out_shape=tuple(pltpu.HBM(b.shape, b.dtype) for b in bufs),
        in_specs=(_HBM,) * nb + (_SEM, _SEM, pl.BlockSpec(memory_space=pl.ANY)), out_specs=(_HBM,) * nb,
        input_output_aliases={i: i for i in range(nb)},
        compiler_params=pltpu.CompilerParams(has_side_effects=_EFFECT),
    )(*bufs, send_sems, recv_sems, after)


def _chip_copies(refs, send_sems, recv_sems):
    pair_ref, land_ref = refs
    x, y, c = _place()
    chips = [(1 - x, y), (x, 1 - y), (1 - x, 1 - y)]
    return [pltpu.make_async_remote_copy(
        src_ref=pair_ref.at[2 * chip[0] + chip[1]], dst_ref=land_ref.at[k],
        send_sem=send_sems.at[k], recv_sem=recv_sems.at[k],
        device_id=(*chip, c), device_id_type=MESH) for k, chip in enumerate(chips)]


def _own_block_copies(targets):
    def copies(refs, send_sems, recv_sems):
        x, y, c = _place()
        mine = refs[0].at[4 * x + 2 * y + c]
        return [pltpu.make_async_remote_copy(
            src_ref=mine, dst_ref=mine, send_sem=send_sems.at[k], recv_sem=recv_sems.at[k],
            device_id=to, device_id_type=MESH) for k, to in enumerate(targets(x, y, c))]
    return copies


def _my_core_and_sibling(x, y, c):
    return [(x, y, 1 - c), (1 - x, y, c), (x, 1 - y, c), (1 - x, 1 - y, c)]


def _all_others(x, y, c):
    flip = lambda v, f: 1 - v if f else v
    return [(flip(x, r & 4), flip(y, r & 2), flip(c, r & 1)) for r in range(1, N_DEV)]


def _forward_copies(refs, send_sems, recv_sems):
    x, y, c = _place()
    chips = [(1 - x, y), (x, 1 - y), (1 - x, 1 - y)]
    return [pltpu.make_async_remote_copy(
        src_ref=refs[0].at[4 * chip[0] + 2 * chip[1] + c], dst_ref=refs[0].at[4 * chip[0] + 2 * chip[1] + c],
        send_sem=send_sems.at[k], recv_sem=recv_sems.at[k],
        device_id=(x, y, 1 - c), device_id_type=MESH) for k, chip in enumerate(chips)]


def _near_targets(x, y, c):
    return [(x, y, 1 - c), (1 - x, y, c), (x, 1 - y, c)]


def _second_stage_copies(refs, send_sems, recv_sems):
    x, y, c = _place()
    plan = [((1 - x, y, c), (x, y, 1 - c)), ((x, 1 - y, c), (x, y, 1 - c)), ((x, y, c), (1 - x, 1 - y, c))]
    copies = []
    for k, ((px, py, pc), to) in enumerate(plan):
        blk = refs[0].at[4 * px + 2 * py + pc]
        copies.append(pltpu.make_async_remote_copy(
            src_ref=blk, dst_ref=blk, send_sem=send_sems.at[k], recv_sem=recv_sems.at[k],
            device_id=to, device_id_type=MESH))
    return copies


def _diagonal_forward_copies(refs, send_sems, recv_sems):
    x, y, c = _place()
    blk = refs[0].at[4 * (1 - x) + 2 * (1 - y) + c]
    return [pltpu.make_async_remote_copy(
        src_ref=blk, dst_ref=blk, send_sem=send_sems.at[0], recv_sem=recv_sems.at[0],
        device_id=(x, y, 1 - c), device_id_type=MESH)]


def _with_own_slot(block, me):
    return lax.dynamic_update_index_in_dim(lax.empty((N_DEV,) + block.shape, block.dtype), block, me, 0)


def _matmul(a, b, dims, out_dtype, tm, tn, name, dep=None):
    if dims == "nn":
        (m, k), n = a.shape, b.shape[1]
        a_spec = pl.BlockSpec((tm, k), lambda i, j: (i, 0))
        b_spec = pl.BlockSpec((k, tn), lambda i, j: (0, j))
        contract = ((1,), (0,))
    elif dims == "nt":
        (m, k), n = a.shape, b.shape[0]
        a_spec = pl.BlockSpec((tm, k), lambda i, j: (i, 0))
        b_spec = pl.BlockSpec((tn, k), lambda i, j: (j, 0))
        contract = ((1,), (1,))
    else:
        (k, m), n = a.shape, b.shape[1]
        a_spec = pl.BlockSpec((k, tm), lambda i, j: (0, i))
        b_spec = pl.BlockSpec((k, tn), lambda i, j: (0, j))
        contract = ((0,), (0,))
    assert m % tm == 0 and n % tn == 0 and a.dtype == BF16 and b.dtype == BF16

    def body(a_ref, b_ref, *rest):
        rest[-1][...] = lax.dot_general(a_ref[...], b_ref[...], (contract, ((), ())),
                                        preferred_element_type=F32).astype(out_dtype)

    deps = [] if dep is None else [dep]
    return pl.pallas_call(
        body, name=name, grid=(m // tm, n // tn),
        in_specs=[a_spec, b_spec] + [pl.BlockSpec((8, 128), lambda i, j: (0, 0))] * len(deps),
        out_specs=pl.BlockSpec((tm, tn), lambda i, j: (i, j)),
        out_shape=jax.ShapeDtypeStruct((m, n), out_dtype),
        compiler_params=_params(dimension_semantics=("arbitrary", "arbitrary")),
    )(a, b, *deps)


Z_TILE = 768
_Z_TILE_ORDER = ((0, 1, 2, 3, 4, 5, 6), (0, 1, 2, 6, 3, 4, 5), (0, 4, 5, 6, 1, 2, 3), (2, 3, 4, 5, 0, 1, 6))
_Z_EARLY_TILES = 4


def _z_proj(h, w_in_t, order, first, count, z_prev, name):
    t = h.shape[0]

    def body(order_ref, h_ref, w_ref, *rest):
        rest[-1][...] = _dot_nt(h_ref[...], w_ref[...])

    prev = [] if z_prev is None else [z_prev]
    return pl.pallas_call(
        body, name=name,
        grid_spec=pltpu.PrefetchScalarGridSpec(
            num_scalar_prefetch=1, grid=(count,),
            in_specs=[pl.BlockSpec((t, D_MODEL), lambda j, o: (0, 0)),
                      pl.BlockSpec((Z_TILE, D_MODEL), lambda j, o: (o[first + j], 0))]
            + [pl.BlockSpec(memory_space=pl.ANY)] * len(prev),
            out_specs=pl.BlockSpec((t, Z_TILE), lambda j, o: (0, o[first + j]))),
        out_shape=jax.ShapeDtypeStruct((t, D_IN), F32),
        input_output_aliases={3: 0} if prev else {},
        compiler_params=_params(dimension_semantics=("arbitrary",)),
    )(order, h, w_in_t, *prev)


def _modulation(c_all, w_ada, b_ada_mine):
    def body(c_ref, w_ref, b_ref, act_ref, mod_ref):
        cv = c_ref[...]
        act = cv * _sigmoid(cv)
        act_ref[...] = act
        mod_ref[...] = jnp.dot(act.astype(BF16), w_ref[...].astype(BF16), preferred_element_type=F32) + b_ref[...]

    return pl.pallas_call(
        body, name="modulation",
        out_shape=(jax.ShapeDtypeStruct(c_all.shape, F32), jax.ShapeDtypeStruct((N_DEV, W_ADA_SHARD), F32)),
        compiler_params=_params(),
    )(c_all, w_ada, b_ada_mine)


def _modulated_norm(x, norm_g, scale, shift, tm=256):
    t, d = x.shape

    def body(x_ref, g_ref, sc_ref, sh_ref, h_ref):
        xv = x_ref[...]
        r = lax.rsqrt(jnp.mean(xv * xv, axis=-1, keepdims=True) + EPS)
        h = (xv * r) * g_ref[...] * (1.0 + sc_ref[...]) + sh_ref[...]
        h_ref[...] = h.astype(BF16)

    row = pl.BlockSpec((1, d), lambda i: (0, 0))
    return pl.pallas_call(
        body, name="modulated_norm", grid=(t // tm,),
        in_specs=[pl.BlockSpec((tm, d), lambda i: (i, 0)), row, row, row],
        out_specs=pl.BlockSpec((tm, d), lambda i: (i, 0)),
        out_shape=jax.ShapeDtypeStruct((t, d), BF16),
        compiler_params=_params(dimension_semantics=("arbitrary",)),
    )(x, norm_g, scale, shift)


def _window_bias(block_index):
    s = lax.broadcasted_iota(jnp.int32, (2 * BLOCK, BLOCK), 0)
    t = lax.broadcasted_iota(jnp.int32, (2 * BLOCK, BLOCK), 1)
    valid = ((s < BLOCK) & (s > t) & (block_index > 0)) | ((s >= BLOCK) & ((s - BLOCK) <= t))
    bias = jnp.where(valid, 0.0, -jnp.inf).astype(F32)
    return jnp.concatenate([bias] * 8, axis=1)


def _heads_t(pair_blocks, g):
    top = lax.broadcasted_iota(jnp.int32, (BLOCK, BLOCK), 0) < HEAD_DIM
    zeros = jnp.zeros((HEAD_DIM, BLOCK), F32)
    tiles = []
    for blk in pair_blocks:
        tp = blk.T
        if g == 0:
            tiles += [jnp.where(top, tp, 0.0), jnp.concatenate([tp[HEAD_DIM:], zeros], axis=0)]
        else:
            tiles += [jnp.concatenate([zeros, tp[:HEAD_DIM]], axis=0), jnp.where(top, 0.0, tp)]
    return jnp.concatenate(tiles, axis=1)


def _pair_block(xt, p, g):
    r0 = HEAD_DIM * g
    even = xt[r0:r0 + HEAD_DIM, (2 * p) * BLOCK:(2 * p + 1) * BLOCK]
    odd = xt[r0:r0 + HEAD_DIM, (2 * p + 1) * BLOCK:(2 * p + 2) * BLOCK]
    return jnp.concatenate([even, odd], axis=0).T


def _softmax_t(scores_t, bias, sink):
    st = scores_t * ATTN_SCALE + bias
    m = jnp.maximum(jnp.max(st, axis=0, keepdims=True), sink)
    e = jnp.exp(st - m)
    es = jnp.exp(sink - m)
    inv = 1.0 / (jnp.sum(e, axis=0, keepdims=True) + es)
    return e * inv, es * inv


def _dot(a, b):
    return jnp.dot(a, b, preferred_element_type=F32)


def _dot_nt(a, b):
    return lax.dot_general(a, b, (((1,), (1,)), ((), ())), preferred_element_type=F32)


def _layer_norm_fwd(v):
    mu = jnp.mean(v, axis=-1, keepdims=True)
    xc = v - mu
    rstd = lax.rsqrt(jnp.mean(xc * xc, axis=-1, keepdims=True) + EPS)
    return xc * rstd, rstd


def _tril(transposed=False):
    t = lax.broadcasted_iota(jnp.int32, (BLOCK, BLOCK), 0)
    s = lax.broadcasted_iota(jnp.int32, (BLOCK, BLOCK), 1)
    return s >= t if transposed else t >= s


def _const_spec(shape):
    return pl.BlockSpec(shape, lambda i: (0,) * len(shape))


def _kv_prev_spec(index):
    return pl.BlockSpec((BLOCK, 2 * D_KV), lambda i: (jnp.maximum(index(i) - 1, 0), SEG_KV // (2 * D_KV)))


def _keys_values(z_ref, kvp_ref):
    kvp, kvc = kvp_ref[...], z_ref[:, SEG_KV:SEG_KV + 2 * D_KV]
    kk = jnp.concatenate([kvp[:, :D_KV], kvc[:, :D_KV]], axis=0)
    vv = jnp.concatenate([kvp[:, D_KV:], kvc[:, D_KV:]], axis=0)
    return kk, vv


def _pair_cols(g, p, base=0):
    return slice(base + (4 * g + p) * 128, base + (4 * g + p + 1) * 128)


def _mixer_fwd(z, sink_rows, ln_g, ln_b, sgu_w, sgu_bt):
    t = z.shape[0]

    def body(z_ref, kvp_ref, sink_ref, lng_ref, lnb_ref, w_ref, bt_ref, a_ref):
        bias = _window_bias(pl.program_id(0))
        kk, vv = _keys_values(z_ref, kvp_ref)
        kk_b, vvt_b = kk.astype(BF16), vv.T.astype(BF16)
        for g in range(2):
            qt = _heads_t([z_ref[:, _pair_cols(g, p, SEG_Q)] for p in range(4)], g).astype(BF16)
            prob, _ = _softmax_t(_dot(kk_b, qt), bias, sink_ref[g])
            ot = _dot(vvt_b, prob.astype(BF16))
            for p in range(4):
                gate = z_ref[:, _pair_cols(g, p, SEG_GA)]
                a_ref[:, _pair_cols(g, p)] = (_pair_block(ot, p, g) * (gate * _sigmoid(gate))).astype(BF16)

        vhat, _ = _layer_norm_fwd(z_ref[:, SEG_VS:SEG_VS + D_SGU])
        vn = vhat * lng_ref[...] + lnb_ref[...]
        tril = _tril()
        for g in range(SGU_GROUPS):
            cols = slice(g * 128, (g + 1) * 128)
            wm = jnp.where(tril, w_ref[g], 0.0).astype(BF16)
            mixed = _dot(wm, vn[:, cols].astype(BF16)) + bt_ref[:, g:g + 1]
            gate = z_ref[:, SEG_GS + g * 128:SEG_GS + (g + 1) * 128]
            a_ref[:, D_ATTN + g * 128:D_ATTN + (g + 1) * 128] = (
                (z_ref[:, SEG_U + g * 128:SEG_U + (g + 1) * 128] * mixed) * (gate * _sigmoid(gate))).astype(BF16)

    return pl.pallas_call(
        body, name="mixer_fwd", grid=(t // BLOCK,),
        in_specs=[pl.BlockSpec((BLOCK, D_IN), lambda i: (i, 0)), _kv_prev_spec(lambda i: i),
                  _const_spec((2, 1, 8 * BLOCK)), _const_spec((1, D_SGU)), _const_spec((1, D_SGU)),
                  _const_spec((SGU_GROUPS, BLOCK, BLOCK)), _const_spec((BLOCK, SGU_GROUPS))],
        out_specs=pl.BlockSpec((BLOCK, D_MODEL), lambda i: (i, 0)),
        out_shape=jax.ShapeDtypeStruct((t, D_MODEL), BF16),
        compiler_params=_params(dimension_semantics=("arbitrary",)),
    )(z, z, sink_rows, ln_g, ln_b, sgu_w, sgu_bt)


def _mixer_bwd(z, da, sink_rows, ln_g, ln_b, sgu_w, sgu_wt, sgu_bt):
    t = z.shape[0]
    nb = t // BLOCK

    def body(z_ref, kvp_ref, da_ref, sink_ref, lng_ref, lnb_ref, w_ref, wt_ref, bt_ref,
             dz_ref, dsink_ref, dw_ref, db_ref, dlng_ref, dlnb_ref, carry_ref, dsink_acc, dbt_acc):
        step = pl.program_id(0)

        @pl.when(step == 0)
        def _():
            carry_ref[...] = jnp.zeros_like(carry_ref)
            dsink_acc[...] = jnp.zeros_like(dsink_acc)
            dbt_acc[...] = jnp.zeros_like(dbt_acc)
            dw_ref[...] = jnp.zeros_like(dw_ref)
            dlng_ref[...] = jnp.zeros_like(dlng_ref)
            dlnb_ref[...] = jnp.zeros_like(dlnb_ref)

        bias = _window_bias(nb - 1 - step)
        kk, vv = _keys_values(z_ref, kvp_ref)
        kk_b, vv_b = kk.astype(BF16), vv.astype(BF16)
        kkt_b, vvt_b = kk.T.astype(BF16), vv.T.astype(BF16)
        dkk = jnp.zeros((2 * BLOCK, D_KV), F32)
        dvv = jnp.zeros((2 * BLOCK, D_KV), F32)
        for g in range(2):
            qt = _heads_t([z_ref[:, _pair_cols(g, p, SEG_Q)] for p in range(4)], g).astype(BF16)
            prob, sink_prob = _softmax_t(_dot(kk_b, qt), bias, sink_ref[g])
            prob_b = prob.astype(BF16)
            ot = _dot(vvt_b, prob_b)
            gates = [z_ref[:, _pair_cols(g, p, SEG_GA)] for p in range(4)]
            sig = [_sigmoid(gt) for gt in gates]
            d_attn = [da_ref[:, _pair_cols(g, p)] for p in range(4)]
            d_ot = _heads_t([d_attn[p] * (gates[p] * sig[p]) for p in range(4)], g).astype(BF16)
            d_prob = _dot(vv_b, d_ot)
            delta = jnp.sum(prob * d_prob, axis=0, keepdims=True)
            d_scores = (prob * (d_prob - delta) * ATTN_SCALE).astype(BF16)
            dsink_acc[g] -= sink_prob * delta
            d_qt = _dot(kkt_b, d_scores)
            dkk = dkk + _dot_nt(d_scores, qt)
            dvv = dvv + _dot_nt(prob_b, d_ot)
            for p in range(4):
                dz_ref[:, _pair_cols(g, p, SEG_Q)] = _pair_block(d_qt, p, g).astype(BF16)
                d_silu = sig[p] * (1.0 + gates[p] * (1.0 - sig[p]))
                dz_ref[:, _pair_cols(g, p, SEG_GA)] = (d_attn[p] * _pair_block(ot, p, g) * d_silu).astype(BF16)
        d_kv = jnp.concatenate([dkk, dvv], axis=1)
        dz_ref[:, SEG_KV:SEG_KV + 2 * D_KV] = (d_kv[BLOCK:] + carry_ref[...]).astype(BF16)
        carry_ref[...] = d_kv[:BLOCK]

        vhat, rstd = _layer_norm_fwd(z_ref[:, SEG_VS:SEG_VS + D_SGU])
        lng = lng_ref[...]
        vn = vhat * lng + lnb_ref[...]
        tril, triu = _tril(), _tril(transposed=True)
        lane = lax.broadcasted_iota(jnp.int32, (BLOCK, 128), 1)
        d_bt = jnp.zeros((BLOCK, 128), F32)
        d_vn = []
        for g in range(SGU_GROUPS):
            cols = slice(g * 128, (g + 1) * 128)
            wm = jnp.where(tril, w_ref[g], 0.0).astype(BF16)
            wmt = jnp.where(triu, wt_ref[g], 0.0).astype(BF16)
            vn_g = vn[:, cols].astype(BF16)
            mixed = _dot(wm, vn_g) + bt_ref[:, g:g + 1]
            gate = z_ref[:, SEG_GS + g * 128:SEG_GS + (g + 1) * 128]
            u = z_ref[:, SEG_U + g * 128:SEG_U + (g + 1) * 128]
            d_out = da_ref[:, D_ATTN + g * 128:D_ATTN + (g + 1) * 128]
            sg = _sigmoid(gate)
            d_um = d_out * (gate * sg)
            dz_ref[:, SEG_U + g * 128:SEG_U + (g + 1) * 128] = (d_um * mixed).astype(BF16)
            dz_ref[:, SEG_GS + g * 128:SEG_GS + (g + 1) * 128] = (
                d_out * (u * mixed) * (sg * (1.0 + gate * (1.0 - sg)))).astype(BF16)
            d_mixed = d_um * u
            d_mixed_b = d_mixed.astype(BF16)
            dw_ref[g] += jnp.where(tril, _dot_nt(d_mixed_b, vn_g), 0.0)
            d_bt = d_bt + jnp.where(lane == g, jnp.sum(d_mixed, axis=-1, keepdims=True), 0.0)
            d_vn.append(_dot(wmt, d_mixed_b))
        dbt_acc[...] += d_bt
        d_vn = jnp.concatenate(d_vn, axis=1)
        dlng_ref[...] += jnp.sum(d_vn * vhat, axis=0, keepdims=True)
        dlnb_ref[...] += jnp.sum(d_vn, axis=0, keepdims=True)
        d_vhat = d_vn * lng
        d_v = rstd * (d_vhat - jnp.mean(d_vhat, axis=-1, keepdims=True)
                      - vhat * jnp.mean(d_vhat * vhat, axis=-1, keepdims=True))
        dz_ref[:, SEG_VS:SEG_VS + D_SGU] = d_v.astype(BF16)

        @pl.when(step == nb - 1)
        def _():
            db_ref[...] = dbt_acc[...].T[:SGU_GROUPS]
            lane_row = lax.broadcasted_iota(jnp.int32, (1, 128), 1)
            d_sink = jnp.zeros((1, 128), F32)
            for g in range(2):
                acc = dsink_acc[g]
                for j in range(8):
                    head_sum = jnp.sum(acc[:, j * BLOCK:(j + 1) * BLOCK], axis=-1, keepdims=True)
                    d_sink = d_sink + jnp.where(lane_row == 8 * g + j, head_sum, 0.0)
            dsink_ref[...] = d_sink

    rev = lambda i: nb - 1 - i
    return pl.pallas_call(
        body, name="mixer_bwd", grid=(nb,),
        in_specs=[pl.BlockSpec((BLOCK, D_IN), lambda i: (rev(i), 0)), _kv_prev_spec(rev),
                  pl.BlockSpec((BLOCK, D_MODEL), lambda i: (rev(i), 0)),
                  _const_spec((2, 1, 8 * BLOCK)), _const_spec((1, D_SGU)), _const_spec((1, D_SGU)),
                  _const_spec((SGU_GROUPS, BLOCK, BLOCK)), _const_spec((SGU_GROUPS, BLOCK, BLOCK)),
                  _const_spec((BLOCK, SGU_GROUPS))],
        out_specs=(pl.BlockSpec((BLOCK, D_IN), lambda i: (rev(i), 0)), _const_spec((1, 128)),
                   _const_spec((SGU_GROUPS, BLOCK, BLOCK)), _const_spec((SGU_GROUPS, BLOCK)),
                   _const_spec((1, D_SGU)), _const_spec((1, D_SGU))),
        out_shape=(jax.ShapeDtypeStruct((t, D_IN), BF16), jax.ShapeDtypeStruct((1, 128), F32),
                   jax.ShapeDtypeStruct((SGU_GROUPS, BLOCK, BLOCK), F32), jax.ShapeDtypeStruct((SGU_GROUPS, BLOCK), F32),
                   jax.ShapeDtypeStruct((1, D_SGU), F32), jax.ShapeDtypeStruct((1, D_SGU), F32)),
        scratch_shapes=[pltpu.VMEM((BLOCK, 2 * D_KV), F32), pltpu.VMEM((2, 1, 8 * BLOCK), F32),
                        pltpu.VMEM((BLOCK, 128), F32)],
        compiler_params=_params(dimension_semantics=("arbitrary",)),
    )(z, z, da, sink_rows, ln_g, ln_b, sgu_w, sgu_wt, sgu_bt)


def _head(y, x, target, gate, final_g, tm=256):
    t, d = x.shape

    def body(y_ref, x_ref, tg_ref, gate_ref, fg_ref, dx2_ref, dy_ref, loss_ref, dfg_ref, dgate_ref):
        @pl.when(pl.program_id(0) == 0)
        def _():
            loss_ref[...] = jnp.zeros_like(loss_ref)
            dfg_ref[...] = jnp.zeros_like(dfg_ref)
            dgate_ref[...] = jnp.zeros_like(dgate_ref)

        yv, gate, fg = y_ref[...], gate_ref[...], fg_ref[...]
        x2 = x_ref[...] + gate * yv
        r2 = lax.rsqrt(jnp.mean(x2 * x2, axis=-1, keepdims=True) + EPS)
        nrm = x2 * r2
        err = nrm * fg - tg_ref[...]
        loss_ref[...] += 0.5 * jnp.sum(jnp.mean(err * err, axis=-1, keepdims=True), axis=0, keepdims=True)
        d_out = err * (1.0 / d)
        dfg_ref[...] += jnp.sum(d_out * nrm, axis=0, keepdims=True)
        d_nrm = d_out * fg
        dx2 = r2 * (d_nrm - nrm * jnp.mean(d_nrm * nrm, axis=-1, keepdims=True))
        dx2_ref[...] = dx2
        dgate_ref[...] += jnp.sum(dx2 * yv, axis=0, keepdims=True)
        dy_ref[...] = (dx2 * gate).astype(BF16)

    blk = pl.BlockSpec((tm, d), lambda i: (i, 0))
    row = _const_spec((1, d))
    return pl.pallas_call(
        body, name="head", grid=(t // tm,),
        in_specs=[blk, blk, blk, row, row],
        out_specs=(blk, blk, _const_spec((1, 128)), row, row),
        out_shape=(jax.ShapeDtypeStruct((t, d), F32), jax.ShapeDtypeStruct((t, d), BF16),
                   jax.ShapeDtypeStruct((1, 128), F32), jax.ShapeDtypeStruct((1, d), F32),
                   jax.ShapeDtypeStruct((1, d), F32)),
        compiler_params=_params(dimension_semantics=("arbitrary",)),
    )(y, x, target, gate, final_g)


def _modulated_norm_bwd(dh, x, dx2, norm_g, scale, tm=256):
    t, d = x.shape

    def body(dh_ref, x_ref, dx2_ref, g_ref, sc_ref, gx_ref, dshift_ref, dscale_ref, dg_ref):
        @pl.when(pl.program_id(0) == 0)
        def _():
            dshift_ref[...] = jnp.zeros_like(dshift_ref)
            dscale_ref[...] = jnp.zeros_like(dscale_ref)
            dg_ref[...] = jnp.zeros_like(dg_ref)

        dh, xv, g = dh_ref[...], x_ref[...], g_ref[...]
        one_plus = 1.0 + sc_ref[...]
        r = lax.rsqrt(jnp.mean(xv * xv, axis=-1, keepdims=True) + EPS)
        xn = xv * r
        dshift_ref[...] += jnp.sum(dh, axis=0, keepdims=True)
        dscale_ref[...] += jnp.sum(dh * (xn * g), axis=0, keepdims=True)
        d_y = dh * one_plus
        dg_ref[...] += jnp.sum(d_y * xn, axis=0, keepdims=True)
        d_xn = d_y * g
        gx_ref[...] = dx2_ref[...] + r * (d_xn - xn * jnp.mean(d_xn * xn, axis=-1, keepdims=True))

    blk = pl.BlockSpec((tm, d), lambda i: (i, 0))
    row = _const_spec((1, d))
    return pl.pallas_call(
        body, name="modulated_norm_bwd", grid=(t // tm,),
        in_specs=[blk, blk, blk, row, row], out_specs=(blk, row, row, row),
        out_shape=(jax.ShapeDtypeStruct((t, d), F32),) + (jax.ShapeDtypeStruct((1, d), F32),) * 3,
        compiler_params=_params(dimension_semantics=("arbitrary",)),
    )(dh, x, dx2, norm_g, scale)


def _pair_sum(core, blocks, got, name, tr):
    _, _, r, c = blocks.shape

    def body(core_ref, a_ref, b_ref, o_ref):
        o_ref[...] = (a_ref[...].astype(F32) + b_ref[...].astype(F32)).astype(BF16)

    return pl.pallas_call(
        body, name=name,
        grid_spec=pltpu.PrefetchScalarGridSpec(
            num_scalar_prefetch=1, grid=(4, r // tr),
            in_specs=[pl.BlockSpec((None, None, tr, c), lambda m, i, core_ref: (m, core_ref[0], i, 0)),
                      pl.BlockSpec((None, tr, c), lambda m, i, core_ref: (m, i, 0))],
            out_specs=pl.BlockSpec((None, tr, c), lambda m, i, core_ref: (m, i, 0))),
        out_shape=jax.ShapeDtypeStruct(got.shape, BF16),
        compiler_params=_params(dimension_semantics=("arbitrary", "arbitrary")),
    )(core, blocks, got)


def _adamw(w, g, m, v):
    m = ADAM_B1 * m + (1.0 - ADAM_B1) * g
    v = ADAM_B2 * v + (1.0 - ADAM_B2) * (g * g)
    m_hat = m / (1.0 - ADAM_B1 ** ADAM_STEP)
    v_hat = v / (1.0 - ADAM_B2 ** ADAM_STEP)
    delta = -ADAM_LR * (m_hat / (jnp.sqrt(v_hat) + ADAM_EPS) + ADAM_WD * w)
    return delta, m, v


def _adam_from_chips(chip, pair, land, w, m, v, name, tc):
    _, r, c = pair.shape

    def body(chip_ref, own_ref, land_ref, w_ref, m_ref, v_ref, g_ref, d_ref, nm_ref, nv_ref):
        g = own_ref[...].astype(F32)
        for k in range(3):
            g = g + land_ref[k].astype(F32)
        g_ref[...] = g
        d_ref[...], nm_ref[...], nv_ref[...] = _adamw(w_ref[...], g, m_ref[...], v_ref[...])

    blk = pl.BlockSpec((r, tc), lambda i, chip_ref: (0, i))
    return pl.pallas_call(
        body, name=name,
        grid_spec=pltpu.PrefetchScalarGridSpec(
            num_scalar_prefetch=1, grid=(c // tc,),
            in_specs=[pl.BlockSpec((None, r, tc), lambda i, chip_ref: (chip_ref[0], 0, i)),
                      pl.BlockSpec((3, r, tc), lambda i, chip_ref: (0, 0, i)), blk, blk, blk],
            out_specs=(blk,) * 4),
        out_shape=(jax.ShapeDtypeStruct((r, c), F32),) * 4,
        compiler_params=_params(dimension_semantics=("arbitrary",)),
    )(chip, pair, land, w, m, v)


def _adam_w_ada(act_t, dmod_mine, w, m, v, tr=256):
    r, c = w.shape

    def body(a_ref, dm_ref, w_ref, m_ref, v_ref, g_ref, d_ref, nm_ref, nv_ref):
        g = _dot(a_ref[...].astype(BF16), dm_ref[...].astype(BF16))
        g_ref[...] = g
        d_ref[...], nm_ref[...], nv_ref[...] = _adamw(w_ref[...], g, m_ref[...], v_ref[...])

    blk = pl.BlockSpec((tr, c), lambda i: (i, 0))
    return pl.pallas_call(
        body, name="adam_w_ada", grid=(r // tr,),
        in_specs=[pl.BlockSpec((tr, N_DEV), lambda i: (i, 0)), _const_spec((N_DEV, c)), blk, blk, blk],
        out_specs=(blk,) * 4, out_shape=(jax.ShapeDtypeStruct((r, c), F32),) * 4,
        compiler_params=_params(dimension_semantics=("arbitrary",)),
    )(act_t, dmod_mine, w, m, v)


def _pack_small(d_shift, d_scale, d_gate, d_norm_g, d_final_g, d_ln_g, d_ln_b, loss, d_sinks, d_sgu_b):
    def body(shift_ref, scale_ref, gate_ref, ng_ref, fg_ref, lng_ref, lnb_ref, loss_ref, sink_ref, b_ref, o_ref):
        o_ref[...] = jnp.zeros_like(o_ref)
        o_ref[ROW_SHIFT:ROW_SHIFT + 1, :] = shift_ref[...]
        o_ref[ROW_SCALE:ROW_SCALE + 1, :] = scale_ref[...]
        o_ref[ROW_GATE:ROW_GATE + 1, :] = gate_ref[...]
        o_ref[ROW_NORM_G:ROW_NORM_G + 1, :] = ng_ref[...]
        o_ref[ROW_FINAL_G:ROW_FINAL_G + 1, :] = fg_ref[...]
        o_ref[ROW_LN:ROW_LN + 1, 0:D_SGU] = lng_ref[...]
        o_ref[ROW_LN:ROW_LN + 1, D_SGU:2 * D_SGU] = lnb_ref[...]
        o_ref[ROW_MISC:ROW_MISC + 1, 0:128] = loss_ref[...]
        o_ref[ROW_MISC:ROW_MISC + 1, 128:256] = sink_ref[...]
        o_ref[ROW_SGU_B:ROW_SGU_B + SGU_GROUPS, 0:BLOCK] = b_ref[...]

    return pl.pallas_call(
        body, name="pack_small", out_shape=jax.ShapeDtypeStruct((SMALL_ROWS, D_MODEL), F32),
        compiler_params=_params(),
    )(d_shift, d_scale, d_gate, d_norm_g, d_final_g, d_ln_g, d_ln_b, loss, d_sinks, d_sgu_b)


_SMALL_NAMES = ("norm_g", "b_ada", "attn_sinks", "sgu_ln_g", "sgu_ln_b", "sgu_w", "sgu_b", "final_g")


def _adam_small(partials, d_sgu_w_all, weights, moments_m, moments_v):
    names = _SMALL_NAMES
    k = len(names)

    def body(*refs):
        p_ref, sw_ref = refs[0], refs[1]
        w_refs, m_refs, v_refs = refs[2:2 + k], refs[2 + k:2 + 2 * k], refs[2 + 2 * k:2 + 3 * k]
        loss_ref, dmod_ref = refs[2 + 3 * k], refs[3 + 3 * k]
        out_refs = refs[4 + 3 * k:4 + 7 * k]
        sum_ref = refs[4 + 7 * k]
        total = p_ref[0]
        for j in range(1, N_DEV):
            total = total + p_ref[j]
        sum_ref[...] = total
        for j in range(N_DEV):
            for part, row in enumerate((ROW_SHIFT, ROW_SCALE, ROW_GATE)):
                dmod_ref[j:j + 1, part * D_MODEL:(part + 1) * D_MODEL] = p_ref[j, row:row + 1, :]
        loss_ref[...] = sum_ref[ROW_MISC:ROW_MISC + 1, 0:1]
        d_sgu_w = sw_ref[0]
        for j in range(1, N_DEV):
            d_sgu_w = d_sgu_w + sw_ref[j]
        grads = {
            "norm_g": sum_ref[ROW_NORM_G:ROW_NORM_G + 1, :],
            "b_ada": jnp.concatenate([sum_ref[r:r + 1, :] for r in (ROW_SHIFT, ROW_SCALE, ROW_GATE)], axis=1),
            "attn_sinks": sum_ref[ROW_MISC:ROW_MISC + 1, 128:128 + N_Q_HEADS],
            "sgu_ln_g": sum_ref[ROW_LN:ROW_LN + 1, 0:D_SGU],
            "sgu_ln_b": sum_ref[ROW_LN:ROW_LN + 1, D_SGU:2 * D_SGU],
            "sgu_w": d_sgu_w[None],
            "sgu_b": sum_ref[ROW_SGU_B:ROW_SGU_B + SGU_GROUPS, 0:BLOCK][None],
            "final_g": sum_ref[ROW_FINAL_G:ROW_FINAL_G + 1, :],
        }
        for i, name in enumerate(names):
            g = grads[name]
            delta, m, v = _adamw(w_refs[i][...], g, m_refs[i][...], v_refs[i][...])
            out_refs[4 * i][...] = g
            out_refs[4 * i + 1][...] = delta
            out_refs[4 * i + 2][...] = m
            out_refs[4 * i + 3][...] = v

    shapes = [jax.ShapeDtypeStruct((1, 1), F32), jax.ShapeDtypeStruct((N_DEV, 3 * D_MODEL), F32)]
    for name in names:
        shapes += [jax.ShapeDtypeStruct(weights[name].shape, F32)] * 4
    outs = pl.pallas_call(
        body, name="adam_small", out_shape=tuple(shapes),
        scratch_shapes=[pltpu.VMEM((SMALL_ROWS, D_MODEL), F32)],
        compiler_params=_params(),
    )(partials, d_sgu_w_all, *[weights[n] for n in names], *[moments_m[n] for n in names],
      *[moments_v[n] for n in names])
    return outs[0], outs[1], {name: outs[2 + 4 * i:6 + 4 * i] for i, name in enumerate(names)}


def kernel(x, c, norm_g, w_ada, b_ada, w_in, attn_sinks, sgu_ln_g, sgu_ln_b, sgu_w, sgu_b, w_out, final_g, loss_target, m_norm_g, m_w_ada, m_b_ada, m_w_in, m_attn_sinks, m_sgu_ln_g, m_sgu_ln_b, m_sgu_w, m_sgu_b, m_w_out, m_final_g, v_norm_g, v_w_ada, v_b_ada, v_w_in, v_attn_sinks, v_sgu_ln_g, v_sgu_ln_b, v_sgu_w, v_sgu_b, v_w_out, v_final_g):
    xi, yi, ci = _place()
    me = 4 * xi + 2 * yi + ci
    x2d, target = x[0], loss_target[0]
    t = x2d.shape[0]

    core = ci.astype(jnp.int32).reshape(1)
    chip = (2 * xi + yi).astype(jnp.int32).reshape(1)

    near = _own_block_copies(_near_targets)
    w_in_flight = _start_copies([_with_own_slot(w_in[0].T.astype(BF16), me)], near, 3, core, "gather_w_in_start")

    c_all = _all_gather([c.reshape(8, 256) + w_in_flight[3][0, 0]], "gather_c", True)[0].reshape(N_DEV, D_MODEL)
    b_mine = lax.dynamic_slice(b_ada, (0, me * W_ADA_SHARD), (1, W_ADA_SHARD))
    c_act, mod_part = _modulation(c_all, w_ada[0], b_mine)
    mod_all = _all_gather([mod_part], "gather_mod", True)[0]
    mod = lax.dynamic_index_in_dim(mod_all, me, axis=1, keepdims=False).reshape(1, 3 * D_MODEL)
    shift, scale, gate = mod[:, :D_MODEL], mod[:, D_MODEL:2 * D_MODEL], mod[:, 2 * D_MODEL:]
    h = _modulated_norm(x2d, norm_g, scale, shift)

    w_in_near = _wait_copies(w_in_flight, near, h, "gather_w_in_wait")
    w_in_flight = _start_copies(w_in_near, _second_stage_copies, 3, core, "gather_w_in_second_start")
    w_in_most = _wait_copies(w_in_flight, lambda *a: _second_stage_copies(*a)[:2], h, "gather_w_in_forward_wait")
    tile_order = jnp.asarray(_Z_TILE_ORDER, jnp.int32)[chip[0]]
    z_early = _z_proj(h, w_in_most[0].reshape(D_IN, D_MODEL), tile_order, 0, _Z_EARLY_TILES, None, "z_proj_early")
    w_in_diag = _wait_copies((w_in_flight[0], w_in_flight[1], w_in_most, None),
                             lambda *a: _second_stage_copies(*a)[2:], z_early, "gather_w_in_diagonal_wait")
    w_out_flight = _start_copies([_with_own_slot(w_out[0].astype(BF16), me)], _own_block_copies(_my_core_and_sibling),
                                 4, w_in_diag[0], "gather_w_out_start")
    w_in_flight = _start_copies(w_in_diag, _diagonal_forward_copies, 1, w_out_flight[3], "gather_w_in_last_start")
    w_in_all = _wait_copies(w_in_flight, _diagonal_forward_copies, z_early, "gather_w_in_last_wait")[0]
    w_in_t = w_in_all.reshape(D_IN, D_MODEL)
    z = _z_proj(h, w_in_t, tile_order, _Z_EARLY_TILES, 7 - _Z_EARLY_TILES, z_early, "z_proj_late")
    w_out_half = _wait_copies(w_out_flight, _own_block_copies(_my_core_and_sibling), z, "gather_w_out_wait")
    w_out_flight = _start_copies(w_out_half, _forward_copies, 3, z, "gather_w_out_forward_start")
    sink_rows = jnp.repeat(attn_sinks.reshape(N_Q_HEADS), BLOCK).reshape(2, 1, 8 * BLOCK)
    sgu_bt = sgu_b[0].T
    a = _mixer_fwd(z, sink_rows + w_out_flight[3][0, 0], sgu_ln_g, sgu_ln_b, sgu_w[0], sgu_bt)
    w_out_all = _wait_copies(w_out_flight, _forward_copies, a, "gather_w_out_forward_wait")[0]
    w_out_full = w_out_all.reshape(D_MODEL, D_MODEL)
    y = _matmul(a, w_out_full, "nn", F32, min(t, 1024), 1024, "out_proj")
    final_g_row = final_g.reshape(1, D_MODEL)
    dx2, dy, loss_part, d_final_g, d_gate = _head(y, x2d, target, gate, final_g_row)

    da = _matmul(dy, w_out_full, "nt", F32, min(t, 1024), 1024, "out_proj_bwd")
    dw_out = _matmul(a, dy, "tn", BF16, 1024, 1024, "w_out_grad").reshape(4, 2, W_OUT_SHARD, D_MODEL)
    got_out = _pair_exchange([dw_out], "w_out_grad_pair_exchange")[0]
    pair_out = _pair_sum(core, dw_out, got_out, "w_out_grad_pair_sum", W_OUT_SHARD)
    out_flight = _start_copies([pair_out, lax.empty((3, W_OUT_SHARD, D_MODEL), BF16)], _chip_copies, 3, core,
                               "w_out_grad_chip_start")
    dz, d_sinks, d_sgu_w, d_sgu_b, d_ln_g, d_ln_b = _mixer_bwd(
        z, da, sink_rows + out_flight[3][0, 0], sgu_ln_g, sgu_ln_b, sgu_w[0], jnp.swapaxes(sgu_w[0], 1, 2), sgu_bt)
    sgu_w_flight = _start_copies([_with_own_slot(d_sgu_w, me)], _own_block_copies(_all_others), N_DEV - 1, core,
                                 "sgu_w_grad_gather_start")
    dw_in_t = _matmul(dz, h, "tn", BF16, 768, D_MODEL, "w_in_grad", dep=sgu_w_flight[3])
    dw_in_t = dw_in_t.reshape(4, 2, W_IN_SHARD, D_MODEL)
    got_in = _pair_exchange([dw_in_t], "w_in_grad_pair_exchange")[0]
    pair_in = _pair_sum(core, dw_in_t, got_in, "w_in_grad_pair_sum", W_IN_SHARD // 2)
    in_flight = _start_copies([pair_in, lax.empty((3, W_IN_SHARD, D_MODEL), BF16)], _chip_copies, 3, core,
                              "w_in_grad_chip_start")
    dh = _matmul(dz, w_in_t, "nn", F32, min(t, 1024), 512, "z_proj_bwd", dep=in_flight[3])
    grad_x, d_shift, d_scale, d_norm_g = _modulated_norm_bwd(dh, x2d, dx2, norm_g, scale)

    partial = _pack_small(d_shift, d_scale, d_gate, d_norm_g, d_final_g, d_ln_g, d_ln_b, loss_part, d_sinks, d_sgu_b)
    partial_all = _all_gather([partial], "gather_small", True)[0]
    d_sgu_w_all = _wait_copies(sgu_w_flight, _own_block_copies(_all_others), partial_all, "sgu_w_grad_gather_wait")[0]
    weights = {"norm_g": norm_g, "b_ada": b_ada, "attn_sinks": attn_sinks, "sgu_ln_g": sgu_ln_g,
               "sgu_ln_b": sgu_ln_b, "sgu_w": sgu_w, "sgu_b": sgu_b, "final_g": final_g_row}
    moments_m = {"norm_g": m_norm_g, "b_ada": m_b_ada, "attn_sinks": m_attn_sinks, "sgu_ln_g": m_sgu_ln_g,
                 "sgu_ln_b": m_sgu_ln_b, "sgu_w": m_sgu_w, "sgu_b": m_sgu_b,
                 "final_g": m_final_g.reshape(1, D_MODEL)}
    moments_v = {"norm_g": v_norm_g, "b_ada": v_b_ada, "attn_sinks": v_attn_sinks, "sgu_ln_g": v_sgu_ln_g,
                 "sgu_ln_b": v_sgu_ln_b, "sgu_w": v_sgu_w, "sgu_b": v_sgu_b,
                 "final_g": v_final_g.reshape(1, D_MODEL)}
    loss, dmod_all, small = _adam_small(partial_all, d_sgu_w_all, weights, moments_m, moments_v)
    small["final_g"] = tuple(o.reshape(D_MODEL) for o in small["final_g"])

    dmod_mine = lax.dynamic_slice(dmod_all, (0, me * W_ADA_SHARD), (N_DEV, W_ADA_SHARD))
    big = {"w_ada": _adam_w_ada(c_act.T, dmod_mine, w_ada[0], m_w_ada[0], v_w_ada[0])}
    pair_out, land_out = _wait_copies(out_flight, _chip_copies, big["w_ada"][0], "w_out_grad_chip_wait")
    big["w_out"] = _adam_from_chips(chip, pair_out, land_out, w_out[0], m_w_out[0], v_w_out[0], "adam_w_out", 1024)
    pair_in, land_in = _wait_copies(in_flight, _chip_copies, big["w_out"][0], "w_in_grad_chip_wait")
    big["w_in"] = tuple(o.T for o in _adam_from_chips(
        chip, pair_in, land_in, w_in[0].T, m_w_in[0].T, v_w_in[0].T, "adam_w_in", 256))
    order = ["norm_g", "w_ada", "b_ada", "w_in", "attn_sinks", "sgu_ln_g", "sgu_ln_b", "sgu_w", "sgu_b", "w_out",
             "final_g"]
    outs = [loss.reshape(()), grad_x[None]]
    for k in range(4):
        for name in order:
            outs.append(big[name][k][None] if name in big else small[name][k])
    return tuple(outs)
```

```python
import jax
import jax.numpy as jnp
from jax import lax
from jax.experimental import pallas as pl
from jax.experimental.pallas import tpu as pltpu

F32 = jnp.float32
BF16 = jnp.bfloat16
MESH = pl.DeviceIdType.MESH

N_DEV = 8
D_MODEL = 2048
HEAD_DIM = 64
D_ATTN = 1024
N_Q_HEADS = 16
D_KV = 128
BLOCK = 128
D_SGU = 1024
SGU_GROUPS = 8
D_IN = 5376
W_IN_SHARD = D_IN // N_DEV
W_OUT_SHARD = D_MODEL // N_DEV
W_ADA_SHARD = 3 * D_MODEL // N_DEV
EPS = 1e-6
ATTN_SCALE = 0.125

ADAM_LR = 0.001
ADAM_B1 = 0.9
ADAM_B2 = 0.999
ADAM_EPS = 1e-08
ADAM_WD = 0.01
ADAM_STEP = 10

SEG_Q, SEG_KV, SEG_GA, SEG_U, SEG_VS, SEG_GS = 0, 1024, 1280, 2304, 3328, 4352

VMEM_LIMIT = 56 * 1024 * 1024

ROW_SHIFT, ROW_SCALE, ROW_GATE, ROW_NORM_G, ROW_FINAL_G, ROW_LN, ROW_MISC, ROW_SGU_B = 0, 1, 2, 3, 4, 5, 6, 8
SMALL_ROWS = 16


def _params(**kw):
    return pltpu.CompilerParams(vmem_limit_bytes=VMEM_LIMIT, **kw)


def _sigmoid(x):
    return 0.5 * (jnp.tanh(0.5 * x) + 1.0)


def _place():
    return lax.axis_index("x"), lax.axis_index("y"), lax.axis_index("c")


def _all_gather(shards, name, in_vmem):
    n = len(shards)

    def body(*refs):
        ins, outs = refs[:n], refs[n:2 * n]
        send_sems, recv_sems, local_sems = refs[2 * n:]
        x, y, c = _place()
        me, sibling = (x, y, c), (x, y, 1 - c)
        chips = [(1 - x, y), (x, 1 - y), (1 - x, 1 - y)]
        first, passed, mine = [], [], []
        for a in range(n):
            out_ref = outs[a]

            def slot(px, py, pc, out_ref=out_ref):
                return out_ref.at[4 * px + 2 * py + pc]

            def copy(k, block, to, src=None, a=a, slot=slot):
                return pltpu.make_async_remote_copy(
                    src_ref=slot(*block) if src is None else src, dst_ref=slot(*block),
                    send_sem=send_sems.at[a, k], recv_sem=recv_sems.at[a, k],
                    device_id=to, device_id_type=MESH)

            own = pltpu.make_async_copy(ins[a], slot(*me), local_sems.at[a])
            own.start()
            mine.append(own)
            mine_out = [copy(0, me, sibling, src=ins[a])]
            mine_out += [copy(1 + j, me, (*chip, c), src=ins[a]) for j, chip in enumerate(chips)]
            for cp in mine_out:
                cp.start()
            first += mine_out
            passed.append([copy(4 + j, (*chip, c), sibling) for j, chip in enumerate(chips)])
        for j, chip in enumerate(chips):
            for a in range(n):
                out_ref = outs[a]
                blk = out_ref.at[4 * chip[0] + 2 * chip[1] + c]
                pltpu.make_async_remote_copy(
                    src_ref=blk, dst_ref=blk, send_sem=send_sems.at[a, 1 + j], recv_sem=recv_sems.at[a, 1 + j],
                    device_id=me, device_id_type=MESH).wait_recv()
                passed[a][j].start()
        for a in range(n):
            out_ref = outs[a]
            blk = out_ref.at[4 * x + 2 * y + (1 - c)]
            pltpu.make_async_remote_copy(
                src_ref=blk, dst_ref=blk, send_sem=send_sems.at[a, 0], recv_sem=recv_sems.at[a, 0],
                device_id=me, device_id_type=MESH).wait_recv()
            for j, chip in enumerate(chips):
                blk = out_ref.at[4 * chip[0] + 2 * chip[1] + (1 - c)]
                pltpu.make_async_remote_copy(
                    src_ref=blk, dst_ref=blk, send_sem=send_sems.at[a, 4 + j], recv_sem=recv_sems.at[a, 4 + j],
                    device_id=me, device_id_type=MESH).wait_recv()
        for cp in first:
            cp.wait_send()
        for a in range(n):
            for cp in passed[a]:
                cp.wait_send()
        for cp in mine:
            cp.wait()

    space = pltpu.VMEM if in_vmem else pl.ANY
    spec = pl.BlockSpec(memory_space=space)
    return pl.pallas_call(
        body, name=name,
        out_shape=tuple(jax.ShapeDtypeStruct((N_DEV,) + s.shape, s.dtype) for s in shards),
        in_specs=[spec] * n, out_specs=tuple([spec] * n),
        scratch_shapes=[pltpu.SemaphoreType.DMA((n, 7)), pltpu.SemaphoreType.DMA((n, 7)),
                        pltpu.SemaphoreType.DMA((n,))],
        compiler_params=_params(),
    )(*shards)


def _pair_exchange(sends, name):
    n = len(sends)

    def body(*refs):
        ins, outs = refs[:n], refs[n:2 * n]
        send_sems, recv_sems = refs[2 * n:]
        x, y, c = _place()
        copies = []
        for a in range(n):
            for m in range(4):
                cp = pltpu.make_async_remote_copy(
                    src_ref=ins[a].at[m, 1 - c], dst_ref=outs[a].at[m],
                    send_sem=send_sems.at[a, m], recv_sem=recv_sems.at[a, m],
                    device_id=(x, y, 1 - c), device_id_type=MESH)
                cp.start()
                copies.append(cp)
        for cp in copies:
            cp.wait()

    spec = pl.BlockSpec(memory_space=pl.ANY)
    return pl.pallas_call(
        body, name=name,
        out_shape=tuple(jax.ShapeDtypeStruct((4,) + s.shape[2:], s.dtype) for s in sends),
        in_specs=[spec] * n, out_specs=tuple([spec] * n),
        scratch_shapes=[pltpu.SemaphoreType.DMA((n, 4)), pltpu.SemaphoreType.DMA((n, 4))],
        compiler_params=_params(),
    )(*sends)


_HBM = pl.BlockSpec(memory_space=pltpu.HBM)
_SEM = pl.BlockSpec(memory_space=pltpu.SEMAPHORE)
_EFFECT = pltpu.SideEffectType.DATAFLOW_SIDE_EFFECTING


def _start_copies(bufs, copies, n_copies, after, name):
    nb = len(bufs)

    def body(*refs):
        for cp in copies(refs[:nb], refs[nb + 1], refs[nb + 2]):
            cp.start()
        refs[-1][...] = jnp.zeros_like(refs[-1])

    out = pl.pallas_call(
        body, name=name,
        out_shape=(pltpu.SemaphoreType.DMA((n_copies,)), pltpu.SemaphoreType.DMA((n_copies,)),
                   *[pltpu.HBM(b.shape, b.dtype) for b in bufs], jax.ShapeDtypeStruct((8, 128), F32)),
        in_specs=(_HBM,) * nb + (pl.BlockSpec(memory_space=pl.ANY),),
        out_specs=(_SEM, _SEM) + (_HBM,) * nb + (pl.BlockSpec(memory_space=pltpu.VMEM),),
        input_output_aliases={i: 2 + i for i in range(nb)},
        compiler_params=pltpu.CompilerParams(has_side_effects=_EFFECT),
    )(*[pltpu.with_memory_space_constraint(b, pltpu.HBM) for b in bufs], after)
    return out[0], out[1], list(out[2:2 + nb]), out[-1]


def _wait_copies(flight, copies, after, name):
    send_sems, recv_sems, bufs, _ = flight
    nb = len(bufs)

    def body(*refs):
        for cp in copies(refs[:nb], refs[nb], refs[nb + 1]):
            cp.wait_send()
            cp.wait_recv()

    return pl.pallas_call(
        body, name=name,
        out_shape=tuple(pltpu.HBM(b.shape, b.dtype) for b in bufs),
        in_specs=(_HBM,) * nb + (_SEM, _SEM, pl.BlockSpec(memory_space=pl.ANY)), out_specs=(_HBM,) * nb,
        input_output_aliases={i: i for i in range(nb)},
        compiler_params=pltpu.CompilerParams(has_side_effects=_EFFECT),
    )(*bufs, send_sems, recv_sems, after)


def _chip_copies(refs, send_sems, recv_sems):
    pair_ref, land_ref = refs
    x, y, c = _place()
    chips = [(1 - x, y), (x, 1 - y), (1 - x, 1 - y)]
    return [pltpu.make_async_remote_copy(
        src_ref=pair_ref.at[2 * chip[0] + chip[1]], dst_ref=land_ref.at[k],
        send_sem=send_sems.at[k], recv_sem=recv_sems.at[k],
        device_id=(*chip, c), device_id_type=MESH) for k, chip in enumerate(chips)]


def _own_block_copies(targets):
    def copies(refs, send_sems, recv_sems):
        x, y, c = _place()
        mine = refs[0].at[4 * x + 2 * y + c]
        return [pltpu.make_async_remote_copy(
            src_ref=mine, dst_ref=mine, send_sem=send_sems.at[k], recv_sem=recv_sems.at[k],
            device_id=to, device_id_type=MESH) for k, to in enumerate(targets(x, y, c))]
    return copies


def _my_core_and_sibling(x, y, c):
    return [(x, y, 1 - c), (1 - x, y, c), (x, 1 - y, c), (1 - x, 1 - y, c)]


def _all_others(x, y, c):
    flip = lambda v, f: 1 - v if f else v
    return [(flip(x, r & 4), flip(y, r & 2), flip(c, r & 1)) for r in range(1, N_DEV)]


def _forward_copies(refs, send_sems, recv_sems):
    x, y, c = _place()
    chips = [(1 - x, y), (x, 1 - y), (1 - x, 1 - y)]
    return [pltpu.make_async_remote_copy(
        src_ref=refs[0].at[4 * chip[0] + 2 * chip[1] + c], dst_ref=refs[0].at[4 * chip[0] + 2 * chip[1] + c],
        send_sem=send_sems.at[k], recv_sem=recv_sems.at[k],
        device_id=(x, y, 1 - c), device_id_type=MESH) for k, chip in enumerate(chips)]


def _near_targets(x, y, c):
    return [(x, y, 1 - c), (1 - x, y, c), (x, 1 - y, c)]


def _second_stage_copies(refs, send_sems, recv_sems):
    x, y, c = _place()
    relayed = ((x + 1 - c) % 2, (y + c) % 2, c)
    relay_to = ((x + c) % 2, (y + 1 - c) % 2, c)
    plan = [((1 - x, y, c), (x, y, 1 - c)), ((x, 1 - y, c), (x, y, 1 - c)), (relayed, relay_to)]
    copies = []
    for k, ((px, py, pc), to) in enumerate(plan):
        blk = refs[0].at[4 * px + 2 * py + pc]
        copies.append(pltpu.make_async_remote_copy(
            src_ref=blk, dst_ref=blk, send_sem=send_sems.at[k], recv_sem=recv_sems.at[k],
            device_id=to, device_id_type=MESH))
    return copies


def _diagonal_forward_copies(refs, send_sems, recv_sems):
    x, y, c = _place()
    blk = refs[0].at[4 * (1 - x) + 2 * (1 - y) + c]
    return [pltpu.make_async_remote_copy(
        src_ref=blk, dst_ref=blk, send_sem=send_sems.at[0], recv_sem=recv_sems.at[0],
        device_id=(x, y, 1 - c), device_id_type=MESH)]


def _with_own_slot(block, me):
    return lax.dynamic_update_index_in_dim(lax.empty((N_DEV,) + block.shape, block.dtype), block, me, 0)


def _matmul(a, b, dims, out_dtype, tm, tn, name, dep=None):
    if dims == "nn":
        (m, k), n = a.shape, b.shape[1]
        a_spec = pl.BlockSpec((tm, k), lambda i, j: (i, 0))
        b_spec = pl.BlockSpec((k, tn), lambda i, j: (0, j))
        contract = ((1,), (0,))
    elif dims == "nt":
        (m, k), n = a.shape, b.shape[0]
        a_spec = pl.BlockSpec((tm, k), lambda i, j: (i, 0))
        b_spec = pl.BlockSpec((tn, k), lambda i, j: (j, 0))
        contract = ((1,), (1,))
    else:
        (k, m), n = a.shape, b.shape[1]
        a_spec = pl.BlockSpec((k, tm), lambda i, j: (0, i))
        b_spec = pl.BlockSpec((k, tn), lambda i, j: (0, j))
        contract = ((0,), (0,))
    assert m % tm == 0 and n % tn == 0 and a.dtype == BF16 and b.dtype == BF16

    def body(a_ref, b_ref, *rest):
        rest[-1][...] = lax.dot_general(a_ref[...], b_ref[...], (contract, ((), ())),
                                        preferred_element_type=F32).astype(out_dtype)

    deps = [] if dep is None else [dep]
    return pl.pallas_call(
        body, name=name, grid=(m // tm, n // tn),
        in_specs=[a_spec, b_spec] + [pl.BlockSpec((8, 128), lambda i, j: (0, 0))] * len(deps),
        out_specs=pl.BlockSpec((tm, tn), lambda i, j: (i, j)),
        out_shape=jax.ShapeDtypeStruct((m, n), out_dtype),
        compiler_params=_params(dimension_semantics=("arbitrary", "arbitrary")),
    )(a, b, *deps)


Z_TILE = 768
_Z_TILE_ORDER = ((0, 1, 2, 3, 4, 5, 6), (0, 1, 2, 6, 3, 4, 5), (0, 4, 5, 6, 1, 2, 3), (2, 3, 4, 5, 0, 1, 6))
_Z_EARLY_TILES = 4


def _z_proj(h, w_in_t, order, first, count, z_prev, name):
    t = h.shape[0]

    def body(order_ref, h_ref, w_ref, *rest):
        rest[-1][...] = _dot_nt(h_ref[...], w_ref[...])

    prev = [] if z_prev is None else [z_prev]
    return pl.pallas_call(
        body, name=name,
        grid_spec=pltpu.PrefetchScalarGridSpec(
            num_scalar_prefetch=1, grid=(count,),
            in_specs=[pl.BlockSpec((t, D_MODEL), lambda j, o: (0, 0)),
                      pl.BlockSpec((Z_TILE, D_MODEL), lambda j, o: (o[first + j], 0))]
            + [pl.BlockSpec(memory_space=pl.ANY)] * len(prev),
            out_specs=pl.BlockSpec((t, Z_TILE), lambda j, o: (0, o[first + j]))),
        out_shape=jax.ShapeDtypeStruct((t, D_IN), F32),
        input_output_aliases={3: 0} if prev else {},
        compiler_params=_params(dimension_semantics=("arbitrary",)),
    )(order, h, w_in_t, *prev)


def _modulation(c_all, w_ada, b_ada_mine):
    def body(c_ref, w_ref, b_ref, act_ref, mod_ref):
        cv = c_ref[...]
        act = cv * _sigmoid(cv)
        act_ref[...] = act
        mod_ref[...] = jnp.dot(act.astype(BF16), w_ref[...].astype(BF16), preferred_element_type=F32) + b_ref[...]

    return pl.pallas_call(
        body, name="modulation",
        out_shape=(jax.ShapeDtypeStruct(c_all.shape, F32), jax.ShapeDtypeStruct((N_DEV, W_ADA_SHARD), F32)),
        compiler_params=_params(),
    )(c_all, w_ada, b_ada_mine)


def _modulated_norm(x, norm_g, scale, shift, tm=256):
    t, d = x.shape

    def body(x_ref, g_ref, sc_ref, sh_ref, h_ref):
        xv = x_ref[...]
        r = lax.rsqrt(jnp.mean(xv * xv, axis=-1, keepdims=True) + EPS)
        h = (xv * r) * g_ref[...] * (1.0 + sc_ref[...]) + sh_ref[...]
        h_ref[...] = h.astype(BF16)

    row = pl.BlockSpec((1, d), lambda i: (0, 0))
    return pl.pallas_call(
        body, name="modulated_norm", grid=(t // tm,),
        in_specs=[pl.BlockSpec((tm, d), lambda i: (i, 0)), row, row, row],
        out_specs=pl.BlockSpec((tm, d), lambda i: (i, 0)),
        out_shape=jax.ShapeDtypeStruct((t, d), BF16),
        compiler_params=_params(dimension_semantics=("arbitrary",)),
    )(x, norm_g, scale, shift)


def _window_bias(block_index):
    s = lax.broadcasted_iota(jnp.int32, (2 * BLOCK, BLOCK), 0)
    t = lax.broadcasted_iota(jnp.int32, (2 * BLOCK, BLOCK), 1)
    valid = ((s < BLOCK) & (s > t) & (block_index > 0)) | ((s >= BLOCK) & ((s - BLOCK) <= t))
    bias = jnp.where(valid, 0.0, -jnp.inf).astype(F32)
    return jnp.concatenate([bias] * 8, axis=1)


def _heads_t(pair_blocks, g):
    top = lax.broadcasted_iota(jnp.int32, (BLOCK, BLOCK), 0) < HEAD_DIM
    zeros = jnp.zeros((HEAD_DIM, BLOCK), F32)
    tiles = []
    for blk in pair_blocks:
        tp = blk.T
        if g == 0:
            tiles += [jnp.where(top, tp, 0.0), jnp.concatenate([tp[HEAD_DIM:], zeros], axis=0)]
        else:
            tiles += [jnp.concatenate([zeros, tp[:HEAD_DIM]], axis=0), jnp.where(top, 0.0, tp)]
    return jnp.concatenate(tiles, axis=1)


def _pair_block(xt, p, g):
    r0 = HEAD_DIM * g
    even = xt[r0:r0 + HEAD_DIM, (2 * p) * BLOCK:(2 * p + 1) * BLOCK]
    odd = xt[r0:r0 + HEAD_DIM, (2 * p + 1) * BLOCK:(2 * p + 2) * BLOCK]
    return jnp.concatenate([even, odd], axis=0).T


def _softmax_t(scores_t, bias, sink):
    st = scores_t * ATTN_SCALE + bias
    m = jnp.maximum(jnp.max(st, axis=0, keepdims=True), sink)
    e = jnp.exp(st - m)
    es = jnp.exp(sink - m)
    inv = 1.0 / (jnp.sum(e, axis=0, keepdims=True) + es)
    return e * inv, es * inv


def _dot(a, b):
    return jnp.dot(a, b, preferred_element_type=F32)


def _dot_nt(a, b):
    return lax.dot_general(a, b, (((1,), (1,)), ((), ())), preferred_element_type=F32)


def _layer_norm_fwd(v):
    mu = jnp.mean(v, axis=-1, keepdims=True)
    xc = v - mu
    rstd = lax.rsqrt(jnp.mean(xc * xc, axis=-1, keepdims=True) + EPS)
    return xc * rstd, rstd


def _tril(transposed=False):
    t = lax.broadcasted_iota(jnp.int32, (BLOCK, BLOCK), 0)
    s = lax.broadcasted_iota(jnp.int32, (BLOCK, BLOCK), 1)
    return s >= t if transposed else t >= s


def _const_spec(shape):
    return pl.BlockSpec(shape, lambda i: (0,) * len(shape))


def _kv_prev_spec(index):
    return pl.BlockSpec((BLOCK, 2 * D_KV), lambda i: (jnp.maximum(index(i) - 1, 0), SEG_KV // (2 * D_KV)))


def _keys_values(z_ref, kvp_ref):
    kvp, kvc = kvp_ref[...], z_ref[:, SEG_KV:SEG_KV + 2 * D_KV]
    kk = jnp.concatenate([kvp[:, :D_KV], kvc[:, :D_KV]], axis=0)
    vv = jnp.concatenate([kvp[:, D_KV:], kvc[:, D_KV:]], axis=0)
    return kk, vv


def _pair_cols(g, p, base=0):
    return slice(base + (4 * g + p) * 128, base + (4 * g + p + 1) * 128)


def _mixer_fwd(z, sink_rows, ln_g, ln_b, sgu_w, sgu_bt):
    t = z.shape[0]

    def body(z_ref, kvp_ref, sink_ref, lng_ref, lnb_ref, w_ref, bt_ref, a_ref):
        bias = _window_bias(pl.program_id(0))
        kk, vv = _keys_values(z_ref, kvp_ref)
        kk_b, vvt_b = kk.astype(BF16), vv.T.astype(BF16)
        for g in range(2):
            qt = _heads_t([z_ref[:, _pair_cols(g, p, SEG_Q)] for p in range(4)], g).astype(BF16)
            prob, _ = _softmax_t(_dot(kk_b, qt), bias, sink_ref[g])
            ot = _dot(vvt_b, prob.astype(BF16))
            for p in range(4):
                gate = z_ref[:, _pair_cols(g, p, SEG_GA)]
                a_ref[:, _pair_cols(g, p)] = (_pair_block(ot, p, g) * (gate * _sigmoid(gate))).astype(BF16)

        vhat, _ = _layer_norm_fwd(z_ref[:, SEG_VS:SEG_VS + D_SGU])
        vn = vhat * lng_ref[...] + lnb_ref[...]
        tril = _tril()
        for g in range(SGU_GROUPS):
            cols = slice(g * 128, (g + 1) * 128)
            wm = jnp.where(tril, w_ref[g], 0.0).astype(BF16)
            mixed = _dot(wm, vn[:, cols].astype(BF16)) + bt_ref[:, g:g + 1]
            gate = z_ref[:, SEG_GS + g * 128:SEG_GS + (g + 1) * 128]
            a_ref[:, D_ATTN + g * 128:D_ATTN + (g + 1) * 128] = (
                (z_ref[:, SEG_U + g * 128:SEG_U + (g + 1) * 128] * mixed) * (gate * _sigmoid(gate))).astype(BF16)

    return pl.pallas_call(
        body, name="mixer_fwd", grid=(t // BLOCK,),
        in_specs=[pl.BlockSpec((BLOCK, D_IN), lambda i: (i, 0)), _kv_prev_spec(lambda i: i),
                  _const_spec((2, 1, 8 * BLOCK)), _const_spec((1, D_SGU)), _const_spec((1, D_SGU)),
                  _const_spec((SGU_GROUPS, BLOCK, BLOCK)), _const_spec((BLOCK, SGU_GROUPS))],
        out_specs=pl.BlockSpec((BLOCK, D_MODEL), lambda i: (i, 0)),
        out_shape=jax.ShapeDtypeStruct((t, D_MODEL), BF16),
        compiler_params=_params(dimension_semantics=("arbitrary",)),
    )(z, z, sink_rows, ln_g, ln_b, sgu_w, sgu_bt)


def _mixer_bwd(z, da, sink_rows, ln_g, ln_b, sgu_w, sgu_wt, sgu_bt):
    t = z.shape[0]
    nb = t // BLOCK

    def body(z_ref, kvp_ref, da_ref, sink_ref, lng_ref, lnb_ref, w_ref, wt_ref, bt_ref,
             dz_ref, dsink_ref, dw_ref, db_ref, dlng_ref, dlnb_ref, carry_ref, dsink_acc, dbt_acc):
        step = pl.program_id(0)

        @pl.when(step == 0)
        def _():
            carry_ref[...] = jnp.zeros_like(carry_ref)
            dsink_acc[...] = jnp.zeros_like(dsink_acc)
            dbt_acc[...] = jnp.zeros_like(dbt_acc)
            dw_ref[...] = jnp.zeros_like(dw_ref)
            dlng_ref[...] = jnp.zeros_like(dlng_ref)
            dlnb_ref[...] = jnp.zeros_like(dlnb_ref)

        bias = _window_bias(nb - 1 - step)
        kk, vv = _keys_values(z_ref, kvp_ref)
        kk_b, vv_b = kk.astype(BF16), vv.astype(BF16)
        kkt_b, vvt_b = kk.T.astype(BF16), vv.T.astype(BF16)
        dkk = jnp.zeros((2 * BLOCK, D_KV), F32)
        dvv = jnp.zeros((2 * BLOCK, D_KV), F32)
        for g in range(2):
            qt = _heads_t([z_ref[:, _pair_cols(g, p, SEG_Q)] for p in range(4)], g).astype(BF16)
            prob, sink_prob = _softmax_t(_dot(kk_b, qt), bias, sink_ref[g])
            prob_b = prob.astype(BF16)
            ot = _dot(vvt_b, prob_b)
            gates = [z_ref[:, _pair_cols(g, p, SEG_GA)] for p in range(4)]
            sig = [_sigmoid(gt) for gt in gates]
            d_attn = [da_ref[:, _pair_cols(g, p)] for p in range(4)]
            d_ot = _heads_t([d_attn[p] * (gates[p] * sig[p]) for p in range(4)], g).astype(BF16)
            d_prob = _dot(vv_b, d_ot)
            delta = jnp.sum(prob * d_prob, axis=0, keepdims=True)
            d_scores = (prob * (d_prob - delta) * ATTN_SCALE).astype(BF16)
            dsink_acc[g] -= sink_prob * delta
            d_qt = _dot(kkt_b, d_scores)
            dkk = dkk + _dot_nt(d_scores, qt)
            dvv = dvv + _dot_nt(prob_b, d_ot)
            for p in range(4):
                dz_ref[:, _pair_cols(g, p, SEG_Q)] = _pair_block(d_qt, p, g).astype(BF16)
                d_silu = sig[p] * (1.0 + gates[p] * (1.0 - sig[p]))
                dz_ref[:, _pair_cols(g, p, SEG_GA)] = (d_attn[p] * _pair_block(ot, p, g) * d_silu).astype(BF16)
        d_kv = jnp.concatenate([dkk, dvv], axis=1)
        dz_ref[:, SEG_KV:SEG_KV + 2 * D_KV] = (d_kv[BLOCK:] + carry_ref[...]).astype(BF16)
        carry_ref[...] = d_kv[:BLOCK]

        vhat, rstd = _layer_norm_fwd(z_ref[:, SEG_VS:SEG_VS + D_SGU])
        lng = lng_ref[...]
        vn = vhat * lng + lnb_ref[...]
        tril, triu = _tril(), _tril(transposed=True)
        lane = lax.broadcasted_iota(jnp.int32, (BLOCK, 128), 1)
        d_bt = jnp.zeros((BLOCK, 128), F32)
        d_vn = []
        for g in range(SGU_GROUPS):
            cols = slice(g * 128, (g + 1) * 128)
            wm = jnp.where(tril, w_ref[g], 0.0).astype(BF16)
            wmt = jnp.where(triu, wt_ref[g], 0.0).astype(BF16)
            vn_g = vn[:, cols].astype(BF16)
            mixed = _dot(wm, vn_g) + bt_ref[:, g:g + 1]
            gate = z_ref[:, SEG_GS + g * 128:SEG_GS + (g + 1) * 128]
            u = z_ref[:, SEG_U + g * 128:SEG_U + (g + 1) * 128]
            d_out = da_ref[:, D_ATTN + g * 128:D_ATTN + (g + 1) * 128]
            sg = _sigmoid(gate)
            d_um = d_out * (gate * sg)
            dz_ref[:, SEG_U + g * 128:SEG_U + (g + 1) * 128] = (d_um * mixed).astype(BF16)
            dz_ref[:, SEG_GS + g * 128:SEG_GS + (g + 1) * 128] = (
                d_out * (u * mixed) * (sg * (1.0 + gate * (1.0 - sg)))).astype(BF16)
            d_mixed = d_um * u
            d_mixed_b = d_mixed.astype(BF16)
            dw_ref[g] += jnp.where(tril, _dot_nt(d_mixed_b, vn_g), 0.0)
            d_bt = d_bt + jnp.where(lane == g, jnp.sum(d_mixed, axis=-1, keepdims=True), 0.0)
            d_vn.append(_dot(wmt, d_mixed_b))
        dbt_acc[...] += d_bt
        d_vn = jnp.concatenate(d_vn, axis=1)
        dlng_ref[...] += jnp.sum(d_vn * vhat, axis=0, keepdims=True)
        dlnb_ref[...] += jnp.sum(d_vn, axis=0, keepdims=True)
        d_vhat = d_vn * lng
        d_v = rstd * (d_vhat - jnp.mean(d_vhat, axis=-1, keepdims=True)
                      - vhat * jnp.mean(d_vhat * vhat, axis=-1, keepdims=True))
        dz_ref[:, SEG_VS:SEG_VS + D_SGU] = d_v.astype(BF16)

        @pl.when(step == nb - 1)
        def _():
            db_ref[...] = dbt_acc[...].T[:SGU_GROUPS]
            lane_row = lax.broadcasted_iota(jnp.int32, (1, 128), 1)
            d_sink = jnp.zeros((1, 128), F32)
            for g in range(2):
                acc = dsink_acc[g]
                for j in range(8):
                    head_sum = jnp.sum(acc[:, j * BLOCK:(j + 1) * BLOCK], axis=-1, keepdims=True)
                    d_sink = d_sink + jnp.where(lane_row == 8 * g + j, head_sum, 0.0)
            dsink_ref[...] = d_sink

    rev = lambda i: nb - 1 - i
    return pl.pallas_call(
        body, name="mixer_bwd", grid=(nb,),
        in_specs=[pl.BlockSpec((BLOCK, D_IN), lambda i: (rev(i), 0)), _kv_prev_spec(rev),
                  pl.BlockSpec((BLOCK, D_MODEL), lambda i: (rev(i), 0)),
                  _const_spec((2, 1, 8 * BLOCK)), _const_spec((1, D_SGU)), _const_spec((1, D_SGU)),
                  _const_spec((SGU_GROUPS, BLOCK, BLOCK)), _const_spec((SGU_GROUPS, BLOCK, BLOCK)),
                  _const_spec((BLOCK, SGU_GROUPS))],
        out_specs=(pl.BlockSpec((BLOCK, D_IN), lambda i: (rev(i), 0)), _const_spec((1, 128)),
                   _const_spec((SGU_GROUPS, BLOCK, BLOCK)), _const_spec((SGU_GROUPS, BLOCK)),
                   _const_spec((1, D_SGU)), _const_spec((1, D_SGU))),
        out_shape=(jax.ShapeDtypeStruct((t, D_IN), BF16), jax.ShapeDtypeStruct((1, 128), F32),
                   jax.ShapeDtypeStruct((SGU_GROUPS, BLOCK, BLOCK), F32), jax.ShapeDtypeStruct((SGU_GROUPS, BLOCK), F32),
                   jax.ShapeDtypeStruct((1, D_SGU), F32), jax.ShapeDtypeStruct((1, D_SGU), F32)),
        scratch_shapes=[pltpu.VMEM((BLOCK, 2 * D_KV), F32), pltpu.VMEM((2, 1, 8 * BLOCK), F32),
                        pltpu.VMEM((BLOCK, 128), F32)],
        compiler_params=_params(dimension_semantics=("arbitrary",)),
    )(z, z, da, sink_rows, ln_g, ln_b, sgu_w, sgu_wt, sgu_bt)


def _head(y, x, target, gate, final_g, tm=256):
    t, d = x.shape

    def body(y_ref, x_ref, tg_ref, gate_ref, fg_ref, dx2_ref, dy_ref, loss_ref, dfg_ref, dgate_ref):
        @pl.when(pl.program_id(0) == 0)
        def _():
            loss_ref[...] = jnp.zeros_like(loss_ref)
            dfg_ref[...] = jnp.zeros_like(dfg_ref)
            dgate_ref[...] = jnp.zeros_like(dgate_ref)

        yv, gate, fg = y_ref[...], gate_ref[...], fg_ref[...]
        x2 = x_ref[...] + gate * yv
        r2 = lax.rsqrt(jnp.mean(x2 * x2, axis=-1, keepdims=True) + EPS)
        nrm = x2 * r2
        err = nrm * fg - tg_ref[...]
        loss_ref[...] += 0.5 * jnp.sum(jnp.mean(err * err, axis=-1, keepdims=True), axis=0, keepdims=True)
        d_out = err * (1.0 / d)
        dfg_ref[...] += jnp.sum(d_out * nrm, axis=0, keepdims=True)
        d_nrm = d_out * fg
        dx2 = r2 * (d_nrm - nrm * jnp.mean(d_nrm * nrm, axis=-1, keepdims=True))
        dx2_ref[...] = dx2
        dgate_ref[...] += jnp.sum(dx2 * yv, axis=0, keepdims=True)
        dy_ref[...] = (dx2 * gate).astype(BF16)

    blk = pl.BlockSpec((tm, d), lambda i: (i, 0))
    row = _const_spec((1, d))
    return pl.pallas_call(
        body, name="head", grid=(t // tm,),
        in_specs=[blk, blk, blk, row, row],
        out_specs=(blk, blk, _const_spec((1, 128)), row, row),
        out_shape=(jax.ShapeDtypeStruct((t, d), F32), jax.ShapeDtypeStruct((t, d), BF16),
                   jax.ShapeDtypeStruct((1, 128), F32), jax.ShapeDtypeStruct((1, d), F32),
                   jax.ShapeDtypeStruct((1, d), F32)),
        compiler_params=_params(dimension_semantics=("arbitrary",)),
    )(y, x, target, gate, final_g)


def _modulated_norm_bwd(dh, x, dx2, norm_g, scale, tm=256):
    t, d = x.shape

    def body(dh_ref, x_ref, dx2_ref, g_ref, sc_ref, gx_ref, dshift_ref, dscale_ref, dg_ref):
        @pl.when(pl.program_id(0) == 0)
        def _():
            dshift_ref[...] = jnp.zeros_like(dshift_ref)
            dscale_ref[...] = jnp.zeros_like(dscale_ref)
            dg_ref[...] = jnp.zeros_like(dg_ref)

        dh, xv, g = dh_ref[...], x_ref[...], g_ref[...]
        one_plus = 1.0 + sc_ref[...]
        r = lax.rsqrt(jnp.mean(xv * xv, axis=-1, keepdims=True) + EPS)
        xn = xv * r
        dshift_ref[...] += jnp.sum(dh, axis=0, keepdims=True)
        dscale_ref[...] += jnp.sum(dh * (xn * g), axis=0, keepdims=True)
        d_y = dh * one_plus
        dg_ref[...] += jnp.sum(d_y * xn, axis=0, keepdims=True)
        d_xn = d_y * g
        gx_ref[...] = dx2_ref[...] + r * (d_xn - xn * jnp.mean(d_xn * xn, axis=-1, keepdims=True))

    blk = pl.BlockSpec((tm, d), lambda i: (i, 0))
    row = _const_spec((1, d))
    return pl.pallas_call(
        body, name="modulated_norm_bwd", grid=(t // tm,),
        in_specs=[blk, blk, blk, row, row], out_specs=(blk, row, row, row),
        out_shape=(jax.ShapeDtypeStruct((t, d), F32),) + (jax.ShapeDtypeStruct((1, d), F32),) * 3,
        compiler_params=_params(dimension_semantics=("arbitrary",)),
    )(dh, x, dx2, norm_g, scale)


def _pair_sum(core, blocks, got, name, tr):
    _, _, r, c = blocks.shape

    def body(core_ref, a_ref, b_ref, o_ref):
        o_ref[...] = (a_ref[...].astype(F32) + b_ref[...].astype(F32)).astype(BF16)

    return pl.pallas_call(
        body, name=name,
        grid_spec=pltpu.PrefetchScalarGridSpec(
            num_scalar_prefetch=1, grid=(4, r // tr),
            in_specs=[pl.BlockSpec((None, None, tr, c), lambda m, i, core_ref: (m, core_ref[0], i, 0)),
                      pl.BlockSpec((None, tr, c), lambda m, i, core_ref: (m, i, 0))],
            out_specs=pl.BlockSpec((None, tr, c), lambda m, i, core_ref: (m, i, 0))),
        out_shape=jax.ShapeDtypeStruct(got.shape, BF16),
        compiler_params=_params(dimension_semantics=("arbitrary", "arbitrary")),
    )(core, blocks, got)


def _adamw(w, g, m, v):
    m = ADAM_B1 * m + (1.0 - ADAM_B1) * g
    v = ADAM_B2 * v + (1.0 - ADAM_B2) * (g * g)
    m_hat = m / (1.0 - ADAM_B1 ** ADAM_STEP)
    v_hat = v / (1.0 - ADAM_B2 ** ADAM_STEP)
    delta = -ADAM_LR * (m_hat / (jnp.sqrt(v_hat) + ADAM_EPS) + ADAM_WD * w)
    return delta, m, v


def _adam_from_chips(chip, pair, land, w, m, v, name, tc):
    _, r, c = pair.shape

    def body(chip_ref, own_ref, land_ref, w_ref, m_ref, v_ref, g_ref, d_ref, nm_ref, nv_ref):
        g = own_ref[...].astype(F32)
        for k in range(3):
            g = g + land_ref[k].astype(F32)
        g_ref[...] = g
        d_ref[...], nm_ref[...], nv_ref[...] = _adamw(w_ref[...], g, m_ref[...], v_ref[...])

    blk = pl.BlockSpec((r, tc), lambda i, chip_ref: (0, i))
    return pl.pallas_call(
        body, name=name,
        grid_spec=pltpu.PrefetchScalarGridSpec(
            num_scalar_prefetch=1, grid=(c // tc,),
            in_specs=[pl.BlockSpec((None, r, tc), lambda i, chip_ref: (chip_ref[0], 0, i)),
                      pl.BlockSpec((3, r, tc), lambda i, chip_ref: (0, 0, i)), blk, blk, blk],
            out_specs=(blk,) * 4),
        out_shape=(jax.ShapeDtypeStruct((r, c), F32),) * 4,
        compiler_params=_params(dimension_semantics=("arbitrary",)),
    )(chip, pair, land, w, m, v)


def _adam_w_ada(act_t, dmod_mine, w, m, v, tr=256):
    r, c = w.shape

    def body(a_ref, dm_ref, w_ref, m_ref, v_ref, g_ref, d_ref, nm_ref, nv_ref):
        g = _dot(a_ref[...].astype(BF16), dm_ref[...].astype(BF16))
        g_ref[...] = g
        d_ref[...], nm_ref[...], nv_ref[...] = _adamw(w_ref[...], g, m_ref[...], v_ref[...])

    blk = pl.BlockSpec((tr, c), lambda i: (i, 0))
    return pl.pallas_call(
        body, name="adam_w_ada", grid=(r // tr,),
        in_specs=[pl.BlockSpec((tr, N_DEV), lambda i: (i, 0)), _const_spec((N_DEV, c)), blk, blk, blk],
        out_specs=(blk,) * 4, out_shape=(jax.ShapeDtypeStruct((r, c), F32),) * 4,
        compiler_params=_params(dimension_semantics=("arbitrary",)),
    )(act_t, dmod_mine, w, m, v)


def _pack_small(d_shift, d_scale, d_gate, d_norm_g, d_final_g, d_ln_g, d_ln_b, loss, d_sinks, d_sgu_b):
    def body(shift_ref, scale_ref, gate_ref, ng_ref, fg_ref, lng_ref, lnb_ref, loss_ref, sink_ref, b_ref, o_ref):
        o_ref[...] = jnp.zeros_like(o_ref)
        o_ref[ROW_SHIFT:ROW_SHIFT + 1, :] = shift_ref[...]
        o_ref[ROW_SCALE:ROW_SCALE + 1, :] = scale_ref[...]
        o_ref[ROW_GATE:ROW_GATE + 1, :] = gate_ref[...]
        o_ref[ROW_NORM_G:ROW_NORM_G + 1, :] = ng_ref[...]
        o_ref[ROW_FINAL_G:ROW_FINAL_G + 1, :] = fg_ref[...]
        o_ref[ROW_LN:ROW_LN + 1, 0:D_SGU] = lng_ref[...]
        o_ref[ROW_LN:ROW_LN + 1, D_SGU:2 * D_SGU] = lnb_ref[...]
        o_ref[ROW_MISC:ROW_MISC + 1, 0:128] = loss_ref[...]
        o_ref[ROW_MISC:ROW_MISC + 1, 128:256] = sink_ref[...]
        o_ref[ROW_SGU_B:ROW_SGU_B + SGU_GROUPS, 0:BLOCK] = b_ref[...]

    return pl.pallas_call(
        body, name="pack_small", out_shape=jax.ShapeDtypeStruct((SMALL_ROWS, D_MODEL), F32),
        compiler_params=_params(),
    )(d_shift, d_scale, d_gate, d_norm_g, d_final_g, d_ln_g, d_ln_b, loss, d_sinks, d_sgu_b)


_SMALL_NAMES = ("norm_g", "b_ada", "attn_sinks", "sgu_ln_g", "sgu_ln_b", "sgu_w", "sgu_b", "final_g")


def _adam_small(partials, d_sgu_w_all, weights, moments_m, moments_v):
    names = _SMALL_NAMES
    k = len(names)

    def body(*refs):
        p_ref, sw_ref = refs[0], refs[1]
        w_refs, m_refs, v_refs = refs[2:2 + k], refs[2 + k:2 + 2 * k], refs[2 + 2 * k:2 + 3 * k]
        loss_ref, dmod_ref = refs[2 + 3 * k], refs[3 + 3 * k]
        out_refs = refs[4 + 3 * k:4 + 7 * k]
        sum_ref = refs[4 + 7 * k]
        total = p_ref[0]
        for j in range(1, N_DEV):
            total = total + p_ref[j]
        sum_ref[...] = total
        for j in range(N_DEV):
            for part, row in enumerate((ROW_SHIFT, ROW_SCALE, ROW_GATE)):
                dmod_ref[j:j + 1, part * D_MODEL:(part + 1) * D_MODEL] = p_ref[j, row:row + 1, :]
        loss_ref[...] = sum_ref[ROW_MISC:ROW_MISC + 1, 0:1]
        d_sgu_w = sw_ref[0]
        for j in range(1, N_DEV):
            d_sgu_w = d_sgu_w + sw_ref[j]
        grads = {
            "norm_g": sum_ref[ROW_NORM_G:ROW_NORM_G + 1, :],
            "b_ada": jnp.concatenate([sum_ref[r:r + 1, :] for r in (ROW_SHIFT, ROW_SCALE, ROW_GATE)], axis=1),
            "attn_sinks": sum_ref[ROW_MISC:ROW_MISC + 1, 128:128 + N_Q_HEADS],
            "sgu_ln_g": sum_ref[ROW_LN:ROW_LN + 1, 0:D_SGU],
            "sgu_ln_b": sum_ref[ROW_LN:ROW_LN + 1, D_SGU:2 * D_SGU],
            "sgu_w": d_sgu_w[None],
            "sgu_b": sum_ref[ROW_SGU_B:ROW_SGU_B + SGU_GROUPS, 0:BLOCK][None],
            "final_g": sum_ref[ROW_FINAL_G:ROW_FINAL_G + 1, :],
        }
        for i, name in enumerate(names):
            g = grads[name]
            delta, m, v = _adamw(w_refs[i][...], g, m_refs[i][...], v_refs[i][...])
            out_refs[4 * i][...] = g
            out_refs[4 * i + 1][...] = delta
            out_refs[4 * i + 2][...] = m
            out_refs[4 * i + 3][...] = v

    shapes = [jax.ShapeDtypeStruct((1, 1), F32), jax.ShapeDtypeStruct((N_DEV, 3 * D_MODEL), F32)]
    for name in names:
        shapes += [jax.ShapeDtypeStruct(weights[name].shape, F32)] * 4
    outs = pl.pallas_call(
        body, name="adam_small", out_shape=tuple(shapes),
        scratch_shapes=[pltpu.VMEM((SMALL_ROWS, D_MODEL), F32)],
        compiler_params=_params(),
    )(partials, d_sgu_w_all, *[weights[n] for n in names], *[moments_m[n] for n in names],
      *[moments_v[n] for n in names])
    return outs[0], outs[1], {name: outs[2 + 4 * i:6 + 4 * i] for i, name in enumerate(names)}


def kernel(x, c, norm_g, w_ada, b_ada, w_in, attn_sinks, sgu_ln_g, sgu_ln_b, sgu_w, sgu_b, w_out, final_g, loss_target, m_norm_g, m_w_ada, m_b_ada, m_w_in, m_attn_sinks, m_sgu_ln_g, m_sgu_ln_b, m_sgu_w, m_sgu_b, m_w_out, m_final_g, v_norm_g, v_w_ada, v_b_ada, v_w_in, v_attn_sinks, v_sgu_ln_g, v_sgu_ln_b, v_sgu_w, v_sgu_b, v_w_out, v_final_g):
    xi, yi, ci = _place()
    me = 4 * xi + 2 * yi + ci
    x2d, target = x[0], loss_target[0]
    t = x2d.shape[0]

    core = ci.astype(jnp.int32).reshape(1)
    chip = (2 * xi + yi).astype(jnp.int32).reshape(1)

    c_all = _all_gather([c.reshape(8, 256)], "gather_c", True)[0].reshape(N_DEV, D_MODEL)
    b_mine = lax.dynamic_slice(b_ada, (0, me * W_ADA_SHARD), (1, W_ADA_SHARD))
    c_act, mod_part = _modulation(c_all, w_ada[0], b_mine)
    mod_all = _all_gather([mod_part], "gather_mod", True)[0]

    near = _own_block_copies(_near_targets)
    w_in_flight = _start_copies([_with_own_slot(w_in[0].T.astype(BF16), me)], near, 3, mod_all, "gather_w_in_start")
    mod = lax.dynamic_index_in_dim(mod_all, me, axis=1, keepdims=False).reshape(1, 3 * D_MODEL)
    mod = mod + w_in_flight[3][0, 0]
    shift, scale, gate = mod[:, :D_MODEL], mod[:, D_MODEL:2 * D_MODEL], mod[:, 2 * D_MODEL:]
    h = _modulated_norm(x2d, norm_g, scale, shift)

    w_in_near = _wait_copies(w_in_flight, near, h, "gather_w_in_wait")
    w_in_flight = _start_copies(w_in_near, _second_stage_copies, 3, core, "gather_w_in_second_start")
    w_in_most = _wait_copies(w_in_flight, lambda *a: _second_stage_copies(*a)[:2], h, "gather_w_in_forward_wait")
    tile_order = jnp.asarray(_Z_TILE_ORDER, jnp.int32)[chip[0]]
    z_early = _z_proj(h, w_in_most[0].reshape(D_IN, D_MODEL), tile_order, 0, _Z_EARLY_TILES, None, "z_proj_early")
    w_in_diag = _wait_copies((w_in_flight[0], w_in_flight[1], w_in_most, None),
                             lambda *a: _second_stage_copies(*a)[2:], z_early, "gather_w_in_diagonal_wait")
    w_out_flight = _start_copies([_with_own_slot(w_out[0].astype(BF16), me)], _own_block_copies(_my_core_and_sibling),
                                 4, w_in_diag[0], "gather_w_out_start")
    w_in_flight = _start_copies(w_in_diag, _diagonal_forward_copies, 1, w_out_flight[3], "gather_w_in_last_start")
    w_in_all = _wait_copies(w_in_flight, _diagonal_forward_copies, z_early, "gather_w_in_last_wait")[0]
    w_in_t = w_in_all.reshape(D_IN, D_MODEL)
    z = _z_proj(h, w_in_t, tile_order, _Z_EARLY_TILES, 7 - _Z_EARLY_TILES, z_early, "z_proj_late")
    w_out_half = _wait_copies(w_out_flight, _own_block_copies(_my_core_and_sibling), z, "gather_w_out_wait")
    w_out_flight = _start_copies(w_out_half, _forward_copies, 3, z, "gather_w_out_forward_start")
    sink_rows = jnp.repeat(attn_sinks.reshape(N_Q_HEADS), BLOCK).reshape(2, 1, 8 * BLOCK)
    sgu_bt = sgu_b[0].T
    a = _mixer_fwd(z, sink_rows + w_out_flight[3][0, 0], sgu_ln_g, sgu_ln_b, sgu_w[0], sgu_bt)
    w_out_all = _wait_copies(w_out_flight, _forward_copies, a, "gather_w_out_forward_wait")[0]
    w_out_full = w_out_all.reshape(D_MODEL, D_MODEL)
    y = _matmul(a, w_out_full, "nn", F32, min(t, 1024), 1024, "out_proj")
    final_g_row = final_g.reshape(1, D_MODEL)
    dx2, dy, loss_part, d_final_g, d_gate = _head(y, x2d, target, gate, final_g_row)

    da = _matmul(dy, w_out_full, "nt", F32, min(t, 1024), 1024, "out_proj_bwd")
    dw_out = _matmul(a, dy, "tn", BF16, 1024, 1024, "w_out_grad").reshape(4, 2, W_OUT_SHARD, D_MODEL)
    got_out = _pair_exchange([dw_out], "w_out_grad_pair_exchange")[0]
    pair_out = _pair_sum(core, dw_out, got_out, "w_out_grad_pair_sum", W_OUT_SHARD)
    out_flight = _start_copies([pair_out, lax.empty((3, W_OUT_SHARD, D_MODEL), BF16)], _chip_copies, 3, core,
                               "w_out_grad_chip_start")
    dz, d_sinks, d_sgu_w, d_sgu_b, d_ln_g, d_ln_b = _mixer_bwd(
        z, da, sink_rows + out_flight[3][0, 0], sgu_ln_g, sgu_ln_b, sgu_w[0], jnp.swapaxes(sgu_w[0], 1, 2), sgu_bt)
    sgu_w_flight = _start_copies([_with_own_slot(d_sgu_w, me)], _own_block_copies(_all_others), N_DEV - 1, core,
                                 "sgu_w_grad_gather_start")
    dw_in_t = _matmul(dz, h, "tn", BF16, 768, D_MODEL, "w_in_grad", dep=sgu_w_flight[3])
    dw_in_t = dw_in_t.reshape(4, 2, W_IN_SHARD, D_MODEL)
    got_in = _pair_exchange([dw_in_t], "w_in_grad_pair_exchange")[0]
    pair_in = _pair_sum(core, dw_in_t, got_in, "w_in_grad_pair_sum", W_IN_SHARD // 2)
    in_flight = _start_copies([pair_in, lax.empty((3, W_IN_SHARD, D_MODEL), BF16)], _chip_copies, 3, core,
                              "w_in_grad_chip_start")
    dh = _matmul(dz, w_in_t, "nn", F32, min(t, 1024), 512, "z_proj_bwd", dep=in_flight[3])
    grad_x, d_shift, d_scale, d_norm_g = _modulated_norm_bwd(dh, x2d, dx2, norm_g, scale)

    partial = _pack_small(d_shift, d_scale, d_gate, d_norm_g, d_final_g, d_ln_g, d_ln_b, loss_part, d_sinks, d_sgu_b)
    partial_all = _all_gather([partial], "gather_small", True)[0]
    d_sgu_w_all = _wait_copies(sgu_w_flight, _own_block_copies(_all_others), partial_all, "sgu_w_grad_gather_wait")[0]
    weights = {"norm_g": norm_g, "b_ada": b_ada, "attn_sinks": attn_sinks, "sgu_ln_g": sgu_ln_g,
               "sgu_ln_b": sgu_ln_b, "sgu_w": sgu_w, "sgu_b": sgu_b, "final_g": final_g_row}
    moments_m = {"norm_g": m_norm_g, "b_ada": m_b_ada, "attn_sinks": m_attn_sinks, "sgu_ln_g": m_sgu_ln_g,
                 "sgu_ln_b": m_sgu_ln_b, "sgu_w": m_sgu_w, "sgu_b": m_sgu_b,
                 "final_g": m_final_g.reshape(1, D_MODEL)}
    moments_v = {"norm_g": v_norm_g, "b_ada": v_b_ada, "attn_sinks": v_attn_sinks, "sgu_ln_g": v_sgu_ln_g,
                 "sgu_ln_b": v_sgu_ln_b, "sgu_w": v_sgu_w, "sgu_b": v_sgu_b,
                 "final_g": v_final_g.reshape(1, D_MODEL)}
    loss, dmod_all, small = _adam_small(partial_all, d_sgu_w_all, weights, moments_m, moments_v)
    small["final_g"] = tuple(o.reshape(D_MODEL) for o in small["final_g"])

    dmod_mine = lax.dynamic_slice(dmod_all, (0, me * W_ADA_SHARD), (N_DEV, W_ADA_SHARD))
    big = {"w_ada": _adam_w_ada(c_act.T, dmod_mine, w_ada[0], m_w_ada[0], v_w_ada[0])}
    pair_out, land_out = _wait_copies(out_flight, _chip_copies, big["w_ada"][0], "w_out_grad_chip_wait")
    big["w_out"] = _adam_from_chips(chip, pair_out, land_out, w_out[0], m_w_out[0], v_w_out[0], "adam_w_out", 1024)
    pair_in, land_in = _wait_copies(in_flight, _chip_copies, big["w_out"][0], "w_in_grad_chip_wait")
    big["w_in"] = tuple(o.T for o in _adam_from_chips(
        chip, pair_in, land_in, w_in[0].T, m_w_in[0].T, v_w_in[0].T, "adam_w_in", 256))
    order = ["norm_g", "w_ada", "b_ada", "w_in", "attn_sinks", "sgu_ln_g", "sgu_ln_b", "sgu_w", "sgu_b", "w_out",
             "final_g"]
    outs = [loss.reshape(()), grad_x[None]]
    for k in range(4):
        for name in order:
            outs.append(big[name][k][None] if name in big else small[name][k])
    return tuple(outs)
```

```python
import jax
import jax.numpy as jnp
from jax import lax
from jax.experimental import pallas as pl
from jax.experimental.pallas import tpu as pltpu

F32 = jnp.float32
BF16 = jnp.bfloat16
MESH = pl.DeviceIdType.MESH

N_DEV = 8
D_MODEL = 2048
HEAD_DIM = 64
D_ATTN = 1024
N_Q_HEADS = 16
D_KV = 128
BLOCK = 128
D_SGU = 1024
SGU_GROUPS = 8
D_IN = 5376
W_IN_SHARD = D_IN // N_DEV
W_OUT_SHARD = D_MODEL // N_DEV
W_ADA_SHARD = 3 * D_MODEL // N_DEV
EPS = 1e-6
ATTN_SCALE = 0.125

ADAM_LR = 0.001
ADAM_B1 = 0.9
ADAM_B2 = 0.999
ADAM_EPS = 1e-08
ADAM_WD = 0.01
ADAM_STEP = 10

SEG_Q, SEG_KV, SEG_GA, SEG_U, SEG_VS, SEG_GS = 0, 1024, 1280, 2304, 3328, 4352

VMEM_LIMIT = 56 * 1024 * 1024

ROW_SHIFT, ROW_SCALE, ROW_GATE, ROW_NORM_G, ROW_FINAL_G, ROW_LN, ROW_MISC, ROW_SGU_B = 0, 1, 2, 3, 4, 5, 6, 8
SMALL_ROWS = 16


def _params(**kw):
    return pltpu.CompilerParams(vmem_limit_bytes=VMEM_LIMIT, **kw)


def _sigmoid(x):
    return 0.5 * (jnp.tanh(0.5 * x) + 1.0)


def _place():
    return lax.axis_index("x"), lax.axis_index("y"), lax.axis_index("c")


def _all_gather(shards, name, in_vmem):
    n = len(shards)

    def body(*refs):
        ins, outs = refs[:n], refs[n:2 * n]
        send_sems, recv_sems, local_sems = refs[2 * n:]
        x, y, c = _place()
        me, sibling = (x, y, c), (x, y, 1 - c)
        chips = [(1 - x, y), (x, 1 - y), (1 - x, 1 - y)]
        first, passed, mine = [], [], []
        for a in range(n):
            out_ref = outs[a]

            def slot(px, py, pc, out_ref=out_ref):
                return out_ref.at[4 * px + 2 * py + pc]

            def copy(k, block, to, src=None, a=a, slot=slot):
                return pltpu.make_async_remote_copy(
                    src_ref=slot(*block) if src is None else src, dst_ref=slot(*block),
                    send_sem=send_sems.at[a, k], recv_sem=recv_sems.at[a, k],
                    device_id=to, device_id_type=MESH)

            own = pltpu.make_async_copy(ins[a], slot(*me), local_sems.at[a])
            own.start()
            mine.append(own)
            mine_out = [copy(0, me, sibling, src=ins[a])]
            mine_out += [copy(1 + j, me, (*chip, c), src=ins[a]) for j, chip in enumerate(chips)]
            for cp in mine_out:
                cp.start()
            first += mine_out
            passed.append([copy(4 + j, (*chip, c), sibling) for j, chip in enumerate(chips)])
        for j, chip in enumerate(chips):
            for a in range(n):
                out_ref = outs[a]
                blk = out_ref.at[4 * chip[0] + 2 * chip[1] + c]
                pltpu.make_async_remote_copy(
                    src_ref=blk, dst_ref=blk, send_sem=send_sems.at[a, 1 + j], recv_sem=recv_sems.at[a, 1 + j],
                    device_id=me, device_id_type=MESH).wait_recv()
                passed[a][j].start()
        for a in range(n):
            out_ref = outs[a]
            blk = out_ref.at[4 * x + 2 * y + (1 - c)]
            pltpu.make_async_remote_copy(
                src_ref=blk, dst_ref=blk, send_sem=send_sems.at[a, 0], recv_sem=recv_sems.at[a, 0],
                device_id=me, device_id_type=MESH).wait_recv()
            for j, chip in enumerate(chips):
                blk = out_ref.at[4 * chip[0] + 2 * chip[1] + (1 - c)]
                pltpu.make_async_remote_copy(
                    src_ref=blk, dst_ref=blk, send_sem=send_sems.at[a, 4 + j], recv_sem=recv_sems.at[a, 4 + j],
                    device_id=me, device_id_type=MESH).wait_recv()
        for cp in first:
            cp.wait_send()
        for a in range(n):
            for cp in passed[a]:
                cp.wait_send()
        for cp in mine:
            cp.wait()

    space = pltpu.VMEM if in_vmem else pl.ANY
    spec = pl.BlockSpec(memory_space=space)
    return pl.pallas_call(
        body, name=name,
        out_shape=tuple(jax.ShapeDtypeStruct((N_DEV,) + s.shape, s.dtype) for s in shards),
        in_specs=[spec] * n, out_specs=tuple([spec] * n),
        scratch_shapes=[pltpu.SemaphoreType.DMA((n, 7)), pltpu.SemaphoreType.DMA((n, 7)),
                        pltpu.SemaphoreType.DMA((n,))],
        compiler_params=_params(),
    )(*shards)


def _pair_reduce(blocks, name, row_chunk):
    _, _, r, cols = blocks.shape
    assert r % row_chunk == 0

    def body(in_ref, out_ref, land, own, summed, send_sems, recv_sems, own_sems, out_sems):
        x, y, c = _place()
        sends, loads, stores = [], [], []
        for m in range(4):
            cp = pltpu.make_async_remote_copy(
                src_ref=in_ref.at[m, 1 - c], dst_ref=land.at[m], send_sem=send_sems.at[m], recv_sem=recv_sems.at[m],
                device_id=(x, y, 1 - c), device_id_type=MESH)
            cp.start()
            sends.append(cp)
            ld = pltpu.make_async_copy(in_ref.at[m, c], own.at[m], own_sems.at[m])
            ld.start()
            loads.append(ld)
        for m in range(4):
            sends[m].wait_recv()
            loads[m].wait()
            for k in range(r // row_chunk):
                rows = slice(k * row_chunk, (k + 1) * row_chunk)
                summed[m, rows, :] = (own[m, rows, :].astype(F32) + land[m, rows, :].astype(F32)).astype(BF16)
            st = pltpu.make_async_copy(summed.at[m], out_ref.at[m], out_sems.at[m])
            st.start()
            stores.append(st)
        for m in range(4):
            sends[m].wait_send()
            stores[m].wait()

    spec = pl.BlockSpec(memory_space=pl.ANY)
    return pl.pallas_call(
        body, name=name, out_shape=jax.ShapeDtypeStruct((4, r, cols), BF16),
        in_specs=[spec], out_specs=spec,
        scratch_shapes=[pltpu.VMEM((4, r, cols), BF16), pltpu.VMEM((4, r, cols), BF16), pltpu.VMEM((4, r, cols), BF16),
                        pltpu.SemaphoreType.DMA((4,)), pltpu.SemaphoreType.DMA((4,)), pltpu.SemaphoreType.DMA((4,)),
                        pltpu.SemaphoreType.DMA((4,))],
        compiler_params=_params(),
    )(blocks)


_HBM = pl.BlockSpec(memory_space=pltpu.HBM)
_SEM = pl.BlockSpec(memory_space=pltpu.SEMAPHORE)
_EFFECT = pltpu.SideEffectType.DATAFLOW_SIDE_EFFECTING


def _start_copies(bufs, copies, n_copies, after, name):
    nb = len(bufs)

    def body(*refs):
        for cp in copies(refs[:nb], refs[nb + 1], refs[nb + 2]):
            cp.start()
        refs[-1][...] = jnp.zeros_like(refs[-1])

    out = pl.pallas_call(
        body, name=name,
        out_shape=(pltpu.SemaphoreType.DMA((n_copies,)), pltpu.SemaphoreType.DMA((n_copies,)),
                   *[pltpu.HBM(b.shape, b.dtype) for b in bufs], jax.ShapeDtypeStruct((8, 128), F32)),
        in_specs=(_HBM,) * nb + (pl.BlockSpec(memory_space=pl.ANY),),
        out_specs=(_SEM, _SEM) + (_HBM,) * nb + (pl.BlockSpec(memory_space=pltpu.VMEM),),
        input_output_aliases={i: 2 + i for i in range(nb)},
        compiler_params=pltpu.CompilerParams(has_side_effects=_EFFECT),
    )(*[pltpu.with_memory_space_constraint(b, pltpu.HBM) for b in bufs], after)
    return out[0], out[1], list(out[2:2 + nb]), out[-1]


def _wait_copies(flight, copies, after, name):
    send_sems, recv_sems, bufs, _ = flight
    nb = len(bufs)

    def body(*refs):
        for cp in copies(refs[:nb], refs[nb], refs[nb + 1]):
            cp.wait_send()
            cp.wait_recv()

    return pl.pallas_call(
        body, name=name,
        out_shape=tuple(pltpu.HBM(b.shape, b.dtype) for b in bufs),
        in_specs=(_HBM,) * nb + (_SEM, _SEM, pl.BlockSpec(memory_space=pl.ANY)), out_specs=(_HBM,) * nb,
        input_output_aliases={i: i for i in range(nb)},
        compiler_params=pltpu.CompilerParams(has_side_effects=_EFFECT),
    )(*bufs, send_sems, recv_sems, after)


def _chip_copies(refs, send_sems, recv_sems):
    pair_ref, land_ref = refs
    x, y, c = _place()
    chips = [(1 - x, y), (x, 1 - y), (1 - x, 1 - y)]
    return [pltpu.make_async_remote_copy(
        src_ref=pair_ref.at[2 * chip[0] + chip[1]], dst_ref=land_ref.at[k],
        send_sem=send_sems.at[k], recv_sem=recv_sems.at[k],
        device_id=(*chip, c), device_id_type=MESH) for k, chip in enumerate(chips)]


def _own_block_copies(targets):
    def copies(refs, send_sems, recv_sems):
        x, y, c = _place()
        mine = refs[0].at[4 * x + 2 * y + c]
        return [pltpu.make_async_remote_copy(
            src_ref=mine, dst_ref=mine, send_sem=send_sems.at[k], recv_sem=recv_sems.at[k],
            device_id=to, device_id_type=MESH) for k, to in enumerate(targets(x, y, c))]
    return copies


def _my_core_and_sibling(x, y, c):
    return [(x, y, 1 - c), (1 - x, y, c), (x, 1 - y, c), (1 - x, 1 - y, c)]


def _all_others(x, y, c):
    flip = lambda v, f: 1 - v if f else v
    return [(flip(x, r & 4), flip(y, r & 2), flip(c, r & 1)) for r in range(1, N_DEV)]


def _forward_copies(refs, send_sems, recv_sems):
    x, y, c = _place()
    chips = [(1 - x, y), (x, 1 - y), (1 - x, 1 - y)]
    return [pltpu.make_async_remote_copy(
        src_ref=refs[0].at[4 * chip[0] + 2 * chip[1] + c], dst_ref=refs[0].at[4 * chip[0] + 2 * chip[1] + c],
        send_sem=send_sems.at[k], recv_sem=recv_sems.at[k],
        device_id=(x, y, 1 - c), device_id_type=MESH) for k, chip in enumerate(chips)]


def _near_targets(x, y, c):
    return [(x, y, 1 - c), (1 - x, y, c), (x, 1 - y, c)]


def _second_stage_copies(refs, send_sems, recv_sems):
    x, y, c = _place()
    relayed = ((x + 1 - c) % 2, (y + c) % 2, c)
    relay_to = ((x + c) % 2, (y + 1 - c) % 2, c)
    plan = [((1 - x, y, c), (x, y, 1 - c)), ((x, 1 - y, c), (x, y, 1 - c)), (relayed, relay_to)]
    copies = []
    for k, ((px, py, pc), to) in enumerate(plan):
        blk = refs[0].at[4 * px + 2 * py + pc]
        copies.append(pltpu.make_async_remote_copy(
            src_ref=blk, dst_ref=blk, send_sem=send_sems.at[k], recv_sem=recv_sems.at[k],
            device_id=to, device_id_type=MESH))
    return copies


def _diagonal_forward_copies(refs, send_sems, recv_sems):
    x, y, c = _place()
    blk = refs[0].at[4 * (1 - x) + 2 * (1 - y) + c]
    return [pltpu.make_async_remote_copy(
        src_ref=blk, dst_ref=blk, send_sem=send_sems.at[0], recv_sem=recv_sems.at[0],
        device_id=(x, y, 1 - c), device_id_type=MESH)]


def _with_own_slot(block, me):
    return lax.dynamic_update_index_in_dim(lax.empty((N_DEV,) + block.shape, block.dtype), block, me, 0)


def _matmul(a, b, dims, out_dtype, tm, tn, name, dep=None):
    if dims == "nn":
        (m, k), n = a.shape, b.shape[1]
        a_spec = pl.BlockSpec((tm, k), lambda i, j: (i, 0))
        b_spec = pl.BlockSpec((k, tn), lambda i, j: (0, j))
        contract = ((1,), (0,))
    elif dims == "nt":
        (m, k), n = a.shape, b.shape[0]
        a_spec = pl.BlockSpec((tm, k), lambda i, j: (i, 0))
        b_spec = pl.BlockSpec((tn, k), lambda i, j: (j, 0))
        contract = ((1,), (1,))
    else:
        (k, m), n = a.shape, b.shape[1]
        a_spec = pl.BlockSpec((k, tm), lambda i, j: (0, i))
        b_spec = pl.BlockSpec((k, tn), lambda i, j: (0, j))
        contract = ((0,), (0,))
    assert m % tm == 0 and n % tn == 0 and a.dtype == BF16 and b.dtype == BF16

    def body(a_ref, b_ref, *rest):
        rest[-1][...] = lax.dot_general(a_ref[...], b_ref[...], (contract, ((), ())),
                                        preferred_element_type=F32).astype(out_dtype)

    deps = [] if dep is None else [dep]
    return pl.pallas_call(
        body, name=name, grid=(m // tm, n // tn),
        in_specs=[a_spec, b_spec] + [pl.BlockSpec((8, 128), lambda i, j: (0, 0))] * len(deps),
        out_specs=pl.BlockSpec((tm, tn), lambda i, j: (i, j)),
        out_shape=jax.ShapeDtypeStruct((m, n), out_dtype),
        compiler_params=_params(dimension_semantics=("arbitrary", "arbitrary")),
    )(a, b, *deps)


Z_TILE = 768
_Z_TILE_ORDER = ((0, 1, 2, 3, 4, 5, 6), (2, 0, 1, 6, 3, 4, 5), (4, 0, 5, 6, 1, 2, 3), (6, 2, 3, 4, 0, 1, 5))
_Z_EARLY_TILES = 4


def _z_proj(h, w_in_t, order, first, count, z_prev, name, dep=None):
    t = h.shape[0]

    def body(order_ref, h_ref, w_ref, *rest):
        rest[-1][...] = _dot_nt(h_ref[...], w_ref[...])

    prev = [] if z_prev is None else [z_prev]
    deps = [] if dep is None else [dep]
    return pl.pallas_call(
        body, name=name,
        grid_spec=pltpu.PrefetchScalarGridSpec(
            num_scalar_prefetch=1, grid=(count,),
            in_specs=[pl.BlockSpec((t, D_MODEL), lambda j, o: (0, 0)),
                      pl.BlockSpec((Z_TILE, D_MODEL), lambda j, o: (o[first + j], 0))]
            + [pl.BlockSpec(memory_space=pl.ANY)] * len(prev)
            + [pl.BlockSpec((8, 128), lambda j, o: (0, 0))] * len(deps),
            out_specs=pl.BlockSpec((t, Z_TILE), lambda j, o: (0, o[first + j]))),
        out_shape=jax.ShapeDtypeStruct((t, D_IN), F32),
        input_output_aliases={3: 0} if prev else {},
        compiler_params=_params(dimension_semantics=("arbitrary",)),
    )(order, h, w_in_t, *prev, *deps)


def _modulation(c_all, w_ada, b_ada_mine):
    def body(c_ref, w_ref, b_ref, act_ref, mod_ref):
        cv = c_ref[...]
        act = cv * _sigmoid(cv)
        act_ref[...] = act
        mod_ref[...] = jnp.dot(act.astype(BF16), w_ref[...].astype(BF16), preferred_element_type=F32) + b_ref[...]

    return pl.pallas_call(
        body, name="modulation",
        out_shape=(jax.ShapeDtypeStruct(c_all.shape, F32), jax.ShapeDtypeStruct((N_DEV, W_ADA_SHARD), F32)),
        compiler_params=_params(),
    )(c_all, w_ada, b_ada_mine)


def _modulated_norm(x, norm_g, scale, shift, tm=256):
    t, d = x.shape

    def body(x_ref, g_ref, sc_ref, sh_ref, h_ref):
        xv = x_ref[...]
        r = lax.rsqrt(jnp.mean(xv * xv, axis=-1, keepdims=True) + EPS)
        h = (xv * r) * g_ref[...] * (1.0 + sc_ref[...]) + sh_ref[...]
        h_ref[...] = h.astype(BF16)

    row = pl.BlockSpec((1, d), lambda i: (0, 0))
    return pl.pallas_call(
        body, name="modulated_norm", grid=(t // tm,),
        in_specs=[pl.BlockSpec((tm, d), lambda i: (i, 0)), row, row, row],
        out_specs=pl.BlockSpec((tm, d), lambda i: (i, 0)),
        out_shape=jax.ShapeDtypeStruct((t, d), BF16),
        compiler_params=_params(dimension_semantics=("arbitrary",)),
    )(x, norm_g, scale, shift)


def _window_bias(block_index):
    s = lax.broadcasted_iota(jnp.int32, (2 * BLOCK, BLOCK), 0)
    t = lax.broadcasted_iota(jnp.int32, (2 * BLOCK, BLOCK), 1)
    valid = ((s < BLOCK) & (s > t) & (block_index > 0)) | ((s >= BLOCK) & ((s - BLOCK) <= t))
    bias = jnp.where(valid, 0.0, -jnp.inf).astype(F32)
    return jnp.concatenate([bias] * 8, axis=1)


def _heads_t(pair_blocks, g):
    top = lax.broadcasted_iota(jnp.int32, (BLOCK, BLOCK), 0) < HEAD_DIM
    zeros = jnp.zeros((HEAD_DIM, BLOCK), F32)
    tiles = []
    for blk in pair_blocks:
        tp = blk.T
        if g == 0:
            tiles += [jnp.where(top, tp, 0.0), jnp.concatenate([tp[HEAD_DIM:], zeros], axis=0)]
        else:
            tiles += [jnp.concatenate([zeros, tp[:HEAD_DIM]], axis=0), jnp.where(top, 0.0, tp)]
    return jnp.concatenate(tiles, axis=1)


def _pair_block(xt, p, g):
    r0 = HEAD_DIM * g
    even = xt[r0:r0 + HEAD_DIM, (2 * p) * BLOCK:(2 * p + 1) * BLOCK]
    odd = xt[r0:r0 + HEAD_DIM, (2 * p + 1) * BLOCK:(2 * p + 2) * BLOCK]
    return jnp.concatenate([even, odd], axis=0).T


def _softmax_t(scores_t, bias, sink):
    st = scores_t * ATTN_SCALE + bias
    m = jnp.maximum(jnp.max(st, axis=0, keepdims=True), sink)
    e = jnp.exp(st - m)
    es = jnp.exp(sink - m)
    inv = 1.0 / (jnp.sum(e, axis=0, keepdims=True) + es)
    return e * inv, es * inv


def _dot(a, b):
    return jnp.dot(a, b, preferred_element_type=F32)


def _dot_nt(a, b):
    return lax.dot_general(a, b, (((1,), (1,)), ((), ())), preferred_element_type=F32)


def _layer_norm_fwd(v):
    mu = jnp.mean(v, axis=-1, keepdims=True)
    xc = v - mu
    rstd = lax.rsqrt(jnp.mean(xc * xc, axis=-1, keepdims=True) + EPS)
    return xc * rstd, rstd


def _tril(transposed=False):
    t = lax.broadcasted_iota(jnp.int32, (BLOCK, BLOCK), 0)
    s = lax.broadcasted_iota(jnp.int32, (BLOCK, BLOCK), 1)
    return s >= t if transposed else t >= s


def _const_spec(shape):
    return pl.BlockSpec(shape, lambda i: (0,) * len(shape))


def _kv_prev_spec(index):
    return pl.BlockSpec((BLOCK, 2 * D_KV), lambda i: (jnp.maximum(index(i) - 1, 0), SEG_KV // (2 * D_KV)))


def _keys_values(z_ref, kvp_ref):
    kvp, kvc = kvp_ref[...], z_ref[:, SEG_KV:SEG_KV + 2 * D_KV]
    kk = jnp.concatenate([kvp[:, :D_KV], kvc[:, :D_KV]], axis=0)
    vv = jnp.concatenate([kvp[:, D_KV:], kvc[:, D_KV:]], axis=0)
    return kk, vv


def _pair_cols(g, p, base=0):
    return slice(base + (4 * g + p) * 128, base + (4 * g + p + 1) * 128)


def _mixer_fwd(z, sink_rows, ln_g, ln_b, sgu_w, sgu_bt):
    t = z.shape[0]

    def body(z_ref, kvp_ref, sink_ref, lng_ref, lnb_ref, w_ref, bt_ref, a_ref):
        bias = _window_bias(pl.program_id(0))
        kk, vv = _keys_values(z_ref, kvp_ref)
        kk_b, vvt_b = kk.astype(BF16), vv.T.astype(BF16)
        for g in range(2):
            qt = _heads_t([z_ref[:, _pair_cols(g, p, SEG_Q)] for p in range(4)], g).astype(BF16)
            prob, _ = _softmax_t(_dot(kk_b, qt), bias, sink_ref[g])
            ot = _dot(vvt_b, prob.astype(BF16))
            for p in range(4):
                gate = z_ref[:, _pair_cols(g, p, SEG_GA)]
                a_ref[:, _pair_cols(g, p)] = (_pair_block(ot, p, g) * (gate * _sigmoid(gate))).astype(BF16)

        vhat, _ = _layer_norm_fwd(z_ref[:, SEG_VS:SEG_VS + D_SGU])
        vn = vhat * lng_ref[...] + lnb_ref[...]
        tril = _tril()
        for g in range(SGU_GROUPS):
            cols = slice(g * 128, (g + 1) * 128)
            wm = jnp.where(tril, w_ref[g], 0.0).astype(BF16)
            mixed = _dot(wm, vn[:, cols].astype(BF16)) + bt_ref[:, g:g + 1]
            gate = z_ref[:, SEG_GS + g * 128:SEG_GS + (g + 1) * 128]
            a_ref[:, D_ATTN + g * 128:D_ATTN + (g + 1) * 128] = (
                (z_ref[:, SEG_U + g * 128:SEG_U + (g + 1) * 128] * mixed) * (gate * _sigmoid(gate))).astype(BF16)

    return pl.pallas_call(
        body, name="mixer_fwd", grid=(t // BLOCK,),
        in_specs=[pl.BlockSpec((BLOCK, D_IN), lambda i: (i, 0)), _kv_prev_spec(lambda i: i),
                  _const_spec((2, 1, 8 * BLOCK)), _const_spec((1, D_SGU)), _const_spec((1, D_SGU)),
                  _const_spec((SGU_GROUPS, BLOCK, BLOCK)), _const_spec((BLOCK, SGU_GROUPS))],
        out_specs=pl.BlockSpec((BLOCK, D_MODEL), lambda i: (i, 0)),
        out_shape=jax.ShapeDtypeStruct((t, D_MODEL), BF16),
        compiler_params=_params(dimension_semantics=("arbitrary",)),
    )(z, z, sink_rows, ln_g, ln_b, sgu_w, sgu_bt)


def _mixer_bwd(z, da, sink_rows, ln_g, ln_b, sgu_w, sgu_wt, sgu_bt):
    t = z.shape[0]
    nb = t // BLOCK

    def body(z_ref, kvp_ref, da_ref, sink_ref, lng_ref, lnb_ref, w_ref, wt_ref, bt_ref,
             dz_ref, dsink_ref, dw_ref, db_ref, dlng_ref, dlnb_ref, carry_ref, dsink_acc, dbt_acc):
        step = pl.program_id(0)

        @pl.when(step == 0)
        def _():
            carry_ref[...] = jnp.zeros_like(carry_ref)
            dsink_acc[...] = jnp.zeros_like(dsink_acc)
            dbt_acc[...] = jnp.zeros_like(dbt_acc)
            dw_ref[...] = jnp.zeros_like(dw_ref)
            dlng_ref[...] = jnp.zeros_like(dlng_ref)
            dlnb_ref[...] = jnp.zeros_like(dlnb_ref)

        bias = _window_bias(nb - 1 - step)
        kk, vv = _keys_values(z_ref, kvp_ref)
        kk_b, vv_b = kk.astype(BF16), vv.astype(BF16)
        kkt_b, vvt_b = kk.T.astype(BF16), vv.T.astype(BF16)
        dkk = jnp.zeros((2 * BLOCK, D_KV), F32)
        dvv = jnp.zeros((2 * BLOCK, D_KV), F32)
        for g in range(2):
            qt = _heads_t([z_ref[:, _pair_cols(g, p, SEG_Q)] for p in range(4)], g).astype(BF16)
            prob, sink_prob = _softmax_t(_dot(kk_b, qt), bias, sink_ref[g])
            prob_b = prob.astype(BF16)
            ot = _dot(vvt_b, prob_b)
            gates = [z_ref[:, _pair_cols(g, p, SEG_GA)] for p in range(4)]
            sig = [_sigmoid(gt) for gt in gates]
            d_attn = [da_ref[:, _pair_cols(g, p)] for p in range(4)]
            d_ot = _heads_t([d_attn[p] * (gates[p] * sig[p]) for p in range(4)], g).astype(BF16)
            d_prob = _dot(vv_b, d_ot)
            delta = jnp.sum(prob * d_prob, axis=0, keepdims=True)
            d_scores = (prob * (d_prob - delta) * ATTN_SCALE).astype(BF16)
            dsink_acc[g] -= sink_prob * delta
            d_qt = _dot(kkt_b, d_scores)
            dkk = dkk + _dot_nt(d_scores, qt)
            dvv = dvv + _dot_nt(prob_b, d_ot)
            for p in range(4):
                dz_ref[:, _pair_cols(g, p, SEG_Q)] = _pair_block(d_qt, p, g).astype(BF16)
                d_silu = sig[p] * (1.0 + gates[p] * (1.0 - sig[p]))
                dz_ref[:, _pair_cols(g, p, SEG_GA)] = (d_attn[p] * _pair_block(ot, p, g) * d_silu).astype(BF16)
        d_kv = jnp.concatenate([dkk, dvv], axis=1)
        dz_ref[:, SEG_KV:SEG_KV + 2 * D_KV] = (d_kv[BLOCK:] + carry_ref[...]).astype(BF16)
        carry_ref[...] = d_kv[:BLOCK]

        vhat, rstd = _layer_norm_fwd(z_ref[:, SEG_VS:SEG_VS + D_SGU])
        lng = lng_ref[...]
        vn = vhat * lng + lnb_ref[...]
        tril, triu = _tril(), _tril(transposed=True)
        lane = lax.broadcasted_iota(jnp.int32, (BLOCK, 128), 1)
        d_bt = jnp.zeros((BLOCK, 128), F32)
        d_vn = []
        for g in range(SGU_GROUPS):
            cols = slice(g * 128, (g + 1) * 128)
            wm = jnp.where(tril, w_ref[g], 0.0).astype(BF16)
            wmt = jnp.where(triu, wt_ref[g], 0.0).astype(BF16)
            vn_g = vn[:, cols].astype(BF16)
            mixed = _dot(wm, vn_g) + bt_ref[:, g:g + 1]
            gate = z_ref[:, SEG_GS + g * 128:SEG_GS + (g + 1) * 128]
            u = z_ref[:, SEG_U + g * 128:SEG_U + (g + 1) * 128]
            d_out = da_ref[:, D_ATTN + g * 128:D_ATTN + (g + 1) * 128]
            sg = _sigmoid(gate)
            d_um = d_out * (gate * sg)
            dz_ref[:, SEG_U + g * 128:SEG_U + (g + 1) * 128] = (d_um * mixed).astype(BF16)
            dz_ref[:, SEG_GS + g * 128:SEG_GS + (g + 1) * 128] = (
                d_out * (u * mixed) * (sg * (1.0 + gate * (1.0 - sg)))).astype(BF16)
            d_mixed = d_um * u
            d_mixed_b = d_mixed.astype(BF16)
            dw_ref[g] += jnp.where(tril, _dot_nt(d_mixed_b, vn_g), 0.0)
            d_bt = d_bt + jnp.where(lane == g, jnp.sum(d_mixed, axis=-1, keepdims=True), 0.0)
            d_vn.append(_dot(wmt, d_mixed_b))
        dbt_acc[...] += d_bt
        d_vn = jnp.concatenate(d_vn, axis=1)
        dlng_ref[...] += jnp.sum(d_vn * vhat, axis=0, keepdims=True)
        dlnb_ref[...] += jnp.sum(d_vn, axis=0, keepdims=True)
        d_vhat = d_vn * lng
        d_v = rstd * (d_vhat - jnp.mean(d_vhat, axis=-1, keepdims=True)
                      - vhat * jnp.mean(d_vhat * vhat, axis=-1, keepdims=True))
        dz_ref[:, SEG_VS:SEG_VS + D_SGU] = d_v.astype(BF16)

        @pl.when(step == nb - 1)
        def _():
            db_ref[...] = dbt_acc[...].T[:SGU_GROUPS]
            lane_row = lax.broadcasted_iota(jnp.int32, (1, 128), 1)
            d_sink = jnp.zeros((1, 128), F32)
            for g in range(2):
                acc = dsink_acc[g]
                for j in range(8):
                    head_sum = jnp.sum(acc[:, j * BLOCK:(j + 1) * BLOCK], axis=-1, keepdims=True)
                    d_sink = d_sink + jnp.where(lane_row == 8 * g + j, head_sum, 0.0)
            dsink_ref[...] = d_sink

    rev = lambda i: nb - 1 - i
    return pl.pallas_call(
        body, name="mixer_bwd", grid=(nb,),
        in_specs=[pl.BlockSpec((BLOCK, D_IN), lambda i: (rev(i), 0)), _kv_prev_spec(rev),
                  pl.BlockSpec((BLOCK, D_MODEL), lambda i: (rev(i), 0)),
                  _const_spec((2, 1, 8 * BLOCK)), _const_spec((1, D_SGU)), _const_spec((1, D_SGU)),
                  _const_spec((SGU_GROUPS, BLOCK, BLOCK)), _const_spec((SGU_GROUPS, BLOCK, BLOCK)),
                  _const_spec((BLOCK, SGU_GROUPS))],
        out_specs=(pl.BlockSpec((BLOCK, D_IN), lambda i: (rev(i), 0)), _const_spec((1, 128)),
                   _const_spec((SGU_GROUPS, BLOCK, BLOCK)), _const_spec((SGU_GROUPS, BLOCK)),
                   _const_spec((1, D_SGU)), _const_spec((1, D_SGU))),
        out_shape=(jax.ShapeDtypeStruct((t, D_IN), BF16), jax.ShapeDtypeStruct((1, 128), F32),
                   jax.ShapeDtypeStruct((SGU_GROUPS, BLOCK, BLOCK), F32), jax.ShapeDtypeStruct((SGU_GROUPS, BLOCK), F32),
                   jax.ShapeDtypeStruct((1, D_SGU), F32), jax.ShapeDtypeStruct((1, D_SGU), F32)),
        scratch_shapes=[pltpu.VMEM((BLOCK, 2 * D_KV), F32), pltpu.VMEM((2, 1, 8 * BLOCK), F32),
                        pltpu.VMEM((BLOCK, 128), F32)],
        compiler_params=_params(dimension_semantics=("arbitrary",)),
    )(z, z, da, sink_rows, ln_g, ln_b, sgu_w, sgu_wt, sgu_bt)


def _head(y, x, target, gate, final_g, tm=256):
    t, d = x.shape

    def body(y_ref, x_ref, tg_ref, gate_ref, fg_ref, dx2_ref, dy_ref, loss_ref, dfg_ref, dgate_ref):
        @pl.when(pl.program_id(0) == 0)
        def _():
            loss_ref[...] = jnp.zeros_like(loss_ref)
            dfg_ref[...] = jnp.zeros_like(dfg_ref)
            dgate_ref[...] = jnp.zeros_like(dgate_ref)

        yv, gate, fg = y_ref[...], gate_ref[...], fg_ref[...]
        x2 = x_ref[...] + gate * yv
        r2 = lax.rsqrt(jnp.mean(x2 * x2, axis=-1, keepdims=True) + EPS)
        nrm = x2 * r2
        err = nrm * fg - tg_ref[...]
        loss_ref[...] += 0.5 * jnp.sum(jnp.mean(err * err, axis=-1, keepdims=True), axis=0, keepdims=True)
        d_out = err * (1.0 / d)
        dfg_ref[...] += jnp.sum(d_out * nrm, axis=0, keepdims=True)
        d_nrm = d_out * fg
        dx2 = r2 * (d_nrm - nrm * jnp.mean(d_nrm * nrm, axis=-1, keepdims=True))
        dx2_ref[...] = dx2
        dgate_ref[...] += jnp.sum(dx2 * yv, axis=0, keepdims=True)
        dy_ref[...] = (dx2 * gate).astype(BF16)

    blk = pl.BlockSpec((tm, d), lambda i: (i, 0))
    row = _const_spec((1, d))
    return pl.pallas_call(
        body, name="head", grid=(t // tm,),
        in_specs=[blk, blk, blk, row, row],
        out_specs=(blk, blk, _const_spec((1, 128)), row, row),
        out_shape=(jax.ShapeDtypeStruct((t, d), F32), jax.ShapeDtypeStruct((t, d), BF16),
                   jax.ShapeDtypeStruct((1, 128), F32), jax.ShapeDtypeStruct((1, d), F32),
                   jax.ShapeDtypeStruct((1, d), F32)),
        compiler_params=_params(dimension_semantics=("arbitrary",)),
    )(y, x, target, gate, final_g)


def _modulated_norm_bwd(dh, x, dx2, norm_g, scale, tm=256):
    t, d = x.shape

    def body(dh_ref, x_ref, dx2_ref, g_ref, sc_ref, gx_ref, dshift_ref, dscale_ref, dg_ref):
        @pl.when(pl.program_id(0) == 0)
        def _():
            dshift_ref[...] = jnp.zeros_like(dshift_ref)
            dscale_ref[...] = jnp.zeros_like(dscale_ref)
            dg_ref[...] = jnp.zeros_like(dg_ref)

        dh, xv, g = dh_ref[...], x_ref[...], g_ref[...]
        one_plus = 1.0 + sc_ref[...]
        r = lax.rsqrt(jnp.mean(xv * xv, axis=-1, keepdims=True) + EPS)
        xn = xv * r
        dshift_ref[...] += jnp.sum(dh, axis=0, keepdims=True)
        dscale_ref[...] += jnp.sum(dh * (xn * g), axis=0, keepdims=True)
        d_y = dh * one_plus
        dg_ref[...] += jnp.sum(d_y * xn, axis=0, keepdims=True)
        d_xn = d_y * g
        gx_ref[...] = dx2_ref[...] + r * (d_xn - xn * jnp.mean(d_xn * xn, axis=-1, keepdims=True))

    blk = pl.BlockSpec((tm, d), lambda i: (i, 0))
    row = _const_spec((1, d))
    return pl.pallas_call(
        body, name="modulated_norm_bwd", grid=(t // tm,),
        in_specs=[blk, blk, blk, row, row], out_specs=(blk, row, row, row),
        out_shape=(jax.ShapeDtypeStruct((t, d), F32),) + (jax.ShapeDtypeStruct((1, d), F32),) * 3,
        compiler_params=_params(dimension_semantics=("arbitrary",)),
    )(dh, x, dx2, norm_g, scale)


def _adamw(w, g, m, v):
    m = ADAM_B1 * m + (1.0 - ADAM_B1) * g
    v = ADAM_B2 * v + (1.0 - ADAM_B2) * (g * g)
    m_hat = m / (1.0 - ADAM_B1 ** ADAM_STEP)
    v_hat = v / (1.0 - ADAM_B2 ** ADAM_STEP)
    delta = -ADAM_LR * (m_hat / (jnp.sqrt(v_hat) + ADAM_EPS) + ADAM_WD * w)
    return delta, m, v


def _adam_from_chips(chip, pair, land, w, m, v, name, tc):
    _, r, c = pair.shape

    def body(chip_ref, own_ref, land_ref, w_ref, m_ref, v_ref, g_ref, d_ref, nm_ref, nv_ref):
        g = own_ref[...].astype(F32)
        for k in range(3):
            g = g + land_ref[k].astype(F32)
        g_ref[...] = g
        d_ref[...], nm_ref[...], nv_ref[...] = _adamw(w_ref[...], g, m_ref[...], v_ref[...])

    blk = pl.BlockSpec((r, tc), lambda i, chip_ref: (0, i))
    return pl.pallas_call(
        body, name=name,
        grid_spec=pltpu.PrefetchScalarGridSpec(
            num_scalar_prefetch=1, grid=(c // tc,),
            in_specs=[pl.BlockSpec((None, r, tc), lambda i, chip_ref: (chip_ref[0], 0, i)),
                      pl.BlockSpec((3, r, tc), lambda i, chip_ref: (0, 0, i)), blk, blk, blk],
            out_specs=(blk,) * 4),
        out_shape=(jax.ShapeDtypeStruct((r, c), F32),) * 4,
        compiler_params=_params(dimension_semantics=("arbitrary",)),
    )(chip, pair, land, w, m, v)


def _adam_w_ada(act_t, dmod_mine, w, m, v, tr=256):
    r, c = w.shape

    def body(a_ref, dm_ref, w_ref, m_ref, v_ref, g_ref, d_ref, nm_ref, nv_ref):
        g = _dot(a_ref[...].astype(BF16), dm_ref[...].astype(BF16))
        g_ref[...] = g
        d_ref[...], nm_ref[...], nv_ref[...] = _adamw(w_ref[...], g, m_ref[...], v_ref[...])

    blk = pl.BlockSpec((tr, c), lambda i: (i, 0))
    return pl.pallas_call(
        body, name="adam_w_ada", grid=(r // tr,),
        in_specs=[pl.BlockSpec((tr, N_DEV), lambda i: (i, 0)), _const_spec((N_DEV, c)), blk, blk, blk],
        out_specs=(blk,) * 4, out_shape=(jax.ShapeDtypeStruct((r, c), F32),) * 4,
        compiler_params=_params(dimension_semantics=("arbitrary",)),
    )(act_t, dmod_mine, w, m, v)


def _pack_small(d_shift, d_scale, d_gate, d_norm_g, d_final_g, d_ln_g, d_ln_b, loss, d_sinks, d_sgu_b):
    def body(shift_ref, scale_ref, gate_ref, ng_ref, fg_ref, lng_ref, lnb_ref, loss_ref, sink_ref, b_ref, o_ref):
        o_ref[...] = jnp.zeros_like(o_ref)
        o_ref[ROW_SHIFT:ROW_SHIFT + 1, :] = shift_ref[...]
        o_ref[ROW_SCALE:ROW_SCALE + 1, :] = scale_ref[...]
        o_ref[ROW_GATE:ROW_GATE + 1, :] = gate_ref[...]
        o_ref[ROW_NORM_G:ROW_NORM_G + 1, :] = ng_ref[...]
        o_ref[ROW_FINAL_G:ROW_FINAL_G + 1, :] = fg_ref[...]
        o_ref[ROW_LN:ROW_LN + 1, 0:D_SGU] = lng_ref[...]
        o_ref[ROW_LN:ROW_LN + 1, D_SGU:2 * D_SGU] = lnb_ref[...]
        o_ref[ROW_MISC:ROW_MISC + 1, 0:128] = loss_ref[...]
        o_ref[ROW_MISC:ROW_MISC + 1, 128:256] = sink_ref[...]
        o_ref[ROW_SGU_B:ROW_SGU_B + SGU_GROUPS, 0:BLOCK] = b_ref[...]

    return pl.pallas_call(
        body, name="pack_small", out_shape=jax.ShapeDtypeStruct((SMALL_ROWS, D_MODEL), F32),
        compiler_params=_params(),
    )(d_shift, d_scale, d_gate, d_norm_g, d_final_g, d_ln_g, d_ln_b, loss, d_sinks, d_sgu_b)


_SMALL_NAMES = ("norm_g", "b_ada", "attn_sinks", "sgu_ln_g", "sgu_ln_b", "sgu_w", "sgu_b", "final_g")


def _adam_small(partials, d_sgu_w_all, weights, moments_m, moments_v):
    names = _SMALL_NAMES
    k = len(names)

    def body(*refs):
        p_ref, sw_ref = refs[0], refs[1]
        w_refs, m_refs, v_refs = refs[2:2 + k], refs[2 + k:2 + 2 * k], refs[2 + 2 * k:2 + 3 * k]
        loss_ref, dmod_ref = refs[2 + 3 * k], refs[3 + 3 * k]
        out_refs = refs[4 + 3 * k:4 + 7 * k]
        sum_ref = refs[4 + 7 * k]
        total = p_ref[0]
        for j in range(1, N_DEV):
            total = total + p_ref[j]
        sum_ref[...] = total
        for j in range(N_DEV):
            for part, row in enumerate((ROW_SHIFT, ROW_SCALE, ROW_GATE)):
                dmod_ref[j:j + 1, part * D_MODEL:(part + 1) * D_MODEL] = p_ref[j, row:row + 1, :]
        loss_ref[...] = sum_ref[ROW_MISC:ROW_MISC + 1, 0:1]
        d_sgu_w = sw_ref[0]
        for j in range(1, N_DEV):
            d_sgu_w = d_sgu_w + sw_ref[j]
        grads = {
            "norm_g": sum_ref[ROW_NORM_G:ROW_NORM_G + 1, :],
            "b_ada": jnp.concatenate([sum_ref[r:r + 1, :] for r in (ROW_SHIFT, ROW_SCALE, ROW_GATE)], axis=1),
            "attn_sinks": sum_ref[ROW_MISC:ROW_MISC + 1, 128:128 + N_Q_HEADS],
            "sgu_ln_g": sum_ref[ROW_LN:ROW_LN + 1, 0:D_SGU],
            "sgu_ln_b": sum_ref[ROW_LN:ROW_LN + 1, D_SGU:2 * D_SGU],
            "sgu_w": d_sgu_w[None],
            "sgu_b": sum_ref[ROW_SGU_B:ROW_SGU_B + SGU_GROUPS, 0:BLOCK][None],
            "final_g": sum_ref[ROW_FINAL_G:ROW_FINAL_G + 1, :],
        }
        for i, name in enumerate(names):
            g = grads[name]
            delta, m, v = _adamw(w_refs[i][...], g, m_refs[i][...], v_refs[i][...])
            out_refs[4 * i][...] = g
            out_refs[4 * i + 1][...] = delta
            out_refs[4 * i + 2][...] = m
            out_refs[4 * i + 3][...] = v

    shapes = [jax.ShapeDtypeStruct((1, 1), F32), jax.ShapeDtypeStruct((N_DEV, 3 * D_MODEL), F32)]
    for name in names:
        shapes += [jax.ShapeDtypeStruct(weights[name].shape, F32)] * 4
    outs = pl.pallas_call(
        body, name="adam_small", out_shape=tuple(shapes),
        scratch_shapes=[pltpu.VMEM((SMALL_ROWS, D_MODEL), F32)],
        compiler_params=_params(),
    )(partials, d_sgu_w_all, *[weights[n] for n in names], *[moments_m[n] for n in names],
      *[moments_v[n] for n in names])
    return outs[0], outs[1], {name: outs[2 + 4 * i:6 + 4 * i] for i, name in enumerate(names)}


def kernel(x, c, norm_g, w_ada, b_ada, w_in, attn_sinks, sgu_ln_g, sgu_ln_b, sgu_w, sgu_b, w_out, final_g, loss_target, m_norm_g, m_w_ada, m_b_ada, m_w_in, m_attn_sinks, m_sgu_ln_g, m_sgu_ln_b, m_sgu_w, m_sgu_b, m_w_out, m_final_g, v_norm_g, v_w_ada, v_b_ada, v_w_in, v_attn_sinks, v_sgu_ln_g, v_sgu_ln_b, v_sgu_w, v_sgu_b, v_w_out, v_final_g):
    xi, yi, ci = _place()
    me = 4 * xi + 2 * yi + ci
    x2d, target = x[0], loss_target[0]
    t = x2d.shape[0]

    core = ci.astype(jnp.int32).reshape(1)
    chip = (2 * xi + yi).astype(jnp.int32).reshape(1)

    c_all = _all_gather([c.reshape(8, 256)], "gather_c", True)[0].reshape(N_DEV, D_MODEL)
    b_mine = lax.dynamic_slice(b_ada, (0, me * W_ADA_SHARD), (1, W_ADA_SHARD))
    c_act, mod_part = _modulation(c_all, w_ada[0], b_mine)
    mod_all = _all_gather([mod_part], "gather_mod", True)[0]

    near = _own_block_copies(_near_targets)
    w_in_flight = _start_copies([_with_own_slot(w_in[0].T.astype(BF16), me)], near, 3, mod_all, "gather_w_in_start")
    mod = lax.dynamic_index_in_dim(mod_all, me, axis=1, keepdims=False).reshape(1, 3 * D_MODEL)
    mod = mod + w_in_flight[3][0, 0]
    shift, scale, gate = mod[:, :D_MODEL], mod[:, D_MODEL:2 * D_MODEL], mod[:, 2 * D_MODEL:]
    h = _modulated_norm(x2d, norm_g, scale, shift)

    w_in_near = _wait_copies(w_in_flight, near, h, "gather_w_in_wait")
    w_in_flight = _start_copies(w_in_near, _second_stage_copies, 3, core, "gather_w_in_second_start")
    w_out_flight = _start_copies([_with_own_slot(w_out[0].astype(BF16), me)], _own_block_copies(_my_core_and_sibling),
                                 4, w_in_flight[3], "gather_w_out_start")
    tile_order = jnp.asarray(_Z_TILE_ORDER, jnp.int32)[chip[0]]
    z_own = _z_proj(h, w_in_flight[2][0].reshape(D_IN, D_MODEL), tile_order, 0, 1, None, "z_proj_own",
                    dep=w_out_flight[3])
    w_in_most = _wait_copies(w_in_flight, lambda *a: _second_stage_copies(*a)[:2], z_own, "gather_w_in_forward_wait")
    z_early = _z_proj(h, w_in_most[0].reshape(D_IN, D_MODEL), tile_order, 1, _Z_EARLY_TILES - 1, z_own, "z_proj_early")
    w_in_diag = _wait_copies((w_in_flight[0], w_in_flight[1], w_in_most, None),
                             lambda *a: _second_stage_copies(*a)[2:], z_early, "gather_w_in_diagonal_wait")
    w_in_flight = _start_copies(w_in_diag, _diagonal_forward_copies, 1, core, "gather_w_in_last_start")
    w_in_all = _wait_copies(w_in_flight, _diagonal_forward_copies, z_early, "gather_w_in_last_wait")[0]
    w_in_t = w_in_all.reshape(D_IN, D_MODEL)
    z = _z_proj(h, w_in_t, tile_order, _Z_EARLY_TILES, 7 - _Z_EARLY_TILES, z_early, "z_proj_late")
    w_out_half = _wait_copies(w_out_flight, _own_block_copies(_my_core_and_sibling), z, "gather_w_out_wait")
    w_out_flight = _start_copies(w_out_half, _forward_copies, 3, z, "gather_w_out_forward_start")
    sink_rows = jnp.repeat(attn_sinks.reshape(N_Q_HEADS), BLOCK).reshape(2, 1, 8 * BLOCK)
    sgu_bt = sgu_b[0].T
    a = _mixer_fwd(z, sink_rows + w_out_flight[3][0, 0], sgu_ln_g, sgu_ln_b, sgu_w[0], sgu_bt)
    w_out_all = _wait_copies(w_out_flight, _forward_copies, a, "gather_w_out_forward_wait")[0]
    w_out_full = w_out_all.reshape(D_MODEL, D_MODEL)
    y = _matmul(a, w_out_full, "nn", F32, min(t, 1024), 1024, "out_proj")
    final_g_row = final_g.reshape(1, D_MODEL)
    dx2, dy, loss_part, d_final_g, d_gate = _head(y, x2d, target, gate, final_g_row)

    da = _matmul(dy, w_out_full, "nt", F32, min(t, 1024), 1024, "out_proj_bwd")
    dw_out = _matmul(a, dy, "tn", BF16, 1024, 1024, "w_out_grad").reshape(4, 2, W_OUT_SHARD, D_MODEL)
    pair_out = _pair_reduce(dw_out, "w_out_grad_pair_reduce", W_OUT_SHARD // 2)
    out_flight = _start_copies([pair_out, lax.empty((3, W_OUT_SHARD, D_MODEL), BF16)], _chip_copies, 3, core,
                               "w_out_grad_chip_start")
    dz, d_sinks, d_sgu_w, d_sgu_b, d_ln_g, d_ln_b = _mixer_bwd(
        z, da, sink_rows + out_flight[3][0, 0], sgu_ln_g, sgu_ln_b, sgu_w[0], jnp.swapaxes(sgu_w[0], 1, 2), sgu_bt)
    sgu_w_flight = _start_copies([_with_own_slot(d_sgu_w, me)], _own_block_copies(_all_others), N_DEV - 1, core,
                                 "sgu_w_grad_gather_start")
    dw_in_t = _matmul(dz, h, "tn", BF16, 768, D_MODEL, "w_in_grad", dep=sgu_w_flight[3])
    dw_in_t = dw_in_t.reshape(4, 2, W_IN_SHARD, D_MODEL)
    pair_in = _pair_reduce(dw_in_t, "w_in_grad_pair_reduce", W_IN_SHARD // 3)
    in_flight = _start_copies([pair_in, lax.empty((3, W_IN_SHARD, D_MODEL), BF16)], _chip_copies, 3, core,
                              "w_in_grad_chip_start")
    dh = _matmul(dz, w_in_t, "nn", F32, min(t, 1024), 512, "z_proj_bwd", dep=in_flight[3])
    grad_x, d_shift, d_scale, d_norm_g = _modulated_norm_bwd(dh, x2d, dx2, norm_g, scale)

    partial = _pack_small(d_shift, d_scale, d_gate, d_norm_g, d_final_g, d_ln_g, d_ln_b, loss_part, d_sinks, d_sgu_b)
    partial_all = _all_gather([partial], "gather_small", True)[0]
    d_sgu_w_all = _wait_copies(sgu_w_flight, _own_block_copies(_all_others), partial_all, "sgu_w_grad_gather_wait")[0]
    weights = {"norm_g": norm_g, "b_ada": b_ada, "attn_sinks": attn_sinks, "sgu_ln_g": sgu_ln_g,
               "sgu_ln_b": sgu_ln_b, "sgu_w": sgu_w, "sgu_b": sgu_b, "final_g": final_g_row}
    moments_m = {"norm_g": m_norm_g, "b_ada": m_b_ada, "attn_sinks": m_attn_sinks, "sgu_ln_g": m_sgu_ln_g,
                 "sgu_ln_b": m_sgu_ln_b, "sgu_w": m_sgu_w, "sgu_b": m_sgu_b,
                 "final_g": m_final_g.reshape(1, D_MODEL)}
    moments_v = {"norm_g": v_norm_g, "b_ada": v_b_ada, "attn_sinks": v_attn_sinks, "sgu_ln_g": v_sgu_ln_g,
                 "sgu_ln_b": v_sgu_ln_b, "sgu_w": v_sgu_w, "sgu_b": v_sgu_b,
                 "final_g": v_final_g.reshape(1, D_MODEL)}
    loss, dmod_all, small = _adam_small(partial_all, d_sgu_w_all, weights, moments_m, moments_v)
    small["final_g"] = tuple(o.reshape(D_MODEL) for o in small["final_g"])

    dmod_mine = lax.dynamic_slice(dmod_all, (0, me * W_ADA_SHARD), (N_DEV, W_ADA_SHARD))
    big = {"w_ada": _adam_w_ada(c_act.T, dmod_mine, w_ada[0], m_w_ada[0], v_w_ada[0])}
    pair_out, land_out = _wait_copies(out_flight, _chip_copies, big["w_ada"][0], "w_out_grad_chip_wait")
    big["w_out"] = _adam_from_chips(chip, pair_out, land_out, w_out[0], m_w_out[0], v_w_out[0], "adam_w_out", 1024)
    pair_in, land_in = _wait_copies(in_flight, _chip_copies, big["w_out"][0], "w_in_grad_chip_wait")
    big["w_in"] = tuple(o.T for o in _adam_from_chips(
        chip, pair_in, land_in, w_in[0].T, m_w_in[0].T, v_w_in[0].T, "adam_w_in", 256))
    order = ["norm_g", "w_ada", "b_ada", "w_in", "attn_sinks", "sgu_ln_g", "sgu_ln_b", "sgu_w", "sgu_b", "w_out",
             "final_g"]
    outs = [loss.reshape(()), grad_x[None]]
    for k in range(4):
        for name in order:
            outs.append(big[name][k][None] if name in big else small[name][k])
    return tuple(outs)
```

```python
import jax
import jax.numpy as jnp
from jax import lax
from jax.experimental import pallas as pl
from jax.experimental.pallas import tpu as pltpu

F32 = jnp.float32
BF16 = jnp.bfloat16
MESH = pl.DeviceIdType.MESH

N_DEV = 8
D_MODEL = 2048
HEAD_DIM = 64
D_ATTN = 1024
N_Q_HEADS = 16
D_KV = 128
BLOCK = 128
D_SGU = 1024
SGU_GROUPS = 8
D_IN = 5376
W_IN_SHARD = D_IN // N_DEV
W_OUT_SHARD = D_MODEL // N_DEV
W_ADA_SHARD = 3 * D_MODEL // N_DEV
EPS = 1e-6
ATTN_SCALE = 0.125

ADAM_LR = 0.001
ADAM_B1 = 0.9
ADAM_B2 = 0.999
ADAM_EPS = 1e-08
ADAM_WD = 0.01
ADAM_STEP = 10

SEG_Q, SEG_KV, SEG_GA, SEG_U, SEG_VS, SEG_GS = 0, 1024, 1280, 2304, 3328, 4352

VMEM_LIMIT = 56 * 1024 * 1024

ROW_SHIFT, ROW_SCALE, ROW_GATE, ROW_NORM_G, ROW_FINAL_G, ROW_LN, ROW_MISC, ROW_SGU_B = 0, 1, 2, 3, 4, 5, 6, 8
SMALL_ROWS = 16


def _params(**kw):
    return pltpu.CompilerParams(vmem_limit_bytes=VMEM_LIMIT, **kw)


def _sigmoid(x):
    return 0.5 * (jnp.tanh(0.5 * x) + 1.0)


def _place():
    return lax.axis_index("x"), lax.axis_index("y"), lax.axis_index("c")


def _all_gather(shards, name, in_vmem):
    n = len(shards)

    def body(*refs):
        ins, outs = refs[:n], refs[n:2 * n]
        send_sems, recv_sems, local_sems = refs[2 * n:]
        x, y, c = _place()
        me, sibling = (x, y, c), (x, y, 1 - c)
        chips = [(1 - x, y), (x, 1 - y), (1 - x, 1 - y)]
        first, passed, mine = [], [], []
        for a in range(n):
            out_ref = outs[a]

            def slot(px, py, pc, out_ref=out_ref):
                return out_ref.at[4 * px + 2 * py + pc]

            def copy(k, block, to, src=None, a=a, slot=slot):
                return pltpu.make_async_remote_copy(
                    src_ref=slot(*block) if src is None else src, dst_ref=slot(*block),
                    send_sem=send_sems.at[a, k], recv_sem=recv_sems.at[a, k],
                    device_id=to, device_id_type=MESH)

            own = pltpu.make_async_copy(ins[a], slot(*me), local_sems.at[a])
            own.start()
            mine.append(own)
            mine_out = [copy(0, me, sibling, src=ins[a])]
            mine_out += [copy(1 + j, me, (*chip, c), src=ins[a]) for j, chip in enumerate(chips)]
            for cp in mine_out:
                cp.start()
            first += mine_out
            passed.append([copy(4 + j, (*chip, c), sibling) for j, chip in enumerate(chips)])
        for j, chip in enumerate(chips):
            for a in range(n):
                out_ref = outs[a]
                blk = out_ref.at[4 * chip[0] + 2 * chip[1] + c]
                pltpu.make_async_remote_copy(
                    src_ref=blk, dst_ref=blk, send_sem=send_sems.at[a, 1 + j], recv_sem=recv_sems.at[a, 1 + j],
                    device_id=me, device_id_type=MESH).wait_recv()
                passed[a][j].start()
        for a in range(n):
            out_ref = outs[a]
            blk = out_ref.at[4 * x + 2 * y + (1 - c)]
            pltpu.make_async_remote_copy(
                src_ref=blk, dst_ref=blk, send_sem=send_sems.at[a, 0], recv_sem=recv_sems.at[a, 0],
                device_id=me, device_id_type=MESH).wait_recv()
            for j, chip in enumerate(chips):
                blk = out_ref.at[4 * chip[0] + 2 * chip[1] + (1 - c)]
                pltpu.make_async_remote_copy(
                    src_ref=blk, dst_ref=blk, send_sem=send_sems.at[a, 4 + j], recv_sem=recv_sems.at[a, 4 + j],
                    device_id=me, device_id_type=MESH).wait_recv()
        for cp in first:
            cp.wait_send()
        for a in range(n):
            for cp in passed[a]:
                cp.wait_send()
        for cp in mine:
            cp.wait()

    space = pltpu.VMEM if in_vmem else pl.ANY
    spec = pl.BlockSpec(memory_space=space)
    return pl.pallas_call(
        body, name=name,
        out_shape=tuple(jax.ShapeDtypeStruct((N_DEV,) + s.shape, s.dtype) for s in shards),
        in_specs=[spec] * n, out_specs=tuple([spec] * n),
        scratch_shapes=[pltpu.SemaphoreType.DMA((n, 7)), pltpu.SemaphoreType.DMA((n, 7)),
                        pltpu.SemaphoreType.DMA((n,))],
        compiler_params=_params(),
    )(*shards)


def _pair_reduce(blocks, name, row_chunk):
    _, _, r, cols = blocks.shape
    assert r % row_chunk == 0

    def body(in_ref, out_ref, land, own, summed, send_sems, recv_sems, own_sems, out_sems):
        x, y, c = _place()
        sends, loads, stores = [], [], []
        for m in range(4):
            cp = pltpu.make_async_remote_copy(
                src_ref=in_ref.at[m, 1 - c], dst_ref=land.at[m], send_sem=send_sems.at[m], recv_sem=recv_sems.at[m],
                device_id=(x, y, 1 - c), device_id_type=MESH)
            cp.start()
            sends.append(cp)
            ld = pltpu.make_async_copy(in_ref.at[m, c], own.at[m], own_sems.at[m])
            ld.start()
            loads.append(ld)
        for m in range(4):
            sends[m].wait_recv()
            loads[m].wait()
            for k in range(r // row_chunk):
                rows = slice(k * row_chunk, (k + 1) * row_chunk)
                summed[m, rows, :] = (own[m, rows, :].astype(F32) + land[m, rows, :].astype(F32)).astype(BF16)
            st = pltpu.make_async_copy(summed.at[m], out_ref.at[m], out_sems.at[m])
            st.start()
            stores.append(st)
        for m in range(4):
            sends[m].wait_send()
            stores[m].wait()

    spec = pl.BlockSpec(memory_space=pl.ANY)
    return pl.pallas_call(
        body, name=name, out_shape=jax.ShapeDtypeStruct((4, r, cols), BF16),
        in_specs=[spec], out_specs=spec,
        scratch_shapes=[pltpu.VMEM((4, r, cols), BF16), pltpu.VMEM((4, r, cols), BF16), pltpu.VMEM((4, r, cols), BF16),
                        pltpu.SemaphoreType.DMA((4,)), pltpu.SemaphoreType.DMA((4,)), pltpu.SemaphoreType.DMA((4,)),
                        pltpu.SemaphoreType.DMA((4,))],
        compiler_params=_params(),
    )(blocks)


_HBM = pl.BlockSpec(memory_space=pltpu.HBM)
_SEM = pl.BlockSpec(memory_space=pltpu.SEMAPHORE)
_EFFECT = pltpu.SideEffectType.DATAFLOW_SIDE_EFFECTING


def _start_copies(bufs, copies, n_copies, after, name):
    nb = len(bufs)

    def body(*refs):
        for cp in copies(refs[:nb], refs[nb + 1], refs[nb + 2]):
            cp.start()
        refs[-1][...] = jnp.zeros_like(refs[-1])

    out = pl.pallas_call(
        body, name=name,
        out_shape=(pltpu.SemaphoreType.DMA((n_copies,)), pltpu.SemaphoreType.DMA((n_copies,)),
                   *[pltpu.HBM(b.shape, b.dtype) for b in bufs], jax.ShapeDtypeStruct((8, 128), F32)),
        in_specs=(_HBM,) * nb + (pl.BlockSpec(memory_space=pl.ANY),),
        out_specs=(_SEM, _SEM) + (_HBM,) * nb + (pl.BlockSpec(memory_space=pltpu.VMEM),),
        input_output_aliases={i: 2 + i for i in range(nb)},
        compiler_params=pltpu.CompilerParams(has_side_effects=_EFFECT),
    )(*[pltpu.with_memory_space_constraint(b, pltpu.HBM) for b in bufs], after)
    return out[0], out[1], list(out[2:2 + nb]), out[-1]


def _wait_copies(flight, copies, after, name):
    send_sems, recv_sems, bufs, _ = flight
    nb = len(bufs)

    def body(*refs):
        for cp in copies(refs[:nb], refs[nb], refs[nb + 1]):
            cp.wait_send()
            cp.wait_recv()

    return pl.pallas_call(
        body, name=name,
        out_shape=tuple(pltpu.HBM(b.shape, b.dtype) for b in bufs),
        in_specs=(_HBM,) * nb + (_SEM, _SEM, pl.BlockSpec(memory_space=pl.ANY)), out_specs=(_HBM,) * nb,
        input_output_aliases={i: i for i in range(nb)},
        compiler_params=pltpu.CompilerParams(has_side_effects=_EFFECT),
    )(*bufs, send_sems, recv_sems, after)


def _chip_copies(refs, send_sems, recv_sems):
    pair_ref, land_ref = refs
    x, y, c = _place()
    chips = [(1 - x, y), (x, 1 - y), (1 - x, 1 - y)]
    return [pltpu.make_async_remote_copy(
        src_ref=pair_ref.at[2 * chip[0] + chip[1]], dst_ref=land_ref.at[k],
        send_sem=send_sems.at[k], recv_sem=recv_sems.at[k],
        device_id=(*chip, c), device_id_type=MESH) for k, chip in enumerate(chips)]


def _first_hop_copies(refs, send_sems, recv_sems):
    pair_ref, land_ref = refs
    x, y, c = _place()
    first = ((x + 1 - c) % 2, (y + c) % 2)
    blocks = [2 * first[0] + first[1], 2 * (1 - x) + (1 - y)]
    return [pltpu.make_async_remote_copy(
        src_ref=pair_ref.at[blocks[k]], dst_ref=land_ref.at[k], send_sem=send_sems.at[k], recv_sem=recv_sems.at[k],
        device_id=(*first, c), device_id_type=MESH) for k in range(2)]


def _second_hop_copies(refs, send_sems, recv_sems):
    relay_ref, land_ref = refs
    x, y, c = _place()
    second = ((x + c) % 2, (y + 1 - c) % 2)
    return [pltpu.make_async_remote_copy(
        src_ref=relay_ref, dst_ref=land_ref.at[0], send_sem=send_sems.at[0], recv_sem=recv_sems.at[0],
        device_id=(*second, c), device_id_type=MESH)]


def _own_block_copies(targets):
    def copies(refs, send_sems, recv_sems):
        x, y, c = _place()
        mine = refs[0].at[4 * x + 2 * y + c]
        return [pltpu.make_async_remote_copy(
            src_ref=mine, dst_ref=mine, send_sem=send_sems.at[k], recv_sem=recv_sems.at[k],
            device_id=to, device_id_type=MESH) for k, to in enumerate(targets(x, y, c))]
    return copies


def _my_core_and_sibling(x, y, c):
    return [(x, y, 1 - c), (1 - x, y, c), (x, 1 - y, c), (1 - x, 1 - y, c)]


def _all_others(x, y, c):
    flip = lambda v, f: 1 - v if f else v
    return [(flip(x, r & 4), flip(y, r & 2), flip(c, r & 1)) for r in range(1, N_DEV)]


def _forward_copies(refs, send_sems, recv_sems):
    x, y, c = _place()
    chips = [(1 - x, y), (x, 1 - y), (1 - x, 1 - y)]
    return [pltpu.make_async_remote_copy(
        src_ref=refs[0].at[4 * chip[0] + 2 * chip[1] + c], dst_ref=refs[0].at[4 * chip[0] + 2 * chip[1] + c],
        send_sem=send_sems.at[k], recv_sem=recv_sems.at[k],
        device_id=(x, y, 1 - c), device_id_type=MESH) for k, chip in enumerate(chips)]


def _near_targets(x, y, c):
    return [(x, y, 1 - c), (1 - x, y, c), (x, 1 - y, c)]


def _second_stage_copies(refs, send_sems, recv_sems):
    x, y, c = _place()
    relayed = ((x + 1 - c) % 2, (y + c) % 2, c)
    relay_to = ((x + c) % 2, (y + 1 - c) % 2, c)
    plan = [((1 - x, y, c), (x, y, 1 - c)), ((x, 1 - y, c), (x, y, 1 - c)), (relayed, relay_to)]
    copies = []
    for k, ((px, py, pc), to) in enumerate(plan):
        blk = refs[0].at[4 * px + 2 * py + pc]
        copies.append(pltpu.make_async_remote_copy(
            src_ref=blk, dst_ref=blk, send_sem=send_sems.at[k], recv_sem=recv_sems.at[k],
            device_id=to, device_id_type=MESH))
    return copies


def _diagonal_forward_copies(refs, send_sems, recv_sems):
    x, y, c = _place()
    blk = refs[0].at[4 * (1 - x) + 2 * (1 - y) + c]
    return [pltpu.make_async_remote_copy(
        src_ref=blk, dst_ref=blk, send_sem=send_sems.at[0], recv_sem=recv_sems.at[0],
        device_id=(x, y, 1 - c), device_id_type=MESH)]


def _with_own_slot(block, me):
    return lax.dynamic_update_index_in_dim(lax.empty((N_DEV,) + block.shape, block.dtype), block, me, 0)


def _matmul(a, b, dims, out_dtype, tm, tn, name, dep=None):
    if dims == "nn":
        (m, k), n = a.shape, b.shape[1]
        a_spec = pl.BlockSpec((tm, k), lambda i, j: (i, 0))
        b_spec = pl.BlockSpec((k, tn), lambda i, j: (0, j))
        contract = ((1,), (0,))
    elif dims == "nt":
        (m, k), n = a.shape, b.shape[0]
        a_spec = pl.BlockSpec((tm, k), lambda i, j: (i, 0))
        b_spec = pl.BlockSpec((tn, k), lambda i, j: (j, 0))
        contract = ((1,), (1,))
    else:
        (k, m), n = a.shape, b.shape[1]
        a_spec = pl.BlockSpec((k, tm), lambda i, j: (0, i))
        b_spec = pl.BlockSpec((k, tn), lambda i, j: (0, j))
        contract = ((0,), (0,))
    assert m % tm == 0 and n % tn == 0 and a.dtype == BF16 and b.dtype == BF16

    def body(a_ref, b_ref, *rest):
        rest[-1][...] = lax.dot_general(a_ref[...], b_ref[...], (contract, ((), ())),
                                        preferred_element_type=F32).astype(out_dtype)

    deps = [] if dep is None else [dep]
    return pl.pallas_call(
        body, name=name, grid=(m // tm, n // tn),
        in_specs=[a_spec, b_spec] + [pl.BlockSpec((8, 128), lambda i, j: (0, 0))] * len(deps),
        out_specs=pl.BlockSpec((tm, tn), lambda i, j: (i, j)),
        out_shape=jax.ShapeDtypeStruct((m, n), out_dtype),
        compiler_params=_params(dimension_semantics=("arbitrary", "arbitrary")),
    )(a, b, *deps)


Z_TILE = 768
_Z_TILE_ORDER = ((0, 1, 2, 3, 4, 5, 6), (2, 0, 1, 6, 3, 4, 5), (4, 0, 5, 6, 1, 2, 3), (6, 2, 3, 4, 0, 1, 5))
_Z_EARLY_TILES = 4


def _z_proj(h, w_in_t, order, first, count, z_prev, name, dep=None):
    t = h.shape[0]

    def body(order_ref, h_ref, w_ref, *rest):
        rest[-1][...] = _dot_nt(h_ref[...], w_ref[...])

    prev = [] if z_prev is None else [z_prev]
    deps = [] if dep is None else [dep]
    return pl.pallas_call(
        body, name=name,
        grid_spec=pltpu.PrefetchScalarGridSpec(
            num_scalar_prefetch=1, grid=(count,),
            in_specs=[pl.BlockSpec((t, D_MODEL), lambda j, o: (0, 0)),
                      pl.BlockSpec((Z_TILE, D_MODEL), lambda j, o: (o[first + j], 0))]
            + [pl.BlockSpec(memory_space=pl.ANY)] * len(prev)
            + [pl.BlockSpec((8, 128), lambda j, o: (0, 0))] * len(deps),
            out_specs=pl.BlockSpec((t, Z_TILE), lambda j, o: (0, o[first + j]))),
        out_shape=jax.ShapeDtypeStruct((t, D_IN), F32),
        input_output_aliases={3: 0} if prev else {},
        compiler_params=_params(dimension_semantics=("arbitrary",)),
    )(order, h, w_in_t, *prev, *deps)


def _modulation(c_all, w_ada, b_ada_mine):
    def body(c_ref, w_ref, b_ref, act_ref, mod_ref):
        cv = c_ref[...]
        act = cv * _sigmoid(cv)
        act_ref[...] = act
        mod_ref[...] = jnp.dot(act.astype(BF16), w_ref[...].astype(BF16), preferred_element_type=F32) + b_ref[...]

    return pl.pallas_call(
        body, name="modulation",
        out_shape=(jax.ShapeDtypeStruct(c_all.shape, F32), jax.ShapeDtypeStruct((N_DEV, W_ADA_SHARD), F32)),
        compiler_params=_params(),
    )(c_all, w_ada, b_ada_mine)


def _modulated_norm(x, norm_g, scale, shift, tm=256):
    t, d = x.shape

    def body(x_ref, g_ref, sc_ref, sh_ref, h_ref):
        xv = x_ref[...]
        r = lax.rsqrt(jnp.mean(xv * xv, axis=-1, keepdims=True) + EPS)
        h = (xv * r) * g_ref[...] * (1.0 + sc_ref[...]) + sh_ref[...]
        h_ref[...] = h.astype(BF16)

    row = pl.BlockSpec((1, d), lambda i: (0, 0))
    return pl.pallas_call(
        body, name="modulated_norm", grid=(t // tm,),
        in_specs=[pl.BlockSpec((tm, d), lambda i: (i, 0)), row, row, row],
        out_specs=pl.BlockSpec((tm, d), lambda i: (i, 0)),
        out_shape=jax.ShapeDtypeStruct((t, d), BF16),
        compiler_params=_params(dimension_semantics=("arbitrary",)),
    )(x, norm_g, scale, shift)


def _window_bias(block_index):
    s = lax.broadcasted_iota(jnp.int32, (2 * BLOCK, BLOCK), 0)
    t = lax.broadcasted_iota(jnp.int32, (2 * BLOCK, BLOCK), 1)
    valid = ((s < BLOCK) & (s > t) & (block_index > 0)) | ((s >= BLOCK) & ((s - BLOCK) <= t))
    bias = jnp.where(valid, 0.0, -jnp.inf).astype(F32)
    return jnp.concatenate([bias] * 8, axis=1)


def _heads_t(pair_blocks, g):
    top = lax.broadcasted_iota(jnp.int32, (BLOCK, BLOCK), 0) < HEAD_DIM
    zeros = jnp.zeros((HEAD_DIM, BLOCK), F32)
    tiles = []
    for blk in pair_blocks:
        tp = blk.T
        if g == 0:
            tiles += [jnp.where(top, tp, 0.0), jnp.concatenate([tp[HEAD_DIM:], zeros], axis=0)]
        else:
            tiles += [jnp.concatenate([zeros, tp[:HEAD_DIM]], axis=0), jnp.where(top, 0.0, tp)]
    return jnp.concatenate(tiles, axis=1)


def _pair_block(xt, p, g):
    r0 = HEAD_DIM * g
    even = xt[r0:r0 + HEAD_DIM, (2 * p) * BLOCK:(2 * p + 1) * BLOCK]
    odd = xt[r0:r0 + HEAD_DIM, (2 * p + 1) * BLOCK:(2 * p + 2) * BLOCK]
    return jnp.concatenate([even, odd], axis=0).T


def _softmax_t(scores_t, bias, sink):
    st = scores_t * ATTN_SCALE + bias
    m = jnp.maximum(jnp.max(st, axis=0, keepdims=True), sink)
    e = jnp.exp(st - m)
    es = jnp.exp(sink - m)
    inv = 1.0 / (jnp.sum(e, axis=0, keepdims=True) + es)
    return e * inv, es * inv


def _dot(a, b):
    return jnp.dot(a, b, preferred_element_type=F32)


def _dot_nt(a, b):
    return lax.dot_general(a, b, (((1,), (1,)), ((), ())), preferred_element_type=F32)


def _layer_norm_fwd(v):
    mu = jnp.mean(v, axis=-1, keepdims=True)
    xc = v - mu
    rstd = lax.rsqrt(jnp.mean(xc * xc, axis=-1, keepdims=True) + EPS)
    return xc * rstd, rstd


def _tril(transposed=False):
    t = lax.broadcasted_iota(jnp.int32, (BLOCK, BLOCK), 0)
    s = lax.broadcasted_iota(jnp.int32, (BLOCK, BLOCK), 1)
    return s >= t if transposed else t >= s


def _const_spec(shape):
    return pl.BlockSpec(shape, lambda i: (0,) * len(shape))


def _kv_prev_spec(index):
    return pl.BlockSpec((BLOCK, 2 * D_KV), lambda i: (jnp.maximum(index(i) - 1, 0), SEG_KV // (2 * D_KV)))


def _keys_values(z_ref, kvp_ref):
    kvp, kvc = kvp_ref[...], z_ref[:, SEG_KV:SEG_KV + 2 * D_KV]
    kk = jnp.concatenate([kvp[:, :D_KV], kvc[:, :D_KV]], axis=0)
    vv = jnp.concatenate([kvp[:, D_KV:], kvc[:, D_KV:]], axis=0)
    return kk, vv


def _pair_cols(g, p, base=0):
    return slice(base + (4 * g + p) * 128, base + (4 * g + p + 1) * 128)


def _mixer_fwd(z, sink_rows, ln_g, ln_b, sgu_w, sgu_bt):
    t = z.shape[0]

    def body(z_ref, kvp_ref, sink_ref, lng_ref, lnb_ref, w_ref, bt_ref, a_ref):
        bias = _window_bias(pl.program_id(0))
        kk, vv = _keys_values(z_ref, kvp_ref)
        kk_b, vvt_b = kk.astype(BF16), vv.T.astype(BF16)
        for g in range(2):
            qt = _heads_t([z_ref[:, _pair_cols(g, p, SEG_Q)] for p in range(4)], g).astype(BF16)
            prob, _ = _softmax_t(_dot(kk_b, qt), bias, sink_ref[g])
            ot = _dot(vvt_b, prob.astype(BF16))
            for p in range(4):
                gate = z_ref[:, _pair_cols(g, p, SEG_GA)]
                a_ref[:, _pair_cols(g, p)] = (_pair_block(ot, p, g) * (gate * _sigmoid(gate))).astype(BF16)

        vhat, _ = _layer_norm_fwd(z_ref[:, SEG_VS:SEG_VS + D_SGU])
        vn = vhat * lng_ref[...] + lnb_ref[...]
        tril = _tril()
        for g in range(SGU_GROUPS):
            cols = slice(g * 128, (g + 1) * 128)
            wm = jnp.where(tril, w_ref[g], 0.0).astype(BF16)
            mixed = _dot(wm, vn[:, cols].astype(BF16)) + bt_ref[:, g:g + 1]
            gate = z_ref[:, SEG_GS + g * 128:SEG_GS + (g + 1) * 128]
            a_ref[:, D_ATTN + g * 128:D_ATTN + (g + 1) * 128] = (
                (z_ref[:, SEG_U + g * 128:SEG_U + (g + 1) * 128] * mixed) * (gate * _sigmoid(gate))).astype(BF16)

    return pl.pallas_call(
        body, name="mixer_fwd", grid=(t // BLOCK,),
        in_specs=[pl.BlockSpec((BLOCK, D_IN), lambda i: (i, 0)), _kv_prev_spec(lambda i: i),
                  _const_spec((2, 1, 8 * BLOCK)), _const_spec((1, D_SGU)), _const_spec((1, D_SGU)),
                  _const_spec((SGU_GROUPS, BLOCK, BLOCK)), _const_spec((BLOCK, SGU_GROUPS))],
        out_specs=pl.BlockSpec((BLOCK, D_MODEL), lambda i: (i, 0)),
        out_shape=jax.ShapeDtypeStruct((t, D_MODEL), BF16),
        compiler_params=_params(dimension_semantics=("arbitrary",)),
    )(z, z, sink_rows, ln_g, ln_b, sgu_w, sgu_bt)


def _mixer_bwd(z, da, sink_rows, ln_g, ln_b, sgu_w, sgu_wt, sgu_bt):
    t = z.shape[0]
    nb = t // BLOCK

    def body(z_ref, kvp_ref, da_ref, sink_ref, lng_ref, lnb_ref, w_ref, wt_ref, bt_ref,
             dz_ref, dsink_ref, dw_ref, db_ref, dlng_ref, dlnb_ref, carry_ref, dsink_acc, dbt_acc):
        step = pl.program_id(0)

        @pl.when(step == 0)
        def _():
            carry_ref[...] = jnp.zeros_like(carry_ref)
            dsink_acc[...] = jnp.zeros_like(dsink_acc)
            dbt_acc[...] = jnp.zeros_like(dbt_acc)
            dw_ref[...] = jnp.zeros_like(dw_ref)
            dlng_ref[...] = jnp.zeros_like(dlng_ref)
            dlnb_ref[...] = jnp.zeros_like(dlnb_ref)

        bias = _window_bias(nb - 1 - step)
        kk, vv = _keys_values(z_ref, kvp_ref)
        kk_b, vv_b = kk.astype(BF16), vv.astype(BF16)
        kkt_b, vvt_b = kk.T.astype(BF16), vv.T.astype(BF16)
        dkk = jnp.zeros((2 * BLOCK, D_KV), F32)
        dvv = jnp.zeros((2 * BLOCK, D_KV), F32)
        for g in range(2):
            qt = _heads_t([z_ref[:, _pair_cols(g, p, SEG_Q)] for p in range(4)], g).astype(BF16)
            prob, sink_prob = _softmax_t(_dot(kk_b, qt), bias, sink_ref[g])
            prob_b = prob.astype(BF16)
            ot = _dot(vvt_b, prob_b)
            gates = [z_ref[:, _pair_cols(g, p, SEG_GA)] for p in range(4)]
            sig = [_sigmoid(gt) for gt in gates]
            d_attn = [da_ref[:, _pair_cols(g, p)] for p in range(4)]
            d_ot = _heads_t([d_attn[p] * (gates[p] * sig[p]) for p in range(4)], g).astype(BF16)
            d_prob = _dot(vv_b, d_ot)
            delta = jnp.sum(prob * d_prob, axis=0, keepdims=True)
            d_scores = (prob * (d_prob - delta) * ATTN_SCALE).astype(BF16)
            dsink_acc[g] -= sink_prob * delta
            d_qt = _dot(kkt_b, d_scores)
            dkk = dkk + _dot_nt(d_scores, qt)
            dvv = dvv + _dot_nt(prob_b, d_ot)
            for p in range(4):
                dz_ref[:, _pair_cols(g, p, SEG_Q)] = _pair_block(d_qt, p, g).astype(BF16)
                d_silu = sig[p] * (1.0 + gates[p] * (1.0 - sig[p]))
                dz_ref[:, _pair_cols(g, p, SEG_GA)] = (d_attn[p] * _pair_block(ot, p, g) * d_silu).astype(BF16)
        d_kv = jnp.concatenate([dkk, dvv], axis=1)
        dz_ref[:, SEG_KV:SEG_KV + 2 * D_KV] = (d_kv[BLOCK:] + carry_ref[...]).astype(BF16)
        carry_ref[...] = d_kv[:BLOCK]

        vhat, rstd = _layer_norm_fwd(z_ref[:, SEG_VS:SEG_VS + D_SGU])
        lng = lng_ref[...]
        vn = vhat * lng + lnb_ref[...]
        tril, triu = _tril(), _tril(transposed=True)
        lane = lax.broadcasted_iota(jnp.int32, (BLOCK, 128), 1)
        d_bt = jnp.zeros((BLOCK, 128), F32)
        d_vn = []
        for g in range(SGU_GROUPS):
            cols = slice(g * 128, (g + 1) * 128)
            wm = jnp.where(tril, w_ref[g], 0.0).astype(BF16)
            wmt = jnp.where(triu, wt_ref[g], 0.0).astype(BF16)
            vn_g = vn[:, cols].astype(BF16)
            mixed = _dot(wm, vn_g) + bt_ref[:, g:g + 1]
            gate = z_ref[:, SEG_GS + g * 128:SEG_GS + (g + 1) * 128]
            u = z_ref[:, SEG_U + g * 128:SEG_U + (g + 1) * 128]
            d_out = da_ref[:, D_ATTN + g * 128:D_ATTN + (g + 1) * 128]
            sg = _sigmoid(gate)
            d_um = d_out * (gate * sg)
            dz_ref[:, SEG_U + g * 128:SEG_U + (g + 1) * 128] = (d_um * mixed).astype(BF16)
            dz_ref[:, SEG_GS + g * 128:SEG_GS + (g + 1) * 128] = (
                d_out * (u * mixed) * (sg * (1.0 + gate * (1.0 - sg)))).astype(BF16)
            d_mixed = d_um * u
            d_mixed_b = d_mixed.astype(BF16)
            dw_ref[g] += jnp.where(tril, _dot_nt(d_mixed_b, vn_g), 0.0)
            d_bt = d_bt + jnp.where(lane == g, jnp.sum(d_mixed, axis=-1, keepdims=True), 0.0)
            d_vn.append(_dot(wmt, d_mixed_b))
        dbt_acc[...] += d_bt
        d_vn = jnp.concatenate(d_vn, axis=1)
        dlng_ref[...] += jnp.sum(d_vn * vhat, axis=0, keepdims=True)
        dlnb_ref[...] += jnp.sum(d_vn, axis=0, keepdims=True)
        d_vhat = d_vn * lng
        d_v = rstd * (d_vhat - jnp.mean(d_vhat, axis=-1, keepdims=True)
                      - vhat * jnp.mean(d_vhat * vhat, axis=-1, keepdims=True))
        dz_ref[:, SEG_VS:SEG_VS + D_SGU] = d_v.astype(BF16)

        @pl.when(step == nb - 1)
        def _():
            db_ref[...] = dbt_acc[...].T[:SGU_GROUPS]
            lane_row = lax.broadcasted_iota(jnp.int32, (1, 128), 1)
            d_sink = jnp.zeros((1, 128), F32)
            for g in range(2):
                acc = dsink_acc[g]
                for j in range(8):
                    head_sum = jnp.sum(acc[:, j * BLOCK:(j + 1) * BLOCK], axis=-1, keepdims=True)
                    d_sink = d_sink + jnp.where(lane_row == 8 * g + j, head_sum, 0.0)
            dsink_ref[...] = d_sink

    rev = lambda i: nb - 1 - i
    return pl.pallas_call(
        body, name="mixer_bwd", grid=(nb,),
        in_specs=[pl.BlockSpec((BLOCK, D_IN), lambda i: (rev(i), 0)), _kv_prev_spec(rev),
                  pl.BlockSpec((BLOCK, D_MODEL), lambda i: (rev(i), 0)),
                  _const_spec((2, 1, 8 * BLOCK)), _const_spec((1, D_SGU)), _const_spec((1, D_SGU)),
                  _const_spec((SGU_GROUPS, BLOCK, BLOCK)), _const_spec((SGU_GROUPS, BLOCK, BLOCK)),
                  _const_spec((BLOCK, SGU_GROUPS))],
        out_specs=(pl.BlockSpec((BLOCK, D_IN), lambda i: (rev(i), 0)), _const_spec((1, 128)),
                   _const_spec((SGU_GROUPS, BLOCK, BLOCK)), _const_spec((SGU_GROUPS, BLOCK)),
                   _const_spec((1, D_SGU)), _const_spec((1, D_SGU))),
        out_shape=(jax.ShapeDtypeStruct((t, D_IN), BF16), jax.ShapeDtypeStruct((1, 128), F32),
                   jax.ShapeDtypeStruct((SGU_GROUPS, BLOCK, BLOCK), F32), jax.ShapeDtypeStruct((SGU_GROUPS, BLOCK), F32),
                   jax.ShapeDtypeStruct((1, D_SGU), F32), jax.ShapeDtypeStruct((1, D_SGU), F32)),
        scratch_shapes=[pltpu.VMEM((BLOCK, 2 * D_KV), F32), pltpu.VMEM((2, 1, 8 * BLOCK), F32),
                        pltpu.VMEM((BLOCK, 128), F32)],
        compiler_params=_params(dimension_semantics=("arbitrary",)),
    )(z, z, da, sink_rows, ln_g, ln_b, sgu_w, sgu_wt, sgu_bt)


def _head(y, x, target, gate, final_g, tm=256):
    t, d = x.shape

    def body(y_ref, x_ref, tg_ref, gate_ref, fg_ref, dx2_ref, dy_ref, loss_ref, dfg_ref, dgate_ref):
        @pl.when(pl.program_id(0) == 0)
        def _():
            loss_ref[...] = jnp.zeros_like(loss_ref)
            dfg_ref[...] = jnp.zeros_like(dfg_ref)
            dgate_ref[...] = jnp.zeros_like(dgate_ref)

        yv, gate, fg = y_ref[...], gate_ref[...], fg_ref[...]
        x2 = x_ref[...] + gate * yv
        r2 = lax.rsqrt(jnp.mean(x2 * x2, axis=-1, keepdims=True) + EPS)
        nrm = x2 * r2
        err = nrm * fg - tg_ref[...]
        loss_ref[...] += 0.5 * jnp.sum(jnp.mean(err * err, axis=-1, keepdims=True), axis=0, keepdims=True)
        d_out = err * (1.0 / d)
        dfg_ref[...] += jnp.sum(d_out * nrm, axis=0, keepdims=True)
        d_nrm = d_out * fg
        dx2 = r2 * (d_nrm - nrm * jnp.mean(d_nrm * nrm, axis=-1, keepdims=True))
        dx2_ref[...] = dx2
        dgate_ref[...] += jnp.sum(dx2 * yv, axis=0, keepdims=True)
        dy_ref[...] = (dx2 * gate).astype(BF16)

    blk = pl.BlockSpec((tm, d), lambda i: (i, 0))
    row = _const_spec((1, d))
    return pl.pallas_call(
        body, name="head", grid=(t // tm,),
        in_specs=[blk, blk, blk, row, row],
        out_specs=(blk, blk, _const_spec((1, 128)), row, row),
        out_shape=(jax.ShapeDtypeStruct((t, d), F32), jax.ShapeDtypeStruct((t, d), BF16),
                   jax.ShapeDtypeStruct((1, 128), F32), jax.ShapeDtypeStruct((1, d), F32),
                   jax.ShapeDtypeStruct((1, d), F32)),
        compiler_params=_params(dimension_semantics=("arbitrary",)),
    )(y, x, target, gate, final_g)


def _modulated_norm_bwd(dh, x, dx2, norm_g, scale, tm=256):
    t, d = x.shape

    def body(dh_ref, x_ref, dx2_ref, g_ref, sc_ref, gx_ref, dshift_ref, dscale_ref, dg_ref):
        @pl.when(pl.program_id(0) == 0)
        def _():
            dshift_ref[...] = jnp.zeros_like(dshift_ref)
            dscale_ref[...] = jnp.zeros_like(dscale_ref)
            dg_ref[...] = jnp.zeros_like(dg_ref)

        dh, xv, g = dh_ref[...], x_ref[...], g_ref[...]
        one_plus = 1.0 + sc_ref[...]
        r = lax.rsqrt(jnp.mean(xv * xv, axis=-1, keepdims=True) + EPS)
        xn = xv * r
        dshift_ref[...] += jnp.sum(dh, axis=0, keepdims=True)
        dscale_ref[...] += jnp.sum(dh * (xn * g), axis=0, keepdims=True)
        d_y = dh * one_plus
        dg_ref[...] += jnp.sum(d_y * xn, axis=0, keepdims=True)
        d_xn = d_y * g
        gx_ref[...] = dx2_ref[...] + r * (d_xn - xn * jnp.mean(d_xn * xn, axis=-1, keepdims=True))

    blk = pl.BlockSpec((tm, d), lambda i: (i, 0))
    row = _const_spec((1, d))
    return pl.pallas_call(
        body, name="modulated_norm_bwd", grid=(t // tm,),
        in_specs=[blk, blk, blk, row, row], out_specs=(blk, row, row, row),
        out_shape=(jax.ShapeDtypeStruct((t, d), F32),) + (jax.ShapeDtypeStruct((1, d), F32),) * 3,
        compiler_params=_params(dimension_semantics=("arbitrary",)),
    )(dh, x, dx2, norm_g, scale)


def _adamw(w, g, m, v):
    m = ADAM_B1 * m + (1.0 - ADAM_B1) * g
    v = ADAM_B2 * v + (1.0 - ADAM_B2) * (g * g)
    m_hat = m / (1.0 - ADAM_B1 ** ADAM_STEP)
    v_hat = v / (1.0 - ADAM_B2 ** ADAM_STEP)
    delta = -ADAM_LR * (m_hat / (jnp.sqrt(v_hat) + ADAM_EPS) + ADAM_WD * w)
    return delta, m, v


def _relay_sum(second_chip, pair, land, tr):
    _, r, c = pair.shape

    def body(chip_ref, a_ref, b_ref, o_ref):
        o_ref[...] = (a_ref[...].astype(F32) + b_ref[...].astype(F32)).astype(BF16)

    return pl.pallas_call(
        body, name="w_in_grad_relay_sum",
        grid_spec=pltpu.PrefetchScalarGridSpec(
            num_scalar_prefetch=1, grid=(r // tr,),
            in_specs=[pl.BlockSpec((None, tr, c), lambda i, chip_ref: (chip_ref[0], i, 0)),
                      pl.BlockSpec((None, tr, c), lambda i, chip_ref: (1, i, 0))],
            out_specs=pl.BlockSpec((tr, c), lambda i, chip_ref: (i, 0))),
        out_shape=jax.ShapeDtypeStruct((r, c), BF16),
        compiler_params=_params(dimension_semantics=("arbitrary",)),
    )(second_chip, pair, land)


def _adam_from_chips(chip, pair, landed, w, m, v, name, tc):
    _, r, c = pair.shape
    n = len(landed)

    def body(chip_ref, own_ref, *refs):
        w_ref, m_ref, v_ref, g_ref, d_ref, nm_ref, nv_ref = refs[n:]
        g = own_ref[...].astype(F32)
        for k in range(n):
            g = g + refs[k][...].astype(F32)
        g_ref[...] = g
        d_ref[...], nm_ref[...], nv_ref[...] = _adamw(w_ref[...], g, m_ref[...], v_ref[...])

    def landed_spec(index):
        return pl.BlockSpec((None, r, tc), lambda i, chip_ref: (index, 0, i))

    blk = pl.BlockSpec((r, tc), lambda i, chip_ref: (0, i))
    return pl.pallas_call(
        body, name=name,
        grid_spec=pltpu.PrefetchScalarGridSpec(
            num_scalar_prefetch=1, grid=(c // tc,),
            in_specs=[pl.BlockSpec((None, r, tc), lambda i, chip_ref: (chip_ref[0], 0, i))]
            + [landed_spec(index) for _, index in landed] + [blk, blk, blk],
            out_specs=(blk,) * 4),
        out_shape=(jax.ShapeDtypeStruct((r, c), F32),) * 4,
        compiler_params=_params(dimension_semantics=("arbitrary",)),
    )(chip, pair, *[array for array, _ in landed], w, m, v)


def _adam_w_ada(act_t, dmod_mine, w, m, v, tr=256):
    r, c = w.shape

    def body(a_ref, dm_ref, w_ref, m_ref, v_ref, g_ref, d_ref, nm_ref, nv_ref):
        g = _dot(a_ref[...].astype(BF16), dm_ref[...].astype(BF16))
        g_ref[...] = g
        d_ref[...], nm_ref[...], nv_ref[...] = _adamw(w_ref[...], g, m_ref[...], v_ref[...])

    blk = pl.BlockSpec((tr, c), lambda i: (i, 0))
    return pl.pallas_call(
        body, name="adam_w_ada", grid=(r // tr,),
        in_specs=[pl.BlockSpec((tr, N_DEV), lambda i: (i, 0)), _const_spec((N_DEV, c)), blk, blk, blk],
        out_specs=(blk,) * 4, out_shape=(jax.ShapeDtypeStruct((r, c), F32),) * 4,
        compiler_params=_params(dimension_semantics=("arbitrary",)),
    )(act_t, dmod_mine, w, m, v)


def _pack_small(d_shift, d_scale, d_gate, d_norm_g, d_final_g, d_ln_g, d_ln_b, loss, d_sinks, d_sgu_b):
    def body(shift_ref, scale_ref, gate_ref, ng_ref, fg_ref, lng_ref, lnb_ref, loss_ref, sink_ref, b_ref, o_ref):
        o_ref[...] = jnp.zeros_like(o_ref)
        o_ref[ROW_SHIFT:ROW_SHIFT + 1, :] = shift_ref[...]
        o_ref[ROW_SCALE:ROW_SCALE + 1, :] = scale_ref[...]
        o_ref[ROW_GATE:ROW_GATE + 1, :] = gate_ref[...]
        o_ref[ROW_NORM_G:ROW_NORM_G + 1, :] = ng_ref[...]
        o_ref[ROW_FINAL_G:ROW_FINAL_G + 1, :] = fg_ref[...]
        o_ref[ROW_LN:ROW_LN + 1, 0:D_SGU] = lng_ref[...]
        o_ref[ROW_LN:ROW_LN + 1, D_SGU:2 * D_SGU] = lnb_ref[...]
        o_ref[ROW_MISC:ROW_MISC + 1, 0:128] = loss_ref[...]
        o_ref[ROW_MISC:ROW_MISC + 1, 128:256] = sink_ref[...]
        o_ref[ROW_SGU_B:ROW_SGU_B + SGU_GROUPS, 0:BLOCK] = b_ref[...]

    return pl.pallas_call(
        body, name="pack_small", out_shape=jax.ShapeDtypeStruct((SMALL_ROWS, D_MODEL), F32),
        compiler_params=_params(),
    )(d_shift, d_scale, d_gate, d_norm_g, d_final_g, d_ln_g, d_ln_b, loss, d_sinks, d_sgu_b)


_SMALL_NAMES = ("norm_g", "b_ada", "attn_sinks", "sgu_ln_g", "sgu_ln_b", "sgu_w", "sgu_b", "final_g")


def _adam_small(partials, d_sgu_w_all, weights, moments_m, moments_v):
    names = _SMALL_NAMES
    k = len(names)

    def body(*refs):
        p_ref, sw_ref = refs[0], refs[1]
        w_refs, m_refs, v_refs = refs[2:2 + k], refs[2 + k:2 + 2 * k], refs[2 + 2 * k:2 + 3 * k]
        loss_ref, dmod_ref = refs[2 + 3 * k], refs[3 + 3 * k]
        out_refs = refs[4 + 3 * k:4 + 7 * k]
        sum_ref = refs[4 + 7 * k]
        total = p_ref[0]
        for j in range(1, N_DEV):
            total = total + p_ref[j]
        sum_ref[...] = total
        for j in range(N_DEV):
            for part, row in enumerate((ROW_SHIFT, ROW_SCALE, ROW_GATE)):
                dmod_ref[j:j + 1, part * D_MODEL:(part + 1) * D_MODEL] = p_ref[j, row:row + 1, :]
        loss_ref[...] = sum_ref[ROW_MISC:ROW_MISC + 1, 0:1]
        d_sgu_w = sw_ref[0]
        for j in range(1, N_DEV):
            d_sgu_w = d_sgu_w + sw_ref[j]
        grads = {
            "norm_g": sum_ref[ROW_NORM_G:ROW_NORM_G + 1, :],
            "b_ada": jnp.concatenate([sum_ref[r:r + 1, :] for r in (ROW_SHIFT, ROW_SCALE, ROW_GATE)], axis=1),
            "attn_sinks": sum_ref[ROW_MISC:ROW_MISC + 1, 128:128 + N_Q_HEADS],
            "sgu_ln_g": sum_ref[ROW_LN:ROW_LN + 1, 0:D_SGU],
            "sgu_ln_b": sum_ref[ROW_LN:ROW_LN + 1, D_SGU:2 * D_SGU],
            "sgu_w": d_sgu_w[None],
            "sgu_b": sum_ref[ROW_SGU_B:ROW_SGU_B + SGU_GROUPS, 0:BLOCK][None],
            "final_g": sum_ref[ROW_FINAL_G:ROW_FINAL_G + 1, :],
        }
        for i, name in enumerate(names):
            g = grads[name]
            delta, m, v = _adamw(w_refs[i][...], g, m_refs[i][...], v_refs[i][...])
            out_refs[4 * i][...] = g
            out_refs[4 * i + 1][...] = delta
            out_refs[4 * i + 2][...] = m
            out_refs[4 * i + 3][...] = v

    shapes = [jax.ShapeDtypeStruct((1, 1), F32), jax.ShapeDtypeStruct((N_DEV, 3 * D_MODEL), F32)]
    for name in names:
        shapes += [jax.ShapeDtypeStruct(weights[name].shape, F32)] * 4
    outs = pl.pallas_call(
        body, name="adam_small", out_shape=tuple(shapes),
        scratch_shapes=[pltpu.VMEM((SMALL_ROWS, D_MODEL), F32)],
        compiler_params=_params(),
    )(partials, d_sgu_w_all, *[weights[n] for n in names], *[moments_m[n] for n in names],
      *[moments_v[n] for n in names])
    return outs[0], outs[1], {name: outs[2 + 4 * i:6 + 4 * i] for i, name in enumerate(names)}


def kernel(x, c, norm_g, w_ada, b_ada, w_in, attn_sinks, sgu_ln_g, sgu_ln_b, sgu_w, sgu_b, w_out, final_g, loss_target, m_norm_g, m_w_ada, m_b_ada, m_w_in, m_attn_sinks, m_sgu_ln_g, m_sgu_ln_b, m_sgu_w, m_sgu_b, m_w_out, m_final_g, v_norm_g, v_w_ada, v_b_ada, v_w_in, v_attn_sinks, v_sgu_ln_g, v_sgu_ln_b, v_sgu_w, v_sgu_b, v_w_out, v_final_g):
    xi, yi, ci = _place()
    me = 4 * xi + 2 * yi + ci
    x2d, target = x[0], loss_target[0]
    t = x2d.shape[0]

    core = ci.astype(jnp.int32).reshape(1)
    chip = (2 * xi + yi).astype(jnp.int32).reshape(1)

    c_all = _all_gather([c.reshape(8, 256)], "gather_c", True)[0].reshape(N_DEV, D_MODEL)
    b_mine = lax.dynamic_slice(b_ada, (0, me * W_ADA_SHARD), (1, W_ADA_SHARD))
    c_act, mod_part = _modulation(c_all, w_ada[0], b_mine)
    mod_all = _all_gather([mod_part], "gather_mod", True)[0]

    near = _own_block_copies(_near_targets)
    w_in_flight = _start_copies([_with_own_slot(w_in[0].T.astype(BF16), me)], near, 3, mod_all, "gather_w_in_start")
    mod = lax.dynamic_index_in_dim(mod_all, me, axis=1, keepdims=False).reshape(1, 3 * D_MODEL)
    mod = mod + w_in_flight[3][0, 0]
    shift, scale, gate = mod[:, :D_MODEL], mod[:, D_MODEL:2 * D_MODEL], mod[:, 2 * D_MODEL:]
    h = _modulated_norm(x2d, norm_g, scale, shift)

    w_in_pair = _wait_copies(w_in_flight, lambda *a: near(*a)[:1], h, "gather_w_in_sibling_wait")
    tile_order = jnp.asarray(_Z_TILE_ORDER, jnp.int32)[chip[0]]
    z_own = _z_proj(h, w_in_pair[0].reshape(D_IN, D_MODEL), tile_order, 0, 1, None, "z_proj_own")
    w_in_near = _wait_copies((w_in_flight[0], w_in_flight[1], w_in_pair, None), lambda *a: near(*a)[1:], z_own,
                             "gather_w_in_wait")
    w_in_flight = _start_copies(w_in_near, _second_stage_copies, 3, core, "gather_w_in_second_start")
    w_out_flight = _start_copies([_with_own_slot(w_out[0].astype(BF16), me)], _own_block_copies(_my_core_and_sibling),
                                 4, w_in_flight[3], "gather_w_out_start")
    w_in_most = _wait_copies(w_in_flight, lambda *a: _second_stage_copies(*a)[:2], w_out_flight[3],
                             "gather_w_in_forward_wait")
    z_early = _z_proj(h, w_in_most[0].reshape(D_IN, D_MODEL), tile_order, 1, _Z_EARLY_TILES - 1, z_own, "z_proj_early")
    w_in_diag = _wait_copies((w_in_flight[0], w_in_flight[1], w_in_most, None),
                             lambda *a: _second_stage_copies(*a)[2:], z_early, "gather_w_in_diagonal_wait")
    w_in_flight = _start_copies(w_in_diag, _diagonal_forward_copies, 1, core, "gather_w_in_last_start")
    w_in_all = _wait_copies(w_in_flight, _diagonal_forward_copies, z_early, "gather_w_in_last_wait")[0]
    w_in_t = w_in_all.reshape(D_IN, D_MODEL)
    z = _z_proj(h, w_in_t, tile_order, _Z_EARLY_TILES, 7 - _Z_EARLY_TILES, z_early, "z_proj_late")
    w_out_half = _wait_copies(w_out_flight, _own_block_copies(_my_core_and_sibling), z, "gather_w_out_wait")
    w_out_flight = _start_copies(w_out_half, _forward_copies, 3, z, "gather_w_out_forward_start")
    sink_rows = jnp.repeat(attn_sinks.reshape(N_Q_HEADS), BLOCK).reshape(2, 1, 8 * BLOCK)
    sgu_bt = sgu_b[0].T
    a = _mixer_fwd(z, sink_rows + w_out_flight[3][0, 0], sgu_ln_g, sgu_ln_b, sgu_w[0], sgu_bt)
    w_out_all = _wait_copies(w_out_flight, _forward_copies, a, "gather_w_out_forward_wait")[0]
    w_out_full = w_out_all.reshape(D_MODEL, D_MODEL)
    y = _matmul(a, w_out_full, "nn", F32, min(t, 1024), 1024, "out_proj")
    final_g_row = final_g.reshape(1, D_MODEL)
    dx2, dy, loss_part, d_final_g, d_gate = _head(y, x2d, target, gate, final_g_row)

    da = _matmul(dy, w_out_full, "nt", F32, min(t, 1024), 1024, "out_proj_bwd")
    dw_out = _matmul(a, dy, "tn", BF16, 1024, 1024, "w_out_grad").reshape(4, 2, W_OUT_SHARD, D_MODEL)
    pair_out = _pair_reduce(dw_out, "w_out_grad_pair_reduce", W_OUT_SHARD // 2)
    out_flight = _start_copies([pair_out, lax.empty((3, W_OUT_SHARD, D_MODEL), BF16)], _chip_copies, 3, core,
                               "w_out_grad_chip_start")
    dz, d_sinks, d_sgu_w, d_sgu_b, d_ln_g, d_ln_b = _mixer_bwd(
        z, da, sink_rows + out_flight[3][0, 0], sgu_ln_g, sgu_ln_b, sgu_w[0], jnp.swapaxes(sgu_w[0], 1, 2), sgu_bt)
    sgu_w_flight = _start_copies([_with_own_slot(d_sgu_w, me)], _own_block_copies(_all_others), N_DEV - 1, core,
                                 "sgu_w_grad_gather_start")
    dw_in_t = _matmul(dz, h, "tn", BF16, 768, D_MODEL, "w_in_grad", dep=sgu_w_flight[3])
    dw_in_t = dw_in_t.reshape(4, 2, W_IN_SHARD, D_MODEL)
    pair_in = _pair_reduce(dw_in_t, "w_in_grad_pair_reduce", W_IN_SHARD // 3)
    hop1 = _start_copies([pair_in, lax.empty((2, W_IN_SHARD, D_MODEL), BF16)], _first_hop_copies, 2, core,
                         "w_in_grad_first_hop_start")
    dh = _matmul(dz, w_in_t, "nn", F32, min(t, 1024), 512, "z_proj_bwd", dep=hop1[3])
    pair_in, land_first = _wait_copies(hop1, _first_hop_copies, dh, "w_in_grad_first_hop_wait")
    second_chip = (2 * ((xi + ci) % 2) + (yi + 1 - ci) % 2).astype(jnp.int32).reshape(1)
    relay = _relay_sum(second_chip, pair_in, land_first, W_IN_SHARD // 3)
    hop2 = _start_copies([relay, lax.empty((1, W_IN_SHARD, D_MODEL), BF16)], _second_hop_copies, 1, core,
                         "w_in_grad_second_hop_start")
    grad_x, d_shift, d_scale, d_norm_g = _modulated_norm_bwd(dh, x2d, dx2, norm_g, scale + hop2[3][0, 0])

    partial = _pack_small(d_shift, d_scale, d_gate, d_norm_g, d_final_g, d_ln_g, d_ln_b, loss_part, d_sinks, d_sgu_b)
    partial_all = _all_gather([partial], "gather_small", True)[0]
    d_sgu_w_all = _wait_copies(sgu_w_flight, _own_block_copies(_all_others), partial_all, "sgu_w_grad_gather_wait")[0]
    weights = {"norm_g": norm_g, "b_ada": b_ada, "attn_sinks": attn_sinks, "sgu_ln_g": sgu_ln_g,
               "sgu_ln_b": sgu_ln_b, "sgu_w": sgu_w, "sgu_b": sgu_b, "final_g": final_g_row}
    moments_m = {"norm_g": m_norm_g, "b_ada": m_b_ada, "attn_sinks": m_attn_sinks, "sgu_ln_g": m_sgu_ln_g,
                 "sgu_ln_b": m_sgu_ln_b, "sgu_w": m_sgu_w, "sgu_b": m_sgu_b,
                 "final_g": m_final_g.reshape(1, D_MODEL)}
    moments_v = {"norm_g": v_norm_g, "b_ada": v_b_ada, "attn_sinks": v_attn_sinks, "sgu_ln_g": v_sgu_ln_g,
                 "sgu_ln_b": v_sgu_ln_b, "sgu_w": v_sgu_w, "sgu_b": v_sgu_b,
                 "final_g": v_final_g.reshape(1, D_MODEL)}
    loss, dmod_all, small = _adam_small(partial_all, d_sgu_w_all, weights, moments_m, moments_v)
    small["final_g"] = tuple(o.reshape(D_MODEL) for o in small["final_g"])

    dmod_mine = lax.dynamic_slice(dmod_all, (0, me * W_ADA_SHARD), (N_DEV, W_ADA_SHARD))
    big = {"w_ada": _adam_w_ada(c_act.T, dmod_mine, w_ada[0], m_w_ada[0], v_w_ada[0])}
    pair_out, land_out = _wait_copies(out_flight, _chip_copies, big["w_ada"][0], "w_out_grad_chip_wait")
    big["w_out"] = _adam_from_chips(chip, pair_out, [(land_out, k) for k in range(3)], w_out[0], m_w_out[0], v_w_out[0],
                                    "adam_w_out", 1024)
    _, land_second = _wait_copies(hop2, _second_hop_copies, big["w_out"][0], "w_in_grad_second_hop_wait")
    big["w_in"] = tuple(o.T for o in _adam_from_chips(
        chip, pair_in, [(land_first, 0), (land_second, 0)], w_in[0].T, m_w_in[0].T, v_w_in[0].T, "adam_w_in", 256))
    order = ["norm_g", "w_ada", "b_ada", "w_in", "attn_sinks", "sgu_ln_g", "sgu_ln_b", "sgu_w", "sgu_b", "w_out",
             "final_g"]
    outs = [loss.reshape(()), grad_x[None]]
    for k in range(4):
        for name in order:
            outs.append(big[name][k][None] if name in big else small[name][k])
    return tuple(outs)
```

```python
import jax
import jax.numpy as jnp
from jax import lax
from jax.experimental import pallas as pl
from jax.experimental.pallas import tpu as pltpu

F32 = jnp.float32
BF16 = jnp.bfloat16
MESH = pl.DeviceIdType.MESH

N_DEV = 8
D_MODEL = 2048
HEAD_DIM = 64
D_ATTN = 1024
N_Q_HEADS = 16
D_KV = 128
BLOCK = 128
D_SGU = 1024
SGU_GROUPS = 8
D_IN = 5376
W_IN_SHARD = D_IN // N_DEV
W_OUT_SHARD = D_MODEL // N_DEV
W_ADA_SHARD = 3 * D_MODEL // N_DEV
EPS = 1e-6
ATTN_SCALE = 0.125

ADAM_LR = 0.001
ADAM_B1 = 0.9
ADAM_B2 = 0.999
ADAM_EPS = 1e-08
ADAM_WD = 0.01
ADAM_STEP = 10

SEG_Q, SEG_KV, SEG_GA, SEG_U, SEG_VS, SEG_GS = 0, 1024, 1280, 2304, 3328, 4352

VMEM_LIMIT = 56 * 1024 * 1024

ROW_SHIFT, ROW_SCALE, ROW_GATE, ROW_NORM_G, ROW_FINAL_G, ROW_LN, ROW_MISC, ROW_SGU_B = 0, 1, 2, 3, 4, 5, 6, 8
SMALL_ROWS = 16


def _params(**kw):
    return pltpu.CompilerParams(vmem_limit_bytes=VMEM_LIMIT, **kw)


def _sigmoid(x):
    return 0.5 * (jnp.tanh(0.5 * x) + 1.0)


def _place():
    return lax.axis_index("x"), lax.axis_index("y"), lax.axis_index("c")


def _all_gather(shards, name, in_vmem):
    n = len(shards)

    def body(*refs):
        ins, outs = refs[:n], refs[n:2 * n]
        send_sems, recv_sems, local_sems = refs[2 * n:]
        x, y, c = _place()
        me, sibling = (x, y, c), (x, y, 1 - c)
        chips = [(1 - x, y), (x, 1 - y), (1 - x, 1 - y)]
        first, passed, mine = [], [], []
        for a in range(n):
            out_ref = outs[a]

            def slot(px, py, pc, out_ref=out_ref):
                return out_ref.at[4 * px + 2 * py + pc]

            def copy(k, block, to, src=None, a=a, slot=slot):
                return pltpu.make_async_remote_copy(
                    src_ref=slot(*block) if src is None else src, dst_ref=slot(*block),
                    send_sem=send_sems.at[a, k], recv_sem=recv_sems.at[a, k],
                    device_id=to, device_id_type=MESH)

            own = pltpu.make_async_copy(ins[a], slot(*me), local_sems.at[a])
            own.start()
            mine.append(own)
            mine_out = [copy(0, me, sibling, src=ins[a])]
            mine_out += [copy(1 + j, me, (*chip, c), src=ins[a]) for j, chip in enumerate(chips)]
            for cp in mine_out:
                cp.start()
            first += mine_out
            passed.append([copy(4 + j, (*chip, c), sibling) for j, chip in enumerate(chips)])
        for j, chip in enumerate(chips):
            for a in range(n):
                out_ref = outs[a]
                blk = out_ref.at[4 * chip[0] + 2 * chip[1] + c]
                pltpu.make_async_remote_copy(
                    src_ref=blk, dst_ref=blk, send_sem=send_sems.at[a, 1 + j], recv_sem=recv_sems.at[a, 1 + j],
                    device_id=me, device_id_type=MESH).wait_recv()
                passed[a][j].start()
        for a in range(n):
            out_ref = outs[a]
            blk = out_ref.at[4 * x + 2 * y + (1 - c)]
            pltpu.make_async_remote_copy(
                src_ref=blk, dst_ref=blk, send_sem=send_sems.at[a, 0], recv_sem=recv_sems.at[a, 0],
                device_id=me, device_id_type=MESH).wait_recv()
            for j, chip in enumerate(chips):
                blk = out_ref.at[4 * chip[0] + 2 * chip[1] + (1 - c)]
                pltpu.make_async_remote_copy(
                    src_ref=blk, dst_ref=blk, send_sem=send_sems.at[a, 4 + j], recv_sem=recv_sems.at[a, 4 + j],
                    device_id=me, device_id_type=MESH).wait_recv()
        for cp in first:
            cp.wait_send()
        for a in range(n):
            for cp in passed[a]:
                cp.wait_send()
        for cp in mine:
            cp.wait()

    space = pltpu.VMEM if in_vmem else pl.ANY
    spec = pl.BlockSpec(memory_space=space)
    return pl.pallas_call(
        body, name=name,
        out_shape=tuple(jax.ShapeDtypeStruct((N_DEV,) + s.shape, s.dtype) for s in shards),
        in_specs=[spec] * n, out_specs=tuple([spec] * n),
        scratch_shapes=[pltpu.SemaphoreType.DMA((n, 7)), pltpu.SemaphoreType.DMA((n, 7)),
                        pltpu.SemaphoreType.DMA((n,))],
        compiler_params=_params(),
    )(*shards)


def _pair_reduce(blocks, name, row_chunk):
    _, _, r, cols = blocks.shape
    assert r % row_chunk == 0

    def body(in_ref, out_ref, land, own, summed, send_sems, recv_sems, own_sems, out_sems):
        x, y, c = _place()
        sends, loads, stores = [], [], []
        for m in range(4):
            cp = pltpu.make_async_remote_copy(
                src_ref=in_ref.at[m, 1 - c], dst_ref=land.at[m], send_sem=send_sems.at[m], recv_sem=recv_sems.at[m],
                device_id=(x, y, 1 - c), device_id_type=MESH)
            cp.start()
            sends.append(cp)
            ld = pltpu.make_async_copy(in_ref.at[m, c], own.at[m], own_sems.at[m])
            ld.start()
            loads.append(ld)
        for m in range(4):
            sends[m].wait_recv()
            loads[m].wait()
            for k in range(r // row_chunk):
                rows = slice(k * row_chunk, (k + 1) * row_chunk)
                summed[m, rows, :] = (own[m, rows, :].astype(F32) + land[m, rows, :].astype(F32)).astype(BF16)
            st = pltpu.make_async_copy(summed.at[m], out_ref.at[m], out_sems.at[m])
            st.start()
            stores.append(st)
        for m in range(4):
            sends[m].wait_send()
            stores[m].wait()

    spec = pl.BlockSpec(memory_space=pl.ANY)
    return pl.pallas_call(
        body, name=name, out_shape=jax.ShapeDtypeStruct((4, r, cols), BF16),
        in_specs=[spec], out_specs=spec,
        scratch_shapes=[pltpu.VMEM((4, r, cols), BF16), pltpu.VMEM((4, r, cols), BF16), pltpu.VMEM((4, r, cols), BF16),
                        pltpu.SemaphoreType.DMA((4,)), pltpu.SemaphoreType.DMA((4,)), pltpu.SemaphoreType.DMA((4,)),
                        pltpu.SemaphoreType.DMA((4,))],
        compiler_params=_params(),
    )(blocks)


_HBM = pl.BlockSpec(memory_space=pltpu.HBM)
_SEM = pl.BlockSpec(memory_space=pltpu.SEMAPHORE)
_EFFECT = pltpu.SideEffectType.DATAFLOW_SIDE_EFFECTING


def _start_copies(bufs, copies, n_copies, after, name):
    nb = len(bufs)

    def body(*refs):
        for cp in copies(refs[:nb], refs[nb + 1], refs[nb + 2]):
            cp.start()
        refs[-1][...] = jnp.zeros_like(refs[-1])

    out = pl.pallas_call(
        body, name=name,
        out_shape=(pltpu.SemaphoreType.DMA((n_copies,)), pltpu.SemaphoreType.DMA((n_copies,)),
                   *[pltpu.HBM(b.shape, b.dtype) for b in bufs], jax.ShapeDtypeStruct((8, 128), F32)),
        in_specs=(_HBM,) * nb + (pl.BlockSpec(memory_space=pl.ANY),),
        out_specs=(_SEM, _SEM) + (_HBM,) * nb + (pl.BlockSpec(memory_space=pltpu.VMEM),),
        input_output_aliases={i: 2 + i for i in range(nb)},
        compiler_params=pltpu.CompilerParams(has_side_effects=_EFFECT),
    )(*[pltpu.with_memory_space_constraint(b, pltpu.HBM) for b in bufs], after)
    return out[0], out[1], list(out[2:2 + nb]), out[-1]


def _wait_copies(flight, copies, after, name):
    send_sems, recv_sems, bufs, _ = flight
    nb = len(bufs)

    def body(*refs):
        for cp in copies(refs[:nb], refs[nb], refs[nb + 1]):
            cp.wait_send()
            cp.wait_recv()

    return pl.pallas_call(
        body, name=name,
        out_shape=tuple(pltpu.HBM(b.shape, b.dtype) for b in bufs),
        in_specs=(_HBM,) * nb + (_SEM, _SEM, pl.BlockSpec(memory_space=pl.ANY)), out_specs=(_HBM,) * nb,
        input_output_aliases={i: i for i in range(nb)},
        compiler_params=pltpu.CompilerParams(has_side_effects=_EFFECT),
    )(*bufs, send_sems, recv_sems, after)


def _wait_then_start(flight, waited, started, n_started, after, name):
    old_send, old_recv, bufs, _ = flight
    nb = len(bufs)

    def body(*refs):
        for cp in waited(refs[:nb], refs[nb], refs[nb + 1]):
            cp.wait_send()
            cp.wait_recv()
        for cp in started(refs[:nb], refs[nb + 3], refs[nb + 4]):
            cp.start()
        refs[-1][...] = jnp.zeros_like(refs[-1])

    out = pl.pallas_call(
        body, name=name,
        out_shape=(pltpu.SemaphoreType.DMA((n_started,)), pltpu.SemaphoreType.DMA((n_started,)),
                   *[pltpu.HBM(b.shape, b.dtype) for b in bufs], jax.ShapeDtypeStruct((8, 128), F32)),
        in_specs=(_HBM,) * nb + (_SEM, _SEM, pl.BlockSpec(memory_space=pl.ANY)),
        out_specs=(_SEM, _SEM) + (_HBM,) * nb + (pl.BlockSpec(memory_space=pltpu.VMEM),),
        input_output_aliases={i: 2 + i for i in range(nb)},
        compiler_params=pltpu.CompilerParams(has_side_effects=_EFFECT),
    )(*bufs, old_send, old_recv, after)
    return out[0], out[1], list(out[2:2 + nb]), out[-1]


def _chip_copies(refs, send_sems, recv_sems):
    pair_ref, land_ref = refs
    x, y, c = _place()
    chips = [(1 - x, y), (x, 1 - y), (1 - x, 1 - y)]
    return [pltpu.make_async_remote_copy(
        src_ref=pair_ref.at[2 * chip[0] + chip[1]], dst_ref=land_ref.at[k],
        send_sem=send_sems.at[k], recv_sem=recv_sems.at[k],
        device_id=(*chip, c), device_id_type=MESH) for k, chip in enumerate(chips)]


def _first_hop_copies(refs, send_sems, recv_sems):
    pair_ref, land_ref = refs
    x, y, c = _place()
    first = ((x + 1 - c) % 2, (y + c) % 2)
    blocks = [2 * first[0] + first[1], 2 * (1 - x) + (1 - y)]
    return [pltpu.make_async_remote_copy(
        src_ref=pair_ref.at[blocks[k]], dst_ref=land_ref.at[k], send_sem=send_sems.at[k], recv_sem=recv_sems.at[k],
        device_id=(*first, c), device_id_type=MESH) for k in range(2)]


def _second_hop_copies(refs, send_sems, recv_sems):
    relay_ref, land_ref = refs
    x, y, c = _place()
    second = ((x + c) % 2, (y + 1 - c) % 2)
    return [pltpu.make_async_remote_copy(
        src_ref=relay_ref, dst_ref=land_ref.at[0], send_sem=send_sems.at[0], recv_sem=recv_sems.at[0],
        device_id=(*second, c), device_id_type=MESH)]


def _own_block_copies(targets):
    def copies(refs, send_sems, recv_sems):
        x, y, c = _place()
        mine = refs[0].at[4 * x + 2 * y + c]
        return [pltpu.make_async_remote_copy(
            src_ref=mine, dst_ref=mine, send_sem=send_sems.at[k], recv_sem=recv_sems.at[k],
            device_id=to, device_id_type=MESH) for k, to in enumerate(targets(x, y, c))]
    return copies


def _my_core_and_sibling(x, y, c):
    return [(x, y, 1 - c), (1 - x, y, c), (x, 1 - y, c), (1 - x, 1 - y, c)]


def _all_others(x, y, c):
    flip = lambda v, f: 1 - v if f else v
    return [(flip(x, r & 4), flip(y, r & 2), flip(c, r & 1)) for r in range(1, N_DEV)]


def _forward_copies(refs, send_sems, recv_sems):
    x, y, c = _place()
    chips = [(1 - x, y), (x, 1 - y), (1 - x, 1 - y)]
    return [pltpu.make_async_remote_copy(
        src_ref=refs[0].at[4 * chip[0] + 2 * chip[1] + c], dst_ref=refs[0].at[4 * chip[0] + 2 * chip[1] + c],
        send_sem=send_sems.at[k], recv_sem=recv_sems.at[k],
        device_id=(x, y, 1 - c), device_id_type=MESH) for k, chip in enumerate(chips)]


def _near_targets(x, y, c):
    return [(x, y, 1 - c), (1 - x, y, c), (x, 1 - y, c)]


def _second_stage_copies(refs, send_sems, recv_sems):
    x, y, c = _place()
    relayed = ((x + 1 - c) % 2, (y + c) % 2, c)
    relay_to = ((x + c) % 2, (y + 1 - c) % 2, c)
    plan = [((1 - x, y, c), (x, y, 1 - c)), ((x, 1 - y, c), (x, y, 1 - c)), (relayed, relay_to)]
    copies = []
    for k, ((px, py, pc), to) in enumerate(plan):
        blk = refs[0].at[4 * px + 2 * py + pc]
        copies.append(pltpu.make_async_remote_copy(
            src_ref=blk, dst_ref=blk, send_sem=send_sems.at[k], recv_sem=recv_sems.at[k],
            device_id=to, device_id_type=MESH))
    return copies


def _diagonal_forward_copies(refs, send_sems, recv_sems):
    x, y, c = _place()
    blk = refs[0].at[4 * (1 - x) + 2 * (1 - y) + c]
    return [pltpu.make_async_remote_copy(
        src_ref=blk, dst_ref=blk, send_sem=send_sems.at[0], recv_sem=recv_sems.at[0],
        device_id=(x, y, 1 - c), device_id_type=MESH)]


def _with_own_slot(block, me):
    return lax.dynamic_update_index_in_dim(lax.empty((N_DEV,) + block.shape, block.dtype), block, me, 0)


def _matmul(a, b, dims, out_dtype, tm, tn, name, dep=None):
    if dims == "nn":
        (m, k), n = a.shape, b.shape[1]
        a_spec = pl.BlockSpec((tm, k), lambda i, j: (i, 0))
        b_spec = pl.BlockSpec((k, tn), lambda i, j: (0, j))
        contract = ((1,), (0,))
    elif dims == "nt":
        (m, k), n = a.shape, b.shape[0]
        a_spec = pl.BlockSpec((tm, k), lambda i, j: (i, 0))
        b_spec = pl.BlockSpec((tn, k), lambda i, j: (j, 0))
        contract = ((1,), (1,))
    else:
        (k, m), n = a.shape, b.shape[1]
        a_spec = pl.BlockSpec((k, tm), lambda i, j: (0, i))
        b_spec = pl.BlockSpec((k, tn), lambda i, j: (0, j))
        contract = ((0,), (0,))
    assert m % tm == 0 and n % tn == 0 and a.dtype == BF16 and b.dtype == BF16

    def body(a_ref, b_ref, *rest):
        rest[-1][...] = lax.dot_general(a_ref[...], b_ref[...], (contract, ((), ())),
                                        preferred_element_type=F32).astype(out_dtype)

    deps = [] if dep is None else [dep]
    return pl.pallas_call(
        body, name=name, grid=(m // tm, n // tn),
        in_specs=[a_spec, b_spec] + [pl.BlockSpec((8, 128), lambda i, j: (0, 0))] * len(deps),
        out_specs=pl.BlockSpec((tm, tn), lambda i, j: (i, j)),
        out_shape=jax.ShapeDtypeStruct((m, n), out_dtype),
        compiler_params=_params(dimension_semantics=("arbitrary", "arbitrary")),
    )(a, b, *deps)


Z_TILE = 768
_Z_TILE_ORDER = ((0, 1, 2, 3, 4, 5, 6), (2, 0, 1, 6, 3, 4, 5), (4, 0, 5, 6, 1, 2, 3), (6, 2, 3, 4, 0, 1, 5))
_Z_EARLY_TILES = 4


def _z_proj(h, w_in_t, order, first, count, z_prev, name, dep=None):
    t = h.shape[0]

    def body(order_ref, h_ref, w_ref, *rest):
        rest[-1][...] = _dot_nt(h_ref[...], w_ref[...])

    prev = [] if z_prev is None else [z_prev]
    deps = [] if dep is None else [dep]
    return pl.pallas_call(
        body, name=name,
        grid_spec=pltpu.PrefetchScalarGridSpec(
            num_scalar_prefetch=1, grid=(count,),
            in_specs=[pl.BlockSpec((t, D_MODEL), lambda j, o: (0, 0)),
                      pl.BlockSpec((Z_TILE, D_MODEL), lambda j, o: (o[first + j], 0))]
            + [pl.BlockSpec(memory_space=pl.ANY)] * len(prev)
            + [pl.BlockSpec((8, 128), lambda j, o: (0, 0))] * len(deps),
            out_specs=pl.BlockSpec((t, Z_TILE), lambda j, o: (0, o[first + j]))),
        out_shape=jax.ShapeDtypeStruct((t, D_IN), F32),
        input_output_aliases={3: 0} if prev else {},
        compiler_params=_params(dimension_semantics=("arbitrary",)),
    )(order, h, w_in_t, *prev, *deps)


def _modulation(c_all, w_ada, b_ada_mine):
    def body(c_ref, w_ref, b_ref, act_ref, mod_ref):
        cv = c_ref[...]
        act = cv * _sigmoid(cv)
        act_ref[...] = act
        mod_ref[...] = jnp.dot(act.astype(BF16), w_ref[...].astype(BF16), preferred_element_type=F32) + b_ref[...]

    return pl.pallas_call(
        body, name="modulation",
        out_shape=(jax.ShapeDtypeStruct(c_all.shape, F32), jax.ShapeDtypeStruct((N_DEV, W_ADA_SHARD), F32)),
        compiler_params=_params(),
    )(c_all, w_ada, b_ada_mine)


def _modulated_norm(x, norm_g, scale, shift, tm=256):
    t, d = x.shape

    def body(x_ref, g_ref, sc_ref, sh_ref, h_ref):
        xv = x_ref[...]
        r = lax.rsqrt(jnp.mean(xv * xv, axis=-1, keepdims=True) + EPS)
        h = (xv * r) * g_ref[...] * (1.0 + sc_ref[...]) + sh_ref[...]
        h_ref[...] = h.astype(BF16)

    row = pl.BlockSpec((1, d), lambda i: (0, 0))
    return pl.pallas_call(
        body, name="modulated_norm", grid=(t // tm,),
        in_specs=[pl.BlockSpec((tm, d), lambda i: (i, 0)), row, row, row],
        out_specs=pl.BlockSpec((tm, d), lambda i: (i, 0)),
        out_shape=jax.ShapeDtypeStruct((t, d), BF16),
        compiler_params=_params(dimension_semantics=("arbitrary",)),
    )(x, norm_g, scale, shift)


def _window_bias(block_index):
    s = lax.broadcasted_iota(jnp.int32, (2 * BLOCK, BLOCK), 0)
    t = lax.broadcasted_iota(jnp.int32, (2 * BLOCK, BLOCK), 1)
    valid = ((s < BLOCK) & (s > t) & (block_index > 0)) | ((s >= BLOCK) & ((s - BLOCK) <= t))
    bias = jnp.where(valid, 0.0, -jnp.inf).astype(F32)
    return jnp.concatenate([bias] * 8, axis=1)


def _heads_t(pair_blocks, g):
    top = lax.broadcasted_iota(jnp.int32, (BLOCK, BLOCK), 0) < HEAD_DIM
    zeros = jnp.zeros((HEAD_DIM, BLOCK), F32)
    tiles = []
    for blk in pair_blocks:
        tp = blk.T
        if g == 0:
            tiles += [jnp.where(top, tp, 0.0), jnp.concatenate([tp[HEAD_DIM:], zeros], axis=0)]
        else:
            tiles += [jnp.concatenate([zeros, tp[:HEAD_DIM]], axis=0), jnp.where(top, 0.0, tp)]
    return jnp.concatenate(tiles, axis=1)


def _pair_block(xt, p, g):
    r0 = HEAD_DIM * g
    even = xt[r0:r0 + HEAD_DIM, (2 * p) * BLOCK:(2 * p + 1) * BLOCK]
    odd = xt[r0:r0 + HEAD_DIM, (2 * p + 1) * BLOCK:(2 * p + 2) * BLOCK]
    return jnp.concatenate([even, odd], axis=0).T


def _softmax_t(scores_t, bias, sink):
    st = scores_t + bias
    m = jnp.maximum(jnp.max(st, axis=0, keepdims=True), sink)
    e = jnp.exp(st - m)
    es = jnp.exp(sink - m)
    inv = 1.0 / (jnp.sum(e, axis=0, keepdims=True) + es)
    return e * inv, es * inv


def _dot(a, b):
    return jnp.dot(a, b, preferred_element_type=F32)


def _dot_nt(a, b):
    return lax.dot_general(a, b, (((1,), (1,)), ((), ())), preferred_element_type=F32)


def _layer_norm_fwd(v):
    mu = jnp.mean(v, axis=-1, keepdims=True)
    xc = v - mu
    rstd = lax.rsqrt(jnp.mean(xc * xc, axis=-1, keepdims=True) + EPS)
    return xc * rstd, rstd


def _tril(transposed=False):
    t = lax.broadcasted_iota(jnp.int32, (BLOCK, BLOCK), 0)
    s = lax.broadcasted_iota(jnp.int32, (BLOCK, BLOCK), 1)
    return s >= t if transposed else t >= s


def _const_spec(shape):
    return pl.BlockSpec(shape, lambda i: (0,) * len(shape))


def _kv_prev_spec(index):
    return pl.BlockSpec((BLOCK, 2 * D_KV), lambda i: (jnp.maximum(index(i) - 1, 0), SEG_KV // (2 * D_KV)))


def _keys_values(z_ref, kvp_ref):
    kvp, kvc = kvp_ref[...], z_ref[:, SEG_KV:SEG_KV + 2 * D_KV]
    kk = jnp.concatenate([kvp[:, :D_KV], kvc[:, :D_KV]], axis=0)
    vv = jnp.concatenate([kvp[:, D_KV:], kvc[:, D_KV:]], axis=0)
    return kk, vv


def _pair_cols(g, p, base=0):
    return slice(base + (4 * g + p) * 128, base + (4 * g + p + 1) * 128)


def _mixer_fwd(z, sink_rows, ln_g, ln_b, sgu_w, sgu_bt):
    t = z.shape[0]

    def body(z_ref, kvp_ref, sink_ref, lng_ref, lnb_ref, w_ref, bt_ref, a_ref):
        bias = _window_bias(pl.program_id(0))
        kk, vv = _keys_values(z_ref, kvp_ref)
        kk_b, vvt_b = kk.astype(BF16), vv.T.astype(BF16)
        for g in range(2):
            qt = _heads_t([z_ref[:, _pair_cols(g, p, SEG_Q)] * ATTN_SCALE for p in range(4)], g).astype(BF16)
            prob, _ = _softmax_t(_dot(kk_b, qt), bias, sink_ref[g])
            ot = _dot(vvt_b, prob.astype(BF16))
            for p in range(4):
                gate = z_ref[:, _pair_cols(g, p, SEG_GA)]
                a_ref[:, _pair_cols(g, p)] = (_pair_block(ot, p, g) * (gate * _sigmoid(gate))).astype(BF16)

        vhat, _ = _layer_norm_fwd(z_ref[:, SEG_VS:SEG_VS + D_SGU])
        vn = vhat * lng_ref[...] + lnb_ref[...]
        tril = _tril()
        for g in range(SGU_GROUPS):
            cols = slice(g * 128, (g + 1) * 128)
            wm = jnp.where(tril, w_ref[g], 0.0).astype(BF16)
            mixed = _dot(wm, vn[:, cols].astype(BF16)) + bt_ref[:, g:g + 1]
            gate = z_ref[:, SEG_GS + g * 128:SEG_GS + (g + 1) * 128]
            a_ref[:, D_ATTN + g * 128:D_ATTN + (g + 1) * 128] = (
                (z_ref[:, SEG_U + g * 128:SEG_U + (g + 1) * 128] * mixed) * (gate * _sigmoid(gate))).astype(BF16)

    return pl.pallas_call(
        body, name="mixer_fwd", grid=(t // BLOCK,),
        in_specs=[pl.BlockSpec((BLOCK, D_IN), lambda i: (i, 0)), _kv_prev_spec(lambda i: i),
                  _const_spec((2, 1, 8 * BLOCK)), _const_spec((1, D_SGU)), _const_spec((1, D_SGU)),
                  _const_spec((SGU_GROUPS, BLOCK, BLOCK)), _const_spec((BLOCK, SGU_GROUPS))],
        out_specs=pl.BlockSpec((BLOCK, D_MODEL), lambda i: (i, 0)),
        out_shape=jax.ShapeDtypeStruct((t, D_MODEL), BF16),
        compiler_params=_params(dimension_semantics=("arbitrary",)),
    )(z, z, sink_rows, ln_g, ln_b, sgu_w, sgu_bt)


def _mixer_bwd(z, da, sink_rows, ln_g, ln_b, sgu_w, sgu_wt, sgu_bt):
    t = z.shape[0]
    nb = t // BLOCK

    def body(z_ref, kvp_ref, da_ref, sink_ref, lng_ref, lnb_ref, w_ref, wt_ref, bt_ref,
             dz_ref, dsink_ref, dw_ref, db_ref, dlng_ref, dlnb_ref, carry_ref, dsink_acc, dbt_acc):
        step = pl.program_id(0)

        @pl.when(step == 0)
        def _():
            carry_ref[...] = jnp.zeros_like(carry_ref)
            dsink_acc[...] = jnp.zeros_like(dsink_acc)
            dbt_acc[...] = jnp.zeros_like(dbt_acc)
            dw_ref[...] = jnp.zeros_like(dw_ref)
            dlng_ref[...] = jnp.zeros_like(dlng_ref)
            dlnb_ref[...] = jnp.zeros_like(dlnb_ref)

        bias = _window_bias(nb - 1 - step)
        kk, vv = _keys_values(z_ref, kvp_ref)
        kk_b, vv_b = kk.astype(BF16), vv.astype(BF16)
        kkt_b, vvt_b = kk.T.astype(BF16), vv.T.astype(BF16)
        dkk = jnp.zeros((2 * BLOCK, D_KV), F32)
        dvv = jnp.zeros((2 * BLOCK, D_KV), F32)
        for g in range(2):
            qt = _heads_t([z_ref[:, _pair_cols(g, p, SEG_Q)] * ATTN_SCALE for p in range(4)], g).astype(BF16)
            prob, sink_prob = _softmax_t(_dot(kk_b, qt), bias, sink_ref[g])
            prob_b = prob.astype(BF16)
            ot = _dot(vvt_b, prob_b)
            gates = [z_ref[:, _pair_cols(g, p, SEG_GA)] for p in range(4)]
            sig = [_sigmoid(gt) for gt in gates]
            d_attn = [da_ref[:, _pair_cols(g, p)] for p in range(4)]
            d_ot = _heads_t([d_attn[p] * (gates[p] * sig[p]) for p in range(4)], g).astype(BF16)
            d_prob = _dot(vv_b, d_ot)
            delta = jnp.sum(prob * d_prob, axis=0, keepdims=True)
            d_scores = (prob * (d_prob - delta)).astype(BF16)
            dsink_acc[g] -= sink_prob * delta
            d_qt = _dot(kkt_b, d_scores)
            dkk = dkk + _dot_nt(d_scores, qt)
            dvv = dvv + _dot_nt(prob_b, d_ot)
            for p in range(4):
                dz_ref[:, _pair_cols(g, p, SEG_Q)] = (_pair_block(d_qt, p, g) * ATTN_SCALE).astype(BF16)
                d_silu = sig[p] * (1.0 + gates[p] * (1.0 - sig[p]))
                dz_ref[:, _pair_cols(g, p, SEG_GA)] = (d_attn[p] * _pair_block(ot, p, g) * d_silu).astype(BF16)
        d_kv = jnp.concatenate([dkk, dvv], axis=1)
        dz_ref[:, SEG_KV:SEG_KV + 2 * D_KV] = (d_kv[BLOCK:] + carry_ref[...]).astype(BF16)
        carry_ref[...] = d_kv[:BLOCK]

        vhat, rstd = _layer_norm_fwd(z_ref[:, SEG_VS:SEG_VS + D_SGU])
        lng = lng_ref[...]
        vn = vhat * lng + lnb_ref[...]
        tril, triu = _tril(), _tril(transposed=True)
        lane = lax.broadcasted_iota(jnp.int32, (BLOCK, 128), 1)
        d_bt = jnp.zeros((BLOCK, 128), F32)
        d_vn = []
        for g in range(SGU_GROUPS):
            cols = slice(g * 128, (g + 1) * 128)
            wm = jnp.where(tril, w_ref[g], 0.0).astype(BF16)
            wmt = jnp.where(triu, wt_ref[g], 0.0).astype(BF16)
            vn_g = vn[:, cols].astype(BF16)
            mixed = _dot(wm, vn_g) + bt_ref[:, g:g + 1]
            gate = z_ref[:, SEG_GS + g * 128:SEG_GS + (g + 1) * 128]
            u = z_ref[:, SEG_U + g * 128:SEG_U + (g + 1) * 128]
            d_out = da_ref[:, D_ATTN + g * 128:D_ATTN + (g + 1) * 128]
            sg = _sigmoid(gate)
            d_um = d_out * (gate * sg)
            dz_ref[:, SEG_U + g * 128:SEG_U + (g + 1) * 128] = (d_um * mixed).astype(BF16)
            dz_ref[:, SEG_GS + g * 128:SEG_GS + (g + 1) * 128] = (
                d_out * (u * mixed) * (sg * (1.0 + gate * (1.0 - sg)))).astype(BF16)
            d_mixed = d_um * u
            d_mixed_b = d_mixed.astype(BF16)
            dw_ref[g] += jnp.where(tril, _dot_nt(d_mixed_b, vn_g), 0.0)
            d_bt = d_bt + jnp.where(lane == g, jnp.sum(d_mixed, axis=-1, keepdims=True), 0.0)
            d_vn.append(_dot(wmt, d_mixed_b))
        dbt_acc[...] += d_bt
        d_vn = jnp.concatenate(d_vn, axis=1)
        dlng_ref[...] += jnp.sum(d_vn * vhat, axis=0, keepdims=True)
        dlnb_ref[...] += jnp.sum(d_vn, axis=0, keepdims=True)
        d_vhat = d_vn * lng
        d_v = rstd * (d_vhat - jnp.mean(d_vhat, axis=-1, keepdims=True)
                      - vhat * jnp.mean(d_vhat * vhat, axis=-1, keepdims=True))
        dz_ref[:, SEG_VS:SEG_VS + D_SGU] = d_v.astype(BF16)

        @pl.when(step == nb - 1)
        def _():
            db_ref[...] = dbt_acc[...].T[:SGU_GROUPS]
            lane_row = lax.broadcasted_iota(jnp.int32, (1, 128), 1)
            d_sink = jnp.zeros((1, 128), F32)
            for g in range(2):
                acc = dsink_acc[g]
                for j in range(8):
                    head_sum = jnp.sum(acc[:, j * BLOCK:(j + 1) * BLOCK], axis=-1, keepdims=True)
                    d_sink = d_sink + jnp.where(lane_row == 8 * g + j, head_sum, 0.0)
            dsink_ref[...] = d_sink

    rev = lambda i: nb - 1 - i
    return pl.pallas_call(
        body, name="mixer_bwd", grid=(nb,),
        in_specs=[pl.BlockSpec((BLOCK, D_IN), lambda i: (rev(i), 0)), _kv_prev_spec(rev),
                  pl.BlockSpec((BLOCK, D_MODEL), lambda i: (rev(i), 0)),
                  _const_spec((2, 1, 8 * BLOCK)), _const_spec((1, D_SGU)), _const_spec((1, D_SGU)),
                  _const_spec((SGU_GROUPS, BLOCK, BLOCK)), _const_spec((SGU_GROUPS, BLOCK, BLOCK)),
                  _const_spec((BLOCK, SGU_GROUPS))],
        out_specs=(pl.BlockSpec((BLOCK, D_IN), lambda i: (rev(i), 0)), _const_spec((1, 128)),
                   _const_spec((SGU_GROUPS, BLOCK, BLOCK)), _const_spec((SGU_GROUPS, BLOCK)),
                   _const_spec((1, D_SGU)), _const_spec((1, D_SGU))),
        out_shape=(jax.ShapeDtypeStruct((t, D_IN), BF16), jax.ShapeDtypeStruct((1, 128), F32),
                   jax.ShapeDtypeStruct((SGU_GROUPS, BLOCK, BLOCK), F32), jax.ShapeDtypeStruct((SGU_GROUPS, BLOCK), F32),
                   jax.ShapeDtypeStruct((1, D_SGU), F32), jax.ShapeDtypeStruct((1, D_SGU), F32)),
        scratch_shapes=[pltpu.VMEM((BLOCK, 2 * D_KV), F32), pltpu.VMEM((2, 1, 8 * BLOCK), F32),
                        pltpu.VMEM((BLOCK, 128), F32)],
        compiler_params=_params(dimension_semantics=("arbitrary",)),
    )(z, z, da, sink_rows, ln_g, ln_b, sgu_w, sgu_wt, sgu_bt)


def _head(y, x, target, gate, final_g, tm=256):
    t, d = x.shape

    def body(y_ref, x_ref, tg_ref, gate_ref, fg_ref, dx2_ref, dy_ref, loss_ref, dfg_ref, dgate_ref):
        @pl.when(pl.program_id(0) == 0)
        def _():
            loss_ref[...] = jnp.zeros_like(loss_ref)
            dfg_ref[...] = jnp.zeros_like(dfg_ref)
            dgate_ref[...] = jnp.zeros_like(dgate_ref)

        yv, gate, fg = y_ref[...], gate_ref[...], fg_ref[...]
        x2 = x_ref[...] + gate * yv
        r2 = lax.rsqrt(jnp.mean(x2 * x2, axis=-1, keepdims=True) + EPS)
        nrm = x2 * r2
        err = nrm * fg - tg_ref[...]
        loss_ref[...] += 0.5 * jnp.sum(jnp.mean(err * err, axis=-1, keepdims=True), axis=0, keepdims=True)
        d_out = err * (1.0 / d)
        dfg_ref[...] += jnp.sum(d_out * nrm, axis=0, keepdims=True)
        d_nrm = d_out * fg
        dx2 = r2 * (d_nrm - nrm * jnp.mean(d_nrm * nrm, axis=-1, keepdims=True))
        dx2_ref[...] = dx2
        dgate_ref[...] += jnp.sum(dx2 * yv, axis=0, keepdims=True)
        dy_ref[...] = (dx2 * gate).astype(BF16)

    blk = pl.BlockSpec((tm, d), lambda i: (i, 0))
    row = _const_spec((1, d))
    return pl.pallas_call(
        body, name="head", grid=(t // tm,),
        in_specs=[blk, blk, blk, row, row],
        out_specs=(blk, blk, _const_spec((1, 128)), row, row),
        out_shape=(jax.ShapeDtypeStruct((t, d), F32), jax.ShapeDtypeStruct((t, d), BF16),
                   jax.ShapeDtypeStruct((1, 128), F32), jax.ShapeDtypeStruct((1, d), F32),
                   jax.ShapeDtypeStruct((1, d), F32)),
        compiler_params=_params(dimension_semantics=("arbitrary",)),
    )(y, x, target, gate, final_g)


def _modulated_norm_bwd(dh, x, dx2, norm_g, scale, tm=256):
    t, d = x.shape

    def body(dh_ref, x_ref, dx2_ref, g_ref, sc_ref, gx_ref, dshift_ref, dscale_ref, dg_ref):
        @pl.when(pl.program_id(0) == 0)
        def _():
            dshift_ref[...] = jnp.zeros_like(dshift_ref)
            dscale_ref[...] = jnp.zeros_like(dscale_ref)
            dg_ref[...] = jnp.zeros_like(dg_ref)

        dh, xv, g = dh_ref[...], x_ref[...], g_ref[...]
        one_plus = 1.0 + sc_ref[...]
        r = lax.rsqrt(jnp.mean(xv * xv, axis=-1, keepdims=True) + EPS)
        xn = xv * r
        dshift_ref[...] += jnp.sum(dh, axis=0, keepdims=True)
        dscale_ref[...] += jnp.sum(dh * (xn * g), axis=0, keepdims=True)
        d_y = dh * one_plus
        dg_ref[...] += jnp.sum(d_y * xn, axis=0, keepdims=True)
        d_xn = d_y * g
        gx_ref[...] = dx2_ref[...] + r * (d_xn - xn * jnp.mean(d_xn * xn, axis=-1, keepdims=True))

    blk = pl.BlockSpec((tm, d), lambda i: (i, 0))
    row = _const_spec((1, d))
    return pl.pallas_call(
        body, name="modulated_norm_bwd", grid=(t // tm,),
        in_specs=[blk, blk, blk, row, row], out_specs=(blk, row, row, row),
        out_shape=(jax.ShapeDtypeStruct((t, d), F32),) + (jax.ShapeDtypeStruct((1, d), F32),) * 3,
        compiler_params=_params(dimension_semantics=("arbitrary",)),
    )(dh, x, dx2, norm_g, scale)


def _adamw(w, g, m, v):
    m = ADAM_B1 * m + (1.0 - ADAM_B1) * g
    v = ADAM_B2 * v + (1.0 - ADAM_B2) * (g * g)
    m_hat = m / (1.0 - ADAM_B1 ** ADAM_STEP)
    v_hat = v / (1.0 - ADAM_B2 ** ADAM_STEP)
    delta = -ADAM_LR * (m_hat / (jnp.sqrt(v_hat) + ADAM_EPS) + ADAM_WD * w)
    return delta, m, v


def _relay_sum(second_chip, pair, land, tr):
    _, r, c = pair.shape

    def body(chip_ref, a_ref, b_ref, o_ref):
        o_ref[...] = (a_ref[...].astype(F32) + b_ref[...].astype(F32)).astype(BF16)

    return pl.pallas_call(
        body, name="w_in_grad_relay_sum",
        grid_spec=pltpu.PrefetchScalarGridSpec(
            num_scalar_prefetch=1, grid=(r // tr,),
            in_specs=[pl.BlockSpec((None, tr, c), lambda i, chip_ref: (chip_ref[0], i, 0)),
                      pl.BlockSpec((None, tr, c), lambda i, chip_ref: (1, i, 0))],
            out_specs=pl.BlockSpec((tr, c), lambda i, chip_ref: (i, 0))),
        out_shape=jax.ShapeDtypeStruct((r, c), BF16),
        compiler_params=_params(dimension_semantics=("arbitrary",)),
    )(second_chip, pair, land)


def _adam_from_chips(chip, pair, landed, w, m, v, name, tc):
    _, r, c = pair.shape
    n = len(landed)

    def body(chip_ref, own_ref, *refs):
        w_ref, m_ref, v_ref, g_ref, d_ref, nm_ref, nv_ref = refs[n:]
        g = own_ref[...].astype(F32)
        for k in range(n):
            g = g + refs[k][...].astype(F32)
        g_ref[...] = g
        d_ref[...], nm_ref[...], nv_ref[...] = _adamw(w_ref[...], g, m_ref[...], v_ref[...])

    def landed_spec(index):
        return pl.BlockSpec((None, r, tc), lambda i, chip_ref: (index, 0, i))

    blk = pl.BlockSpec((r, tc), lambda i, chip_ref: (0, i))
    return pl.pallas_call(
        body, name=name,
        grid_spec=pltpu.PrefetchScalarGridSpec(
            num_scalar_prefetch=1, grid=(c // tc,),
            in_specs=[pl.BlockSpec((None, r, tc), lambda i, chip_ref: (chip_ref[0], 0, i))]
            + [landed_spec(index) for _, index in landed] + [blk, blk, blk],
            out_specs=(blk,) * 4),
        out_shape=(jax.ShapeDtypeStruct((r, c), F32),) * 4,
        compiler_params=_params(dimension_semantics=("arbitrary",)),
    )(chip, pair, *[array for array, _ in landed], w, m, v)


def _adam_w_ada(act_t, dmod_mine, w, m, v, tr=256):
    r, c = w.shape

    def body(a_ref, dm_ref, w_ref, m_ref, v_ref, g_ref, d_ref, nm_ref, nv_ref):
        g = _dot(a_ref[...].astype(BF16), dm_ref[...].astype(BF16))
        g_ref[...] = g
        d_ref[...], nm_ref[...], nv_ref[...] = _adamw(w_ref[...], g, m_ref[...], v_ref[...])

    blk = pl.BlockSpec((tr, c), lambda i: (i, 0))
    return pl.pallas_call(
        body, name="adam_w_ada", grid=(r // tr,),
        in_specs=[pl.BlockSpec((tr, N_DEV), lambda i: (i, 0)), _const_spec((N_DEV, c)), blk, blk, blk],
        out_specs=(blk,) * 4, out_shape=(jax.ShapeDtypeStruct((r, c), F32),) * 4,
        compiler_params=_params(dimension_semantics=("arbitrary",)),
    )(act_t, dmod_mine, w, m, v)


def _pack_small(d_shift, d_scale, d_gate, d_norm_g, d_final_g, d_ln_g, d_ln_b, loss, d_sinks, d_sgu_b):
    def body(shift_ref, scale_ref, gate_ref, ng_ref, fg_ref, lng_ref, lnb_ref, loss_ref, sink_ref, b_ref, o_ref):
        o_ref[...] = jnp.zeros_like(o_ref)
        o_ref[ROW_SHIFT:ROW_SHIFT + 1, :] = shift_ref[...]
        o_ref[ROW_SCALE:ROW_SCALE + 1, :] = scale_ref[...]
        o_ref[ROW_GATE:ROW_GATE + 1, :] = gate_ref[...]
        o_ref[ROW_NORM_G:ROW_NORM_G + 1, :] = ng_ref[...]
        o_ref[ROW_FINAL_G:ROW_FINAL_G + 1, :] = fg_ref[...]
        o_ref[ROW_LN:ROW_LN + 1, 0:D_SGU] = lng_ref[...]
        o_ref[ROW_LN:ROW_LN + 1, D_SGU:2 * D_SGU] = lnb_ref[...]
        o_ref[ROW_MISC:ROW_MISC + 1, 0:128] = loss_ref[...]
        o_ref[ROW_MISC:ROW_MISC + 1, 128:256] = sink_ref[...]
        o_ref[ROW_SGU_B:ROW_SGU_B + SGU_GROUPS, 0:BLOCK] = b_ref[...]

    return pl.pallas_call(
        body, name="pack_small", out_shape=jax.ShapeDtypeStruct((SMALL_ROWS, D_MODEL), F32),
        compiler_params=_params(),
    )(d_shift, d_scale, d_gate, d_norm_g, d_final_g, d_ln_g, d_ln_b, loss, d_sinks, d_sgu_b)


_SMALL_NAMES = ("norm_g", "b_ada", "attn_sinks", "sgu_ln_g", "sgu_ln_b", "sgu_w", "sgu_b", "final_g")


def _adam_small(partials, d_sgu_w_all, weights, moments_m, moments_v):
    names = _SMALL_NAMES
    k = len(names)

    def body(*refs):
        p_ref, sw_ref = refs[0], refs[1]
        w_refs, m_refs, v_refs = refs[2:2 + k], refs[2 + k:2 + 2 * k], refs[2 + 2 * k:2 + 3 * k]
        loss_ref, dmod_ref = refs[2 + 3 * k], refs[3 + 3 * k]
        out_refs = refs[4 + 3 * k:4 + 7 * k]
        sum_ref = refs[4 + 7 * k]
        total = p_ref[0]
        for j in range(1, N_DEV):
            total = total + p_ref[j]
        sum_ref[...] = total
        for j in range(N_DEV):
            for part, row in enumerate((ROW_SHIFT, ROW_SCALE, ROW_GATE)):
                dmod_ref[j:j + 1, part * D_MODEL:(part + 1) * D_MODEL] = p_ref[j, row:row + 1, :]
        loss_ref[...] = sum_ref[ROW_MISC:ROW_MISC + 1, 0:1]
        d_sgu_w = sw_ref[0]
        for j in range(1, N_DEV):
            d_sgu_w = d_sgu_w + sw_ref[j]
        grads = {
            "norm_g": sum_ref[ROW_NORM_G:ROW_NORM_G + 1, :],
            "b_ada": jnp.concatenate([sum_ref[r:r + 1, :] for r in (ROW_SHIFT, ROW_SCALE, ROW_GATE)], axis=1),
            "attn_sinks": sum_ref[ROW_MISC:ROW_MISC + 1, 128:128 + N_Q_HEADS],
            "sgu_ln_g": sum_ref[ROW_LN:ROW_LN + 1, 0:D_SGU],
            "sgu_ln_b": sum_ref[ROW_LN:ROW_LN + 1, D_SGU:2 * D_SGU],
            "sgu_w": d_sgu_w[None],
            "sgu_b": sum_ref[ROW_SGU_B:ROW_SGU_B + SGU_GROUPS, 0:BLOCK][None],
            "final_g": sum_ref[ROW_FINAL_G:ROW_FINAL_G + 1, :],
        }
        for i, name in enumerate(names):
            g = grads[name]
            delta, m, v = _adamw(w_refs[i][...], g, m_refs[i][...], v_refs[i][...])
            out_refs[4 * i][...] = g
            out_refs[4 * i + 1][...] = delta
            out_refs[4 * i + 2][...] = m
            out_refs[4 * i + 3][...] = v

    shapes = [jax.ShapeDtypeStruct((1, 1), F32), jax.ShapeDtypeStruct((N_DEV, 3 * D_MODEL), F32)]
    for name in names:
        shapes += [jax.ShapeDtypeStruct(weights[name].shape, F32)] * 4
    outs = pl.pallas_call(
        body, name="adam_small", out_shape=tuple(shapes),
        scratch_shapes=[pltpu.VMEM((SMALL_ROWS, D_MODEL), F32)],
        compiler_params=_params(),
    )(partials, d_sgu_w_all, *[weights[n] for n in names], *[moments_m[n] for n in names],
      *[moments_v[n] for n in names])
    return outs[0], outs[1], {name: outs[2 + 4 * i:6 + 4 * i] for i, name in enumerate(names)}


def kernel(x, c, norm_g, w_ada, b_ada, w_in, attn_sinks, sgu_ln_g, sgu_ln_b, sgu_w, sgu_b, w_out, final_g, loss_target, m_norm_g, m_w_ada, m_b_ada, m_w_in, m_attn_sinks, m_sgu_ln_g, m_sgu_ln_b, m_sgu_w, m_sgu_b, m_w_out, m_final_g, v_norm_g, v_w_ada, v_b_ada, v_w_in, v_attn_sinks, v_sgu_ln_g, v_sgu_ln_b, v_sgu_w, v_sgu_b, v_w_out, v_final_g):
    xi, yi, ci = _place()
    me = 4 * xi + 2 * yi + ci
    x2d, target = x[0], loss_target[0]
    t = x2d.shape[0]

    core = ci.astype(jnp.int32).reshape(1)
    chip = (2 * xi + yi).astype(jnp.int32).reshape(1)

    c_all = _all_gather([c.reshape(8, 256)], "gather_c", True)[0].reshape(N_DEV, D_MODEL)
    b_mine = lax.dynamic_slice(b_ada, (0, me * W_ADA_SHARD), (1, W_ADA_SHARD))
    c_act, mod_part = _modulation(c_all, w_ada[0], b_mine)
    mod_all = _all_gather([mod_part], "gather_mod", True)[0]

    near = _own_block_copies(_near_targets)
    w_in_flight = _start_copies([_with_own_slot(w_in[0].T.astype(BF16), me)], near, 3, mod_all, "gather_w_in_start")
    mod = lax.dynamic_index_in_dim(mod_all, me, axis=1, keepdims=False).reshape(1, 3 * D_MODEL)
    mod = mod + w_in_flight[3][0, 0]
    shift, scale, gate = mod[:, :D_MODEL], mod[:, D_MODEL:2 * D_MODEL], mod[:, 2 * D_MODEL:]
    h = _modulated_norm(x2d, norm_g, scale, shift)

    w_in_pair = _wait_copies(w_in_flight, lambda *a: near(*a)[:1], h, "gather_w_in_sibling_wait")
    tile_order = jnp.asarray(_Z_TILE_ORDER, jnp.int32)[chip[0]]
    z_own = _z_proj(h, w_in_pair[0].reshape(D_IN, D_MODEL), tile_order, 0, 1, None, "z_proj_own")
    w_in_flight = _wait_then_start((w_in_flight[0], w_in_flight[1], w_in_pair, None), lambda *a: near(*a)[1:],
                                   _second_stage_copies, 3, z_own, "gather_w_in_second_stage")
    w_in_most = _wait_copies(w_in_flight, lambda *a: _second_stage_copies(*a)[:2], z_own, "gather_w_in_forward_wait")
    z_early = _z_proj(h, w_in_most[0].reshape(D_IN, D_MODEL), tile_order, 1, _Z_EARLY_TILES - 1, z_own, "z_proj_early")
    w_out_flight = _start_copies([_with_own_slot(w_out[0].astype(BF16), me)], _own_block_copies(_my_core_and_sibling),
                                 4, z_early, "gather_w_out_start")
    w_in_flight = _wait_then_start((w_in_flight[0], w_in_flight[1], w_in_most, None),
                                   lambda *a: _second_stage_copies(*a)[2:], _diagonal_forward_copies, 1,
                                   w_out_flight[3], "gather_w_in_last_stage")
    w_in_all = _wait_copies(w_in_flight, _diagonal_forward_copies, z_early, "gather_w_in_last_wait")[0]
    w_in_t = w_in_all.reshape(D_IN, D_MODEL)
    z = _z_proj(h, w_in_t, tile_order, _Z_EARLY_TILES, 7 - _Z_EARLY_TILES, z_early, "z_proj_late")
    w_out_flight = _wait_then_start(w_out_flight, _own_block_copies(_my_core_and_sibling), _forward_copies, 3, z,
                                    "gather_w_out_forward_stage")
    sink_rows = jnp.repeat(attn_sinks.reshape(N_Q_HEADS), BLOCK).reshape(2, 1, 8 * BLOCK)
    sgu_bt = sgu_b[0].T
    a = _mixer_fwd(z, sink_rows + w_out_flight[3][0, 0], sgu_ln_g, sgu_ln_b, sgu_w[0], sgu_bt)
    w_out_all = _wait_copies(w_out_flight, _forward_copies, a, "gather_w_out_forward_wait")[0]
    w_out_full = w_out_all.reshape(D_MODEL, D_MODEL)
    y = _matmul(a, w_out_full, "nn", F32, min(t, 1024), 1024, "out_proj")
    final_g_row = final_g.reshape(1, D_MODEL)
    dx2, dy, loss_part, d_final_g, d_gate = _head(y, x2d, target, gate, final_g_row)

    da = _matmul(dy, w_out_full, "nt", F32, min(t, 1024), 1024, "out_proj_bwd")
    dw_out = _matmul(a, dy, "tn", BF16, 1024, 1024, "w_out_grad").reshape(4, 2, W_OUT_SHARD, D_MODEL)
    pair_out = _pair_reduce(dw_out, "w_out_grad_pair_reduce", W_OUT_SHARD // 2)
    out_flight = _start_copies([pair_out, lax.empty((3, W_OUT_SHARD, D_MODEL), BF16)], _chip_copies, 3, core,
                               "w_out_grad_chip_start")
    dz, d_sinks, d_sgu_w, d_sgu_b, d_ln_g, d_ln_b = _mixer_bwd(
        z, da, sink_rows + out_flight[3][0, 0], sgu_ln_g, sgu_ln_b, sgu_w[0], jnp.swapaxes(sgu_w[0], 1, 2), sgu_bt)
    sgu_w_flight = _start_copies([_with_own_slot(d_sgu_w, me)], _own_block_copies(_all_others), N_DEV - 1, core,
                                 "sgu_w_grad_gather_start")
    dw_in_t = _matmul(dz, h, "tn", BF16, 768, D_MODEL, "w_in_grad", dep=sgu_w_flight[3])
    dw_in_t = dw_in_t.reshape(4, 2, W_IN_SHARD, D_MODEL)
    pair_in = _pair_reduce(dw_in_t, "w_in_grad_pair_reduce", W_IN_SHARD // 3)
    hop1 = _start_copies([pair_in, lax.empty((2, W_IN_SHARD, D_MODEL), BF16)], _first_hop_copies, 2, core,
                         "w_in_grad_first_hop_start")
    dh = _matmul(dz, w_in_t, "nn", F32, min(t, 1024), 512, "z_proj_bwd", dep=hop1[3])
    pair_in, land_first = _wait_copies(hop1, _first_hop_copies, dh, "w_in_grad_first_hop_wait")
    second_chip = (2 * ((xi + ci) % 2) + (yi + 1 - ci) % 2).astype(jnp.int32).reshape(1)
    relay = _relay_sum(second_chip, pair_in, land_first, W_IN_SHARD // 3)
    hop2 = _start_copies([relay, lax.empty((1, W_IN_SHARD, D_MODEL), BF16)], _second_hop_copies, 1, core,
                         "w_in_grad_second_hop_start")
    grad_x, d_shift, d_scale, d_norm_g = _modulated_norm_bwd(dh, x2d, dx2, norm_g, scale + hop2[3][0, 0])

    partial = _pack_small(d_shift, d_scale, d_gate, d_norm_g, d_final_g, d_ln_g, d_ln_b, loss_part, d_sinks, d_sgu_b)
    partial_all = _all_gather([partial], "gather_small", True)[0]
    d_sgu_w_all = _wait_copies(sgu_w_flight, _own_block_copies(_all_others), partial_all, "sgu_w_grad_gather_wait")[0]
    weights = {"norm_g": norm_g, "b_ada": b_ada, "attn_sinks": attn_sinks, "sgu_ln_g": sgu_ln_g,
               "sgu_ln_b": sgu_ln_b, "sgu_w": sgu_w, "sgu_b": sgu_b, "final_g": final_g_row}
    moments_m = {"norm_g": m_norm_g, "b_ada": m_b_ada, "attn_sinks": m_attn_sinks, "sgu_ln_g": m_sgu_ln_g,
                 "sgu_ln_b": m_sgu_ln_b, "sgu_w": m_sgu_w, "sgu_b": m_sgu_b,
                 "final_g": m_final_g.reshape(1, D_MODEL)}
    moments_v = {"norm_g": v_norm_g, "b_ada": v_b_ada, "attn_sinks": v_attn_sinks, "sgu_ln_g": v_sgu_ln_g,
                 "sgu_ln_b": v_sgu_ln_b, "sgu_w": v_sgu_w, "sgu_b": v_sgu_b,
                 "final_g": v_final_g.reshape(1, D_MODEL)}
    loss, dmod_all, small = _adam_small(partial_all, d_sgu_w_all, weights, moments_m, moments_v)
    small["final_g"] = tuple(o.reshape(D_MODEL) for o in small["final_g"])

    dmod_mine = lax.dynamic_slice(dmod_all, (0, me * W_ADA_SHARD), (N_DEV, W_ADA_SHARD))
    big = {"w_ada": _adam_w_ada(c_act.T, dmod_mine, w_ada[0], m_w_ada[0], v_w_ada[0])}
    pair_out, land_out = _wait_copies(out_flight, _chip_copies, big["w_ada"][0], "w_out_grad_chip_wait")
    big["w_out"] = _adam_from_chips(chip, pair_out, [(land_out, k) for k in range(3)], w_out[0], m_w_out[0], v_w_out[0],
                                    "adam_w_out", 1024)
    _, land_second = _wait_copies(hop2, _second_hop_copies, big["w_out"][0], "w_in_grad_second_hop_wait")
    big["w_in"] = tuple(o.T for o in _adam_from_chips(
        chip, pair_in, [(land_first, 0), (land_second, 0)], w_in[0].T, m_w_in[0].T, v_w_in[0].T, "adam_w_in", 256))
    order = ["norm_g", "w_ada", "b_ada", "w_in", "attn_sinks", "sgu_ln_g", "sgu_ln_b", "sgu_w", "sgu_b", "w_out",
             "final_g"]
    outs = [loss.reshape(()), grad_x[None]]
    for k in range(4):
        for name in order:
            outs.append(big[name][k][None] if name in big else small[name][k])
    return tuple(outs)
```

```python
import jax
import jax.numpy as jnp
from jax import lax
from jax.experimental import pallas as pl
from jax.experimental.pallas import tpu as pltpu

F32 = jnp.float32
BF16 = jnp.bfloat16
MESH = pl.DeviceIdType.MESH

N_DEV = 8
D_MODEL = 2048
HEAD_DIM = 64
D_ATTN = 1024
N_Q_HEADS = 16
D_KV = 128
BLOCK = 128
D_SGU = 1024
SGU_GROUPS = 8
D_IN = 5376
W_IN_SHARD = D_IN // N_DEV
W_OUT_SHARD = D_MODEL // N_DEV
W_ADA_SHARD = 3 * D_MODEL // N_DEV
EPS = 1e-6
ATTN_SCALE = 0.125

ADAM_LR = 0.001
ADAM_B1 = 0.9
ADAM_B2 = 0.999
ADAM_EPS = 1e-08
ADAM_WD = 0.01
ADAM_STEP = 10

SEG_Q, SEG_KV, SEG_GA, SEG_U, SEG_VS, SEG_GS = 0, 1024, 1280, 2304, 3328, 4352

VMEM_LIMIT = 56 * 1024 * 1024

ROW_SHIFT, ROW_SCALE, ROW_GATE, ROW_NORM_G, ROW_FINAL_G, ROW_LN, ROW_MISC, ROW_SGU_B = 0, 1, 2, 3, 4, 5, 6, 8
SMALL_ROWS = 16


def _params(**kw):
    return pltpu.CompilerParams(vmem_limit_bytes=VMEM_LIMIT, **kw)


def _sigmoid(x):
    return 0.5 * (jnp.tanh(0.5 * x) + 1.0)


def _place():
    return lax.axis_index("x"), lax.axis_index("y"), lax.axis_index("c")


def _all_gather(shards, name, in_vmem):
    n = len(shards)

    def body(*refs):
        ins, outs = refs[:n], refs[n:2 * n]
        send_sems, recv_sems, local_sems = refs[2 * n:]
        x, y, c = _place()
        me, sibling = (x, y, c), (x, y, 1 - c)
        chips = [(1 - x, y), (x, 1 - y), (1 - x, 1 - y)]
        first, passed, mine = [], [], []
        for a in range(n):
            out_ref = outs[a]

            def slot(px, py, pc, out_ref=out_ref):
                return out_ref.at[4 * px + 2 * py + pc]

            def copy(k, block, to, src=None, a=a, slot=slot):
                return pltpu.make_async_remote_copy(
                    src_ref=slot(*block) if src is None else src, dst_ref=slot(*block),
                    send_sem=send_sems.at[a, k], recv_sem=recv_sems.at[a, k],
                    device_id=to, device_id_type=MESH)

            own = pltpu.make_async_copy(ins[a], slot(*me), local_sems.at[a])
            own.start()
            mine.append(own)
            mine_out = [copy(0, me, sibling, src=ins[a])]
            mine_out += [copy(1 + j, me, (*chip, c), src=ins[a]) for j, chip in enumerate(chips)]
            for cp in mine_out:
                cp.start()
            first += mine_out
            passed.append([copy(4 + j, (*chip, c), sibling) for j, chip in enumerate(chips)])
        for j, chip in enumerate(chips):
            for a in range(n):
                out_ref = outs[a]
                blk = out_ref.at[4 * chip[0] + 2 * chip[1] + c]
                pltpu.make_async_remote_copy(
                    src_ref=blk, dst_ref=blk, send_sem=send_sems.at[a, 1 + j], recv_sem=recv_sems.at[a, 1 + j],
                    device_id=me, device_id_type=MESH).wait_recv()
                passed[a][j].start()
        for a in range(n):
            out_ref = outs[a]
            blk = out_ref.at[4 * x + 2 * y + (1 - c)]
            pltpu.make_async_remote_copy(
                src_ref=blk, dst_ref=blk, send_sem=send_sems.at[a, 0], recv_sem=recv_sems.at[a, 0],
                device_id=me, device_id_type=MESH).wait_recv()
            for j, chip in enumerate(chips):
                blk = out_ref.at[4 * chip[0] + 2 * chip[1] + (1 - c)]
                pltpu.make_async_remote_copy(
                    src_ref=blk, dst_ref=blk, send_sem=send_sems.at[a, 4 + j], recv_sem=recv_sems.at[a, 4 + j],
                    device_id=me, device_id_type=MESH).wait_recv()
        for cp in first:
            cp.wait_send()
        for a in range(n):
            for cp in passed[a]:
                cp.wait_send()
        for cp in mine:
            cp.wait()

    space = pltpu.VMEM if in_vmem else pl.ANY
    spec = pl.BlockSpec(memory_space=space)
    return pl.pallas_call(
        body, name=name,
        out_shape=tuple(jax.ShapeDtypeStruct((N_DEV,) + s.shape, s.dtype) for s in shards),
        in_specs=[spec] * n, out_specs=tuple([spec] * n),
        scratch_shapes=[pltpu.SemaphoreType.DMA((n, 7)), pltpu.SemaphoreType.DMA((n, 7)),
                        pltpu.SemaphoreType.DMA((n,))],
        compiler_params=_params(),
    )(*shards)


def _pair_reduce(blocks, name, row_chunk):
    _, _, r, cols = blocks.shape
    assert r % row_chunk == 0

    def body(in_ref, out_ref, land, own, summed, send_sems, recv_sems, own_sems, out_sems):
        x, y, c = _place()
        sends, loads, stores = [], [], []
        for m in range(4):
            cp = pltpu.make_async_remote_copy(
                src_ref=in_ref.at[m, 1 - c], dst_ref=land.at[m], send_sem=send_sems.at[m], recv_sem=recv_sems.at[m],
                device_id=(x, y, 1 - c), device_id_type=MESH)
            cp.start()
            sends.append(cp)
            ld = pltpu.make_async_copy(in_ref.at[m, c], own.at[m], own_sems.at[m])
            ld.start()
            loads.append(ld)
        for m in range(4):
            sends[m].wait_recv()
            loads[m].wait()
            for k in range(r // row_chunk):
                rows = slice(k * row_chunk, (k + 1) * row_chunk)
                summed[m, rows, :] = (own[m, rows, :].astype(F32) + land[m, rows, :].astype(F32)).astype(BF16)
            st = pltpu.make_async_copy(summed.at[m], out_ref.at[m], out_sems.at[m])
            st.start()
            stores.append(st)
        for m in range(4):
            sends[m].wait_send()
            stores[m].wait()

    spec = pl.BlockSpec(memory_space=pl.ANY)
    return pl.pallas_call(
        body, name=name, out_shape=jax.ShapeDtypeStruct((4, r, cols), BF16),
        in_specs=[spec], out_specs=spec,
        scratch_shapes=[pltpu.VMEM((4, r, cols), BF16), pltpu.VMEM((4, r, cols), BF16), pltpu.VMEM((4, r, cols), BF16),
                        pltpu.SemaphoreType.DMA((4,)), pltpu.SemaphoreType.DMA((4,)), pltpu.SemaphoreType.DMA((4,)),
                        pltpu.SemaphoreType.DMA((4,))],
        compiler_params=_params(),
    )(blocks)


_HBM = pl.BlockSpec(memory_space=pltpu.HBM)
_SEM = pl.BlockSpec(memory_space=pltpu.SEMAPHORE)
_EFFECT = pltpu.SideEffectType.DATAFLOW_SIDE_EFFECTING


def _start_copies(bufs, copies, n_copies, after, name):
    nb = len(bufs)

    def body(*refs):
        for cp in copies(refs[:nb], refs[nb + 1], refs[nb + 2]):
            cp.start()
        refs[-1][...] = jnp.zeros_like(refs[-1])

    out = pl.pallas_call(
        body, name=name,
        out_shape=(pltpu.SemaphoreType.DMA((n_copies,)), pltpu.SemaphoreType.DMA((n_copies,)),
                   *[pltpu.HBM(b.shape, b.dtype) for b in bufs], jax.ShapeDtypeStruct((8, 128), F32)),
        in_specs=(_HBM,) * nb + (pl.BlockSpec(memory_space=pl.ANY),),
        out_specs=(_SEM, _SEM) + (_HBM,) * nb + (pl.BlockSpec(memory_space=pltpu.VMEM),),
        input_output_aliases={i: 2 + i for i in range(nb)},
        compiler_params=pltpu.CompilerParams(has_side_effects=_EFFECT),
    )(*[pltpu.with_memory_space_constraint(b, pltpu.HBM) for b in bufs], after)
    return out[0], out[1], list(out[2:2 + nb]), out[-1]


def _wait_copies(flight, copies, after, name):
    send_sems, recv_sems, bufs, _ = flight
    nb = len(bufs)

    def body(*refs):
        for cp in copies(refs[:nb], refs[nb], refs[nb + 1]):
            cp.wait_send()
            cp.wait_recv()

    return pl.pallas_call(
        body, name=name,
        out_shape=tuple(pltpu.HBM(b.shape, b.dtype) for b in bufs),
        in_specs=(_HBM,) * nb + (_SEM, _SEM, pl.BlockSpec(memory_space=pl.ANY)), out_specs=(_HBM,) * nb,
        input_output_aliases={i: i for i in range(nb)},
        compiler_params=pltpu.CompilerParams(has_side_effects=_EFFECT),
    )(*bufs, send_sems, recv_sems, after)


def _wait_then_start(flight, waited, started, n_started, after, name):
    old_send, old_recv, bufs, _ = flight
    nb = len(bufs)

    def body(*refs):
        for cp in waited(refs[:nb], refs[nb], refs[nb + 1]):
            cp.wait_send()
            cp.wait_recv()
        for cp in started(refs[:nb], refs[nb + 3], refs[nb + 4]):
            cp.start()
        refs[-1][...] = jnp.zeros_like(refs[-1])

    out = pl.pallas_call(
        body, name=name,
        out_shape=(pltpu.SemaphoreType.DMA((n_started,)), pltpu.SemaphoreType.DMA((n_started,)),
                   *[pltpu.HBM(b.shape, b.dtype) for b in bufs], jax.ShapeDtypeStruct((8, 128), F32)),
        in_specs=(_HBM,) * nb + (_SEM, _SEM, pl.BlockSpec(memory_space=pl.ANY)),
        out_specs=(_SEM, _SEM) + (_HBM,) * nb + (pl.BlockSpec(memory_space=pltpu.VMEM),),
        input_output_aliases={i: 2 + i for i in range(nb)},
        compiler_params=pltpu.CompilerParams(has_side_effects=_EFFECT),
    )(*bufs, old_send, old_recv, after)
    return out[0], out[1], list(out[2:2 + nb]), out[-1]


def _chip_copies(refs, send_sems, recv_sems):
    pair_ref, land_ref = refs
    x, y, c = _place()
    chips = [(1 - x, y), (x, 1 - y), (1 - x, 1 - y)]
    return [pltpu.make_async_remote_copy(
        src_ref=pair_ref.at[2 * chip[0] + chip[1]], dst_ref=land_ref.at[k],
        send_sem=send_sems.at[k], recv_sem=recv_sems.at[k],
        device_id=(*chip, c), device_id_type=MESH) for k, chip in enumerate(chips)]


def _first_hop_copies(refs, send_sems, recv_sems):
    pair_ref, land_ref = refs
    x, y, c = _place()
    first = ((x + 1 - c) % 2, (y + c) % 2)
    blocks = [2 * first[0] + first[1], 2 * (1 - x) + (1 - y)]
    return [pltpu.make_async_remote_copy(
        src_ref=pair_ref.at[blocks[k]], dst_ref=land_ref.at[k], send_sem=send_sems.at[k], recv_sem=recv_sems.at[k],
        device_id=(*first, c), device_id_type=MESH) for k in range(2)]


def _second_hop_copies(refs, send_sems, recv_sems):
    relay_ref, land_ref = refs
    x, y, c = _place()
    second = ((x + c) % 2, (y + 1 - c) % 2)
    return [pltpu.make_async_remote_copy(
        src_ref=relay_ref, dst_ref=land_ref.at[0], send_sem=send_sems.at[0], recv_sem=recv_sems.at[0],
        device_id=(*second, c), device_id_type=MESH)]


def _own_block_copies(targets):
    def copies(refs, send_sems, recv_sems):
        x, y, c = _place()
        mine = refs[0].at[4 * x + 2 * y + c]
        return [pltpu.make_async_remote_copy(
            src_ref=mine, dst_ref=mine, send_sem=send_sems.at[k], recv_sem=recv_sems.at[k],
            device_id=to, device_id_type=MESH) for k, to in enumerate(targets(x, y, c))]
    return copies


def _my_core_and_sibling(x, y, c):
    return [(x, y, 1 - c), (1 - x, y, c), (x, 1 - y, c), (1 - x, 1 - y, c)]


def _all_others(x, y, c):
    flip = lambda v, f: 1 - v if f else v
    return [(flip(x, r & 4), flip(y, r & 2), flip(c, r & 1)) for r in range(1, N_DEV)]


def _forward_copies(refs, send_sems, recv_sems):
    x, y, c = _place()
    chips = [(1 - x, y), (x, 1 - y), (1 - x, 1 - y)]
    return [pltpu.make_async_remote_copy(
        src_ref=refs[0].at[4 * chip[0] + 2 * chip[1] + c], dst_ref=refs[0].at[4 * chip[0] + 2 * chip[1] + c],
        send_sem=send_sems.at[k], recv_sem=recv_sems.at[k],
        device_id=(x, y, 1 - c), device_id_type=MESH) for k, chip in enumerate(chips)]


def _near_targets(x, y, c):
    return [(x, y, 1 - c), (1 - x, y, c), (x, 1 - y, c)]


def _second_stage_copies(refs, send_sems, recv_sems):
    x, y, c = _place()
    relayed = ((x + 1 - c) % 2, (y + c) % 2, c)
    relay_to = ((x + c) % 2, (y + 1 - c) % 2, c)
    plan = [((1 - x, y, c), (x, y, 1 - c)), ((x, 1 - y, c), (x, y, 1 - c)), (relayed, relay_to)]
    copies = []
    for k, ((px, py, pc), to) in enumerate(plan):
        blk = refs[0].at[4 * px + 2 * py + pc]
        copies.append(pltpu.make_async_remote_copy(
            src_ref=blk, dst_ref=blk, send_sem=send_sems.at[k], recv_sem=recv_sems.at[k],
            device_id=to, device_id_type=MESH))
    return copies


def _diagonal_forward_copies(refs, send_sems, recv_sems):
    x, y, c = _place()
    blk = refs[0].at[4 * (1 - x) + 2 * (1 - y) + c]
    return [pltpu.make_async_remote_copy(
        src_ref=blk, dst_ref=blk, send_sem=send_sems.at[0], recv_sem=recv_sems.at[0],
        device_id=(x, y, 1 - c), device_id_type=MESH)]


def _with_own_slot(block, me):
    return lax.dynamic_update_index_in_dim(lax.empty((N_DEV,) + block.shape, block.dtype), block, me, 0)


def _matmul(a, b, dims, out_dtype, tm, tn, name, dep=None):
    if dims == "nn":
        (m, k), n = a.shape, b.shape[1]
        a_spec = pl.BlockSpec((tm, k), lambda i, j: (i, 0))
        b_spec = pl.BlockSpec((k, tn), lambda i, j: (0, j))
        contract = ((1,), (0,))
    elif dims == "nt":
        (m, k), n = a.shape, b.shape[0]
        a_spec = pl.BlockSpec((tm, k), lambda i, j: (i, 0))
        b_spec = pl.BlockSpec((tn, k), lambda i, j: (j, 0))
        contract = ((1,), (1,))
    else:
        (k, m), n = a.shape, b.shape[1]
        a_spec = pl.BlockSpec((k, tm), lambda i, j: (0, i))
        b_spec = pl.BlockSpec((k, tn), lambda i, j: (0, j))
        contract = ((0,), (0,))
    assert m % tm == 0 and n % tn == 0 and a.dtype == BF16 and b.dtype == BF16

    def body(a_ref, b_ref, *rest):
        rest[-1][...] = lax.dot_general(a_ref[...], b_ref[...], (contract, ((), ())),
                                        preferred_element_type=F32).astype(out_dtype)

    deps = [] if dep is None else [dep]
    return pl.pallas_call(
        body, name=name, grid=(m // tm, n // tn),
        in_specs=[a_spec, b_spec] + [pl.BlockSpec((8, 128), lambda i, j: (0, 0))] * len(deps),
        out_specs=pl.BlockSpec((tm, tn), lambda i, j: (i, j)),
        out_shape=jax.ShapeDtypeStruct((m, n), out_dtype),
        compiler_params=_params(dimension_semantics=("arbitrary", "arbitrary")),
    )(a, b, *deps)


Z_TILE = 768
_Z_TILE_ORDER = ((0, 1, 2, 3, 4, 5, 6), (2, 0, 1, 6, 3, 4, 5), (4, 0, 5, 6, 1, 2, 3), (6, 2, 3, 4, 0, 1, 5))
_Z_EARLY_TILES = 4


def _z_proj(h, w_in_t, order, first, count, z_prev, name, dep=None):
    t = h.shape[0]

    def body(order_ref, h_ref, w_ref, *rest):
        rest[-1][...] = _dot_nt(h_ref[...], w_ref[...])

    prev = [] if z_prev is None else [z_prev]
    deps = [] if dep is None else [dep]
    return pl.pallas_call(
        body, name=name,
        grid_spec=pltpu.PrefetchScalarGridSpec(
            num_scalar_prefetch=1, grid=(count,),
            in_specs=[pl.BlockSpec((t, D_MODEL), lambda j, o: (0, 0)),
                      pl.BlockSpec((Z_TILE, D_MODEL), lambda j, o: (o[first + j], 0))]
            + [pl.BlockSpec(memory_space=pl.ANY)] * len(prev)
            + [pl.BlockSpec((8, 128), lambda j, o: (0, 0))] * len(deps),
            out_specs=pl.BlockSpec((t, Z_TILE), lambda j, o: (0, o[first + j]))),
        out_shape=jax.ShapeDtypeStruct((t, D_IN), F32),
        input_output_aliases={3: 0} if prev else {},
        compiler_params=_params(dimension_semantics=("arbitrary",)),
    )(order, h, w_in_t, *prev, *deps)


def _modulation(c_all, w_ada, b_ada_mine):
    def body(c_ref, w_ref, b_ref, act_ref, mod_ref):
        cv = c_ref[...]
        act = cv * _sigmoid(cv)
        act_ref[...] = act
        mod_ref[...] = jnp.dot(act.astype(BF16), w_ref[...].astype(BF16), preferred_element_type=F32) + b_ref[...]

    return pl.pallas_call(
        body, name="modulation",
        out_shape=(jax.ShapeDtypeStruct(c_all.shape, F32), jax.ShapeDtypeStruct((N_DEV, W_ADA_SHARD), F32)),
        compiler_params=_params(),
    )(c_all, w_ada, b_ada_mine)


def _modulated_norm(x, norm_g, scale, shift, tm=256):
    t, d = x.shape

    def body(x_ref, g_ref, sc_ref, sh_ref, h_ref):
        xv = x_ref[...]
        r = lax.rsqrt(jnp.mean(xv * xv, axis=-1, keepdims=True) + EPS)
        h = (xv * r) * g_ref[...] * (1.0 + sc_ref[...]) + sh_ref[...]
        h_ref[...] = h.astype(BF16)

    row = pl.BlockSpec((1, d), lambda i: (0, 0))
    return pl.pallas_call(
        body, name="modulated_norm", grid=(t // tm,),
        in_specs=[pl.BlockSpec((tm, d), lambda i: (i, 0)), row, row, row],
        out_specs=pl.BlockSpec((tm, d), lambda i: (i, 0)),
        out_shape=jax.ShapeDtypeStruct((t, d), BF16),
        compiler_params=_params(dimension_semantics=("arbitrary",)),
    )(x, norm_g, scale, shift)


def _window_bias(block_index):
    s = lax.broadcasted_iota(jnp.int32, (2 * BLOCK, BLOCK), 0)
    t = lax.broadcasted_iota(jnp.int32, (2 * BLOCK, BLOCK), 1)
    valid = ((s < BLOCK) & (s > t) & (block_index > 0)) | ((s >= BLOCK) & ((s - BLOCK) <= t))
    bias = jnp.where(valid, 0.0, -jnp.inf).astype(F32)
    return jnp.concatenate([bias] * 8, axis=1)


def _heads_t(pair_blocks, g):
    top = lax.broadcasted_iota(jnp.int32, (BLOCK, BLOCK), 0) < HEAD_DIM
    zeros = jnp.zeros((HEAD_DIM, BLOCK), F32)
    tiles = []
    for blk in pair_blocks:
        tp = blk.T
        if g == 0:
            tiles += [jnp.where(top, tp, 0.0), jnp.concatenate([tp[HEAD_DIM:], zeros], axis=0)]
        else:
            tiles += [jnp.concatenate([zeros, tp[:HEAD_DIM]], axis=0), jnp.where(top, 0.0, tp)]
    return jnp.concatenate(tiles, axis=1)


def _pair_block(xt, p, g):
    r0 = HEAD_DIM * g
    even = xt[r0:r0 + HEAD_DIM, (2 * p) * BLOCK:(2 * p + 1) * BLOCK]
    odd = xt[r0:r0 + HEAD_DIM, (2 * p + 1) * BLOCK:(2 * p + 2) * BLOCK]
    return jnp.concatenate([even, odd], axis=0).T


def _softmax_t(scores_t, bias, sink):
    st = scores_t + bias
    m = jnp.maximum(jnp.max(st, axis=0, keepdims=True), sink)
    e = jnp.exp(st - m)
    es = jnp.exp(sink - m)
    inv = 1.0 / (jnp.sum(e, axis=0, keepdims=True) + es)
    return e * inv, es * inv


def _dot(a, b):
    return jnp.dot(a, b, preferred_element_type=F32)


def _dot_nt(a, b):
    return lax.dot_general(a, b, (((1,), (1,)), ((), ())), preferred_element_type=F32)


def _layer_norm_fwd(v):
    mu = jnp.mean(v, axis=-1, keepdims=True)
    xc = v - mu
    rstd = lax.rsqrt(jnp.mean(xc * xc, axis=-1, keepdims=True) + EPS)
    return xc * rstd, rstd


def _tril(transposed=False):
    t = lax.broadcasted_iota(jnp.int32, (BLOCK, BLOCK), 0)
    s = lax.broadcasted_iota(jnp.int32, (BLOCK, BLOCK), 1)
    return s >= t if transposed else t >= s


def _const_spec(shape):
    return pl.BlockSpec(shape, lambda i: (0,) * len(shape))


def _kv_prev_spec(index):
    return pl.BlockSpec((BLOCK, 2 * D_KV), lambda i: (jnp.maximum(index(i) - 1, 0), SEG_KV // (2 * D_KV)))


def _keys_values(z_ref, kvp_ref):
    kvp, kvc = kvp_ref[...], z_ref[:, SEG_KV:SEG_KV + 2 * D_KV]
    kk = jnp.concatenate([kvp[:, :D_KV], kvc[:, :D_KV]], axis=0)
    vv = jnp.concatenate([kvp[:, D_KV:], kvc[:, D_KV:]], axis=0)
    return kk, vv


def _pair_cols(g, p, base=0):
    return slice(base + (4 * g + p) * 128, base + (4 * g + p + 1) * 128)


def _mixer_fwd(z, sink_rows, ln_g, ln_b, sgu_w, sgu_bt):
    t = z.shape[0]

    def body(z_ref, kvp_ref, sink_ref, lng_ref, lnb_ref, w_ref, bt_ref, a_ref):
        bias = _window_bias(pl.program_id(0))
        kk, vv = _keys_values(z_ref, kvp_ref)
        kk_b, vvt_b = kk.astype(BF16), vv.T.astype(BF16)
        for g in range(2):
            qt = _heads_t([z_ref[:, _pair_cols(g, p, SEG_Q)] * ATTN_SCALE for p in range(4)], g).astype(BF16)
            prob, _ = _softmax_t(_dot(kk_b, qt), bias, sink_ref[g])
            ot = _dot(vvt_b, prob.astype(BF16))
            for p in range(4):
                gate = z_ref[:, _pair_cols(g, p, SEG_GA)]
                a_ref[:, _pair_cols(g, p)] = (_pair_block(ot, p, g) * (gate * _sigmoid(gate))).astype(BF16)

        vhat, _ = _layer_norm_fwd(z_ref[:, SEG_VS:SEG_VS + D_SGU])
        vn = vhat * lng_ref[...] + lnb_ref[...]
        tril = _tril()
        for g in range(SGU_GROUPS):
            cols = slice(g * 128, (g + 1) * 128)
            wm = jnp.where(tril, w_ref[g], 0.0).astype(BF16)
            mixed = _dot(wm, vn[:, cols].astype(BF16)) + bt_ref[:, g:g + 1]
            gate = z_ref[:, SEG_GS + g * 128:SEG_GS + (g + 1) * 128]
            a_ref[:, D_ATTN + g * 128:D_ATTN + (g + 1) * 128] = (
                (z_ref[:, SEG_U + g * 128:SEG_U + (g + 1) * 128] * mixed) * (gate * _sigmoid(gate))).astype(BF16)

    return pl.pallas_call(
        body, name="mixer_fwd", grid=(t // BLOCK,),
        in_specs=[pl.BlockSpec((BLOCK, D_IN), lambda i: (i, 0)), _kv_prev_spec(lambda i: i),
                  _const_spec((2, 1, 8 * BLOCK)), _const_spec((1, D_SGU)), _const_spec((1, D_SGU)),
                  _const_spec((SGU_GROUPS, BLOCK, BLOCK)), _const_spec((BLOCK, SGU_GROUPS))],
        out_specs=pl.BlockSpec((BLOCK, D_MODEL), lambda i: (i, 0)),
        out_shape=jax.ShapeDtypeStruct((t, D_MODEL), BF16),
        compiler_params=_params(dimension_semantics=("arbitrary",)),
    )(z, z, sink_rows, ln_g, ln_b, sgu_w, sgu_bt)


def _mixer_bwd(z, da, sink_rows, ln_g, ln_b, sgu_w, sgu_wt, sgu_bt):
    t = z.shape[0]
    nb = t // BLOCK

    def body(z_ref, kvp_ref, da_ref, sink_ref, lng_ref, lnb_ref, w_ref, wt_ref, bt_ref,
             dz_ref, dsink_ref, dw_ref, db_ref, dlng_ref, dlnb_ref, carry_ref, dsink_acc, dbt_acc):
        step = pl.program_id(0)

        @pl.when(step == 0)
        def _():
            carry_ref[...] = jnp.zeros_like(carry_ref)
            dsink_acc[...] = jnp.zeros_like(dsink_acc)
            dbt_acc[...] = jnp.zeros_like(dbt_acc)
            dw_ref[...] = jnp.zeros_like(dw_ref)
            dlng_ref[...] = jnp.zeros_like(dlng_ref)
            dlnb_ref[...] = jnp.zeros_like(dlnb_ref)

        bias = _window_bias(nb - 1 - step)
        kk, vv = _keys_values(z_ref, kvp_ref)
        kk_b, vv_b = kk.astype(BF16), vv.astype(BF16)
        kkt_b, vvt_b = kk.T.astype(BF16), vv.T.astype(BF16)
        dkk = jnp.zeros((2 * BLOCK, D_KV), F32)
        dvv = jnp.zeros((2 * BLOCK, D_KV), F32)
        for g in range(2):
            qt = _heads_t([z_ref[:, _pair_cols(g, p, SEG_Q)] * ATTN_SCALE for p in range(4)], g).astype(BF16)
            prob, sink_prob = _softmax_t(_dot(kk_b, qt), bias, sink_ref[g])
            prob_b = prob.astype(BF16)
            ot = _dot(vvt_b, prob_b)
            gates = [z_ref[:, _pair_cols(g, p, SEG_GA)] for p in range(4)]
            sig = [_sigmoid(gt) for gt in gates]
            d_attn = [da_ref[:, _pair_cols(g, p)] for p in range(4)]
            d_ot = _heads_t([d_attn[p] * (gates[p] * sig[p]) for p in range(4)], g).astype(BF16)
            d_prob = _dot(vv_b, d_ot)
            delta = jnp.sum(prob * d_prob, axis=0, keepdims=True)
            d_scores = (prob * (d_prob - delta)).astype(BF16)
            dsink_acc[g] -= sink_prob * delta
            d_qt = _dot(kkt_b, d_scores)
            dkk = dkk + _dot_nt(d_scores, qt)
            dvv = dvv + _dot_nt(prob_b, d_ot)
            for p in range(4):
                dz_ref[:, _pair_cols(g, p, SEG_Q)] = (_pair_block(d_qt, p, g) * ATTN_SCALE).astype(BF16)
                d_silu = sig[p] * (1.0 + gates[p] * (1.0 - sig[p]))
                dz_ref[:, _pair_cols(g, p, SEG_GA)] = (d_attn[p] * _pair_block(ot, p, g) * d_silu).astype(BF16)
        d_kv = jnp.concatenate([dkk, dvv], axis=1)
        dz_ref[:, SEG_KV:SEG_KV + 2 * D_KV] = (d_kv[BLOCK:] + carry_ref[...]).astype(BF16)
        carry_ref[...] = d_kv[:BLOCK]

        vhat, rstd = _layer_norm_fwd(z_ref[:, SEG_VS:SEG_VS + D_SGU])
        lng = lng_ref[...]
        vn = vhat * lng + lnb_ref[...]
        tril, triu = _tril(), _tril(transposed=True)
        lane = lax.broadcasted_iota(jnp.int32, (BLOCK, 128), 1)
        d_bt = jnp.zeros((BLOCK, 128), F32)
        d_vn = []
        for g in range(SGU_GROUPS):
            cols = slice(g * 128, (g + 1) * 128)
            wm = jnp.where(tril, w_ref[g], 0.0).astype(BF16)
            wmt = jnp.where(triu, wt_ref[g], 0.0).astype(BF16)
            vn_g = vn[:, cols].astype(BF16)
            mixed = _dot(wm, vn_g) + bt_ref[:, g:g + 1]
            gate = z_ref[:, SEG_GS + g * 128:SEG_GS + (g + 1) * 128]
            u = z_ref[:, SEG_U + g * 128:SEG_U + (g + 1) * 128]
            d_out = da_ref[:, D_ATTN + g * 128:D_ATTN + (g + 1) * 128]
            sg = _sigmoid(gate)
            d_um = d_out * (gate * sg)
            dz_ref[:, SEG_U + g * 128:SEG_U + (g + 1) * 128] = (d_um * mixed).astype(BF16)
            dz_ref[:, SEG_GS + g * 128:SEG_GS + (g + 1) * 128] = (
                d_out * (u * mixed) * (sg * (1.0 + gate * (1.0 - sg)))).astype(BF16)
            d_mixed = d_um * u
            d_mixed_b = d_mixed.astype(BF16)
            dw_ref[g] += jnp.where(tril, _dot_nt(d_mixed_b, vn_g), 0.0)
            d_bt = d_bt + jnp.where(lane == g, jnp.sum(d_mixed, axis=-1, keepdims=True), 0.0)
            d_vn.append(_dot(wmt, d_mixed_b))
        dbt_acc[...] += d_bt
        d_vn = jnp.concatenate(d_vn, axis=1)
        dlng_ref[...] += jnp.sum(d_vn * vhat, axis=0, keepdims=True)
        dlnb_ref[...] += jnp.sum(d_vn, axis=0, keepdims=True)
        d_vhat = d_vn * lng
        d_v = rstd * (d_vhat - jnp.mean(d_vhat, axis=-1, keepdims=True)
                      - vhat * jnp.mean(d_vhat * vhat, axis=-1, keepdims=True))
        dz_ref[:, SEG_VS:SEG_VS + D_SGU] = d_v.astype(BF16)

        @pl.when(step == nb - 1)
        def _():
            db_ref[...] = dbt_acc[...].T[:SGU_GROUPS]
            lane_row = lax.broadcasted_iota(jnp.int32, (1, 128), 1)
            d_sink = jnp.zeros((1, 128), F32)
            for g in range(2):
                acc = dsink_acc[g]
                for j in range(8):
                    head_sum = jnp.sum(acc[:, j * BLOCK:(j + 1) * BLOCK], axis=-1, keepdims=True)
                    d_sink = d_sink + jnp.where(lane_row == 8 * g + j, head_sum, 0.0)
            dsink_ref[...] = d_sink

    rev = lambda i: nb - 1 - i
    return pl.pallas_call(
        body, name="mixer_bwd", grid=(nb,),
        in_specs=[pl.BlockSpec((BLOCK, D_IN), lambda i: (rev(i), 0)), _kv_prev_spec(rev),
                  pl.BlockSpec((BLOCK, D_MODEL), lambda i: (rev(i), 0)),
                  _const_spec((2, 1, 8 * BLOCK)), _const_spec((1, D_SGU)), _const_spec((1, D_SGU)),
                  _const_spec((SGU_GROUPS, BLOCK, BLOCK)), _const_spec((SGU_GROUPS, BLOCK, BLOCK)),
                  _const_spec((BLOCK, SGU_GROUPS))],
        out_specs=(pl.BlockSpec((BLOCK, D_IN), lambda i: (rev(i), 0)), _const_spec((1, 128)),
                   _const_spec((SGU_GROUPS, BLOCK, BLOCK)), _const_spec((SGU_GROUPS, BLOCK)),
                   _const_spec((1, D_SGU)), _const_spec((1, D_SGU))),
        out_shape=(jax.ShapeDtypeStruct((t, D_IN), BF16), jax.ShapeDtypeStruct((1, 128), F32),
                   jax.ShapeDtypeStruct((SGU_GROUPS, BLOCK, BLOCK), F32), jax.ShapeDtypeStruct((SGU_GROUPS, BLOCK), F32),
                   jax.ShapeDtypeStruct((1, D_SGU), F32), jax.ShapeDtypeStruct((1, D_SGU), F32)),
        scratch_shapes=[pltpu.VMEM((BLOCK, 2 * D_KV), F32), pltpu.VMEM((2, 1, 8 * BLOCK), F32),
                        pltpu.VMEM((BLOCK, 128), F32)],
        compiler_params=_params(dimension_semantics=("arbitrary",)),
    )(z, z, da, sink_rows, ln_g, ln_b, sgu_w, sgu_wt, sgu_bt)


def _out_proj_head(a, w_out_full, x, target, gate, final_g, tm=256):
    t, d = x.shape

    def body(a_ref, w_ref, x_ref, tg_ref, gate_ref, fg_ref, dx2_ref, dy_ref, loss_ref, dfg_ref, dgate_ref):
        @pl.when(pl.program_id(0) == 0)
        def _():
            loss_ref[...] = jnp.zeros_like(loss_ref)
            dfg_ref[...] = jnp.zeros_like(dfg_ref)
            dgate_ref[...] = jnp.zeros_like(dgate_ref)

        yv, gate, fg = _dot(a_ref[...], w_ref[...]), gate_ref[...], fg_ref[...]
        x2 = x_ref[...] + gate * yv
        r2 = lax.rsqrt(jnp.mean(x2 * x2, axis=-1, keepdims=True) + EPS)
        nrm = x2 * r2
        err = nrm * fg - tg_ref[...]
        loss_ref[...] += 0.5 * jnp.sum(jnp.mean(err * err, axis=-1, keepdims=True), axis=0, keepdims=True)
        d_out = err * (1.0 / d)
        dfg_ref[...] += jnp.sum(d_out * nrm, axis=0, keepdims=True)
        d_nrm = d_out * fg
        dx2 = r2 * (d_nrm - nrm * jnp.mean(d_nrm * nrm, axis=-1, keepdims=True))
        dx2_ref[...] = dx2
        dgate_ref[...] += jnp.sum(dx2 * yv, axis=0, keepdims=True)
        dy_ref[...] = (dx2 * gate).astype(BF16)

    blk = pl.BlockSpec((tm, d), lambda i: (i, 0))
    row = _const_spec((1, d))
    whole = pl.BlockSpec(w_out_full.shape, lambda i: (0, 0), pipeline_mode=pl.Buffered(1))
    return pl.pallas_call(
        body, name="out_proj_head", grid=(t // tm,),
        in_specs=[pl.BlockSpec((tm, a.shape[1]), lambda i: (i, 0)), whole, blk, blk, row, row],
        out_specs=(blk, blk, _const_spec((1, 128)), row, row),
        out_shape=(jax.ShapeDtypeStruct((t, d), F32), jax.ShapeDtypeStruct((t, d), BF16),
                   jax.ShapeDtypeStruct((1, 128), F32), jax.ShapeDtypeStruct((1, d), F32),
                   jax.ShapeDtypeStruct((1, d), F32)),
        compiler_params=_params(dimension_semantics=("arbitrary",)),
    )(a, w_out_full, x, target, gate, final_g)


def _modulated_norm_bwd(dh, x, dx2, norm_g, scale, tm=256):
    t, d = x.shape

    def body(dh_ref, x_ref, dx2_ref, g_ref, sc_ref, gx_ref, dshift_ref, dscale_ref, dg_ref):
        @pl.when(pl.program_id(0) == 0)
        def _():
            dshift_ref[...] = jnp.zeros_like(dshift_ref)
            dscale_ref[...] = jnp.zeros_like(dscale_ref)
            dg_ref[...] = jnp.zeros_like(dg_ref)

        dh, xv, g = dh_ref[...], x_ref[...], g_ref[...]
        one_plus = 1.0 + sc_ref[...]
        r = lax.rsqrt(jnp.mean(xv * xv, axis=-1, keepdims=True) + EPS)
        xn = xv * r
        dshift_ref[...] += jnp.sum(dh, axis=0, keepdims=True)
        dscale_ref[...] += jnp.sum(dh * (xn * g), axis=0, keepdims=True)
        d_y = dh * one_plus
        dg_ref[...] += jnp.sum(d_y * xn, axis=0, keepdims=True)
        d_xn = d_y * g
        gx_ref[...] = dx2_ref[...] + r * (d_xn - xn * jnp.mean(d_xn * xn, axis=-1, keepdims=True))

    blk = pl.BlockSpec((tm, d), lambda i: (i, 0))
    row = _const_spec((1, d))
    return pl.pallas_call(
        body, name="modulated_norm_bwd", grid=(t // tm,),
        in_specs=[blk, blk, blk, row, row], out_specs=(blk, row, row, row),
        out_shape=(jax.ShapeDtypeStruct((t, d), F32),) + (jax.ShapeDtypeStruct((1, d), F32),) * 3,
        compiler_params=_params(dimension_semantics=("arbitrary",)),
    )(dh, x, dx2, norm_g, scale)


def _adamw(w, g, m, v):
    m = ADAM_B1 * m + (1.0 - ADAM_B1) * g
    v = ADAM_B2 * v + (1.0 - ADAM_B2) * (g * g)
    m_hat = m / (1.0 - ADAM_B1 ** ADAM_STEP)
    v_hat = v / (1.0 - ADAM_B2 ** ADAM_STEP)
    delta = -ADAM_LR * (m_hat / (jnp.sqrt(v_hat) + ADAM_EPS) + ADAM_WD * w)
    return delta, m, v


def _relay_sum(second_chip, pair, land, tr):
    _, r, c = pair.shape

    def body(chip_ref, a_ref, b_ref, o_ref):
        o_ref[...] = (a_ref[...].astype(F32) + b_ref[...].astype(F32)).astype(BF16)

    return pl.pallas_call(
        body, name="w_in_grad_relay_sum",
        grid_spec=pltpu.PrefetchScalarGridSpec(
            num_scalar_prefetch=1, grid=(r // tr,),
            in_specs=[pl.BlockSpec((None, tr, c), lambda i, chip_ref: (chip_ref[0], i, 0)),
                      pl.BlockSpec((None, tr, c), lambda i, chip_ref: (1, i, 0))],
            out_specs=pl.BlockSpec((tr, c), lambda i, chip_ref: (i, 0))),
        out_shape=jax.ShapeDtypeStruct((r, c), BF16),
        compiler_params=_params(dimension_semantics=("arbitrary",)),
    )(second_chip, pair, land)


def _adam_from_chips(chip, pair, landed, w, m, v, name, tc):
    _, r, c = pair.shape
    n = len(landed)

    def body(chip_ref, own_ref, *refs):
        w_ref, m_ref, v_ref, g_ref, d_ref, nm_ref, nv_ref = refs[n:]
        g = own_ref[...].astype(F32)
        for k in range(n):
            g = g + refs[k][...].astype(F32)
        g_ref[...] = g
        d_ref[...], nm_ref[...], nv_ref[...] = _adamw(w_ref[...], g, m_ref[...], v_ref[...])

    def landed_spec(index):
        return pl.BlockSpec((None, r, tc), lambda i, chip_ref: (index, 0, i))

    blk = pl.BlockSpec((r, tc), lambda i, chip_ref: (0, i))
    return pl.pallas_call(
        body, name=name,
        grid_spec=pltpu.PrefetchScalarGridSpec(
            num_scalar_prefetch=1, grid=(c // tc,),
            in_specs=[pl.BlockSpec((None, r, tc), lambda i, chip_ref: (chip_ref[0], 0, i))]
            + [landed_spec(index) for _, index in landed] + [blk, blk, blk],
            out_specs=(blk,) * 4),
        out_shape=(jax.ShapeDtypeStruct((r, c), F32),) * 4,
        compiler_params=_params(dimension_semantics=("arbitrary",)),
    )(chip, pair, *[array for array, _ in landed], w, m, v)


def _adam_w_ada(act_t, dmod_mine, w, m, v, tr=256):
    r, c = w.shape

    def body(a_ref, dm_ref, w_ref, m_ref, v_ref, g_ref, d_ref, nm_ref, nv_ref):
        g = _dot(a_ref[...].astype(BF16), dm_ref[...].astype(BF16))
        g_ref[...] = g
        d_ref[...], nm_ref[...], nv_ref[...] = _adamw(w_ref[...], g, m_ref[...], v_ref[...])

    blk = pl.BlockSpec((tr, c), lambda i: (i, 0))
    return pl.pallas_call(
        body, name="adam_w_ada", grid=(r // tr,),
        in_specs=[pl.BlockSpec((tr, N_DEV), lambda i: (i, 0)), _const_spec((N_DEV, c)), blk, blk, blk],
        out_specs=(blk,) * 4, out_shape=(jax.ShapeDtypeStruct((r, c), F32),) * 4,
        compiler_params=_params(dimension_semantics=("arbitrary",)),
    )(act_t, dmod_mine, w, m, v)


def _pack_small(d_shift, d_scale, d_gate, d_norm_g, d_final_g, d_ln_g, d_ln_b, loss, d_sinks, d_sgu_b):
    def body(shift_ref, scale_ref, gate_ref, ng_ref, fg_ref, lng_ref, lnb_ref, loss_ref, sink_ref, b_ref, o_ref):
        o_ref[...] = jnp.zeros_like(o_ref)
        o_ref[ROW_SHIFT:ROW_SHIFT + 1, :] = shift_ref[...]
        o_ref[ROW_SCALE:ROW_SCALE + 1, :] = scale_ref[...]
        o_ref[ROW_GATE:ROW_GATE + 1, :] = gate_ref[...]
        o_ref[ROW_NORM_G:ROW_NORM_G + 1, :] = ng_ref[...]
        o_ref[ROW_FINAL_G:ROW_FINAL_G + 1, :] = fg_ref[...]
        o_ref[ROW_LN:ROW_LN + 1, 0:D_SGU] = lng_ref[...]
        o_ref[ROW_LN:ROW_LN + 1, D_SGU:2 * D_SGU] = lnb_ref[...]
        o_ref[ROW_MISC:ROW_MISC + 1, 0:128] = loss_ref[...]
        o_ref[ROW_MISC:ROW_MISC + 1, 128:256] = sink_ref[...]
        o_ref[ROW_SGU_B:ROW_SGU_B + SGU_GROUPS, 0:BLOCK] = b_ref[...]

    return pl.pallas_call(
        body, name="pack_small", out_shape=jax.ShapeDtypeStruct((SMALL_ROWS, D_MODEL), F32),
        compiler_params=_params(),
    )(d_shift, d_scale, d_gate, d_norm_g, d_final_g, d_ln_g, d_ln_b, loss, d_sinks, d_sgu_b)


_SMALL_NAMES = ("norm_g", "b_ada", "attn_sinks", "sgu_ln_g", "sgu_ln_b", "sgu_w", "sgu_b", "final_g")


def _adam_small(partials, d_sgu_w_all, weights, moments_m, moments_v):
    names = _SMALL_NAMES
    k = len(names)

    def body(*refs):
        p_ref, sw_ref = refs[0], refs[1]
        w_refs, m_refs, v_refs = refs[2:2 + k], refs[2 + k:2 + 2 * k], refs[2 + 2 * k:2 + 3 * k]
        loss_ref, dmod_ref = refs[2 + 3 * k], refs[3 + 3 * k]
        out_refs = refs[4 + 3 * k:4 + 7 * k]
        sum_ref = refs[4 + 7 * k]
        total = p_ref[0]
        for j in range(1, N_DEV):
            total = total + p_ref[j]
        sum_ref[...] = total
        for j in range(N_DEV):
            for part, row in enumerate((ROW_SHIFT, ROW_SCALE, ROW_GATE)):
                dmod_ref[j:j + 1, part * D_MODEL:(part + 1) * D_MODEL] = p_ref[j, row:row + 1, :]
        loss_ref[...] = sum_ref[ROW_MISC:ROW_MISC + 1, 0:1]
        d_sgu_w = sw_ref[0]
        for j in range(1, N_DEV):
            d_sgu_w = d_sgu_w + sw_ref[j]
        grads = {
            "norm_g": sum_ref[ROW_NORM_G:ROW_NORM_G + 1, :],
            "b_ada": jnp.concatenate([sum_ref[r:r + 1, :] for r in (ROW_SHIFT, ROW_SCALE, ROW_GATE)], axis=1),
            "attn_sinks": sum_ref[ROW_MISC:ROW_MISC + 1, 128:128 + N_Q_HEADS],
            "sgu_ln_g": sum_ref[ROW_LN:ROW_LN + 1, 0:D_SGU],
            "sgu_ln_b": sum_ref[ROW_LN:ROW_LN + 1, D_SGU:2 * D_SGU],
            "sgu_w": d_sgu_w[None],
            "sgu_b": sum_ref[ROW_SGU_B:ROW_SGU_B + SGU_GROUPS, 0:BLOCK][None],
            "final_g": sum_ref[ROW_FINAL_G:ROW_FINAL_G + 1, :],
        }
        for i, name in enumerate(names):
            g = grads[name]
            delta, m, v = _adamw(w_refs[i][...], g, m_refs[i][...], v_refs[i][...])
            out_refs[4 * i][...] = g
            out_refs[4 * i + 1][...] = delta
            out_refs[4 * i + 2][...] = m
            out_refs[4 * i + 3][...] = v

    shapes = [jax.ShapeDtypeStruct((1, 1), F32), jax.ShapeDtypeStruct((N_DEV, 3 * D_MODEL), F32)]
    for name in names:
        shapes += [jax.ShapeDtypeStruct(weights[name].shape, F32)] * 4
    outs = pl.pallas_call(
        body, name="adam_small", out_shape=tuple(shapes),
        scratch_shapes=[pltpu.VMEM((SMALL_ROWS, D_MODEL), F32)],
        compiler_params=_params(),
    )(partials, d_sgu_w_all, *[weights[n] for n in names], *[moments_m[n] for n in names],
      *[moments_v[n] for n in names])
    return outs[0], outs[1], {name: outs[2 + 4 * i:6 + 4 * i] for i, name in enumerate(names)}


def kernel(x, c, norm_g, w_ada, b_ada, w_in, attn_sinks, sgu_ln_g, sgu_ln_b, sgu_w, sgu_b, w_out, final_g, loss_target, m_norm_g, m_w_ada, m_b_ada, m_w_in, m_attn_sinks, m_sgu_ln_g, m_sgu_ln_b, m_sgu_w, m_sgu_b, m_w_out, m_final_g, v_norm_g, v_w_ada, v_b_ada, v_w_in, v_attn_sinks, v_sgu_ln_g, v_sgu_ln_b, v_sgu_w, v_sgu_b, v_w_out, v_final_g):
    xi, yi, ci = _place()
    me = 4 * xi + 2 * yi + ci
    x2d, target = x[0], loss_target[0]
    t = x2d.shape[0]

    core = ci.astype(jnp.int32).reshape(1)
    chip = (2 * xi + yi).astype(jnp.int32).reshape(1)

    c_all = _all_gather([c.reshape(8, 256)], "gather_c", True)[0].reshape(N_DEV, D_MODEL)
    b_mine = lax.dynamic_slice(b_ada, (0, me * W_ADA_SHARD), (1, W_ADA_SHARD))
    c_act, mod_part = _modulation(c_all, w_ada[0], b_mine)
    mod_all = _all_gather([mod_part], "gather_mod", True)[0]

    near = _own_block_copies(_near_targets)
    w_in_flight = _start_copies([_with_own_slot(w_in[0].T.astype(BF16), me)], near, 3, mod_all, "gather_w_in_start")
    mod = lax.dynamic_index_in_dim(mod_all, me, axis=1, keepdims=False).reshape(1, 3 * D_MODEL)
    mod = mod + w_in_flight[3][0, 0]
    shift, scale, gate = mod[:, :D_MODEL], mod[:, D_MODEL:2 * D_MODEL], mod[:, 2 * D_MODEL:]
    h = _modulated_norm(x2d, norm_g, scale, shift)

    w_in_pair = _wait_copies(w_in_flight, lambda *a: near(*a)[:1], h, "gather_w_in_sibling_wait")
    tile_order = jnp.asarray(_Z_TILE_ORDER, jnp.int32)[chip[0]]
    z_own = _z_proj(h, w_in_pair[0].reshape(D_IN, D_MODEL), tile_order, 0, 1, None, "z_proj_own")
    w_in_flight = _wait_then_start((w_in_flight[0], w_in_flight[1], w_in_pair, None), lambda *a: near(*a)[1:],
                                   _second_stage_copies, 3, z_own, "gather_w_in_second_stage")
    w_in_most = _wait_copies(w_in_flight, lambda *a: _second_stage_copies(*a)[:2], z_own, "gather_w_in_forward_wait")
    z_early = _z_proj(h, w_in_most[0].reshape(D_IN, D_MODEL), tile_order, 1, _Z_EARLY_TILES - 1, z_own, "z_proj_early")
    w_out_flight = _start_copies([_with_own_slot(w_out[0].astype(BF16), me)], _own_block_copies(_my_core_and_sibling),
                                 4, z_early, "gather_w_out_start")
    w_in_flight = _wait_then_start((w_in_flight[0], w_in_flight[1], w_in_most, None),
                                   lambda *a: _second_stage_copies(*a)[2:], _diagonal_forward_copies, 1,
                                   w_out_flight[3], "gather_w_in_last_stage")
    w_in_all = _wait_copies(w_in_flight, _diagonal_forward_copies, z_early, "gather_w_in_last_wait")[0]
    w_in_t = w_in_all.reshape(D_IN, D_MODEL)
    z = _z_proj(h, w_in_t, tile_order, _Z_EARLY_TILES, 7 - _Z_EARLY_TILES, z_early, "z_proj_late")
    w_out_flight = _wait_then_start(w_out_flight, _own_block_copies(_my_core_and_sibling), _forward_copies, 3, z,
                                    "gather_w_out_forward_stage")
    sink_rows = jnp.repeat(attn_sinks.reshape(N_Q_HEADS), BLOCK).reshape(2, 1, 8 * BLOCK)
    sgu_bt = sgu_b[0].T
    a = _mixer_fwd(z, sink_rows + w_out_flight[3][0, 0], sgu_ln_g, sgu_ln_b, sgu_w[0], sgu_bt)
    w_out_all = _wait_copies(w_out_flight, _forward_copies, a, "gather_w_out_forward_wait")[0]
    w_out_full = w_out_all.reshape(D_MODEL, D_MODEL)
    final_g_row = final_g.reshape(1, D_MODEL)
    dx2, dy, loss_part, d_final_g, d_gate = _out_proj_head(a, w_out_full, x2d, target, gate, final_g_row)

    da = _matmul(dy, w_out_full, "nt", F32, min(t, 1024), 1024, "out_proj_bwd")
    dw_out = _matmul(a, dy, "tn", BF16, 1024, 1024, "w_out_grad").reshape(4, 2, W_OUT_SHARD, D_MODEL)
    pair_out = _pair_reduce(dw_out, "w_out_grad_pair_reduce", W_OUT_SHARD // 2)
    out_flight = _start_copies([pair_out, lax.empty((3, W_OUT_SHARD, D_MODEL), BF16)], _chip_copies, 3, core,
                               "w_out_grad_chip_start")
    dz, d_sinks, d_sgu_w, d_sgu_b, d_ln_g, d_ln_b = _mixer_bwd(
        z, da, sink_rows + out_flight[3][0, 0], sgu_ln_g, sgu_ln_b, sgu_w[0], jnp.swapaxes(sgu_w[0], 1, 2), sgu_bt)
    sgu_w_flight = _start_copies([_with_own_slot(d_sgu_w, me)], _own_block_copies(_all_others), N_DEV - 1, core,
                                 "sgu_w_grad_gather_start")
    dw_in_t = _matmul(dz, h, "tn", BF16, 768, D_MODEL, "w_in_grad", dep=sgu_w_flight[3])
    dw_in_t = dw_in_t.reshape(4, 2, W_IN_SHARD, D_MODEL)
    pair_in = _pair_reduce(dw_in_t, "w_in_grad_pair_reduce", W_IN_SHARD // 3)
    hop1 = _start_copies([pair_in, lax.empty((2, W_IN_SHARD, D_MODEL), BF16)], _first_hop_copies, 2, core,
                         "w_in_grad_first_hop_start")
    dh = _matmul(dz, w_in_t, "nn", F32, min(t, 1024), 512, "z_proj_bwd", dep=hop1[3])
    pair_in, land_first = _wait_copies(hop1, _first_hop_copies, dh, "w_in_grad_first_hop_wait")
    second_chip = (2 * ((xi + ci) % 2) + (yi + 1 - ci) % 2).astype(jnp.int32).reshape(1)
    relay = _relay_sum(second_chip, pair_in, land_first, W_IN_SHARD // 3)
    hop2 = _start_copies([relay, lax.empty((1, W_IN_SHARD, D_MODEL), BF16)], _second_hop_copies, 1, core,
                         "w_in_grad_second_hop_start")
    grad_x, d_shift, d_scale, d_norm_g = _modulated_norm_bwd(dh, x2d, dx2, norm_g, scale + hop2[3][0, 0])

    partial = _pack_small(d_shift, d_scale, d_gate, d_norm_g, d_final_g, d_ln_g, d_ln_b, loss_part, d_sinks, d_sgu_b)
    partial_all = _all_gather([partial], "gather_small", True)[0]
    d_sgu_w_all = _wait_copies(sgu_w_flight, _own_block_copies(_all_others), partial_all, "sgu_w_grad_gather_wait")[0]
    weights = {"norm_g": norm_g, "b_ada": b_ada, "attn_sinks": attn_sinks, "sgu_ln_g": sgu_ln_g,
               "sgu_ln_b": sgu_ln_b, "sgu_w": sgu_w, "sgu_b": sgu_b, "final_g": final_g_row}
    moments_m = {"norm_g": m_norm_g, "b_ada": m_b_ada, "attn_sinks": m_attn_sinks, "sgu_ln_g": m_sgu_ln_g,
                 "sgu_ln_b": m_sgu_ln_b, "sgu_w": m_sgu_w, "sgu_b": m_sgu_b,
                 "final_g": m_final_g.reshape(1, D_MODEL)}
    moments_v = {"norm_g": v_norm_g, "b_ada": v_b_ada, "attn_sinks": v_attn_sinks, "sgu_ln_g": v_sgu_ln_g,
                 "sgu_ln_b": v_sgu_ln_b, "sgu_w": v_sgu_w, "sgu_b": v_sgu_b,
                 "final_g": v_final_g.reshape(1, D_MODEL)}
    loss, dmod_all, small = _adam_small(partial_all, d_sgu_w_all, weights, moments_m, moments_v)
    small["final_g"] = tuple(o.reshape(D_MODEL) for o in small["final_g"])

    dmod_mine = lax.dynamic_slice(dmod_all, (0, me * W_ADA_SHARD), (N_DEV, W_ADA_SHARD))
    big = {"w_ada": _adam_w_ada(c_act.T, dmod_mine, w_ada[0], m_w_ada[0], v_w_ada[0])}
    pair_out, land_out = _wait_copies(out_flight, _chip_copies, big["w_ada"][0], "w_out_grad_chip_wait")
    big["w_out"] = _adam_from_chips(chip, pair_out, [(land_out, k) for k in range(3)], w_out[0], m_w_out[0], v_w_out[0],
                                    "adam_w_out", 1024)
    _, land_second = _wait_copies(hop2, _second_hop_copies, big["w_out"][0], "w_in_grad_second_hop_wait")
    big["w_in"] = tuple(o.T for o in _adam_from_chips(
        chip, pair_in, [(land_first, 0), (land_second, 0)], w_in[0].T, m_w_in[0].T, v_w_in[0].T, "adam_w_in", 256))
    order = ["norm_g", "w_ada", "b_ada", "w_in", "attn_sinks", "sgu_ln_g", "sgu_ln_b", "sgu_w", "sgu_b", "w_out",
             "final_g"]
    outs = [loss.reshape(()), grad_x[None]]
    for k in range(4):
        for name in order:
            outs.append(big[name][k][None] if name in big else small[name][k])
    return tuple(outs)
```

```python
import jax
import jax.numpy as jnp
from jax import lax
from jax.experimental import pallas as pl
from jax.experimental.pallas import tpu as pltpu

F32 = jnp.float32
BF16 = jnp.bfloat16
MESH = pl.DeviceIdType.MESH

N_DEV = 8
D_MODEL = 2048
HEAD_DIM = 64
D_ATTN = 1024
N_Q_HEADS = 16
D_KV = 128
BLOCK = 128
D_SGU = 1024
SGU_GROUPS = 8
D_IN = 5376
W_IN_SHARD = D_IN // N_DEV
W_OUT_SHARD = D_MODEL // N_DEV
W_ADA_SHARD = 3 * D_MODEL // N_DEV
EPS = 1e-6
ATTN_SCALE = 0.125

ADAM_LR = 0.001
ADAM_B1 = 0.9
ADAM_B2 = 0.999
ADAM_EPS = 1e-08
ADAM_WD = 0.01
ADAM_STEP = 10

SEG_Q, SEG_KV, SEG_GA, SEG_U, SEG_VS, SEG_GS = 0, 1024, 1280, 2304, 3328, 4352

VMEM_LIMIT = 56 * 1024 * 1024

ROW_SHIFT, ROW_SCALE, ROW_GATE, ROW_NORM_G, ROW_FINAL_G, ROW_LN, ROW_MISC, ROW_SGU_B = 0, 1, 2, 3, 4, 5, 6, 8
SMALL_ROWS = 16


def _params(**kw):
    return pltpu.CompilerParams(vmem_limit_bytes=VMEM_LIMIT, **kw)


def _sigmoid(x):
    return 0.5 * (jnp.tanh(0.5 * x) + 1.0)


def _place():
    return lax.axis_index("x"), lax.axis_index("y"), lax.axis_index("c")


def _all_gather(shards, name, in_vmem):
    n = len(shards)

    def body(*refs):
        ins, outs = refs[:n], refs[n:2 * n]
        send_sems, recv_sems, local_sems = refs[2 * n:]
        x, y, c = _place()
        me, sibling = (x, y, c), (x, y, 1 - c)
        chips = [(1 - x, y), (x, 1 - y), (1 - x, 1 - y)]
        first, passed, mine = [], [], []
        for a in range(n):
            out_ref = outs[a]

            def slot(px, py, pc, out_ref=out_ref):
                return out_ref.at[4 * px + 2 * py + pc]

            def copy(k, block, to, src=None, a=a, slot=slot):
                return pltpu.make_async_remote_copy(
                    src_ref=slot(*block) if src is None else src, dst_ref=slot(*block),
                    send_sem=send_sems.at[a, k], recv_sem=recv_sems.at[a, k],
                    device_id=to, device_id_type=MESH)

            own = pltpu.make_async_copy(ins[a], slot(*me), local_sems.at[a])
            own.start()
            mine.append(own)
            mine_out = [copy(0, me, sibling, src=ins[a])]
            mine_out += [copy(1 + j, me, (*chip, c), src=ins[a]) for j, chip in enumerate(chips)]
            for cp in mine_out:
                cp.start()
            first += mine_out
            passed.append([copy(4 + j, (*chip, c), sibling) for j, chip in enumerate(chips)])
        for j, chip in enumerate(chips):
            for a in range(n):
                out_ref = outs[a]
                blk = out_ref.at[4 * chip[0] + 2 * chip[1] + c]
                pltpu.make_async_remote_copy(
                    src_ref=blk, dst_ref=blk, send_sem=send_sems.at[a, 1 + j], recv_sem=recv_sems.at[a, 1 + j],
                    device_id=me, device_id_type=MESH).wait_recv()
                passed[a][j].start()
        for a in range(n):
            out_ref = outs[a]
            blk = out_ref.at[4 * x + 2 * y + (1 - c)]
            pltpu.make_async_remote_copy(
                src_ref=blk, dst_ref=blk, send_sem=send_sems.at[a, 0], recv_sem=recv_sems.at[a, 0],
                device_id=me, device_id_type=MESH).wait_recv()
            for j, chip in enumerate(chips):
                blk = out_ref.at[4 * chip[0] + 2 * chip[1] + (1 - c)]
                pltpu.make_async_remote_copy(
                    src_ref=blk, dst_ref=blk, send_sem=send_sems.at[a, 4 + j], recv_sem=recv_sems.at[a, 4 + j],
                    device_id=me, device_id_type=MESH).wait_recv()
        for cp in first:
            cp.wait_send()
        for a in range(n):
            for cp in passed[a]:
                cp.wait_send()
        for cp in mine:
            cp.wait()

    space = pltpu.VMEM if in_vmem else pl.ANY
    spec = pl.BlockSpec(memory_space=space)
    return pl.pallas_call(
        body, name=name,
        out_shape=tuple(jax.ShapeDtypeStruct((N_DEV,) + s.shape, s.dtype) for s in shards),
        in_specs=[spec] * n, out_specs=tuple([spec] * n),
        scratch_shapes=[pltpu.SemaphoreType.DMA((n, 7)), pltpu.SemaphoreType.DMA((n, 7)),
                        pltpu.SemaphoreType.DMA((n,))],
        compiler_params=_params(),
    )(*shards)


def _pair_reduce(blocks, name, row_chunk):
    _, _, r, cols = blocks.shape
    assert r % row_chunk == 0

    def body(in_ref, out_ref, land, own, summed, send_sems, recv_sems, own_sems, out_sems):
        x, y, c = _place()
        sends, loads, stores = [], [], []
        for m in range(4):
            cp = pltpu.make_async_remote_copy(
                src_ref=in_ref.at[m, 1 - c], dst_ref=land.at[m], send_sem=send_sems.at[m], recv_sem=recv_sems.at[m],
                device_id=(x, y, 1 - c), device_id_type=MESH)
            cp.start()
            sends.append(cp)
            ld = pltpu.make_async_copy(in_ref.at[m, c], own.at[m], own_sems.at[m])
            ld.start()
            loads.append(ld)
        for m in range(4):
            sends[m].wait_recv()
            loads[m].wait()
            for k in range(r // row_chunk):
                rows = slice(k * row_chunk, (k + 1) * row_chunk)
                summed[m, rows, :] = (own[m, rows, :].astype(F32) + land[m, rows, :].astype(F32)).astype(BF16)
            st = pltpu.make_async_copy(summed.at[m], out_ref.at[m], out_sems.at[m])
            st.start()
            stores.append(st)
        for m in range(4):
            sends[m].wait_send()
            stores[m].wait()

    spec = pl.BlockSpec(memory_space=pl.ANY)
    return pl.pallas_call(
        body, name=name, out_shape=jax.ShapeDtypeStruct((4, r, cols), BF16),
        in_specs=[spec], out_specs=spec,
        scratch_shapes=[pltpu.VMEM((4, r, cols), BF16), pltpu.VMEM((4, r, cols), BF16), pltpu.VMEM((4, r, cols), BF16),
                        pltpu.SemaphoreType.DMA((4,)), pltpu.SemaphoreType.DMA((4,)), pltpu.SemaphoreType.DMA((4,)),
                        pltpu.SemaphoreType.DMA((4,))],
        compiler_params=_params(),
    )(blocks)


_HBM = pl.BlockSpec(memory_space=pltpu.HBM)
_SEM = pl.BlockSpec(memory_space=pltpu.SEMAPHORE)
_EFFECT = pltpu.SideEffectType.DATAFLOW_SIDE_EFFECTING


def _start_copies(bufs, copies, n_copies, after, name):
    nb = len(bufs)

    def body(*refs):
        for cp in copies(refs[:nb], refs[nb + 1], refs[nb + 2]):
            cp.start()
        refs[-1][...] = jnp.zeros_like(refs[-1])

    out = pl.pallas_call(
        body, name=name,
        out_shape=(pltpu.SemaphoreType.DMA((n_copies,)), pltpu.SemaphoreType.DMA((n_copies,)),
                   *[pltpu.HBM(b.shape, b.dtype) for b in bufs], jax.ShapeDtypeStruct((8, 128), F32)),
        in_specs=(_HBM,) * nb + (pl.BlockSpec(memory_space=pl.ANY),),
        out_specs=(_SEM, _SEM) + (_HBM,) * nb + (pl.BlockSpec(memory_space=pltpu.VMEM),),
        input_output_aliases={i: 2 + i for i in range(nb)},
        compiler_params=pltpu.CompilerParams(has_side_effects=_EFFECT),
    )(*[pltpu.with_memory_space_constraint(b, pltpu.HBM) for b in bufs], after)
    return out[0], out[1], list(out[2:2 + nb]), out[-1]


def _wait_copies(flight, copies, after, name):
    send_sems, recv_sems, bufs, _ = flight
    nb = len(bufs)

    def body(*refs):
        for cp in copies(refs[:nb], refs[nb], refs[nb + 1]):
            cp.wait_send()
            cp.wait_recv()

    return pl.pallas_call(
        body, name=name,
        out_shape=tuple(pltpu.HBM(b.shape, b.dtype) for b in bufs),
        in_specs=(_HBM,) * nb + (_SEM, _SEM, pl.BlockSpec(memory_space=pl.ANY)), out_specs=(_HBM,) * nb,
        input_output_aliases={i: i for i in range(nb)},
        compiler_params=pltpu.CompilerParams(has_side_effects=_EFFECT),
    )(*bufs, send_sems, recv_sems, after)


def _wait_then_start(flight, waited, started, n_started, after, name):
    old_send, old_recv, bufs, _ = flight
    nb = len(bufs)

    def body(*refs):
        for cp in waited(refs[:nb], refs[nb], refs[nb + 1]):
            cp.wait_send()
            cp.wait_recv()
        for cp in started(refs[:nb], refs[nb + 3], refs[nb + 4]):
            cp.start()
        refs[-1][...] = jnp.zeros_like(refs[-1])

    out = pl.pallas_call(
        body, name=name,
        out_shape=(pltpu.SemaphoreType.DMA((n_started,)), pltpu.SemaphoreType.DMA((n_started,)),
                   *[pltpu.HBM(b.shape, b.dtype) for b in bufs], jax.ShapeDtypeStruct((8, 128), F32)),
        in_specs=(_HBM,) * nb + (_SEM, _SEM, pl.BlockSpec(memory_space=pl.ANY)),
        out_specs=(_SEM, _SEM) + (_HBM,) * nb + (pl.BlockSpec(memory_space=pltpu.VMEM),),
        input_output_aliases={i: 2 + i for i in range(nb)},
        compiler_params=pltpu.CompilerParams(has_side_effects=_EFFECT),
    )(*bufs, old_send, old_recv, after)
    return out[0], out[1], list(out[2:2 + nb]), out[-1]


def _chip_copies(refs, send_sems, recv_sems):
    pair_ref, land_ref = refs
    x, y, c = _place()
    chips = [(1 - x, y), (x, 1 - y), (1 - x, 1 - y)]
    return [pltpu.make_async_remote_copy(
        src_ref=pair_ref.at[2 * chip[0] + chip[1]], dst_ref=land_ref.at[k],
        send_sem=send_sems.at[k], recv_sem=recv_sems.at[k],
        device_id=(*chip, c), device_id_type=MESH) for k, chip in enumerate(chips)]


def _first_hop_copies(refs, send_sems, recv_sems):
    pair_ref, land_ref = refs
    x, y, c = _place()
    first = ((x + 1 - c) % 2, (y + c) % 2)
    blocks = [2 * first[0] + first[1], 2 * (1 - x) + (1 - y)]
    return [pltpu.make_async_remote_copy(
        src_ref=pair_ref.at[blocks[k]], dst_ref=land_ref.at[k], send_sem=send_sems.at[k], recv_sem=recv_sems.at[k],
        device_id=(*first, c), device_id_type=MESH) for k in range(2)]


def _second_hop_copies(refs, send_sems, recv_sems):
    relay_ref, land_ref = refs
    x, y, c = _place()
    second = ((x + c) % 2, (y + 1 - c) % 2)
    return [pltpu.make_async_remote_copy(
        src_ref=relay_ref, dst_ref=land_ref.at[0], send_sem=send_sems.at[0], recv_sem=recv_sems.at[0],
        device_id=(*second, c), device_id_type=MESH)]


def _own_block_copies(targets):
    def copies(refs, send_sems, recv_sems):
        x, y, c = _place()
        mine = refs[0].at[4 * x + 2 * y + c]
        return [pltpu.make_async_remote_copy(
            src_ref=mine, dst_ref=mine, send_sem=send_sems.at[k], recv_sem=recv_sems.at[k],
            device_id=to, device_id_type=MESH) for k, to in enumerate(targets(x, y, c))]
    return copies


def _my_core_and_sibling(x, y, c):
    return [(x, y, 1 - c), (1 - x, y, c), (x, 1 - y, c), (1 - x, 1 - y, c)]


def _all_others(x, y, c):
    flip = lambda v, f: 1 - v if f else v
    return [(flip(x, r & 4), flip(y, r & 2), flip(c, r & 1)) for r in range(1, N_DEV)]


def _forward_copies(refs, send_sems, recv_sems):
    x, y, c = _place()
    chips = [(1 - x, y), (x, 1 - y), (1 - x, 1 - y)]
    return [pltpu.make_async_remote_copy(
        src_ref=refs[0].at[4 * chip[0] + 2 * chip[1] + c], dst_ref=refs[0].at[4 * chip[0] + 2 * chip[1] + c],
        send_sem=send_sems.at[k], recv_sem=recv_sems.at[k],
        device_id=(x, y, 1 - c), device_id_type=MESH) for k, chip in enumerate(chips)]


def _near_targets(x, y, c):
    return [(x, y, 1 - c), (1 - x, y, c), (x, 1 - y, c)]


def _second_stage_copies(refs, send_sems, recv_sems):
    x, y, c = _place()
    relayed = ((x + 1 - c) % 2, (y + c) % 2, c)
    relay_to = ((x + c) % 2, (y + 1 - c) % 2, c)
    plan = [((1 - x, y, c), (x, y, 1 - c)), ((x, 1 - y, c), (x, y, 1 - c)), (relayed, relay_to)]
    copies = []
    for k, ((px, py, pc), to) in enumerate(plan):
        blk = refs[0].at[4 * px + 2 * py + pc]
        copies.append(pltpu.make_async_remote_copy(
            src_ref=blk, dst_ref=blk, send_sem=send_sems.at[k], recv_sem=recv_sems.at[k],
            device_id=to, device_id_type=MESH))
    return copies


def _diagonal_forward_copies(refs, send_sems, recv_sems):
    x, y, c = _place()
    blk = refs[0].at[4 * (1 - x) + 2 * (1 - y) + c]
    return [pltpu.make_async_remote_copy(
        src_ref=blk, dst_ref=blk, send_sem=send_sems.at[0], recv_sem=recv_sems.at[0],
        device_id=(x, y, 1 - c), device_id_type=MESH)]


def _with_own_slot(block, me):
    return lax.dynamic_update_index_in_dim(lax.empty((N_DEV,) + block.shape, block.dtype), block, me, 0)


def _matmul(a, b, dims, out_dtype, tm, tn, name, dep=None):
    if dims == "nn":
        (m, k), n = a.shape, b.shape[1]
        a_spec = pl.BlockSpec((tm, k), lambda i, j: (i, 0))
        b_spec = pl.BlockSpec((k, tn), lambda i, j: (0, j))
        contract = ((1,), (0,))
    elif dims == "nt":
        (m, k), n = a.shape, b.shape[0]
        a_spec = pl.BlockSpec((tm, k), lambda i, j: (i, 0))
        b_spec = pl.BlockSpec((tn, k), lambda i, j: (j, 0))
        contract = ((1,), (1,))
    else:
        (k, m), n = a.shape, b.shape[1]
        a_spec = pl.BlockSpec((k, tm), lambda i, j: (0, i))
        b_spec = pl.BlockSpec((k, tn), lambda i, j: (0, j))
        contract = ((0,), (0,))
    assert m % tm == 0 and n % tn == 0 and a.dtype == BF16 and b.dtype == BF16

    def body(a_ref, b_ref, *rest):
        rest[-1][...] = lax.dot_general(a_ref[...], b_ref[...], (contract, ((), ())),
                                        preferred_element_type=F32).astype(out_dtype)

    deps = [] if dep is None else [dep]
    return pl.pallas_call(
        body, name=name, grid=(m // tm, n // tn),
        in_specs=[a_spec, b_spec] + [pl.BlockSpec((8, 128), lambda i, j: (0, 0))] * len(deps),
        out_specs=pl.BlockSpec((tm, tn), lambda i, j: (i, j)),
        out_shape=jax.ShapeDtypeStruct((m, n), out_dtype),
        compiler_params=_params(dimension_semantics=("arbitrary", "arbitrary")),
    )(a, b, *deps)


Z_TILE = 768
_Z_TILE_ORDER = ((0, 1, 2, 3, 4, 5, 6), (2, 0, 1, 6, 3, 4, 5), (4, 0, 5, 6, 1, 2, 3), (6, 2, 3, 4, 0, 1, 5))
_Z_EARLY_TILES = 4


def _z_proj(h, w_in_t, order, first, count, z_prev, name, dep=None):
    t = h.shape[0]

    def body(order_ref, h_ref, w_ref, *rest):
        rest[-1][...] = _dot_nt(h_ref[...], w_ref[...])

    prev = [] if z_prev is None else [z_prev]
    deps = [] if dep is None else [dep]
    return pl.pallas_call(
        body, name=name,
        grid_spec=pltpu.PrefetchScalarGridSpec(
            num_scalar_prefetch=1, grid=(count,),
            in_specs=[pl.BlockSpec((t, D_MODEL), lambda j, o: (0, 0)),
                      pl.BlockSpec((Z_TILE, D_MODEL), lambda j, o: (o[first + j], 0))]
            + [pl.BlockSpec(memory_space=pl.ANY)] * len(prev)
            + [pl.BlockSpec((8, 128), lambda j, o: (0, 0))] * len(deps),
            out_specs=pl.BlockSpec((t, Z_TILE), lambda j, o: (0, o[first + j]))),
        out_shape=jax.ShapeDtypeStruct((t, D_IN), F32),
        input_output_aliases={3: 0} if prev else {},
        compiler_params=_params(dimension_semantics=("arbitrary",)),
    )(order, h, w_in_t, *prev, *deps)


def _modulation(c_all, w_ada, b_ada_mine):
    def body(c_ref, w_ref, b_ref, act_ref, mod_ref):
        cv = c_ref[...]
        act = cv * _sigmoid(cv)
        act_ref[...] = act
        mod_ref[...] = jnp.dot(act.astype(BF16), w_ref[...].astype(BF16), preferred_element_type=F32) + b_ref[...]

    return pl.pallas_call(
        body, name="modulation",
        out_shape=(jax.ShapeDtypeStruct(c_all.shape, F32), jax.ShapeDtypeStruct((N_DEV, W_ADA_SHARD), F32)),
        compiler_params=_params(),
    )(c_all, w_ada, b_ada_mine)


def _modulated_norm(x, norm_g, scale, shift, tm=256):
    t, d = x.shape

    def body(x_ref, g_ref, sc_ref, sh_ref, h_ref):
        xv = x_ref[...]
        r = lax.rsqrt(jnp.mean(xv * xv, axis=-1, keepdims=True) + EPS)
        h = (xv * r) * g_ref[...] * (1.0 + sc_ref[...]) + sh_ref[...]
        h_ref[...] = h.astype(BF16)

    row = pl.BlockSpec((1, d), lambda i: (0, 0))
    return pl.pallas_call(
        body, name="modulated_norm", grid=(t // tm,),
        in_specs=[pl.BlockSpec((tm, d), lambda i: (i, 0)), row, row, row],
        out_specs=pl.BlockSpec((tm, d), lambda i: (i, 0)),
        out_shape=jax.ShapeDtypeStruct((t, d), BF16),
        compiler_params=_params(dimension_semantics=("arbitrary",)),
    )(x, norm_g, scale, shift)


def _window_bias(block_index):
    s = lax.broadcasted_iota(jnp.int32, (2 * BLOCK, BLOCK), 0)
    t = lax.broadcasted_iota(jnp.int32, (2 * BLOCK, BLOCK), 1)
    valid = ((s < BLOCK) & (s > t) & (block_index > 0)) | ((s >= BLOCK) & ((s - BLOCK) <= t))
    bias = jnp.where(valid, 0.0, -jnp.inf).astype(F32)
    return jnp.concatenate([bias] * 8, axis=1)


def _heads_t(pair_blocks, g):
    top = lax.broadcasted_iota(jnp.int32, (BLOCK, BLOCK), 0) < HEAD_DIM
    zeros = jnp.zeros((HEAD_DIM, BLOCK), F32)
    tiles = []
    for blk in pair_blocks:
        tp = blk.T
        if g == 0:
            tiles += [jnp.where(top, tp, 0.0), jnp.concatenate([tp[HEAD_DIM:], zeros], axis=0)]
        else:
            tiles += [jnp.concatenate([zeros, tp[:HEAD_DIM]], axis=0), jnp.where(top, 0.0, tp)]
    return jnp.concatenate(tiles, axis=1)


def _pair_block(xt, p, g):
    r0 = HEAD_DIM * g
    even = xt[r0:r0 + HEAD_DIM, (2 * p) * BLOCK:(2 * p + 1) * BLOCK]
    odd = xt[r0:r0 + HEAD_DIM, (2 * p + 1) * BLOCK:(2 * p + 2) * BLOCK]
    return jnp.concatenate([even, odd], axis=0).T


def _softmax_t(scores_t, bias, sink):
    st = scores_t + bias
    m = jnp.maximum(jnp.max(st, axis=0, keepdims=True), sink)
    e = jnp.exp(st - m)
    es = jnp.exp(sink - m)
    inv = 1.0 / (jnp.sum(e, axis=0, keepdims=True) + es)
    return e * inv, es * inv


def _dot(a, b):
    return jnp.dot(a, b, preferred_element_type=F32)


def _dot_nt(a, b):
    return lax.dot_general(a, b, (((1,), (1,)), ((), ())), preferred_element_type=F32)


def _layer_norm_fwd(v):
    mu = jnp.mean(v, axis=-1, keepdims=True)
    xc = v - mu
    rstd = lax.rsqrt(jnp.mean(xc * xc, axis=-1, keepdims=True) + EPS)
    return xc * rstd, rstd


def _tril(transposed=False):
    t = lax.broadcasted_iota(jnp.int32, (BLOCK, BLOCK), 0)
    s = lax.broadcasted_iota(jnp.int32, (BLOCK, BLOCK), 1)
    return s >= t if transposed else t >= s


def _const_spec(shape):
    return pl.BlockSpec(shape, lambda i: (0,) * len(shape))


def _kv_prev_spec(index):
    return pl.BlockSpec((BLOCK, 2 * D_KV), lambda i: (jnp.maximum(index(i) - 1, 0), SEG_KV // (2 * D_KV)))


def _keys_values(z_ref, kvp_ref):
    kvp, kvc = kvp_ref[...], z_ref[:, SEG_KV:SEG_KV + 2 * D_KV]
    kk = jnp.concatenate([kvp[:, :D_KV], kvc[:, :D_KV]], axis=0)
    vv = jnp.concatenate([kvp[:, D_KV:], kvc[:, D_KV:]], axis=0)
    return kk, vv


def _pair_cols(g, p, base=0):
    return slice(base + (4 * g + p) * 128, base + (4 * g + p + 1) * 128)


def _mixer_fwd(z, sink_rows, ln_g, ln_b, sgu_w, sgu_bt):
    t = z.shape[0]

    def body(z_ref, kvp_ref, sink_ref, lng_ref, lnb_ref, w_ref, bt_ref, a_ref):
        bias = _window_bias(pl.program_id(0))
        kk, vv = _keys_values(z_ref, kvp_ref)
        kk_b, vvt_b = kk.astype(BF16), vv.T.astype(BF16)
        for g in range(2):
            qt = _heads_t([z_ref[:, _pair_cols(g, p, SEG_Q)] * ATTN_SCALE for p in range(4)], g).astype(BF16)
            prob, _ = _softmax_t(_dot(kk_b, qt), bias, sink_ref[g])
            ot = _dot(vvt_b, prob.astype(BF16))
            for p in range(4):
                gate = z_ref[:, _pair_cols(g, p, SEG_GA)]
                a_ref[:, _pair_cols(g, p)] = (_pair_block(ot, p, g) * (gate * _sigmoid(gate))).astype(BF16)

        vhat, _ = _layer_norm_fwd(z_ref[:, SEG_VS:SEG_VS + D_SGU])
        vn = vhat * lng_ref[...] + lnb_ref[...]
        tril = _tril()
        for g in range(SGU_GROUPS):
            cols = slice(g * 128, (g + 1) * 128)
            wm = jnp.where(tril, w_ref[g], 0.0).astype(BF16)
            mixed = _dot(wm, vn[:, cols].astype(BF16)) + bt_ref[:, g:g + 1]
            gate = z_ref[:, SEG_GS + g * 128:SEG_GS + (g + 1) * 128]
            a_ref[:, D_ATTN + g * 128:D_ATTN + (g + 1) * 128] = (
                (z_ref[:, SEG_U + g * 128:SEG_U + (g + 1) * 128] * mixed) * (gate * _sigmoid(gate))).astype(BF16)

    return pl.pallas_call(
        body, name="mixer_fwd", grid=(t // BLOCK,),
        in_specs=[pl.BlockSpec((BLOCK, D_IN), lambda i: (i, 0)), _kv_prev_spec(lambda i: i),
                  _const_spec((2, 1, 8 * BLOCK)), _const_spec((1, D_SGU)), _const_spec((1, D_SGU)),
                  _const_spec((SGU_GROUPS, BLOCK, BLOCK)), _const_spec((BLOCK, SGU_GROUPS))],
        out_specs=pl.BlockSpec((BLOCK, D_MODEL), lambda i: (i, 0)),
        out_shape=jax.ShapeDtypeStruct((t, D_MODEL), BF16),
        compiler_params=_params(dimension_semantics=("arbitrary",)),
    )(z, z, sink_rows, ln_g, ln_b, sgu_w, sgu_bt)


def _mixer_bwd(z, da, sink_rows, ln_g, ln_b, sgu_w, sgu_wt, sgu_bt):
    t = z.shape[0]
    nb = t // BLOCK

    def body(z_ref, kvp_ref, da_ref, sink_ref, lng_ref, lnb_ref, w_ref, wt_ref, bt_ref,
             dz_ref, dsink_ref, dw_ref, db_ref, dlng_ref, dlnb_ref, carry_ref, dsink_acc, dbt_acc):
        step = pl.program_id(0)

        @pl.when(step == 0)
        def _():
            carry_ref[...] = jnp.zeros_like(carry_ref)
            dsink_acc[...] = jnp.zeros_like(dsink_acc)
            dbt_acc[...] = jnp.zeros_like(dbt_acc)
            dw_ref[...] = jnp.zeros_like(dw_ref)
            dlng_ref[...] = jnp.zeros_like(dlng_ref)
            dlnb_ref[...] = jnp.zeros_like(dlnb_ref)

        bias = _window_bias(nb - 1 - step)
        kk, vv = _keys_values(z_ref, kvp_ref)
        kk_b, vv_b = kk.astype(BF16), vv.astype(BF16)
        kkt_b, vvt_b = kk.T.astype(BF16), vv.T.astype(BF16)
        dkk = jnp.zeros((2 * BLOCK, D_KV), F32)
        dvv = jnp.zeros((2 * BLOCK, D_KV), F32)
        for g in range(2):
            qt = _heads_t([z_ref[:, _pair_cols(g, p, SEG_Q)] * ATTN_SCALE for p in range(4)], g).astype(BF16)
            prob, sink_prob = _softmax_t(_dot(kk_b, qt), bias, sink_ref[g])
            prob_b = prob.astype(BF16)
            ot = _dot(vvt_b, prob_b)
            gates = [z_ref[:, _pair_cols(g, p, SEG_GA)] for p in range(4)]
            sig = [_sigmoid(gt) for gt in gates]
            d_attn = [da_ref[:, _pair_cols(g, p)] for p in range(4)]
            d_ot = _heads_t([d_attn[p] * (gates[p] * sig[p]) for p in range(4)], g).astype(BF16)
            d_prob = _dot(vv_b, d_ot)
            delta = jnp.sum(prob * d_prob, axis=0, keepdims=True)
            d_scores = (prob * (d_prob - delta)).astype(BF16)
            dsink_acc[g] -= sink_prob * delta
            d_qt = _dot(kkt_b, d_scores)
            dkk = dkk + _dot_nt(d_scores, qt)
            dvv = dvv + _dot_nt(prob_b, d_ot)
            for p in range(4):
                dz_ref[:, _pair_cols(g, p, SEG_Q)] = (_pair_block(d_qt, p, g) * ATTN_SCALE).astype(BF16)
                d_silu = sig[p] * (1.0 + gates[p] * (1.0 - sig[p]))
                dz_ref[:, _pair_cols(g, p, SEG_GA)] = (d_attn[p] * _pair_block(ot, p, g) * d_silu).astype(BF16)
        d_kv = jnp.concatenate([dkk, dvv], axis=1)
        dz_ref[:, SEG_KV:SEG_KV + 2 * D_KV] = (d_kv[BLOCK:] + carry_ref[...]).astype(BF16)
        carry_ref[...] = d_kv[:BLOCK]

        vhat, rstd = _layer_norm_fwd(z_ref[:, SEG_VS:SEG_VS + D_SGU])
        lng = lng_ref[...]
        vn = vhat * lng + lnb_ref[...]
        tril, triu = _tril(), _tril(transposed=True)
        lane = lax.broadcasted_iota(jnp.int32, (BLOCK, 128), 1)
        d_bt = jnp.zeros((BLOCK, 128), F32)
        d_vn = []
        for g in range(SGU_GROUPS):
            cols = slice(g * 128, (g + 1) * 128)
            wm = jnp.where(tril, w_ref[g], 0.0).astype(BF16)
            wmt = jnp.where(triu, wt_ref[g], 0.0).astype(BF16)
            vn_g = vn[:, cols].astype(BF16)
            mixed = _dot(wm, vn_g) + bt_ref[:, g:g + 1]
            gate = z_ref[:, SEG_GS + g * 128:SEG_GS + (g + 1) * 128]
            u = z_ref[:, SEG_U + g * 128:SEG_U + (g + 1) * 128]
            d_out = da_ref[:, D_ATTN + g * 128:D_ATTN + (g + 1) * 128]
            sg = _sigmoid(gate)
            d_um = d_out * (gate * sg)
            dz_ref[:, SEG_U + g * 128:SEG_U + (g + 1) * 128] = (d_um * mixed).astype(BF16)
            dz_ref[:, SEG_GS + g * 128:SEG_GS + (g + 1) * 128] = (
                d_out * (u * mixed) * (sg * (1.0 + gate * (1.0 - sg)))).astype(BF16)
            d_mixed = d_um * u
            d_mixed_b = d_mixed.astype(BF16)
            dw_ref[g] += jnp.where(tril, _dot_nt(d_mixed_b, vn_g), 0.0)
            d_bt = d_bt + jnp.where(lane == g, jnp.sum(d_mixed, axis=-1, keepdims=True), 0.0)
            d_vn.append(_dot(wmt, d_mixed_b))
        dbt_acc[...] += d_bt
        d_vn = jnp.concatenate(d_vn, axis=1)
        dlng_ref[...] += jnp.sum(d_vn * vhat, axis=0, keepdims=True)
        dlnb_ref[...] += jnp.sum(d_vn, axis=0, keepdims=True)
        d_vhat = d_vn * lng
        d_v = rstd * (d_vhat - jnp.mean(d_vhat, axis=-1, keepdims=True)
                      - vhat * jnp.mean(d_vhat * vhat, axis=-1, keepdims=True))
        dz_ref[:, SEG_VS:SEG_VS + D_SGU] = d_v.astype(BF16)

        @pl.when(step == nb - 1)
        def _():
            db_ref[...] = dbt_acc[...].T[:SGU_GROUPS]
            lane_row = lax.broadcasted_iota(jnp.int32, (1, 128), 1)
            d_sink = jnp.zeros((1, 128), F32)
            for g in range(2):
                acc = dsink_acc[g]
                for j in range(8):
                    head_sum = jnp.sum(acc[:, j * BLOCK:(j + 1) * BLOCK], axis=-1, keepdims=True)
                    d_sink = d_sink + jnp.where(lane_row == 8 * g + j, head_sum, 0.0)
            dsink_ref[...] = d_sink

    rev = lambda i: nb - 1 - i
    return pl.pallas_call(
        body, name="mixer_bwd", grid=(nb,),
        in_specs=[pl.BlockSpec((BLOCK, D_IN), lambda i: (rev(i), 0)), _kv_prev_spec(rev),
                  pl.BlockSpec((BLOCK, D_MODEL), lambda i: (rev(i), 0)),
                  _const_spec((2, 1, 8 * BLOCK)), _const_spec((1, D_SGU)), _const_spec((1, D_SGU)),
                  _const_spec((SGU_GROUPS, BLOCK, BLOCK)), _const_spec((SGU_GROUPS, BLOCK, BLOCK)),
                  _const_spec((BLOCK, SGU_GROUPS))],
        out_specs=(pl.BlockSpec((BLOCK, D_IN), lambda i: (rev(i), 0)), _const_spec((1, 128)),
                   _const_spec((SGU_GROUPS, BLOCK, BLOCK)), _const_spec((SGU_GROUPS, BLOCK)),
                   _const_spec((1, D_SGU)), _const_spec((1, D_SGU))),
        out_shape=(jax.ShapeDtypeStruct((t, D_IN), BF16), jax.ShapeDtypeStruct((1, 128), F32),
                   jax.ShapeDtypeStruct((SGU_GROUPS, BLOCK, BLOCK), F32), jax.ShapeDtypeStruct((SGU_GROUPS, BLOCK), F32),
                   jax.ShapeDtypeStruct((1, D_SGU), F32), jax.ShapeDtypeStruct((1, D_SGU), F32)),
        scratch_shapes=[pltpu.VMEM((BLOCK, 2 * D_KV), F32), pltpu.VMEM((2, 1, 8 * BLOCK), F32),
                        pltpu.VMEM((BLOCK, 128), F32)],
        compiler_params=_params(dimension_semantics=("arbitrary",)),
    )(z, z, da, sink_rows, ln_g, ln_b, sgu_w, sgu_wt, sgu_bt)


def _out_proj_head(a, w_out_full, x, target, gate, final_g, tm=256):
    t, d = x.shape

    def body(a_ref, w_ref, x_ref, tg_ref, gate_ref, fg_ref, dx2_ref, dy_ref, loss_ref, dfg_ref, dgate_ref):
        @pl.when(pl.program_id(0) == 0)
        def _():
            loss_ref[...] = jnp.zeros_like(loss_ref)
            dfg_ref[...] = jnp.zeros_like(dfg_ref)
            dgate_ref[...] = jnp.zeros_like(dgate_ref)

        yv, gate, fg = _dot(a_ref[...], w_ref[...]), gate_ref[...], fg_ref[...]
        x2 = x_ref[...] + gate * yv
        r2 = lax.rsqrt(jnp.mean(x2 * x2, axis=-1, keepdims=True) + EPS)
        nrm = x2 * r2
        err = nrm * fg - tg_ref[...]
        loss_ref[...] += 0.5 * jnp.sum(jnp.mean(err * err, axis=-1, keepdims=True), axis=0, keepdims=True)
        d_out = err * (1.0 / d)
        dfg_ref[...] += jnp.sum(d_out * nrm, axis=0, keepdims=True)
        d_nrm = d_out * fg
        dx2 = r2 * (d_nrm - nrm * jnp.mean(d_nrm * nrm, axis=-1, keepdims=True))
        dx2_ref[...] = dx2
        dgate_ref[...] += jnp.sum(dx2 * yv, axis=0, keepdims=True)
        dy_ref[...] = (dx2 * gate).astype(BF16)

    blk = pl.BlockSpec((tm, d), lambda i: (i, 0))
    row = _const_spec((1, d))
    whole = pl.BlockSpec(w_out_full.shape, lambda i: (0, 0), pipeline_mode=pl.Buffered(1))
    return pl.pallas_call(
        body, name="out_proj_head", grid=(t // tm,),
        in_specs=[pl.BlockSpec((tm, a.shape[1]), lambda i: (i, 0)), whole, blk, blk, row, row],
        out_specs=(blk, blk, _const_spec((1, 128)), row, row),
        out_shape=(jax.ShapeDtypeStruct((t, d), F32), jax.ShapeDtypeStruct((t, d), BF16),
                   jax.ShapeDtypeStruct((1, 128), F32), jax.ShapeDtypeStruct((1, d), F32),
                   jax.ShapeDtypeStruct((1, d), F32)),
        compiler_params=_params(dimension_semantics=("arbitrary",)),
    )(a, w_out_full, x, target, gate, final_g)


def _z_proj_bwd_norm(dz, w_in_t, x, dx2, norm_g, scale, dep, tm=256):
    t, d = x.shape

    def body(dz_ref, w_ref, x_ref, dx2_ref, g_ref, sc_ref, dep_ref, gx_ref, dshift_ref, dscale_ref, dg_ref):
        @pl.when(pl.program_id(0) == 0)
        def _():
            dshift_ref[...] = jnp.zeros_like(dshift_ref)
            dscale_ref[...] = jnp.zeros_like(dscale_ref)
            dg_ref[...] = jnp.zeros_like(dg_ref)

        dh, xv, g = _dot(dz_ref[...], w_ref[...]), x_ref[...], g_ref[...]
        one_plus = 1.0 + sc_ref[...]
        r = lax.rsqrt(jnp.mean(xv * xv, axis=-1, keepdims=True) + EPS)
        xn = xv * r
        dshift_ref[...] += jnp.sum(dh, axis=0, keepdims=True)
        dscale_ref[...] += jnp.sum(dh * (xn * g), axis=0, keepdims=True)
        d_y = dh * one_plus
        dg_ref[...] += jnp.sum(d_y * xn, axis=0, keepdims=True)
        d_xn = d_y * g
        gx_ref[...] = dx2_ref[...] + r * (d_xn - xn * jnp.mean(d_xn * xn, axis=-1, keepdims=True))

    blk = pl.BlockSpec((tm, d), lambda i: (i, 0))
    row = _const_spec((1, d))
    whole = pl.BlockSpec(w_in_t.shape, lambda i: (0, 0), pipeline_mode=pl.Buffered(1))
    return pl.pallas_call(
        body, name="z_proj_bwd_norm", grid=(t // tm,),
        in_specs=[pl.BlockSpec((tm, dz.shape[1]), lambda i: (i, 0)), whole, blk, blk, row, row, _const_spec((8, 128))],
        out_specs=(blk, row, row, row),
        out_shape=(jax.ShapeDtypeStruct((t, d), F32),) + (jax.ShapeDtypeStruct((1, d), F32),) * 3,
        compiler_params=_params(dimension_semantics=("arbitrary",)),
    )(dz, w_in_t, x, dx2, norm_g, scale, dep)


def _adamw(w, g, m, v):
    m = ADAM_B1 * m + (1.0 - ADAM_B1) * g
    v = ADAM_B2 * v + (1.0 - ADAM_B2) * (g * g)
    m_hat = m / (1.0 - ADAM_B1 ** ADAM_STEP)
    v_hat = v / (1.0 - ADAM_B2 ** ADAM_STEP)
    delta = -ADAM_LR * (m_hat / (jnp.sqrt(v_hat) + ADAM_EPS) + ADAM_WD * w)
    return delta, m, v


def _relay_sum(second_chip, pair, land, tr):
    _, r, c = pair.shape

    def body(chip_ref, a_ref, b_ref, o_ref):
        o_ref[...] = (a_ref[...].astype(F32) + b_ref[...].astype(F32)).astype(BF16)

    return pl.pallas_call(
        body, name="w_in_grad_relay_sum",
        grid_spec=pltpu.PrefetchScalarGridSpec(
            num_scalar_prefetch=1, grid=(r // tr,),
            in_specs=[pl.BlockSpec((None, tr, c), lambda i, chip_ref: (chip_ref[0], i, 0)),
                      pl.BlockSpec((None, tr, c), lambda i, chip_ref: (1, i, 0))],
            out_specs=pl.BlockSpec((tr, c), lambda i, chip_ref: (i, 0))),
        out_shape=jax.ShapeDtypeStruct((r, c), BF16),
        compiler_params=_params(dimension_semantics=("arbitrary",)),
    )(second_chip, pair, land)


def _adam_from_chips(chip, pair, landed, w, m, v, name, tc):
    _, r, c = pair.shape
    n = len(landed)

    def body(chip_ref, own_ref, *refs):
        w_ref, m_ref, v_ref, g_ref, d_ref, nm_ref, nv_ref = refs[n:]
        g = own_ref[...].astype(F32)
        for k in range(n):
            g = g + refs[k][...].astype(F32)
        g_ref[...] = g
        d_ref[...], nm_ref[...], nv_ref[...] = _adamw(w_ref[...], g, m_ref[...], v_ref[...])

    def landed_spec(index):
        return pl.BlockSpec((None, r, tc), lambda i, chip_ref: (index, 0, i))

    blk = pl.BlockSpec((r, tc), lambda i, chip_ref: (0, i))
    return pl.pallas_call(
        body, name=name,
        grid_spec=pltpu.PrefetchScalarGridSpec(
            num_scalar_prefetch=1, grid=(c // tc,),
            in_specs=[pl.BlockSpec((None, r, tc), lambda i, chip_ref: (chip_ref[0], 0, i))]
            + [landed_spec(index) for _, index in landed] + [blk, blk, blk],
            out_specs=(blk,) * 4),
        out_shape=(jax.ShapeDtypeStruct((r, c), F32),) * 4,
        compiler_params=_params(dimension_semantics=("arbitrary",)),
    )(chip, pair, *[array for array, _ in landed], w, m, v)


def _adam_w_ada(act_t, dmod_mine, w, m, v, tr=256):
    r, c = w.shape

    def body(a_ref, dm_ref, w_ref, m_ref, v_ref, g_ref, d_ref, nm_ref, nv_ref):
        g = _dot(a_ref[...].astype(BF16), dm_ref[...].astype(BF16))
        g_ref[...] = g
        d_ref[...], nm_ref[...], nv_ref[...] = _adamw(w_ref[...], g, m_ref[...], v_ref[...])

    blk = pl.BlockSpec((tr, c), lambda i: (i, 0))
    return pl.pallas_call(
        body, name="adam_w_ada", grid=(r // tr,),
        in_specs=[pl.BlockSpec((tr, N_DEV), lambda i: (i, 0)), _const_spec((N_DEV, c)), blk, blk, blk],
        out_specs=(blk,) * 4, out_shape=(jax.ShapeDtypeStruct((r, c), F32),) * 4,
        compiler_params=_params(dimension_semantics=("arbitrary",)),
    )(act_t, dmod_mine, w, m, v)


def _pack_small(d_shift, d_scale, d_gate, d_norm_g, d_final_g, d_ln_g, d_ln_b, loss, d_sinks, d_sgu_b):
    def body(shift_ref, scale_ref, gate_ref, ng_ref, fg_ref, lng_ref, lnb_ref, loss_ref, sink_ref, b_ref, o_ref):
        o_ref[...] = jnp.zeros_like(o_ref)
        o_ref[ROW_SHIFT:ROW_SHIFT + 1, :] = shift_ref[...]
        o_ref[ROW_SCALE:ROW_SCALE + 1, :] = scale_ref[...]
        o_ref[ROW_GATE:ROW_GATE + 1, :] = gate_ref[...]
        o_ref[ROW_NORM_G:ROW_NORM_G + 1, :] = ng_ref[...]
        o_ref[ROW_FINAL_G:ROW_FINAL_G + 1, :] = fg_ref[...]
        o_ref[ROW_LN:ROW_LN + 1, 0:D_SGU] = lng_ref[...]
        o_ref[ROW_LN:ROW_LN + 1, D_SGU:2 * D_SGU] = lnb_ref[...]
        o_ref[ROW_MISC:ROW_MISC + 1, 0:128] = loss_ref[...]
        o_ref[ROW_MISC:ROW_MISC + 1, 128:256] = sink_ref[...]
        o_ref[ROW_SGU_B:ROW_SGU_B + SGU_GROUPS, 0:BLOCK] = b_ref[...]

    return pl.pallas_call(
        body, name="pack_small", out_shape=jax.ShapeDtypeStruct((SMALL_ROWS, D_MODEL), F32),
        compiler_params=_params(),
    )(d_shift, d_scale, d_gate, d_norm_g, d_final_g, d_ln_g, d_ln_b, loss, d_sinks, d_sgu_b)


_SMALL_NAMES = ("norm_g", "b_ada", "attn_sinks", "sgu_ln_g", "sgu_ln_b", "sgu_w", "sgu_b", "final_g")


def _adam_small(partials, d_sgu_w_all, weights, moments_m, moments_v):
    names = _SMALL_NAMES
    k = len(names)

    def body(*refs):
        p_ref, sw_ref = refs[0], refs[1]
        w_refs, m_refs, v_refs = refs[2:2 + k], refs[2 + k:2 + 2 * k], refs[2 + 2 * k:2 + 3 * k]
        loss_ref, dmod_ref = refs[2 + 3 * k], refs[3 + 3 * k]
        out_refs = refs[4 + 3 * k:4 + 7 * k]
        sum_ref = refs[4 + 7 * k]
        total = p_ref[0]
        for j in range(1, N_DEV):
            total = total + p_ref[j]
        sum_ref[...] = total
        for j in range(N_DEV):
            for part, row in enumerate((ROW_SHIFT, ROW_SCALE, ROW_GATE)):
                dmod_ref[j:j + 1, part * D_MODEL:(part + 1) * D_MODEL] = p_ref[j, row:row + 1, :]
        loss_ref[...] = sum_ref[ROW_MISC:ROW_MISC + 1, 0:1]
        d_sgu_w = sw_ref[0]
        for j in range(1, N_DEV):
            d_sgu_w = d_sgu_w + sw_ref[j]
        grads = {
            "norm_g": sum_ref[ROW_NORM_G:ROW_NORM_G + 1, :],
            "b_ada": jnp.concatenate([sum_ref[r:r + 1, :] for r in (ROW_SHIFT, ROW_SCALE, ROW_GATE)], axis=1),
            "attn_sinks": sum_ref[ROW_MISC:ROW_MISC + 1, 128:128 + N_Q_HEADS],
            "sgu_ln_g": sum_ref[ROW_LN:ROW_LN + 1, 0:D_SGU],
            "sgu_ln_b": sum_ref[ROW_LN:ROW_LN + 1, D_SGU:2 * D_SGU],
            "sgu_w": d_sgu_w[None],
            "sgu_b": sum_ref[ROW_SGU_B:ROW_SGU_B + SGU_GROUPS, 0:BLOCK][None],
            "final_g": sum_ref[ROW_FINAL_G:ROW_FINAL_G + 1, :],
        }
        for i, name in enumerate(names):
            g = grads[name]
            delta, m, v = _adamw(w_refs[i][...], g, m_refs[i][...], v_refs[i][...])
            out_refs[4 * i][...] = g
            out_refs[4 * i + 1][...] = delta
            out_refs[4 * i + 2][...] = m
            out_refs[4 * i + 3][...] = v

    shapes = [jax.ShapeDtypeStruct((1, 1), F32), jax.ShapeDtypeStruct((N_DEV, 3 * D_MODEL), F32)]
    for name in names:
        shapes += [jax.ShapeDtypeStruct(weights[name].shape, F32)] * 4
    outs = pl.pallas_call(
        body, name="adam_small", out_shape=tuple(shapes),
        scratch_shapes=[pltpu.VMEM((SMALL_ROWS, D_MODEL), F32)],
        compiler_params=_params(),
    )(partials, d_sgu_w_all, *[weights[n] for n in names], *[moments_m[n] for n in names],
      *[moments_v[n] for n in names])
    return outs[0], outs[1], {name: outs[2 + 4 * i:6 + 4 * i] for i, name in enumerate(names)}


def kernel(x, c, norm_g, w_ada, b_ada, w_in, attn_sinks, sgu_ln_g, sgu_ln_b, sgu_w, sgu_b, w_out, final_g, loss_target, m_norm_g, m_w_ada, m_b_ada, m_w_in, m_attn_sinks, m_sgu_ln_g, m_sgu_ln_b, m_sgu_w, m_sgu_b, m_w_out, m_final_g, v_norm_g, v_w_ada, v_b_ada, v_w_in, v_attn_sinks, v_sgu_ln_g, v_sgu_ln_b, v_sgu_w, v_sgu_b, v_w_out, v_final_g):
    xi, yi, ci = _place()
    me = 4 * xi + 2 * yi + ci
    x2d, target = x[0], loss_target[0]
    t = x2d.shape[0]

    core = ci.astype(jnp.int32).reshape(1)
    chip = (2 * xi + yi).astype(jnp.int32).reshape(1)

    c_all = _all_gather([c.reshape(8, 256)], "gather_c", True)[0].reshape(N_DEV, D_MODEL)
    b_mine = lax.dynamic_slice(b_ada, (0, me * W_ADA_SHARD), (1, W_ADA_SHARD))
    c_act, mod_part = _modulation(c_all, w_ada[0], b_mine)
    mod_all = _all_gather([mod_part], "gather_mod", True)[0]

    near = _own_block_copies(_near_targets)
    w_in_flight = _start_copies([_with_own_slot(w_in[0].T.astype(BF16), me)], near, 3, mod_all, "gather_w_in_start")
    mod = lax.dynamic_index_in_dim(mod_all, me, axis=1, keepdims=False).reshape(1, 3 * D_MODEL)
    mod = mod + w_in_flight[3][0, 0]
    shift, scale, gate = mod[:, :D_MODEL], mod[:, D_MODEL:2 * D_MODEL], mod[:, 2 * D_MODEL:]
    h = _modulated_norm(x2d, norm_g, scale, shift)

    w_in_pair = _wait_copies(w_in_flight, lambda *a: near(*a)[:1], h, "gather_w_in_sibling_wait")
    tile_order = jnp.asarray(_Z_TILE_ORDER, jnp.int32)[chip[0]]
    z_own = _z_proj(h, w_in_pair[0].reshape(D_IN, D_MODEL), tile_order, 0, 1, None, "z_proj_own")
    w_in_flight = _wait_then_start((w_in_flight[0], w_in_flight[1], w_in_pair, None), lambda *a: near(*a)[1:],
                                   _second_stage_copies, 3, z_own, "gather_w_in_second_stage")
    w_in_most = _wait_copies(w_in_flight, lambda *a: _second_stage_copies(*a)[:2], z_own, "gather_w_in_forward_wait")
    z_early = _z_proj(h, w_in_most[0].reshape(D_IN, D_MODEL), tile_order, 1, _Z_EARLY_TILES - 1, z_own, "z_proj_early")
    w_out_flight = _start_copies([_with_own_slot(w_out[0].astype(BF16), me)], _own_block_copies(_my_core_and_sibling),
                                 4, z_early, "gather_w_out_start")
    w_in_flight = _wait_then_start((w_in_flight[0], w_in_flight[1], w_in_most, None),
                                   lambda *a: _second_stage_copies(*a)[2:], _diagonal_forward_copies, 1,
                                   w_out_flight[3], "gather_w_in_last_stage")
    w_in_all = _wait_copies(w_in_flight, _diagonal_forward_copies, z_early, "gather_w_in_last_wait")[0]
    w_in_t = w_in_all.reshape(D_IN, D_MODEL)
    z = _z_proj(h, w_in_t, tile_order, _Z_EARLY_TILES, 7 - _Z_EARLY_TILES, z_early, "z_proj_late")
    w_out_flight = _wait_then_start(w_out_flight, _own_block_copies(_my_core_and_sibling), _forward_copies, 3, z,
                                    "gather_w_out_forward_stage")
    sink_rows = jnp.repeat(attn_sinks.reshape(N_Q_HEADS), BLOCK).reshape(2, 1, 8 * BLOCK)
    sgu_bt = sgu_b[0].T
    a = _mixer_fwd(z, sink_rows + w_out_flight[3][0, 0], sgu_ln_g, sgu_ln_b, sgu_w[0], sgu_bt)
    w_out_all = _wait_copies(w_out_flight, _forward_copies, a, "gather_w_out_forward_wait")[0]
    w_out_full = w_out_all.reshape(D_MODEL, D_MODEL)
    final_g_row = final_g.reshape(1, D_MODEL)
    dx2, dy, loss_part, d_final_g, d_gate = _out_proj_head(a, w_out_full, x2d, target, gate, final_g_row)

    da = _matmul(dy, w_out_full, "nt", F32, min(t, 1024), 1024, "out_proj_bwd")
    dw_out = _matmul(a, dy, "tn", BF16, 1024, 1024, "w_out_grad").reshape(4, 2, W_OUT_SHARD, D_MODEL)
    pair_out = _pair_reduce(dw_out, "w_out_grad_pair_reduce", W_OUT_SHARD // 2)
    out_flight = _start_copies([pair_out, lax.empty((3, W_OUT_SHARD, D_MODEL), BF16)], _chip_copies, 3, core,
                               "w_out_grad_chip_start")
    dz, d_sinks, d_sgu_w, d_sgu_b, d_ln_g, d_ln_b = _mixer_bwd(
        z, da, sink_rows + out_flight[3][0, 0], sgu_ln_g, sgu_ln_b, sgu_w[0], jnp.swapaxes(sgu_w[0], 1, 2), sgu_bt)
    sgu_w_flight = _start_copies([_with_own_slot(d_sgu_w, me)], _own_block_copies(_all_others), N_DEV - 1, core,
                                 "sgu_w_grad_gather_start")
    dw_in_t = _matmul(dz, h, "tn", BF16, 768, D_MODEL, "w_in_grad", dep=sgu_w_flight[3])
    dw_in_t = dw_in_t.reshape(4, 2, W_IN_SHARD, D_MODEL)
    pair_in = _pair_reduce(dw_in_t, "w_in_grad_pair_reduce", W_IN_SHARD // 3)
    hop1 = _start_copies([pair_in, lax.empty((2, W_IN_SHARD, D_MODEL), BF16)], _first_hop_copies, 2, core,
                         "w_in_grad_first_hop_start")
    grad_x, d_shift, d_scale, d_norm_g = _z_proj_bwd_norm(dz, w_in_t, x2d, dx2, norm_g, scale, hop1[3])

    partial = _pack_small(d_shift, d_scale, d_gate, d_norm_g, d_final_g, d_ln_g, d_ln_b, loss_part, d_sinks, d_sgu_b)
    partial_all = _all_gather([partial], "gather_small", True)[0]
    d_sgu_w_all = _wait_copies(sgu_w_flight, _own_block_copies(_all_others), partial_all, "sgu_w_grad_gather_wait")[0]
    pair_in, land_first = _wait_copies(hop1, _first_hop_copies, partial_all, "w_in_grad_first_hop_wait")
    second_chip = (2 * ((xi + ci) % 2) + (yi + 1 - ci) % 2).astype(jnp.int32).reshape(1)
    relay = _relay_sum(second_chip, pair_in, land_first, W_IN_SHARD // 3)
    hop2 = _start_copies([relay, lax.empty((1, W_IN_SHARD, D_MODEL), BF16)], _second_hop_copies, 1, core,
                         "w_in_grad_second_hop_start")
    weights = {"norm_g": norm_g, "b_ada": b_ada, "attn_sinks": attn_sinks + hop2[3][0, 0], "sgu_ln_g": sgu_ln_g,
               "sgu_ln_b": sgu_ln_b, "sgu_w": sgu_w, "sgu_b": sgu_b, "final_g": final_g_row}
    moments_m = {"norm_g": m_norm_g, "b_ada": m_b_ada, "attn_sinks": m_attn_sinks, "sgu_ln_g": m_sgu_ln_g,
                 "sgu_ln_b": m_sgu_ln_b, "sgu_w": m_sgu_w, "sgu_b": m_sgu_b,
                 "final_g": m_final_g.reshape(1, D_MODEL)}
    moments_v = {"norm_g": v_norm_g, "b_ada": v_b_ada, "attn_sinks": v_attn_sinks, "sgu_ln_g": v_sgu_ln_g,
                 "sgu_ln_b": v_sgu_ln_b, "sgu_w": v_sgu_w, "sgu_b": v_sgu_b,
                 "final_g": v_final_g.reshape(1, D_MODEL)}
    loss, dmod_all, small = _adam_small(partial_all, d_sgu_w_all, weights, moments_m, moments_v)
    small["final_g"] = tuple(o.reshape(D_MODEL) for o in small["final_g"])

    dmod_mine = lax.dynamic_slice(dmod_all, (0, me * W_ADA_SHARD), (N_DEV, W_ADA_SHARD))
    big = {"w_ada": _adam_w_ada(c_act.T, dmod_mine, w_ada[0], m_w_ada[0], v_w_ada[0])}
    pair_out, land_out = _wait_copies(out_flight, _chip_copies, big["w_ada"][0], "w_out_grad_chip_wait")
    big["w_out"] = _adam_from_chips(chip, pair_out, [(land_out, k) for k in range(3)], w_out[0], m_w_out[0], v_w_out[0],
                                    "adam_w_out", 1024)
    _, land_second = _wait_copies(hop2, _second_hop_copies, big["w_out"][0], "w_in_grad_second_hop_wait")
    big["w_in"] = tuple(o.T for o in _adam_from_chips(
        chip, pair_in, [(land_first, 0), (land_second, 0)], w_in[0].T, m_w_in[0].T, v_w_in[0].T, "adam_w_in", 256))
    order = ["norm_g", "w_ada", "b_ada", "w_in", "attn_sinks", "sgu_ln_g", "sgu_ln_b", "sgu_w", "sgu_b", "w_out",
             "final_g"]
    outs = [loss.reshape(()), grad_x[None]]
    for k in range(4):
        for name in order:
            outs.append(big[name][k][None] if name in big else small[name][k])
    return tuple(outs)
```

```python
import jax
import jax.numpy as jnp
from jax import lax
from jax.experimental import pallas as pl
from jax.experimental.pallas import tpu as pltpu

F32 = jnp.float32
BF16 = jnp.bfloat16
MESH = pl.DeviceIdType.MESH

N_DEV = 8
D_MODEL = 2048
HEAD_DIM = 64
D_ATTN = 1024
N_Q_HEADS = 16
D_KV = 128
BLOCK = 128
D_SGU = 1024
SGU_GROUPS = 8
D_IN = 5376
W_IN_SHARD = D_IN // N_DEV
W_OUT_SHARD = D_MODEL // N_DEV
W_ADA_SHARD = 3 * D_MODEL // N_DEV
EPS = 1e-6
ATTN_SCALE = 0.125

ADAM_LR = 0.001
ADAM_B1 = 0.9
ADAM_B2 = 0.999
ADAM_EPS = 1e-08
ADAM_WD = 0.01
ADAM_STEP = 10

SEG_Q, SEG_KV, SEG_GA, SEG_U, SEG_VS, SEG_GS = 0, 1024, 1280, 2304, 3328, 4352

VMEM_LIMIT = 56 * 1024 * 1024

ROW_SHIFT, ROW_SCALE, ROW_GATE, ROW_NORM_G, ROW_FINAL_G, ROW_LN, ROW_MISC, ROW_SGU_B = 0, 1, 2, 3, 4, 5, 6, 8
SMALL_ROWS = 16


def _params(**kw):
    return pltpu.CompilerParams(vmem_limit_bytes=VMEM_LIMIT, **kw)


def _sigmoid(x):
    return 0.5 * (jnp.tanh(0.5 * x) + 1.0)


def _place():
    return lax.axis_index("x"), lax.axis_index("y"), lax.axis_index("c")


def _all_gather_small(shard, name):
    def body(in_ref, out_ref, send_sems, recv_sems):
        x, y, c = _place()
        me = 4 * x + 2 * y + c
        out_ref[me] = in_ref[...]
        copies = []
        for r in range(1, N_DEV):
            to = ((x + (r >> 2)) % 2, (y + ((r >> 1) & 1)) % 2, (c + (r & 1)) % 2)
            cp = pltpu.make_async_remote_copy(
                src_ref=in_ref, dst_ref=out_ref.at[me], send_sem=send_sems.at[r - 1], recv_sem=recv_sems.at[r - 1],
                device_id=to, device_id_type=MESH)
            cp.start()
            copies.append(cp)
        for cp in copies:
            cp.wait()

    spec = pl.BlockSpec(memory_space=pltpu.VMEM)
    return pl.pallas_call(
        body, name=name, out_shape=jax.ShapeDtypeStruct((N_DEV,) + shard.shape, shard.dtype),
        in_specs=[spec], out_specs=spec,
        scratch_shapes=[pltpu.SemaphoreType.DMA((N_DEV - 1,)), pltpu.SemaphoreType.DMA((N_DEV - 1,))],
        compiler_params=_params(),
    )(shard)


def _pair_reduce(blocks, name, row_chunk):
    _, _, r, cols = blocks.shape
    assert r % row_chunk == 0

    def body(in_ref, out_ref, land, own, summed, send_sems, recv_sems, own_sems, out_sems):
        x, y, c = _place()
        sends, loads, stores = [], [], []
        for m in range(4):
            cp = pltpu.make_async_remote_copy(
                src_ref=in_ref.at[m, 1 - c], dst_ref=land.at[m], send_sem=send_sems.at[m], recv_sem=recv_sems.at[m],
                device_id=(x, y, 1 - c), device_id_type=MESH)
            cp.start()
            sends.append(cp)
            ld = pltpu.make_async_copy(in_ref.at[m, c], own.at[m], own_sems.at[m])
            ld.start()
            loads.append(ld)
        for m in range(4):
            sends[m].wait_recv()
            loads[m].wait()
            for k in range(r // row_chunk):
                rows = slice(k * row_chunk, (k + 1) * row_chunk)
                summed[m, rows, :] = (own[m, rows, :].astype(F32) + land[m, rows, :].astype(F32)).astype(BF16)
            st = pltpu.make_async_copy(summed.at[m], out_ref.at[m], out_sems.at[m])
            st.start()
            stores.append(st)
        for m in range(4):
            sends[m].wait_send()
            stores[m].wait()

    spec = pl.BlockSpec(memory_space=pl.ANY)
    return pl.pallas_call(
        body, name=name, out_shape=jax.ShapeDtypeStruct((4, r, cols), BF16),
        in_specs=[spec], out_specs=spec,
        scratch_shapes=[pltpu.VMEM((4, r, cols), BF16), pltpu.VMEM((4, r, cols), BF16), pltpu.VMEM((4, r, cols), BF16),
                        pltpu.SemaphoreType.DMA((4,)), pltpu.SemaphoreType.DMA((4,)), pltpu.SemaphoreType.DMA((4,)),
                        pltpu.SemaphoreType.DMA((4,))],
        compiler_params=_params(),
    )(blocks)


_HBM = pl.BlockSpec(memory_space=pltpu.HBM)
_SEM = pl.BlockSpec(memory_space=pltpu.SEMAPHORE)
_EFFECT = pltpu.SideEffectType.DATAFLOW_SIDE_EFFECTING


def _start_copies(bufs, copies, n_copies, after, name):
    nb = len(bufs)

    def body(*refs):
        for cp in copies(refs[:nb], refs[nb + 1], refs[nb + 2]):
            cp.start()
        refs[-1][...] = jnp.zeros_like(refs[-1])

    out = pl.pallas_call(
        body, name=name,
        out_shape=(pltpu.SemaphoreType.DMA((n_copies,)), pltpu.SemaphoreType.DMA((n_copies,)),
                   *[pltpu.HBM(b.shape, b.dtype) for b in bufs], jax.ShapeDtypeStruct((8, 128), F32)),
        in_specs=(_HBM,) * nb + (pl.BlockSpec(memory_space=pl.ANY),),
        out_specs=(_SEM, _SEM) + (_HBM,) * nb + (pl.BlockSpec(memory_space=pltpu.VMEM),),
        input_output_aliases={i: 2 + i for i in range(nb)},
        compiler_params=pltpu.CompilerParams(has_side_effects=_EFFECT),
    )(*[pltpu.with_memory_space_constraint(b, pltpu.HBM) for b in bufs], after)
    return out[0], out[1], list(out[2:2 + nb]), out[-1]


def _wait_copies(flight, copies, after, name):
    send_sems, recv_sems, bufs, _ = flight
    nb = len(bufs)

    def body(*refs):
        for cp in copies(refs[:nb], refs[nb], refs[nb + 1]):
            cp.wait_send()
            cp.wait_recv()

    return pl.pallas_call(
        body, name=name,
        out_shape=tuple(pltpu.HBM(b.shape, b.dtype) for b in bufs),
        in_specs=(_HBM,) * nb + (_SEM, _SEM, pl.BlockSpec(memory_space=pl.ANY)), out_specs=(_HBM,) * nb,
        input_output_aliases={i: i for i in range(nb)},
        compiler_params=pltpu.CompilerParams(has_side_effects=_EFFECT),
    )(*bufs, send_sems, recv_sems, after)


def _wait_then_start(flight, waited, started, n_started, after, name):
    old_send, old_recv, bufs, _ = flight
    nb = len(bufs)

    def body(*refs):
        for cp in waited(refs[:nb], refs[nb], refs[nb + 1]):
            cp.wait_send()
            cp.wait_recv()
        for cp in started(refs[:nb], refs[nb + 3], refs[nb + 4]):
            cp.start()
        refs[-1][...] = jnp.zeros_like(refs[-1])

    out = pl.pallas_call(
        body, name=name,
        out_shape=(pltpu.SemaphoreType.DMA((n_started,)), pltpu.SemaphoreType.DMA((n_started,)),
                   *[pltpu.HBM(b.shape, b.dtype) for b in bufs], jax.ShapeDtypeStruct((8, 128), F32)),
        in_specs=(_HBM,) * nb + (_SEM, _SEM, pl.BlockSpec(memory_space=pl.ANY)),
        out_specs=(_SEM, _SEM) + (_HBM,) * nb + (pl.BlockSpec(memory_space=pltpu.VMEM),),
        input_output_aliases={i: 2 + i for i in range(nb)},
        compiler_params=pltpu.CompilerParams(has_side_effects=_EFFECT),
    )(*bufs, old_send, old_recv, after)
    return out[0], out[1], list(out[2:2 + nb]), out[-1]


def _chip_copies(refs, send_sems, recv_sems):
    pair_ref, land_ref = refs
    x, y, c = _place()
    chips = [(1 - x, y), (x, 1 - y), (1 - x, 1 - y)]
    return [pltpu.make_async_remote_copy(
        src_ref=pair_ref.at[2 * chip[0] + chip[1]], dst_ref=land_ref.at[k],
        send_sem=send_sems.at[k], recv_sem=recv_sems.at[k],
        device_id=(*chip, c), device_id_type=MESH) for k, chip in enumerate(chips)]


def _first_hop_copies(refs, send_sems, recv_sems):
    pair_ref, land_ref = refs
    x, y, c = _place()
    first = ((x + 1 - c) % 2, (y + c) % 2)
    blocks = [2 * first[0] + first[1], 2 * (1 - x) + (1 - y)]
    return [pltpu.make_async_remote_copy(
        src_ref=pair_ref.at[blocks[k]], dst_ref=land_ref.at[k], send_sem=send_sems.at[k], recv_sem=recv_sems.at[k],
        device_id=(*first, c), device_id_type=MESH) for k in range(2)]


def _second_hop_copies(refs, send_sems, recv_sems):
    relay_ref, land_ref = refs
    x, y, c = _place()
    second = ((x + c) % 2, (y + 1 - c) % 2)
    return [pltpu.make_async_remote_copy(
        src_ref=relay_ref, dst_ref=land_ref.at[0], send_sem=send_sems.at[0], recv_sem=recv_sems.at[0],
        device_id=(*second, c), device_id_type=MESH)]


def _own_block_copies(targets):
    def copies(refs, send_sems, recv_sems):
        x, y, c = _place()
        mine = refs[0].at[4 * x + 2 * y + c]
        return [pltpu.make_async_remote_copy(
            src_ref=mine, dst_ref=mine, send_sem=send_sems.at[k], recv_sem=recv_sems.at[k],
            device_id=to, device_id_type=MESH) for k, to in enumerate(targets(x, y, c))]
    return copies


def _my_core_and_sibling(x, y, c):
    return [(x, y, 1 - c), (1 - x, y, c), (x, 1 - y, c), (1 - x, 1 - y, c)]


def _all_others(x, y, c):
    flip = lambda v, f: 1 - v if f else v
    return [(flip(x, r & 4), flip(y, r & 2), flip(c, r & 1)) for r in range(1, N_DEV)]


def _forward_copies(refs, send_sems, recv_sems):
    x, y, c = _place()
    chips = [(1 - x, y), (x, 1 - y), (1 - x, 1 - y)]
    return [pltpu.make_async_remote_copy(
        src_ref=refs[0].at[4 * chip[0] + 2 * chip[1] + c], dst_ref=refs[0].at[4 * chip[0] + 2 * chip[1] + c],
        send_sem=send_sems.at[k], recv_sem=recv_sems.at[k],
        device_id=(x, y, 1 - c), device_id_type=MESH) for k, chip in enumerate(chips)]


def _near_targets(x, y, c):
    return [(x, y, 1 - c), (1 - x, y, c), (x, 1 - y, c)]


def _second_stage_copies(refs, send_sems, recv_sems):
    x, y, c = _place()
    relayed = ((x + 1 - c) % 2, (y + c) % 2, c)
    relay_to = ((x + c) % 2, (y + 1 - c) % 2, c)
    plan = [((1 - x, y, c), (x, y, 1 - c)), ((x, 1 - y, c), (x, y, 1 - c)), (relayed, relay_to)]
    copies = []
    for k, ((px, py, pc), to) in enumerate(plan):
        blk = refs[0].at[4 * px + 2 * py + pc]
        copies.append(pltpu.make_async_remote_copy(
            src_ref=blk, dst_ref=blk, send_sem=send_sems.at[k], recv_sem=recv_sems.at[k],
            device_id=to, device_id_type=MESH))
    return copies


def _diagonal_forward_copies(refs, send_sems, recv_sems):
    x, y, c = _place()
    blk = refs[0].at[4 * (1 - x) + 2 * (1 - y) + c]
    return [pltpu.make_async_remote_copy(
        src_ref=blk, dst_ref=blk, send_sem=send_sems.at[0], recv_sem=recv_sems.at[0],
        device_id=(x, y, 1 - c), device_id_type=MESH)]


def _with_own_slot(block, me):
    return lax.dynamic_update_index_in_dim(lax.empty((N_DEV,) + block.shape, block.dtype), block, me, 0)


def _matmul(a, b, dims, out_dtype, tm, tn, name, dep=None):
    if dims == "nn":
        (m, k), n = a.shape, b.shape[1]
        a_spec = pl.BlockSpec((tm, k), lambda i, j: (i, 0))
        b_spec = pl.BlockSpec((k, tn), lambda i, j: (0, j))
        contract = ((1,), (0,))
    elif dims == "nt":
        (m, k), n = a.shape, b.shape[0]
        a_spec = pl.BlockSpec((tm, k), lambda i, j: (i, 0))
        b_spec = pl.BlockSpec((tn, k), lambda i, j: (j, 0))
        contract = ((1,), (1,))
    else:
        (k, m), n = a.shape, b.shape[1]
        a_spec = pl.BlockSpec((k, tm), lambda i, j: (0, i))
        b_spec = pl.BlockSpec((k, tn), lambda i, j: (0, j))
        contract = ((0,), (0,))
    assert m % tm == 0 and n % tn == 0 and a.dtype == BF16 and b.dtype == BF16

    def body(a_ref, b_ref, *rest):
        rest[-1][...] = lax.dot_general(a_ref[...], b_ref[...], (contract, ((), ())),
                                        preferred_element_type=F32).astype(out_dtype)

    deps = [] if dep is None else [dep]
    return pl.pallas_call(
        body, name=name, grid=(m // tm, n // tn),
        in_specs=[a_spec, b_spec] + [pl.BlockSpec((8, 128), lambda i, j: (0, 0))] * len(deps),
        out_specs=pl.BlockSpec((tm, tn), lambda i, j: (i, j)),
        out_shape=jax.ShapeDtypeStruct((m, n), out_dtype),
        compiler_params=_params(dimension_semantics=("arbitrary", "arbitrary")),
    )(a, b, *deps)


Z_TILE = 768
_Z_TILE_ORDER = ((0, 1, 2, 3, 4, 5, 6), (2, 0, 1, 6, 3, 4, 5), (4, 0, 5, 6, 1, 2, 3), (6, 2, 3, 4, 0, 1, 5))
_Z_EARLY_TILES = 4


def _z_proj(h, w_in_t, order, first, count, z_prev, name, dep=None):
    t = h.shape[0]

    def body(order_ref, h_ref, w_ref, *rest):
        rest[-1][...] = _dot_nt(h_ref[...], w_ref[...])

    prev = [] if z_prev is None else [z_prev]
    deps = [] if dep is None else [dep]
    return pl.pallas_call(
        body, name=name,
        grid_spec=pltpu.PrefetchScalarGridSpec(
            num_scalar_prefetch=1, grid=(count,),
            in_specs=[pl.BlockSpec((t, D_MODEL), lambda j, o: (0, 0)),
                      pl.BlockSpec((Z_TILE, D_MODEL), lambda j, o: (o[first + j], 0))]
            + [pl.BlockSpec(memory_space=pl.ANY)] * len(prev)
            + [pl.BlockSpec((8, 128), lambda j, o: (0, 0))] * len(deps),
            out_specs=pl.BlockSpec((t, Z_TILE), lambda j, o: (0, o[first + j]))),
        out_shape=jax.ShapeDtypeStruct((t, D_IN), F32),
        input_output_aliases={3: 0} if prev else {},
        compiler_params=_params(dimension_semantics=("arbitrary",)),
    )(order, h, w_in_t, *prev, *deps)


def _modulation(c_all, w_ada, b_ada_mine):
    def body(c_ref, w_ref, b_ref, act_ref, mod_ref):
        cv = c_ref[...]
        act = cv * _sigmoid(cv)
        act_ref[...] = act
        mod_ref[...] = jnp.dot(act.astype(BF16), w_ref[...].astype(BF16), preferred_element_type=F32) + b_ref[...]

    return pl.pallas_call(
        body, name="modulation",
        out_shape=(jax.ShapeDtypeStruct(c_all.shape, F32), jax.ShapeDtypeStruct((N_DEV, W_ADA_SHARD), F32)),
        compiler_params=_params(),
    )(c_all, w_ada, b_ada_mine)


def _modulated_norm(x, norm_g, scale, shift, tm=256):
    t, d = x.shape

    def body(x_ref, g_ref, sc_ref, sh_ref, h_ref):
        xv = x_ref[...]
        r = lax.rsqrt(jnp.mean(xv * xv, axis=-1, keepdims=True) + EPS)
        h = (xv * r) * g_ref[...] * (1.0 + sc_ref[...]) + sh_ref[...]
        h_ref[...] = h.astype(BF16)

    row = pl.BlockSpec((1, d), lambda i: (0, 0))
    return pl.pallas_call(
        body, name="modulated_norm", grid=(t // tm,),
        in_specs=[pl.BlockSpec((tm, d), lambda i: (i, 0)), row, row, row],
        out_specs=pl.BlockSpec((tm, d), lambda i: (i, 0)),
        out_shape=jax.ShapeDtypeStruct((t, d), BF16),
        compiler_params=_params(dimension_semantics=("arbitrary",)),
    )(x, norm_g, scale, shift)


def _window_bias(block_index):
    s = lax.broadcasted_iota(jnp.int32, (2 * BLOCK, BLOCK), 0)
    t = lax.broadcasted_iota(jnp.int32, (2 * BLOCK, BLOCK), 1)
    valid = ((s < BLOCK) & (s > t) & (block_index > 0)) | ((s >= BLOCK) & ((s - BLOCK) <= t))
    bias = jnp.where(valid, 0.0, -jnp.inf).astype(F32)
    return jnp.concatenate([bias] * 8, axis=1)


def _heads_t(pair_blocks, g):
    top = lax.broadcasted_iota(jnp.int32, (BLOCK, BLOCK), 0) < HEAD_DIM
    zeros = jnp.zeros((HEAD_DIM, BLOCK), F32)
    tiles = []
    for blk in pair_blocks:
        tp = blk.T
        if g == 0:
            tiles += [jnp.where(top, tp, 0.0), jnp.concatenate([tp[HEAD_DIM:], zeros], axis=0)]
        else:
            tiles += [jnp.concatenate([zeros, tp[:HEAD_DIM]], axis=0), jnp.where(top, 0.0, tp)]
    return jnp.concatenate(tiles, axis=1)


def _pair_block(xt, p, g):
    r0 = HEAD_DIM * g
    even = xt[r0:r0 + HEAD_DIM, (2 * p) * BLOCK:(2 * p + 1) * BLOCK]
    odd = xt[r0:r0 + HEAD_DIM, (2 * p + 1) * BLOCK:(2 * p + 2) * BLOCK]
    return jnp.concatenate([even, odd], axis=0).T


def _softmax_t(scores_t, bias, sink):
    st = scores_t + bias
    m = jnp.maximum(jnp.max(st, axis=0, keepdims=True), sink)
    e = jnp.exp(st - m)
    es = jnp.exp(sink - m)
    inv = 1.0 / (jnp.sum(e, axis=0, keepdims=True) + es)
    return e * inv, es * inv


def _dot(a, b):
    return jnp.dot(a, b, preferred_element_type=F32)


def _dot_nt(a, b):
    return lax.dot_general(a, b, (((1,), (1,)), ((), ())), preferred_element_type=F32)


def _layer_norm_fwd(v):
    mu = jnp.mean(v, axis=-1, keepdims=True)
    xc = v - mu
    rstd = lax.rsqrt(jnp.mean(xc * xc, axis=-1, keepdims=True) + EPS)
    return xc * rstd, rstd


def _tril(transposed=False):
    t = lax.broadcasted_iota(jnp.int32, (BLOCK, BLOCK), 0)
    s = lax.broadcasted_iota(jnp.int32, (BLOCK, BLOCK), 1)
    return s >= t if transposed else t >= s


def _const_spec(shape):
    return pl.BlockSpec(shape, lambda i: (0,) * len(shape))


def _kv_prev_spec(index):
    return pl.BlockSpec((BLOCK, 2 * D_KV), lambda i: (jnp.maximum(index(i) - 1, 0), SEG_KV // (2 * D_KV)))


def _keys_values(z_ref, kvp_ref):
    kvp, kvc = kvp_ref[...], z_ref[:, SEG_KV:SEG_KV + 2 * D_KV]
    kk = jnp.concatenate([kvp[:, :D_KV], kvc[:, :D_KV]], axis=0)
    vv = jnp.concatenate([kvp[:, D_KV:], kvc[:, D_KV:]], axis=0)
    return kk, vv


def _pair_cols(g, p, base=0):
    return slice(base + (4 * g + p) * 128, base + (4 * g + p + 1) * 128)


def _mixer_fwd(z, sink_rows, ln_g, ln_b, sgu_w, sgu_bt):
    t = z.shape[0]

    def body(z_ref, kvp_ref, sink_ref, lng_ref, lnb_ref, w_ref, bt_ref, a_ref):
        bias = _window_bias(pl.program_id(0))
        kk, vv = _keys_values(z_ref, kvp_ref)
        kk_b, vvt_b = kk.astype(BF16), vv.T.astype(BF16)
        for g in range(2):
            qt = _heads_t([z_ref[:, _pair_cols(g, p, SEG_Q)] * ATTN_SCALE for p in range(4)], g).astype(BF16)
            prob, _ = _softmax_t(_dot(kk_b, qt), bias, sink_ref[g])
            ot = _dot(vvt_b, prob.astype(BF16))
            for p in range(4):
                gate = z_ref[:, _pair_cols(g, p, SEG_GA)]
                a_ref[:, _pair_cols(g, p)] = (_pair_block(ot, p, g) * (gate * _sigmoid(gate))).astype(BF16)

        vhat, _ = _layer_norm_fwd(z_ref[:, SEG_VS:SEG_VS + D_SGU])
        vn = vhat * lng_ref[...] + lnb_ref[...]
        tril = _tril()
        for g in range(SGU_GROUPS):
            cols = slice(g * 128, (g + 1) * 128)
            wm = jnp.where(tril, w_ref[g], 0.0).astype(BF16)
            mixed = _dot(wm, vn[:, cols].astype(BF16)) + bt_ref[:, g:g + 1]
            gate = z_ref[:, SEG_GS + g * 128:SEG_GS + (g + 1) * 128]
            a_ref[:, D_ATTN + g * 128:D_ATTN + (g + 1) * 128] = (
                (z_ref[:, SEG_U + g * 128:SEG_U + (g + 1) * 128] * mixed) * (gate * _sigmoid(gate))).astype(BF16)

    return pl.pallas_call(
        body, name="mixer_fwd", grid=(t // BLOCK,),
        in_specs=[pl.BlockSpec((BLOCK, D_IN), lambda i: (i, 0)), _kv_prev_spec(lambda i: i),
                  _const_spec((2, 1, 8 * BLOCK)), _const_spec((1, D_SGU)), _const_spec((1, D_SGU)),
                  _const_spec((SGU_GROUPS, BLOCK, BLOCK)), _const_spec((BLOCK, SGU_GROUPS))],
        out_specs=pl.BlockSpec((BLOCK, D_MODEL), lambda i: (i, 0)),
        out_shape=jax.ShapeDtypeStruct((t, D_MODEL), BF16),
        compiler_params=_params(dimension_semantics=("arbitrary",)),
    )(z, z, sink_rows, ln_g, ln_b, sgu_w, sgu_bt)


def _mixer_bwd(z, da, sink_rows, ln_g, ln_b, sgu_w, sgu_wt, sgu_bt):
    t = z.shape[0]
    nb = t // BLOCK

    def body(z_ref, kvp_ref, da_ref, sink_ref, lng_ref, lnb_ref, w_ref, wt_ref, bt_ref,
             dz_ref, dsink_ref, dw_ref, db_ref, dlng_ref, dlnb_ref, carry_ref, dsink_acc, dbt_acc):
        step = pl.program_id(0)

        @pl.when(step == 0)
        def _():
            carry_ref[...] = jnp.zeros_like(carry_ref)
            dsink_acc[...] = jnp.zeros_like(dsink_acc)
            dbt_acc[...] = jnp.zeros_like(dbt_acc)
            dw_ref[...] = jnp.zeros_like(dw_ref)
            dlng_ref[...] = jnp.zeros_like(dlng_ref)
            dlnb_ref[...] = jnp.zeros_like(dlnb_ref)

        bias = _window_bias(nb - 1 - step)
        kk, vv = _keys_values(z_ref, kvp_ref)
        kk_b, vv_b = kk.astype(BF16), vv.astype(BF16)
        kkt_b, vvt_b = kk.T.astype(BF16), vv.T.astype(BF16)
        dkk = jnp.zeros((2 * BLOCK, D_KV), F32)
        dvv = jnp.zeros((2 * BLOCK, D_KV), F32)
        for g in range(2):
            qt = _heads_t([z_ref[:, _pair_cols(g, p, SEG_Q)] * ATTN_SCALE for p in range(4)], g).astype(BF16)
            prob, sink_prob = _softmax_t(_dot(kk_b, qt), bias, sink_ref[g])
            prob_b = prob.astype(BF16)
            ot = _dot(vvt_b, prob_b)
            gates = [z_ref[:, _pair_cols(g, p, SEG_GA)] for p in range(4)]
            sig = [_sigmoid(gt) for gt in gates]
            d_attn = [da_ref[:, _pair_cols(g, p)] for p in range(4)]
            d_ot = _heads_t([d_attn[p] * (gates[p] * sig[p]) for p in range(4)], g).astype(BF16)
            d_prob = _dot(vv_b, d_ot)
            delta = jnp.sum(prob * d_prob, axis=0, keepdims=True)
            d_scores = (prob * (d_prob - delta)).astype(BF16)
            dsink_acc[g] -= sink_prob * delta
            d_qt = _dot(kkt_b, d_scores)
            dkk = dkk + _dot_nt(d_scores, qt)
            dvv = dvv + _dot_nt(prob_b, d_ot)
            for p in range(4):
                dz_ref[:, _pair_cols(g, p, SEG_Q)] = (_pair_block(d_qt, p, g) * ATTN_SCALE).astype(BF16)
                d_silu = sig[p] * (1.0 + gates[p] * (1.0 - sig[p]))
                dz_ref[:, _pair_cols(g, p, SEG_GA)] = (d_attn[p] * _pair_block(ot, p, g) * d_silu).astype(BF16)
        d_kv = jnp.concatenate([dkk, dvv], axis=1)
        dz_ref[:, SEG_KV:SEG_KV + 2 * D_KV] = (d_kv[BLOCK:] + carry_ref[...]).astype(BF16)
        carry_ref[...] = d_kv[:BLOCK]

        vhat, rstd = _layer_norm_fwd(z_ref[:, SEG_VS:SEG_VS + D_SGU])
        lng = lng_ref[...]
        vn = vhat * lng + lnb_ref[...]
        tril, triu = _tril(), _tril(transposed=True)
        lane = lax.broadcasted_iota(jnp.int32, (BLOCK, 128), 1)
        d_bt = jnp.zeros((BLOCK, 128), F32)
        d_vn = []
        for g in range(SGU_GROUPS):
            cols = slice(g * 128, (g + 1) * 128)
            wm = jnp.where(tril, w_ref[g], 0.0).astype(BF16)
            wmt = jnp.where(triu, wt_ref[g], 0.0).astype(BF16)
            vn_g = vn[:, cols].astype(BF16)
            mixed = _dot(wm, vn_g) + bt_ref[:, g:g + 1]
            gate = z_ref[:, SEG_GS + g * 128:SEG_GS + (g + 1) * 128]
            u = z_ref[:, SEG_U + g * 128:SEG_U + (g + 1) * 128]
            d_out = da_ref[:, D_ATTN + g * 128:D_ATTN + (g + 1) * 128]
            sg = _sigmoid(gate)
            d_um = d_out * (gate * sg)
            dz_ref[:, SEG_U + g * 128:SEG_U + (g + 1) * 128] = (d_um * mixed).astype(BF16)
            dz_ref[:, SEG_GS + g * 128:SEG_GS + (g + 1) * 128] = (
                d_out * (u * mixed) * (sg * (1.0 + gate * (1.0 - sg)))).astype(BF16)
            d_mixed = d_um * u
            d_mixed_b = d_mixed.astype(BF16)
            dw_ref[g] += jnp.where(tril, _dot_nt(d_mixed_b, vn_g), 0.0)
            d_bt = d_bt + jnp.where(lane == g, jnp.sum(d_mixed, axis=-1, keepdims=True), 0.0)
            d_vn.append(_dot(wmt, d_mixed_b))
        dbt_acc[...] += d_bt
        d_vn = jnp.concatenate(d_vn, axis=1)
        dlng_ref[...] += jnp.sum(d_vn * vhat, axis=0, keepdims=True)
        dlnb_ref[...] += jnp.sum(d_vn, axis=0, keepdims=True)
        d_vhat = d_vn * lng
        d_v = rstd * (d_vhat - jnp.mean(d_vhat, axis=-1, keepdims=True)
                      - vhat * jnp.mean(d_vhat * vhat, axis=-1, keepdims=True))
        dz_ref[:, SEG_VS:SEG_VS + D_SGU] = d_v.astype(BF16)

        @pl.when(step == nb - 1)
        def _():
            db_ref[...] = dbt_acc[...].T[:SGU_GROUPS]
            lane_row = lax.broadcasted_iota(jnp.int32, (1, 128), 1)
            d_sink = jnp.zeros((1, 128), F32)
            for g in range(2):
                acc = dsink_acc[g]
                for j in range(8):
                    head_sum = jnp.sum(acc[:, j * BLOCK:(j + 1) * BLOCK], axis=-1, keepdims=True)
                    d_sink = d_sink + jnp.where(lane_row == 8 * g + j, head_sum, 0.0)
            dsink_ref[...] = d_sink

    rev = lambda i: nb - 1 - i
    return pl.pallas_call(
        body, name="mixer_bwd", grid=(nb,),
        in_specs=[pl.BlockSpec((BLOCK, D_IN), lambda i: (rev(i), 0)), _kv_prev_spec(rev),
                  pl.BlockSpec((BLOCK, D_MODEL), lambda i: (rev(i), 0)),
                  _const_spec((2, 1, 8 * BLOCK)), _const_spec((1, D_SGU)), _const_spec((1, D_SGU)),
                  _const_spec((SGU_GROUPS, BLOCK, BLOCK)), _const_spec((SGU_GROUPS, BLOCK, BLOCK)),
                  _const_spec((BLOCK, SGU_GROUPS))],
        out_specs=(pl.BlockSpec((BLOCK, D_IN), lambda i: (rev(i), 0)), _const_spec((1, 128)),
                   _const_spec((SGU_GROUPS, BLOCK, BLOCK)), _const_spec((SGU_GROUPS, BLOCK)),
                   _const_spec((1, D_SGU)), _const_spec((1, D_SGU))),
        out_shape=(jax.ShapeDtypeStruct((t, D_IN), BF16), jax.ShapeDtypeStruct((1, 128), F32),
                   jax.ShapeDtypeStruct((SGU_GROUPS, BLOCK, BLOCK), F32), jax.ShapeDtypeStruct((SGU_GROUPS, BLOCK), F32),
                   jax.ShapeDtypeStruct((1, D_SGU), F32), jax.ShapeDtypeStruct((1, D_SGU), F32)),
        scratch_shapes=[pltpu.VMEM((BLOCK, 2 * D_KV), F32), pltpu.VMEM((2, 1, 8 * BLOCK), F32),
                        pltpu.VMEM((BLOCK, 128), F32)],
        compiler_params=_params(dimension_semantics=("arbitrary",)),
    )(z, z, da, sink_rows, ln_g, ln_b, sgu_w, sgu_wt, sgu_bt)


def _out_proj_head(a, w_out_full, x, target, gate, final_g, tm=256):
    t, d = x.shape

    def body(a_ref, w_ref, x_ref, tg_ref, gate_ref, fg_ref, dx2_ref, dy_ref, loss_ref, dfg_ref, dgate_ref):
        @pl.when(pl.program_id(0) == 0)
        def _():
            loss_ref[...] = jnp.zeros_like(loss_ref)
            dfg_ref[...] = jnp.zeros_like(dfg_ref)
            dgate_ref[...] = jnp.zeros_like(dgate_ref)

        yv, gate, fg = _dot(a_ref[...], w_ref[...]), gate_ref[...], fg_ref[...]
        x2 = x_ref[...] + gate * yv
        r2 = lax.rsqrt(jnp.mean(x2 * x2, axis=-1, keepdims=True) + EPS)
        nrm = x2 * r2
        err = nrm * fg - tg_ref[...]
        loss_ref[...] += 0.5 * jnp.sum(jnp.mean(err * err, axis=-1, keepdims=True), axis=0, keepdims=True)
        d_out = err * (1.0 / d)
        dfg_ref[...] += jnp.sum(d_out * nrm, axis=0, keepdims=True)
        d_nrm = d_out * fg
        dx2 = r2 * (d_nrm - nrm * jnp.mean(d_nrm * nrm, axis=-1, keepdims=True))
        dx2_ref[...] = dx2
        dgate_ref[...] += jnp.sum(dx2 * yv, axis=0, keepdims=True)
        dy_ref[...] = (dx2 * gate).astype(BF16)

    blk = pl.BlockSpec((tm, d), lambda i: (i, 0))
    row = _const_spec((1, d))
    whole = pl.BlockSpec(w_out_full.shape, lambda i: (0, 0), pipeline_mode=pl.Buffered(1))
    return pl.pallas_call(
        body, name="out_proj_head", grid=(t // tm,),
        in_specs=[pl.BlockSpec((tm, a.shape[1]), lambda i: (i, 0)), whole, blk, blk, row, row],
        out_specs=(blk, blk, _const_spec((1, 128)), row, row),
        out_shape=(jax.ShapeDtypeStruct((t, d), F32), jax.ShapeDtypeStruct((t, d), BF16),
                   jax.ShapeDtypeStruct((1, 128), F32), jax.ShapeDtypeStruct((1, d), F32),
                   jax.ShapeDtypeStruct((1, d), F32)),
        compiler_params=_params(dimension_semantics=("arbitrary",)),
    )(a, w_out_full, x, target, gate, final_g)


def _z_proj_bwd_norm(dz, w_in_t, x, dx2, norm_g, scale, dep, tm=256):
    t, d = x.shape

    def body(dz_ref, w_ref, x_ref, dx2_ref, g_ref, sc_ref, dep_ref, gx_ref, dshift_ref, dscale_ref, dg_ref):
        @pl.when(pl.program_id(0) == 0)
        def _():
            dshift_ref[...] = jnp.zeros_like(dshift_ref)
            dscale_ref[...] = jnp.zeros_like(dscale_ref)
            dg_ref[...] = jnp.zeros_like(dg_ref)

        dh, xv, g = _dot(dz_ref[...], w_ref[...]), x_ref[...], g_ref[...]
        one_plus = 1.0 + sc_ref[...]
        r = lax.rsqrt(jnp.mean(xv * xv, axis=-1, keepdims=True) + EPS)
        xn = xv * r
        dshift_ref[...] += jnp.sum(dh, axis=0, keepdims=True)
        dscale_ref[...] += jnp.sum(dh * (xn * g), axis=0, keepdims=True)
        d_y = dh * one_plus
        dg_ref[...] += jnp.sum(d_y * xn, axis=0, keepdims=True)
        d_xn = d_y * g
        gx_ref[...] = dx2_ref[...] + r * (d_xn - xn * jnp.mean(d_xn * xn, axis=-1, keepdims=True))

    blk = pl.BlockSpec((tm, d), lambda i: (i, 0))
    row = _const_spec((1, d))
    whole = pl.BlockSpec(w_in_t.shape, lambda i: (0, 0), pipeline_mode=pl.Buffered(1))
    return pl.pallas_call(
        body, name="z_proj_bwd_norm", grid=(t // tm,),
        in_specs=[pl.BlockSpec((tm, dz.shape[1]), lambda i: (i, 0)), whole, blk, blk, row, row, _const_spec((8, 128))],
        out_specs=(blk, row, row, row),
        out_shape=(jax.ShapeDtypeStruct((t, d), F32),) + (jax.ShapeDtypeStruct((1, d), F32),) * 3,
        compiler_params=_params(dimension_semantics=("arbitrary",)),
    )(dz, w_in_t, x, dx2, norm_g, scale, dep)


def _adamw(w, g, m, v):
    m = ADAM_B1 * m + (1.0 - ADAM_B1) * g
    v = ADAM_B2 * v + (1.0 - ADAM_B2) * (g * g)
    m_hat = m / (1.0 - ADAM_B1 ** ADAM_STEP)
    v_hat = v / (1.0 - ADAM_B2 ** ADAM_STEP)
    delta = -ADAM_LR * (m_hat / (jnp.sqrt(v_hat) + ADAM_EPS) + ADAM_WD * w)
    return delta, m, v


def _relay_sum(second_chip, pair, land, tr):
    _, r, c = pair.shape

    def body(chip_ref, a_ref, b_ref, o_ref):
        o_ref[...] = (a_ref[...].astype(F32) + b_ref[...].astype(F32)).astype(BF16)

    return pl.pallas_call(
        body, name="w_in_grad_relay_sum",
        grid_spec=pltpu.PrefetchScalarGridSpec(
            num_scalar_prefetch=1, grid=(r // tr,),
            in_specs=[pl.BlockSpec((None, tr, c), lambda i, chip_ref: (chip_ref[0], i, 0)),
                      pl.BlockSpec((None, tr, c), lambda i, chip_ref: (1, i, 0))],
            out_specs=pl.BlockSpec((tr, c), lambda i, chip_ref: (i, 0))),
        out_shape=jax.ShapeDtypeStruct((r, c), BF16),
        compiler_params=_params(dimension_semantics=("arbitrary",)),
    )(second_chip, pair, land)


def _adam_from_chips(chip, pair, landed, w, m, v, name, tc):
    _, r, c = pair.shape
    n = len(landed)

    def body(chip_ref, own_ref, *refs):
        w_ref, m_ref, v_ref, g_ref, d_ref, nm_ref, nv_ref = refs[n:]
        g = own_ref[...].astype(F32)
        for k in range(n):
            g = g + refs[k][...].astype(F32)
        g_ref[...] = g
        d_ref[...], nm_ref[...], nv_ref[...] = _adamw(w_ref[...], g, m_ref[...], v_ref[...])

    def landed_spec(index):
        return pl.BlockSpec((None, r, tc), lambda i, chip_ref: (index, 0, i))

    blk = pl.BlockSpec((r, tc), lambda i, chip_ref: (0, i))
    return pl.pallas_call(
        body, name=name,
        grid_spec=pltpu.PrefetchScalarGridSpec(
            num_scalar_prefetch=1, grid=(c // tc,),
            in_specs=[pl.BlockSpec((None, r, tc), lambda i, chip_ref: (chip_ref[0], 0, i))]
            + [landed_spec(index) for _, index in landed] + [blk, blk, blk],
            out_specs=(blk,) * 4),
        out_shape=(jax.ShapeDtypeStruct((r, c), F32),) * 4,
        compiler_params=_params(dimension_semantics=("arbitrary",)),
    )(chip, pair, *[array for array, _ in landed], w, m, v)


def _adam_w_ada(act_t, dmod_mine, w, m, v, tr=256):
    r, c = w.shape

    def body(a_ref, dm_ref, w_ref, m_ref, v_ref, g_ref, d_ref, nm_ref, nv_ref):
        g = _dot(a_ref[...].astype(BF16), dm_ref[...].astype(BF16))
        g_ref[...] = g
        d_ref[...], nm_ref[...], nv_ref[...] = _adamw(w_ref[...], g, m_ref[...], v_ref[...])

    blk = pl.BlockSpec((tr, c), lambda i: (i, 0))
    return pl.pallas_call(
        body, name="adam_w_ada", grid=(r // tr,),
        in_specs=[pl.BlockSpec((tr, N_DEV), lambda i: (i, 0)), _const_spec((N_DEV, c)), blk, blk, blk],
        out_specs=(blk,) * 4, out_shape=(jax.ShapeDtypeStruct((r, c), F32),) * 4,
        compiler_params=_params(dimension_semantics=("arbitrary",)),
    )(act_t, dmod_mine, w, m, v)


def _pack_small(d_shift, d_scale, d_gate, d_norm_g, d_final_g, d_ln_g, d_ln_b, loss, d_sinks, d_sgu_b):
    def body(shift_ref, scale_ref, gate_ref, ng_ref, fg_ref, lng_ref, lnb_ref, loss_ref, sink_ref, b_ref, o_ref):
        o_ref[...] = jnp.zeros_like(o_ref)
        o_ref[ROW_SHIFT:ROW_SHIFT + 1, :] = shift_ref[...]
        o_ref[ROW_SCALE:ROW_SCALE + 1, :] = scale_ref[...]
        o_ref[ROW_GATE:ROW_GATE + 1, :] = gate_ref[...]
        o_ref[ROW_NORM_G:ROW_NORM_G + 1, :] = ng_ref[...]
        o_ref[ROW_FINAL_G:ROW_FINAL_G + 1, :] = fg_ref[...]
        o_ref[ROW_LN:ROW_LN + 1, 0:D_SGU] = lng_ref[...]
        o_ref[ROW_LN:ROW_LN + 1, D_SGU:2 * D_SGU] = lnb_ref[...]
        o_ref[ROW_MISC:ROW_MISC + 1, 0:128] = loss_ref[...]
        o_ref[ROW_MISC:ROW_MISC + 1, 128:256] = sink_ref[...]
        o_ref[ROW_SGU_B:ROW_SGU_B + SGU_GROUPS, 0:BLOCK] = b_ref[...]

    return pl.pallas_call(
        body, name="pack_small", out_shape=jax.ShapeDtypeStruct((SMALL_ROWS, D_MODEL), F32),
        compiler_params=_params(),
    )(d_shift, d_scale, d_gate, d_norm_g, d_final_g, d_ln_g, d_ln_b, loss, d_sinks, d_sgu_b)


_SMALL_NAMES = ("norm_g", "b_ada", "attn_sinks", "sgu_ln_g", "sgu_ln_b", "sgu_w", "sgu_b", "final_g")


def _adam_small(partials, d_sgu_w_all, weights, moments_m, moments_v):
    names = _SMALL_NAMES
    k = len(names)

    def body(*refs):
        p_ref, sw_ref = refs[0], refs[1]
        w_refs, m_refs, v_refs = refs[2:2 + k], refs[2 + k:2 + 2 * k], refs[2 + 2 * k:2 + 3 * k]
        loss_ref, dmod_ref = refs[2 + 3 * k], refs[3 + 3 * k]
        out_refs = refs[4 + 3 * k:4 + 7 * k]
        sum_ref = refs[4 + 7 * k]
        total = p_ref[0]
        for j in range(1, N_DEV):
            total = total + p_ref[j]
        sum_ref[...] = total
        for j in range(N_DEV):
            for part, row in enumerate((ROW_SHIFT, ROW_SCALE, ROW_GATE)):
                dmod_ref[j:j + 1, part * D_MODEL:(part + 1) * D_MODEL] = p_ref[j, row:row + 1, :]
        loss_ref[...] = sum_ref[ROW_MISC:ROW_MISC + 1, 0:1]
        d_sgu_w = sw_ref[0]
        for j in range(1, N_DEV):
            d_sgu_w = d_sgu_w + sw_ref[j]
        grads = {
            "norm_g": sum_ref[ROW_NORM_G:ROW_NORM_G + 1, :],
            "b_ada": jnp.concatenate([sum_ref[r:r + 1, :] for r in (ROW_SHIFT, ROW_SCALE, ROW_GATE)], axis=1),
            "attn_sinks": sum_ref[ROW_MISC:ROW_MISC + 1, 128:128 + N_Q_HEADS],
            "sgu_ln_g": sum_ref[ROW_LN:ROW_LN + 1, 0:D_SGU],
            "sgu_ln_b": sum_ref[ROW_LN:ROW_LN + 1, D_SGU:2 * D_SGU],
            "sgu_w": d_sgu_w[None],
            "sgu_b": sum_ref[ROW_SGU_B:ROW_SGU_B + SGU_GROUPS, 0:BLOCK][None],
            "final_g": sum_ref[ROW_FINAL_G:ROW_FINAL_G + 1, :],
        }
        for i, name in enumerate(names):
            g = grads[name]
            delta, m, v = _adamw(w_refs[i][...], g, m_refs[i][...], v_refs[i][...])
            out_refs[4 * i][...] = g
            out_refs[4 * i + 1][...] = delta
            out_refs[4 * i + 2][...] = m
            out_refs[4 * i + 3][...] = v

    shapes = [jax.ShapeDtypeStruct((1, 1), F32), jax.ShapeDtypeStruct((N_DEV, 3 * D_MODEL), F32)]
    for name in names:
        shapes += [jax.ShapeDtypeStruct(weights[name].shape, F32)] * 4
    outs = pl.pallas_call(
        body, name="adam_small", out_shape=tuple(shapes),
        scratch_shapes=[pltpu.VMEM((SMALL_ROWS, D_MODEL), F32)],
        compiler_params=_params(),
    )(partials, d_sgu_w_all, *[weights[n] for n in names], *[moments_m[n] for n in names],
      *[moments_v[n] for n in names])
    return outs[0], outs[1], {name: outs[2 + 4 * i:6 + 4 * i] for i, name in enumerate(names)}


def kernel(x, c, norm_g, w_ada, b_ada, w_in, attn_sinks, sgu_ln_g, sgu_ln_b, sgu_w, sgu_b, w_out, final_g, loss_target, m_norm_g, m_w_ada, m_b_ada, m_w_in, m_attn_sinks, m_sgu_ln_g, m_sgu_ln_b, m_sgu_w, m_sgu_b, m_w_out, m_final_g, v_norm_g, v_w_ada, v_b_ada, v_w_in, v_attn_sinks, v_sgu_ln_g, v_sgu_ln_b, v_sgu_w, v_sgu_b, v_w_out, v_final_g):
    xi, yi, ci = _place()
    me = 4 * xi + 2 * yi + ci
    x2d, target = x[0], loss_target[0]
    t = x2d.shape[0]

    core = ci.astype(jnp.int32).reshape(1)
    chip = (2 * xi + yi).astype(jnp.int32).reshape(1)

    c_all = _all_gather_small(c.reshape(8, 256), "gather_c").reshape(N_DEV, D_MODEL)
    b_mine = lax.dynamic_slice(b_ada, (0, me * W_ADA_SHARD), (1, W_ADA_SHARD))
    c_act, mod_part = _modulation(c_all, w_ada[0], b_mine)
    mod_all = _all_gather_small(mod_part, "gather_mod")

    near = _own_block_copies(_near_targets)
    w_in_flight = _start_copies([_with_own_slot(w_in[0].T.astype(BF16), me)], near, 3, mod_all, "gather_w_in_start")
    mod = lax.dynamic_index_in_dim(mod_all, me, axis=1, keepdims=False).reshape(1, 3 * D_MODEL)
    mod = mod + w_in_flight[3][0, 0]
    shift, scale, gate = mod[:, :D_MODEL], mod[:, D_MODEL:2 * D_MODEL], mod[:, 2 * D_MODEL:]
    h = _modulated_norm(x2d, norm_g, scale, shift)

    w_in_pair = _wait_copies(w_in_flight, lambda *a: near(*a)[:1], h, "gather_w_in_sibling_wait")
    tile_order = jnp.asarray(_Z_TILE_ORDER, jnp.int32)[chip[0]]
    z_own = _z_proj(h, w_in_pair[0].reshape(D_IN, D_MODEL), tile_order, 0, 1, None, "z_proj_own")
    w_in_flight = _wait_then_start((w_in_flight[0], w_in_flight[1], w_in_pair, None), lambda *a: near(*a)[1:],
                                   _second_stage_copies, 3, z_own, "gather_w_in_second_stage")
    w_in_most = _wait_copies(w_in_flight, lambda *a: _second_stage_copies(*a)[:2], z_own, "gather_w_in_forward_wait")
    z_early = _z_proj(h, w_in_most[0].reshape(D_IN, D_MODEL), tile_order, 1, _Z_EARLY_TILES - 1, z_own, "z_proj_early")
    w_out_flight = _start_copies([_with_own_slot(w_out[0].astype(BF16), me)], _own_block_copies(_my_core_and_sibling),
                                 4, z_early, "gather_w_out_start")
    w_in_flight = _wait_then_start((w_in_flight[0], w_in_flight[1], w_in_most, None),
                                   lambda *a: _second_stage_copies(*a)[2:], _diagonal_forward_copies, 1,
                                   w_out_flight[3], "gather_w_in_last_stage")
    w_in_all = _wait_copies(w_in_flight, _diagonal_forward_copies, z_early, "gather_w_in_last_wait")[0]
    w_in_t = w_in_all.reshape(D_IN, D_MODEL)
    z = _z_proj(h, w_in_t, tile_order, _Z_EARLY_TILES, 7 - _Z_EARLY_TILES, z_early, "z_proj_late")
    sink_rows = jnp.repeat(attn_sinks.reshape(N_Q_HEADS), BLOCK).reshape(2, 1, 8 * BLOCK)
    sgu_bt = sgu_b[0].T
    a = _mixer_fwd(z, sink_rows, sgu_ln_g, sgu_ln_b, sgu_w[0], sgu_bt)
    w_out_flight = _wait_then_start(w_out_flight, _own_block_copies(_my_core_and_sibling), _forward_copies, 3, a,
                                    "gather_w_out_forward_stage")
    w_out_all = _wait_copies(w_out_flight, _forward_copies, a, "gather_w_out_forward_wait")[0]
    w_out_full = w_out_all.reshape(D_MODEL, D_MODEL)
    final_g_row = final_g.reshape(1, D_MODEL)
    dx2, dy, loss_part, d_final_g, d_gate = _out_proj_head(a, w_out_full, x2d, target, gate, final_g_row)

    da = _matmul(dy, w_out_full, "nt", F32, min(t, 1024), 1024, "out_proj_bwd")
    dw_out = _matmul(a, dy, "tn", BF16, 1024, 1024, "w_out_grad").reshape(4, 2, W_OUT_SHARD, D_MODEL)
    pair_out = _pair_reduce(dw_out, "w_out_grad_pair_reduce", W_OUT_SHARD // 2)
    out_flight = _start_copies([pair_out, lax.empty((3, W_OUT_SHARD, D_MODEL), BF16)], _chip_copies, 3, core,
                               "w_out_grad_chip_start")
    dz, d_sinks, d_sgu_w, d_sgu_b, d_ln_g, d_ln_b = _mixer_bwd(
        z, da, sink_rows + out_flight[3][0, 0], sgu_ln_g, sgu_ln_b, sgu_w[0], jnp.swapaxes(sgu_w[0], 1, 2), sgu_bt)
    sgu_w_flight = _start_copies([_with_own_slot(d_sgu_w, me)], _own_block_copies(_all_others), N_DEV - 1, core,
                                 "sgu_w_grad_gather_start")
    dw_in_t = _matmul(dz, h, "tn", BF16, 768, D_MODEL, "w_in_grad", dep=sgu_w_flight[3])
    dw_in_t = dw_in_t.reshape(4, 2, W_IN_SHARD, D_MODEL)
    pair_in = _pair_reduce(dw_in_t, "w_in_grad_pair_reduce", W_IN_SHARD // 3)
    hop1 = _start_copies([pair_in, lax.empty((2, W_IN_SHARD, D_MODEL), BF16)], _first_hop_copies, 2, core,
                         "w_in_grad_first_hop_start")
    grad_x, d_shift, d_scale, d_norm_g = _z_proj_bwd_norm(dz, w_in_t, x2d, dx2, norm_g, scale, hop1[3])

    partial = _pack_small(d_shift, d_scale, d_gate, d_norm_g, d_final_g, d_ln_g, d_ln_b, loss_part, d_sinks, d_sgu_b)
    partial_all = _all_gather_small(partial, "gather_small")
    d_sgu_w_all = _wait_copies(sgu_w_flight, _own_block_copies(_all_others), partial_all, "sgu_w_grad_gather_wait")[0]
    pair_in, land_first = _wait_copies(hop1, _first_hop_copies, partial_all, "w_in_grad_first_hop_wait")
    second_chip = (2 * ((xi + ci) % 2) + (yi + 1 - ci) % 2).astype(jnp.int32).reshape(1)
    relay = _relay_sum(second_chip, pair_in, land_first, W_IN_SHARD // 3)
    hop2 = _start_copies([relay, lax.empty((1, W_IN_SHARD, D_MODEL), BF16)], _second_hop_copies, 1, core,
                         "w_in_grad_second_hop_start")
    weights = {"norm_g": norm_g, "b_ada": b_ada, "attn_sinks": attn_sinks + hop2[3][0, 0], "sgu_ln_g": sgu_ln_g,
               "sgu_ln_b": sgu_ln_b, "sgu_w": sgu_w, "sgu_b": sgu_b, "final_g": final_g_row}
    moments_m = {"norm_g": m_norm_g, "b_ada": m_b_ada, "attn_sinks": m_attn_sinks, "sgu_ln_g": m_sgu_ln_g,
                 "sgu_ln_b": m_sgu_ln_b, "sgu_w": m_sgu_w, "sgu_b": m_sgu_b,
                 "final_g": m_final_g.reshape(1, D_MODEL)}
    moments_v = {"norm_g": v_norm_g, "b_ada": v_b_ada, "attn_sinks": v_attn_sinks, "sgu_ln_g": v_sgu_ln_g,
                 "sgu_ln_b": v_sgu_ln_b, "sgu_w": v_sgu_w, "sgu_b": v_sgu_b,
                 "final_g": v_final_g.reshape(1, D_MODEL)}
    loss, dmod_all, small = _adam_small(partial_all, d_sgu_w_all, weights, moments_m, moments_v)
    small["final_g"] = tuple(o.reshape(D_MODEL) for o in small["final_g"])

    dmod_mine = lax.dynamic_slice(dmod_all, (0, me * W_ADA_SHARD), (N_DEV, W_ADA_SHARD))
    big = {"w_ada": _adam_w_ada(c_act.T, dmod_mine, w_ada[0], m_w_ada[0], v_w_ada[0])}
    pair_out, land_out = _wait_copies(out_flight, _chip_copies, big["w_ada"][0], "w_out_grad_chip_wait")
    big["w_out"] = _adam_from_chips(chip, pair_out, [(land_out, k) for k in range(3)], w_out[0], m_w_out[0], v_w_out[0],
                                    "adam_w_out", 1024)
    _, land_second = _wait_copies(hop2, _second_hop_copies, big["w_out"][0], "w_in_grad_second_hop_wait")
    big["w_in"] = tuple(o.T for o in _adam_from_chips(
        chip, pair_in, [(land_first, 0), (land_second, 0)], w_in[0].T, m_w_in[0].T, v_w_in[0].T, "adam_w_in", 256))
    order = ["norm_g", "w_ada", "b_ada", "w_in", "attn_sinks", "sgu_ln_g", "sgu_ln_b", "sgu_w", "sgu_b", "w_out",
             "final_g"]
    outs = [loss.reshape(()), grad_x[None]]
    for k in range(4):
        for name in order:
            outs.append(big[name][k][None] if name in big else small[name][k])
    return tuple(outs)
```

```python
import jax
import jax.numpy as jnp
from jax import lax
from jax.experimental import pallas as pl
from jax.experimental.pallas import tpu as pltpu

F32 = jnp.float32
BF16 = jnp.bfloat16
MESH = pl.DeviceIdType.MESH

N_DEV = 8
D_MODEL = 2048
HEAD_DIM = 64
D_ATTN = 1024
N_Q_HEADS = 16
D_KV = 128
BLOCK = 128
D_SGU = 1024
SGU_GROUPS = 8
D_IN = 5376
W_IN_SHARD = D_IN // N_DEV
W_OUT_SHARD = D_MODEL // N_DEV
W_ADA_SHARD = 3 * D_MODEL // N_DEV
EPS = 1e-6
ATTN_SCALE = 0.125

ADAM_LR = 0.001
ADAM_B1 = 0.9
ADAM_B2 = 0.999
ADAM_EPS = 1e-08
ADAM_WD = 0.01
ADAM_STEP = 10

SEG_Q, SEG_KV, SEG_GA, SEG_U, SEG_VS, SEG_GS = 0, 1024, 1280, 2304, 3328, 4352

VMEM_LIMIT = 56 * 1024 * 1024

ROW_SHIFT, ROW_SCALE, ROW_GATE, ROW_NORM_G, ROW_FINAL_G, ROW_LN, ROW_MISC, ROW_SGU_B = 0, 1, 2, 3, 4, 5, 6, 8
SMALL_ROWS = 16


def _params(**kw):
    return pltpu.CompilerParams(vmem_limit_bytes=VMEM_LIMIT, **kw)


def _sigmoid(x):
    return 0.5 * (jnp.tanh(0.5 * x) + 1.0)


def _place():
    return lax.axis_index("x"), lax.axis_index("y"), lax.axis_index("c")


def _all_gather(shards, name, in_vmem):
    n = len(shards)

    def body(*refs):
        ins, outs = refs[:n], refs[n:2 * n]
        send_sems, recv_sems, local_sems = refs[2 * n:]
        x, y, c = _place()
        me, sibling = (x, y, c), (x, y, 1 - c)
        chips = [(1 - x, y), (x, 1 - y), (1 - x, 1 - y)]
        first, passed, mine = [], [], []
        for a in range(n):
            out_ref = outs[a]

            def slot(px, py, pc, out_ref=out_ref):
                return out_ref.at[4 * px + 2 * py + pc]

            def copy(k, block, to, src=None, a=a, slot=slot):
                return pltpu.make_async_remote_copy(
                    src_ref=slot(*block) if src is None else src, dst_ref=slot(*block),
                    send_sem=send_sems.at[a, k], recv_sem=recv_sems.at[a, k],
                    device_id=to, device_id_type=MESH)

            own = pltpu.make_async_copy(ins[a], slot(*me), local_sems.at[a])
            own.start()
            mine.append(own)
            mine_out = [copy(0, me, sibling, src=ins[a])]
            mine_out += [copy(1 + j, me, (*chip, c), src=ins[a]) for j, chip in enumerate(chips)]
            for cp in mine_out:
                cp.start()
            first += mine_out
            passed.append([copy(4 + j, (*chip, c), sibling) for j, chip in enumerate(chips)])
        for j, chip in enumerate(chips):
            for a in range(n):
                out_ref = outs[a]
                blk = out_ref.at[4 * chip[0] + 2 * chip[1] + c]
                pltpu.make_async_remote_copy(
                    src_ref=blk, dst_ref=blk, send_sem=send_sems.at[a, 1 + j], recv_sem=recv_sems.at[a, 1 + j],
                    device_id=me, device_id_type=MESH).wait_recv()
                passed[a][j].start()
        for a in range(n):
            out_ref = outs[a]
            blk = out_ref.at[4 * x + 2 * y + (1 - c)]
            pltpu.make_async_remote_copy(
                src_ref=blk, dst_ref=blk, send_sem=send_sems.at[a, 0], recv_sem=recv_sems.at[a, 0],
                device_id=me, device_id_type=MESH).wait_recv()
            for j, chip in enumerate(chips):
                blk = out_ref.at[4 * chip[0] + 2 * chip[1] + (1 - c)]
                pltpu.make_async_remote_copy(
                    src_ref=blk, dst_ref=blk, send_sem=send_sems.at[a, 4 + j], recv_sem=recv_sems.at[a, 4 + j],
                    device_id=me, device_id_type=MESH).wait_recv()
        for cp in first:
            cp.wait_send()
        for a in range(n):
            for cp in passed[a]:
                cp.wait_send()
        for cp in mine:
            cp.wait()

    space = pltpu.VMEM if in_vmem else pl.ANY
    spec = pl.BlockSpec(memory_space=space)
    return pl.pallas_call(
        body, name=name,
        out_shape=tuple(jax.ShapeDtypeStruct((N_DEV,) + s.shape, s.dtype) for s in shards),
        in_specs=[spec] * n, out_specs=tuple([spec] * n),
        scratch_shapes=[pltpu.SemaphoreType.DMA((n, 7)), pltpu.SemaphoreType.DMA((n, 7)),
                        pltpu.SemaphoreType.DMA((n,))],
        compiler_params=_params(),
    )(*shards)


def _pair_reduce(blocks, name, row_chunk):
    _, _, r, cols = blocks.shape
    assert r % row_chunk == 0

    def body(in_ref, out_ref, land, own, summed, send_sems, recv_sems, own_sems, out_sems):
        x, y, c = _place()
        sends, loads, stores = [], [], []
        for m in range(4):
            cp = pltpu.make_async_remote_copy(
                src_ref=in_ref.at[m, 1 - c], dst_ref=land.at[m], send_sem=send_sems.at[m], recv_sem=recv_sems.at[m],
                device_id=(x, y, 1 - c), device_id_type=MESH)
            cp.start()
            sends.append(cp)
            ld = pltpu.make_async_copy(in_ref.at[m, c], own.at[m], own_sems.at[m])
            ld.start()
            loads.append(ld)
        for m in range(4):
            sends[m].wait_recv()
            loads[m].wait()
            for k in range(r // row_chunk):
                rows = slice(k * row_chunk, (k + 1) * row_chunk)
                summed[m, rows, :] = (own[m, rows, :].astype(F32) + land[m, rows, :].astype(F32)).astype(BF16)
            st = pltpu.make_async_copy(summed.at[m], out_ref.at[m], out_sems.at[m])
            st.start()
            stores.append(st)
        for m in range(4):
            sends[m].wait_send()
            stores[m].wait()

    spec = pl.BlockSpec(memory_space=pl.ANY)
    return pl.pallas_call(
        body, name=name, out_shape=jax.ShapeDtypeStruct((4, r, cols), BF16),
        in_specs=[spec], out_specs=spec,
        scratch_shapes=[pltpu.VMEM((4, r, cols), BF16), pltpu.VMEM((4, r, cols), BF16), pltpu.VMEM((4, r, cols), BF16),
                        pltpu.SemaphoreType.DMA((4,)), pltpu.SemaphoreType.DMA((4,)), pltpu.SemaphoreType.DMA((4,)),
                        pltpu.SemaphoreType.DMA((4,))],
        compiler_params=_params(),
    )(blocks)


_HBM = pl.BlockSpec(memory_space=pltpu.HBM)
_SEM = pl.BlockSpec(memory_space=pltpu.SEMAPHORE)
_EFFECT = pltpu.SideEffectType.DATAFLOW_SIDE_EFFECTING


def _start_copies(bufs, copies, n_copies, after, name):
    nb = len(bufs)

    def body(*refs):
        for cp in copies(refs[:nb], refs[nb + 1], refs[nb + 2]):
            cp.start()
        refs[-1][...] = jnp.zeros_like(refs[-1])

    out = pl.pallas_call(
        body, name=name,
        out_shape=(pltpu.SemaphoreType.DMA((n_copies,)), pltpu.SemaphoreType.DMA((n_copies,)),
                   *[pltpu.HBM(b.shape, b.dtype) for b in bufs], jax.ShapeDtypeStruct((8, 128), F32)),
        in_specs=(_HBM,) * nb + (pl.BlockSpec(memory_space=pl.ANY),),
        out_specs=(_SEM, _SEM) + (_HBM,) * nb + (pl.BlockSpec(memory_space=pltpu.VMEM),),
        input_output_aliases={i: 2 + i for i in range(nb)},
        compiler_params=pltpu.CompilerParams(has_side_effects=_EFFECT),
    )(*[pltpu.with_memory_space_constraint(b, pltpu.HBM) for b in bufs], after)
    return out[0], out[1], list(out[2:2 + nb]), out[-1]


def _wait_copies(flight, copies, after, name):
    send_sems, recv_sems, bufs, _ = flight
    nb = len(bufs)

    def body(*refs):
        for cp in copies(refs[:nb], refs[nb], refs[nb + 1]):
            cp.wait_send()
            cp.wait_recv()

    return pl.pallas_call(
        body, name=name,
        out_shape=tuple(pltpu.HBM(b.shape, b.dtype) for b in bufs),
        in_specs=(_HBM,) * nb + (_SEM, _SEM, pl.BlockSpec(memory_space=pl.ANY)), out_specs=(_HBM,) * nb,
        input_output_aliases={i: i for i in range(nb)},
        compiler_params=pltpu.CompilerParams(has_side_effects=_EFFECT),
    )(*bufs, send_sems, recv_sems, after)


def _wait_then_start(flight, waited, started, n_started, after, name):
    old_send, old_recv, bufs, _ = flight
    nb = len(bufs)

    def body(*refs):
        for cp in waited(refs[:nb], refs[nb], refs[nb + 1]):
            cp.wait_send()
            cp.wait_recv()
        for cp in started(refs[:nb], refs[nb + 3], refs[nb + 4]):
            cp.start()
        refs[-1][...] = jnp.zeros_like(refs[-1])

    out = pl.pallas_call(
        body, name=name,
        out_shape=(pltpu.SemaphoreType.DMA((n_started,)), pltpu.SemaphoreType.DMA((n_started,)),
                   *[pltpu.HBM(b.shape, b.dtype) for b in bufs], jax.ShapeDtypeStruct((8, 128), F32)),
        in_specs=(_HBM,) * nb + (_SEM, _SEM, pl.BlockSpec(memory_space=pl.ANY)),
        out_specs=(_SEM, _SEM) + (_HBM,) * nb + (pl.BlockSpec(memory_space=pltpu.VMEM),),
        input_output_aliases={i: 2 + i for i in range(nb)},
        compiler_params=pltpu.CompilerParams(has_side_effects=_EFFECT),
    )(*bufs, old_send, old_recv, after)
    return out[0], out[1], list(out[2:2 + nb]), out[-1]


def _chip_copies(refs, send_sems, recv_sems):
    pair_ref, land_ref = refs
    x, y, c = _place()
    chips = [(1 - x, y), (x, 1 - y), (1 - x, 1 - y)]
    return [pltpu.make_async_remote_copy(
        src_ref=pair_ref.at[2 * chip[0] + chip[1]], dst_ref=land_ref.at[k],
        send_sem=send_sems.at[k], recv_sem=recv_sems.at[k],
        device_id=(*chip, c), device_id_type=MESH) for k, chip in enumerate(chips)]


def _first_hop_copies(refs, send_sems, recv_sems):
    pair_ref, land_ref = refs
    x, y, c = _place()
    first = ((x + 1 - c) % 2, (y + c) % 2)
    blocks = [2 * first[0] + first[1], 2 * (1 - x) + (1 - y)]
    return [pltpu.make_async_remote_copy(
        src_ref=pair_ref.at[blocks[k]], dst_ref=land_ref.at[k], send_sem=send_sems.at[k], recv_sem=recv_sems.at[k],
        device_id=(*first, c), device_id_type=MESH) for k in range(2)]


def _second_hop_copies(refs, send_sems, recv_sems):
    relay_ref, land_ref = refs
    x, y, c = _place()
    second = ((x + c) % 2, (y + 1 - c) % 2)
    return [pltpu.make_async_remote_copy(
        src_ref=relay_ref, dst_ref=land_ref.at[0], send_sem=send_sems.at[0], recv_sem=recv_sems.at[0],
        device_id=(*second, c), device_id_type=MESH)]


def _own_block_copies(targets):
    def copies(refs, send_sems, recv_sems):
        x, y, c = _place()
        mine = refs[0].at[4 * x + 2 * y + c]
        return [pltpu.make_async_remote_copy(
            src_ref=mine, dst_ref=mine, send_sem=send_sems.at[k], recv_sem=recv_sems.at[k],
            device_id=to, device_id_type=MESH) for k, to in enumerate(targets(x, y, c))]
    return copies


def _my_core_and_sibling(x, y, c):
    return [(x, y, 1 - c), (1 - x, y, c), (x, 1 - y, c), (1 - x, 1 - y, c)]


def _all_others(x, y, c):
    flip = lambda v, f: 1 - v if f else v
    return [(flip(x, r & 4), flip(y, r & 2), flip(c, r & 1)) for r in range(1, N_DEV)]


def _forward_copies(refs, send_sems, recv_sems):
    x, y, c = _place()
    chips = [(1 - x, y), (x, 1 - y), (1 - x, 1 - y)]
    return [pltpu.make_async_remote_copy(
        src_ref=refs[0].at[4 * chip[0] + 2 * chip[1] + c], dst_ref=refs[0].at[4 * chip[0] + 2 * chip[1] + c],
        send_sem=send_sems.at[k], recv_sem=recv_sems.at[k],
        device_id=(x, y, 1 - c), device_id_type=MESH) for k, chip in enumerate(chips)]


def _near_targets(x, y, c):
    return [(x, y, 1 - c), (1 - x, y, c), (x, 1 - y, c)]


def _second_stage_copies(refs, send_sems, recv_sems):
    x, y, c = _place()
    relayed = ((x + 1 - c) % 2, (y + c) % 2, c)
    relay_to = ((x + c) % 2, (y + 1 - c) % 2, c)
    plan = [((1 - x, y, c), (x, y, 1 - c)), ((x, 1 - y, c), (x, y, 1 - c)), (relayed, relay_to)]
    copies = []
    for k, ((px, py, pc), to) in enumerate(plan):
        blk = refs[0].at[4 * px + 2 * py + pc]
        copies.append(pltpu.make_async_remote_copy(
            src_ref=blk, dst_ref=blk, send_sem=send_sems.at[k], recv_sem=recv_sems.at[k],
            device_id=to, device_id_type=MESH))
    return copies


def _diagonal_forward_copies(refs, send_sems, recv_sems):
    x, y, c = _place()
    blk = refs[0].at[4 * (1 - x) + 2 * (1 - y) + c]
    return [pltpu.make_async_remote_copy(
        src_ref=blk, dst_ref=blk, send_sem=send_sems.at[0], recv_sem=recv_sems.at[0],
        device_id=(x, y, 1 - c), device_id_type=MESH)]


def _with_own_slot(block, me):
    return lax.dynamic_update_index_in_dim(lax.empty((N_DEV,) + block.shape, block.dtype), block, me, 0)


def _matmul(a, b, dims, out_dtype, tm, tn, name, dep=None):
    if dims == "nn":
        (m, k), n = a.shape, b.shape[1]
        a_spec = pl.BlockSpec((tm, k), lambda i, j: (i, 0))
        b_spec = pl.BlockSpec((k, tn), lambda i, j: (0, j))
        contract = ((1,), (0,))
    elif dims == "nt":
        (m, k), n = a.shape, b.shape[0]
        a_spec = pl.BlockSpec((tm, k), lambda i, j: (i, 0))
        b_spec = pl.BlockSpec((tn, k), lambda i, j: (j, 0))
        contract = ((1,), (1,))
    else:
        (k, m), n = a.shape, b.shape[1]
        a_spec = pl.BlockSpec((k, tm), lambda i, j: (0, i))
        b_spec = pl.BlockSpec((k, tn), lambda i, j: (0, j))
        contract = ((0,), (0,))
    assert m % tm == 0 and n % tn == 0 and a.dtype == BF16 and b.dtype == BF16

    def body(a_ref, b_ref, *rest):
        rest[-1][...] = lax.dot_general(a_ref[...], b_ref[...], (contract, ((), ())),
                                        preferred_element_type=F32).astype(out_dtype)

    deps = [] if dep is None else [dep]
    return pl.pallas_call(
        body, name=name, grid=(m // tm, n // tn),
        in_specs=[a_spec, b_spec] + [pl.BlockSpec((8, 128), lambda i, j: (0, 0))] * len(deps),
        out_specs=pl.BlockSpec((tm, tn), lambda i, j: (i, j)),
        out_shape=jax.ShapeDtypeStruct((m, n), out_dtype),
        compiler_params=_params(dimension_semantics=("arbitrary", "arbitrary")),
    )(a, b, *deps)


Z_TILE = 768
_Z_TILE_ORDER = ((0, 1, 2, 3, 4, 5, 6), (2, 0, 1, 6, 3, 4, 5), (4, 0, 5, 6, 1, 2, 3), (6, 2, 3, 4, 0, 1, 5))
_Z_EARLY_TILES = 4


def _z_proj(h, w_in_t, order, first, count, z_prev, name, dep=None):
    t = h.shape[0]

    def body(order_ref, h_ref, w_ref, *rest):
        rest[-1][...] = _dot_nt(h_ref[...], w_ref[...])

    prev = [] if z_prev is None else [z_prev]
    deps = [] if dep is None else [dep]
    return pl.pallas_call(
        body, name=name,
        grid_spec=pltpu.PrefetchScalarGridSpec(
            num_scalar_prefetch=1, grid=(count,),
            in_specs=[pl.BlockSpec((t, D_MODEL), lambda j, o: (0, 0)),
                      pl.BlockSpec((Z_TILE, D_MODEL), lambda j, o: (o[first + j], 0))]
            + [pl.BlockSpec(memory_space=pl.ANY)] * len(prev)
            + [pl.BlockSpec((8, 128), lambda j, o: (0, 0))] * len(deps),
            out_specs=pl.BlockSpec((t, Z_TILE), lambda j, o: (0, o[first + j]))),
        out_shape=jax.ShapeDtypeStruct((t, D_IN), F32),
        input_output_aliases={3: 0} if prev else {},
        compiler_params=_params(dimension_semantics=("arbitrary",)),
    )(order, h, w_in_t, *prev, *deps)


def _modulation(c_all, w_ada, b_ada_mine):
    def body(c_ref, w_ref, b_ref, act_ref, mod_ref):
        cv = c_ref[...]
        act = cv * _sigmoid(cv)
        act_ref[...] = act
        mod_ref[...] = jnp.dot(act.astype(BF16), w_ref[...].astype(BF16), preferred_element_type=F32) + b_ref[...]

    return pl.pallas_call(
        body, name="modulation",
        out_shape=(jax.ShapeDtypeStruct(c_all.shape, F32), jax.ShapeDtypeStruct((N_DEV, W_ADA_SHARD), F32)),
        compiler_params=_params(),
    )(c_all, w_ada, b_ada_mine)


def _modulated_norm(x, norm_g, scale, shift, tm=256):
    t, d = x.shape

    def body(x_ref, g_ref, sc_ref, sh_ref, h_ref):
        xv = x_ref[...]
        r = lax.rsqrt(jnp.mean(xv * xv, axis=-1, keepdims=True) + EPS)
        h = (xv * r) * g_ref[...] * (1.0 + sc_ref[...]) + sh_ref[...]
        h_ref[...] = h.astype(BF16)

    row = pl.BlockSpec((1, d), lambda i: (0, 0))
    return pl.pallas_call(
        body, name="modulated_norm", grid=(t // tm,),
        in_specs=[pl.BlockSpec((tm, d), lambda i: (i, 0)), row, row, row],
        out_specs=pl.BlockSpec((tm, d), lambda i: (i, 0)),
        out_shape=jax.ShapeDtypeStruct((t, d), BF16),
        compiler_params=_params(dimension_semantics=("arbitrary",)),
    )(x, norm_g, scale, shift)


def _window_bias(block_index):
    s = lax.broadcasted_iota(jnp.int32, (2 * BLOCK, BLOCK), 0)
    t = lax.broadcasted_iota(jnp.int32, (2 * BLOCK, BLOCK), 1)
    valid = ((s < BLOCK) & (s > t) & (block_index > 0)) | ((s >= BLOCK) & ((s - BLOCK) <= t))
    bias = jnp.where(valid, 0.0, -jnp.inf).astype(F32)
    return jnp.concatenate([bias] * 8, axis=1)


def _heads_t(pair_blocks, g):
    top = lax.broadcasted_iota(jnp.int32, (BLOCK, BLOCK), 0) < HEAD_DIM
    zeros = jnp.zeros((HEAD_DIM, BLOCK), F32)
    tiles = []
    for blk in pair_blocks:
        tp = blk.T
        if g == 0:
            tiles += [jnp.where(top, tp, 0.0), jnp.concatenate([tp[HEAD_DIM:], zeros], axis=0)]
        else:
            tiles += [jnp.concatenate([zeros, tp[:HEAD_DIM]], axis=0), jnp.where(top, 0.0, tp)]
    return jnp.concatenate(tiles, axis=1)


def _pair_block(xt, p, g):
    r0 = HEAD_DIM * g
    even = xt[r0:r0 + HEAD_DIM, (2 * p) * BLOCK:(2 * p + 1) * BLOCK]
    odd = xt[r0:r0 + HEAD_DIM, (2 * p + 1) * BLOCK:(2 * p + 2) * BLOCK]
    return jnp.concatenate([even, odd], axis=0).T


def _softmax_t(scores_t, bias, sink):
    st = scores_t + bias
    m = jnp.maximum(jnp.max(st, axis=0, keepdims=True), sink)
    e = jnp.exp(st - m)
    es = jnp.exp(sink - m)
    inv = 1.0 / (jnp.sum(e, axis=0, keepdims=True) + es)
    return e * inv, es * inv


def _dot(a, b):
    return jnp.dot(a, b, preferred_element_type=F32)


def _dot_nt(a, b):
    return lax.dot_general(a, b, (((1,), (1,)), ((), ())), preferred_element_type=F32)


def _layer_norm_fwd(v):
    mu = jnp.mean(v, axis=-1, keepdims=True)
    xc = v - mu
    rstd = lax.rsqrt(jnp.mean(xc * xc, axis=-1, keepdims=True) + EPS)
    return xc * rstd, rstd


def _tril(transposed=False):
    t = lax.broadcasted_iota(jnp.int32, (BLOCK, BLOCK), 0)
    s = lax.broadcasted_iota(jnp.int32, (BLOCK, BLOCK), 1)
    return s >= t if transposed else t >= s


def _const_spec(shape):
    return pl.BlockSpec(shape, lambda i: (0,) * len(shape))


def _kv_prev_spec(index):
    return pl.BlockSpec((BLOCK, 2 * D_KV), lambda i: (jnp.maximum(index(i) - 1, 0), SEG_KV // (2 * D_KV)))


def _keys_values(z_ref, kvp_ref):
    kvp, kvc = kvp_ref[...], z_ref[:, SEG_KV:SEG_KV + 2 * D_KV]
    kk = jnp.concatenate([kvp[:, :D_KV], kvc[:, :D_KV]], axis=0)
    vv = jnp.concatenate([kvp[:, D_KV:], kvc[:, D_KV:]], axis=0)
    return kk, vv


def _pair_cols(g, p, base=0):
    return slice(base + (4 * g + p) * 128, base + (4 * g + p + 1) * 128)


def _mixer_fwd(z, sink_rows, ln_g, ln_b, sgu_w, sgu_bt):
    t = z.shape[0]

    def body(z_ref, kvp_ref, sink_ref, lng_ref, lnb_ref, w_ref, bt_ref, a_ref):
        bias = _window_bias(pl.program_id(0))
        kk, vv = _keys_values(z_ref, kvp_ref)
        kk_b, vvt_b = kk.astype(BF16), vv.T.astype(BF16)
        for g in range(2):
            qt = _heads_t([z_ref[:, _pair_cols(g, p, SEG_Q)] * ATTN_SCALE for p in range(4)], g).astype(BF16)
            prob, _ = _softmax_t(_dot(kk_b, qt), bias, sink_ref[g])
            ot = _dot(vvt_b, prob.astype(BF16))
            for p in range(4):
                gate = z_ref[:, _pair_cols(g, p, SEG_GA)]
                a_ref[:, _pair_cols(g, p)] = (_pair_block(ot, p, g) * (gate * _sigmoid(gate))).astype(BF16)

        vhat, _ = _layer_norm_fwd(z_ref[:, SEG_VS:SEG_VS + D_SGU])
        vn = vhat * lng_ref[...] + lnb_ref[...]
        tril = _tril()
        for g in range(SGU_GROUPS):
            cols = slice(g * 128, (g + 1) * 128)
            wm = jnp.where(tril, w_ref[g], 0.0).astype(BF16)
            mixed = _dot(wm, vn[:, cols].astype(BF16)) + bt_ref[:, g:g + 1]
            gate = z_ref[:, SEG_GS + g * 128:SEG_GS + (g + 1) * 128]
            a_ref[:, D_ATTN + g * 128:D_ATTN + (g + 1) * 128] = (
                (z_ref[:, SEG_U + g * 128:SEG_U + (g + 1) * 128] * mixed) * (gate * _sigmoid(gate))).astype(BF16)

    return pl.pallas_call(
        body, name="mixer_fwd", grid=(t // BLOCK,),
        in_specs=[pl.BlockSpec((BLOCK, D_IN), lambda i: (i, 0)), _kv_prev_spec(lambda i: i),
                  _const_spec((2, 1, 8 * BLOCK)), _const_spec((1, D_SGU)), _const_spec((1, D_SGU)),
                  _const_spec((SGU_GROUPS, BLOCK, BLOCK)), _const_spec((BLOCK, SGU_GROUPS))],
        out_specs=pl.BlockSpec((BLOCK, D_MODEL), lambda i: (i, 0)),
        out_shape=jax.ShapeDtypeStruct((t, D_MODEL), BF16),
        compiler_params=_params(dimension_semantics=("arbitrary",)),
    )(z, z, sink_rows, ln_g, ln_b, sgu_w, sgu_bt)


def _mixer_bwd(z, da, sink_rows, ln_g, ln_b, sgu_w, sgu_wt, sgu_bt):
    t = z.shape[0]
    nb = t // BLOCK

    def body(z_ref, kvp_ref, da_ref, sink_ref, lng_ref, lnb_ref, w_ref, wt_ref, bt_ref,
             dz_ref, dsink_ref, dw_ref, db_ref, dlng_ref, dlnb_ref, carry_ref, dsink_acc, dbt_acc):
        step = pl.program_id(0)

        @pl.when(step == 0)
        def _():
            carry_ref[...] = jnp.zeros_like(carry_ref)
            dsink_acc[...] = jnp.zeros_like(dsink_acc)
            dbt_acc[...] = jnp.zeros_like(dbt_acc)
            dw_ref[...] = jnp.zeros_like(dw_ref)
            dlng_ref[...] = jnp.zeros_like(dlng_ref)
            dlnb_ref[...] = jnp.zeros_like(dlnb_ref)

        bias = _window_bias(nb - 1 - step)
        kk, vv = _keys_values(z_ref, kvp_ref)
        kk_b, vv_b = kk.astype(BF16), vv.astype(BF16)
        kkt_b, vvt_b = kk.T.astype(BF16), vv.T.astype(BF16)
        dkk = jnp.zeros((2 * BLOCK, D_KV), F32)
        dvv = jnp.zeros((2 * BLOCK, D_KV), F32)
        for g in range(2):
            qt = _heads_t([z_ref[:, _pair_cols(g, p, SEG_Q)] * ATTN_SCALE for p in range(4)], g).astype(BF16)
            prob, sink_prob = _softmax_t(_dot(kk_b, qt), bias, sink_ref[g])
            prob_b = prob.astype(BF16)
            ot = _dot(vvt_b, prob_b)
            gates = [z_ref[:, _pair_cols(g, p, SEG_GA)] for p in range(4)]
            sig = [_sigmoid(gt) for gt in gates]
            d_attn = [da_ref[:, _pair_cols(g, p)] for p in range(4)]
            d_ot = _heads_t([d_attn[p] * (gates[p] * sig[p]) for p in range(4)], g).astype(BF16)
            d_prob = _dot(vv_b, d_ot)
            delta = jnp.sum(prob * d_prob, axis=0, keepdims=True)
            d_scores = (prob * (d_prob - delta)).astype(BF16)
            dsink_acc[g] -= sink_prob * delta
            d_qt = _dot(kkt_b, d_scores)
            dkk = dkk + _dot_nt(d_scores, qt)
            dvv = dvv + _dot_nt(prob_b, d_ot)
            for p in range(4):
                dz_ref[:, _pair_cols(g, p, SEG_Q)] = (_pair_block(d_qt, p, g) * ATTN_SCALE).astype(BF16)
                d_silu = sig[p] * (1.0 + gates[p] * (1.0 - sig[p]))
                dz_ref[:, _pair_cols(g, p, SEG_GA)] = (d_attn[p] * _pair_block(ot, p, g) * d_silu).astype(BF16)
        d_kv = jnp.concatenate([dkk, dvv], axis=1)
        dz_ref[:, SEG_KV:SEG_KV + 2 * D_KV] = (d_kv[BLOCK:] + carry_ref[...]).astype(BF16)
        carry_ref[...] = d_kv[:BLOCK]

        vhat, rstd = _layer_norm_fwd(z_ref[:, SEG_VS:SEG_VS + D_SGU])
        lng = lng_ref[...]
        vn = vhat * lng + lnb_ref[...]
        tril, triu = _tril(), _tril(transposed=True)
        lane = lax.broadcasted_iota(jnp.int32, (BLOCK, 128), 1)
        d_bt = jnp.zeros((BLOCK, 128), F32)
        d_vn = []
        for g in range(SGU_GROUPS):
            cols = slice(g * 128, (g + 1) * 128)
            wm = jnp.where(tril, w_ref[g], 0.0).astype(BF16)
            wmt = jnp.where(triu, wt_ref[g], 0.0).astype(BF16)
            vn_g = vn[:, cols].astype(BF16)
            mixed = _dot(wm, vn_g) + bt_ref[:, g:g + 1]
            gate = z_ref[:, SEG_GS + g * 128:SEG_GS + (g + 1) * 128]
            u = z_ref[:, SEG_U + g * 128:SEG_U + (g + 1) * 128]
            d_out = da_ref[:, D_ATTN + g * 128:D_ATTN + (g + 1) * 128]
            sg = _sigmoid(gate)
            d_um = d_out * (gate * sg)
            dz_ref[:, SEG_U + g * 128:SEG_U + (g + 1) * 128] = (d_um * mixed).astype(BF16)
            dz_ref[:, SEG_GS + g * 128:SEG_GS + (g + 1) * 128] = (
                d_out * (u * mixed) * (sg * (1.0 + gate * (1.0 - sg)))).astype(BF16)
            d_mixed = d_um * u
            d_mixed_b = d_mixed.astype(BF16)
            dw_ref[g] += jnp.where(tril, _dot_nt(d_mixed_b, vn_g), 0.0)
            d_bt = d_bt + jnp.where(lane == g, jnp.sum(d_mixed, axis=-1, keepdims=True), 0.0)
            d_vn.append(_dot(wmt, d_mixed_b))
        dbt_acc[...] += d_bt
        d_vn = jnp.concatenate(d_vn, axis=1)
        dlng_ref[...] += jnp.sum(d_vn * vhat, axis=0, keepdims=True)
        dlnb_ref[...] += jnp.sum(d_vn, axis=0, keepdims=True)
        d_vhat = d_vn * lng
        d_v = rstd * (d_vhat - jnp.mean(d_vhat, axis=-1, keepdims=True)
                      - vhat * jnp.mean(d_vhat * vhat, axis=-1, keepdims=True))
        dz_ref[:, SEG_VS:SEG_VS + D_SGU] = d_v.astype(BF16)

        @pl.when(step == nb - 1)
        def _():
            db_ref[...] = dbt_acc[...].T[:SGU_GROUPS]
            lane_row = lax.broadcasted_iota(jnp.int32, (1, 128), 1)
            d_sink = jnp.zeros((1, 128), F32)
            for g in range(2):
                acc = dsink_acc[g]
                for j in range(8):
                    head_sum = jnp.sum(acc[:, j * BLOCK:(j + 1) * BLOCK], axis=-1, keepdims=True)
                    d_sink = d_sink + jnp.where(lane_row == 8 * g + j, head_sum, 0.0)
            dsink_ref[...] = d_sink

    rev = lambda i: nb - 1 - i
    return pl.pallas_call(
        body, name="mixer_bwd", grid=(nb,),
        in_specs=[pl.BlockSpec((BLOCK, D_IN), lambda i: (rev(i), 0)), _kv_prev_spec(rev),
                  pl.BlockSpec((BLOCK, D_MODEL), lambda i: (rev(i), 0)),
                  _const_spec((2, 1, 8 * BLOCK)), _const_spec((1, D_SGU)), _const_spec((1, D_SGU)),
                  _const_spec((SGU_GROUPS, BLOCK, BLOCK)), _const_spec((SGU_GROUPS, BLOCK, BLOCK)),
                  _const_spec((BLOCK, SGU_GROUPS))],
        out_specs=(pl.BlockSpec((BLOCK, D_IN), lambda i: (rev(i), 0)), _const_spec((1, 128)),
                   _const_spec((SGU_GROUPS, BLOCK, BLOCK)), _const_spec((SGU_GROUPS, BLOCK)),
                   _const_spec((1, D_SGU)), _const_spec((1, D_SGU))),
        out_shape=(jax.ShapeDtypeStruct((t, D_IN), BF16), jax.ShapeDtypeStruct((1, 128), F32),
                   jax.ShapeDtypeStruct((SGU_GROUPS, BLOCK, BLOCK), F32), jax.ShapeDtypeStruct((SGU_GROUPS, BLOCK), F32),
                   jax.ShapeDtypeStruct((1, D_SGU), F32), jax.ShapeDtypeStruct((1, D_SGU), F32)),
        scratch_shapes=[pltpu.VMEM((BLOCK, 2 * D_KV), F32), pltpu.VMEM((2, 1, 8 * BLOCK), F32),
                        pltpu.VMEM((BLOCK, 128), F32)],
        compiler_params=_params(dimension_semantics=("arbitrary",)),
    )(z, z, da, sink_rows, ln_g, ln_b, sgu_w, sgu_wt, sgu_bt)


def _out_proj_head(a, w_out_full, x, target, gate, final_g, tm=256):
    t, d = x.shape

    def body(a_ref, w_ref, x_ref, tg_ref, gate_ref, fg_ref, dx2_ref, dy_ref, loss_ref, dfg_ref, dgate_ref):
        @pl.when(pl.program_id(0) == 0)
        def _():
            loss_ref[...] = jnp.zeros_like(loss_ref)
            dfg_ref[...] = jnp.zeros_like(dfg_ref)
            dgate_ref[...] = jnp.zeros_like(dgate_ref)

        yv, gate, fg = _dot(a_ref[...], w_ref[...]), gate_ref[...], fg_ref[...]
        x2 = x_ref[...] + gate * yv
        r2 = lax.rsqrt(jnp.mean(x2 * x2, axis=-1, keepdims=True) + EPS)
        nrm = x2 * r2
        err = nrm * fg - tg_ref[...]
        loss_ref[...] += 0.5 * jnp.sum(jnp.mean(err * err, axis=-1, keepdims=True), axis=0, keepdims=True)
        d_out = err * (1.0 / d)
        dfg_ref[...] += jnp.sum(d_out * nrm, axis=0, keepdims=True)
        d_nrm = d_out * fg
        dx2 = r2 * (d_nrm - nrm * jnp.mean(d_nrm * nrm, axis=-1, keepdims=True))
        dx2_ref[...] = dx2
        dgate_ref[...] += jnp.sum(dx2 * yv, axis=0, keepdims=True)
        dy_ref[...] = (dx2 * gate).astype(BF16)

    blk = pl.BlockSpec((tm, d), lambda i: (i, 0))
    row = _const_spec((1, d))
    whole = pl.BlockSpec(w_out_full.shape, lambda i: (0, 0), pipeline_mode=pl.Buffered(1))
    return pl.pallas_call(
        body, name="out_proj_head", grid=(t // tm,),
        in_specs=[pl.BlockSpec((tm, a.shape[1]), lambda i: (i, 0)), whole, blk, blk, row, row],
        out_specs=(blk, blk, _const_spec((1, 128)), row, row),
        out_shape=(jax.ShapeDtypeStruct((t, d), F32), jax.ShapeDtypeStruct((t, d), BF16),
                   jax.ShapeDtypeStruct((1, 128), F32), jax.ShapeDtypeStruct((1, d), F32),
                   jax.ShapeDtypeStruct((1, d), F32)),
        compiler_params=_params(dimension_semantics=("arbitrary",)),
    )(a, w_out_full, x, target, gate, final_g)


def _z_proj_bwd_norm(dz, w_in_t, x, dx2, norm_g, scale, dep, tm=256):
    t, d = x.shape

    def body(dz_ref, w_ref, x_ref, dx2_ref, g_ref, sc_ref, dep_ref, gx_ref, dshift_ref, dscale_ref, dg_ref):
        @pl.when(pl.program_id(0) == 0)
        def _():
            dshift_ref[...] = jnp.zeros_like(dshift_ref)
            dscale_ref[...] = jnp.zeros_like(dscale_ref)
            dg_ref[...] = jnp.zeros_like(dg_ref)

        dh, xv, g = _dot(dz_ref[...], w_ref[...]), x_ref[...], g_ref[...]
        one_plus = 1.0 + sc_ref[...]
        r = lax.rsqrt(jnp.mean(xv * xv, axis=-1, keepdims=True) + EPS)
        xn = xv * r
        dshift_ref[...] += jnp.sum(dh, axis=0, keepdims=True)
        dscale_ref[...] += jnp.sum(dh * (xn * g), axis=0, keepdims=True)
        d_y = dh * one_plus
        dg_ref[...] += jnp.sum(d_y * xn, axis=0, keepdims=True)
        d_xn = d_y * g
        gx_ref[...] = dx2_ref[...] + r * (d_xn - xn * jnp.mean(d_xn * xn, axis=-1, keepdims=True))

    blk = pl.BlockSpec((tm, d), lambda i: (i, 0))
    row = _const_spec((1, d))
    whole = pl.BlockSpec(w_in_t.shape, lambda i: (0, 0), pipeline_mode=pl.Buffered(1))
    return pl.pallas_call(
        body, name="z_proj_bwd_norm", grid=(t // tm,),
        in_specs=[pl.BlockSpec((tm, dz.shape[1]), lambda i: (i, 0)), whole, blk, blk, row, row, _const_spec((8, 128))],
        out_specs=(blk, row, row, row),
        out_shape=(jax.ShapeDtypeStruct((t, d), F32),) + (jax.ShapeDtypeStruct((1, d), F32),) * 3,
        compiler_params=_params(dimension_semantics=("arbitrary",)),
    )(dz, w_in_t, x, dx2, norm_g, scale, dep)


def _adamw(w, g, m, v):
    m = ADAM_B1 * m + (1.0 - ADAM_B1) * g
    v = ADAM_B2 * v + (1.0 - ADAM_B2) * (g * g)
    m_hat = m / (1.0 - ADAM_B1 ** ADAM_STEP)
    v_hat = v / (1.0 - ADAM_B2 ** ADAM_STEP)
    delta = -ADAM_LR * (m_hat / (jnp.sqrt(v_hat) + ADAM_EPS) + ADAM_WD * w)
    return delta, m, v


def _relay_sum(second_chip, pair, land, tr):
    _, r, c = pair.shape

    def body(chip_ref, a_ref, b_ref, o_ref):
        o_ref[...] = (a_ref[...].astype(F32) + b_ref[...].astype(F32)).astype(BF16)

    return pl.pallas_call(
        body, name="w_in_grad_relay_sum",
        grid_spec=pltpu.PrefetchScalarGridSpec(
            num_scalar_prefetch=1, grid=(r // tr,),
            in_specs=[pl.BlockSpec((None, tr, c), lambda i, chip_ref: (chip_ref[0], i, 0)),
                      pl.BlockSpec((None, tr, c), lambda i, chip_ref: (1, i, 0))],
            out_specs=pl.BlockSpec((tr, c), lambda i, chip_ref: (i, 0))),
        out_shape=jax.ShapeDtypeStruct((r, c), BF16),
        compiler_params=_params(dimension_semantics=("arbitrary",)),
    )(second_chip, pair, land)


def _adam_from_chips(chip, pair, landed, w, m, v, name, tc):
    _, r, c = pair.shape
    n = len(landed)

    def body(chip_ref, own_ref, *refs):
        w_ref, m_ref, v_ref, g_ref, d_ref, nm_ref, nv_ref = refs[n:]
        g = own_ref[...].astype(F32)
        for k in range(n):
            g = g + refs[k][...].astype(F32)
        g_ref[...] = g
        d_ref[...], nm_ref[...], nv_ref[...] = _adamw(w_ref[...], g, m_ref[...], v_ref[...])

    def landed_spec(index):
        return pl.BlockSpec((None, r, tc), lambda i, chip_ref: (index, 0, i))

    blk = pl.BlockSpec((r, tc), lambda i, chip_ref: (0, i))
    return pl.pallas_call(
        body, name=name,
        grid_spec=pltpu.PrefetchScalarGridSpec(
            num_scalar_prefetch=1, grid=(c // tc,),
            in_specs=[pl.BlockSpec((None, r, tc), lambda i, chip_ref: (chip_ref[0], 0, i))]
            + [landed_spec(index) for _, index in landed] + [blk, blk, blk],
            out_specs=(blk,) * 4),
        out_shape=(jax.ShapeDtypeStruct((r, c), F32),) * 4,
        compiler_params=_params(dimension_semantics=("arbitrary",)),
    )(chip, pair, *[array for array, _ in landed], w, m, v)


def _adam_w_ada(act_t, dmod_mine, w, m, v, tr=256):
    r, c = w.shape

    def body(a_ref, dm_ref, w_ref, m_ref, v_ref, g_ref, d_ref, nm_ref, nv_ref):
        g = _dot(a_ref[...].astype(BF16), dm_ref[...].astype(BF16))
        g_ref[...] = g
        d_ref[...], nm_ref[...], nv_ref[...] = _adamw(w_ref[...], g, m_ref[...], v_ref[...])

    blk = pl.BlockSpec((tr, c), lambda i: (i, 0))
    return pl.pallas_call(
        body, name="adam_w_ada", grid=(r // tr,),
        in_specs=[pl.BlockSpec((tr, N_DEV), lambda i: (i, 0)), _const_spec((N_DEV, c)), blk, blk, blk],
        out_specs=(blk,) * 4, out_shape=(jax.ShapeDtypeStruct((r, c), F32),) * 4,
        compiler_params=_params(dimension_semantics=("arbitrary",)),
    )(act_t, dmod_mine, w, m, v)


def _pack_small(d_shift, d_scale, d_gate, d_norm_g, d_final_g, d_ln_g, d_ln_b, loss, d_sinks, d_sgu_b):
    def body(shift_ref, scale_ref, gate_ref, ng_ref, fg_ref, lng_ref, lnb_ref, loss_ref, sink_ref, b_ref, o_ref):
        o_ref[...] = jnp.zeros_like(o_ref)
        o_ref[ROW_SHIFT:ROW_SHIFT + 1, :] = shift_ref[...]
        o_ref[ROW_SCALE:ROW_SCALE + 1, :] = scale_ref[...]
        o_ref[ROW_GATE:ROW_GATE + 1, :] = gate_ref[...]
        o_ref[ROW_NORM_G:ROW_NORM_G + 1, :] = ng_ref[...]
        o_ref[ROW_FINAL_G:ROW_FINAL_G + 1, :] = fg_ref[...]
        o_ref[ROW_LN:ROW_LN + 1, 0:D_SGU] = lng_ref[...]
        o_ref[ROW_LN:ROW_LN + 1, D_SGU:2 * D_SGU] = lnb_ref[...]
        o_ref[ROW_MISC:ROW_MISC + 1, 0:128] = loss_ref[...]
        o_ref[ROW_MISC:ROW_MISC + 1, 128:256] = sink_ref[...]
        o_ref[ROW_SGU_B:ROW_SGU_B + SGU_GROUPS, 0:BLOCK] = b_ref[...]

    return pl.pallas_call(
        body, name="pack_small", out_shape=jax.ShapeDtypeStruct((SMALL_ROWS, D_MODEL), F32),
        compiler_params=_params(),
    )(d_shift, d_scale, d_gate, d_norm_g, d_final_g, d_ln_g, d_ln_b, loss, d_sinks, d_sgu_b)


_SMALL_NAMES = ("norm_g", "b_ada", "attn_sinks", "sgu_ln_g", "sgu_ln_b", "sgu_w", "sgu_b", "final_g")


def _adam_small(partials, d_sgu_w_all, weights, moments_m, moments_v):
    names = _SMALL_NAMES
    k = len(names)

    def body(*refs):
        p_ref, sw_ref = refs[0], refs[1]
        w_refs, m_refs, v_refs = refs[2:2 + k], refs[2 + k:2 + 2 * k], refs[2 + 2 * k:2 + 3 * k]
        loss_ref, dmod_ref = refs[2 + 3 * k], refs[3 + 3 * k]
        out_refs = refs[4 + 3 * k:4 + 7 * k]
        sum_ref = refs[4 + 7 * k]
        total = p_ref[0]
        for j in range(1, N_DEV):
            total = total + p_ref[j]
        sum_ref[...] = total
        for j in range(N_DEV):
            for part, row in enumerate((ROW_SHIFT, ROW_SCALE, ROW_GATE)):
                dmod_ref[j:j + 1, part * D_MODEL:(part + 1) * D_MODEL] = p_ref[j, row:row + 1, :]
        loss_ref[...] = sum_ref[ROW_MISC:ROW_MISC + 1, 0:1]
        d_sgu_w = sw_ref[0]
        for j in range(1, N_DEV):
            d_sgu_w = d_sgu_w + sw_ref[j]
        grads = {
            "norm_g": sum_ref[ROW_NORM_G:ROW_NORM_G + 1, :],
            "b_ada": jnp.concatenate([sum_ref[r:r + 1, :] for r in (ROW_SHIFT, ROW_SCALE, ROW_GATE)], axis=1),
            "attn_sinks": sum_ref[ROW_MISC:ROW_MISC + 1, 128:128 + N_Q_HEADS],
            "sgu_ln_g": sum_ref[ROW_LN:ROW_LN + 1, 0:D_SGU],
            "sgu_ln_b": sum_ref[ROW_LN:ROW_LN + 1, D_SGU:2 * D_SGU],
            "sgu_w": d_sgu_w[None],
            "sgu_b": sum_ref[ROW_SGU_B:ROW_SGU_B + SGU_GROUPS, 0:BLOCK][None],
            "final_g": sum_ref[ROW_FINAL_G:ROW_FINAL_G + 1, :],
        }
        for i, name in enumerate(names):
            g = grads[name]
            delta, m, v = _adamw(w_refs[i][...], g, m_refs[i][...], v_refs[i][...])
            out_refs[4 * i][...] = g
            out_refs[4 * i + 1][...] = delta
            out_refs[4 * i + 2][...] = m
            out_refs[4 * i + 3][...] = v

    shapes = [jax.ShapeDtypeStruct((1, 1), F32), jax.ShapeDtypeStruct((N_DEV, 3 * D_MODEL), F32)]
    for name in names:
        shapes += [jax.ShapeDtypeStruct(weights[name].shape, F32)] * 4
    outs = pl.pallas_call(
        body, name="adam_small", out_shape=tuple(shapes),
        scratch_shapes=[pltpu.VMEM((SMALL_ROWS, D_MODEL), F32)],
        compiler_params=_params(),
    )(partials, d_sgu_w_all, *[weights[n] for n in names], *[moments_m[n] for n in names],
      *[moments_v[n] for n in names])
    return outs[0], outs[1], {name: outs[2 + 4 * i:6 + 4 * i] for i, name in enumerate(names)}


def kernel(x, c, norm_g, w_ada, b_ada, w_in, attn_sinks, sgu_ln_g, sgu_ln_b, sgu_w, sgu_b, w_out, final_g, loss_target, m_norm_g, m_w_ada, m_b_ada, m_w_in, m_attn_sinks, m_sgu_ln_g, m_sgu_ln_b, m_sgu_w, m_sgu_b, m_w_out, m_final_g, v_norm_g, v_w_ada, v_b_ada, v_w_in, v_attn_sinks, v_sgu_ln_g, v_sgu_ln_b, v_sgu_w, v_sgu_b, v_w_out, v_final_g):
    xi, yi, ci = _place()
    me = 4 * xi + 2 * yi + ci
    x2d, target = x[0], loss_target[0]
    t = x2d.shape[0]

    core = ci.astype(jnp.int32).reshape(1)
    chip = (2 * xi + yi).astype(jnp.int32).reshape(1)

    c_all = _all_gather([c.reshape(8, 256)], "gather_c", True)[0].reshape(N_DEV, D_MODEL)
    b_mine = lax.dynamic_slice(b_ada, (0, me * W_ADA_SHARD), (1, W_ADA_SHARD))
    c_act, mod_part = _modulation(c_all, w_ada[0], b_mine)
    mod_all = _all_gather([mod_part], "gather_mod", True)[0]

    near = _own_block_copies(_near_targets)
    w_in_flight = _start_copies([_with_own_slot(w_in[0].T.astype(BF16), me)], near, 3, mod_all, "gather_w_in_start")
    mod = lax.dynamic_index_in_dim(mod_all, me, axis=1, keepdims=False).reshape(1, 3 * D_MODEL)
    mod = mod + w_in_flight[3][0, 0]
    shift, scale, gate = mod[:, :D_MODEL], mod[:, D_MODEL:2 * D_MODEL], mod[:, 2 * D_MODEL:]
    h = _modulated_norm(x2d, norm_g, scale, shift)

    w_in_pair = _wait_copies(w_in_flight, lambda *a: near(*a)[:1], h, "gather_w_in_sibling_wait")
    tile_order = jnp.asarray(_Z_TILE_ORDER, jnp.int32)[chip[0]]
    z_own = _z_proj(h, w_in_pair[0].reshape(D_IN, D_MODEL), tile_order, 0, 1, None, "z_proj_own")
    w_in_flight = _wait_then_start((w_in_flight[0], w_in_flight[1], w_in_pair, None), lambda *a: near(*a)[1:],
                                   _second_stage_copies, 3, z_own, "gather_w_in_second_stage")
    w_in_most = _wait_copies(w_in_flight, lambda *a: _second_stage_copies(*a)[:2], z_own, "gather_w_in_forward_wait")
    z_early = _z_proj(h, w_in_most[0].reshape(D_IN, D_MODEL), tile_order, 1, _Z_EARLY_TILES - 1, z_own, "z_proj_early")
    w_out_flight = _start_copies([_with_own_slot(w_out[0].astype(BF16), me)], _own_block_copies(_my_core_and_sibling),
                                 4, z_early, "gather_w_out_start")
    w_in_flight = _wait_then_start((w_in_flight[0], w_in_flight[1], w_in_most, None),
                                   lambda *a: _second_stage_copies(*a)[2:], _diagonal_forward_copies, 1,
                                   w_out_flight[3], "gather_w_in_last_stage")
    w_in_all = _wait_copies(w_in_flight, _diagonal_forward_copies, z_early, "gather_w_in_last_wait")[0]
    w_in_t = w_in_all.reshape(D_IN, D_MODEL)
    z = _z_proj(h, w_in_t, tile_order, _Z_EARLY_TILES, 7 - _Z_EARLY_TILES, z_early, "z_proj_late")
    w_out_flight = _wait_then_start(w_out_flight, _own_block_copies(_my_core_and_sibling), _forward_copies, 3, z,
                                    "gather_w_out_forward_stage")
    sink_rows = jnp.repeat(attn_sinks.reshape(N_Q_HEADS), BLOCK).reshape(2, 1, 8 * BLOCK)
    sgu_bt = sgu_b[0].T
    a = _mixer_fwd(z, sink_rows + w_out_flight[3][0, 0], sgu_ln_g, sgu_ln_b, sgu_w[0], sgu_bt)
    w_out_all = _wait_copies(w_out_flight, _forward_copies, a, "gather_w_out_forward_wait")[0]
    w_out_full = w_out_all.reshape(D_MODEL, D_MODEL)
    final_g_row = final_g.reshape(1, D_MODEL)
    dx2, dy, loss_part, d_final_g, d_gate = _out_proj_head(a, w_out_full, x2d, target, gate, final_g_row)

    da = _matmul(dy, w_out_full, "nt", F32, min(t, 1024), 1024, "out_proj_bwd")
    dw_out = _matmul(a, dy, "tn", BF16, 1024, 1024, "w_out_grad").reshape(4, 2, W_OUT_SHARD, D_MODEL)
    pair_out = _pair_reduce(dw_out, "w_out_grad_pair_reduce", W_OUT_SHARD // 2)
    out_flight = _start_copies([pair_out, lax.empty((3, W_OUT_SHARD, D_MODEL), BF16)], _chip_copies, 3, core,
                               "w_out_grad_chip_start")
    dz, d_sinks, d_sgu_w, d_sgu_b, d_ln_g, d_ln_b = _mixer_bwd(
        z, da, sink_rows + out_flight[3][0, 0], sgu_ln_g, sgu_ln_b, sgu_w[0], jnp.swapaxes(sgu_w[0], 1, 2), sgu_bt)
    sgu_w_flight = _start_copies([_with_own_slot(d_sgu_w, me)], _own_block_copies(_all_others), N_DEV - 1, core,
                                 "sgu_w_grad_gather_start")
    dw_in_t = _matmul(dz, h, "tn", BF16, 768, D_MODEL, "w_in_grad", dep=sgu_w_flight[3])
    dw_in_t = dw_in_t.reshape(4, 2, W_IN_SHARD, D_MODEL)
    pair_in = _pair_reduce(dw_in_t, "w_in_grad_pair_reduce", W_IN_SHARD // 3)
    hop1 = _start_copies([pair_in, lax.empty((2, W_IN_SHARD, D_MODEL), BF16)], _first_hop_copies, 2, core,
                         "w_in_grad_first_hop_start")
    grad_x, d_shift, d_scale, d_norm_g = _z_proj_bwd_norm(dz, w_in_t, x2d, dx2, norm_g, scale, hop1[3])

    partial = _pack_small(d_shift, d_scale, d_gate, d_norm_g, d_final_g, d_ln_g, d_ln_b, loss_part, d_sinks, d_sgu_b)
    small_flight = _start_copies([_with_own_slot(partial, me)], _own_block_copies(_all_others), N_DEV - 1, core,
                                 "small_grad_gather_start")
    pair_in, land_first = _wait_copies(hop1, _first_hop_copies, small_flight[3], "w_in_grad_first_hop_wait")
    second_chip = (2 * ((xi + ci) % 2) + (yi + 1 - ci) % 2).astype(jnp.int32).reshape(1)
    relay = _relay_sum(second_chip, pair_in, land_first, W_IN_SHARD // 3)
    hop2 = _start_copies([relay, lax.empty((1, W_IN_SHARD, D_MODEL), BF16)], _second_hop_copies, 1, core,
                         "w_in_grad_second_hop_start")
    pair_out, land_out = _wait_copies(out_flight, _chip_copies, hop2[3], "w_out_grad_chip_wait")
    big = {"w_out": _adam_from_chips(chip, pair_out, [(land_out, k) for k in range(3)], w_out[0], m_w_out[0],
                                     v_w_out[0], "adam_w_out", 1024)}
    partial_all = _wait_copies(small_flight, _own_block_copies(_all_others), big["w_out"][0],
                               "small_grad_gather_wait")[0]
    d_sgu_w_all = _wait_copies(sgu_w_flight, _own_block_copies(_all_others), partial_all, "sgu_w_grad_gather_wait")[0]
    weights = {"norm_g": norm_g, "b_ada": b_ada, "attn_sinks": attn_sinks, "sgu_ln_g": sgu_ln_g,
               "sgu_ln_b": sgu_ln_b, "sgu_w": sgu_w, "sgu_b": sgu_b, "final_g": final_g_row}
    moments_m = {"norm_g": m_norm_g, "b_ada": m_b_ada, "attn_sinks": m_attn_sinks, "sgu_ln_g": m_sgu_ln_g,
                 "sgu_ln_b": m_sgu_ln_b, "sgu_w": m_sgu_w, "sgu_b": m_sgu_b,
                 "final_g": m_final_g.reshape(1, D_MODEL)}
    moments_v = {"norm_g": v_norm_g, "b_ada": v_b_ada, "attn_sinks": v_attn_sinks, "sgu_ln_g": v_sgu_ln_g,
                 "sgu_ln_b": v_sgu_ln_b, "sgu_w": v_sgu_w, "sgu_b": v_sgu_b,
                 "final_g": v_final_g.reshape(1, D_MODEL)}
    loss, dmod_all, small = _adam_small(partial_all, d_sgu_w_all, weights, moments_m, moments_v)
    small["final_g"] = tuple(o.reshape(D_MODEL) for o in small["final_g"])

    dmod_mine = lax.dynamic_slice(dmod_all, (0, me * W_ADA_SHARD), (N_DEV, W_ADA_SHARD))
    big["w_ada"] = _adam_w_ada(c_act.T, dmod_mine, w_ada[0], m_w_ada[0], v_w_ada[0])
    _, land_second = _wait_copies(hop2, _second_hop_copies, big["w_ada"][0], "w_in_grad_second_hop_wait")
    big["w_in"] = tuple(o.T for o in _adam_from_chips(
        chip, pair_in, [(land_first, 0), (land_second, 0)], w_in[0].T, m_w_in[0].T, v_w_in[0].T, "adam_w_in", 256))
    order = ["norm_g", "w_ada", "b_ada", "w_in", "attn_sinks", "sgu_ln_g", "sgu_ln_b", "sgu_w", "sgu_b", "w_out",
             "final_g"]
    outs = [loss.reshape(()), grad_x[None]]
    for k in range(4):
        for name in order:
            outs.append(big[name][k][None] if name in big else small[name][k])
    return tuple(outs)
```

```python
import jax
import jax.numpy as jnp
from jax import lax
from jax.experimental import pallas as pl
from jax.experimental.pallas import tpu as pltpu

F32 = jnp.float32
BF16 = jnp.bfloat16
MESH = pl.DeviceIdType.MESH

N_DEV = 8
D_MODEL = 2048
HEAD_DIM = 64
D_ATTN = 1024
N_Q_HEADS = 16
D_KV = 128
BLOCK = 128
D_SGU = 1024
SGU_GROUPS = 8
D_IN = 5376
W_IN_SHARD = D_IN // N_DEV
W_OUT_SHARD = D_MODEL // N_DEV
W_ADA_SHARD = 3 * D_MODEL // N_DEV
EPS = 1e-6
ATTN_SCALE = 0.125

ADAM_LR = 0.001
ADAM_B1 = 0.9
ADAM_B2 = 0.999
ADAM_EPS = 1e-08
ADAM_WD = 0.01
ADAM_STEP = 10

SEG_Q, SEG_KV, SEG_GA, SEG_U, SEG_VS, SEG_GS = 0, 1024, 1280, 2304, 3328, 4352

VMEM_LIMIT = 56 * 1024 * 1024

ROW_SHIFT, ROW_SCALE, ROW_GATE, ROW_NORM_G, ROW_FINAL_G, ROW_LN, ROW_MISC, ROW_SGU_B = 0, 1, 2, 3, 4, 5, 6, 8
SMALL_ROWS = 16


def _params(**kw):
    return pltpu.CompilerParams(vmem_limit_bytes=VMEM_LIMIT, **kw)


def _sigmoid(x):
    return 0.5 * (jnp.tanh(0.5 * x) + 1.0)


def _place():
    return lax.axis_index("x"), lax.axis_index("y"), lax.axis_index("c")


def _pair_reduce(blocks, name, row_chunk):
    _, _, r, cols = blocks.shape
    assert r % row_chunk == 0

    def body(in_ref, out_ref, land, own, summed, send_sems, recv_sems, own_sems, out_sems):
        x, y, c = _place()
        sends, loads, stores = [], [], []
        for m in range(4):
            cp = pltpu.make_async_remote_copy(
                src_ref=in_ref.at[m, 1 - c], dst_ref=land.at[m], send_sem=send_sems.at[m], recv_sem=recv_sems.at[m],
                device_id=(x, y, 1 - c), device_id_type=MESH)
            cp.start()
            sends.append(cp)
            ld = pltpu.make_async_copy(in_ref.at[m, c], own.at[m], own_sems.at[m])
            ld.start()
            loads.append(ld)
        for m in range(4):
            sends[m].wait_recv()
            loads[m].wait()
            for k in range(r // row_chunk):
                rows = slice(k * row_chunk, (k + 1) * row_chunk)
                summed[m, rows, :] = (own[m, rows, :].astype(F32) + land[m, rows, :].astype(F32)).astype(BF16)
            st = pltpu.make_async_copy(summed.at[m], out_ref.at[m], out_sems.at[m])
            st.start()
            stores.append(st)
        for m in range(4):
            sends[m].wait_send()
            stores[m].wait()

    spec = pl.BlockSpec(memory_space=pl.ANY)
    return pl.pallas_call(
        body, name=name, out_shape=jax.ShapeDtypeStruct((4, r, cols), BF16),
        in_specs=[spec], out_specs=spec,
        scratch_shapes=[pltpu.VMEM((4, r, cols), BF16), pltpu.VMEM((4, r, cols), BF16), pltpu.VMEM((4, r, cols), BF16),
                        pltpu.SemaphoreType.DMA((4,)), pltpu.SemaphoreType.DMA((4,)), pltpu.SemaphoreType.DMA((4,)),
                        pltpu.SemaphoreType.DMA((4,))],
        compiler_params=_params(),
    )(blocks)


_HBM = pl.BlockSpec(memory_space=pltpu.HBM)
_SEM = pl.BlockSpec(memory_space=pltpu.SEMAPHORE)
_EFFECT = pltpu.SideEffectType.DATAFLOW_SIDE_EFFECTING


def _start_copies(bufs, copies, n_copies, after, name):
    nb = len(bufs)

    def body(*refs):
        for cp in copies(refs[:nb], refs[nb + 1], refs[nb + 2]):
            cp.start()
        refs[-1][...] = jnp.zeros_like(refs[-1])

    out = pl.pallas_call(
        body, name=name,
        out_shape=(pltpu.SemaphoreType.DMA((n_copies,)), pltpu.SemaphoreType.DMA((n_copies,)),
                   *[pltpu.HBM(b.shape, b.dtype) for b in bufs], jax.ShapeDtypeStruct((8, 128), F32)),
        in_specs=(_HBM,) * nb + (pl.BlockSpec(memory_space=pl.ANY),),
        out_specs=(_SEM, _SEM) + (_HBM,) * nb + (pl.BlockSpec(memory_space=pltpu.VMEM),),
        input_output_aliases={i: 2 + i for i in range(nb)},
        compiler_params=pltpu.CompilerParams(has_side_effects=_EFFECT),
    )(*[pltpu.with_memory_space_constraint(b, pltpu.HBM) for b in bufs], after)
    return out[0], out[1], list(out[2:2 + nb]), out[-1]


def _wait_copies(flight, copies, after, name):
    send_sems, recv_sems, bufs, _ = flight
    nb = len(bufs)

    def body(*refs):
        for cp in copies(refs[:nb], refs[nb], refs[nb + 1]):
            cp.wait_send()
            cp.wait_recv()

    return pl.pallas_call(
        body, name=name,
        out_shape=tuple(pltpu.HBM(b.shape, b.dtype) for b in bufs),
        in_specs=(_HBM,) * nb + (_SEM, _SEM, pl.BlockSpec(memory_space=pl.ANY)), out_specs=(_HBM,) * nb,
        input_output_aliases={i: i for i in range(nb)},
        compiler_params=pltpu.CompilerParams(has_side_effects=_EFFECT),
    )(*bufs, send_sems, recv_sems, after)


def _wait_then_start(flight, waited, started, n_started, after, name):
    old_send, old_recv, bufs, _ = flight
    nb = len(bufs)

    def body(*refs):
        for cp in waited(refs[:nb], refs[nb], refs[nb + 1]):
            cp.wait_send()
            cp.wait_recv()
        for cp in started(refs[:nb], refs[nb + 3], refs[nb + 4]):
            cp.start()
        refs[-1][...] = jnp.zeros_like(refs[-1])

    out = pl.pallas_call(
        body, name=name,
        out_shape=(pltpu.SemaphoreType.DMA((n_started,)), pltpu.SemaphoreType.DMA((n_started,)),
                   *[pltpu.HBM(b.shape, b.dtype) for b in bufs], jax.ShapeDtypeStruct((8, 128), F32)),
        in_specs=(_HBM,) * nb + (_SEM, _SEM, pl.BlockSpec(memory_space=pl.ANY)),
        out_specs=(_SEM, _SEM) + (_HBM,) * nb + (pl.BlockSpec(memory_space=pltpu.VMEM),),
        input_output_aliases={i: 2 + i for i in range(nb)},
        compiler_params=pltpu.CompilerParams(has_side_effects=_EFFECT),
    )(*bufs, old_send, old_recv, after)
    return out[0], out[1], list(out[2:2 + nb]), out[-1]


def _chip_copies(refs, send_sems, recv_sems):
    pair_ref, land_ref = refs
    x, y, c = _place()
    chips = [(1 - x, y), (x, 1 - y), (1 - x, 1 - y)]
    return [pltpu.make_async_remote_copy(
        src_ref=pair_ref.at[2 * chip[0] + chip[1]], dst_ref=land_ref.at[k],
        send_sem=send_sems.at[k], recv_sem=recv_sems.at[k],
        device_id=(*chip, c), device_id_type=MESH) for k, chip in enumerate(chips)]


def _first_hop_copies(refs, send_sems, recv_sems):
    pair_ref, land_ref = refs
    x, y, c = _place()
    first = ((x + 1 - c) % 2, (y + c) % 2)
    blocks = [2 * first[0] + first[1], 2 * (1 - x) + (1 - y)]
    return [pltpu.make_async_remote_copy(
        src_ref=pair_ref.at[blocks[k]], dst_ref=land_ref.at[k], send_sem=send_sems.at[k], recv_sem=recv_sems.at[k],
        device_id=(*first, c), device_id_type=MESH) for k in range(2)]


def _second_hop_copies(refs, send_sems, recv_sems):
    relay_ref, land_ref = refs
    x, y, c = _place()
    second = ((x + c) % 2, (y + 1 - c) % 2)
    return [pltpu.make_async_remote_copy(
        src_ref=relay_ref, dst_ref=land_ref.at[0], send_sem=send_sems.at[0], recv_sem=recv_sems.at[0],
        device_id=(*second, c), device_id_type=MESH)]


def _own_block_copies(targets):
    def copies(refs, send_sems, recv_sems):
        x, y, c = _place()
        mine = refs[0].at[4 * x + 2 * y + c]
        return [pltpu.make_async_remote_copy(
            src_ref=mine, dst_ref=mine, send_sem=send_sems.at[k], recv_sem=recv_sems.at[k],
            device_id=to, device_id_type=MESH) for k, to in enumerate(targets(x, y, c))]
    return copies


def _my_core_and_sibling(x, y, c):
    return [(x, y, 1 - c), (1 - x, y, c), (x, 1 - y, c), (1 - x, 1 - y, c)]


def _all_others(x, y, c):
    flip = lambda v, f: 1 - v if f else v
    return [(flip(x, r & 4), flip(y, r & 2), flip(c, r & 1)) for r in range(1, N_DEV)]


def _forward_copies(refs, send_sems, recv_sems):
    x, y, c = _place()
    chips = [(1 - x, y), (x, 1 - y), (1 - x, 1 - y)]
    return [pltpu.make_async_remote_copy(
        src_ref=refs[0].at[4 * chip[0] + 2 * chip[1] + c], dst_ref=refs[0].at[4 * chip[0] + 2 * chip[1] + c],
        send_sem=send_sems.at[k], recv_sem=recv_sems.at[k],
        device_id=(x, y, 1 - c), device_id_type=MESH) for k, chip in enumerate(chips)]


def _first_axis_chip(x, y, c):
    return (x + 1 - c) % 2, (y + c) % 2


def _second_axis_chip(x, y, c):
    return (x + c) % 2, (y + 1 - c) % 2


def _first_targets(x, y, c):
    return [(x, y, 1 - c), (*_first_axis_chip(x, y, c), c)]


def _second_targets(x, y, c):
    return [(*_second_axis_chip(x, y, c), c)]


def _all_gather_small(shard, name):
    def body(in_ref, out_ref, send_sems, recv_sems, local_sem):
        x, y, c = _place()
        me, sibling = 4 * x + 2 * y + c, (x, y, 1 - c)
        first, second = _first_axis_chip(x, y, c), _second_axis_chip(x, y, c)

        def pair(chip):
            return out_ref.at[pl.ds(2 * (2 * chip[0] + chip[1]), 2)]

        def exchange(k, src, dst, to):
            cp = pltpu.make_async_remote_copy(src_ref=src, dst_ref=dst, send_sem=send_sems.at[k],
                                              recv_sem=recv_sems.at[k], device_id=to, device_id_type=MESH)
            cp.start()
            cp.wait()

        own = pltpu.make_async_copy(in_ref, out_ref.at[me], local_sem)
        own.start()
        exchange(0, in_ref, out_ref.at[me], sibling)
        own.wait()
        exchange(1, pair((x, y)), pair((x, y)), (*second, c))
        exchange(2, pair(second), pair(second), sibling)
        exchange(3, pair(first), pair(first), (*second, c))

    spec = pl.BlockSpec(memory_space=pltpu.VMEM)
    return pl.pallas_call(
        body, name=name, out_shape=jax.ShapeDtypeStruct((N_DEV,) + shard.shape, shard.dtype),
        in_specs=[spec], out_specs=spec,
        scratch_shapes=[pltpu.SemaphoreType.DMA((4,)), pltpu.SemaphoreType.DMA((4,)), pltpu.SemaphoreType.DMA],
        compiler_params=_params(),
    )(shard)


def _second_stage_copies(refs, send_sems, recv_sems):
    x, y, c = _place()
    relayed = ((x + 1 - c) % 2, (y + c) % 2, c)
    relay_to = ((x + c) % 2, (y + 1 - c) % 2, c)
    plan = [((1 - x, y, c), (x, y, 1 - c)), ((x, 1 - y, c), (x, y, 1 - c)), (relayed, relay_to)]
    copies = []
    for k, ((px, py, pc), to) in enumerate(plan):
        blk = refs[0].at[4 * px + 2 * py + pc]
        copies.append(pltpu.make_async_remote_copy(
            src_ref=blk, dst_ref=blk, send_sem=send_sems.at[k], recv_sem=recv_sems.at[k],
            device_id=to, device_id_type=MESH))
    return copies


def _diagonal_forward_copies(refs, send_sems, recv_sems):
    x, y, c = _place()
    blk = refs[0].at[4 * (1 - x) + 2 * (1 - y) + c]
    return [pltpu.make_async_remote_copy(
        src_ref=blk, dst_ref=blk, send_sem=send_sems.at[0], recv_sem=recv_sems.at[0],
        device_id=(x, y, 1 - c), device_id_type=MESH)]


def _with_own_slot(block, me):
    return lax.dynamic_update_index_in_dim(lax.empty((N_DEV,) + block.shape, block.dtype), block, me, 0)


def _matmul(a, b, dims, out_dtype, tm, tn, name, dep=None):
    if dims == "nn":
        (m, k), n = a.shape, b.shape[1]
        a_spec = pl.BlockSpec((tm, k), lambda i, j: (i, 0))
        b_spec = pl.BlockSpec((k, tn), lambda i, j: (0, j))
        contract = ((1,), (0,))
    elif dims == "nt":
        (m, k), n = a.shape, b.shape[0]
        a_spec = pl.BlockSpec((tm, k), lambda i, j: (i, 0))
        b_spec = pl.BlockSpec((tn, k), lambda i, j: (j, 0))
        contract = ((1,), (1,))
    else:
        (k, m), n = a.shape, b.shape[1]
        a_spec = pl.BlockSpec((k, tm), lambda i, j: (0, i))
        b_spec = pl.BlockSpec((k, tn), lambda i, j: (0, j))
        contract = ((0,), (0,))
    assert m % tm == 0 and n % tn == 0 and a.dtype == BF16 and b.dtype == BF16

    def body(a_ref, b_ref, *rest):
        rest[-1][...] = lax.dot_general(a_ref[...], b_ref[...], (contract, ((), ())),
                                        preferred_element_type=F32).astype(out_dtype)

    deps = [] if dep is None else [dep]
    return pl.pallas_call(
        body, name=name, grid=(m // tm, n // tn),
        in_specs=[a_spec, b_spec] + [pl.BlockSpec((8, 128), lambda i, j: (0, 0))] * len(deps),
        out_specs=pl.BlockSpec((tm, tn), lambda i, j: (i, j)),
        out_shape=jax.ShapeDtypeStruct((m, n), out_dtype),
        compiler_params=_params(dimension_semantics=("arbitrary", "arbitrary")),
    )(a, b, *deps)


Z_TILE = 768
_Z_TILE_ORDER = ((0, 1, 2, 3, 4, 5, 6), (2, 0, 1, 6, 3, 4, 5), (4, 0, 5, 6, 1, 2, 3), (6, 2, 3, 4, 0, 1, 5))
_Z_EARLY_TILES = 4


def _z_proj(h, w_in_t, order, first, count, z_prev, name, dep=None):
    t = h.shape[0]

    def body(order_ref, h_ref, w_ref, *rest):
        rest[-1][...] = _dot_nt(h_ref[...], w_ref[...])

    prev = [] if z_prev is None else [z_prev]
    deps = [] if dep is None else [dep]
    return pl.pallas_call(
        body, name=name,
        grid_spec=pltpu.PrefetchScalarGridSpec(
            num_scalar_prefetch=1, grid=(count,),
            in_specs=[pl.BlockSpec((t, D_MODEL), lambda j, o: (0, 0)),
                      pl.BlockSpec((Z_TILE, D_MODEL), lambda j, o: (o[first + j], 0))]
            + [pl.BlockSpec(memory_space=pl.ANY)] * len(prev)
            + [pl.BlockSpec((8, 128), lambda j, o: (0, 0))] * len(deps),
            out_specs=pl.BlockSpec((t, Z_TILE), lambda j, o: (0, o[first + j]))),
        out_shape=jax.ShapeDtypeStruct((t, D_IN), F32),
        input_output_aliases={3: 0} if prev else {},
        compiler_params=_params(dimension_semantics=("arbitrary",)),
    )(order, h, w_in_t, *prev, *deps)


def _modulation(c_all, w_ada, b_ada_mine):
    def body(c_ref, w_ref, b_ref, act_ref, mod_ref):
        cv = c_ref[...]
        act = cv * _sigmoid(cv)
        act_ref[...] = act
        mod_ref[...] = jnp.dot(act.astype(BF16), w_ref[...].astype(BF16), preferred_element_type=F32) + b_ref[...]

    return pl.pallas_call(
        body, name="modulation",
        out_shape=(jax.ShapeDtypeStruct(c_all.shape, F32), jax.ShapeDtypeStruct((N_DEV, W_ADA_SHARD), F32)),
        compiler_params=_params(),
    )(c_all, w_ada, b_ada_mine)


def _modulated_norm(x, norm_g, scale, shift, tm=256):
    t, d = x.shape

    def body(x_ref, g_ref, sc_ref, sh_ref, h_ref):
        xv = x_ref[...]
        r = lax.rsqrt(jnp.mean(xv * xv, axis=-1, keepdims=True) + EPS)
        h = (xv * r) * g_ref[...] * (1.0 + sc_ref[...]) + sh_ref[...]
        h_ref[...] = h.astype(BF16)

    row = pl.BlockSpec((1, d), lambda i: (0, 0))
    return pl.pallas_call(
        body, name="modulated_norm", grid=(t // tm,),
        in_specs=[pl.BlockSpec((tm, d), lambda i: (i, 0)), row, row, row],
        out_specs=pl.BlockSpec((tm, d), lambda i: (i, 0)),
        out_shape=jax.ShapeDtypeStruct((t, d), BF16),
        compiler_params=_params(dimension_semantics=("arbitrary",)),
    )(x, norm_g, scale, shift)


def _window_bias(block_index):
    s = lax.broadcasted_iota(jnp.int32, (2 * BLOCK, BLOCK), 0)
    t = lax.broadcasted_iota(jnp.int32, (2 * BLOCK, BLOCK), 1)
    valid = ((s < BLOCK) & (s > t) & (block_index > 0)) | ((s >= BLOCK) & ((s - BLOCK) <= t))
    bias = jnp.where(valid, 0.0, -jnp.inf).astype(F32)
    return jnp.concatenate([bias] * 8, axis=1)


def _heads_t(pair_blocks, g):
    top = lax.broadcasted_iota(jnp.int32, (BLOCK, BLOCK), 0) < HEAD_DIM
    zeros = jnp.zeros((HEAD_DIM, BLOCK), F32)
    tiles = []
    for blk in pair_blocks:
        tp = blk.T
        if g == 0:
            tiles += [jnp.where(top, tp, 0.0), jnp.concatenate([tp[HEAD_DIM:], zeros], axis=0)]
        else:
            tiles += [jnp.concatenate([zeros, tp[:HEAD_DIM]], axis=0), jnp.where(top, 0.0, tp)]
    return jnp.concatenate(tiles, axis=1)


def _pair_block(xt, p, g):
    r0 = HEAD_DIM * g
    even = xt[r0:r0 + HEAD_DIM, (2 * p) * BLOCK:(2 * p + 1) * BLOCK]
    odd = xt[r0:r0 + HEAD_DIM, (2 * p + 1) * BLOCK:(2 * p + 2) * BLOCK]
    return jnp.concatenate([even, odd], axis=0).T


def _softmax_t(scores_t, bias, sink):
    st = scores_t + bias
    m = jnp.maximum(jnp.max(st, axis=0, keepdims=True), sink)
    e = jnp.exp(st - m)
    es = jnp.exp(sink - m)
    inv = 1.0 / (jnp.sum(e, axis=0, keepdims=True) + es)
    return e * inv, es * inv


def _dot(a, b):
    return jnp.dot(a, b, preferred_element_type=F32)


def _dot_nt(a, b):
    return lax.dot_general(a, b, (((1,), (1,)), ((), ())), preferred_element_type=F32)


def _layer_norm_fwd(v):
    mu = jnp.mean(v, axis=-1, keepdims=True)
    xc = v - mu
    rstd = lax.rsqrt(jnp.mean(xc * xc, axis=-1, keepdims=True) + EPS)
    return xc * rstd, rstd


def _tril(transposed=False):
    t = lax.broadcasted_iota(jnp.int32, (BLOCK, BLOCK), 0)
    s = lax.broadcasted_iota(jnp.int32, (BLOCK, BLOCK), 1)
    return s >= t if transposed else t >= s


def _const_spec(shape):
    return pl.BlockSpec(shape, lambda i: (0,) * len(shape))


def _kv_prev_spec(index):
    return pl.BlockSpec((BLOCK, 2 * D_KV), lambda i: (jnp.maximum(index(i) - 1, 0), SEG_KV // (2 * D_KV)))


def _keys_values(z_ref, kvp_ref):
    kvp, kvc = kvp_ref[...], z_ref[:, SEG_KV:SEG_KV + 2 * D_KV]
    kk = jnp.concatenate([kvp[:, :D_KV], kvc[:, :D_KV]], axis=0)
    vv = jnp.concatenate([kvp[:, D_KV:], kvc[:, D_KV:]], axis=0)
    return kk, vv


def _pair_cols(g, p, base=0):
    return slice(base + (4 * g + p) * 128, base + (4 * g + p + 1) * 128)


def _mixer_fwd(z, sink_rows, ln_g, ln_b, sgu_w, sgu_bt):
    t = z.shape[0]

    def body(z_ref, kvp_ref, sink_ref, lng_ref, lnb_ref, w_ref, bt_ref, a_ref):
        bias = _window_bias(pl.program_id(0))
        kk, vv = _keys_values(z_ref, kvp_ref)
        kk_b, vvt_b = kk.astype(BF16), vv.T.astype(BF16)
        for g in range(2):
            qt = _heads_t([z_ref[:, _pair_cols(g, p, SEG_Q)] * ATTN_SCALE for p in range(4)], g).astype(BF16)
            prob, _ = _softmax_t(_dot(kk_b, qt), bias, sink_ref[g])
            ot = _dot(vvt_b, prob.astype(BF16))
            for p in range(4):
                gate = z_ref[:, _pair_cols(g, p, SEG_GA)]
                a_ref[:, _pair_cols(g, p)] = (_pair_block(ot, p, g) * (gate * _sigmoid(gate))).astype(BF16)

        vhat, _ = _layer_norm_fwd(z_ref[:, SEG_VS:SEG_VS + D_SGU])
        vn = vhat * lng_ref[...] + lnb_ref[...]
        tril = _tril()
        for g in range(SGU_GROUPS):
            cols = slice(g * 128, (g + 1) * 128)
            wm = jnp.where(tril, w_ref[g], 0.0).astype(BF16)
            mixed = _dot(wm, vn[:, cols].astype(BF16)) + bt_ref[:, g:g + 1]
            gate = z_ref[:, SEG_GS + g * 128:SEG_GS + (g + 1) * 128]
            a_ref[:, D_ATTN + g * 128:D_ATTN + (g + 1) * 128] = (
                (z_ref[:, SEG_U + g * 128:SEG_U + (g + 1) * 128] * mixed) * (gate * _sigmoid(gate))).astype(BF16)

    return pl.pallas_call(
        body, name="mixer_fwd", grid=(t // BLOCK,),
        in_specs=[pl.BlockSpec((BLOCK, D_IN), lambda i: (i, 0)), _kv_prev_spec(lambda i: i),
                  _const_spec((2, 1, 8 * BLOCK)), _const_spec((1, D_SGU)), _const_spec((1, D_SGU)),
                  _const_spec((SGU_GROUPS, BLOCK, BLOCK)), _const_spec((BLOCK, SGU_GROUPS))],
        out_specs=pl.BlockSpec((BLOCK, D_MODEL), lambda i: (i, 0)),
        out_shape=jax.ShapeDtypeStruct((t, D_MODEL), BF16),
        compiler_params=_params(dimension_semantics=("arbitrary",)),
    )(z, z, sink_rows, ln_g, ln_b, sgu_w, sgu_bt)


def _mixer_bwd(z, da, sink_rows, ln_g, ln_b, sgu_w, sgu_wt, sgu_bt):
    t = z.shape[0]
    nb = t // BLOCK

    def body(z_ref, kvp_ref, da_ref, sink_ref, lng_ref, lnb_ref, w_ref, wt_ref, bt_ref,
             dz_ref, dsink_ref, dw_ref, db_ref, dlng_ref, dlnb_ref, carry_ref, dsink_acc, dbt_acc):
        step = pl.program_id(0)

        @pl.when(step == 0)
        def _():
            carry_ref[...] = jnp.zeros_like(carry_ref)
            dsink_acc[...] = jnp.zeros_like(dsink_acc)
            dbt_acc[...] = jnp.zeros_like(dbt_acc)
            dw_ref[...] = jnp.zeros_like(dw_ref)
            dlng_ref[...] = jnp.zeros_like(dlng_ref)
            dlnb_ref[...] = jnp.zeros_like(dlnb_ref)

        bias = _window_bias(nb - 1 - step)
        kk, vv = _keys_values(z_ref, kvp_ref)
        kk_b, vv_b = kk.astype(BF16), vv.astype(BF16)
        kkt_b, vvt_b = kk.T.astype(BF16), vv.T.astype(BF16)
        dkk = jnp.zeros((2 * BLOCK, D_KV), F32)
        dvv = jnp.zeros((2 * BLOCK, D_KV), F32)
        for g in range(2):
            qt = _heads_t([z_ref[:, _pair_cols(g, p, SEG_Q)] * ATTN_SCALE for p in range(4)], g).astype(BF16)
            prob, sink_prob = _softmax_t(_dot(kk_b, qt), bias, sink_ref[g])
            prob_b = prob.astype(BF16)
            ot = _dot(vvt_b, prob_b)
            gates = [z_ref[:, _pair_cols(g, p, SEG_GA)] for p in range(4)]
            sig = [_sigmoid(gt) for gt in gates]
            d_attn = [da_ref[:, _pair_cols(g, p)] for p in range(4)]
            d_ot = _heads_t([d_attn[p] * (gates[p] * sig[p]) for p in range(4)], g).astype(BF16)
            d_prob = _dot(vv_b, d_ot)
            delta = jnp.sum(prob * d_prob, axis=0, keepdims=True)
            d_scores = (prob * (d_prob - delta)).astype(BF16)
            dsink_acc[g] -= sink_prob * delta
            d_qt = _dot(kkt_b, d_scores)
            dkk = dkk + _dot_nt(d_scores, qt)
            dvv = dvv + _dot_nt(prob_b, d_ot)
            for p in range(4):
                dz_ref[:, _pair_cols(g, p, SEG_Q)] = (_pair_block(d_qt, p, g) * ATTN_SCALE).astype(BF16)
                d_silu = sig[p] * (1.0 + gates[p] * (1.0 - sig[p]))
                dz_ref[:, _pair_cols(g, p, SEG_GA)] = (d_attn[p] * _pair_block(ot, p, g) * d_silu).astype(BF16)
        d_kv = jnp.concatenate([dkk, dvv], axis=1)
        dz_ref[:, SEG_KV:SEG_KV + 2 * D_KV] = (d_kv[BLOCK:] + carry_ref[...]).astype(BF16)
        carry_ref[...] = d_kv[:BLOCK]

        vhat, rstd = _layer_norm_fwd(z_ref[:, SEG_VS:SEG_VS + D_SGU])
        lng = lng_ref[...]
        vn = vhat * lng + lnb_ref[...]
        tril, triu = _tril(), _tril(transposed=True)
        lane = lax.broadcasted_iota(jnp.int32, (BLOCK, 128), 1)
        d_bt = jnp.zeros((BLOCK, 128), F32)
        d_vn = []
        for g in range(SGU_GROUPS):
            cols = slice(g * 128, (g + 1) * 128)
            wm = jnp.where(tril, w_ref[g], 0.0).astype(BF16)
            wmt = jnp.where(triu, wt_ref[g], 0.0).astype(BF16)
            vn_g = vn[:, cols].astype(BF16)
            mixed = _dot(wm, vn_g) + bt_ref[:, g:g + 1]
            gate = z_ref[:, SEG_GS + g * 128:SEG_GS + (g + 1) * 128]
            u = z_ref[:, SEG_U + g * 128:SEG_U + (g + 1) * 128]
            d_out = da_ref[:, D_ATTN + g * 128:D_ATTN + (g + 1) * 128]
            sg = _sigmoid(gate)
            d_um = d_out * (gate * sg)
            dz_ref[:, SEG_U + g * 128:SEG_U + (g + 1) * 128] = (d_um * mixed).astype(BF16)
            dz_ref[:, SEG_GS + g * 128:SEG_GS + (g + 1) * 128] = (
                d_out * (u * mixed) * (sg * (1.0 + gate * (1.0 - sg)))).astype(BF16)
            d_mixed = d_um * u
            d_mixed_b = d_mixed.astype(BF16)
            dw_ref[g] += jnp.where(tril, _dot_nt(d_mixed_b, vn_g), 0.0)
            d_bt = d_bt + jnp.where(lane == g, jnp.sum(d_mixed, axis=-1, keepdims=True), 0.0)
            d_vn.append(_dot(wmt, d_mixed_b))
        dbt_acc[...] += d_bt
        d_vn = jnp.concatenate(d_vn, axis=1)
        dlng_ref[...] += jnp.sum(d_vn * vhat, axis=0, keepdims=True)
        dlnb_ref[...] += jnp.sum(d_vn, axis=0, keepdims=True)
        d_vhat = d_vn * lng
        d_v = rstd * (d_vhat - jnp.mean(d_vhat, axis=-1, keepdims=True)
                      - vhat * jnp.mean(d_vhat * vhat, axis=-1, keepdims=True))
        dz_ref[:, SEG_VS:SEG_VS + D_SGU] = d_v.astype(BF16)

        @pl.when(step == nb - 1)
        def _():
            db_ref[...] = dbt_acc[...].T[:SGU_GROUPS]
            lane_row = lax.broadcasted_iota(jnp.int32, (1, 128), 1)
            d_sink = jnp.zeros((1, 128), F32)
            for g in range(2):
                acc = dsink_acc[g]
                for j in range(8):
                    head_sum = jnp.sum(acc[:, j * BLOCK:(j + 1) * BLOCK], axis=-1, keepdims=True)
                    d_sink = d_sink + jnp.where(lane_row == 8 * g + j, head_sum, 0.0)
            dsink_ref[...] = d_sink

    rev = lambda i: nb - 1 - i
    return pl.pallas_call(
        body, name="mixer_bwd", grid=(nb,),
        in_specs=[pl.BlockSpec((BLOCK, D_IN), lambda i: (rev(i), 0)), _kv_prev_spec(rev),
                  pl.BlockSpec((BLOCK, D_MODEL), lambda i: (rev(i), 0)),
                  _const_spec((2, 1, 8 * BLOCK)), _const_spec((1, D_SGU)), _const_spec((1, D_SGU)),
                  _const_spec((SGU_GROUPS, BLOCK, BLOCK)), _const_spec((SGU_GROUPS, BLOCK, BLOCK)),
                  _const_spec((BLOCK, SGU_GROUPS))],
        out_specs=(pl.BlockSpec((BLOCK, D_IN), lambda i: (rev(i), 0)), _const_spec((1, 128)),
                   _const_spec((SGU_GROUPS, BLOCK, BLOCK)), _const_spec((SGU_GROUPS, BLOCK)),
                   _const_spec((1, D_SGU)), _const_spec((1, D_SGU))),
        out_shape=(jax.ShapeDtypeStruct((t, D_IN), BF16), jax.ShapeDtypeStruct((1, 128), F32),
                   jax.ShapeDtypeStruct((SGU_GROUPS, BLOCK, BLOCK), F32), jax.ShapeDtypeStruct((SGU_GROUPS, BLOCK), F32),
                   jax.ShapeDtypeStruct((1, D_SGU), F32), jax.ShapeDtypeStruct((1, D_SGU), F32)),
        scratch_shapes=[pltpu.VMEM((BLOCK, 2 * D_KV), F32), pltpu.VMEM((2, 1, 8 * BLOCK), F32),
                        pltpu.VMEM((BLOCK, 128), F32)],
        compiler_params=_params(dimension_semantics=("arbitrary",)),
    )(z, z, da, sink_rows, ln_g, ln_b, sgu_w, sgu_wt, sgu_bt)


def _out_proj_head(a, w_out_full, x, target, gate, final_g, tm=256):
    t, d = x.shape

    def body(a_ref, w_ref, x_ref, tg_ref, gate_ref, fg_ref, dx2_ref, dy_ref, loss_ref, dfg_ref, dgate_ref):
        @pl.when(pl.program_id(0) == 0)
        def _():
            loss_ref[...] = jnp.zeros_like(loss_ref)
            dfg_ref[...] = jnp.zeros_like(dfg_ref)
            dgate_ref[...] = jnp.zeros_like(dgate_ref)

        yv, gate, fg = _dot(a_ref[...], w_ref[...]), gate_ref[...], fg_ref[...]
        x2 = x_ref[...] + gate * yv
        r2 = lax.rsqrt(jnp.mean(x2 * x2, axis=-1, keepdims=True) + EPS)
        nrm = x2 * r2
        err = nrm * fg - tg_ref[...]
        loss_ref[...] += 0.5 * jnp.sum(jnp.mean(err * err, axis=-1, keepdims=True), axis=0, keepdims=True)
        d_out = err * (1.0 / d)
        dfg_ref[...] += jnp.sum(d_out * nrm, axis=0, keepdims=True)
        d_nrm = d_out * fg
        dx2 = r2 * (d_nrm - nrm * jnp.mean(d_nrm * nrm, axis=-1, keepdims=True))
        dx2_ref[...] = dx2
        dgate_ref[...] += jnp.sum(dx2 * yv, axis=0, keepdims=True)
        dy_ref[...] = (dx2 * gate).astype(BF16)

    blk = pl.BlockSpec((tm, d), lambda i: (i, 0))
    row = _const_spec((1, d))
    whole = pl.BlockSpec(w_out_full.shape, lambda i: (0, 0), pipeline_mode=pl.Buffered(1))
    return pl.pallas_call(
        body, name="out_proj_head", grid=(t // tm,),
        in_specs=[pl.BlockSpec((tm, a.shape[1]), lambda i: (i, 0)), whole, blk, blk, row, row],
        out_specs=(blk, blk, _const_spec((1, 128)), row, row),
        out_shape=(jax.ShapeDtypeStruct((t, d), F32), jax.ShapeDtypeStruct((t, d), BF16),
                   jax.ShapeDtypeStruct((1, 128), F32), jax.ShapeDtypeStruct((1, d), F32),
                   jax.ShapeDtypeStruct((1, d), F32)),
        compiler_params=_params(dimension_semantics=("arbitrary",)),
    )(a, w_out_full, x, target, gate, final_g)


def _z_proj_bwd_norm(dz, w_in_t, x, dx2, norm_g, scale, dep, tm=256):
    t, d = x.shape

    def body(dz_ref, w_ref, x_ref, dx2_ref, g_ref, sc_ref, dep_ref, gx_ref, dshift_ref, dscale_ref, dg_ref):
        @pl.when(pl.program_id(0) == 0)
        def _():
            dshift_ref[...] = jnp.zeros_like(dshift_ref)
            dscale_ref[...] = jnp.zeros_like(dscale_ref)
            dg_ref[...] = jnp.zeros_like(dg_ref)

        dh, xv, g = _dot(dz_ref[...], w_ref[...]), x_ref[...], g_ref[...]
        one_plus = 1.0 + sc_ref[...]
        r = lax.rsqrt(jnp.mean(xv * xv, axis=-1, keepdims=True) + EPS)
        xn = xv * r
        dshift_ref[...] += jnp.sum(dh, axis=0, keepdims=True)
        dscale_ref[...] += jnp.sum(dh * (xn * g), axis=0, keepdims=True)
        d_y = dh * one_plus
        dg_ref[...] += jnp.sum(d_y * xn, axis=0, keepdims=True)
        d_xn = d_y * g
        gx_ref[...] = dx2_ref[...] + r * (d_xn - xn * jnp.mean(d_xn * xn, axis=-1, keepdims=True))

    blk = pl.BlockSpec((tm, d), lambda i: (i, 0))
    row = _const_spec((1, d))
    whole = pl.BlockSpec(w_in_t.shape, lambda i: (0, 0), pipeline_mode=pl.Buffered(1))
    return pl.pallas_call(
        body, name="z_proj_bwd_norm", grid=(t // tm,),
        in_specs=[pl.BlockSpec((tm, dz.shape[1]), lambda i: (i, 0)), whole, blk, blk, row, row, _const_spec((8, 128))],
        out_specs=(blk, row, row, row),
        out_shape=(jax.ShapeDtypeStruct((t, d), F32),) + (jax.ShapeDtypeStruct((1, d), F32),) * 3,
        compiler_params=_params(dimension_semantics=("arbitrary",)),
    )(dz, w_in_t, x, dx2, norm_g, scale, dep)


def _adamw(w, g, m, v):
    m = ADAM_B1 * m + (1.0 - ADAM_B1) * g
    v = ADAM_B2 * v + (1.0 - ADAM_B2) * (g * g)
    m_hat = m / (1.0 - ADAM_B1 ** ADAM_STEP)
    v_hat = v / (1.0 - ADAM_B2 ** ADAM_STEP)
    delta = -ADAM_LR * (m_hat / (jnp.sqrt(v_hat) + ADAM_EPS) + ADAM_WD * w)
    return delta, m, v


def _relay_sum(second_chip, pair, land, tr):
    _, r, c = pair.shape

    def body(chip_ref, a_ref, b_ref, o_ref):
        o_ref[...] = (a_ref[...].astype(F32) + b_ref[...].astype(F32)).astype(BF16)

    return pl.pallas_call(
        body, name="w_in_grad_relay_sum",
        grid_spec=pltpu.PrefetchScalarGridSpec(
            num_scalar_prefetch=1, grid=(r // tr,),
            in_specs=[pl.BlockSpec((None, tr, c), lambda i, chip_ref: (chip_ref[0], i, 0)),
                      pl.BlockSpec((None, tr, c), lambda i, chip_ref: (1, i, 0))],
            out_specs=pl.BlockSpec((tr, c), lambda i, chip_ref: (i, 0))),
        out_shape=jax.ShapeDtypeStruct((r, c), BF16),
        compiler_params=_params(dimension_semantics=("arbitrary",)),
    )(second_chip, pair, land)


def _adam_from_chips(chip, pair, landed, w, m, v, name, tc):
    _, r, c = pair.shape
    n = len(landed)

    def body(chip_ref, own_ref, *refs):
        w_ref, m_ref, v_ref, g_ref, d_ref, nm_ref, nv_ref = refs[n:]
        g = own_ref[...].astype(F32)
        for k in range(n):
            g = g + refs[k][...].astype(F32)
        g_ref[...] = g
        d_ref[...], nm_ref[...], nv_ref[...] = _adamw(w_ref[...], g, m_ref[...], v_ref[...])

    def landed_spec(index):
        return pl.BlockSpec((None, r, tc), lambda i, chip_ref: (index, 0, i))

    blk = pl.BlockSpec((r, tc), lambda i, chip_ref: (0, i))
    return pl.pallas_call(
        body, name=name,
        grid_spec=pltpu.PrefetchScalarGridSpec(
            num_scalar_prefetch=1, grid=(c // tc,),
            in_specs=[pl.BlockSpec((None, r, tc), lambda i, chip_ref: (chip_ref[0], 0, i))]
            + [landed_spec(index) for _, index in landed] + [blk, blk, blk],
            out_specs=(blk,) * 4),
        out_shape=(jax.ShapeDtypeStruct((r, c), F32),) * 4,
        compiler_params=_params(dimension_semantics=("arbitrary",)),
    )(chip, pair, *[array for array, _ in landed], w, m, v)


def _adam_w_ada(act_t, dmod_mine, w, m, v, tr=256):
    r, c = w.shape

    def body(a_ref, dm_ref, w_ref, m_ref, v_ref, g_ref, d_ref, nm_ref, nv_ref):
        g = _dot(a_ref[...].astype(BF16), dm_ref[...].astype(BF16))
        g_ref[...] = g
        d_ref[...], nm_ref[...], nv_ref[...] = _adamw(w_ref[...], g, m_ref[...], v_ref[...])

    blk = pl.BlockSpec((tr, c), lambda i: (i, 0))
    return pl.pallas_call(
        body, name="adam_w_ada", grid=(r // tr,),
        in_specs=[pl.BlockSpec((tr, N_DEV), lambda i: (i, 0)), _const_spec((N_DEV, c)), blk, blk, blk],
        out_specs=(blk,) * 4, out_shape=(jax.ShapeDtypeStruct((r, c), F32),) * 4,
        compiler_params=_params(dimension_semantics=("arbitrary",)),
    )(act_t, dmod_mine, w, m, v)


def _pack_small(d_shift, d_scale, d_gate, d_norm_g, d_final_g, d_ln_g, d_ln_b, loss, d_sinks, d_sgu_b):
    def body(shift_ref, scale_ref, gate_ref, ng_ref, fg_ref, lng_ref, lnb_ref, loss_ref, sink_ref, b_ref, o_ref):
        o_ref[...] = jnp.zeros_like(o_ref)
        o_ref[ROW_SHIFT:ROW_SHIFT + 1, :] = shift_ref[...]
        o_ref[ROW_SCALE:ROW_SCALE + 1, :] = scale_ref[...]
        o_ref[ROW_GATE:ROW_GATE + 1, :] = gate_ref[...]
        o_ref[ROW_NORM_G:ROW_NORM_G + 1, :] = ng_ref[...]
        o_ref[ROW_FINAL_G:ROW_FINAL_G + 1, :] = fg_ref[...]
        o_ref[ROW_LN:ROW_LN + 1, 0:D_SGU] = lng_ref[...]
        o_ref[ROW_LN:ROW_LN + 1, D_SGU:2 * D_SGU] = lnb_ref[...]
        o_ref[ROW_MISC:ROW_MISC + 1, 0:128] = loss_ref[...]
        o_ref[ROW_MISC:ROW_MISC + 1, 128:256] = sink_ref[...]
        o_ref[ROW_SGU_B:ROW_SGU_B + SGU_GROUPS, 0:BLOCK] = b_ref[...]

    return pl.pallas_call(
        body, name="pack_small", out_shape=jax.ShapeDtypeStruct((SMALL_ROWS, D_MODEL), F32),
        compiler_params=_params(),
    )(d_shift, d_scale, d_gate, d_norm_g, d_final_g, d_ln_g, d_ln_b, loss, d_sinks, d_sgu_b)


_SMALL_NAMES = ("norm_g", "b_ada", "attn_sinks", "sgu_ln_g", "sgu_ln_b", "sgu_w", "sgu_b", "final_g")


def _adam_small(partials, d_sgu_w_all, weights, moments_m, moments_v):
    names = _SMALL_NAMES
    k = len(names)

    def body(*refs):
        p_ref, sw_ref = refs[0], refs[1]
        w_refs, m_refs, v_refs = refs[2:2 + k], refs[2 + k:2 + 2 * k], refs[2 + 2 * k:2 + 3 * k]
        loss_ref, dmod_ref = refs[2 + 3 * k], refs[3 + 3 * k]
        out_refs = refs[4 + 3 * k:4 + 7 * k]
        sum_ref = refs[4 + 7 * k]
        total = p_ref[0]
        for j in range(1, N_DEV):
            total = total + p_ref[j]
        sum_ref[...] = total
        for j in range(N_DEV):
            for part, row in enumerate((ROW_SHIFT, ROW_SCALE, ROW_GATE)):
                dmod_ref[j:j + 1, part * D_MODEL:(part + 1) * D_MODEL] = p_ref[j, row:row + 1, :]
        loss_ref[...] = sum_ref[ROW_MISC:ROW_MISC + 1, 0:1]
        d_sgu_w = sw_ref[0]
        for j in range(1, N_DEV):
            d_sgu_w = d_sgu_w + sw_ref[j]
        grads = {
            "norm_g": sum_ref[ROW_NORM_G:ROW_NORM_G + 1, :],
            "b_ada": jnp.concatenate([sum_ref[r:r + 1, :] for r in (ROW_SHIFT, ROW_SCALE, ROW_GATE)], axis=1),
            "attn_sinks": sum_ref[ROW_MISC:ROW_MISC + 1, 128:128 + N_Q_HEADS],
            "sgu_ln_g": sum_ref[ROW_LN:ROW_LN + 1, 0:D_SGU],
            "sgu_ln_b": sum_ref[ROW_LN:ROW_LN + 1, D_SGU:2 * D_SGU],
            "sgu_w": d_sgu_w[None],
            "sgu_b": sum_ref[ROW_SGU_B:ROW_SGU_B + SGU_GROUPS, 0:BLOCK][None],
            "final_g": sum_ref[ROW_FINAL_G:ROW_FINAL_G + 1, :],
        }
        for i, name in enumerate(names):
            g = grads[name]
            delta, m, v = _adamw(w_refs[i][...], g, m_refs[i][...], v_refs[i][...])
            out_refs[4 * i][...] = g
            out_refs[4 * i + 1][...] = delta
            out_refs[4 * i + 2][...] = m
            out_refs[4 * i + 3][...] = v

    shapes = [jax.ShapeDtypeStruct((1, 1), F32), jax.ShapeDtypeStruct((N_DEV, 3 * D_MODEL), F32)]
    for name in names:
        shapes += [jax.ShapeDtypeStruct(weights[name].shape, F32)] * 4
    outs = pl.pallas_call(
        body, name="adam_small", out_shape=tuple(shapes),
        scratch_shapes=[pltpu.VMEM((SMALL_ROWS, D_MODEL), F32)],
        compiler_params=_params(),
    )(partials, d_sgu_w_all, *[weights[n] for n in names], *[moments_m[n] for n in names],
      *[moments_v[n] for n in names])
    return outs[0], outs[1], {name: outs[2 + 4 * i:6 + 4 * i] for i, name in enumerate(names)}


def kernel(x, c, norm_g, w_ada, b_ada, w_in, attn_sinks, sgu_ln_g, sgu_ln_b, sgu_w, sgu_b, w_out, final_g, loss_target, m_norm_g, m_w_ada, m_b_ada, m_w_in, m_attn_sinks, m_sgu_ln_g, m_sgu_ln_b, m_sgu_w, m_sgu_b, m_w_out, m_final_g, v_norm_g, v_w_ada, v_b_ada, v_w_in, v_attn_sinks, v_sgu_ln_g, v_sgu_ln_b, v_sgu_w, v_sgu_b, v_w_out, v_final_g):
    xi, yi, ci = _place()
    me = 4 * xi + 2 * yi + ci
    x2d, target = x[0], loss_target[0]
    t = x2d.shape[0]

    core = ci.astype(jnp.int32).reshape(1)
    chip = (2 * xi + yi).astype(jnp.int32).reshape(1)

    first, second = _own_block_copies(_first_targets), _own_block_copies(_second_targets)
    first_flight = _start_copies([_with_own_slot(w_in[0].T.astype(BF16), me)], first, 2, core, "gather_w_in_start")

    c_all = _all_gather_small(c.reshape(8, 256) + first_flight[3][0, 0], "gather_c").reshape(N_DEV, D_MODEL)
    b_mine = lax.dynamic_slice(b_ada, (0, me * W_ADA_SHARD), (1, W_ADA_SHARD))
    c_act, mod_part = _modulation(c_all, w_ada[0], b_mine)
    mod_all = _all_gather_small(mod_part, "gather_mod")

    second_flight = _start_copies(first_flight[2], second, 1, mod_all, "gather_w_in_second_axis_start")
    mod = lax.dynamic_index_in_dim(mod_all, me, axis=1, keepdims=False).reshape(1, 3 * D_MODEL)
    mod = mod + second_flight[3][0, 0]
    shift, scale, gate = mod[:, :D_MODEL], mod[:, D_MODEL:2 * D_MODEL], mod[:, 2 * D_MODEL:]
    h = _modulated_norm(x2d, norm_g, scale, shift)

    w_in_pair = _wait_copies((first_flight[0], first_flight[1], second_flight[2], None), lambda *a: first(*a)[:1], h,
                             "gather_w_in_sibling_wait")
    tile_order = jnp.asarray(_Z_TILE_ORDER, jnp.int32)[chip[0]]
    z_own = _z_proj(h, w_in_pair[0].reshape(D_IN, D_MODEL), tile_order, 0, 1, None, "z_proj_own")
    w_in_pair = _wait_copies((second_flight[0], second_flight[1], w_in_pair, None), second, z_own,
                             "gather_w_in_second_axis_wait")
    w_in_flight = _wait_then_start((first_flight[0], first_flight[1], w_in_pair, None), lambda *a: first(*a)[1:],
                                   _second_stage_copies, 3, z_own, "gather_w_in_second_stage")
    w_in_most = _wait_copies(w_in_flight, lambda *a: _second_stage_copies(*a)[:2], z_own, "gather_w_in_forward_wait")
    z_early = _z_proj(h, w_in_most[0].reshape(D_IN, D_MODEL), tile_order, 1, _Z_EARLY_TILES - 1, z_own, "z_proj_early")
    w_out_flight = _start_copies([_with_own_slot(w_out[0].astype(BF16), me)], _own_block_copies(_my_core_and_sibling),
                                 4, z_early, "gather_w_out_start")
    w_in_flight = _wait_then_start((w_in_flight[0], w_in_flight[1], w_in_most, None),
                                   lambda *a: _second_stage_copies(*a)[2:], _diagonal_forward_copies, 1,
                                   w_out_flight[3], "gather_w_in_last_stage")
    w_in_all = _wait_copies(w_in_flight, _diagonal_forward_copies, z_early, "gather_w_in_last_wait")[0]
    w_in_t = w_in_all.reshape(D_IN, D_MODEL)
    z = _z_proj(h, w_in_t, tile_order, _Z_EARLY_TILES, 7 - _Z_EARLY_TILES, z_early, "z_proj_late")
    w_out_flight = _wait_then_start(w_out_flight, _own_block_copies(_my_core_and_sibling), _forward_copies, 3, z,
                                    "gather_w_out_forward_stage")
    sink_rows = jnp.repeat(attn_sinks.reshape(N_Q_HEADS), BLOCK).reshape(2, 1, 8 * BLOCK)
    sgu_bt = sgu_b[0].T
    a = _mixer_fwd(z, sink_rows + w_out_flight[3][0, 0], sgu_ln_g, sgu_ln_b, sgu_w[0], sgu_bt)
    w_out_all = _wait_copies(w_out_flight, _forward_copies, a, "gather_w_out_forward_wait")[0]
    w_out_full = w_out_all.reshape(D_MODEL, D_MODEL)
    final_g_row = final_g.reshape(1, D_MODEL)
    dx2, dy, loss_part, d_final_g, d_gate = _out_proj_head(a, w_out_full, x2d, target, gate, final_g_row)

    da = _matmul(dy, w_out_full, "nt", F32, min(t, 1024), 1024, "out_proj_bwd")
    dw_out = _matmul(a, dy, "tn", BF16, 1024, 1024, "w_out_grad").reshape(4, 2, W_OUT_SHARD, D_MODEL)
    pair_out = _pair_reduce(dw_out, "w_out_grad_pair_reduce", W_OUT_SHARD // 2)
    out_flight = _start_copies([pair_out, lax.empty((3, W_OUT_SHARD, D_MODEL), BF16)], _chip_copies, 3, core,
                               "w_out_grad_chip_start")
    dz, d_sinks, d_sgu_w, d_sgu_b, d_ln_g, d_ln_b = _mixer_bwd(
        z, da, sink_rows + out_flight[3][0, 0], sgu_ln_g, sgu_ln_b, sgu_w[0], jnp.swapaxes(sgu_w[0], 1, 2), sgu_bt)
    sgu_w_flight = _start_copies([_with_own_slot(d_sgu_w, me)], _own_block_copies(_all_others), N_DEV - 1, core,
                                 "sgu_w_grad_gather_start")
    dw_in_t = _matmul(dz, h, "tn", BF16, 768, D_MODEL, "w_in_grad", dep=sgu_w_flight[3])
    dw_in_t = dw_in_t.reshape(4, 2, W_IN_SHARD, D_MODEL)
    pair_in = _pair_reduce(dw_in_t, "w_in_grad_pair_reduce", W_IN_SHARD // 3)
    hop1 = _start_copies([pair_in, lax.empty((2, W_IN_SHARD, D_MODEL), BF16)], _first_hop_copies, 2, core,
                         "w_in_grad_first_hop_start")
    grad_x, d_shift, d_scale, d_norm_g = _z_proj_bwd_norm(dz, w_in_t, x2d, dx2, norm_g, scale, hop1[3])

    partial = _pack_small(d_shift, d_scale, d_gate, d_norm_g, d_final_g, d_ln_g, d_ln_b, loss_part, d_sinks, d_sgu_b)
    small_flight = _start_copies([_with_own_slot(partial, me)], _own_block_copies(_all_others), N_DEV - 1, core,
                                 "small_grad_gather_start")
    pair_in, land_first = _wait_copies(hop1, _first_hop_copies, small_flight[3], "w_in_grad_first_hop_wait")
    second_chip = (2 * ((xi + ci) % 2) + (yi + 1 - ci) % 2).astype(jnp.int32).reshape(1)
    relay = _relay_sum(second_chip, pair_in, land_first, W_IN_SHARD // 3)
    hop2 = _start_copies([relay, lax.empty((1, W_IN_SHARD, D_MODEL), BF16)], _second_hop_copies, 1, core,
                         "w_in_grad_second_hop_start")
    pair_out, land_out = _wait_copies(out_flight, _chip_copies, hop2[3], "w_out_grad_chip_wait")
    big = {"w_out": _adam_from_chips(chip, pair_out, [(land_out, k) for k in range(3)], w_out[0], m_w_out[0],
                                     v_w_out[0], "adam_w_out", 1024)}
    partial_all = _wait_copies(small_flight, _own_block_copies(_all_others), big["w_out"][0],
                               "small_grad_gather_wait")[0]
    d_sgu_w_all = _wait_copies(sgu_w_flight, _own_block_copies(_all_others), partial_all, "sgu_w_grad_gather_wait")[0]
    weights = {"norm_g": norm_g, "b_ada": b_ada, "attn_sinks": attn_sinks, "sgu_ln_g": sgu_ln_g,
               "sgu_ln_b": sgu_ln_b, "sgu_w": sgu_w, "sgu_b": sgu_b, "final_g": final_g_row}
    moments_m = {"norm_g": m_norm_g, "b_ada": m_b_ada, "attn_sinks": m_attn_sinks, "sgu_ln_g": m_sgu_ln_g,
                 "sgu_ln_b": m_sgu_ln_b, "sgu_w": m_sgu_w, "sgu_b": m_sgu_b,
                 "final_g": m_final_g.reshape(1, D_MODEL)}
    moments_v = {"norm_g": v_norm_g, "b_ada": v_b_ada, "attn_sinks": v_attn_sinks, "sgu_ln_g": v_sgu_ln_g,
                 "sgu_ln_b": v_sgu_ln_b, "sgu_w": v_sgu_w, "sgu_b": v_sgu_b,
                 "final_g": v_final_g.reshape(1, D_MODEL)}
    loss, dmod_all, small = _adam_small(partial_all, d_sgu_w_all, weights, moments_m, moments_v)
    small["final_g"] = tuple(o.reshape(D_MODEL) for o in small["final_g"])

    dmod_mine = lax.dynamic_slice(dmod_all, (0, me * W_ADA_SHARD), (N_DEV, W_ADA_SHARD))
    big["w_ada"] = _adam_w_ada(c_act.T, dmod_mine, w_ada[0], m_w_ada[0], v_w_ada[0])
    _, land_second = _wait_copies(hop2, _second_hop_copies, big["w_ada"][0], "w_in_grad_second_hop_wait")
    big["w_in"] = tuple(o.T for o in _adam_from_chips(
        chip, pair_in, [(land_first, 0), (land_second, 0)], w_in[0].T, m_w_in[0].T, v_w_in[0].T, "adam_w_in", 256))
    order = ["norm_g", "w_ada", "b_ada", "w_in", "attn_sinks", "sgu_ln_g", "sgu_ln_b", "sgu_w", "sgu_b", "w_out",
             "final_g"]
    outs = [loss.reshape(()), grad_x[None]]
    for k in range(4):
        for name in order:
            outs.append(big[name][k][None] if name in big else small[name][k])
    return tuple(outs)
```

```python
import jax
import jax.numpy as jnp
from jax import lax
from jax.experimental import pallas as pl
from jax.experimental.pallas import tpu as pltpu

F32 = jnp.float32
BF16 = jnp.bfloat16
MESH = pl.DeviceIdType.MESH

N_DEV = 8
D_MODEL = 2048
HEAD_DIM = 64
D_ATTN = 1024
N_Q_HEADS = 16
D_KV = 128
BLOCK = 128
D_SGU = 1024
SGU_GROUPS = 8
D_IN = 5376
W_IN_SHARD = D_IN // N_DEV
W_OUT_SHARD = D_MODEL // N_DEV
W_ADA_SHARD = 3 * D_MODEL // N_DEV
EPS = 1e-6
ATTN_SCALE = 0.125

ADAM_LR = 0.001
ADAM_B1 = 0.9
ADAM_B2 = 0.999
ADAM_EPS = 1e-08
ADAM_WD = 0.01
ADAM_STEP = 10

SEG_Q, SEG_KV, SEG_GA, SEG_U, SEG_VS, SEG_GS = 0, 1024, 1280, 2304, 3328, 4352

VMEM_LIMIT = 56 * 1024 * 1024

ROW_SHIFT, ROW_SCALE, ROW_GATE, ROW_NORM_G, ROW_FINAL_G, ROW_LN, ROW_MISC, ROW_SGU_B = 0, 1, 2, 3, 4, 5, 6, 8
SMALL_ROWS = 16


def _params(**kw):
    return pltpu.CompilerParams(vmem_limit_bytes=VMEM_LIMIT, **kw)


def _sigmoid(x):
    return 0.5 * (jnp.tanh(0.5 * x) + 1.0)


def _place():
    return lax.axis_index("x"), lax.axis_index("y"), lax.axis_index("c")


def _pair_reduce(blocks, name, row_chunk):
    _, _, r, cols = blocks.shape
    assert r % row_chunk == 0

    def body(in_ref, out_ref, land, own, summed, send_sems, recv_sems, own_sems, out_sems):
        x, y, c = _place()
        sends, loads, stores = [], [], []
        for m in range(4):
            cp = pltpu.make_async_remote_copy(
                src_ref=in_ref.at[m, 1 - c], dst_ref=land.at[m], send_sem=send_sems.at[m], recv_sem=recv_sems.at[m],
                device_id=(x, y, 1 - c), device_id_type=MESH)
            cp.start()
            sends.append(cp)
            ld = pltpu.make_async_copy(in_ref.at[m, c], own.at[m], own_sems.at[m])
            ld.start()
            loads.append(ld)
        for m in range(4):
            sends[m].wait_recv()
            loads[m].wait()
            for k in range(r // row_chunk):
                rows = slice(k * row_chunk, (k + 1) * row_chunk)
                summed[m, rows, :] = (own[m, rows, :].astype(F32) + land[m, rows, :].astype(F32)).astype(BF16)
            st = pltpu.make_async_copy(summed.at[m], out_ref.at[m], out_sems.at[m])
            st.start()
            stores.append(st)
        for m in range(4):
            sends[m].wait_send()
            stores[m].wait()

    spec = pl.BlockSpec(memory_space=pl.ANY)
    return pl.pallas_call(
        body, name=name, out_shape=jax.ShapeDtypeStruct((4, r, cols), BF16),
        in_specs=[spec], out_specs=spec,
        scratch_shapes=[pltpu.VMEM((4, r, cols), BF16), pltpu.VMEM((4, r, cols), BF16), pltpu.VMEM((4, r, cols), BF16),
                        pltpu.SemaphoreType.DMA((4,)), pltpu.SemaphoreType.DMA((4,)), pltpu.SemaphoreType.DMA((4,)),
                        pltpu.SemaphoreType.DMA((4,))],
        compiler_params=_params(),
    )(blocks)


_HBM = pl.BlockSpec(memory_space=pltpu.HBM)
_SEM = pl.BlockSpec(memory_space=pltpu.SEMAPHORE)
_EFFECT = pltpu.SideEffectType.DATAFLOW_SIDE_EFFECTING


def _start_copies(bufs, copies, n_copies, after, name):
    nb = len(bufs)

    def body(*refs):
        for cp in copies(refs[:nb], refs[nb + 1], refs[nb + 2]):
            cp.start()
        refs[-1][...] = jnp.zeros_like(refs[-1])

    out = pl.pallas_call(
        body, name=name,
        out_shape=(pltpu.SemaphoreType.DMA((n_copies,)), pltpu.SemaphoreType.DMA((n_copies,)),
                   *[pltpu.HBM(b.shape, b.dtype) for b in bufs], jax.ShapeDtypeStruct((8, 128), F32)),
        in_specs=(_HBM,) * nb + (pl.BlockSpec(memory_space=pl.ANY),),
        out_specs=(_SEM, _SEM) + (_HBM,) * nb + (pl.BlockSpec(memory_space=pltpu.VMEM),),
        input_output_aliases={i: 2 + i for i in range(nb)},
        compiler_params=pltpu.CompilerParams(has_side_effects=_EFFECT),
    )(*[pltpu.with_memory_space_constraint(b, pltpu.HBM) for b in bufs], after)
    return out[0], out[1], list(out[2:2 + nb]), out[-1]


def _wait_copies(flight, copies, after, name):
    send_sems, recv_sems, bufs, _ = flight
    nb = len(bufs)

    def body(*refs):
        for cp in copies(refs[:nb], refs[nb], refs[nb + 1]):
            cp.wait_send()
            cp.wait_recv()

    return pl.pallas_call(
        body, name=name,
        out_shape=tuple(pltpu.HBM(b.shape, b.dtype) for b in bufs),
        in_specs=(_HBM,) * nb + (_SEM, _SEM, pl.BlockSpec(memory_space=pl.ANY)), out_specs=(_HBM,) * nb,
        input_output_aliases={i: i for i in range(nb)},
        compiler_params=pltpu.CompilerParams(has_side_effects=_EFFECT),
    )(*bufs, send_sems, recv_sems, after)


def _wait_then_start(flight, waited, started, n_started, after, name):
    old_send, old_recv, bufs, _ = flight
    nb = len(bufs)

    def body(*refs):
        for cp in waited(refs[:nb], refs[nb], refs[nb + 1]):
            cp.wait_send()
            cp.wait_recv()
        for cp in started(refs[:nb], refs[nb + 3], refs[nb + 4]):
            cp.start()
        refs[-1][...] = jnp.zeros_like(refs[-1])

    out = pl.pallas_call(
        body, name=name,
        out_shape=(pltpu.SemaphoreType.DMA((n_started,)), pltpu.SemaphoreType.DMA((n_started,)),
                   *[pltpu.HBM(b.shape, b.dtype) for b in bufs], jax.ShapeDtypeStruct((8, 128), F32)),
        in_specs=(_HBM,) * nb + (_SEM, _SEM, pl.BlockSpec(memory_space=pl.ANY)),
        out_specs=(_SEM, _SEM) + (_HBM,) * nb + (pl.BlockSpec(memory_space=pltpu.VMEM),),
        input_output_aliases={i: 2 + i for i in range(nb)},
        compiler_params=pltpu.CompilerParams(has_side_effects=_EFFECT),
    )(*bufs, old_send, old_recv, after)
    return out[0], out[1], list(out[2:2 + nb]), out[-1]


def _chip_copies(refs, send_sems, recv_sems):
    pair_ref, land_ref = refs
    x, y, c = _place()
    chips = [(1 - x, y), (x, 1 - y), (1 - x, 1 - y)]
    return [pltpu.make_async_remote_copy(
        src_ref=pair_ref.at[2 * chip[0] + chip[1]], dst_ref=land_ref.at[k],
        send_sem=send_sems.at[k], recv_sem=recv_sems.at[k],
        device_id=(*chip, c), device_id_type=MESH) for k, chip in enumerate(chips)]


def _first_hop_copies(refs, send_sems, recv_sems):
    pair_ref, land_ref = refs
    x, y, c = _place()
    first = ((x + 1 - c) % 2, (y + c) % 2)
    blocks = [2 * first[0] + first[1], 2 * (1 - x) + (1 - y)]
    return [pltpu.make_async_remote_copy(
        src_ref=pair_ref.at[blocks[k]], dst_ref=land_ref.at[k], send_sem=send_sems.at[k], recv_sem=recv_sems.at[k],
        device_id=(*first, c), device_id_type=MESH) for k in range(2)]


def _second_hop_copies(refs, send_sems, recv_sems):
    relay_ref, land_ref = refs
    x, y, c = _place()
    second = ((x + c) % 2, (y + 1 - c) % 2)
    return [pltpu.make_async_remote_copy(
        src_ref=relay_ref, dst_ref=land_ref.at[0], send_sem=send_sems.at[0], recv_sem=recv_sems.at[0],
        device_id=(*second, c), device_id_type=MESH)]


def _own_block_copies(targets):
    def copies(refs, send_sems, recv_sems):
        x, y, c = _place()
        mine = refs[0].at[4 * x + 2 * y + c]
        return [pltpu.make_async_remote_copy(
            src_ref=mine, dst_ref=mine, send_sem=send_sems.at[k], recv_sem=recv_sems.at[k],
            device_id=to, device_id_type=MESH) for k, to in enumerate(targets(x, y, c))]
    return copies


def _my_core_and_sibling(x, y, c):
    return [(x, y, 1 - c), (1 - x, y, c), (x, 1 - y, c), (1 - x, 1 - y, c)]


def _all_others(x, y, c):
    flip = lambda v, f: 1 - v if f else v
    return [(flip(x, r & 4), flip(y, r & 2), flip(c, r & 1)) for r in range(1, N_DEV)]


def _forward_copies(refs, send_sems, recv_sems):
    x, y, c = _place()
    chips = [(1 - x, y), (x, 1 - y), (1 - x, 1 - y)]
    return [pltpu.make_async_remote_copy(
        src_ref=refs[0].at[4 * chip[0] + 2 * chip[1] + c], dst_ref=refs[0].at[4 * chip[0] + 2 * chip[1] + c],
        send_sem=send_sems.at[k], recv_sem=recv_sems.at[k],
        device_id=(x, y, 1 - c), device_id_type=MESH) for k, chip in enumerate(chips)]


def _first_axis_chip(x, y, c):
    return (x + 1 - c) % 2, (y + c) % 2


def _second_axis_chip(x, y, c):
    return (x + c) % 2, (y + 1 - c) % 2


def _first_targets(x, y, c):
    return [(x, y, 1 - c), (*_first_axis_chip(x, y, c), c)]


def _all_gather_small(shard, name):
    def body(in_ref, out_ref, send_sems, recv_sems, local_sem):
        x, y, c = _place()
        me, sibling = 4 * x + 2 * y + c, (x, y, 1 - c)
        first, second = _first_axis_chip(x, y, c), _second_axis_chip(x, y, c)

        def pair(chip):
            return out_ref.at[pl.ds(2 * (2 * chip[0] + chip[1]), 2)]

        def exchange(k, src, dst, to):
            cp = pltpu.make_async_remote_copy(src_ref=src, dst_ref=dst, send_sem=send_sems.at[k],
                                              recv_sem=recv_sems.at[k], device_id=to, device_id_type=MESH)
            cp.start()
            cp.wait()

        own = pltpu.make_async_copy(in_ref, out_ref.at[me], local_sem)
        own.start()
        exchange(0, in_ref, out_ref.at[me], sibling)
        own.wait()
        exchange(1, pair((x, y)), pair((x, y)), (*second, c))
        exchange(2, pair(second), pair(second), sibling)
        exchange(3, pair(first), pair(first), (*second, c))

    spec = pl.BlockSpec(memory_space=pltpu.VMEM)
    return pl.pallas_call(
        body, name=name, out_shape=jax.ShapeDtypeStruct((N_DEV,) + shard.shape, shard.dtype),
        in_specs=[spec], out_specs=spec,
        scratch_shapes=[pltpu.SemaphoreType.DMA((4,)), pltpu.SemaphoreType.DMA((4,)), pltpu.SemaphoreType.DMA],
        compiler_params=_params(),
    )(shard)


def _slot_copies(refs, send_sems, recv_sems, plan):
    copies = []
    for k, ((px, py, pc), to) in enumerate(plan):
        blk = refs[0].at[4 * px + 2 * py + pc]
        copies.append(pltpu.make_async_remote_copy(
            src_ref=blk, dst_ref=blk, send_sem=send_sems.at[k], recv_sem=recv_sems.at[k],
            device_id=to, device_id_type=MESH))
    return copies


def _second_axis_stage_copies(refs, send_sems, recv_sems):
    x, y, c = _place()
    first, second = (*_first_axis_chip(x, y, c), c), (*_second_axis_chip(x, y, c), c)
    return _slot_copies(refs, send_sems, recv_sems, [((x, y, c), second), (first, (x, y, 1 - c)), (first, second)])


def _second_axis_forward_copies(refs, send_sems, recv_sems):
    x, y, c = _place()
    return _slot_copies(refs, send_sems, recv_sems, [((*_second_axis_chip(x, y, c), c), (x, y, 1 - c))])


def _diagonal_forward_copies(refs, send_sems, recv_sems):
    x, y, c = _place()
    blk = refs[0].at[4 * (1 - x) + 2 * (1 - y) + c]
    return [pltpu.make_async_remote_copy(
        src_ref=blk, dst_ref=blk, send_sem=send_sems.at[0], recv_sem=recv_sems.at[0],
        device_id=(x, y, 1 - c), device_id_type=MESH)]


def _with_own_slot(block, me):
    return lax.dynamic_update_index_in_dim(lax.empty((N_DEV,) + block.shape, block.dtype), block, me, 0)


def _matmul(a, b, dims, out_dtype, tm, tn, name, dep=None):
    if dims == "nn":
        (m, k), n = a.shape, b.shape[1]
        a_spec = pl.BlockSpec((tm, k), lambda i, j: (i, 0))
        b_spec = pl.BlockSpec((k, tn), lambda i, j: (0, j))
        contract = ((1,), (0,))
    elif dims == "nt":
        (m, k), n = a.shape, b.shape[0]
        a_spec = pl.BlockSpec((tm, k), lambda i, j: (i, 0))
        b_spec = pl.BlockSpec((tn, k), lambda i, j: (j, 0))
        contract = ((1,), (1,))
    else:
        (k, m), n = a.shape, b.shape[1]
        a_spec = pl.BlockSpec((k, tm), lambda i, j: (0, i))
        b_spec = pl.BlockSpec((k, tn), lambda i, j: (0, j))
        contract = ((0,), (0,))
    assert m % tm == 0 and n % tn == 0 and a.dtype == BF16 and b.dtype == BF16

    def body(a_ref, b_ref, *rest):
        rest[-1][...] = lax.dot_general(a_ref[...], b_ref[...], (contract, ((), ())),
                                        preferred_element_type=F32).astype(out_dtype)

    deps = [] if dep is None else [dep]
    return pl.pallas_call(
        body, name=name, grid=(m // tm, n // tn),
        in_specs=[a_spec, b_spec] + [pl.BlockSpec((8, 128), lambda i, j: (0, 0))] * len(deps),
        out_specs=pl.BlockSpec((tm, tn), lambda i, j: (i, j)),
        out_shape=jax.ShapeDtypeStruct((m, n), out_dtype),
        compiler_params=_params(dimension_semantics=("arbitrary", "arbitrary")),
    )(a, b, *deps)


Z_TILE = 768
_Z_TILE_ORDER = ((0, 1, 2, 3, 4, 5, 6), (2, 0, 1, 6, 3, 4, 5), (4, 0, 5, 6, 1, 2, 3), (6, 2, 3, 4, 0, 1, 5))
_Z_EARLY_TILES = 4


def _z_proj(h, w_in_t, order, first, count, z_prev, name, dep=None):
    t = h.shape[0]

    def body(order_ref, h_ref, w_ref, *rest):
        rest[-1][...] = _dot_nt(h_ref[...], w_ref[...])

    prev = [] if z_prev is None else [z_prev]
    deps = [] if dep is None else [dep]
    return pl.pallas_call(
        body, name=name,
        grid_spec=pltpu.PrefetchScalarGridSpec(
            num_scalar_prefetch=1, grid=(count,),
            in_specs=[pl.BlockSpec((t, D_MODEL), lambda j, o: (0, 0)),
                      pl.BlockSpec((Z_TILE, D_MODEL), lambda j, o: (o[first + j], 0))]
            + [pl.BlockSpec(memory_space=pl.ANY)] * len(prev)
            + [pl.BlockSpec((8, 128), lambda j, o: (0, 0))] * len(deps),
            out_specs=pl.BlockSpec((t, Z_TILE), lambda j, o: (0, o[first + j]))),
        out_shape=jax.ShapeDtypeStruct((t, D_IN), F32),
        input_output_aliases={3: 0} if prev else {},
        compiler_params=_params(dimension_semantics=("arbitrary",)),
    )(order, h, w_in_t, *prev, *deps)


def _modulation(c_all, w_ada, b_ada_mine):
    def body(c_ref, w_ref, b_ref, act_ref, mod_ref):
        cv = c_ref[...]
        act = cv * _sigmoid(cv)
        act_ref[...] = act
        mod_ref[...] = jnp.dot(act.astype(BF16), w_ref[...].astype(BF16), preferred_element_type=F32) + b_ref[...]

    return pl.pallas_call(
        body, name="modulation",
        out_shape=(jax.ShapeDtypeStruct(c_all.shape, F32), jax.ShapeDtypeStruct((N_DEV, W_ADA_SHARD), F32)),
        compiler_params=_params(),
    )(c_all, w_ada, b_ada_mine)


def _modulated_norm(x, norm_g, scale, shift, tm=256):
    t, d = x.shape

    def body(x_ref, g_ref, sc_ref, sh_ref, h_ref):
        xv = x_ref[...]
        r = lax.rsqrt(jnp.mean(xv * xv, axis=-1, keepdims=True) + EPS)
        h = (xv * r) * g_ref[...] * (1.0 + sc_ref[...]) + sh_ref[...]
        h_ref[...] = h.astype(BF16)

    row = pl.BlockSpec((1, d), lambda i: (0, 0))
    return pl.pallas_call(
        body, name="modulated_norm", grid=(t // tm,),
        in_specs=[pl.BlockSpec((tm, d), lambda i: (i, 0)), row, row, row],
        out_specs=pl.BlockSpec((tm, d), lambda i: (i, 0)),
        out_shape=jax.ShapeDtypeStruct((t, d), BF16),
        compiler_params=_params(dimension_semantics=("arbitrary",)),
    )(x, norm_g, scale, shift)


def _window_bias(block_index):
    s = lax.broadcasted_iota(jnp.int32, (2 * BLOCK, BLOCK), 0)
    t = lax.broadcasted_iota(jnp.int32, (2 * BLOCK, BLOCK), 1)
    valid = ((s < BLOCK) & (s > t) & (block_index > 0)) | ((s >= BLOCK) & ((s - BLOCK) <= t))
    bias = jnp.where(valid, 0.0, -jnp.inf).astype(F32)
    return jnp.concatenate([bias] * 8, axis=1)


def _heads_t(pair_blocks, g):
    top = lax.broadcasted_iota(jnp.int32, (BLOCK, BLOCK), 0) < HEAD_DIM
    zeros = jnp.zeros((HEAD_DIM, BLOCK), F32)
    tiles = []
    for blk in pair_blocks:
        tp = blk.T
        if g == 0:
            tiles += [jnp.where(top, tp, 0.0), jnp.concatenate([tp[HEAD_DIM:], zeros], axis=0)]
        else:
            tiles += [jnp.concatenate([zeros, tp[:HEAD_DIM]], axis=0), jnp.where(top, 0.0, tp)]
    return jnp.concatenate(tiles, axis=1)


def _pair_block(xt, p, g):
    r0 = HEAD_DIM * g
    even = xt[r0:r0 + HEAD_DIM, (2 * p) * BLOCK:(2 * p + 1) * BLOCK]
    odd = xt[r0:r0 + HEAD_DIM, (2 * p + 1) * BLOCK:(2 * p + 2) * BLOCK]
    return jnp.concatenate([even, odd], axis=0).T


def _softmax_t(scores_t, bias, sink):
    st = scores_t + bias
    m = jnp.maximum(jnp.max(st, axis=0, keepdims=True), sink)
    e = jnp.exp(st - m)
    es = jnp.exp(sink - m)
    inv = 1.0 / (jnp.sum(e, axis=0, keepdims=True) + es)
    return e * inv, es * inv


def _dot(a, b):
    return jnp.dot(a, b, preferred_element_type=F32)


def _dot_nt(a, b):
    return lax.dot_general(a, b, (((1,), (1,)), ((), ())), preferred_element_type=F32)


def _layer_norm_fwd(v):
    mu = jnp.mean(v, axis=-1, keepdims=True)
    xc = v - mu
    rstd = lax.rsqrt(jnp.mean(xc * xc, axis=-1, keepdims=True) + EPS)
    return xc * rstd, rstd


def _tril(transposed=False):
    t = lax.broadcasted_iota(jnp.int32, (BLOCK, BLOCK), 0)
    s = lax.broadcasted_iota(jnp.int32, (BLOCK, BLOCK), 1)
    return s >= t if transposed else t >= s


def _const_spec(shape):
    return pl.BlockSpec(shape, lambda i: (0,) * len(shape))


def _kv_prev_spec(index):
    return pl.BlockSpec((BLOCK, 2 * D_KV), lambda i: (jnp.maximum(index(i) - 1, 0), SEG_KV // (2 * D_KV)))


def _keys_values(z_ref, kvp_ref):
    kvp, kvc = kvp_ref[...], z_ref[:, SEG_KV:SEG_KV + 2 * D_KV]
    kk = jnp.concatenate([kvp[:, :D_KV], kvc[:, :D_KV]], axis=0)
    vv = jnp.concatenate([kvp[:, D_KV:], kvc[:, D_KV:]], axis=0)
    return kk, vv


def _pair_cols(g, p, base=0):
    return slice(base + (4 * g + p) * 128, base + (4 * g + p + 1) * 128)


def _mixer_fwd(z, sink_rows, ln_g, ln_b, sgu_w, sgu_bt):
    t = z.shape[0]

    def body(z_ref, kvp_ref, sink_ref, lng_ref, lnb_ref, w_ref, bt_ref, a_ref):
        bias = _window_bias(pl.program_id(0))
        kk, vv = _keys_values(z_ref, kvp_ref)
        kk_b, vvt_b = kk.astype(BF16), vv.T.astype(BF16)
        for g in range(2):
            qt = _heads_t([z_ref[:, _pair_cols(g, p, SEG_Q)] * ATTN_SCALE for p in range(4)], g).astype(BF16)
            prob, _ = _softmax_t(_dot(kk_b, qt), bias, sink_ref[g])
            ot = _dot(vvt_b, prob.astype(BF16))
            for p in range(4):
                gate = z_ref[:, _pair_cols(g, p, SEG_GA)]
                a_ref[:, _pair_cols(g, p)] = (_pair_block(ot, p, g) * (gate * _sigmoid(gate))).astype(BF16)

        vhat, _ = _layer_norm_fwd(z_ref[:, SEG_VS:SEG_VS + D_SGU])
        vn = vhat * lng_ref[...] + lnb_ref[...]
        tril = _tril()
        for g in range(SGU_GROUPS):
            cols = slice(g * 128, (g + 1) * 128)
            wm = jnp.where(tril, w_ref[g], 0.0).astype(BF16)
            mixed = _dot(wm, vn[:, cols].astype(BF16)) + bt_ref[:, g:g + 1]
            gate = z_ref[:, SEG_GS + g * 128:SEG_GS + (g + 1) * 128]
            a_ref[:, D_ATTN + g * 128:D_ATTN + (g + 1) * 128] = (
                (z_ref[:, SEG_U + g * 128:SEG_U + (g + 1) * 128] * mixed) * (gate * _sigmoid(gate))).astype(BF16)

    return pl.pallas_call(
        body, name="mixer_fwd", grid=(t // BLOCK,),
        in_specs=[pl.BlockSpec((BLOCK, D_IN), lambda i: (i, 0)), _kv_prev_spec(lambda i: i),
                  _const_spec((2, 1, 8 * BLOCK)), _const_spec((1, D_SGU)), _const_spec((1, D_SGU)),
                  _const_spec((SGU_GROUPS, BLOCK, BLOCK)), _const_spec((BLOCK, SGU_GROUPS))],
        out_specs=pl.BlockSpec((BLOCK, D_MODEL), lambda i: (i, 0)),
        out_shape=jax.ShapeDtypeStruct((t, D_MODEL), BF16),
        compiler_params=_params(dimension_semantics=("arbitrary",)),
    )(z, z, sink_rows, ln_g, ln_b, sgu_w, sgu_bt)


def _mixer_bwd(z, da, sink_rows, ln_g, ln_b, sgu_w, sgu_wt, sgu_bt):
    t = z.shape[0]
    nb = t // BLOCK

    def body(z_ref, kvp_ref, da_ref, sink_ref, lng_ref, lnb_ref, w_ref, wt_ref, bt_ref,
             dz_ref, dsink_ref, dw_ref, db_ref, dlng_ref, dlnb_ref, carry_ref, dsink_acc, dbt_acc):
        step = pl.program_id(0)

        @pl.when(step == 0)
        def _():
            carry_ref[...] = jnp.zeros_like(carry_ref)
            dsink_acc[...] = jnp.zeros_like(dsink_acc)
            dbt_acc[...] = jnp.zeros_like(dbt_acc)
            dw_ref[...] = jnp.zeros_like(dw_ref)
            dlng_ref[...] = jnp.zeros_like(dlng_ref)
            dlnb_ref[...] = jnp.zeros_like(dlnb_ref)

        bias = _window_bias(nb - 1 - step)
        kk, vv = _keys_values(z_ref, kvp_ref)
        kk_b, vv_b = kk.astype(BF16), vv.astype(BF16)
        kkt_b, vvt_b = kk.T.astype(BF16), vv.T.astype(BF16)
        dkk = jnp.zeros((2 * BLOCK, D_KV), F32)
        dvv = jnp.zeros((2 * BLOCK, D_KV), F32)
        for g in range(2):
            qt = _heads_t([z_ref[:, _pair_cols(g, p, SEG_Q)] * ATTN_SCALE for p in range(4)], g).astype(BF16)
            prob, sink_prob = _softmax_t(_dot(kk_b, qt), bias, sink_ref[g])
            prob_b = prob.astype(BF16)
            ot = _dot(vvt_b, prob_b)
            gates = [z_ref[:, _pair_cols(g, p, SEG_GA)] for p in range(4)]
            sig = [_sigmoid(gt) for gt in gates]
            d_attn = [da_ref[:, _pair_cols(g, p)] for p in range(4)]
            d_ot = _heads_t([d_attn[p] * (gates[p] * sig[p]) for p in range(4)], g).astype(BF16)
            d_prob = _dot(vv_b, d_ot)
            delta = jnp.sum(prob * d_prob, axis=0, keepdims=True)
            d_scores = (prob * (d_prob - delta)).astype(BF16)
            dsink_acc[g] -= sink_prob * delta
            d_qt = _dot(kkt_b, d_scores)
            dkk = dkk + _dot_nt(d_scores, qt)
            dvv = dvv + _dot_nt(prob_b, d_ot)
            for p in range(4):
                dz_ref[:, _pair_cols(g, p, SEG_Q)] = (_pair_block(d_qt, p, g) * ATTN_SCALE).astype(BF16)
                d_silu = sig[p] * (1.0 + gates[p] * (1.0 - sig[p]))
                dz_ref[:, _pair_cols(g, p, SEG_GA)] = (d_attn[p] * _pair_block(ot, p, g) * d_silu).astype(BF16)
        d_kv = jnp.concatenate([dkk, dvv], axis=1)
        dz_ref[:, SEG_KV:SEG_KV + 2 * D_KV] = (d_kv[BLOCK:] + carry_ref[...]).astype(BF16)
        carry_ref[...] = d_kv[:BLOCK]

        vhat, rstd = _layer_norm_fwd(z_ref[:, SEG_VS:SEG_VS + D_SGU])
        lng = lng_ref[...]
        vn = vhat * lng + lnb_ref[...]
        tril, triu = _tril(), _tril(transposed=True)
        lane = lax.broadcasted_iota(jnp.int32, (BLOCK, 128), 1)
        d_bt = jnp.zeros((BLOCK, 128), F32)
        d_vn = []
        for g in range(SGU_GROUPS):
            cols = slice(g * 128, (g + 1) * 128)
            wm = jnp.where(tril, w_ref[g], 0.0).astype(BF16)
            wmt = jnp.where(triu, wt_ref[g], 0.0).astype(BF16)
            vn_g = vn[:, cols].astype(BF16)
            mixed = _dot(wm, vn_g) + bt_ref[:, g:g + 1]
            gate = z_ref[:, SEG_GS + g * 128:SEG_GS + (g + 1) * 128]
            u = z_ref[:, SEG_U + g * 128:SEG_U + (g + 1) * 128]
            d_out = da_ref[:, D_ATTN + g * 128:D_ATTN + (g + 1) * 128]
            sg = _sigmoid(gate)
            d_um = d_out * (gate * sg)
            dz_ref[:, SEG_U + g * 128:SEG_U + (g + 1) * 128] = (d_um * mixed).astype(BF16)
            dz_ref[:, SEG_GS + g * 128:SEG_GS + (g + 1) * 128] = (
                d_out * (u * mixed) * (sg * (1.0 + gate * (1.0 - sg)))).astype(BF16)
            d_mixed = d_um * u
            d_mixed_b = d_mixed.astype(BF16)
            dw_ref[g] += jnp.where(tril, _dot_nt(d_mixed_b, vn_g), 0.0)
            d_bt = d_bt + jnp.where(lane == g, jnp.sum(d_mixed, axis=-1, keepdims=True), 0.0)
            d_vn.append(_dot(wmt, d_mixed_b))
        dbt_acc[...] += d_bt
        d_vn = jnp.concatenate(d_vn, axis=1)
        dlng_ref[...] += jnp.sum(d_vn * vhat, axis=0, keepdims=True)
        dlnb_ref[...] += jnp.sum(d_vn, axis=0, keepdims=True)
        d_vhat = d_vn * lng
        d_v = rstd * (d_vhat - jnp.mean(d_vhat, axis=-1, keepdims=True)
                      - vhat * jnp.mean(d_vhat * vhat, axis=-1, keepdims=True))
        dz_ref[:, SEG_VS:SEG_VS + D_SGU] = d_v.astype(BF16)

        @pl.when(step == nb - 1)
        def _():
            db_ref[...] = dbt_acc[...].T[:SGU_GROUPS]
            lane_row = lax.broadcasted_iota(jnp.int32, (1, 128), 1)
            d_sink = jnp.zeros((1, 128), F32)
            for g in range(2):
                acc = dsink_acc[g]
                for j in range(8):
                    head_sum = jnp.sum(acc[:, j * BLOCK:(j + 1) * BLOCK], axis=-1, keepdims=True)
                    d_sink = d_sink + jnp.where(lane_row == 8 * g + j, head_sum, 0.0)
            dsink_ref[...] = d_sink

    rev = lambda i: nb - 1 - i
    return pl.pallas_call(
        body, name="mixer_bwd", grid=(nb,),
        in_specs=[pl.BlockSpec((BLOCK, D_IN), lambda i: (rev(i), 0)), _kv_prev_spec(rev),
                  pl.BlockSpec((BLOCK, D_MODEL), lambda i: (rev(i), 0)),
                  _const_spec((2, 1, 8 * BLOCK)), _const_spec((1, D_SGU)), _const_spec((1, D_SGU)),
                  _const_spec((SGU_GROUPS, BLOCK, BLOCK)), _const_spec((SGU_GROUPS, BLOCK, BLOCK)),
                  _const_spec((BLOCK, SGU_GROUPS))],
        out_specs=(pl.BlockSpec((BLOCK, D_IN), lambda i: (rev(i), 0)), _const_spec((1, 128)),
                   _const_spec((SGU_GROUPS, BLOCK, BLOCK)), _const_spec((SGU_GROUPS, BLOCK)),
                   _const_spec((1, D_SGU)), _const_spec((1, D_SGU))),
        out_shape=(jax.ShapeDtypeStruct((t, D_IN), BF16), jax.ShapeDtypeStruct((1, 128), F32),
                   jax.ShapeDtypeStruct((SGU_GROUPS, BLOCK, BLOCK), F32), jax.ShapeDtypeStruct((SGU_GROUPS, BLOCK), F32),
                   jax.ShapeDtypeStruct((1, D_SGU), F32), jax.ShapeDtypeStruct((1, D_SGU), F32)),
        scratch_shapes=[pltpu.VMEM((BLOCK, 2 * D_KV), F32), pltpu.VMEM((2, 1, 8 * BLOCK), F32),
                        pltpu.VMEM((BLOCK, 128), F32)],
        compiler_params=_params(dimension_semantics=("arbitrary",)),
    )(z, z, da, sink_rows, ln_g, ln_b, sgu_w, sgu_wt, sgu_bt)


def _out_proj_head(a, w_out_full, x, target, gate, final_g, tm=256):
    t, d = x.shape

    def body(a_ref, w_ref, x_ref, tg_ref, gate_ref, fg_ref, dx2_ref, dy_ref, loss_ref, dfg_ref, dgate_ref):
        @pl.when(pl.program_id(0) == 0)
        def _():
            loss_ref[...] = jnp.zeros_like(loss_ref)
            dfg_ref[...] = jnp.zeros_like(dfg_ref)
            dgate_ref[...] = jnp.zeros_like(dgate_ref)

        yv, gate, fg = _dot(a_ref[...], w_ref[...]), gate_ref[...], fg_ref[...]
        x2 = x_ref[...] + gate * yv
        r2 = lax.rsqrt(jnp.mean(x2 * x2, axis=-1, keepdims=True) + EPS)
        nrm = x2 * r2
        err = nrm * fg - tg_ref[...]
        loss_ref[...] += 0.5 * jnp.sum(jnp.mean(err * err, axis=-1, keepdims=True), axis=0, keepdims=True)
        d_out = err * (1.0 / d)
        dfg_ref[...] += jnp.sum(d_out * nrm, axis=0, keepdims=True)
        d_nrm = d_out * fg
        dx2 = r2 * (d_nrm - nrm * jnp.mean(d_nrm * nrm, axis=-1, keepdims=True))
        dx2_ref[...] = dx2
        dgate_ref[...] += jnp.sum(dx2 * yv, axis=0, keepdims=True)
        dy_ref[...] = (dx2 * gate).astype(BF16)

    blk = pl.BlockSpec((tm, d), lambda i: (i, 0))
    row = _const_spec((1, d))
    whole = pl.BlockSpec(w_out_full.shape, lambda i: (0, 0), pipeline_mode=pl.Buffered(1))
    return pl.pallas_call(
        body, name="out_proj_head", grid=(t // tm,),
        in_specs=[pl.BlockSpec((tm, a.shape[1]), lambda i: (i, 0)), whole, blk, blk, row, row],
        out_specs=(blk, blk, _const_spec((1, 128)), row, row),
        out_shape=(jax.ShapeDtypeStruct((t, d), F32), jax.ShapeDtypeStruct((t, d), BF16),
                   jax.ShapeDtypeStruct((1, 128), F32), jax.ShapeDtypeStruct((1, d), F32),
                   jax.ShapeDtypeStruct((1, d), F32)),
        compiler_params=_params(dimension_semantics=("arbitrary",)),
    )(a, w_out_full, x, target, gate, final_g)


def _z_proj_bwd_norm(dz, w_in_t, x, dx2, norm_g, scale, dep, tm=256):
    t, d = x.shape

    def body(dz_ref, w_ref, x_ref, dx2_ref, g_ref, sc_ref, dep_ref, gx_ref, dshift_ref, dscale_ref, dg_ref):
        @pl.when(pl.program_id(0) == 0)
        def _():
            dshift_ref[...] = jnp.zeros_like(dshift_ref)
            dscale_ref[...] = jnp.zeros_like(dscale_ref)
            dg_ref[...] = jnp.zeros_like(dg_ref)

        dh, xv, g = _dot(dz_ref[...], w_ref[...]), x_ref[...], g_ref[...]
        one_plus = 1.0 + sc_ref[...]
        r = lax.rsqrt(jnp.mean(xv * xv, axis=-1, keepdims=True) + EPS)
        xn = xv * r
        dshift_ref[...] += jnp.sum(dh, axis=0, keepdims=True)
        dscale_ref[...] += jnp.sum(dh * (xn * g), axis=0, keepdims=True)
        d_y = dh * one_plus
        dg_ref[...] += jnp.sum(d_y * xn, axis=0, keepdims=True)
        d_xn = d_y * g
        gx_ref[...] = dx2_ref[...] + r * (d_xn - xn * jnp.mean(d_xn * xn, axis=-1, keepdims=True))

    blk = pl.BlockSpec((tm, d), lambda i: (i, 0))
    row = _const_spec((1, d))
    whole = pl.BlockSpec(w_in_t.shape, lambda i: (0, 0), pipeline_mode=pl.Buffered(1))
    return pl.pallas_call(
        body, name="z_proj_bwd_norm", grid=(t // tm,),
        in_specs=[pl.BlockSpec((tm, dz.shape[1]), lambda i: (i, 0)), whole, blk, blk, row, row, _const_spec((8, 128))],
        out_specs=(blk, row, row, row),
        out_shape=(jax.ShapeDtypeStruct((t, d), F32),) + (jax.ShapeDtypeStruct((1, d), F32),) * 3,
        compiler_params=_params(dimension_semantics=("arbitrary",)),
    )(dz, w_in_t, x, dx2, norm_g, scale, dep)


def _adamw(w, g, m, v):
    m = ADAM_B1 * m + (1.0 - ADAM_B1) * g
    v = ADAM_B2 * v + (1.0 - ADAM_B2) * (g * g)
    m_hat = m / (1.0 - ADAM_B1 ** ADAM_STEP)
    v_hat = v / (1.0 - ADAM_B2 ** ADAM_STEP)
    delta = -ADAM_LR * (m_hat / (jnp.sqrt(v_hat) + ADAM_EPS) + ADAM_WD * w)
    return delta, m, v


def _relay_sum(second_chip, pair, land, tr):
    _, r, c = pair.shape

    def body(chip_ref, a_ref, b_ref, o_ref):
        o_ref[...] = (a_ref[...].astype(F32) + b_ref[...].astype(F32)).astype(BF16)

    return pl.pallas_call(
        body, name="w_in_grad_relay_sum",
        grid_spec=pltpu.PrefetchScalarGridSpec(
            num_scalar_prefetch=1, grid=(r // tr,),
            in_specs=[pl.BlockSpec((None, tr, c), lambda i, chip_ref: (chip_ref[0], i, 0)),
                      pl.BlockSpec((None, tr, c), lambda i, chip_ref: (1, i, 0))],
            out_specs=pl.BlockSpec((tr, c), lambda i, chip_ref: (i, 0))),
        out_shape=jax.ShapeDtypeStruct((r, c), BF16),
        compiler_params=_params(dimension_semantics=("arbitrary",)),
    )(second_chip, pair, land)


def _adam_from_chips(chip, pair, landed, w, m, v, name, tc):
    _, r, c = pair.shape
    n = len(landed)

    def body(chip_ref, own_ref, *refs):
        w_ref, m_ref, v_ref, g_ref, d_ref, nm_ref, nv_ref = refs[n:]
        g = own_ref[...].astype(F32)
        for k in range(n):
            g = g + refs[k][...].astype(F32)
        g_ref[...] = g
        d_ref[...], nm_ref[...], nv_ref[...] = _adamw(w_ref[...], g, m_ref[...], v_ref[...])

    def landed_spec(index):
        return pl.BlockSpec((None, r, tc), lambda i, chip_ref: (index, 0, i))

    blk = pl.BlockSpec((r, tc), lambda i, chip_ref: (0, i))
    return pl.pallas_call(
        body, name=name,
        grid_spec=pltpu.PrefetchScalarGridSpec(
            num_scalar_prefetch=1, grid=(c // tc,),
            in_specs=[pl.BlockSpec((None, r, tc), lambda i, chip_ref: (chip_ref[0], 0, i))]
            + [landed_spec(index) for _, index in landed] + [blk, blk, blk],
            out_specs=(blk,) * 4),
        out_shape=(jax.ShapeDtypeStruct((r, c), F32),) * 4,
        compiler_params=_params(dimension_semantics=("arbitrary",)),
    )(chip, pair, *[array for array, _ in landed], w, m, v)


def _adam_w_ada(act_t, dmod_mine, w, m, v, tr=256):
    r, c = w.shape

    def body(a_ref, dm_ref, w_ref, m_ref, v_ref, g_ref, d_ref, nm_ref, nv_ref):
        g = _dot(a_ref[...].astype(BF16), dm_ref[...].astype(BF16))
        g_ref[...] = g
        d_ref[...], nm_ref[...], nv_ref[...] = _adamw(w_ref[...], g, m_ref[...], v_ref[...])

    blk = pl.BlockSpec((tr, c), lambda i: (i, 0))
    return pl.pallas_call(
        body, name="adam_w_ada", grid=(r // tr,),
        in_specs=[pl.BlockSpec((tr, N_DEV), lambda i: (i, 0)), _const_spec((N_DEV, c)), blk, blk, blk],
        out_specs=(blk,) * 4, out_shape=(jax.ShapeDtypeStruct((r, c), F32),) * 4,
        compiler_params=_params(dimension_semantics=("arbitrary",)),
    )(act_t, dmod_mine, w, m, v)


def _pack_small(d_shift, d_scale, d_gate, d_norm_g, d_final_g, d_ln_g, d_ln_b, loss, d_sinks, d_sgu_b):
    def body(shift_ref, scale_ref, gate_ref, ng_ref, fg_ref, lng_ref, lnb_ref, loss_ref, sink_ref, b_ref, o_ref):
        o_ref[...] = jnp.zeros_like(o_ref)
        o_ref[ROW_SHIFT:ROW_SHIFT + 1, :] = shift_ref[...]
        o_ref[ROW_SCALE:ROW_SCALE + 1, :] = scale_ref[...]
        o_ref[ROW_GATE:ROW_GATE + 1, :] = gate_ref[...]
        o_ref[ROW_NORM_G:ROW_NORM_G + 1, :] = ng_ref[...]
        o_ref[ROW_FINAL_G:ROW_FINAL_G + 1, :] = fg_ref[...]
        o_ref[ROW_LN:ROW_LN + 1, 0:D_SGU] = lng_ref[...]
        o_ref[ROW_LN:ROW_LN + 1, D_SGU:2 * D_SGU] = lnb_ref[...]
        o_ref[ROW_MISC:ROW_MISC + 1, 0:128] = loss_ref[...]
        o_ref[ROW_MISC:ROW_MISC + 1, 128:256] = sink_ref[...]
        o_ref[ROW_SGU_B:ROW_SGU_B + SGU_GROUPS, 0:BLOCK] = b_ref[...]

    return pl.pallas_call(
        body, name="pack_small", out_shape=jax.ShapeDtypeStruct((SMALL_ROWS, D_MODEL), F32),
        compiler_params=_params(),
    )(d_shift, d_scale, d_gate, d_norm_g, d_final_g, d_ln_g, d_ln_b, loss, d_sinks, d_sgu_b)


_SMALL_NAMES = ("norm_g", "b_ada", "attn_sinks", "sgu_ln_g", "sgu_ln_b", "sgu_w", "sgu_b", "final_g")


def _adam_small(partials, d_sgu_w_all, weights, moments_m, moments_v):
    names = _SMALL_NAMES
    k = len(names)

    def body(*refs):
        p_ref, sw_ref = refs[0], refs[1]
        w_refs, m_refs, v_refs = refs[2:2 + k], refs[2 + k:2 + 2 * k], refs[2 + 2 * k:2 + 3 * k]
        loss_ref, dmod_ref = refs[2 + 3 * k], refs[3 + 3 * k]
        out_refs = refs[4 + 3 * k:4 + 7 * k]
        sum_ref = refs[4 + 7 * k]
        total = p_ref[0]
        for j in range(1, N_DEV):
            total = total + p_ref[j]
        sum_ref[...] = total
        for j in range(N_DEV):
            for part, row in enumerate((ROW_SHIFT, ROW_SCALE, ROW_GATE)):
                dmod_ref[j:j + 1, part * D_MODEL:(part + 1) * D_MODEL] = p_ref[j, row:row + 1, :]
        loss_ref[...] = sum_ref[ROW_MISC:ROW_MISC + 1, 0:1]
        d_sgu_w = sw_ref[0]
        for j in range(1, N_DEV):
            d_sgu_w = d_sgu_w + sw_ref[j]
        grads = {
            "norm_g": sum_ref[ROW_NORM_G:ROW_NORM_G + 1, :],
            "b_ada": jnp.concatenate([sum_ref[r:r + 1, :] for r in (ROW_SHIFT, ROW_SCALE, ROW_GATE)], axis=1),
            "attn_sinks": sum_ref[ROW_MISC:ROW_MISC + 1, 128:128 + N_Q_HEADS],
            "sgu_ln_g": sum_ref[ROW_LN:ROW_LN + 1, 0:D_SGU],
            "sgu_ln_b": sum_ref[ROW_LN:ROW_LN + 1, D_SGU:2 * D_SGU],
            "sgu_w": d_sgu_w[None],
            "sgu_b": sum_ref[ROW_SGU_B:ROW_SGU_B + SGU_GROUPS, 0:BLOCK][None],
            "final_g": sum_ref[ROW_FINAL_G:ROW_FINAL_G + 1, :],
        }
        for i, name in enumerate(names):
            g = grads[name]
            delta, m, v = _adamw(w_refs[i][...], g, m_refs[i][...], v_refs[i][...])
            out_refs[4 * i][...] = g
            out_refs[4 * i + 1][...] = delta
            out_refs[4 * i + 2][...] = m
            out_refs[4 * i + 3][...] = v

    shapes = [jax.ShapeDtypeStruct((1, 1), F32), jax.ShapeDtypeStruct((N_DEV, 3 * D_MODEL), F32)]
    for name in names:
        shapes += [jax.ShapeDtypeStruct(weights[name].shape, F32)] * 4
    outs = pl.pallas_call(
        body, name="adam_small", out_shape=tuple(shapes),
        scratch_shapes=[pltpu.VMEM((SMALL_ROWS, D_MODEL), F32)],
        compiler_params=_params(),
    )(partials, d_sgu_w_all, *[weights[n] for n in names], *[moments_m[n] for n in names],
      *[moments_v[n] for n in names])
    return outs[0], outs[1], {name: outs[2 + 4 * i:6 + 4 * i] for i, name in enumerate(names)}


def kernel(x, c, norm_g, w_ada, b_ada, w_in, attn_sinks, sgu_ln_g, sgu_ln_b, sgu_w, sgu_b, w_out, final_g, loss_target, m_norm_g, m_w_ada, m_b_ada, m_w_in, m_attn_sinks, m_sgu_ln_g, m_sgu_ln_b, m_sgu_w, m_sgu_b, m_w_out, m_final_g, v_norm_g, v_w_ada, v_b_ada, v_w_in, v_attn_sinks, v_sgu_ln_g, v_sgu_ln_b, v_sgu_w, v_sgu_b, v_w_out, v_final_g):
    xi, yi, ci = _place()
    me = 4 * xi + 2 * yi + ci
    x2d, target = x[0], loss_target[0]
    t = x2d.shape[0]

    core = ci.astype(jnp.int32).reshape(1)
    chip = (2 * xi + yi).astype(jnp.int32).reshape(1)

    first = _own_block_copies(_first_targets)
    first_flight = _start_copies([_with_own_slot(w_in[0].T.astype(BF16), me)], first, 2, core, "gather_w_in_start")

    c_all = _all_gather_small(c.reshape(8, 256) + first_flight[3][0, 0], "gather_c").reshape(N_DEV, D_MODEL)
    b_mine = lax.dynamic_slice(b_ada, (0, me * W_ADA_SHARD), (1, W_ADA_SHARD))
    c_act, mod_part = _modulation(c_all, w_ada[0], b_mine)
    mod_all = _all_gather_small(mod_part, "gather_mod")

    across = _wait_then_start(first_flight, lambda *a: first(*a)[1:], _second_axis_stage_copies, 3, mod_all,
                              "gather_w_in_second_axis_stage")
    mod = lax.dynamic_index_in_dim(mod_all, me, axis=1, keepdims=False).reshape(1, 3 * D_MODEL)
    mod = mod + across[3][0, 0]
    shift, scale, gate = mod[:, :D_MODEL], mod[:, D_MODEL:2 * D_MODEL], mod[:, 2 * D_MODEL:]
    h = _modulated_norm(x2d, norm_g, scale, shift)

    w_in_pair = _wait_copies((first_flight[0], first_flight[1], across[2], None), lambda *a: first(*a)[:1], h,
                             "gather_w_in_sibling_wait")
    tile_order = jnp.asarray(_Z_TILE_ORDER, jnp.int32)[chip[0]]
    z_own = _z_proj(h, w_in_pair[0].reshape(D_IN, D_MODEL), tile_order, 0, 1, None, "z_proj_own")
    forward = _wait_then_start((across[0], across[1], w_in_pair, None), lambda *a: _second_axis_stage_copies(*a)[:1],
                               _second_axis_forward_copies, 1, z_own, "gather_w_in_second_axis_forward")
    w_in_most = _wait_copies((across[0], across[1], forward[2], None),
                             lambda *a: _second_axis_stage_copies(*a)[1:2], z_own, "gather_w_in_first_forward_wait")
    w_in_most = _wait_copies((forward[0], forward[1], w_in_most, None), _second_axis_forward_copies, z_own,
                             "gather_w_in_second_forward_wait")
    z_early = _z_proj(h, w_in_most[0].reshape(D_IN, D_MODEL), tile_order, 1, _Z_EARLY_TILES - 1, z_own, "z_proj_early")
    w_out_flight = _start_copies([_with_own_slot(w_out[0].astype(BF16), me)], _own_block_copies(_my_core_and_sibling),
                                 4, z_early, "gather_w_out_start")
    w_in_flight = _wait_then_start((across[0], across[1], w_in_most, None),
                                   lambda *a: _second_axis_stage_copies(*a)[2:], _diagonal_forward_copies, 1,
                                   w_out_flight[3], "gather_w_in_last_stage")
    w_in_all = _wait_copies(w_in_flight, _diagonal_forward_copies, z_early, "gather_w_in_last_wait")[0]
    w_in_t = w_in_all.reshape(D_IN, D_MODEL)
    z = _z_proj(h, w_in_t, tile_order, _Z_EARLY_TILES, 7 - _Z_EARLY_TILES, z_early, "z_proj_late")
    w_out_flight = _wait_then_start(w_out_flight, _own_block_copies(_my_core_and_sibling), _forward_copies, 3, z,
                                    "gather_w_out_forward_stage")
    sink_rows = jnp.repeat(attn_sinks.reshape(N_Q_HEADS), BLOCK).reshape(2, 1, 8 * BLOCK)
    sgu_bt = sgu_b[0].T
    a = _mixer_fwd(z, sink_rows + w_out_flight[3][0, 0], sgu_ln_g, sgu_ln_b, sgu_w[0], sgu_bt)
    w_out_all = _wait_copies(w_out_flight, _forward_copies, a, "gather_w_out_forward_wait")[0]
    w_out_full = w_out_all.reshape(D_MODEL, D_MODEL)
    final_g_row = final_g.reshape(1, D_MODEL)
    dx2, dy, loss_part, d_final_g, d_gate = _out_proj_head(a, w_out_full, x2d, target, gate, final_g_row)

    da = _matmul(dy, w_out_full, "nt", F32, min(t, 1024), 1024, "out_proj_bwd")
    dw_out = _matmul(a, dy, "tn", BF16, 1024, 1024, "w_out_grad").reshape(4, 2, W_OUT_SHARD, D_MODEL)
    pair_out = _pair_reduce(dw_out, "w_out_grad_pair_reduce", W_OUT_SHARD // 2)
    out_flight = _start_copies([pair_out, lax.empty((3, W_OUT_SHARD, D_MODEL), BF16)], _chip_copies, 3, core,
                               "w_out_grad_chip_start")
    dz, d_sinks, d_sgu_w, d_sgu_b, d_ln_g, d_ln_b = _mixer_bwd(
        z, da, sink_rows + out_flight[3][0, 0], sgu_ln_g, sgu_ln_b, sgu_w[0], jnp.swapaxes(sgu_w[0], 1, 2), sgu_bt)
    sgu_w_flight = _start_copies([_with_own_slot(d_sgu_w, me)], _own_block_copies(_all_others), N_DEV - 1, core,
                                 "sgu_w_grad_gather_start")
    dw_in_t = _matmul(dz, h, "tn", BF16, 768, D_MODEL, "w_in_grad", dep=sgu_w_flight[3])
    dw_in_t = dw_in_t.reshape(4, 2, W_IN_SHARD, D_MODEL)
    pair_in = _pair_reduce(dw_in_t, "w_in_grad_pair_reduce", W_IN_SHARD // 3)
    hop1 = _start_copies([pair_in, lax.empty((2, W_IN_SHARD, D_MODEL), BF16)], _first_hop_copies, 2, core,
                         "w_in_grad_first_hop_start")
    grad_x, d_shift, d_scale, d_norm_g = _z_proj_bwd_norm(dz, w_in_t, x2d, dx2, norm_g, scale, hop1[3])

    partial = _pack_small(d_shift, d_scale, d_gate, d_norm_g, d_final_g, d_ln_g, d_ln_b, loss_part, d_sinks, d_sgu_b)
    small_flight = _start_copies([_with_own_slot(partial, me)], _own_block_copies(_all_others), N_DEV - 1, core,
                                 "small_grad_gather_start")
    pair_in, land_first = _wait_copies(hop1, _first_hop_copies, small_flight[3], "w_in_grad_first_hop_wait")
    second_chip = (2 * ((xi + ci) % 2) + (yi + 1 - ci) % 2).astype(jnp.int32).reshape(1)
    relay = _relay_sum(second_chip, pair_in, land_first, W_IN_SHARD // 3)
    hop2 = _start_copies([relay, lax.empty((1, W_IN_SHARD, D_MODEL), BF16)], _second_hop_copies, 1, core,
                         "w_in_grad_second_hop_start")
    pair_out, land_out = _wait_copies(out_flight, _chip_copies, hop2[3], "w_out_grad_chip_wait")
    big = {"w_out": _adam_from_chips(chip, pair_out, [(land_out, k) for k in range(3)], w_out[0], m_w_out[0],
                                     v_w_out[0], "adam_w_out", 1024)}
    partial_all = _wait_copies(small_flight, _own_block_copies(_all_others), big["w_out"][0],
                               "small_grad_gather_wait")[0]
    d_sgu_w_all = _wait_copies(sgu_w_flight, _own_block_copies(_all_others), partial_all, "sgu_w_grad_gather_wait")[0]
    weights = {"norm_g": norm_g, "b_ada": b_ada, "attn_sinks": attn_sinks, "sgu_ln_g": sgu_ln_g,
               "sgu_ln_b": sgu_ln_b, "sgu_w": sgu_w, "sgu_b": sgu_b, "final_g": final_g_row}
    moments_m = {"norm_g": m_norm_g, "b_ada": m_b_ada, "attn_sinks": m_attn_sinks, "sgu_ln_g": m_sgu_ln_g,
                 "sgu_ln_b": m_sgu_ln_b, "sgu_w": m_sgu_w, "sgu_b": m_sgu_b,
                 "final_g": m_final_g.reshape(1, D_MODEL)}
    moments_v = {"norm_g": v_norm_g, "b_ada": v_b_ada, "attn_sinks": v_attn_sinks, "sgu_ln_g": v_sgu_ln_g,
                 "sgu_ln_b": v_sgu_ln_b, "sgu_w": v_sgu_w, "sgu_b": v_sgu_b,
                 "final_g": v_final_g.reshape(1, D_MODEL)}
    loss, dmod_all, small = _adam_small(partial_all, d_sgu_w_all, weights, moments_m, moments_v)
    small["final_g"] = tuple(o.reshape(D_MODEL) for o in small["final_g"])

    dmod_mine = lax.dynamic_slice(dmod_all, (0, me * W_ADA_SHARD), (N_DEV, W_ADA_SHARD))
    big["w_ada"] = _adam_w_ada(c_act.T, dmod_mine, w_ada[0], m_w_ada[0], v_w_ada[0])
    _, land_second = _wait_copies(hop2, _second_hop_copies, big["w_ada"][0], "w_in_grad_second_hop_wait")
    big["w_in"] = tuple(o.T for o in _adam_from_chips(
        chip, pair_in, [(land_first, 0), (land_second, 0)], w_in[0].T, m_w_in[0].T, v_w_in[0].T, "adam_w_in", 256))
    order = ["norm_g", "w_ada", "b_ada", "w_in", "attn_sinks", "sgu_ln_g", "sgu_ln_b", "sgu_w", "sgu_b", "w_out",
             "final_g"]
    outs = [loss.reshape(()), grad_x[None]]
    for k in range(4):
        for name in order:
            outs.append(big[name][k][None] if name in big else small[name][k])
    return tuple(outs)
```

```python
import jax
import jax.numpy as jnp
from jax import lax
from jax.experimental import pallas as pl
from jax.experimental.pallas import tpu as pltpu

F32 = jnp.float32
BF16 = jnp.bfloat16
MESH = pl.DeviceIdType.MESH

N_DEV = 8
D_MODEL = 2048
HEAD_DIM = 64
D_ATTN = 1024
N_Q_HEADS = 16
D_KV = 128
BLOCK = 128
D_SGU = 1024
SGU_GROUPS = 8
D_IN = 5376
W_IN_SHARD = D_IN // N_DEV
W_OUT_SHARD = D_MODEL // N_DEV
W_ADA_SHARD = 3 * D_MODEL // N_DEV
EPS = 1e-6
ATTN_SCALE = 0.125

ADAM_LR = 0.001
ADAM_B1 = 0.9
ADAM_B2 = 0.999
ADAM_EPS = 1e-08
ADAM_WD = 0.01
ADAM_STEP = 10

SEG_Q, SEG_KV, SEG_GA, SEG_U, SEG_VS, SEG_GS = 0, 1024, 1280, 2304, 3328, 4352

VMEM_LIMIT = 56 * 1024 * 1024

ROW_SHIFT, ROW_SCALE, ROW_GATE, ROW_NORM_G, ROW_FINAL_G, ROW_LN, ROW_MISC, ROW_SGU_B = 0, 1, 2, 3, 4, 5, 6, 8
SMALL_ROWS = 16


def _params(**kw):
    return pltpu.CompilerParams(vmem_limit_bytes=VMEM_LIMIT, **kw)


def _sigmoid(x):
    return 0.5 * (jnp.tanh(0.5 * x) + 1.0)


def _place():
    return lax.axis_index("x"), lax.axis_index("y"), lax.axis_index("c")


def _pair_reduce(blocks, name, row_chunk):
    _, _, r, cols = blocks.shape
    assert r % row_chunk == 0

    def body(in_ref, out_ref, land, own, summed, send_sems, recv_sems, own_sems, out_sems):
        x, y, c = _place()
        sends, loads, stores = [], [], []
        for m in range(4):
            cp = pltpu.make_async_remote_copy(
                src_ref=in_ref.at[m, 1 - c], dst_ref=land.at[m], send_sem=send_sems.at[m], recv_sem=recv_sems.at[m],
                device_id=(x, y, 1 - c), device_id_type=MESH)
            cp.start()
            sends.append(cp)
            ld = pltpu.make_async_copy(in_ref.at[m, c], own.at[m], own_sems.at[m])
            ld.start()
            loads.append(ld)
        for m in range(4):
            sends[m].wait_recv()
            loads[m].wait()
            for k in range(r // row_chunk):
                rows = slice(k * row_chunk, (k + 1) * row_chunk)
                summed[m, rows, :] = (own[m, rows, :].astype(F32) + land[m, rows, :].astype(F32)).astype(BF16)
            st = pltpu.make_async_copy(summed.at[m], out_ref.at[m], out_sems.at[m])
            st.start()
            stores.append(st)
        for m in range(4):
            sends[m].wait_send()
            stores[m].wait()

    spec = pl.BlockSpec(memory_space=pl.ANY)
    return pl.pallas_call(
        body, name=name, out_shape=jax.ShapeDtypeStruct((4, r, cols), BF16),
        in_specs=[spec], out_specs=spec,
        scratch_shapes=[pltpu.VMEM((4, r, cols), BF16), pltpu.VMEM((4, r, cols), BF16), pltpu.VMEM((4, r, cols), BF16),
                        pltpu.SemaphoreType.DMA((4,)), pltpu.SemaphoreType.DMA((4,)), pltpu.SemaphoreType.DMA((4,)),
                        pltpu.SemaphoreType.DMA((4,))],
        compiler_params=_params(),
    )(blocks)


_HBM = pl.BlockSpec(memory_space=pltpu.HBM)
_SEM = pl.BlockSpec(memory_space=pltpu.SEMAPHORE)
_EFFECT = pltpu.SideEffectType.DATAFLOW_SIDE_EFFECTING


def _start_copies(bufs, copies, n_copies, after, name):
    nb = len(bufs)

    def body(*refs):
        for cp in copies(refs[:nb], refs[nb + 1], refs[nb + 2]):
            cp.start()
        refs[-1][...] = jnp.zeros_like(refs[-1])

    out = pl.pallas_call(
        body, name=name,
        out_shape=(pltpu.SemaphoreType.DMA((n_copies,)), pltpu.SemaphoreType.DMA((n_copies,)),
                   *[pltpu.HBM(b.shape, b.dtype) for b in bufs], jax.ShapeDtypeStruct((8, 128), F32)),
        in_specs=(_HBM,) * nb + (pl.BlockSpec(memory_space=pl.ANY),),
        out_specs=(_SEM, _SEM) + (_HBM,) * nb + (pl.BlockSpec(memory_space=pltpu.VMEM),),
        input_output_aliases={i: 2 + i for i in range(nb)},
        compiler_params=pltpu.CompilerParams(has_side_effects=_EFFECT),
    )(*[pltpu.with_memory_space_constraint(b, pltpu.HBM) for b in bufs], after)
    return out[0], out[1], list(out[2:2 + nb]), out[-1]


def _wait_copies(flight, copies, after, name):
    send_sems, recv_sems, bufs, _ = flight
    nb = len(bufs)

    def body(*refs):
        for cp in copies(refs[:nb], refs[nb], refs[nb + 1]):
            cp.wait_send()
            cp.wait_recv()

    return pl.pallas_call(
        body, name=name,
        out_shape=tuple(pltpu.HBM(b.shape, b.dtype) for b in bufs),
        in_specs=(_HBM,) * nb + (_SEM, _SEM, pl.BlockSpec(memory_space=pl.ANY)), out_specs=(_HBM,) * nb,
        input_output_aliases={i: i for i in range(nb)},
        compiler_params=pltpu.CompilerParams(has_side_effects=_EFFECT),
    )(*bufs, send_sems, recv_sems, after)


def _wait_then_start(flight, waited, started, n_started, after, name):
    old_send, old_recv, bufs, _ = flight
    nb = len(bufs)

    def body(*refs):
        for cp in waited(refs[:nb], refs[nb], refs[nb + 1]):
            cp.wait_send()
            cp.wait_recv()
        for cp in started(refs[:nb], refs[nb + 3], refs[nb + 4]):
            cp.start()
        refs[-1][...] = jnp.zeros_like(refs[-1])

    out = pl.pallas_call(
        body, name=name,
        out_shape=(pltpu.SemaphoreType.DMA((n_started,)), pltpu.SemaphoreType.DMA((n_started,)),
                   *[pltpu.HBM(b.shape, b.dtype) for b in bufs], jax.ShapeDtypeStruct((8, 128), F32)),
        in_specs=(_HBM,) * nb + (_SEM, _SEM, pl.BlockSpec(memory_space=pl.ANY)),
        out_specs=(_SEM, _SEM) + (_HBM,) * nb + (pl.BlockSpec(memory_space=pltpu.VMEM),),
        input_output_aliases={i: 2 + i for i in range(nb)},
        compiler_params=pltpu.CompilerParams(has_side_effects=_EFFECT),
    )(*bufs, old_send, old_recv, after)
    return out[0], out[1], list(out[2:2 + nb]), out[-1]


def _chip_copies(refs, send_sems, recv_sems):
    pair_ref, land_ref = refs
    x, y, c = _place()
    chips = [(1 - x, y), (x, 1 - y), (1 - x, 1 - y)]
    return [pltpu.make_async_remote_copy(
        src_ref=pair_ref.at[2 * chip[0] + chip[1]], dst_ref=land_ref.at[k],
        send_sem=send_sems.at[k], recv_sem=recv_sems.at[k],
        device_id=(*chip, c), device_id_type=MESH) for k, chip in enumerate(chips)]


def _first_hop_copies(refs, send_sems, recv_sems):
    pair_ref, land_ref = refs
    x, y, c = _place()
    first = ((x + 1 - c) % 2, (y + c) % 2)
    blocks = [2 * first[0] + first[1], 2 * (1 - x) + (1 - y)]
    return [pltpu.make_async_remote_copy(
        src_ref=pair_ref.at[blocks[k]], dst_ref=land_ref.at[k], send_sem=send_sems.at[k], recv_sem=recv_sems.at[k],
        device_id=(*first, c), device_id_type=MESH) for k in range(2)]


def _second_hop_copies(refs, send_sems, recv_sems):
    relay_ref, land_ref = refs
    x, y, c = _place()
    second = ((x + c) % 2, (y + 1 - c) % 2)
    return [pltpu.make_async_remote_copy(
        src_ref=relay_ref, dst_ref=land_ref.at[0], send_sem=send_sems.at[0], recv_sem=recv_sems.at[0],
        device_id=(*second, c), device_id_type=MESH)]


def _own_block_copies(targets):
    def copies(refs, send_sems, recv_sems):
        x, y, c = _place()
        mine = refs[0].at[4 * x + 2 * y + c]
        return [pltpu.make_async_remote_copy(
            src_ref=mine, dst_ref=mine, send_sem=send_sems.at[k], recv_sem=recv_sems.at[k],
            device_id=to, device_id_type=MESH) for k, to in enumerate(targets(x, y, c))]
    return copies


def _my_core_and_sibling(x, y, c):
    return [(x, y, 1 - c), (1 - x, y, c), (x, 1 - y, c), (1 - x, 1 - y, c)]


def _all_others(x, y, c):
    flip = lambda v, f: 1 - v if f else v
    return [(flip(x, r & 4), flip(y, r & 2), flip(c, r & 1)) for r in range(1, N_DEV)]


def _forward_copies(refs, send_sems, recv_sems):
    x, y, c = _place()
    chips = [(1 - x, y), (x, 1 - y), (1 - x, 1 - y)]
    return [pltpu.make_async_remote_copy(
        src_ref=refs[0].at[4 * chip[0] + 2 * chip[1] + c], dst_ref=refs[0].at[4 * chip[0] + 2 * chip[1] + c],
        send_sem=send_sems.at[k], recv_sem=recv_sems.at[k],
        device_id=(x, y, 1 - c), device_id_type=MESH) for k, chip in enumerate(chips)]


def _first_axis_chip(x, y, c):
    return (x + 1 - c) % 2, (y + c) % 2


def _second_axis_chip(x, y, c):
    return (x + c) % 2, (y + 1 - c) % 2


def _first_targets(x, y, c):
    return [(x, y, 1 - c), (*_first_axis_chip(x, y, c), c)]


def _all_gather_small(shard, name):
    def body(in_ref, out_ref, send_sems, recv_sems, local_sem):
        x, y, c = _place()
        me, sibling = 4 * x + 2 * y + c, (x, y, 1 - c)
        first, second = _first_axis_chip(x, y, c), _second_axis_chip(x, y, c)

        def pair(chip):
            return out_ref.at[pl.ds(2 * (2 * chip[0] + chip[1]), 2)]

        def exchange(k, src, dst, to):
            cp = pltpu.make_async_remote_copy(src_ref=src, dst_ref=dst, send_sem=send_sems.at[k],
                                              recv_sem=recv_sems.at[k], device_id=to, device_id_type=MESH)
            cp.start()
            cp.wait()

        own = pltpu.make_async_copy(in_ref, out_ref.at[me], local_sem)
        own.start()
        exchange(0, in_ref, out_ref.at[me], sibling)
        own.wait()
        exchange(1, pair((x, y)), pair((x, y)), (*second, c))
        exchange(2, pair(second), pair(second), sibling)
        exchange(3, pair(first), pair(first), (*second, c))

    spec = pl.BlockSpec(memory_space=pltpu.VMEM)
    return pl.pallas_call(
        body, name=name, out_shape=jax.ShapeDtypeStruct((N_DEV,) + shard.shape, shard.dtype),
        in_specs=[spec], out_specs=spec,
        scratch_shapes=[pltpu.SemaphoreType.DMA((4,)), pltpu.SemaphoreType.DMA((4,)), pltpu.SemaphoreType.DMA],
        compiler_params=_params(),
    )(shard)


def _slot_copies(refs, send_sems, recv_sems, plan):
    copies = []
    for k, ((px, py, pc), to) in enumerate(plan):
        blk = refs[0].at[4 * px + 2 * py + pc]
        copies.append(pltpu.make_async_remote_copy(
            src_ref=blk, dst_ref=blk, send_sem=send_sems.at[k], recv_sem=recv_sems.at[k],
            device_id=to, device_id_type=MESH))
    return copies


def _second_axis_stage_copies(refs, send_sems, recv_sems):
    x, y, c = _place()
    first, second = (*_first_axis_chip(x, y, c), c), (*_second_axis_chip(x, y, c), c)
    return _slot_copies(refs, send_sems, recv_sems, [((x, y, c), second), (first, (x, y, 1 - c)), (first, second)])


def _second_axis_forward_copies(refs, send_sems, recv_sems):
    x, y, c = _place()
    return _slot_copies(refs, send_sems, recv_sems, [((*_second_axis_chip(x, y, c), c), (x, y, 1 - c))])


def _diagonal_forward_copies(refs, send_sems, recv_sems):
    x, y, c = _place()
    blk = refs[0].at[4 * (1 - x) + 2 * (1 - y) + c]
    return [pltpu.make_async_remote_copy(
        src_ref=blk, dst_ref=blk, send_sem=send_sems.at[0], recv_sem=recv_sems.at[0],
        device_id=(x, y, 1 - c), device_id_type=MESH)]


def _with_own_slot(block, me):
    return lax.dynamic_update_index_in_dim(lax.empty((N_DEV,) + block.shape, block.dtype), block, me, 0)


def _matmul(a, b, dims, out_dtype, tm, tn, name, dep=None):
    if dims == "nn":
        (m, k), n = a.shape, b.shape[1]
        a_spec = pl.BlockSpec((tm, k), lambda i, j: (i, 0))
        b_spec = pl.BlockSpec((k, tn), lambda i, j: (0, j))
        contract = ((1,), (0,))
    elif dims == "nt":
        (m, k), n = a.shape, b.shape[0]
        a_spec = pl.BlockSpec((tm, k), lambda i, j: (i, 0))
        b_spec = pl.BlockSpec((tn, k), lambda i, j: (j, 0))
        contract = ((1,), (1,))
    else:
        (k, m), n = a.shape, b.shape[1]
        a_spec = pl.BlockSpec((k, tm), lambda i, j: (0, i))
        b_spec = pl.BlockSpec((k, tn), lambda i, j: (0, j))
        contract = ((0,), (0,))
    assert m % tm == 0 and n % tn == 0 and a.dtype == BF16 and b.dtype == BF16

    def body(a_ref, b_ref, *rest):
        rest[-1][...] = lax.dot_general(a_ref[...], b_ref[...], (contract, ((), ())),
                                        preferred_element_type=F32).astype(out_dtype)

    deps = [] if dep is None else [dep]
    return pl.pallas_call(
        body, name=name, grid=(m // tm, n // tn),
        in_specs=[a_spec, b_spec] + [pl.BlockSpec((8, 128), lambda i, j: (0, 0))] * len(deps),
        out_specs=pl.BlockSpec((tm, tn), lambda i, j: (i, j)),
        out_shape=jax.ShapeDtypeStruct((m, n), out_dtype),
        compiler_params=_params(dimension_semantics=("arbitrary", "arbitrary")),
    )(a, b, *deps)


Z_TILE = 768
_Z_TILE_ORDER = ((0, 1, 2, 3, 4, 5, 6), (2, 0, 1, 6, 3, 4, 5), (4, 0, 5, 6, 1, 2, 3), (6, 2, 3, 4, 0, 1, 5))
_Z_EARLY_TILES = 4


def _z_proj(h, w_in_t, order, first, count, z_prev, name, dep=None):
    t = h.shape[0]

    def body(order_ref, h_ref, w_ref, *rest):
        rest[-1][...] = _dot_nt(h_ref[...], w_ref[...])

    prev = [] if z_prev is None else [z_prev]
    deps = [] if dep is None else [dep]
    return pl.pallas_call(
        body, name=name,
        grid_spec=pltpu.PrefetchScalarGridSpec(
            num_scalar_prefetch=1, grid=(count,),
            in_specs=[pl.BlockSpec((t, D_MODEL), lambda j, o: (0, 0)),
                      pl.BlockSpec((Z_TILE, D_MODEL), lambda j, o: (o[first + j], 0))]
            + [pl.BlockSpec(memory_space=pl.ANY)] * len(prev)
            + [pl.BlockSpec((8, 128), lambda j, o: (0, 0))] * len(deps),
            out_specs=pl.BlockSpec((t, Z_TILE), lambda j, o: (0, o[first + j]))),
        out_shape=jax.ShapeDtypeStruct((t, D_IN), F32),
        input_output_aliases={3: 0} if prev else {},
        compiler_params=_params(dimension_semantics=("arbitrary",)),
    )(order, h, w_in_t, *prev, *deps)


def _modulation(c_all, w_ada, b_ada_mine):
    def body(c_ref, w_ref, b_ref, act_ref, mod_ref):
        cv = c_ref[...]
        act = cv * _sigmoid(cv)
        act_ref[...] = act
        mod_ref[...] = jnp.dot(act.astype(BF16), w_ref[...].astype(BF16), preferred_element_type=F32) + b_ref[...]

    return pl.pallas_call(
        body, name="modulation",
        out_shape=(jax.ShapeDtypeStruct(c_all.shape, F32), jax.ShapeDtypeStruct((N_DEV, W_ADA_SHARD), F32)),
        compiler_params=_params(),
    )(c_all, w_ada, b_ada_mine)


def _modulated_norm(x, norm_g, scale, shift, tm=256):
    t, d = x.shape

    def body(x_ref, g_ref, sc_ref, sh_ref, h_ref):
        xv = x_ref[...]
        r = lax.rsqrt(jnp.mean(xv * xv, axis=-1, keepdims=True) + EPS)
        h = (xv * r) * g_ref[...] * (1.0 + sc_ref[...]) + sh_ref[...]
        h_ref[...] = h.astype(BF16)

    row = pl.BlockSpec((1, d), lambda i: (0, 0))
    return pl.pallas_call(
        body, name="modulated_norm", grid=(t // tm,),
        in_specs=[pl.BlockSpec((tm, d), lambda i: (i, 0)), row, row, row],
        out_specs=pl.BlockSpec((tm, d), lambda i: (i, 0)),
        out_shape=jax.ShapeDtypeStruct((t, d), BF16),
        compiler_params=_params(dimension_semantics=("arbitrary",)),
    )(x, norm_g, scale, shift)


def _window_bias(block_index):
    s = lax.broadcasted_iota(jnp.int32, (2 * BLOCK, BLOCK), 0)
    t = lax.broadcasted_iota(jnp.int32, (2 * BLOCK, BLOCK), 1)
    valid = ((s < BLOCK) & (s > t) & (block_index > 0)) | ((s >= BLOCK) & ((s - BLOCK) <= t))
    bias = jnp.where(valid, 0.0, -jnp.inf).astype(F32)
    return jnp.concatenate([bias] * 8, axis=1)


def _heads_t(pair_blocks, g):
    top = lax.broadcasted_iota(jnp.int32, (BLOCK, BLOCK), 0) < HEAD_DIM
    zeros = jnp.zeros((HEAD_DIM, BLOCK), F32)
    tiles = []
    for blk in pair_blocks:
        tp = blk.T
        if g == 0:
            tiles += [jnp.where(top, tp, 0.0), jnp.concatenate([tp[HEAD_DIM:], zeros], axis=0)]
        else:
            tiles += [jnp.concatenate([zeros, tp[:HEAD_DIM]], axis=0), jnp.where(top, 0.0, tp)]
    return jnp.concatenate(tiles, axis=1)


def _pair_block(xt, p, g):
    r0 = HEAD_DIM * g
    even = xt[r0:r0 + HEAD_DIM, (2 * p) * BLOCK:(2 * p + 1) * BLOCK]
    odd = xt[r0:r0 + HEAD_DIM, (2 * p + 1) * BLOCK:(2 * p + 2) * BLOCK]
    return jnp.concatenate([even, odd], axis=0).T


def _softmax_t(scores_t, bias, sink):
    st = scores_t + bias
    m = jnp.maximum(jnp.max(st, axis=0, keepdims=True), sink)
    e = jnp.exp(st - m)
    es = jnp.exp(sink - m)
    inv = 1.0 / (jnp.sum(e, axis=0, keepdims=True) + es)
    return e * inv, es * inv


def _dot(a, b):
    return jnp.dot(a, b, preferred_element_type=F32)


def _dot_nt(a, b):
    return lax.dot_general(a, b, (((1,), (1,)), ((), ())), preferred_element_type=F32)


def _layer_norm_fwd(v):
    mu = jnp.mean(v, axis=-1, keepdims=True)
    xc = v - mu
    rstd = lax.rsqrt(jnp.mean(xc * xc, axis=-1, keepdims=True) + EPS)
    return xc * rstd, rstd


def _tril(transposed=False):
    t = lax.broadcasted_iota(jnp.int32, (BLOCK, BLOCK), 0)
    s = lax.broadcasted_iota(jnp.int32, (BLOCK, BLOCK), 1)
    return s >= t if transposed else t >= s


def _const_spec(shape):
    return pl.BlockSpec(shape, lambda i: (0,) * len(shape))


def _kv_prev_spec(index):
    return pl.BlockSpec((BLOCK, 2 * D_KV), lambda i: (jnp.maximum(index(i) - 1, 0), SEG_KV // (2 * D_KV)))


def _keys_values(z_ref, kvp_ref):
    kvp, kvc = kvp_ref[...], z_ref[:, SEG_KV:SEG_KV + 2 * D_KV]
    kk = jnp.concatenate([kvp[:, :D_KV], kvc[:, :D_KV]], axis=0)
    vv = jnp.concatenate([kvp[:, D_KV:], kvc[:, D_KV:]], axis=0)
    return kk, vv


def _pair_cols(g, p, base=0):
    return slice(base + (4 * g + p) * 128, base + (4 * g + p + 1) * 128)


def _mixer_fwd(z, sink_rows, ln_g, ln_b, sgu_w, sgu_bt):
    t = z.shape[0]

    def body(z_ref, kvp_ref, sink_ref, lng_ref, lnb_ref, w_ref, bt_ref, a_ref):
        bias = _window_bias(pl.program_id(0))
        kk, vv = _keys_values(z_ref, kvp_ref)
        kk_b, vvt_b = kk.astype(BF16), vv.T.astype(BF16)
        for g in range(2):
            qt = _heads_t([z_ref[:, _pair_cols(g, p, SEG_Q)] * ATTN_SCALE for p in range(4)], g).astype(BF16)
            prob, _ = _softmax_t(_dot(kk_b, qt), bias, sink_ref[g])
            ot = _dot(vvt_b, prob.astype(BF16))
            for p in range(4):
                gate = z_ref[:, _pair_cols(g, p, SEG_GA)]
                a_ref[:, _pair_cols(g, p)] = (_pair_block(ot, p, g) * (gate * _sigmoid(gate))).astype(BF16)

        vhat, _ = _layer_norm_fwd(z_ref[:, SEG_VS:SEG_VS + D_SGU])
        vn = vhat * lng_ref[...] + lnb_ref[...]
        tril = _tril()
        for g in range(SGU_GROUPS):
            cols = slice(g * 128, (g + 1) * 128)
            wm = jnp.where(tril, w_ref[g], 0.0).astype(BF16)
            mixed = _dot(wm, vn[:, cols].astype(BF16)) + bt_ref[:, g:g + 1]
            gate = z_ref[:, SEG_GS + g * 128:SEG_GS + (g + 1) * 128]
            a_ref[:, D_ATTN + g * 128:D_ATTN + (g + 1) * 128] = (
                (z_ref[:, SEG_U + g * 128:SEG_U + (g + 1) * 128] * mixed) * (gate * _sigmoid(gate))).astype(BF16)

    return pl.pallas_call(
        body, name="mixer_fwd", grid=(t // BLOCK,),
        in_specs=[pl.BlockSpec((BLOCK, D_IN), lambda i: (i, 0)), _kv_prev_spec(lambda i: i),
                  _const_spec((2, 1, 8 * BLOCK)), _const_spec((1, D_SGU)), _const_spec((1, D_SGU)),
                  _const_spec((SGU_GROUPS, BLOCK, BLOCK)), _const_spec((BLOCK, SGU_GROUPS))],
        out_specs=pl.BlockSpec((BLOCK, D_MODEL), lambda i: (i, 0)),
        out_shape=jax.ShapeDtypeStruct((t, D_MODEL), BF16),
        compiler_params=_params(dimension_semantics=("arbitrary",)),
    )(z, z, sink_rows, ln_g, ln_b, sgu_w, sgu_bt)


def _mixer_bwd(z, da, sink_rows, ln_g, ln_b, sgu_w, sgu_wt, sgu_bt):
    t = z.shape[0]
    nb = t // BLOCK

    def body(z_ref, kvp_ref, da_ref, sink_ref, lng_ref, lnb_ref, w_ref, wt_ref, bt_ref,
             dz_ref, dsink_ref, dw_ref, db_ref, dlng_ref, dlnb_ref, carry_ref, dsink_acc, dbt_acc):
        step = pl.program_id(0)

        @pl.when(step == 0)
        def _():
            carry_ref[...] = jnp.zeros_like(carry_ref)
            dsink_acc[...] = jnp.zeros_like(dsink_acc)
            dbt_acc[...] = jnp.zeros_like(dbt_acc)
            dw_ref[...] = jnp.zeros_like(dw_ref)
            dlng_ref[...] = jnp.zeros_like(dlng_ref)
            dlnb_ref[...] = jnp.zeros_like(dlnb_ref)

        bias = _window_bias(nb - 1 - step)
        kk, vv = _keys_values(z_ref, kvp_ref)
        kk_b, vv_b = kk.astype(BF16), vv.astype(BF16)
        kkt_b, vvt_b = kk.T.astype(BF16), vv.T.astype(BF16)
        dkk = jnp.zeros((2 * BLOCK, D_KV), F32)
        dvv = jnp.zeros((2 * BLOCK, D_KV), F32)
        for g in range(2):
            qt = _heads_t([z_ref[:, _pair_cols(g, p, SEG_Q)] * ATTN_SCALE for p in range(4)], g).astype(BF16)
            prob, sink_prob = _softmax_t(_dot(kk_b, qt), bias, sink_ref[g])
            prob_b = prob.astype(BF16)
            ot = _dot(vvt_b, prob_b)
            gates = [z_ref[:, _pair_cols(g, p, SEG_GA)] for p in range(4)]
            sig = [_sigmoid(gt) for gt in gates]
            d_attn = [da_ref[:, _pair_cols(g, p)] for p in range(4)]
            d_ot = _heads_t([d_attn[p] * (gates[p] * sig[p]) for p in range(4)], g).astype(BF16)
            d_prob = _dot(vv_b, d_ot)
            delta = jnp.sum(prob * d_prob, axis=0, keepdims=True)
            d_scores = (prob * (d_prob - delta)).astype(BF16)
            dsink_acc[g] -= sink_prob * delta
            d_qt = _dot(kkt_b, d_scores)
            dkk = dkk + _dot_nt(d_scores, qt)
            dvv = dvv + _dot_nt(prob_b, d_ot)
            for p in range(4):
                dz_ref[:, _pair_cols(g, p, SEG_Q)] = (_pair_block(d_qt, p, g) * ATTN_SCALE).astype(BF16)
                d_silu = sig[p] * (1.0 + gates[p] * (1.0 - sig[p]))
                dz_ref[:, _pair_cols(g, p, SEG_GA)] = (d_attn[p] * _pair_block(ot, p, g) * d_silu).astype(BF16)
        d_kv = jnp.concatenate([dkk, dvv], axis=1)
        dz_ref[:, SEG_KV:SEG_KV + 2 * D_KV] = (d_kv[BLOCK:] + carry_ref[...]).astype(BF16)
        carry_ref[...] = d_kv[:BLOCK]

        vhat, rstd = _layer_norm_fwd(z_ref[:, SEG_VS:SEG_VS + D_SGU])
        lng = lng_ref[...]
        vn = vhat * lng + lnb_ref[...]
        tril, triu = _tril(), _tril(transposed=True)
        lane = lax.broadcasted_iota(jnp.int32, (BLOCK, 128), 1)
        d_bt = jnp.zeros((BLOCK, 128), F32)
        d_vn = []
        for g in range(SGU_GROUPS):
            cols = slice(g * 128, (g + 1) * 128)
            wm = jnp.where(tril, w_ref[g], 0.0).astype(BF16)
            wmt = jnp.where(triu, wt_ref[g], 0.0).astype(BF16)
            vn_g = vn[:, cols].astype(BF16)
            mixed = _dot(wm, vn_g) + bt_ref[:, g:g + 1]
            gate = z_ref[:, SEG_GS + g * 128:SEG_GS + (g + 1) * 128]
            u = z_ref[:, SEG_U + g * 128:SEG_U + (g + 1) * 128]
            d_out = da_ref[:, D_ATTN + g * 128:D_ATTN + (g + 1) * 128]
            sg = _sigmoid(gate)
            d_um = d_out * (gate * sg)
            dz_ref[:, SEG_U + g * 128:SEG_U + (g + 1) * 128] = (d_um * mixed).astype(BF16)
            dz_ref[:, SEG_GS + g * 128:SEG_GS + (g + 1) * 128] = (
                d_out * (u * mixed) * (sg * (1.0 + gate * (1.0 - sg)))).astype(BF16)
            d_mixed = d_um * u
            d_mixed_b = d_mixed.astype(BF16)
            dw_ref[g] += jnp.where(tril, _dot_nt(d_mixed_b, vn_g), 0.0)
            d_bt = d_bt + jnp.where(lane == g, jnp.sum(d_mixed, axis=-1, keepdims=True), 0.0)
            d_vn.append(_dot(wmt, d_mixed_b))
        dbt_acc[...] += d_bt
        d_vn = jnp.concatenate(d_vn, axis=1)
        dlng_ref[...] += jnp.sum(d_vn * vhat, axis=0, keepdims=True)
        dlnb_ref[...] += jnp.sum(d_vn, axis=0, keepdims=True)
        d_vhat = d_vn * lng
        d_v = rstd * (d_vhat - jnp.mean(d_vhat, axis=-1, keepdims=True)
                      - vhat * jnp.mean(d_vhat * vhat, axis=-1, keepdims=True))
        dz_ref[:, SEG_VS:SEG_VS + D_SGU] = d_v.astype(BF16)

        @pl.when(step == nb - 1)
        def _():
            db_ref[...] = dbt_acc[...].T[:SGU_GROUPS]
            lane_row = lax.broadcasted_iota(jnp.int32, (1, 128), 1)
            d_sink = jnp.zeros((1, 128), F32)
            for g in range(2):
                acc = dsink_acc[g]
                for j in range(8):
                    head_sum = jnp.sum(acc[:, j * BLOCK:(j + 1) * BLOCK], axis=-1, keepdims=True)
                    d_sink = d_sink + jnp.where(lane_row == 8 * g + j, head_sum, 0.0)
            dsink_ref[...] = d_sink

    rev = lambda i: nb - 1 - i
    return pl.pallas_call(
        body, name="mixer_bwd", grid=(nb,),
        in_specs=[pl.BlockSpec((BLOCK, D_IN), lambda i: (rev(i), 0)), _kv_prev_spec(rev),
                  pl.BlockSpec((BLOCK, D_MODEL), lambda i: (rev(i), 0)),
                  _const_spec((2, 1, 8 * BLOCK)), _const_spec((1, D_SGU)), _const_spec((1, D_SGU)),
                  _const_spec((SGU_GROUPS, BLOCK, BLOCK)), _const_spec((SGU_GROUPS, BLOCK, BLOCK)),
                  _const_spec((BLOCK, SGU_GROUPS))],
        out_specs=(pl.BlockSpec((BLOCK, D_IN), lambda i: (rev(i), 0)), _const_spec((1, 128)),
                   _const_spec((SGU_GROUPS, BLOCK, BLOCK)), _const_spec((SGU_GROUPS, BLOCK)),
                   _const_spec((1, D_SGU)), _const_spec((1, D_SGU))),
        out_shape=(jax.ShapeDtypeStruct((t, D_IN), BF16), jax.ShapeDtypeStruct((1, 128), F32),
                   jax.ShapeDtypeStruct((SGU_GROUPS, BLOCK, BLOCK), F32), jax.ShapeDtypeStruct((SGU_GROUPS, BLOCK), F32),
                   jax.ShapeDtypeStruct((1, D_SGU), F32), jax.ShapeDtypeStruct((1, D_SGU), F32)),
        scratch_shapes=[pltpu.VMEM((BLOCK, 2 * D_KV), F32), pltpu.VMEM((2, 1, 8 * BLOCK), F32),
                        pltpu.VMEM((BLOCK, 128), F32)],
        compiler_params=_params(dimension_semantics=("arbitrary",)),
    )(z, z, da, sink_rows, ln_g, ln_b, sgu_w, sgu_wt, sgu_bt)


def _out_proj_head(a, w_out_full, x, target, gate, final_g, tm=256):
    t, d = x.shape

    def body(a_ref, w_ref, x_ref, tg_ref, gate_ref, fg_ref, dx2_ref, dy_ref, loss_ref, dfg_ref, dgate_ref):
        @pl.when(pl.program_id(0) == 0)
        def _():
            loss_ref[...] = jnp.zeros_like(loss_ref)
            dfg_ref[...] = jnp.zeros_like(dfg_ref)
            dgate_ref[...] = jnp.zeros_like(dgate_ref)

        yv, gate, fg = _dot(a_ref[...], w_ref[...]), gate_ref[...], fg_ref[...]
        x2 = x_ref[...] + gate * yv
        r2 = lax.rsqrt(jnp.mean(x2 * x2, axis=-1, keepdims=True) + EPS)
        nrm = x2 * r2
        err = nrm * fg - tg_ref[...]
        loss_ref[...] += 0.5 * jnp.sum(jnp.mean(err * err, axis=-1, keepdims=True), axis=0, keepdims=True)
        d_out = err * (1.0 / d)
        dfg_ref[...] += jnp.sum(d_out * nrm, axis=0, keepdims=True)
        d_nrm = d_out * fg
        dx2 = r2 * (d_nrm - nrm * jnp.mean(d_nrm * nrm, axis=-1, keepdims=True))
        dx2_ref[...] = dx2
        dgate_ref[...] += jnp.sum(dx2 * yv, axis=0, keepdims=True)
        dy_ref[...] = (dx2 * gate).astype(BF16)

    blk = pl.BlockSpec((tm, d), lambda i: (i, 0))
    row = _const_spec((1, d))
    whole = pl.BlockSpec(w_out_full.shape, lambda i: (0, 0), pipeline_mode=pl.Buffered(1))
    return pl.pallas_call(
        body, name="out_proj_head", grid=(t // tm,),
        in_specs=[pl.BlockSpec((tm, a.shape[1]), lambda i: (i, 0)), whole, blk, blk, row, row],
        out_specs=(blk, blk, _const_spec((1, 128)), row, row),
        out_shape=(jax.ShapeDtypeStruct((t, d), F32), jax.ShapeDtypeStruct((t, d), BF16),
                   jax.ShapeDtypeStruct((1, 128), F32), jax.ShapeDtypeStruct((1, d), F32),
                   jax.ShapeDtypeStruct((1, d), F32)),
        compiler_params=_params(dimension_semantics=("arbitrary",)),
    )(a, w_out_full, x, target, gate, final_g)


def _z_proj_bwd_norm(dz, w_in_t, x, dx2, norm_g, scale, dep, tm=256):
    t, d = x.shape

    def body(dz_ref, w_ref, x_ref, dx2_ref, g_ref, sc_ref, dep_ref, gx_ref, dshift_ref, dscale_ref, dg_ref):
        @pl.when(pl.program_id(0) == 0)
        def _():
            dshift_ref[...] = jnp.zeros_like(dshift_ref)
            dscale_ref[...] = jnp.zeros_like(dscale_ref)
            dg_ref[...] = jnp.zeros_like(dg_ref)

        dh, xv, g = _dot(dz_ref[...], w_ref[...]), x_ref[...], g_ref[...]
        one_plus = 1.0 + sc_ref[...]
        r = lax.rsqrt(jnp.mean(xv * xv, axis=-1, keepdims=True) + EPS)
        xn = xv * r
        dshift_ref[...] += jnp.sum(dh, axis=0, keepdims=True)
        dscale_ref[...] += jnp.sum(dh * (xn * g), axis=0, keepdims=True)
        d_y = dh * one_plus
        dg_ref[...] += jnp.sum(d_y * xn, axis=0, keepdims=True)
        d_xn = d_y * g
        gx_ref[...] = dx2_ref[...] + r * (d_xn - xn * jnp.mean(d_xn * xn, axis=-1, keepdims=True))

    blk = pl.BlockSpec((tm, d), lambda i: (i, 0))
    row = _const_spec((1, d))
    whole = pl.BlockSpec(w_in_t.shape, lambda i: (0, 0), pipeline_mode=pl.Buffered(1))
    return pl.pallas_call(
        body, name="z_proj_bwd_norm", grid=(t // tm,),
        in_specs=[pl.BlockSpec((tm, dz.shape[1]), lambda i: (i, 0)), whole, blk, blk, row, row, _const_spec((8, 128))],
        out_specs=(blk, row, row, row),
        out_shape=(jax.ShapeDtypeStruct((t, d), F32),) + (jax.ShapeDtypeStruct((1, d), F32),) * 3,
        compiler_params=_params(dimension_semantics=("arbitrary",)),
    )(dz, w_in_t, x, dx2, norm_g, scale, dep)


def _adamw(w, g, m, v):
    m = ADAM_B1 * m + (1.0 - ADAM_B1) * g
    v = ADAM_B2 * v + (1.0 - ADAM_B2) * (g * g)
    m_hat = m / (1.0 - ADAM_B1 ** ADAM_STEP)
    v_hat = v / (1.0 - ADAM_B2 ** ADAM_STEP)
    delta = -ADAM_LR * (m_hat / (jnp.sqrt(v_hat) + ADAM_EPS) + ADAM_WD * w)
    return delta, m, v


def _relay_sum(second_chip, pair, land, tr):
    _, r, c = pair.shape

    def body(chip_ref, a_ref, b_ref, o_ref):
        o_ref[...] = (a_ref[...].astype(F32) + b_ref[...].astype(F32)).astype(BF16)

    return pl.pallas_call(
        body, name="w_in_grad_relay_sum",
        grid_spec=pltpu.PrefetchScalarGridSpec(
            num_scalar_prefetch=1, grid=(r // tr,),
            in_specs=[pl.BlockSpec((None, tr, c), lambda i, chip_ref: (chip_ref[0], i, 0)),
                      pl.BlockSpec((None, tr, c), lambda i, chip_ref: (1, i, 0))],
            out_specs=pl.BlockSpec((tr, c), lambda i, chip_ref: (i, 0))),
        out_shape=jax.ShapeDtypeStruct((r, c), BF16),
        compiler_params=_params(dimension_semantics=("arbitrary",)),
    )(second_chip, pair, land)


def _adam_from_chips(chip, pair, landed, w, m, v, name, tc):
    _, r, c = pair.shape
    n = len(landed)

    def body(chip_ref, own_ref, *refs):
        w_ref, m_ref, v_ref, g_ref, d_ref, nm_ref, nv_ref = refs[n:]
        g = own_ref[...].astype(F32)
        for k in range(n):
            g = g + refs[k][...].astype(F32)
        g_ref[...] = g
        d_ref[...], nm_ref[...], nv_ref[...] = _adamw(w_ref[...], g, m_ref[...], v_ref[...])

    def landed_spec(index):
        return pl.BlockSpec((None, r, tc), lambda i, chip_ref: (index, 0, i))

    blk = pl.BlockSpec((r, tc), lambda i, chip_ref: (0, i))
    return pl.pallas_call(
        body, name=name,
        grid_spec=pltpu.PrefetchScalarGridSpec(
            num_scalar_prefetch=1, grid=(c // tc,),
            in_specs=[pl.BlockSpec((None, r, tc), lambda i, chip_ref: (chip_ref[0], 0, i))]
            + [landed_spec(index) for _, index in landed] + [blk, blk, blk],
            out_specs=(blk,) * 4),
        out_shape=(jax.ShapeDtypeStruct((r, c), F32),) * 4,
        compiler_params=_params(dimension_semantics=("arbitrary",)),
    )(chip, pair, *[array for array, _ in landed], w, m, v)


def _adam_w_ada(act_t, dmod_mine, w, m, v, tr=256):
    r, c = w.shape

    def body(a_ref, dm_ref, w_ref, m_ref, v_ref, g_ref, d_ref, nm_ref, nv_ref):
        g = _dot(a_ref[...].astype(BF16), dm_ref[...].astype(BF16))
        g_ref[...] = g
        d_ref[...], nm_ref[...], nv_ref[...] = _adamw(w_ref[...], g, m_ref[...], v_ref[...])

    blk = pl.BlockSpec((tr, c), lambda i: (i, 0))
    return pl.pallas_call(
        body, name="adam_w_ada", grid=(r // tr,),
        in_specs=[pl.BlockSpec((tr, N_DEV), lambda i: (i, 0)), _const_spec((N_DEV, c)), blk, blk, blk],
        out_specs=(blk,) * 4, out_shape=(jax.ShapeDtypeStruct((r, c), F32),) * 4,
        compiler_params=_params(dimension_semantics=("arbitrary",)),
    )(act_t, dmod_mine, w, m, v)


def _pack_small(d_shift, d_scale, d_gate, d_norm_g, d_final_g, d_ln_g, d_ln_b, loss, d_sinks, d_sgu_b):
    def body(shift_ref, scale_ref, gate_ref, ng_ref, fg_ref, lng_ref, lnb_ref, loss_ref, sink_ref, b_ref, o_ref):
        o_ref[...] = jnp.zeros_like(o_ref)
        o_ref[ROW_SHIFT:ROW_SHIFT + 1, :] = shift_ref[...]
        o_ref[ROW_SCALE:ROW_SCALE + 1, :] = scale_ref[...]
        o_ref[ROW_GATE:ROW_GATE + 1, :] = gate_ref[...]
        o_ref[ROW_NORM_G:ROW_NORM_G + 1, :] = ng_ref[...]
        o_ref[ROW_FINAL_G:ROW_FINAL_G + 1, :] = fg_ref[...]
        o_ref[ROW_LN:ROW_LN + 1, 0:D_SGU] = lng_ref[...]
        o_ref[ROW_LN:ROW_LN + 1, D_SGU:2 * D_SGU] = lnb_ref[...]
        o_ref[ROW_MISC:ROW_MISC + 1, 0:128] = loss_ref[...]
        o_ref[ROW_MISC:ROW_MISC + 1, 128:256] = sink_ref[...]
        o_ref[ROW_SGU_B:ROW_SGU_B + SGU_GROUPS, 0:BLOCK] = b_ref[...]

    return pl.pallas_call(
        body, name="pack_small", out_shape=jax.ShapeDtypeStruct((SMALL_ROWS, D_MODEL), F32),
        compiler_params=_params(),
    )(d_shift, d_scale, d_gate, d_norm_g, d_final_g, d_ln_g, d_ln_b, loss, d_sinks, d_sgu_b)


_SMALL_NAMES = ("norm_g", "b_ada", "attn_sinks", "sgu_ln_g", "sgu_ln_b", "sgu_w", "sgu_b", "final_g")


def _adam_small(partials, d_sgu_w_all, weights, moments_m, moments_v):
    names = _SMALL_NAMES
    k = len(names)

    def body(*refs):
        p_ref, sw_ref = refs[0], refs[1]
        w_refs, m_refs, v_refs = refs[2:2 + k], refs[2 + k:2 + 2 * k], refs[2 + 2 * k:2 + 3 * k]
        loss_ref, dmod_ref = refs[2 + 3 * k], refs[3 + 3 * k]
        out_refs = refs[4 + 3 * k:4 + 7 * k]
        sum_ref = refs[4 + 7 * k]
        total = p_ref[0]
        for j in range(1, N_DEV):
            total = total + p_ref[j]
        sum_ref[...] = total
        for j in range(N_DEV):
            for part, row in enumerate((ROW_SHIFT, ROW_SCALE, ROW_GATE)):
                dmod_ref[j:j + 1, part * D_MODEL:(part + 1) * D_MODEL] = p_ref[j, row:row + 1, :]
        loss_ref[...] = sum_ref[ROW_MISC:ROW_MISC + 1, 0:1]
        d_sgu_w = sw_ref[0]
        for j in range(1, N_DEV):
            d_sgu_w = d_sgu_w + sw_ref[j]
        grads = {
            "norm_g": sum_ref[ROW_NORM_G:ROW_NORM_G + 1, :],
            "b_ada": jnp.concatenate([sum_ref[r:r + 1, :] for r in (ROW_SHIFT, ROW_SCALE, ROW_GATE)], axis=1),
            "attn_sinks": sum_ref[ROW_MISC:ROW_MISC + 1, 128:128 + N_Q_HEADS],
            "sgu_ln_g": sum_ref[ROW_LN:ROW_LN + 1, 0:D_SGU],
            "sgu_ln_b": sum_ref[ROW_LN:ROW_LN + 1, D_SGU:2 * D_SGU],
            "sgu_w": d_sgu_w[None],
            "sgu_b": sum_ref[ROW_SGU_B:ROW_SGU_B + SGU_GROUPS, 0:BLOCK][None],
            "final_g": sum_ref[ROW_FINAL_G:ROW_FINAL_G + 1, :],
        }
        for i, name in enumerate(names):
            g = grads[name]
            delta, m, v = _adamw(w_refs[i][...], g, m_refs[i][...], v_refs[i][...])
            out_refs[4 * i][...] = g
            out_refs[4 * i + 1][...] = delta
            out_refs[4 * i + 2][...] = m
            out_refs[4 * i + 3][...] = v

    shapes = [jax.ShapeDtypeStruct((1, 1), F32), jax.ShapeDtypeStruct((N_DEV, 3 * D_MODEL), F32)]
    for name in names:
        shapes += [jax.ShapeDtypeStruct(weights[name].shape, F32)] * 4
    outs = pl.pallas_call(
        body, name="adam_small", out_shape=tuple(shapes),
        scratch_shapes=[pltpu.VMEM((SMALL_ROWS, D_MODEL), F32)],
        compiler_params=_params(),
    )(partials, d_sgu_w_all, *[weights[n] for n in names], *[moments_m[n] for n in names],
      *[moments_v[n] for n in names])
    return outs[0], outs[1], {name: outs[2 + 4 * i:6 + 4 * i] for i, name in enumerate(names)}


def kernel(x, c, norm_g, w_ada, b_ada, w_in, attn_sinks, sgu_ln_g, sgu_ln_b, sgu_w, sgu_b, w_out, final_g, loss_target, m_norm_g, m_w_ada, m_b_ada, m_w_in, m_attn_sinks, m_sgu_ln_g, m_sgu_ln_b, m_sgu_w, m_sgu_b, m_w_out, m_final_g, v_norm_g, v_w_ada, v_b_ada, v_w_in, v_attn_sinks, v_sgu_ln_g, v_sgu_ln_b, v_sgu_w, v_sgu_b, v_w_out, v_final_g):
    xi, yi, ci = _place()
    me = 4 * xi + 2 * yi + ci
    x2d, target = x[0], loss_target[0]
    t = x2d.shape[0]

    core = ci.astype(jnp.int32).reshape(1)
    chip = (2 * xi + yi).astype(jnp.int32).reshape(1)

    first = _own_block_copies(_first_targets)
    first_flight = _start_copies([_with_own_slot(w_in[0].T.astype(BF16), me)], first, 2, core, "gather_w_in_start")

    c_all = _all_gather_small(c.reshape(8, 256) + first_flight[3][0, 0], "gather_c").reshape(N_DEV, D_MODEL)
    b_mine = lax.dynamic_slice(b_ada, (0, me * W_ADA_SHARD), (1, W_ADA_SHARD))
    c_act, mod_part = _modulation(c_all, w_ada[0], b_mine)
    mod_all = _all_gather_small(mod_part, "gather_mod")

    across = _wait_then_start(first_flight, lambda *a: first(*a)[1:], _second_axis_stage_copies, 3, mod_all,
                              "gather_w_in_second_axis_stage")
    mod = lax.dynamic_index_in_dim(mod_all, me, axis=1, keepdims=False).reshape(1, 3 * D_MODEL)
    mod = mod + across[3][0, 0]
    shift, scale, gate = mod[:, :D_MODEL], mod[:, D_MODEL:2 * D_MODEL], mod[:, 2 * D_MODEL:]
    h = _modulated_norm(x2d, norm_g, scale, shift)

    w_in_pair = _wait_copies((first_flight[0], first_flight[1], across[2], None), lambda *a: first(*a)[:1], h,
                             "gather_w_in_sibling_wait")
    tile_order = jnp.asarray(_Z_TILE_ORDER, jnp.int32)[chip[0]]
    z_own = _z_proj(h, w_in_pair[0].reshape(D_IN, D_MODEL), tile_order, 0, 1, None, "z_proj_own")
    forward = _wait_then_start((across[0], across[1], w_in_pair, None), lambda *a: _second_axis_stage_copies(*a)[:1],
                               _second_axis_forward_copies, 1, z_own, "gather_w_in_second_axis_forward")
    w_out_flight = _start_copies([_with_own_slot(w_out[0].astype(BF16), me)], _own_block_copies(_my_core_and_sibling),
                                 4, forward[3], "gather_w_out_start")
    w_in_most = _wait_copies((across[0], across[1], forward[2], None),
                             lambda *a: _second_axis_stage_copies(*a)[1:2], w_out_flight[3],
                             "gather_w_in_first_forward_wait")
    w_in_most = _wait_copies((forward[0], forward[1], w_in_most, None), _second_axis_forward_copies, z_own,
                             "gather_w_in_second_forward_wait")
    z_early = _z_proj(h, w_in_most[0].reshape(D_IN, D_MODEL), tile_order, 1, _Z_EARLY_TILES - 1, z_own, "z_proj_early")
    w_in_flight = _wait_then_start((across[0], across[1], w_in_most, None),
                                   lambda *a: _second_axis_stage_copies(*a)[2:], _diagonal_forward_copies, 1,
                                   z_early, "gather_w_in_last_stage")
    w_in_all = _wait_copies(w_in_flight, _diagonal_forward_copies, z_early, "gather_w_in_last_wait")[0]
    w_in_t = w_in_all.reshape(D_IN, D_MODEL)
    z = _z_proj(h, w_in_t, tile_order, _Z_EARLY_TILES, 7 - _Z_EARLY_TILES, z_early, "z_proj_late")
    w_out_flight = _wait_then_start(w_out_flight, _own_block_copies(_my_core_and_sibling), _forward_copies, 3, z,
                                    "gather_w_out_forward_stage")
    sink_rows = jnp.repeat(attn_sinks.reshape(N_Q_HEADS), BLOCK).reshape(2, 1, 8 * BLOCK)
    sgu_bt = sgu_b[0].T
    a = _mixer_fwd(z, sink_rows + w_out_flight[3][0, 0], sgu_ln_g, sgu_ln_b, sgu_w[0], sgu_bt)
    w_out_all = _wait_copies(w_out_flight, _forward_copies, a, "gather_w_out_forward_wait")[0]
    w_out_full = w_out_all.reshape(D_MODEL, D_MODEL)
    final_g_row = final_g.reshape(1, D_MODEL)
    dx2, dy, loss_part, d_final_g, d_gate = _out_proj_head(a, w_out_full, x2d, target, gate, final_g_row)

    da = _matmul(dy, w_out_full, "nt", F32, min(t, 1024), 1024, "out_proj_bwd")
    dw_out = _matmul(a, dy, "tn", BF16, 1024, 1024, "w_out_grad").reshape(4, 2, W_OUT_SHARD, D_MODEL)
    pair_out = _pair_reduce(dw_out, "w_out_grad_pair_reduce", W_OUT_SHARD // 2)
    out_flight = _start_copies([pair_out, lax.empty((3, W_OUT_SHARD, D_MODEL), BF16)], _chip_copies, 3, core,
                               "w_out_grad_chip_start")
    dz, d_sinks, d_sgu_w, d_sgu_b, d_ln_g, d_ln_b = _mixer_bwd(
        z, da, sink_rows + out_flight[3][0, 0], sgu_ln_g, sgu_ln_b, sgu_w[0], jnp.swapaxes(sgu_w[0], 1, 2), sgu_bt)
    sgu_w_flight = _start_copies([_with_own_slot(d_sgu_w, me)], _own_block_copies(_all_others), N_DEV - 1, core,
                                 "sgu_w_grad_gather_start")
    dw_in_t = _matmul(dz, h, "tn", BF16, 768, D_MODEL, "w_in_grad", dep=sgu_w_flight[3])
    dw_in_t = dw_in_t.reshape(4, 2, W_IN_SHARD, D_MODEL)
    pair_in = _pair_reduce(dw_in_t, "w_in_grad_pair_reduce", W_IN_SHARD // 3)
    hop1 = _start_copies([pair_in, lax.empty((2, W_IN_SHARD, D_MODEL), BF16)], _first_hop_copies, 2, core,
                         "w_in_grad_first_hop_start")
    grad_x, d_shift, d_scale, d_norm_g = _z_proj_bwd_norm(dz, w_in_t, x2d, dx2, norm_g, scale, hop1[3])

    partial = _pack_small(d_shift, d_scale, d_gate, d_norm_g, d_final_g, d_ln_g, d_ln_b, loss_part, d_sinks, d_sgu_b)
    small_flight = _start_copies([_with_own_slot(partial, me)], _own_block_copies(_all_others), N_DEV - 1, core,
                                 "small_grad_gather_start")
    pair_in, land_first = _wait_copies(hop1, _first_hop_copies, small_flight[3], "w_in_grad_first_hop_wait")
    second_chip = (2 * ((xi + ci) % 2) + (yi + 1 - ci) % 2).astype(jnp.int32).reshape(1)
    relay = _relay_sum(second_chip, pair_in, land_first, W_IN_SHARD // 3)
    hop2 = _start_copies([relay, lax.empty((1, W_IN_SHARD, D_MODEL), BF16)], _second_hop_copies, 1, core,
                         "w_in_grad_second_hop_start")
    pair_out, land_out = _wait_copies(out_flight, _chip_copies, hop2[3], "w_out_grad_chip_wait")
    big = {"w_out": _adam_from_chips(chip, pair_out, [(land_out, k) for k in range(3)], w_out[0], m_w_out[0],
                                     v_w_out[0], "adam_w_out", 1024)}
    partial_all = _wait_copies(small_flight, _own_block_copies(_all_others), big["w_out"][0],
                               "small_grad_gather_wait")[0]
    d_sgu_w_all = _wait_copies(sgu_w_flight, _own_block_copies(_all_others), partial_all, "sgu_w_grad_gather_wait")[0]
    weights = {"norm_g": norm_g, "b_ada": b_ada, "attn_sinks": attn_sinks, "sgu_ln_g": sgu_ln_g,
               "sgu_ln_b": sgu_ln_b, "sgu_w": sgu_w, "sgu_b": sgu_b, "final_g": final_g_row}
    moments_m = {"norm_g": m_norm_g, "b_ada": m_b_ada, "attn_sinks": m_attn_sinks, "sgu_ln_g": m_sgu_ln_g,
                 "sgu_ln_b": m_sgu_ln_b, "sgu_w": m_sgu_w, "sgu_b": m_sgu_b,
                 "final_g": m_final_g.reshape(1, D_MODEL)}
    moments_v = {"norm_g": v_norm_g, "b_ada": v_b_ada, "attn_sinks": v_attn_sinks, "sgu_ln_g": v_sgu_ln_g,
                 "sgu_ln_b": v_sgu_ln_b, "sgu_w": v_sgu_w, "sgu_b": v_sgu_b,
                 "final_g": v_final_g.reshape(1, D_MODEL)}
    loss, dmod_all, small = _adam_small(partial_all, d_sgu_w_all, weights, moments_m, moments_v)
    small["final_g"] = tuple(o.reshape(D_MODEL) for o in small["final_g"])

    dmod_mine = lax.dynamic_slice(dmod_all, (0, me * W_ADA_SHARD), (N_DEV, W_ADA_SHARD))
    big["w_ada"] = _adam_w_ada(c_act.T, dmod_mine, w_ada[0], m_w_ada[0], v_w_ada[0])
    _, land_second = _wait_copies(hop2, _second_hop_copies, big["w_ada"][0], "w_in_grad_second_hop_wait")
    big["w_in"] = tuple(o.T for o in _adam_from_chips(
        chip, pair_in, [(land_first, 0), (land_second, 0)], w_in[0].T, m_w_in[0].T, v_w_in[0].T, "adam_w_in", 256))
    order = ["norm_g", "w_ada", "b_ada", "w_in", "attn_sinks", "sgu_ln_g", "sgu_ln_b", "sgu_w", "sgu_b", "w_out",
             "final_g"]
    outs = [loss.reshape(()), grad_x[None]]
    for k in range(4):
        for name in order:
            outs.append(big[name][k][None] if name in big else small[name][k])
    return tuple(outs)
```

```python
import jax
import jax.numpy as jnp
from jax import lax
from jax.experimental import pallas as pl
from jax.experimental.pallas import tpu as pltpu

F32 = jnp.float32
BF16 = jnp.bfloat16
MESH = pl.DeviceIdType.MESH

N_DEV = 8
D_MODEL = 2048
HEAD_DIM = 64
D_ATTN = 1024
N_Q_HEADS = 16
D_KV = 128
BLOCK = 128
D_SGU = 1024
SGU_GROUPS = 8
D_IN = 5376
W_IN_SHARD = D_IN // N_DEV
W_OUT_SHARD = D_MODEL // N_DEV
W_ADA_SHARD = 3 * D_MODEL // N_DEV
EPS = 1e-6
ATTN_SCALE = 0.125

ADAM_LR = 0.001
ADAM_B1 = 0.9
ADAM_B2 = 0.999
ADAM_EPS = 1e-08
ADAM_WD = 0.01
ADAM_STEP = 10

SEG_Q, SEG_KV, SEG_GA, SEG_U, SEG_VS, SEG_GS = 0, 1024, 1280, 2304, 3328, 4352

VMEM_LIMIT = 56 * 1024 * 1024

ROW_SHIFT, ROW_SCALE, ROW_GATE, ROW_NORM_G, ROW_FINAL_G, ROW_LN, ROW_MISC, ROW_SGU_B = 0, 1, 2, 3, 4, 5, 6, 8
SMALL_ROWS = 16


def _params(**kw):
    return pltpu.CompilerParams(vmem_limit_bytes=VMEM_LIMIT, **kw)


def _sigmoid(x):
    return 0.5 * (jnp.tanh(0.5 * x) + 1.0)


def _place():
    return lax.axis_index("x"), lax.axis_index("y"), lax.axis_index("c")


def _pair_reduce(blocks, name, row_chunk):
    _, _, r, cols = blocks.shape
    assert r % row_chunk == 0

    def body(in_ref, out_ref, land, own, summed, send_sems, recv_sems, own_sems, out_sems):
        x, y, c = _place()
        sends, loads, stores = [], [], []
        for m in range(4):
            cp = pltpu.make_async_remote_copy(
                src_ref=in_ref.at[m, 1 - c], dst_ref=land.at[m], send_sem=send_sems.at[m], recv_sem=recv_sems.at[m],
                device_id=(x, y, 1 - c), device_id_type=MESH)
            cp.start()
            sends.append(cp)
            ld = pltpu.make_async_copy(in_ref.at[m, c], own.at[m], own_sems.at[m])
            ld.start()
            loads.append(ld)
        for m in range(4):
            sends[m].wait_recv()
            loads[m].wait()
            for k in range(r // row_chunk):
                rows = slice(k * row_chunk, (k + 1) * row_chunk)
                summed[m, rows, :] = (own[m, rows, :].astype(F32) + land[m, rows, :].astype(F32)).astype(BF16)
            st = pltpu.make_async_copy(summed.at[m], out_ref.at[m], out_sems.at[m])
            st.start()
            stores.append(st)
        for m in range(4):
            sends[m].wait_send()
            stores[m].wait()

    spec = pl.BlockSpec(memory_space=pl.ANY)
    return pl.pallas_call(
        body, name=name, out_shape=jax.ShapeDtypeStruct((4, r, cols), BF16),
        in_specs=[spec], out_specs=spec,
        scratch_shapes=[pltpu.VMEM((4, r, cols), BF16), pltpu.VMEM((4, r, cols), BF16), pltpu.VMEM((4, r, cols), BF16),
                        pltpu.SemaphoreType.DMA((4,)), pltpu.SemaphoreType.DMA((4,)), pltpu.SemaphoreType.DMA((4,)),
                        pltpu.SemaphoreType.DMA((4,))],
        compiler_params=_params(),
    )(blocks)


_HBM = pl.BlockSpec(memory_space=pltpu.HBM)
_SEM = pl.BlockSpec(memory_space=pltpu.SEMAPHORE)
_EFFECT = pltpu.SideEffectType.DATAFLOW_SIDE_EFFECTING


def _start_copies(bufs, copies, n_copies, after, name):
    nb = len(bufs)

    def body(*refs):
        for cp in copies(refs[:nb], refs[nb + 1], refs[nb + 2]):
            cp.start()
        refs[-1][...] = jnp.zeros_like(refs[-1])

    out = pl.pallas_call(
        body, name=name,
        out_shape=(pltpu.SemaphoreType.DMA((n_copies,)), pltpu.SemaphoreType.DMA((n_copies,)),
                   *[pltpu.HBM(b.shape, b.dtype) for b in bufs], jax.ShapeDtypeStruct((8, 128), F32)),
        in_specs=(_HBM,) * nb + (pl.BlockSpec(memory_space=pl.ANY),),
        out_specs=(_SEM, _SEM) + (_HBM,) * nb + (pl.BlockSpec(memory_space=pltpu.VMEM),),
        input_output_aliases={i: 2 + i for i in range(nb)},
        compiler_params=pltpu.CompilerParams(has_side_effects=_EFFECT),
    )(*[pltpu.with_memory_space_constraint(b, pltpu.HBM) for b in bufs], after)
    return out[0], out[1], list(out[2:2 + nb]), out[-1]


def _wait_copies(flight, copies, after, name):
    send_sems, recv_sems, bufs, _ = flight
    nb = len(bufs)

    def body(*refs):
        for cp in copies(refs[:nb], refs[nb], refs[nb + 1]):
            cp.wait_send()
            cp.wait_recv()

    return pl.pallas_call(
        body, name=name,
        out_shape=tuple(pltpu.HBM(b.shape, b.dtype) for b in bufs),
        in_specs=(_HBM,) * nb + (_SEM, _SEM, pl.BlockSpec(memory_space=pl.ANY)), out_specs=(_HBM,) * nb,
        input_output_aliases={i: i for i in range(nb)},
        compiler_params=pltpu.CompilerParams(has_side_effects=_EFFECT),
    )(*bufs, send_sems, recv_sems, after)


def _wait_then_start(flight, waited, started, n_started, after, name):
    old_send, old_recv, bufs, _ = flight
    nb = len(bufs)

    def body(*refs):
        for cp in waited(refs[:nb], refs[nb], refs[nb + 1]):
            cp.wait_send()
            cp.wait_recv()
        for cp in started(refs[:nb], refs[nb + 3], refs[nb + 4]):
            cp.start()
        refs[-1][...] = jnp.zeros_like(refs[-1])

    out = pl.pallas_call(
        body, name=name,
        out_shape=(pltpu.SemaphoreType.DMA((n_started,)), pltpu.SemaphoreType.DMA((n_started,)),
                   *[pltpu.HBM(b.shape, b.dtype) for b in bufs], jax.ShapeDtypeStruct((8, 128), F32)),
        in_specs=(_HBM,) * nb + (_SEM, _SEM, pl.BlockSpec(memory_space=pl.ANY)),
        out_specs=(_SEM, _SEM) + (_HBM,) * nb + (pl.BlockSpec(memory_space=pltpu.VMEM),),
        input_output_aliases={i: 2 + i for i in range(nb)},
        compiler_params=pltpu.CompilerParams(has_side_effects=_EFFECT),
    )(*bufs, old_send, old_recv, after)
    return out[0], out[1], list(out[2:2 + nb]), out[-1]


def _chip_copies(refs, send_sems, recv_sems):
    pair_ref, land_ref = refs
    x, y, c = _place()
    chips = [(1 - x, y), (x, 1 - y), (1 - x, 1 - y)]
    return [pltpu.make_async_remote_copy(
        src_ref=pair_ref.at[2 * chip[0] + chip[1]], dst_ref=land_ref.at[k],
        send_sem=send_sems.at[k], recv_sem=recv_sems.at[k],
        device_id=(*chip, c), device_id_type=MESH) for k, chip in enumerate(chips)]


def _first_hop_copies(refs, send_sems, recv_sems):
    pair_ref, land_ref = refs
    x, y, c = _place()
    first = ((x + 1 - c) % 2, (y + c) % 2)
    blocks = [2 * first[0] + first[1], 2 * (1 - x) + (1 - y)]
    return [pltpu.make_async_remote_copy(
        src_ref=pair_ref.at[blocks[k]], dst_ref=land_ref.at[k], send_sem=send_sems.at[k], recv_sem=recv_sems.at[k],
        device_id=(*first, c), device_id_type=MESH) for k in range(2)]


def _second_hop_copies(refs, send_sems, recv_sems):
    relay_ref, land_ref = refs
    x, y, c = _place()
    second = ((x + c) % 2, (y + 1 - c) % 2)
    return [pltpu.make_async_remote_copy(
        src_ref=relay_ref, dst_ref=land_ref.at[0], send_sem=send_sems.at[0], recv_sem=recv_sems.at[0],
        device_id=(*second, c), device_id_type=MESH)]


def _own_block_copies(targets):
    def copies(refs, send_sems, recv_sems):
        x, y, c = _place()
        mine = refs[0].at[4 * x + 2 * y + c]
        return [pltpu.make_async_remote_copy(
            src_ref=mine, dst_ref=mine, send_sem=send_sems.at[k], recv_sem=recv_sems.at[k],
            device_id=to, device_id_type=MESH) for k, to in enumerate(targets(x, y, c))]
    return copies


def _my_core_and_sibling(x, y, c):
    return [(x, y, 1 - c), (1 - x, y, c), (x, 1 - y, c), (1 - x, 1 - y, c)]


def _all_others(x, y, c):
    flip = lambda v, f: 1 - v if f else v
    return [(flip(x, r & 4), flip(y, r & 2), flip(c, r & 1)) for r in range(1, N_DEV)]


def _forward_copies(refs, send_sems, recv_sems):
    x, y, c = _place()
    chips = [(1 - x, y), (x, 1 - y), (1 - x, 1 - y)]
    return [pltpu.make_async_remote_copy(
        src_ref=refs[0].at[4 * chip[0] + 2 * chip[1] + c], dst_ref=refs[0].at[4 * chip[0] + 2 * chip[1] + c],
        send_sem=send_sems.at[k], recv_sem=recv_sems.at[k],
        device_id=(x, y, 1 - c), device_id_type=MESH) for k, chip in enumerate(chips)]


def _first_axis_chip(x, y, c):
    return (x + 1 - c) % 2, (y + c) % 2


def _second_axis_chip(x, y, c):
    return (x + c) % 2, (y + 1 - c) % 2


def _first_targets(x, y, c):
    return [(x, y, 1 - c), (*_first_axis_chip(x, y, c), c)]


def _all_gather_small(shard, name):
    def body(in_ref, out_ref, send_sems, recv_sems, local_sem):
        x, y, c = _place()
        me, sibling = 4 * x + 2 * y + c, (x, y, 1 - c)
        first, second = _first_axis_chip(x, y, c), _second_axis_chip(x, y, c)

        def pair(chip):
            return out_ref.at[pl.ds(2 * (2 * chip[0] + chip[1]), 2)]

        def exchange(k, src, dst, to):
            cp = pltpu.make_async_remote_copy(src_ref=src, dst_ref=dst, send_sem=send_sems.at[k],
                                              recv_sem=recv_sems.at[k], device_id=to, device_id_type=MESH)
            cp.start()
            cp.wait()

        own = pltpu.make_async_copy(in_ref, out_ref.at[me], local_sem)
        own.start()
        exchange(0, in_ref, out_ref.at[me], sibling)
        own.wait()
        exchange(1, pair((x, y)), pair((x, y)), (*second, c))
        exchange(2, pair(second), pair(second), sibling)
        exchange(3, pair(first), pair(first), (*second, c))

    spec = pl.BlockSpec(memory_space=pltpu.VMEM)
    return pl.pallas_call(
        body, name=name, out_shape=jax.ShapeDtypeStruct((N_DEV,) + shard.shape, shard.dtype),
        in_specs=[spec], out_specs=spec,
        scratch_shapes=[pltpu.SemaphoreType.DMA((4,)), pltpu.SemaphoreType.DMA((4,)), pltpu.SemaphoreType.DMA],
        compiler_params=_params(),
    )(shard)


def _slot_copies(refs, send_sems, recv_sems, plan):
    copies = []
    for k, ((px, py, pc), to) in enumerate(plan):
        blk = refs[0].at[4 * px + 2 * py + pc]
        copies.append(pltpu.make_async_remote_copy(
            src_ref=blk, dst_ref=blk, send_sem=send_sems.at[k], recv_sem=recv_sems.at[k],
            device_id=to, device_id_type=MESH))
    return copies


def _second_axis_stage_copies(refs, send_sems, recv_sems):
    x, y, c = _place()
    first, second = (*_first_axis_chip(x, y, c), c), (*_second_axis_chip(x, y, c), c)
    return _slot_copies(refs, send_sems, recv_sems, [((x, y, c), second), (first, (x, y, 1 - c)), (first, second)])


def _second_axis_forward_copies(refs, send_sems, recv_sems):
    x, y, c = _place()
    return _slot_copies(refs, send_sems, recv_sems, [((*_second_axis_chip(x, y, c), c), (x, y, 1 - c))])


def _diagonal_forward_copies(refs, send_sems, recv_sems):
    x, y, c = _place()
    blk = refs[0].at[4 * (1 - x) + 2 * (1 - y) + c]
    return [pltpu.make_async_remote_copy(
        src_ref=blk, dst_ref=blk, send_sem=send_sems.at[0], recv_sem=recv_sems.at[0],
        device_id=(x, y, 1 - c), device_id_type=MESH)]


def _with_own_slot(block, me):
    return lax.dynamic_update_index_in_dim(lax.empty((N_DEV,) + block.shape, block.dtype), block, me, 0)


def _matmul(a, b, dims, out_dtype, tm, tn, name, dep=None):
    if dims == "nn":
        (m, k), n = a.shape, b.shape[1]
        a_spec = pl.BlockSpec((tm, k), lambda i, j: (i, 0))
        b_spec = pl.BlockSpec((k, tn), lambda i, j: (0, j))
        contract = ((1,), (0,))
    elif dims == "nt":
        (m, k), n = a.shape, b.shape[0]
        a_spec = pl.BlockSpec((tm, k), lambda i, j: (i, 0))
        b_spec = pl.BlockSpec((tn, k), lambda i, j: (j, 0))
        contract = ((1,), (1,))
    else:
        (k, m), n = a.shape, b.shape[1]
        a_spec = pl.BlockSpec((k, tm), lambda i, j: (0, i))
        b_spec = pl.BlockSpec((k, tn), lambda i, j: (0, j))
        contract = ((0,), (0,))
    assert m % tm == 0 and n % tn == 0 and a.dtype == BF16 and b.dtype == BF16

    def body(a_ref, b_ref, *rest):
        rest[-1][...] = lax.dot_general(a_ref[...], b_ref[...], (contract, ((), ())),
                                        preferred_element_type=F32).astype(out_dtype)

    deps = [] if dep is None else [dep]
    return pl.pallas_call(
        body, name=name, grid=(m // tm, n // tn),
        in_specs=[a_spec, b_spec] + [pl.BlockSpec((8, 128), lambda i, j: (0, 0))] * len(deps),
        out_specs=pl.BlockSpec((tm, tn), lambda i, j: (i, j)),
        out_shape=jax.ShapeDtypeStruct((m, n), out_dtype),
        compiler_params=_params(dimension_semantics=("arbitrary", "arbitrary")),
    )(a, b, *deps)


Z_TILE = 768
_Z_TILE_ORDER = ((0, 1, 2, 3, 4, 5, 6), (2, 0, 1, 6, 3, 4, 5), (4, 0, 5, 6, 1, 2, 3), (6, 2, 3, 4, 0, 1, 5))
_Z_EARLY_TILES = 4


def _z_proj(h, w_in_t, order, first, count, z_prev, name, dep=None):
    t = h.shape[0]

    def body(order_ref, h_ref, w_ref, *rest):
        rest[-1][...] = _dot_nt(h_ref[...], w_ref[...])

    prev = [] if z_prev is None else [z_prev]
    deps = [] if dep is None else [dep]
    return pl.pallas_call(
        body, name=name,
        grid_spec=pltpu.PrefetchScalarGridSpec(
            num_scalar_prefetch=1, grid=(count,),
            in_specs=[pl.BlockSpec((t, D_MODEL), lambda j, o: (0, 0)),
                      pl.BlockSpec((Z_TILE, D_MODEL), lambda j, o: (o[first + j], 0))]
            + [pl.BlockSpec(memory_space=pl.ANY)] * len(prev)
            + [pl.BlockSpec((8, 128), lambda j, o: (0, 0))] * len(deps),
            out_specs=pl.BlockSpec((t, Z_TILE), lambda j, o: (0, o[first + j]))),
        out_shape=jax.ShapeDtypeStruct((t, D_IN), F32),
        input_output_aliases={3: 0} if prev else {},
        compiler_params=_params(dimension_semantics=("arbitrary",)),
    )(order, h, w_in_t, *prev, *deps)


def _w_in_grad_exchange(dz, h, dep):
    n = dz.shape[1]
    d = h.shape[1]
    n_tiles = n // Z_TILE
    pieces = []
    for k in range(n_tiles):
        for b in range(N_DEV):
            lo, hi = max(Z_TILE * k, W_IN_SHARD * b), min(Z_TILE * (k + 1), W_IN_SHARD * (b + 1))
            if lo < hi:
                pieces.append((k, b, lo, hi))

    def body(dz_ref, h_ref, dep_ref, mine_ref, land_ref, tile_scr, send_sems, recv_sems, local_sems):
        x, y, c = _place()

        def copies(p):
            k, b, lo, hi = pieces[p]
            src = tile_scr.at[k % 2, pl.ds(lo - Z_TILE * k, hi - lo), :]
            rows = pl.ds(lo - W_IN_SHARD * b, hi - lo)
            local = pltpu.make_async_copy(src, mine_ref.at[b // 2, rows, :], local_sems.at[p])
            remote = pltpu.make_async_remote_copy(
                src_ref=src, dst_ref=land_ref.at[b // 2, rows, :], send_sem=send_sems.at[p], recv_sem=recv_sems.at[p],
                device_id=(x, y, 1 - c), device_id_type=MESH)
            return b % 2, local, remote

        def of_tile(k):
            return [p for p in range(len(pieces)) if pieces[p][0] == k]

        def start(k):
            for p in of_tile(k):
                core, local, remote = copies(p)

                @pl.when(c == core)
                def _():
                    local.start()

                @pl.when(c != core)
                def _():
                    remote.start()

        def finish(k):
            for p in of_tile(k):
                core, local, remote = copies(p)

                @pl.when(c == core)
                def _():
                    local.wait()
                    remote.wait_recv()

                @pl.when(c != core)
                def _():
                    remote.wait_send()

        for k in range(n_tiles):
            @pl.when(pl.program_id(0) == k)
            def _():
                if k >= 2:
                    finish(k - 2)
                tile_scr[k % 2] = lax.dot_general(dz_ref[...], h_ref[...], (((0,), (0,)), ((), ())),
                                                  preferred_element_type=F32).astype(BF16)
                start(k)
                if k == n_tiles - 1:
                    finish(k - 1)
                    finish(k)

    spec = pl.BlockSpec(memory_space=pl.ANY)
    n_pieces = len(pieces)
    return pl.pallas_call(
        body, name="w_in_grad", grid=(n_tiles,),
        in_specs=[pl.BlockSpec((dz.shape[0], Z_TILE), lambda j: (0, j)),
                  pl.BlockSpec(h.shape, lambda j: (0, 0), pipeline_mode=pl.Buffered(1)),
                  pl.BlockSpec((8, 128), lambda j: (0, 0))],
        out_specs=(spec, spec),
        out_shape=(jax.ShapeDtypeStruct((4, W_IN_SHARD, d), BF16), jax.ShapeDtypeStruct((4, W_IN_SHARD, d), BF16)),
        scratch_shapes=[pltpu.VMEM((2, Z_TILE, d), BF16), pltpu.SemaphoreType.DMA((n_pieces,)),
                        pltpu.SemaphoreType.DMA((n_pieces,)), pltpu.SemaphoreType.DMA((n_pieces,))],
        compiler_params=_params(dimension_semantics=("arbitrary",)),
    )(dz, h, dep)


def _sum_bf16(a, b, name, tr):
    _, r, c = a.shape

    def body(a_ref, b_ref, o_ref):
        o_ref[...] = (a_ref[...].astype(F32) + b_ref[...].astype(F32)).astype(BF16)

    blk = pl.BlockSpec((1, tr, c), lambda m, i: (m, i, 0))
    return pl.pallas_call(
        body, name=name, grid=(4, r // tr), in_specs=[blk, blk], out_specs=blk,
        out_shape=jax.ShapeDtypeStruct(a.shape, BF16),
        compiler_params=_params(dimension_semantics=("arbitrary", "arbitrary")),
    )(a, b)


def _modulation(c_all, w_ada, b_ada_mine):
    def body(c_ref, w_ref, b_ref, act_ref, mod_ref):
        cv = c_ref[...]
        act = cv * _sigmoid(cv)
        act_ref[...] = act
        mod_ref[...] = jnp.dot(act.astype(BF16), w_ref[...].astype(BF16), preferred_element_type=F32) + b_ref[...]

    return pl.pallas_call(
        body, name="modulation",
        out_shape=(jax.ShapeDtypeStruct(c_all.shape, F32), jax.ShapeDtypeStruct((N_DEV, W_ADA_SHARD), F32)),
        compiler_params=_params(),
    )(c_all, w_ada, b_ada_mine)


def _modulated_norm(x, norm_g, scale, shift, tm=256):
    t, d = x.shape

    def body(x_ref, g_ref, sc_ref, sh_ref, h_ref):
        xv = x_ref[...]
        r = lax.rsqrt(jnp.mean(xv * xv, axis=-1, keepdims=True) + EPS)
        h = (xv * r) * g_ref[...] * (1.0 + sc_ref[...]) + sh_ref[...]
        h_ref[...] = h.astype(BF16)

    row = pl.BlockSpec((1, d), lambda i: (0, 0))
    return pl.pallas_call(
        body, name="modulated_norm", grid=(t // tm,),
        in_specs=[pl.BlockSpec((tm, d), lambda i: (i, 0)), row, row, row],
        out_specs=pl.BlockSpec((tm, d), lambda i: (i, 0)),
        out_shape=jax.ShapeDtypeStruct((t, d), BF16),
        compiler_params=_params(dimension_semantics=("arbitrary",)),
    )(x, norm_g, scale, shift)


def _window_bias(block_index):
    s = lax.broadcasted_iota(jnp.int32, (2 * BLOCK, BLOCK), 0)
    t = lax.broadcasted_iota(jnp.int32, (2 * BLOCK, BLOCK), 1)
    valid = ((s < BLOCK) & (s > t) & (block_index > 0)) | ((s >= BLOCK) & ((s - BLOCK) <= t))
    bias = jnp.where(valid, 0.0, -jnp.inf).astype(F32)
    return jnp.concatenate([bias] * 8, axis=1)


def _heads_t(pair_blocks, g):
    top = lax.broadcasted_iota(jnp.int32, (BLOCK, BLOCK), 0) < HEAD_DIM
    zeros = jnp.zeros((HEAD_DIM, BLOCK), F32)
    tiles = []
    for blk in pair_blocks:
        tp = blk.T
        if g == 0:
            tiles += [jnp.where(top, tp, 0.0), jnp.concatenate([tp[HEAD_DIM:], zeros], axis=0)]
        else:
            tiles += [jnp.concatenate([zeros, tp[:HEAD_DIM]], axis=0), jnp.where(top, 0.0, tp)]
    return jnp.concatenate(tiles, axis=1)


def _pair_block(xt, p, g):
    r0 = HEAD_DIM * g
    even = xt[r0:r0 + HEAD_DIM, (2 * p) * BLOCK:(2 * p + 1) * BLOCK]
    odd = xt[r0:r0 + HEAD_DIM, (2 * p + 1) * BLOCK:(2 * p + 2) * BLOCK]
    return jnp.concatenate([even, odd], axis=0).T


def _softmax_t(scores_t, bias, sink):
    st = scores_t + bias
    m = jnp.maximum(jnp.max(st, axis=0, keepdims=True), sink)
    e = jnp.exp(st - m)
    es = jnp.exp(sink - m)
    inv = 1.0 / (jnp.sum(e, axis=0, keepdims=True) + es)
    return e * inv, es * inv


def _dot(a, b):
    return jnp.dot(a, b, preferred_element_type=F32)


def _dot_nt(a, b):
    return lax.dot_general(a, b, (((1,), (1,)), ((), ())), preferred_element_type=F32)


def _layer_norm_fwd(v):
    mu = jnp.mean(v, axis=-1, keepdims=True)
    xc = v - mu
    rstd = lax.rsqrt(jnp.mean(xc * xc, axis=-1, keepdims=True) + EPS)
    return xc * rstd, rstd


def _tril(transposed=False):
    t = lax.broadcasted_iota(jnp.int32, (BLOCK, BLOCK), 0)
    s = lax.broadcasted_iota(jnp.int32, (BLOCK, BLOCK), 1)
    return s >= t if transposed else t >= s


def _const_spec(shape):
    return pl.BlockSpec(shape, lambda i: (0,) * len(shape))


def _kv_prev_spec(index):
    return pl.BlockSpec((BLOCK, 2 * D_KV), lambda i: (jnp.maximum(index(i) - 1, 0), SEG_KV // (2 * D_KV)))


def _keys_values(z_ref, kvp_ref):
    kvp, kvc = kvp_ref[...], z_ref[:, SEG_KV:SEG_KV + 2 * D_KV]
    kk = jnp.concatenate([kvp[:, :D_KV], kvc[:, :D_KV]], axis=0)
    vv = jnp.concatenate([kvp[:, D_KV:], kvc[:, D_KV:]], axis=0)
    return kk, vv


def _pair_cols(g, p, base=0):
    return slice(base + (4 * g + p) * 128, base + (4 * g + p + 1) * 128)


def _mixer_fwd(z, sink_rows, ln_g, ln_b, sgu_w, sgu_bt):
    t = z.shape[0]

    def body(z_ref, kvp_ref, sink_ref, lng_ref, lnb_ref, w_ref, bt_ref, a_ref):
        bias = _window_bias(pl.program_id(0))
        kk, vv = _keys_values(z_ref, kvp_ref)
        kk_b, vvt_b = kk.astype(BF16), vv.T.astype(BF16)
        for g in range(2):
            qt = _heads_t([z_ref[:, _pair_cols(g, p, SEG_Q)] * ATTN_SCALE for p in range(4)], g).astype(BF16)
            prob, _ = _softmax_t(_dot(kk_b, qt), bias, sink_ref[g])
            ot = _dot(vvt_b, prob.astype(BF16))
            for p in range(4):
                gate = z_ref[:, _pair_cols(g, p, SEG_GA)]
                a_ref[:, _pair_cols(g, p)] = (_pair_block(ot, p, g) * (gate * _sigmoid(gate))).astype(BF16)

        vhat, _ = _layer_norm_fwd(z_ref[:, SEG_VS:SEG_VS + D_SGU])
        vn = vhat * lng_ref[...] + lnb_ref[...]
        tril = _tril()
        for g in range(SGU_GROUPS):
            cols = slice(g * 128, (g + 1) * 128)
            wm = jnp.where(tril, w_ref[g], 0.0).astype(BF16)
            mixed = _dot(wm, vn[:, cols].astype(BF16)) + bt_ref[:, g:g + 1]
            gate = z_ref[:, SEG_GS + g * 128:SEG_GS + (g + 1) * 128]
            a_ref[:, D_ATTN + g * 128:D_ATTN + (g + 1) * 128] = (
                (z_ref[:, SEG_U + g * 128:SEG_U + (g + 1) * 128] * mixed) * (gate * _sigmoid(gate))).astype(BF16)

    return pl.pallas_call(
        body, name="mixer_fwd", grid=(t // BLOCK,),
        in_specs=[pl.BlockSpec((BLOCK, D_IN), lambda i: (i, 0)), _kv_prev_spec(lambda i: i),
                  _const_spec((2, 1, 8 * BLOCK)), _const_spec((1, D_SGU)), _const_spec((1, D_SGU)),
                  _const_spec((SGU_GROUPS, BLOCK, BLOCK)), _const_spec((BLOCK, SGU_GROUPS))],
        out_specs=pl.BlockSpec((BLOCK, D_MODEL), lambda i: (i, 0)),
        out_shape=jax.ShapeDtypeStruct((t, D_MODEL), BF16),
        compiler_params=_params(dimension_semantics=("arbitrary",)),
    )(z, z, sink_rows, ln_g, ln_b, sgu_w, sgu_bt)


def _mixer_bwd(z, da, sink_rows, ln_g, ln_b, sgu_w, sgu_wt, sgu_bt):
    t = z.shape[0]
    nb = t // BLOCK

    def body(z_ref, kvp_ref, da_ref, sink_ref, lng_ref, lnb_ref, w_ref, wt_ref, bt_ref,
             dz_ref, dsink_ref, dw_ref, db_ref, dlng_ref, dlnb_ref, carry_ref, dsink_acc, dbt_acc):
        step = pl.program_id(0)

        @pl.when(step == 0)
        def _():
            carry_ref[...] = jnp.zeros_like(carry_ref)
            dsink_acc[...] = jnp.zeros_like(dsink_acc)
            dbt_acc[...] = jnp.zeros_like(dbt_acc)
            dw_ref[...] = jnp.zeros_like(dw_ref)
            dlng_ref[...] = jnp.zeros_like(dlng_ref)
            dlnb_ref[...] = jnp.zeros_like(dlnb_ref)

        bias = _window_bias(nb - 1 - step)
        kk, vv = _keys_values(z_ref, kvp_ref)
        kk_b, vv_b = kk.astype(BF16), vv.astype(BF16)
        kkt_b, vvt_b = kk.T.astype(BF16), vv.T.astype(BF16)
        dkk = jnp.zeros((2 * BLOCK, D_KV), F32)
        dvv = jnp.zeros((2 * BLOCK, D_KV), F32)
        for g in range(2):
            qt = _heads_t([z_ref[:, _pair_cols(g, p, SEG_Q)] * ATTN_SCALE for p in range(4)], g).astype(BF16)
            prob, sink_prob = _softmax_t(_dot(kk_b, qt), bias, sink_ref[g])
            prob_b = prob.astype(BF16)
            ot = _dot(vvt_b, prob_b)
            gates = [z_ref[:, _pair_cols(g, p, SEG_GA)] for p in range(4)]
            sig = [_sigmoid(gt) for gt in gates]
            d_attn = [da_ref[:, _pair_cols(g, p)] for p in range(4)]
            d_ot = _heads_t([d_attn[p] * (gates[p] * sig[p]) for p in range(4)], g).astype(BF16)
            d_prob = _dot(vv_b, d_ot)
            delta = jnp.sum(prob * d_prob, axis=0, keepdims=True)
            d_scores = (prob * (d_prob - delta)).astype(BF16)
            dsink_acc[g] -= sink_prob * delta
            d_qt = _dot(kkt_b, d_scores)
            dkk = dkk + _dot_nt(d_scores, qt)
            dvv = dvv + _dot_nt(prob_b, d_ot)
            for p in range(4):
                dz_ref[:, _pair_cols(g, p, SEG_Q)] = (_pair_block(d_qt, p, g) * ATTN_SCALE).astype(BF16)
                d_silu = sig[p] * (1.0 + gates[p] * (1.0 - sig[p]))
                dz_ref[:, _pair_cols(g, p, SEG_GA)] = (d_attn[p] * _pair_block(ot, p, g) * d_silu).astype(BF16)
        d_kv = jnp.concatenate([dkk, dvv], axis=1)
        dz_ref[:, SEG_KV:SEG_KV + 2 * D_KV] = (d_kv[BLOCK:] + carry_ref[...]).astype(BF16)
        carry_ref[...] = d_kv[:BLOCK]

        vhat, rstd = _layer_norm_fwd(z_ref[:, SEG_VS:SEG_VS + D_SGU])
        lng = lng_ref[...]
        vn = vhat * lng + lnb_ref[...]
        tril, triu = _tril(), _tril(transposed=True)
        lane = lax.broadcasted_iota(jnp.int32, (BLOCK, 128), 1)
        d_bt = jnp.zeros((BLOCK, 128), F32)
        d_vn = []
        for g in range(SGU_GROUPS):
            cols = slice(g * 128, (g + 1) * 128)
            wm = jnp.where(tril, w_ref[g], 0.0).astype(BF16)
            wmt = jnp.where(triu, wt_ref[g], 0.0).astype(BF16)
            vn_g = vn[:, cols].astype(BF16)
            mixed = _dot(wm, vn_g) + bt_ref[:, g:g + 1]
            gate = z_ref[:, SEG_GS + g * 128:SEG_GS + (g + 1) * 128]
            u = z_ref[:, SEG_U + g * 128:SEG_U + (g + 1) * 128]
            d_out = da_ref[:, D_ATTN + g * 128:D_ATTN + (g + 1) * 128]
            sg = _sigmoid(gate)
            d_um = d_out * (gate * sg)
            dz_ref[:, SEG_U + g * 128:SEG_U + (g + 1) * 128] = (d_um * mixed).astype(BF16)
            dz_ref[:, SEG_GS + g * 128:SEG_GS + (g + 1) * 128] = (
                d_out * (u * mixed) * (sg * (1.0 + gate * (1.0 - sg)))).astype(BF16)
            d_mixed = d_um * u
            d_mixed_b = d_mixed.astype(BF16)
            dw_ref[g] += jnp.where(tril, _dot_nt(d_mixed_b, vn_g), 0.0)
            d_bt = d_bt + jnp.where(lane == g, jnp.sum(d_mixed, axis=-1, keepdims=True), 0.0)
            d_vn.append(_dot(wmt, d_mixed_b))
        dbt_acc[...] += d_bt
        d_vn = jnp.concatenate(d_vn, axis=1)
        dlng_ref[...] += jnp.sum(d_vn * vhat, axis=0, keepdims=True)
        dlnb_ref[...] += jnp.sum(d_vn, axis=0, keepdims=True)
        d_vhat = d_vn * lng
        d_v = rstd * (d_vhat - jnp.mean(d_vhat, axis=-1, keepdims=True)
                      - vhat * jnp.mean(d_vhat * vhat, axis=-1, keepdims=True))
        dz_ref[:, SEG_VS:SEG_VS + D_SGU] = d_v.astype(BF16)

        @pl.when(step == nb - 1)
        def _():
            db_ref[...] = dbt_acc[...].T[:SGU_GROUPS]
            lane_row = lax.broadcasted_iota(jnp.int32, (1, 128), 1)
            d_sink = jnp.zeros((1, 128), F32)
            for g in range(2):
                acc = dsink_acc[g]
                for j in range(8):
                    head_sum = jnp.sum(acc[:, j * BLOCK:(j + 1) * BLOCK], axis=-1, keepdims=True)
                    d_sink = d_sink + jnp.where(lane_row == 8 * g + j, head_sum, 0.0)
            dsink_ref[...] = d_sink

    rev = lambda i: nb - 1 - i
    return pl.pallas_call(
        body, name="mixer_bwd", grid=(nb,),
        in_specs=[pl.BlockSpec((BLOCK, D_IN), lambda i: (rev(i), 0)), _kv_prev_spec(rev),
                  pl.BlockSpec((BLOCK, D_MODEL), lambda i: (rev(i), 0)),
                  _const_spec((2, 1, 8 * BLOCK)), _const_spec((1, D_SGU)), _const_spec((1, D_SGU)),
                  _const_spec((SGU_GROUPS, BLOCK, BLOCK)), _const_spec((SGU_GROUPS, BLOCK, BLOCK)),
                  _const_spec((BLOCK, SGU_GROUPS))],
        out_specs=(pl.BlockSpec((BLOCK, D_IN), lambda i: (rev(i), 0)), _const_spec((1, 128)),
                   _const_spec((SGU_GROUPS, BLOCK, BLOCK)), _const_spec((SGU_GROUPS, BLOCK)),
                   _const_spec((1, D_SGU)), _const_spec((1, D_SGU))),
        out_shape=(jax.ShapeDtypeStruct((t, D_IN), BF16), jax.ShapeDtypeStruct((1, 128), F32),
                   jax.ShapeDtypeStruct((SGU_GROUPS, BLOCK, BLOCK), F32), jax.ShapeDtypeStruct((SGU_GROUPS, BLOCK), F32),
                   jax.ShapeDtypeStruct((1, D_SGU), F32), jax.ShapeDtypeStruct((1, D_SGU), F32)),
        scratch_shapes=[pltpu.VMEM((BLOCK, 2 * D_KV), F32), pltpu.VMEM((2, 1, 8 * BLOCK), F32),
                        pltpu.VMEM((BLOCK, 128), F32)],
        compiler_params=_params(dimension_semantics=("arbitrary",)),
    )(z, z, da, sink_rows, ln_g, ln_b, sgu_w, sgu_wt, sgu_bt)


def _out_proj_head(a, w_out_full, x, target, gate, final_g, tm=256):
    t, d = x.shape

    def body(a_ref, w_ref, x_ref, tg_ref, gate_ref, fg_ref, dx2_ref, dy_ref, loss_ref, dfg_ref, dgate_ref):
        @pl.when(pl.program_id(0) == 0)
        def _():
            loss_ref[...] = jnp.zeros_like(loss_ref)
            dfg_ref[...] = jnp.zeros_like(dfg_ref)
            dgate_ref[...] = jnp.zeros_like(dgate_ref)

        yv, gate, fg = _dot(a_ref[...], w_ref[...]), gate_ref[...], fg_ref[...]
        x2 = x_ref[...] + gate * yv
        r2 = lax.rsqrt(jnp.mean(x2 * x2, axis=-1, keepdims=True) + EPS)
        nrm = x2 * r2
        err = nrm * fg - tg_ref[...]
        loss_ref[...] += 0.5 * jnp.sum(jnp.mean(err * err, axis=-1, keepdims=True), axis=0, keepdims=True)
        d_out = err * (1.0 / d)
        dfg_ref[...] += jnp.sum(d_out * nrm, axis=0, keepdims=True)
        d_nrm = d_out * fg
        dx2 = r2 * (d_nrm - nrm * jnp.mean(d_nrm * nrm, axis=-1, keepdims=True))
        dx2_ref[...] = dx2
        dgate_ref[...] += jnp.sum(dx2 * yv, axis=0, keepdims=True)
        dy_ref[...] = (dx2 * gate).astype(BF16)

    blk = pl.BlockSpec((tm, d), lambda i: (i, 0))
    row = _const_spec((1, d))
    whole = pl.BlockSpec(w_out_full.shape, lambda i: (0, 0), pipeline_mode=pl.Buffered(1))
    return pl.pallas_call(
        body, name="out_proj_head", grid=(t // tm,),
        in_specs=[pl.BlockSpec((tm, a.shape[1]), lambda i: (i, 0)), whole, blk, blk, row, row],
        out_specs=(blk, blk, _const_spec((1, 128)), row, row),
        out_shape=(jax.ShapeDtypeStruct((t, d), F32), jax.ShapeDtypeStruct((t, d), BF16),
                   jax.ShapeDtypeStruct((1, 128), F32), jax.ShapeDtypeStruct((1, d), F32),
                   jax.ShapeDtypeStruct((1, d), F32)),
        compiler_params=_params(dimension_semantics=("arbitrary",)),
    )(a, w_out_full, x, target, gate, final_g)


def _z_proj_bwd_norm(dz, w_in_t, x, dx2, norm_g, scale, dep, tm=256):
    t, d = x.shape

    def body(dz_ref, w_ref, x_ref, dx2_ref, g_ref, sc_ref, dep_ref, gx_ref, dshift_ref, dscale_ref, dg_ref):
        @pl.when(pl.program_id(0) == 0)
        def _():
            dshift_ref[...] = jnp.zeros_like(dshift_ref)
            dscale_ref[...] = jnp.zeros_like(dscale_ref)
            dg_ref[...] = jnp.zeros_like(dg_ref)

        dh, xv, g = _dot(dz_ref[...], w_ref[...]), x_ref[...], g_ref[...]
        one_plus = 1.0 + sc_ref[...]
        r = lax.rsqrt(jnp.mean(xv * xv, axis=-1, keepdims=True) + EPS)
        xn = xv * r
        dshift_ref[...] += jnp.sum(dh, axis=0, keepdims=True)
        dscale_ref[...] += jnp.sum(dh * (xn * g), axis=0, keepdims=True)
        d_y = dh * one_plus
        dg_ref[...] += jnp.sum(d_y * xn, axis=0, keepdims=True)
        d_xn = d_y * g
        gx_ref[...] = dx2_ref[...] + r * (d_xn - xn * jnp.mean(d_xn * xn, axis=-1, keepdims=True))

    blk = pl.BlockSpec((tm, d), lambda i: (i, 0))
    row = _const_spec((1, d))
    whole = pl.BlockSpec(w_in_t.shape, lambda i: (0, 0), pipeline_mode=pl.Buffered(1))
    return pl.pallas_call(
        body, name="z_proj_bwd_norm", grid=(t // tm,),
        in_specs=[pl.BlockSpec((tm, dz.shape[1]), lambda i: (i, 0)), whole, blk, blk, row, row, _const_spec((8, 128))],
        out_specs=(blk, row, row, row),
        out_shape=(jax.ShapeDtypeStruct((t, d), F32),) + (jax.ShapeDtypeStruct((1, d), F32),) * 3,
        compiler_params=_params(dimension_semantics=("arbitrary",)),
    )(dz, w_in_t, x, dx2, norm_g, scale, dep)


def _adamw(w, g, m, v):
    m = ADAM_B1 * m + (1.0 - ADAM_B1) * g
    v = ADAM_B2 * v + (1.0 - ADAM_B2) * (g * g)
    m_hat = m / (1.0 - ADAM_B1 ** ADAM_STEP)
    v_hat = v / (1.0 - ADAM_B2 ** ADAM_STEP)
    delta = -ADAM_LR * (m_hat / (jnp.sqrt(v_hat) + ADAM_EPS) + ADAM_WD * w)
    return delta, m, v


def _relay_sum(second_chip, pair, land, tr):
    _, r, c = pair.shape

    def body(chip_ref, a_ref, b_ref, o_ref):
        o_ref[...] = (a_ref[...].astype(F32) + b_ref[...].astype(F32)).astype(BF16)

    return pl.pallas_call(
        body, name="w_in_grad_relay_sum",
        grid_spec=pltpu.PrefetchScalarGridSpec(
            num_scalar_prefetch=1, grid=(r // tr,),
            in_specs=[pl.BlockSpec((None, tr, c), lambda i, chip_ref: (chip_ref[0], i, 0)),
                      pl.BlockSpec((None, tr, c), lambda i, chip_ref: (1, i, 0))],
            out_specs=pl.BlockSpec((tr, c), lambda i, chip_ref: (i, 0))),
        out_shape=jax.ShapeDtypeStruct((r, c), BF16),
        compiler_params=_params(dimension_semantics=("arbitrary",)),
    )(second_chip, pair, land)


def _adam_from_chips(chip, pair, landed, w, m, v, name, tc):
    _, r, c = pair.shape
    n = len(landed)

    def body(chip_ref, own_ref, *refs):
        w_ref, m_ref, v_ref, g_ref, d_ref, nm_ref, nv_ref = refs[n:]
        g = own_ref[...].astype(F32)
        for k in range(n):
            g = g + refs[k][...].astype(F32)
        g_ref[...] = g
        d_ref[...], nm_ref[...], nv_ref[...] = _adamw(w_ref[...], g, m_ref[...], v_ref[...])

    def landed_spec(index):
        return pl.BlockSpec((None, r, tc), lambda i, chip_ref: (index, 0, i))

    blk = pl.BlockSpec((r, tc), lambda i, chip_ref: (0, i))
    return pl.pallas_call(
        body, name=name,
        grid_spec=pltpu.PrefetchScalarGridSpec(
            num_scalar_prefetch=1, grid=(c // tc,),
            in_specs=[pl.BlockSpec((None, r, tc), lambda i, chip_ref: (chip_ref[0], 0, i))]
            + [landed_spec(index) for _, index in landed] + [blk, blk, blk],
            out_specs=(blk,) * 4),
        out_shape=(jax.ShapeDtypeStruct((r, c), F32),) * 4,
        compiler_params=_params(dimension_semantics=("arbitrary",)),
    )(chip, pair, *[array for array, _ in landed], w, m, v)


def _adam_w_ada(act_t, dmod_mine, w, m, v, tr=256):
    r, c = w.shape

    def body(a_ref, dm_ref, w_ref, m_ref, v_ref, g_ref, d_ref, nm_ref, nv_ref):
        g = _dot(a_ref[...].astype(BF16), dm_ref[...].astype(BF16))
        g_ref[...] = g
        d_ref[...], nm_ref[...], nv_ref[...] = _adamw(w_ref[...], g, m_ref[...], v_ref[...])

    blk = pl.BlockSpec((tr, c), lambda i: (i, 0))
    return pl.pallas_call(
        body, name="adam_w_ada", grid=(r // tr,),
        in_specs=[pl.BlockSpec((tr, N_DEV), lambda i: (i, 0)), _const_spec((N_DEV, c)), blk, blk, blk],
        out_specs=(blk,) * 4, out_shape=(jax.ShapeDtypeStruct((r, c), F32),) * 4,
        compiler_params=_params(dimension_semantics=("arbitrary",)),
    )(act_t, dmod_mine, w, m, v)


def _pack_small(d_shift, d_scale, d_gate, d_norm_g, d_final_g, d_ln_g, d_ln_b, loss, d_sinks, d_sgu_b):
    def body(shift_ref, scale_ref, gate_ref, ng_ref, fg_ref, lng_ref, lnb_ref, loss_ref, sink_ref, b_ref, o_ref):
        o_ref[...] = jnp.zeros_like(o_ref)
        o_ref[ROW_SHIFT:ROW_SHIFT + 1, :] = shift_ref[...]
        o_ref[ROW_SCALE:ROW_SCALE + 1, :] = scale_ref[...]
        o_ref[ROW_GATE:ROW_GATE + 1, :] = gate_ref[...]
        o_ref[ROW_NORM_G:ROW_NORM_G + 1, :] = ng_ref[...]
        o_ref[ROW_FINAL_G:ROW_FINAL_G + 1, :] = fg_ref[...]
        o_ref[ROW_LN:ROW_LN + 1, 0:D_SGU] = lng_ref[...]
        o_ref[ROW_LN:ROW_LN + 1, D_SGU:2 * D_SGU] = lnb_ref[...]
        o_ref[ROW_MISC:ROW_MISC + 1, 0:128] = loss_ref[...]
        o_ref[ROW_MISC:ROW_MISC + 1, 128:256] = sink_ref[...]
        o_ref[ROW_SGU_B:ROW_SGU_B + SGU_GROUPS, 0:BLOCK] = b_ref[...]

    return pl.pallas_call(
        body, name="pack_small", out_shape=jax.ShapeDtypeStruct((SMALL_ROWS, D_MODEL), F32),
        compiler_params=_params(),
    )(d_shift, d_scale, d_gate, d_norm_g, d_final_g, d_ln_g, d_ln_b, loss, d_sinks, d_sgu_b)


_SMALL_NAMES = ("norm_g", "b_ada", "attn_sinks", "sgu_ln_g", "sgu_ln_b", "sgu_w", "sgu_b", "final_g")


def _adam_small(partials, d_sgu_w_all, weights, moments_m, moments_v):
    names = _SMALL_NAMES
    k = len(names)

    def body(*refs):
        p_ref, sw_ref = refs[0], refs[1]
        w_refs, m_refs, v_refs = refs[2:2 + k], refs[2 + k:2 + 2 * k], refs[2 + 2 * k:2 + 3 * k]
        loss_ref, dmod_ref = refs[2 + 3 * k], refs[3 + 3 * k]
        out_refs = refs[4 + 3 * k:4 + 7 * k]
        sum_ref = refs[4 + 7 * k]
        total = p_ref[0]
        for j in range(1, N_DEV):
            total = total + p_ref[j]
        sum_ref[...] = total
        for j in range(N_DEV):
            for part, row in enumerate((ROW_SHIFT, ROW_SCALE, ROW_GATE)):
                dmod_ref[j:j + 1, part * D_MODEL:(part + 1) * D_MODEL] = p_ref[j, row:row + 1, :]
        loss_ref[...] = sum_ref[ROW_MISC:ROW_MISC + 1, 0:1]
        d_sgu_w = sw_ref[0]
        for j in range(1, N_DEV):
            d_sgu_w = d_sgu_w + sw_ref[j]
        grads = {
            "norm_g": sum_ref[ROW_NORM_G:ROW_NORM_G + 1, :],
            "b_ada": jnp.concatenate([sum_ref[r:r + 1, :] for r in (ROW_SHIFT, ROW_SCALE, ROW_GATE)], axis=1),
            "attn_sinks": sum_ref[ROW_MISC:ROW_MISC + 1, 128:128 + N_Q_HEADS],
            "sgu_ln_g": sum_ref[ROW_LN:ROW_LN + 1, 0:D_SGU],
            "sgu_ln_b": sum_ref[ROW_LN:ROW_LN + 1, D_SGU:2 * D_SGU],
            "sgu_w": d_sgu_w[None],
            "sgu_b": sum_ref[ROW_SGU_B:ROW_SGU_B + SGU_GROUPS, 0:BLOCK][None],
            "final_g": sum_ref[ROW_FINAL_G:ROW_FINAL_G + 1, :],
        }
        for i, name in enumerate(names):
            g = grads[name]
            delta, m, v = _adamw(w_refs[i][...], g, m_refs[i][...], v_refs[i][...])
            out_refs[4 * i][...] = g
            out_refs[4 * i + 1][...] = delta
            out_refs[4 * i + 2][...] = m
            out_refs[4 * i + 3][...] = v

    shapes = [jax.ShapeDtypeStruct((1, 1), F32), jax.ShapeDtypeStruct((N_DEV, 3 * D_MODEL), F32)]
    for name in names:
        shapes += [jax.ShapeDtypeStruct(weights[name].shape, F32)] * 4
    outs = pl.pallas_call(
        body, name="adam_small", out_shape=tuple(shapes),
        scratch_shapes=[pltpu.VMEM((SMALL_ROWS, D_MODEL), F32)],
        compiler_params=_params(),
    )(partials, d_sgu_w_all, *[weights[n] for n in names], *[moments_m[n] for n in names],
      *[moments_v[n] for n in names])
    return outs[0], outs[1], {name: outs[2 + 4 * i:6 + 4 * i] for i, name in enumerate(names)}


def kernel(x, c, norm_g, w_ada, b_ada, w_in, attn_sinks, sgu_ln_g, sgu_ln_b, sgu_w, sgu_b, w_out, final_g, loss_target, m_norm_g, m_w_ada, m_b_ada, m_w_in, m_attn_sinks, m_sgu_ln_g, m_sgu_ln_b, m_sgu_w, m_sgu_b, m_w_out, m_final_g, v_norm_g, v_w_ada, v_b_ada, v_w_in, v_attn_sinks, v_sgu_ln_g, v_sgu_ln_b, v_sgu_w, v_sgu_b, v_w_out, v_final_g):
    xi, yi, ci = _place()
    me = 4 * xi + 2 * yi + ci
    x2d, target = x[0], loss_target[0]
    t = x2d.shape[0]

    core = ci.astype(jnp.int32).reshape(1)
    chip = (2 * xi + yi).astype(jnp.int32).reshape(1)

    first = _own_block_copies(_first_targets)
    first_flight = _start_copies([_with_own_slot(w_in[0].T.astype(BF16), me)], first, 2, core, "gather_w_in_start")

    c_all = _all_gather_small(c.reshape(8, 256) + first_flight[3][0, 0], "gather_c").reshape(N_DEV, D_MODEL)
    b_mine = lax.dynamic_slice(b_ada, (0, me * W_ADA_SHARD), (1, W_ADA_SHARD))
    c_act, mod_part = _modulation(c_all, w_ada[0], b_mine)
    mod_all = _all_gather_small(mod_part, "gather_mod")

    across = _wait_then_start(first_flight, lambda *a: first(*a)[1:], _second_axis_stage_copies, 3, mod_all,
                              "gather_w_in_second_axis_stage")
    mod = lax.dynamic_index_in_dim(mod_all, me, axis=1, keepdims=False).reshape(1, 3 * D_MODEL)
    mod = mod + across[3][0, 0]
    shift, scale, gate = mod[:, :D_MODEL], mod[:, D_MODEL:2 * D_MODEL], mod[:, 2 * D_MODEL:]
    h = _modulated_norm(x2d, norm_g, scale, shift)

    w_in_pair = _wait_copies((first_flight[0], first_flight[1], across[2], None), lambda *a: first(*a)[:1], h,
                             "gather_w_in_sibling_wait")
    tile_order = jnp.asarray(_Z_TILE_ORDER, jnp.int32)[chip[0]]
    z_own = _z_proj(h, w_in_pair[0].reshape(D_IN, D_MODEL), tile_order, 0, 1, None, "z_proj_own")
    forward = _wait_then_start((across[0], across[1], w_in_pair, None), lambda *a: _second_axis_stage_copies(*a)[:1],
                               _second_axis_forward_copies, 1, z_own, "gather_w_in_second_axis_forward")
    w_out_flight = _start_copies([_with_own_slot(w_out[0].astype(BF16), me)], _own_block_copies(_my_core_and_sibling),
                                 4, forward[3], "gather_w_out_start")
    w_in_most = _wait_copies((across[0], across[1], forward[2], None),
                             lambda *a: _second_axis_stage_copies(*a)[1:2], w_out_flight[3],
                             "gather_w_in_first_forward_wait")
    w_in_most = _wait_copies((forward[0], forward[1], w_in_most, None), _second_axis_forward_copies, z_own,
                             "gather_w_in_second_forward_wait")
    z_early = _z_proj(h, w_in_most[0].reshape(D_IN, D_MODEL), tile_order, 1, _Z_EARLY_TILES - 1, z_own, "z_proj_early")
    w_in_flight = _wait_then_start((across[0], across[1], w_in_most, None),
                                   lambda *a: _second_axis_stage_copies(*a)[2:], _diagonal_forward_copies, 1,
                                   z_early, "gather_w_in_last_stage")
    w_in_all = _wait_copies(w_in_flight, _diagonal_forward_copies, z_early, "gather_w_in_last_wait")[0]
    w_in_t = w_in_all.reshape(D_IN, D_MODEL)
    z = _z_proj(h, w_in_t, tile_order, _Z_EARLY_TILES, 7 - _Z_EARLY_TILES, z_early, "z_proj_late")
    w_out_flight = _wait_then_start(w_out_flight, _own_block_copies(_my_core_and_sibling), _forward_copies, 3, z,
                                    "gather_w_out_forward_stage")
    sink_rows = jnp.repeat(attn_sinks.reshape(N_Q_HEADS), BLOCK).reshape(2, 1, 8 * BLOCK)
    sgu_bt = sgu_b[0].T
    a = _mixer_fwd(z, sink_rows + w_out_flight[3][0, 0], sgu_ln_g, sgu_ln_b, sgu_w[0], sgu_bt)
    w_out_all = _wait_copies(w_out_flight, _forward_copies, a, "gather_w_out_forward_wait")[0]
    w_out_full = w_out_all.reshape(D_MODEL, D_MODEL)
    final_g_row = final_g.reshape(1, D_MODEL)
    dx2, dy, loss_part, d_final_g, d_gate = _out_proj_head(a, w_out_full, x2d, target, gate, final_g_row)

    da = _matmul(dy, w_out_full, "nt", F32, min(t, 1024), 1024, "out_proj_bwd")
    dw_out = _matmul(a, dy, "tn", BF16, 1024, 1024, "w_out_grad").reshape(4, 2, W_OUT_SHARD, D_MODEL)
    pair_out = _pair_reduce(dw_out, "w_out_grad_pair_reduce", W_OUT_SHARD // 2)
    out_flight = _start_copies([pair_out, lax.empty((3, W_OUT_SHARD, D_MODEL), BF16)], _chip_copies, 3, core,
                               "w_out_grad_chip_start")
    dz, d_sinks, d_sgu_w, d_sgu_b, d_ln_g, d_ln_b = _mixer_bwd(
        z, da, sink_rows + out_flight[3][0, 0], sgu_ln_g, sgu_ln_b, sgu_w[0], jnp.swapaxes(sgu_w[0], 1, 2), sgu_bt)
    sgu_w_flight = _start_copies([_with_own_slot(d_sgu_w, me)], _own_block_copies(_all_others), N_DEV - 1, core,
                                 "sgu_w_grad_gather_start")
    mine_in, land_in = _w_in_grad_exchange(dz, h, sgu_w_flight[3])
    pair_in = _sum_bf16(mine_in, land_in, "w_in_grad_pair_sum", W_IN_SHARD // 3)
    hop1 = _start_copies([pair_in, lax.empty((2, W_IN_SHARD, D_MODEL), BF16)], _first_hop_copies, 2, core,
                         "w_in_grad_first_hop_start")
    grad_x, d_shift, d_scale, d_norm_g = _z_proj_bwd_norm(dz, w_in_t, x2d, dx2, norm_g, scale, hop1[3])

    partial = _pack_small(d_shift, d_scale, d_gate, d_norm_g, d_final_g, d_ln_g, d_ln_b, loss_part, d_sinks, d_sgu_b)
    small_flight = _start_copies([_with_own_slot(partial, me)], _own_block_copies(_all_others), N_DEV - 1, core,
                                 "small_grad_gather_start")
    pair_in, land_first = _wait_copies(hop1, _first_hop_copies, small_flight[3], "w_in_grad_first_hop_wait")
    second_chip = (2 * ((xi + ci) % 2) + (yi + 1 - ci) % 2).astype(jnp.int32).reshape(1)
    relay = _relay_sum(second_chip, pair_in, land_first, W_IN_SHARD // 3)
    hop2 = _start_copies([relay, lax.empty((1, W_IN_SHARD, D_MODEL), BF16)], _second_hop_copies, 1, core,
                         "w_in_grad_second_hop_start")
    pair_out, land_out = _wait_copies(out_flight, _chip_copies, hop2[3], "w_out_grad_chip_wait")
    big = {"w_out": _adam_from_chips(chip, pair_out, [(land_out, k) for k in range(3)], w_out[0], m_w_out[0],
                                     v_w_out[0], "adam_w_out", 1024)}
    partial_all = _wait_copies(small_flight, _own_block_copies(_all_others), big["w_out"][0],
                               "small_grad_gather_wait")[0]
    d_sgu_w_all = _wait_copies(sgu_w_flight, _own_block_copies(_all_others), partial_all, "sgu_w_grad_gather_wait")[0]
    weights = {"norm_g": norm_g, "b_ada": b_ada, "attn_sinks": attn_sinks, "sgu_ln_g": sgu_ln_g,
               "sgu_ln_b": sgu_ln_b, "sgu_w": sgu_w, "sgu_b": sgu_b, "final_g": final_g_row}
    moments_m = {"norm_g": m_norm_g, "b_ada": m_b_ada, "attn_sinks": m_attn_sinks, "sgu_ln_g": m_sgu_ln_g,
                 "sgu_ln_b": m_sgu_ln_b, "sgu_w": m_sgu_w, "sgu_b": m_sgu_b,
                 "final_g": m_final_g.reshape(1, D_MODEL)}
    moments_v = {"norm_g": v_norm_g, "b_ada": v_b_ada, "attn_sinks": v_attn_sinks, "sgu_ln_g": v_sgu_ln_g,
                 "sgu_ln_b": v_sgu_ln_b, "sgu_w": v_sgu_w, "sgu_b": v_sgu_b,
                 "final_g": v_final_g.reshape(1, D_MODEL)}
    loss, dmod_all, small = _adam_small(partial_all, d_sgu_w_all, weights, moments_m, moments_v)
    small["final_g"] = tuple(o.reshape(D_MODEL) for o in small["final_g"])

    dmod_mine = lax.dynamic_slice(dmod_all, (0, me * W_ADA_SHARD), (N_DEV, W_ADA_SHARD))
    big["w_ada"] = _adam_w_ada(c_act.T, dmod_mine, w_ada[0], m_w_ada[0], v_w_ada[0])
    _, land_second = _wait_copies(hop2, _second_hop_copies, big["w_ada"][0], "w_in_grad_second_hop_wait")
    big["w_in"] = tuple(o.T for o in _adam_from_chips(
        chip, pair_in, [(land_first, 0), (land_second, 0)], w_in[0].T, m_w_in[0].T, v_w_in[0].T, "adam_w_in", 256))
    order = ["norm_g", "w_ada", "b_ada", "w_in", "attn_sinks", "sgu_ln_g", "sgu_ln_b", "sgu_w", "sgu_b", "w_out",
             "final_g"]
    outs = [loss.reshape(()), grad_x[None]]
    for k in range(4):
        for name in order:
            outs.append(big[name][k][None] if name in big else small[name][k])
    return tuple(outs)
```

```python
import jax
import jax.numpy as jnp
from jax import lax
from jax.experimental import pallas as pl
from jax.experimental.pallas import tpu as pltpu

F32 = jnp.float32
BF16 = jnp.bfloat16
MESH = pl.DeviceIdType.MESH

N_DEV = 8
D_MODEL = 2048
HEAD_DIM = 64
D_ATTN = 1024
N_Q_HEADS = 16
D_KV = 128
BLOCK = 128
D_SGU = 1024
SGU_GROUPS = 8
D_IN = 5376
W_IN_SHARD = D_IN // N_DEV
W_OUT_SHARD = D_MODEL // N_DEV
W_ADA_SHARD = 3 * D_MODEL // N_DEV
EPS = 1e-6
ATTN_SCALE = 0.125

ADAM_LR = 0.001
ADAM_B1 = 0.9
ADAM_B2 = 0.999
ADAM_EPS = 1e-08
ADAM_WD = 0.01
ADAM_STEP = 10

SEG_Q, SEG_KV, SEG_GA, SEG_U, SEG_VS, SEG_GS = 0, 1024, 1280, 2304, 3328, 4352

VMEM_LIMIT = 56 * 1024 * 1024

ROW_SHIFT, ROW_SCALE, ROW_GATE, ROW_NORM_G, ROW_FINAL_G, ROW_LN, ROW_MISC, ROW_SGU_B = 0, 1, 2, 3, 4, 5, 6, 8
SMALL_ROWS = 16


def _params(**kw):
    return pltpu.CompilerParams(vmem_limit_bytes=VMEM_LIMIT, **kw)


def _sigmoid(x):
    return 0.5 * (jnp.tanh(0.5 * x) + 1.0)


def _place():
    return lax.axis_index("x"), lax.axis_index("y"), lax.axis_index("c")


def _pair_reduce(blocks, name, row_chunk):
    _, _, r, cols = blocks.shape
    assert r % row_chunk == 0

    def body(in_ref, out_ref, land, own, summed, send_sems, recv_sems, own_sems, out_sems):
        x, y, c = _place()
        sends, loads, stores = [], [], []
        for m in range(4):
            cp = pltpu.make_async_remote_copy(
                src_ref=in_ref.at[m, 1 - c], dst_ref=land.at[m], send_sem=send_sems.at[m], recv_sem=recv_sems.at[m],
                device_id=(x, y, 1 - c), device_id_type=MESH)
            cp.start()
            sends.append(cp)
            ld = pltpu.make_async_copy(in_ref.at[m, c], own.at[m], own_sems.at[m])
            ld.start()
            loads.append(ld)
        for m in range(4):
            sends[m].wait_recv()
            loads[m].wait()
            for k in range(r // row_chunk):
                rows = slice(k * row_chunk, (k + 1) * row_chunk)
                summed[m, rows, :] = (own[m, rows, :].astype(F32) + land[m, rows, :].astype(F32)).astype(BF16)
            st = pltpu.make_async_copy(summed.at[m], out_ref.at[m], out_sems.at[m])
            st.start()
            stores.append(st)
        for m in range(4):
            sends[m].wait_send()
            stores[m].wait()

    spec = pl.BlockSpec(memory_space=pl.ANY)
    return pl.pallas_call(
        body, name=name, out_shape=jax.ShapeDtypeStruct((4, r, cols), BF16),
        in_specs=[spec], out_specs=spec,
        scratch_shapes=[pltpu.VMEM((4, r, cols), BF16), pltpu.VMEM((4, r, cols), BF16), pltpu.VMEM((4, r, cols), BF16),
                        pltpu.SemaphoreType.DMA((4,)), pltpu.SemaphoreType.DMA((4,)), pltpu.SemaphoreType.DMA((4,)),
                        pltpu.SemaphoreType.DMA((4,))],
        compiler_params=_params(),
    )(blocks)


_HBM = pl.BlockSpec(memory_space=pltpu.HBM)
_SEM = pl.BlockSpec(memory_space=pltpu.SEMAPHORE)
_EFFECT = pltpu.SideEffectType.DATAFLOW_SIDE_EFFECTING


def _start_copies(bufs, copies, n_copies, after, name):
    nb = len(bufs)

    def body(*refs):
        for cp in copies(refs[:nb], refs[nb + 1], refs[nb + 2]):
            cp.start()
        refs[-1][...] = jnp.zeros_like(refs[-1])

    out = pl.pallas_call(
        body, name=name,
        out_shape=(pltpu.SemaphoreType.DMA((n_copies,)), pltpu.SemaphoreType.DMA((n_copies,)),
                   *[pltpu.HBM(b.shape, b.dtype) for b in bufs], jax.ShapeDtypeStruct((8, 128), F32)),
        in_specs=(_HBM,) * nb + (pl.BlockSpec(memory_space=pl.ANY),),
        out_specs=(_SEM, _SEM) + (_HBM,) * nb + (pl.BlockSpec(memory_space=pltpu.VMEM),),
        input_output_aliases={i: 2 + i for i in range(nb)},
        compiler_params=pltpu.CompilerParams(has_side_effects=_EFFECT),
    )(*[pltpu.with_memory_space_constraint(b, pltpu.HBM) for b in bufs], after)
    return out[0], out[1], list(out[2:2 + nb]), out[-1]


def _start_own_block(block, targets, n_copies, after, name):
    def body(block_ref, buf_ref, after_ref, send_sems, recv_sems, buf_thru, block_thru, token, local_sem):
        x, y, c = _place()
        mine = buf_ref.at[4 * x + 2 * y + c]
        local = pltpu.make_async_copy(block_ref, mine, local_sem)
        local.start()
        for k, to in enumerate(targets(x, y, c)):
            pltpu.make_async_remote_copy(
                src_ref=block_ref, dst_ref=mine, send_sem=send_sems.at[k], recv_sem=recv_sems.at[k],
                device_id=to, device_id_type=MESH).start()
        local.wait()
        token[...] = jnp.zeros_like(token)

    shape = (N_DEV,) + block.shape
    out = pl.pallas_call(
        body, name=name,
        out_shape=(pltpu.SemaphoreType.DMA((n_copies,)), pltpu.SemaphoreType.DMA((n_copies,)),
                   pltpu.HBM(shape, block.dtype), pltpu.HBM(block.shape, block.dtype),
                   jax.ShapeDtypeStruct((8, 128), F32)),
        in_specs=(_HBM, _HBM, pl.BlockSpec(memory_space=pl.ANY)),
        out_specs=(_SEM, _SEM, _HBM, _HBM, pl.BlockSpec(memory_space=pltpu.VMEM)),
        input_output_aliases={0: 3, 1: 2},
        scratch_shapes=[pltpu.SemaphoreType.DMA],
        compiler_params=pltpu.CompilerParams(has_side_effects=_EFFECT),
    )(pltpu.with_memory_space_constraint(block, pltpu.HBM),
      pltpu.with_memory_space_constraint(lax.empty(shape, block.dtype), pltpu.HBM), after)
    return out[0], out[1], [out[2], out[3]], out[4]


def _wait_copies(flight, copies, after, name):
    send_sems, recv_sems, bufs, _ = flight
    nb = len(bufs)

    def body(*refs):
        for cp in copies(refs[:nb], refs[nb], refs[nb + 1]):
            cp.wait_send()
            cp.wait_recv()

    return pl.pallas_call(
        body, name=name,
        out_shape=tuple(pltpu.HBM(b.shape, b.dtype) for b in bufs),
        in_specs=(_HBM,) * nb + (_SEM, _SEM, pl.BlockSpec(memory_space=pl.ANY)), out_specs=(_HBM,) * nb,
        input_output_aliases={i: i for i in range(nb)},
        compiler_params=pltpu.CompilerParams(has_side_effects=_EFFECT),
    )(*bufs, send_sems, recv_sems, after)


def _wait_then_start(flight, waited, started, n_started, after, name):
    old_send, old_recv, bufs, _ = flight
    nb = len(bufs)

    def body(*refs):
        for cp in waited(refs[:nb], refs[nb], refs[nb + 1]):
            cp.wait_send()
            cp.wait_recv()
        for cp in started(refs[:nb], refs[nb + 3], refs[nb + 4]):
            cp.start()
        refs[-1][...] = jnp.zeros_like(refs[-1])

    out = pl.pallas_call(
        body, name=name,
        out_shape=(pltpu.SemaphoreType.DMA((n_started,)), pltpu.SemaphoreType.DMA((n_started,)),
                   *[pltpu.HBM(b.shape, b.dtype) for b in bufs], jax.ShapeDtypeStruct((8, 128), F32)),
        in_specs=(_HBM,) * nb + (_SEM, _SEM, pl.BlockSpec(memory_space=pl.ANY)),
        out_specs=(_SEM, _SEM) + (_HBM,) * nb + (pl.BlockSpec(memory_space=pltpu.VMEM),),
        input_output_aliases={i: 2 + i for i in range(nb)},
        compiler_params=pltpu.CompilerParams(has_side_effects=_EFFECT),
    )(*bufs, old_send, old_recv, after)
    return out[0], out[1], list(out[2:2 + nb]), out[-1]


def _chip_copies(refs, send_sems, recv_sems):
    pair_ref, land_ref = refs
    x, y, c = _place()
    chips = [(1 - x, y), (x, 1 - y), (1 - x, 1 - y)]
    return [pltpu.make_async_remote_copy(
        src_ref=pair_ref.at[2 * chip[0] + chip[1]], dst_ref=land_ref.at[k],
        send_sem=send_sems.at[k], recv_sem=recv_sems.at[k],
        device_id=(*chip, c), device_id_type=MESH) for k, chip in enumerate(chips)]


def _first_hop_copies(refs, send_sems, recv_sems):
    pair_ref, land_ref = refs
    x, y, c = _place()
    first = ((x + 1 - c) % 2, (y + c) % 2)
    blocks = [2 * first[0] + first[1], 2 * (1 - x) + (1 - y)]
    return [pltpu.make_async_remote_copy(
        src_ref=pair_ref.at[blocks[k]], dst_ref=land_ref.at[k], send_sem=send_sems.at[k], recv_sem=recv_sems.at[k],
        device_id=(*first, c), device_id_type=MESH) for k in range(2)]


def _second_hop_copies(refs, send_sems, recv_sems):
    relay_ref, land_ref = refs
    x, y, c = _place()
    second = ((x + c) % 2, (y + 1 - c) % 2)
    return [pltpu.make_async_remote_copy(
        src_ref=relay_ref, dst_ref=land_ref.at[0], send_sem=send_sems.at[0], recv_sem=recv_sems.at[0],
        device_id=(*second, c), device_id_type=MESH)]


def _own_block_copies(targets):
    def copies(refs, send_sems, recv_sems):
        x, y, c = _place()
        mine = refs[0].at[4 * x + 2 * y + c]
        return [pltpu.make_async_remote_copy(
            src_ref=mine, dst_ref=mine, send_sem=send_sems.at[k], recv_sem=recv_sems.at[k],
            device_id=to, device_id_type=MESH) for k, to in enumerate(targets(x, y, c))]
    return copies


def _my_core_and_sibling(x, y, c):
    return [(x, y, 1 - c), (1 - x, y, c), (x, 1 - y, c), (1 - x, 1 - y, c)]


def _all_others(x, y, c):
    flip = lambda v, f: 1 - v if f else v
    return [(flip(x, r & 4), flip(y, r & 2), flip(c, r & 1)) for r in range(1, N_DEV)]


def _forward_copies(refs, send_sems, recv_sems):
    x, y, c = _place()
    chips = [(1 - x, y), (x, 1 - y), (1 - x, 1 - y)]
    return [pltpu.make_async_remote_copy(
        src_ref=refs[0].at[4 * chip[0] + 2 * chip[1] + c], dst_ref=refs[0].at[4 * chip[0] + 2 * chip[1] + c],
        send_sem=send_sems.at[k], recv_sem=recv_sems.at[k],
        device_id=(x, y, 1 - c), device_id_type=MESH) for k, chip in enumerate(chips)]


def _first_axis_chip(x, y, c):
    return (x + 1 - c) % 2, (y + c) % 2


def _second_axis_chip(x, y, c):
    return (x + c) % 2, (y + 1 - c) % 2


def _first_targets(x, y, c):
    return [(x, y, 1 - c), (*_first_axis_chip(x, y, c), c)]


def _all_gather_small(shard, name):
    def body(in_ref, out_ref, send_sems, recv_sems, local_sem):
        x, y, c = _place()
        me, sibling = 4 * x + 2 * y + c, (x, y, 1 - c)
        first, second = _first_axis_chip(x, y, c), _second_axis_chip(x, y, c)

        def pair(chip):
            return out_ref.at[pl.ds(2 * (2 * chip[0] + chip[1]), 2)]

        def exchange(k, src, dst, to):
            cp = pltpu.make_async_remote_copy(src_ref=src, dst_ref=dst, send_sem=send_sems.at[k],
                                              recv_sem=recv_sems.at[k], device_id=to, device_id_type=MESH)
            cp.start()
            cp.wait()

        own = pltpu.make_async_copy(in_ref, out_ref.at[me], local_sem)
        own.start()
        exchange(0, in_ref, out_ref.at[me], sibling)
        own.wait()
        exchange(1, pair((x, y)), pair((x, y)), (*second, c))
        exchange(2, pair(second), pair(second), sibling)
        exchange(3, pair(first), pair(first), (*second, c))

    spec = pl.BlockSpec(memory_space=pltpu.VMEM)
    return pl.pallas_call(
        body, name=name, out_shape=jax.ShapeDtypeStruct((N_DEV,) + shard.shape, shard.dtype),
        in_specs=[spec], out_specs=spec,
        scratch_shapes=[pltpu.SemaphoreType.DMA((4,)), pltpu.SemaphoreType.DMA((4,)), pltpu.SemaphoreType.DMA],
        compiler_params=_params(),
    )(shard)


def _slot_copies(refs, send_sems, recv_sems, plan):
    copies = []
    for k, ((px, py, pc), to) in enumerate(plan):
        blk = refs[0].at[4 * px + 2 * py + pc]
        copies.append(pltpu.make_async_remote_copy(
            src_ref=blk, dst_ref=blk, send_sem=send_sems.at[k], recv_sem=recv_sems.at[k],
            device_id=to, device_id_type=MESH))
    return copies


def _second_axis_stage_copies(refs, send_sems, recv_sems):
    x, y, c = _place()
    first, second = (*_first_axis_chip(x, y, c), c), (*_second_axis_chip(x, y, c), c)
    return _slot_copies(refs, send_sems, recv_sems, [((x, y, c), second), (first, (x, y, 1 - c)), (first, second)])


def _second_axis_forward_copies(refs, send_sems, recv_sems):
    x, y, c = _place()
    return _slot_copies(refs, send_sems, recv_sems, [((*_second_axis_chip(x, y, c), c), (x, y, 1 - c))])


def _diagonal_forward_copies(refs, send_sems, recv_sems):
    x, y, c = _place()
    blk = refs[0].at[4 * (1 - x) + 2 * (1 - y) + c]
    return [pltpu.make_async_remote_copy(
        src_ref=blk, dst_ref=blk, send_sem=send_sems.at[0], recv_sem=recv_sems.at[0],
        device_id=(x, y, 1 - c), device_id_type=MESH)]


def _matmul(a, b, dims, out_dtype, tm, tn, name, dep=None):
    if dims == "nn":
        (m, k), n = a.shape, b.shape[1]
        a_spec = pl.BlockSpec((tm, k), lambda i, j: (i, 0))
        b_spec = pl.BlockSpec((k, tn), lambda i, j: (0, j))
        contract = ((1,), (0,))
    elif dims == "nt":
        (m, k), n = a.shape, b.shape[0]
        a_spec = pl.BlockSpec((tm, k), lambda i, j: (i, 0))
        b_spec = pl.BlockSpec((tn, k), lambda i, j: (j, 0))
        contract = ((1,), (1,))
    else:
        (k, m), n = a.shape, b.shape[1]
        a_spec = pl.BlockSpec((k, tm), lambda i, j: (0, i))
        b_spec = pl.BlockSpec((k, tn), lambda i, j: (0, j))
        contract = ((0,), (0,))
    assert m % tm == 0 and n % tn == 0 and a.dtype == BF16 and b.dtype == BF16

    def body(a_ref, b_ref, *rest):
        rest[-1][...] = lax.dot_general(a_ref[...], b_ref[...], (contract, ((), ())),
                                        preferred_element_type=F32).astype(out_dtype)

    deps = [] if dep is None else [dep]
    return pl.pallas_call(
        body, name=name, grid=(m // tm, n // tn),
        in_specs=[a_spec, b_spec] + [pl.BlockSpec((8, 128), lambda i, j: (0, 0))] * len(deps),
        out_specs=pl.BlockSpec((tm, tn), lambda i, j: (i, j)),
        out_shape=jax.ShapeDtypeStruct((m, n), out_dtype),
        compiler_params=_params(dimension_semantics=("arbitrary", "arbitrary")),
    )(a, b, *deps)


Z_TILE = 768
_Z_TILE_ORDER = ((0, 1, 2, 3, 4, 5, 6), (2, 0, 1, 6, 3, 4, 5), (4, 0, 5, 6, 1, 2, 3), (6, 2, 3, 4, 0, 1, 5))
_Z_EARLY_TILES = 4


def _z_proj(h, w_in_t, order, first, count, z_prev, name, dep=None):
    t = h.shape[0]

    def body(order_ref, h_ref, w_ref, *rest):
        rest[-1][...] = _dot_nt(h_ref[...], w_ref[...])

    prev = [] if z_prev is None else [z_prev]
    deps = [] if dep is None else [dep]
    return pl.pallas_call(
        body, name=name,
        grid_spec=pltpu.PrefetchScalarGridSpec(
            num_scalar_prefetch=1, grid=(count,),
            in_specs=[pl.BlockSpec((t, D_MODEL), lambda j, o: (0, 0)),
                      pl.BlockSpec((Z_TILE, D_MODEL), lambda j, o: (o[first + j], 0))]
            + [pl.BlockSpec(memory_space=pl.ANY)] * len(prev)
            + [pl.BlockSpec((8, 128), lambda j, o: (0, 0))] * len(deps),
            out_specs=pl.BlockSpec((t, Z_TILE), lambda j, o: (0, o[first + j]))),
        out_shape=jax.ShapeDtypeStruct((t, D_IN), F32),
        input_output_aliases={3: 0} if prev else {},
        compiler_params=_params(dimension_semantics=("arbitrary",)),
    )(order, h, w_in_t, *prev, *deps)


def _modulation(c_all, w_ada, b_ada_mine):
    def body(c_ref, w_ref, b_ref, act_ref, mod_ref):
        cv = c_ref[...]
        act = cv * _sigmoid(cv)
        act_ref[...] = act
        mod_ref[...] = jnp.dot(act.astype(BF16), w_ref[...].astype(BF16), preferred_element_type=F32) + b_ref[...]

    return pl.pallas_call(
        body, name="modulation",
        out_shape=(jax.ShapeDtypeStruct(c_all.shape, F32), jax.ShapeDtypeStruct((N_DEV, W_ADA_SHARD), F32)),
        compiler_params=_params(),
    )(c_all, w_ada, b_ada_mine)


def _modulated_norm(x, norm_g, scale, shift, tm=256):
    t, d = x.shape

    def body(x_ref, g_ref, sc_ref, sh_ref, h_ref):
        xv = x_ref[...]
        r = lax.rsqrt(jnp.mean(xv * xv, axis=-1, keepdims=True) + EPS)
        h = (xv * r) * g_ref[...] * (1.0 + sc_ref[...]) + sh_ref[...]
        h_ref[...] = h.astype(BF16)

    row = pl.BlockSpec((1, d), lambda i: (0, 0))
    return pl.pallas_call(
        body, name="modulated_norm", grid=(t // tm,),
        in_specs=[pl.BlockSpec((tm, d), lambda i: (i, 0)), row, row, row],
        out_specs=pl.BlockSpec((tm, d), lambda i: (i, 0)),
        out_shape=jax.ShapeDtypeStruct((t, d), BF16),
        compiler_params=_params(dimension_semantics=("arbitrary",)),
    )(x, norm_g, scale, shift)


def _window_bias(block_index):
    s = lax.broadcasted_iota(jnp.int32, (2 * BLOCK, BLOCK), 0)
    t = lax.broadcasted_iota(jnp.int32, (2 * BLOCK, BLOCK), 1)
    valid = ((s < BLOCK) & (s > t) & (block_index > 0)) | ((s >= BLOCK) & ((s - BLOCK) <= t))
    return jnp.where(valid, 0.0, -jnp.inf).astype(F32)


def _heads_t(pair_blocks, g):
    top = lax.broadcasted_iota(jnp.int32, (BLOCK, BLOCK), 0) < HEAD_DIM
    zeros = jnp.zeros((HEAD_DIM, BLOCK), F32)
    tiles = []
    for blk in pair_blocks:
        tp = blk.T
        if g == 0:
            tiles += [jnp.where(top, tp, 0.0), jnp.concatenate([tp[HEAD_DIM:], zeros], axis=0)]
        else:
            tiles += [jnp.concatenate([zeros, tp[:HEAD_DIM]], axis=0), jnp.where(top, 0.0, tp)]
    return jnp.concatenate(tiles, axis=1)


def _pair_block(xt, p, g):
    r0 = HEAD_DIM * g
    even = xt[r0:r0 + HEAD_DIM, (2 * p) * BLOCK:(2 * p + 1) * BLOCK]
    odd = xt[r0:r0 + HEAD_DIM, (2 * p + 1) * BLOCK:(2 * p + 2) * BLOCK]
    return jnp.concatenate([even, odd], axis=0).T


def _softmax_t(scores_t, bias, sink):
    probs, sink_probs = [], []
    for j in range(8):
        lanes = slice(j * BLOCK, (j + 1) * BLOCK)
        st, sk = scores_t[:, lanes] + bias, sink[:, lanes]
        m = jnp.maximum(jnp.max(st, axis=0, keepdims=True), sk)
        e = jnp.exp(st - m)
        es = jnp.exp(sk - m)
        inv = 1.0 / (jnp.sum(e, axis=0, keepdims=True) + es)
        probs.append(e * inv)
        sink_probs.append(es * inv)
    return jnp.concatenate(probs, axis=1), jnp.concatenate(sink_probs, axis=1)


def _dot(a, b):
    return jnp.dot(a, b, preferred_element_type=F32)


def _dot_nt(a, b):
    return lax.dot_general(a, b, (((1,), (1,)), ((), ())), preferred_element_type=F32)


def _layer_norm_fwd(v):
    mu = jnp.mean(v, axis=-1, keepdims=True)
    xc = v - mu
    rstd = lax.rsqrt(jnp.mean(xc * xc, axis=-1, keepdims=True) + EPS)
    return xc * rstd, rstd


def _tril(transposed=False):
    t = lax.broadcasted_iota(jnp.int32, (BLOCK, BLOCK), 0)
    s = lax.broadcasted_iota(jnp.int32, (BLOCK, BLOCK), 1)
    return s >= t if transposed else t >= s


def _const_spec(shape):
    return pl.BlockSpec(shape, lambda i: (0,) * len(shape))


def _kv_prev_spec(index):
    return pl.BlockSpec((BLOCK, 2 * D_KV), lambda i: (jnp.maximum(index(i) - 1, 0), SEG_KV // (2 * D_KV)))


def _keys_values(z_ref, kvp_ref):
    kvp, kvc = kvp_ref[...], z_ref[:, SEG_KV:SEG_KV + 2 * D_KV]
    kk = jnp.concatenate([kvp[:, :D_KV], kvc[:, :D_KV]], axis=0)
    vv = jnp.concatenate([kvp[:, D_KV:], kvc[:, D_KV:]], axis=0)
    return kk, vv


def _pair_cols(g, p, base=0):
    return slice(base + (4 * g + p) * 128, base + (4 * g + p + 1) * 128)


def _mixer_fwd(z, sink_rows, ln_g, ln_b, sgu_w, sgu_bt):
    t = z.shape[0]

    def body(z_ref, kvp_ref, sink_ref, lng_ref, lnb_ref, w_ref, bt_ref, a_ref):
        bias = _window_bias(pl.program_id(0))
        kk, vv = _keys_values(z_ref, kvp_ref)
        kk_b, vvt_b = kk.astype(BF16), vv.T.astype(BF16)
        for g in range(2):
            qt = _heads_t([z_ref[:, _pair_cols(g, p, SEG_Q)] * ATTN_SCALE for p in range(4)], g).astype(BF16)
            prob, _ = _softmax_t(_dot(kk_b, qt), bias, sink_ref[g])
            ot = _dot(vvt_b, prob.astype(BF16))
            for p in range(4):
                gate = z_ref[:, _pair_cols(g, p, SEG_GA)]
                a_ref[:, _pair_cols(g, p)] = (_pair_block(ot, p, g) * (gate * _sigmoid(gate))).astype(BF16)

        vhat, _ = _layer_norm_fwd(z_ref[:, SEG_VS:SEG_VS + D_SGU])
        vn = vhat * lng_ref[...] + lnb_ref[...]
        tril = _tril()
        for g in range(SGU_GROUPS):
            cols = slice(g * 128, (g + 1) * 128)
            wm = jnp.where(tril, w_ref[g], 0.0).astype(BF16)
            mixed = _dot(wm, vn[:, cols].astype(BF16)) + bt_ref[:, g:g + 1]
            gate = z_ref[:, SEG_GS + g * 128:SEG_GS + (g + 1) * 128]
            a_ref[:, D_ATTN + g * 128:D_ATTN + (g + 1) * 128] = (
                (z_ref[:, SEG_U + g * 128:SEG_U + (g + 1) * 128] * mixed) * (gate * _sigmoid(gate))).astype(BF16)

    return pl.pallas_call(
        body, name="mixer_fwd", grid=(t // BLOCK,),
        in_specs=[pl.BlockSpec((BLOCK, D_IN), lambda i: (i, 0)), _kv_prev_spec(lambda i: i),
                  _const_spec((2, 1, 8 * BLOCK)), _const_spec((1, D_SGU)), _const_spec((1, D_SGU)),
                  _const_spec((SGU_GROUPS, BLOCK, BLOCK)), _const_spec((BLOCK, SGU_GROUPS))],
        out_specs=pl.BlockSpec((BLOCK, D_MODEL), lambda i: (i, 0)),
        out_shape=jax.ShapeDtypeStruct((t, D_MODEL), BF16),
        compiler_params=_params(dimension_semantics=("arbitrary",)),
    )(z, z, sink_rows, ln_g, ln_b, sgu_w, sgu_bt)


def _mixer_bwd(z, da, sink_rows, ln_g, ln_b, sgu_w, sgu_wt, sgu_bt):
    t = z.shape[0]
    nb = t // BLOCK

    def body(z_ref, kvp_ref, da_ref, sink_ref, lng_ref, lnb_ref, w_ref, wt_ref, bt_ref,
             dz_ref, dsink_ref, dw_ref, db_ref, dlng_ref, dlnb_ref, carry_ref, dsink_acc, dbt_acc):
        step = pl.program_id(0)

        @pl.when(step == 0)
        def _():
            carry_ref[...] = jnp.zeros_like(carry_ref)
            dsink_acc[...] = jnp.zeros_like(dsink_acc)
            dbt_acc[...] = jnp.zeros_like(dbt_acc)
            dw_ref[...] = jnp.zeros_like(dw_ref)
            dlng_ref[...] = jnp.zeros_like(dlng_ref)
            dlnb_ref[...] = jnp.zeros_like(dlnb_ref)

        bias = _window_bias(nb - 1 - step)
        kk, vv = _keys_values(z_ref, kvp_ref)
        kk_b, vv_b = kk.astype(BF16), vv.astype(BF16)
        kkt_b, vvt_b = kk.T.astype(BF16), vv.T.astype(BF16)
        dkk = jnp.zeros((2 * BLOCK, D_KV), F32)
        dvv = jnp.zeros((2 * BLOCK, D_KV), F32)
        for g in range(2):
            qt = _heads_t([z_ref[:, _pair_cols(g, p, SEG_Q)] * ATTN_SCALE for p in range(4)], g).astype(BF16)
            prob, sink_prob = _softmax_t(_dot(kk_b, qt), bias, sink_ref[g])
            prob_b = prob.astype(BF16)
            ot = _dot(vvt_b, prob_b)
            gates = [z_ref[:, _pair_cols(g, p, SEG_GA)] for p in range(4)]
            sig = [_sigmoid(gt) for gt in gates]
            d_attn = [da_ref[:, _pair_cols(g, p)] for p in range(4)]
            d_ot = _heads_t([d_attn[p] * (gates[p] * sig[p]) for p in range(4)], g).astype(BF16)
            d_prob = _dot(vv_b, d_ot)
            d_scores = []
            for j in range(8):
                lanes = slice(j * BLOCK, (j + 1) * BLOCK)
                p_j, dp_j = prob[:, lanes], d_prob[:, lanes]
                delta = jnp.sum(p_j * dp_j, axis=0, keepdims=True)
                d_scores.append((p_j * (dp_j - delta)).astype(BF16))
                dsink_acc[g, :, lanes] -= sink_prob[:, lanes] * delta
            d_scores = jnp.concatenate(d_scores, axis=1)
            d_qt = _dot(kkt_b, d_scores)
            dkk = dkk + _dot_nt(d_scores, qt)
            dvv = dvv + _dot_nt(prob_b, d_ot)
            for p in range(4):
                dz_ref[:, _pair_cols(g, p, SEG_Q)] = (_pair_block(d_qt, p, g) * ATTN_SCALE).astype(BF16)
                d_silu = sig[p] * (1.0 + gates[p] * (1.0 - sig[p]))
                dz_ref[:, _pair_cols(g, p, SEG_GA)] = (d_attn[p] * _pair_block(ot, p, g) * d_silu).astype(BF16)
        d_kv = jnp.concatenate([dkk, dvv], axis=1)
        dz_ref[:, SEG_KV:SEG_KV + 2 * D_KV] = (d_kv[BLOCK:] + carry_ref[...]).astype(BF16)
        carry_ref[...] = d_kv[:BLOCK]

        vhat, rstd = _layer_norm_fwd(z_ref[:, SEG_VS:SEG_VS + D_SGU])
        lng = lng_ref[...]
        vn = vhat * lng + lnb_ref[...]
        tril, triu = _tril(), _tril(transposed=True)
        lane = lax.broadcasted_iota(jnp.int32, (BLOCK, 128), 1)
        d_bt = jnp.zeros((BLOCK, 128), F32)
        d_vn = []
        for g in range(SGU_GROUPS):
            cols = slice(g * 128, (g + 1) * 128)
            wm = jnp.where(tril, w_ref[g], 0.0).astype(BF16)
            wmt = jnp.where(triu, wt_ref[g], 0.0).astype(BF16)
            vn_g = vn[:, cols].astype(BF16)
            mixed = _dot(wm, vn_g) + bt_ref[:, g:g + 1]
            gate = z_ref[:, SEG_GS + g * 128:SEG_GS + (g + 1) * 128]
            u = z_ref[:, SEG_U + g * 128:SEG_U + (g + 1) * 128]
            d_out = da_ref[:, D_ATTN + g * 128:D_ATTN + (g + 1) * 128]
            sg = _sigmoid(gate)
            d_um = d_out * (gate * sg)
            dz_ref[:, SEG_U + g * 128:SEG_U + (g + 1) * 128] = (d_um * mixed).astype(BF16)
            dz_ref[:, SEG_GS + g * 128:SEG_GS + (g + 1) * 128] = (
                d_out * (u * mixed) * (sg * (1.0 + gate * (1.0 - sg)))).astype(BF16)
            d_mixed = d_um * u
            d_mixed_b = d_mixed.astype(BF16)
            dw_ref[g] += jnp.where(tril, _dot_nt(d_mixed_b, vn_g), 0.0)
            d_bt = d_bt + jnp.where(lane == g, jnp.sum(d_mixed, axis=-1, keepdims=True), 0.0)
            d_vn.append(_dot(wmt, d_mixed_b))
        dbt_acc[...] += d_bt
        d_vn = jnp.concatenate(d_vn, axis=1)
        dlng_ref[...] += jnp.sum(d_vn * vhat, axis=0, keepdims=True)
        dlnb_ref[...] += jnp.sum(d_vn, axis=0, keepdims=True)
        d_vhat = d_vn * lng
        d_v = rstd * (d_vhat - jnp.mean(d_vhat, axis=-1, keepdims=True)
                      - vhat * jnp.mean(d_vhat * vhat, axis=-1, keepdims=True))
        dz_ref[:, SEG_VS:SEG_VS + D_SGU] = d_v.astype(BF16)

        @pl.when(step == nb - 1)
        def _():
            db_ref[...] = dbt_acc[...].T[:SGU_GROUPS]
            lane_row = lax.broadcasted_iota(jnp.int32, (1, 128), 1)
            d_sink = jnp.zeros((1, 128), F32)
            for g in range(2):
                acc = dsink_acc[g]
                for j in range(8):
                    head_sum = jnp.sum(acc[:, j * BLOCK:(j + 1) * BLOCK], axis=-1, keepdims=True)
                    d_sink = d_sink + jnp.where(lane_row == 8 * g + j, head_sum, 0.0)
            dsink_ref[...] = d_sink

    rev = lambda i: nb - 1 - i
    return pl.pallas_call(
        body, name="mixer_bwd", grid=(nb,),
        in_specs=[pl.BlockSpec((BLOCK, D_IN), lambda i: (rev(i), 0)), _kv_prev_spec(rev),
                  pl.BlockSpec((BLOCK, D_MODEL), lambda i: (rev(i), 0)),
                  _const_spec((2, 1, 8 * BLOCK)), _const_spec((1, D_SGU)), _const_spec((1, D_SGU)),
                  _const_spec((SGU_GROUPS, BLOCK, BLOCK)), _const_spec((SGU_GROUPS, BLOCK, BLOCK)),
                  _const_spec((BLOCK, SGU_GROUPS))],
        out_specs=(pl.BlockSpec((BLOCK, D_IN), lambda i: (rev(i), 0)), _const_spec((1, 128)),
                   _const_spec((SGU_GROUPS, BLOCK, BLOCK)), _const_spec((SGU_GROUPS, BLOCK)),
                   _const_spec((1, D_SGU)), _const_spec((1, D_SGU))),
        out_shape=(jax.ShapeDtypeStruct((t, D_IN), BF16), jax.ShapeDtypeStruct((1, 128), F32),
                   jax.ShapeDtypeStruct((SGU_GROUPS, BLOCK, BLOCK), F32), jax.ShapeDtypeStruct((SGU_GROUPS, BLOCK), F32),
                   jax.ShapeDtypeStruct((1, D_SGU), F32), jax.ShapeDtypeStruct((1, D_SGU), F32)),
        scratch_shapes=[pltpu.VMEM((BLOCK, 2 * D_KV), F32), pltpu.VMEM((2, 1, 8 * BLOCK), F32),
                        pltpu.VMEM((BLOCK, 128), F32)],
        compiler_params=_params(dimension_semantics=("arbitrary",)),
    )(z, z, da, sink_rows, ln_g, ln_b, sgu_w, sgu_wt, sgu_bt)


def _out_proj_head(a, w_out_full, x, target, gate, final_g, tm=256):
    t, d = x.shape

    def body(a_ref, w_ref, x_ref, tg_ref, gate_ref, fg_ref, dx2_ref, dy_ref, loss_ref, dfg_ref, dgate_ref):
        @pl.when(pl.program_id(0) == 0)
        def _():
            loss_ref[...] = jnp.zeros_like(loss_ref)
            dfg_ref[...] = jnp.zeros_like(dfg_ref)
            dgate_ref[...] = jnp.zeros_like(dgate_ref)

        yv, gate, fg = _dot(a_ref[...], w_ref[...]), gate_ref[...], fg_ref[...]
        x2 = x_ref[...] + gate * yv
        r2 = lax.rsqrt(jnp.mean(x2 * x2, axis=-1, keepdims=True) + EPS)
        nrm = x2 * r2
        err = nrm * fg - tg_ref[...]
        loss_ref[...] += 0.5 * jnp.sum(jnp.mean(err * err, axis=-1, keepdims=True), axis=0, keepdims=True)
        d_out = err * (1.0 / d)
        dfg_ref[...] += jnp.sum(d_out * nrm, axis=0, keepdims=True)
        d_nrm = d_out * fg
        dx2 = r2 * (d_nrm - nrm * jnp.mean(d_nrm * nrm, axis=-1, keepdims=True))
        dx2_ref[...] = dx2
        dgate_ref[...] += jnp.sum(dx2 * yv, axis=0, keepdims=True)
        dy_ref[...] = (dx2 * gate).astype(BF16)

    blk = pl.BlockSpec((tm, d), lambda i: (i, 0))
    row = _const_spec((1, d))
    whole = pl.BlockSpec(w_out_full.shape, lambda i: (0, 0), pipeline_mode=pl.Buffered(1))
    return pl.pallas_call(
        body, name="out_proj_head", grid=(t // tm,),
        in_specs=[pl.BlockSpec((tm, a.shape[1]), lambda i: (i, 0)), whole, blk, blk, row, row],
        out_specs=(blk, blk, _const_spec((1, 128)), row, row),
        out_shape=(jax.ShapeDtypeStruct((t, d), F32), jax.ShapeDtypeStruct((t, d), BF16),
                   jax.ShapeDtypeStruct((1, 128), F32), jax.ShapeDtypeStruct((1, d), F32),
                   jax.ShapeDtypeStruct((1, d), F32)),
        compiler_params=_params(dimension_semantics=("arbitrary",)),
    )(a, w_out_full, x, target, gate, final_g)


def _z_proj_bwd_norm(dz, w_in_t, x, dx2, norm_g, scale, dep, tm=256):
    t, d = x.shape

    def body(dz_ref, w_ref, x_ref, dx2_ref, g_ref, sc_ref, dep_ref, gx_ref, dshift_ref, dscale_ref, dg_ref):
        @pl.when(pl.program_id(0) == 0)
        def _():
            dshift_ref[...] = jnp.zeros_like(dshift_ref)
            dscale_ref[...] = jnp.zeros_like(dscale_ref)
            dg_ref[...] = jnp.zeros_like(dg_ref)

        dh, xv, g = _dot(dz_ref[...], w_ref[...]), x_ref[...], g_ref[...]
        one_plus = 1.0 + sc_ref[...]
        r = lax.rsqrt(jnp.mean(xv * xv, axis=-1, keepdims=True) + EPS)
        xn = xv * r
        dshift_ref[...] += jnp.sum(dh, axis=0, keepdims=True)
        dscale_ref[...] += jnp.sum(dh * (xn * g), axis=0, keepdims=True)
        d_y = dh * one_plus
        dg_ref[...] += jnp.sum(d_y * xn, axis=0, keepdims=True)
        d_xn = d_y * g
        gx_ref[...] = dx2_ref[...] + r * (d_xn - xn * jnp.mean(d_xn * xn, axis=-1, keepdims=True))

    blk = pl.BlockSpec((tm, d), lambda i: (i, 0))
    row = _const_spec((1, d))
    whole = pl.BlockSpec(w_in_t.shape, lambda i: (0, 0), pipeline_mode=pl.Buffered(1))
    return pl.pallas_call(
        body, name="z_proj_bwd_norm", grid=(t // tm,),
        in_specs=[pl.BlockSpec((tm, dz.shape[1]), lambda i: (i, 0)), whole, blk, blk, row, row, _const_spec((8, 128))],
        out_specs=(blk, row, row, row),
        out_shape=(jax.ShapeDtypeStruct((t, d), F32),) + (jax.ShapeDtypeStruct((1, d), F32),) * 3,
        compiler_params=_params(dimension_semantics=("arbitrary",)),
    )(dz, w_in_t, x, dx2, norm_g, scale, dep)


def _adamw(w, g, m, v):
    m = ADAM_B1 * m + (1.0 - ADAM_B1) * g
    v = ADAM_B2 * v + (1.0 - ADAM_B2) * (g * g)
    m_hat = m / (1.0 - ADAM_B1 ** ADAM_STEP)
    v_hat = v / (1.0 - ADAM_B2 ** ADAM_STEP)
    delta = -ADAM_LR * (m_hat / (jnp.sqrt(v_hat) + ADAM_EPS) + ADAM_WD * w)
    return delta, m, v


def _relay_sum(second_chip, pair, land, tr):
    _, r, c = pair.shape

    def body(chip_ref, a_ref, b_ref, o_ref):
        o_ref[...] = (a_ref[...].astype(F32) + b_ref[...].astype(F32)).astype(BF16)

    return pl.pallas_call(
        body, name="w_in_grad_relay_sum",
        grid_spec=pltpu.PrefetchScalarGridSpec(
            num_scalar_prefetch=1, grid=(r // tr,),
            in_specs=[pl.BlockSpec((None, tr, c), lambda i, chip_ref: (chip_ref[0], i, 0)),
                      pl.BlockSpec((None, tr, c), lambda i, chip_ref: (1, i, 0))],
            out_specs=pl.BlockSpec((tr, c), lambda i, chip_ref: (i, 0))),
        out_shape=jax.ShapeDtypeStruct((r, c), BF16),
        compiler_params=_params(dimension_semantics=("arbitrary",)),
    )(second_chip, pair, land)


def _adam_from_chips(chip, pair, landed, w, m, v, name, tc):
    _, r, c = pair.shape
    n = len(landed)

    def body(chip_ref, own_ref, *refs):
        w_ref, m_ref, v_ref, g_ref, d_ref, nm_ref, nv_ref = refs[n:]
        g = own_ref[...].astype(F32)
        for k in range(n):
            g = g + refs[k][...].astype(F32)
        g_ref[...] = g
        d_ref[...], nm_ref[...], nv_ref[...] = _adamw(w_ref[...], g, m_ref[...], v_ref[...])

    def landed_spec(index):
        return pl.BlockSpec((None, r, tc), lambda i, chip_ref: (index, 0, i))

    blk = pl.BlockSpec((r, tc), lambda i, chip_ref: (0, i))
    return pl.pallas_call(
        body, name=name,
        grid_spec=pltpu.PrefetchScalarGridSpec(
            num_scalar_prefetch=1, grid=(c // tc,),
            in_specs=[pl.BlockSpec((None, r, tc), lambda i, chip_ref: (chip_ref[0], 0, i))]
            + [landed_spec(index) for _, index in landed] + [blk, blk, blk],
            out_specs=(blk,) * 4),
        out_shape=(jax.ShapeDtypeStruct((r, c), F32),) * 4,
        compiler_params=_params(dimension_semantics=("arbitrary",)),
    )(chip, pair, *[array for array, _ in landed], w, m, v)


def _adam_w_ada(act_t, dmod_mine, w, m, v, tr=256):
    r, c = w.shape

    def body(a_ref, dm_ref, w_ref, m_ref, v_ref, g_ref, d_ref, nm_ref, nv_ref):
        g = _dot(a_ref[...].astype(BF16), dm_ref[...].astype(BF16))
        g_ref[...] = g
        d_ref[...], nm_ref[...], nv_ref[...] = _adamw(w_ref[...], g, m_ref[...], v_ref[...])

    blk = pl.BlockSpec((tr, c), lambda i: (i, 0))
    return pl.pallas_call(
        body, name="adam_w_ada", grid=(r // tr,),
        in_specs=[pl.BlockSpec((tr, N_DEV), lambda i: (i, 0)), _const_spec((N_DEV, c)), blk, blk, blk],
        out_specs=(blk,) * 4, out_shape=(jax.ShapeDtypeStruct((r, c), F32),) * 4,
        compiler_params=_params(dimension_semantics=("arbitrary",)),
    )(act_t, dmod_mine, w, m, v)


def _pack_small(d_shift, d_scale, d_gate, d_norm_g, d_final_g, d_ln_g, d_ln_b, loss, d_sinks, d_sgu_b):
    def body(shift_ref, scale_ref, gate_ref, ng_ref, fg_ref, lng_ref, lnb_ref, loss_ref, sink_ref, b_ref, o_ref):
        o_ref[...] = jnp.zeros_like(o_ref)
        o_ref[ROW_SHIFT:ROW_SHIFT + 1, :] = shift_ref[...]
        o_ref[ROW_SCALE:ROW_SCALE + 1, :] = scale_ref[...]
        o_ref[ROW_GATE:ROW_GATE + 1, :] = gate_ref[...]
        o_ref[ROW_NORM_G:ROW_NORM_G + 1, :] = ng_ref[...]
        o_ref[ROW_FINAL_G:ROW_FINAL_G + 1, :] = fg_ref[...]
        o_ref[ROW_LN:ROW_LN + 1, 0:D_SGU] = lng_ref[...]
        o_ref[ROW_LN:ROW_LN + 1, D_SGU:2 * D_SGU] = lnb_ref[...]
        o_ref[ROW_MISC:ROW_MISC + 1, 0:128] = loss_ref[...]
        o_ref[ROW_MISC:ROW_MISC + 1, 128:256] = sink_ref[...]
        o_ref[ROW_SGU_B:ROW_SGU_B + SGU_GROUPS, 0:BLOCK] = b_ref[...]

    return pl.pallas_call(
        body, name="pack_small", out_shape=jax.ShapeDtypeStruct((SMALL_ROWS, D_MODEL), F32),
        compiler_params=_params(),
    )(d_shift, d_scale, d_gate, d_norm_g, d_final_g, d_ln_g, d_ln_b, loss, d_sinks, d_sgu_b)


_SMALL_NAMES = ("norm_g", "b_ada", "attn_sinks", "sgu_ln_g", "sgu_ln_b", "sgu_w", "sgu_b", "final_g")


def _adam_small(partials, d_sgu_w_all, weights, moments_m, moments_v):
    names = _SMALL_NAMES
    k = len(names)

    def body(*refs):
        p_ref, sw_ref = refs[0], refs[1]
        w_refs, m_refs, v_refs = refs[2:2 + k], refs[2 + k:2 + 2 * k], refs[2 + 2 * k:2 + 3 * k]
        loss_ref, dmod_ref = refs[2 + 3 * k], refs[3 + 3 * k]
        out_refs = refs[4 + 3 * k:4 + 7 * k]
        sum_ref = refs[4 + 7 * k]
        total = p_ref[0]
        for j in range(1, N_DEV):
            total = total + p_ref[j]
        sum_ref[...] = total
        for j in range(N_DEV):
            for part, row in enumerate((ROW_SHIFT, ROW_SCALE, ROW_GATE)):
                dmod_ref[j:j + 1, part * D_MODEL:(part + 1) * D_MODEL] = p_ref[j, row:row + 1, :]
        loss_ref[...] = sum_ref[ROW_MISC:ROW_MISC + 1, 0:1]
        d_sgu_w = sw_ref[0]
        for j in range(1, N_DEV):
            d_sgu_w = d_sgu_w + sw_ref[j]
        grads = {
            "norm_g": sum_ref[ROW_NORM_G:ROW_NORM_G + 1, :],
            "b_ada": jnp.concatenate([sum_ref[r:r + 1, :] for r in (ROW_SHIFT, ROW_SCALE, ROW_GATE)], axis=1),
            "attn_sinks": sum_ref[ROW_MISC:ROW_MISC + 1, 128:128 + N_Q_HEADS],
            "sgu_ln_g": sum_ref[ROW_LN:ROW_LN + 1, 0:D_SGU],
            "sgu_ln_b": sum_ref[ROW_LN:ROW_LN + 1, D_SGU:2 * D_SGU],
            "sgu_w": d_sgu_w[None],
            "sgu_b": sum_ref[ROW_SGU_B:ROW_SGU_B + SGU_GROUPS, 0:BLOCK][None],
            "final_g": sum_ref[ROW_FINAL_G:ROW_FINAL_G + 1, :],
        }
        for i, name in enumerate(names):
            g = grads[name]
            delta, m, v = _adamw(w_refs[i][...], g, m_refs[i][...], v_refs[i][...])
            out_refs[4 * i][...] = g
            out_refs[4 * i + 1][...] = delta
            out_refs[4 * i + 2][...] = m
            out_refs[4 * i + 3][...] = v

    shapes = [jax.ShapeDtypeStruct((1, 1), F32), jax.ShapeDtypeStruct((N_DEV, 3 * D_MODEL), F32)]
    for name in names:
        shapes += [jax.ShapeDtypeStruct(weights[name].shape, F32)] * 4
    outs = pl.pallas_call(
        body, name="adam_small", out_shape=tuple(shapes),
        scratch_shapes=[pltpu.VMEM((SMALL_ROWS, D_MODEL), F32)],
        compiler_params=_params(),
    )(partials, d_sgu_w_all, *[weights[n] for n in names], *[moments_m[n] for n in names],
      *[moments_v[n] for n in names])
    return outs[0], outs[1], {name: outs[2 + 4 * i:6 + 4 * i] for i, name in enumerate(names)}


def kernel(x, c, norm_g, w_ada, b_ada, w_in, attn_sinks, sgu_ln_g, sgu_ln_b, sgu_w, sgu_b, w_out, final_g, loss_target, m_norm_g, m_w_ada, m_b_ada, m_w_in, m_attn_sinks, m_sgu_ln_g, m_sgu_ln_b, m_sgu_w, m_sgu_b, m_w_out, m_final_g, v_norm_g, v_w_ada, v_b_ada, v_w_in, v_attn_sinks, v_sgu_ln_g, v_sgu_ln_b, v_sgu_w, v_sgu_b, v_w_out, v_final_g):
    xi, yi, ci = _place()
    me = 4 * xi + 2 * yi + ci
    x2d, target = x[0], loss_target[0]
    t = x2d.shape[0]

    core = ci.astype(jnp.int32).reshape(1)
    chip = (2 * xi + yi).astype(jnp.int32).reshape(1)

    first = _own_block_copies(_first_targets)
    first_flight = _start_own_block(w_in[0].T.astype(BF16), _first_targets, 2, core, "gather_w_in_start")

    c_all = _all_gather_small(c.reshape(8, 256) + first_flight[3][0, 0], "gather_c").reshape(N_DEV, D_MODEL)
    b_mine = lax.dynamic_slice(b_ada, (0, me * W_ADA_SHARD), (1, W_ADA_SHARD))
    c_act, mod_part = _modulation(c_all, w_ada[0], b_mine)
    mod_all = _all_gather_small(mod_part, "gather_mod")

    across = _wait_then_start(first_flight, lambda *a: first(*a)[1:], _second_axis_stage_copies, 3, mod_all,
                              "gather_w_in_second_axis_stage")
    mod = lax.dynamic_index_in_dim(mod_all, me, axis=1, keepdims=False).reshape(1, 3 * D_MODEL)
    mod = mod + across[3][0, 0]
    shift, scale, gate = mod[:, :D_MODEL], mod[:, D_MODEL:2 * D_MODEL], mod[:, 2 * D_MODEL:]
    h = _modulated_norm(x2d, norm_g, scale, shift)

    w_in_pair = _wait_copies((first_flight[0], first_flight[1], across[2], None), lambda *a: first(*a)[:1], h,
                             "gather_w_in_sibling_wait")
    tile_order = jnp.asarray(_Z_TILE_ORDER, jnp.int32)[chip[0]]
    z_own = _z_proj(h, w_in_pair[0].reshape(D_IN, D_MODEL), tile_order, 0, 1, None, "z_proj_own")
    forward = _wait_then_start((across[0], across[1], w_in_pair, None), lambda *a: _second_axis_stage_copies(*a)[:1],
                               _second_axis_forward_copies, 1, z_own, "gather_w_in_second_axis_forward")
    w_out_flight = _start_own_block(w_out[0].astype(BF16), _my_core_and_sibling, 4, forward[3], "gather_w_out_start")
    w_in_most = _wait_copies((across[0], across[1], forward[2], None),
                             lambda *a: _second_axis_stage_copies(*a)[1:2], w_out_flight[3],
                             "gather_w_in_first_forward_wait")
    w_in_most = _wait_copies((forward[0], forward[1], w_in_most, None), _second_axis_forward_copies, z_own,
                             "gather_w_in_second_forward_wait")
    z_early = _z_proj(h, w_in_most[0].reshape(D_IN, D_MODEL), tile_order, 1, _Z_EARLY_TILES - 1, z_own, "z_proj_early")
    w_in_flight = _wait_then_start((across[0], across[1], w_in_most, None),
                                   lambda *a: _second_axis_stage_copies(*a)[2:], _diagonal_forward_copies, 1,
                                   z_early, "gather_w_in_last_stage")
    w_in_all = _wait_copies(w_in_flight, _diagonal_forward_copies, z_early, "gather_w_in_last_wait")[0]
    w_in_t = w_in_all.reshape(D_IN, D_MODEL)
    z = _z_proj(h, w_in_t, tile_order, _Z_EARLY_TILES, 7 - _Z_EARLY_TILES, z_early, "z_proj_late")
    w_out_flight = _wait_then_start(w_out_flight, _own_block_copies(_my_core_and_sibling), _forward_copies, 3, z,
                                    "gather_w_out_forward_stage")
    sink_rows = jnp.repeat(attn_sinks.reshape(N_Q_HEADS), BLOCK).reshape(2, 1, 8 * BLOCK)
    sgu_bt = sgu_b[0].T
    a = _mixer_fwd(z, sink_rows + w_out_flight[3][0, 0], sgu_ln_g, sgu_ln_b, sgu_w[0], sgu_bt)
    w_out_all = _wait_copies(w_out_flight, _forward_copies, a, "gather_w_out_forward_wait")[0]
    w_out_full = w_out_all.reshape(D_MODEL, D_MODEL)
    final_g_row = final_g.reshape(1, D_MODEL)
    dx2, dy, loss_part, d_final_g, d_gate = _out_proj_head(a, w_out_full, x2d, target, gate, final_g_row)

    da = _matmul(dy, w_out_full, "nt", F32, min(t, 1024), 1024, "out_proj_bwd")
    dw_out = _matmul(a, dy, "tn", BF16, 1024, 1024, "w_out_grad").reshape(4, 2, W_OUT_SHARD, D_MODEL)
    pair_out = _pair_reduce(dw_out, "w_out_grad_pair_reduce", W_OUT_SHARD // 2)
    out_flight = _start_copies([pair_out, lax.empty((3, W_OUT_SHARD, D_MODEL), BF16)], _chip_copies, 3, core,
                               "w_out_grad_chip_start")
    dz, d_sinks, d_sgu_w, d_sgu_b, d_ln_g, d_ln_b = _mixer_bwd(
        z, da, sink_rows + out_flight[3][0, 0], sgu_ln_g, sgu_ln_b, sgu_w[0], jnp.swapaxes(sgu_w[0], 1, 2), sgu_bt)
    sgu_w_flight = _start_own_block(d_sgu_w, _all_others, N_DEV - 1, core, "sgu_w_grad_gather_start")
    dw_in_t = _matmul(dz, h, "tn", BF16, 768, D_MODEL, "w_in_grad", dep=sgu_w_flight[3])
    dw_in_t = dw_in_t.reshape(4, 2, W_IN_SHARD, D_MODEL)
    pair_in = _pair_reduce(dw_in_t, "w_in_grad_pair_reduce", W_IN_SHARD // 3)
    hop1 = _start_copies([pair_in, lax.empty((2, W_IN_SHARD, D_MODEL), BF16)], _first_hop_copies, 2, core,
                         "w_in_grad_first_hop_start")
    grad_x, d_shift, d_scale, d_norm_g = _z_proj_bwd_norm(dz, w_in_t, x2d, dx2, norm_g, scale, hop1[3])

    partial = _pack_small(d_shift, d_scale, d_gate, d_norm_g, d_final_g, d_ln_g, d_ln_b, loss_part, d_sinks, d_sgu_b)
    small_flight = _start_own_block(partial, _all_others, N_DEV - 1, core, "small_grad_gather_start")
    pair_in, land_first = _wait_copies(hop1, _first_hop_copies, small_flight[3], "w_in_grad_first_hop_wait")
    second_chip = (2 * ((xi + ci) % 2) + (yi + 1 - ci) % 2).astype(jnp.int32).reshape(1)
    relay = _relay_sum(second_chip, pair_in, land_first, W_IN_SHARD // 3)
    hop2 = _start_copies([relay, lax.empty((1, W_IN_SHARD, D_MODEL), BF16)], _second_hop_copies, 1, core,
                         "w_in_grad_second_hop_start")
    pair_out, land_out = _wait_copies(out_flight, _chip_copies, hop2[3], "w_out_grad_chip_wait")
    big = {"w_out": _adam_from_chips(chip, pair_out, [(land_out, k) for k in range(3)], w_out[0], m_w_out[0],
                                     v_w_out[0], "adam_w_out", 1024)}
    partial_all = _wait_copies(small_flight, _own_block_copies(_all_others), big["w_out"][0],
                               "small_grad_gather_wait")[0]
    d_sgu_w_all = _wait_copies(sgu_w_flight, _own_block_copies(_all_others), partial_all, "sgu_w_grad_gather_wait")[0]
    weights = {"norm_g": norm_g, "b_ada": b_ada, "attn_sinks": attn_sinks, "sgu_ln_g": sgu_ln_g,
               "sgu_ln_b": sgu_ln_b, "sgu_w": sgu_w, "sgu_b": sgu_b, "final_g": final_g_row}
    moments_m = {"norm_g": m_norm_g, "b_ada": m_b_ada, "attn_sinks": m_attn_sinks, "sgu_ln_g": m_sgu_ln_g,
                 "sgu_ln_b": m_sgu_ln_b, "sgu_w": m_sgu_w, "sgu_b": m_sgu_b,
                 "final_g": m_final_g.reshape(1, D_MODEL)}
    moments_v = {"norm_g": v_norm_g, "b_ada": v_b_ada, "attn_sinks": v_attn_sinks, "sgu_ln_g": v_sgu_ln_g,
                 "sgu_ln_b": v_sgu_ln_b, "sgu_w": v_sgu_w, "sgu_b": v_sgu_b,
                 "final_g": v_final_g.reshape(1, D_MODEL)}
    loss, dmod_all, small = _adam_small(partial_all, d_sgu_w_all, weights, moments_m, moments_v)
    small["final_g"] = tuple(o.reshape(D_MODEL) for o in small["final_g"])

    dmod_mine = lax.dynamic_slice(dmod_all, (0, me * W_ADA_SHARD), (N_DEV, W_ADA_SHARD))
    big["w_ada"] = _adam_w_ada(c_act.T, dmod_mine, w_ada[0], m_w_ada[0], v_w_ada[0])
    _, land_second = _wait_copies(hop2, _second_hop_copies, big["w_ada"][0], "w_in_grad_second_hop_wait")
    big["w_in"] = tuple(o.T for o in _adam_from_chips(
        chip, pair_in, [(land_first, 0), (land_second, 0)], w_in[0].T, m_w_in[0].T, v_w_in[0].T, "adam_w_in", 256))
    order = ["norm_g", "w_ada", "b_ada", "w_in", "attn_sinks", "sgu_ln_g", "sgu_ln_b", "sgu_w", "sgu_b", "w_out",
             "final_g"]
    outs = [loss.reshape(()), grad_x[None]]
    for k in range(4):
        for name in order:
            outs.append(big[name][k][None] if name in big else small[name][k])
    return tuple(outs)
```

```python
import jax
import jax.numpy as jnp
from jax import lax
from jax.experimental import pallas as pl
from jax.experimental.pallas import tpu as pltpu

F32 = jnp.float32
BF16 = jnp.bfloat16
MESH = pl.DeviceIdType.MESH

N_DEV = 8
D_MODEL = 2048
HEAD_DIM = 64
D_ATTN = 1024
N_Q_HEADS = 16
D_KV = 128
BLOCK = 128
D_SGU = 1024
SGU_GROUPS = 8
D_IN = 5376
W_IN_SHARD = D_IN // N_DEV
W_OUT_SHARD = D_MODEL // N_DEV
W_ADA_SHARD = 3 * D_MODEL // N_DEV
EPS = 1e-6
ATTN_SCALE = 0.125

ADAM_LR = 0.001
ADAM_B1 = 0.9
ADAM_B2 = 0.999
ADAM_EPS = 1e-08
ADAM_WD = 0.01
ADAM_STEP = 10

SEG_Q, SEG_KV, SEG_GA, SEG_U, SEG_VS, SEG_GS = 0, 1024, 1280, 2304, 3328, 4352

VMEM_LIMIT = 56 * 1024 * 1024

ROW_SHIFT, ROW_SCALE, ROW_GATE, ROW_NORM_G, ROW_FINAL_G, ROW_LN, ROW_MISC, ROW_SGU_B = 0, 1, 2, 3, 4, 5, 6, 8
SMALL_ROWS = 16


def _params(**kw):
    return pltpu.CompilerParams(vmem_limit_bytes=VMEM_LIMIT, **kw)


def _sigmoid(x):
    return 0.5 * (jnp.tanh(0.5 * x) + 1.0)


def _place():
    return lax.axis_index("x"), lax.axis_index("y"), lax.axis_index("c")


def _pair_reduce(blocks, name, row_chunk):
    _, _, r, cols = blocks.shape
    assert r % row_chunk == 0

    def body(in_ref, out_ref, land, own, summed, send_sems, recv_sems, own_sems, out_sems):
        x, y, c = _place()
        sends, loads, stores = [], [], []
        for m in range(4):
            cp = pltpu.make_async_remote_copy(
                src_ref=in_ref.at[m, 1 - c], dst_ref=land.at[m], send_sem=send_sems.at[m], recv_sem=recv_sems.at[m],
                device_id=(x, y, 1 - c), device_id_type=MESH)
            cp.start()
            sends.append(cp)
            ld = pltpu.make_async_copy(in_ref.at[m, c], own.at[m], own_sems.at[m])
            ld.start()
            loads.append(ld)
        for m in range(4):
            sends[m].wait_recv()
            loads[m].wait()
            for k in range(r // row_chunk):
                rows = slice(k * row_chunk, (k + 1) * row_chunk)
                summed[m, rows, :] = (own[m, rows, :].astype(F32) + land[m, rows, :].astype(F32)).astype(BF16)
            st = pltpu.make_async_copy(summed.at[m], out_ref.at[m], out_sems.at[m])
            st.start()
            stores.append(st)
        for m in range(4):
            sends[m].wait_send()
            stores[m].wait()

    spec = pl.BlockSpec(memory_space=pl.ANY)
    return pl.pallas_call(
        body, name=name, out_shape=jax.ShapeDtypeStruct((4, r, cols), BF16),
        in_specs=[spec], out_specs=spec,
        scratch_shapes=[pltpu.VMEM((4, r, cols), BF16), pltpu.VMEM((4, r, cols), BF16), pltpu.VMEM((4, r, cols), BF16),
                        pltpu.SemaphoreType.DMA((4,)), pltpu.SemaphoreType.DMA((4,)), pltpu.SemaphoreType.DMA((4,)),
                        pltpu.SemaphoreType.DMA((4,))],
        compiler_params=_params(),
    )(blocks)


_HBM = pl.BlockSpec(memory_space=pltpu.HBM)
_SEM = pl.BlockSpec(memory_space=pltpu.SEMAPHORE)
_EFFECT = pltpu.SideEffectType.DATAFLOW_SIDE_EFFECTING


def _start_copies(bufs, copies, n_copies, after, name):
    nb = len(bufs)

    def body(*refs):
        for cp in copies(refs[:nb], refs[nb + 1], refs[nb + 2]):
            cp.start()
        refs[-1][...] = jnp.zeros_like(refs[-1])

    out = pl.pallas_call(
        body, name=name,
        out_shape=(pltpu.SemaphoreType.DMA((n_copies,)), pltpu.SemaphoreType.DMA((n_copies,)),
                   *[pltpu.HBM(b.shape, b.dtype) for b in bufs], jax.ShapeDtypeStruct((8, 128), F32)),
        in_specs=(_HBM,) * nb + (pl.BlockSpec(memory_space=pl.ANY),),
        out_specs=(_SEM, _SEM) + (_HBM,) * nb + (pl.BlockSpec(memory_space=pltpu.VMEM),),
        input_output_aliases={i: 2 + i for i in range(nb)},
        compiler_params=pltpu.CompilerParams(has_side_effects=_EFFECT),
    )(*[pltpu.with_memory_space_constraint(b, pltpu.HBM) for b in bufs], after)
    return out[0], out[1], list(out[2:2 + nb]), out[-1]


def _start_own_block(block, targets, n_copies, after, name):
    def body(block_ref, buf_ref, after_ref, send_sems, recv_sems, buf_thru, block_thru, token):
        x, y, c = _place()
        mine = buf_ref.at[4 * x + 2 * y + c]
        for k, to in enumerate(targets(x, y, c)):
            pltpu.make_async_remote_copy(
                src_ref=block_ref, dst_ref=mine, send_sem=send_sems.at[k], recv_sem=recv_sems.at[k],
                device_id=to, device_id_type=MESH).start()
        pltpu.make_async_copy(block_ref, mine, send_sems.at[n_copies]).start()
        token[...] = jnp.zeros_like(token)

    shape = (N_DEV,) + block.shape
    out = pl.pallas_call(
        body, name=name,
        out_shape=(pltpu.SemaphoreType.DMA((n_copies + 1,)), pltpu.SemaphoreType.DMA((n_copies,)),
                   pltpu.HBM(shape, block.dtype), pltpu.HBM(block.shape, block.dtype),
                   jax.ShapeDtypeStruct((8, 128), F32)),
        in_specs=(_HBM, _HBM, pl.BlockSpec(memory_space=pl.ANY)),
        out_specs=(_SEM, _SEM, _HBM, _HBM, pl.BlockSpec(memory_space=pltpu.VMEM)),
        input_output_aliases={0: 3, 1: 2},
        compiler_params=pltpu.CompilerParams(has_side_effects=_EFFECT),
    )(pltpu.with_memory_space_constraint(block, pltpu.HBM),
      pltpu.with_memory_space_constraint(lax.empty(shape, block.dtype), pltpu.HBM), after)
    return out[0], out[1], [out[2], out[3]], out[4]


class _LocalCopy:
    def __init__(self, copy):
        self.copy = copy

    def wait_send(self):
        self.copy.wait()

    def wait_recv(self):
        pass


def _own_fill(n_copies):
    def copies(refs, send_sems, recv_sems):
        x, y, c = _place()
        return [_LocalCopy(pltpu.make_async_copy(refs[1], refs[0].at[4 * x + 2 * y + c], send_sems.at[n_copies]))]
    return copies


def _wait_copies(flight, copies, after, name):
    send_sems, recv_sems, bufs, _ = flight
    nb = len(bufs)

    def body(*refs):
        for cp in copies(refs[:nb], refs[nb], refs[nb + 1]):
            cp.wait_send()
            cp.wait_recv()

    return pl.pallas_call(
        body, name=name,
        out_shape=tuple(pltpu.HBM(b.shape, b.dtype) for b in bufs),
        in_specs=(_HBM,) * nb + (_SEM, _SEM, pl.BlockSpec(memory_space=pl.ANY)), out_specs=(_HBM,) * nb,
        input_output_aliases={i: i for i in range(nb)},
        compiler_params=pltpu.CompilerParams(has_side_effects=_EFFECT),
    )(*bufs, send_sems, recv_sems, after)


def _wait_then_start(flight, waited, started, n_started, after, name):
    old_send, old_recv, bufs, _ = flight
    nb = len(bufs)

    def body(*refs):
        for cp in waited(refs[:nb], refs[nb], refs[nb + 1]):
            cp.wait_send()
            cp.wait_recv()
        for cp in started(refs[:nb], refs[nb + 3], refs[nb + 4]):
            cp.start()
        refs[-1][...] = jnp.zeros_like(refs[-1])

    out = pl.pallas_call(
        body, name=name,
        out_shape=(pltpu.SemaphoreType.DMA((n_started,)), pltpu.SemaphoreType.DMA((n_started,)),
                   *[pltpu.HBM(b.shape, b.dtype) for b in bufs], jax.ShapeDtypeStruct((8, 128), F32)),
        in_specs=(_HBM,) * nb + (_SEM, _SEM, pl.BlockSpec(memory_space=pl.ANY)),
        out_specs=(_SEM, _SEM) + (_HBM,) * nb + (pl.BlockSpec(memory_space=pltpu.VMEM),),
        input_output_aliases={i: 2 + i for i in range(nb)},
        compiler_params=pltpu.CompilerParams(has_side_effects=_EFFECT),
    )(*bufs, old_send, old_recv, after)
    return out[0], out[1], list(out[2:2 + nb]), out[-1]


def _chip_copies(refs, send_sems, recv_sems):
    pair_ref, land_ref = refs
    x, y, c = _place()
    chips = [(1 - x, y), (x, 1 - y), (1 - x, 1 - y)]
    return [pltpu.make_async_remote_copy(
        src_ref=pair_ref.at[2 * chip[0] + chip[1]], dst_ref=land_ref.at[k],
        send_sem=send_sems.at[k], recv_sem=recv_sems.at[k],
        device_id=(*chip, c), device_id_type=MESH) for k, chip in enumerate(chips)]


def _first_hop_copies(refs, send_sems, recv_sems):
    pair_ref, land_ref = refs
    x, y, c = _place()
    first = ((x + 1 - c) % 2, (y + c) % 2)
    blocks = [2 * first[0] + first[1], 2 * (1 - x) + (1 - y)]
    return [pltpu.make_async_remote_copy(
        src_ref=pair_ref.at[blocks[k]], dst_ref=land_ref.at[k], send_sem=send_sems.at[k], recv_sem=recv_sems.at[k],
        device_id=(*first, c), device_id_type=MESH) for k in range(2)]


def _second_hop_copies(refs, send_sems, recv_sems):
    relay_ref, land_ref = refs
    x, y, c = _place()
    second = ((x + c) % 2, (y + 1 - c) % 2)
    return [pltpu.make_async_remote_copy(
        src_ref=relay_ref, dst_ref=land_ref.at[0], send_sem=send_sems.at[0], recv_sem=recv_sems.at[0],
        device_id=(*second, c), device_id_type=MESH)]


def _own_block_copies(targets):
    def copies(refs, send_sems, recv_sems):
        x, y, c = _place()
        mine = refs[0].at[4 * x + 2 * y + c]
        return [pltpu.make_async_remote_copy(
            src_ref=mine, dst_ref=mine, send_sem=send_sems.at[k], recv_sem=recv_sems.at[k],
            device_id=to, device_id_type=MESH) for k, to in enumerate(targets(x, y, c))]
    return copies


def _my_core_and_sibling(x, y, c):
    return [(x, y, 1 - c), (1 - x, y, c), (x, 1 - y, c), (1 - x, 1 - y, c)]


def _all_others(x, y, c):
    flip = lambda v, f: 1 - v if f else v
    return [(flip(x, r & 4), flip(y, r & 2), flip(c, r & 1)) for r in range(1, N_DEV)]


def _forward_copies(refs, send_sems, recv_sems):
    x, y, c = _place()
    chips = [(1 - x, y), (x, 1 - y), (1 - x, 1 - y)]
    return [pltpu.make_async_remote_copy(
        src_ref=refs[0].at[4 * chip[0] + 2 * chip[1] + c], dst_ref=refs[0].at[4 * chip[0] + 2 * chip[1] + c],
        send_sem=send_sems.at[k], recv_sem=recv_sems.at[k],
        device_id=(x, y, 1 - c), device_id_type=MESH) for k, chip in enumerate(chips)]


def _first_axis_chip(x, y, c):
    return (x + 1 - c) % 2, (y + c) % 2


def _second_axis_chip(x, y, c):
    return (x + c) % 2, (y + 1 - c) % 2


def _first_targets(x, y, c):
    return [(x, y, 1 - c), (*_first_axis_chip(x, y, c), c)]


def _all_gather_small(shard, name):
    def body(in_ref, out_ref, send_sems, recv_sems, local_sem):
        x, y, c = _place()
        me, sibling = 4 * x + 2 * y + c, (x, y, 1 - c)
        first, second = _first_axis_chip(x, y, c), _second_axis_chip(x, y, c)

        def pair(chip):
            return out_ref.at[pl.ds(2 * (2 * chip[0] + chip[1]), 2)]

        def exchange(k, src, dst, to):
            cp = pltpu.make_async_remote_copy(src_ref=src, dst_ref=dst, send_sem=send_sems.at[k],
                                              recv_sem=recv_sems.at[k], device_id=to, device_id_type=MESH)
            cp.start()
            cp.wait()

        own = pltpu.make_async_copy(in_ref, out_ref.at[me], local_sem)
        own.start()
        exchange(0, in_ref, out_ref.at[me], sibling)
        own.wait()
        exchange(1, pair((x, y)), pair((x, y)), (*second, c))
        exchange(2, pair(second), pair(second), sibling)
        exchange(3, pair(first), pair(first), (*second, c))

    spec = pl.BlockSpec(memory_space=pltpu.VMEM)
    return pl.pallas_call(
        body, name=name, out_shape=jax.ShapeDtypeStruct((N_DEV,) + shard.shape, shard.dtype),
        in_specs=[spec], out_specs=spec,
        scratch_shapes=[pltpu.SemaphoreType.DMA((4,)), pltpu.SemaphoreType.DMA((4,)), pltpu.SemaphoreType.DMA],
        compiler_params=_params(),
    )(shard)


def _slot_copies(refs, send_sems, recv_sems, plan):
    copies = []
    for k, ((px, py, pc), to) in enumerate(plan):
        blk = refs[0].at[4 * px + 2 * py + pc]
        copies.append(pltpu.make_async_remote_copy(
            src_ref=blk, dst_ref=blk, send_sem=send_sems.at[k], recv_sem=recv_sems.at[k],
            device_id=to, device_id_type=MESH))
    return copies


def _second_axis_stage_copies(refs, send_sems, recv_sems):
    x, y, c = _place()
    first, second = (*_first_axis_chip(x, y, c), c), (*_second_axis_chip(x, y, c), c)
    return _slot_copies(refs, send_sems, recv_sems, [((x, y, c), second), (first, (x, y, 1 - c)), (first, second)])


def _second_axis_forward_copies(refs, send_sems, recv_sems):
    x, y, c = _place()
    return _slot_copies(refs, send_sems, recv_sems, [((*_second_axis_chip(x, y, c), c), (x, y, 1 - c))])


def _diagonal_forward_copies(refs, send_sems, recv_sems):
    x, y, c = _place()
    blk = refs[0].at[4 * (1 - x) + 2 * (1 - y) + c]
    return [pltpu.make_async_remote_copy(
        src_ref=blk, dst_ref=blk, send_sem=send_sems.at[0], recv_sem=recv_sems.at[0],
        device_id=(x, y, 1 - c), device_id_type=MESH)]


def _matmul(a, b, dims, out_dtype, tm, tn, name, dep=None):
    if dims == "nn":
        (m, k), n = a.shape, b.shape[1]
        a_spec = pl.BlockSpec((tm, k), lambda i, j: (i, 0))
        b_spec = pl.BlockSpec((k, tn), lambda i, j: (0, j))
        contract = ((1,), (0,))
    elif dims == "nt":
        (m, k), n = a.shape, b.shape[0]
        a_spec = pl.BlockSpec((tm, k), lambda i, j: (i, 0))
        b_spec = pl.BlockSpec((tn, k), lambda i, j: (j, 0))
        contract = ((1,), (1,))
    else:
        (k, m), n = a.shape, b.shape[1]
        a_spec = pl.BlockSpec((k, tm), lambda i, j: (0, i))
        b_spec = pl.BlockSpec((k, tn), lambda i, j: (0, j))
        contract = ((0,), (0,))
    assert m % tm == 0 and n % tn == 0 and a.dtype == BF16 and b.dtype == BF16

    def body(a_ref, b_ref, *rest):
        rest[-1][...] = lax.dot_general(a_ref[...], b_ref[...], (contract, ((), ())),
                                        preferred_element_type=F32).astype(out_dtype)

    deps = [] if dep is None else [dep]
    return pl.pallas_call(
        body, name=name, grid=(m // tm, n // tn),
        in_specs=[a_spec, b_spec] + [pl.BlockSpec((8, 128), lambda i, j: (0, 0))] * len(deps),
        out_specs=pl.BlockSpec((tm, tn), lambda i, j: (i, j)),
        out_shape=jax.ShapeDtypeStruct((m, n), out_dtype),
        compiler_params=_params(dimension_semantics=("arbitrary", "arbitrary")),
    )(a, b, *deps)


Z_TILE = 768
_Z_TILE_ORDER = ((0, 1, 2, 3, 4, 5, 6), (2, 0, 1, 6, 3, 4, 5), (4, 0, 5, 6, 1, 2, 3), (6, 2, 3, 4, 0, 1, 5))
_Z_EARLY_TILES = 4


def _z_proj(h, w_in_t, order, first, count, z_prev, name, dep=None):
    t = h.shape[0]

    def body(order_ref, h_ref, w_ref, *rest):
        rest[-1][...] = _dot_nt(h_ref[...], w_ref[...])

    prev = [] if z_prev is None else [z_prev]
    deps = [] if dep is None else [dep]
    return pl.pallas_call(
        body, name=name,
        grid_spec=pltpu.PrefetchScalarGridSpec(
            num_scalar_prefetch=1, grid=(count,),
            in_specs=[pl.BlockSpec((t, D_MODEL), lambda j, o: (0, 0)),
                      pl.BlockSpec((Z_TILE, D_MODEL), lambda j, o: (o[first + j], 0))]
            + [pl.BlockSpec(memory_space=pl.ANY)] * len(prev)
            + [pl.BlockSpec((8, 128), lambda j, o: (0, 0))] * len(deps),
            out_specs=pl.BlockSpec((t, Z_TILE), lambda j, o: (0, o[first + j]))),
        out_shape=jax.ShapeDtypeStruct((t, D_IN), F32),
        input_output_aliases={3: 0} if prev else {},
        compiler_params=_params(dimension_semantics=("arbitrary",)),
    )(order, h, w_in_t, *prev, *deps)


def _modulation(c_all, w_ada, b_ada_mine):
    def body(c_ref, w_ref, b_ref, act_ref, mod_ref):
        cv = c_ref[...]
        act = cv * _sigmoid(cv)
        act_ref[...] = act
        mod_ref[...] = jnp.dot(act.astype(BF16), w_ref[...].astype(BF16), preferred_element_type=F32) + b_ref[...]

    return pl.pallas_call(
        body, name="modulation",
        out_shape=(jax.ShapeDtypeStruct(c_all.shape, F32), jax.ShapeDtypeStruct((N_DEV, W_ADA_SHARD), F32)),
        compiler_params=_params(),
    )(c_all, w_ada, b_ada_mine)


def _modulated_norm(x, norm_g, scale, shift, tm=256):
    t, d = x.shape

    def body(x_ref, g_ref, sc_ref, sh_ref, h_ref):
        xv = x_ref[...]
        r = lax.rsqrt(jnp.mean(xv * xv, axis=-1, keepdims=True) + EPS)
        h = (xv * r) * g_ref[...] * (1.0 + sc_ref[...]) + sh_ref[...]
        h_ref[...] = h.astype(BF16)

    row = pl.BlockSpec((1, d), lambda i: (0, 0))
    return pl.pallas_call(
        body, name="modulated_norm", grid=(t // tm,),
        in_specs=[pl.BlockSpec((tm, d), lambda i: (i, 0)), row, row, row],
        out_specs=pl.BlockSpec((tm, d), lambda i: (i, 0)),
        out_shape=jax.ShapeDtypeStruct((t, d), BF16),
        compiler_params=_params(dimension_semantics=("arbitrary",)),
    )(x, norm_g, scale, shift)


def _window_bias(block_index):
    s = lax.broadcasted_iota(jnp.int32, (2 * BLOCK, BLOCK), 0)
    t = lax.broadcasted_iota(jnp.int32, (2 * BLOCK, BLOCK), 1)
    valid = ((s < BLOCK) & (s > t) & (block_index > 0)) | ((s >= BLOCK) & ((s - BLOCK) <= t))
    bias = jnp.where(valid, 0.0, -jnp.inf).astype(F32)
    return jnp.concatenate([bias] * 8, axis=1)


def _heads_t(pair_blocks, g):
    top = lax.broadcasted_iota(jnp.int32, (BLOCK, BLOCK), 0) < HEAD_DIM
    zeros = jnp.zeros((HEAD_DIM, BLOCK), F32)
    tiles = []
    for blk in pair_blocks:
        tp = blk.T
        if g == 0:
            tiles += [jnp.where(top, tp, 0.0), jnp.concatenate([tp[HEAD_DIM:], zeros], axis=0)]
        else:
            tiles += [jnp.concatenate([zeros, tp[:HEAD_DIM]], axis=0), jnp.where(top, 0.0, tp)]
    return jnp.concatenate(tiles, axis=1)


def _pair_block(xt, p, g):
    r0 = HEAD_DIM * g
    even = xt[r0:r0 + HEAD_DIM, (2 * p) * BLOCK:(2 * p + 1) * BLOCK]
    odd = xt[r0:r0 + HEAD_DIM, (2 * p + 1) * BLOCK:(2 * p + 2) * BLOCK]
    return jnp.concatenate([even, odd], axis=0).T


def _softmax_t(scores_t, bias, sink):
    st = scores_t + bias
    m = jnp.maximum(jnp.max(st, axis=0, keepdims=True), sink)
    e = jnp.exp(st - m)
    es = jnp.exp(sink - m)
    inv = 1.0 / (jnp.sum(e, axis=0, keepdims=True) + es)
    return e * inv, es * inv


def _dot(a, b):
    return jnp.dot(a, b, preferred_element_type=F32)


def _dot_nt(a, b):
    return lax.dot_general(a, b, (((1,), (1,)), ((), ())), preferred_element_type=F32)


def _layer_norm_fwd(v):
    mu = jnp.mean(v, axis=-1, keepdims=True)
    xc = v - mu
    rstd = lax.rsqrt(jnp.mean(xc * xc, axis=-1, keepdims=True) + EPS)
    return xc * rstd, rstd


def _tril(transposed=False):
    t = lax.broadcasted_iota(jnp.int32, (BLOCK, BLOCK), 0)
    s = lax.broadcasted_iota(jnp.int32, (BLOCK, BLOCK), 1)
    return s >= t if transposed else t >= s


def _const_spec(shape):
    return pl.BlockSpec(shape, lambda i: (0,) * len(shape))


def _kv_prev_spec(index):
    return pl.BlockSpec((BLOCK, 2 * D_KV), lambda i: (jnp.maximum(index(i) - 1, 0), SEG_KV // (2 * D_KV)))


def _keys_values(z_ref, kvp_ref):
    kvp, kvc = kvp_ref[...], z_ref[:, SEG_KV:SEG_KV + 2 * D_KV]
    kk = jnp.concatenate([kvp[:, :D_KV], kvc[:, :D_KV]], axis=0)
    vv = jnp.concatenate([kvp[:, D_KV:], kvc[:, D_KV:]], axis=0)
    return kk, vv


def _pair_cols(g, p, base=0):
    return slice(base + (4 * g + p) * 128, base + (4 * g + p + 1) * 128)


def _mixer_fwd(z, sink_rows, ln_g, ln_b, sgu_w, sgu_bt):
    t = z.shape[0]

    def body(z_ref, kvp_ref, sink_ref, lng_ref, lnb_ref, w_ref, bt_ref, a_ref):
        bias = _window_bias(pl.program_id(0))
        kk, vv = _keys_values(z_ref, kvp_ref)
        kk_b, vvt_b = kk.astype(BF16), vv.T.astype(BF16)
        for g in range(2):
            qt = _heads_t([z_ref[:, _pair_cols(g, p, SEG_Q)] * ATTN_SCALE for p in range(4)], g).astype(BF16)
            prob, _ = _softmax_t(_dot(kk_b, qt), bias, sink_ref[g])
            ot = _dot(vvt_b, prob.astype(BF16))
            for p in range(4):
                gate = z_ref[:, _pair_cols(g, p, SEG_GA)]
                a_ref[:, _pair_cols(g, p)] = (_pair_block(ot, p, g) * (gate * _sigmoid(gate))).astype(BF16)

        vhat, _ = _layer_norm_fwd(z_ref[:, SEG_VS:SEG_VS + D_SGU])
        vn = vhat * lng_ref[...] + lnb_ref[...]
        tril = _tril()
        for g in range(SGU_GROUPS):
            cols = slice(g * 128, (g + 1) * 128)
            wm = jnp.where(tril, w_ref[g], 0.0).astype(BF16)
            mixed = _dot(wm, vn[:, cols].astype(BF16)) + bt_ref[:, g:g + 1]
            gate = z_ref[:, SEG_GS + g * 128:SEG_GS + (g + 1) * 128]
            a_ref[:, D_ATTN + g * 128:D_ATTN + (g + 1) * 128] = (
                (z_ref[:, SEG_U + g * 128:SEG_U + (g + 1) * 128] * mixed) * (gate * _sigmoid(gate))).astype(BF16)

    return pl.pallas_call(
        body, name="mixer_fwd", grid=(t // BLOCK,),
        in_specs=[pl.BlockSpec((BLOCK, D_IN), lambda i: (i, 0)), _kv_prev_spec(lambda i: i),
                  _const_spec((2, 1, 8 * BLOCK)), _const_spec((1, D_SGU)), _const_spec((1, D_SGU)),
                  _const_spec((SGU_GROUPS, BLOCK, BLOCK)), _const_spec((BLOCK, SGU_GROUPS))],
        out_specs=pl.BlockSpec((BLOCK, D_MODEL), lambda i: (i, 0)),
        out_shape=jax.ShapeDtypeStruct((t, D_MODEL), BF16),
        compiler_params=_params(dimension_semantics=("arbitrary",)),
    )(z, z, sink_rows, ln_g, ln_b, sgu_w, sgu_bt)


def _mixer_bwd(z, da, sink_rows, ln_g, ln_b, sgu_w, sgu_wt, sgu_bt):
    t = z.shape[0]
    nb = t // BLOCK

    def body(z_ref, kvp_ref, da_ref, sink_ref, lng_ref, lnb_ref, w_ref, wt_ref, bt_ref,
             dz_ref, dsink_ref, dw_ref, db_ref, dlng_ref, dlnb_ref, carry_ref, dsink_acc, dbt_acc):
        step = pl.program_id(0)

        @pl.when(step == 0)
        def _():
            carry_ref[...] = jnp.zeros_like(carry_ref)
            dsink_acc[...] = jnp.zeros_like(dsink_acc)
            dbt_acc[...] = jnp.zeros_like(dbt_acc)
            dw_ref[...] = jnp.zeros_like(dw_ref)
            dlng_ref[...] = jnp.zeros_like(dlng_ref)
            dlnb_ref[...] = jnp.zeros_like(dlnb_ref)

        bias = _window_bias(nb - 1 - step)
        kk, vv = _keys_values(z_ref, kvp_ref)
        kk_b, vv_b = kk.astype(BF16), vv.astype(BF16)
        kkt_b, vvt_b = kk.T.astype(BF16), vv.T.astype(BF16)
        dkk = jnp.zeros((2 * BLOCK, D_KV), F32)
        dvv = jnp.zeros((2 * BLOCK, D_KV), F32)
        for g in range(2):
            qt = _heads_t([z_ref[:, _pair_cols(g, p, SEG_Q)] * ATTN_SCALE for p in range(4)], g).astype(BF16)
            prob, sink_prob = _softmax_t(_dot(kk_b, qt), bias, sink_ref[g])
            prob_b = prob.astype(BF16)
            ot = _dot(vvt_b, prob_b)
            gates = [z_ref[:, _pair_cols(g, p, SEG_GA)] for p in range(4)]
            sig = [_sigmoid(gt) for gt in gates]
            d_attn = [da_ref[:, _pair_cols(g, p)] for p in range(4)]
            d_ot = _heads_t([d_attn[p] * (gates[p] * sig[p]) for p in range(4)], g).astype(BF16)
            d_prob = _dot(vv_b, d_ot)
            delta = jnp.sum(prob * d_prob, axis=0, keepdims=True)
            d_scores = (prob * (d_prob - delta)).astype(BF16)
            dsink_acc[g] -= sink_prob * delta
            d_qt = _dot(kkt_b, d_scores)
            dkk = dkk + _dot_nt(d_scores, qt)
            dvv = dvv + _dot_nt(prob_b, d_ot)
            for p in range(4):
                dz_ref[:, _pair_cols(g, p, SEG_Q)] = (_pair_block(d_qt, p, g) * ATTN_SCALE).astype(BF16)
                d_silu = sig[p] * (1.0 + gates[p] * (1.0 - sig[p]))
                dz_ref[:, _pair_cols(g, p, SEG_GA)] = (d_attn[p] * _pair_block(ot, p, g) * d_silu).astype(BF16)
        d_kv = jnp.concatenate([dkk, dvv], axis=1)
        dz_ref[:, SEG_KV:SEG_KV + 2 * D_KV] = (d_kv[BLOCK:] + carry_ref[...]).astype(BF16)
        carry_ref[...] = d_kv[:BLOCK]

        vhat, rstd = _layer_norm_fwd(z_ref[:, SEG_VS:SEG_VS + D_SGU])
        lng = lng_ref[...]
        vn = vhat * lng + lnb_ref[...]
        tril, triu = _tril(), _tril(transposed=True)
        lane = lax.broadcasted_iota(jnp.int32, (BLOCK, 128), 1)
        d_bt = jnp.zeros((BLOCK, 128), F32)
        d_vn = []
        for g in range(SGU_GROUPS):
            cols = slice(g * 128, (g + 1) * 128)
            wm = jnp.where(tril, w_ref[g], 0.0).astype(BF16)
            wmt = jnp.where(triu, wt_ref[g], 0.0).astype(BF16)
            vn_g = vn[:, cols].astype(BF16)
            mixed = _dot(wm, vn_g) + bt_ref[:, g:g + 1]
            gate = z_ref[:, SEG_GS + g * 128:SEG_GS + (g + 1) * 128]
            u = z_ref[:, SEG_U + g * 128:SEG_U + (g + 1) * 128]
            d_out = da_ref[:, D_ATTN + g * 128:D_ATTN + (g + 1) * 128]
            sg = _sigmoid(gate)
            d_um = d_out * (gate * sg)
            dz_ref[:, SEG_U + g * 128:SEG_U + (g + 1) * 128] = (d_um * mixed).astype(BF16)
            dz_ref[:, SEG_GS + g * 128:SEG_GS + (g + 1) * 128] = (
                d_out * (u * mixed) * (sg * (1.0 + gate * (1.0 - sg)))).astype(BF16)
            d_mixed = d_um * u
            d_mixed_b = d_mixed.astype(BF16)
            dw_ref[g] += jnp.where(tril, _dot_nt(d_mixed_b, vn_g), 0.0)
            d_bt = d_bt + jnp.where(lane == g, jnp.sum(d_mixed, axis=-1, keepdims=True), 0.0)
            d_vn.append(_dot(wmt, d_mixed_b))
        dbt_acc[...] += d_bt
        d_vn = jnp.concatenate(d_vn, axis=1)
        dlng_ref[...] += jnp.sum(d_vn * vhat, axis=0, keepdims=True)
        dlnb_ref[...] += jnp.sum(d_vn, axis=0, keepdims=True)
        d_vhat = d_vn * lng
        d_v = rstd * (d_vhat - jnp.mean(d_vhat, axis=-1, keepdims=True)
                      - vhat * jnp.mean(d_vhat * vhat, axis=-1, keepdims=True))
        dz_ref[:, SEG_VS:SEG_VS + D_SGU] = d_v.astype(BF16)

        @pl.when(step == nb - 1)
        def _():
            db_ref[...] = dbt_acc[...].T[:SGU_GROUPS]
            lane_row = lax.broadcasted_iota(jnp.int32, (1, 128), 1)
            d_sink = jnp.zeros((1, 128), F32)
            for g in range(2):
                acc = dsink_acc[g]
                for j in range(8):
                    head_sum = jnp.sum(acc[:, j * BLOCK:(j + 1) * BLOCK], axis=-1, keepdims=True)
                    d_sink = d_sink + jnp.where(lane_row == 8 * g + j, head_sum, 0.0)
            dsink_ref[...] = d_sink

    rev = lambda i: nb - 1 - i
    return pl.pallas_call(
        body, name="mixer_bwd", grid=(nb,),
        in_specs=[pl.BlockSpec((BLOCK, D_IN), lambda i: (rev(i), 0)), _kv_prev_spec(rev),
                  pl.BlockSpec((BLOCK, D_MODEL), lambda i: (rev(i), 0)),
                  _const_spec((2, 1, 8 * BLOCK)), _const_spec((1, D_SGU)), _const_spec((1, D_SGU)),
                  _const_spec((SGU_GROUPS, BLOCK, BLOCK)), _const_spec((SGU_GROUPS, BLOCK, BLOCK)),
                  _const_spec((BLOCK, SGU_GROUPS))],
        out_specs=(pl.BlockSpec((BLOCK, D_IN), lambda i: (rev(i), 0)), _const_spec((1, 128)),
                   _const_spec((SGU_GROUPS, BLOCK, BLOCK)), _const_spec((SGU_GROUPS, BLOCK)),
                   _const_spec((1, D_SGU)), _const_spec((1, D_SGU))),
        out_shape=(jax.ShapeDtypeStruct((t, D_IN), BF16), jax.ShapeDtypeStruct((1, 128), F32),
                   jax.ShapeDtypeStruct((SGU_GROUPS, BLOCK, BLOCK), F32), jax.ShapeDtypeStruct((SGU_GROUPS, BLOCK), F32),
                   jax.ShapeDtypeStruct((1, D_SGU), F32), jax.ShapeDtypeStruct((1, D_SGU), F32)),
        scratch_shapes=[pltpu.VMEM((BLOCK, 2 * D_KV), F32), pltpu.VMEM((2, 1, 8 * BLOCK), F32),
                        pltpu.VMEM((BLOCK, 128), F32)],
        compiler_params=_params(dimension_semantics=("arbitrary",)),
    )(z, z, da, sink_rows, ln_g, ln_b, sgu_w, sgu_wt, sgu_bt)


def _out_proj_head(a, w_out_full, x, target, gate, final_g, tm=256):
    t, d = x.shape

    def body(a_ref, w_ref, x_ref, tg_ref, gate_ref, fg_ref, dx2_ref, dy_ref, loss_ref, dfg_ref, dgate_ref):
        @pl.when(pl.program_id(0) == 0)
        def _():
            loss_ref[...] = jnp.zeros_like(loss_ref)
            dfg_ref[...] = jnp.zeros_like(dfg_ref)
            dgate_ref[...] = jnp.zeros_like(dgate_ref)

        yv, gate, fg = _dot(a_ref[...], w_ref[...]), gate_ref[...], fg_ref[...]
        x2 = x_ref[...] + gate * yv
        r2 = lax.rsqrt(jnp.mean(x2 * x2, axis=-1, keepdims=True) + EPS)
        nrm = x2 * r2
        err = nrm * fg - tg_ref[...]
        loss_ref[...] += 0.5 * jnp.sum(jnp.mean(err * err, axis=-1, keepdims=True), axis=0, keepdims=True)
        d_out = err * (1.0 / d)
        dfg_ref[...] += jnp.sum(d_out * nrm, axis=0, keepdims=True)
        d_nrm = d_out * fg
        dx2 = r2 * (d_nrm - nrm * jnp.mean(d_nrm * nrm, axis=-1, keepdims=True))
        dx2_ref[...] = dx2
        dgate_ref[...] += jnp.sum(dx2 * yv, axis=0, keepdims=True)
        dy_ref[...] = (dx2 * gate).astype(BF16)

    blk = pl.BlockSpec((tm, d), lambda i: (i, 0))
    row = _const_spec((1, d))
    whole = pl.BlockSpec(w_out_full.shape, lambda i: (0, 0), pipeline_mode=pl.Buffered(1))
    return pl.pallas_call(
        body, name="out_proj_head", grid=(t // tm,),
        in_specs=[pl.BlockSpec((tm, a.shape[1]), lambda i: (i, 0)), whole, blk, blk, row, row],
        out_specs=(blk, blk, _const_spec((1, 128)), row, row),
        out_shape=(jax.ShapeDtypeStruct((t, d), F32), jax.ShapeDtypeStruct((t, d), BF16),
                   jax.ShapeDtypeStruct((1, 128), F32), jax.ShapeDtypeStruct((1, d), F32),
                   jax.ShapeDtypeStruct((1, d), F32)),
        compiler_params=_params(dimension_semantics=("arbitrary",)),
    )(a, w_out_full, x, target, gate, final_g)


def _z_proj_bwd_norm(dz, w_in_t, x, dx2, norm_g, scale, dep, tm=256):
    t, d = x.shape

    def body(dz_ref, w_ref, x_ref, dx2_ref, g_ref, sc_ref, dep_ref, gx_ref, dshift_ref, dscale_ref, dg_ref):
        @pl.when(pl.program_id(0) == 0)
        def _():
            dshift_ref[...] = jnp.zeros_like(dshift_ref)
            dscale_ref[...] = jnp.zeros_like(dscale_ref)
            dg_ref[...] = jnp.zeros_like(dg_ref)

        dh, xv, g = _dot(dz_ref[...], w_ref[...]), x_ref[...], g_ref[...]
        one_plus = 1.0 + sc_ref[...]
        r = lax.rsqrt(jnp.mean(xv * xv, axis=-1, keepdims=True) + EPS)
        xn = xv * r
        dshift_ref[...] += jnp.sum(dh, axis=0, keepdims=True)
        dscale_ref[...] += jnp.sum(dh * (xn * g), axis=0, keepdims=True)
        d_y = dh * one_plus
        dg_ref[...] += jnp.sum(d_y * xn, axis=0, keepdims=True)
        d_xn = d_y * g
        gx_ref[...] = dx2_ref[...] + r * (d_xn - xn * jnp.mean(d_xn * xn, axis=-1, keepdims=True))

    blk = pl.BlockSpec((tm, d), lambda i: (i, 0))
    row = _const_spec((1, d))
    whole = pl.BlockSpec(w_in_t.shape, lambda i: (0, 0), pipeline_mode=pl.Buffered(1))
    return pl.pallas_call(
        body, name="z_proj_bwd_norm", grid=(t // tm,),
        in_specs=[pl.BlockSpec((tm, dz.shape[1]), lambda i: (i, 0)), whole, blk, blk, row, row, _const_spec((8, 128))],
        out_specs=(blk, row, row, row),
        out_shape=(jax.ShapeDtypeStruct((t, d), F32),) + (jax.ShapeDtypeStruct((1, d), F32),) * 3,
        compiler_params=_params(dimension_semantics=("arbitrary",)),
    )(dz, w_in_t, x, dx2, norm_g, scale, dep)


def _adamw(w, g, m, v):
    m = ADAM_B1 * m + (1.0 - ADAM_B1) * g
    v = ADAM_B2 * v + (1.0 - ADAM_B2) * (g * g)
    m_hat = m / (1.0 - ADAM_B1 ** ADAM_STEP)
    v_hat = v / (1.0 - ADAM_B2 ** ADAM_STEP)
    delta = -ADAM_LR * (m_hat / (jnp.sqrt(v_hat) + ADAM_EPS) + ADAM_WD * w)
    return delta, m, v


def _relay_sum(second_chip, pair, land, tr):
    _, r, c = pair.shape

    def body(chip_ref, a_ref, b_ref, o_ref):
        o_ref[...] = (a_ref[...].astype(F32) + b_ref[...].astype(F32)).astype(BF16)

    return pl.pallas_call(
        body, name="w_in_grad_relay_sum",
        grid_spec=pltpu.PrefetchScalarGridSpec(
            num_scalar_prefetch=1, grid=(r // tr,),
            in_specs=[pl.BlockSpec((None, tr, c), lambda i, chip_ref: (chip_ref[0], i, 0)),
                      pl.BlockSpec((None, tr, c), lambda i, chip_ref: (1, i, 0))],
            out_specs=pl.BlockSpec((tr, c), lambda i, chip_ref: (i, 0))),
        out_shape=jax.ShapeDtypeStruct((r, c), BF16),
        compiler_params=_params(dimension_semantics=("arbitrary",)),
    )(second_chip, pair, land)


def _adam_from_chips(chip, pair, landed, w, m, v, name, tc):
    _, r, c = pair.shape
    n = len(landed)

    def body(chip_ref, own_ref, *refs):
        w_ref, m_ref, v_ref, g_ref, d_ref, nm_ref, nv_ref = refs[n:]
        g = own_ref[...].astype(F32)
        for k in range(n):
            g = g + refs[k][...].astype(F32)
        g_ref[...] = g
        d_ref[...], nm_ref[...], nv_ref[...] = _adamw(w_ref[...], g, m_ref[...], v_ref[...])

    def landed_spec(index):
        return pl.BlockSpec((None, r, tc), lambda i, chip_ref: (index, 0, i))

    blk = pl.BlockSpec((r, tc), lambda i, chip_ref: (0, i))
    return pl.pallas_call(
        body, name=name,
        grid_spec=pltpu.PrefetchScalarGridSpec(
            num_scalar_prefetch=1, grid=(c // tc,),
            in_specs=[pl.BlockSpec((None, r, tc), lambda i, chip_ref: (chip_ref[0], 0, i))]
            + [landed_spec(index) for _, index in landed] + [blk, blk, blk],
            out_specs=(blk,) * 4),
        out_shape=(jax.ShapeDtypeStruct((r, c), F32),) * 4,
        compiler_params=_params(dimension_semantics=("arbitrary",)),
    )(chip, pair, *[array for array, _ in landed], w, m, v)


def _adam_w_ada(act_t, dmod_mine, w, m, v, tr=256):
    r, c = w.shape

    def body(a_ref, dm_ref, w_ref, m_ref, v_ref, g_ref, d_ref, nm_ref, nv_ref):
        g = _dot(a_ref[...].astype(BF16), dm_ref[...].astype(BF16))
        g_ref[...] = g
        d_ref[...], nm_ref[...], nv_ref[...] = _adamw(w_ref[...], g, m_ref[...], v_ref[...])

    blk = pl.BlockSpec((tr, c), lambda i: (i, 0))
    return pl.pallas_call(
        body, name="adam_w_ada", grid=(r // tr,),
        in_specs=[pl.BlockSpec((tr, N_DEV), lambda i: (i, 0)), _const_spec((N_DEV, c)), blk, blk, blk],
        out_specs=(blk,) * 4, out_shape=(jax.ShapeDtypeStruct((r, c), F32),) * 4,
        compiler_params=_params(dimension_semantics=("arbitrary",)),
    )(act_t, dmod_mine, w, m, v)


def _pack_small(d_shift, d_scale, d_gate, d_norm_g, d_final_g, d_ln_g, d_ln_b, loss, d_sinks, d_sgu_b):
    def body(shift_ref, scale_ref, gate_ref, ng_ref, fg_ref, lng_ref, lnb_ref, loss_ref, sink_ref, b_ref, o_ref):
        o_ref[...] = jnp.zeros_like(o_ref)
        o_ref[ROW_SHIFT:ROW_SHIFT + 1, :] = shift_ref[...]
        o_ref[ROW_SCALE:ROW_SCALE + 1, :] = scale_ref[...]
        o_ref[ROW_GATE:ROW_GATE + 1, :] = gate_ref[...]
        o_ref[ROW_NORM_G:ROW_NORM_G + 1, :] = ng_ref[...]
        o_ref[ROW_FINAL_G:ROW_FINAL_G + 1, :] = fg_ref[...]
        o_ref[ROW_LN:ROW_LN + 1, 0:D_SGU] = lng_ref[...]
        o_ref[ROW_LN:ROW_LN + 1, D_SGU:2 * D_SGU] = lnb_ref[...]
        o_ref[ROW_MISC:ROW_MISC + 1, 0:128] = loss_ref[...]
        o_ref[ROW_MISC:ROW_MISC + 1, 128:256] = sink_ref[...]
        o_ref[ROW_SGU_B:ROW_SGU_B + SGU_GROUPS, 0:BLOCK] = b_ref[...]

    return pl.pallas_call(
        body, name="pack_small", out_shape=jax.ShapeDtypeStruct((SMALL_ROWS, D_MODEL), F32),
        compiler_params=_params(),
    )(d_shift, d_scale, d_gate, d_norm_g, d_final_g, d_ln_g, d_ln_b, loss, d_sinks, d_sgu_b)


_SMALL_NAMES = ("norm_g", "b_ada", "attn_sinks", "sgu_ln_g", "sgu_ln_b", "sgu_w", "sgu_b", "final_g")


def _adam_small(partials, d_sgu_w_all, weights, moments_m, moments_v):
    names = _SMALL_NAMES
    k = len(names)

    def body(*refs):
        p_ref, sw_ref = refs[0], refs[1]
        w_refs, m_refs, v_refs = refs[2:2 + k], refs[2 + k:2 + 2 * k], refs[2 + 2 * k:2 + 3 * k]
        loss_ref, dmod_ref = refs[2 + 3 * k], refs[3 + 3 * k]
        out_refs = refs[4 + 3 * k:4 + 7 * k]
        sum_ref = refs[4 + 7 * k]
        total = p_ref[0]
        for j in range(1, N_DEV):
            total = total + p_ref[j]
        sum_ref[...] = total
        for j in range(N_DEV):
            for part, row in enumerate((ROW_SHIFT, ROW_SCALE, ROW_GATE)):
                dmod_ref[j:j + 1, part * D_MODEL:(part + 1) * D_MODEL] = p_ref[j, row:row + 1, :]
        loss_ref[...] = sum_ref[ROW_MISC:ROW_MISC + 1, 0:1]
        d_sgu_w = sw_ref[0]
        for j in range(1, N_DEV):
            d_sgu_w = d_sgu_w + sw_ref[j]
        grads = {
            "norm_g": sum_ref[ROW_NORM_G:ROW_NORM_G + 1, :],
            "b_ada": jnp.concatenate([sum_ref[r:r + 1, :] for r in (ROW_SHIFT, ROW_SCALE, ROW_GATE)], axis=1),
            "attn_sinks": sum_ref[ROW_MISC:ROW_MISC + 1, 128:128 + N_Q_HEADS],
            "sgu_ln_g": sum_ref[ROW_LN:ROW_LN + 1, 0:D_SGU],
            "sgu_ln_b": sum_ref[ROW_LN:ROW_LN + 1, D_SGU:2 * D_SGU],
            "sgu_w": d_sgu_w[None],
            "sgu_b": sum_ref[ROW_SGU_B:ROW_SGU_B + SGU_GROUPS, 0:BLOCK][None],
            "final_g": sum_ref[ROW_FINAL_G:ROW_FINAL_G + 1, :],
        }
        for i, name in enumerate(names):
            g = grads[name]
            delta, m, v = _adamw(w_refs[i][...], g, m_refs[i][...], v_refs[i][...])
            out_refs[4 * i][...] = g
            out_refs[4 * i + 1][...] = delta
            out_refs[4 * i + 2][...] = m
            out_refs[4 * i + 3][...] = v

    shapes = [jax.ShapeDtypeStruct((1, 1), F32), jax.ShapeDtypeStruct((N_DEV, 3 * D_MODEL), F32)]
    for name in names:
        shapes += [jax.ShapeDtypeStruct(weights[name].shape, F32)] * 4
    outs = pl.pallas_call(
        body, name="adam_small", out_shape=tuple(shapes),
        scratch_shapes=[pltpu.VMEM((SMALL_ROWS, D_MODEL), F32)],
        compiler_params=_params(),
    )(partials, d_sgu_w_all, *[weights[n] for n in names], *[moments_m[n] for n in names],
      *[moments_v[n] for n in names])
    return outs[0], outs[1], {name: outs[2 + 4 * i:6 + 4 * i] for i, name in enumerate(names)}


def kernel(x, c, norm_g, w_ada, b_ada, w_in, attn_sinks, sgu_ln_g, sgu_ln_b, sgu_w, sgu_b, w_out, final_g, loss_target, m_norm_g, m_w_ada, m_b_ada, m_w_in, m_attn_sinks, m_sgu_ln_g, m_sgu_ln_b, m_sgu_w, m_sgu_b, m_w_out, m_final_g, v_norm_g, v_w_ada, v_b_ada, v_w_in, v_attn_sinks, v_sgu_ln_g, v_sgu_ln_b, v_sgu_w, v_sgu_b, v_w_out, v_final_g):
    xi, yi, ci = _place()
    me = 4 * xi + 2 * yi + ci
    x2d, target = x[0], loss_target[0]
    t = x2d.shape[0]

    core = ci.astype(jnp.int32).reshape(1)
    chip = (2 * xi + yi).astype(jnp.int32).reshape(1)

    first = _own_block_copies(_first_targets)
    first_flight = _start_own_block(w_in[0].T.astype(BF16), _first_targets, 2, core, "gather_w_in_start")

    c_all = _all_gather_small(c.reshape(8, 256) + first_flight[3][0, 0], "gather_c").reshape(N_DEV, D_MODEL)
    b_mine = lax.dynamic_slice(b_ada, (0, me * W_ADA_SHARD), (1, W_ADA_SHARD))
    c_act, mod_part = _modulation(c_all, w_ada[0], b_mine)
    mod_all = _all_gather_small(mod_part, "gather_mod")

    across = _wait_then_start(first_flight, lambda *a: first(*a)[1:] + _own_fill(2)(*a), _second_axis_stage_copies, 3,
                              mod_all, "gather_w_in_second_axis_stage")
    mod = lax.dynamic_index_in_dim(mod_all, me, axis=1, keepdims=False).reshape(1, 3 * D_MODEL)
    mod = mod + across[3][0, 0]
    shift, scale, gate = mod[:, :D_MODEL], mod[:, D_MODEL:2 * D_MODEL], mod[:, 2 * D_MODEL:]
    h = _modulated_norm(x2d, norm_g, scale, shift)

    w_in_pair = _wait_copies((first_flight[0], first_flight[1], across[2], None), lambda *a: first(*a)[:1], h,
                             "gather_w_in_sibling_wait")
    tile_order = jnp.asarray(_Z_TILE_ORDER, jnp.int32)[chip[0]]
    z_own = _z_proj(h, w_in_pair[0].reshape(D_IN, D_MODEL), tile_order, 0, 1, None, "z_proj_own")
    forward = _wait_then_start((across[0], across[1], w_in_pair, None), lambda *a: _second_axis_stage_copies(*a)[:1],
                               _second_axis_forward_copies, 1, z_own, "gather_w_in_second_axis_forward")
    w_out_flight = _start_own_block(w_out[0].astype(BF16), _my_core_and_sibling, 4, forward[3], "gather_w_out_start")
    w_in_most = _wait_copies((across[0], across[1], forward[2], None),
                             lambda *a: _second_axis_stage_copies(*a)[1:2], w_out_flight[3],
                             "gather_w_in_first_forward_wait")
    w_in_most = _wait_copies((forward[0], forward[1], w_in_most, None), _second_axis_forward_copies, z_own,
                             "gather_w_in_second_forward_wait")
    z_early = _z_proj(h, w_in_most[0].reshape(D_IN, D_MODEL), tile_order, 1, _Z_EARLY_TILES - 1, z_own, "z_proj_early")
    w_in_flight = _wait_then_start((across[0], across[1], w_in_most, None),
                                   lambda *a: _second_axis_stage_copies(*a)[2:], _diagonal_forward_copies, 1,
                                   z_early, "gather_w_in_last_stage")
    w_in_all = _wait_copies(w_in_flight, _diagonal_forward_copies, z_early, "gather_w_in_last_wait")[0]
    w_in_t = w_in_all.reshape(D_IN, D_MODEL)
    z = _z_proj(h, w_in_t, tile_order, _Z_EARLY_TILES, 7 - _Z_EARLY_TILES, z_early, "z_proj_late")
    w_out_flight = _wait_then_start(
        w_out_flight, lambda *a: _own_block_copies(_my_core_and_sibling)(*a) + _own_fill(4)(*a), _forward_copies, 3, z,
        "gather_w_out_forward_stage")
    sink_rows = jnp.repeat(attn_sinks.reshape(N_Q_HEADS), BLOCK).reshape(2, 1, 8 * BLOCK)
    sgu_bt = sgu_b[0].T
    a = _mixer_fwd(z, sink_rows + w_out_flight[3][0, 0], sgu_ln_g, sgu_ln_b, sgu_w[0], sgu_bt)
    w_out_all = _wait_copies(w_out_flight, _forward_copies, a, "gather_w_out_forward_wait")[0]
    w_out_full = w_out_all.reshape(D_MODEL, D_MODEL)
    final_g_row = final_g.reshape(1, D_MODEL)
    dx2, dy, loss_part, d_final_g, d_gate = _out_proj_head(a, w_out_full, x2d, target, gate, final_g_row)

    da = _matmul(dy, w_out_full, "nt", F32, min(t, 1024), 1024, "out_proj_bwd")
    dw_out = _matmul(a, dy, "tn", BF16, 1024, 1024, "w_out_grad").reshape(4, 2, W_OUT_SHARD, D_MODEL)
    pair_out = _pair_reduce(dw_out, "w_out_grad_pair_reduce", W_OUT_SHARD // 2)
    out_flight = _start_copies([pair_out, lax.empty((3, W_OUT_SHARD, D_MODEL), BF16)], _chip_copies, 3, core,
                               "w_out_grad_chip_start")
    dz, d_sinks, d_sgu_w, d_sgu_b, d_ln_g, d_ln_b = _mixer_bwd(
        z, da, sink_rows + out_flight[3][0, 0], sgu_ln_g, sgu_ln_b, sgu_w[0], jnp.swapaxes(sgu_w[0], 1, 2), sgu_bt)
    sgu_w_flight = _start_own_block(d_sgu_w, _all_others, N_DEV - 1, core, "sgu_w_grad_gather_start")
    dw_in_t = _matmul(dz, h, "tn", BF16, 768, D_MODEL, "w_in_grad", dep=sgu_w_flight[3])
    dw_in_t = dw_in_t.reshape(4, 2, W_IN_SHARD, D_MODEL)
    pair_in = _pair_reduce(dw_in_t, "w_in_grad_pair_reduce", W_IN_SHARD // 3)
    hop1 = _start_copies([pair_in, lax.empty((2, W_IN_SHARD, D_MODEL), BF16)], _first_hop_copies, 2, core,
                         "w_in_grad_first_hop_start")
    grad_x, d_shift, d_scale, d_norm_g = _z_proj_bwd_norm(dz, w_in_t, x2d, dx2, norm_g, scale, hop1[3])

    partial = _pack_small(d_shift, d_scale, d_gate, d_norm_g, d_final_g, d_ln_g, d_ln_b, loss_part, d_sinks, d_sgu_b)
    small_flight = _start_own_block(partial, _all_others, N_DEV - 1, core, "small_grad_gather_start")
    pair_in, land_first = _wait_copies(hop1, _first_hop_copies, small_flight[3], "w_in_grad_first_hop_wait")
    second_chip = (2 * ((xi + ci) % 2) + (yi + 1 - ci) % 2).astype(jnp.int32).reshape(1)
    relay = _relay_sum(second_chip, pair_in, land_first, W_IN_SHARD // 3)
    hop2 = _start_copies([relay, lax.empty((1, W_IN_SHARD, D_MODEL), BF16)], _second_hop_copies, 1, core,
                         "w_in_grad_second_hop_start")
    pair_out, land_out = _wait_copies(out_flight, _chip_copies, hop2[3], "w_out_grad_chip_wait")
    big = {"w_out": _adam_from_chips(chip, pair_out, [(land_out, k) for k in range(3)], w_out[0], m_w_out[0],
                                     v_w_out[0], "adam_w_out", 1024)}
    from_all = lambda *a: _own_block_copies(_all_others)(*a) + _own_fill(N_DEV - 1)(*a)
    partial_all = _wait_copies(small_flight, from_all, big["w_out"][0], "small_grad_gather_wait")[0]
    d_sgu_w_all = _wait_copies(sgu_w_flight, from_all, partial_all, "sgu_w_grad_gather_wait")[0]
    weights = {"norm_g": norm_g, "b_ada": b_ada, "attn_sinks": attn_sinks, "sgu_ln_g": sgu_ln_g,
               "sgu_ln_b": sgu_ln_b, "sgu_w": sgu_w, "sgu_b": sgu_b, "final_g": final_g_row}
    moments_m = {"norm_g": m_norm_g, "b_ada": m_b_ada, "attn_sinks": m_attn_sinks, "sgu_ln_g": m_sgu_ln_g,
                 "sgu_ln_b": m_sgu_ln_b, "sgu_w": m_sgu_w, "sgu_b": m_sgu_b,
                 "final_g": m_final_g.reshape(1, D_MODEL)}
    moments_v = {"norm_g": v_norm_g, "b_ada": v_b_ada, "attn_sinks": v_attn_sinks, "sgu_ln_g": v_sgu_ln_g,
                 "sgu_ln_b": v_sgu_ln_b, "sgu_w": v_sgu_w, "sgu_b": v_sgu_b,
                 "final_g": v_final_g.reshape(1, D_MODEL)}
    loss, dmod_all, small = _adam_small(partial_all, d_sgu_w_all, weights, moments_m, moments_v)
    small["final_g"] = tuple(o.reshape(D_MODEL) for o in small["final_g"])

    dmod_mine = lax.dynamic_slice(dmod_all, (0, me * W_ADA_SHARD), (N_DEV, W_ADA_SHARD))
    big["w_ada"] = _adam_w_ada(c_act.T, dmod_mine, w_ada[0], m_w_ada[0], v_w_ada[0])
    _, land_second = _wait_copies(hop2, _second_hop_copies, big["w_ada"][0], "w_in_grad_second_hop_wait")
    big["w_in"] = tuple(o.T for o in _adam_from_chips(
        chip, pair_in, [(land_first, 0), (land_second, 0)], w_in[0].T, m_w_in[0].T, v_w_in[0].T, "adam_w_in", 256))
    order = ["norm_g", "w_ada", "b_ada", "w_in", "attn_sinks", "sgu_ln_g", "sgu_ln_b", "sgu_w", "sgu_b", "w_out",
             "final_g"]
    outs = [loss.reshape(()), grad_x[None]]
    for k in range(4):
        for name in order:
            outs.append(big[name][k][None] if name in big else small[name][k])
    return tuple(outs)
```

```python
import jax
import jax.numpy as jnp
from jax import lax
from jax.experimental import pallas as pl
from jax.experimental.pallas import tpu as pltpu

F32 = jnp.float32
BF16 = jnp.bfloat16
MESH = pl.DeviceIdType.MESH

N_DEV = 8
D_MODEL = 2048
HEAD_DIM = 64
D_ATTN = 1024
N_Q_HEADS = 16
D_KV = 128
BLOCK = 128
D_SGU = 1024
SGU_GROUPS = 8
D_IN = 5376
W_IN_SHARD = D_IN // N_DEV
W_OUT_SHARD = D_MODEL // N_DEV
W_ADA_SHARD = 3 * D_MODEL // N_DEV
EPS = 1e-6
ATTN_SCALE = 0.125

ADAM_LR = 0.001
ADAM_B1 = 0.9
ADAM_B2 = 0.999
ADAM_EPS = 1e-08
ADAM_WD = 0.01
ADAM_STEP = 10

SEG_Q, SEG_KV, SEG_GA, SEG_U, SEG_VS, SEG_GS = 0, 1024, 1280, 2304, 3328, 4352

VMEM_LIMIT = 56 * 1024 * 1024

ROW_SHIFT, ROW_SCALE, ROW_GATE, ROW_NORM_G, ROW_FINAL_G, ROW_LN, ROW_MISC, ROW_SGU_B = 0, 1, 2, 3, 4, 5, 6, 8
SMALL_ROWS = 16


def _params(**kw):
    return pltpu.CompilerParams(vmem_limit_bytes=VMEM_LIMIT, **kw)


def _sigmoid(x):
    return 0.5 * (jnp.tanh(0.5 * x) + 1.0)


def _place():
    return lax.axis_index("x"), lax.axis_index("y"), lax.axis_index("c")


def _pair_reduce(blocks, name, row_chunk):
    _, _, r, cols = blocks.shape
    assert r % row_chunk == 0

    def body(in_ref, out_ref, land, own, summed, send_sems, recv_sems, own_sems, out_sems):
        x, y, c = _place()
        sends, loads, stores = [], [], []
        for m in range(4):
            cp = pltpu.make_async_remote_copy(
                src_ref=in_ref.at[m, 1 - c], dst_ref=land.at[m], send_sem=send_sems.at[m], recv_sem=recv_sems.at[m],
                device_id=(x, y, 1 - c), device_id_type=MESH)
            cp.start()
            sends.append(cp)
            ld = pltpu.make_async_copy(in_ref.at[m, c], own.at[m], own_sems.at[m])
            ld.start()
            loads.append(ld)
        for m in range(4):
            sends[m].wait_recv()
            loads[m].wait()
            for k in range(r // row_chunk):
                rows = slice(k * row_chunk, (k + 1) * row_chunk)
                summed[m, rows, :] = (own[m, rows, :].astype(F32) + land[m, rows, :].astype(F32)).astype(BF16)
            st = pltpu.make_async_copy(summed.at[m], out_ref.at[m], out_sems.at[m])
            st.start()
            stores.append(st)
        for m in range(4):
            sends[m].wait_send()
            stores[m].wait()

    spec = pl.BlockSpec(memory_space=pl.ANY)
    return pl.pallas_call(
        body, name=name, out_shape=jax.ShapeDtypeStruct((4, r, cols), BF16),
        in_specs=[spec], out_specs=spec,
        scratch_shapes=[pltpu.VMEM((4, r, cols), BF16), pltpu.VMEM((4, r, cols), BF16), pltpu.VMEM((4, r, cols), BF16),
                        pltpu.SemaphoreType.DMA((4,)), pltpu.SemaphoreType.DMA((4,)), pltpu.SemaphoreType.DMA((4,)),
                        pltpu.SemaphoreType.DMA((4,))],
        compiler_params=_params(),
    )(blocks)


_HBM = pl.BlockSpec(memory_space=pltpu.HBM)
_SEM = pl.BlockSpec(memory_space=pltpu.SEMAPHORE)
_EFFECT = pltpu.SideEffectType.DATAFLOW_SIDE_EFFECTING


def _start_copies(bufs, copies, n_copies, after, name):
    nb = len(bufs)

    def body(*refs):
        for cp in copies(refs[:nb], refs[nb + 1], refs[nb + 2]):
            cp.start()
        refs[-1][...] = jnp.zeros_like(refs[-1])

    out = pl.pallas_call(
        body, name=name,
        out_shape=(pltpu.SemaphoreType.DMA((n_copies,)), pltpu.SemaphoreType.DMA((n_copies,)),
                   *[pltpu.HBM(b.shape, b.dtype) for b in bufs], jax.ShapeDtypeStruct((8, 128), F32)),
        in_specs=(_HBM,) * nb + (pl.BlockSpec(memory_space=pl.ANY),),
        out_specs=(_SEM, _SEM) + (_HBM,) * nb + (pl.BlockSpec(memory_space=pltpu.VMEM),),
        input_output_aliases={i: 2 + i for i in range(nb)},
        compiler_params=pltpu.CompilerParams(has_side_effects=_EFFECT),
    )(*[pltpu.with_memory_space_constraint(b, pltpu.HBM) for b in bufs], after)
    return out[0], out[1], list(out[2:2 + nb]), out[-1]


def _wait_copies(flight, copies, after, name):
    send_sems, recv_sems, bufs, _ = flight
    nb = len(bufs)

    def body(*refs):
        for cp in copies(refs[:nb], refs[nb], refs[nb + 1]):
            cp.wait_send()
            cp.wait_recv()

    return pl.pallas_call(
        body, name=name,
        out_shape=tuple(pltpu.HBM(b.shape, b.dtype) for b in bufs),
        in_specs=(_HBM,) * nb + (_SEM, _SEM, pl.BlockSpec(memory_space=pl.ANY)), out_specs=(_HBM,) * nb,
        input_output_aliases={i: i for i in range(nb)},
        compiler_params=pltpu.CompilerParams(has_side_effects=_EFFECT),
    )(*bufs, send_sems, recv_sems, after)


class _From:
    def __init__(self, sems, offset):
        self.sems, self.offset = sems, offset

    @property
    def at(self):
        return self

    def __getitem__(self, k):
        return self.sems.at[k + self.offset]


def _group(copies, first_buf, n_bufs, offset):
    def grouped(refs, send_sems, recv_sems):
        return copies(refs[first_buf:first_buf + n_bufs], _From(send_sems, offset), _From(recv_sems, offset))
    return grouped


def _wait_then_start(flight, waited, started, n_started, after, name, more_bufs=()):
    old_send, old_recv, bufs, _ = flight
    bufs = list(bufs) + [pltpu.with_memory_space_constraint(b, pltpu.HBM) for b in more_bufs]
    nb = len(bufs)

    def body(*refs):
        for cp in waited(refs[:nb], refs[nb], refs[nb + 1]):
            cp.wait_send()
            cp.wait_recv()
        for cp in started(refs[:nb], refs[nb + 3], refs[nb + 4]):
            cp.start()
        refs[-1][...] = jnp.zeros_like(refs[-1])

    out = pl.pallas_call(
        body, name=name,
        out_shape=(pltpu.SemaphoreType.DMA((n_started,)), pltpu.SemaphoreType.DMA((n_started,)),
                   *[pltpu.HBM(b.shape, b.dtype) for b in bufs], jax.ShapeDtypeStruct((8, 128), F32)),
        in_specs=(_HBM,) * nb + (_SEM, _SEM, pl.BlockSpec(memory_space=pl.ANY)),
        out_specs=(_SEM, _SEM) + (_HBM,) * nb + (pl.BlockSpec(memory_space=pltpu.VMEM),),
        input_output_aliases={i: 2 + i for i in range(nb)},
        compiler_params=pltpu.CompilerParams(has_side_effects=_EFFECT),
    )(*bufs, old_send, old_recv, after)
    return out[0], out[1], list(out[2:2 + nb]), out[-1]


def _chip_copies(refs, send_sems, recv_sems):
    pair_ref, land_ref = refs
    x, y, c = _place()
    chips = [(1 - x, y), (x, 1 - y), (1 - x, 1 - y)]
    return [pltpu.make_async_remote_copy(
        src_ref=pair_ref.at[2 * chip[0] + chip[1]], dst_ref=land_ref.at[k],
        send_sem=send_sems.at[k], recv_sem=recv_sems.at[k],
        device_id=(*chip, c), device_id_type=MESH) for k, chip in enumerate(chips)]


def _first_hop_copies(refs, send_sems, recv_sems):
    pair_ref, land_ref = refs
    x, y, c = _place()
    first = ((x + 1 - c) % 2, (y + c) % 2)
    blocks = [2 * first[0] + first[1], 2 * (1 - x) + (1 - y)]
    return [pltpu.make_async_remote_copy(
        src_ref=pair_ref.at[blocks[k]], dst_ref=land_ref.at[k], send_sem=send_sems.at[k], recv_sem=recv_sems.at[k],
        device_id=(*first, c), device_id_type=MESH) for k in range(2)]


def _second_hop_copies(refs, send_sems, recv_sems):
    relay_ref, land_ref = refs
    x, y, c = _place()
    second = ((x + c) % 2, (y + 1 - c) % 2)
    return [pltpu.make_async_remote_copy(
        src_ref=relay_ref, dst_ref=land_ref.at[0], send_sem=send_sems.at[0], recv_sem=recv_sems.at[0],
        device_id=(*second, c), device_id_type=MESH)]


def _own_block_copies(targets):
    def copies(refs, send_sems, recv_sems):
        x, y, c = _place()
        mine = refs[0].at[4 * x + 2 * y + c]
        return [pltpu.make_async_remote_copy(
            src_ref=mine, dst_ref=mine, send_sem=send_sems.at[k], recv_sem=recv_sems.at[k],
            device_id=to, device_id_type=MESH) for k, to in enumerate(targets(x, y, c))]
    return copies


def _my_core_and_sibling(x, y, c):
    return [(x, y, 1 - c), (1 - x, y, c), (x, 1 - y, c), (1 - x, 1 - y, c)]


def _all_others(x, y, c):
    flip = lambda v, f: 1 - v if f else v
    return [(flip(x, r & 4), flip(y, r & 2), flip(c, r & 1)) for r in range(1, N_DEV)]


def _forward_copies(refs, send_sems, recv_sems):
    x, y, c = _place()
    chips = [(1 - x, y), (x, 1 - y), (1 - x, 1 - y)]
    return [pltpu.make_async_remote_copy(
        src_ref=refs[0].at[4 * chip[0] + 2 * chip[1] + c], dst_ref=refs[0].at[4 * chip[0] + 2 * chip[1] + c],
        send_sem=send_sems.at[k], recv_sem=recv_sems.at[k],
        device_id=(x, y, 1 - c), device_id_type=MESH) for k, chip in enumerate(chips)]


def _first_axis_chip(x, y, c):
    return (x + 1 - c) % 2, (y + c) % 2


def _second_axis_chip(x, y, c):
    return (x + c) % 2, (y + 1 - c) % 2


def _first_targets(x, y, c):
    return [(x, y, 1 - c), (*_first_axis_chip(x, y, c), c)]


def _all_gather_small(shard, name):
    def body(in_ref, out_ref, send_sems, recv_sems, local_sem):
        x, y, c = _place()
        me, sibling = 4 * x + 2 * y + c, (x, y, 1 - c)
        first, second = _first_axis_chip(x, y, c), _second_axis_chip(x, y, c)

        def pair(chip):
            return out_ref.at[pl.ds(2 * (2 * chip[0] + chip[1]), 2)]

        def exchange(k, src, dst, to):
            cp = pltpu.make_async_remote_copy(src_ref=src, dst_ref=dst, send_sem=send_sems.at[k],
                                              recv_sem=recv_sems.at[k], device_id=to, device_id_type=MESH)
            cp.start()
            cp.wait()

        own = pltpu.make_async_copy(in_ref, out_ref.at[me], local_sem)
        own.start()
        exchange(0, in_ref, out_ref.at[me], sibling)
        own.wait()
        exchange(1, pair((x, y)), pair((x, y)), (*second, c))
        exchange(2, pair(second), pair(second), sibling)
        exchange(3, pair(first), pair(first), (*second, c))

    spec = pl.BlockSpec(memory_space=pltpu.VMEM)
    return pl.pallas_call(
        body, name=name, out_shape=jax.ShapeDtypeStruct((N_DEV,) + shard.shape, shard.dtype),
        in_specs=[spec], out_specs=spec,
        scratch_shapes=[pltpu.SemaphoreType.DMA((4,)), pltpu.SemaphoreType.DMA((4,)), pltpu.SemaphoreType.DMA],
        compiler_params=_params(),
    )(shard)


def _slot_copies(refs, send_sems, recv_sems, plan):
    copies = []
    for k, ((px, py, pc), to) in enumerate(plan):
        blk = refs[0].at[4 * px + 2 * py + pc]
        copies.append(pltpu.make_async_remote_copy(
            src_ref=blk, dst_ref=blk, send_sem=send_sems.at[k], recv_sem=recv_sems.at[k],
            device_id=to, device_id_type=MESH))
    return copies


def _second_axis_stage_copies(refs, send_sems, recv_sems):
    x, y, c = _place()
    first, second = (*_first_axis_chip(x, y, c), c), (*_second_axis_chip(x, y, c), c)
    return _slot_copies(refs, send_sems, recv_sems, [((x, y, c), second), (first, (x, y, 1 - c)), (first, second)])


def _second_axis_forward_copies(refs, send_sems, recv_sems):
    x, y, c = _place()
    return _slot_copies(refs, send_sems, recv_sems, [((*_second_axis_chip(x, y, c), c), (x, y, 1 - c))])


def _diagonal_forward_copies(refs, send_sems, recv_sems):
    x, y, c = _place()
    blk = refs[0].at[4 * (1 - x) + 2 * (1 - y) + c]
    return [pltpu.make_async_remote_copy(
        src_ref=blk, dst_ref=blk, send_sem=send_sems.at[0], recv_sem=recv_sems.at[0],
        device_id=(x, y, 1 - c), device_id_type=MESH)]


def _with_own_slot(block, me):
    return lax.dynamic_update_index_in_dim(lax.empty((N_DEV,) + block.shape, block.dtype), block, me, 0)


def _matmul(a, b, dims, out_dtype, tm, tn, name, dep=None):
    if dims == "nn":
        (m, k), n = a.shape, b.shape[1]
        a_spec = pl.BlockSpec((tm, k), lambda i, j: (i, 0))
        b_spec = pl.BlockSpec((k, tn), lambda i, j: (0, j))
        contract = ((1,), (0,))
    elif dims == "nt":
        (m, k), n = a.shape, b.shape[0]
        a_spec = pl.BlockSpec((tm, k), lambda i, j: (i, 0))
        b_spec = pl.BlockSpec((tn, k), lambda i, j: (j, 0))
        contract = ((1,), (1,))
    else:
        (k, m), n = a.shape, b.shape[1]
        a_spec = pl.BlockSpec((k, tm), lambda i, j: (0, i))
        b_spec = pl.BlockSpec((k, tn), lambda i, j: (0, j))
        contract = ((0,), (0,))
    assert m % tm == 0 and n % tn == 0 and a.dtype == BF16 and b.dtype == BF16

    def body(a_ref, b_ref, *rest):
        rest[-1][...] = lax.dot_general(a_ref[...], b_ref[...], (contract, ((), ())),
                                        preferred_element_type=F32).astype(out_dtype)

    deps = [] if dep is None else [dep]
    return pl.pallas_call(
        body, name=name, grid=(m // tm, n // tn),
        in_specs=[a_spec, b_spec] + [pl.BlockSpec((8, 128), lambda i, j: (0, 0))] * len(deps),
        out_specs=pl.BlockSpec((tm, tn), lambda i, j: (i, j)),
        out_shape=jax.ShapeDtypeStruct((m, n), out_dtype),
        compiler_params=_params(dimension_semantics=("arbitrary", "arbitrary")),
    )(a, b, *deps)


Z_TILE = 768
_Z_TILE_ORDER = ((0, 1, 2, 3, 4, 5, 6), (2, 0, 1, 6, 3, 4, 5), (4, 0, 5, 6, 1, 2, 3), (6, 2, 3, 4, 0, 1, 5))
_Z_EARLY_TILES = 4


def _z_proj(h, w_in_t, order, first, count, z_prev, name, dep=None):
    t = h.shape[0]

    def body(order_ref, h_ref, w_ref, *rest):
        rest[-1][...] = _dot_nt(h_ref[...], w_ref[...])

    prev = [] if z_prev is None else [z_prev]
    deps = [] if dep is None else [dep]
    return pl.pallas_call(
        body, name=name,
        grid_spec=pltpu.PrefetchScalarGridSpec(
            num_scalar_prefetch=1, grid=(count,),
            in_specs=[pl.BlockSpec((t, D_MODEL), lambda j, o: (0, 0)),
                      pl.BlockSpec((Z_TILE, D_MODEL), lambda j, o: (o[first + j], 0))]
            + [pl.BlockSpec(memory_space=pl.ANY)] * len(prev)
            + [pl.BlockSpec((8, 128), lambda j, o: (0, 0))] * len(deps),
            out_specs=pl.BlockSpec((t, Z_TILE), lambda j, o: (0, o[first + j]))),
        out_shape=jax.ShapeDtypeStruct((t, D_IN), F32),
        input_output_aliases={3: 0} if prev else {},
        compiler_params=_params(dimension_semantics=("arbitrary",)),
    )(order, h, w_in_t, *prev, *deps)


def _modulation(c_all, w_ada, b_ada_mine):
    def body(c_ref, w_ref, b_ref, act_ref, mod_ref):
        cv = c_ref[...]
        act = cv * _sigmoid(cv)
        act_ref[...] = act
        mod_ref[...] = jnp.dot(act.astype(BF16), w_ref[...].astype(BF16), preferred_element_type=F32) + b_ref[...]

    return pl.pallas_call(
        body, name="modulation",
        out_shape=(jax.ShapeDtypeStruct(c_all.shape, F32), jax.ShapeDtypeStruct((N_DEV, W_ADA_SHARD), F32)),
        compiler_params=_params(),
    )(c_all, w_ada, b_ada_mine)


def _modulated_norm(x, norm_g, scale, shift, tm=512):
    t, d = x.shape

    def body(x_ref, g_ref, sc_ref, sh_ref, h_ref):
        xv = x_ref[...]
        r = lax.rsqrt(jnp.mean(xv * xv, axis=-1, keepdims=True) + EPS)
        h = (xv * r) * g_ref[...] * (1.0 + sc_ref[...]) + sh_ref[...]
        h_ref[...] = h.astype(BF16)

    row = pl.BlockSpec((1, d), lambda i: (0, 0))
    return pl.pallas_call(
        body, name="modulated_norm", grid=(t // tm,),
        in_specs=[pl.BlockSpec((tm, d), lambda i: (i, 0)), row, row, row],
        out_specs=pl.BlockSpec((tm, d), lambda i: (i, 0)),
        out_shape=jax.ShapeDtypeStruct((t, d), BF16),
        compiler_params=_params(dimension_semantics=("arbitrary",)),
    )(x, norm_g, scale, shift)


def _window_bias(block_index):
    s = lax.broadcasted_iota(jnp.int32, (2 * BLOCK, BLOCK), 0)
    t = lax.broadcasted_iota(jnp.int32, (2 * BLOCK, BLOCK), 1)
    valid = ((s < BLOCK) & (s > t) & (block_index > 0)) | ((s >= BLOCK) & ((s - BLOCK) <= t))
    bias = jnp.where(valid, 0.0, -jnp.inf).astype(F32)
    return jnp.concatenate([bias] * 8, axis=1)


def _heads_t(pair_blocks, g):
    top = lax.broadcasted_iota(jnp.int32, (BLOCK, BLOCK), 0) < HEAD_DIM
    zeros = jnp.zeros((HEAD_DIM, BLOCK), F32)
    tiles = []
    for blk in pair_blocks:
        tp = blk.T
        if g == 0:
            tiles += [jnp.where(top, tp, 0.0), jnp.concatenate([tp[HEAD_DIM:], zeros], axis=0)]
        else:
            tiles += [jnp.concatenate([zeros, tp[:HEAD_DIM]], axis=0), jnp.where(top, 0.0, tp)]
    return jnp.concatenate(tiles, axis=1)


def _pair_block(xt, p, g):
    r0 = HEAD_DIM * g
    even = xt[r0:r0 + HEAD_DIM, (2 * p) * BLOCK:(2 * p + 1) * BLOCK]
    odd = xt[r0:r0 + HEAD_DIM, (2 * p + 1) * BLOCK:(2 * p + 2) * BLOCK]
    return jnp.concatenate([even, odd], axis=0).T


def _softmax_t(scores_t, bias, sink):
    st = scores_t + bias
    m = jnp.maximum(jnp.max(st, axis=0, keepdims=True), sink)
    e = jnp.exp(st - m)
    es = jnp.exp(sink - m)
    inv = 1.0 / (jnp.sum(e, axis=0, keepdims=True) + es)
    return e * inv, es * inv


def _dot(a, b):
    return jnp.dot(a, b, preferred_element_type=F32)


def _dot_nt(a, b):
    return lax.dot_general(a, b, (((1,), (1,)), ((), ())), preferred_element_type=F32)


def _layer_norm_fwd(v):
    mu = jnp.mean(v, axis=-1, keepdims=True)
    xc = v - mu
    rstd = lax.rsqrt(jnp.mean(xc * xc, axis=-1, keepdims=True) + EPS)
    return xc * rstd, rstd


def _tril(transposed=False):
    t = lax.broadcasted_iota(jnp.int32, (BLOCK, BLOCK), 0)
    s = lax.broadcasted_iota(jnp.int32, (BLOCK, BLOCK), 1)
    return s >= t if transposed else t >= s


def _const_spec(shape):
    return pl.BlockSpec(shape, lambda i: (0,) * len(shape))


def _kv_prev_spec(index):
    return pl.BlockSpec((BLOCK, 2 * D_KV), lambda i: (jnp.maximum(index(i) - 1, 0), SEG_KV // (2 * D_KV)))


def _keys_values(z_ref, kvp_ref):
    kvp, kvc = kvp_ref[...], z_ref[:, SEG_KV:SEG_KV + 2 * D_KV]
    kk = jnp.concatenate([kvp[:, :D_KV], kvc[:, :D_KV]], axis=0)
    vv = jnp.concatenate([kvp[:, D_KV:], kvc[:, D_KV:]], axis=0)
    return kk, vv


def _pair_cols(g, p, base=0):
    return slice(base + (4 * g + p) * 128, base + (4 * g + p + 1) * 128)


def _mixer_fwd(z, sink_rows, ln_g, ln_b, sgu_w, sgu_bt):
    t = z.shape[0]

    def body(z_ref, kvp_ref, sink_ref, lng_ref, lnb_ref, w_ref, bt_ref, a_ref):
        bias = _window_bias(pl.program_id(0))
        kk, vv = _keys_values(z_ref, kvp_ref)
        kk_b, vvt_b = kk.astype(BF16), vv.T.astype(BF16)
        for g in range(2):
            qt = _heads_t([z_ref[:, _pair_cols(g, p, SEG_Q)] * ATTN_SCALE for p in range(4)], g).astype(BF16)
            prob, _ = _softmax_t(_dot(kk_b, qt), bias, sink_ref[g])
            ot = _dot(vvt_b, prob.astype(BF16))
            for p in range(4):
                gate = z_ref[:, _pair_cols(g, p, SEG_GA)]
                a_ref[:, _pair_cols(g, p)] = (_pair_block(ot, p, g) * (gate * _sigmoid(gate))).astype(BF16)

        vhat, _ = _layer_norm_fwd(z_ref[:, SEG_VS:SEG_VS + D_SGU])
        vn = vhat * lng_ref[...] + lnb_ref[...]
        tril = _tril()
        for g in range(SGU_GROUPS):
            cols = slice(g * 128, (g + 1) * 128)
            wm = jnp.where(tril, w_ref[g], 0.0).astype(BF16)
            mixed = _dot(wm, vn[:, cols].astype(BF16)) + bt_ref[:, g:g + 1]
            gate = z_ref[:, SEG_GS + g * 128:SEG_GS + (g + 1) * 128]
            a_ref[:, D_ATTN + g * 128:D_ATTN + (g + 1) * 128] = (
                (z_ref[:, SEG_U + g * 128:SEG_U + (g + 1) * 128] * mixed) * (gate * _sigmoid(gate))).astype(BF16)

    return pl.pallas_call(
        body, name="mixer_fwd", grid=(t // BLOCK,),
        in_specs=[pl.BlockSpec((BLOCK, D_IN), lambda i: (i, 0)), _kv_prev_spec(lambda i: i),
                  _const_spec((2, 1, 8 * BLOCK)), _const_spec((1, D_SGU)), _const_spec((1, D_SGU)),
                  _const_spec((SGU_GROUPS, BLOCK, BLOCK)), _const_spec((BLOCK, SGU_GROUPS))],
        out_specs=pl.BlockSpec((BLOCK, D_MODEL), lambda i: (i, 0)),
        out_shape=jax.ShapeDtypeStruct((t, D_MODEL), BF16),
        compiler_params=_params(dimension_semantics=("arbitrary",)),
    )(z, z, sink_rows, ln_g, ln_b, sgu_w, sgu_bt)


def _mixer_bwd(z, da, sink_rows, ln_g, ln_b, sgu_w, sgu_wt, sgu_bt):
    t = z.shape[0]
    nb = t // BLOCK

    def body(z_ref, kvp_ref, da_ref, sink_ref, lng_ref, lnb_ref, w_ref, wt_ref, bt_ref,
             dz_ref, dsink_ref, dw_ref, db_ref, dlng_ref, dlnb_ref, carry_ref, dsink_acc, dbt_acc):
        step = pl.program_id(0)

        @pl.when(step == 0)
        def _():
            carry_ref[...] = jnp.zeros_like(carry_ref)
            dsink_acc[...] = jnp.zeros_like(dsink_acc)
            dbt_acc[...] = jnp.zeros_like(dbt_acc)
            dw_ref[...] = jnp.zeros_like(dw_ref)
            dlng_ref[...] = jnp.zeros_like(dlng_ref)
            dlnb_ref[...] = jnp.zeros_like(dlnb_ref)

        bias = _window_bias(nb - 1 - step)
        kk, vv = _keys_values(z_ref, kvp_ref)
        kk_b, vv_b = kk.astype(BF16), vv.astype(BF16)
        kkt_b, vvt_b = kk.T.astype(BF16), vv.T.astype(BF16)
        dkk = jnp.zeros((2 * BLOCK, D_KV), F32)
        dvv = jnp.zeros((2 * BLOCK, D_KV), F32)
        for g in range(2):
            qt = _heads_t([z_ref[:, _pair_cols(g, p, SEG_Q)] * ATTN_SCALE for p in range(4)], g).astype(BF16)
            prob, sink_prob = _softmax_t(_dot(kk_b, qt), bias, sink_ref[g])
            prob_b = prob.astype(BF16)
            ot = _dot(vvt_b, prob_b)
            gates = [z_ref[:, _pair_cols(g, p, SEG_GA)] for p in range(4)]
            sig = [_sigmoid(gt) for gt in gates]
            d_attn = [da_ref[:, _pair_cols(g, p)] for p in range(4)]
            d_ot = _heads_t([d_attn[p] * (gates[p] * sig[p]) for p in range(4)], g).astype(BF16)
            d_prob = _dot(vv_b, d_ot)
            delta = jnp.sum(prob * d_prob, axis=0, keepdims=True)
            d_scores = (prob * (d_prob - delta)).astype(BF16)
            dsink_acc[g] -= sink_prob * delta
            d_qt = _dot(kkt_b, d_scores)
            dkk = dkk + _dot_nt(d_scores, qt)
            dvv = dvv + _dot_nt(prob_b, d_ot)
            for p in range(4):
                dz_ref[:, _pair_cols(g, p, SEG_Q)] = (_pair_block(d_qt, p, g) * ATTN_SCALE).astype(BF16)
                d_silu = sig[p] * (1.0 + gates[p] * (1.0 - sig[p]))
                dz_ref[:, _pair_cols(g, p, SEG_GA)] = (d_attn[p] * _pair_block(ot, p, g) * d_silu).astype(BF16)
        d_kv = jnp.concatenate([dkk, dvv], axis=1)
        dz_ref[:, SEG_KV:SEG_KV + 2 * D_KV] = (d_kv[BLOCK:] + carry_ref[...]).astype(BF16)
        carry_ref[...] = d_kv[:BLOCK]

        vhat, rstd = _layer_norm_fwd(z_ref[:, SEG_VS:SEG_VS + D_SGU])
        lng = lng_ref[...]
        vn = vhat * lng + lnb_ref[...]
        tril, triu = _tril(), _tril(transposed=True)
        lane = lax.broadcasted_iota(jnp.int32, (BLOCK, 128), 1)
        d_bt = jnp.zeros((BLOCK, 128), F32)
        d_vn = []
        for g in range(SGU_GROUPS):
            cols = slice(g * 128, (g + 1) * 128)
            wm = jnp.where(tril, w_ref[g], 0.0).astype(BF16)
            wmt = jnp.where(triu, wt_ref[g], 0.0).astype(BF16)
            vn_g = vn[:, cols].astype(BF16)
            mixed = _dot(wm, vn_g) + bt_ref[:, g:g + 1]
            gate = z_ref[:, SEG_GS + g * 128:SEG_GS + (g + 1) * 128]
            u = z_ref[:, SEG_U + g * 128:SEG_U + (g + 1) * 128]
            d_out = da_ref[:, D_ATTN + g * 128:D_ATTN + (g + 1) * 128]
            sg = _sigmoid(gate)
            d_um = d_out * (gate * sg)
            dz_ref[:, SEG_U + g * 128:SEG_U + (g + 1) * 128] = (d_um * mixed).astype(BF16)
            dz_ref[:, SEG_GS + g * 128:SEG_GS + (g + 1) * 128] = (
                d_out * (u * mixed) * (sg * (1.0 + gate * (1.0 - sg)))).astype(BF16)
            d_mixed = d_um * u
            d_mixed_b = d_mixed.astype(BF16)
            dw_ref[g] += jnp.where(tril, _dot_nt(d_mixed_b, vn_g), 0.0)
            d_bt = d_bt + jnp.where(lane == g, jnp.sum(d_mixed, axis=-1, keepdims=True), 0.0)
            d_vn.append(_dot(wmt, d_mixed_b))
        dbt_acc[...] += d_bt
        d_vn = jnp.concatenate(d_vn, axis=1)
        dlng_ref[...] += jnp.sum(d_vn * vhat, axis=0, keepdims=True)
        dlnb_ref[...] += jnp.sum(d_vn, axis=0, keepdims=True)
        d_vhat = d_vn * lng
        d_v = rstd * (d_vhat - jnp.mean(d_vhat, axis=-1, keepdims=True)
                      - vhat * jnp.mean(d_vhat * vhat, axis=-1, keepdims=True))
        dz_ref[:, SEG_VS:SEG_VS + D_SGU] = d_v.astype(BF16)

        @pl.when(step == nb - 1)
        def _():
            db_ref[...] = dbt_acc[...].T[:SGU_GROUPS]
            lane_row = lax.broadcasted_iota(jnp.int32, (1, 128), 1)
            d_sink = jnp.zeros((1, 128), F32)
            for g in range(2):
                acc = dsink_acc[g]
                for j in range(8):
                    head_sum = jnp.sum(acc[:, j * BLOCK:(j + 1) * BLOCK], axis=-1, keepdims=True)
                    d_sink = d_sink + jnp.where(lane_row == 8 * g + j, head_sum, 0.0)
            dsink_ref[...] = d_sink

    rev = lambda i: nb - 1 - i
    return pl.pallas_call(
        body, name="mixer_bwd", grid=(nb,),
        in_specs=[pl.BlockSpec((BLOCK, D_IN), lambda i: (rev(i), 0)), _kv_prev_spec(rev),
                  pl.BlockSpec((BLOCK, D_MODEL), lambda i: (rev(i), 0)),
                  _const_spec((2, 1, 8 * BLOCK)), _const_spec((1, D_SGU)), _const_spec((1, D_SGU)),
                  _const_spec((SGU_GROUPS, BLOCK, BLOCK)), _const_spec((SGU_GROUPS, BLOCK, BLOCK)),
                  _const_spec((BLOCK, SGU_GROUPS))],
        out_specs=(pl.BlockSpec((BLOCK, D_IN), lambda i: (rev(i), 0)), _const_spec((1, 128)),
                   _const_spec((SGU_GROUPS, BLOCK, BLOCK)), _const_spec((SGU_GROUPS, BLOCK)),
                   _const_spec((1, D_SGU)), _const_spec((1, D_SGU))),
        out_shape=(jax.ShapeDtypeStruct((t, D_IN), BF16), jax.ShapeDtypeStruct((1, 128), F32),
                   jax.ShapeDtypeStruct((SGU_GROUPS, BLOCK, BLOCK), F32), jax.ShapeDtypeStruct((SGU_GROUPS, BLOCK), F32),
                   jax.ShapeDtypeStruct((1, D_SGU), F32), jax.ShapeDtypeStruct((1, D_SGU), F32)),
        scratch_shapes=[pltpu.VMEM((BLOCK, 2 * D_KV), F32), pltpu.VMEM((2, 1, 8 * BLOCK), F32),
                        pltpu.VMEM((BLOCK, 128), F32)],
        compiler_params=_params(dimension_semantics=("arbitrary",)),
    )(z, z, da, sink_rows, ln_g, ln_b, sgu_w, sgu_wt, sgu_bt)


def _out_proj_head(a, w_out_full, x, target, gate, final_g, tm=256):
    t, d = x.shape

    def body(a_ref, w_ref, x_ref, tg_ref, gate_ref, fg_ref, dx2_ref, dy_ref, loss_ref, dfg_ref, dgate_ref):
        @pl.when(pl.program_id(0) == 0)
        def _():
            loss_ref[...] = jnp.zeros_like(loss_ref)
            dfg_ref[...] = jnp.zeros_like(dfg_ref)
            dgate_ref[...] = jnp.zeros_like(dgate_ref)

        yv, gate, fg = _dot(a_ref[...], w_ref[...]), gate_ref[...], fg_ref[...]
        x2 = x_ref[...] + gate * yv
        r2 = lax.rsqrt(jnp.mean(x2 * x2, axis=-1, keepdims=True) + EPS)
        nrm = x2 * r2
        err = nrm * fg - tg_ref[...]
        loss_ref[...] += 0.5 * jnp.sum(jnp.mean(err * err, axis=-1, keepdims=True), axis=0, keepdims=True)
        d_out = err * (1.0 / d)
        dfg_ref[...] += jnp.sum(d_out * nrm, axis=0, keepdims=True)
        d_nrm = d_out * fg
        dx2 = r2 * (d_nrm - nrm * jnp.mean(d_nrm * nrm, axis=-1, keepdims=True))
        dx2_ref[...] = dx2
        dgate_ref[...] += jnp.sum(dx2 * yv, axis=0, keepdims=True)
        dy_ref[...] = (dx2 * gate).astype(BF16)

    blk = pl.BlockSpec((tm, d), lambda i: (i, 0))
    row = _const_spec((1, d))
    whole = pl.BlockSpec(w_out_full.shape, lambda i: (0, 0), pipeline_mode=pl.Buffered(1))
    return pl.pallas_call(
        body, name="out_proj_head", grid=(t // tm,),
        in_specs=[pl.BlockSpec((tm, a.shape[1]), lambda i: (i, 0)), whole, blk, blk, row, row],
        out_specs=(blk, blk, _const_spec((1, 128)), row, row),
        out_shape=(jax.ShapeDtypeStruct((t, d), F32), jax.ShapeDtypeStruct((t, d), BF16),
                   jax.ShapeDtypeStruct((1, 128), F32), jax.ShapeDtypeStruct((1, d), F32),
                   jax.ShapeDtypeStruct((1, d), F32)),
        compiler_params=_params(dimension_semantics=("arbitrary",)),
    )(a, w_out_full, x, target, gate, final_g)


def _z_proj_bwd_norm(dz, w_in_t, x, dx2, norm_g, scale, dep, tm=256):
    t, d = x.shape

    def body(dz_ref, w_ref, x_ref, dx2_ref, g_ref, sc_ref, dep_ref, gx_ref, dshift_ref, dscale_ref, dg_ref):
        @pl.when(pl.program_id(0) == 0)
        def _():
            dshift_ref[...] = jnp.zeros_like(dshift_ref)
            dscale_ref[...] = jnp.zeros_like(dscale_ref)
            dg_ref[...] = jnp.zeros_like(dg_ref)

        dh, xv, g = _dot(dz_ref[...], w_ref[...]), x_ref[...], g_ref[...]
        one_plus = 1.0 + sc_ref[...]
        r = lax.rsqrt(jnp.mean(xv * xv, axis=-1, keepdims=True) + EPS)
        xn = xv * r
        dshift_ref[...] += jnp.sum(dh, axis=0, keepdims=True)
        dscale_ref[...] += jnp.sum(dh * (xn * g), axis=0, keepdims=True)
        d_y = dh * one_plus
        dg_ref[...] += jnp.sum(d_y * xn, axis=0, keepdims=True)
        d_xn = d_y * g
        gx_ref[...] = dx2_ref[...] + r * (d_xn - xn * jnp.mean(d_xn * xn, axis=-1, keepdims=True))

    blk = pl.BlockSpec((tm, d), lambda i: (i, 0))
    row = _const_spec((1, d))
    whole = pl.BlockSpec(w_in_t.shape, lambda i: (0, 0), pipeline_mode=pl.Buffered(1))
    return pl.pallas_call(
        body, name="z_proj_bwd_norm", grid=(t // tm,),
        in_specs=[pl.BlockSpec((tm, dz.shape[1]), lambda i: (i, 0)), whole, blk, blk, row, row, _const_spec((8, 128))],
        out_specs=(blk, row, row, row),
        out_shape=(jax.ShapeDtypeStruct((t, d), F32),) + (jax.ShapeDtypeStruct((1, d), F32),) * 3,
        compiler_params=_params(dimension_semantics=("arbitrary",)),
    )(dz, w_in_t, x, dx2, norm_g, scale, dep)


def _adamw(w, g, m, v):
    m = ADAM_B1 * m + (1.0 - ADAM_B1) * g
    v = ADAM_B2 * v + (1.0 - ADAM_B2) * (g * g)
    m_hat = m / (1.0 - ADAM_B1 ** ADAM_STEP)
    v_hat = v / (1.0 - ADAM_B2 ** ADAM_STEP)
    delta = -ADAM_LR * (m_hat / (jnp.sqrt(v_hat) + ADAM_EPS) + ADAM_WD * w)
    return delta, m, v


def _relay_sum(second_chip, pair, land, tr):
    _, r, c = pair.shape

    def body(chip_ref, a_ref, b_ref, o_ref):
        o_ref[...] = (a_ref[...].astype(F32) + b_ref[...].astype(F32)).astype(BF16)

    return pl.pallas_call(
        body, name="w_in_grad_relay_sum",
        grid_spec=pltpu.PrefetchScalarGridSpec(
            num_scalar_prefetch=1, grid=(r // tr,),
            in_specs=[pl.BlockSpec((None, tr, c), lambda i, chip_ref: (chip_ref[0], i, 0)),
                      pl.BlockSpec((None, tr, c), lambda i, chip_ref: (1, i, 0))],
            out_specs=pl.BlockSpec((tr, c), lambda i, chip_ref: (i, 0))),
        out_shape=jax.ShapeDtypeStruct((r, c), BF16),
        compiler_params=_params(dimension_semantics=("arbitrary",)),
    )(second_chip, pair, land)


def _adam_from_chips(chip, pair, landed, w, m, v, name, tc):
    _, r, c = pair.shape
    n = len(landed)

    def body(chip_ref, own_ref, *refs):
        w_ref, m_ref, v_ref, g_ref, d_ref, nm_ref, nv_ref = refs[n:]
        g = own_ref[...].astype(F32)
        for k in range(n):
            g = g + refs[k][...].astype(F32)
        g_ref[...] = g
        d_ref[...], nm_ref[...], nv_ref[...] = _adamw(w_ref[...], g, m_ref[...], v_ref[...])

    def landed_spec(index):
        return pl.BlockSpec((None, r, tc), lambda i, chip_ref: (index, 0, i))

    blk = pl.BlockSpec((r, tc), lambda i, chip_ref: (0, i))
    return pl.pallas_call(
        body, name=name,
        grid_spec=pltpu.PrefetchScalarGridSpec(
            num_scalar_prefetch=1, grid=(c // tc,),
            in_specs=[pl.BlockSpec((None, r, tc), lambda i, chip_ref: (chip_ref[0], 0, i))]
            + [landed_spec(index) for _, index in landed] + [blk, blk, blk],
            out_specs=(blk,) * 4),
        out_shape=(jax.ShapeDtypeStruct((r, c), F32),) * 4,
        compiler_params=_params(dimension_semantics=("arbitrary",)),
    )(chip, pair, *[array for array, _ in landed], w, m, v)


def _adam_w_ada(act_t, dmod_mine, w, m, v, tr=512):
    r, c = w.shape

    def body(a_ref, dm_ref, w_ref, m_ref, v_ref, g_ref, d_ref, nm_ref, nv_ref):
        g = _dot(a_ref[...].astype(BF16), dm_ref[...].astype(BF16))
        g_ref[...] = g
        d_ref[...], nm_ref[...], nv_ref[...] = _adamw(w_ref[...], g, m_ref[...], v_ref[...])

    blk = pl.BlockSpec((tr, c), lambda i: (i, 0))
    return pl.pallas_call(
        body, name="adam_w_ada", grid=(r // tr,),
        in_specs=[pl.BlockSpec((tr, N_DEV), lambda i: (i, 0)), _const_spec((N_DEV, c)), blk, blk, blk],
        out_specs=(blk,) * 4, out_shape=(jax.ShapeDtypeStruct((r, c), F32),) * 4,
        compiler_params=_params(dimension_semantics=("arbitrary",)),
    )(act_t, dmod_mine, w, m, v)


def _pack_small(d_shift, d_scale, d_gate, d_norm_g, d_final_g, d_ln_g, d_ln_b, loss, d_sinks, d_sgu_b):
    def body(shift_ref, scale_ref, gate_ref, ng_ref, fg_ref, lng_ref, lnb_ref, loss_ref, sink_ref, b_ref, o_ref):
        o_ref[...] = jnp.zeros_like(o_ref)
        o_ref[ROW_SHIFT:ROW_SHIFT + 1, :] = shift_ref[...]
        o_ref[ROW_SCALE:ROW_SCALE + 1, :] = scale_ref[...]
        o_ref[ROW_GATE:ROW_GATE + 1, :] = gate_ref[...]
        o_ref[ROW_NORM_G:ROW_NORM_G + 1, :] = ng_ref[...]
        o_ref[ROW_FINAL_G:ROW_FINAL_G + 1, :] = fg_ref[...]
        o_ref[ROW_LN:ROW_LN + 1, 0:D_SGU] = lng_ref[...]
        o_ref[ROW_LN:ROW_LN + 1, D_SGU:2 * D_SGU] = lnb_ref[...]
        o_ref[ROW_MISC:ROW_MISC + 1, 0:128] = loss_ref[...]
        o_ref[ROW_MISC:ROW_MISC + 1, 128:256] = sink_ref[...]
        o_ref[ROW_SGU_B:ROW_SGU_B + SGU_GROUPS, 0:BLOCK] = b_ref[...]

    return pl.pallas_call(
        body, name="pack_small", out_shape=jax.ShapeDtypeStruct((SMALL_ROWS, D_MODEL), F32),
        compiler_params=_params(),
    )(d_shift, d_scale, d_gate, d_norm_g, d_final_g, d_ln_g, d_ln_b, loss, d_sinks, d_sgu_b)


_SMALL_NAMES = ("norm_g", "b_ada", "attn_sinks", "sgu_ln_g", "sgu_ln_b", "sgu_w", "sgu_b", "final_g")


def _adam_small(partials, d_sgu_w_all, weights, moments_m, moments_v):
    names = _SMALL_NAMES
    k = len(names)

    def body(*refs):
        p_ref, sw_ref = refs[0], refs[1]
        w_refs, m_refs, v_refs = refs[2:2 + k], refs[2 + k:2 + 2 * k], refs[2 + 2 * k:2 + 3 * k]
        loss_ref, dmod_ref = refs[2 + 3 * k], refs[3 + 3 * k]
        out_refs = refs[4 + 3 * k:4 + 7 * k]
        sum_ref = refs[4 + 7 * k]
        total = p_ref[0]
        for j in range(1, N_DEV):
            total = total + p_ref[j]
        sum_ref[...] = total
        for j in range(N_DEV):
            for part, row in enumerate((ROW_SHIFT, ROW_SCALE, ROW_GATE)):
                dmod_ref[j:j + 1, part * D_MODEL:(part + 1) * D_MODEL] = p_ref[j, row:row + 1, :]
        loss_ref[...] = sum_ref[ROW_MISC:ROW_MISC + 1, 0:1]
        d_sgu_w = sw_ref[0]
        for j in range(1, N_DEV):
            d_sgu_w = d_sgu_w + sw_ref[j]
        grads = {
            "norm_g": sum_ref[ROW_NORM_G:ROW_NORM_G + 1, :],
            "b_ada": jnp.concatenate([sum_ref[r:r + 1, :] for r in (ROW_SHIFT, ROW_SCALE, ROW_GATE)], axis=1),
            "attn_sinks": sum_ref[ROW_MISC:ROW_MISC + 1, 128:128 + N_Q_HEADS],
            "sgu_ln_g": sum_ref[ROW_LN:ROW_LN + 1, 0:D_SGU],
            "sgu_ln_b": sum_ref[ROW_LN:ROW_LN + 1, D_SGU:2 * D_SGU],
            "sgu_w": d_sgu_w[None],
            "sgu_b": sum_ref[ROW_SGU_B:ROW_SGU_B + SGU_GROUPS, 0:BLOCK][None],
            "final_g": sum_ref[ROW_FINAL_G:ROW_FINAL_G + 1, :],
        }
        for i, name in enumerate(names):
            g = grads[name]
            delta, m, v = _adamw(w_refs[i][...], g, m_refs[i][...], v_refs[i][...])
            out_refs[4 * i][...] = g
            out_refs[4 * i + 1][...] = delta
            out_refs[4 * i + 2][...] = m
            out_refs[4 * i + 3][...] = v

    shapes = [jax.ShapeDtypeStruct((1, 1), F32), jax.ShapeDtypeStruct((N_DEV, 3 * D_MODEL), F32)]
    for name in names:
        shapes += [jax.ShapeDtypeStruct(weights[name].shape, F32)] * 4
    outs = pl.pallas_call(
        body, name="adam_small", out_shape=tuple(shapes),
        scratch_shapes=[pltpu.VMEM((SMALL_ROWS, D_MODEL), F32)],
        compiler_params=_params(),
    )(partials, d_sgu_w_all, *[weights[n] for n in names], *[moments_m[n] for n in names],
      *[moments_v[n] for n in names])
    return outs[0], outs[1], {name: outs[2 + 4 * i:6 + 4 * i] for i, name in enumerate(names)}


def kernel(x, c, norm_g, w_ada, b_ada, w_in, attn_sinks, sgu_ln_g, sgu_ln_b, sgu_w, sgu_b, w_out, final_g, loss_target, m_norm_g, m_w_ada, m_b_ada, m_w_in, m_attn_sinks, m_sgu_ln_g, m_sgu_ln_b, m_sgu_w, m_sgu_b, m_w_out, m_final_g, v_norm_g, v_w_ada, v_b_ada, v_w_in, v_attn_sinks, v_sgu_ln_g, v_sgu_ln_b, v_sgu_w, v_sgu_b, v_w_out, v_final_g):
    xi, yi, ci = _place()
    me = 4 * xi + 2 * yi + ci
    x2d, target = x[0], loss_target[0]
    t = x2d.shape[0]

    core = ci.astype(jnp.int32).reshape(1)
    chip = (2 * xi + yi).astype(jnp.int32).reshape(1)

    first = _own_block_copies(_first_targets)
    first_flight = _start_copies([_with_own_slot(w_in[0].T.astype(BF16), me)], first, 2, core, "gather_w_in_start")

    c_all = _all_gather_small(c.reshape(8, 256) + first_flight[3][0, 0], "gather_c").reshape(N_DEV, D_MODEL)
    b_mine = lax.dynamic_slice(b_ada, (0, me * W_ADA_SHARD), (1, W_ADA_SHARD))
    c_act, mod_part = _modulation(c_all, w_ada[0], b_mine)
    mod_all = _all_gather_small(mod_part, "gather_mod")

    across = _wait_then_start(first_flight, lambda *a: first(*a)[1:], _second_axis_stage_copies, 3, mod_all,
                              "gather_w_in_second_axis_stage")
    mod = lax.dynamic_index_in_dim(mod_all, me, axis=1, keepdims=False).reshape(1, 3 * D_MODEL)
    mod = mod + across[3][0, 0]
    shift, scale, gate = mod[:, :D_MODEL], mod[:, D_MODEL:2 * D_MODEL], mod[:, 2 * D_MODEL:]
    h = _modulated_norm(x2d, norm_g, scale, shift)

    w_in_pair = _wait_copies((first_flight[0], first_flight[1], across[2], None), lambda *a: first(*a)[:1], h,
                             "gather_w_in_sibling_wait")
    tile_order = jnp.asarray(_Z_TILE_ORDER, jnp.int32)[chip[0]]
    z_own = _z_proj(h, w_in_pair[0].reshape(D_IN, D_MODEL), tile_order, 0, 1, None, "z_proj_own")
    w_out_first = _group(_own_block_copies(_my_core_and_sibling), 1, 1, 1)
    forward = _wait_then_start(
        (across[0], across[1], w_in_pair, None), lambda *a: _second_axis_stage_copies(*a)[:1],
        lambda refs, s, r: _second_axis_forward_copies(refs[:1], s, r) + w_out_first(refs, s, r), 5, z_own,
        "gather_w_in_second_axis_forward", more_bufs=[_with_own_slot(w_out[0].astype(BF16), me)])
    w_out_flight = (forward[0], forward[1], forward[2][1:], None)
    w_in_most = _wait_copies((across[0], across[1], forward[2][:1], None),
                             lambda *a: _second_axis_stage_copies(*a)[1:2], z_own, "gather_w_in_first_forward_wait")
    w_in_most = _wait_copies((forward[0], forward[1], w_in_most, None), _second_axis_forward_copies, z_own,
                             "gather_w_in_second_forward_wait")
    z_early = _z_proj(h, w_in_most[0].reshape(D_IN, D_MODEL), tile_order, 1, _Z_EARLY_TILES - 1, z_own, "z_proj_early")
    w_in_flight = _wait_then_start((across[0], across[1], w_in_most, None),
                                   lambda *a: _second_axis_stage_copies(*a)[2:], _diagonal_forward_copies, 1,
                                   z_early, "gather_w_in_last_stage")
    w_in_all = _wait_copies(w_in_flight, _diagonal_forward_copies, z_early, "gather_w_in_last_wait")[0]
    w_in_t = w_in_all.reshape(D_IN, D_MODEL)
    z = _z_proj(h, w_in_t, tile_order, _Z_EARLY_TILES, 7 - _Z_EARLY_TILES, z_early, "z_proj_late")
    w_out_flight = _wait_then_start(w_out_flight, _group(_own_block_copies(_my_core_and_sibling), 0, 1, 1),
                                    _forward_copies, 3, z, "gather_w_out_forward_stage")
    sink_rows = jnp.repeat(attn_sinks.reshape(N_Q_HEADS), BLOCK).reshape(2, 1, 8 * BLOCK)
    sgu_bt = sgu_b[0].T
    a = _mixer_fwd(z, sink_rows + w_out_flight[3][0, 0], sgu_ln_g, sgu_ln_b, sgu_w[0], sgu_bt)
    w_out_all = _wait_copies(w_out_flight, _forward_copies, a, "gather_w_out_forward_wait")[0]
    w_out_full = w_out_all.reshape(D_MODEL, D_MODEL)
    final_g_row = final_g.reshape(1, D_MODEL)
    dx2, dy, loss_part, d_final_g, d_gate = _out_proj_head(a, w_out_full, x2d, target, gate, final_g_row)

    da = _matmul(dy, w_out_full, "nt", F32, min(t, 1024), 1024, "out_proj_bwd")
    dw_out = _matmul(a, dy, "tn", BF16, 1024, 1024, "w_out_grad").reshape(4, 2, W_OUT_SHARD, D_MODEL)
    pair_out = _pair_reduce(dw_out, "w_out_grad_pair_reduce", W_OUT_SHARD // 2)
    dz, d_sinks, d_sgu_w, d_sgu_b, d_ln_g, d_ln_b = _mixer_bwd(
        z, da, sink_rows, sgu_ln_g, sgu_ln_b, sgu_w[0], jnp.swapaxes(sgu_w[0], 1, 2), sgu_bt)
    sgu_w_to_all = _group(_own_block_copies(_all_others), 2, 1, 3)
    both = _start_copies(
        [pair_out, lax.empty((3, W_OUT_SHARD, D_MODEL), BF16), _with_own_slot(d_sgu_w, me)],
        lambda refs, s, r: _chip_copies(refs[:2], s, r) + sgu_w_to_all(refs, s, r), 3 + N_DEV - 1, core,
        "w_out_grad_chip_and_sgu_w_gather_start")
    out_flight, sgu_w_flight = (both[0], both[1], both[2][:2], None), (both[0], both[1], both[2][2:], None)
    dw_in_t = _matmul(dz, h, "tn", BF16, 768, D_MODEL, "w_in_grad", dep=both[3])
    dw_in_t = dw_in_t.reshape(4, 2, W_IN_SHARD, D_MODEL)
    pair_in = _pair_reduce(dw_in_t, "w_in_grad_pair_reduce", W_IN_SHARD // 3)
    hop1 = _start_copies([pair_in, lax.empty((2, W_IN_SHARD, D_MODEL), BF16)], _first_hop_copies, 2, core,
                         "w_in_grad_first_hop_start")
    grad_x, d_shift, d_scale, d_norm_g = _z_proj_bwd_norm(dz, w_in_t, x2d, dx2, norm_g, scale, hop1[3])

    partial = _pack_small(d_shift, d_scale, d_gate, d_norm_g, d_final_g, d_ln_g, d_ln_b, loss_part, d_sinks, d_sgu_b)
    small_flight = _start_copies([_with_own_slot(partial, me)], _own_block_copies(_all_others), N_DEV - 1, core,
                                 "small_grad_gather_start")
    pair_in, land_first = _wait_copies(hop1, _first_hop_copies, small_flight[3], "w_in_grad_first_hop_wait")
    second_chip = (2 * ((xi + ci) % 2) + (yi + 1 - ci) % 2).astype(jnp.int32).reshape(1)
    relay = _relay_sum(second_chip, pair_in, land_first, W_IN_SHARD // 3)
    hop2 = _start_copies([relay, lax.empty((1, W_IN_SHARD, D_MODEL), BF16)], _second_hop_copies, 1, core,
                         "w_in_grad_second_hop_start")
    pair_out, land_out = _wait_copies(out_flight, _chip_copies, hop2[3], "w_out_grad_chip_wait")
    big = {"w_out": _adam_from_chips(chip, pair_out, [(land_out, k) for k in range(3)], w_out[0], m_w_out[0],
                                     v_w_out[0], "adam_w_out", 1024)}
    partial_all = _wait_copies(small_flight, _own_block_copies(_all_others), big["w_out"][0],
                               "small_grad_gather_wait")[0]
    d_sgu_w_all = _wait_copies(sgu_w_flight, _group(_own_block_copies(_all_others), 0, 1, 3), partial_all,
                               "sgu_w_grad_gather_wait")[0]
    weights = {"norm_g": norm_g, "b_ada": b_ada, "attn_sinks": attn_sinks, "sgu_ln_g": sgu_ln_g,
               "sgu_ln_b": sgu_ln_b, "sgu_w": sgu_w, "sgu_b": sgu_b, "final_g": final_g_row}
    moments_m = {"norm_g": m_norm_g, "b_ada": m_b_ada, "attn_sinks": m_attn_sinks, "sgu_ln_g": m_sgu_ln_g,
                 "sgu_ln_b": m_sgu_ln_b, "sgu_w": m_sgu_w, "sgu_b": m_sgu_b,
                 "final_g": m_final_g.reshape(1, D_MODEL)}
    moments_v = {"norm_g": v_norm_g, "b_ada": v_b_ada, "attn_sinks": v_attn_sinks, "sgu_ln_g": v_sgu_ln_g,
                 "sgu_ln_b": v_sgu_ln_b, "sgu_w": v_sgu_w, "sgu_b": v_sgu_b,
                 "final_g": v_final_g.reshape(1, D_MODEL)}
    loss, dmod_all, small = _adam_small(partial_all, d_sgu_w_all, weights, moments_m, moments_v)
    small["final_g"] = tuple(o.reshape(D_MODEL) for o in small["final_g"])

    dmod_mine = lax.dynamic_slice(dmod_all, (0, me * W_ADA_SHARD), (N_DEV, W_ADA_SHARD))
    big["w_ada"] = _adam_w_ada(c_act.T, dmod_mine, w_ada[0], m_w_ada[0], v_w_ada[0])
    _, land_second = _wait_copies(hop2, _second_hop_copies, big["w_ada"][0], "w_in_grad_second_hop_wait")
    big["w_in"] = tuple(o.T for o in _adam_from_chips(
        chip, pair_in, [(land_first, 0), (land_second, 0)], w_in[0].T, m_w_in[0].T, v_w_in[0].T, "adam_w_in", 512))
    order = ["norm_g", "w_ada", "b_ada", "w_in", "attn_sinks", "sgu_ln_g", "sgu_ln_b", "sgu_w", "sgu_b", "w_out",
             "final_g"]
    outs = [loss.reshape(()), grad_x[None]]
    for k in range(4):
        for name in order:
            outs.append(big[name][k][None] if name in big else small[name][k])
    return tuple(outs)
```

```python
import jax
import jax.numpy as jnp
from jax import lax
from jax.experimental import pallas as pl
from jax.experimental.pallas import tpu as pltpu

F32 = jnp.float32
BF16 = jnp.bfloat16
MESH = pl.DeviceIdType.MESH

N_DEV = 8
D_MODEL = 2048
HEAD_DIM = 64
D_ATTN = 1024
N_Q_HEADS = 16
D_KV = 128
BLOCK = 128
D_SGU = 1024
SGU_GROUPS = 8
D_IN = 5376
W_IN_SHARD = D_IN // N_DEV
W_OUT_SHARD = D_MODEL // N_DEV
W_ADA_SHARD = 3 * D_MODEL // N_DEV
EPS = 1e-6
ATTN_SCALE = 0.125

ADAM_LR = 0.001
ADAM_B1 = 0.9
ADAM_B2 = 0.999
ADAM_EPS = 1e-08
ADAM_WD = 0.01
ADAM_STEP = 10

SEG_Q, SEG_KV, SEG_GA, SEG_U, SEG_VS, SEG_GS = 0, 1024, 1280, 2304, 3328, 4352

VMEM_LIMIT = 56 * 1024 * 1024

ROW_SHIFT, ROW_SCALE, ROW_GATE, ROW_NORM_G, ROW_FINAL_G, ROW_LN, ROW_MISC, ROW_SGU_B = 0, 1, 2, 3, 4, 5, 6, 8
SMALL_ROWS = 16


def _params(**kw):
    return pltpu.CompilerParams(vmem_limit_bytes=VMEM_LIMIT, **kw)


def _sigmoid(x):
    return 0.5 * (jnp.tanh(0.5 * x) + 1.0)


def _place():
    return lax.axis_index("x"), lax.axis_index("y"), lax.axis_index("c")


def _pair_reduce(blocks, name, row_chunk):
    _, _, r, cols = blocks.shape
    assert r % row_chunk == 0

    def body(in_ref, out_ref, land, own, summed, send_sems, recv_sems, own_sems, out_sems):
        x, y, c = _place()
        sends, loads, stores = [], [], []
        for m in range(4):
            cp = pltpu.make_async_remote_copy(
                src_ref=in_ref.at[m, 1 - c], dst_ref=land.at[m], send_sem=send_sems.at[m], recv_sem=recv_sems.at[m],
                device_id=(x, y, 1 - c), device_id_type=MESH)
            cp.start()
            sends.append(cp)
            ld = pltpu.make_async_copy(in_ref.at[m, c], own.at[m], own_sems.at[m])
            ld.start()
            loads.append(ld)
        for m in range(4):
            sends[m].wait_recv()
            loads[m].wait()
            for k in range(r // row_chunk):
                rows = slice(k * row_chunk, (k + 1) * row_chunk)
                summed[m, rows, :] = (own[m, rows, :].astype(F32) + land[m, rows, :].astype(F32)).astype(BF16)
            st = pltpu.make_async_copy(summed.at[m], out_ref.at[m], out_sems.at[m])
            st.start()
            stores.append(st)
        for m in range(4):
            sends[m].wait_send()
            stores[m].wait()

    spec = pl.BlockSpec(memory_space=pl.ANY)
    return pl.pallas_call(
        body, name=name, out_shape=jax.ShapeDtypeStruct((4, r, cols), BF16),
        in_specs=[spec], out_specs=spec,
        scratch_shapes=[pltpu.VMEM((4, r, cols), BF16), pltpu.VMEM((4, r, cols), BF16), pltpu.VMEM((4, r, cols), BF16),
                        pltpu.SemaphoreType.DMA((4,)), pltpu.SemaphoreType.DMA((4,)), pltpu.SemaphoreType.DMA((4,)),
                        pltpu.SemaphoreType.DMA((4,))],
        compiler_params=_params(),
    )(blocks)


_HBM = pl.BlockSpec(memory_space=pltpu.HBM)
_SEM = pl.BlockSpec(memory_space=pltpu.SEMAPHORE)
_EFFECT = pltpu.SideEffectType.DATAFLOW_SIDE_EFFECTING


def _start_copies(bufs, copies, n_copies, after, name):
    nb = len(bufs)

    def body(*refs):
        for cp in copies(refs[:nb], refs[nb + 1], refs[nb + 2]):
            cp.start()
        refs[-1][...] = jnp.zeros_like(refs[-1])

    out = pl.pallas_call(
        body, name=name,
        out_shape=(pltpu.SemaphoreType.DMA((n_copies,)), pltpu.SemaphoreType.DMA((n_copies,)),
                   *[pltpu.HBM(b.shape, b.dtype) for b in bufs], jax.ShapeDtypeStruct((8, 128), F32)),
        in_specs=(_HBM,) * nb + (pl.BlockSpec(memory_space=pl.ANY),),
        out_specs=(_SEM, _SEM) + (_HBM,) * nb + (pl.BlockSpec(memory_space=pltpu.VMEM),),
        input_output_aliases={i: 2 + i for i in range(nb)},
        compiler_params=pltpu.CompilerParams(has_side_effects=_EFFECT),
    )(*[pltpu.with_memory_space_constraint(b, pltpu.HBM) for b in bufs], after)
    return out[0], out[1], list(out[2:2 + nb]), out[-1]


def _wait_copies(flight, copies, after, name):
    send_sems, recv_sems, bufs, _ = flight
    nb = len(bufs)

    def body(*refs):
        for cp in copies(refs[:nb], refs[nb], refs[nb + 1]):
            cp.wait_send()
            cp.wait_recv()

    return pl.pallas_call(
        body, name=name,
        out_shape=tuple(pltpu.HBM(b.shape, b.dtype) for b in bufs),
        in_specs=(_HBM,) * nb + (_SEM, _SEM, pl.BlockSpec(memory_space=pl.ANY)), out_specs=(_HBM,) * nb,
        input_output_aliases={i: i for i in range(nb)},
        compiler_params=pltpu.CompilerParams(has_side_effects=_EFFECT),
    )(*bufs, send_sems, recv_sems, after)


class _From:
    def __init__(self, sems, offset):
        self.sems, self.offset = sems, offset

    @property
    def at(self):
        return self

    def __getitem__(self, k):
        return self.sems.at[k + self.offset]


def _group(copies, first_buf, n_bufs, offset):
    def grouped(refs, send_sems, recv_sems):
        return copies(refs[first_buf:first_buf + n_bufs], _From(send_sems, offset), _From(recv_sems, offset))
    return grouped


def _wait_then_start(flight, waited, started, n_started, after, name, more_bufs=()):
    old_send, old_recv, bufs, _ = flight
    bufs = list(bufs) + [pltpu.with_memory_space_constraint(b, pltpu.HBM) for b in more_bufs]
    nb = len(bufs)

    def body(*refs):
        for cp in waited(refs[:nb], refs[nb], refs[nb + 1]):
            cp.wait_send()
            cp.wait_recv()
        for cp in started(refs[:nb], refs[nb + 3], refs[nb + 4]):
            cp.start()
        refs[-1][...] = jnp.zeros_like(refs[-1])

    out = pl.pallas_call(
        body, name=name,
        out_shape=(pltpu.SemaphoreType.DMA((n_started,)), pltpu.SemaphoreType.DMA((n_started,)),
                   *[pltpu.HBM(b.shape, b.dtype) for b in bufs], jax.ShapeDtypeStruct((8, 128), F32)),
        in_specs=(_HBM,) * nb + (_SEM, _SEM, pl.BlockSpec(memory_space=pl.ANY)),
        out_specs=(_SEM, _SEM) + (_HBM,) * nb + (pl.BlockSpec(memory_space=pltpu.VMEM),),
        input_output_aliases={i: 2 + i for i in range(nb)},
        compiler_params=pltpu.CompilerParams(has_side_effects=_EFFECT),
    )(*bufs, old_send, old_recv, after)
    return out[0], out[1], list(out[2:2 + nb]), out[-1]


def _chip_copies(refs, send_sems, recv_sems):
    pair_ref, land_ref = refs
    x, y, c = _place()
    chips = [(1 - x, y), (x, 1 - y), (1 - x, 1 - y)]
    return [pltpu.make_async_remote_copy(
        src_ref=pair_ref.at[2 * chip[0] + chip[1]], dst_ref=land_ref.at[k],
        send_sem=send_sems.at[k], recv_sem=recv_sems.at[k],
        device_id=(*chip, c), device_id_type=MESH) for k, chip in enumerate(chips)]


def _first_hop_copies(refs, send_sems, recv_sems):
    pair_ref, land_ref = refs
    x, y, c = _place()
    first = ((x + 1 - c) % 2, (y + c) % 2)
    blocks = [2 * first[0] + first[1], 2 * (1 - x) + (1 - y)]
    return [pltpu.make_async_remote_copy(
        src_ref=pair_ref.at[blocks[k]], dst_ref=land_ref.at[k], send_sem=send_sems.at[k], recv_sem=recv_sems.at[k],
        device_id=(*first, c), device_id_type=MESH) for k in range(2)]


def _second_hop_copies(refs, send_sems, recv_sems):
    relay_ref, land_ref = refs
    x, y, c = _place()
    second = ((x + c) % 2, (y + 1 - c) % 2)
    return [pltpu.make_async_remote_copy(
        src_ref=relay_ref, dst_ref=land_ref.at[0], send_sem=send_sems.at[0], recv_sem=recv_sems.at[0],
        device_id=(*second, c), device_id_type=MESH)]


def _own_block_copies(targets):
    def copies(refs, send_sems, recv_sems):
        x, y, c = _place()
        mine = refs[0].at[4 * x + 2 * y + c]
        return [pltpu.make_async_remote_copy(
            src_ref=mine, dst_ref=mine, send_sem=send_sems.at[k], recv_sem=recv_sems.at[k],
            device_id=to, device_id_type=MESH) for k, to in enumerate(targets(x, y, c))]
    return copies


def _my_core_and_sibling(x, y, c):
    return [(x, y, 1 - c), (1 - x, y, c), (x, 1 - y, c), (1 - x, 1 - y, c)]


def _all_others(x, y, c):
    flip = lambda v, f: 1 - v if f else v
    return [(flip(x, r & 4), flip(y, r & 2), flip(c, r & 1)) for r in range(1, N_DEV)]


def _forward_copies(refs, send_sems, recv_sems):
    x, y, c = _place()
    chips = [(1 - x, y), (x, 1 - y), (1 - x, 1 - y)]
    return [pltpu.make_async_remote_copy(
        src_ref=refs[0].at[4 * chip[0] + 2 * chip[1] + c], dst_ref=refs[0].at[4 * chip[0] + 2 * chip[1] + c],
        send_sem=send_sems.at[k], recv_sem=recv_sems.at[k],
        device_id=(x, y, 1 - c), device_id_type=MESH) for k, chip in enumerate(chips)]


def _first_axis_chip(x, y, c):
    return (x + 1 - c) % 2, (y + c) % 2


def _second_axis_chip(x, y, c):
    return (x + c) % 2, (y + 1 - c) % 2


def _first_targets(x, y, c):
    return [(x, y, 1 - c), (*_first_axis_chip(x, y, c), c)]


def _all_gather_small(shard, name):
    def body(in_ref, out_ref, send_sems, recv_sems, local_sem):
        x, y, c = _place()
        me, sibling = 4 * x + 2 * y + c, (x, y, 1 - c)
        first, second = _first_axis_chip(x, y, c), _second_axis_chip(x, y, c)

        def pair(chip):
            return out_ref.at[pl.ds(2 * (2 * chip[0] + chip[1]), 2)]

        def exchange(k, src, dst, to):
            cp = pltpu.make_async_remote_copy(src_ref=src, dst_ref=dst, send_sem=send_sems.at[k],
                                              recv_sem=recv_sems.at[k], device_id=to, device_id_type=MESH)
            cp.start()
            cp.wait()

        own = pltpu.make_async_copy(in_ref, out_ref.at[me], local_sem)
        own.start()
        exchange(0, in_ref, out_ref.at[me], sibling)
        own.wait()
        exchange(1, pair((x, y)), pair((x, y)), (*second, c))
        exchange(2, pair(second), pair(second), sibling)
        exchange(3, pair(first), pair(first), (*second, c))

    spec = pl.BlockSpec(memory_space=pltpu.VMEM)
    return pl.pallas_call(
        body, name=name, out_shape=jax.ShapeDtypeStruct((N_DEV,) + shard.shape, shard.dtype),
        in_specs=[spec], out_specs=spec,
        scratch_shapes=[pltpu.SemaphoreType.DMA((4,)), pltpu.SemaphoreType.DMA((4,)), pltpu.SemaphoreType.DMA],
        compiler_params=_params(),
    )(shard)


def _slot_copies(refs, send_sems, recv_sems, plan):
    copies = []
    for k, ((px, py, pc), to) in enumerate(plan):
        blk = refs[0].at[4 * px + 2 * py + pc]
        copies.append(pltpu.make_async_remote_copy(
            src_ref=blk, dst_ref=blk, send_sem=send_sems.at[k], recv_sem=recv_sems.at[k],
            device_id=to, device_id_type=MESH))
    return copies


def _second_axis_stage_copies(refs, send_sems, recv_sems):
    x, y, c = _place()
    first, second = (*_first_axis_chip(x, y, c), c), (*_second_axis_chip(x, y, c), c)
    return _slot_copies(refs, send_sems, recv_sems, [((x, y, c), second), (first, (x, y, 1 - c)), (first, second)])


def _second_axis_forward_copies(refs, send_sems, recv_sems):
    x, y, c = _place()
    return _slot_copies(refs, send_sems, recv_sems, [((*_second_axis_chip(x, y, c), c), (x, y, 1 - c))])


def _diagonal_forward_copies(refs, send_sems, recv_sems):
    x, y, c = _place()
    blk = refs[0].at[4 * (1 - x) + 2 * (1 - y) + c]
    return [pltpu.make_async_remote_copy(
        src_ref=blk, dst_ref=blk, send_sem=send_sems.at[0], recv_sem=recv_sems.at[0],
        device_id=(x, y, 1 - c), device_id_type=MESH)]


def _with_own_slot(block, me):
    return lax.dynamic_update_index_in_dim(lax.empty((N_DEV,) + block.shape, block.dtype), block, me, 0)


def _matmul(a, b, dims, out_dtype, tm, tn, name, dep=None):
    if dims == "nn":
        (m, k), n = a.shape, b.shape[1]
        a_spec = pl.BlockSpec((tm, k), lambda i, j: (i, 0))
        b_spec = pl.BlockSpec((k, tn), lambda i, j: (0, j))
        contract = ((1,), (0,))
    elif dims == "nt":
        (m, k), n = a.shape, b.shape[0]
        a_spec = pl.BlockSpec((tm, k), lambda i, j: (i, 0))
        b_spec = pl.BlockSpec((tn, k), lambda i, j: (j, 0))
        contract = ((1,), (1,))
    else:
        (k, m), n = a.shape, b.shape[1]
        a_spec = pl.BlockSpec((k, tm), lambda i, j: (0, i))
        b_spec = pl.BlockSpec((k, tn), lambda i, j: (0, j))
        contract = ((0,), (0,))
    assert m % tm == 0 and n % tn == 0 and a.dtype == BF16 and b.dtype == BF16

    def body(a_ref, b_ref, *rest):
        rest[-1][...] = lax.dot_general(a_ref[...], b_ref[...], (contract, ((), ())),
                                        preferred_element_type=F32).astype(out_dtype)

    deps = [] if dep is None else [dep]
    return pl.pallas_call(
        body, name=name, grid=(m // tm, n // tn),
        in_specs=[a_spec, b_spec] + [pl.BlockSpec((8, 128), lambda i, j: (0, 0))] * len(deps),
        out_specs=pl.BlockSpec((tm, tn), lambda i, j: (i, j)),
        out_shape=jax.ShapeDtypeStruct((m, n), out_dtype),
        compiler_params=_params(dimension_semantics=("arbitrary", "arbitrary")),
    )(a, b, *deps)


Z_TILE = 768
_Z_TILE_ORDER = ((0, 1, 2, 3, 4, 5, 6), (2, 0, 1, 6, 3, 4, 5), (4, 0, 5, 6, 1, 2, 3), (6, 2, 3, 4, 0, 1, 5))
_Z_EARLY_TILES = 4


def _z_proj(h, w_in_t, order, first, count, z_prev, name, dep=None):
    t = h.shape[0]

    def body(order_ref, h_ref, w_ref, *rest):
        rest[-1][...] = _dot_nt(h_ref[...], w_ref[...])

    prev = [] if z_prev is None else [z_prev]
    deps = [] if dep is None else [dep]
    return pl.pallas_call(
        body, name=name,
        grid_spec=pltpu.PrefetchScalarGridSpec(
            num_scalar_prefetch=1, grid=(count,),
            in_specs=[pl.BlockSpec((t, D_MODEL), lambda j, o: (0, 0)),
                      pl.BlockSpec((Z_TILE, D_MODEL), lambda j, o: (o[first + j], 0))]
            + [pl.BlockSpec(memory_space=pl.ANY)] * len(prev)
            + [pl.BlockSpec((8, 128), lambda j, o: (0, 0))] * len(deps),
            out_specs=pl.BlockSpec((t, Z_TILE), lambda j, o: (0, o[first + j]))),
        out_shape=jax.ShapeDtypeStruct((t, D_IN), F32),
        input_output_aliases={3: 0} if prev else {},
        compiler_params=_params(dimension_semantics=("arbitrary",)),
    )(order, h, w_in_t, *prev, *deps)


def _modulation(c_all, w_ada, b_ada_mine):
    def body(c_ref, w_ref, b_ref, act_ref, mod_ref):
        cv = c_ref[...]
        act = cv * _sigmoid(cv)
        act_ref[...] = act
        mod_ref[...] = jnp.dot(act.astype(BF16), w_ref[...].astype(BF16), preferred_element_type=F32) + b_ref[...]

    return pl.pallas_call(
        body, name="modulation",
        out_shape=(jax.ShapeDtypeStruct(c_all.shape, F32), jax.ShapeDtypeStruct((N_DEV, W_ADA_SHARD), F32)),
        compiler_params=_params(),
    )(c_all, w_ada, b_ada_mine)


def _modulated_norm(x, norm_g, scale, shift, tm=512):
    t, d = x.shape

    def body(x_ref, g_ref, sc_ref, sh_ref, h_ref):
        xv = x_ref[...]
        r = lax.rsqrt(jnp.mean(xv * xv, axis=-1, keepdims=True) + EPS)
        h = (xv * r) * g_ref[...] * (1.0 + sc_ref[...]) + sh_ref[...]
        h_ref[...] = h.astype(BF16)

    row = pl.BlockSpec((1, d), lambda i: (0, 0))
    return pl.pallas_call(
        body, name="modulated_norm", grid=(t // tm,),
        in_specs=[pl.BlockSpec((tm, d), lambda i: (i, 0)), row, row, row],
        out_specs=pl.BlockSpec((tm, d), lambda i: (i, 0)),
        out_shape=jax.ShapeDtypeStruct((t, d), BF16),
        compiler_params=_params(dimension_semantics=("arbitrary",)),
    )(x, norm_g, scale, shift)


def _window_bias(block_index):
    s = lax.broadcasted_iota(jnp.int32, (2 * BLOCK, BLOCK), 0)
    t = lax.broadcasted_iota(jnp.int32, (2 * BLOCK, BLOCK), 1)
    valid = ((s < BLOCK) & (s > t) & (block_index > 0)) | ((s >= BLOCK) & ((s - BLOCK) <= t))
    bias = jnp.where(valid, 0.0, -jnp.inf).astype(F32)
    return jnp.concatenate([bias] * 8, axis=1)


def _heads_t(pair_blocks, g):
    top = lax.broadcasted_iota(jnp.int32, (BLOCK, BLOCK), 0) < HEAD_DIM
    zeros = jnp.zeros((HEAD_DIM, BLOCK), F32)
    tiles = []
    for blk in pair_blocks:
        tp = blk.T
        if g == 0:
            tiles += [jnp.where(top, tp, 0.0), jnp.concatenate([tp[HEAD_DIM:], zeros], axis=0)]
        else:
            tiles += [jnp.concatenate([zeros, tp[:HEAD_DIM]], axis=0), jnp.where(top, 0.0, tp)]
    return jnp.concatenate(tiles, axis=1)


def _pair_block(xt, p, g):
    r0 = HEAD_DIM * g
    even = xt[r0:r0 + HEAD_DIM, (2 * p) * BLOCK:(2 * p + 1) * BLOCK]
    odd = xt[r0:r0 + HEAD_DIM, (2 * p + 1) * BLOCK:(2 * p + 2) * BLOCK]
    return jnp.concatenate([even, odd], axis=0).T


def _softmax_t(scores_t, bias, sink):
    st = scores_t + bias
    m = jnp.maximum(jnp.max(st, axis=0, keepdims=True), sink)
    e = jnp.exp(st - m)
    es = jnp.exp(sink - m)
    inv = 1.0 / (jnp.sum(e, axis=0, keepdims=True) + es)
    return e * inv, es * inv


def _dot(a, b):
    return jnp.dot(a, b, preferred_element_type=F32)


def _dot_nt(a, b):
    return lax.dot_general(a, b, (((1,), (1,)), ((), ())), preferred_element_type=F32)


def _layer_norm_fwd(v):
    mu = jnp.mean(v, axis=-1, keepdims=True)
    xc = v - mu
    rstd = lax.rsqrt(jnp.mean(xc * xc, axis=-1, keepdims=True) + EPS)
    return xc * rstd, rstd


def _tril(transposed=False):
    t = lax.broadcasted_iota(jnp.int32, (BLOCK, BLOCK), 0)
    s = lax.broadcasted_iota(jnp.int32, (BLOCK, BLOCK), 1)
    return s >= t if transposed else t >= s


def _const_spec(shape):
    return pl.BlockSpec(shape, lambda i: (0,) * len(shape))


def _keys_values(z_ref, kvp):
    kvc = z_ref[:, SEG_KV:SEG_KV + 2 * D_KV]
    kk = jnp.concatenate([kvp[:, :D_KV], kvc[:, :D_KV]], axis=0)
    vv = jnp.concatenate([kvp[:, D_KV:], kvc[:, D_KV:]], axis=0)
    return kk, vv


MIXER_BLOCKS = 2


class _Rows:
    def __init__(self, ref, sub):
        self.ref, self.rows = ref, slice(sub * BLOCK, (sub + 1) * BLOCK)

    def __getitem__(self, idx):
        return self.ref[self.rows, idx[1]]

    def __setitem__(self, idx, value):
        self.ref[self.rows, idx[1]] = value


def _kv_before_spec(index):
    return pl.BlockSpec((BLOCK, 2 * D_KV),
                        lambda i: (jnp.maximum(MIXER_BLOCKS * index(i) - 1, 0), SEG_KV // (2 * D_KV)))


def _pair_cols(g, p, base=0):
    return slice(base + (4 * g + p) * 128, base + (4 * g + p + 1) * 128)


def _mixer_fwd(z, sink_rows, ln_g, ln_b, sgu_w, sgu_bt):
    t = z.shape[0]

    def body(z_all, kvp_ref, sink_ref, lng_ref, lnb_ref, w_ref, bt_ref, a_all):
        kv_before = kvp_ref[...]
        for sub in range(MIXER_BLOCKS):
            z_ref, a_ref = _Rows(z_all, sub), _Rows(a_all, sub)
            one_block(z_ref, kv_before, MIXER_BLOCKS * pl.program_id(0) + sub, sink_ref, lng_ref, lnb_ref, w_ref,
                      bt_ref, a_ref)
            kv_before = z_ref[:, SEG_KV:SEG_KV + 2 * D_KV]

    def one_block(z_ref, kv_before, block_index, sink_ref, lng_ref, lnb_ref, w_ref, bt_ref, a_ref):
        bias = _window_bias(block_index)
        kk, vv = _keys_values(z_ref, kv_before)
        kk_b, vvt_b = kk.astype(BF16), vv.T.astype(BF16)
        for g in range(2):
            qt = _heads_t([z_ref[:, _pair_cols(g, p, SEG_Q)] * ATTN_SCALE for p in range(4)], g).astype(BF16)
            prob, _ = _softmax_t(_dot(kk_b, qt), bias, sink_ref[g])
            ot = _dot(vvt_b, prob.astype(BF16))
            for p in range(4):
                gate = z_ref[:, _pair_cols(g, p, SEG_GA)]
                a_ref[:, _pair_cols(g, p)] = (_pair_block(ot, p, g) * (gate * _sigmoid(gate))).astype(BF16)

        vhat, _ = _layer_norm_fwd(z_ref[:, SEG_VS:SEG_VS + D_SGU])
        vn = vhat * lng_ref[...] + lnb_ref[...]
        tril = _tril()
        for g in range(SGU_GROUPS):
            cols = slice(g * 128, (g + 1) * 128)
            wm = jnp.where(tril, w_ref[g], 0.0).astype(BF16)
            mixed = _dot(wm, vn[:, cols].astype(BF16)) + bt_ref[:, g:g + 1]
            gate = z_ref[:, SEG_GS + g * 128:SEG_GS + (g + 1) * 128]
            a_ref[:, D_ATTN + g * 128:D_ATTN + (g + 1) * 128] = (
                (z_ref[:, SEG_U + g * 128:SEG_U + (g + 1) * 128] * mixed) * (gate * _sigmoid(gate))).astype(BF16)

    rows = MIXER_BLOCKS * BLOCK
    return pl.pallas_call(
        body, name="mixer_fwd", grid=(t // rows,),
        in_specs=[pl.BlockSpec((rows, D_IN), lambda i: (i, 0)), _kv_before_spec(lambda i: i),
                  _const_spec((2, 1, 8 * BLOCK)), _const_spec((1, D_SGU)), _const_spec((1, D_SGU)),
                  _const_spec((SGU_GROUPS, BLOCK, BLOCK)), _const_spec((BLOCK, SGU_GROUPS))],
        out_specs=pl.BlockSpec((rows, D_MODEL), lambda i: (i, 0)),
        out_shape=jax.ShapeDtypeStruct((t, D_MODEL), BF16),
        compiler_params=_params(dimension_semantics=("arbitrary",)),
    )(z, z, sink_rows, ln_g, ln_b, sgu_w, sgu_bt)


def _mixer_bwd(z, da, sink_rows, ln_g, ln_b, sgu_w, sgu_wt, sgu_bt):
    t = z.shape[0]

    def body(z_all, kvp_ref, da_all, sink_ref, lng_ref, lnb_ref, w_ref, wt_ref, bt_ref,
             dz_all, dsink_ref, dw_ref, db_ref, dlng_ref, dlnb_ref, carry_ref, dsink_acc, dbt_acc):
        step = pl.program_id(0)

        @pl.when(step == 0)
        def _():
            carry_ref[...] = jnp.zeros_like(carry_ref)
            dsink_acc[...] = jnp.zeros_like(dsink_acc)
            dbt_acc[...] = jnp.zeros_like(dbt_acc)
            dw_ref[...] = jnp.zeros_like(dw_ref)
            dlng_ref[...] = jnp.zeros_like(dlng_ref)
            dlnb_ref[...] = jnp.zeros_like(dlnb_ref)

        carry = carry_ref[...]
        for sub in reversed(range(MIXER_BLOCKS)):
            kv_before = kvp_ref[...] if sub == 0 else _Rows(z_all, sub - 1)[:, SEG_KV:SEG_KV + 2 * D_KV]
            carry = one_block(_Rows(z_all, sub), kv_before, _Rows(da_all, sub), MIXER_BLOCKS * (ns - 1 - step) + sub,
                              carry, sink_ref, lng_ref, lnb_ref, w_ref, wt_ref, bt_ref, _Rows(dz_all, sub),
                              dw_ref, dlng_ref, dlnb_ref, dsink_acc, dbt_acc)
        carry_ref[...] = carry

        @pl.when(step == ns - 1)
        def _():
            db_ref[...] = dbt_acc[...].T[:SGU_GROUPS]
            lane_row = lax.broadcasted_iota(jnp.int32, (1, 128), 1)
            d_sink = jnp.zeros((1, 128), F32)
            for g in range(2):
                acc = dsink_acc[g]
                for j in range(8):
                    head_sum = jnp.sum(acc[:, j * BLOCK:(j + 1) * BLOCK], axis=-1, keepdims=True)
                    d_sink = d_sink + jnp.where(lane_row == 8 * g + j, head_sum, 0.0)
            dsink_ref[...] = d_sink

    def one_block(z_ref, kv_before, da_ref, block_index, carry, sink_ref, lng_ref, lnb_ref, w_ref, wt_ref, bt_ref,
                  dz_ref, dw_ref, dlng_ref, dlnb_ref, dsink_acc, dbt_acc):
        bias = _window_bias(block_index)
        kk, vv = _keys_values(z_ref, kv_before)
        kk_b, vv_b = kk.astype(BF16), vv.astype(BF16)
        kkt_b, vvt_b = kk.T.astype(BF16), vv.T.astype(BF16)
        dkk = jnp.zeros((2 * BLOCK, D_KV), F32)
        dvv = jnp.zeros((2 * BLOCK, D_KV), F32)
        for g in range(2):
            qt = _heads_t([z_ref[:, _pair_cols(g, p, SEG_Q)] * ATTN_SCALE for p in range(4)], g).astype(BF16)
            prob, sink_prob = _softmax_t(_dot(kk_b, qt), bias, sink_ref[g])
            prob_b = prob.astype(BF16)
            ot = _dot(vvt_b, prob_b)
            gates = [z_ref[:, _pair_cols(g, p, SEG_GA)] for p in range(4)]
            sig = [_sigmoid(gt) for gt in gates]
            d_attn = [da_ref[:, _pair_cols(g, p)] for p in range(4)]
            d_ot = _heads_t([d_attn[p] * (gates[p] * sig[p]) for p in range(4)], g).astype(BF16)
            d_prob = _dot(vv_b, d_ot)
            delta = jnp.sum(prob * d_prob, axis=0, keepdims=True)
            d_scores = (prob * (d_prob - delta)).astype(BF16)
            dsink_acc[g] -= sink_prob * delta
            d_qt = _dot(kkt_b, d_scores)
            dkk = dkk + _dot_nt(d_scores, qt)
            dvv = dvv + _dot_nt(prob_b, d_ot)
            for p in range(4):
                dz_ref[:, _pair_cols(g, p, SEG_Q)] = (_pair_block(d_qt, p, g) * ATTN_SCALE).astype(BF16)
                d_silu = sig[p] * (1.0 + gates[p] * (1.0 - sig[p]))
                dz_ref[:, _pair_cols(g, p, SEG_GA)] = (d_attn[p] * _pair_block(ot, p, g) * d_silu).astype(BF16)
        d_kv = jnp.concatenate([dkk, dvv], axis=1)
        dz_ref[:, SEG_KV:SEG_KV + 2 * D_KV] = (d_kv[BLOCK:] + carry).astype(BF16)

        vhat, rstd = _layer_norm_fwd(z_ref[:, SEG_VS:SEG_VS + D_SGU])
        lng = lng_ref[...]
        vn = vhat * lng + lnb_ref[...]
        tril, triu = _tril(), _tril(transposed=True)
        lane = lax.broadcasted_iota(jnp.int32, (BLOCK, 128), 1)
        d_bt = jnp.zeros((BLOCK, 128), F32)
        d_vn = []
        for g in range(SGU_GROUPS):
            cols = slice(g * 128, (g + 1) * 128)
            wm = jnp.where(tril, w_ref[g], 0.0).astype(BF16)
            wmt = jnp.where(triu, wt_ref[g], 0.0).astype(BF16)
            vn_g = vn[:, cols].astype(BF16)
            mixed = _dot(wm, vn_g) + bt_ref[:, g:g + 1]
            gate = z_ref[:, SEG_GS + g * 128:SEG_GS + (g + 1) * 128]
            u = z_ref[:, SEG_U + g * 128:SEG_U + (g + 1) * 128]
            d_out = da_ref[:, D_ATTN + g * 128:D_ATTN + (g + 1) * 128]
            sg = _sigmoid(gate)
            d_um = d_out * (gate * sg)
            dz_ref[:, SEG_U + g * 128:SEG_U + (g + 1) * 128] = (d_um * mixed).astype(BF16)
            dz_ref[:, SEG_GS + g * 128:SEG_GS + (g + 1) * 128] = (
                d_out * (u * mixed) * (sg * (1.0 + gate * (1.0 - sg)))).astype(BF16)
            d_mixed = d_um * u
            d_mixed_b = d_mixed.astype(BF16)
            dw_ref[g] += jnp.where(tril, _dot_nt(d_mixed_b, vn_g), 0.0)
            d_bt = d_bt + jnp.where(lane == g, jnp.sum(d_mixed, axis=-1, keepdims=True), 0.0)
            d_vn.append(_dot(wmt, d_mixed_b))
        dbt_acc[...] += d_bt
        d_vn = jnp.concatenate(d_vn, axis=1)
        dlng_ref[...] += jnp.sum(d_vn * vhat, axis=0, keepdims=True)
        dlnb_ref[...] += jnp.sum(d_vn, axis=0, keepdims=True)
        d_vhat = d_vn * lng
        d_v = rstd * (d_vhat - jnp.mean(d_vhat, axis=-1, keepdims=True)
                      - vhat * jnp.mean(d_vhat * vhat, axis=-1, keepdims=True))
        dz_ref[:, SEG_VS:SEG_VS + D_SGU] = d_v.astype(BF16)
        return d_kv[:BLOCK]

    rows = MIXER_BLOCKS * BLOCK
    ns = t // rows
    rev = lambda i: ns - 1 - i
    return pl.pallas_call(
        body, name="mixer_bwd", grid=(ns,),
        in_specs=[pl.BlockSpec((rows, D_IN), lambda i: (rev(i), 0)), _kv_before_spec(rev),
                  pl.BlockSpec((rows, D_MODEL), lambda i: (rev(i), 0)),
                  _const_spec((2, 1, 8 * BLOCK)), _const_spec((1, D_SGU)), _const_spec((1, D_SGU)),
                  _const_spec((SGU_GROUPS, BLOCK, BLOCK)), _const_spec((SGU_GROUPS, BLOCK, BLOCK)),
                  _const_spec((BLOCK, SGU_GROUPS))],
        out_specs=(pl.BlockSpec((rows, D_IN), lambda i: (rev(i), 0)), _const_spec((1, 128)),
                   _const_spec((SGU_GROUPS, BLOCK, BLOCK)), _const_spec((SGU_GROUPS, BLOCK)),
                   _const_spec((1, D_SGU)), _const_spec((1, D_SGU))),
        out_shape=(jax.ShapeDtypeStruct((t, D_IN), BF16), jax.ShapeDtypeStruct((1, 128), F32),
                   jax.ShapeDtypeStruct((SGU_GROUPS, BLOCK, BLOCK), F32), jax.ShapeDtypeStruct((SGU_GROUPS, BLOCK), F32),
                   jax.ShapeDtypeStruct((1, D_SGU), F32), jax.ShapeDtypeStruct((1, D_SGU), F32)),
        scratch_shapes=[pltpu.VMEM((BLOCK, 2 * D_KV), F32), pltpu.VMEM((2, 1, 8 * BLOCK), F32),
                        pltpu.VMEM((BLOCK, 128), F32)],
        compiler_params=_params(dimension_semantics=("arbitrary",)),
    )(z, z, da, sink_rows, ln_g, ln_b, sgu_w, sgu_wt, sgu_bt)


def _out_proj_head(a, w_out_full, x, target, gate, final_g, tm=256):
    t, d = x.shape

    def body(a_ref, w_ref, x_ref, tg_ref, gate_ref, fg_ref, dx2_ref, dy_ref, loss_ref, dfg_ref, dgate_ref):
        @pl.when(pl.program_id(0) == 0)
        def _():
            loss_ref[...] = jnp.zeros_like(loss_ref)
            dfg_ref[...] = jnp.zeros_like(dfg_ref)
            dgate_ref[...] = jnp.zeros_like(dgate_ref)

        yv, gate, fg = _dot(a_ref[...], w_ref[...]), gate_ref[...], fg_ref[...]
        x2 = x_ref[...] + gate * yv
        r2 = lax.rsqrt(jnp.mean(x2 * x2, axis=-1, keepdims=True) + EPS)
        nrm = x2 * r2
        err = nrm * fg - tg_ref[...]
        loss_ref[...] += 0.5 * jnp.sum(jnp.mean(err * err, axis=-1, keepdims=True), axis=0, keepdims=True)
        d_out = err * (1.0 / d)
        dfg_ref[...] += jnp.sum(d_out * nrm, axis=0, keepdims=True)
        d_nrm = d_out * fg
        dx2 = r2 * (d_nrm - nrm * jnp.mean(d_nrm * nrm, axis=-1, keepdims=True))
        dx2_ref[...] = dx2
        dgate_ref[...] += jnp.sum(dx2 * yv, axis=0, keepdims=True)
        dy_ref[...] = (dx2 * gate).astype(BF16)

    blk = pl.BlockSpec((tm, d), lambda i: (i, 0))
    row = _const_spec((1, d))
    whole = pl.BlockSpec(w_out_full.shape, lambda i: (0, 0), pipeline_mode=pl.Buffered(1))
    return pl.pallas_call(
        body, name="out_proj_head", grid=(t // tm,),
        in_specs=[pl.BlockSpec((tm, a.shape[1]), lambda i: (i, 0)), whole, blk, blk, row, row],
        out_specs=(blk, blk, _const_spec((1, 128)), row, row),
        out_shape=(jax.ShapeDtypeStruct((t, d), F32), jax.ShapeDtypeStruct((t, d), BF16),
                   jax.ShapeDtypeStruct((1, 128), F32), jax.ShapeDtypeStruct((1, d), F32),
                   jax.ShapeDtypeStruct((1, d), F32)),
        compiler_params=_params(dimension_semantics=("arbitrary",)),
    )(a, w_out_full, x, target, gate, final_g)


def _z_proj_bwd_norm(dz, w_in_t, x, dx2, norm_g, scale, dep, tm=256):
    t, d = x.shape

    def body(dz_ref, w_ref, x_ref, dx2_ref, g_ref, sc_ref, dep_ref, gx_ref, dshift_ref, dscale_ref, dg_ref):
        @pl.when(pl.program_id(0) == 0)
        def _():
            dshift_ref[...] = jnp.zeros_like(dshift_ref)
            dscale_ref[...] = jnp.zeros_like(dscale_ref)
            dg_ref[...] = jnp.zeros_like(dg_ref)

        dh, xv, g = _dot(dz_ref[...], w_ref[...]), x_ref[...], g_ref[...]
        one_plus = 1.0 + sc_ref[...]
        r = lax.rsqrt(jnp.mean(xv * xv, axis=-1, keepdims=True) + EPS)
        xn = xv * r
        dshift_ref[...] += jnp.sum(dh, axis=0, keepdims=True)
        dscale_ref[...] += jnp.sum(dh * (xn * g), axis=0, keepdims=True)
        d_y = dh * one_plus
        dg_ref[...] += jnp.sum(d_y * xn, axis=0, keepdims=True)
        d_xn = d_y * g
        gx_ref[...] = dx2_ref[...] + r * (d_xn - xn * jnp.mean(d_xn * xn, axis=-1, keepdims=True))

    blk = pl.BlockSpec((tm, d), lambda i: (i, 0))
    row = _const_spec((1, d))
    whole = pl.BlockSpec(w_in_t.shape, lambda i: (0, 0), pipeline_mode=pl.Buffered(1))
    return pl.pallas_call(
        body, name="z_proj_bwd_norm", grid=(t // tm,),
        in_specs=[pl.BlockSpec((tm, dz.shape[1]), lambda i: (i, 0)), whole, blk, blk, row, row, _const_spec((8, 128))],
        out_specs=(blk, row, row, row),
        out_shape=(jax.ShapeDtypeStruct((t, d), F32),) + (jax.ShapeDtypeStruct((1, d), F32),) * 3,
        compiler_params=_params(dimension_semantics=("arbitrary",)),
    )(dz, w_in_t, x, dx2, norm_g, scale, dep)


def _adamw(w, g, m, v):
    m = ADAM_B1 * m + (1.0 - ADAM_B1) * g
    v = ADAM_B2 * v + (1.0 - ADAM_B2) * (g * g)
    m_hat = m / (1.0 - ADAM_B1 ** ADAM_STEP)
    v_hat = v / (1.0 - ADAM_B2 ** ADAM_STEP)
    delta = -ADAM_LR * (m_hat / (jnp.sqrt(v_hat) + ADAM_EPS) + ADAM_WD * w)
    return delta, m, v


def _relay_sum(second_chip, pair, land, tr):
    _, r, c = pair.shape

    def body(chip_ref, a_ref, b_ref, o_ref):
        o_ref[...] = (a_ref[...].astype(F32) + b_ref[...].astype(F32)).astype(BF16)

    return pl.pallas_call(
        body, name="w_in_grad_relay_sum",
        grid_spec=pltpu.PrefetchScalarGridSpec(
            num_scalar_prefetch=1, grid=(r // tr,),
            in_specs=[pl.BlockSpec((None, tr, c), lambda i, chip_ref: (chip_ref[0], i, 0)),
                      pl.BlockSpec((None, tr, c), lambda i, chip_ref: (1, i, 0))],
            out_specs=pl.BlockSpec((tr, c), lambda i, chip_ref: (i, 0))),
        out_shape=jax.ShapeDtypeStruct((r, c), BF16),
        compiler_params=_params(dimension_semantics=("arbitrary",)),
    )(second_chip, pair, land)


def _adam_from_chips(chip, pair, landed, w, m, v, name, tc):
    _, r, c = pair.shape
    n = len(landed)

    def body(chip_ref, own_ref, *refs):
        w_ref, m_ref, v_ref, g_ref, d_ref, nm_ref, nv_ref = refs[n:]
        g = own_ref[...].astype(F32)
        for k in range(n):
            g = g + refs[k][...].astype(F32)
        g_ref[...] = g
        d_ref[...], nm_ref[...], nv_ref[...] = _adamw(w_ref[...], g, m_ref[...], v_ref[...])

    def landed_spec(index):
        return pl.BlockSpec((None, r, tc), lambda i, chip_ref: (index, 0, i))

    blk = pl.BlockSpec((r, tc), lambda i, chip_ref: (0, i))
    return pl.pallas_call(
        body, name=name,
        grid_spec=pltpu.PrefetchScalarGridSpec(
            num_scalar_prefetch=1, grid=(c // tc,),
            in_specs=[pl.BlockSpec((None, r, tc), lambda i, chip_ref: (chip_ref[0], 0, i))]
            + [landed_spec(index) for _, index in landed] + [blk, blk, blk],
            out_specs=(blk,) * 4),
        out_shape=(jax.ShapeDtypeStruct((r, c), F32),) * 4,
        compiler_params=_params(dimension_semantics=("arbitrary",)),
    )(chip, pair, *[array for array, _ in landed], w, m, v)


def _adam_w_ada(act_t, dmod_mine, w, m, v, tr=512):
    r, c = w.shape

    def body(a_ref, dm_ref, w_ref, m_ref, v_ref, g_ref, d_ref, nm_ref, nv_ref):
        g = _dot(a_ref[...].astype(BF16), dm_ref[...].astype(BF16))
        g_ref[...] = g
        d_ref[...], nm_ref[...], nv_ref[...] = _adamw(w_ref[...], g, m_ref[...], v_ref[...])

    blk = pl.BlockSpec((tr, c), lambda i: (i, 0))
    return pl.pallas_call(
        body, name="adam_w_ada", grid=(r // tr,),
        in_specs=[pl.BlockSpec((tr, N_DEV), lambda i: (i, 0)), _const_spec((N_DEV, c)), blk, blk, blk],
        out_specs=(blk,) * 4, out_shape=(jax.ShapeDtypeStruct((r, c), F32),) * 4,
        compiler_params=_params(dimension_semantics=("arbitrary",)),
    )(act_t, dmod_mine, w, m, v)


def _pack_small(d_shift, d_scale, d_gate, d_norm_g, d_final_g, d_ln_g, d_ln_b, loss, d_sinks, d_sgu_b):
    def body(shift_ref, scale_ref, gate_ref, ng_ref, fg_ref, lng_ref, lnb_ref, loss_ref, sink_ref, b_ref, o_ref):
        o_ref[...] = jnp.zeros_like(o_ref)
        o_ref[ROW_SHIFT:ROW_SHIFT + 1, :] = shift_ref[...]
        o_ref[ROW_SCALE:ROW_SCALE + 1, :] = scale_ref[...]
        o_ref[ROW_GATE:ROW_GATE + 1, :] = gate_ref[...]
        o_ref[ROW_NORM_G:ROW_NORM_G + 1, :] = ng_ref[...]
        o_ref[ROW_FINAL_G:ROW_FINAL_G + 1, :] = fg_ref[...]
        o_ref[ROW_LN:ROW_LN + 1, 0:D_SGU] = lng_ref[...]
        o_ref[ROW_LN:ROW_LN + 1, D_SGU:2 * D_SGU] = lnb_ref[...]
        o_ref[ROW_MISC:ROW_MISC + 1, 0:128] = loss_ref[...]
        o_ref[ROW_MISC:ROW_MISC + 1, 128:256] = sink_ref[...]
        o_ref[ROW_SGU_B:ROW_SGU_B + SGU_GROUPS, 0:BLOCK] = b_ref[...]

    return pl.pallas_call(
        body, name="pack_small", out_shape=jax.ShapeDtypeStruct((SMALL_ROWS, D_MODEL), F32),
        compiler_params=_params(),
    )(d_shift, d_scale, d_gate, d_norm_g, d_final_g, d_ln_g, d_ln_b, loss, d_sinks, d_sgu_b)


_SMALL_NAMES = ("norm_g", "b_ada", "attn_sinks", "sgu_ln_g", "sgu_ln_b", "sgu_w", "sgu_b", "final_g")


def _adam_small(partials, d_sgu_w_all, weights, moments_m, moments_v):
    names = _SMALL_NAMES
    k = len(names)

    def body(*refs):
        p_ref, sw_ref = refs[0], refs[1]
        w_refs, m_refs, v_refs = refs[2:2 + k], refs[2 + k:2 + 2 * k], refs[2 + 2 * k:2 + 3 * k]
        loss_ref, dmod_ref = refs[2 + 3 * k], refs[3 + 3 * k]
        out_refs = refs[4 + 3 * k:4 + 7 * k]
        sum_ref = refs[4 + 7 * k]
        total = p_ref[0]
        for j in range(1, N_DEV):
            total = total + p_ref[j]
        sum_ref[...] = total
        for j in range(N_DEV):
            for part, row in enumerate((ROW_SHIFT, ROW_SCALE, ROW_GATE)):
                dmod_ref[j:j + 1, part * D_MODEL:(part + 1) * D_MODEL] = p_ref[j, row:row + 1, :]
        loss_ref[...] = sum_ref[ROW_MISC:ROW_MISC + 1, 0:1]
        d_sgu_w = sw_ref[0]
        for j in range(1, N_DEV):
            d_sgu_w = d_sgu_w + sw_ref[j]
        grads = {
            "norm_g": sum_ref[ROW_NORM_G:ROW_NORM_G + 1, :],
            "b_ada": jnp.concatenate([sum_ref[r:r + 1, :] for r in (ROW_SHIFT, ROW_SCALE, ROW_GATE)], axis=1),
            "attn_sinks": sum_ref[ROW_MISC:ROW_MISC + 1, 128:128 + N_Q_HEADS],
            "sgu_ln_g": sum_ref[ROW_LN:ROW_LN + 1, 0:D_SGU],
            "sgu_ln_b": sum_ref[ROW_LN:ROW_LN + 1, D_SGU:2 * D_SGU],
            "sgu_w": d_sgu_w[None],
            "sgu_b": sum_ref[ROW_SGU_B:ROW_SGU_B + SGU_GROUPS, 0:BLOCK][None],
            "final_g": sum_ref[ROW_FINAL_G:ROW_FINAL_G + 1, :],
        }
        for i, name in enumerate(names):
            g = grads[name]
            delta, m, v = _adamw(w_refs[i][...], g, m_refs[i][...], v_refs[i][...])
            out_refs[4 * i][...] = g
            out_refs[4 * i + 1][...] = delta
            out_refs[4 * i + 2][...] = m
            out_refs[4 * i + 3][...] = v

    shapes = [jax.ShapeDtypeStruct((1, 1), F32), jax.ShapeDtypeStruct((N_DEV, 3 * D_MODEL), F32)]
    for name in names:
        shapes += [jax.ShapeDtypeStruct(weights[name].shape, F32)] * 4
    outs = pl.pallas_call(
        body, name="adam_small", out_shape=tuple(shapes),
        scratch_shapes=[pltpu.VMEM((SMALL_ROWS, D_MODEL), F32)],
        compiler_params=_params(),
    )(partials, d_sgu_w_all, *[weights[n] for n in names], *[moments_m[n] for n in names],
      *[moments_v[n] for n in names])
    return outs[0], outs[1], {name: outs[2 + 4 * i:6 + 4 * i] for i, name in enumerate(names)}


def kernel(x, c, norm_g, w_ada, b_ada, w_in, attn_sinks, sgu_ln_g, sgu_ln_b, sgu_w, sgu_b, w_out, final_g, loss_target, m_norm_g, m_w_ada, m_b_ada, m_w_in, m_attn_sinks, m_sgu_ln_g, m_sgu_ln_b, m_sgu_w, m_sgu_b, m_w_out, m_final_g, v_norm_g, v_w_ada, v_b_ada, v_w_in, v_attn_sinks, v_sgu_ln_g, v_sgu_ln_b, v_sgu_w, v_sgu_b, v_w_out, v_final_g):
    xi, yi, ci = _place()
    me = 4 * xi + 2 * yi + ci
    x2d, target = x[0], loss_target[0]
    t = x2d.shape[0]

    core = ci.astype(jnp.int32).reshape(1)
    chip = (2 * xi + yi).astype(jnp.int32).reshape(1)

    first = _own_block_copies(_first_targets)
    first_flight = _start_copies([_with_own_slot(w_in[0].T.astype(BF16), me)], first, 2, core, "gather_w_in_start")

    c_all = _all_gather_small(c.reshape(8, 256) + first_flight[3][0, 0], "gather_c").reshape(N_DEV, D_MODEL)
    b_mine = lax.dynamic_slice(b_ada, (0, me * W_ADA_SHARD), (1, W_ADA_SHARD))
    c_act, mod_part = _modulation(c_all, w_ada[0], b_mine)
    mod_all = _all_gather_small(mod_part, "gather_mod")

    across = _wait_then_start(first_flight, lambda *a: first(*a)[1:], _second_axis_stage_copies, 3, mod_all,
                              "gather_w_in_second_axis_stage")
    mod = lax.dynamic_index_in_dim(mod_all, me, axis=1, keepdims=False).reshape(1, 3 * D_MODEL)
    mod = mod + across[3][0, 0]
    shift, scale, gate = mod[:, :D_MODEL], mod[:, D_MODEL:2 * D_MODEL], mod[:, 2 * D_MODEL:]
    h = _modulated_norm(x2d, norm_g, scale, shift)

    w_in_pair = _wait_copies((first_flight[0], first_flight[1], across[2], None), lambda *a: first(*a)[:1], h,
                             "gather_w_in_sibling_wait")
    tile_order = jnp.asarray(_Z_TILE_ORDER, jnp.int32)[chip[0]]
    z_own = _z_proj(h, w_in_pair[0].reshape(D_IN, D_MODEL), tile_order, 0, 1, None, "z_proj_own")
    w_out_first = _group(_own_block_copies(_my_core_and_sibling), 1, 1, 1)
    forward = _wait_then_start(
        (across[0], across[1], w_in_pair, None), lambda *a: _second_axis_stage_copies(*a)[:1],
        lambda refs, s, r: _second_axis_forward_copies(refs[:1], s, r) + w_out_first(refs, s, r), 5, z_own,
        "gather_w_in_second_axis_forward", more_bufs=[_with_own_slot(w_out[0].astype(BF16), me)])
    w_out_flight = (forward[0], forward[1], forward[2][1:], None)
    w_in_most = _wait_copies((across[0], across[1], forward[2][:1], None),
                             lambda *a: _second_axis_stage_copies(*a)[1:2], z_own, "gather_w_in_first_forward_wait")
    w_in_most = _wait_copies((forward[0], forward[1], w_in_most, None), _second_axis_forward_copies, z_own,
                             "gather_w_in_second_forward_wait")
    z_early = _z_proj(h, w_in_most[0].reshape(D_IN, D_MODEL), tile_order, 1, _Z_EARLY_TILES - 1, z_own, "z_proj_early")
    w_in_flight = _wait_then_start((across[0], across[1], w_in_most, None),
                                   lambda *a: _second_axis_stage_copies(*a)[2:], _diagonal_forward_copies, 1,
                                   z_early, "gather_w_in_last_stage")
    w_in_all = _wait_copies(w_in_flight, _diagonal_forward_copies, z_early, "gather_w_in_last_wait")[0]
    w_in_t = w_in_all.reshape(D_IN, D_MODEL)
    z = _z_proj(h, w_in_t, tile_order, _Z_EARLY_TILES, 7 - _Z_EARLY_TILES, z_early, "z_proj_late")
    w_out_flight = _wait_then_start(w_out_flight, _group(_own_block_copies(_my_core_and_sibling), 0, 1, 1),
                                    _forward_copies, 3, z, "gather_w_out_forward_stage")
    sink_rows = jnp.repeat(attn_sinks.reshape(N_Q_HEADS), BLOCK).reshape(2, 1, 8 * BLOCK)
    sgu_bt = sgu_b[0].T
    a = _mixer_fwd(z, sink_rows + w_out_flight[3][0, 0], sgu_ln_g, sgu_ln_b, sgu_w[0], sgu_bt)
    w_out_all = _wait_copies(w_out_flight, _forward_copies, a, "gather_w_out_forward_wait")[0]
    w_out_full = w_out_all.reshape(D_MODEL, D_MODEL)
    final_g_row = final_g.reshape(1, D_MODEL)
    dx2, dy, loss_part, d_final_g, d_gate = _out_proj_head(a, w_out_full, x2d, target, gate, final_g_row)

    da = _matmul(dy, w_out_full, "nt", F32, min(t, 1024), 1024, "out_proj_bwd")
    dw_out = _matmul(a, dy, "tn", BF16, 1024, 1024, "w_out_grad").reshape(4, 2, W_OUT_SHARD, D_MODEL)
    pair_out = _pair_reduce(dw_out, "w_out_grad_pair_reduce", W_OUT_SHARD // 2)
    dz, d_sinks, d_sgu_w, d_sgu_b, d_ln_g, d_ln_b = _mixer_bwd(
        z, da, sink_rows, sgu_ln_g, sgu_ln_b, sgu_w[0], jnp.swapaxes(sgu_w[0], 1, 2), sgu_bt)
    sgu_w_to_all = _group(_own_block_copies(_all_others), 2, 1, 3)
    both = _start_copies(
        [pair_out, lax.empty((3, W_OUT_SHARD, D_MODEL), BF16), _with_own_slot(d_sgu_w, me)],
        lambda refs, s, r: _chip_copies(refs[:2], s, r) + sgu_w_to_all(refs, s, r), 3 + N_DEV - 1, core,
        "w_out_grad_chip_and_sgu_w_gather_start")
    out_flight, sgu_w_flight = (both[0], both[1], both[2][:2], None), (both[0], both[1], both[2][2:], None)
    dw_in_t = _matmul(dz, h, "tn", BF16, 768, D_MODEL, "w_in_grad", dep=both[3])
    dw_in_t = dw_in_t.reshape(4, 2, W_IN_SHARD, D_MODEL)
    pair_in = _pair_reduce(dw_in_t, "w_in_grad_pair_reduce", W_IN_SHARD // 3)
    hop1 = _start_copies([pair_in, lax.empty((2, W_IN_SHARD, D_MODEL), BF16)], _first_hop_copies, 2, core,
                         "w_in_grad_first_hop_start")
    grad_x, d_shift, d_scale, d_norm_g = _z_proj_bwd_norm(dz, w_in_t, x2d, dx2, norm_g, scale, hop1[3])

    partial = _pack_small(d_shift, d_scale, d_gate, d_norm_g, d_final_g, d_ln_g, d_ln_b, loss_part, d_sinks, d_sgu_b)
    small_flight = _start_copies([_with_own_slot(partial, me)], _own_block_copies(_all_others), N_DEV - 1, core,
                                 "small_grad_gather_start")
    pair_in, land_first = _wait_copies(hop1, _first_hop_copies, small_flight[3], "w_in_grad_first_hop_wait")
    second_chip = (2 * ((xi + ci) % 2) + (yi + 1 - ci) % 2).astype(jnp.int32).reshape(1)
    relay = _relay_sum(second_chip, pair_in, land_first, W_IN_SHARD // 3)
    hop2 = _start_copies([relay, lax.empty((1, W_IN_SHARD, D_MODEL), BF16)], _second_hop_copies, 1, core,
                         "w_in_grad_second_hop_start")
    pair_out, land_out = _wait_copies(out_flight, _chip_copies, hop2[3], "w_out_grad_chip_wait")
    big = {"w_out": _adam_from_chips(chip, pair_out, [(land_out, k) for k in range(3)], w_out[0], m_w_out[0],
                                     v_w_out[0], "adam_w_out", 1024)}
    partial_all = _wait_copies(small_flight, _own_block_copies(_all_others), big["w_out"][0],
                               "small_grad_gather_wait")[0]
    d_sgu_w_all = _wait_copies(sgu_w_flight, _group(_own_block_copies(_all_others), 0, 1, 3), partial_all,
                               "sgu_w_grad_gather_wait")[0]
    weights = {"norm_g": norm_g, "b_ada": b_ada, "attn_sinks": attn_sinks, "sgu_ln_g": sgu_ln_g,
               "sgu_ln_b": sgu_ln_b, "sgu_w": sgu_w, "sgu_b": sgu_b, "final_g": final_g_row}
    moments_m = {"norm_g": m_norm_g, "b_ada": m_b_ada, "attn_sinks": m_attn_sinks, "sgu_ln_g": m_sgu_ln_g,
                 "sgu_ln_b": m_sgu_ln_b, "sgu_w": m_sgu_w, "sgu_b": m_sgu_b,
                 "final_g": m_final_g.reshape(1, D_MODEL)}
    moments_v = {"norm_g": v_norm_g, "b_ada": v_b_ada, "attn_sinks": v_attn_sinks, "sgu_ln_g": v_sgu_ln_g,
                 "sgu_ln_b": v_sgu_ln_b, "sgu_w": v_sgu_w, "sgu_b": v_sgu_b,
                 "final_g": v_final_g.reshape(1, D_MODEL)}
    loss, dmod_all, small = _adam_small(partial_all, d_sgu_w_all, weights, moments_m, moments_v)
    small["final_g"] = tuple(o.reshape(D_MODEL) for o in small["final_g"])

    dmod_mine = lax.dynamic_slice(dmod_all, (0, me * W_ADA_SHARD), (N_DEV, W_ADA_SHARD))
    big["w_ada"] = _adam_w_ada(c_act.T, dmod_mine, w_ada[0], m_w_ada[0], v_w_ada[0])
    _, land_second = _wait_copies(hop2, _second_hop_copies, big["w_ada"][0], "w_in_grad_second_hop_wait")
    big["w_in"] = tuple(o.T for o in _adam_from_chips(
        chip, pair_in, [(land_first, 0), (land_second, 0)], w_in[0].T, m_w_in[0].T, v_w_in[0].T, "adam_w_in", 512))
    order = ["norm_g", "w_ada", "b_ada", "w_in", "attn_sinks", "sgu_ln_g", "sgu_ln_b", "sgu_w", "sgu_b", "w_out",
             "final_g"]
    outs = [loss.reshape(()), grad_x[None]]
    for k in range(4):
        for name in order:
            outs.append(big[name][k][None] if name in big else small[name][k])
    return tuple(outs)
```

```python
import jax
import jax.numpy as jnp
from jax import lax
from jax.experimental import pallas as pl
from jax.experimental.pallas import tpu as pltpu

F32 = jnp.float32
BF16 = jnp.bfloat16
MESH = pl.DeviceIdType.MESH

N_DEV = 8
D_MODEL = 2048
HEAD_DIM = 64
D_ATTN = 1024
N_Q_HEADS = 16
D_KV = 128
BLOCK = 128
D_SGU = 1024
SGU_GROUPS = 8
D_IN = 5376
W_IN_SHARD = D_IN // N_DEV
W_OUT_SHARD = D_MODEL // N_DEV
W_ADA_SHARD = 3 * D_MODEL // N_DEV
EPS = 1e-6
ATTN_SCALE = 0.125

ADAM_LR = 0.001
ADAM_B1 = 0.9
ADAM_B2 = 0.999
ADAM_EPS = 1e-08
ADAM_WD = 0.01
ADAM_STEP = 10

SEG_Q, SEG_KV, SEG_GA, SEG_U, SEG_VS, SEG_GS = 0, 1024, 1280, 2304, 3328, 4352

VMEM_LIMIT = 56 * 1024 * 1024

ROW_SHIFT, ROW_SCALE, ROW_GATE, ROW_NORM_G, ROW_FINAL_G, ROW_LN, ROW_MISC, ROW_SGU_B = 0, 1, 2, 3, 4, 5, 6, 8
SMALL_ROWS = 16


def _params(**kw):
    return pltpu.CompilerParams(vmem_limit_bytes=VMEM_LIMIT, **kw)


def _sigmoid(x):
    return 0.5 * (jnp.tanh(0.5 * x) + 1.0)


def _place():
    return lax.axis_index("x"), lax.axis_index("y"), lax.axis_index("c")


def _pair_reduce(blocks, name, row_chunk):
    _, _, r, cols = blocks.shape
    assert r % row_chunk == 0

    def body(in_ref, out_ref, land, own, summed, send_sems, recv_sems, own_sems, out_sems):
        x, y, c = _place()
        sends, loads, stores = [], [], []
        for m in range(4):
            cp = pltpu.make_async_remote_copy(
                src_ref=in_ref.at[m, 1 - c], dst_ref=land.at[m], send_sem=send_sems.at[m], recv_sem=recv_sems.at[m],
                device_id=(x, y, 1 - c), device_id_type=MESH)
            cp.start()
            sends.append(cp)
            ld = pltpu.make_async_copy(in_ref.at[m, c], own.at[m], own_sems.at[m])
            ld.start()
            loads.append(ld)
        for m in range(4):
            sends[m].wait_recv()
            loads[m].wait()
            for k in range(r // row_chunk):
                rows = slice(k * row_chunk, (k + 1) * row_chunk)
                summed[m, rows, :] = (own[m, rows, :].astype(F32) + land[m, rows, :].astype(F32)).astype(BF16)
            st = pltpu.make_async_copy(summed.at[m], out_ref.at[m], out_sems.at[m])
            st.start()
            stores.append(st)
        for m in range(4):
            sends[m].wait_send()
            stores[m].wait()

    spec = pl.BlockSpec(memory_space=pl.ANY)
    return pl.pallas_call(
        body, name=name, out_shape=jax.ShapeDtypeStruct((4, r, cols), BF16),
        in_specs=[spec], out_specs=spec,
        scratch_shapes=[pltpu.VMEM((4, r, cols), BF16), pltpu.VMEM((4, r, cols), BF16), pltpu.VMEM((4, r, cols), BF16),
                        pltpu.SemaphoreType.DMA((4,)), pltpu.SemaphoreType.DMA((4,)), pltpu.SemaphoreType.DMA((4,)),
                        pltpu.SemaphoreType.DMA((4,))],
        compiler_params=_params(),
    )(blocks)


_HBM = pl.BlockSpec(memory_space=pltpu.HBM)
_SEM = pl.BlockSpec(memory_space=pltpu.SEMAPHORE)
_EFFECT = pltpu.SideEffectType.DATAFLOW_SIDE_EFFECTING


def _start_copies(bufs, copies, n_copies, after, name):
    nb = len(bufs)

    def body(*refs):
        for cp in copies(refs[:nb], refs[nb + 1], refs[nb + 2]):
            cp.start()
        refs[-1][...] = jnp.zeros_like(refs[-1])

    out = pl.pallas_call(
        body, name=name,
        out_shape=(pltpu.SemaphoreType.DMA((n_copies,)), pltpu.SemaphoreType.DMA((n_copies,)),
                   *[pltpu.HBM(b.shape, b.dtype) for b in bufs], jax.ShapeDtypeStruct((8, 128), F32)),
        in_specs=(_HBM,) * nb + (pl.BlockSpec(memory_space=pl.ANY),),
        out_specs=(_SEM, _SEM) + (_HBM,) * nb + (pl.BlockSpec(memory_space=pltpu.VMEM),),
        input_output_aliases={i: 2 + i for i in range(nb)},
        compiler_params=pltpu.CompilerParams(has_side_effects=_EFFECT),
    )(*[pltpu.with_memory_space_constraint(b, pltpu.HBM) for b in bufs], after)
    return out[0], out[1], list(out[2:2 + nb]), out[-1]


def _wait_copies(flight, copies, after, name):
    send_sems, recv_sems, bufs, _ = flight
    nb = len(bufs)

    def body(*refs):
        for cp in copies(refs[:nb], refs[nb], refs[nb + 1]):
            cp.wait_send()
            cp.wait_recv()

    return pl.pallas_call(
        body, name=name,
        out_shape=tuple(pltpu.HBM(b.shape, b.dtype) for b in bufs),
        in_specs=(_HBM,) * nb + (_SEM, _SEM, pl.BlockSpec(memory_space=pl.ANY)), out_specs=(_HBM,) * nb,
        input_output_aliases={i: i for i in range(nb)},
        compiler_params=pltpu.CompilerParams(has_side_effects=_EFFECT),
    )(*bufs, send_sems, recv_sems, after)


class _From:
    def __init__(self, sems, offset):
        self.sems, self.offset = sems, offset

    @property
    def at(self):
        return self

    def __getitem__(self, k):
        return self.sems.at[k + self.offset]


def _group(copies, first_buf, n_bufs, offset):
    def grouped(refs, send_sems, recv_sems):
        return copies(refs[first_buf:first_buf + n_bufs], _From(send_sems, offset), _From(recv_sems, offset))
    return grouped


def _wait_then_start(flight, waited, started, n_started, after, name, more_bufs=()):
    old_send, old_recv, bufs, _ = flight
    bufs = list(bufs) + [pltpu.with_memory_space_constraint(b, pltpu.HBM) for b in more_bufs]
    nb = len(bufs)

    def body(*refs):
        for cp in waited(refs[:nb], refs[nb], refs[nb + 1]):
            cp.wait_send()
            cp.wait_recv()
        for cp in started(refs[:nb], refs[nb + 3], refs[nb + 4]):
            cp.start()
        refs[-1][...] = jnp.zeros_like(refs[-1])

    out = pl.pallas_call(
        body, name=name,
        out_shape=(pltpu.SemaphoreType.DMA((n_started,)), pltpu.SemaphoreType.DMA((n_started,)),
                   *[pltpu.HBM(b.shape, b.dtype) for b in bufs], jax.ShapeDtypeStruct((8, 128), F32)),
        in_specs=(_HBM,) * nb + (_SEM, _SEM, pl.BlockSpec(memory_space=pl.ANY)),
        out_specs=(_SEM, _SEM) + (_HBM,) * nb + (pl.BlockSpec(memory_space=pltpu.VMEM),),
        input_output_aliases={i: 2 + i for i in range(nb)},
        compiler_params=pltpu.CompilerParams(has_side_effects=_EFFECT),
    )(*bufs, old_send, old_recv, after)
    return out[0], out[1], list(out[2:2 + nb]), out[-1]


def _chip_copies(refs, send_sems, recv_sems):
    pair_ref, land_ref = refs
    x, y, c = _place()
    chips = [(1 - x, y), (x, 1 - y), (1 - x, 1 - y)]
    return [pltpu.make_async_remote_copy(
        src_ref=pair_ref.at[2 * chip[0] + chip[1]], dst_ref=land_ref.at[k],
        send_sem=send_sems.at[k], recv_sem=recv_sems.at[k],
        device_id=(*chip, c), device_id_type=MESH) for k, chip in enumerate(chips)]


def _first_hop_copies(refs, send_sems, recv_sems):
    pair_ref, land_ref = refs
    x, y, c = _place()
    first = ((x + 1 - c) % 2, (y + c) % 2)
    blocks = [2 * first[0] + first[1], 2 * (1 - x) + (1 - y)]
    return [pltpu.make_async_remote_copy(
        src_ref=pair_ref.at[blocks[k]], dst_ref=land_ref.at[k], send_sem=send_sems.at[k], recv_sem=recv_sems.at[k],
        device_id=(*first, c), device_id_type=MESH) for k in range(2)]


def _second_hop_copies(refs, send_sems, recv_sems):
    relay_ref, land_ref = refs
    x, y, c = _place()
    second = ((x + c) % 2, (y + 1 - c) % 2)
    return [pltpu.make_async_remote_copy(
        src_ref=relay_ref, dst_ref=land_ref.at[0], send_sem=send_sems.at[0], recv_sem=recv_sems.at[0],
        device_id=(*second, c), device_id_type=MESH)]


def _own_block_copies(targets):
    def copies(refs, send_sems, recv_sems):
        x, y, c = _place()
        mine = refs[0].at[4 * x + 2 * y + c]
        return [pltpu.make_async_remote_copy(
            src_ref=mine, dst_ref=mine, send_sem=send_sems.at[k], recv_sem=recv_sems.at[k],
            device_id=to, device_id_type=MESH) for k, to in enumerate(targets(x, y, c))]
    return copies


def _my_core_and_sibling(x, y, c):
    return [(x, y, 1 - c), (1 - x, y, c), (x, 1 - y, c), (1 - x, 1 - y, c)]


def _all_others(x, y, c):
    flip = lambda v, f: 1 - v if f else v
    return [(flip(x, r & 4), flip(y, r & 2), flip(c, r & 1)) for r in range(1, N_DEV)]


def _forward_copies(refs, send_sems, recv_sems):
    x, y, c = _place()
    chips = [(1 - x, y), (x, 1 - y), (1 - x, 1 - y)]
    return [pltpu.make_async_remote_copy(
        src_ref=refs[0].at[4 * chip[0] + 2 * chip[1] + c], dst_ref=refs[0].at[4 * chip[0] + 2 * chip[1] + c],
        send_sem=send_sems.at[k], recv_sem=recv_sems.at[k],
        device_id=(x, y, 1 - c), device_id_type=MESH) for k, chip in enumerate(chips)]


def _first_axis_chip(x, y, c):
    return (x + 1 - c) % 2, (y + c) % 2


def _second_axis_chip(x, y, c):
    return (x + c) % 2, (y + 1 - c) % 2


def _first_targets(x, y, c):
    return [(x, y, 1 - c), (*_first_axis_chip(x, y, c), c)]


def _all_gather_small(shard, name):
    def body(in_ref, out_ref, send_sems, recv_sems, local_sem):
        x, y, c = _place()
        me, sibling = 4 * x + 2 * y + c, (x, y, 1 - c)
        first, second = _first_axis_chip(x, y, c), _second_axis_chip(x, y, c)

        def pair(chip):
            return out_ref.at[pl.ds(2 * (2 * chip[0] + chip[1]), 2)]

        def exchange(k, src, dst, to):
            cp = pltpu.make_async_remote_copy(src_ref=src, dst_ref=dst, send_sem=send_sems.at[k],
                                              recv_sem=recv_sems.at[k], device_id=to, device_id_type=MESH)
            cp.start()
            cp.wait()

        own = pltpu.make_async_copy(in_ref, out_ref.at[me], local_sem)
        own.start()
        exchange(0, in_ref, out_ref.at[me], sibling)
        own.wait()
        exchange(1, pair((x, y)), pair((x, y)), (*second, c))
        exchange(2, pair(second), pair(second), sibling)
        exchange(3, pair(first), pair(first), (*second, c))

    spec = pl.BlockSpec(memory_space=pltpu.VMEM)
    return pl.pallas_call(
        body, name=name, out_shape=jax.ShapeDtypeStruct((N_DEV,) + shard.shape, shard.dtype),
        in_specs=[spec], out_specs=spec,
        scratch_shapes=[pltpu.SemaphoreType.DMA((4,)), pltpu.SemaphoreType.DMA((4,)), pltpu.SemaphoreType.DMA],
        compiler_params=_params(),
    )(shard)


def _slot_copies(refs, send_sems, recv_sems, plan):
    copies = []
    for k, ((px, py, pc), to) in enumerate(plan):
        blk = refs[0].at[4 * px + 2 * py + pc]
        copies.append(pltpu.make_async_remote_copy(
            src_ref=blk, dst_ref=blk, send_sem=send_sems.at[k], recv_sem=recv_sems.at[k],
            device_id=to, device_id_type=MESH))
    return copies


def _second_axis_stage_copies(refs, send_sems, recv_sems):
    x, y, c = _place()
    first, second = (*_first_axis_chip(x, y, c), c), (*_second_axis_chip(x, y, c), c)
    return _slot_copies(refs, send_sems, recv_sems, [((x, y, c), second), (first, (x, y, 1 - c)), (first, second)])


def _second_axis_forward_copies(refs, send_sems, recv_sems):
    x, y, c = _place()
    return _slot_copies(refs, send_sems, recv_sems, [((*_second_axis_chip(x, y, c), c), (x, y, 1 - c))])


def _diagonal_forward_copies(refs, send_sems, recv_sems):
    x, y, c = _place()
    blk = refs[0].at[4 * (1 - x) + 2 * (1 - y) + c]
    return [pltpu.make_async_remote_copy(
        src_ref=blk, dst_ref=blk, send_sem=send_sems.at[0], recv_sem=recv_sems.at[0],
        device_id=(x, y, 1 - c), device_id_type=MESH)]


def _with_own_slot(block, me):
    return lax.dynamic_update_index_in_dim(lax.empty((N_DEV,) + block.shape, block.dtype), block, me, 0)


def _matmul(a, b, dims, out_dtype, tm, tn, name, dep=None):
    if dims == "nn":
        (m, k), n = a.shape, b.shape[1]
        a_spec = pl.BlockSpec((tm, k), lambda i, j: (i, 0))
        b_spec = pl.BlockSpec((k, tn), lambda i, j: (0, j))
        contract = ((1,), (0,))
    elif dims == "nt":
        (m, k), n = a.shape, b.shape[0]
        a_spec = pl.BlockSpec((tm, k), lambda i, j: (i, 0))
        b_spec = pl.BlockSpec((tn, k), lambda i, j: (j, 0))
        contract = ((1,), (1,))
    else:
        (k, m), n = a.shape, b.shape[1]
        a_spec = pl.BlockSpec((k, tm), lambda i, j: (0, i))
        b_spec = pl.BlockSpec((k, tn), lambda i, j: (0, j))
        contract = ((0,), (0,))
    assert m % tm == 0 and n % tn == 0 and a.dtype == BF16 and b.dtype == BF16

    def body(a_ref, b_ref, *rest):
        rest[-1][...] = lax.dot_general(a_ref[...], b_ref[...], (contract, ((), ())),
                                        preferred_element_type=F32).astype(out_dtype)

    deps = [] if dep is None else [dep]
    return pl.pallas_call(
        body, name=name, grid=(m // tm, n // tn),
        in_specs=[a_spec, b_spec] + [pl.BlockSpec((8, 128), lambda i, j: (0, 0))] * len(deps),
        out_specs=pl.BlockSpec((tm, tn), lambda i, j: (i, j)),
        out_shape=jax.ShapeDtypeStruct((m, n), out_dtype),
        compiler_params=_params(dimension_semantics=("arbitrary", "arbitrary")),
    )(a, b, *deps)


Z_TILE = 768
_Z_TILE_ORDER = ((0, 1, 2, 3, 4, 5, 6), (2, 0, 1, 6, 3, 4, 5), (4, 0, 5, 6, 1, 2, 3), (6, 2, 3, 4, 0, 1, 5))
_Z_EARLY_TILES = 4


def _z_proj(h, w_in_t, order, first, count, z_prev, name, dep=None):
    t = h.shape[0]

    def body(order_ref, h_ref, w_ref, *rest):
        rest[-1][...] = _dot_nt(h_ref[...], w_ref[...])

    prev = [] if z_prev is None else [z_prev]
    deps = [] if dep is None else [dep]
    return pl.pallas_call(
        body, name=name,
        grid_spec=pltpu.PrefetchScalarGridSpec(
            num_scalar_prefetch=1, grid=(count,),
            in_specs=[pl.BlockSpec((t, D_MODEL), lambda j, o: (0, 0)),
                      pl.BlockSpec((Z_TILE, D_MODEL), lambda j, o: (o[first + j], 0))]
            + [pl.BlockSpec(memory_space=pl.ANY)] * len(prev)
            + [pl.BlockSpec((8, 128), lambda j, o: (0, 0))] * len(deps),
            out_specs=pl.BlockSpec((t, Z_TILE), lambda j, o: (0, o[first + j]))),
        out_shape=jax.ShapeDtypeStruct((t, D_IN), F32),
        input_output_aliases={3: 0} if prev else {},
        compiler_params=_params(dimension_semantics=("arbitrary",)),
    )(order, h, w_in_t, *prev, *deps)


def _modulation(c_all, w_ada, b_ada_mine):
    def body(c_ref, w_ref, b_ref, act_ref, mod_ref):
        cv = c_ref[...]
        act = cv * _sigmoid(cv)
        act_ref[...] = act
        mod_ref[...] = jnp.dot(act.astype(BF16), w_ref[...].astype(BF16), preferred_element_type=F32) + b_ref[...]

    return pl.pallas_call(
        body, name="modulation",
        out_shape=(jax.ShapeDtypeStruct(c_all.shape, F32), jax.ShapeDtypeStruct((N_DEV, W_ADA_SHARD), F32)),
        compiler_params=_params(),
    )(c_all, w_ada, b_ada_mine)


def _modulated_norm(x, norm_g, scale, shift, tm=512):
    t, d = x.shape

    def body(x_ref, g_ref, sc_ref, sh_ref, h_ref):
        xv = x_ref[...]
        r = lax.rsqrt(jnp.mean(xv * xv, axis=-1, keepdims=True) + EPS)
        h = (xv * r) * g_ref[...] * (1.0 + sc_ref[...]) + sh_ref[...]
        h_ref[...] = h.astype(BF16)

    row = pl.BlockSpec((1, d), lambda i: (0, 0))
    return pl.pallas_call(
        body, name="modulated_norm", grid=(t // tm,),
        in_specs=[pl.BlockSpec((tm, d), lambda i: (i, 0)), row, row, row],
        out_specs=pl.BlockSpec((tm, d), lambda i: (i, 0)),
        out_shape=jax.ShapeDtypeStruct((t, d), BF16),
        compiler_params=_params(dimension_semantics=("arbitrary",)),
    )(x, norm_g, scale, shift)


def _window_bias(block_index):
    s = lax.broadcasted_iota(jnp.int32, (2 * BLOCK, BLOCK), 0)
    t = lax.broadcasted_iota(jnp.int32, (2 * BLOCK, BLOCK), 1)
    valid = ((s < BLOCK) & (s > t) & (block_index > 0)) | ((s >= BLOCK) & ((s - BLOCK) <= t))
    bias = jnp.where(valid, 0.0, -jnp.inf).astype(F32)
    return jnp.concatenate([bias] * 8, axis=1)


def _heads_t(pair_blocks, g):
    top = lax.broadcasted_iota(jnp.int32, (BLOCK, BLOCK), 0) < HEAD_DIM
    zeros = jnp.zeros((HEAD_DIM, BLOCK), F32)
    tiles = []
    for blk in pair_blocks:
        tp = blk.T
        if g == 0:
            tiles += [jnp.where(top, tp, 0.0), jnp.concatenate([tp[HEAD_DIM:], zeros], axis=0)]
        else:
            tiles += [jnp.concatenate([zeros, tp[:HEAD_DIM]], axis=0), jnp.where(top, 0.0, tp)]
    return jnp.concatenate(tiles, axis=1)


def _pair_block(xt, p, g):
    r0 = HEAD_DIM * g
    even = xt[r0:r0 + HEAD_DIM, (2 * p) * BLOCK:(2 * p + 1) * BLOCK]
    odd = xt[r0:r0 + HEAD_DIM, (2 * p + 1) * BLOCK:(2 * p + 2) * BLOCK]
    return jnp.concatenate([even, odd], axis=0).T


def _softmax_t(scores_t, bias, sink):
    st = scores_t + bias
    m = jnp.maximum(jnp.max(st, axis=0, keepdims=True), sink)
    e = jnp.exp(st - m)
    es = jnp.exp(sink - m)
    inv = 1.0 / (jnp.sum(e, axis=0, keepdims=True) + es)
    return e * inv, es * inv


def _dot(a, b):
    return jnp.dot(a, b, preferred_element_type=F32)


def _dot_nt(a, b):
    return lax.dot_general(a, b, (((1,), (1,)), ((), ())), preferred_element_type=F32)


def _layer_norm_fwd(v):
    mu = jnp.mean(v, axis=-1, keepdims=True)
    xc = v - mu
    rstd = lax.rsqrt(jnp.mean(xc * xc, axis=-1, keepdims=True) + EPS)
    return xc * rstd, rstd


def _tril(transposed=False):
    t = lax.broadcasted_iota(jnp.int32, (BLOCK, BLOCK), 0)
    s = lax.broadcasted_iota(jnp.int32, (BLOCK, BLOCK), 1)
    return s >= t if transposed else t >= s


def _const_spec(shape):
    return pl.BlockSpec(shape, lambda i: (0,) * len(shape))


def _keys_values(z_ref, kvp):
    kvc = z_ref[:, SEG_KV:SEG_KV + 2 * D_KV]
    kk = jnp.concatenate([kvp[:, :D_KV], kvc[:, :D_KV]], axis=0)
    vv = jnp.concatenate([kvp[:, D_KV:], kvc[:, D_KV:]], axis=0)
    return kk, vv


MIXER_BLOCKS = 2


class _Rows:
    def __init__(self, ref, sub):
        self.ref, self.rows = ref, slice(sub * BLOCK, (sub + 1) * BLOCK)

    def __getitem__(self, idx):
        return self.ref[self.rows, idx[1]]

    def __setitem__(self, idx, value):
        self.ref[self.rows, idx[1]] = value


def _kv_before_spec(index):
    return pl.BlockSpec((BLOCK, 2 * D_KV),
                        lambda i: (jnp.maximum(MIXER_BLOCKS * index(i) - 1, 0), SEG_KV // (2 * D_KV)))


def _pair_cols(g, p, base=0):
    return slice(base + (4 * g + p) * 128, base + (4 * g + p + 1) * 128)


def _mixer_fwd(z, sink_rows, ln_g, ln_b, sgu_w, sgu_bt):
    t = z.shape[0]

    def body(z_all, kvp_ref, sink_ref, lng_ref, lnb_ref, w_ref, bt_ref, a_all, prob_ref, sink_prob_ref):
        kv_before = kvp_ref[...]
        for sub in range(MIXER_BLOCKS):
            z_ref, a_ref = _Rows(z_all, sub), _Rows(a_all, sub)
            one_block(z_ref, kv_before, MIXER_BLOCKS * pl.program_id(0) + sub, sink_ref, lng_ref, lnb_ref, w_ref,
                      bt_ref, a_ref, prob_ref.at[sub], sink_prob_ref.at[sub])
            kv_before = z_ref[:, SEG_KV:SEG_KV + 2 * D_KV]

    def one_block(z_ref, kv_before, block_index, sink_ref, lng_ref, lnb_ref, w_ref, bt_ref, a_ref, prob_ref,
                  sink_prob_ref):
        bias = _window_bias(block_index)
        kk, vv = _keys_values(z_ref, kv_before)
        kk_b, vvt_b = kk.astype(BF16), vv.T.astype(BF16)
        for g in range(2):
            qt = _heads_t([z_ref[:, _pair_cols(g, p, SEG_Q)] * ATTN_SCALE for p in range(4)], g).astype(BF16)
            prob, sink_prob = _softmax_t(_dot(kk_b, qt), bias, sink_ref[g])
            prob_b = prob.astype(BF16)
            prob_ref[g] = prob_b
            sink_prob_ref[g] = sink_prob
            ot = _dot(vvt_b, prob_b)
            for p in range(4):
                gate = z_ref[:, _pair_cols(g, p, SEG_GA)]
                a_ref[:, _pair_cols(g, p)] = (_pair_block(ot, p, g) * (gate * _sigmoid(gate))).astype(BF16)

        vhat, _ = _layer_norm_fwd(z_ref[:, SEG_VS:SEG_VS + D_SGU])
        vn = vhat * lng_ref[...] + lnb_ref[...]
        tril = _tril()
        for g in range(SGU_GROUPS):
            cols = slice(g * 128, (g + 1) * 128)
            wm = jnp.where(tril, w_ref[g], 0.0).astype(BF16)
            mixed = _dot(wm, vn[:, cols].astype(BF16)) + bt_ref[:, g:g + 1]
            gate = z_ref[:, SEG_GS + g * 128:SEG_GS + (g + 1) * 128]
            a_ref[:, D_ATTN + g * 128:D_ATTN + (g + 1) * 128] = (
                (z_ref[:, SEG_U + g * 128:SEG_U + (g + 1) * 128] * mixed) * (gate * _sigmoid(gate))).astype(BF16)

    rows = MIXER_BLOCKS * BLOCK
    return pl.pallas_call(
        body, name="mixer_fwd", grid=(t // rows,),
        in_specs=[pl.BlockSpec((rows, D_IN), lambda i: (i, 0)), _kv_before_spec(lambda i: i),
                  _const_spec((2, 1, 8 * BLOCK)), _const_spec((1, D_SGU)), _const_spec((1, D_SGU)),
                  _const_spec((SGU_GROUPS, BLOCK, BLOCK)), _const_spec((BLOCK, SGU_GROUPS))],
        out_specs=(pl.BlockSpec((rows, D_MODEL), lambda i: (i, 0)),
                   pl.BlockSpec((MIXER_BLOCKS, 2, 2 * BLOCK, 8 * BLOCK), lambda i: (i, 0, 0, 0)),
                   pl.BlockSpec((MIXER_BLOCKS, 2, 1, 8 * BLOCK), lambda i: (i, 0, 0, 0))),
        out_shape=(jax.ShapeDtypeStruct((t, D_MODEL), BF16),
                   jax.ShapeDtypeStruct((t // BLOCK, 2, 2 * BLOCK, 8 * BLOCK), BF16),
                   jax.ShapeDtypeStruct((t // BLOCK, 2, 1, 8 * BLOCK), F32)),
        compiler_params=_params(dimension_semantics=("arbitrary",)),
    )(z, z, sink_rows, ln_g, ln_b, sgu_w, sgu_bt)


def _mixer_bwd(z, da, probs, sink_probs, ln_g, ln_b, sgu_w, sgu_wt, sgu_bt):
    t = z.shape[0]

    def body(z_all, kvp_ref, da_all, prob_ref, sink_prob_ref, lng_ref, lnb_ref, w_ref, wt_ref, bt_ref,
             dz_all, dsink_ref, dw_ref, db_ref, dlng_ref, dlnb_ref, carry_ref, dsink_acc, dbt_acc):
        step = pl.program_id(0)

        @pl.when(step == 0)
        def _():
            carry_ref[...] = jnp.zeros_like(carry_ref)
            dsink_acc[...] = jnp.zeros_like(dsink_acc)
            dbt_acc[...] = jnp.zeros_like(dbt_acc)
            dw_ref[...] = jnp.zeros_like(dw_ref)
            dlng_ref[...] = jnp.zeros_like(dlng_ref)
            dlnb_ref[...] = jnp.zeros_like(dlnb_ref)

        carry = carry_ref[...]
        for sub in reversed(range(MIXER_BLOCKS)):
            kv_before = kvp_ref[...] if sub == 0 else _Rows(z_all, sub - 1)[:, SEG_KV:SEG_KV + 2 * D_KV]
            carry = one_block(_Rows(z_all, sub), kv_before, _Rows(da_all, sub), prob_ref.at[sub], sink_prob_ref.at[sub],
                              carry, lng_ref, lnb_ref, w_ref, wt_ref, bt_ref, _Rows(dz_all, sub),
                              dw_ref, dlng_ref, dlnb_ref, dsink_acc, dbt_acc)
        carry_ref[...] = carry

        @pl.when(step == ns - 1)
        def _():
            db_ref[...] = dbt_acc[...].T[:SGU_GROUPS]
            lane_row = lax.broadcasted_iota(jnp.int32, (1, 128), 1)
            d_sink = jnp.zeros((1, 128), F32)
            for g in range(2):
                acc = dsink_acc[g]
                for j in range(8):
                    head_sum = jnp.sum(acc[:, j * BLOCK:(j + 1) * BLOCK], axis=-1, keepdims=True)
                    d_sink = d_sink + jnp.where(lane_row == 8 * g + j, head_sum, 0.0)
            dsink_ref[...] = d_sink

    def one_block(z_ref, kv_before, da_ref, prob_ref, sink_prob_ref, carry, lng_ref, lnb_ref, w_ref, wt_ref, bt_ref,
                  dz_ref, dw_ref, dlng_ref, dlnb_ref, dsink_acc, dbt_acc):
        kk, vv = _keys_values(z_ref, kv_before)
        vv_b = vv.astype(BF16)
        kkt_b, vvt_b = kk.T.astype(BF16), vv.T.astype(BF16)
        dkk = jnp.zeros((2 * BLOCK, D_KV), F32)
        dvv = jnp.zeros((2 * BLOCK, D_KV), F32)
        for g in range(2):
            qt = _heads_t([z_ref[:, _pair_cols(g, p, SEG_Q)] * ATTN_SCALE for p in range(4)], g).astype(BF16)
            prob_b, sink_prob = prob_ref[g], sink_prob_ref[g]
            prob = prob_b.astype(F32)
            ot = _dot(vvt_b, prob_b)
            gates = [z_ref[:, _pair_cols(g, p, SEG_GA)] for p in range(4)]
            sig = [_sigmoid(gt) for gt in gates]
            d_attn = [da_ref[:, _pair_cols(g, p)] for p in range(4)]
            d_ot = _heads_t([d_attn[p] * (gates[p] * sig[p]) for p in range(4)], g).astype(BF16)
            d_prob = _dot(vv_b, d_ot)
            delta = jnp.sum(prob * d_prob, axis=0, keepdims=True)
            d_scores = (prob * (d_prob - delta)).astype(BF16)
            dsink_acc[g] -= sink_prob * delta
            d_qt = _dot(kkt_b, d_scores)
            dkk = dkk + _dot_nt(d_scores, qt)
            dvv = dvv + _dot_nt(prob_b, d_ot)
            for p in range(4):
                dz_ref[:, _pair_cols(g, p, SEG_Q)] = (_pair_block(d_qt, p, g) * ATTN_SCALE).astype(BF16)
                d_silu = sig[p] * (1.0 + gates[p] * (1.0 - sig[p]))
                dz_ref[:, _pair_cols(g, p, SEG_GA)] = (d_attn[p] * _pair_block(ot, p, g) * d_silu).astype(BF16)
        d_kv = jnp.concatenate([dkk, dvv], axis=1)
        dz_ref[:, SEG_KV:SEG_KV + 2 * D_KV] = (d_kv[BLOCK:] + carry).astype(BF16)

        vhat, rstd = _layer_norm_fwd(z_ref[:, SEG_VS:SEG_VS + D_SGU])
        lng = lng_ref[...]
        vn = vhat * lng + lnb_ref[...]
        tril, triu = _tril(), _tril(transposed=True)
        lane = lax.broadcasted_iota(jnp.int32, (BLOCK, 128), 1)
        d_bt = jnp.zeros((BLOCK, 128), F32)
        d_vn = []
        for g in range(SGU_GROUPS):
            cols = slice(g * 128, (g + 1) * 128)
            wm = jnp.where(tril, w_ref[g], 0.0).astype(BF16)
            wmt = jnp.where(triu, wt_ref[g], 0.0).astype(BF16)
            vn_g = vn[:, cols].astype(BF16)
            mixed = _dot(wm, vn_g) + bt_ref[:, g:g + 1]
            gate = z_ref[:, SEG_GS + g * 128:SEG_GS + (g + 1) * 128]
            u = z_ref[:, SEG_U + g * 128:SEG_U + (g + 1) * 128]
            d_out = da_ref[:, D_ATTN + g * 128:D_ATTN + (g + 1) * 128]
            sg = _sigmoid(gate)
            d_um = d_out * (gate * sg)
            dz_ref[:, SEG_U + g * 128:SEG_U + (g + 1) * 128] = (d_um * mixed).astype(BF16)
            dz_ref[:, SEG_GS + g * 128:SEG_GS + (g + 1) * 128] = (
                d_out * (u * mixed) * (sg * (1.0 + gate * (1.0 - sg)))).astype(BF16)
            d_mixed = d_um * u
            d_mixed_b = d_mixed.astype(BF16)
            dw_ref[g] += jnp.where(tril, _dot_nt(d_mixed_b, vn_g), 0.0)
            d_bt = d_bt + jnp.where(lane == g, jnp.sum(d_mixed, axis=-1, keepdims=True), 0.0)
            d_vn.append(_dot(wmt, d_mixed_b))
        dbt_acc[...] += d_bt
        d_vn = jnp.concatenate(d_vn, axis=1)
        dlng_ref[...] += jnp.sum(d_vn * vhat, axis=0, keepdims=True)
        dlnb_ref[...] += jnp.sum(d_vn, axis=0, keepdims=True)
        d_vhat = d_vn * lng
        d_v = rstd * (d_vhat - jnp.mean(d_vhat, axis=-1, keepdims=True)
                      - vhat * jnp.mean(d_vhat * vhat, axis=-1, keepdims=True))
        dz_ref[:, SEG_VS:SEG_VS + D_SGU] = d_v.astype(BF16)
        return d_kv[:BLOCK]

    rows = MIXER_BLOCKS * BLOCK
    ns = t // rows
    rev = lambda i: ns - 1 - i
    return pl.pallas_call(
        body, name="mixer_bwd", grid=(ns,),
        in_specs=[pl.BlockSpec((rows, D_IN), lambda i: (rev(i), 0)), _kv_before_spec(rev),
                  pl.BlockSpec((rows, D_MODEL), lambda i: (rev(i), 0)),
                  pl.BlockSpec((MIXER_BLOCKS, 2, 2 * BLOCK, 8 * BLOCK), lambda i: (rev(i), 0, 0, 0)),
                  pl.BlockSpec((MIXER_BLOCKS, 2, 1, 8 * BLOCK), lambda i: (rev(i), 0, 0, 0)),
                  _const_spec((1, D_SGU)), _const_spec((1, D_SGU)),
                  _const_spec((SGU_GROUPS, BLOCK, BLOCK)), _const_spec((SGU_GROUPS, BLOCK, BLOCK)),
                  _const_spec((BLOCK, SGU_GROUPS))],
        out_specs=(pl.BlockSpec((rows, D_IN), lambda i: (rev(i), 0)), _const_spec((1, 128)),
                   _const_spec((SGU_GROUPS, BLOCK, BLOCK)), _const_spec((SGU_GROUPS, BLOCK)),
                   _const_spec((1, D_SGU)), _const_spec((1, D_SGU))),
        out_shape=(jax.ShapeDtypeStruct((t, D_IN), BF16), jax.ShapeDtypeStruct((1, 128), F32),
                   jax.ShapeDtypeStruct((SGU_GROUPS, BLOCK, BLOCK), F32), jax.ShapeDtypeStruct((SGU_GROUPS, BLOCK), F32),
                   jax.ShapeDtypeStruct((1, D_SGU), F32), jax.ShapeDtypeStruct((1, D_SGU), F32)),
        scratch_shapes=[pltpu.VMEM((BLOCK, 2 * D_KV), F32), pltpu.VMEM((2, 1, 8 * BLOCK), F32),
                        pltpu.VMEM((BLOCK, 128), F32)],
        compiler_params=_params(dimension_semantics=("arbitrary",)),
    )(z, z, da, probs, sink_probs, ln_g, ln_b, sgu_w, sgu_wt, sgu_bt)


def _out_proj_head(a, w_out_full, x, target, gate, final_g, tm=256):
    t, d = x.shape

    def body(a_ref, w_ref, x_ref, tg_ref, gate_ref, fg_ref, dx2_ref, dy_ref, loss_ref, dfg_ref, dgate_ref):
        @pl.when(pl.program_id(0) == 0)
        def _():
            loss_ref[...] = jnp.zeros_like(loss_ref)
            dfg_ref[...] = jnp.zeros_like(dfg_ref)
            dgate_ref[...] = jnp.zeros_like(dgate_ref)

        yv, gate, fg = _dot(a_ref[...], w_ref[...]), gate_ref[...], fg_ref[...]
        x2 = x_ref[...] + gate * yv
        r2 = lax.rsqrt(jnp.mean(x2 * x2, axis=-1, keepdims=True) + EPS)
        nrm = x2 * r2
        err = nrm * fg - tg_ref[...]
        loss_ref[...] += 0.5 * jnp.sum(jnp.mean(err * err, axis=-1, keepdims=True), axis=0, keepdims=True)
        d_out = err * (1.0 / d)
        dfg_ref[...] += jnp.sum(d_out * nrm, axis=0, keepdims=True)
        d_nrm = d_out * fg
        dx2 = r2 * (d_nrm - nrm * jnp.mean(d_nrm * nrm, axis=-1, keepdims=True))
        dx2_ref[...] = dx2
        dgate_ref[...] += jnp.sum(dx2 * yv, axis=0, keepdims=True)
        dy_ref[...] = (dx2 * gate).astype(BF16)

    blk = pl.BlockSpec((tm, d), lambda i: (i, 0))
    row = _const_spec((1, d))
    whole = pl.BlockSpec(w_out_full.shape, lambda i: (0, 0), pipeline_mode=pl.Buffered(1))
    return pl.pallas_call(
        body, name="out_proj_head", grid=(t // tm,),
        in_specs=[pl.BlockSpec((tm, a.shape[1]), lambda i: (i, 0)), whole, blk, blk, row, row],
        out_specs=(blk, blk, _const_spec((1, 128)), row, row),
        out_shape=(jax.ShapeDtypeStruct((t, d), F32), jax.ShapeDtypeStruct((t, d), BF16),
                   jax.ShapeDtypeStruct((1, 128), F32), jax.ShapeDtypeStruct((1, d), F32),
                   jax.ShapeDtypeStruct((1, d), F32)),
        compiler_params=_params(dimension_semantics=("arbitrary",)),
    )(a, w_out_full, x, target, gate, final_g)


def _z_proj_bwd_norm(dz, w_in_t, x, dx2, norm_g, scale, dep, tm=256):
    t, d = x.shape

    def body(dz_ref, w_ref, x_ref, dx2_ref, g_ref, sc_ref, dep_ref, gx_ref, dshift_ref, dscale_ref, dg_ref):
        @pl.when(pl.program_id(0) == 0)
        def _():
            dshift_ref[...] = jnp.zeros_like(dshift_ref)
            dscale_ref[...] = jnp.zeros_like(dscale_ref)
            dg_ref[...] = jnp.zeros_like(dg_ref)

        dh, xv, g = _dot(dz_ref[...], w_ref[...]), x_ref[...], g_ref[...]
        one_plus = 1.0 + sc_ref[...]
        r = lax.rsqrt(jnp.mean(xv * xv, axis=-1, keepdims=True) + EPS)
        xn = xv * r
        dshift_ref[...] += jnp.sum(dh, axis=0, keepdims=True)
        dscale_ref[...] += jnp.sum(dh * (xn * g), axis=0, keepdims=True)
        d_y = dh * one_plus
        dg_ref[...] += jnp.sum(d_y * xn, axis=0, keepdims=True)
        d_xn = d_y * g
        gx_ref[...] = dx2_ref[...] + r * (d_xn - xn * jnp.mean(d_xn * xn, axis=-1, keepdims=True))

    blk = pl.BlockSpec((tm, d), lambda i: (i, 0))
    row = _const_spec((1, d))
    whole = pl.BlockSpec(w_in_t.shape, lambda i: (0, 0), pipeline_mode=pl.Buffered(1))
    return pl.pallas_call(
        body, name="z_proj_bwd_norm", grid=(t // tm,),
        in_specs=[pl.BlockSpec((tm, dz.shape[1]), lambda i: (i, 0)), whole, blk, blk, row, row, _const_spec((8, 128))],
        out_specs=(blk, row, row, row),
        out_shape=(jax.ShapeDtypeStruct((t, d), F32),) + (jax.ShapeDtypeStruct((1, d), F32),) * 3,
        compiler_params=_params(dimension_semantics=("arbitrary",)),
    )(dz, w_in_t, x, dx2, norm_g, scale, dep)


def _adamw(w, g, m, v):
    m = ADAM_B1 * m + (1.0 - ADAM_B1) * g
    v = ADAM_B2 * v + (1.0 - ADAM_B2) * (g * g)
    m_hat = m / (1.0 - ADAM_B1 ** ADAM_STEP)
    v_hat = v / (1.0 - ADAM_B2 ** ADAM_STEP)
    delta = -ADAM_LR * (m_hat / (jnp.sqrt(v_hat) + ADAM_EPS) + ADAM_WD * w)
    return delta, m, v


def _relay_sum(second_chip, pair, land, tr):
    _, r, c = pair.shape

    def body(chip_ref, a_ref, b_ref, o_ref):
        o_ref[...] = (a_ref[...].astype(F32) + b_ref[...].astype(F32)).astype(BF16)

    return pl.pallas_call(
        body, name="w_in_grad_relay_sum",
        grid_spec=pltpu.PrefetchScalarGridSpec(
            num_scalar_prefetch=1, grid=(r // tr,),
            in_specs=[pl.BlockSpec((None, tr, c), lambda i, chip_ref: (chip_ref[0], i, 0)),
                      pl.BlockSpec((None, tr, c), lambda i, chip_ref: (1, i, 0))],
            out_specs=pl.BlockSpec((tr, c), lambda i, chip_ref: (i, 0))),
        out_shape=jax.ShapeDtypeStruct((r, c), BF16),
        compiler_params=_params(dimension_semantics=("arbitrary",)),
    )(second_chip, pair, land)


def _adam_from_chips(chip, pair, landed, w, m, v, name, tc):
    _, r, c = pair.shape
    n = len(landed)

    def body(chip_ref, own_ref, *refs):
        w_ref, m_ref, v_ref, g_ref, d_ref, nm_ref, nv_ref = refs[n:]
        g = own_ref[...].astype(F32)
        for k in range(n):
            g = g + refs[k][...].astype(F32)
        g_ref[...] = g
        d_ref[...], nm_ref[...], nv_ref[...] = _adamw(w_ref[...], g, m_ref[...], v_ref[...])

    def landed_spec(index):
        return pl.BlockSpec((None, r, tc), lambda i, chip_ref: (index, 0, i))

    blk = pl.BlockSpec((r, tc), lambda i, chip_ref: (0, i))
    return pl.pallas_call(
        body, name=name,
        grid_spec=pltpu.PrefetchScalarGridSpec(
            num_scalar_prefetch=1, grid=(c // tc,),
            in_specs=[pl.BlockSpec((None, r, tc), lambda i, chip_ref: (chip_ref[0], 0, i))]
            + [landed_spec(index) for _, index in landed] + [blk, blk, blk],
            out_specs=(blk,) * 4),
        out_shape=(jax.ShapeDtypeStruct((r, c), F32),) * 4,
        compiler_params=_params(dimension_semantics=("arbitrary",)),
    )(chip, pair, *[array for array, _ in landed], w, m, v)


def _adam_w_ada(act_t, dmod_mine, w, m, v, tr=512):
    r, c = w.shape

    def body(a_ref, dm_ref, w_ref, m_ref, v_ref, g_ref, d_ref, nm_ref, nv_ref):
        g = _dot(a_ref[...].astype(BF16), dm_ref[...].astype(BF16))
        g_ref[...] = g
        d_ref[...], nm_ref[...], nv_ref[...] = _adamw(w_ref[...], g, m_ref[...], v_ref[...])

    blk = pl.BlockSpec((tr, c), lambda i: (i, 0))
    return pl.pallas_call(
        body, name="adam_w_ada", grid=(r // tr,),
        in_specs=[pl.BlockSpec((tr, N_DEV), lambda i: (i, 0)), _const_spec((N_DEV, c)), blk, blk, blk],
        out_specs=(blk,) * 4, out_shape=(jax.ShapeDtypeStruct((r, c), F32),) * 4,
        compiler_params=_params(dimension_semantics=("arbitrary",)),
    )(act_t, dmod_mine, w, m, v)


def _pack_small(d_shift, d_scale, d_gate, d_norm_g, d_final_g, d_ln_g, d_ln_b, loss, d_sinks, d_sgu_b):
    def body(shift_ref, scale_ref, gate_ref, ng_ref, fg_ref, lng_ref, lnb_ref, loss_ref, sink_ref, b_ref, o_ref):
        o_ref[...] = jnp.zeros_like(o_ref)
        o_ref[ROW_SHIFT:ROW_SHIFT + 1, :] = shift_ref[...]
        o_ref[ROW_SCALE:ROW_SCALE + 1, :] = scale_ref[...]
        o_ref[ROW_GATE:ROW_GATE + 1, :] = gate_ref[...]
        o_ref[ROW_NORM_G:ROW_NORM_G + 1, :] = ng_ref[...]
        o_ref[ROW_FINAL_G:ROW_FINAL_G + 1, :] = fg_ref[...]
        o_ref[ROW_LN:ROW_LN + 1, 0:D_SGU] = lng_ref[...]
        o_ref[ROW_LN:ROW_LN + 1, D_SGU:2 * D_SGU] = lnb_ref[...]
        o_ref[ROW_MISC:ROW_MISC + 1, 0:128] = loss_ref[...]
        o_ref[ROW_MISC:ROW_MISC + 1, 128:256] = sink_ref[...]
        o_ref[ROW_SGU_B:ROW_SGU_B + SGU_GROUPS, 0:BLOCK] = b_ref[...]

    return pl.pallas_call(
        body, name="pack_small", out_shape=jax.ShapeDtypeStruct((SMALL_ROWS, D_MODEL), F32),
        compiler_params=_params(),
    )(d_shift, d_scale, d_gate, d_norm_g, d_final_g, d_ln_g, d_ln_b, loss, d_sinks, d_sgu_b)


_SMALL_NAMES = ("norm_g", "b_ada", "attn_sinks", "sgu_ln_g", "sgu_ln_b", "sgu_w", "sgu_b", "final_g")


def _adam_small(partials, d_sgu_w_all, weights, moments_m, moments_v):
    names = _SMALL_NAMES
    k = len(names)

    def body(*refs):
        p_ref, sw_ref = refs[0], refs[1]
        w_refs, m_refs, v_refs = refs[2:2 + k], refs[2 + k:2 + 2 * k], refs[2 + 2 * k:2 + 3 * k]
        loss_ref, dmod_ref = refs[2 + 3 * k], refs[3 + 3 * k]
        out_refs = refs[4 + 3 * k:4 + 7 * k]
        sum_ref = refs[4 + 7 * k]
        total = p_ref[0]
        for j in range(1, N_DEV):
            total = total + p_ref[j]
        sum_ref[...] = total
        for j in range(N_DEV):
            for part, row in enumerate((ROW_SHIFT, ROW_SCALE, ROW_GATE)):
                dmod_ref[j:j + 1, part * D_MODEL:(part + 1) * D_MODEL] = p_ref[j, row:row + 1, :]
        loss_ref[...] = sum_ref[ROW_MISC:ROW_MISC + 1, 0:1]
        d_sgu_w = sw_ref[0]
        for j in range(1, N_DEV):
            d_sgu_w = d_sgu_w + sw_ref[j]
        grads = {
            "norm_g": sum_ref[ROW_NORM_G:ROW_NORM_G + 1, :],
            "b_ada": jnp.concatenate([sum_ref[r:r + 1, :] for r in (ROW_SHIFT, ROW_SCALE, ROW_GATE)], axis=1),
            "attn_sinks": sum_ref[ROW_MISC:ROW_MISC + 1, 128:128 + N_Q_HEADS],
            "sgu_ln_g": sum_ref[ROW_LN:ROW_LN + 1, 0:D_SGU],
            "sgu_ln_b": sum_ref[ROW_LN:ROW_LN + 1, D_SGU:2 * D_SGU],
            "sgu_w": d_sgu_w[None],
            "sgu_b": sum_ref[ROW_SGU_B:ROW_SGU_B + SGU_GROUPS, 0:BLOCK][None],
            "final_g": sum_ref[ROW_FINAL_G:ROW_FINAL_G + 1, :],
        }
        for i, name in enumerate(names):
            g = grads[name]
            delta, m, v = _adamw(w_refs[i][...], g, m_refs[i][...], v_refs[i][...])
            out_refs[4 * i][...] = g
            out_refs[4 * i + 1][...] = delta
            out_refs[4 * i + 2][...] = m
            out_refs[4 * i + 3][...] = v

    shapes = [jax.ShapeDtypeStruct((1, 1), F32), jax.ShapeDtypeStruct((N_DEV, 3 * D_MODEL), F32)]
    for name in names:
        shapes += [jax.ShapeDtypeStruct(weights[name].shape, F32)] * 4
    outs = pl.pallas_call(
        body, name="adam_small", out_shape=tuple(shapes),
        scratch_shapes=[pltpu.VMEM((SMALL_ROWS, D_MODEL), F32)],
        compiler_params=_params(),
    )(partials, d_sgu_w_all, *[weights[n] for n in names], *[moments_m[n] for n in names],
      *[moments_v[n] for n in names])
    return outs[0], outs[1], {name: outs[2 + 4 * i:6 + 4 * i] for i, name in enumerate(names)}


def kernel(x, c, norm_g, w_ada, b_ada, w_in, attn_sinks, sgu_ln_g, sgu_ln_b, sgu_w, sgu_b, w_out, final_g, loss_target, m_norm_g, m_w_ada, m_b_ada, m_w_in, m_attn_sinks, m_sgu_ln_g, m_sgu_ln_b, m_sgu_w, m_sgu_b, m_w_out, m_final_g, v_norm_g, v_w_ada, v_b_ada, v_w_in, v_attn_sinks, v_sgu_ln_g, v_sgu_ln_b, v_sgu_w, v_sgu_b, v_w_out, v_final_g):
    xi, yi, ci = _place()
    me = 4 * xi + 2 * yi + ci
    x2d, target = x[0], loss_target[0]
    t = x2d.shape[0]

    core = ci.astype(jnp.int32).reshape(1)
    chip = (2 * xi + yi).astype(jnp.int32).reshape(1)

    first = _own_block_copies(_first_targets)
    first_flight = _start_copies([_with_own_slot(w_in[0].T.astype(BF16), me)], first, 2, core, "gather_w_in_start")

    c_all = _all_gather_small(c.reshape(8, 256) + first_flight[3][0, 0], "gather_c").reshape(N_DEV, D_MODEL)
    b_mine = lax.dynamic_slice(b_ada, (0, me * W_ADA_SHARD), (1, W_ADA_SHARD))
    c_act, mod_part = _modulation(c_all, w_ada[0], b_mine)
    mod_all = _all_gather_small(mod_part, "gather_mod")

    across = _wait_then_start(first_flight, lambda *a: first(*a)[1:], _second_axis_stage_copies, 3, mod_all,
                              "gather_w_in_second_axis_stage")
    mod = lax.dynamic_index_in_dim(mod_all, me, axis=1, keepdims=False).reshape(1, 3 * D_MODEL)
    mod = mod + across[3][0, 0]
    shift, scale, gate = mod[:, :D_MODEL], mod[:, D_MODEL:2 * D_MODEL], mod[:, 2 * D_MODEL:]
    h = _modulated_norm(x2d, norm_g, scale, shift)

    w_in_pair = _wait_copies((first_flight[0], first_flight[1], across[2], None), lambda *a: first(*a)[:1], h,
                             "gather_w_in_sibling_wait")
    tile_order = jnp.asarray(_Z_TILE_ORDER, jnp.int32)[chip[0]]
    z_own = _z_proj(h, w_in_pair[0].reshape(D_IN, D_MODEL), tile_order, 0, 1, None, "z_proj_own")
    w_out_first = _group(_own_block_copies(_my_core_and_sibling), 1, 1, 1)
    forward = _wait_then_start(
        (across[0], across[1], w_in_pair, None), lambda *a: _second_axis_stage_copies(*a)[:1],
        lambda refs, s, r: _second_axis_forward_copies(refs[:1], s, r) + w_out_first(refs, s, r), 5, z_own,
        "gather_w_in_second_axis_forward", more_bufs=[_with_own_slot(w_out[0].astype(BF16), me)])
    w_out_flight = (forward[0], forward[1], forward[2][1:], None)
    w_in_most = _wait_copies((across[0], across[1], forward[2][:1], None),
                             lambda *a: _second_axis_stage_copies(*a)[1:2], z_own, "gather_w_in_first_forward_wait")
    w_in_most = _wait_copies((forward[0], forward[1], w_in_most, None), _second_axis_forward_copies, z_own,
                             "gather_w_in_second_forward_wait")
    z_early = _z_proj(h, w_in_most[0].reshape(D_IN, D_MODEL), tile_order, 1, _Z_EARLY_TILES - 1, z_own, "z_proj_early")
    w_in_flight = _wait_then_start((across[0], across[1], w_in_most, None),
                                   lambda *a: _second_axis_stage_copies(*a)[2:], _diagonal_forward_copies, 1,
                                   z_early, "gather_w_in_last_stage")
    w_in_all = _wait_copies(w_in_flight, _diagonal_forward_copies, z_early, "gather_w_in_last_wait")[0]
    w_in_t = w_in_all.reshape(D_IN, D_MODEL)
    z = _z_proj(h, w_in_t, tile_order, _Z_EARLY_TILES, 7 - _Z_EARLY_TILES, z_early, "z_proj_late")
    w_out_flight = _wait_then_start(w_out_flight, _group(_own_block_copies(_my_core_and_sibling), 0, 1, 1),
                                    _forward_copies, 3, z, "gather_w_out_forward_stage")
    sink_rows = jnp.repeat(attn_sinks.reshape(N_Q_HEADS), BLOCK).reshape(2, 1, 8 * BLOCK)
    sgu_bt = sgu_b[0].T
    a, probs, sink_probs = _mixer_fwd(z, sink_rows + w_out_flight[3][0, 0], sgu_ln_g, sgu_ln_b, sgu_w[0], sgu_bt)
    w_out_all = _wait_copies(w_out_flight, _forward_copies, a, "gather_w_out_forward_wait")[0]
    w_out_full = w_out_all.reshape(D_MODEL, D_MODEL)
    final_g_row = final_g.reshape(1, D_MODEL)
    dx2, dy, loss_part, d_final_g, d_gate = _out_proj_head(a, w_out_full, x2d, target, gate, final_g_row)

    da = _matmul(dy, w_out_full, "nt", F32, min(t, 1024), 1024, "out_proj_bwd")
    dw_out = _matmul(a, dy, "tn", BF16, 1024, 1024, "w_out_grad").reshape(4, 2, W_OUT_SHARD, D_MODEL)
    pair_out = _pair_reduce(dw_out, "w_out_grad_pair_reduce", W_OUT_SHARD // 2)
    dz, d_sinks, d_sgu_w, d_sgu_b, d_ln_g, d_ln_b = _mixer_bwd(
        z, da, probs, sink_probs, sgu_ln_g, sgu_ln_b, sgu_w[0], jnp.swapaxes(sgu_w[0], 1, 2), sgu_bt)
    sgu_w_to_all = _group(_own_block_copies(_all_others), 2, 1, 3)
    both = _start_copies(
        [pair_out, lax.empty((3, W_OUT_SHARD, D_MODEL), BF16), _with_own_slot(d_sgu_w, me)],
        lambda refs, s, r: _chip_copies(refs[:2], s, r) + sgu_w_to_all(refs, s, r), 3 + N_DEV - 1, core,
        "w_out_grad_chip_and_sgu_w_gather_start")
    out_flight, sgu_w_flight = (both[0], both[1], both[2][:2], None), (both[0], both[1], both[2][2:], None)
    dw_in_t = _matmul(dz, h, "tn", BF16, 768, D_MODEL, "w_in_grad", dep=both[3])
    dw_in_t = dw_in_t.reshape(4, 2, W_IN_SHARD, D_MODEL)
    pair_in = _pair_reduce(dw_in_t, "w_in_grad_pair_reduce", W_IN_SHARD // 3)
    hop1 = _start_copies([pair_in, lax.empty((2, W_IN_SHARD, D_MODEL), BF16)], _first_hop_copies, 2, core,
                         "w_in_grad_first_hop_start")
    grad_x, d_shift, d_scale, d_norm_g = _z_proj_bwd_norm(dz, w_in_t, x2d, dx2, norm_g, scale, hop1[3])

    partial = _pack_small(d_shift, d_scale, d_gate, d_norm_g, d_final_g, d_ln_g, d_ln_b, loss_part, d_sinks, d_sgu_b)
    small_flight = _start_copies([_with_own_slot(partial, me)], _own_block_copies(_all_others), N_DEV - 1, core,
                                 "small_grad_gather_start")
    pair_in, land_first = _wait_copies(hop1, _first_hop_copies, small_flight[3], "w_in_grad_first_hop_wait")
    second_chip = (2 * ((xi + ci) % 2) + (yi + 1 - ci) % 2).astype(jnp.int32).reshape(1)
    relay = _relay_sum(second_chip, pair_in, land_first, W_IN_SHARD // 3)
    hop2 = _start_copies([relay, lax.empty((1, W_IN_SHARD, D_MODEL), BF16)], _second_hop_copies, 1, core,
                         "w_in_grad_second_hop_start")
    pair_out, land_out = _wait_copies(out_flight, _chip_copies, hop2[3], "w_out_grad_chip_wait")
    big = {"w_out": _adam_from_chips(chip, pair_out, [(land_out, k) for k in range(3)], w_out[0], m_w_out[0],
                                     v_w_out[0], "adam_w_out", 1024)}
    partial_all = _wait_copies(small_flight, _own_block_copies(_all_others), big["w_out"][0],
                               "small_grad_gather_wait")[0]
    d_sgu_w_all = _wait_copies(sgu_w_flight, _group(_own_block_copies(_all_others), 0, 1, 3), partial_all,
                               "sgu_w_grad_gather_wait")[0]
    weights = {"norm_g": norm_g, "b_ada": b_ada, "attn_sinks": attn_sinks, "sgu_ln_g": sgu_ln_g,
               "sgu_ln_b": sgu_ln_b, "sgu_w": sgu_w, "sgu_b": sgu_b, "final_g": final_g_row}
    moments_m = {"norm_g": m_norm_g, "b_ada": m_b_ada, "attn_sinks": m_attn_sinks, "sgu_ln_g": m_sgu_ln_g,
                 "sgu_ln_b": m_sgu_ln_b, "sgu_w": m_sgu_w, "sgu_b": m_sgu_b,
                 "final_g": m_final_g.reshape(1, D_MODEL)}
    moments_v = {"norm_g": v_norm_g, "b_ada": v_b_ada, "attn_sinks": v_attn_sinks, "sgu_ln_g": v_sgu_ln_g,
                 "sgu_ln_b": v_sgu_ln_b, "sgu_w": v_sgu_w, "sgu_b": v_sgu_b,
                 "final_g": v_final_g.reshape(1, D_MODEL)}
    loss, dmod_all, small = _adam_small(partial_all, d_sgu_w_all, weights, moments_m, moments_v)
    small["final_g"] = tuple(o.reshape(D_MODEL) for o in small["final_g"])

    dmod_mine = lax.dynamic_slice(dmod_all, (0, me * W_ADA_SHARD), (N_DEV, W_ADA_SHARD))
    big["w_ada"] = _adam_w_ada(c_act.T, dmod_mine, w_ada[0], m_w_ada[0], v_w_ada[0])
    _, land_second = _wait_copies(hop2, _second_hop_copies, big["w_ada"][0], "w_in_grad_second_hop_wait")
    big["w_in"] = tuple(o.T for o in _adam_from_chips(
        chip, pair_in, [(land_first, 0), (land_second, 0)], w_in[0].T, m_w_in[0].T, v_w_in[0].T, "adam_w_in", 512))
    order = ["norm_g", "w_ada", "b_ada", "w_in", "attn_sinks", "sgu_ln_g", "sgu_ln_b", "sgu_w", "sgu_b", "w_out",
             "final_g"]
    outs = [loss.reshape(()), grad_x[None]]
    for k in range(4):
        for name in order:
            outs.append(big[name][k][None] if name in big else small[name][k])
    return tuple(outs)
```

```python
import jax
import jax.numpy as jnp
from jax import lax
from jax.experimental import pallas as pl
from jax.experimental.pallas import tpu as pltpu

F32 = jnp.float32
BF16 = jnp.bfloat16
MESH = pl.DeviceIdType.MESH

N_DEV = 8
D_MODEL = 2048
HEAD_DIM = 64
D_ATTN = 1024
N_Q_HEADS = 16
D_KV = 128
BLOCK = 128
D_SGU = 1024
SGU_GROUPS = 8
D_IN = 5376
W_IN_SHARD = D_IN // N_DEV
W_OUT_SHARD = D_MODEL // N_DEV
W_ADA_SHARD = 3 * D_MODEL // N_DEV
EPS = 1e-6
ATTN_SCALE = 0.125

ADAM_LR = 0.001
ADAM_B1 = 0.9
ADAM_B2 = 0.999
ADAM_EPS = 1e-08
ADAM_WD = 0.01
ADAM_STEP = 10

SEG_Q, SEG_KV, SEG_GA, SEG_U, SEG_VS, SEG_GS = 0, 1024, 1280, 2304, 3328, 4352

VMEM_LIMIT = 56 * 1024 * 1024

ROW_SHIFT, ROW_SCALE, ROW_GATE, ROW_NORM_G, ROW_FINAL_G, ROW_LN, ROW_MISC, ROW_SGU_B = 0, 1, 2, 3, 4, 5, 6, 8
SMALL_ROWS = 16


def _params(**kw):
    return pltpu.CompilerParams(vmem_limit_bytes=VMEM_LIMIT, **kw)


def _sigmoid(x):
    return 0.5 * (jnp.tanh(0.5 * x) + 1.0)


def _place():
    return lax.axis_index("x"), lax.axis_index("y"), lax.axis_index("c")


def _pair_reduce(blocks, name, row_chunk):
    _, _, r, cols = blocks.shape
    assert r % row_chunk == 0

    def body(in_ref, out_ref, land, own, summed, send_sems, recv_sems, own_sems, out_sems):
        x, y, c = _place()
        sends, loads, stores = [], [], []
        for m in range(4):
            cp = pltpu.make_async_remote_copy(
                src_ref=in_ref.at[m, 1 - c], dst_ref=land.at[m], send_sem=send_sems.at[m], recv_sem=recv_sems.at[m],
                device_id=(x, y, 1 - c), device_id_type=MESH)
            cp.start()
            sends.append(cp)
            ld = pltpu.make_async_copy(in_ref.at[m, c], own.at[m], own_sems.at[m])
            ld.start()
            loads.append(ld)
        for m in range(4):
            sends[m].wait_recv()
            loads[m].wait()
            for k in range(r // row_chunk):
                rows = slice(k * row_chunk, (k + 1) * row_chunk)
                summed[m, rows, :] = (own[m, rows, :].astype(F32) + land[m, rows, :].astype(F32)).astype(BF16)
            st = pltpu.make_async_copy(summed.at[m], out_ref.at[m], out_sems.at[m])
            st.start()
            stores.append(st)
        for m in range(4):
            sends[m].wait_send()
            stores[m].wait()

    spec = pl.BlockSpec(memory_space=pl.ANY)
    return pl.pallas_call(
        body, name=name, out_shape=jax.ShapeDtypeStruct((4, r, cols), BF16),
        in_specs=[spec], out_specs=spec,
        scratch_shapes=[pltpu.VMEM((4, r, cols), BF16), pltpu.VMEM((4, r, cols), BF16), pltpu.VMEM((4, r, cols), BF16),
                        pltpu.SemaphoreType.DMA((4,)), pltpu.SemaphoreType.DMA((4,)), pltpu.SemaphoreType.DMA((4,)),
                        pltpu.SemaphoreType.DMA((4,))],
        compiler_params=_params(),
    )(blocks)


_HBM = pl.BlockSpec(memory_space=pltpu.HBM)
_SEM = pl.BlockSpec(memory_space=pltpu.SEMAPHORE)
_EFFECT = pltpu.SideEffectType.DATAFLOW_SIDE_EFFECTING


def _start_copies(bufs, copies, n_copies, after, name):
    nb = len(bufs)

    def body(*refs):
        for cp in copies(refs[:nb], refs[nb + 1], refs[nb + 2]):
            cp.start()
        refs[-1][...] = jnp.zeros_like(refs[-1])

    out = pl.pallas_call(
        body, name=name,
        out_shape=(pltpu.SemaphoreType.DMA((n_copies,)), pltpu.SemaphoreType.DMA((n_copies,)),
                   *[pltpu.HBM(b.shape, b.dtype) for b in bufs], jax.ShapeDtypeStruct((8, 128), F32)),
        in_specs=(_HBM,) * nb + (pl.BlockSpec(memory_space=pl.ANY),),
        out_specs=(_SEM, _SEM) + (_HBM,) * nb + (pl.BlockSpec(memory_space=pltpu.VMEM),),
        input_output_aliases={i: 2 + i for i in range(nb)},
        compiler_params=pltpu.CompilerParams(has_side_effects=_EFFECT),
    )(*[pltpu.with_memory_space_constraint(b, pltpu.HBM) for b in bufs], after)
    return out[0], out[1], list(out[2:2 + nb]), out[-1]


def _wait_copies(flight, copies, after, name):
    send_sems, recv_sems, bufs, _ = flight
    nb = len(bufs)

    def body(*refs):
        for cp in copies(refs[:nb], refs[nb], refs[nb + 1]):
            cp.wait_send()
            cp.wait_recv()

    return pl.pallas_call(
        body, name=name,
        out_shape=tuple(pltpu.HBM(b.shape, b.dtype) for b in bufs),
        in_specs=(_HBM,) * nb + (_SEM, _SEM, pl.BlockSpec(memory_space=pl.ANY)), out_specs=(_HBM,) * nb,
        input_output_aliases={i: i for i in range(nb)},
        compiler_params=pltpu.CompilerParams(has_side_effects=_EFFECT),
    )(*bufs, send_sems, recv_sems, after)


class _From:
    def __init__(self, sems, offset):
        self.sems, self.offset = sems, offset

    @property
    def at(self):
        return self

    def __getitem__(self, k):
        return self.sems.at[k + self.offset]


def _group(copies, first_buf, n_bufs, offset):
    def grouped(refs, send_sems, recv_sems):
        return copies(refs[first_buf:first_buf + n_bufs], _From(send_sems, offset), _From(recv_sems, offset))
    return grouped


def _wait_then_start(flight, waited, started, n_started, after, name, more_bufs=()):
    old_send, old_recv, bufs, _ = flight
    bufs = list(bufs) + [pltpu.with_memory_space_constraint(b, pltpu.HBM) for b in more_bufs]
    nb = len(bufs)

    def body(*refs):
        for cp in waited(refs[:nb], refs[nb], refs[nb + 1]):
            cp.wait_send()
            cp.wait_recv()
        for cp in started(refs[:nb], refs[nb + 3], refs[nb + 4]):
            cp.start()
        refs[-1][...] = jnp.zeros_like(refs[-1])

    out = pl.pallas_call(
        body, name=name,
        out_shape=(pltpu.SemaphoreType.DMA((n_started,)), pltpu.SemaphoreType.DMA((n_started,)),
                   *[pltpu.HBM(b.shape, b.dtype) for b in bufs], jax.ShapeDtypeStruct((8, 128), F32)),
        in_specs=(_HBM,) * nb + (_SEM, _SEM, pl.BlockSpec(memory_space=pl.ANY)),
        out_specs=(_SEM, _SEM) + (_HBM,) * nb + (pl.BlockSpec(memory_space=pltpu.VMEM),),
        input_output_aliases={i: 2 + i for i in range(nb)},
        compiler_params=pltpu.CompilerParams(has_side_effects=_EFFECT),
    )(*bufs, old_send, old_recv, after)
    return out[0], out[1], list(out[2:2 + nb]), out[-1]


def _chip_copies(refs, send_sems, recv_sems):
    pair_ref, land_ref = refs
    x, y, c = _place()
    chips = [(1 - x, y), (x, 1 - y), (1 - x, 1 - y)]
    return [pltpu.make_async_remote_copy(
        src_ref=pair_ref.at[2 * chip[0] + chip[1]], dst_ref=land_ref.at[k],
        send_sem=send_sems.at[k], recv_sem=recv_sems.at[k],
        device_id=(*chip, c), device_id_type=MESH) for k, chip in enumerate(chips)]


def _first_hop_copies(refs, send_sems, recv_sems):
    pair_ref, land_ref = refs
    x, y, c = _place()
    first = ((x + 1 - c) % 2, (y + c) % 2)
    blocks = [2 * first[0] + first[1], 2 * (1 - x) + (1 - y)]
    return [pltpu.make_async_remote_copy(
        src_ref=pair_ref.at[blocks[k]], dst_ref=land_ref.at[k], send_sem=send_sems.at[k], recv_sem=recv_sems.at[k],
        device_id=(*first, c), device_id_type=MESH) for k in range(2)]


def _second_hop_copies(refs, send_sems, recv_sems):
    relay_ref, land_ref = refs
    x, y, c = _place()
    second = ((x + c) % 2, (y + 1 - c) % 2)
    return [pltpu.make_async_remote_copy(
        src_ref=relay_ref, dst_ref=land_ref.at[0], send_sem=send_sems.at[0], recv_sem=recv_sems.at[0],
        device_id=(*second, c), device_id_type=MESH)]


def _own_block_copies(targets):
    def copies(refs, send_sems, recv_sems):
        x, y, c = _place()
        mine = refs[0].at[4 * x + 2 * y + c]
        return [pltpu.make_async_remote_copy(
            src_ref=mine, dst_ref=mine, send_sem=send_sems.at[k], recv_sem=recv_sems.at[k],
            device_id=to, device_id_type=MESH) for k, to in enumerate(targets(x, y, c))]
    return copies


def _my_core_and_sibling(x, y, c):
    return [(x, y, 1 - c), (1 - x, y, c), (x, 1 - y, c), (1 - x, 1 - y, c)]


def _all_others(x, y, c):
    flip = lambda v, f: 1 - v if f else v
    return [(flip(x, r & 4), flip(y, r & 2), flip(c, r & 1)) for r in range(1, N_DEV)]


def _forward_copies(refs, send_sems, recv_sems):
    x, y, c = _place()
    chips = [(1 - x, y), (x, 1 - y), (1 - x, 1 - y)]
    return [pltpu.make_async_remote_copy(
        src_ref=refs[0].at[4 * chip[0] + 2 * chip[1] + c], dst_ref=refs[0].at[4 * chip[0] + 2 * chip[1] + c],
        send_sem=send_sems.at[k], recv_sem=recv_sems.at[k],
        device_id=(x, y, 1 - c), device_id_type=MESH) for k, chip in enumerate(chips)]


def _first_axis_chip(x, y, c):
    return (x + 1 - c) % 2, (y + c) % 2


def _second_axis_chip(x, y, c):
    return (x + c) % 2, (y + 1 - c) % 2


def _first_targets(x, y, c):
    return [(x, y, 1 - c), (*_first_axis_chip(x, y, c), c)]


def _all_gather_small(shard, name):
    def body(in_ref, out_ref, send_sems, recv_sems, local_sem):
        x, y, c = _place()
        me, sibling = 4 * x + 2 * y + c, (x, y, 1 - c)
        first, second = _first_axis_chip(x, y, c), _second_axis_chip(x, y, c)

        def pair(chip):
            return out_ref.at[pl.ds(2 * (2 * chip[0] + chip[1]), 2)]

        def exchange(k, src, dst, to):
            cp = pltpu.make_async_remote_copy(src_ref=src, dst_ref=dst, send_sem=send_sems.at[k],
                                              recv_sem=recv_sems.at[k], device_id=to, device_id_type=MESH)
            cp.start()
            cp.wait()

        own = pltpu.make_async_copy(in_ref, out_ref.at[me], local_sem)
        own.start()
        exchange(0, in_ref, out_ref.at[me], sibling)
        own.wait()
        exchange(1, pair((x, y)), pair((x, y)), (*second, c))
        exchange(2, pair(second), pair(second), sibling)
        exchange(3, pair(first), pair(first), (*second, c))

    spec = pl.BlockSpec(memory_space=pltpu.VMEM)
    return pl.pallas_call(
        body, name=name, out_shape=jax.ShapeDtypeStruct((N_DEV,) + shard.shape, shard.dtype),
        in_specs=[spec], out_specs=spec,
        scratch_shapes=[pltpu.SemaphoreType.DMA((4,)), pltpu.SemaphoreType.DMA((4,)), pltpu.SemaphoreType.DMA],
        compiler_params=_params(),
    )(shard)


def _slot_copies(refs, send_sems, recv_sems, plan):
    copies = []
    for k, ((px, py, pc), to) in enumerate(plan):
        blk = refs[0].at[4 * px + 2 * py + pc]
        copies.append(pltpu.make_async_remote_copy(
            src_ref=blk, dst_ref=blk, send_sem=send_sems.at[k], recv_sem=recv_sems.at[k],
            device_id=to, device_id_type=MESH))
    return copies


def _second_axis_stage_copies(refs, send_sems, recv_sems):
    x, y, c = _place()
    first, second = (*_first_axis_chip(x, y, c), c), (*_second_axis_chip(x, y, c), c)
    return _slot_copies(refs, send_sems, recv_sems, [((x, y, c), second), (first, (x, y, 1 - c)), (first, second)])


def _second_axis_forward_copies(refs, send_sems, recv_sems):
    x, y, c = _place()
    return _slot_copies(refs, send_sems, recv_sems, [((*_second_axis_chip(x, y, c), c), (x, y, 1 - c))])


def _diagonal_forward_copies(refs, send_sems, recv_sems):
    x, y, c = _place()
    blk = refs[0].at[4 * (1 - x) + 2 * (1 - y) + c]
    return [pltpu.make_async_remote_copy(
        src_ref=blk, dst_ref=blk, send_sem=send_sems.at[0], recv_sem=recv_sems.at[0],
        device_id=(x, y, 1 - c), device_id_type=MESH)]


def _with_own_slot(block, me):
    return lax.dynamic_update_index_in_dim(lax.empty((N_DEV,) + block.shape, block.dtype), block, me, 0)


def _matmul(a, b, dims, out_dtype, tm, tn, name, dep=None):
    if dims == "nn":
        (m, k), n = a.shape, b.shape[1]
        a_spec = pl.BlockSpec((tm, k), lambda i, j: (i, 0))
        b_spec = pl.BlockSpec((k, tn), lambda i, j: (0, j))
        contract = ((1,), (0,))
    elif dims == "nt":
        (m, k), n = a.shape, b.shape[0]
        a_spec = pl.BlockSpec((tm, k), lambda i, j: (i, 0))
        b_spec = pl.BlockSpec((tn, k), lambda i, j: (j, 0))
        contract = ((1,), (1,))
    else:
        (k, m), n = a.shape, b.shape[1]
        a_spec = pl.BlockSpec((k, tm), lambda i, j: (0, i))
        b_spec = pl.BlockSpec((k, tn), lambda i, j: (0, j))
        contract = ((0,), (0,))
    assert m % tm == 0 and n % tn == 0 and a.dtype == BF16 and b.dtype == BF16

    def body(a_ref, b_ref, *rest):
        rest[-1][...] = lax.dot_general(a_ref[...], b_ref[...], (contract, ((), ())),
                                        preferred_element_type=F32).astype(out_dtype)

    deps = [] if dep is None else [dep]
    return pl.pallas_call(
        body, name=name, grid=(m // tm, n // tn),
        in_specs=[a_spec, b_spec] + [pl.BlockSpec((8, 128), lambda i, j: (0, 0))] * len(deps),
        out_specs=pl.BlockSpec((tm, tn), lambda i, j: (i, j)),
        out_shape=jax.ShapeDtypeStruct((m, n), out_dtype),
        compiler_params=_params(dimension_semantics=("arbitrary", "arbitrary")),
    )(a, b, *deps)


Z_TILE = 768
_Z_TILE_ORDER = ((0, 1, 2, 3, 4, 5, 6), (2, 0, 1, 6, 3, 4, 5), (4, 0, 5, 6, 1, 2, 3), (6, 2, 3, 4, 0, 1, 5))
_Z_EARLY_TILES = 4


def _z_proj(h, w_in_t, order, first, count, z_prev, name, dep=None):
    t = h.shape[0]

    def body(order_ref, h_ref, w_ref, *rest):
        rest[-1][...] = _dot_nt(h_ref[...], w_ref[...])

    prev = [] if z_prev is None else [z_prev]
    deps = [] if dep is None else [dep]
    return pl.pallas_call(
        body, name=name,
        grid_spec=pltpu.PrefetchScalarGridSpec(
            num_scalar_prefetch=1, grid=(count,),
            in_specs=[pl.BlockSpec((t, D_MODEL), lambda j, o: (0, 0)),
                      pl.BlockSpec((Z_TILE, D_MODEL), lambda j, o: (o[first + j], 0))]
            + [pl.BlockSpec(memory_space=pl.ANY)] * len(prev)
            + [pl.BlockSpec((8, 128), lambda j, o: (0, 0))] * len(deps),
            out_specs=pl.BlockSpec((t, Z_TILE), lambda j, o: (0, o[first + j]))),
        out_shape=jax.ShapeDtypeStruct((t, D_IN), F32),
        input_output_aliases={3: 0} if prev else {},
        compiler_params=_params(dimension_semantics=("arbitrary",)),
    )(order, h, w_in_t, *prev, *deps)


def _modulation(c_all, w_ada, b_ada_mine):
    def body(c_ref, w_ref, b_ref, act_ref, mod_ref):
        cv = c_ref[...]
        act = cv * _sigmoid(cv)
        act_ref[...] = act
        mod_ref[...] = jnp.dot(act.astype(BF16), w_ref[...].astype(BF16), preferred_element_type=F32) + b_ref[...]

    return pl.pallas_call(
        body, name="modulation",
        out_shape=(jax.ShapeDtypeStruct(c_all.shape, F32), jax.ShapeDtypeStruct((N_DEV, W_ADA_SHARD), F32)),
        compiler_params=_params(),
    )(c_all, w_ada, b_ada_mine)


def _modulated_norm(x, norm_g, scale, shift, tm=512):
    t, d = x.shape

    def body(x_ref, g_ref, sc_ref, sh_ref, h_ref):
        xv = x_ref[...]
        r = lax.rsqrt(jnp.mean(xv * xv, axis=-1, keepdims=True) + EPS)
        h = (xv * r) * g_ref[...] * (1.0 + sc_ref[...]) + sh_ref[...]
        h_ref[...] = h.astype(BF16)

    row = pl.BlockSpec((1, d), lambda i: (0, 0))
    return pl.pallas_call(
        body, name="modulated_norm", grid=(t // tm,),
        in_specs=[pl.BlockSpec((tm, d), lambda i: (i, 0)), row, row, row],
        out_specs=pl.BlockSpec((tm, d), lambda i: (i, 0)),
        out_shape=jax.ShapeDtypeStruct((t, d), BF16),
        compiler_params=_params(dimension_semantics=("arbitrary",)),
    )(x, norm_g, scale, shift)


def _window_bias(block_index):
    s = lax.broadcasted_iota(jnp.int32, (2 * BLOCK, BLOCK), 0)
    t = lax.broadcasted_iota(jnp.int32, (2 * BLOCK, BLOCK), 1)
    valid = ((s < BLOCK) & (s > t) & (block_index > 0)) | ((s >= BLOCK) & ((s - BLOCK) <= t))
    bias = jnp.where(valid, 0.0, -jnp.inf).astype(F32)
    return jnp.concatenate([bias] * 8, axis=1)


def _heads_t(pair_blocks, g):
    top = lax.broadcasted_iota(jnp.int32, (BLOCK, BLOCK), 0) < HEAD_DIM
    zeros = jnp.zeros((HEAD_DIM, BLOCK), F32)
    tiles = []
    for blk in pair_blocks:
        tp = blk.T
        if g == 0:
            tiles += [jnp.where(top, tp, 0.0), jnp.concatenate([tp[HEAD_DIM:], zeros], axis=0)]
        else:
            tiles += [jnp.concatenate([zeros, tp[:HEAD_DIM]], axis=0), jnp.where(top, 0.0, tp)]
    return jnp.concatenate(tiles, axis=1)


def _pair_block(xt, p, g):
    r0 = HEAD_DIM * g
    even = xt[r0:r0 + HEAD_DIM, (2 * p) * BLOCK:(2 * p + 1) * BLOCK]
    odd = xt[r0:r0 + HEAD_DIM, (2 * p + 1) * BLOCK:(2 * p + 2) * BLOCK]
    return jnp.concatenate([even, odd], axis=0).T


def _softmax_t(scores_t, bias, sink):
    st = scores_t + bias
    m = jnp.maximum(jnp.max(st, axis=0, keepdims=True), sink)
    e = jnp.exp(st - m)
    es = jnp.exp(sink - m)
    inv = 1.0 / (jnp.sum(e, axis=0, keepdims=True) + es)
    return e * inv, es * inv


def _dot(a, b):
    return jnp.dot(a, b, preferred_element_type=F32)


def _dot_nt(a, b):
    return lax.dot_general(a, b, (((1,), (1,)), ((), ())), preferred_element_type=F32)


def _layer_norm_fwd(v):
    mu = jnp.mean(v, axis=-1, keepdims=True)
    xc = v - mu
    rstd = lax.rsqrt(jnp.mean(xc * xc, axis=-1, keepdims=True) + EPS)
    return xc * rstd, rstd


def _tril(transposed=False):
    t = lax.broadcasted_iota(jnp.int32, (BLOCK, BLOCK), 0)
    s = lax.broadcasted_iota(jnp.int32, (BLOCK, BLOCK), 1)
    return s >= t if transposed else t >= s


def _const_spec(shape):
    return pl.BlockSpec(shape, lambda i: (0,) * len(shape))


def _keys_values(z_ref, kvp):
    kvc = z_ref[:, SEG_KV:SEG_KV + 2 * D_KV]
    kk = jnp.concatenate([kvp[:, :D_KV], kvc[:, :D_KV]], axis=0)
    vv = jnp.concatenate([kvp[:, D_KV:], kvc[:, D_KV:]], axis=0)
    return kk, vv


MIXER_BLOCKS = 2
KEEP_VHAT, KEEP_MIXED, KEEP_RSTD, KEEP_WIDTH = 0, D_SGU, 2 * D_SGU, 2 * D_SGU + 128


class _Rows:
    def __init__(self, ref, sub):
        self.ref, self.rows = ref, slice(sub * BLOCK, (sub + 1) * BLOCK)

    def __getitem__(self, idx):
        return self.ref[self.rows, idx[1]]

    def __setitem__(self, idx, value):
        self.ref[self.rows, idx[1]] = value


def _kv_before_spec(index):
    return pl.BlockSpec((BLOCK, 2 * D_KV),
                        lambda i: (jnp.maximum(MIXER_BLOCKS * index(i) - 1, 0), SEG_KV // (2 * D_KV)))


def _pair_cols(g, p, base=0):
    return slice(base + (4 * g + p) * 128, base + (4 * g + p + 1) * 128)


def _mixer_fwd(z, sink_rows, ln_g, ln_b, sgu_w, sgu_bt):
    t = z.shape[0]

    def body(z_all, kvp_ref, sink_ref, lng_ref, lnb_ref, w_ref, bt_ref, a_all, prob_ref, sink_prob_ref, keep_all):
        kv_before = kvp_ref[...]
        for sub in range(MIXER_BLOCKS):
            z_ref, a_ref = _Rows(z_all, sub), _Rows(a_all, sub)
            one_block(z_ref, kv_before, MIXER_BLOCKS * pl.program_id(0) + sub, sink_ref, lng_ref, lnb_ref, w_ref,
                      bt_ref, a_ref, prob_ref.at[sub], sink_prob_ref.at[sub], _Rows(keep_all, sub))
            kv_before = z_ref[:, SEG_KV:SEG_KV + 2 * D_KV]

    def one_block(z_ref, kv_before, block_index, sink_ref, lng_ref, lnb_ref, w_ref, bt_ref, a_ref, prob_ref,
                  sink_prob_ref, keep_ref):
        bias = _window_bias(block_index)
        kk, vv = _keys_values(z_ref, kv_before)
        kk_b, vvt_b = kk.astype(BF16), vv.T.astype(BF16)
        for g in range(2):
            qt = _heads_t([z_ref[:, _pair_cols(g, p, SEG_Q)] * ATTN_SCALE for p in range(4)], g).astype(BF16)
            prob, sink_prob = _softmax_t(_dot(kk_b, qt), bias, sink_ref[g])
            prob_b = prob.astype(BF16)
            prob_ref[g] = prob_b
            sink_prob_ref[g] = sink_prob
            ot = _dot(vvt_b, prob_b)
            for p in range(4):
                gate = z_ref[:, _pair_cols(g, p, SEG_GA)]
                a_ref[:, _pair_cols(g, p)] = (_pair_block(ot, p, g) * (gate * _sigmoid(gate))).astype(BF16)

        vhat, rstd = _layer_norm_fwd(z_ref[:, SEG_VS:SEG_VS + D_SGU])
        keep_ref[:, KEEP_VHAT:KEEP_VHAT + D_SGU] = vhat
        keep_ref[:, KEEP_RSTD:KEEP_RSTD + 128] = jnp.broadcast_to(rstd, (BLOCK, 128))
        vn = vhat * lng_ref[...] + lnb_ref[...]
        tril = _tril()
        for g in range(SGU_GROUPS):
            cols = slice(g * 128, (g + 1) * 128)
            wm = jnp.where(tril, w_ref[g], 0.0).astype(BF16)
            mixed = _dot(wm, vn[:, cols].astype(BF16)) + bt_ref[:, g:g + 1]
            keep_ref[:, KEEP_MIXED + g * 128:KEEP_MIXED + (g + 1) * 128] = mixed
            gate = z_ref[:, SEG_GS + g * 128:SEG_GS + (g + 1) * 128]
            a_ref[:, D_ATTN + g * 128:D_ATTN + (g + 1) * 128] = (
                (z_ref[:, SEG_U + g * 128:SEG_U + (g + 1) * 128] * mixed) * (gate * _sigmoid(gate))).astype(BF16)

    rows = MIXER_BLOCKS * BLOCK
    return pl.pallas_call(
        body, name="mixer_fwd", grid=(t // rows,),
        in_specs=[pl.BlockSpec((rows, D_IN), lambda i: (i, 0)), _kv_before_spec(lambda i: i),
                  _const_spec((2, 1, 8 * BLOCK)), _const_spec((1, D_SGU)), _const_spec((1, D_SGU)),
                  _const_spec((SGU_GROUPS, BLOCK, BLOCK)), _const_spec((BLOCK, SGU_GROUPS))],
        out_specs=(pl.BlockSpec((rows, D_MODEL), lambda i: (i, 0)),
                   pl.BlockSpec((MIXER_BLOCKS, 2, 2 * BLOCK, 8 * BLOCK), lambda i: (i, 0, 0, 0)),
                   pl.BlockSpec((MIXER_BLOCKS, 2, 1, 8 * BLOCK), lambda i: (i, 0, 0, 0)),
                   pl.BlockSpec((rows, KEEP_WIDTH), lambda i: (i, 0))),
        out_shape=(jax.ShapeDtypeStruct((t, D_MODEL), BF16),
                   jax.ShapeDtypeStruct((t // BLOCK, 2, 2 * BLOCK, 8 * BLOCK), BF16),
                   jax.ShapeDtypeStruct((t // BLOCK, 2, 1, 8 * BLOCK), F32),
                   jax.ShapeDtypeStruct((t, KEEP_WIDTH), F32)),
        compiler_params=_params(dimension_semantics=("arbitrary",)),
    )(z, z, sink_rows, ln_g, ln_b, sgu_w, sgu_bt)


def _mixer_bwd(z, da, probs, sink_probs, keep, ln_g, ln_b, sgu_wt):
    t = z.shape[0]

    def body(z_all, kvp_ref, da_all, prob_ref, sink_prob_ref, keep_all, lng_ref, lnb_ref, wt_ref,
             dz_all, dsink_ref, dw_ref, db_ref, dlng_ref, dlnb_ref, carry_ref, dsink_acc, dbt_acc):
        step = pl.program_id(0)

        @pl.when(step == 0)
        def _():
            carry_ref[...] = jnp.zeros_like(carry_ref)
            dsink_acc[...] = jnp.zeros_like(dsink_acc)
            dbt_acc[...] = jnp.zeros_like(dbt_acc)
            dw_ref[...] = jnp.zeros_like(dw_ref)
            dlng_ref[...] = jnp.zeros_like(dlng_ref)
            dlnb_ref[...] = jnp.zeros_like(dlnb_ref)

        carry = carry_ref[...]
        for sub in reversed(range(MIXER_BLOCKS)):
            kv_before = kvp_ref[...] if sub == 0 else _Rows(z_all, sub - 1)[:, SEG_KV:SEG_KV + 2 * D_KV]
            carry = one_block(_Rows(z_all, sub), kv_before, _Rows(da_all, sub), prob_ref.at[sub], sink_prob_ref.at[sub],
                              _Rows(keep_all, sub), carry, lng_ref, lnb_ref, wt_ref, _Rows(dz_all, sub),
                              dw_ref, dlng_ref, dlnb_ref, dsink_acc, dbt_acc)
        carry_ref[...] = carry

        @pl.when(step == ns - 1)
        def _():
            db_ref[...] = dbt_acc[...].T[:SGU_GROUPS]
            lane_row = lax.broadcasted_iota(jnp.int32, (1, 128), 1)
            d_sink = jnp.zeros((1, 128), F32)
            for g in range(2):
                acc = dsink_acc[g]
                for j in range(8):
                    head_sum = jnp.sum(acc[:, j * BLOCK:(j + 1) * BLOCK], axis=-1, keepdims=True)
                    d_sink = d_sink + jnp.where(lane_row == 8 * g + j, head_sum, 0.0)
            dsink_ref[...] = d_sink

    def one_block(z_ref, kv_before, da_ref, prob_ref, sink_prob_ref, keep_ref, carry, lng_ref, lnb_ref, wt_ref,
                  dz_ref, dw_ref, dlng_ref, dlnb_ref, dsink_acc, dbt_acc):
        kk, vv = _keys_values(z_ref, kv_before)
        vv_b = vv.astype(BF16)
        kkt_b, vvt_b = kk.T.astype(BF16), vv.T.astype(BF16)
        dkk = jnp.zeros((2 * BLOCK, D_KV), F32)
        dvv = jnp.zeros((2 * BLOCK, D_KV), F32)
        for g in range(2):
            qt = _heads_t([z_ref[:, _pair_cols(g, p, SEG_Q)] * ATTN_SCALE for p in range(4)], g).astype(BF16)
            prob_b, sink_prob = prob_ref[g], sink_prob_ref[g]
            prob = prob_b.astype(F32)
            ot = _dot(vvt_b, prob_b)
            gates = [z_ref[:, _pair_cols(g, p, SEG_GA)] for p in range(4)]
            sig = [_sigmoid(gt) for gt in gates]
            d_attn = [da_ref[:, _pair_cols(g, p)] for p in range(4)]
            d_ot = _heads_t([d_attn[p] * (gates[p] * sig[p]) for p in range(4)], g).astype(BF16)
            d_prob = _dot(vv_b, d_ot)
            delta = jnp.sum(prob * d_prob, axis=0, keepdims=True)
            d_scores = (prob * (d_prob - delta)).astype(BF16)
            dsink_acc[g] -= sink_prob * delta
            d_qt = _dot(kkt_b, d_scores)
            dkk = dkk + _dot_nt(d_scores, qt)
            dvv = dvv + _dot_nt(prob_b, d_ot)
            for p in range(4):
                dz_ref[:, _pair_cols(g, p, SEG_Q)] = (_pair_block(d_qt, p, g) * ATTN_SCALE).astype(BF16)
                d_silu = sig[p] * (1.0 + gates[p] * (1.0 - sig[p]))
                dz_ref[:, _pair_cols(g, p, SEG_GA)] = (d_attn[p] * _pair_block(ot, p, g) * d_silu).astype(BF16)
        d_kv = jnp.concatenate([dkk, dvv], axis=1)
        dz_ref[:, SEG_KV:SEG_KV + 2 * D_KV] = (d_kv[BLOCK:] + carry).astype(BF16)

        vhat, rstd = keep_ref[:, KEEP_VHAT:KEEP_VHAT + D_SGU], keep_ref[:, KEEP_RSTD:KEEP_RSTD + 1]
        lng = lng_ref[...]
        vn = vhat * lng + lnb_ref[...]
        tril, triu = _tril(), _tril(transposed=True)
        lane = lax.broadcasted_iota(jnp.int32, (BLOCK, 128), 1)
        d_bt = jnp.zeros((BLOCK, 128), F32)
        d_vn = []
        for g in range(SGU_GROUPS):
            cols = slice(g * 128, (g + 1) * 128)
            wmt = jnp.where(triu, wt_ref[g], 0.0).astype(BF16)
            vn_g = vn[:, cols].astype(BF16)
            mixed = keep_ref[:, KEEP_MIXED + g * 128:KEEP_MIXED + (g + 1) * 128]
            gate = z_ref[:, SEG_GS + g * 128:SEG_GS + (g + 1) * 128]
            u = z_ref[:, SEG_U + g * 128:SEG_U + (g + 1) * 128]
            d_out = da_ref[:, D_ATTN + g * 128:D_ATTN + (g + 1) * 128]
            sg = _sigmoid(gate)
            d_um = d_out * (gate * sg)
            dz_ref[:, SEG_U + g * 128:SEG_U + (g + 1) * 128] = (d_um * mixed).astype(BF16)
            dz_ref[:, SEG_GS + g * 128:SEG_GS + (g + 1) * 128] = (
                d_out * (u * mixed) * (sg * (1.0 + gate * (1.0 - sg)))).astype(BF16)
            d_mixed = d_um * u
            d_mixed_b = d_mixed.astype(BF16)
            dw_ref[g] += jnp.where(tril, _dot_nt(d_mixed_b, vn_g), 0.0)
            d_bt = d_bt + jnp.where(lane == g, jnp.sum(d_mixed, axis=-1, keepdims=True), 0.0)
            d_vn.append(_dot(wmt, d_mixed_b))
        dbt_acc[...] += d_bt
        d_vn = jnp.concatenate(d_vn, axis=1)
        dlng_ref[...] += jnp.sum(d_vn * vhat, axis=0, keepdims=True)
        dlnb_ref[...] += jnp.sum(d_vn, axis=0, keepdims=True)
        d_vhat = d_vn * lng
        d_v = rstd * (d_vhat - jnp.mean(d_vhat, axis=-1, keepdims=True)
                      - vhat * jnp.mean(d_vhat * vhat, axis=-1, keepdims=True))
        dz_ref[:, SEG_VS:SEG_VS + D_SGU] = d_v.astype(BF16)
        return d_kv[:BLOCK]

    rows = MIXER_BLOCKS * BLOCK
    ns = t // rows
    rev = lambda i: ns - 1 - i
    return pl.pallas_call(
        body, name="mixer_bwd", grid=(ns,),
        in_specs=[pl.BlockSpec((rows, D_IN), lambda i: (rev(i), 0)), _kv_before_spec(rev),
                  pl.BlockSpec((rows, D_MODEL), lambda i: (rev(i), 0)),
                  pl.BlockSpec((MIXER_BLOCKS, 2, 2 * BLOCK, 8 * BLOCK), lambda i: (rev(i), 0, 0, 0)),
                  pl.BlockSpec((MIXER_BLOCKS, 2, 1, 8 * BLOCK), lambda i: (rev(i), 0, 0, 0)),
                  pl.BlockSpec((rows, KEEP_WIDTH), lambda i: (rev(i), 0)),
                  _const_spec((1, D_SGU)), _const_spec((1, D_SGU)), _const_spec((SGU_GROUPS, BLOCK, BLOCK))],
        out_specs=(pl.BlockSpec((rows, D_IN), lambda i: (rev(i), 0)), _const_spec((1, 128)),
                   _const_spec((SGU_GROUPS, BLOCK, BLOCK)), _const_spec((SGU_GROUPS, BLOCK)),
                   _const_spec((1, D_SGU)), _const_spec((1, D_SGU))),
        out_shape=(jax.ShapeDtypeStruct((t, D_IN), BF16), jax.ShapeDtypeStruct((1, 128), F32),
                   jax.ShapeDtypeStruct((SGU_GROUPS, BLOCK, BLOCK), F32), jax.ShapeDtypeStruct((SGU_GROUPS, BLOCK), F32),
                   jax.ShapeDtypeStruct((1, D_SGU), F32), jax.ShapeDtypeStruct((1, D_SGU), F32)),
        scratch_shapes=[pltpu.VMEM((BLOCK, 2 * D_KV), F32), pltpu.VMEM((2, 1, 8 * BLOCK), F32),
                        pltpu.VMEM((BLOCK, 128), F32)],
        compiler_params=_params(dimension_semantics=("arbitrary",)),
    )(z, z, da, probs, sink_probs, keep, ln_g, ln_b, sgu_wt)


def _out_proj_head(a, w_out_full, x, target, gate, final_g, tm=256):
    t, d = x.shape

    def body(a_ref, w_ref, x_ref, tg_ref, gate_ref, fg_ref, dx2_ref, dy_ref, loss_ref, dfg_ref, dgate_ref):
        @pl.when(pl.program_id(0) == 0)
        def _():
            loss_ref[...] = jnp.zeros_like(loss_ref)
            dfg_ref[...] = jnp.zeros_like(dfg_ref)
            dgate_ref[...] = jnp.zeros_like(dgate_ref)

        yv, gate, fg = _dot(a_ref[...], w_ref[...]), gate_ref[...], fg_ref[...]
        x2 = x_ref[...] + gate * yv
        r2 = lax.rsqrt(jnp.mean(x2 * x2, axis=-1, keepdims=True) + EPS)
        nrm = x2 * r2
        err = nrm * fg - tg_ref[...]
        loss_ref[...] += 0.5 * jnp.sum(jnp.mean(err * err, axis=-1, keepdims=True), axis=0, keepdims=True)
        d_out = err * (1.0 / d)
        dfg_ref[...] += jnp.sum(d_out * nrm, axis=0, keepdims=True)
        d_nrm = d_out * fg
        dx2 = r2 * (d_nrm - nrm * jnp.mean(d_nrm * nrm, axis=-1, keepdims=True))
        dx2_ref[...] = dx2
        dgate_ref[...] += jnp.sum(dx2 * yv, axis=0, keepdims=True)
        dy_ref[...] = (dx2 * gate).astype(BF16)

    blk = pl.BlockSpec((tm, d), lambda i: (i, 0))
    row = _const_spec((1, d))
    whole = pl.BlockSpec(w_out_full.shape, lambda i: (0, 0), pipeline_mode=pl.Buffered(1))
    return pl.pallas_call(
        body, name="out_proj_head", grid=(t // tm,),
        in_specs=[pl.BlockSpec((tm, a.shape[1]), lambda i: (i, 0)), whole, blk, blk, row, row],
        out_specs=(blk, blk, _const_spec((1, 128)), row, row),
        out_shape=(jax.ShapeDtypeStruct((t, d), F32), jax.ShapeDtypeStruct((t, d), BF16),
                   jax.ShapeDtypeStruct((1, 128), F32), jax.ShapeDtypeStruct((1, d), F32),
                   jax.ShapeDtypeStruct((1, d), F32)),
        compiler_params=_params(dimension_semantics=("arbitrary",)),
    )(a, w_out_full, x, target, gate, final_g)


def _z_proj_bwd_norm(dz, w_in_t, x, dx2, norm_g, scale, dep, tm=256):
    t, d = x.shape

    def body(dz_ref, w_ref, x_ref, dx2_ref, g_ref, sc_ref, dep_ref, gx_ref, dshift_ref, dscale_ref, dg_ref):
        @pl.when(pl.program_id(0) == 0)
        def _():
            dshift_ref[...] = jnp.zeros_like(dshift_ref)
            dscale_ref[...] = jnp.zeros_like(dscale_ref)
            dg_ref[...] = jnp.zeros_like(dg_ref)

        dh, xv, g = _dot(dz_ref[...], w_ref[...]), x_ref[...], g_ref[...]
        one_plus = 1.0 + sc_ref[...]
        r = lax.rsqrt(jnp.mean(xv * xv, axis=-1, keepdims=True) + EPS)
        xn = xv * r
        dshift_ref[...] += jnp.sum(dh, axis=0, keepdims=True)
        dscale_ref[...] += jnp.sum(dh * (xn * g), axis=0, keepdims=True)
        d_y = dh * one_plus
        dg_ref[...] += jnp.sum(d_y * xn, axis=0, keepdims=True)
        d_xn = d_y * g
        gx_ref[...] = dx2_ref[...] + r * (d_xn - xn * jnp.mean(d_xn * xn, axis=-1, keepdims=True))

    blk = pl.BlockSpec((tm, d), lambda i: (i, 0))
    row = _const_spec((1, d))
    whole = pl.BlockSpec(w_in_t.shape, lambda i: (0, 0), pipeline_mode=pl.Buffered(1))
    return pl.pallas_call(
        body, name="z_proj_bwd_norm", grid=(t // tm,),
        in_specs=[pl.BlockSpec((tm, dz.shape[1]), lambda i: (i, 0)), whole, blk, blk, row, row, _const_spec((8, 128))],
        out_specs=(blk, row, row, row),
        out_shape=(jax.ShapeDtypeStruct((t, d), F32),) + (jax.ShapeDtypeStruct((1, d), F32),) * 3,
        compiler_params=_params(dimension_semantics=("arbitrary",)),
    )(dz, w_in_t, x, dx2, norm_g, scale, dep)


def _adamw(w, g, m, v):
    m = ADAM_B1 * m + (1.0 - ADAM_B1) * g
    v = ADAM_B2 * v + (1.0 - ADAM_B2) * (g * g)
    m_hat = m / (1.0 - ADAM_B1 ** ADAM_STEP)
    v_hat = v / (1.0 - ADAM_B2 ** ADAM_STEP)
    delta = -ADAM_LR * (m_hat / (jnp.sqrt(v_hat) + ADAM_EPS) + ADAM_WD * w)
    return delta, m, v


def _relay_sum(second_chip, pair, land, tr):
    _, r, c = pair.shape

    def body(chip_ref, a_ref, b_ref, o_ref):
        o_ref[...] = (a_ref[...].astype(F32) + b_ref[...].astype(F32)).astype(BF16)

    return pl.pallas_call(
        body, name="w_in_grad_relay_sum",
        grid_spec=pltpu.PrefetchScalarGridSpec(
            num_scalar_prefetch=1, grid=(r // tr,),
            in_specs=[pl.BlockSpec((None, tr, c), lambda i, chip_ref: (chip_ref[0], i, 0)),
                      pl.BlockSpec((None, tr, c), lambda i, chip_ref: (1, i, 0))],
            out_specs=pl.BlockSpec((tr, c), lambda i, chip_ref: (i, 0))),
        out_shape=jax.ShapeDtypeStruct((r, c), BF16),
        compiler_params=_params(dimension_semantics=("arbitrary",)),
    )(second_chip, pair, land)


def _adam_from_chips(chip, pair, landed, w, m, v, name, tc):
    _, r, c = pair.shape
    n = len(landed)

    def body(chip_ref, own_ref, *refs):
        w_ref, m_ref, v_ref, g_ref, d_ref, nm_ref, nv_ref = refs[n:]
        g = own_ref[...].astype(F32)
        for k in range(n):
            g = g + refs[k][...].astype(F32)
        g_ref[...] = g
        d_ref[...], nm_ref[...], nv_ref[...] = _adamw(w_ref[...], g, m_ref[...], v_ref[...])

    def landed_spec(index):
        return pl.BlockSpec((None, r, tc), lambda i, chip_ref: (index, 0, i))

    blk = pl.BlockSpec((r, tc), lambda i, chip_ref: (0, i))
    return pl.pallas_call(
        body, name=name,
        grid_spec=pltpu.PrefetchScalarGridSpec(
            num_scalar_prefetch=1, grid=(c // tc,),
            in_specs=[pl.BlockSpec((None, r, tc), lambda i, chip_ref: (chip_ref[0], 0, i))]
            + [landed_spec(index) for _, index in landed] + [blk, blk, blk],
            out_specs=(blk,) * 4),
        out_shape=(jax.ShapeDtypeStruct((r, c), F32),) * 4,
        compiler_params=_params(dimension_semantics=("arbitrary",)),
    )(chip, pair, *[array for array, _ in landed], w, m, v)


def _adam_w_ada(act_t, dmod_mine, w, m, v, tr=512):
    r, c = w.shape

    def body(a_ref, dm_ref, w_ref, m_ref, v_ref, g_ref, d_ref, nm_ref, nv_ref):
        g = _dot(a_ref[...].astype(BF16), dm_ref[...].astype(BF16))
        g_ref[...] = g
        d_ref[...], nm_ref[...], nv_ref[...] = _adamw(w_ref[...], g, m_ref[...], v_ref[...])

    blk = pl.BlockSpec((tr, c), lambda i: (i, 0))
    return pl.pallas_call(
        body, name="adam_w_ada", grid=(r // tr,),
        in_specs=[pl.BlockSpec((tr, N_DEV), lambda i: (i, 0)), _const_spec((N_DEV, c)), blk, blk, blk],
        out_specs=(blk,) * 4, out_shape=(jax.ShapeDtypeStruct((r, c), F32),) * 4,
        compiler_params=_params(dimension_semantics=("arbitrary",)),
    )(act_t, dmod_mine, w, m, v)


def _pack_small(d_shift, d_scale, d_gate, d_norm_g, d_final_g, d_ln_g, d_ln_b, loss, d_sinks, d_sgu_b):
    def body(shift_ref, scale_ref, gate_ref, ng_ref, fg_ref, lng_ref, lnb_ref, loss_ref, sink_ref, b_ref, o_ref):
        o_ref[...] = jnp.zeros_like(o_ref)
        o_ref[ROW_SHIFT:ROW_SHIFT + 1, :] = shift_ref[...]
        o_ref[ROW_SCALE:ROW_SCALE + 1, :] = scale_ref[...]
        o_ref[ROW_GATE:ROW_GATE + 1, :] = gate_ref[...]
        o_ref[ROW_NORM_G:ROW_NORM_G + 1, :] = ng_ref[...]
        o_ref[ROW_FINAL_G:ROW_FINAL_G + 1, :] = fg_ref[...]
        o_ref[ROW_LN:ROW_LN + 1, 0:D_SGU] = lng_ref[...]
        o_ref[ROW_LN:ROW_LN + 1, D_SGU:2 * D_SGU] = lnb_ref[...]
        o_ref[ROW_MISC:ROW_MISC + 1, 0:128] = loss_ref[...]
        o_ref[ROW_MISC:ROW_MISC + 1, 128:256] = sink_ref[...]
        o_ref[ROW_SGU_B:ROW_SGU_B + SGU_GROUPS, 0:BLOCK] = b_ref[...]

    return pl.pallas_call(
        body, name="pack_small", out_shape=jax.ShapeDtypeStruct((SMALL_ROWS, D_MODEL), F32),
        compiler_params=_params(),
    )(d_shift, d_scale, d_gate, d_norm_g, d_final_g, d_ln_g, d_ln_b, loss, d_sinks, d_sgu_b)


_SMALL_NAMES = ("norm_g", "b_ada", "attn_sinks", "sgu_ln_g", "sgu_ln_b", "sgu_w", "sgu_b", "final_g")


def _adam_small(partials, d_sgu_w_all, weights, moments_m, moments_v):
    names = _SMALL_NAMES
    k = len(names)

    def body(*refs):
        p_ref, sw_ref = refs[0], refs[1]
        w_refs, m_refs, v_refs = refs[2:2 + k], refs[2 + k:2 + 2 * k], refs[2 + 2 * k:2 + 3 * k]
        loss_ref, dmod_ref = refs[2 + 3 * k], refs[3 + 3 * k]
        out_refs = refs[4 + 3 * k:4 + 7 * k]
        sum_ref = refs[4 + 7 * k]
        total = p_ref[0]
        for j in range(1, N_DEV):
            total = total + p_ref[j]
        sum_ref[...] = total
        for j in range(N_DEV):
            for part, row in enumerate((ROW_SHIFT, ROW_SCALE, ROW_GATE)):
                dmod_ref[j:j + 1, part * D_MODEL:(part + 1) * D_MODEL] = p_ref[j, row:row + 1, :]
        loss_ref[...] = sum_ref[ROW_MISC:ROW_MISC + 1, 0:1]
        d_sgu_w = sw_ref[0]
        for j in range(1, N_DEV):
            d_sgu_w = d_sgu_w + sw_ref[j]
        grads = {
            "norm_g": sum_ref[ROW_NORM_G:ROW_NORM_G + 1, :],
            "b_ada": jnp.concatenate([sum_ref[r:r + 1, :] for r in (ROW_SHIFT, ROW_SCALE, ROW_GATE)], axis=1),
            "attn_sinks": sum_ref[ROW_MISC:ROW_MISC + 1, 128:128 + N_Q_HEADS],
            "sgu_ln_g": sum_ref[ROW_LN:ROW_LN + 1, 0:D_SGU],
            "sgu_ln_b": sum_ref[ROW_LN:ROW_LN + 1, D_SGU:2 * D_SGU],
            "sgu_w": d_sgu_w[None],
            "sgu_b": sum_ref[ROW_SGU_B:ROW_SGU_B + SGU_GROUPS, 0:BLOCK][None],
            "final_g": sum_ref[ROW_FINAL_G:ROW_FINAL_G + 1, :],
        }
        for i, name in enumerate(names):
            g = grads[name]
            delta, m, v = _adamw(w_refs[i][...], g, m_refs[i][...], v_refs[i][...])
            out_refs[4 * i][...] = g
            out_refs[4 * i + 1][...] = delta
            out_refs[4 * i + 2][...] = m
            out_refs[4 * i + 3][...] = v

    shapes = [jax.ShapeDtypeStruct((1, 1), F32), jax.ShapeDtypeStruct((N_DEV, 3 * D_MODEL), F32)]
    for name in names:
        shapes += [jax.ShapeDtypeStruct(weights[name].shape, F32)] * 4
    outs = pl.pallas_call(
        body, name="adam_small", out_shape=tuple(shapes),
        scratch_shapes=[pltpu.VMEM((SMALL_ROWS, D_MODEL), F32)],
        compiler_params=_params(),
    )(partials, d_sgu_w_all, *[weights[n] for n in names], *[moments_m[n] for n in names],
      *[moments_v[n] for n in names])
    return outs[0], outs[1], {name: outs[2 + 4 * i:6 + 4 * i] for i, name in enumerate(names)}


def kernel(x, c, norm_g, w_ada, b_ada, w_in, attn_sinks, sgu_ln_g, sgu_ln_b, sgu_w, sgu_b, w_out, final_g, loss_target, m_norm_g, m_w_ada, m_b_ada, m_w_in, m_attn_sinks, m_sgu_ln_g, m_sgu_ln_b, m_sgu_w, m_sgu_b, m_w_out, m_final_g, v_norm_g, v_w_ada, v_b_ada, v_w_in, v_attn_sinks, v_sgu_ln_g, v_sgu_ln_b, v_sgu_w, v_sgu_b, v_w_out, v_final_g):
    xi, yi, ci = _place()
    me = 4 * xi + 2 * yi + ci
    x2d, target = x[0], loss_target[0]
    t = x2d.shape[0]

    core = ci.astype(jnp.int32).reshape(1)
    chip = (2 * xi + yi).astype(jnp.int32).reshape(1)

    first = _own_block_copies(_first_targets)
    first_flight = _start_copies([_with_own_slot(w_in[0].T.astype(BF16), me)], first, 2, core, "gather_w_in_start")

    c_all = _all_gather_small(c.reshape(8, 256) + first_flight[3][0, 0], "gather_c").reshape(N_DEV, D_MODEL)
    b_mine = lax.dynamic_slice(b_ada, (0, me * W_ADA_SHARD), (1, W_ADA_SHARD))
    c_act, mod_part = _modulation(c_all, w_ada[0], b_mine)
    mod_all = _all_gather_small(mod_part, "gather_mod")

    across = _wait_then_start(first_flight, lambda *a: first(*a)[1:], _second_axis_stage_copies, 3, mod_all,
                              "gather_w_in_second_axis_stage")
    mod = lax.dynamic_index_in_dim(mod_all, me, axis=1, keepdims=False).reshape(1, 3 * D_MODEL)
    mod = mod + across[3][0, 0]
    shift, scale, gate = mod[:, :D_MODEL], mod[:, D_MODEL:2 * D_MODEL], mod[:, 2 * D_MODEL:]
    h = _modulated_norm(x2d, norm_g, scale, shift)

    w_in_pair = _wait_copies((first_flight[0], first_flight[1], across[2], None), lambda *a: first(*a)[:1], h,
                             "gather_w_in_sibling_wait")
    tile_order = jnp.asarray(_Z_TILE_ORDER, jnp.int32)[chip[0]]
    z_own = _z_proj(h, w_in_pair[0].reshape(D_IN, D_MODEL), tile_order, 0, 1, None, "z_proj_own")
    w_out_first = _group(_own_block_copies(_my_core_and_sibling), 1, 1, 1)
    forward = _wait_then_start(
        (across[0], across[1], w_in_pair, None), lambda *a: _second_axis_stage_copies(*a)[:1],
        lambda refs, s, r: _second_axis_forward_copies(refs[:1], s, r) + w_out_first(refs, s, r), 5, z_own,
        "gather_w_in_second_axis_forward", more_bufs=[_with_own_slot(w_out[0].astype(BF16), me)])
    w_out_flight = (forward[0], forward[1], forward[2][1:], None)
    w_in_most = _wait_copies((across[0], across[1], forward[2][:1], None),
                             lambda *a: _second_axis_stage_copies(*a)[1:2], z_own, "gather_w_in_first_forward_wait")
    w_in_most = _wait_copies((forward[0], forward[1], w_in_most, None), _second_axis_forward_copies, z_own,
                             "gather_w_in_second_forward_wait")
    z_early = _z_proj(h, w_in_most[0].reshape(D_IN, D_MODEL), tile_order, 1, _Z_EARLY_TILES - 1, z_own, "z_proj_early")
    w_in_flight = _wait_then_start((across[0], across[1], w_in_most, None),
                                   lambda *a: _second_axis_stage_copies(*a)[2:], _diagonal_forward_copies, 1,
                                   z_early, "gather_w_in_last_stage")
    w_in_all = _wait_copies(w_in_flight, _diagonal_forward_copies, z_early, "gather_w_in_last_wait")[0]
    w_in_t = w_in_all.reshape(D_IN, D_MODEL)
    z = _z_proj(h, w_in_t, tile_order, _Z_EARLY_TILES, 7 - _Z_EARLY_TILES, z_early, "z_proj_late")
    w_out_flight = _wait_then_start(w_out_flight, _group(_own_block_copies(_my_core_and_sibling), 0, 1, 1),
                                    _forward_copies, 3, z, "gather_w_out_forward_stage")
    sink_rows = jnp.repeat(attn_sinks.reshape(N_Q_HEADS), BLOCK).reshape(2, 1, 8 * BLOCK)
    sgu_bt = sgu_b[0].T
    a, probs, sink_probs, sgu_keep = _mixer_fwd(z, sink_rows + w_out_flight[3][0, 0], sgu_ln_g, sgu_ln_b, sgu_w[0],
                                                sgu_bt)
    w_out_all = _wait_copies(w_out_flight, _forward_copies, a, "gather_w_out_forward_wait")[0]
    w_out_full = w_out_all.reshape(D_MODEL, D_MODEL)
    final_g_row = final_g.reshape(1, D_MODEL)
    dx2, dy, loss_part, d_final_g, d_gate = _out_proj_head(a, w_out_full, x2d, target, gate, final_g_row)

    da = _matmul(dy, w_out_full, "nt", F32, min(t, 1024), 1024, "out_proj_bwd")
    dw_out = _matmul(a, dy, "tn", BF16, 1024, 1024, "w_out_grad").reshape(4, 2, W_OUT_SHARD, D_MODEL)
    pair_out = _pair_reduce(dw_out, "w_out_grad_pair_reduce", W_OUT_SHARD // 2)
    dz, d_sinks, d_sgu_w, d_sgu_b, d_ln_g, d_ln_b = _mixer_bwd(
        z, da, probs, sink_probs, sgu_keep, sgu_ln_g, sgu_ln_b, jnp.swapaxes(sgu_w[0], 1, 2))
    sgu_w_to_all = _group(_own_block_copies(_all_others), 2, 1, 3)
    both = _start_copies(
        [pair_out, lax.empty((3, W_OUT_SHARD, D_MODEL), BF16), _with_own_slot(d_sgu_w, me)],
        lambda refs, s, r: _chip_copies(refs[:2], s, r) + sgu_w_to_all(refs, s, r), 3 + N_DEV - 1, core,
        "w_out_grad_chip_and_sgu_w_gather_start")
    out_flight, sgu_w_flight = (both[0], both[1], both[2][:2], None), (both[0], both[1], both[2][2:], None)
    dw_in_t = _matmul(dz, h, "tn", BF16, 768, D_MODEL, "w_in_grad", dep=both[3])
    dw_in_t = dw_in_t.reshape(4, 2, W_IN_SHARD, D_MODEL)
    pair_in = _pair_reduce(dw_in_t, "w_in_grad_pair_reduce", W_IN_SHARD // 3)
    hop1 = _start_copies([pair_in, lax.empty((2, W_IN_SHARD, D_MODEL), BF16)], _first_hop_copies, 2, core,
                         "w_in_grad_first_hop_start")
    grad_x, d_shift, d_scale, d_norm_g = _z_proj_bwd_norm(dz, w_in_t, x2d, dx2, norm_g, scale, hop1[3])

    partial = _pack_small(d_shift, d_scale, d_gate, d_norm_g, d_final_g, d_ln_g, d_ln_b, loss_part, d_sinks, d_sgu_b)
    small_flight = _start_copies([_with_own_slot(partial, me)], _own_block_copies(_all_others), N_DEV - 1, core,
                                 "small_grad_gather_start")
    pair_in, land_first = _wait_copies(hop1, _first_hop_copies, small_flight[3], "w_in_grad_first_hop_wait")
    second_chip = (2 * ((xi + ci) % 2) + (yi + 1 - ci) % 2).astype(jnp.int32).reshape(1)
    relay = _relay_sum(second_chip, pair_in, land_first, W_IN_SHARD // 3)
    hop2 = _start_copies([relay, lax.empty((1, W_IN_SHARD, D_MODEL), BF16)], _second_hop_copies, 1, core,
                         "w_in_grad_second_hop_start")
    pair_out, land_out = _wait_copies(out_flight, _chip_copies, hop2[3], "w_out_grad_chip_wait")
    big = {"w_out": _adam_from_chips(chip, pair_out, [(land_out, k) for k in range(3)], w_out[0], m_w_out[0],
                                     v_w_out[0], "adam_w_out", 1024)}
    partial_all = _wait_copies(small_flight, _own_block_copies(_all_others), big["w_out"][0],
                               "small_grad_gather_wait")[0]
    d_sgu_w_all = _wait_copies(sgu_w_flight, _group(_own_block_copies(_all_others), 0, 1, 3), partial_all,
                               "sgu_w_grad_gather_wait")[0]
    weights = {"norm_g": norm_g, "b_ada": b_ada, "attn_sinks": attn_sinks, "sgu_ln_g": sgu_ln_g,
               "sgu_ln_b": sgu_ln_b, "sgu_w": sgu_w, "sgu_b": sgu_b, "final_g": final_g_row}
    moments_m = {"norm_g": m_norm_g, "b_ada": m_b_ada, "attn_sinks": m_attn_sinks, "sgu_ln_g": m_sgu_ln_g,
                 "sgu_ln_b": m_sgu_ln_b, "sgu_w": m_sgu_w, "sgu_b": m_sgu_b,
                 "final_g": m_final_g.reshape(1, D_MODEL)}
    moments_v = {"norm_g": v_norm_g, "b_ada": v_b_ada, "attn_sinks": v_attn_sinks, "sgu_ln_g": v_sgu_ln_g,
                 "sgu_ln_b": v_sgu_ln_b, "sgu_w": v_sgu_w, "sgu_b": v_sgu_b,
                 "final_g": v_final_g.reshape(1, D_MODEL)}
    loss, dmod_all, small = _adam_small(partial_all, d_sgu_w_all, weights, moments_m, moments_v)
    small["final_g"] = tuple(o.reshape(D_MODEL) for o in small["final_g"])

    dmod_mine = lax.dynamic_slice(dmod_all, (0, me * W_ADA_SHARD), (N_DEV, W_ADA_SHARD))
    big["w_ada"] = _adam_w_ada(c_act.T, dmod_mine, w_ada[0], m_w_ada[0], v_w_ada[0])
    _, land_second = _wait_copies(hop2, _second_hop_copies, big["w_ada"][0], "w_in_grad_second_hop_wait")
    big["w_in"] = tuple(o.T for o in _adam_from_chips(
        chip, pair_in, [(land_first, 0), (land_second, 0)], w_in[0].T, m_w_in[0].T, v_w_in[0].T, "adam_w_in", 512))
    order = ["norm_g", "w_ada", "b_ada", "w_in", "attn_sinks", "sgu_ln_g", "sgu_ln_b", "sgu_w", "sgu_b", "w_out",
             "final_g"]
    outs = [loss.reshape(()), grad_x[None]]
    for k in range(4):
        for name in order:
            outs.append(big[name][k][None] if name in big else small[name][k])
    return tuple(outs)
```

```python
import jax
import jax.numpy as jnp
from jax import lax
from jax.experimental import pallas as pl
from jax.experimental.pallas import tpu as pltpu

F32 = jnp.float32
BF16 = jnp.bfloat16
MESH = pl.DeviceIdType.MESH

N_DEV = 8
D_MODEL = 2048
HEAD_DIM = 64
D_ATTN = 1024
N_Q_HEADS = 16
D_KV = 128
BLOCK = 128
D_SGU = 1024
SGU_GROUPS = 8
D_IN = 5376
W_IN_SHARD = D_IN // N_DEV
W_OUT_SHARD = D_MODEL // N_DEV
W_ADA_SHARD = 3 * D_MODEL // N_DEV
EPS = 1e-6
ATTN_SCALE = 0.125

ADAM_LR = 0.001
ADAM_B1 = 0.9
ADAM_B2 = 0.999
ADAM_EPS = 1e-08
ADAM_WD = 0.01
ADAM_STEP = 10

SEG_Q, SEG_KV, SEG_GA, SEG_U, SEG_VS, SEG_GS = 0, 1024, 1280, 2304, 3328, 4352

VMEM_LIMIT = 56 * 1024 * 1024

ROW_SHIFT, ROW_SCALE, ROW_GATE, ROW_NORM_G, ROW_FINAL_G, ROW_LN, ROW_MISC, ROW_SGU_B = 0, 1, 2, 3, 4, 5, 6, 8
SMALL_ROWS = 16


def _params(**kw):
    return pltpu.CompilerParams(vmem_limit_bytes=VMEM_LIMIT, **kw)


def _sigmoid(x):
    return 0.5 * (jnp.tanh(0.5 * x) + 1.0)


def _place():
    return lax.axis_index("x"), lax.axis_index("y"), lax.axis_index("c")


def _pair_reduce(blocks, name, row_chunk):
    _, _, r, cols = blocks.shape
    assert r % row_chunk == 0

    def body(in_ref, out_ref, land, own, summed, send_sems, recv_sems, own_sems, out_sems):
        x, y, c = _place()
        sends, loads, stores = [], [], []
        for m in range(4):
            cp = pltpu.make_async_remote_copy(
                src_ref=in_ref.at[m, 1 - c], dst_ref=land.at[m], send_sem=send_sems.at[m], recv_sem=recv_sems.at[m],
                device_id=(x, y, 1 - c), device_id_type=MESH)
            cp.start()
            sends.append(cp)
            ld = pltpu.make_async_copy(in_ref.at[m, c], own.at[m], own_sems.at[m])
            ld.start()
            loads.append(ld)
        for m in range(4):
            sends[m].wait_recv()
            loads[m].wait()
            for k in range(r // row_chunk):
                rows = slice(k * row_chunk, (k + 1) * row_chunk)
                summed[m, rows, :] = (own[m, rows, :].astype(F32) + land[m, rows, :].astype(F32)).astype(BF16)
            st = pltpu.make_async_copy(summed.at[m], out_ref.at[m], out_sems.at[m])
            st.start()
            stores.append(st)
        for m in range(4):
            sends[m].wait_send()
            stores[m].wait()

    spec = pl.BlockSpec(memory_space=pl.ANY)
    return pl.pallas_call(
        body, name=name, out_shape=jax.ShapeDtypeStruct((4, r, cols), BF16),
        in_specs=[spec], out_specs=spec,
        scratch_shapes=[pltpu.VMEM((4, r, cols), BF16), pltpu.VMEM((4, r, cols), BF16), pltpu.VMEM((4, r, cols), BF16),
                        pltpu.SemaphoreType.DMA((4,)), pltpu.SemaphoreType.DMA((4,)), pltpu.SemaphoreType.DMA((4,)),
                        pltpu.SemaphoreType.DMA((4,))],
        compiler_params=_params(),
    )(blocks)


_HBM = pl.BlockSpec(memory_space=pltpu.HBM)
_SEM = pl.BlockSpec(memory_space=pltpu.SEMAPHORE)
_EFFECT = pltpu.SideEffectType.DATAFLOW_SIDE_EFFECTING


def _start_copies(bufs, copies, n_copies, after, name):
    nb = len(bufs)

    def body(*refs):
        for cp in copies(refs[:nb], refs[nb + 1], refs[nb + 2]):
            cp.start()
        refs[-1][...] = jnp.zeros_like(refs[-1])

    out = pl.pallas_call(
        body, name=name,
        out_shape=(pltpu.SemaphoreType.DMA((n_copies,)), pltpu.SemaphoreType.DMA((n_copies,)),
                   *[pltpu.HBM(b.shape, b.dtype) for b in bufs], jax.ShapeDtypeStruct((8, 128), F32)),
        in_specs=(_HBM,) * nb + (pl.BlockSpec(memory_space=pl.ANY),),
        out_specs=(_SEM, _SEM) + (_HBM,) * nb + (pl.BlockSpec(memory_space=pltpu.VMEM),),
        input_output_aliases={i: 2 + i for i in range(nb)},
        compiler_params=pltpu.CompilerParams(has_side_effects=_EFFECT),
    )(*[pltpu.with_memory_space_constraint(b, pltpu.HBM) for b in bufs], after)
    return out[0], out[1], list(out[2:2 + nb]), out[-1]


def _wait_copies(flight, copies, after, name):
    send_sems, recv_sems, bufs, _ = flight
    nb = len(bufs)

    def body(*refs):
        for cp in copies(refs[:nb], refs[nb], refs[nb + 1]):
            cp.wait_send()
            cp.wait_recv()

    return pl.pallas_call(
        body, name=name,
        out_shape=tuple(pltpu.HBM(b.shape, b.dtype) for b in bufs),
        in_specs=(_HBM,) * nb + (_SEM, _SEM, pl.BlockSpec(memory_space=pl.ANY)), out_specs=(_HBM,) * nb,
        input_output_aliases={i: i for i in range(nb)},
        compiler_params=pltpu.CompilerParams(has_side_effects=_EFFECT),
    )(*bufs, send_sems, recv_sems, after)


class _From:
    def __init__(self, sems, offset):
        self.sems, self.offset = sems, offset

    @property
    def at(self):
        return self

    def __getitem__(self, k):
        return self.sems.at[k + self.offset]


def _group(copies, first_buf, n_bufs, offset):
    def grouped(refs, send_sems, recv_sems):
        return copies(refs[first_buf:first_buf + n_bufs], _From(send_sems, offset), _From(recv_sems, offset))
    return grouped


def _wait_then_start(flight, waited, started, n_started, after, name, more_bufs=()):
    old_send, old_recv, bufs, _ = flight
    bufs = list(bufs) + [pltpu.with_memory_space_constraint(b, pltpu.HBM) for b in more_bufs]
    nb = len(bufs)

    def body(*refs):
        for cp in waited(refs[:nb], refs[nb], refs[nb + 1]):
            cp.wait_send()
            cp.wait_recv()
        for cp in started(refs[:nb], refs[nb + 3], refs[nb + 4]):
            cp.start()
        refs[-1][...] = jnp.zeros_like(refs[-1])

    out = pl.pallas_call(
        body, name=name,
        out_shape=(pltpu.SemaphoreType.DMA((n_started,)), pltpu.SemaphoreType.DMA((n_started,)),
                   *[pltpu.HBM(b.shape, b.dtype) for b in bufs], jax.ShapeDtypeStruct((8, 128), F32)),
        in_specs=(_HBM,) * nb + (_SEM, _SEM, pl.BlockSpec(memory_space=pl.ANY)),
        out_specs=(_SEM, _SEM) + (_HBM,) * nb + (pl.BlockSpec(memory_space=pltpu.VMEM),),
        input_output_aliases={i: 2 + i for i in range(nb)},
        compiler_params=pltpu.CompilerParams(has_side_effects=_EFFECT),
    )(*bufs, old_send, old_recv, after)
    return out[0], out[1], list(out[2:2 + nb]), out[-1]


def _chip_copies(refs, send_sems, recv_sems):
    pair_ref, land_ref = refs
    x, y, c = _place()
    chips = [(1 - x, y), (x, 1 - y), (1 - x, 1 - y)]
    return [pltpu.make_async_remote_copy(
        src_ref=pair_ref.at[2 * chip[0] + chip[1]], dst_ref=land_ref.at[k],
        send_sem=send_sems.at[k], recv_sem=recv_sems.at[k],
        device_id=(*chip, c), device_id_type=MESH) for k, chip in enumerate(chips)]


def _first_hop_copies(refs, send_sems, recv_sems):
    pair_ref, land_ref = refs
    x, y, c = _place()
    first = ((x + 1 - c) % 2, (y + c) % 2)
    blocks = [2 * first[0] + first[1], 2 * (1 - x) + (1 - y)]
    return [pltpu.make_async_remote_copy(
        src_ref=pair_ref.at[blocks[k]], dst_ref=land_ref.at[k], send_sem=send_sems.at[k], recv_sem=recv_sems.at[k],
        device_id=(*first, c), device_id_type=MESH) for k in range(2)]


def _second_hop_copies(refs, send_sems, recv_sems):
    relay_ref, land_ref = refs
    x, y, c = _place()
    second = ((x + c) % 2, (y + 1 - c) % 2)
    return [pltpu.make_async_remote_copy(
        src_ref=relay_ref, dst_ref=land_ref.at[0], send_sem=send_sems.at[0], recv_sem=recv_sems.at[0],
        device_id=(*second, c), device_id_type=MESH)]


def _own_block_copies(targets):
    def copies(refs, send_sems, recv_sems):
        x, y, c = _place()
        mine = refs[0].at[4 * x + 2 * y + c]
        return [pltpu.make_async_remote_copy(
            src_ref=mine, dst_ref=mine, send_sem=send_sems.at[k], recv_sem=recv_sems.at[k],
            device_id=to, device_id_type=MESH) for k, to in enumerate(targets(x, y, c))]
    return copies


def _my_core_and_sibling(x, y, c):
    return [(x, y, 1 - c), (1 - x, y, c), (x, 1 - y, c), (1 - x, 1 - y, c)]


def _all_others(x, y, c):
    flip = lambda v, f: 1 - v if f else v
    return [(flip(x, r & 4), flip(y, r & 2), flip(c, r & 1)) for r in range(1, N_DEV)]


def _forward_copies(refs, send_sems, recv_sems):
    x, y, c = _place()
    chips = [(1 - x, y), (x, 1 - y), (1 - x, 1 - y)]
    return [pltpu.make_async_remote_copy(
        src_ref=refs[0].at[4 * chip[0] + 2 * chip[1] + c], dst_ref=refs[0].at[4 * chip[0] + 2 * chip[1] + c],
        send_sem=send_sems.at[k], recv_sem=recv_sems.at[k],
        device_id=(x, y, 1 - c), device_id_type=MESH) for k, chip in enumerate(chips)]


def _first_axis_chip(x, y, c):
    return (x + 1 - c) % 2, (y + c) % 2


def _second_axis_chip(x, y, c):
    return (x + c) % 2, (y + 1 - c) % 2


def _first_targets(x, y, c):
    return [(x, y, 1 - c), (*_first_axis_chip(x, y, c), c)]


def _all_gather_small(shard, name):
    def body(in_ref, out_ref, send_sems, recv_sems, local_sem):
        x, y, c = _place()
        me, sibling = 4 * x + 2 * y + c, (x, y, 1 - c)
        first, second = _first_axis_chip(x, y, c), _second_axis_chip(x, y, c)

        def pair(chip):
            return out_ref.at[pl.ds(2 * (2 * chip[0] + chip[1]), 2)]

        def exchange(k, src, dst, to):
            cp = pltpu.make_async_remote_copy(src_ref=src, dst_ref=dst, send_sem=send_sems.at[k],
                                              recv_sem=recv_sems.at[k], device_id=to, device_id_type=MESH)
            cp.start()
            cp.wait()

        own = pltpu.make_async_copy(in_ref, out_ref.at[me], local_sem)
        own.start()
        exchange(0, in_ref, out_ref.at[me], sibling)
        own.wait()
        exchange(1, pair((x, y)), pair((x, y)), (*second, c))
        exchange(2, pair(second), pair(second), sibling)
        exchange(3, pair(first), pair(first), (*second, c))

    spec = pl.BlockSpec(memory_space=pltpu.VMEM)
    return pl.pallas_call(
        body, name=name, out_shape=jax.ShapeDtypeStruct((N_DEV,) + shard.shape, shard.dtype),
        in_specs=[spec], out_specs=spec,
        scratch_shapes=[pltpu.SemaphoreType.DMA((4,)), pltpu.SemaphoreType.DMA((4,)), pltpu.SemaphoreType.DMA],
        compiler_params=_params(),
    )(shard)


def _slot_copies(refs, send_sems, recv_sems, plan):
    copies = []
    for k, ((px, py, pc), to) in enumerate(plan):
        blk = refs[0].at[4 * px + 2 * py + pc]
        copies.append(pltpu.make_async_remote_copy(
            src_ref=blk, dst_ref=blk, send_sem=send_sems.at[k], recv_sem=recv_sems.at[k],
            device_id=to, device_id_type=MESH))
    return copies


def _second_axis_stage_copies(refs, send_sems, recv_sems):
    x, y, c = _place()
    first, second = (*_first_axis_chip(x, y, c), c), (*_second_axis_chip(x, y, c), c)
    return _slot_copies(refs, send_sems, recv_sems, [((x, y, c), second), (first, (x, y, 1 - c)), (first, second)])


def _second_axis_forward_copies(refs, send_sems, recv_sems):
    x, y, c = _place()
    return _slot_copies(refs, send_sems, recv_sems, [((*_second_axis_chip(x, y, c), c), (x, y, 1 - c))])


def _diagonal_forward_copies(refs, send_sems, recv_sems):
    x, y, c = _place()
    blk = refs[0].at[4 * (1 - x) + 2 * (1 - y) + c]
    return [pltpu.make_async_remote_copy(
        src_ref=blk, dst_ref=blk, send_sem=send_sems.at[0], recv_sem=recv_sems.at[0],
        device_id=(x, y, 1 - c), device_id_type=MESH)]


def _with_own_slot(block, me):
    return lax.dynamic_update_index_in_dim(lax.empty((N_DEV,) + block.shape, block.dtype), block, me, 0)


def _matmul(a, b, dims, out_dtype, tm, tn, name, dep=None):
    if dims == "nn":
        (m, k), n = a.shape, b.shape[1]
        a_spec = pl.BlockSpec((tm, k), lambda i, j: (i, 0))
        b_spec = pl.BlockSpec((k, tn), lambda i, j: (0, j))
        contract = ((1,), (0,))
    elif dims == "nt":
        (m, k), n = a.shape, b.shape[0]
        a_spec = pl.BlockSpec((tm, k), lambda i, j: (i, 0))
        b_spec = pl.BlockSpec((tn, k), lambda i, j: (j, 0))
        contract = ((1,), (1,))
    else:
        (k, m), n = a.shape, b.shape[1]
        a_spec = pl.BlockSpec((k, tm), lambda i, j: (0, i))
        b_spec = pl.BlockSpec((k, tn), lambda i, j: (0, j))
        contract = ((0,), (0,))
    assert m % tm == 0 and n % tn == 0 and a.dtype == BF16 and b.dtype == BF16

    def body(a_ref, b_ref, *rest):
        rest[-1][...] = lax.dot_general(a_ref[...], b_ref[...], (contract, ((), ())),
                                        preferred_element_type=F32).astype(out_dtype)

    deps = [] if dep is None else [dep]
    return pl.pallas_call(
        body, name=name, grid=(m // tm, n // tn),
        in_specs=[a_spec, b_spec] + [pl.BlockSpec((8, 128), lambda i, j: (0, 0))] * len(deps),
        out_specs=pl.BlockSpec((tm, tn), lambda i, j: (i, j)),
        out_shape=jax.ShapeDtypeStruct((m, n), out_dtype),
        compiler_params=_params(dimension_semantics=("arbitrary", "arbitrary")),
    )(a, b, *deps)


Z_TILE = 768
_Z_TILE_ORDER = ((0, 1, 2, 3, 4, 5, 6), (2, 0, 1, 6, 3, 4, 5), (4, 0, 5, 6, 1, 2, 3), (6, 2, 3, 4, 0, 1, 5))
_Z_EARLY_TILES = 4


def _z_proj(h, w_in_t, chip, first, count, z_prev, name, dep=None):
    t = h.shape[0]

    def body(chip_ref, h_ref, w_ref, *rest):
        rest[-1][...] = _dot_nt(h_ref[...], w_ref[...])

    def tile(j, chip_ref):
        picked = 0
        for c, order in enumerate(_Z_TILE_ORDER):
            for k in range(count):
                picked = picked + jnp.where((chip_ref[0] == c) & (j == k), order[first + k], 0)
        return picked

    prev = [] if z_prev is None else [z_prev]
    deps = [] if dep is None else [dep]
    return pl.pallas_call(
        body, name=name,
        grid_spec=pltpu.PrefetchScalarGridSpec(
            num_scalar_prefetch=1, grid=(count,),
            in_specs=[pl.BlockSpec((t, D_MODEL), lambda j, o: (0, 0)),
                      pl.BlockSpec((Z_TILE, D_MODEL), lambda j, o: (tile(j, o), 0))]
            + [pl.BlockSpec(memory_space=pl.ANY)] * len(prev)
            + [pl.BlockSpec((8, 128), lambda j, o: (0, 0))] * len(deps),
            out_specs=pl.BlockSpec((t, Z_TILE), lambda j, o: (0, tile(j, o)))),
        out_shape=jax.ShapeDtypeStruct((t, D_IN), F32),
        input_output_aliases={3: 0} if prev else {},
        compiler_params=_params(dimension_semantics=("arbitrary",)),
    )(chip, h, w_in_t, *prev, *deps)


def _modulation(device, c_all, w_ada, b_ada):
    def body(device_ref, c_ref, w_ref, b_ref, act_ref, mod_ref):
        cv = c_ref[...]
        act = cv * _sigmoid(cv)
        act_ref[...] = act
        mod_ref[...] = jnp.dot(act.astype(BF16), w_ref[...].astype(BF16), preferred_element_type=F32) + b_ref[...]

    whole = lambda a: pl.BlockSpec(a.shape, lambda i, device_ref: (0,) * a.ndim)
    return pl.pallas_call(
        body, name="modulation",
        grid_spec=pltpu.PrefetchScalarGridSpec(
            num_scalar_prefetch=1, grid=(1,),
            in_specs=[whole(c_all), whole(w_ada), pl.BlockSpec((1, W_ADA_SHARD), lambda i, device_ref: (0, device_ref[0]))],
            out_specs=(whole(c_all), pl.BlockSpec((N_DEV, W_ADA_SHARD), lambda i, device_ref: (0, 0)))),
        out_shape=(jax.ShapeDtypeStruct(c_all.shape, F32), jax.ShapeDtypeStruct((N_DEV, W_ADA_SHARD), F32)),
        compiler_params=_params(dimension_semantics=("arbitrary",)),
    )(device, c_all, w_ada, b_ada)


MOD_SHIFT, MOD_SCALE, MOD_GATE = 0, 1, 2


def _mod_spec(part, d):
    return pl.BlockSpec((1, d), lambda i: (0, part))


def _modulated_norm(x, norm_g, mod, tm=512):
    t, d = x.shape

    def body(x_ref, g_ref, sc_ref, sh_ref, h_ref):
        xv = x_ref[...]
        r = lax.rsqrt(jnp.mean(xv * xv, axis=-1, keepdims=True) + EPS)
        h = (xv * r) * g_ref[...] * (1.0 + sc_ref[...]) + sh_ref[...]
        h_ref[...] = h.astype(BF16)

    row = pl.BlockSpec((1, d), lambda i: (0, 0))
    return pl.pallas_call(
        body, name="modulated_norm", grid=(t // tm,),
        in_specs=[pl.BlockSpec((tm, d), lambda i: (i, 0)), row, _mod_spec(MOD_SCALE, d), _mod_spec(MOD_SHIFT, d)],
        out_specs=pl.BlockSpec((tm, d), lambda i: (i, 0)),
        out_shape=jax.ShapeDtypeStruct((t, d), BF16),
        compiler_params=_params(dimension_semantics=("arbitrary",)),
    )(x, norm_g, mod, mod)


def _window_bias(block_index):
    s = lax.broadcasted_iota(jnp.int32, (2 * BLOCK, BLOCK), 0)
    t = lax.broadcasted_iota(jnp.int32, (2 * BLOCK, BLOCK), 1)
    valid = ((s < BLOCK) & (s > t) & (block_index > 0)) | ((s >= BLOCK) & ((s - BLOCK) <= t))
    bias = jnp.where(valid, 0.0, -jnp.inf).astype(F32)
    return jnp.concatenate([bias] * 8, axis=1)


def _heads_t(pair_blocks, g):
    top = lax.broadcasted_iota(jnp.int32, (BLOCK, BLOCK), 0) < HEAD_DIM
    zeros = jnp.zeros((HEAD_DIM, BLOCK), F32)
    tiles = []
    for blk in pair_blocks:
        tp = blk.T
        if g == 0:
            tiles += [jnp.where(top, tp, 0.0), jnp.concatenate([tp[HEAD_DIM:], zeros], axis=0)]
        else:
            tiles += [jnp.concatenate([zeros, tp[:HEAD_DIM]], axis=0), jnp.where(top, 0.0, tp)]
    return jnp.concatenate(tiles, axis=1)


def _pair_block(xt, p, g):
    r0 = HEAD_DIM * g
    even = xt[r0:r0 + HEAD_DIM, (2 * p) * BLOCK:(2 * p + 1) * BLOCK]
    odd = xt[r0:r0 + HEAD_DIM, (2 * p + 1) * BLOCK:(2 * p + 2) * BLOCK]
    return jnp.concatenate([even, odd], axis=0).T


def _softmax_t(scores_t, bias, sink):
    st = scores_t + bias
    m = jnp.maximum(jnp.max(st, axis=0, keepdims=True), sink)
    e = jnp.exp(st - m)
    es = jnp.exp(sink - m)
    inv = 1.0 / (jnp.sum(e, axis=0, keepdims=True) + es)
    return e * inv, es * inv


def _dot(a, b):
    return jnp.dot(a, b, preferred_element_type=F32)


def _dot_nt(a, b):
    return lax.dot_general(a, b, (((1,), (1,)), ((), ())), preferred_element_type=F32)


def _layer_norm_fwd(v):
    mu = jnp.mean(v, axis=-1, keepdims=True)
    xc = v - mu
    rstd = lax.rsqrt(jnp.mean(xc * xc, axis=-1, keepdims=True) + EPS)
    return xc * rstd, rstd


def _tril(transposed=False):
    t = lax.broadcasted_iota(jnp.int32, (BLOCK, BLOCK), 0)
    s = lax.broadcasted_iota(jnp.int32, (BLOCK, BLOCK), 1)
    return s >= t if transposed else t >= s


def _const_spec(shape):
    return pl.BlockSpec(shape, lambda i: (0,) * len(shape))


def _keys_values(z_ref, kvp):
    kvc = z_ref[:, SEG_KV:SEG_KV + 2 * D_KV]
    kk = jnp.concatenate([kvp[:, :D_KV], kvc[:, :D_KV]], axis=0)
    vv = jnp.concatenate([kvp[:, D_KV:], kvc[:, D_KV:]], axis=0)
    return kk, vv


MIXER_BLOCKS = 2


class _Rows:
    def __init__(self, ref, sub):
        self.ref, self.rows = ref, slice(sub * BLOCK, (sub + 1) * BLOCK)

    def __getitem__(self, idx):
        return self.ref[self.rows, idx[1]]

    def __setitem__(self, idx, value):
        self.ref[self.rows, idx[1]] = value


def _kv_before_spec(index):
    return pl.BlockSpec((BLOCK, 2 * D_KV),
                        lambda i: (jnp.maximum(MIXER_BLOCKS * index(i) - 1, 0), SEG_KV // (2 * D_KV)))


def _pair_cols(g, p, base=0):
    return slice(base + (4 * g + p) * 128, base + (4 * g + p + 1) * 128)


def _mixer_fwd(z, sink_rows, ln_g, ln_b, sgu_w, sgu_bt):
    t = z.shape[0]

    def body(z_all, kvp_ref, sink_ref, lng_ref, lnb_ref, w_ref, bt_ref, a_all, prob_ref, sink_prob_ref):
        kv_before = kvp_ref[...]
        for sub in range(MIXER_BLOCKS):
            z_ref, a_ref = _Rows(z_all, sub), _Rows(a_all, sub)
            one_block(z_ref, kv_before, MIXER_BLOCKS * pl.program_id(0) + sub, sink_ref, lng_ref, lnb_ref, w_ref,
                      bt_ref, a_ref, prob_ref.at[sub], sink_prob_ref.at[sub])
            kv_before = z_ref[:, SEG_KV:SEG_KV + 2 * D_KV]

    def one_block(z_ref, kv_before, block_index, sink_ref, lng_ref, lnb_ref, w_ref, bt_ref, a_ref, prob_ref,
                  sink_prob_ref):
        bias = _window_bias(block_index)
        kk, vv = _keys_values(z_ref, kv_before)
        kk_b, vvt_b = kk.astype(BF16), vv.T.astype(BF16)
        for g in range(2):
            qt = _heads_t([z_ref[:, _pair_cols(g, p, SEG_Q)] * ATTN_SCALE for p in range(4)], g).astype(BF16)
            prob, sink_prob = _softmax_t(_dot(kk_b, qt), bias, sink_ref[g])
            prob_b = prob.astype(BF16)
            prob_ref[g] = prob_b
            sink_prob_ref[g] = sink_prob
            ot = _dot(vvt_b, prob_b)
            for p in range(4):
                gate = z_ref[:, _pair_cols(g, p, SEG_GA)]
                a_ref[:, _pair_cols(g, p)] = (_pair_block(ot, p, g) * (gate * _sigmoid(gate))).astype(BF16)

        vhat, _ = _layer_norm_fwd(z_ref[:, SEG_VS:SEG_VS + D_SGU])
        vn = vhat * lng_ref[...] + lnb_ref[...]
        tril = _tril()
        for g in range(SGU_GROUPS):
            cols = slice(g * 128, (g + 1) * 128)
            wm = jnp.where(tril, w_ref[g], 0.0).astype(BF16)
            mixed = _dot(wm, vn[:, cols].astype(BF16)) + bt_ref[:, g:g + 1]
            gate = z_ref[:, SEG_GS + g * 128:SEG_GS + (g + 1) * 128]
            a_ref[:, D_ATTN + g * 128:D_ATTN + (g + 1) * 128] = (
                (z_ref[:, SEG_U + g * 128:SEG_U + (g + 1) * 128] * mixed) * (gate * _sigmoid(gate))).astype(BF16)

    rows = MIXER_BLOCKS * BLOCK
    return pl.pallas_call(
        body, name="mixer_fwd", grid=(t // rows,),
        in_specs=[pl.BlockSpec((rows, D_IN), lambda i: (i, 0)), _kv_before_spec(lambda i: i),
                  _const_spec((2, 1, 8 * BLOCK)), _const_spec((1, D_SGU)), _const_spec((1, D_SGU)),
                  _const_spec((SGU_GROUPS, BLOCK, BLOCK)), _const_spec((BLOCK, SGU_GROUPS))],
        out_specs=(pl.BlockSpec((rows, D_MODEL), lambda i: (i, 0)),
                   pl.BlockSpec((MIXER_BLOCKS, 2, 2 * BLOCK, 8 * BLOCK), lambda i: (i, 0, 0, 0)),
                   pl.BlockSpec((MIXER_BLOCKS, 2, 1, 8 * BLOCK), lambda i: (i, 0, 0, 0))),
        out_shape=(jax.ShapeDtypeStruct((t, D_MODEL), BF16),
                   jax.ShapeDtypeStruct((t // BLOCK, 2, 2 * BLOCK, 8 * BLOCK), BF16),
                   jax.ShapeDtypeStruct((t // BLOCK, 2, 1, 8 * BLOCK), F32)),
        compiler_params=_params(dimension_semantics=("arbitrary",)),
    )(z, z, sink_rows, ln_g, ln_b, sgu_w, sgu_bt)


def _mixer_bwd(z, da, probs, sink_probs, ln_g, ln_b, sgu_w, sgu_wt, sgu_bt):
    t = z.shape[0]

    def body(z_all, kvp_ref, da_all, prob_ref, sink_prob_ref, lng_ref, lnb_ref, w_ref, wt_ref, bt_ref,
             dz_all, dsink_ref, dw_ref, db_ref, dlng_ref, dlnb_ref, carry_ref, dsink_acc, dbt_acc):
        step = pl.program_id(0)

        @pl.when(step == 0)
        def _():
            carry_ref[...] = jnp.zeros_like(carry_ref)
            dsink_acc[...] = jnp.zeros_like(dsink_acc)
            dbt_acc[...] = jnp.zeros_like(dbt_acc)
            dw_ref[...] = jnp.zeros_like(dw_ref)
            dlng_ref[...] = jnp.zeros_like(dlng_ref)
            dlnb_ref[...] = jnp.zeros_like(dlnb_ref)

        carry = carry_ref[...]
        for sub in reversed(range(MIXER_BLOCKS)):
            kv_before = kvp_ref[...] if sub == 0 else _Rows(z_all, sub - 1)[:, SEG_KV:SEG_KV + 2 * D_KV]
            carry = one_block(_Rows(z_all, sub), kv_before, _Rows(da_all, sub), prob_ref.at[sub], sink_prob_ref.at[sub],
                              carry, lng_ref, lnb_ref, w_ref, wt_ref, bt_ref, _Rows(dz_all, sub),
                              dw_ref, dlng_ref, dlnb_ref, dsink_acc, dbt_acc)
        carry_ref[...] = carry

        @pl.when(step == ns - 1)
        def _():
            db_ref[...] = dbt_acc[...].T[:SGU_GROUPS]
            lane_row = lax.broadcasted_iota(jnp.int32, (1, 128), 1)
            d_sink = jnp.zeros((1, 128), F32)
            for g in range(2):
                acc = dsink_acc[g]
                for j in range(8):
                    head_sum = jnp.sum(acc[:, j * BLOCK:(j + 1) * BLOCK], axis=-1, keepdims=True)
                    d_sink = d_sink + jnp.where(lane_row == 8 * g + j, head_sum, 0.0)
            dsink_ref[...] = d_sink

    def one_block(z_ref, kv_before, da_ref, prob_ref, sink_prob_ref, carry, lng_ref, lnb_ref, w_ref, wt_ref, bt_ref,
                  dz_ref, dw_ref, dlng_ref, dlnb_ref, dsink_acc, dbt_acc):
        kk, vv = _keys_values(z_ref, kv_before)
        vv_b = vv.astype(BF16)
        kkt_b, vvt_b = kk.T.astype(BF16), vv.T.astype(BF16)
        dkk = jnp.zeros((2 * BLOCK, D_KV), F32)
        dvv = jnp.zeros((2 * BLOCK, D_KV), F32)
        for g in range(2):
            qt = _heads_t([z_ref[:, _pair_cols(g, p, SEG_Q)] * ATTN_SCALE for p in range(4)], g).astype(BF16)
            prob_b, sink_prob = prob_ref[g], sink_prob_ref[g]
            prob = prob_b.astype(F32)
            ot = _dot(vvt_b, prob_b)
            gates = [z_ref[:, _pair_cols(g, p, SEG_GA)] for p in range(4)]
            sig = [_sigmoid(gt) for gt in gates]
            d_attn = [da_ref[:, _pair_cols(g, p)] for p in range(4)]
            d_ot = _heads_t([d_attn[p] * (gates[p] * sig[p]) for p in range(4)], g).astype(BF16)
            d_prob = _dot(vv_b, d_ot)
            delta = jnp.sum(prob * d_prob, axis=0, keepdims=True)
            d_scores = (prob * (d_prob - delta)).astype(BF16)
            dsink_acc[g] -= sink_prob * delta
            d_qt = _dot(kkt_b, d_scores)
            dkk = dkk + _dot_nt(d_scores, qt)
            dvv = dvv + _dot_nt(prob_b, d_ot)
            for p in range(4):
                dz_ref[:, _pair_cols(g, p, SEG_Q)] = (_pair_block(d_qt, p, g) * ATTN_SCALE).astype(BF16)
                d_silu = sig[p] * (1.0 + gates[p] * (1.0 - sig[p]))
                dz_ref[:, _pair_cols(g, p, SEG_GA)] = (d_attn[p] * _pair_block(ot, p, g) * d_silu).astype(BF16)
        d_kv = jnp.concatenate([dkk, dvv], axis=1)
        dz_ref[:, SEG_KV:SEG_KV + 2 * D_KV] = (d_kv[BLOCK:] + carry).astype(BF16)

        vhat, rstd = _layer_norm_fwd(z_ref[:, SEG_VS:SEG_VS + D_SGU])
        lng = lng_ref[...]
        vn = vhat * lng + lnb_ref[...]
        tril, triu = _tril(), _tril(transposed=True)
        lane = lax.broadcasted_iota(jnp.int32, (BLOCK, 128), 1)
        d_bt = jnp.zeros((BLOCK, 128), F32)
        d_vn = []
        for g in range(SGU_GROUPS):
            cols = slice(g * 128, (g + 1) * 128)
            wm = jnp.where(tril, w_ref[g], 0.0).astype(BF16)
            wmt = jnp.where(triu, wt_ref[g], 0.0).astype(BF16)
            vn_g = vn[:, cols].astype(BF16)
            mixed = _dot(wm, vn_g) + bt_ref[:, g:g + 1]
            gate = z_ref[:, SEG_GS + g * 128:SEG_GS + (g + 1) * 128]
            u = z_ref[:, SEG_U + g * 128:SEG_U + (g + 1) * 128]
            d_out = da_ref[:, D_ATTN + g * 128:D_ATTN + (g + 1) * 128]
            sg = _sigmoid(gate)
            d_um = d_out * (gate * sg)
            dz_ref[:, SEG_U + g * 128:SEG_U + (g + 1) * 128] = (d_um * mixed).astype(BF16)
            dz_ref[:, SEG_GS + g * 128:SEG_GS + (g + 1) * 128] = (
                d_out * (u * mixed) * (sg * (1.0 + gate * (1.0 - sg)))).astype(BF16)
            d_mixed = d_um * u
            d_mixed_b = d_mixed.astype(BF16)
            dw_ref[g] += jnp.where(tril, _dot_nt(d_mixed_b, vn_g), 0.0)
            d_bt = d_bt + jnp.where(lane == g, jnp.sum(d_mixed, axis=-1, keepdims=True), 0.0)
            d_vn.append(_dot(wmt, d_mixed_b))
        dbt_acc[...] += d_bt
        d_vn = jnp.concatenate(d_vn, axis=1)
        dlng_ref[...] += jnp.sum(d_vn * vhat, axis=0, keepdims=True)
        dlnb_ref[...] += jnp.sum(d_vn, axis=0, keepdims=True)
        d_vhat = d_vn * lng
        d_v = rstd * (d_vhat - jnp.mean(d_vhat, axis=-1, keepdims=True)
                      - vhat * jnp.mean(d_vhat * vhat, axis=-1, keepdims=True))
        dz_ref[:, SEG_VS:SEG_VS + D_SGU] = d_v.astype(BF16)
        return d_kv[:BLOCK]

    rows = MIXER_BLOCKS * BLOCK
    ns = t // rows
    rev = lambda i: ns - 1 - i
    return pl.pallas_call(
        body, name="mixer_bwd", grid=(ns,),
        in_specs=[pl.BlockSpec((rows, D_IN), lambda i: (rev(i), 0)), _kv_before_spec(rev),
                  pl.BlockSpec((rows, D_MODEL), lambda i: (rev(i), 0)),
                  pl.BlockSpec((MIXER_BLOCKS, 2, 2 * BLOCK, 8 * BLOCK), lambda i: (rev(i), 0, 0, 0)),
                  pl.BlockSpec((MIXER_BLOCKS, 2, 1, 8 * BLOCK), lambda i: (rev(i), 0, 0, 0)),
                  _const_spec((1, D_SGU)), _const_spec((1, D_SGU)),
                  _const_spec((SGU_GROUPS, BLOCK, BLOCK)), _const_spec((SGU_GROUPS, BLOCK, BLOCK)),
                  _const_spec((BLOCK, SGU_GROUPS))],
        out_specs=(pl.BlockSpec((rows, D_IN), lambda i: (rev(i), 0)), _const_spec((1, 128)),
                   _const_spec((SGU_GROUPS, BLOCK, BLOCK)), _const_spec((SGU_GROUPS, BLOCK)),
                   _const_spec((1, D_SGU)), _const_spec((1, D_SGU))),
        out_shape=(jax.ShapeDtypeStruct((t, D_IN), BF16), jax.ShapeDtypeStruct((1, 128), F32),
                   jax.ShapeDtypeStruct((SGU_GROUPS, BLOCK, BLOCK), F32), jax.ShapeDtypeStruct((SGU_GROUPS, BLOCK), F32),
                   jax.ShapeDtypeStruct((1, D_SGU), F32), jax.ShapeDtypeStruct((1, D_SGU), F32)),
        scratch_shapes=[pltpu.VMEM((BLOCK, 2 * D_KV), F32), pltpu.VMEM((2, 1, 8 * BLOCK), F32),
                        pltpu.VMEM((BLOCK, 128), F32)],
        compiler_params=_params(dimension_semantics=("arbitrary",)),
    )(z, z, da, probs, sink_probs, ln_g, ln_b, sgu_w, sgu_wt, sgu_bt)


def _out_proj_head(a, w_out_full, x, target, mod, final_g, tm=256):
    t, d = x.shape

    def body(a_ref, w_ref, x_ref, tg_ref, gate_ref, fg_ref, dx2_ref, dy_ref, loss_ref, dfg_ref, dgate_ref):
        @pl.when(pl.program_id(0) == 0)
        def _():
            loss_ref[...] = jnp.zeros_like(loss_ref)
            dfg_ref[...] = jnp.zeros_like(dfg_ref)
            dgate_ref[...] = jnp.zeros_like(dgate_ref)

        yv, gate, fg = _dot(a_ref[...], w_ref[...]), gate_ref[...], fg_ref[...]
        x2 = x_ref[...] + gate * yv
        r2 = lax.rsqrt(jnp.mean(x2 * x2, axis=-1, keepdims=True) + EPS)
        nrm = x2 * r2
        err = nrm * fg - tg_ref[...]
        loss_ref[...] += 0.5 * jnp.sum(jnp.mean(err * err, axis=-1, keepdims=True), axis=0, keepdims=True)
        d_out = err * (1.0 / d)
        dfg_ref[...] += jnp.sum(d_out * nrm, axis=0, keepdims=True)
        d_nrm = d_out * fg
        dx2 = r2 * (d_nrm - nrm * jnp.mean(d_nrm * nrm, axis=-1, keepdims=True))
        dx2_ref[...] = dx2
        dgate_ref[...] += jnp.sum(dx2 * yv, axis=0, keepdims=True)
        dy_ref[...] = (dx2 * gate).astype(BF16)

    blk = pl.BlockSpec((tm, d), lambda i: (i, 0))
    row = _const_spec((1, d))
    whole = pl.BlockSpec(w_out_full.shape, lambda i: (0, 0), pipeline_mode=pl.Buffered(1))
    return pl.pallas_call(
        body, name="out_proj_head", grid=(t // tm,),
        in_specs=[pl.BlockSpec((tm, a.shape[1]), lambda i: (i, 0)), whole, blk, blk, _mod_spec(MOD_GATE, d), row],
        out_specs=(blk, blk, _const_spec((1, 128)), row, row),
        out_shape=(jax.ShapeDtypeStruct((t, d), F32), jax.ShapeDtypeStruct((t, d), BF16),
                   jax.ShapeDtypeStruct((1, 128), F32), jax.ShapeDtypeStruct((1, d), F32),
                   jax.ShapeDtypeStruct((1, d), F32)),
        compiler_params=_params(dimension_semantics=("arbitrary",)),
    )(a, w_out_full, x, target, mod, final_g)


def _z_proj_bwd_norm(dz, w_in_t, x, dx2, norm_g, mod, dep, tm=256):
    t, d = x.shape

    def body(dz_ref, w_ref, x_ref, dx2_ref, g_ref, sc_ref, dep_ref, gx_ref, dshift_ref, dscale_ref, dg_ref):
        @pl.when(pl.program_id(0) == 0)
        def _():
            dshift_ref[...] = jnp.zeros_like(dshift_ref)
            dscale_ref[...] = jnp.zeros_like(dscale_ref)
            dg_ref[...] = jnp.zeros_like(dg_ref)

        dh, xv, g = _dot(dz_ref[...], w_ref[...]), x_ref[...], g_ref[...]
        one_plus = 1.0 + sc_ref[...]
        r = lax.rsqrt(jnp.mean(xv * xv, axis=-1, keepdims=True) + EPS)
        xn = xv * r
        dshift_ref[...] += jnp.sum(dh, axis=0, keepdims=True)
        dscale_ref[...] += jnp.sum(dh * (xn * g), axis=0, keepdims=True)
        d_y = dh * one_plus
        dg_ref[...] += jnp.sum(d_y * xn, axis=0, keepdims=True)
        d_xn = d_y * g
        gx_ref[...] = dx2_ref[...] + r * (d_xn - xn * jnp.mean(d_xn * xn, axis=-1, keepdims=True))

    blk = pl.BlockSpec((tm, d), lambda i: (i, 0))
    row = _const_spec((1, d))
    whole = pl.BlockSpec(w_in_t.shape, lambda i: (0, 0), pipeline_mode=pl.Buffered(1))
    return pl.pallas_call(
        body, name="z_proj_bwd_norm", grid=(t // tm,),
        in_specs=[pl.BlockSpec((tm, dz.shape[1]), lambda i: (i, 0)), whole, blk, blk, row, _mod_spec(MOD_SCALE, d),
                  _const_spec((8, 128))],
        out_specs=(blk, row, row, row),
        out_shape=(jax.ShapeDtypeStruct((t, d), F32),) + (jax.ShapeDtypeStruct((1, d), F32),) * 3,
        compiler_params=_params(dimension_semantics=("arbitrary",)),
    )(dz, w_in_t, x, dx2, norm_g, mod, dep)


def _adamw(w, g, m, v):
    m = ADAM_B1 * m + (1.0 - ADAM_B1) * g
    v = ADAM_B2 * v + (1.0 - ADAM_B2) * (g * g)
    m_hat = m / (1.0 - ADAM_B1 ** ADAM_STEP)
    v_hat = v / (1.0 - ADAM_B2 ** ADAM_STEP)
    delta = -ADAM_LR * (m_hat / (jnp.sqrt(v_hat) + ADAM_EPS) + ADAM_WD * w)
    return delta, m, v


def _relay_sum(second_chip, pair, land, tr):
    _, r, c = pair.shape

    def body(chip_ref, a_ref, b_ref, o_ref):
        o_ref[...] = (a_ref[...].astype(F32) + b_ref[...].astype(F32)).astype(BF16)

    return pl.pallas_call(
        body, name="w_in_grad_relay_sum",
        grid_spec=pltpu.PrefetchScalarGridSpec(
            num_scalar_prefetch=1, grid=(r // tr,),
            in_specs=[pl.BlockSpec((None, tr, c), lambda i, chip_ref: (chip_ref[0], i, 0)),
                      pl.BlockSpec((None, tr, c), lambda i, chip_ref: (1, i, 0))],
            out_specs=pl.BlockSpec((tr, c), lambda i, chip_ref: (i, 0))),
        out_shape=jax.ShapeDtypeStruct((r, c), BF16),
        compiler_params=_params(dimension_semantics=("arbitrary",)),
    )(second_chip, pair, land)


def _adam_from_chips(chip, pair, landed, w, m, v, name, tc):
    _, r, c = pair.shape
    n = len(landed)

    def body(chip_ref, own_ref, *refs):
        w_ref, m_ref, v_ref, g_ref, d_ref, nm_ref, nv_ref = refs[n:]
        g = own_ref[...].astype(F32)
        for k in range(n):
            g = g + refs[k][...].astype(F32)
        g_ref[...] = g
        d_ref[...], nm_ref[...], nv_ref[...] = _adamw(w_ref[...], g, m_ref[...], v_ref[...])

    def landed_spec(index):
        return pl.BlockSpec((None, r, tc), lambda i, chip_ref: (index, 0, i))

    blk = pl.BlockSpec((r, tc), lambda i, chip_ref: (0, i))
    return pl.pallas_call(
        body, name=name,
        grid_spec=pltpu.PrefetchScalarGridSpec(
            num_scalar_prefetch=1, grid=(c // tc,),
            in_specs=[pl.BlockSpec((None, r, tc), lambda i, chip_ref: (chip_ref[0], 0, i))]
            + [landed_spec(index) for _, index in landed] + [blk, blk, blk],
            out_specs=(blk,) * 4),
        out_shape=(jax.ShapeDtypeStruct((r, c), F32),) * 4,
        compiler_params=_params(dimension_semantics=("arbitrary",)),
    )(chip, pair, *[array for array, _ in landed], w, m, v)


def _adam_w_ada(device, act_t, dmod_all, w, m, v, tr=512):
    r, c = w.shape

    def body(device_ref, a_ref, dm_ref, w_ref, m_ref, v_ref, g_ref, d_ref, nm_ref, nv_ref):
        g = _dot(a_ref[...].astype(BF16), dm_ref[...].astype(BF16))
        g_ref[...] = g
        d_ref[...], nm_ref[...], nv_ref[...] = _adamw(w_ref[...], g, m_ref[...], v_ref[...])

    blk = pl.BlockSpec((tr, c), lambda i, device_ref: (i, 0))
    return pl.pallas_call(
        body, name="adam_w_ada",
        grid_spec=pltpu.PrefetchScalarGridSpec(
            num_scalar_prefetch=1, grid=(r // tr,),
            in_specs=[pl.BlockSpec((tr, N_DEV), lambda i, device_ref: (i, 0)),
                      pl.BlockSpec((N_DEV, c), lambda i, device_ref: (0, device_ref[0])), blk, blk, blk],
            out_specs=(blk,) * 4),
        out_shape=(jax.ShapeDtypeStruct((r, c), F32),) * 4,
        compiler_params=_params(dimension_semantics=("arbitrary",)),
    )(device, act_t, dmod_all, w, m, v)


def _pack_small(d_shift, d_scale, d_gate, d_norm_g, d_final_g, d_ln_g, d_ln_b, loss, d_sinks, d_sgu_b):
    def body(shift_ref, scale_ref, gate_ref, ng_ref, fg_ref, lng_ref, lnb_ref, loss_ref, sink_ref, b_ref, o_ref):
        o_ref[...] = jnp.zeros_like(o_ref)
        o_ref[ROW_SHIFT:ROW_SHIFT + 1, :] = shift_ref[...]
        o_ref[ROW_SCALE:ROW_SCALE + 1, :] = scale_ref[...]
        o_ref[ROW_GATE:ROW_GATE + 1, :] = gate_ref[...]
        o_ref[ROW_NORM_G:ROW_NORM_G + 1, :] = ng_ref[...]
        o_ref[ROW_FINAL_G:ROW_FINAL_G + 1, :] = fg_ref[...]
        o_ref[ROW_LN:ROW_LN + 1, 0:D_SGU] = lng_ref[...]
        o_ref[ROW_LN:ROW_LN + 1, D_SGU:2 * D_SGU] = lnb_ref[...]
        o_ref[ROW_MISC:ROW_MISC + 1, 0:128] = loss_ref[...]
        o_ref[ROW_MISC:ROW_MISC + 1, 128:256] = sink_ref[...]
        o_ref[ROW_SGU_B:ROW_SGU_B + SGU_GROUPS, 0:BLOCK] = b_ref[...]

    return pl.pallas_call(
        body, name="pack_small", out_shape=jax.ShapeDtypeStruct((SMALL_ROWS, D_MODEL), F32),
        compiler_params=_params(),
    )(d_shift, d_scale, d_gate, d_norm_g, d_final_g, d_ln_g, d_ln_b, loss, d_sinks, d_sgu_b)


_SMALL_NAMES = ("norm_g", "b_ada", "attn_sinks", "sgu_ln_g", "sgu_ln_b", "sgu_w", "sgu_b", "final_g")


def _adam_small(partials, d_sgu_w_all, weights, moments_m, moments_v):
    names = _SMALL_NAMES
    k = len(names)

    def body(*refs):
        p_ref, sw_ref = refs[0], refs[1]
        w_refs, m_refs, v_refs = refs[2:2 + k], refs[2 + k:2 + 2 * k], refs[2 + 2 * k:2 + 3 * k]
        loss_ref, dmod_ref = refs[2 + 3 * k], refs[3 + 3 * k]
        out_refs = refs[4 + 3 * k:4 + 7 * k]
        sum_ref = refs[4 + 7 * k]
        total = p_ref[0]
        for j in range(1, N_DEV):
            total = total + p_ref[j]
        sum_ref[...] = total
        for j in range(N_DEV):
            for part, row in enumerate((ROW_SHIFT, ROW_SCALE, ROW_GATE)):
                dmod_ref[j:j + 1, part * D_MODEL:(part + 1) * D_MODEL] = p_ref[j, row:row + 1, :]
        loss_ref[...] = sum_ref[ROW_MISC:ROW_MISC + 1, 0:1]
        d_sgu_w = sw_ref[0]
        for j in range(1, N_DEV):
            d_sgu_w = d_sgu_w + sw_ref[j]
        grads = {
            "norm_g": sum_ref[ROW_NORM_G:ROW_NORM_G + 1, :],
            "b_ada": jnp.concatenate([sum_ref[r:r + 1, :] for r in (ROW_SHIFT, ROW_SCALE, ROW_GATE)], axis=1),
            "attn_sinks": sum_ref[ROW_MISC:ROW_MISC + 1, 128:128 + N_Q_HEADS],
            "sgu_ln_g": sum_ref[ROW_LN:ROW_LN + 1, 0:D_SGU],
            "sgu_ln_b": sum_ref[ROW_LN:ROW_LN + 1, D_SGU:2 * D_SGU],
            "sgu_w": d_sgu_w[None],
            "sgu_b": sum_ref[ROW_SGU_B:ROW_SGU_B + SGU_GROUPS, 0:BLOCK][None],
            "final_g": sum_ref[ROW_FINAL_G:ROW_FINAL_G + 1, :],
        }
        for i, name in enumerate(names):
            g = grads[name]
            delta, m, v = _adamw(w_refs[i][...], g, m_refs[i][...], v_refs[i][...])
            out_refs[4 * i][...] = g
            out_refs[4 * i + 1][...] = delta
            out_refs[4 * i + 2][...] = m
            out_refs[4 * i + 3][...] = v

    shapes = [jax.ShapeDtypeStruct((1, 1), F32), jax.ShapeDtypeStruct((N_DEV, 3 * D_MODEL), F32)]
    for name in names:
        shapes += [jax.ShapeDtypeStruct(weights[name].shape, F32)] * 4
    outs = pl.pallas_call(
        body, name="adam_small", out_shape=tuple(shapes),
        scratch_shapes=[pltpu.VMEM((SMALL_ROWS, D_MODEL), F32)],
        compiler_params=_params(),
    )(partials, d_sgu_w_all, *[weights[n] for n in names], *[moments_m[n] for n in names],
      *[moments_v[n] for n in names])
    return outs[0], outs[1], {name: outs[2 + 4 * i:6 + 4 * i] for i, name in enumerate(names)}


def kernel(x, c, norm_g, w_ada, b_ada, w_in, attn_sinks, sgu_ln_g, sgu_ln_b, sgu_w, sgu_b, w_out, final_g, loss_target, m_norm_g, m_w_ada, m_b_ada, m_w_in, m_attn_sinks, m_sgu_ln_g, m_sgu_ln_b, m_sgu_w, m_sgu_b, m_w_out, m_final_g, v_norm_g, v_w_ada, v_b_ada, v_w_in, v_attn_sinks, v_sgu_ln_g, v_sgu_ln_b, v_sgu_w, v_sgu_b, v_w_out, v_final_g):
    xi, yi, ci = _place()
    me = 4 * xi + 2 * yi + ci
    x2d, target = x[0], loss_target[0]
    t = x2d.shape[0]

    core = ci.astype(jnp.int32).reshape(1)
    chip = (2 * xi + yi).astype(jnp.int32).reshape(1)

    first = _own_block_copies(_first_targets)
    first_flight = _start_copies([_with_own_slot(w_in[0].T.astype(BF16), me)], first, 2, core, "gather_w_in_start")

    c_all = _all_gather_small(c.reshape(8, 256) + first_flight[3][0, 0], "gather_c").reshape(N_DEV, D_MODEL)
    device = me.astype(jnp.int32).reshape(1)
    c_act, mod_part = _modulation(device, c_all, w_ada[0], b_ada)
    mod_all = _all_gather_small(mod_part, "gather_mod")

    across = _wait_then_start(first_flight, lambda *a: first(*a)[1:], _second_axis_stage_copies, 3, mod_all,
                              "gather_w_in_second_axis_stage")
    mod = lax.dynamic_index_in_dim(mod_all, me, axis=1, keepdims=False).reshape(1, 3 * D_MODEL)
    mod = mod + across[3][0, 0]
    h = _modulated_norm(x2d, norm_g, mod)

    w_in_pair = _wait_copies((first_flight[0], first_flight[1], across[2], None), lambda *a: first(*a)[:1], h,
                             "gather_w_in_sibling_wait")
    z_own = _z_proj(h, w_in_pair[0].reshape(D_IN, D_MODEL), chip, 0, 1, None, "z_proj_own")
    w_out_first = _group(_own_block_copies(_my_core_and_sibling), 1, 1, 1)
    forward = _wait_then_start(
        (across[0], across[1], w_in_pair, None), lambda *a: _second_axis_stage_copies(*a)[:1],
        lambda refs, s, r: _second_axis_forward_copies(refs[:1], s, r) + w_out_first(refs, s, r), 5, z_own,
        "gather_w_in_second_axis_forward", more_bufs=[_with_own_slot(w_out[0].astype(BF16), me)])
    w_out_flight = (forward[0], forward[1], forward[2][1:], None)
    w_in_most = _wait_copies((across[0], across[1], forward[2][:1], None),
                             lambda *a: _second_axis_stage_copies(*a)[1:2], z_own, "gather_w_in_first_forward_wait")
    w_in_most = _wait_copies((forward[0], forward[1], w_in_most, None), _second_axis_forward_copies, z_own,
                             "gather_w_in_second_forward_wait")
    z_early = _z_proj(h, w_in_most[0].reshape(D_IN, D_MODEL), chip, 1, _Z_EARLY_TILES - 1, z_own, "z_proj_early")
    w_in_flight = _wait_then_start((across[0], across[1], w_in_most, None),
                                   lambda *a: _second_axis_stage_copies(*a)[2:], _diagonal_forward_copies, 1,
                                   z_early, "gather_w_in_last_stage")
    w_in_all = _wait_copies(w_in_flight, _diagonal_forward_copies, z_early, "gather_w_in_last_wait")[0]
    w_in_t = w_in_all.reshape(D_IN, D_MODEL)
    z = _z_proj(h, w_in_t, chip, _Z_EARLY_TILES, 7 - _Z_EARLY_TILES, z_early, "z_proj_late")
    w_out_flight = _wait_then_start(w_out_flight, _group(_own_block_copies(_my_core_and_sibling), 0, 1, 1),
                                    _forward_copies, 3, z, "gather_w_out_forward_stage")
    sink_rows = jnp.repeat(attn_sinks.reshape(N_Q_HEADS), BLOCK).reshape(2, 1, 8 * BLOCK)
    sgu_bt = sgu_b[0].T
    a, probs, sink_probs = _mixer_fwd(z, sink_rows + w_out_flight[3][0, 0], sgu_ln_g, sgu_ln_b, sgu_w[0], sgu_bt)
    w_out_all = _wait_copies(w_out_flight, _forward_copies, a, "gather_w_out_forward_wait")[0]
    w_out_full = w_out_all.reshape(D_MODEL, D_MODEL)
    final_g_row = final_g.reshape(1, D_MODEL)
    dx2, dy, loss_part, d_final_g, d_gate = _out_proj_head(a, w_out_full, x2d, target, mod, final_g_row)

    da = _matmul(dy, w_out_full, "nt", F32, min(t, 1024), 1024, "out_proj_bwd")
    dw_out = _matmul(a, dy, "tn", BF16, 1024, 1024, "w_out_grad").reshape(4, 2, W_OUT_SHARD, D_MODEL)
    pair_out = _pair_reduce(dw_out, "w_out_grad_pair_reduce", W_OUT_SHARD // 2)
    dz, d_sinks, d_sgu_w, d_sgu_b, d_ln_g, d_ln_b = _mixer_bwd(
        z, da, probs, sink_probs, sgu_ln_g, sgu_ln_b, sgu_w[0], jnp.swapaxes(sgu_w[0], 1, 2), sgu_bt)
    sgu_w_to_all = _group(_own_block_copies(_all_others), 2, 1, 3)
    both = _start_copies(
        [pair_out, lax.empty((3, W_OUT_SHARD, D_MODEL), BF16), _with_own_slot(d_sgu_w, me)],
        lambda refs, s, r: _chip_copies(refs[:2], s, r) + sgu_w_to_all(refs, s, r), 3 + N_DEV - 1, core,
        "w_out_grad_chip_and_sgu_w_gather_start")
    out_flight, sgu_w_flight = (both[0], both[1], both[2][:2], None), (both[0], both[1], both[2][2:], None)
    dw_in_t = _matmul(dz, h, "tn", BF16, 768, D_MODEL, "w_in_grad", dep=both[3])
    dw_in_t = dw_in_t.reshape(4, 2, W_IN_SHARD, D_MODEL)
    pair_in = _pair_reduce(dw_in_t, "w_in_grad_pair_reduce", W_IN_SHARD // 3)
    hop1 = _start_copies([pair_in, lax.empty((2, W_IN_SHARD, D_MODEL), BF16)], _first_hop_copies, 2, core,
                         "w_in_grad_first_hop_start")
    grad_x, d_shift, d_scale, d_norm_g = _z_proj_bwd_norm(dz, w_in_t, x2d, dx2, norm_g, mod, hop1[3])

    partial = _pack_small(d_shift, d_scale, d_gate, d_norm_g, d_final_g, d_ln_g, d_ln_b, loss_part, d_sinks, d_sgu_b)
    small_flight = _start_copies([_with_own_slot(partial, me)], _own_block_copies(_all_others), N_DEV - 1, core,
                                 "small_grad_gather_start")
    pair_in, land_first = _wait_copies(hop1, _first_hop_copies, small_flight[3], "w_in_grad_first_hop_wait")
    second_chip = (2 * ((xi + ci) % 2) + (yi + 1 - ci) % 2).astype(jnp.int32).reshape(1)
    relay = _relay_sum(second_chip, pair_in, land_first, W_IN_SHARD // 3)
    hop2 = _start_copies([relay, lax.empty((1, W_IN_SHARD, D_MODEL), BF16)], _second_hop_copies, 1, core,
                         "w_in_grad_second_hop_start")
    pair_out, land_out = _wait_copies(out_flight, _chip_copies, hop2[3], "w_out_grad_chip_wait")
    big = {"w_out": _adam_from_chips(chip, pair_out, [(land_out, k) for k in range(3)], w_out[0], m_w_out[0],
                                     v_w_out[0], "adam_w_out", 1024)}
    partial_all = _wait_copies(small_flight, _own_block_copies(_all_others), big["w_out"][0],
                               "small_grad_gather_wait")[0]
    d_sgu_w_all = _wait_copies(sgu_w_flight, _group(_own_block_copies(_all_others), 0, 1, 3), partial_all,
                               "sgu_w_grad_gather_wait")[0]
    weights = {"norm_g": norm_g, "b_ada": b_ada, "attn_sinks": attn_sinks, "sgu_ln_g": sgu_ln_g,
               "sgu_ln_b": sgu_ln_b, "sgu_w": sgu_w, "sgu_b": sgu_b, "final_g": final_g_row}
    moments_m = {"norm_g": m_norm_g, "b_ada": m_b_ada, "attn_sinks": m_attn_sinks, "sgu_ln_g": m_sgu_ln_g,
                 "sgu_ln_b": m_sgu_ln_b, "sgu_w": m_sgu_w, "sgu_b": m_sgu_b,
                 "final_g": m_final_g.reshape(1, D_MODEL)}
    moments_v = {"norm_g": v_norm_g, "b_ada": v_b_ada, "attn_sinks": v_attn_sinks, "sgu_ln_g": v_sgu_ln_g,
                 "sgu_ln_b": v_sgu_ln_b, "sgu_w": v_sgu_w, "sgu_b": v_sgu_b,
                 "final_g": v_final_g.reshape(1, D_MODEL)}
    loss, dmod_all, small = _adam_small(partial_all, d_sgu_w_all, weights, moments_m, moments_v)
    small["final_g"] = tuple(o.reshape(D_MODEL) for o in small["final_g"])

    big["w_ada"] = _adam_w_ada(device, c_act.T, dmod_all, w_ada[0], m_w_ada[0], v_w_ada[0])
    _, land_second = _wait_copies(hop2, _second_hop_copies, big["w_ada"][0], "w_in_grad_second_hop_wait")
    big["w_in"] = tuple(o.T for o in _adam_from_chips(
        chip, pair_in, [(land_first, 0), (land_second, 0)], w_in[0].T, m_w_in[0].T, v_w_in[0].T, "adam_w_in", 512))
    order = ["norm_g", "w_ada", "b_ada", "w_in", "attn_sinks", "sgu_ln_g", "sgu_ln_b", "sgu_w", "sgu_b", "w_out",
             "final_g"]
    outs = [loss.reshape(()), grad_x[None]]
    for k in range(4):
        for name in order:
            outs.append(big[name][k][None] if name in big else small[name][k])
    return tuple(outs)
```

```python
import jax
import jax.numpy as jnp
from jax import lax
from jax.experimental import pallas as pl
from jax.experimental.pallas import tpu as pltpu

F32 = jnp.float32
BF16 = jnp.bfloat16
MESH = pl.DeviceIdType.MESH

N_DEV = 8
D_MODEL = 2048
HEAD_DIM = 64
D_ATTN = 1024
N_Q_HEADS = 16
D_KV = 128
BLOCK = 128
D_SGU = 1024
SGU_GROUPS = 8
D_IN = 5376
W_IN_SHARD = D_IN // N_DEV
W_OUT_SHARD = D_MODEL // N_DEV
W_ADA_SHARD = 3 * D_MODEL // N_DEV
EPS = 1e-6
ATTN_SCALE = 0.125

ADAM_LR = 0.001
ADAM_B1 = 0.9
ADAM_B2 = 0.999
ADAM_EPS = 1e-08
ADAM_WD = 0.01
ADAM_STEP = 10

SEG_Q, SEG_KV, SEG_GA, SEG_U, SEG_VS, SEG_GS = 0, 1024, 1280, 2304, 3328, 4352

VMEM_LIMIT = 56 * 1024 * 1024

ROW_SHIFT, ROW_SCALE, ROW_GATE, ROW_NORM_G, ROW_FINAL_G, ROW_LN, ROW_MISC, ROW_SGU_B = 0, 1, 2, 3, 4, 5, 6, 8
SMALL_ROWS = 16


def _params(**kw):
    return pltpu.CompilerParams(vmem_limit_bytes=VMEM_LIMIT, **kw)


def _sigmoid(x):
    return 0.5 * (jnp.tanh(0.5 * x) + 1.0)


def _place():
    return lax.axis_index("x"), lax.axis_index("y"), lax.axis_index("c")


def _pair_reduce(blocks, name, row_chunk):
    _, _, r, cols = blocks.shape
    assert r % row_chunk == 0

    def body(in_ref, out_ref, land, own, summed, send_sems, recv_sems, own_sems, out_sems):
        x, y, c = _place()
        sends, loads, stores = [], [], []
        for m in range(4):
            cp = pltpu.make_async_remote_copy(
                src_ref=in_ref.at[m, 1 - c], dst_ref=land.at[m], send_sem=send_sems.at[m], recv_sem=recv_sems.at[m],
                device_id=(x, y, 1 - c), device_id_type=MESH)
            cp.start()
            sends.append(cp)
            ld = pltpu.make_async_copy(in_ref.at[m, c], own.at[m], own_sems.at[m])
            ld.start()
            loads.append(ld)
        for m in range(4):
            sends[m].wait_recv()
            loads[m].wait()
            for k in range(r // row_chunk):
                rows = slice(k * row_chunk, (k + 1) * row_chunk)
                summed[m, rows, :] = (own[m, rows, :].astype(F32) + land[m, rows, :].astype(F32)).astype(BF16)
            st = pltpu.make_async_copy(summed.at[m], out_ref.at[m], out_sems.at[m])
            st.start()
            stores.append(st)
        for m in range(4):
            sends[m].wait_send()
            stores[m].wait()

    spec = pl.BlockSpec(memory_space=pl.ANY)
    return pl.pallas_call(
        body, name=name, out_shape=jax.ShapeDtypeStruct((4, r, cols), BF16),
        in_specs=[spec], out_specs=spec,
        scratch_shapes=[pltpu.VMEM((4, r, cols), BF16), pltpu.VMEM((4, r, cols), BF16), pltpu.VMEM((4, r, cols), BF16),
                        pltpu.SemaphoreType.DMA((4,)), pltpu.SemaphoreType.DMA((4,)), pltpu.SemaphoreType.DMA((4,)),
                        pltpu.SemaphoreType.DMA((4,))],
        compiler_params=_params(),
    )(blocks)


_HBM = pl.BlockSpec(memory_space=pltpu.HBM)
_SEM = pl.BlockSpec(memory_space=pltpu.SEMAPHORE)
_EFFECT = pltpu.SideEffectType.DATAFLOW_SIDE_EFFECTING


def _start_copies(bufs, copies, n_copies, after, name):
    nb = len(bufs)

    def body(*refs):
        for cp in copies(refs[:nb], refs[nb + 1], refs[nb + 2]):
            cp.start()
        refs[-1][...] = jnp.zeros_like(refs[-1])

    out = pl.pallas_call(
        body, name=name,
        out_shape=(pltpu.SemaphoreType.DMA((n_copies,)), pltpu.SemaphoreType.DMA((n_copies,)),
                   *[pltpu.HBM(b.shape, b.dtype) for b in bufs], jax.ShapeDtypeStruct((8, 128), F32)),
        in_specs=(_HBM,) * nb + (pl.BlockSpec(memory_space=pl.ANY),),
        out_specs=(_SEM, _SEM) + (_HBM,) * nb + (pl.BlockSpec(memory_space=pltpu.VMEM),),
        input_output_aliases={i: 2 + i for i in range(nb)},
        compiler_params=pltpu.CompilerParams(has_side_effects=_EFFECT),
    )(*[pltpu.with_memory_space_constraint(b, pltpu.HBM) for b in bufs], after)
    return out[0], out[1], list(out[2:2 + nb]), out[-1]


def _wait_copies(flight, copies, after, name):
    send_sems, recv_sems, bufs, _ = flight
    nb = len(bufs)

    def body(*refs):
        for cp in copies(refs[:nb], refs[nb], refs[nb + 1]):
            cp.wait_send()
            cp.wait_recv()

    return pl.pallas_call(
        body, name=name,
        out_shape=tuple(pltpu.HBM(b.shape, b.dtype) for b in bufs),
        in_specs=(_HBM,) * nb + (_SEM, _SEM, pl.BlockSpec(memory_space=pl.ANY)), out_specs=(_HBM,) * nb,
        input_output_aliases={i: i for i in range(nb)},
        compiler_params=pltpu.CompilerParams(has_side_effects=_EFFECT),
    )(*bufs, send_sems, recv_sems, after)


class _From:
    def __init__(self, sems, offset):
        self.sems, self.offset = sems, offset

    @property
    def at(self):
        return self

    def __getitem__(self, k):
        return self.sems.at[k + self.offset]


def _group(copies, first_buf, n_bufs, offset):
    def grouped(refs, send_sems, recv_sems):
        return copies(refs[first_buf:first_buf + n_bufs], _From(send_sems, offset), _From(recv_sems, offset))
    return grouped


def _wait_then_start(flight, waited, started, n_started, after, name, more_bufs=()):
    old_send, old_recv, bufs, _ = flight
    bufs = list(bufs) + [pltpu.with_memory_space_constraint(b, pltpu.HBM) for b in more_bufs]
    nb = len(bufs)

    def body(*refs):
        for cp in waited(refs[:nb], refs[nb], refs[nb + 1]):
            cp.wait_send()
            cp.wait_recv()
        for cp in started(refs[:nb], refs[nb + 3], refs[nb + 4]):
            cp.start()
        refs[-1][...] = jnp.zeros_like(refs[-1])

    out = pl.pallas_call(
        body, name=name,
        out_shape=(pltpu.SemaphoreType.DMA((n_started,)), pltpu.SemaphoreType.DMA((n_started,)),
                   *[pltpu.HBM(b.shape, b.dtype) for b in bufs], jax.ShapeDtypeStruct((8, 128), F32)),
        in_specs=(_HBM,) * nb + (_SEM, _SEM, pl.BlockSpec(memory_space=pl.ANY)),
        out_specs=(_SEM, _SEM) + (_HBM,) * nb + (pl.BlockSpec(memory_space=pltpu.VMEM),),
        input_output_aliases={i: 2 + i for i in range(nb)},
        compiler_params=pltpu.CompilerParams(has_side_effects=_EFFECT),
    )(*bufs, old_send, old_recv, after)
    return out[0], out[1], list(out[2:2 + nb]), out[-1]


def _chip_copies(refs, send_sems, recv_sems):
    pair_ref, land_ref = refs
    x, y, c = _place()
    chips = [(1 - x, y), (x, 1 - y), (1 - x, 1 - y)]
    return [pltpu.make_async_remote_copy(
        src_ref=pair_ref.at[2 * chip[0] + chip[1]], dst_ref=land_ref.at[k],
        send_sem=send_sems.at[k], recv_sem=recv_sems.at[k],
        device_id=(*chip, c), device_id_type=MESH) for k, chip in enumerate(chips)]


def _first_hop_copies(refs, send_sems, recv_sems):
    pair_ref, land_ref = refs
    x, y, c = _place()
    first = ((x + 1 - c) % 2, (y + c) % 2)
    blocks = [2 * first[0] + first[1], 2 * (1 - x) + (1 - y)]
    return [pltpu.make_async_remote_copy(
        src_ref=pair_ref.at[blocks[k]], dst_ref=land_ref.at[k], send_sem=send_sems.at[k], recv_sem=recv_sems.at[k],
        device_id=(*first, c), device_id_type=MESH) for k in range(2)]


def _second_hop_copies(refs, send_sems, recv_sems):
    relay_ref, land_ref = refs
    x, y, c = _place()
    second = ((x + c) % 2, (y + 1 - c) % 2)
    return [pltpu.make_async_remote_copy(
        src_ref=relay_ref, dst_ref=land_ref.at[0], send_sem=send_sems.at[0], recv_sem=recv_sems.at[0],
        device_id=(*second, c), device_id_type=MESH)]


def _own_block_copies(targets):
    def copies(refs, send_sems, recv_sems):
        x, y, c = _place()
        mine = refs[0].at[4 * x + 2 * y + c]
        return [pltpu.make_async_remote_copy(
            src_ref=mine, dst_ref=mine, send_sem=send_sems.at[k], recv_sem=recv_sems.at[k],
            device_id=to, device_id_type=MESH) for k, to in enumerate(targets(x, y, c))]
    return copies


def _all_others(x, y, c):
    flip = lambda v, f: 1 - v if f else v
    return [(flip(x, r & 4), flip(y, r & 2), flip(c, r & 1)) for r in range(1, N_DEV)]


def _forward_copies(refs, send_sems, recv_sems):
    x, y, c = _place()
    chips = [(1 - x, y), (x, 1 - y), (1 - x, 1 - y)]
    return [pltpu.make_async_remote_copy(
        src_ref=refs[0].at[4 * chip[0] + 2 * chip[1] + c], dst_ref=refs[0].at[4 * chip[0] + 2 * chip[1] + c],
        send_sem=send_sems.at[k], recv_sem=recv_sems.at[k],
        device_id=(x, y, 1 - c), device_id_type=MESH) for k, chip in enumerate(chips)]


def _first_axis_chip(x, y, c):
    return (x + 1 - c) % 2, (y + c) % 2


def _second_axis_chip(x, y, c):
    return (x + c) % 2, (y + 1 - c) % 2


def _first_targets(x, y, c):
    return [(x, y, 1 - c), (*_first_axis_chip(x, y, c), c)]


def _all_gather_small(shard, name):
    def body(in_ref, out_ref, send_sems, recv_sems, local_sem):
        x, y, c = _place()
        me, sibling = 4 * x + 2 * y + c, (x, y, 1 - c)
        first, second = _first_axis_chip(x, y, c), _second_axis_chip(x, y, c)

        def pair(chip):
            return out_ref.at[pl.ds(2 * (2 * chip[0] + chip[1]), 2)]

        def exchange(k, src, dst, to):
            cp = pltpu.make_async_remote_copy(src_ref=src, dst_ref=dst, send_sem=send_sems.at[k],
                                              recv_sem=recv_sems.at[k], device_id=to, device_id_type=MESH)
            cp.start()
            cp.wait()

        own = pltpu.make_async_copy(in_ref, out_ref.at[me], local_sem)
        own.start()
        exchange(0, in_ref, out_ref.at[me], sibling)
        own.wait()
        exchange(1, pair((x, y)), pair((x, y)), (*second, c))
        exchange(2, pair(second), pair(second), sibling)
        exchange(3, pair(first), pair(first), (*second, c))

    spec = pl.BlockSpec(memory_space=pltpu.VMEM)
    return pl.pallas_call(
        body, name=name, out_shape=jax.ShapeDtypeStruct((N_DEV,) + shard.shape, shard.dtype),
        in_specs=[spec], out_specs=spec,
        scratch_shapes=[pltpu.SemaphoreType.DMA((4,)), pltpu.SemaphoreType.DMA((4,)), pltpu.SemaphoreType.DMA],
        compiler_params=_params(),
    )(shard)


def _slot_copies(refs, send_sems, recv_sems, plan):
    copies = []
    for k, ((px, py, pc), to) in enumerate(plan):
        blk = refs[0].at[4 * px + 2 * py + pc]
        copies.append(pltpu.make_async_remote_copy(
            src_ref=blk, dst_ref=blk, send_sem=send_sems.at[k], recv_sem=recv_sems.at[k],
            device_id=to, device_id_type=MESH))
    return copies


def _second_axis_stage_copies(refs, send_sems, recv_sems):
    x, y, c = _place()
    first, second = (*_first_axis_chip(x, y, c), c), (*_second_axis_chip(x, y, c), c)
    return _slot_copies(refs, send_sems, recv_sems, [((x, y, c), second), (first, (x, y, 1 - c)), (first, second)])


def _second_axis_forward_copies(refs, send_sems, recv_sems):
    x, y, c = _place()
    return _slot_copies(refs, send_sems, recv_sems, [((*_second_axis_chip(x, y, c), c), (x, y, 1 - c))])


def _diagonal_forward_copies(refs, send_sems, recv_sems):
    x, y, c = _place()
    blk = refs[0].at[4 * (1 - x) + 2 * (1 - y) + c]
    return [pltpu.make_async_remote_copy(
        src_ref=blk, dst_ref=blk, send_sem=send_sems.at[0], recv_sem=recv_sems.at[0],
        device_id=(x, y, 1 - c), device_id_type=MESH)]


def _with_own_slot(block, me):
    return lax.dynamic_update_index_in_dim(lax.empty((N_DEV,) + block.shape, block.dtype), block, me, 0)


def _matmul(a, b, dims, out_dtype, tm, tn, name, dep=None):
    if dims == "nn":
        (m, k), n = a.shape, b.shape[1]
        a_spec = pl.BlockSpec((tm, k), lambda i, j: (i, 0))
        b_spec = pl.BlockSpec((k, tn), lambda i, j: (0, j))
        contract = ((1,), (0,))
    elif dims == "nt":
        (m, k), n = a.shape, b.shape[0]
        a_spec = pl.BlockSpec((tm, k), lambda i, j: (i, 0))
        b_spec = pl.BlockSpec((tn, k), lambda i, j: (j, 0))
        contract = ((1,), (1,))
    else:
        (k, m), n = a.shape, b.shape[1]
        a_spec = pl.BlockSpec((k, tm), lambda i, j: (0, i))
        b_spec = pl.BlockSpec((k, tn), lambda i, j: (0, j))
        contract = ((0,), (0,))
    assert m % tm == 0 and n % tn == 0 and a.dtype == BF16 and b.dtype == BF16

    def body(a_ref, b_ref, *rest):
        rest[-1][...] = lax.dot_general(a_ref[...], b_ref[...], (contract, ((), ())),
                                        preferred_element_type=F32).astype(out_dtype)

    deps = [] if dep is None else [dep]
    return pl.pallas_call(
        body, name=name, grid=(m // tm, n // tn),
        in_specs=[a_spec, b_spec] + [pl.BlockSpec((8, 128), lambda i, j: (0, 0))] * len(deps),
        out_specs=pl.BlockSpec((tm, tn), lambda i, j: (i, j)),
        out_shape=jax.ShapeDtypeStruct((m, n), out_dtype),
        compiler_params=_params(dimension_semantics=("arbitrary", "arbitrary")),
    )(a, b, *deps)


Z_TILE = 768
_Z_TILE_ORDER = ((0, 1, 2, 3, 4, 5, 6), (2, 0, 1, 6, 3, 4, 5), (4, 0, 5, 6, 1, 2, 3), (6, 2, 3, 4, 0, 1, 5))
_Z_EARLY_TILES = 4


def _z_proj(h, w_in_t, chip, first, count, z_prev, name, dep=None):
    t = h.shape[0]

    def body(chip_ref, h_ref, w_ref, *rest):
        rest[-1][...] = _dot_nt(h_ref[...], w_ref[...])

    def tile(j, chip_ref):
        picked = 0
        for c, order in enumerate(_Z_TILE_ORDER):
            for k in range(count):
                picked = picked + jnp.where((chip_ref[0] == c) & (j == k), order[first + k], 0)
        return picked

    prev = [] if z_prev is None else [z_prev]
    deps = [] if dep is None else [dep]
    return pl.pallas_call(
        body, name=name,
        grid_spec=pltpu.PrefetchScalarGridSpec(
            num_scalar_prefetch=1, grid=(count,),
            in_specs=[pl.BlockSpec((t, D_MODEL), lambda j, o: (0, 0)),
                      pl.BlockSpec((Z_TILE, D_MODEL), lambda j, o: (tile(j, o), 0))]
            + [pl.BlockSpec(memory_space=pl.ANY)] * len(prev)
            + [pl.BlockSpec((8, 128), lambda j, o: (0, 0))] * len(deps),
            out_specs=pl.BlockSpec((t, Z_TILE), lambda j, o: (0, tile(j, o)))),
        out_shape=jax.ShapeDtypeStruct((t, D_IN), F32),
        input_output_aliases={3: 0} if prev else {},
        compiler_params=_params(dimension_semantics=("arbitrary",)),
    )(chip, h, w_in_t, *prev, *deps)


def _modulation(device, c_all, w_ada, b_ada):
    def body(device_ref, c_ref, w_ref, b_ref, act_ref, mod_ref):
        cv = c_ref[...]
        act = cv * _sigmoid(cv)
        act_ref[...] = act
        mod_ref[...] = jnp.dot(act.astype(BF16), w_ref[...].astype(BF16), preferred_element_type=F32) + b_ref[...]

    whole = lambda a: pl.BlockSpec(a.shape, lambda i, device_ref: (0,) * a.ndim)
    return pl.pallas_call(
        body, name="modulation",
        grid_spec=pltpu.PrefetchScalarGridSpec(
            num_scalar_prefetch=1, grid=(1,),
            in_specs=[whole(c_all), whole(w_ada), pl.BlockSpec((1, W_ADA_SHARD), lambda i, device_ref: (0, device_ref[0]))],
            out_specs=(whole(c_all), pl.BlockSpec((N_DEV, W_ADA_SHARD), lambda i, device_ref: (0, 0)))),
        out_shape=(jax.ShapeDtypeStruct(c_all.shape, F32), jax.ShapeDtypeStruct((N_DEV, W_ADA_SHARD), F32)),
        compiler_params=_params(dimension_semantics=("arbitrary",)),
    )(device, c_all, w_ada, b_ada)


MOD_SHIFT, MOD_SCALE, MOD_GATE = 0, 1, 2


def _mod_spec(part, d):
    return pl.BlockSpec((1, d), lambda i: (0, part))


def _modulated_norm(x, norm_g, mod, tm=512):
    t, d = x.shape

    def body(x_ref, g_ref, sc_ref, sh_ref, h_ref):
        xv = x_ref[...]
        r = lax.rsqrt(jnp.mean(xv * xv, axis=-1, keepdims=True) + EPS)
        h = (xv * r) * g_ref[...] * (1.0 + sc_ref[...]) + sh_ref[...]
        h_ref[...] = h.astype(BF16)

    row = pl.BlockSpec((1, d), lambda i: (0, 0))
    return pl.pallas_call(
        body, name="modulated_norm", grid=(t // tm,),
        in_specs=[pl.BlockSpec((tm, d), lambda i: (i, 0)), row, _mod_spec(MOD_SCALE, d), _mod_spec(MOD_SHIFT, d)],
        out_specs=pl.BlockSpec((tm, d), lambda i: (i, 0)),
        out_shape=jax.ShapeDtypeStruct((t, d), BF16),
        compiler_params=_params(dimension_semantics=("arbitrary",)),
    )(x, norm_g, mod, mod)


def _window_bias(block_index):
    s = lax.broadcasted_iota(jnp.int32, (2 * BLOCK, BLOCK), 0)
    t = lax.broadcasted_iota(jnp.int32, (2 * BLOCK, BLOCK), 1)
    valid = ((s < BLOCK) & (s > t) & (block_index > 0)) | ((s >= BLOCK) & ((s - BLOCK) <= t))
    bias = jnp.where(valid, 0.0, -jnp.inf).astype(F32)
    return jnp.concatenate([bias] * 8, axis=1)


def _heads_t(pair_blocks, g):
    top = lax.broadcasted_iota(jnp.int32, (BLOCK, BLOCK), 0) < HEAD_DIM
    zeros = jnp.zeros((HEAD_DIM, BLOCK), F32)
    tiles = []
    for blk in pair_blocks:
        tp = blk.T
        if g == 0:
            tiles += [jnp.where(top, tp, 0.0), jnp.concatenate([tp[HEAD_DIM:], zeros], axis=0)]
        else:
            tiles += [jnp.concatenate([zeros, tp[:HEAD_DIM]], axis=0), jnp.where(top, 0.0, tp)]
    return jnp.concatenate(tiles, axis=1)


def _pair_block(xt, p, g):
    r0 = HEAD_DIM * g
    even = xt[r0:r0 + HEAD_DIM, (2 * p) * BLOCK:(2 * p + 1) * BLOCK]
    odd = xt[r0:r0 + HEAD_DIM, (2 * p + 1) * BLOCK:(2 * p + 2) * BLOCK]
    return jnp.concatenate([even, odd], axis=0).T


def _softmax_t(scores_t, bias, sink):
    st = scores_t + bias
    m = jnp.maximum(jnp.max(st, axis=0, keepdims=True), sink)
    e = jnp.exp(st - m)
    es = jnp.exp(sink - m)
    inv = 1.0 / (jnp.sum(e, axis=0, keepdims=True) + es)
    return e * inv, es * inv


def _dot(a, b):
    return jnp.dot(a, b, preferred_element_type=F32)


def _dot_nt(a, b):
    return lax.dot_general(a, b, (((1,), (1,)), ((), ())), preferred_element_type=F32)


def _layer_norm_fwd(v):
    mu = jnp.mean(v, axis=-1, keepdims=True)
    xc = v - mu
    rstd = lax.rsqrt(jnp.mean(xc * xc, axis=-1, keepdims=True) + EPS)
    return xc * rstd, rstd


def _tril(transposed=False):
    t = lax.broadcasted_iota(jnp.int32, (BLOCK, BLOCK), 0)
    s = lax.broadcasted_iota(jnp.int32, (BLOCK, BLOCK), 1)
    return s >= t if transposed else t >= s


def _const_spec(shape):
    return pl.BlockSpec(shape, lambda i: (0,) * len(shape))


def _keys_values(z_ref, kvp):
    kvc = z_ref[:, SEG_KV:SEG_KV + 2 * D_KV]
    kk = jnp.concatenate([kvp[:, :D_KV], kvc[:, :D_KV]], axis=0)
    vv = jnp.concatenate([kvp[:, D_KV:], kvc[:, D_KV:]], axis=0)
    return kk, vv


MIXER_BLOCKS = 2


class _Rows:
    def __init__(self, ref, sub):
        self.ref, self.rows = ref, slice(sub * BLOCK, (sub + 1) * BLOCK)

    def __getitem__(self, idx):
        return self.ref[self.rows, idx[1]]

    def __setitem__(self, idx, value):
        self.ref[self.rows, idx[1]] = value


def _kv_before_spec(index):
    return pl.BlockSpec((BLOCK, 2 * D_KV),
                        lambda i: (jnp.maximum(MIXER_BLOCKS * index(i) - 1, 0), SEG_KV // (2 * D_KV)))


def _pair_cols(g, p, base=0):
    return slice(base + (4 * g + p) * 128, base + (4 * g + p + 1) * 128)


def _mixer_fwd(z, sink_rows, ln_g, ln_b, sgu_w, sgu_bt):
    t = z.shape[0]

    def body(z_all, kvp_ref, sink_ref, lng_ref, lnb_ref, w_ref, bt_ref, a_all, prob_ref, sink_prob_ref):
        kv_before = kvp_ref[...]
        for sub in range(MIXER_BLOCKS):
            z_ref, a_ref = _Rows(z_all, sub), _Rows(a_all, sub)
            one_block(z_ref, kv_before, MIXER_BLOCKS * pl.program_id(0) + sub, sink_ref, lng_ref, lnb_ref, w_ref,
                      bt_ref, a_ref, prob_ref.at[sub], sink_prob_ref.at[sub])
            kv_before = z_ref[:, SEG_KV:SEG_KV + 2 * D_KV]

    def one_block(z_ref, kv_before, block_index, sink_ref, lng_ref, lnb_ref, w_ref, bt_ref, a_ref, prob_ref,
                  sink_prob_ref):
        bias = _window_bias(block_index)
        kk, vv = _keys_values(z_ref, kv_before)
        kk_b, vvt_b = kk.astype(BF16), vv.T.astype(BF16)
        for g in range(2):
            qt = _heads_t([z_ref[:, _pair_cols(g, p, SEG_Q)] * ATTN_SCALE for p in range(4)], g).astype(BF16)
            prob, sink_prob = _softmax_t(_dot(kk_b, qt), bias, sink_ref[g])
            prob_b = prob.astype(BF16)
            prob_ref[g] = prob_b
            sink_prob_ref[g] = sink_prob
            ot = _dot(vvt_b, prob_b)
            for p in range(4):
                gate = z_ref[:, _pair_cols(g, p, SEG_GA)]
                a_ref[:, _pair_cols(g, p)] = (_pair_block(ot, p, g) * (gate * _sigmoid(gate))).astype(BF16)

        vhat, _ = _layer_norm_fwd(z_ref[:, SEG_VS:SEG_VS + D_SGU])
        vn = vhat * lng_ref[...] + lnb_ref[...]
        tril = _tril()
        for g in range(SGU_GROUPS):
            cols = slice(g * 128, (g + 1) * 128)
            wm = jnp.where(tril, w_ref[g], 0.0).astype(BF16)
            mixed = _dot(wm, vn[:, cols].astype(BF16)) + bt_ref[:, g:g + 1]
            gate = z_ref[:, SEG_GS + g * 128:SEG_GS + (g + 1) * 128]
            a_ref[:, D_ATTN + g * 128:D_ATTN + (g + 1) * 128] = (
                (z_ref[:, SEG_U + g * 128:SEG_U + (g + 1) * 128] * mixed) * (gate * _sigmoid(gate))).astype(BF16)

    rows = MIXER_BLOCKS * BLOCK
    return pl.pallas_call(
        body, name="mixer_fwd", grid=(t // rows,),
        in_specs=[pl.BlockSpec((rows, D_IN), lambda i: (i, 0)), _kv_before_spec(lambda i: i),
                  _const_spec((2, 1, 8 * BLOCK)), _const_spec((1, D_SGU)), _const_spec((1, D_SGU)),
                  _const_spec((SGU_GROUPS, BLOCK, BLOCK)), _const_spec((BLOCK, SGU_GROUPS))],
        out_specs=(pl.BlockSpec((rows, D_MODEL), lambda i: (i, 0)),
                   pl.BlockSpec((MIXER_BLOCKS, 2, 2 * BLOCK, 8 * BLOCK), lambda i: (i, 0, 0, 0)),
                   pl.BlockSpec((MIXER_BLOCKS, 2, 1, 8 * BLOCK), lambda i: (i, 0, 0, 0))),
        out_shape=(jax.ShapeDtypeStruct((t, D_MODEL), BF16),
                   jax.ShapeDtypeStruct((t // BLOCK, 2, 2 * BLOCK, 8 * BLOCK), BF16),
                   jax.ShapeDtypeStruct((t // BLOCK, 2, 1, 8 * BLOCK), F32)),
        compiler_params=_params(dimension_semantics=("arbitrary",)),
    )(z, z, sink_rows, ln_g, ln_b, sgu_w, sgu_bt)


def _mixer_bwd(z, da, probs, sink_probs, ln_g, ln_b, sgu_w, sgu_wt, sgu_bt):
    t = z.shape[0]

    def body(z_all, kvp_ref, da_all, prob_ref, sink_prob_ref, lng_ref, lnb_ref, w_ref, wt_ref, bt_ref,
             dz_all, dsink_ref, dw_ref, db_ref, dlng_ref, dlnb_ref, carry_ref, dsink_acc, dbt_acc):
        step = pl.program_id(0)

        @pl.when(step == 0)
        def _():
            carry_ref[...] = jnp.zeros_like(carry_ref)
            dsink_acc[...] = jnp.zeros_like(dsink_acc)
            dbt_acc[...] = jnp.zeros_like(dbt_acc)
            dw_ref[...] = jnp.zeros_like(dw_ref)
            dlng_ref[...] = jnp.zeros_like(dlng_ref)
            dlnb_ref[...] = jnp.zeros_like(dlnb_ref)

        carry = carry_ref[...]
        for sub in reversed(range(MIXER_BLOCKS)):
            kv_before = kvp_ref[...] if sub == 0 else _Rows(z_all, sub - 1)[:, SEG_KV:SEG_KV + 2 * D_KV]
            carry = one_block(_Rows(z_all, sub), kv_before, _Rows(da_all, sub), prob_ref.at[sub], sink_prob_ref.at[sub],
                              carry, lng_ref, lnb_ref, w_ref, wt_ref, bt_ref, _Rows(dz_all, sub),
                              dw_ref, dlng_ref, dlnb_ref, dsink_acc, dbt_acc)
        carry_ref[...] = carry

        @pl.when(step == ns - 1)
        def _():
            db_ref[...] = dbt_acc[...].T[:SGU_GROUPS]
            lane_row = lax.broadcasted_iota(jnp.int32, (1, 128), 1)
            d_sink = jnp.zeros((1, 128), F32)
            for g in range(2):
                acc = dsink_acc[g]
                for j in range(8):
                    head_sum = jnp.sum(acc[:, j * BLOCK:(j + 1) * BLOCK], axis=-1, keepdims=True)
                    d_sink = d_sink + jnp.where(lane_row == 8 * g + j, head_sum, 0.0)
            dsink_ref[...] = d_sink

    def one_block(z_ref, kv_before, da_ref, prob_ref, sink_prob_ref, carry, lng_ref, lnb_ref, w_ref, wt_ref, bt_ref,
                  dz_ref, dw_ref, dlng_ref, dlnb_ref, dsink_acc, dbt_acc):
        kk, vv = _keys_values(z_ref, kv_before)
        vv_b = vv.astype(BF16)
        kkt_b, vvt_b = kk.T.astype(BF16), vv.T.astype(BF16)
        dkk = jnp.zeros((2 * BLOCK, D_KV), F32)
        dvv = jnp.zeros((2 * BLOCK, D_KV), F32)
        for g in range(2):
            qt = _heads_t([z_ref[:, _pair_cols(g, p, SEG_Q)] * ATTN_SCALE for p in range(4)], g).astype(BF16)
            prob_b, sink_prob = prob_ref[g], sink_prob_ref[g]
            prob = prob_b.astype(F32)
            ot = _dot(vvt_b, prob_b)
            gates = [z_ref[:, _pair_cols(g, p, SEG_GA)] for p in range(4)]
            sig = [_sigmoid(gt) for gt in gates]
            d_attn = [da_ref[:, _pair_cols(g, p)] for p in range(4)]
            d_ot = _heads_t([d_attn[p] * (gates[p] * sig[p]) for p in range(4)], g).astype(BF16)
            d_prob = _dot(vv_b, d_ot)
            delta = jnp.sum(prob * d_prob, axis=0, keepdims=True)
            d_scores = (prob * (d_prob - delta)).astype(BF16)
            dsink_acc[g] -= sink_prob * delta
            d_qt = _dot(kkt_b, d_scores)
            dkk = dkk + _dot_nt(d_scores, qt)
            dvv = dvv + _dot_nt(prob_b, d_ot)
            for p in range(4):
                dz_ref[:, _pair_cols(g, p, SEG_Q)] = (_pair_block(d_qt, p, g) * ATTN_SCALE).astype(BF16)
                d_silu = sig[p] * (1.0 + gates[p] * (1.0 - sig[p]))
                dz_ref[:, _pair_cols(g, p, SEG_GA)] = (d_attn[p] * _pair_block(ot, p, g) * d_silu).astype(BF16)
        d_kv = jnp.concatenate([dkk, dvv], axis=1)
        dz_ref[:, SEG_KV:SEG_KV + 2 * D_KV] = (d_kv[BLOCK:] + carry).astype(BF16)

        vhat, rstd = _layer_norm_fwd(z_ref[:, SEG_VS:SEG_VS + D_SGU])
        lng = lng_ref[...]
        vn = vhat * lng + lnb_ref[...]
        tril, triu = _tril(), _tril(transposed=True)
        lane = lax.broadcasted_iota(jnp.int32, (BLOCK, 128), 1)
        d_bt = jnp.zeros((BLOCK, 128), F32)
        d_vn = []
        for g in range(SGU_GROUPS):
            cols = slice(g * 128, (g + 1) * 128)
            wm = jnp.where(tril, w_ref[g], 0.0).astype(BF16)
            wmt = jnp.where(triu, wt_ref[g], 0.0).astype(BF16)
            vn_g = vn[:, cols].astype(BF16)
            mixed = _dot(wm, vn_g) + bt_ref[:, g:g + 1]
            gate = z_ref[:, SEG_GS + g * 128:SEG_GS + (g + 1) * 128]
            u = z_ref[:, SEG_U + g * 128:SEG_U + (g + 1) * 128]
            d_out = da_ref[:, D_ATTN + g * 128:D_ATTN + (g + 1) * 128]
            sg = _sigmoid(gate)
            d_um = d_out * (gate * sg)
            dz_ref[:, SEG_U + g * 128:SEG_U + (g + 1) * 128] = (d_um * mixed).astype(BF16)
            dz_ref[:, SEG_GS + g * 128:SEG_GS + (g + 1) * 128] = (
                d_out * (u * mixed) * (sg * (1.0 + gate * (1.0 - sg)))).astype(BF16)
            d_mixed = d_um * u
            d_mixed_b = d_mixed.astype(BF16)
            dw_ref[g] += jnp.where(tril, _dot_nt(d_mixed_b, vn_g), 0.0)
            d_bt = d_bt + jnp.where(lane == g, jnp.sum(d_mixed, axis=-1, keepdims=True), 0.0)
            d_vn.append(_dot(wmt, d_mixed_b))
        dbt_acc[...] += d_bt
        d_vn = jnp.concatenate(d_vn, axis=1)
        dlng_ref[...] += jnp.sum(d_vn * vhat, axis=0, keepdims=True)
        dlnb_ref[...] += jnp.sum(d_vn, axis=0, keepdims=True)
        d_vhat = d_vn * lng
        d_v = rstd * (d_vhat - jnp.mean(d_vhat, axis=-1, keepdims=True)
                      - vhat * jnp.mean(d_vhat * vhat, axis=-1, keepdims=True))
        dz_ref[:, SEG_VS:SEG_VS + D_SGU] = d_v.astype(BF16)
        return d_kv[:BLOCK]

    rows = MIXER_BLOCKS * BLOCK
    ns = t // rows
    rev = lambda i: ns - 1 - i
    return pl.pallas_call(
        body, name="mixer_bwd", grid=(ns,),
        in_specs=[pl.BlockSpec((rows, D_IN), lambda i: (rev(i), 0)), _kv_before_spec(rev),
                  pl.BlockSpec((rows, D_MODEL), lambda i: (rev(i), 0)),
                  pl.BlockSpec((MIXER_BLOCKS, 2, 2 * BLOCK, 8 * BLOCK), lambda i: (rev(i), 0, 0, 0)),
                  pl.BlockSpec((MIXER_BLOCKS, 2, 1, 8 * BLOCK), lambda i: (rev(i), 0, 0, 0)),
                  _const_spec((1, D_SGU)), _const_spec((1, D_SGU)),
                  _const_spec((SGU_GROUPS, BLOCK, BLOCK)), _const_spec((SGU_GROUPS, BLOCK, BLOCK)),
                  _const_spec((BLOCK, SGU_GROUPS))],
        out_specs=(pl.BlockSpec((rows, D_IN), lambda i: (rev(i), 0)), _const_spec((1, 128)),
                   _const_spec((SGU_GROUPS, BLOCK, BLOCK)), _const_spec((SGU_GROUPS, BLOCK)),
                   _const_spec((1, D_SGU)), _const_spec((1, D_SGU))),
        out_shape=(jax.ShapeDtypeStruct((t, D_IN), BF16), jax.ShapeDtypeStruct((1, 128), F32),
                   jax.ShapeDtypeStruct((SGU_GROUPS, BLOCK, BLOCK), F32), jax.ShapeDtypeStruct((SGU_GROUPS, BLOCK), F32),
                   jax.ShapeDtypeStruct((1, D_SGU), F32), jax.ShapeDtypeStruct((1, D_SGU), F32)),
        scratch_shapes=[pltpu.VMEM((BLOCK, 2 * D_KV), F32), pltpu.VMEM((2, 1, 8 * BLOCK), F32),
                        pltpu.VMEM((BLOCK, 128), F32)],
        compiler_params=_params(dimension_semantics=("arbitrary",)),
    )(z, z, da, probs, sink_probs, ln_g, ln_b, sgu_w, sgu_wt, sgu_bt)


def _out_proj_head(a, w_out_full, x, target, mod, final_g, tm=256):
    t, d = x.shape

    def body(a_ref, w_ref, x_ref, tg_ref, gate_ref, fg_ref, dx2_ref, dy_ref, loss_ref, dfg_ref, dgate_ref):
        @pl.when(pl.program_id(0) == 0)
        def _():
            loss_ref[...] = jnp.zeros_like(loss_ref)
            dfg_ref[...] = jnp.zeros_like(dfg_ref)
            dgate_ref[...] = jnp.zeros_like(dgate_ref)

        yv, gate, fg = _dot(a_ref[...], w_ref[...]), gate_ref[...], fg_ref[...]
        x2 = x_ref[...] + gate * yv
        r2 = lax.rsqrt(jnp.mean(x2 * x2, axis=-1, keepdims=True) + EPS)
        nrm = x2 * r2
        err = nrm * fg - tg_ref[...]
        loss_ref[...] += 0.5 * jnp.sum(jnp.mean(err * err, axis=-1, keepdims=True), axis=0, keepdims=True)
        d_out = err * (1.0 / d)
        dfg_ref[...] += jnp.sum(d_out * nrm, axis=0, keepdims=True)
        d_nrm = d_out * fg
        dx2 = r2 * (d_nrm - nrm * jnp.mean(d_nrm * nrm, axis=-1, keepdims=True))
        dx2_ref[...] = dx2
        dgate_ref[...] += jnp.sum(dx2 * yv, axis=0, keepdims=True)
        dy_ref[...] = (dx2 * gate).astype(BF16)

    blk = pl.BlockSpec((tm, d), lambda i: (i, 0))
    row = _const_spec((1, d))
    whole = pl.BlockSpec(w_out_full.shape, lambda i: (0, 0), pipeline_mode=pl.Buffered(1))
    return pl.pallas_call(
        body, name="out_proj_head", grid=(t // tm,),
        in_specs=[pl.BlockSpec((tm, a.shape[1]), lambda i: (i, 0)), whole, blk, blk, _mod_spec(MOD_GATE, d), row],
        out_specs=(blk, blk, _const_spec((1, 128)), row, row),
        out_shape=(jax.ShapeDtypeStruct((t, d), F32), jax.ShapeDtypeStruct((t, d), BF16),
                   jax.ShapeDtypeStruct((1, 128), F32), jax.ShapeDtypeStruct((1, d), F32),
                   jax.ShapeDtypeStruct((1, d), F32)),
        compiler_params=_params(dimension_semantics=("arbitrary",)),
    )(a, w_out_full, x, target, mod, final_g)


def _z_proj_bwd_norm(dz, w_in_t, x, dx2, norm_g, mod, dep, tm=256):
    t, d = x.shape

    def body(dz_ref, w_ref, x_ref, dx2_ref, g_ref, sc_ref, dep_ref, gx_ref, dshift_ref, dscale_ref, dg_ref):
        @pl.when(pl.program_id(0) == 0)
        def _():
            dshift_ref[...] = jnp.zeros_like(dshift_ref)
            dscale_ref[...] = jnp.zeros_like(dscale_ref)
            dg_ref[...] = jnp.zeros_like(dg_ref)

        dh, xv, g = _dot(dz_ref[...], w_ref[...]), x_ref[...], g_ref[...]
        one_plus = 1.0 + sc_ref[...]
        r = lax.rsqrt(jnp.mean(xv * xv, axis=-1, keepdims=True) + EPS)
        xn = xv * r
        dshift_ref[...] += jnp.sum(dh, axis=0, keepdims=True)
        dscale_ref[...] += jnp.sum(dh * (xn * g), axis=0, keepdims=True)
        d_y = dh * one_plus
        dg_ref[...] += jnp.sum(d_y * xn, axis=0, keepdims=True)
        d_xn = d_y * g
        gx_ref[...] = dx2_ref[...] + r * (d_xn - xn * jnp.mean(d_xn * xn, axis=-1, keepdims=True))

    blk = pl.BlockSpec((tm, d), lambda i: (i, 0))
    row = _const_spec((1, d))
    whole = pl.BlockSpec(w_in_t.shape, lambda i: (0, 0), pipeline_mode=pl.Buffered(1))
    return pl.pallas_call(
        body, name="z_proj_bwd_norm", grid=(t // tm,),
        in_specs=[pl.BlockSpec((tm, dz.shape[1]), lambda i: (i, 0)), whole, blk, blk, row, _mod_spec(MOD_SCALE, d),
                  _const_spec((8, 128))],
        out_specs=(blk, row, row, row),
        out_shape=(jax.ShapeDtypeStruct((t, d), F32),) + (jax.ShapeDtypeStruct((1, d), F32),) * 3,
        compiler_params=_params(dimension_semantics=("arbitrary",)),
    )(dz, w_in_t, x, dx2, norm_g, mod, dep)


def _adamw(w, g, m, v):
    m = ADAM_B1 * m + (1.0 - ADAM_B1) * g
    v = ADAM_B2 * v + (1.0 - ADAM_B2) * (g * g)
    m_hat = m / (1.0 - ADAM_B1 ** ADAM_STEP)
    v_hat = v / (1.0 - ADAM_B2 ** ADAM_STEP)
    delta = -ADAM_LR * (m_hat / (jnp.sqrt(v_hat) + ADAM_EPS) + ADAM_WD * w)
    return delta, m, v


def _relay_sum(second_chip, pair, land, tr):
    _, r, c = pair.shape

    def body(chip_ref, a_ref, b_ref, o_ref):
        o_ref[...] = (a_ref[...].astype(F32) + b_ref[...].astype(F32)).astype(BF16)

    return pl.pallas_call(
        body, name="w_in_grad_relay_sum",
        grid_spec=pltpu.PrefetchScalarGridSpec(
            num_scalar_prefetch=1, grid=(r // tr,),
            in_specs=[pl.BlockSpec((None, tr, c), lambda i, chip_ref: (chip_ref[0], i, 0)),
                      pl.BlockSpec((None, tr, c), lambda i, chip_ref: (1, i, 0))],
            out_specs=pl.BlockSpec((tr, c), lambda i, chip_ref: (i, 0))),
        out_shape=jax.ShapeDtypeStruct((r, c), BF16),
        compiler_params=_params(dimension_semantics=("arbitrary",)),
    )(second_chip, pair, land)


def _adam_from_chips(chip, pair, landed, w, m, v, name, tc):
    _, r, c = pair.shape
    n = len(landed)

    def body(chip_ref, own_ref, *refs):
        w_ref, m_ref, v_ref, g_ref, d_ref, nm_ref, nv_ref = refs[n:]
        g = own_ref[...].astype(F32)
        for k in range(n):
            g = g + refs[k][...].astype(F32)
        g_ref[...] = g
        d_ref[...], nm_ref[...], nv_ref[...] = _adamw(w_ref[...], g, m_ref[...], v_ref[...])

    def landed_spec(index):
        return pl.BlockSpec((None, r, tc), lambda i, chip_ref: (index, 0, i))

    blk = pl.BlockSpec((r, tc), lambda i, chip_ref: (0, i))
    return pl.pallas_call(
        body, name=name,
        grid_spec=pltpu.PrefetchScalarGridSpec(
            num_scalar_prefetch=1, grid=(c // tc,),
            in_specs=[pl.BlockSpec((None, r, tc), lambda i, chip_ref: (chip_ref[0], 0, i))]
            + [landed_spec(index) for _, index in landed] + [blk, blk, blk],
            out_specs=(blk,) * 4),
        out_shape=(jax.ShapeDtypeStruct((r, c), F32),) * 4,
        compiler_params=_params(dimension_semantics=("arbitrary",)),
    )(chip, pair, *[array for array, _ in landed], w, m, v)


def _adam_w_ada(device, act_t, dmod_all, w, m, v, tr=512):
    r, c = w.shape

    def body(device_ref, a_ref, dm_ref, w_ref, m_ref, v_ref, g_ref, d_ref, nm_ref, nv_ref):
        g = _dot(a_ref[...].astype(BF16), dm_ref[...].astype(BF16))
        g_ref[...] = g
        d_ref[...], nm_ref[...], nv_ref[...] = _adamw(w_ref[...], g, m_ref[...], v_ref[...])

    blk = pl.BlockSpec((tr, c), lambda i, device_ref: (i, 0))
    return pl.pallas_call(
        body, name="adam_w_ada",
        grid_spec=pltpu.PrefetchScalarGridSpec(
            num_scalar_prefetch=1, grid=(r // tr,),
            in_specs=[pl.BlockSpec((tr, N_DEV), lambda i, device_ref: (i, 0)),
                      pl.BlockSpec((N_DEV, c), lambda i, device_ref: (0, device_ref[0])), blk, blk, blk],
            out_specs=(blk,) * 4),
        out_shape=(jax.ShapeDtypeStruct((r, c), F32),) * 4,
        compiler_params=_params(dimension_semantics=("arbitrary",)),
    )(device, act_t, dmod_all, w, m, v)


def _pack_small(d_shift, d_scale, d_gate, d_norm_g, d_final_g, d_ln_g, d_ln_b, loss, d_sinks, d_sgu_b):
    def body(shift_ref, scale_ref, gate_ref, ng_ref, fg_ref, lng_ref, lnb_ref, loss_ref, sink_ref, b_ref, o_ref):
        o_ref[...] = jnp.zeros_like(o_ref)
        o_ref[ROW_SHIFT:ROW_SHIFT + 1, :] = shift_ref[...]
        o_ref[ROW_SCALE:ROW_SCALE + 1, :] = scale_ref[...]
        o_ref[ROW_GATE:ROW_GATE + 1, :] = gate_ref[...]
        o_ref[ROW_NORM_G:ROW_NORM_G + 1, :] = ng_ref[...]
        o_ref[ROW_FINAL_G:ROW_FINAL_G + 1, :] = fg_ref[...]
        o_ref[ROW_LN:ROW_LN + 1, 0:D_SGU] = lng_ref[...]
        o_ref[ROW_LN:ROW_LN + 1, D_SGU:2 * D_SGU] = lnb_ref[...]
        o_ref[ROW_MISC:ROW_MISC + 1, 0:128] = loss_ref[...]
        o_ref[ROW_MISC:ROW_MISC + 1, 128:256] = sink_ref[...]
        o_ref[ROW_SGU_B:ROW_SGU_B + SGU_GROUPS, 0:BLOCK] = b_ref[...]

    return pl.pallas_call(
        body, name="pack_small", out_shape=jax.ShapeDtypeStruct((SMALL_ROWS, D_MODEL), F32),
        compiler_params=_params(),
    )(d_shift, d_scale, d_gate, d_norm_g, d_final_g, d_ln_g, d_ln_b, loss, d_sinks, d_sgu_b)


_SMALL_NAMES = ("norm_g", "b_ada", "attn_sinks", "sgu_ln_g", "sgu_ln_b", "sgu_w", "sgu_b", "final_g")


def _adam_small(partials, d_sgu_w_all, weights, moments_m, moments_v):
    names = _SMALL_NAMES
    k = len(names)

    def body(*refs):
        p_ref, sw_ref = refs[0], refs[1]
        w_refs, m_refs, v_refs = refs[2:2 + k], refs[2 + k:2 + 2 * k], refs[2 + 2 * k:2 + 3 * k]
        loss_ref, dmod_ref = refs[2 + 3 * k], refs[3 + 3 * k]
        out_refs = refs[4 + 3 * k:4 + 7 * k]
        sum_ref = refs[4 + 7 * k]
        total = p_ref[0]
        for j in range(1, N_DEV):
            total = total + p_ref[j]
        sum_ref[...] = total
        for j in range(N_DEV):
            for part, row in enumerate((ROW_SHIFT, ROW_SCALE, ROW_GATE)):
                dmod_ref[j:j + 1, part * D_MODEL:(part + 1) * D_MODEL] = p_ref[j, row:row + 1, :]
        loss_ref[...] = sum_ref[ROW_MISC:ROW_MISC + 1, 0:1]
        d_sgu_w = sw_ref[0]
        for j in range(1, N_DEV):
            d_sgu_w = d_sgu_w + sw_ref[j]
        grads = {
            "norm_g": sum_ref[ROW_NORM_G:ROW_NORM_G + 1, :],
            "b_ada": jnp.concatenate([sum_ref[r:r + 1, :] for r in (ROW_SHIFT, ROW_SCALE, ROW_GATE)], axis=1),
            "attn_sinks": sum_ref[ROW_MISC:ROW_MISC + 1, 128:128 + N_Q_HEADS],
            "sgu_ln_g": sum_ref[ROW_LN:ROW_LN + 1, 0:D_SGU],
            "sgu_ln_b": sum_ref[ROW_LN:ROW_LN + 1, D_SGU:2 * D_SGU],
            "sgu_w": d_sgu_w[None],
            "sgu_b": sum_ref[ROW_SGU_B:ROW_SGU_B + SGU_GROUPS, 0:BLOCK][None],
            "final_g": sum_ref[ROW_FINAL_G:ROW_FINAL_G + 1, :],
        }
        for i, name in enumerate(names):
            g = grads[name]
            delta, m, v = _adamw(w_refs[i][...], g, m_refs[i][...], v_refs[i][...])
            out_refs[4 * i][...] = g
            out_refs[4 * i + 1][...] = delta
            out_refs[4 * i + 2][...] = m
            out_refs[4 * i + 3][...] = v

    shapes = [jax.ShapeDtypeStruct((1, 1), F32), jax.ShapeDtypeStruct((N_DEV, 3 * D_MODEL), F32)]
    for name in names:
        shapes += [jax.ShapeDtypeStruct(weights[name].shape, F32)] * 4
    outs = pl.pallas_call(
        body, name="adam_small", out_shape=tuple(shapes),
        scratch_shapes=[pltpu.VMEM((SMALL_ROWS, D_MODEL), F32)],
        compiler_params=_params(),
    )(partials, d_sgu_w_all, *[weights[n] for n in names], *[moments_m[n] for n in names],
      *[moments_v[n] for n in names])
    return outs[0], outs[1], {name: outs[2 + 4 * i:6 + 4 * i] for i, name in enumerate(names)}


def kernel(x, c, norm_g, w_ada, b_ada, w_in, attn_sinks, sgu_ln_g, sgu_ln_b, sgu_w, sgu_b, w_out, final_g, loss_target, m_norm_g, m_w_ada, m_b_ada, m_w_in, m_attn_sinks, m_sgu_ln_g, m_sgu_ln_b, m_sgu_w, m_sgu_b, m_w_out, m_final_g, v_norm_g, v_w_ada, v_b_ada, v_w_in, v_attn_sinks, v_sgu_ln_g, v_sgu_ln_b, v_sgu_w, v_sgu_b, v_w_out, v_final_g):
    xi, yi, ci = _place()
    me = 4 * xi + 2 * yi + ci
    x2d, target = x[0], loss_target[0]
    t = x2d.shape[0]

    core = ci.astype(jnp.int32).reshape(1)
    chip = (2 * xi + yi).astype(jnp.int32).reshape(1)

    first = _own_block_copies(_first_targets)
    first_flight = _start_copies([_with_own_slot(w_in[0].T.astype(BF16), me)], first, 2, core, "gather_w_in_start")

    c_all = _all_gather_small(c.reshape(8, 256) + first_flight[3][0, 0], "gather_c").reshape(N_DEV, D_MODEL)
    device = me.astype(jnp.int32).reshape(1)
    c_act, mod_part = _modulation(device, c_all, w_ada[0], b_ada)
    mod_all = _all_gather_small(mod_part, "gather_mod")

    across = _wait_then_start(first_flight, lambda *a: first(*a)[1:], _second_axis_stage_copies, 3, mod_all,
                              "gather_w_in_second_axis_stage")
    mod = lax.dynamic_index_in_dim(mod_all, me, axis=1, keepdims=False).reshape(1, 3 * D_MODEL)
    mod = mod + across[3][0, 0]
    h = _modulated_norm(x2d, norm_g, mod)

    w_in_pair = _wait_copies((first_flight[0], first_flight[1], across[2], None), lambda *a: first(*a)[:1], h,
                             "gather_w_in_sibling_wait")
    z_own = _z_proj(h, w_in_pair[0].reshape(D_IN, D_MODEL), chip, 0, 1, None, "z_proj_own")
    w_out_early = _own_block_copies(lambda x, y, c: [(x, y, 1 - c), (*_second_axis_chip(x, y, c), c)])
    w_out_late = _own_block_copies(lambda x, y, c: [(*_first_axis_chip(x, y, c), c), (1 - x, 1 - y, c)])
    forward = _wait_then_start(
        (across[0], across[1], w_in_pair, None), lambda *a: _second_axis_stage_copies(*a)[:1],
        lambda refs, s, r: _second_axis_forward_copies(refs[:1], s, r) + _group(w_out_early, 1, 1, 1)(refs, s, r),
        3, z_own, "gather_w_in_second_axis_forward", more_bufs=[_with_own_slot(w_out[0].astype(BF16), me)])
    w_in_most = _wait_copies((across[0], across[1], forward[2][:1], None),
                             lambda *a: _second_axis_stage_copies(*a)[1:2], z_own, "gather_w_in_first_forward_wait")
    w_in_most = _wait_copies((forward[0], forward[1], w_in_most, None), _second_axis_forward_copies, z_own,
                             "gather_w_in_second_forward_wait")
    z_early = _z_proj(h, w_in_most[0].reshape(D_IN, D_MODEL), chip, 1, _Z_EARLY_TILES - 1, z_own, "z_proj_early")
    last = _wait_then_start(
        (across[0], across[1], [w_in_most[0], forward[2][1]], None), lambda *a: _second_axis_stage_copies(*a)[2:],
        lambda refs, s, r: _diagonal_forward_copies(refs[:1], s, r) + _group(w_out_late, 1, 1, 1)(refs, s, r),
        3, z_early, "gather_w_in_last_stage")
    w_in_all = _wait_copies((last[0], last[1], last[2][:1], None), _diagonal_forward_copies, z_early,
                            "gather_w_in_last_wait")[0]
    w_in_t = w_in_all.reshape(D_IN, D_MODEL)
    z = _z_proj(h, w_in_t, chip, _Z_EARLY_TILES, 7 - _Z_EARLY_TILES, z_early, "z_proj_late")
    w_out_half = _wait_copies((forward[0], forward[1], last[2][1:], None), _group(w_out_early, 0, 1, 1), z,
                              "gather_w_out_early_wait")
    w_out_flight = _wait_then_start((last[0], last[1], w_out_half, None), _group(w_out_late, 0, 1, 1),
                                    _forward_copies, 3, z, "gather_w_out_forward_stage")
    sink_rows = jnp.repeat(attn_sinks.reshape(N_Q_HEADS), BLOCK).reshape(2, 1, 8 * BLOCK)
    sgu_bt = sgu_b[0].T
    a, probs, sink_probs = _mixer_fwd(z, sink_rows + w_out_flight[3][0, 0], sgu_ln_g, sgu_ln_b, sgu_w[0], sgu_bt)
    w_out_all = _wait_copies(w_out_flight, _forward_copies, a, "gather_w_out_forward_wait")[0]
    w_out_full = w_out_all.reshape(D_MODEL, D_MODEL)
    final_g_row = final_g.reshape(1, D_MODEL)
    dx2, dy, loss_part, d_final_g, d_gate = _out_proj_head(a, w_out_full, x2d, target, mod, final_g_row)

    da = _matmul(dy, w_out_full, "nt", F32, min(t, 1024), 1024, "out_proj_bwd")
    dw_out = _matmul(a, dy, "tn", BF16, 1024, 1024, "w_out_grad").reshape(4, 2, W_OUT_SHARD, D_MODEL)
    pair_out = _pair_reduce(dw_out, "w_out_grad_pair_reduce", W_OUT_SHARD // 2)
    dz, d_sinks, d_sgu_w, d_sgu_b, d_ln_g, d_ln_b = _mixer_bwd(
        z, da, probs, sink_probs, sgu_ln_g, sgu_ln_b, sgu_w[0], jnp.swapaxes(sgu_w[0], 1, 2), sgu_bt)
    sgu_w_to_all = _group(_own_block_copies(_all_others), 2, 1, 3)
    both = _start_copies(
        [pair_out, lax.empty((3, W_OUT_SHARD, D_MODEL), BF16), _with_own_slot(d_sgu_w, me)],
        lambda refs, s, r: _chip_copies(refs[:2], s, r) + sgu_w_to_all(refs, s, r), 3 + N_DEV - 1, core,
        "w_out_grad_chip_and_sgu_w_gather_start")
    out_flight, sgu_w_flight = (both[0], both[1], both[2][:2], None), (both[0], both[1], both[2][2:], None)
    dw_in_t = _matmul(dz, h, "tn", BF16, 768, D_MODEL, "w_in_grad", dep=both[3])
    dw_in_t = dw_in_t.reshape(4, 2, W_IN_SHARD, D_MODEL)
    pair_in = _pair_reduce(dw_in_t, "w_in_grad_pair_reduce", W_IN_SHARD // 3)
    hop1 = _start_copies([pair_in, lax.empty((2, W_IN_SHARD, D_MODEL), BF16)], _first_hop_copies, 2, core,
                         "w_in_grad_first_hop_start")
    grad_x, d_shift, d_scale, d_norm_g = _z_proj_bwd_norm(dz, w_in_t, x2d, dx2, norm_g, mod, hop1[3])

    partial = _pack_small(d_shift, d_scale, d_gate, d_norm_g, d_final_g, d_ln_g, d_ln_b, loss_part, d_sinks, d_sgu_b)
    small_flight = _start_copies([_with_own_slot(partial, me)], _own_block_copies(_all_others), N_DEV - 1, core,
                                 "small_grad_gather_start")
    pair_in, land_first = _wait_copies(hop1, _first_hop_copies, small_flight[3], "w_in_grad_first_hop_wait")
    second_chip = (2 * ((xi + ci) % 2) + (yi + 1 - ci) % 2).astype(jnp.int32).reshape(1)
    relay = _relay_sum(second_chip, pair_in, land_first, W_IN_SHARD // 3)
    hop2 = _start_copies([relay, lax.empty((1, W_IN_SHARD, D_MODEL), BF16)], _second_hop_copies, 1, core,
                         "w_in_grad_second_hop_start")
    pair_out, land_out = _wait_copies(out_flight, _chip_copies, hop2[3], "w_out_grad_chip_wait")
    big = {"w_out": _adam_from_chips(chip, pair_out, [(land_out, k) for k in range(3)], w_out[0], m_w_out[0],
                                     v_w_out[0], "adam_w_out", 1024)}
    partial_all = _wait_copies(small_flight, _own_block_copies(_all_others), big["w_out"][0],
                               "small_grad_gather_wait")[0]
    d_sgu_w_all = _wait_copies(sgu_w_flight, _group(_own_block_copies(_all_others), 0, 1, 3), partial_all,
                               "sgu_w_grad_gather_wait")[0]
    weights = {"norm_g": norm_g, "b_ada": b_ada, "attn_sinks": attn_sinks, "sgu_ln_g": sgu_ln_g,
               "sgu_ln_b": sgu_ln_b, "sgu_w": sgu_w, "sgu_b": sgu_b, "final_g": final_g_row}
    moments_m = {"norm_g": m_norm_g, "b_ada": m_b_ada, "attn_sinks": m_attn_sinks, "sgu_ln_g": m_sgu_ln_g,
                 "sgu_ln_b": m_sgu_ln_b, "sgu_w": m_sgu_w, "sgu_b": m_sgu_b,
                 "final_g": m_final_g.reshape(1, D_MODEL)}
    moments_v = {"norm_g": v_norm_g, "b_ada": v_b_ada, "attn_sinks": v_attn_sinks, "sgu_ln_g": v_sgu_ln_g,
                 "sgu_ln_b": v_sgu_ln_b, "sgu_w": v_sgu_w, "sgu_b": v_sgu_b,
                 "final_g": v_final_g.reshape(1, D_MODEL)}
    loss, dmod_all, small = _adam_small(partial_all, d_sgu_w_all, weights, moments_m, moments_v)
    small["final_g"] = tuple(o.reshape(D_MODEL) for o in small["final_g"])

    big["w_ada"] = _adam_w_ada(device, c_act.T, dmod_all, w_ada[0], m_w_ada[0], v_w_ada[0])
    _, land_second = _wait_copies(hop2, _second_hop_copies, big["w_ada"][0], "w_in_grad_second_hop_wait")
    big["w_in"] = tuple(o.T for o in _adam_from_chips(
        chip, pair_in, [(land_first, 0), (land_second, 0)], w_in[0].T, m_w_in[0].T, v_w_in[0].T, "adam_w_in", 512))
    order = ["norm_g", "w_ada", "b_ada", "w_in", "attn_sinks", "sgu_ln_g", "sgu_ln_b", "sgu_w", "sgu_b", "w_out",
             "final_g"]
    outs = [loss.reshape(()), grad_x[None]]
    for k in range(4):
        for name in order:
            outs.append(big[name][k][None] if name in big else small[name][k])
    return tuple(outs)
```

```python
import jax
import jax.numpy as jnp
from jax import lax
from jax.experimental import pallas as pl
from jax.experimental.pallas import tpu as pltpu

F32 = jnp.float32
BF16 = jnp.bfloat16
MESH = pl.DeviceIdType.MESH

N_DEV = 8
D_MODEL = 2048
HEAD_DIM = 64
D_ATTN = 1024
N_Q_HEADS = 16
D_KV = 128
BLOCK = 128
D_SGU = 1024
SGU_GROUPS = 8
D_IN = 5376
W_IN_SHARD = D_IN // N_DEV
W_OUT_SHARD = D_MODEL // N_DEV
W_ADA_SHARD = 3 * D_MODEL // N_DEV
EPS = 1e-6
ATTN_SCALE = 0.125

ADAM_LR = 0.001
ADAM_B1 = 0.9
ADAM_B2 = 0.999
ADAM_EPS = 1e-08
ADAM_WD = 0.01
ADAM_STEP = 10

SEG_Q, SEG_KV, SEG_GA, SEG_U, SEG_VS, SEG_GS = 0, 1024, 1280, 2304, 3328, 4352

VMEM_LIMIT = 56 * 1024 * 1024

ROW_SHIFT, ROW_SCALE, ROW_GATE, ROW_NORM_G, ROW_FINAL_G, ROW_LN, ROW_MISC, ROW_SGU_B = 0, 1, 2, 3, 4, 5, 6, 8
SMALL_ROWS = 16


def _params(**kw):
    return pltpu.CompilerParams(vmem_limit_bytes=VMEM_LIMIT, **kw)


def _sigmoid(x):
    return 0.5 * (jnp.tanh(0.5 * x) + 1.0)


def _place():
    return lax.axis_index("x"), lax.axis_index("y"), lax.axis_index("c")


def _pair_reduce(blocks, name, row_chunk):
    _, _, r, cols = blocks.shape
    assert r % row_chunk == 0

    def body(in_ref, out_ref, land, own, summed, send_sems, recv_sems, own_sems, out_sems):
        x, y, c = _place()
        sends, loads, stores = [], [], []
        for m in range(4):
            cp = pltpu.make_async_remote_copy(
                src_ref=in_ref.at[m, 1 - c], dst_ref=land.at[m], send_sem=send_sems.at[m], recv_sem=recv_sems.at[m],
                device_id=(x, y, 1 - c), device_id_type=MESH)
            cp.start()
            sends.append(cp)
            ld = pltpu.make_async_copy(in_ref.at[m, c], own.at[m], own_sems.at[m])
            ld.start()
            loads.append(ld)
        for m in range(4):
            sends[m].wait_recv()
            loads[m].wait()
            for k in range(r // row_chunk):
                rows = slice(k * row_chunk, (k + 1) * row_chunk)
                summed[m, rows, :] = (own[m, rows, :].astype(F32) + land[m, rows, :].astype(F32)).astype(BF16)
            st = pltpu.make_async_copy(summed.at[m], out_ref.at[m], out_sems.at[m])
            st.start()
            stores.append(st)
        for m in range(4):
            sends[m].wait_send()
            stores[m].wait()

    spec = pl.BlockSpec(memory_space=pl.ANY)
    return pl.pallas_call(
        body, name=name, out_shape=jax.ShapeDtypeStruct((4, r, cols), BF16),
        in_specs=[spec], out_specs=spec,
        scratch_shapes=[pltpu.VMEM((4, r, cols), BF16), pltpu.VMEM((4, r, cols), BF16), pltpu.VMEM((4, r, cols), BF16),
                        pltpu.SemaphoreType.DMA((4,)), pltpu.SemaphoreType.DMA((4,)), pltpu.SemaphoreType.DMA((4,)),
                        pltpu.SemaphoreType.DMA((4,))],
        compiler_params=_params(),
    )(blocks)


_HBM = pl.BlockSpec(memory_space=pltpu.HBM)
_SEM = pl.BlockSpec(memory_space=pltpu.SEMAPHORE)
_EFFECT = pltpu.SideEffectType.DATAFLOW_SIDE_EFFECTING


def _start_copies(bufs, copies, n_copies, after, name):
    nb = len(bufs)

    def body(*refs):
        for cp in copies(refs[:nb], refs[nb + 1], refs[nb + 2]):
            cp.start()
        refs[-1][...] = jnp.zeros_like(refs[-1])

    out = pl.pallas_call(
        body, name=name,
        out_shape=(pltpu.SemaphoreType.DMA((n_copies,)), pltpu.SemaphoreType.DMA((n_copies,)),
                   *[pltpu.HBM(b.shape, b.dtype) for b in bufs], jax.ShapeDtypeStruct((8, 128), F32)),
        in_specs=(_HBM,) * nb + (pl.BlockSpec(memory_space=pl.ANY),),
        out_specs=(_SEM, _SEM) + (_HBM,) * nb + (pl.BlockSpec(memory_space=pltpu.VMEM),),
        input_output_aliases={i: 2 + i for i in range(nb)},
        compiler_params=pltpu.CompilerParams(has_side_effects=_EFFECT),
    )(*[pltpu.with_memory_space_constraint(b, pltpu.HBM) for b in bufs], after)
    return out[0], out[1], list(out[2:2 + nb]), out[-1]


def _wait_copies(flight, copies, after, name):
    send_sems, recv_sems, bufs, _ = flight
    nb = len(bufs)

    def body(*refs):
        for cp in copies(refs[:nb], refs[nb], refs[nb + 1]):
            cp.wait_send()
            cp.wait_recv()

    return pl.pallas_call(
        body, name=name,
        out_shape=tuple(pltpu.HBM(b.shape, b.dtype) for b in bufs),
        in_specs=(_HBM,) * nb + (_SEM, _SEM, pl.BlockSpec(memory_space=pl.ANY)), out_specs=(_HBM,) * nb,
        input_output_aliases={i: i for i in range(nb)},
        compiler_params=pltpu.CompilerParams(has_side_effects=_EFFECT),
    )(*bufs, send_sems, recv_sems, after)


class _From:
    def __init__(self, sems, offset):
        self.sems, self.offset = sems, offset

    @property
    def at(self):
        return self

    def __getitem__(self, k):
        return self.sems.at[k + self.offset]


def _group(copies, first_buf, n_bufs, offset):
    def grouped(refs, send_sems, recv_sems):
        return copies(refs[first_buf:first_buf + n_bufs], _From(send_sems, offset), _From(recv_sems, offset))
    return grouped


def _wait_then_start(flight, waited, started, n_started, after, name, more_bufs=()):
    old_send, old_recv, bufs, _ = flight
    bufs = list(bufs) + [pltpu.with_memory_space_constraint(b, pltpu.HBM) for b in more_bufs]
    nb = len(bufs)

    def body(*refs):
        for cp in waited(refs[:nb], refs[nb], refs[nb + 1]):
            cp.wait_send()
            cp.wait_recv()
        for cp in started(refs[:nb], refs[nb + 3], refs[nb + 4]):
            cp.start()
        refs[-1][...] = jnp.zeros_like(refs[-1])

    out = pl.pallas_call(
        body, name=name,
        out_shape=(pltpu.SemaphoreType.DMA((n_started,)), pltpu.SemaphoreType.DMA((n_started,)),
                   *[pltpu.HBM(b.shape, b.dtype) for b in bufs], jax.ShapeDtypeStruct((8, 128), F32)),
        in_specs=(_HBM,) * nb + (_SEM, _SEM, pl.BlockSpec(memory_space=pl.ANY)),
        out_specs=(_SEM, _SEM) + (_HBM,) * nb + (pl.BlockSpec(memory_space=pltpu.VMEM),),
        input_output_aliases={i: 2 + i for i in range(nb)},
        compiler_params=pltpu.CompilerParams(has_side_effects=_EFFECT),
    )(*bufs, old_send, old_recv, after)
    return out[0], out[1], list(out[2:2 + nb]), out[-1]


def _chip_copies(refs, send_sems, recv_sems):
    pair_ref, land_ref = refs
    x, y, c = _place()
    chips = [(1 - x, y), (x, 1 - y), (1 - x, 1 - y)]
    return [pltpu.make_async_remote_copy(
        src_ref=pair_ref.at[2 * chip[0] + chip[1]], dst_ref=land_ref.at[k],
        send_sem=send_sems.at[k], recv_sem=recv_sems.at[k],
        device_id=(*chip, c), device_id_type=MESH) for k, chip in enumerate(chips)]


def _first_hop_copies(refs, send_sems, recv_sems):
    pair_ref, land_ref = refs
    x, y, c = _place()
    first = ((x + 1 - c) % 2, (y + c) % 2)
    blocks = [2 * first[0] + first[1], 2 * (1 - x) + (1 - y)]
    return [pltpu.make_async_remote_copy(
        src_ref=pair_ref.at[blocks[k]], dst_ref=land_ref.at[k], send_sem=send_sems.at[k], recv_sem=recv_sems.at[k],
        device_id=(*first, c), device_id_type=MESH) for k in range(2)]


def _second_hop_copies(refs, send_sems, recv_sems):
    relay_ref, land_ref = refs
    x, y, c = _place()
    second = ((x + c) % 2, (y + 1 - c) % 2)
    return [pltpu.make_async_remote_copy(
        src_ref=relay_ref, dst_ref=land_ref.at[0], send_sem=send_sems.at[0], recv_sem=recv_sems.at[0],
        device_id=(*second, c), device_id_type=MESH)]


def _own_block_copies(targets):
    def copies(refs, send_sems, recv_sems):
        x, y, c = _place()
        mine = refs[0].at[4 * x + 2 * y + c]
        return [pltpu.make_async_remote_copy(
            src_ref=mine, dst_ref=mine, send_sem=send_sems.at[k], recv_sem=recv_sems.at[k],
            device_id=to, device_id_type=MESH) for k, to in enumerate(targets(x, y, c))]
    return copies


def _all_others(x, y, c):
    flip = lambda v, f: 1 - v if f else v
    return [(flip(x, r & 4), flip(y, r & 2), flip(c, r & 1)) for r in range(1, N_DEV)]


def _forward_copies(refs, send_sems, recv_sems):
    x, y, c = _place()
    chips = [(1 - x, y), (x, 1 - y), (1 - x, 1 - y)]
    return [pltpu.make_async_remote_copy(
        src_ref=refs[0].at[4 * chip[0] + 2 * chip[1] + c], dst_ref=refs[0].at[4 * chip[0] + 2 * chip[1] + c],
        send_sem=send_sems.at[k], recv_sem=recv_sems.at[k],
        device_id=(x, y, 1 - c), device_id_type=MESH) for k, chip in enumerate(chips)]


def _first_axis_chip(x, y, c):
    return (x + 1 - c) % 2, (y + c) % 2


def _second_axis_chip(x, y, c):
    return (x + c) % 2, (y + 1 - c) % 2


def _first_targets(x, y, c):
    return [(x, y, 1 - c), (*_first_axis_chip(x, y, c), c)]


def _all_gather_small(shard, name):
    def body(in_ref, out_ref, send_sems, recv_sems, local_sem):
        x, y, c = _place()
        me, sibling = 4 * x + 2 * y + c, (x, y, 1 - c)
        first, second = _first_axis_chip(x, y, c), _second_axis_chip(x, y, c)

        def pair(chip):
            return out_ref.at[pl.ds(2 * (2 * chip[0] + chip[1]), 2)]

        def exchange(k, src, dst, to):
            cp = pltpu.make_async_remote_copy(src_ref=src, dst_ref=dst, send_sem=send_sems.at[k],
                                              recv_sem=recv_sems.at[k], device_id=to, device_id_type=MESH)
            cp.start()
            cp.wait()

        own = pltpu.make_async_copy(in_ref, out_ref.at[me], local_sem)
        own.start()
        exchange(0, in_ref, out_ref.at[me], sibling)
        own.wait()
        exchange(1, pair((x, y)), pair((x, y)), (*second, c))
        exchange(2, pair(second), pair(second), sibling)
        exchange(3, pair(first), pair(first), (*second, c))

    spec = pl.BlockSpec(memory_space=pltpu.VMEM)
    return pl.pallas_call(
        body, name=name, out_shape=jax.ShapeDtypeStruct((N_DEV,) + shard.shape, shard.dtype),
        in_specs=[spec], out_specs=spec,
        scratch_shapes=[pltpu.SemaphoreType.DMA((4,)), pltpu.SemaphoreType.DMA((4,)), pltpu.SemaphoreType.DMA],
        compiler_params=_params(),
    )(shard)


def _slot_copies(refs, send_sems, recv_sems, plan):
    copies = []
    for k, ((px, py, pc), to) in enumerate(plan):
        blk = refs[0].at[4 * px + 2 * py + pc]
        copies.append(pltpu.make_async_remote_copy(
            src_ref=blk, dst_ref=blk, send_sem=send_sems.at[k], recv_sem=recv_sems.at[k],
            device_id=to, device_id_type=MESH))
    return copies


def _second_axis_stage_copies(refs, send_sems, recv_sems):
    x, y, c = _place()
    first, second = (*_first_axis_chip(x, y, c), c), (*_second_axis_chip(x, y, c), c)
    return _slot_copies(refs, send_sems, recv_sems, [((x, y, c), second), (first, (x, y, 1 - c)), (first, second)])


def _second_axis_forward_copies(refs, send_sems, recv_sems):
    x, y, c = _place()
    return _slot_copies(refs, send_sems, recv_sems, [((*_second_axis_chip(x, y, c), c), (x, y, 1 - c))])


def _diagonal_forward_copies(refs, send_sems, recv_sems):
    x, y, c = _place()
    blk = refs[0].at[4 * (1 - x) + 2 * (1 - y) + c]
    return [pltpu.make_async_remote_copy(
        src_ref=blk, dst_ref=blk, send_sem=send_sems.at[0], recv_sem=recv_sems.at[0],
        device_id=(x, y, 1 - c), device_id_type=MESH)]


def _with_own_slot(block, me):
    return lax.dynamic_update_index_in_dim(lax.empty((N_DEV,) + block.shape, block.dtype), block, me, 0)


def _matmul(a, b, dims, out_dtype, tm, tn, name, dep=None):
    if dims == "nn":
        (m, k), n = a.shape, b.shape[1]
        a_spec = pl.BlockSpec((tm, k), lambda i, j: (i, 0))
        b_spec = pl.BlockSpec((k, tn), lambda i, j: (0, j))
        contract = ((1,), (0,))
    elif dims == "nt":
        (m, k), n = a.shape, b.shape[0]
        a_spec = pl.BlockSpec((tm, k), lambda i, j: (i, 0))
        b_spec = pl.BlockSpec((tn, k), lambda i, j: (j, 0))
        contract = ((1,), (1,))
    else:
        (k, m), n = a.shape, b.shape[1]
        a_spec = pl.BlockSpec((k, tm), lambda i, j: (0, i))
        b_spec = pl.BlockSpec((k, tn), lambda i, j: (0, j))
        contract = ((0,), (0,))
    assert m % tm == 0 and n % tn == 0 and a.dtype == BF16 and b.dtype == BF16

    def body(a_ref, b_ref, *rest):
        rest[-1][...] = lax.dot_general(a_ref[...], b_ref[...], (contract, ((), ())),
                                        preferred_element_type=F32).astype(out_dtype)

    deps = [] if dep is None else [dep]
    return pl.pallas_call(
        body, name=name, grid=(m // tm, n // tn),
        in_specs=[a_spec, b_spec] + [pl.BlockSpec((8, 128), lambda i, j: (0, 0))] * len(deps),
        out_specs=pl.BlockSpec((tm, tn), lambda i, j: (i, j)),
        out_shape=jax.ShapeDtypeStruct((m, n), out_dtype),
        compiler_params=_params(dimension_semantics=("arbitrary", "arbitrary")),
    )(a, b, *deps)


Z_TILE = 768
_Z_TILE_ORDER = ((0, 1, 2, 3, 4, 5, 6), (2, 0, 1, 6, 3, 4, 5), (4, 0, 5, 6, 1, 2, 3), (6, 2, 3, 4, 0, 1, 5))
_Z_EARLY_TILES = 4


def _z_proj(h, w_in_t, chip, first, count, z_prev, name, dep=None):
    t = h.shape[0]

    def body(chip_ref, h_ref, w_ref, *rest):
        rest[-1][...] = _dot_nt(h_ref[...], w_ref[...])

    def tile(j, chip_ref):
        picked = 0
        for c, order in enumerate(_Z_TILE_ORDER):
            for k in range(count):
                picked = picked + jnp.where((chip_ref[0] == c) & (j == k), order[first + k], 0)
        return picked

    prev = [] if z_prev is None else [z_prev]
    deps = [] if dep is None else [dep]
    return pl.pallas_call(
        body, name=name,
        grid_spec=pltpu.PrefetchScalarGridSpec(
            num_scalar_prefetch=1, grid=(count,),
            in_specs=[pl.BlockSpec((t, D_MODEL), lambda j, o: (0, 0)),
                      pl.BlockSpec((Z_TILE, D_MODEL), lambda j, o: (tile(j, o), 0))]
            + [pl.BlockSpec(memory_space=pl.ANY)] * len(prev)
            + [pl.BlockSpec((8, 128), lambda j, o: (0, 0))] * len(deps),
            out_specs=pl.BlockSpec((t, Z_TILE), lambda j, o: (0, tile(j, o)))),
        out_shape=jax.ShapeDtypeStruct((t, D_IN), F32),
        input_output_aliases={3: 0} if prev else {},
        compiler_params=_params(dimension_semantics=("arbitrary",)),
    )(chip, h, w_in_t, *prev, *deps)


def _modulation(device, c_all, w_ada, b_ada):
    def body(device_ref, c_ref, w_ref, b_ref, act_ref, mod_ref):
        cv = c_ref[...]
        act = cv * _sigmoid(cv)
        act_ref[...] = act
        mod_ref[...] = jnp.dot(act.astype(BF16), w_ref[...].astype(BF16), preferred_element_type=F32) + b_ref[...]

    whole = lambda a: pl.BlockSpec(a.shape, lambda i, device_ref: (0,) * a.ndim)
    return pl.pallas_call(
        body, name="modulation",
        grid_spec=pltpu.PrefetchScalarGridSpec(
            num_scalar_prefetch=1, grid=(1,),
            in_specs=[whole(c_all), whole(w_ada), pl.BlockSpec((1, W_ADA_SHARD), lambda i, device_ref: (0, device_ref[0]))],
            out_specs=(whole(c_all), pl.BlockSpec((N_DEV, W_ADA_SHARD), lambda i, device_ref: (0, 0)))),
        out_shape=(jax.ShapeDtypeStruct(c_all.shape, F32), jax.ShapeDtypeStruct((N_DEV, W_ADA_SHARD), F32)),
        compiler_params=_params(dimension_semantics=("arbitrary",)),
    )(device, c_all, w_ada, b_ada)


MOD_SHIFT, MOD_SCALE, MOD_GATE = 0, 1, 2


def _mod_spec(part, d):
    return pl.BlockSpec((1, d), lambda i: (0, part))


def _modulated_norm(x, norm_g, mod, tm=512):
    t, d = x.shape

    def body(x_ref, g_ref, sc_ref, sh_ref, h_ref):
        xv = x_ref[...]
        r = lax.rsqrt(jnp.mean(xv * xv, axis=-1, keepdims=True) + EPS)
        h = (xv * r) * g_ref[...] * (1.0 + sc_ref[...]) + sh_ref[...]
        h_ref[...] = h.astype(BF16)

    row = pl.BlockSpec((1, d), lambda i: (0, 0))
    return pl.pallas_call(
        body, name="modulated_norm", grid=(t // tm,),
        in_specs=[pl.BlockSpec((tm, d), lambda i: (i, 0)), row, _mod_spec(MOD_SCALE, d), _mod_spec(MOD_SHIFT, d)],
        out_specs=pl.BlockSpec((tm, d), lambda i: (i, 0)),
        out_shape=jax.ShapeDtypeStruct((t, d), BF16),
        compiler_params=_params(dimension_semantics=("arbitrary",)),
    )(x, norm_g, mod, mod)


def _window_bias(block_index):
    s = lax.broadcasted_iota(jnp.int32, (2 * BLOCK, BLOCK), 0)
    t = lax.broadcasted_iota(jnp.int32, (2 * BLOCK, BLOCK), 1)
    valid = ((s < BLOCK) & (s > t) & (block_index > 0)) | ((s >= BLOCK) & ((s - BLOCK) <= t))
    bias = jnp.where(valid, 0.0, -jnp.inf).astype(F32)
    return jnp.concatenate([bias] * 8, axis=1)


def _heads_t(pair_blocks, g):
    top = lax.broadcasted_iota(jnp.int32, (BLOCK, BLOCK), 0) < HEAD_DIM
    zeros = jnp.zeros((HEAD_DIM, BLOCK), F32)
    tiles = []
    for blk in pair_blocks:
        tp = blk.T
        if g == 0:
            tiles += [jnp.where(top, tp, 0.0), jnp.concatenate([tp[HEAD_DIM:], zeros], axis=0)]
        else:
            tiles += [jnp.concatenate([zeros, tp[:HEAD_DIM]], axis=0), jnp.where(top, 0.0, tp)]
    return jnp.concatenate(tiles, axis=1)


def _pair_block(xt, p, g):
    r0 = HEAD_DIM * g
    even = xt[r0:r0 + HEAD_DIM, (2 * p) * BLOCK:(2 * p + 1) * BLOCK]
    odd = xt[r0:r0 + HEAD_DIM, (2 * p + 1) * BLOCK:(2 * p + 2) * BLOCK]
    return jnp.concatenate([even, odd], axis=0).T


def _softmax_t(scores_t, bias, sink):
    st = scores_t + bias
    m = jnp.maximum(jnp.max(st, axis=0, keepdims=True), sink)
    e = jnp.exp(st - m)
    es = jnp.exp(sink - m)
    inv = 1.0 / (jnp.sum(e, axis=0, keepdims=True) + es)
    return e * inv, es * inv


def _dot(a, b):
    return jnp.dot(a, b, preferred_element_type=F32)


def _dot_nt(a, b):
    return lax.dot_general(a, b, (((1,), (1,)), ((), ())), preferred_element_type=F32)


def _layer_norm_fwd(v):
    mu = jnp.mean(v, axis=-1, keepdims=True)
    xc = v - mu
    rstd = lax.rsqrt(jnp.mean(xc * xc, axis=-1, keepdims=True) + EPS)
    return xc * rstd, rstd


def _tril(transposed=False):
    t = lax.broadcasted_iota(jnp.int32, (BLOCK, BLOCK), 0)
    s = lax.broadcasted_iota(jnp.int32, (BLOCK, BLOCK), 1)
    return s >= t if transposed else t >= s


def _const_spec(shape):
    return pl.BlockSpec(shape, lambda i: (0,) * len(shape))


def _keys_values(z_ref, kvp):
    kvc = z_ref[:, SEG_KV:SEG_KV + 2 * D_KV]
    kk = jnp.concatenate([kvp[:, :D_KV], kvc[:, :D_KV]], axis=0)
    vv = jnp.concatenate([kvp[:, D_KV:], kvc[:, D_KV:]], axis=0)
    return kk, vv


MIXER_BLOCKS = 2


class _Rows:
    def __init__(self, ref, sub):
        self.ref, self.rows = ref, slice(sub * BLOCK, (sub + 1) * BLOCK)

    def __getitem__(self, idx):
        return self.ref[self.rows, idx[1]]

    def __setitem__(self, idx, value):
        self.ref[self.rows, idx[1]] = value


def _kv_before_spec(index):
    return pl.BlockSpec((BLOCK, 2 * D_KV),
                        lambda i: (jnp.maximum(MIXER_BLOCKS * index(i) - 1, 0), SEG_KV // (2 * D_KV)))


def _pair_cols(g, p, base=0):
    return slice(base + (4 * g + p) * 128, base + (4 * g + p + 1) * 128)


def _mixer_fwd(z, sink_rows, ln_g, ln_b, sgu_w, sgu_bt):
    t = z.shape[0]

    def body(z_all, kvp_ref, sink_ref, lng_ref, lnb_ref, w_ref, bt_ref, a_all, prob_ref, sink_prob_ref):
        kv_before = kvp_ref[...]
        for sub in range(MIXER_BLOCKS):
            z_ref, a_ref = _Rows(z_all, sub), _Rows(a_all, sub)
            one_block(z_ref, kv_before, MIXER_BLOCKS * pl.program_id(0) + sub, sink_ref, lng_ref, lnb_ref, w_ref,
                      bt_ref, a_ref, prob_ref.at[sub], sink_prob_ref.at[sub])
            kv_before = z_ref[:, SEG_KV:SEG_KV + 2 * D_KV]

    def one_block(z_ref, kv_before, block_index, sink_ref, lng_ref, lnb_ref, w_ref, bt_ref, a_ref, prob_ref,
                  sink_prob_ref):
        bias = _window_bias(block_index)
        kk, vv = _keys_values(z_ref, kv_before)
        kk_b, vvt_b = kk.astype(BF16), vv.T.astype(BF16)
        for g in range(2):
            qt = _heads_t([z_ref[:, _pair_cols(g, p, SEG_Q)] * ATTN_SCALE for p in range(4)], g).astype(BF16)
            prob, sink_prob = _softmax_t(_dot(kk_b, qt), bias, sink_ref[g])
            prob_b = prob.astype(BF16)
            prob_ref[g] = prob_b
            sink_prob_ref[g] = sink_prob
            ot = _dot(vvt_b, prob_b)
            for p in range(4):
                gate = z_ref[:, _pair_cols(g, p, SEG_GA)]
                a_ref[:, _pair_cols(g, p)] = (_pair_block(ot, p, g) * (gate * _sigmoid(gate))).astype(BF16)

        vhat, _ = _layer_norm_fwd(z_ref[:, SEG_VS:SEG_VS + D_SGU])
        vn = vhat * lng_ref[...] + lnb_ref[...]
        tril = _tril()
        for g in range(SGU_GROUPS):
            cols = slice(g * 128, (g + 1) * 128)
            wm = jnp.where(tril, w_ref[g], 0.0).astype(BF16)
            mixed = _dot(wm, vn[:, cols].astype(BF16)) + bt_ref[:, g:g + 1]
            gate = z_ref[:, SEG_GS + g * 128:SEG_GS + (g + 1) * 128]
            a_ref[:, D_ATTN + g * 128:D_ATTN + (g + 1) * 128] = (
                (z_ref[:, SEG_U + g * 128:SEG_U + (g + 1) * 128] * mixed) * (gate * _sigmoid(gate))).astype(BF16)

    rows = MIXER_BLOCKS * BLOCK
    return pl.pallas_call(
        body, name="mixer_fwd", grid=(t // rows,),
        in_specs=[pl.BlockSpec((rows, D_IN), lambda i: (i, 0)), _kv_before_spec(lambda i: i),
                  _const_spec((2, 1, 8 * BLOCK)), _const_spec((1, D_SGU)), _const_spec((1, D_SGU)),
                  _const_spec((SGU_GROUPS, BLOCK, BLOCK)), _const_spec((BLOCK, SGU_GROUPS))],
        out_specs=(pl.BlockSpec((rows, D_MODEL), lambda i: (i, 0)),
                   pl.BlockSpec((MIXER_BLOCKS, 2, 2 * BLOCK, 8 * BLOCK), lambda i: (i, 0, 0, 0)),
                   pl.BlockSpec((MIXER_BLOCKS, 2, 1, 8 * BLOCK), lambda i: (i, 0, 0, 0))),
        out_shape=(jax.ShapeDtypeStruct((t, D_MODEL), BF16),
                   jax.ShapeDtypeStruct((t // BLOCK, 2, 2 * BLOCK, 8 * BLOCK), BF16),
                   jax.ShapeDtypeStruct((t // BLOCK, 2, 1, 8 * BLOCK), F32)),
        compiler_params=_params(dimension_semantics=("arbitrary",)),
    )(z, z, sink_rows, ln_g, ln_b, sgu_w, sgu_bt)


def _mixer_bwd(z, da, probs, sink_probs, ln_g, ln_b, sgu_w, sgu_wt, sgu_bt):
    t = z.shape[0]

    def body(z_all, kvp_ref, da_all, prob_ref, sink_prob_ref, lng_ref, lnb_ref, w_ref, wt_ref, bt_ref,
             dz_all, dsink_ref, dw_ref, db_ref, dlng_ref, dlnb_ref, carry_ref, dsink_acc, dbt_acc):
        step = pl.program_id(0)

        @pl.when(step == 0)
        def _():
            carry_ref[...] = jnp.zeros_like(carry_ref)
            dsink_acc[...] = jnp.zeros_like(dsink_acc)
            dbt_acc[...] = jnp.zeros_like(dbt_acc)
            dw_ref[...] = jnp.zeros_like(dw_ref)
            dlng_ref[...] = jnp.zeros_like(dlng_ref)
            dlnb_ref[...] = jnp.zeros_like(dlnb_ref)

        carry = carry_ref[...]
        for sub in reversed(range(MIXER_BLOCKS)):
            kv_before = kvp_ref[...] if sub == 0 else _Rows(z_all, sub - 1)[:, SEG_KV:SEG_KV + 2 * D_KV]
            carry = one_block(_Rows(z_all, sub), kv_before, _Rows(da_all, sub), prob_ref.at[sub], sink_prob_ref.at[sub],
                              carry, lng_ref, lnb_ref, w_ref, wt_ref, bt_ref, _Rows(dz_all, sub),
                              dw_ref, dlng_ref, dlnb_ref, dsink_acc, dbt_acc)
        carry_ref[...] = carry

        @pl.when(step == ns - 1)
        def _():
            db_ref[...] = dbt_acc[...].T[:SGU_GROUPS]
            lane_row = lax.broadcasted_iota(jnp.int32, (1, 128), 1)
            d_sink = jnp.zeros((1, 128), F32)
            for g in range(2):
                acc = dsink_acc[g]
                for j in range(8):
                    head_sum = jnp.sum(acc[:, j * BLOCK:(j + 1) * BLOCK], axis=-1, keepdims=True)
                    d_sink = d_sink + jnp.where(lane_row == 8 * g + j, head_sum, 0.0)
            dsink_ref[...] = d_sink

    def one_block(z_ref, kv_before, da_ref, prob_ref, sink_prob_ref, carry, lng_ref, lnb_ref, w_ref, wt_ref, bt_ref,
                  dz_ref, dw_ref, dlng_ref, dlnb_ref, dsink_acc, dbt_acc):
        kk, vv = _keys_values(z_ref, kv_before)
        vv_b = vv.astype(BF16)
        kkt_b, vvt_b = kk.T.astype(BF16), vv.T.astype(BF16)
        dkk = jnp.zeros((2 * BLOCK, D_KV), F32)
        dvv = jnp.zeros((2 * BLOCK, D_KV), F32)
        for g in range(2):
            qt = _heads_t([z_ref[:, _pair_cols(g, p, SEG_Q)] * ATTN_SCALE for p in range(4)], g).astype(BF16)
            prob_b, sink_prob = prob_ref[g], sink_prob_ref[g]
            prob = prob_b.astype(F32)
            ot = _dot(vvt_b, prob_b)
            gates = [z_ref[:, _pair_cols(g, p, SEG_GA)] for p in range(4)]
            sig = [_sigmoid(gt) for gt in gates]
            d_attn = [da_ref[:, _pair_cols(g, p)] for p in range(4)]
            d_ot = _heads_t([d_attn[p] * (gates[p] * sig[p]) for p in range(4)], g).astype(BF16)
            d_prob = _dot(vv_b, d_ot)
            delta = jnp.sum(prob * d_prob, axis=0, keepdims=True)
            d_scores = (prob * (d_prob - delta)).astype(BF16)
            dsink_acc[g] -= sink_prob * delta
            d_qt = _dot(kkt_b, d_scores)
            dkk = dkk + _dot_nt(d_scores, qt)
            dvv = dvv + _dot_nt(prob_b, d_ot)
            for p in range(4):
                dz_ref[:, _pair_cols(g, p, SEG_Q)] = (_pair_block(d_qt, p, g) * ATTN_SCALE).astype(BF16)
                d_silu = sig[p] * (1.0 + gates[p] * (1.0 - sig[p]))
                dz_ref[:, _pair_cols(g, p, SEG_GA)] = (d_attn[p] * _pair_block(ot, p, g) * d_silu).astype(BF16)
        d_kv = jnp.concatenate([dkk, dvv], axis=1)
        dz_ref[:, SEG_KV:SEG_KV + 2 * D_KV] = (d_kv[BLOCK:] + carry).astype(BF16)

        vhat, rstd = _layer_norm_fwd(z_ref[:, SEG_VS:SEG_VS + D_SGU])
        lng = lng_ref[...]
        vn = vhat * lng + lnb_ref[...]
        tril, triu = _tril(), _tril(transposed=True)
        lane = lax.broadcasted_iota(jnp.int32, (BLOCK, 128), 1)
        d_bt = jnp.zeros((BLOCK, 128), F32)
        d_vn = []
        for g in range(SGU_GROUPS):
            cols = slice(g * 128, (g + 1) * 128)
            wm = jnp.where(tril, w_ref[g], 0.0).astype(BF16)
            wmt = jnp.where(triu, wt_ref[g], 0.0).astype(BF16)
            vn_g = vn[:, cols].astype(BF16)
            mixed = _dot(wm, vn_g) + bt_ref[:, g:g + 1]
            gate = z_ref[:, SEG_GS + g * 128:SEG_GS + (g + 1) * 128]
            u = z_ref[:, SEG_U + g * 128:SEG_U + (g + 1) * 128]
            d_out = da_ref[:, D_ATTN + g * 128:D_ATTN + (g + 1) * 128]
            sg = _sigmoid(gate)
            d_um = d_out * (gate * sg)
            dz_ref[:, SEG_U + g * 128:SEG_U + (g + 1) * 128] = (d_um * mixed).astype(BF16)
            dz_ref[:, SEG_GS + g * 128:SEG_GS + (g + 1) * 128] = (
                d_out * (u * mixed) * (sg * (1.0 + gate * (1.0 - sg)))).astype(BF16)
            d_mixed = d_um * u
            d_mixed_b = d_mixed.astype(BF16)
            dw_ref[g] += jnp.where(tril, _dot_nt(d_mixed_b, vn_g), 0.0)
            d_bt = d_bt + jnp.where(lane == g, jnp.sum(d_mixed, axis=-1, keepdims=True), 0.0)
            d_vn.append(_dot(wmt, d_mixed_b))
        dbt_acc[...] += d_bt
        d_vn = jnp.concatenate(d_vn, axis=1)
        dlng_ref[...] += jnp.sum(d_vn * vhat, axis=0, keepdims=True)
        dlnb_ref[...] += jnp.sum(d_vn, axis=0, keepdims=True)
        d_vhat = d_vn * lng
        d_v = rstd * (d_vhat - jnp.mean(d_vhat, axis=-1, keepdims=True)
                      - vhat * jnp.mean(d_vhat * vhat, axis=-1, keepdims=True))
        dz_ref[:, SEG_VS:SEG_VS + D_SGU] = d_v.astype(BF16)
        return d_kv[:BLOCK]

    rows = MIXER_BLOCKS * BLOCK
    ns = t // rows
    rev = lambda i: ns - 1 - i
    return pl.pallas_call(
        body, name="mixer_bwd", grid=(ns,),
        in_specs=[pl.BlockSpec((rows, D_IN), lambda i: (rev(i), 0)), _kv_before_spec(rev),
                  pl.BlockSpec((rows, D_MODEL), lambda i: (rev(i), 0)),
                  pl.BlockSpec((MIXER_BLOCKS, 2, 2 * BLOCK, 8 * BLOCK), lambda i: (rev(i), 0, 0, 0)),
                  pl.BlockSpec((MIXER_BLOCKS, 2, 1, 8 * BLOCK), lambda i: (rev(i), 0, 0, 0)),
                  _const_spec((1, D_SGU)), _const_spec((1, D_SGU)),
                  _const_spec((SGU_GROUPS, BLOCK, BLOCK)), _const_spec((SGU_GROUPS, BLOCK, BLOCK)),
                  _const_spec((BLOCK, SGU_GROUPS))],
        out_specs=(pl.BlockSpec((rows, D_IN), lambda i: (rev(i), 0)), _const_spec((1, 128)),
                   _const_spec((SGU_GROUPS, BLOCK, BLOCK)), _const_spec((SGU_GROUPS, BLOCK)),
                   _const_spec((1, D_SGU)), _const_spec((1, D_SGU))),
        out_shape=(jax.ShapeDtypeStruct((t, D_IN), BF16), jax.ShapeDtypeStruct((1, 128), F32),
                   jax.ShapeDtypeStruct((SGU_GROUPS, BLOCK, BLOCK), F32), jax.ShapeDtypeStruct((SGU_GROUPS, BLOCK), F32),
                   jax.ShapeDtypeStruct((1, D_SGU), F32), jax.ShapeDtypeStruct((1, D_SGU), F32)),
        scratch_shapes=[pltpu.VMEM((BLOCK, 2 * D_KV), F32), pltpu.VMEM((2, 1, 8 * BLOCK), F32),
                        pltpu.VMEM((BLOCK, 128), F32)],
        compiler_params=_params(dimension_semantics=("arbitrary",)),
    )(z, z, da, probs, sink_probs, ln_g, ln_b, sgu_w, sgu_wt, sgu_bt)


def _out_proj_head(a, w_out_full, x, target, mod, final_g, tm=256):
    t, d = x.shape

    def body(a_ref, w_ref, x_ref, tg_ref, gate_ref, fg_ref, dx2_ref, dy_ref, loss_ref, dfg_ref, dgate_ref):
        @pl.when(pl.program_id(0) == 0)
        def _():
            loss_ref[...] = jnp.zeros_like(loss_ref)
            dfg_ref[...] = jnp.zeros_like(dfg_ref)
            dgate_ref[...] = jnp.zeros_like(dgate_ref)

        yv, gate, fg = _dot(a_ref[...], w_ref[...]), gate_ref[...], fg_ref[...]
        x2 = x_ref[...] + gate * yv
        r2 = lax.rsqrt(jnp.mean(x2 * x2, axis=-1, keepdims=True) + EPS)
        nrm = x2 * r2
        err = nrm * fg - tg_ref[...]
        loss_ref[...] += 0.5 * jnp.sum(jnp.mean(err * err, axis=-1, keepdims=True), axis=0, keepdims=True)
        fg_d = fg * (1.0 / d)
        err_nrm = err * nrm
        dfg_ref[...] += jnp.sum(err_nrm, axis=0, keepdims=True) * (1.0 / d)
        d_nrm = err * fg_d
        dx2 = r2 * (d_nrm - nrm * jnp.mean(err_nrm * fg_d, axis=-1, keepdims=True))
        dx2_ref[...] = dx2
        dgate_ref[...] += jnp.sum(dx2 * yv, axis=0, keepdims=True)
        dy_ref[...] = (dx2 * gate).astype(BF16)

    blk = pl.BlockSpec((tm, d), lambda i: (i, 0))
    row = _const_spec((1, d))
    whole = pl.BlockSpec(w_out_full.shape, lambda i: (0, 0), pipeline_mode=pl.Buffered(1))
    return pl.pallas_call(
        body, name="out_proj_head", grid=(t // tm,),
        in_specs=[pl.BlockSpec((tm, a.shape[1]), lambda i: (i, 0)), whole, blk, blk, _mod_spec(MOD_GATE, d), row],
        out_specs=(blk, blk, _const_spec((1, 128)), row, row),
        out_shape=(jax.ShapeDtypeStruct((t, d), F32), jax.ShapeDtypeStruct((t, d), BF16),
                   jax.ShapeDtypeStruct((1, 128), F32), jax.ShapeDtypeStruct((1, d), F32),
                   jax.ShapeDtypeStruct((1, d), F32)),
        compiler_params=_params(dimension_semantics=("arbitrary",)),
    )(a, w_out_full, x, target, mod, final_g)


def _z_proj_bwd_norm(dz, w_in_t, x, dx2, norm_g, mod, dep, tm=256):
    t, d = x.shape

    def body(dz_ref, w_ref, x_ref, dx2_ref, g_ref, sc_ref, dep_ref, gx_ref, dshift_ref, dscale_ref, dg_ref):
        @pl.when(pl.program_id(0) == 0)
        def _():
            dshift_ref[...] = jnp.zeros_like(dshift_ref)
            dscale_ref[...] = jnp.zeros_like(dscale_ref)
            dg_ref[...] = jnp.zeros_like(dg_ref)

        dh, xv, g = _dot(dz_ref[...], w_ref[...]), x_ref[...], g_ref[...]
        one_plus = 1.0 + sc_ref[...]
        r = lax.rsqrt(jnp.mean(xv * xv, axis=-1, keepdims=True) + EPS)
        xn = xv * r
        gain = one_plus * g
        dh_xn = dh * xn
        dh_xn_sum = jnp.sum(dh_xn, axis=0, keepdims=True)
        dshift_ref[...] += jnp.sum(dh, axis=0, keepdims=True)
        dscale_ref[...] += dh_xn_sum * g
        dg_ref[...] += dh_xn_sum * one_plus
        d_xn = dh * gain
        gx_ref[...] = dx2_ref[...] + r * (d_xn - xn * jnp.mean(dh_xn * gain, axis=-1, keepdims=True))

    blk = pl.BlockSpec((tm, d), lambda i: (i, 0))
    row = _const_spec((1, d))
    whole = pl.BlockSpec(w_in_t.shape, lambda i: (0, 0), pipeline_mode=pl.Buffered(1))
    return pl.pallas_call(
        body, name="z_proj_bwd_norm", grid=(t // tm,),
        in_specs=[pl.BlockSpec((tm, dz.shape[1]), lambda i: (i, 0)), whole, blk, blk, row, _mod_spec(MOD_SCALE, d),
                  _const_spec((8, 128))],
        out_specs=(blk, row, row, row),
        out_shape=(jax.ShapeDtypeStruct((t, d), F32),) + (jax.ShapeDtypeStruct((1, d), F32),) * 3,
        compiler_params=_params(dimension_semantics=("arbitrary",)),
    )(dz, w_in_t, x, dx2, norm_g, mod, dep)


def _adamw(w, g, m, v):
    m = ADAM_B1 * m + (1.0 - ADAM_B1) * g
    v = ADAM_B2 * v + (1.0 - ADAM_B2) * (g * g)
    m_hat = m / (1.0 - ADAM_B1 ** ADAM_STEP)
    v_hat = v / (1.0 - ADAM_B2 ** ADAM_STEP)
    delta = -ADAM_LR * (m_hat / (jnp.sqrt(v_hat) + ADAM_EPS) + ADAM_WD * w)
    return delta, m, v


def _relay_sum(second_chip, pair, land, tr):
    _, r, c = pair.shape

    def body(chip_ref, a_ref, b_ref, o_ref):
        o_ref[...] = (a_ref[...].astype(F32) + b_ref[...].astype(F32)).astype(BF16)

    return pl.pallas_call(
        body, name="w_in_grad_relay_sum",
        grid_spec=pltpu.PrefetchScalarGridSpec(
            num_scalar_prefetch=1, grid=(r // tr,),
            in_specs=[pl.BlockSpec((None, tr, c), lambda i, chip_ref: (chip_ref[0], i, 0)),
                      pl.BlockSpec((None, tr, c), lambda i, chip_ref: (1, i, 0))],
            out_specs=pl.BlockSpec((tr, c), lambda i, chip_ref: (i, 0))),
        out_shape=jax.ShapeDtypeStruct((r, c), BF16),
        compiler_params=_params(dimension_semantics=("arbitrary",)),
    )(second_chip, pair, land)


def _adam_from_chips(chip, pair, landed, w, m, v, name, tc):
    _, r, c = pair.shape
    n = len(landed)

    def body(chip_ref, own_ref, *refs):
        w_ref, m_ref, v_ref, g_ref, d_ref, nm_ref, nv_ref = refs[n:]
        g = own_ref[...].astype(F32)
        for k in range(n):
            g = g + refs[k][...].astype(F32)
        g_ref[...] = g
        d_ref[...], nm_ref[...], nv_ref[...] = _adamw(w_ref[...], g, m_ref[...], v_ref[...])

    def landed_spec(index):
        return pl.BlockSpec((None, r, tc), lambda i, chip_ref: (index, 0, i))

    blk = pl.BlockSpec((r, tc), lambda i, chip_ref: (0, i))
    return pl.pallas_call(
        body, name=name,
        grid_spec=pltpu.PrefetchScalarGridSpec(
            num_scalar_prefetch=1, grid=(c // tc,),
            in_specs=[pl.BlockSpec((None, r, tc), lambda i, chip_ref: (chip_ref[0], 0, i))]
            + [landed_spec(index) for _, index in landed] + [blk, blk, blk],
            out_specs=(blk,) * 4),
        out_shape=(jax.ShapeDtypeStruct((r, c), F32),) * 4,
        compiler_params=_params(dimension_semantics=("arbitrary",)),
    )(chip, pair, *[array for array, _ in landed], w, m, v)


def _adam_w_ada(device, act_t, dmod_all, w, m, v, tr=512):
    r, c = w.shape

    def body(device_ref, a_ref, dm_ref, w_ref, m_ref, v_ref, g_ref, d_ref, nm_ref, nv_ref):
        g = _dot(a_ref[...].astype(BF16), dm_ref[...].astype(BF16))
        g_ref[...] = g
        d_ref[...], nm_ref[...], nv_ref[...] = _adamw(w_ref[...], g, m_ref[...], v_ref[...])

    blk = pl.BlockSpec((tr, c), lambda i, device_ref: (i, 0))
    return pl.pallas_call(
        body, name="adam_w_ada",
        grid_spec=pltpu.PrefetchScalarGridSpec(
            num_scalar_prefetch=1, grid=(r // tr,),
            in_specs=[pl.BlockSpec((tr, N_DEV), lambda i, device_ref: (i, 0)),
                      pl.BlockSpec((N_DEV, c), lambda i, device_ref: (0, device_ref[0])), blk, blk, blk],
            out_specs=(blk,) * 4),
        out_shape=(jax.ShapeDtypeStruct((r, c), F32),) * 4,
        compiler_params=_params(dimension_semantics=("arbitrary",)),
    )(device, act_t, dmod_all, w, m, v)


def _pack_small(d_shift, d_scale, d_gate, d_norm_g, d_final_g, d_ln_g, d_ln_b, loss, d_sinks, d_sgu_b):
    def body(shift_ref, scale_ref, gate_ref, ng_ref, fg_ref, lng_ref, lnb_ref, loss_ref, sink_ref, b_ref, o_ref):
        o_ref[...] = jnp.zeros_like(o_ref)
        o_ref[ROW_SHIFT:ROW_SHIFT + 1, :] = shift_ref[...]
        o_ref[ROW_SCALE:ROW_SCALE + 1, :] = scale_ref[...]
        o_ref[ROW_GATE:ROW_GATE + 1, :] = gate_ref[...]
        o_ref[ROW_NORM_G:ROW_NORM_G + 1, :] = ng_ref[...]
        o_ref[ROW_FINAL_G:ROW_FINAL_G + 1, :] = fg_ref[...]
        o_ref[ROW_LN:ROW_LN + 1, 0:D_SGU] = lng_ref[...]
        o_ref[ROW_LN:ROW_LN + 1, D_SGU:2 * D_SGU] = lnb_ref[...]
        o_ref[ROW_MISC:ROW_MISC + 1, 0:128] = loss_ref[...]
        o_ref[ROW_MISC:ROW_MISC + 1, 128:256] = sink_ref[...]
        o_ref[ROW_SGU_B:ROW_SGU_B + SGU_GROUPS, 0:BLOCK] = b_ref[...]

    return pl.pallas_call(
        body, name="pack_small", out_shape=jax.ShapeDtypeStruct((SMALL_ROWS, D_MODEL), F32),
        compiler_params=_params(),
    )(d_shift, d_scale, d_gate, d_norm_g, d_final_g, d_ln_g, d_ln_b, loss, d_sinks, d_sgu_b)


_SMALL_NAMES = ("norm_g", "b_ada", "attn_sinks", "sgu_ln_g", "sgu_ln_b", "sgu_w", "sgu_b", "final_g")


def _adam_small(partials, d_sgu_w_all, weights, moments_m, moments_v):
    names = _SMALL_NAMES
    k = len(names)

    def body(*refs):
        p_ref, sw_ref = refs[0], refs[1]
        w_refs, m_refs, v_refs = refs[2:2 + k], refs[2 + k:2 + 2 * k], refs[2 + 2 * k:2 + 3 * k]
        loss_ref, dmod_ref = refs[2 + 3 * k], refs[3 + 3 * k]
        out_refs = refs[4 + 3 * k:4 + 7 * k]
        sum_ref = refs[4 + 7 * k]
        total = p_ref[0]
        for j in range(1, N_DEV):
            total = total + p_ref[j]
        sum_ref[...] = total
        for j in range(N_DEV):
            for part, row in enumerate((ROW_SHIFT, ROW_SCALE, ROW_GATE)):
                dmod_ref[j:j + 1, part * D_MODEL:(part + 1) * D_MODEL] = p_ref[j, row:row + 1, :]
        loss_ref[...] = sum_ref[ROW_MISC:ROW_MISC + 1, 0:1]
        d_sgu_w = sw_ref[0]
        for j in range(1, N_DEV):
            d_sgu_w = d_sgu_w + sw_ref[j]
        grads = {
            "norm_g": sum_ref[ROW_NORM_G:ROW_NORM_G + 1, :],
            "b_ada": jnp.concatenate([sum_ref[r:r + 1, :] for r in (ROW_SHIFT, ROW_SCALE, ROW_GATE)], axis=1),
            "attn_sinks": sum_ref[ROW_MISC:ROW_MISC + 1, 128:128 + N_Q_HEADS],
            "sgu_ln_g": sum_ref[ROW_LN:ROW_LN + 1, 0:D_SGU],
            "sgu_ln_b": sum_ref[ROW_LN:ROW_LN + 1, D_SGU:2 * D_SGU],
            "sgu_w": d_sgu_w[None],
            "sgu_b": sum_ref[ROW_SGU_B:ROW_SGU_B + SGU_GROUPS, 0:BLOCK][None],
            "final_g": sum_ref[ROW_FINAL_G:ROW_FINAL_G + 1, :],
        }
        for i, name in enumerate(names):
            g = grads[name]
            delta, m, v = _adamw(w_refs[i][...], g, m_refs[i][...], v_refs[i][...])
            out_refs[4 * i][...] = g
            out_refs[4 * i + 1][...] = delta
            out_refs[4 * i + 2][...] = m
            out_refs[4 * i + 3][...] = v

    shapes = [jax.ShapeDtypeStruct((1, 1), F32), jax.ShapeDtypeStruct((N_DEV, 3 * D_MODEL), F32)]
    for name in names:
        shapes += [jax.ShapeDtypeStruct(weights[name].shape, F32)] * 4
    outs = pl.pallas_call(
        body, name="adam_small", out_shape=tuple(shapes),
        scratch_shapes=[pltpu.VMEM((SMALL_ROWS, D_MODEL), F32)],
        compiler_params=_params(),
    )(partials, d_sgu_w_all, *[weights[n] for n in names], *[moments_m[n] for n in names],
      *[moments_v[n] for n in names])
    return outs[0], outs[1], {name: outs[2 + 4 * i:6 + 4 * i] for i, name in enumerate(names)}


def kernel(x, c, norm_g, w_ada, b_ada, w_in, attn_sinks, sgu_ln_g, sgu_ln_b, sgu_w, sgu_b, w_out, final_g, loss_target, m_norm_g, m_w_ada, m_b_ada, m_w_in, m_attn_sinks, m_sgu_ln_g, m_sgu_ln_b, m_sgu_w, m_sgu_b, m_w_out, m_final_g, v_norm_g, v_w_ada, v_b_ada, v_w_in, v_attn_sinks, v_sgu_ln_g, v_sgu_ln_b, v_sgu_w, v_sgu_b, v_w_out, v_final_g):
    xi, yi, ci = _place()
    me = 4 * xi + 2 * yi + ci
    x2d, target = x[0], loss_target[0]
    t = x2d.shape[0]

    core = ci.astype(jnp.int32).reshape(1)
    chip = (2 * xi + yi).astype(jnp.int32).reshape(1)

    first = _own_block_copies(_first_targets)
    first_flight = _start_copies([_with_own_slot(w_in[0].T.astype(BF16), me)], first, 2, core, "gather_w_in_start")

    c_all = _all_gather_small(c.reshape(8, 256) + first_flight[3][0, 0], "gather_c").reshape(N_DEV, D_MODEL)
    device = me.astype(jnp.int32).reshape(1)
    c_act, mod_part = _modulation(device, c_all, w_ada[0], b_ada)
    mod_all = _all_gather_small(mod_part, "gather_mod")

    across = _wait_then_start(first_flight, lambda *a: first(*a)[1:], _second_axis_stage_copies, 3, mod_all,
                              "gather_w_in_second_axis_stage")
    mod = lax.dynamic_index_in_dim(mod_all, me, axis=1, keepdims=False).reshape(1, 3 * D_MODEL)
    mod = mod + across[3][0, 0]
    h = _modulated_norm(x2d, norm_g, mod)

    w_in_pair = _wait_copies((first_flight[0], first_flight[1], across[2], None), lambda *a: first(*a)[:1], h,
                             "gather_w_in_sibling_wait")
    z_own = _z_proj(h, w_in_pair[0].reshape(D_IN, D_MODEL), chip, 0, 1, None, "z_proj_own")
    w_out_early = _own_block_copies(lambda x, y, c: [(x, y, 1 - c), (*_second_axis_chip(x, y, c), c)])
    w_out_late = _own_block_copies(lambda x, y, c: [(*_first_axis_chip(x, y, c), c), (1 - x, 1 - y, c)])
    forward = _wait_then_start(
        (across[0], across[1], w_in_pair, None), lambda *a: _second_axis_stage_copies(*a)[:1],
        lambda refs, s, r: _second_axis_forward_copies(refs[:1], s, r) + _group(w_out_early, 1, 1, 1)(refs, s, r),
        3, z_own, "gather_w_in_second_axis_forward", more_bufs=[_with_own_slot(w_out[0].astype(BF16), me)])
    w_in_most = _wait_copies((across[0], across[1], forward[2][:1], None),
                             lambda *a: _second_axis_stage_copies(*a)[1:2], z_own, "gather_w_in_first_forward_wait")
    w_in_most = _wait_copies((forward[0], forward[1], w_in_most, None), _second_axis_forward_copies, z_own,
                             "gather_w_in_second_forward_wait")
    z_early = _z_proj(h, w_in_most[0].reshape(D_IN, D_MODEL), chip, 1, _Z_EARLY_TILES - 1, z_own, "z_proj_early")
    last = _wait_then_start(
        (across[0], across[1], [w_in_most[0], forward[2][1]], None), lambda *a: _second_axis_stage_copies(*a)[2:],
        lambda refs, s, r: _diagonal_forward_copies(refs[:1], s, r) + _group(w_out_late, 1, 1, 1)(refs, s, r),
        3, z_early, "gather_w_in_last_stage")
    w_in_all = _wait_copies((last[0], last[1], last[2][:1], None), _diagonal_forward_copies, z_early,
                            "gather_w_in_last_wait")[0]
    w_in_t = w_in_all.reshape(D_IN, D_MODEL)
    z = _z_proj(h, w_in_t, chip, _Z_EARLY_TILES, 7 - _Z_EARLY_TILES, z_early, "z_proj_late")
    w_out_half = _wait_copies((forward[0], forward[1], last[2][1:], None), _group(w_out_early, 0, 1, 1), z,
                              "gather_w_out_early_wait")
    w_out_flight = _wait_then_start((last[0], last[1], w_out_half, None), _group(w_out_late, 0, 1, 1),
                                    _forward_copies, 3, z, "gather_w_out_forward_stage")
    sink_rows = jnp.repeat(attn_sinks.reshape(N_Q_HEADS), BLOCK).reshape(2, 1, 8 * BLOCK)
    sgu_bt = sgu_b[0].T
    a, probs, sink_probs = _mixer_fwd(z, sink_rows + w_out_flight[3][0, 0], sgu_ln_g, sgu_ln_b, sgu_w[0], sgu_bt)
    w_out_all = _wait_copies(w_out_flight, _forward_copies, a, "gather_w_out_forward_wait")[0]
    w_out_full = w_out_all.reshape(D_MODEL, D_MODEL)
    final_g_row = final_g.reshape(1, D_MODEL)
    dx2, dy, loss_part, d_final_g, d_gate = _out_proj_head(a, w_out_full, x2d, target, mod, final_g_row)

    da = _matmul(dy, w_out_full, "nt", F32, min(t, 1024), 1024, "out_proj_bwd")
    dw_out = _matmul(a, dy, "tn", BF16, 1024, 1024, "w_out_grad").reshape(4, 2, W_OUT_SHARD, D_MODEL)
    pair_out = _pair_reduce(dw_out, "w_out_grad_pair_reduce", W_OUT_SHARD // 2)
    dz, d_sinks, d_sgu_w, d_sgu_b, d_ln_g, d_ln_b = _mixer_bwd(
        z, da, probs, sink_probs, sgu_ln_g, sgu_ln_b, sgu_w[0], jnp.swapaxes(sgu_w[0], 1, 2), sgu_bt)
    sgu_w_to_all = _group(_own_block_copies(_all_others), 2, 1, 3)
    both = _start_copies(
        [pair_out, lax.empty((3, W_OUT_SHARD, D_MODEL), BF16), _with_own_slot(d_sgu_w, me)],
        lambda refs, s, r: _chip_copies(refs[:2], s, r) + sgu_w_to_all(refs, s, r), 3 + N_DEV - 1, core,
        "w_out_grad_chip_and_sgu_w_gather_start")
    out_flight, sgu_w_flight = (both[0], both[1], both[2][:2], None), (both[0], both[1], both[2][2:], None)
    dw_in_t = _matmul(dz, h, "tn", BF16, 768, D_MODEL, "w_in_grad", dep=both[3])
    dw_in_t = dw_in_t.reshape(4, 2, W_IN_SHARD, D_MODEL)
    pair_in = _pair_reduce(dw_in_t, "w_in_grad_pair_reduce", W_IN_SHARD // 3)
    hop1 = _start_copies([pair_in, lax.empty((2, W_IN_SHARD, D_MODEL), BF16)], _first_hop_copies, 2, core,
                         "w_in_grad_first_hop_start")
    grad_x, d_shift, d_scale, d_norm_g = _z_proj_bwd_norm(dz, w_in_t, x2d, dx2, norm_g, mod, hop1[3])

    partial = _pack_small(d_shift, d_scale, d_gate, d_norm_g, d_final_g, d_ln_g, d_ln_b, loss_part, d_sinks, d_sgu_b)
    small_flight = _start_copies([_with_own_slot(partial, me)], _own_block_copies(_all_others), N_DEV - 1, core,
                                 "small_grad_gather_start")
    pair_in, land_first = _wait_copies(hop1, _first_hop_copies, small_flight[3], "w_in_grad_first_hop_wait")
    second_chip = (2 * ((xi + ci) % 2) + (yi + 1 - ci) % 2).astype(jnp.int32).reshape(1)
    relay = _relay_sum(second_chip, pair_in, land_first, W_IN_SHARD // 3)
    hop2 = _start_copies([relay, lax.empty((1, W_IN_SHARD, D_MODEL), BF16)], _second_hop_copies, 1, core,
                         "w_in_grad_second_hop_start")
    pair_out, land_out = _wait_copies(out_flight, _chip_copies, hop2[3], "w_out_grad_chip_wait")
    big = {"w_out": _adam_from_chips(chip, pair_out, [(land_out, k) for k in range(3)], w_out[0], m_w_out[0],
                                     v_w_out[0], "adam_w_out", 1024)}
    partial_all = _wait_copies(small_flight, _own_block_copies(_all_others), big["w_out"][0],
                               "small_grad_gather_wait")[0]
    d_sgu_w_all = _wait_copies(sgu_w_flight, _group(_own_block_copies(_all_others), 0, 1, 3), partial_all,
                               "sgu_w_grad_gather_wait")[0]
    weights = {"norm_g": norm_g, "b_ada": b_ada, "attn_sinks": attn_sinks, "sgu_ln_g": sgu_ln_g,
               "sgu_ln_b": sgu_ln_b, "sgu_w": sgu_w, "sgu_b": sgu_b, "final_g": final_g_row}
    moments_m = {"norm_g": m_norm_g, "b_ada": m_b_ada, "attn_sinks": m_attn_sinks, "sgu_ln_g": m_sgu_ln_g,
                 "sgu_ln_b": m_sgu_ln_b, "sgu_w": m_sgu_w, "sgu_b": m_sgu_b,
                 "final_g": m_final_g.reshape(1, D_MODEL)}
    moments_v = {"norm_g": v_norm_g, "b_ada": v_b_ada, "attn_sinks": v_attn_sinks, "sgu_ln_g": v_sgu_ln_g,
                 "sgu_ln_b": v_sgu_ln_b, "sgu_w": v_sgu_w, "sgu_b": v_sgu_b,
                 "final_g": v_final_g.reshape(1, D_MODEL)}
    loss, dmod_all, small = _adam_small(partial_all, d_sgu_w_all, weights, moments_m, moments_v)
    small["final_g"] = tuple(o.reshape(D_MODEL) for o in small["final_g"])

    big["w_ada"] = _adam_w_ada(device, c_act.T, dmod_all, w_ada[0], m_w_ada[0], v_w_ada[0])
    _, land_second = _wait_copies(hop2, _second_hop_copies, big["w_ada"][0], "w_in_grad_second_hop_wait")
    big["w_in"] = tuple(o.T for o in _adam_from_chips(
        chip, pair_in, [(land_first, 0), (land_second, 0)], w_in[0].T, m_w_in[0].T, v_w_in[0].T, "adam_w_in", 512))
    order = ["norm_g", "w_ada", "b_ada", "w_in", "attn_sinks", "sgu_ln_g", "sgu_ln_b", "sgu_w", "sgu_b", "w_out",
             "final_g"]
    outs = [loss.reshape(()), grad_x[None]]
    for k in range(4):
        for name in order:
            outs.append(big[name][k][None] if name in big else small[name][k])
    return tuple(outs)
```

```python
import jax
import jax.numpy as jnp
from jax import lax
from jax.experimental import pallas as pl
from jax.experimental.pallas import tpu as pltpu

F32 = jnp.float32
BF16 = jnp.bfloat16
MESH = pl.DeviceIdType.MESH

N_DEV = 8
D_MODEL = 2048
HEAD_DIM = 64
D_ATTN = 1024
N_Q_HEADS = 16
D_KV = 128
BLOCK = 128
D_SGU = 1024
SGU_GROUPS = 8
D_IN = 5376
W_IN_SHARD = D_IN // N_DEV
W_OUT_SHARD = D_MODEL // N_DEV
W_ADA_SHARD = 3 * D_MODEL // N_DEV
EPS = 1e-6
ATTN_SCALE = 0.125

ADAM_LR = 0.001
ADAM_B1 = 0.9
ADAM_B2 = 0.999
ADAM_EPS = 1e-08
ADAM_WD = 0.01
ADAM_STEP = 10

SEG_Q, SEG_KV, SEG_GA, SEG_U, SEG_VS, SEG_GS = 0, 1024, 1280, 2304, 3328, 4352

VMEM_LIMIT = 56 * 1024 * 1024

ROW_SHIFT, ROW_SCALE, ROW_GATE, ROW_NORM_G, ROW_FINAL_G, ROW_LN, ROW_MISC, ROW_SGU_B = 0, 1, 2, 3, 4, 5, 6, 8
SMALL_ROWS = 16


def _params(**kw):
    return pltpu.CompilerParams(vmem_limit_bytes=VMEM_LIMIT, **kw)


def _sigmoid(x):
    return 0.5 * (jnp.tanh(0.5 * x) + 1.0)


def _place():
    return lax.axis_index("x"), lax.axis_index("y"), lax.axis_index("c")


def _pair_reduce(blocks, name, row_chunk):
    _, _, r, cols = blocks.shape
    assert r % row_chunk == 0

    def body(in_ref, out_ref, land, own, summed, send_sems, recv_sems, own_sems, out_sems):
        x, y, c = _place()
        sends, loads, stores = [], [], []
        for m in range(4):
            cp = pltpu.make_async_remote_copy(
                src_ref=in_ref.at[m, 1 - c], dst_ref=land.at[m], send_sem=send_sems.at[m], recv_sem=recv_sems.at[m],
                device_id=(x, y, 1 - c), device_id_type=MESH)
            cp.start()
            sends.append(cp)
            ld = pltpu.make_async_copy(in_ref.at[m, c], own.at[m], own_sems.at[m])
            ld.start()
            loads.append(ld)
        for m in range(4):
            sends[m].wait_recv()
            loads[m].wait()
            for k in range(r // row_chunk):
                rows = slice(k * row_chunk, (k + 1) * row_chunk)
                summed[m, rows, :] = (own[m, rows, :].astype(F32) + land[m, rows, :].astype(F32)).astype(BF16)
            st = pltpu.make_async_copy(summed.at[m], out_ref.at[m], out_sems.at[m])
            st.start()
            stores.append(st)
        for m in range(4):
            sends[m].wait_send()
            stores[m].wait()

    spec = pl.BlockSpec(memory_space=pl.ANY)
    return pl.pallas_call(
        body, name=name, out_shape=jax.ShapeDtypeStruct((4, r, cols), BF16),
        in_specs=[spec], out_specs=spec,
        scratch_shapes=[pltpu.VMEM((4, r, cols), BF16), pltpu.VMEM((4, r, cols), BF16), pltpu.VMEM((4, r, cols), BF16),
                        pltpu.SemaphoreType.DMA((4,)), pltpu.SemaphoreType.DMA((4,)), pltpu.SemaphoreType.DMA((4,)),
                        pltpu.SemaphoreType.DMA((4,))],
        compiler_params=_params(),
    )(blocks)


_HBM = pl.BlockSpec(memory_space=pltpu.HBM)
_SEM = pl.BlockSpec(memory_space=pltpu.SEMAPHORE)
_EFFECT = pltpu.SideEffectType.DATAFLOW_SIDE_EFFECTING


def _start_copies(bufs, copies, n_copies, after, name):
    nb = len(bufs)

    def body(*refs):
        for cp in copies(refs[:nb], refs[nb + 1], refs[nb + 2]):
            cp.start()
        refs[-1][...] = jnp.zeros_like(refs[-1])

    out = pl.pallas_call(
        body, name=name,
        out_shape=(pltpu.SemaphoreType.DMA((n_copies,)), pltpu.SemaphoreType.DMA((n_copies,)),
                   *[pltpu.HBM(b.shape, b.dtype) for b in bufs], jax.ShapeDtypeStruct((8, 128), F32)),
        in_specs=(_HBM,) * nb + (pl.BlockSpec(memory_space=pl.ANY),),
        out_specs=(_SEM, _SEM) + (_HBM,) * nb + (pl.BlockSpec(memory_space=pltpu.VMEM),),
        input_output_aliases={i: 2 + i for i in range(nb)},
        compiler_params=pltpu.CompilerParams(has_side_effects=_EFFECT),
    )(*[pltpu.with_memory_space_constraint(b, pltpu.HBM) for b in bufs], after)
    return out[0], out[1], list(out[2:2 + nb]), out[-1]


def _wait_copies(flight, copies, after, name):
    send_sems, recv_sems, bufs, _ = flight
    nb = len(bufs)

    def body(*refs):
        for cp in copies(refs[:nb], refs[nb], refs[nb + 1]):
            cp.wait_send()
            cp.wait_recv()

    return pl.pallas_call(
        body, name=name,
        out_shape=tuple(pltpu.HBM(b.shape, b.dtype) for b in bufs),
        in_specs=(_HBM,) * nb + (_SEM, _SEM, pl.BlockSpec(memory_space=pl.ANY)), out_specs=(_HBM,) * nb,
        input_output_aliases={i: i for i in range(nb)},
        compiler_params=pltpu.CompilerParams(has_side_effects=_EFFECT),
    )(*bufs, send_sems, recv_sems, after)


class _From:
    def __init__(self, sems, offset):
        self.sems, self.offset = sems, offset

    @property
    def at(self):
        return self

    def __getitem__(self, k):
        return self.sems.at[k + self.offset]


def _group(copies, first_buf, n_bufs, offset):
    def grouped(refs, send_sems, recv_sems):
        return copies(refs[first_buf:first_buf + n_bufs], _From(send_sems, offset), _From(recv_sems, offset))
    return grouped


def _wait_then_start(flight, waited, started, n_started, after, name, more_bufs=()):
    old_send, old_recv, bufs, _ = flight
    bufs = list(bufs) + [pltpu.with_memory_space_constraint(b, pltpu.HBM) for b in more_bufs]
    nb = len(bufs)

    def body(*refs):
        for cp in waited(refs[:nb], refs[nb], refs[nb + 1]):
            cp.wait_send()
            cp.wait_recv()
        for cp in started(refs[:nb], refs[nb + 3], refs[nb + 4]):
            cp.start()
        refs[-1][...] = jnp.zeros_like(refs[-1])

    out = pl.pallas_call(
        body, name=name,
        out_shape=(pltpu.SemaphoreType.DMA((n_started,)), pltpu.SemaphoreType.DMA((n_started,)),
                   *[pltpu.HBM(b.shape, b.dtype) for b in bufs], jax.ShapeDtypeStruct((8, 128), F32)),
        in_specs=(_HBM,) * nb + (_SEM, _SEM, pl.BlockSpec(memory_space=pl.ANY)),
        out_specs=(_SEM, _SEM) + (_HBM,) * nb + (pl.BlockSpec(memory_space=pltpu.VMEM),),
        input_output_aliases={i: 2 + i for i in range(nb)},
        compiler_params=pltpu.CompilerParams(has_side_effects=_EFFECT),
    )(*bufs, old_send, old_recv, after)
    return out[0], out[1], list(out[2:2 + nb]), out[-1]


def _chip_copies(refs, send_sems, recv_sems):
    pair_ref, land_ref = refs
    x, y, c = _place()
    chips = [(1 - x, y), (x, 1 - y), (1 - x, 1 - y)]
    return [pltpu.make_async_remote_copy(
        src_ref=pair_ref.at[2 * chip[0] + chip[1]], dst_ref=land_ref.at[k],
        send_sem=send_sems.at[k], recv_sem=recv_sems.at[k],
        device_id=(*chip, c), device_id_type=MESH) for k, chip in enumerate(chips)]


def _first_hop_copies(refs, send_sems, recv_sems):
    pair_ref, land_ref = refs
    x, y, c = _place()
    first = ((x + 1 - c) % 2, (y + c) % 2)
    blocks = [2 * first[0] + first[1], 2 * (1 - x) + (1 - y)]
    return [pltpu.make_async_remote_copy(
        src_ref=pair_ref.at[blocks[k]], dst_ref=land_ref.at[k], send_sem=send_sems.at[k], recv_sem=recv_sems.at[k],
        device_id=(*first, c), device_id_type=MESH) for k in range(2)]


def _second_hop_copies(refs, send_sems, recv_sems):
    relay_ref, land_ref = refs
    x, y, c = _place()
    second = ((x + c) % 2, (y + 1 - c) % 2)
    return [pltpu.make_async_remote_copy(
        src_ref=relay_ref, dst_ref=land_ref.at[0], send_sem=send_sems.at[0], recv_sem=recv_sems.at[0],
        device_id=(*second, c), device_id_type=MESH)]


def _own_block_copies(targets):
    def copies(refs, send_sems, recv_sems):
        x, y, c = _place()
        mine = refs[0].at[4 * x + 2 * y + c]
        return [pltpu.make_async_remote_copy(
            src_ref=mine, dst_ref=mine, send_sem=send_sems.at[k], recv_sem=recv_sems.at[k],
            device_id=to, device_id_type=MESH) for k, to in enumerate(targets(x, y, c))]
    return copies


def _all_others(x, y, c):
    flip = lambda v, f: 1 - v if f else v
    return [(flip(x, r & 4), flip(y, r & 2), flip(c, r & 1)) for r in range(1, N_DEV)]


def _forward_copies(refs, send_sems, recv_sems):
    x, y, c = _place()
    chips = [(1 - x, y), (x, 1 - y), (1 - x, 1 - y)]
    return [pltpu.make_async_remote_copy(
        src_ref=refs[0].at[4 * chip[0] + 2 * chip[1] + c], dst_ref=refs[0].at[4 * chip[0] + 2 * chip[1] + c],
        send_sem=send_sems.at[k], recv_sem=recv_sems.at[k],
        device_id=(x, y, 1 - c), device_id_type=MESH) for k, chip in enumerate(chips)]


def _first_axis_chip(x, y, c):
    return (x + 1 - c) % 2, (y + c) % 2


def _second_axis_chip(x, y, c):
    return (x + c) % 2, (y + 1 - c) % 2


def _first_targets(x, y, c):
    return [(x, y, 1 - c), (*_first_axis_chip(x, y, c), c)]


def _all_gather_small(shard, name):
    def body(in_ref, out_ref, send_sems, recv_sems, local_sem):
        x, y, c = _place()
        me, sibling = 4 * x + 2 * y + c, (x, y, 1 - c)
        first, second = _first_axis_chip(x, y, c), _second_axis_chip(x, y, c)

        def pair(chip):
            return out_ref.at[pl.ds(2 * (2 * chip[0] + chip[1]), 2)]

        def exchange(k, src, dst, to):
            cp = pltpu.make_async_remote_copy(src_ref=src, dst_ref=dst, send_sem=send_sems.at[k],
                                              recv_sem=recv_sems.at[k], device_id=to, device_id_type=MESH)
            cp.start()
            cp.wait()

        own = pltpu.make_async_copy(in_ref, out_ref.at[me], local_sem)
        own.start()
        exchange(0, in_ref, out_ref.at[me], sibling)
        own.wait()
        exchange(1, pair((x, y)), pair((x, y)), (*second, c))
        exchange(2, pair(second), pair(second), sibling)
        exchange(3, pair(first), pair(first), (*second, c))

    spec = pl.BlockSpec(memory_space=pltpu.VMEM)
    return pl.pallas_call(
        body, name=name, out_shape=jax.ShapeDtypeStruct((N_DEV,) + shard.shape, shard.dtype),
        in_specs=[spec], out_specs=spec,
        scratch_shapes=[pltpu.SemaphoreType.DMA((4,)), pltpu.SemaphoreType.DMA((4,)), pltpu.SemaphoreType.DMA],
        compiler_params=_params(),
    )(shard)


def _slot_copies(refs, send_sems, recv_sems, plan):
    copies = []
    for k, ((px, py, pc), to) in enumerate(plan):
        blk = refs[0].at[4 * px + 2 * py + pc]
        copies.append(pltpu.make_async_remote_copy(
            src_ref=blk, dst_ref=blk, send_sem=send_sems.at[k], recv_sem=recv_sems.at[k],
            device_id=to, device_id_type=MESH))
    return copies


def _second_axis_stage_copies(refs, send_sems, recv_sems):
    x, y, c = _place()
    first, second = (*_first_axis_chip(x, y, c), c), (*_second_axis_chip(x, y, c), c)
    return _slot_copies(refs, send_sems, recv_sems, [((x, y, c), second), (first, (x, y, 1 - c)), (first, second)])


def _second_axis_forward_copies(refs, send_sems, recv_sems):
    x, y, c = _place()
    return _slot_copies(refs, send_sems, recv_sems, [((*_second_axis_chip(x, y, c), c), (x, y, 1 - c))])


def _diagonal_forward_copies(refs, send_sems, recv_sems):
    x, y, c = _place()
    blk = refs[0].at[4 * (1 - x) + 2 * (1 - y) + c]
    return [pltpu.make_async_remote_copy(
        src_ref=blk, dst_ref=blk, send_sem=send_sems.at[0], recv_sem=recv_sems.at[0],
        device_id=(x, y, 1 - c), device_id_type=MESH)]


def _with_own_slot(block, me):
    return lax.dynamic_update_index_in_dim(lax.empty((N_DEV,) + block.shape, block.dtype), block, me, 0)


def _matmul(a, b, dims, out_dtype, tm, tn, name, dep=None):
    if dims == "nn":
        (m, k), n = a.shape, b.shape[1]
        a_spec = pl.BlockSpec((tm, k), lambda i, j: (i, 0))
        b_spec = pl.BlockSpec((k, tn), lambda i, j: (0, j))
        contract = ((1,), (0,))
    elif dims == "nt":
        (m, k), n = a.shape, b.shape[0]
        a_spec = pl.BlockSpec((tm, k), lambda i, j: (i, 0))
        b_spec = pl.BlockSpec((tn, k), lambda i, j: (j, 0))
        contract = ((1,), (1,))
    else:
        (k, m), n = a.shape, b.shape[1]
        a_spec = pl.BlockSpec((k, tm), lambda i, j: (0, i))
        b_spec = pl.BlockSpec((k, tn), lambda i, j: (0, j))
        contract = ((0,), (0,))
    assert m % tm == 0 and n % tn == 0 and a.dtype == BF16 and b.dtype == BF16

    def body(a_ref, b_ref, *rest):
        rest[-1][...] = lax.dot_general(a_ref[...], b_ref[...], (contract, ((), ())),
                                        preferred_element_type=F32).astype(out_dtype)

    deps = [] if dep is None else [dep]
    return pl.pallas_call(
        body, name=name, grid=(m // tm, n // tn),
        in_specs=[a_spec, b_spec] + [pl.BlockSpec((8, 128), lambda i, j: (0, 0))] * len(deps),
        out_specs=pl.BlockSpec((tm, tn), lambda i, j: (i, j)),
        out_shape=jax.ShapeDtypeStruct((m, n), out_dtype),
        compiler_params=_params(dimension_semantics=("arbitrary", "arbitrary")),
    )(a, b, *deps)


Z_TILE = 768
_Z_TILE_ORDER = ((0, 1, 2, 3, 4, 5, 6), (2, 0, 1, 6, 3, 4, 5), (4, 0, 5, 6, 1, 2, 3), (6, 2, 3, 4, 0, 1, 5))
_Z_EARLY_TILES = 4


def _z_proj(h, w_in_t, chip, first, count, z_prev, name, dep=None):
    t = h.shape[0]

    def body(chip_ref, h_ref, w_ref, *rest):
        rest[-1][...] = _dot_nt(h_ref[...], w_ref[...])

    def tile(j, chip_ref):
        picked = 0
        for c, order in enumerate(_Z_TILE_ORDER):
            for k in range(count):
                picked = picked + jnp.where((chip_ref[0] == c) & (j == k), order[first + k], 0)
        return picked

    prev = [] if z_prev is None else [z_prev]
    deps = [] if dep is None else [dep]
    return pl.pallas_call(
        body, name=name,
        grid_spec=pltpu.PrefetchScalarGridSpec(
            num_scalar_prefetch=1, grid=(count,),
            in_specs=[pl.BlockSpec((t, D_MODEL), lambda j, o: (0, 0)),
                      pl.BlockSpec((Z_TILE, D_MODEL), lambda j, o: (tile(j, o), 0))]
            + [pl.BlockSpec(memory_space=pl.ANY)] * len(prev)
            + [pl.BlockSpec((8, 128), lambda j, o: (0, 0))] * len(deps),
            out_specs=pl.BlockSpec((t, Z_TILE), lambda j, o: (0, tile(j, o)))),
        out_shape=jax.ShapeDtypeStruct((t, D_IN), F32),
        input_output_aliases={3: 0} if prev else {},
        compiler_params=_params(dimension_semantics=("arbitrary",)),
    )(chip, h, w_in_t, *prev, *deps)


def _modulation(device, c_all, w_ada, b_ada):
    def body(device_ref, c_ref, w_ref, b_ref, act_ref, mod_ref):
        cv = c_ref[...]
        act = cv * _sigmoid(cv)
        act_ref[...] = act
        mod_ref[...] = jnp.dot(act.astype(BF16), w_ref[...].astype(BF16), preferred_element_type=F32) + b_ref[...]

    whole = lambda a: pl.BlockSpec(a.shape, lambda i, device_ref: (0,) * a.ndim)
    return pl.pallas_call(
        body, name="modulation",
        grid_spec=pltpu.PrefetchScalarGridSpec(
            num_scalar_prefetch=1, grid=(1,),
            in_specs=[whole(c_all), whole(w_ada), pl.BlockSpec((1, W_ADA_SHARD), lambda i, device_ref: (0, device_ref[0]))],
            out_specs=(whole(c_all), pl.BlockSpec((N_DEV, W_ADA_SHARD), lambda i, device_ref: (0, 0)))),
        out_shape=(jax.ShapeDtypeStruct(c_all.shape, F32), jax.ShapeDtypeStruct((N_DEV, W_ADA_SHARD), F32)),
        compiler_params=_params(dimension_semantics=("arbitrary",)),
    )(device, c_all, w_ada, b_ada)


MOD_SHIFT, MOD_SCALE, MOD_GATE = 0, 1, 2


def _mod_spec(part, d):
    return pl.BlockSpec((1, d), lambda i: (0, part))


def _modulated_norm(x, norm_g, mod, tm=512):
    t, d = x.shape

    def body(x_ref, g_ref, sc_ref, sh_ref, h_ref, r_ref):
        xv = x_ref[...]
        r = lax.rsqrt(jnp.mean(xv * xv, axis=-1, keepdims=True) + EPS)
        h = (xv * r) * g_ref[...] * (1.0 + sc_ref[...]) + sh_ref[...]
        h_ref[...] = h.astype(BF16)
        r_ref[...] = r

    row = pl.BlockSpec((1, d), lambda i: (0, 0))
    return pl.pallas_call(
        body, name="modulated_norm", grid=(t // tm,),
        in_specs=[pl.BlockSpec((tm, d), lambda i: (i, 0)), row, _mod_spec(MOD_SCALE, d), _mod_spec(MOD_SHIFT, d)],
        out_specs=(pl.BlockSpec((tm, d), lambda i: (i, 0)), pl.BlockSpec((tm, 1), lambda i: (i, 0))),
        out_shape=(jax.ShapeDtypeStruct((t, d), BF16), jax.ShapeDtypeStruct((t, 1), F32)),
        compiler_params=_params(dimension_semantics=("arbitrary",)),
    )(x, norm_g, mod, mod)


def _window_bias(block_index):
    s = lax.broadcasted_iota(jnp.int32, (2 * BLOCK, BLOCK), 0)
    t = lax.broadcasted_iota(jnp.int32, (2 * BLOCK, BLOCK), 1)
    valid = ((s < BLOCK) & (s > t) & (block_index > 0)) | ((s >= BLOCK) & ((s - BLOCK) <= t))
    bias = jnp.where(valid, 0.0, -jnp.inf).astype(F32)
    return jnp.concatenate([bias] * 8, axis=1)


def _heads_t(pair_blocks, g):
    top = lax.broadcasted_iota(jnp.int32, (BLOCK, BLOCK), 0) < HEAD_DIM
    zeros = jnp.zeros((HEAD_DIM, BLOCK), F32)
    tiles = []
    for blk in pair_blocks:
        tp = blk.T
        if g == 0:
            tiles += [jnp.where(top, tp, 0.0), jnp.concatenate([tp[HEAD_DIM:], zeros], axis=0)]
        else:
            tiles += [jnp.concatenate([zeros, tp[:HEAD_DIM]], axis=0), jnp.where(top, 0.0, tp)]
    return jnp.concatenate(tiles, axis=1)


def _pair_block(xt, p, g):
    r0 = HEAD_DIM * g
    even = xt[r0:r0 + HEAD_DIM, (2 * p) * BLOCK:(2 * p + 1) * BLOCK]
    odd = xt[r0:r0 + HEAD_DIM, (2 * p + 1) * BLOCK:(2 * p + 2) * BLOCK]
    return jnp.concatenate([even, odd], axis=0).T


def _softmax_t(scores_t, bias, sink):
    st = scores_t + bias
    m = jnp.maximum(jnp.max(st, axis=0, keepdims=True), sink)
    e = jnp.exp(st - m)
    es = jnp.exp(sink - m)
    inv = 1.0 / (jnp.sum(e, axis=0, keepdims=True) + es)
    return e * inv, es * inv


def _dot(a, b):
    return jnp.dot(a, b, preferred_element_type=F32)


def _dot_nt(a, b):
    return lax.dot_general(a, b, (((1,), (1,)), ((), ())), preferred_element_type=F32)


def _layer_norm_fwd(v):
    mu = jnp.mean(v, axis=-1, keepdims=True)
    xc = v - mu
    rstd = lax.rsqrt(jnp.mean(xc * xc, axis=-1, keepdims=True) + EPS)
    return xc * rstd, rstd


def _tril(transposed=False):
    t = lax.broadcasted_iota(jnp.int32, (BLOCK, BLOCK), 0)
    s = lax.broadcasted_iota(jnp.int32, (BLOCK, BLOCK), 1)
    return s >= t if transposed else t >= s


def _const_spec(shape):
    return pl.BlockSpec(shape, lambda i: (0,) * len(shape))


def _keys_values(z_ref, kvp):
    kvc = z_ref[:, SEG_KV:SEG_KV + 2 * D_KV]
    kk = jnp.concatenate([kvp[:, :D_KV], kvc[:, :D_KV]], axis=0)
    vv = jnp.concatenate([kvp[:, D_KV:], kvc[:, D_KV:]], axis=0)
    return kk, vv


MIXER_BLOCKS = 2


class _Rows:
    def __init__(self, ref, sub):
        self.ref, self.rows = ref, slice(sub * BLOCK, (sub + 1) * BLOCK)

    def __getitem__(self, idx):
        return self.ref[self.rows, idx[1]]

    def __setitem__(self, idx, value):
        self.ref[self.rows, idx[1]] = value


def _kv_before_spec(index):
    return pl.BlockSpec((BLOCK, 2 * D_KV),
                        lambda i: (jnp.maximum(MIXER_BLOCKS * index(i) - 1, 0), SEG_KV // (2 * D_KV)))


def _pair_cols(g, p, base=0):
    return slice(base + (4 * g + p) * 128, base + (4 * g + p + 1) * 128)


def _mixer_fwd(z, sink_rows, ln_g, ln_b, sgu_w, sgu_bt):
    t = z.shape[0]

    def body(z_all, kvp_ref, sink_ref, lng_ref, lnb_ref, w_ref, bt_ref, a_all, prob_ref, sink_prob_ref):
        kv_before = kvp_ref[...]
        for sub in range(MIXER_BLOCKS):
            z_ref, a_ref = _Rows(z_all, sub), _Rows(a_all, sub)
            one_block(z_ref, kv_before, MIXER_BLOCKS * pl.program_id(0) + sub, sink_ref, lng_ref, lnb_ref, w_ref,
                      bt_ref, a_ref, prob_ref.at[sub], sink_prob_ref.at[sub])
            kv_before = z_ref[:, SEG_KV:SEG_KV + 2 * D_KV]

    def one_block(z_ref, kv_before, block_index, sink_ref, lng_ref, lnb_ref, w_ref, bt_ref, a_ref, prob_ref,
                  sink_prob_ref):
        bias = _window_bias(block_index)
        kk, vv = _keys_values(z_ref, kv_before)
        kk_b, vvt_b = kk.astype(BF16), vv.T.astype(BF16)
        for g in range(2):
            qt = _heads_t([z_ref[:, _pair_cols(g, p, SEG_Q)] * ATTN_SCALE for p in range(4)], g).astype(BF16)
            prob, sink_prob = _softmax_t(_dot(kk_b, qt), bias, sink_ref[g])
            prob_b = prob.astype(BF16)
            prob_ref[g] = prob_b
            sink_prob_ref[g] = sink_prob
            ot = _dot(vvt_b, prob_b)
            for p in range(4):
                gate = z_ref[:, _pair_cols(g, p, SEG_GA)]
                a_ref[:, _pair_cols(g, p)] = (_pair_block(ot, p, g) * (gate * _sigmoid(gate))).astype(BF16)

        vhat, _ = _layer_norm_fwd(z_ref[:, SEG_VS:SEG_VS + D_SGU])
        vn = vhat * lng_ref[...] + lnb_ref[...]
        tril = _tril()
        for g in range(SGU_GROUPS):
            cols = slice(g * 128, (g + 1) * 128)
            wm = jnp.where(tril, w_ref[g], 0.0).astype(BF16)
            mixed = _dot(wm, vn[:, cols].astype(BF16)) + bt_ref[:, g:g + 1]
            gate = z_ref[:, SEG_GS + g * 128:SEG_GS + (g + 1) * 128]
            a_ref[:, D_ATTN + g * 128:D_ATTN + (g + 1) * 128] = (
                (z_ref[:, SEG_U + g * 128:SEG_U + (g + 1) * 128] * mixed) * (gate * _sigmoid(gate))).astype(BF16)

    rows = MIXER_BLOCKS * BLOCK
    return pl.pallas_call(
        body, name="mixer_fwd", grid=(t // rows,),
        in_specs=[pl.BlockSpec((rows, D_IN), lambda i: (i, 0)), _kv_before_spec(lambda i: i),
                  _const_spec((2, 1, 8 * BLOCK)), _const_spec((1, D_SGU)), _const_spec((1, D_SGU)),
                  _const_spec((SGU_GROUPS, BLOCK, BLOCK)), _const_spec((BLOCK, SGU_GROUPS))],
        out_specs=(pl.BlockSpec((rows, D_MODEL), lambda i: (i, 0)),
                   pl.BlockSpec((MIXER_BLOCKS, 2, 2 * BLOCK, 8 * BLOCK), lambda i: (i, 0, 0, 0)),
                   pl.BlockSpec((MIXER_BLOCKS, 2, 1, 8 * BLOCK), lambda i: (i, 0, 0, 0))),
        out_shape=(jax.ShapeDtypeStruct((t, D_MODEL), BF16),
                   jax.ShapeDtypeStruct((t // BLOCK, 2, 2 * BLOCK, 8 * BLOCK), BF16),
                   jax.ShapeDtypeStruct((t // BLOCK, 2, 1, 8 * BLOCK), F32)),
        compiler_params=_params(dimension_semantics=("arbitrary",)),
    )(z, z, sink_rows, ln_g, ln_b, sgu_w, sgu_bt)


def _mixer_bwd(z, da, probs, sink_probs, ln_g, ln_b, sgu_w, sgu_wt, sgu_bt):
    t = z.shape[0]

    def body(z_all, kvp_ref, da_all, prob_ref, sink_prob_ref, lng_ref, lnb_ref, w_ref, wt_ref, bt_ref,
             dz_all, dsink_ref, dw_ref, db_ref, dlng_ref, dlnb_ref, carry_ref, dsink_acc, dbt_acc):
        step = pl.program_id(0)

        @pl.when(step == 0)
        def _():
            carry_ref[...] = jnp.zeros_like(carry_ref)
            dsink_acc[...] = jnp.zeros_like(dsink_acc)
            dbt_acc[...] = jnp.zeros_like(dbt_acc)
            dw_ref[...] = jnp.zeros_like(dw_ref)
            dlng_ref[...] = jnp.zeros_like(dlng_ref)
            dlnb_ref[...] = jnp.zeros_like(dlnb_ref)

        carry = carry_ref[...]
        for sub in reversed(range(MIXER_BLOCKS)):
            kv_before = kvp_ref[...] if sub == 0 else _Rows(z_all, sub - 1)[:, SEG_KV:SEG_KV + 2 * D_KV]
            carry = one_block(_Rows(z_all, sub), kv_before, _Rows(da_all, sub), prob_ref.at[sub], sink_prob_ref.at[sub],
                              carry, lng_ref, lnb_ref, w_ref, wt_ref, bt_ref, _Rows(dz_all, sub),
                              dw_ref, dlng_ref, dlnb_ref, dsink_acc, dbt_acc)
        carry_ref[...] = carry

        @pl.when(step == ns - 1)
        def _():
            db_ref[...] = dbt_acc[...].T[:SGU_GROUPS]
            lane_row = lax.broadcasted_iota(jnp.int32, (1, 128), 1)
            d_sink = jnp.zeros((1, 128), F32)
            for g in range(2):
                acc = dsink_acc[g]
                for j in range(8):
                    head_sum = jnp.sum(acc[:, j * BLOCK:(j + 1) * BLOCK], axis=-1, keepdims=True)
                    d_sink = d_sink + jnp.where(lane_row == 8 * g + j, head_sum, 0.0)
            dsink_ref[...] = d_sink

    def one_block(z_ref, kv_before, da_ref, prob_ref, sink_prob_ref, carry, lng_ref, lnb_ref, w_ref, wt_ref, bt_ref,
                  dz_ref, dw_ref, dlng_ref, dlnb_ref, dsink_acc, dbt_acc):
        kk, vv = _keys_values(z_ref, kv_before)
        vv_b = vv.astype(BF16)
        kkt_b, vvt_b = kk.T.astype(BF16), vv.T.astype(BF16)
        dkk = jnp.zeros((2 * BLOCK, D_KV), F32)
        dvv = jnp.zeros((2 * BLOCK, D_KV), F32)
        for g in range(2):
            qt = _heads_t([z_ref[:, _pair_cols(g, p, SEG_Q)] * ATTN_SCALE for p in range(4)], g).astype(BF16)
            prob_b, sink_prob = prob_ref[g], sink_prob_ref[g]
            prob = prob_b.astype(F32)
            ot = _dot(vvt_b, prob_b)
            gates = [z_ref[:, _pair_cols(g, p, SEG_GA)] for p in range(4)]
            sig = [_sigmoid(gt) for gt in gates]
            d_attn = [da_ref[:, _pair_cols(g, p)] for p in range(4)]
            d_ot = _heads_t([d_attn[p] * (gates[p] * sig[p]) for p in range(4)], g).astype(BF16)
            d_prob = _dot(vv_b, d_ot)
            delta = jnp.sum(prob * d_prob, axis=0, keepdims=True)
            d_scores = (prob * (d_prob - delta)).astype(BF16)
            dsink_acc[g] -= sink_prob * delta
            d_qt = _dot(kkt_b, d_scores)
            dkk = dkk + _dot_nt(d_scores, qt)
            dvv = dvv + _dot_nt(prob_b, d_ot)
            for p in range(4):
                dz_ref[:, _pair_cols(g, p, SEG_Q)] = (_pair_block(d_qt, p, g) * ATTN_SCALE).astype(BF16)
                d_silu = sig[p] * (1.0 + gates[p] * (1.0 - sig[p]))
                dz_ref[:, _pair_cols(g, p, SEG_GA)] = (d_attn[p] * _pair_block(ot, p, g) * d_silu).astype(BF16)
        d_kv = jnp.concatenate([dkk, dvv], axis=1)
        dz_ref[:, SEG_KV:SEG_KV + 2 * D_KV] = (d_kv[BLOCK:] + carry).astype(BF16)

        vhat, rstd = _layer_norm_fwd(z_ref[:, SEG_VS:SEG_VS + D_SGU])
        lng = lng_ref[...]
        vn = vhat * lng + lnb_ref[...]
        tril, triu = _tril(), _tril(transposed=True)
        lane = lax.broadcasted_iota(jnp.int32, (BLOCK, 128), 1)
        d_bt = jnp.zeros((BLOCK, 128), F32)
        d_vn = []
        for g in range(SGU_GROUPS):
            cols = slice(g * 128, (g + 1) * 128)
            wm = jnp.where(tril, w_ref[g], 0.0).astype(BF16)
            wmt = jnp.where(triu, wt_ref[g], 0.0).astype(BF16)
            vn_g = vn[:, cols].astype(BF16)
            mixed = _dot(wm, vn_g) + bt_ref[:, g:g + 1]
            gate = z_ref[:, SEG_GS + g * 128:SEG_GS + (g + 1) * 128]
            u = z_ref[:, SEG_U + g * 128:SEG_U + (g + 1) * 128]
            d_out = da_ref[:, D_ATTN + g * 128:D_ATTN + (g + 1) * 128]
            sg = _sigmoid(gate)
            d_um = d_out * (gate * sg)
            dz_ref[:, SEG_U + g * 128:SEG_U + (g + 1) * 128] = (d_um * mixed).astype(BF16)
            dz_ref[:, SEG_GS + g * 128:SEG_GS + (g + 1) * 128] = (
                d_out * (u * mixed) * (sg * (1.0 + gate * (1.0 - sg)))).astype(BF16)
            d_mixed = d_um * u
            d_mixed_b = d_mixed.astype(BF16)
            dw_ref[g] += jnp.where(tril, _dot_nt(d_mixed_b, vn_g), 0.0)
            d_bt = d_bt + jnp.where(lane == g, jnp.sum(d_mixed, axis=-1, keepdims=True), 0.0)
            d_vn.append(_dot(wmt, d_mixed_b))
        dbt_acc[...] += d_bt
        d_vn = jnp.concatenate(d_vn, axis=1)
        dlng_ref[...] += jnp.sum(d_vn * vhat, axis=0, keepdims=True)
        dlnb_ref[...] += jnp.sum(d_vn, axis=0, keepdims=True)
        d_vhat = d_vn * lng
        d_v = rstd * (d_vhat - jnp.mean(d_vhat, axis=-1, keepdims=True)
                      - vhat * jnp.mean(d_vhat * vhat, axis=-1, keepdims=True))
        dz_ref[:, SEG_VS:SEG_VS + D_SGU] = d_v.astype(BF16)
        return d_kv[:BLOCK]

    rows = MIXER_BLOCKS * BLOCK
    ns = t // rows
    rev = lambda i: ns - 1 - i
    return pl.pallas_call(
        body, name="mixer_bwd", grid=(ns,),
        in_specs=[pl.BlockSpec((rows, D_IN), lambda i: (rev(i), 0)), _kv_before_spec(rev),
                  pl.BlockSpec((rows, D_MODEL), lambda i: (rev(i), 0)),
                  pl.BlockSpec((MIXER_BLOCKS, 2, 2 * BLOCK, 8 * BLOCK), lambda i: (rev(i), 0, 0, 0)),
                  pl.BlockSpec((MIXER_BLOCKS, 2, 1, 8 * BLOCK), lambda i: (rev(i), 0, 0, 0)),
                  _const_spec((1, D_SGU)), _const_spec((1, D_SGU)),
                  _const_spec((SGU_GROUPS, BLOCK, BLOCK)), _const_spec((SGU_GROUPS, BLOCK, BLOCK)),
                  _const_spec((BLOCK, SGU_GROUPS))],
        out_specs=(pl.BlockSpec((rows, D_IN), lambda i: (rev(i), 0)), _const_spec((1, 128)),
                   _const_spec((SGU_GROUPS, BLOCK, BLOCK)), _const_spec((SGU_GROUPS, BLOCK)),
                   _const_spec((1, D_SGU)), _const_spec((1, D_SGU))),
        out_shape=(jax.ShapeDtypeStruct((t, D_IN), BF16), jax.ShapeDtypeStruct((1, 128), F32),
                   jax.ShapeDtypeStruct((SGU_GROUPS, BLOCK, BLOCK), F32), jax.ShapeDtypeStruct((SGU_GROUPS, BLOCK), F32),
                   jax.ShapeDtypeStruct((1, D_SGU), F32), jax.ShapeDtypeStruct((1, D_SGU), F32)),
        scratch_shapes=[pltpu.VMEM((BLOCK, 2 * D_KV), F32), pltpu.VMEM((2, 1, 8 * BLOCK), F32),
                        pltpu.VMEM((BLOCK, 128), F32)],
        compiler_params=_params(dimension_semantics=("arbitrary",)),
    )(z, z, da, probs, sink_probs, ln_g, ln_b, sgu_w, sgu_wt, sgu_bt)


def _out_proj_head(a, w_out_full, x, target, mod, final_g, tm=256):
    t, d = x.shape

    def body(a_ref, w_ref, x_ref, tg_ref, gate_ref, fg_ref, dx2_ref, dy_ref, loss_ref, dfg_ref, dgate_ref):
        @pl.when(pl.program_id(0) == 0)
        def _():
            loss_ref[...] = jnp.zeros_like(loss_ref)
            dfg_ref[...] = jnp.zeros_like(dfg_ref)
            dgate_ref[...] = jnp.zeros_like(dgate_ref)

        yv, gate, fg = _dot(a_ref[...], w_ref[...]), gate_ref[...], fg_ref[...]
        x2 = x_ref[...] + gate * yv
        r2 = lax.rsqrt(jnp.mean(x2 * x2, axis=-1, keepdims=True) + EPS)
        nrm = x2 * r2
        err = nrm * fg - tg_ref[...]
        loss_ref[...] += 0.5 * jnp.sum(jnp.mean(err * err, axis=-1, keepdims=True), axis=0, keepdims=True)
        fg_d = fg * (1.0 / d)
        err_nrm = err * nrm
        dfg_ref[...] += jnp.sum(err_nrm, axis=0, keepdims=True) * (1.0 / d)
        d_nrm = err * fg_d
        dx2 = r2 * (d_nrm - nrm * jnp.mean(err_nrm * fg_d, axis=-1, keepdims=True))
        dx2_ref[...] = dx2
        dgate_ref[...] += jnp.sum(dx2 * yv, axis=0, keepdims=True)
        dy_ref[...] = (dx2 * gate).astype(BF16)

    blk = pl.BlockSpec((tm, d), lambda i: (i, 0))
    row = _const_spec((1, d))
    whole = pl.BlockSpec(w_out_full.shape, lambda i: (0, 0), pipeline_mode=pl.Buffered(1))
    return pl.pallas_call(
        body, name="out_proj_head", grid=(t // tm,),
        in_specs=[pl.BlockSpec((tm, a.shape[1]), lambda i: (i, 0)), whole, blk, blk, _mod_spec(MOD_GATE, d), row],
        out_specs=(blk, blk, _const_spec((1, 128)), row, row),
        out_shape=(jax.ShapeDtypeStruct((t, d), F32), jax.ShapeDtypeStruct((t, d), BF16),
                   jax.ShapeDtypeStruct((1, 128), F32), jax.ShapeDtypeStruct((1, d), F32),
                   jax.ShapeDtypeStruct((1, d), F32)),
        compiler_params=_params(dimension_semantics=("arbitrary",)),
    )(a, w_out_full, x, target, mod, final_g)


def _z_proj_bwd_norm(dz, w_in_t, x, rinv, dx2, norm_g, mod, dep, tm=256):
    t, d = x.shape

    def body(dz_ref, w_ref, x_ref, r_ref, dx2_ref, g_ref, sc_ref, dep_ref, gx_ref, dshift_ref, dscale_ref, dg_ref):
        @pl.when(pl.program_id(0) == 0)
        def _():
            dshift_ref[...] = jnp.zeros_like(dshift_ref)
            dscale_ref[...] = jnp.zeros_like(dscale_ref)
            dg_ref[...] = jnp.zeros_like(dg_ref)

        dh, r, g = _dot(dz_ref[...], w_ref[...]), r_ref[...], g_ref[...]
        one_plus = 1.0 + sc_ref[...]
        xn = x_ref[...] * r
        gain = one_plus * g
        dh_xn = dh * xn
        dh_xn_sum = jnp.sum(dh_xn, axis=0, keepdims=True)
        dshift_ref[...] += jnp.sum(dh, axis=0, keepdims=True)
        dscale_ref[...] += dh_xn_sum * g
        dg_ref[...] += dh_xn_sum * one_plus
        d_xn = dh * gain
        gx_ref[...] = dx2_ref[...] + r * (d_xn - xn * jnp.mean(dh_xn * gain, axis=-1, keepdims=True))

    blk = pl.BlockSpec((tm, d), lambda i: (i, 0))
    row = _const_spec((1, d))
    whole = pl.BlockSpec(w_in_t.shape, lambda i: (0, 0), pipeline_mode=pl.Buffered(1))
    return pl.pallas_call(
        body, name="z_proj_bwd_norm", grid=(t // tm,),
        in_specs=[pl.BlockSpec((tm, dz.shape[1]), lambda i: (i, 0)), whole, blk, pl.BlockSpec((tm, 1), lambda i: (i, 0)),
                  blk, row, _mod_spec(MOD_SCALE, d), _const_spec((8, 128))],
        out_specs=(blk, row, row, row),
        out_shape=(jax.ShapeDtypeStruct((t, d), F32),) + (jax.ShapeDtypeStruct((1, d), F32),) * 3,
        compiler_params=_params(dimension_semantics=("arbitrary",)),
    )(dz, w_in_t, x, rinv, dx2, norm_g, mod, dep)


def _adamw(w, g, m, v):
    m = ADAM_B1 * m + (1.0 - ADAM_B1) * g
    v = ADAM_B2 * v + (1.0 - ADAM_B2) * (g * g)
    m_hat = m / (1.0 - ADAM_B1 ** ADAM_STEP)
    v_hat = v / (1.0 - ADAM_B2 ** ADAM_STEP)
    delta = -ADAM_LR * (m_hat / (jnp.sqrt(v_hat) + ADAM_EPS) + ADAM_WD * w)
    return delta, m, v


def _relay_sum(second_chip, pair, land, tr):
    _, r, c = pair.shape

    def body(chip_ref, a_ref, b_ref, o_ref):
        o_ref[...] = (a_ref[...].astype(F32) + b_ref[...].astype(F32)).astype(BF16)

    return pl.pallas_call(
        body, name="w_in_grad_relay_sum",
        grid_spec=pltpu.PrefetchScalarGridSpec(
            num_scalar_prefetch=1, grid=(r // tr,),
            in_specs=[pl.BlockSpec((None, tr, c), lambda i, chip_ref: (chip_ref[0], i, 0)),
                      pl.BlockSpec((None, tr, c), lambda i, chip_ref: (1, i, 0))],
            out_specs=pl.BlockSpec((tr, c), lambda i, chip_ref: (i, 0))),
        out_shape=jax.ShapeDtypeStruct((r, c), BF16),
        compiler_params=_params(dimension_semantics=("arbitrary",)),
    )(second_chip, pair, land)


def _adam_from_chips(chip, pair, landed, w, m, v, name, tc):
    _, r, c = pair.shape
    n = len(landed)

    def body(chip_ref, own_ref, *refs):
        w_ref, m_ref, v_ref, g_ref, d_ref, nm_ref, nv_ref = refs[n:]
        g = own_ref[...].astype(F32)
        for k in range(n):
            g = g + refs[k][...].astype(F32)
        g_ref[...] = g
        d_ref[...], nm_ref[...], nv_ref[...] = _adamw(w_ref[...], g, m_ref[...], v_ref[...])

    def landed_spec(index):
        return pl.BlockSpec((None, r, tc), lambda i, chip_ref: (index, 0, i))

    blk = pl.BlockSpec((r, tc), lambda i, chip_ref: (0, i))
    return pl.pallas_call(
        body, name=name,
        grid_spec=pltpu.PrefetchScalarGridSpec(
            num_scalar_prefetch=1, grid=(c // tc,),
            in_specs=[pl.BlockSpec((None, r, tc), lambda i, chip_ref: (chip_ref[0], 0, i))]
            + [landed_spec(index) for _, index in landed] + [blk, blk, blk],
            out_specs=(blk,) * 4),
        out_shape=(jax.ShapeDtypeStruct((r, c), F32),) * 4,
        compiler_params=_params(dimension_semantics=("arbitrary",)),
    )(chip, pair, *[array for array, _ in landed], w, m, v)


def _adam_w_ada(device, act_t, dmod_all, w, m, v, tr=512):
    r, c = w.shape

    def body(device_ref, a_ref, dm_ref, w_ref, m_ref, v_ref, g_ref, d_ref, nm_ref, nv_ref):
        g = _dot(a_ref[...].astype(BF16), dm_ref[...].astype(BF16))
        g_ref[...] = g
        d_ref[...], nm_ref[...], nv_ref[...] = _adamw(w_ref[...], g, m_ref[...], v_ref[...])

    blk = pl.BlockSpec((tr, c), lambda i, device_ref: (i, 0))
    return pl.pallas_call(
        body, name="adam_w_ada",
        grid_spec=pltpu.PrefetchScalarGridSpec(
            num_scalar_prefetch=1, grid=(r // tr,),
            in_specs=[pl.BlockSpec((tr, N_DEV), lambda i, device_ref: (i, 0)),
                      pl.BlockSpec((N_DEV, c), lambda i, device_ref: (0, device_ref[0])), blk, blk, blk],
            out_specs=(blk,) * 4),
        out_shape=(jax.ShapeDtypeStruct((r, c), F32),) * 4,
        compiler_params=_params(dimension_semantics=("arbitrary",)),
    )(device, act_t, dmod_all, w, m, v)


def _pack_small(d_shift, d_scale, d_gate, d_norm_g, d_final_g, d_ln_g, d_ln_b, loss, d_sinks, d_sgu_b):
    def body(shift_ref, scale_ref, gate_ref, ng_ref, fg_ref, lng_ref, lnb_ref, loss_ref, sink_ref, b_ref, o_ref):
        o_ref[...] = jnp.zeros_like(o_ref)
        o_ref[ROW_SHIFT:ROW_SHIFT + 1, :] = shift_ref[...]
        o_ref[ROW_SCALE:ROW_SCALE + 1, :] = scale_ref[...]
        o_ref[ROW_GATE:ROW_GATE + 1, :] = gate_ref[...]
        o_ref[ROW_NORM_G:ROW_NORM_G + 1, :] = ng_ref[...]
        o_ref[ROW_FINAL_G:ROW_FINAL_G + 1, :] = fg_ref[...]
        o_ref[ROW_LN:ROW_LN + 1, 0:D_SGU] = lng_ref[...]
        o_ref[ROW_LN:ROW_LN + 1, D_SGU:2 * D_SGU] = lnb_ref[...]
        o_ref[ROW_MISC:ROW_MISC + 1, 0:128] = loss_ref[...]
        o_ref[ROW_MISC:ROW_MISC + 1, 128:256] = sink_ref[...]
        o_ref[ROW_SGU_B:ROW_SGU_B + SGU_GROUPS, 0:BLOCK] = b_ref[...]

    return pl.pallas_call(
        body, name="pack_small", out_shape=jax.ShapeDtypeStruct((SMALL_ROWS, D_MODEL), F32),
        compiler_params=_params(),
    )(d_shift, d_scale, d_gate, d_norm_g, d_final_g, d_ln_g, d_ln_b, loss, d_sinks, d_sgu_b)


_SMALL_NAMES = ("norm_g", "b_ada", "attn_sinks", "sgu_ln_g", "sgu_ln_b", "sgu_w", "sgu_b", "final_g")


def _adam_small(partials, d_sgu_w_all, weights, moments_m, moments_v):
    names = _SMALL_NAMES
    k = len(names)

    def body(*refs):
        p_ref, sw_ref = refs[0], refs[1]
        w_refs, m_refs, v_refs = refs[2:2 + k], refs[2 + k:2 + 2 * k], refs[2 + 2 * k:2 + 3 * k]
        loss_ref, dmod_ref = refs[2 + 3 * k], refs[3 + 3 * k]
        out_refs = refs[4 + 3 * k:4 + 7 * k]
        sum_ref = refs[4 + 7 * k]
        total = p_ref[0]
        for j in range(1, N_DEV):
            total = total + p_ref[j]
        sum_ref[...] = total
        for j in range(N_DEV):
            for part, row in enumerate((ROW_SHIFT, ROW_SCALE, ROW_GATE)):
                dmod_ref[j:j + 1, part * D_MODEL:(part + 1) * D_MODEL] = p_ref[j, row:row + 1, :]
        loss_ref[...] = sum_ref[ROW_MISC:ROW_MISC + 1, 0:1]
        d_sgu_w = sw_ref[0]
        for j in range(1, N_DEV):
            d_sgu_w = d_sgu_w + sw_ref[j]
        grads = {
            "norm_g": sum_ref[ROW_NORM_G:ROW_NORM_G + 1, :],
            "b_ada": jnp.concatenate([sum_ref[r:r + 1, :] for r in (ROW_SHIFT, ROW_SCALE, ROW_GATE)], axis=1),
            "attn_sinks": sum_ref[ROW_MISC:ROW_MISC + 1, 128:128 + N_Q_HEADS],
            "sgu_ln_g": sum_ref[ROW_LN:ROW_LN + 1, 0:D_SGU],
            "sgu_ln_b": sum_ref[ROW_LN:ROW_LN + 1, D_SGU:2 * D_SGU],
            "sgu_w": d_sgu_w[None],
            "sgu_b": sum_ref[ROW_SGU_B:ROW_SGU_B + SGU_GROUPS, 0:BLOCK][None],
            "final_g": sum_ref[ROW_FINAL_G:ROW_FINAL_G + 1, :],
        }
        for i, name in enumerate(names):
            g = grads[name]
            delta, m, v = _adamw(w_refs[i][...], g, m_refs[i][...], v_refs[i][...])
            out_refs[4 * i][...] = g
            out_refs[4 * i + 1][...] = delta
            out_refs[4 * i + 2][...] = m
            out_refs[4 * i + 3][...] = v

    shapes = [jax.ShapeDtypeStruct((1, 1), F32), jax.ShapeDtypeStruct((N_DEV, 3 * D_MODEL), F32)]
    for name in names:
        shapes += [jax.ShapeDtypeStruct(weights[name].shape, F32)] * 4
    outs = pl.pallas_call(
        body, name="adam_small", out_shape=tuple(shapes),
        scratch_shapes=[pltpu.VMEM((SMALL_ROWS, D_MODEL), F32)],
        compiler_params=_params(),
    )(partials, d_sgu_w_all, *[weights[n] for n in names], *[moments_m[n] for n in names],
      *[moments_v[n] for n in names])
    return outs[0], outs[1], {name: outs[2 + 4 * i:6 + 4 * i] for i, name in enumerate(names)}


def kernel(x, c, norm_g, w_ada, b_ada, w_in, attn_sinks, sgu_ln_g, sgu_ln_b, sgu_w, sgu_b, w_out, final_g, loss_target, m_norm_g, m_w_ada, m_b_ada, m_w_in, m_attn_sinks, m_sgu_ln_g, m_sgu_ln_b, m_sgu_w, m_sgu_b, m_w_out, m_final_g, v_norm_g, v_w_ada, v_b_ada, v_w_in, v_attn_sinks, v_sgu_ln_g, v_sgu_ln_b, v_sgu_w, v_sgu_b, v_w_out, v_final_g):
    xi, yi, ci = _place()
    me = 4 * xi + 2 * yi + ci
    x2d, target = x[0], loss_target[0]
    t = x2d.shape[0]

    core = ci.astype(jnp.int32).reshape(1)
    chip = (2 * xi + yi).astype(jnp.int32).reshape(1)

    first = _own_block_copies(_first_targets)
    first_flight = _start_copies([_with_own_slot(w_in[0].T.astype(BF16), me)], first, 2, core, "gather_w_in_start")

    c_all = _all_gather_small(c.reshape(8, 256) + first_flight[3][0, 0], "gather_c").reshape(N_DEV, D_MODEL)
    device = me.astype(jnp.int32).reshape(1)
    c_act, mod_part = _modulation(device, c_all, w_ada[0], b_ada)
    mod_all = _all_gather_small(mod_part, "gather_mod")

    across = _wait_then_start(first_flight, lambda *a: first(*a)[1:], _second_axis_stage_copies, 3, mod_all,
                              "gather_w_in_second_axis_stage")
    mod = lax.dynamic_index_in_dim(mod_all, me, axis=1, keepdims=False).reshape(1, 3 * D_MODEL)
    mod = mod + across[3][0, 0]
    h, rinv = _modulated_norm(x2d, norm_g, mod)

    w_in_pair = _wait_copies((first_flight[0], first_flight[1], across[2], None), lambda *a: first(*a)[:1], h,
                             "gather_w_in_sibling_wait")
    z_own = _z_proj(h, w_in_pair[0].reshape(D_IN, D_MODEL), chip, 0, 1, None, "z_proj_own")
    w_out_early = _own_block_copies(lambda x, y, c: [(x, y, 1 - c), (*_second_axis_chip(x, y, c), c)])
    w_out_late = _own_block_copies(lambda x, y, c: [(*_first_axis_chip(x, y, c), c), (1 - x, 1 - y, c)])
    forward = _wait_then_start(
        (across[0], across[1], w_in_pair, None), lambda *a: _second_axis_stage_copies(*a)[:1],
        lambda refs, s, r: _second_axis_forward_copies(refs[:1], s, r) + _group(w_out_early, 1, 1, 1)(refs, s, r),
        3, z_own, "gather_w_in_second_axis_forward", more_bufs=[_with_own_slot(w_out[0].astype(BF16), me)])
    w_in_most = _wait_copies((across[0], across[1], forward[2][:1], None),
                             lambda *a: _second_axis_stage_copies(*a)[1:2], z_own, "gather_w_in_first_forward_wait")
    w_in_most = _wait_copies((forward[0], forward[1], w_in_most, None), _second_axis_forward_copies, z_own,
                             "gather_w_in_second_forward_wait")
    z_early = _z_proj(h, w_in_most[0].reshape(D_IN, D_MODEL), chip, 1, _Z_EARLY_TILES - 1, z_own, "z_proj_early")
    last = _wait_then_start(
        (across[0], across[1], [w_in_most[0], forward[2][1]], None), lambda *a: _second_axis_stage_copies(*a)[2:],
        lambda refs, s, r: _diagonal_forward_copies(refs[:1], s, r) + _group(w_out_late, 1, 1, 1)(refs, s, r),
        3, z_early, "gather_w_in_last_stage")
    w_in_all = _wait_copies((last[0], last[1], last[2][:1], None), _diagonal_forward_copies, z_early,
                            "gather_w_in_last_wait")[0]
    w_in_t = w_in_all.reshape(D_IN, D_MODEL)
    z = _z_proj(h, w_in_t, chip, _Z_EARLY_TILES, 7 - _Z_EARLY_TILES, z_early, "z_proj_late")
    w_out_half = _wait_copies((forward[0], forward[1], last[2][1:], None), _group(w_out_early, 0, 1, 1), z,
                              "gather_w_out_early_wait")
    w_out_flight = _wait_then_start((last[0], last[1], w_out_half, None), _group(w_out_late, 0, 1, 1),
                                    _forward_copies, 3, z, "gather_w_out_forward_stage")
    sink_rows = jnp.repeat(attn_sinks.reshape(N_Q_HEADS), BLOCK).reshape(2, 1, 8 * BLOCK)
    sgu_bt = sgu_b[0].T
    a, probs, sink_probs = _mixer_fwd(z, sink_rows + w_out_flight[3][0, 0], sgu_ln_g, sgu_ln_b, sgu_w[0], sgu_bt)
    w_out_all = _wait_copies(w_out_flight, _forward_copies, a, "gather_w_out_forward_wait")[0]
    w_out_full = w_out_all.reshape(D_MODEL, D_MODEL)
    final_g_row = final_g.reshape(1, D_MODEL)
    dx2, dy, loss_part, d_final_g, d_gate = _out_proj_head(a, w_out_full, x2d, target, mod, final_g_row)

    da = _matmul(dy, w_out_full, "nt", F32, min(t, 1024), 1024, "out_proj_bwd")
    dw_out = _matmul(a, dy, "tn", BF16, 1024, 1024, "w_out_grad").reshape(4, 2, W_OUT_SHARD, D_MODEL)
    pair_out = _pair_reduce(dw_out, "w_out_grad_pair_reduce", W_OUT_SHARD // 2)
    dz, d_sinks, d_sgu_w, d_sgu_b, d_ln_g, d_ln_b = _mixer_bwd(
        z, da, probs, sink_probs, sgu_ln_g, sgu_ln_b, sgu_w[0], jnp.swapaxes(sgu_w[0], 1, 2), sgu_bt)
    sgu_w_to_all = _group(_own_block_copies(_all_others), 2, 1, 3)
    both = _start_copies(
        [pair_out, lax.empty((3, W_OUT_SHARD, D_MODEL), BF16), _with_own_slot(d_sgu_w, me)],
        lambda refs, s, r: _chip_copies(refs[:2], s, r) + sgu_w_to_all(refs, s, r), 3 + N_DEV - 1, core,
        "w_out_grad_chip_and_sgu_w_gather_start")
    out_flight, sgu_w_flight = (both[0], both[1], both[2][:2], None), (both[0], both[1], both[2][2:], None)
    dw_in_t = _matmul(dz, h, "tn", BF16, 768, D_MODEL, "w_in_grad", dep=both[3])
    dw_in_t = dw_in_t.reshape(4, 2, W_IN_SHARD, D_MODEL)
    pair_in = _pair_reduce(dw_in_t, "w_in_grad_pair_reduce", W_IN_SHARD // 3)
    hop1 = _start_copies([pair_in, lax.empty((2, W_IN_SHARD, D_MODEL), BF16)], _first_hop_copies, 2, core,
                         "w_in_grad_first_hop_start")
    grad_x, d_shift, d_scale, d_norm_g = _z_proj_bwd_norm(dz, w_in_t, x2d, rinv, dx2, norm_g, mod, hop1[3])

    partial = _pack_small(d_shift, d_scale, d_gate, d_norm_g, d_final_g, d_ln_g, d_ln_b, loss_part, d_sinks, d_sgu_b)
    small_flight = _start_copies([_with_own_slot(partial, me)], _own_block_copies(_all_others), N_DEV - 1, core,
                                 "small_grad_gather_start")
    pair_in, land_first = _wait_copies(hop1, _first_hop_copies, small_flight[3], "w_in_grad_first_hop_wait")
    second_chip = (2 * ((xi + ci) % 2) + (yi + 1 - ci) % 2).astype(jnp.int32).reshape(1)
    relay = _relay_sum(second_chip, pair_in, land_first, W_IN_SHARD // 3)
    hop2 = _start_copies([relay, lax.empty((1, W_IN_SHARD, D_MODEL), BF16)], _second_hop_copies, 1, core,
                         "w_in_grad_second_hop_start")
    pair_out, land_out = _wait_copies(out_flight, _chip_copies, hop2[3], "w_out_grad_chip_wait")
    big = {"w_out": _adam_from_chips(chip, pair_out, [(land_out, k) for k in range(3)], w_out[0], m_w_out[0],
                                     v_w_out[0], "adam_w_out", 1024)}
    partial_all = _wait_copies(small_flight, _own_block_copies(_all_others), big["w_out"][0],
                               "small_grad_gather_wait")[0]
    d_sgu_w_all = _wait_copies(sgu_w_flight, _group(_own_block_copies(_all_others), 0, 1, 3), partial_all,
                               "sgu_w_grad_gather_wait")[0]
    weights = {"norm_g": norm_g, "b_ada": b_ada, "attn_sinks": attn_sinks, "sgu_ln_g": sgu_ln_g,
               "sgu_ln_b": sgu_ln_b, "sgu_w": sgu_w, "sgu_b": sgu_b, "final_g": final_g_row}
    moments_m = {"norm_g": m_norm_g, "b_ada": m_b_ada, "attn_sinks": m_attn_sinks, "sgu_ln_g": m_sgu_ln_g,
                 "sgu_ln_b": m_sgu_ln_b, "sgu_w": m_sgu_w, "sgu_b": m_sgu_b,
                 "final_g": m_final_g.reshape(1, D_MODEL)}
    moments_v = {"norm_g": v_norm_g, "b_ada": v_b_ada, "attn_sinks": v_attn_sinks, "sgu_ln_g": v_sgu_ln_g,
                 "sgu_ln_b": v_sgu_ln_b, "sgu_w": v_sgu_w, "sgu_b": v_sgu_b,
                 "final_g": v_final_g.reshape(1, D_MODEL)}
    loss, dmod_all, small = _adam_small(partial_all, d_sgu_w_all, weights, moments_m, moments_v)
    small["final_g"] = tuple(o.reshape(D_MODEL) for o in small["final_g"])

    big["w_ada"] = _adam_w_ada(device, c_act.T, dmod_all, w_ada[0], m_w_ada[0], v_w_ada[0])
    _, land_second = _wait_copies(hop2, _second_hop_copies, big["w_ada"][0], "w_in_grad_second_hop_wait")
    big["w_in"] = tuple(o.T for o in _adam_from_chips(
        chip, pair_in, [(land_first, 0), (land_second, 0)], w_in[0].T, m_w_in[0].T, v_w_in[0].T, "adam_w_in", 512))
    order = ["norm_g", "w_ada", "b_ada", "w_in", "attn_sinks", "sgu_ln_g", "sgu_ln_b", "sgu_w", "sgu_b", "w_out",
             "final_g"]
    outs = [loss.reshape(()), grad_x[None]]
    for k in range(4):
        for name in order:
            outs.append(big[name][k][None] if name in big else small[name][k])
    return tuple(outs)
```

```python
import jax
import jax.numpy as jnp
from jax import lax
from jax.experimental import pallas as pl
from jax.experimental.pallas import tpu as pltpu

F32 = jnp.float32
BF16 = jnp.bfloat16
MESH = pl.DeviceIdType.MESH

N_DEV = 8
D_MODEL = 2048
HEAD_DIM = 64
D_ATTN = 1024
N_Q_HEADS = 16
D_KV = 128
BLOCK = 128
D_SGU = 1024
SGU_GROUPS = 8
D_IN = 5376
W_IN_SHARD = D_IN // N_DEV
W_OUT_SHARD = D_MODEL // N_DEV
W_ADA_SHARD = 3 * D_MODEL // N_DEV
EPS = 1e-6
ATTN_SCALE = 0.125

ADAM_LR = 0.001
ADAM_B1 = 0.9
ADAM_B2 = 0.999
ADAM_EPS = 1e-08
ADAM_WD = 0.01
ADAM_STEP = 10

SEG_Q, SEG_KV, SEG_GA, SEG_U, SEG_VS, SEG_GS = 0, 1024, 1280, 2304, 3328, 4352

VMEM_LIMIT = 56 * 1024 * 1024

ROW_SHIFT, ROW_SCALE, ROW_GATE, ROW_NORM_G, ROW_FINAL_G, ROW_LN, ROW_MISC, ROW_SGU_B = 0, 1, 2, 3, 4, 5, 6, 8
SMALL_ROWS = 16


def _params(**kw):
    return pltpu.CompilerParams(vmem_limit_bytes=VMEM_LIMIT, **kw)


def _sigmoid(x):
    return 0.5 * (jnp.tanh(0.5 * x) + 1.0)


def _place():
    return lax.axis_index("x"), lax.axis_index("y"), lax.axis_index("c")


def _pair_reduce(blocks, name, row_chunk):
    _, _, r, cols = blocks.shape
    assert r % row_chunk == 0

    def body(in_ref, out_ref, land, own, summed, send_sems, recv_sems, own_sems, out_sems):
        x, y, c = _place()
        sends, loads, stores = [], [], []
        for m in range(4):
            cp = pltpu.make_async_remote_copy(
                src_ref=in_ref.at[m, 1 - c], dst_ref=land.at[m], send_sem=send_sems.at[m], recv_sem=recv_sems.at[m],
                device_id=(x, y, 1 - c), device_id_type=MESH)
            cp.start()
            sends.append(cp)
            ld = pltpu.make_async_copy(in_ref.at[m, c], own.at[m], own_sems.at[m])
            ld.start()
            loads.append(ld)
        for m in range(4):
            sends[m].wait_recv()
            loads[m].wait()
            for k in range(r // row_chunk):
                rows = slice(k * row_chunk, (k + 1) * row_chunk)
                summed[m, rows, :] = (own[m, rows, :].astype(F32) + land[m, rows, :].astype(F32)).astype(BF16)
            st = pltpu.make_async_copy(summed.at[m], out_ref.at[m], out_sems.at[m])
            st.start()
            stores.append(st)
        for m in range(4):
            sends[m].wait_send()
            stores[m].wait()

    spec = pl.BlockSpec(memory_space=pl.ANY)
    return pl.pallas_call(
        body, name=name, out_shape=jax.ShapeDtypeStruct((4, r, cols), BF16),
        in_specs=[spec], out_specs=spec,
        scratch_shapes=[pltpu.VMEM((4, r, cols), BF16), pltpu.VMEM((4, r, cols), BF16), pltpu.VMEM((4, r, cols), BF16),
                        pltpu.SemaphoreType.DMA((4,)), pltpu.SemaphoreType.DMA((4,)), pltpu.SemaphoreType.DMA((4,)),
                        pltpu.SemaphoreType.DMA((4,))],
        compiler_params=_params(),
    )(blocks)


_HBM = pl.BlockSpec(memory_space=pltpu.HBM)
_SEM = pl.BlockSpec(memory_space=pltpu.SEMAPHORE)
_EFFECT = pltpu.SideEffectType.DATAFLOW_SIDE_EFFECTING


def _start_copies(bufs, copies, n_copies, after, name):
    nb = len(bufs)

    def body(*refs):
        for cp in copies(refs[:nb], refs[nb + 1], refs[nb + 2]):
            cp.start()
        refs[-1][...] = jnp.zeros_like(refs[-1])

    out = pl.pallas_call(
        body, name=name,
        out_shape=(pltpu.SemaphoreType.DMA((n_copies,)), pltpu.SemaphoreType.DMA((n_copies,)),
                   *[pltpu.HBM(b.shape, b.dtype) for b in bufs], jax.ShapeDtypeStruct((8, 128), F32)),
        in_specs=(_HBM,) * nb + (pl.BlockSpec(memory_space=pl.ANY),),
        out_specs=(_SEM, _SEM) + (_HBM,) * nb + (pl.BlockSpec(memory_space=pltpu.VMEM),),
        input_output_aliases={i: 2 + i for i in range(nb)},
        compiler_params=pltpu.CompilerParams(has_side_effects=_EFFECT),
    )(*[pltpu.with_memory_space_constraint(b, pltpu.HBM) for b in bufs], after)
    return out[0], out[1], list(out[2:2 + nb]), out[-1]


def _wait_copies(flight, copies, after, name):
    send_sems, recv_sems, bufs, _ = flight
    nb = len(bufs)

    def body(*refs):
        for cp in copies(refs[:nb], refs[nb], refs[nb + 1]):
            cp.wait_send()
            cp.wait_recv()

    return pl.pallas_call(
        body, name=name,
        out_shape=tuple(pltpu.HBM(b.shape, b.dtype) for b in bufs),
        in_specs=(_HBM,) * nb + (_SEM, _SEM, pl.BlockSpec(memory_space=pl.ANY)), out_specs=(_HBM,) * nb,
        input_output_aliases={i: i for i in range(nb)},
        compiler_params=pltpu.CompilerParams(has_side_effects=_EFFECT),
    )(*bufs, send_sems, recv_sems, after)


class _From:
    def __init__(self, sems, offset):
        self.sems, self.offset = sems, offset

    @property
    def at(self):
        return self

    def __getitem__(self, k):
        return self.sems.at[k + self.offset]


def _group(copies, first_buf, n_bufs, offset):
    def grouped(refs, send_sems, recv_sems):
        return copies(refs[first_buf:first_buf + n_bufs], _From(send_sems, offset), _From(recv_sems, offset))
    return grouped


def _wait_then_start(flight, waited, started, n_started, after, name, more_bufs=()):
    old_send, old_recv, bufs, _ = flight
    bufs = list(bufs) + [pltpu.with_memory_space_constraint(b, pltpu.HBM) for b in more_bufs]
    nb = len(bufs)

    def body(*refs):
        for cp in waited(refs[:nb], refs[nb], refs[nb + 1]):
            cp.wait_send()
            cp.wait_recv()
        for cp in started(refs[:nb], refs[nb + 3], refs[nb + 4]):
            cp.start()
        refs[-1][...] = jnp.zeros_like(refs[-1])

    out = pl.pallas_call(
        body, name=name,
        out_shape=(pltpu.SemaphoreType.DMA((n_started,)), pltpu.SemaphoreType.DMA((n_started,)),
                   *[pltpu.HBM(b.shape, b.dtype) for b in bufs], jax.ShapeDtypeStruct((8, 128), F32)),
        in_specs=(_HBM,) * nb + (_SEM, _SEM, pl.BlockSpec(memory_space=pl.ANY)),
        out_specs=(_SEM, _SEM) + (_HBM,) * nb + (pl.BlockSpec(memory_space=pltpu.VMEM),),
        input_output_aliases={i: 2 + i for i in range(nb)},
        compiler_params=pltpu.CompilerParams(has_side_effects=_EFFECT),
    )(*bufs, old_send, old_recv, after)
    return out[0], out[1], list(out[2:2 + nb]), out[-1]


def _sibling_copies(refs, send_sems, recv_sems):
    blocks_ref, land_ref = refs
    x, y, c = _place()
    return [pltpu.make_async_remote_copy(
        src_ref=blocks_ref.at[m, 1 - c], dst_ref=land_ref.at[m], send_sem=send_sems.at[m], recv_sem=recv_sems.at[m],
        device_id=(x, y, 1 - c), device_id_type=MESH) for m in range(4)]


def _chip_copies(refs, send_sems, recv_sems):
    pair_ref, land_ref = refs
    x, y, c = _place()
    chips = [(1 - x, y), (x, 1 - y), (1 - x, 1 - y)]
    return [pltpu.make_async_remote_copy(
        src_ref=pair_ref.at[2 * chip[0] + chip[1]], dst_ref=land_ref.at[k],
        send_sem=send_sems.at[k], recv_sem=recv_sems.at[k],
        device_id=(*chip, c), device_id_type=MESH) for k, chip in enumerate(chips)]


def _first_hop_copies(refs, send_sems, recv_sems):
    pair_ref, land_ref = refs
    x, y, c = _place()
    first = ((x + 1 - c) % 2, (y + c) % 2)
    blocks = [2 * first[0] + first[1], 2 * (1 - x) + (1 - y)]
    return [pltpu.make_async_remote_copy(
        src_ref=pair_ref.at[blocks[k]], dst_ref=land_ref.at[k], send_sem=send_sems.at[k], recv_sem=recv_sems.at[k],
        device_id=(*first, c), device_id_type=MESH) for k in range(2)]


def _second_hop_copies(refs, send_sems, recv_sems):
    relay_ref, land_ref = refs
    x, y, c = _place()
    second = ((x + c) % 2, (y + 1 - c) % 2)
    return [pltpu.make_async_remote_copy(
        src_ref=relay_ref, dst_ref=land_ref.at[0], send_sem=send_sems.at[0], recv_sem=recv_sems.at[0],
        device_id=(*second, c), device_id_type=MESH)]


def _own_block_copies(targets):
    def copies(refs, send_sems, recv_sems):
        x, y, c = _place()
        mine = refs[0].at[4 * x + 2 * y + c]
        return [pltpu.make_async_remote_copy(
            src_ref=mine, dst_ref=mine, send_sem=send_sems.at[k], recv_sem=recv_sems.at[k],
            device_id=to, device_id_type=MESH) for k, to in enumerate(targets(x, y, c))]
    return copies


def _all_others(x, y, c):
    flip = lambda v, f: 1 - v if f else v
    return [(flip(x, r & 4), flip(y, r & 2), flip(c, r & 1)) for r in range(1, N_DEV)]


def _forward_copies(refs, send_sems, recv_sems):
    x, y, c = _place()
    chips = [(1 - x, y), (x, 1 - y), (1 - x, 1 - y)]
    return [pltpu.make_async_remote_copy(
        src_ref=refs[0].at[4 * chip[0] + 2 * chip[1] + c], dst_ref=refs[0].at[4 * chip[0] + 2 * chip[1] + c],
        send_sem=send_sems.at[k], recv_sem=recv_sems.at[k],
        device_id=(x, y, 1 - c), device_id_type=MESH) for k, chip in enumerate(chips)]


def _first_axis_chip(x, y, c):
    return (x + 1 - c) % 2, (y + c) % 2


def _second_axis_chip(x, y, c):
    return (x + c) % 2, (y + 1 - c) % 2


def _first_targets(x, y, c):
    return [(x, y, 1 - c), (*_first_axis_chip(x, y, c), c)]


def _all_gather_small(shard, name):
    def body(in_ref, out_ref, send_sems, recv_sems, local_sem):
        x, y, c = _place()
        me, sibling = 4 * x + 2 * y + c, (x, y, 1 - c)
        first, second = _first_axis_chip(x, y, c), _second_axis_chip(x, y, c)

        def pair(chip):
            return out_ref.at[pl.ds(2 * (2 * chip[0] + chip[1]), 2)]

        def exchange(k, src, dst, to):
            cp = pltpu.make_async_remote_copy(src_ref=src, dst_ref=dst, send_sem=send_sems.at[k],
                                              recv_sem=recv_sems.at[k], device_id=to, device_id_type=MESH)
            cp.start()
            cp.wait()

        own = pltpu.make_async_copy(in_ref, out_ref.at[me], local_sem)
        own.start()
        exchange(0, in_ref, out_ref.at[me], sibling)
        own.wait()
        exchange(1, pair((x, y)), pair((x, y)), (*second, c))
        exchange(2, pair(second), pair(second), sibling)
        exchange(3, pair(first), pair(first), (*second, c))

    spec = pl.BlockSpec(memory_space=pltpu.VMEM)
    return pl.pallas_call(
        body, name=name, out_shape=jax.ShapeDtypeStruct((N_DEV,) + shard.shape, shard.dtype),
        in_specs=[spec], out_specs=spec,
        scratch_shapes=[pltpu.SemaphoreType.DMA((4,)), pltpu.SemaphoreType.DMA((4,)), pltpu.SemaphoreType.DMA],
        compiler_params=_params(),
    )(shard)


def _slot_copies(refs, send_sems, recv_sems, plan):
    copies = []
    for k, ((px, py, pc), to) in enumerate(plan):
        blk = refs[0].at[4 * px + 2 * py + pc]
        copies.append(pltpu.make_async_remote_copy(
            src_ref=blk, dst_ref=blk, send_sem=send_sems.at[k], recv_sem=recv_sems.at[k],
            device_id=to, device_id_type=MESH))
    return copies


def _second_axis_stage_copies(refs, send_sems, recv_sems):
    x, y, c = _place()
    first, second = (*_first_axis_chip(x, y, c), c), (*_second_axis_chip(x, y, c), c)
    return _slot_copies(refs, send_sems, recv_sems, [((x, y, c), second), (first, (x, y, 1 - c)), (first, second)])


def _second_axis_forward_copies(refs, send_sems, recv_sems):
    x, y, c = _place()
    return _slot_copies(refs, send_sems, recv_sems, [((*_second_axis_chip(x, y, c), c), (x, y, 1 - c))])


def _diagonal_forward_copies(refs, send_sems, recv_sems):
    x, y, c = _place()
    blk = refs[0].at[4 * (1 - x) + 2 * (1 - y) + c]
    return [pltpu.make_async_remote_copy(
        src_ref=blk, dst_ref=blk, send_sem=send_sems.at[0], recv_sem=recv_sems.at[0],
        device_id=(x, y, 1 - c), device_id_type=MESH)]


def _with_own_slot(block, me):
    return lax.dynamic_update_index_in_dim(lax.empty((N_DEV,) + block.shape, block.dtype), block, me, 0)


def _matmul(a, b, dims, out_dtype, tm, tn, name, dep=None):
    if dims == "nn":
        (m, k), n = a.shape, b.shape[1]
        a_spec = pl.BlockSpec((tm, k), lambda i, j: (i, 0))
        b_spec = pl.BlockSpec((k, tn), lambda i, j: (0, j))
        contract = ((1,), (0,))
    elif dims == "nt":
        (m, k), n = a.shape, b.shape[0]
        a_spec = pl.BlockSpec((tm, k), lambda i, j: (i, 0))
        b_spec = pl.BlockSpec((tn, k), lambda i, j: (j, 0))
        contract = ((1,), (1,))
    else:
        (k, m), n = a.shape, b.shape[1]
        a_spec = pl.BlockSpec((k, tm), lambda i, j: (0, i))
        b_spec = pl.BlockSpec((k, tn), lambda i, j: (0, j))
        contract = ((0,), (0,))
    assert m % tm == 0 and n % tn == 0 and a.dtype == BF16 and b.dtype == BF16

    def body(a_ref, b_ref, *rest):
        rest[-1][...] = lax.dot_general(a_ref[...], b_ref[...], (contract, ((), ())),
                                        preferred_element_type=F32).astype(out_dtype)

    deps = [] if dep is None else [dep]
    return pl.pallas_call(
        body, name=name, grid=(m // tm, n // tn),
        in_specs=[a_spec, b_spec] + [pl.BlockSpec((8, 128), lambda i, j: (0, 0))] * len(deps),
        out_specs=pl.BlockSpec((tm, tn), lambda i, j: (i, j)),
        out_shape=jax.ShapeDtypeStruct((m, n), out_dtype),
        compiler_params=_params(dimension_semantics=("arbitrary", "arbitrary")),
    )(a, b, *deps)


Z_TILE = 768
_Z_TILE_ORDER = ((0, 1, 2, 3, 4, 5, 6), (2, 0, 1, 6, 3, 4, 5), (4, 0, 5, 6, 1, 2, 3), (6, 2, 3, 4, 0, 1, 5))
_Z_EARLY_TILES = 4


def _z_proj(h, w_in_t, chip, first, count, z_prev, name, dep=None):
    t = h.shape[0]

    def body(chip_ref, h_ref, w_ref, *rest):
        rest[-1][...] = _dot_nt(h_ref[...], w_ref[...])

    def tile(j, chip_ref):
        picked = 0
        for c, order in enumerate(_Z_TILE_ORDER):
            for k in range(count):
                picked = picked + jnp.where((chip_ref[0] == c) & (j == k), order[first + k], 0)
        return picked

    prev = [] if z_prev is None else [z_prev]
    deps = [] if dep is None else [dep]
    return pl.pallas_call(
        body, name=name,
        grid_spec=pltpu.PrefetchScalarGridSpec(
            num_scalar_prefetch=1, grid=(count,),
            in_specs=[pl.BlockSpec((t, D_MODEL), lambda j, o: (0, 0)),
                      pl.BlockSpec((Z_TILE, D_MODEL), lambda j, o: (tile(j, o), 0))]
            + [pl.BlockSpec(memory_space=pl.ANY)] * len(prev)
            + [pl.BlockSpec((8, 128), lambda j, o: (0, 0))] * len(deps),
            out_specs=pl.BlockSpec((t, Z_TILE), lambda j, o: (0, tile(j, o)))),
        out_shape=jax.ShapeDtypeStruct((t, D_IN), F32),
        input_output_aliases={3: 0} if prev else {},
        compiler_params=_params(dimension_semantics=("arbitrary",)),
    )(chip, h, w_in_t, *prev, *deps)


def _modulation(device, c_all, w_ada, b_ada):
    def body(device_ref, c_ref, w_ref, b_ref, act_ref, mod_ref):
        cv = c_ref[...]
        act = cv * _sigmoid(cv)
        act_ref[...] = act
        mod_ref[...] = jnp.dot(act.astype(BF16), w_ref[...].astype(BF16), preferred_element_type=F32) + b_ref[...]

    whole = lambda a: pl.BlockSpec(a.shape, lambda i, device_ref: (0,) * a.ndim)
    return pl.pallas_call(
        body, name="modulation",
        grid_spec=pltpu.PrefetchScalarGridSpec(
            num_scalar_prefetch=1, grid=(1,),
            in_specs=[whole(c_all), whole(w_ada), pl.BlockSpec((1, W_ADA_SHARD), lambda i, device_ref: (0, device_ref[0]))],
            out_specs=(whole(c_all), pl.BlockSpec((N_DEV, W_ADA_SHARD), lambda i, device_ref: (0, 0)))),
        out_shape=(jax.ShapeDtypeStruct(c_all.shape, F32), jax.ShapeDtypeStruct((N_DEV, W_ADA_SHARD), F32)),
        compiler_params=_params(dimension_semantics=("arbitrary",)),
    )(device, c_all, w_ada, b_ada)


MOD_SHIFT, MOD_SCALE, MOD_GATE = 0, 1, 2


def _mod_spec(part, d):
    return pl.BlockSpec((1, d), lambda i: (0, part))


def _modulated_norm(x, norm_g, mod, tm=512):
    t, d = x.shape

    def body(x_ref, g_ref, sc_ref, sh_ref, h_ref):
        xv = x_ref[...]
        r = lax.rsqrt(jnp.mean(xv * xv, axis=-1, keepdims=True) + EPS)
        h = (xv * r) * g_ref[...] * (1.0 + sc_ref[...]) + sh_ref[...]
        h_ref[...] = h.astype(BF16)

    row = pl.BlockSpec((1, d), lambda i: (0, 0))
    return pl.pallas_call(
        body, name="modulated_norm", grid=(t // tm,),
        in_specs=[pl.BlockSpec((tm, d), lambda i: (i, 0)), row, _mod_spec(MOD_SCALE, d), _mod_spec(MOD_SHIFT, d)],
        out_specs=pl.BlockSpec((tm, d), lambda i: (i, 0)),
        out_shape=jax.ShapeDtypeStruct((t, d), BF16),
        compiler_params=_params(dimension_semantics=("arbitrary",)),
    )(x, norm_g, mod, mod)


def _window_bias(block_index):
    s = lax.broadcasted_iota(jnp.int32, (2 * BLOCK, BLOCK), 0)
    t = lax.broadcasted_iota(jnp.int32, (2 * BLOCK, BLOCK), 1)
    valid = ((s < BLOCK) & (s > t) & (block_index > 0)) | ((s >= BLOCK) & ((s - BLOCK) <= t))
    bias = jnp.where(valid, 0.0, -jnp.inf).astype(F32)
    return jnp.concatenate([bias] * 8, axis=1)


def _heads_t(pair_blocks, g):
    top = lax.broadcasted_iota(jnp.int32, (BLOCK, BLOCK), 0) < HEAD_DIM
    zeros = jnp.zeros((HEAD_DIM, BLOCK), F32)
    tiles = []
    for blk in pair_blocks:
        tp = blk.T
        if g == 0:
            tiles += [jnp.where(top, tp, 0.0), jnp.concatenate([tp[HEAD_DIM:], zeros], axis=0)]
        else:
            tiles += [jnp.concatenate([zeros, tp[:HEAD_DIM]], axis=0), jnp.where(top, 0.0, tp)]
    return jnp.concatenate(tiles, axis=1)


def _pair_block(xt, p, g):
    r0 = HEAD_DIM * g
    even = xt[r0:r0 + HEAD_DIM, (2 * p) * BLOCK:(2 * p + 1) * BLOCK]
    odd = xt[r0:r0 + HEAD_DIM, (2 * p + 1) * BLOCK:(2 * p + 2) * BLOCK]
    return jnp.concatenate([even, odd], axis=0).T


def _softmax_t(scores_t, bias, sink):
    st = scores_t + bias
    m = jnp.maximum(jnp.max(st, axis=0, keepdims=True), sink)
    e = jnp.exp(st - m)
    es = jnp.exp(sink - m)
    inv = 1.0 / (jnp.sum(e, axis=0, keepdims=True) + es)
    return e * inv, es * inv


def _dot(a, b):
    return jnp.dot(a, b, preferred_element_type=F32)


def _dot_nt(a, b):
    return lax.dot_general(a, b, (((1,), (1,)), ((), ())), preferred_element_type=F32)


def _layer_norm_fwd(v):
    mu = jnp.mean(v, axis=-1, keepdims=True)
    xc = v - mu
    rstd = lax.rsqrt(jnp.mean(xc * xc, axis=-1, keepdims=True) + EPS)
    return xc * rstd, rstd


def _tril(transposed=False):
    t = lax.broadcasted_iota(jnp.int32, (BLOCK, BLOCK), 0)
    s = lax.broadcasted_iota(jnp.int32, (BLOCK, BLOCK), 1)
    return s >= t if transposed else t >= s


def _const_spec(shape):
    return pl.BlockSpec(shape, lambda i: (0,) * len(shape))


def _keys_values(z_ref, kvp):
    kvc = z_ref[:, SEG_KV:SEG_KV + 2 * D_KV]
    kk = jnp.concatenate([kvp[:, :D_KV], kvc[:, :D_KV]], axis=0)
    vv = jnp.concatenate([kvp[:, D_KV:], kvc[:, D_KV:]], axis=0)
    return kk, vv


MIXER_BLOCKS = 2


class _Rows:
    def __init__(self, ref, sub):
        self.ref, self.rows = ref, slice(sub * BLOCK, (sub + 1) * BLOCK)

    def __getitem__(self, idx):
        return self.ref[self.rows, idx[1]]

    def __setitem__(self, idx, value):
        self.ref[self.rows, idx[1]] = value


def _kv_before_spec(index):
    return pl.BlockSpec((BLOCK, 2 * D_KV),
                        lambda i: (jnp.maximum(MIXER_BLOCKS * index(i) - 1, 0), SEG_KV // (2 * D_KV)))


def _pair_cols(g, p, base=0):
    return slice(base + (4 * g + p) * 128, base + (4 * g + p + 1) * 128)


def _mixer_fwd(z, sink_rows, ln_g, ln_b, sgu_w, sgu_bt):
    t = z.shape[0]

    def body(z_all, kvp_ref, sink_ref, lng_ref, lnb_ref, w_ref, bt_ref, a_all, prob_ref, sink_prob_ref):
        kv_before = kvp_ref[...]
        for sub in range(MIXER_BLOCKS):
            z_ref, a_ref = _Rows(z_all, sub), _Rows(a_all, sub)
            one_block(z_ref, kv_before, MIXER_BLOCKS * pl.program_id(0) + sub, sink_ref, lng_ref, lnb_ref, w_ref,
                      bt_ref, a_ref, prob_ref.at[sub], sink_prob_ref.at[sub])
            kv_before = z_ref[:, SEG_KV:SEG_KV + 2 * D_KV]

    def one_block(z_ref, kv_before, block_index, sink_ref, lng_ref, lnb_ref, w_ref, bt_ref, a_ref, prob_ref,
                  sink_prob_ref):
        bias = _window_bias(block_index)
        kk, vv = _keys_values(z_ref, kv_before)
        kk_b, vvt_b = kk.astype(BF16), vv.T.astype(BF16)
        for g in range(2):
            qt = _heads_t([z_ref[:, _pair_cols(g, p, SEG_Q)] * ATTN_SCALE for p in range(4)], g).astype(BF16)
            prob, sink_prob = _softmax_t(_dot(kk_b, qt), bias, sink_ref[g])
            prob_b = prob.astype(BF16)
            prob_ref[g] = prob_b
            sink_prob_ref[g] = sink_prob
            ot = _dot(vvt_b, prob_b)
            for p in range(4):
                gate = z_ref[:, _pair_cols(g, p, SEG_GA)]
                a_ref[:, _pair_cols(g, p)] = (_pair_block(ot, p, g) * (gate * _sigmoid(gate))).astype(BF16)

        vhat, _ = _layer_norm_fwd(z_ref[:, SEG_VS:SEG_VS + D_SGU])
        vn = vhat * lng_ref[...] + lnb_ref[...]
        tril = _tril()
        for g in range(SGU_GROUPS):
            cols = slice(g * 128, (g + 1) * 128)
            wm = jnp.where(tril, w_ref[g], 0.0).astype(BF16)
            mixed = _dot(wm, vn[:, cols].astype(BF16)) + bt_ref[:, g:g + 1]
            gate = z_ref[:, SEG_GS + g * 128:SEG_GS + (g + 1) * 128]
            a_ref[:, D_ATTN + g * 128:D_ATTN + (g + 1) * 128] = (
                (z_ref[:, SEG_U + g * 128:SEG_U + (g + 1) * 128] * mixed) * (gate * _sigmoid(gate))).astype(BF16)

    rows = MIXER_BLOCKS * BLOCK
    return pl.pallas_call(
        body, name="mixer_fwd", grid=(t // rows,),
        in_specs=[pl.BlockSpec((rows, D_IN), lambda i: (i, 0)), _kv_before_spec(lambda i: i),
                  _const_spec((2, 1, 8 * BLOCK)), _const_spec((1, D_SGU)), _const_spec((1, D_SGU)),
                  _const_spec((SGU_GROUPS, BLOCK, BLOCK)), _const_spec((BLOCK, SGU_GROUPS))],
        out_specs=(pl.BlockSpec((rows, D_MODEL), lambda i: (i, 0)),
                   pl.BlockSpec((MIXER_BLOCKS, 2, 2 * BLOCK, 8 * BLOCK), lambda i: (i, 0, 0, 0)),
                   pl.BlockSpec((MIXER_BLOCKS, 2, 1, 8 * BLOCK), lambda i: (i, 0, 0, 0))),
        out_shape=(jax.ShapeDtypeStruct((t, D_MODEL), BF16),
                   jax.ShapeDtypeStruct((t // BLOCK, 2, 2 * BLOCK, 8 * BLOCK), BF16),
                   jax.ShapeDtypeStruct((t // BLOCK, 2, 1, 8 * BLOCK), F32)),
        compiler_params=_params(dimension_semantics=("arbitrary",)),
    )(z, z, sink_rows, ln_g, ln_b, sgu_w, sgu_bt)


def _mixer_bwd(z, da, probs, sink_probs, ln_g, ln_b, sgu_w, sgu_wt, sgu_bt):
    t = z.shape[0]

    def body(z_all, kvp_ref, da_all, prob_ref, sink_prob_ref, lng_ref, lnb_ref, w_ref, wt_ref, bt_ref,
             dz_all, dsink_ref, dw_ref, db_ref, dlng_ref, dlnb_ref, carry_ref, dsink_acc, dbt_acc):
        step = pl.program_id(0)

        @pl.when(step == 0)
        def _():
            carry_ref[...] = jnp.zeros_like(carry_ref)
            dsink_acc[...] = jnp.zeros_like(dsink_acc)
            dbt_acc[...] = jnp.zeros_like(dbt_acc)
            dw_ref[...] = jnp.zeros_like(dw_ref)
            dlng_ref[...] = jnp.zeros_like(dlng_ref)
            dlnb_ref[...] = jnp.zeros_like(dlnb_ref)

        carry = carry_ref[...]
        for sub in reversed(range(MIXER_BLOCKS)):
            kv_before = kvp_ref[...] if sub == 0 else _Rows(z_all, sub - 1)[:, SEG_KV:SEG_KV + 2 * D_KV]
            carry = one_block(_Rows(z_all, sub), kv_before, _Rows(da_all, sub), prob_ref.at[sub], sink_prob_ref.at[sub],
                              carry, lng_ref, lnb_ref, w_ref, wt_ref, bt_ref, _Rows(dz_all, sub),
                              dw_ref, dlng_ref, dlnb_ref, dsink_acc, dbt_acc)
        carry_ref[...] = carry

        @pl.when(step == ns - 1)
        def _():
            db_ref[...] = dbt_acc[...].T[:SGU_GROUPS]
            lane_row = lax.broadcasted_iota(jnp.int32, (1, 128), 1)
            d_sink = jnp.zeros((1, 128), F32)
            for g in range(2):
                acc = dsink_acc[g]
                for j in range(8):
                    head_sum = jnp.sum(acc[:, j * BLOCK:(j + 1) * BLOCK], axis=-1, keepdims=True)
                    d_sink = d_sink + jnp.where(lane_row == 8 * g + j, head_sum, 0.0)
            dsink_ref[...] = d_sink

    def one_block(z_ref, kv_before, da_ref, prob_ref, sink_prob_ref, carry, lng_ref, lnb_ref, w_ref, wt_ref, bt_ref,
                  dz_ref, dw_ref, dlng_ref, dlnb_ref, dsink_acc, dbt_acc):
        kk, vv = _keys_values(z_ref, kv_before)
        vv_b = vv.astype(BF16)
        kkt_b, vvt_b = kk.T.astype(BF16), vv.T.astype(BF16)
        dkk = jnp.zeros((2 * BLOCK, D_KV), F32)
        dvv = jnp.zeros((2 * BLOCK, D_KV), F32)
        for g in range(2):
            qt = _heads_t([z_ref[:, _pair_cols(g, p, SEG_Q)] * ATTN_SCALE for p in range(4)], g).astype(BF16)
            prob_b, sink_prob = prob_ref[g], sink_prob_ref[g]
            prob = prob_b.astype(F32)
            ot = _dot(vvt_b, prob_b)
            gates = [z_ref[:, _pair_cols(g, p, SEG_GA)] for p in range(4)]
            sig = [_sigmoid(gt) for gt in gates]
            d_attn = [da_ref[:, _pair_cols(g, p)] for p in range(4)]
            d_ot = _heads_t([d_attn[p] * (gates[p] * sig[p]) for p in range(4)], g).astype(BF16)
            d_prob = _dot(vv_b, d_ot)
            delta = jnp.sum(prob * d_prob, axis=0, keepdims=True)
            d_scores = (prob * (d_prob - delta)).astype(BF16)
            dsink_acc[g] -= sink_prob * delta
            d_qt = _dot(kkt_b, d_scores)
            dkk = dkk + _dot_nt(d_scores, qt)
            dvv = dvv + _dot_nt(prob_b, d_ot)
            for p in range(4):
                dz_ref[:, _pair_cols(g, p, SEG_Q)] = (_pair_block(d_qt, p, g) * ATTN_SCALE).astype(BF16)
                d_silu = sig[p] * (1.0 + gates[p] * (1.0 - sig[p]))
                dz_ref[:, _pair_cols(g, p, SEG_GA)] = (d_attn[p] * _pair_block(ot, p, g) * d_silu).astype(BF16)
        d_kv = jnp.concatenate([dkk, dvv], axis=1)
        dz_ref[:, SEG_KV:SEG_KV + 2 * D_KV] = (d_kv[BLOCK:] + carry).astype(BF16)

        vhat, rstd = _layer_norm_fwd(z_ref[:, SEG_VS:SEG_VS + D_SGU])
        lng = lng_ref[...]
        vn = vhat * lng + lnb_ref[...]
        tril, triu = _tril(), _tril(transposed=True)
        lane = lax.broadcasted_iota(jnp.int32, (BLOCK, 128), 1)
        d_bt = jnp.zeros((BLOCK, 128), F32)
        d_vn = []
        for g in range(SGU_GROUPS):
            cols = slice(g * 128, (g + 1) * 128)
            wm = jnp.where(tril, w_ref[g], 0.0).astype(BF16)
            wmt = jnp.where(triu, wt_ref[g], 0.0).astype(BF16)
            vn_g = vn[:, cols].astype(BF16)
            mixed = _dot(wm, vn_g) + bt_ref[:, g:g + 1]
            gate = z_ref[:, SEG_GS + g * 128:SEG_GS + (g + 1) * 128]
            u = z_ref[:, SEG_U + g * 128:SEG_U + (g + 1) * 128]
            d_out = da_ref[:, D_ATTN + g * 128:D_ATTN + (g + 1) * 128]
            sg = _sigmoid(gate)
            d_um = d_out * (gate * sg)
            dz_ref[:, SEG_U + g * 128:SEG_U + (g + 1) * 128] = (d_um * mixed).astype(BF16)
            dz_ref[:, SEG_GS + g * 128:SEG_GS + (g + 1) * 128] = (
                d_out * (u * mixed) * (sg * (1.0 + gate * (1.0 - sg)))).astype(BF16)
            d_mixed = d_um * u
            d_mixed_b = d_mixed.astype(BF16)
            dw_ref[g] += jnp.where(tril, _dot_nt(d_mixed_b, vn_g), 0.0)
            d_bt = d_bt + jnp.where(lane == g, jnp.sum(d_mixed, axis=-1, keepdims=True), 0.0)
            d_vn.append(_dot(wmt, d_mixed_b))
        dbt_acc[...] += d_bt
        d_vn = jnp.concatenate(d_vn, axis=1)
        dlng_ref[...] += jnp.sum(d_vn * vhat, axis=0, keepdims=True)
        dlnb_ref[...] += jnp.sum(d_vn, axis=0, keepdims=True)
        d_vhat = d_vn * lng
        d_v = rstd * (d_vhat - jnp.mean(d_vhat, axis=-1, keepdims=True)
                      - vhat * jnp.mean(d_vhat * vhat, axis=-1, keepdims=True))
        dz_ref[:, SEG_VS:SEG_VS + D_SGU] = d_v.astype(BF16)
        return d_kv[:BLOCK]

    rows = MIXER_BLOCKS * BLOCK
    ns = t // rows
    rev = lambda i: ns - 1 - i
    return pl.pallas_call(
        body, name="mixer_bwd", grid=(ns,),
        in_specs=[pl.BlockSpec((rows, D_IN), lambda i: (rev(i), 0)), _kv_before_spec(rev),
                  pl.BlockSpec((rows, D_MODEL), lambda i: (rev(i), 0)),
                  pl.BlockSpec((MIXER_BLOCKS, 2, 2 * BLOCK, 8 * BLOCK), lambda i: (rev(i), 0, 0, 0)),
                  pl.BlockSpec((MIXER_BLOCKS, 2, 1, 8 * BLOCK), lambda i: (rev(i), 0, 0, 0)),
                  _const_spec((1, D_SGU)), _const_spec((1, D_SGU)),
                  _const_spec((SGU_GROUPS, BLOCK, BLOCK)), _const_spec((SGU_GROUPS, BLOCK, BLOCK)),
                  _const_spec((BLOCK, SGU_GROUPS))],
        out_specs=(pl.BlockSpec((rows, D_IN), lambda i: (rev(i), 0)), _const_spec((1, 128)),
                   _const_spec((SGU_GROUPS, BLOCK, BLOCK)), _const_spec((SGU_GROUPS, BLOCK)),
                   _const_spec((1, D_SGU)), _const_spec((1, D_SGU))),
        out_shape=(jax.ShapeDtypeStruct((t, D_IN), BF16), jax.ShapeDtypeStruct((1, 128), F32),
                   jax.ShapeDtypeStruct((SGU_GROUPS, BLOCK, BLOCK), F32), jax.ShapeDtypeStruct((SGU_GROUPS, BLOCK), F32),
                   jax.ShapeDtypeStruct((1, D_SGU), F32), jax.ShapeDtypeStruct((1, D_SGU), F32)),
        scratch_shapes=[pltpu.VMEM((BLOCK, 2 * D_KV), F32), pltpu.VMEM((2, 1, 8 * BLOCK), F32),
                        pltpu.VMEM((BLOCK, 128), F32)],
        compiler_params=_params(dimension_semantics=("arbitrary",)),
    )(z, z, da, probs, sink_probs, ln_g, ln_b, sgu_w, sgu_wt, sgu_bt)


def _out_proj_head(a, w_out_full, x, target, mod, final_g, tm=256):
    t, d = x.shape

    def body(a_ref, w_ref, x_ref, tg_ref, gate_ref, fg_ref, dx2_ref, dy_ref, loss_ref, dfg_ref, dgate_ref):
        @pl.when(pl.program_id(0) == 0)
        def _():
            loss_ref[...] = jnp.zeros_like(loss_ref)
            dfg_ref[...] = jnp.zeros_like(dfg_ref)
            dgate_ref[...] = jnp.zeros_like(dgate_ref)

        yv, gate, fg = _dot(a_ref[...], w_ref[...]), gate_ref[...], fg_ref[...]
        x2 = x_ref[...] + gate * yv
        r2 = lax.rsqrt(jnp.mean(x2 * x2, axis=-1, keepdims=True) + EPS)
        nrm = x2 * r2
        err = nrm * fg - tg_ref[...]
        loss_ref[...] += 0.5 * jnp.sum(jnp.mean(err * err, axis=-1, keepdims=True), axis=0, keepdims=True)
        fg_d = fg * (1.0 / d)
        err_nrm = err * nrm
        dfg_ref[...] += jnp.sum(err_nrm, axis=0, keepdims=True) * (1.0 / d)
        d_nrm = err * fg_d
        dx2 = r2 * (d_nrm - nrm * jnp.mean(err_nrm * fg_d, axis=-1, keepdims=True))
        dx2_ref[...] = dx2
        dgate_ref[...] += jnp.sum(dx2 * yv, axis=0, keepdims=True)
        dy_ref[...] = (dx2 * gate).astype(BF16)

    blk = pl.BlockSpec((tm, d), lambda i: (i, 0))
    row = _const_spec((1, d))
    whole = pl.BlockSpec(w_out_full.shape, lambda i: (0, 0), pipeline_mode=pl.Buffered(1))
    return pl.pallas_call(
        body, name="out_proj_head", grid=(t // tm,),
        in_specs=[pl.BlockSpec((tm, a.shape[1]), lambda i: (i, 0)), whole, blk, blk, _mod_spec(MOD_GATE, d), row],
        out_specs=(blk, blk, _const_spec((1, 128)), row, row),
        out_shape=(jax.ShapeDtypeStruct((t, d), F32), jax.ShapeDtypeStruct((t, d), BF16),
                   jax.ShapeDtypeStruct((1, 128), F32), jax.ShapeDtypeStruct((1, d), F32),
                   jax.ShapeDtypeStruct((1, d), F32)),
        compiler_params=_params(dimension_semantics=("arbitrary",)),
    )(a, w_out_full, x, target, mod, final_g)


def _z_proj_bwd_norm(dz, w_in_t, x, dx2, norm_g, mod, dep, tm=256):
    t, d = x.shape

    def body(dz_ref, w_ref, x_ref, dx2_ref, g_ref, sc_ref, dep_ref, gx_ref, dshift_ref, dscale_ref, dg_ref):
        @pl.when(pl.program_id(0) == 0)
        def _():
            dshift_ref[...] = jnp.zeros_like(dshift_ref)
            dscale_ref[...] = jnp.zeros_like(dscale_ref)
            dg_ref[...] = jnp.zeros_like(dg_ref)

        dh, xv, g = _dot(dz_ref[...], w_ref[...]), x_ref[...], g_ref[...]
        one_plus = 1.0 + sc_ref[...]
        r = lax.rsqrt(jnp.mean(xv * xv, axis=-1, keepdims=True) + EPS)
        xn = xv * r
        gain = one_plus * g
        dh_xn = dh * xn
        dh_xn_sum = jnp.sum(dh_xn, axis=0, keepdims=True)
        dshift_ref[...] += jnp.sum(dh, axis=0, keepdims=True)
        dscale_ref[...] += dh_xn_sum * g
        dg_ref[...] += dh_xn_sum * one_plus
        d_xn = dh * gain
        gx_ref[...] = dx2_ref[...] + r * (d_xn - xn * jnp.mean(dh_xn * gain, axis=-1, keepdims=True))

    blk = pl.BlockSpec((tm, d), lambda i: (i, 0))
    row = _const_spec((1, d))
    whole = pl.BlockSpec(w_in_t.shape, lambda i: (0, 0), pipeline_mode=pl.Buffered(1))
    return pl.pallas_call(
        body, name="z_proj_bwd_norm", grid=(t // tm,),
        in_specs=[pl.BlockSpec((tm, dz.shape[1]), lambda i: (i, 0)), whole, blk, blk, row, _mod_spec(MOD_SCALE, d),
                  _const_spec((8, 128))],
        out_specs=(blk, row, row, row),
        out_shape=(jax.ShapeDtypeStruct((t, d), F32),) + (jax.ShapeDtypeStruct((1, d), F32),) * 3,
        compiler_params=_params(dimension_semantics=("arbitrary",)),
    )(dz, w_in_t, x, dx2, norm_g, mod, dep)


def _adamw(w, g, m, v):
    m = ADAM_B1 * m + (1.0 - ADAM_B1) * g
    v = ADAM_B2 * v + (1.0 - ADAM_B2) * (g * g)
    m_hat = m / (1.0 - ADAM_B1 ** ADAM_STEP)
    v_hat = v / (1.0 - ADAM_B2 ** ADAM_STEP)
    delta = -ADAM_LR * (m_hat / (jnp.sqrt(v_hat) + ADAM_EPS) + ADAM_WD * w)
    return delta, m, v


def _pair_sum(core, blocks, land, tr):
    _, _, r, c = blocks.shape

    def body(core_ref, a_ref, b_ref, o_ref):
        o_ref[...] = (a_ref[...].astype(F32) + b_ref[...].astype(F32)).astype(BF16)

    return pl.pallas_call(
        body, name="w_in_grad_pair_sum",
        grid_spec=pltpu.PrefetchScalarGridSpec(
            num_scalar_prefetch=1, grid=(4, r // tr),
            in_specs=[pl.BlockSpec((None, None, tr, c), lambda m, i, core_ref: (m, core_ref[0], i, 0)),
                      pl.BlockSpec((None, tr, c), lambda m, i, core_ref: (m, i, 0))],
            out_specs=pl.BlockSpec((None, tr, c), lambda m, i, core_ref: (m, i, 0))),
        out_shape=jax.ShapeDtypeStruct((4, r, c), BF16),
        compiler_params=_params(dimension_semantics=("arbitrary", "arbitrary")),
    )(core, blocks, land)


def _relay_sum(second_chip, pair, land, tr):
    _, r, c = pair.shape

    def body(chip_ref, a_ref, b_ref, o_ref):
        o_ref[...] = (a_ref[...].astype(F32) + b_ref[...].astype(F32)).astype(BF16)

    return pl.pallas_call(
        body, name="w_in_grad_relay_sum",
        grid_spec=pltpu.PrefetchScalarGridSpec(
            num_scalar_prefetch=1, grid=(r // tr,),
            in_specs=[pl.BlockSpec((None, tr, c), lambda i, chip_ref: (chip_ref[0], i, 0)),
                      pl.BlockSpec((None, tr, c), lambda i, chip_ref: (1, i, 0))],
            out_specs=pl.BlockSpec((tr, c), lambda i, chip_ref: (i, 0))),
        out_shape=jax.ShapeDtypeStruct((r, c), BF16),
        compiler_params=_params(dimension_semantics=("arbitrary",)),
    )(second_chip, pair, land)


def _adam_from_chips(chip, pair, landed, w, m, v, name, tc):
    _, r, c = pair.shape
    n = len(landed)

    def body(chip_ref, own_ref, *refs):
        w_ref, m_ref, v_ref, g_ref, d_ref, nm_ref, nv_ref = refs[n:]
        g = own_ref[...].astype(F32)
        for k in range(n):
            g = g + refs[k][...].astype(F32)
        g_ref[...] = g
        d_ref[...], nm_ref[...], nv_ref[...] = _adamw(w_ref[...], g, m_ref[...], v_ref[...])

    def landed_spec(index):
        return pl.BlockSpec((None, r, tc), lambda i, chip_ref: (index, 0, i))

    blk = pl.BlockSpec((r, tc), lambda i, chip_ref: (0, i))
    return pl.pallas_call(
        body, name=name,
        grid_spec=pltpu.PrefetchScalarGridSpec(
            num_scalar_prefetch=1, grid=(c // tc,),
            in_specs=[pl.BlockSpec((None, r, tc), lambda i, chip_ref: (chip_ref[0], 0, i))]
            + [landed_spec(index) for _, index in landed] + [blk, blk, blk],
            out_specs=(blk,) * 4),
        out_shape=(jax.ShapeDtypeStruct((r, c), F32),) * 4,
        compiler_params=_params(dimension_semantics=("arbitrary",)),
    )(chip, pair, *[array for array, _ in landed], w, m, v)


def _adam_w_ada(device, act_t, dmod_all, w, m, v, tr=512):
    r, c = w.shape

    def body(device_ref, a_ref, dm_ref, w_ref, m_ref, v_ref, g_ref, d_ref, nm_ref, nv_ref):
        g = _dot(a_ref[...].astype(BF16), dm_ref[...].astype(BF16))
        g_ref[...] = g
        d_ref[...], nm_ref[...], nv_ref[...] = _adamw(w_ref[...], g, m_ref[...], v_ref[...])

    blk = pl.BlockSpec((tr, c), lambda i, device_ref: (i, 0))
    return pl.pallas_call(
        body, name="adam_w_ada",
        grid_spec=pltpu.PrefetchScalarGridSpec(
            num_scalar_prefetch=1, grid=(r // tr,),
            in_specs=[pl.BlockSpec((tr, N_DEV), lambda i, device_ref: (i, 0)),
                      pl.BlockSpec((N_DEV, c), lambda i, device_ref: (0, device_ref[0])), blk, blk, blk],
            out_specs=(blk,) * 4),
        out_shape=(jax.ShapeDtypeStruct((r, c), F32),) * 4,
        compiler_params=_params(dimension_semantics=("arbitrary",)),
    )(device, act_t, dmod_all, w, m, v)


def _pack_small(d_shift, d_scale, d_gate, d_norm_g, d_final_g, d_ln_g, d_ln_b, loss, d_sinks, d_sgu_b):
    def body(shift_ref, scale_ref, gate_ref, ng_ref, fg_ref, lng_ref, lnb_ref, loss_ref, sink_ref, b_ref, o_ref):
        o_ref[...] = jnp.zeros_like(o_ref)
        o_ref[ROW_SHIFT:ROW_SHIFT + 1, :] = shift_ref[...]
        o_ref[ROW_SCALE:ROW_SCALE + 1, :] = scale_ref[...]
        o_ref[ROW_GATE:ROW_GATE + 1, :] = gate_ref[...]
        o_ref[ROW_NORM_G:ROW_NORM_G + 1, :] = ng_ref[...]
        o_ref[ROW_FINAL_G:ROW_FINAL_G + 1, :] = fg_ref[...]
        o_ref[ROW_LN:ROW_LN + 1, 0:D_SGU] = lng_ref[...]
        o_ref[ROW_LN:ROW_LN + 1, D_SGU:2 * D_SGU] = lnb_ref[...]
        o_ref[ROW_MISC:ROW_MISC + 1, 0:128] = loss_ref[...]
        o_ref[ROW_MISC:ROW_MISC + 1, 128:256] = sink_ref[...]
        o_ref[ROW_SGU_B:ROW_SGU_B + SGU_GROUPS, 0:BLOCK] = b_ref[...]

    return pl.pallas_call(
        body, name="pack_small", out_shape=jax.ShapeDtypeStruct((SMALL_ROWS, D_MODEL), F32),
        compiler_params=_params(),
    )(d_shift, d_scale, d_gate, d_norm_g, d_final_g, d_ln_g, d_ln_b, loss, d_sinks, d_sgu_b)


_SMALL_NAMES = ("norm_g", "b_ada", "attn_sinks", "sgu_ln_g", "sgu_ln_b", "sgu_w", "sgu_b", "final_g")


def _adam_small(partials, d_sgu_w_all, weights, moments_m, moments_v):
    names = _SMALL_NAMES
    k = len(names)

    def body(*refs):
        p_ref, sw_ref = refs[0], refs[1]
        w_refs, m_refs, v_refs = refs[2:2 + k], refs[2 + k:2 + 2 * k], refs[2 + 2 * k:2 + 3 * k]
        loss_ref, dmod_ref = refs[2 + 3 * k], refs[3 + 3 * k]
        out_refs = refs[4 + 3 * k:4 + 7 * k]
        sum_ref = refs[4 + 7 * k]
        total = p_ref[0]
        for j in range(1, N_DEV):
            total = total + p_ref[j]
        sum_ref[...] = total
        for j in range(N_DEV):
            for part, row in enumerate((ROW_SHIFT, ROW_SCALE, ROW_GATE)):
                dmod_ref[j:j + 1, part * D_MODEL:(part + 1) * D_MODEL] = p_ref[j, row:row + 1, :]
        loss_ref[...] = sum_ref[ROW_MISC:ROW_MISC + 1, 0:1]
        d_sgu_w = sw_ref[0]
        for j in range(1, N_DEV):
            d_sgu_w = d_sgu_w + sw_ref[j]
        grads = {
            "norm_g": sum_ref[ROW_NORM_G:ROW_NORM_G + 1, :],
            "b_ada": jnp.concatenate([sum_ref[r:r + 1, :] for r in (ROW_SHIFT, ROW_SCALE, ROW_GATE)], axis=1),
            "attn_sinks": sum_ref[ROW_MISC:ROW_MISC + 1, 128:128 + N_Q_HEADS],
            "sgu_ln_g": sum_ref[ROW_LN:ROW_LN + 1, 0:D_SGU],
            "sgu_ln_b": sum_ref[ROW_LN:ROW_LN + 1, D_SGU:2 * D_SGU],
            "sgu_w": d_sgu_w[None],
            "sgu_b": sum_ref[ROW_SGU_B:ROW_SGU_B + SGU_GROUPS, 0:BLOCK][None],
            "final_g": sum_ref[ROW_FINAL_G:ROW_FINAL_G + 1, :],
        }
        for i, name in enumerate(names):
            g = grads[name]
            delta, m, v = _adamw(w_refs[i][...], g, m_refs[i][...], v_refs[i][...])
            out_refs[4 * i][...] = g
            out_refs[4 * i + 1][...] = delta
            out_refs[4 * i + 2][...] = m
            out_refs[4 * i + 3][...] = v

    shapes = [jax.ShapeDtypeStruct((1, 1), F32), jax.ShapeDtypeStruct((N_DEV, 3 * D_MODEL), F32)]
    for name in names:
        shapes += [jax.ShapeDtypeStruct(weights[name].shape, F32)] * 4
    outs = pl.pallas_call(
        body, name="adam_small", out_shape=tuple(shapes),
        scratch_shapes=[pltpu.VMEM((SMALL_ROWS, D_MODEL), F32)],
        compiler_params=_params(),
    )(partials, d_sgu_w_all, *[weights[n] for n in names], *[moments_m[n] for n in names],
      *[moments_v[n] for n in names])
    return outs[0], outs[1], {name: outs[2 + 4 * i:6 + 4 * i] for i, name in enumerate(names)}


def kernel(x, c, norm_g, w_ada, b_ada, w_in, attn_sinks, sgu_ln_g, sgu_ln_b, sgu_w, sgu_b, w_out, final_g, loss_target, m_norm_g, m_w_ada, m_b_ada, m_w_in, m_attn_sinks, m_sgu_ln_g, m_sgu_ln_b, m_sgu_w, m_sgu_b, m_w_out, m_final_g, v_norm_g, v_w_ada, v_b_ada, v_w_in, v_attn_sinks, v_sgu_ln_g, v_sgu_ln_b, v_sgu_w, v_sgu_b, v_w_out, v_final_g):
    xi, yi, ci = _place()
    me = 4 * xi + 2 * yi + ci
    x2d, target = x[0], loss_target[0]
    t = x2d.shape[0]

    core = ci.astype(jnp.int32).reshape(1)
    chip = (2 * xi + yi).astype(jnp.int32).reshape(1)

    first = _own_block_copies(_first_targets)
    first_flight = _start_copies([_with_own_slot(w_in[0].T.astype(BF16), me)], first, 2, core, "gather_w_in_start")

    c_all = _all_gather_small(c.reshape(8, 256) + first_flight[3][0, 0], "gather_c").reshape(N_DEV, D_MODEL)
    device = me.astype(jnp.int32).reshape(1)
    c_act, mod_part = _modulation(device, c_all, w_ada[0], b_ada)
    mod_all = _all_gather_small(mod_part, "gather_mod")

    across = _wait_then_start(first_flight, lambda *a: first(*a)[1:], _second_axis_stage_copies, 3, mod_all,
                              "gather_w_in_second_axis_stage")
    mod = lax.dynamic_index_in_dim(mod_all, me, axis=1, keepdims=False).reshape(1, 3 * D_MODEL)
    mod = mod + across[3][0, 0]
    h = _modulated_norm(x2d, norm_g, mod)

    w_in_pair = _wait_copies((first_flight[0], first_flight[1], across[2], None), lambda *a: first(*a)[:1], h,
                             "gather_w_in_sibling_wait")
    z_own = _z_proj(h, w_in_pair[0].reshape(D_IN, D_MODEL), chip, 0, 1, None, "z_proj_own")
    w_out_early = _own_block_copies(lambda x, y, c: [(x, y, 1 - c), (*_second_axis_chip(x, y, c), c)])
    w_out_late = _own_block_copies(lambda x, y, c: [(*_first_axis_chip(x, y, c), c), (1 - x, 1 - y, c)])
    forward = _wait_then_start(
        (across[0], across[1], w_in_pair, None), lambda *a: _second_axis_stage_copies(*a)[:1],
        lambda refs, s, r: _second_axis_forward_copies(refs[:1], s, r) + _group(w_out_early, 1, 1, 1)(refs, s, r),
        3, z_own, "gather_w_in_second_axis_forward", more_bufs=[_with_own_slot(w_out[0].astype(BF16), me)])
    w_in_most = _wait_copies((across[0], across[1], forward[2][:1], None),
                             lambda *a: _second_axis_stage_copies(*a)[1:2], z_own, "gather_w_in_first_forward_wait")
    w_in_most = _wait_copies((forward[0], forward[1], w_in_most, None), _second_axis_forward_copies, z_own,
                             "gather_w_in_second_forward_wait")
    z_early = _z_proj(h, w_in_most[0].reshape(D_IN, D_MODEL), chip, 1, _Z_EARLY_TILES - 1, z_own, "z_proj_early")
    last = _wait_then_start(
        (across[0], across[1], [w_in_most[0], forward[2][1]], None), lambda *a: _second_axis_stage_copies(*a)[2:],
        lambda refs, s, r: _diagonal_forward_copies(refs[:1], s, r) + _group(w_out_late, 1, 1, 1)(refs, s, r),
        3, z_early, "gather_w_in_last_stage")
    w_in_all = _wait_copies((last[0], last[1], last[2][:1], None), _diagonal_forward_copies, z_early,
                            "gather_w_in_last_wait")[0]
    w_in_t = w_in_all.reshape(D_IN, D_MODEL)
    z = _z_proj(h, w_in_t, chip, _Z_EARLY_TILES, 7 - _Z_EARLY_TILES, z_early, "z_proj_late")
    w_out_half = _wait_copies((forward[0], forward[1], last[2][1:], None), _group(w_out_early, 0, 1, 1), z,
                              "gather_w_out_early_wait")
    w_out_flight = _wait_then_start((last[0], last[1], w_out_half, None), _group(w_out_late, 0, 1, 1),
                                    _forward_copies, 3, z, "gather_w_out_forward_stage")
    sink_rows = jnp.repeat(attn_sinks.reshape(N_Q_HEADS), BLOCK).reshape(2, 1, 8 * BLOCK)
    sgu_bt = sgu_b[0].T
    a, probs, sink_probs = _mixer_fwd(z, sink_rows + w_out_flight[3][0, 0], sgu_ln_g, sgu_ln_b, sgu_w[0], sgu_bt)
    w_out_all = _wait_copies(w_out_flight, _forward_copies, a, "gather_w_out_forward_wait")[0]
    w_out_full = w_out_all.reshape(D_MODEL, D_MODEL)
    final_g_row = final_g.reshape(1, D_MODEL)
    dx2, dy, loss_part, d_final_g, d_gate = _out_proj_head(a, w_out_full, x2d, target, mod, final_g_row)

    da = _matmul(dy, w_out_full, "nt", F32, min(t, 1024), 1024, "out_proj_bwd")
    dz, d_sinks, d_sgu_w, d_sgu_b, d_ln_g, d_ln_b = _mixer_bwd(
        z, da, probs, sink_probs, sgu_ln_g, sgu_ln_b, sgu_w[0], jnp.swapaxes(sgu_w[0], 1, 2), sgu_bt)
    dw_in_t = _matmul(dz, h, "tn", BF16, 768, D_MODEL, "w_in_grad").reshape(4, 2, W_IN_SHARD, D_MODEL)
    sgu_w_to_all = _group(_own_block_copies(_all_others), 2, 1, 4)
    both = _start_copies(
        [dw_in_t, lax.empty((4, W_IN_SHARD, D_MODEL), BF16), _with_own_slot(d_sgu_w, me)],
        lambda refs, s, r: _sibling_copies(refs[:2], s, r) + sgu_w_to_all(refs, s, r), 4 + N_DEV - 1, core,
        "w_in_grad_sibling_and_sgu_w_gather_start")
    sibling_flight, sgu_w_flight = (both[0], both[1], both[2][:2], None), (both[0], both[1], both[2][2:], None)
    dw_out = _matmul(a, dy, "tn", BF16, 1024, 1024, "w_out_grad", dep=both[3]).reshape(4, 2, W_OUT_SHARD, D_MODEL)
    pair_out = _pair_reduce(dw_out, "w_out_grad_pair_reduce", W_OUT_SHARD // 2)
    dw_in_t, land_pair = _wait_copies(sibling_flight, _sibling_copies, pair_out, "w_in_grad_sibling_wait")
    pair_in = _pair_sum(core, dw_in_t, land_pair, W_IN_SHARD // 2)
    hops = _start_copies(
        [pair_in, lax.empty((2, W_IN_SHARD, D_MODEL), BF16), pair_out, lax.empty((3, W_OUT_SHARD, D_MODEL), BF16)],
        lambda refs, s, r: _first_hop_copies(refs[:2], s, r) + _group(_chip_copies, 2, 2, 2)(refs, s, r), 5, core,
        "w_in_grad_first_hop_and_w_out_grad_chip_start")
    hop1, out_flight = (hops[0], hops[1], hops[2][:2], hops[3]), (hops[0], hops[1], hops[2][2:], None)
    grad_x, d_shift, d_scale, d_norm_g = _z_proj_bwd_norm(dz, w_in_t, x2d, dx2, norm_g, mod, hop1[3])

    partial = _pack_small(d_shift, d_scale, d_gate, d_norm_g, d_final_g, d_ln_g, d_ln_b, loss_part, d_sinks, d_sgu_b)
    small_flight = _start_copies([_with_own_slot(partial, me)], _own_block_copies(_all_others), N_DEV - 1, core,
                                 "small_grad_gather_start")
    pair_in, land_first = _wait_copies(hop1, _first_hop_copies, small_flight[3], "w_in_grad_first_hop_wait")
    second_chip = (2 * ((xi + ci) % 2) + (yi + 1 - ci) % 2).astype(jnp.int32).reshape(1)
    relay = _relay_sum(second_chip, pair_in, land_first, W_IN_SHARD // 3)
    hop2 = _start_copies([relay, lax.empty((1, W_IN_SHARD, D_MODEL), BF16)], _second_hop_copies, 1, core,
                         "w_in_grad_second_hop_start")
    pair_out, land_out = _wait_copies(out_flight, _group(_chip_copies, 0, 2, 2), hop2[3], "w_out_grad_chip_wait")
    big = {"w_out": _adam_from_chips(chip, pair_out, [(land_out, k) for k in range(3)], w_out[0], m_w_out[0],
                                     v_w_out[0], "adam_w_out", 1024)}
    partial_all = _wait_copies(small_flight, _own_block_copies(_all_others), big["w_out"][0],
                               "small_grad_gather_wait")[0]
    d_sgu_w_all = _wait_copies(sgu_w_flight, _group(_own_block_copies(_all_others), 0, 1, 4), partial_all,
                               "sgu_w_grad_gather_wait")[0]
    weights = {"norm_g": norm_g, "b_ada": b_ada, "attn_sinks": attn_sinks, "sgu_ln_g": sgu_ln_g,
               "sgu_ln_b": sgu_ln_b, "sgu_w": sgu_w, "sgu_b": sgu_b, "final_g": final_g_row}
    moments_m = {"norm_g": m_norm_g, "b_ada": m_b_ada, "attn_sinks": m_attn_sinks, "sgu_ln_g": m_sgu_ln_g,
                 "sgu_ln_b": m_sgu_ln_b, "sgu_w": m_sgu_w, "sgu_b": m_sgu_b,
                 "final_g": m_final_g.reshape(1, D_MODEL)}
    moments_v = {"norm_g": v_norm_g, "b_ada": v_b_ada, "attn_sinks": v_attn_sinks, "sgu_ln_g": v_sgu_ln_g,
                 "sgu_ln_b": v_sgu_ln_b, "sgu_w": v_sgu_w, "sgu_b": v_sgu_b,
                 "final_g": v_final_g.reshape(1, D_MODEL)}
    loss, dmod_all, small = _adam_small(partial_all, d_sgu_w_all, weights, moments_m, moments_v)
    small["final_g"] = tuple(o.reshape(D_MODEL) for o in small["final_g"])

    big["w_ada"] = _adam_w_ada(device, c_act.T, dmod_all, w_ada[0], m_w_ada[0], v_w_ada[0])
    _, land_second = _wait_copies(hop2, _second_hop_copies, big["w_ada"][0], "w_in_grad_second_hop_wait")
    big["w_in"] = tuple(o.T for o in _adam_from_chips(
        chip, pair_in, [(land_first, 0), (land_second, 0)], w_in[0].T, m_w_in[0].T, v_w_in[0].T, "adam_w_in", 512))
    order = ["norm_g", "w_ada", "b_ada", "w_in", "attn_sinks", "sgu_ln_g", "sgu_ln_b", "sgu_w", "sgu_b", "w_out",
             "final_g"]
    outs = [loss.reshape(()), grad_x[None]]
    for k in range(4):
        for name in order:
            outs.append(big[name][k][None] if name in big else small[name][k])
    return tuple(outs)
```

```python
import jax
import jax.numpy as jnp
from jax import lax
from jax.experimental import pallas as pl
from jax.experimental.pallas import tpu as pltpu

F32 = jnp.float32
BF16 = jnp.bfloat16
MESH = pl.DeviceIdType.MESH

N_DEV = 8
D_MODEL = 2048
HEAD_DIM = 64
D_ATTN = 1024
N_Q_HEADS = 16
D_KV = 128
BLOCK = 128
D_SGU = 1024
SGU_GROUPS = 8
D_IN = 5376
W_IN_SHARD = D_IN // N_DEV
W_OUT_SHARD = D_MODEL // N_DEV
W_ADA_SHARD = 3 * D_MODEL // N_DEV
EPS = 1e-6
ATTN_SCALE = 0.125

ADAM_LR = 0.001
ADAM_B1 = 0.9
ADAM_B2 = 0.999
ADAM_EPS = 1e-08
ADAM_WD = 0.01
ADAM_STEP = 10

SEG_Q, SEG_KV, SEG_GA, SEG_U, SEG_VS, SEG_GS = 0, 1024, 1280, 2304, 3328, 4352

VMEM_LIMIT = 56 * 1024 * 1024

ROW_SHIFT, ROW_SCALE, ROW_GATE, ROW_NORM_G, ROW_FINAL_G, ROW_LN, ROW_MISC, ROW_SGU_B = 0, 1, 2, 3, 4, 5, 6, 8
SMALL_ROWS = 16


def _params(**kw):
    return pltpu.CompilerParams(vmem_limit_bytes=VMEM_LIMIT, **kw)


def _sigmoid(x):
    return 0.5 * (jnp.tanh(0.5 * x) + 1.0)


def _place():
    return lax.axis_index("x"), lax.axis_index("y"), lax.axis_index("c")


def _every_chip(x, y, c):
    return [0, 1, 2, 3]


def _first_hop_chips(x, y, c):
    first = _first_axis_chip(x, y, c)
    return [2 * first[0] + first[1], 2 * (1 - x) + (1 - y)]


def _pair_reduce(blocks, chips, name, row_chunk):
    _, _, r, cols = blocks.shape
    n = len(chips(0, 0, 0))
    assert r % row_chunk == 0

    def body(in_ref, out_ref, land, own, summed, send_sems, recv_sems, own_sems, out_sems):
        x, y, c = _place()
        sends, loads, stores = [], [], []
        for m in range(n):
            cp = pltpu.make_async_remote_copy(
                src_ref=in_ref.at[chips(x, y, 1 - c)[m], 1 - c], dst_ref=land.at[m], send_sem=send_sems.at[m],
                recv_sem=recv_sems.at[m], device_id=(x, y, 1 - c), device_id_type=MESH)
            cp.start()
            sends.append(cp)
            ld = pltpu.make_async_copy(in_ref.at[chips(x, y, c)[m], c], own.at[m], own_sems.at[m])
            ld.start()
            loads.append(ld)
        for m in range(n):
            sends[m].wait_recv()
            loads[m].wait()
            for k in range(r // row_chunk):
                rows = slice(k * row_chunk, (k + 1) * row_chunk)
                summed[m, rows, :] = (own[m, rows, :].astype(F32) + land[m, rows, :].astype(F32)).astype(BF16)
            st = pltpu.make_async_copy(summed.at[m], out_ref.at[m], out_sems.at[m])
            st.start()
            stores.append(st)
        for m in range(n):
            sends[m].wait_send()
            stores[m].wait()

    spec = pl.BlockSpec(memory_space=pl.ANY)
    return pl.pallas_call(
        body, name=name, out_shape=jax.ShapeDtypeStruct((n, r, cols), BF16),
        in_specs=[spec], out_specs=spec,
        scratch_shapes=[pltpu.VMEM((n, r, cols), BF16), pltpu.VMEM((n, r, cols), BF16), pltpu.VMEM((n, r, cols), BF16),
                        pltpu.SemaphoreType.DMA((n,)), pltpu.SemaphoreType.DMA((n,)), pltpu.SemaphoreType.DMA((n,)),
                        pltpu.SemaphoreType.DMA((n,))],
        compiler_params=_params(),
    )(blocks)


_HBM = pl.BlockSpec(memory_space=pltpu.HBM)
_SEM = pl.BlockSpec(memory_space=pltpu.SEMAPHORE)
_EFFECT = pltpu.SideEffectType.DATAFLOW_SIDE_EFFECTING


def _start_copies(bufs, copies, n_copies, after, name):
    nb = len(bufs)

    def body(*refs):
        for cp in copies(refs[:nb], refs[nb + 1], refs[nb + 2]):
            cp.start()
        refs[-1][...] = jnp.zeros_like(refs[-1])

    out = pl.pallas_call(
        body, name=name,
        out_shape=(pltpu.SemaphoreType.DMA((n_copies,)), pltpu.SemaphoreType.DMA((n_copies,)),
                   *[pltpu.HBM(b.shape, b.dtype) for b in bufs], jax.ShapeDtypeStruct((8, 128), F32)),
        in_specs=(_HBM,) * nb + (pl.BlockSpec(memory_space=pl.ANY),),
        out_specs=(_SEM, _SEM) + (_HBM,) * nb + (pl.BlockSpec(memory_space=pltpu.VMEM),),
        input_output_aliases={i: 2 + i for i in range(nb)},
        compiler_params=pltpu.CompilerParams(has_side_effects=_EFFECT),
    )(*[pltpu.with_memory_space_constraint(b, pltpu.HBM) for b in bufs], after)
    return out[0], out[1], list(out[2:2 + nb]), out[-1]


def _wait_copies(flight, copies, after, name):
    send_sems, recv_sems, bufs, _ = flight
    nb = len(bufs)

    def body(*refs):
        for cp in copies(refs[:nb], refs[nb], refs[nb + 1]):
            cp.wait_send()
            cp.wait_recv()

    return pl.pallas_call(
        body, name=name,
        out_shape=tuple(pltpu.HBM(b.shape, b.dtype) for b in bufs),
        in_specs=(_HBM,) * nb + (_SEM, _SEM, pl.BlockSpec(memory_space=pl.ANY)), out_specs=(_HBM,) * nb,
        input_output_aliases={i: i for i in range(nb)},
        compiler_params=pltpu.CompilerParams(has_side_effects=_EFFECT),
    )(*bufs, send_sems, recv_sems, after)


class _From:
    def __init__(self, sems, offset):
        self.sems, self.offset = sems, offset

    @property
    def at(self):
        return self

    def __getitem__(self, k):
        return self.sems.at[k + self.offset]


def _group(copies, first_buf, n_bufs, offset):
    def grouped(refs, send_sems, recv_sems):
        return copies(refs[first_buf:first_buf + n_bufs], _From(send_sems, offset), _From(recv_sems, offset))
    return grouped


def _wait_then_start(flight, waited, started, n_started, after, name, more_bufs=()):
    old_send, old_recv, bufs, _ = flight
    bufs = list(bufs) + [pltpu.with_memory_space_constraint(b, pltpu.HBM) for b in more_bufs]
    nb = len(bufs)

    def body(*refs):
        for cp in waited(refs[:nb], refs[nb], refs[nb + 1]):
            cp.wait_send()
            cp.wait_recv()
        for cp in started(refs[:nb], refs[nb + 3], refs[nb + 4]):
            cp.start()
        refs[-1][...] = jnp.zeros_like(refs[-1])

    out = pl.pallas_call(
        body, name=name,
        out_shape=(pltpu.SemaphoreType.DMA((n_started,)), pltpu.SemaphoreType.DMA((n_started,)),
                   *[pltpu.HBM(b.shape, b.dtype) for b in bufs], jax.ShapeDtypeStruct((8, 128), F32)),
        in_specs=(_HBM,) * nb + (_SEM, _SEM, pl.BlockSpec(memory_space=pl.ANY)),
        out_specs=(_SEM, _SEM) + (_HBM,) * nb + (pl.BlockSpec(memory_space=pltpu.VMEM),),
        input_output_aliases={i: 2 + i for i in range(nb)},
        compiler_params=pltpu.CompilerParams(has_side_effects=_EFFECT),
    )(*bufs, old_send, old_recv, after)
    return out[0], out[1], list(out[2:2 + nb]), out[-1]


def _late_pair_copies(refs, send_sems, recv_sems):
    blocks_ref, land_ref = refs
    x, y, c = _place()
    first = _first_axis_chip(x, y, c)
    devices = [4 * x + 2 * y + 1 - c, 4 * first[0] + 2 * first[1] + 1 - c]
    return [pltpu.make_async_remote_copy(
        src_ref=blocks_ref.at[devices[k]], dst_ref=land_ref.at[k], send_sem=send_sems.at[k], recv_sem=recv_sems.at[k],
        device_id=(x, y, 1 - c), device_id_type=MESH) for k in range(2)]


def _chip_copies(refs, send_sems, recv_sems):
    pair_ref, land_ref = refs
    x, y, c = _place()
    chips = [(1 - x, y), (x, 1 - y), (1 - x, 1 - y)]
    return [pltpu.make_async_remote_copy(
        src_ref=pair_ref.at[2 * chip[0] + chip[1]], dst_ref=land_ref.at[k],
        send_sem=send_sems.at[k], recv_sem=recv_sems.at[k],
        device_id=(*chip, c), device_id_type=MESH) for k, chip in enumerate(chips)]


def _first_hop_copies(refs, send_sems, recv_sems):
    pair_ref, land_ref = refs
    x, y, c = _place()
    return [pltpu.make_async_remote_copy(
        src_ref=pair_ref.at[k], dst_ref=land_ref.at[k], send_sem=send_sems.at[k], recv_sem=recv_sems.at[k],
        device_id=(*_first_axis_chip(x, y, c), c), device_id_type=MESH) for k in range(2)]


def _second_hop_copies(refs, send_sems, recv_sems):
    relay_ref, land_ref = refs
    x, y, c = _place()
    second = ((x + c) % 2, (y + 1 - c) % 2)
    return [pltpu.make_async_remote_copy(
        src_ref=relay_ref, dst_ref=land_ref.at[0], send_sem=send_sems.at[0], recv_sem=recv_sems.at[0],
        device_id=(*second, c), device_id_type=MESH)]


def _own_block_copies(targets):
    def copies(refs, send_sems, recv_sems):
        x, y, c = _place()
        mine = refs[0].at[4 * x + 2 * y + c]
        return [pltpu.make_async_remote_copy(
            src_ref=mine, dst_ref=mine, send_sem=send_sems.at[k], recv_sem=recv_sems.at[k],
            device_id=to, device_id_type=MESH) for k, to in enumerate(targets(x, y, c))]
    return copies


def _all_others(x, y, c):
    flip = lambda v, f: 1 - v if f else v
    return [(flip(x, r & 4), flip(y, r & 2), flip(c, r & 1)) for r in range(1, N_DEV)]


def _forward_copies(refs, send_sems, recv_sems):
    x, y, c = _place()
    chips = [(1 - x, y), (x, 1 - y), (1 - x, 1 - y)]
    return [pltpu.make_async_remote_copy(
        src_ref=refs[0].at[4 * chip[0] + 2 * chip[1] + c], dst_ref=refs[0].at[4 * chip[0] + 2 * chip[1] + c],
        send_sem=send_sems.at[k], recv_sem=recv_sems.at[k],
        device_id=(x, y, 1 - c), device_id_type=MESH) for k, chip in enumerate(chips)]


def _first_axis_chip(x, y, c):
    return (x + 1 - c) % 2, (y + c) % 2


def _second_axis_chip(x, y, c):
    return (x + c) % 2, (y + 1 - c) % 2


def _first_targets(x, y, c):
    return [(x, y, 1 - c), (*_first_axis_chip(x, y, c), c)]


def _all_gather_small(shard, name):
    def body(in_ref, out_ref, send_sems, recv_sems, local_sem):
        x, y, c = _place()
        me, sibling = 4 * x + 2 * y + c, (x, y, 1 - c)
        first, second = _first_axis_chip(x, y, c), _second_axis_chip(x, y, c)

        def pair(chip):
            return out_ref.at[pl.ds(2 * (2 * chip[0] + chip[1]), 2)]

        def exchange(k, src, dst, to):
            cp = pltpu.make_async_remote_copy(src_ref=src, dst_ref=dst, send_sem=send_sems.at[k],
                                              recv_sem=recv_sems.at[k], device_id=to, device_id_type=MESH)
            cp.start()
            cp.wait()

        own = pltpu.make_async_copy(in_ref, out_ref.at[me], local_sem)
        own.start()
        exchange(0, in_ref, out_ref.at[me], sibling)
        own.wait()
        exchange(1, pair((x, y)), pair((x, y)), (*second, c))
        exchange(2, pair(second), pair(second), sibling)
        exchange(3, pair(first), pair(first), (*second, c))

    spec = pl.BlockSpec(memory_space=pltpu.VMEM)
    return pl.pallas_call(
        body, name=name, out_shape=jax.ShapeDtypeStruct((N_DEV,) + shard.shape, shard.dtype),
        in_specs=[spec], out_specs=spec,
        scratch_shapes=[pltpu.SemaphoreType.DMA((4,)), pltpu.SemaphoreType.DMA((4,)), pltpu.SemaphoreType.DMA],
        compiler_params=_params(),
    )(shard)


def _slot_copies(refs, send_sems, recv_sems, plan):
    copies = []
    for k, ((px, py, pc), to) in enumerate(plan):
        blk = refs[0].at[4 * px + 2 * py + pc]
        copies.append(pltpu.make_async_remote_copy(
            src_ref=blk, dst_ref=blk, send_sem=send_sems.at[k], recv_sem=recv_sems.at[k],
            device_id=to, device_id_type=MESH))
    return copies


def _second_axis_stage_copies(refs, send_sems, recv_sems):
    x, y, c = _place()
    first, second = (*_first_axis_chip(x, y, c), c), (*_second_axis_chip(x, y, c), c)
    return _slot_copies(refs, send_sems, recv_sems, [((x, y, c), second), (first, (x, y, 1 - c)), (first, second)])


def _second_axis_forward_copies(refs, send_sems, recv_sems):
    x, y, c = _place()
    return _slot_copies(refs, send_sems, recv_sems, [((*_second_axis_chip(x, y, c), c), (x, y, 1 - c))])


def _diagonal_forward_copies(refs, send_sems, recv_sems):
    x, y, c = _place()
    blk = refs[0].at[4 * (1 - x) + 2 * (1 - y) + c]
    return [pltpu.make_async_remote_copy(
        src_ref=blk, dst_ref=blk, send_sem=send_sems.at[0], recv_sem=recv_sems.at[0],
        device_id=(x, y, 1 - c), device_id_type=MESH)]


def _with_own_slot(block, me):
    return lax.dynamic_update_index_in_dim(lax.empty((N_DEV,) + block.shape, block.dtype), block, me, 0)


def _matmul(a, b, dims, out_dtype, tm, tn, name, dep=None):
    if dims == "nn":
        (m, k), n = a.shape, b.shape[1]
        a_spec = pl.BlockSpec((tm, k), lambda i, j: (i, 0))
        b_spec = pl.BlockSpec((k, tn), lambda i, j: (0, j))
        contract = ((1,), (0,))
    elif dims == "nt":
        (m, k), n = a.shape, b.shape[0]
        a_spec = pl.BlockSpec((tm, k), lambda i, j: (i, 0))
        b_spec = pl.BlockSpec((tn, k), lambda i, j: (j, 0))
        contract = ((1,), (1,))
    else:
        (k, m), n = a.shape, b.shape[1]
        a_spec = pl.BlockSpec((k, tm), lambda i, j: (0, i))
        b_spec = pl.BlockSpec((k, tn), lambda i, j: (0, j))
        contract = ((0,), (0,))
    assert m % tm == 0 and n % tn == 0 and a.dtype == BF16 and b.dtype == BF16

    def body(a_ref, b_ref, *rest):
        rest[-1][...] = lax.dot_general(a_ref[...], b_ref[...], (contract, ((), ())),
                                        preferred_element_type=F32).astype(out_dtype)

    deps = [] if dep is None else [dep]
    return pl.pallas_call(
        body, name=name, grid=(m // tm, n // tn),
        in_specs=[a_spec, b_spec] + [pl.BlockSpec((8, 128), lambda i, j: (0, 0))] * len(deps),
        out_specs=pl.BlockSpec((tm, tn), lambda i, j: (i, j)),
        out_shape=jax.ShapeDtypeStruct((m, n), out_dtype),
        compiler_params=_params(dimension_semantics=("arbitrary", "arbitrary")),
    )(a, b, *deps)


Z_TILE = 768
_Z_TILE_ORDER = ((0, 1, 2, 3, 4, 5, 6), (2, 0, 1, 6, 3, 4, 5), (4, 0, 5, 6, 1, 2, 3), (6, 2, 3, 4, 0, 1, 5))
_Z_EARLY_TILES = 4


def _z_proj(h, w_in_t, chip, first, count, z_prev, name, dep=None):
    t = h.shape[0]

    def body(chip_ref, h_ref, w_ref, *rest):
        rest[-1][...] = _dot_nt(h_ref[...], w_ref[...])

    def tile(j, chip_ref):
        picked = 0
        for c, order in enumerate(_Z_TILE_ORDER):
            for k in range(count):
                picked = picked + jnp.where((chip_ref[0] == c) & (j == k), order[first + k], 0)
        return picked

    prev = [] if z_prev is None else [z_prev]
    deps = [] if dep is None else [dep]
    return pl.pallas_call(
        body, name=name,
        grid_spec=pltpu.PrefetchScalarGridSpec(
            num_scalar_prefetch=1, grid=(count,),
            in_specs=[pl.BlockSpec((t, D_MODEL), lambda j, o: (0, 0)),
                      pl.BlockSpec((Z_TILE, D_MODEL), lambda j, o: (tile(j, o), 0))]
            + [pl.BlockSpec(memory_space=pl.ANY)] * len(prev)
            + [pl.BlockSpec((8, 128), lambda j, o: (0, 0))] * len(deps),
            out_specs=pl.BlockSpec((t, Z_TILE), lambda j, o: (0, tile(j, o)))),
        out_shape=jax.ShapeDtypeStruct((t, D_IN), F32),
        input_output_aliases={3: 0} if prev else {},
        compiler_params=_params(dimension_semantics=("arbitrary",)),
    )(chip, h, w_in_t, *prev, *deps)


def _modulation(device, c_all, w_ada, b_ada):
    def body(device_ref, c_ref, w_ref, b_ref, act_ref, mod_ref):
        cv = c_ref[...]
        act = cv * _sigmoid(cv)
        act_ref[...] = act
        mod_ref[...] = jnp.dot(act.astype(BF16), w_ref[...].astype(BF16), preferred_element_type=F32) + b_ref[...]

    whole = lambda a: pl.BlockSpec(a.shape, lambda i, device_ref: (0,) * a.ndim)
    return pl.pallas_call(
        body, name="modulation",
        grid_spec=pltpu.PrefetchScalarGridSpec(
            num_scalar_prefetch=1, grid=(1,),
            in_specs=[whole(c_all), whole(w_ada), pl.BlockSpec((1, W_ADA_SHARD), lambda i, device_ref: (0, device_ref[0]))],
            out_specs=(whole(c_all), pl.BlockSpec((N_DEV, W_ADA_SHARD), lambda i, device_ref: (0, 0)))),
        out_shape=(jax.ShapeDtypeStruct(c_all.shape, F32), jax.ShapeDtypeStruct((N_DEV, W_ADA_SHARD), F32)),
        compiler_params=_params(dimension_semantics=("arbitrary",)),
    )(device, c_all, w_ada, b_ada)


MOD_SHIFT, MOD_SCALE, MOD_GATE = 0, 1, 2


def _mod_spec(part, d):
    return pl.BlockSpec((1, d), lambda i: (0, part))


def _modulated_norm(x, norm_g, mod, tm=512):
    t, d = x.shape

    def body(x_ref, g_ref, sc_ref, sh_ref, h_ref):
        xv = x_ref[...]
        r = lax.rsqrt(jnp.mean(xv * xv, axis=-1, keepdims=True) + EPS)
        h = (xv * r) * g_ref[...] * (1.0 + sc_ref[...]) + sh_ref[...]
        h_ref[...] = h.astype(BF16)

    row = pl.BlockSpec((1, d), lambda i: (0, 0))
    return pl.pallas_call(
        body, name="modulated_norm", grid=(t // tm,),
        in_specs=[pl.BlockSpec((tm, d), lambda i: (i, 0)), row, _mod_spec(MOD_SCALE, d), _mod_spec(MOD_SHIFT, d)],
        out_specs=pl.BlockSpec((tm, d), lambda i: (i, 0)),
        out_shape=jax.ShapeDtypeStruct((t, d), BF16),
        compiler_params=_params(dimension_semantics=("arbitrary",)),
    )(x, norm_g, mod, mod)


def _window_bias(block_index):
    s = lax.broadcasted_iota(jnp.int32, (2 * BLOCK, BLOCK), 0)
    t = lax.broadcasted_iota(jnp.int32, (2 * BLOCK, BLOCK), 1)
    valid = ((s < BLOCK) & (s > t) & (block_index > 0)) | ((s >= BLOCK) & ((s - BLOCK) <= t))
    bias = jnp.where(valid, 0.0, -jnp.inf).astype(F32)
    return jnp.concatenate([bias] * 8, axis=1)


def _heads_t(pair_blocks, g):
    top = lax.broadcasted_iota(jnp.int32, (BLOCK, BLOCK), 0) < HEAD_DIM
    zeros = jnp.zeros((HEAD_DIM, BLOCK), F32)
    tiles = []
    for blk in pair_blocks:
        tp = blk.T
        if g == 0:
            tiles += [jnp.where(top, tp, 0.0), jnp.concatenate([tp[HEAD_DIM:], zeros], axis=0)]
        else:
            tiles += [jnp.concatenate([zeros, tp[:HEAD_DIM]], axis=0), jnp.where(top, 0.0, tp)]
    return jnp.concatenate(tiles, axis=1)


def _pair_block(xt, p, g):
    r0 = HEAD_DIM * g
    even = xt[r0:r0 + HEAD_DIM, (2 * p) * BLOCK:(2 * p + 1) * BLOCK]
    odd = xt[r0:r0 + HEAD_DIM, (2 * p + 1) * BLOCK:(2 * p + 2) * BLOCK]
    return jnp.concatenate([even, odd], axis=0).T


def _softmax_t(scores_t, bias, sink):
    st = scores_t + bias
    m = jnp.maximum(jnp.max(st, axis=0, keepdims=True), sink)
    e = jnp.exp(st - m)
    es = jnp.exp(sink - m)
    inv = 1.0 / (jnp.sum(e, axis=0, keepdims=True) + es)
    return e * inv, es * inv


def _dot(a, b):
    return jnp.dot(a, b, preferred_element_type=F32)


def _dot_nt(a, b):
    return lax.dot_general(a, b, (((1,), (1,)), ((), ())), preferred_element_type=F32)


def _layer_norm_fwd(v):
    mu = jnp.mean(v, axis=-1, keepdims=True)
    xc = v - mu
    rstd = lax.rsqrt(jnp.mean(xc * xc, axis=-1, keepdims=True) + EPS)
    return xc * rstd, rstd


def _tril(transposed=False):
    t = lax.broadcasted_iota(jnp.int32, (BLOCK, BLOCK), 0)
    s = lax.broadcasted_iota(jnp.int32, (BLOCK, BLOCK), 1)
    return s >= t if transposed else t >= s


def _const_spec(shape):
    return pl.BlockSpec(shape, lambda i: (0,) * len(shape))


def _keys_values(z_ref, kvp):
    kvc = z_ref[:, SEG_KV:SEG_KV + 2 * D_KV]
    kk = jnp.concatenate([kvp[:, :D_KV], kvc[:, :D_KV]], axis=0)
    vv = jnp.concatenate([kvp[:, D_KV:], kvc[:, D_KV:]], axis=0)
    return kk, vv


MIXER_BLOCKS = 2


class _Rows:
    def __init__(self, ref, sub):
        self.ref, self.rows = ref, slice(sub * BLOCK, (sub + 1) * BLOCK)

    def __getitem__(self, idx):
        return self.ref[self.rows, idx[1]]

    def __setitem__(self, idx, value):
        self.ref[self.rows, idx[1]] = value


def _kv_before_spec(index):
    return pl.BlockSpec((BLOCK, 2 * D_KV),
                        lambda i: (jnp.maximum(MIXER_BLOCKS * index(i) - 1, 0), SEG_KV // (2 * D_KV)))


def _pair_cols(g, p, base=0):
    return slice(base + (4 * g + p) * 128, base + (4 * g + p + 1) * 128)


def _mixer_fwd(z, sink_rows, ln_g, ln_b, sgu_w, sgu_bt):
    t = z.shape[0]

    def body(z_all, kvp_ref, sink_ref, lng_ref, lnb_ref, w_ref, bt_ref, a_all, prob_ref, sink_prob_ref):
        kv_before = kvp_ref[...]
        for sub in range(MIXER_BLOCKS):
            z_ref, a_ref = _Rows(z_all, sub), _Rows(a_all, sub)
            one_block(z_ref, kv_before, MIXER_BLOCKS * pl.program_id(0) + sub, sink_ref, lng_ref, lnb_ref, w_ref,
                      bt_ref, a_ref, prob_ref.at[sub], sink_prob_ref.at[sub])
            kv_before = z_ref[:, SEG_KV:SEG_KV + 2 * D_KV]

    def one_block(z_ref, kv_before, block_index, sink_ref, lng_ref, lnb_ref, w_ref, bt_ref, a_ref, prob_ref,
                  sink_prob_ref):
        bias = _window_bias(block_index)
        kk, vv = _keys_values(z_ref, kv_before)
        kk_b, vvt_b = kk.astype(BF16), vv.T.astype(BF16)
        for g in range(2):
            qt = _heads_t([z_ref[:, _pair_cols(g, p, SEG_Q)] * ATTN_SCALE for p in range(4)], g).astype(BF16)
            prob, sink_prob = _softmax_t(_dot(kk_b, qt), bias, sink_ref[g])
            prob_b = prob.astype(BF16)
            prob_ref[g] = prob_b
            sink_prob_ref[g] = sink_prob
            ot = _dot(vvt_b, prob_b)
            for p in range(4):
                gate = z_ref[:, _pair_cols(g, p, SEG_GA)]
                a_ref[:, _pair_cols(g, p)] = (_pair_block(ot, p, g) * (gate * _sigmoid(gate))).astype(BF16)

        vhat, _ = _layer_norm_fwd(z_ref[:, SEG_VS:SEG_VS + D_SGU])
        vn = vhat * lng_ref[...] + lnb_ref[...]
        tril = _tril()
        for g in range(SGU_GROUPS):
            cols = slice(g * 128, (g + 1) * 128)
            wm = jnp.where(tril, w_ref[g], 0.0).astype(BF16)
            mixed = _dot(wm, vn[:, cols].astype(BF16)) + bt_ref[:, g:g + 1]
            gate = z_ref[:, SEG_GS + g * 128:SEG_GS + (g + 1) * 128]
            a_ref[:, D_ATTN + g * 128:D_ATTN + (g + 1) * 128] = (
                (z_ref[:, SEG_U + g * 128:SEG_U + (g + 1) * 128] * mixed) * (gate * _sigmoid(gate))).astype(BF16)

    rows = MIXER_BLOCKS * BLOCK
    return pl.pallas_call(
        body, name="mixer_fwd", grid=(t // rows,),
        in_specs=[pl.BlockSpec((rows, D_IN), lambda i: (i, 0)), _kv_before_spec(lambda i: i),
                  _const_spec((2, 1, 8 * BLOCK)), _const_spec((1, D_SGU)), _const_spec((1, D_SGU)),
                  _const_spec((SGU_GROUPS, BLOCK, BLOCK)), _const_spec((BLOCK, SGU_GROUPS))],
        out_specs=(pl.BlockSpec((rows, D_MODEL), lambda i: (i, 0)),
                   pl.BlockSpec((MIXER_BLOCKS, 2, 2 * BLOCK, 8 * BLOCK), lambda i: (i, 0, 0, 0)),
                   pl.BlockSpec((MIXER_BLOCKS, 2, 1, 8 * BLOCK), lambda i: (i, 0, 0, 0))),
        out_shape=(jax.ShapeDtypeStruct((t, D_MODEL), BF16),
                   jax.ShapeDtypeStruct((t // BLOCK, 2, 2 * BLOCK, 8 * BLOCK), BF16),
                   jax.ShapeDtypeStruct((t // BLOCK, 2, 1, 8 * BLOCK), F32)),
        compiler_params=_params(dimension_semantics=("arbitrary",)),
    )(z, z, sink_rows, ln_g, ln_b, sgu_w, sgu_bt)


def _mixer_bwd(z, da, probs, sink_probs, ln_g, ln_b, sgu_w, sgu_wt, sgu_bt):
    t = z.shape[0]

    def body(z_all, kvp_ref, da_all, prob_ref, sink_prob_ref, lng_ref, lnb_ref, w_ref, wt_ref, bt_ref,
             dz_all, dsink_ref, dw_ref, db_ref, dlng_ref, dlnb_ref, carry_ref, dsink_acc, dbt_acc):
        step = pl.program_id(0)

        @pl.when(step == 0)
        def _():
            carry_ref[...] = jnp.zeros_like(carry_ref)
            dsink_acc[...] = jnp.zeros_like(dsink_acc)
            dbt_acc[...] = jnp.zeros_like(dbt_acc)
            dw_ref[...] = jnp.zeros_like(dw_ref)
            dlng_ref[...] = jnp.zeros_like(dlng_ref)
            dlnb_ref[...] = jnp.zeros_like(dlnb_ref)

        carry = carry_ref[...]
        for sub in reversed(range(MIXER_BLOCKS)):
            kv_before = kvp_ref[...] if sub == 0 else _Rows(z_all, sub - 1)[:, SEG_KV:SEG_KV + 2 * D_KV]
            carry = one_block(_Rows(z_all, sub), kv_before, _Rows(da_all, sub), prob_ref.at[sub], sink_prob_ref.at[sub],
                              carry, lng_ref, lnb_ref, w_ref, wt_ref, bt_ref, _Rows(dz_all, sub),
                              dw_ref, dlng_ref, dlnb_ref, dsink_acc, dbt_acc)
        carry_ref[...] = carry

        @pl.when(step == ns - 1)
        def _():
            db_ref[...] = dbt_acc[...].T[:SGU_GROUPS]
            lane_row = lax.broadcasted_iota(jnp.int32, (1, 128), 1)
            d_sink = jnp.zeros((1, 128), F32)
            for g in range(2):
                acc = dsink_acc[g]
                for j in range(8):
                    head_sum = jnp.sum(acc[:, j * BLOCK:(j + 1) * BLOCK], axis=-1, keepdims=True)
                    d_sink = d_sink + jnp.where(lane_row == 8 * g + j, head_sum, 0.0)
            dsink_ref[...] = d_sink

    def one_block(z_ref, kv_before, da_ref, prob_ref, sink_prob_ref, carry, lng_ref, lnb_ref, w_ref, wt_ref, bt_ref,
                  dz_ref, dw_ref, dlng_ref, dlnb_ref, dsink_acc, dbt_acc):
        kk, vv = _keys_values(z_ref, kv_before)
        vv_b = vv.astype(BF16)
        kkt_b, vvt_b = kk.T.astype(BF16), vv.T.astype(BF16)
        dkk = jnp.zeros((2 * BLOCK, D_KV), F32)
        dvv = jnp.zeros((2 * BLOCK, D_KV), F32)
        for g in range(2):
            qt = _heads_t([z_ref[:, _pair_cols(g, p, SEG_Q)] * ATTN_SCALE for p in range(4)], g).astype(BF16)
            prob_b, sink_prob = prob_ref[g], sink_prob_ref[g]
            prob = prob_b.astype(F32)
            ot = _dot(vvt_b, prob_b)
            gates = [z_ref[:, _pair_cols(g, p, SEG_GA)] for p in range(4)]
            sig = [_sigmoid(gt) for gt in gates]
            d_attn = [da_ref[:, _pair_cols(g, p)] for p in range(4)]
            d_ot = _heads_t([d_attn[p] * (gates[p] * sig[p]) for p in range(4)], g).astype(BF16)
            d_prob = _dot(vv_b, d_ot)
            delta = jnp.sum(prob * d_prob, axis=0, keepdims=True)
            d_scores = (prob * (d_prob - delta)).astype(BF16)
            dsink_acc[g] -= sink_prob * delta
            d_qt = _dot(kkt_b, d_scores)
            dkk = dkk + _dot_nt(d_scores, qt)
            dvv = dvv + _dot_nt(prob_b, d_ot)
            for p in range(4):
                dz_ref[:, _pair_cols(g, p, SEG_Q)] = (_pair_block(d_qt, p, g) * ATTN_SCALE).astype(BF16)
                d_silu = sig[p] * (1.0 + gates[p] * (1.0 - sig[p]))
                dz_ref[:, _pair_cols(g, p, SEG_GA)] = (d_attn[p] * _pair_block(ot, p, g) * d_silu).astype(BF16)
        d_kv = jnp.concatenate([dkk, dvv], axis=1)
        dz_ref[:, SEG_KV:SEG_KV + 2 * D_KV] = (d_kv[BLOCK:] + carry).astype(BF16)

        vhat, rstd = _layer_norm_fwd(z_ref[:, SEG_VS:SEG_VS + D_SGU])
        lng = lng_ref[...]
        vn = vhat * lng + lnb_ref[...]
        tril, triu = _tril(), _tril(transposed=True)
        lane = lax.broadcasted_iota(jnp.int32, (BLOCK, 128), 1)
        d_bt = jnp.zeros((BLOCK, 128), F32)
        d_vn = []
        for g in range(SGU_GROUPS):
            cols = slice(g * 128, (g + 1) * 128)
            wm = jnp.where(tril, w_ref[g], 0.0).astype(BF16)
            wmt = jnp.where(triu, wt_ref[g], 0.0).astype(BF16)
            vn_g = vn[:, cols].astype(BF16)
            mixed = _dot(wm, vn_g) + bt_ref[:, g:g + 1]
            gate = z_ref[:, SEG_GS + g * 128:SEG_GS + (g + 1) * 128]
            u = z_ref[:, SEG_U + g * 128:SEG_U + (g + 1) * 128]
            d_out = da_ref[:, D_ATTN + g * 128:D_ATTN + (g + 1) * 128]
            sg = _sigmoid(gate)
            d_um = d_out * (gate * sg)
            dz_ref[:, SEG_U + g * 128:SEG_U + (g + 1) * 128] = (d_um * mixed).astype(BF16)
            dz_ref[:, SEG_GS + g * 128:SEG_GS + (g + 1) * 128] = (
                d_out * (u * mixed) * (sg * (1.0 + gate * (1.0 - sg)))).astype(BF16)
            d_mixed = d_um * u
            d_mixed_b = d_mixed.astype(BF16)
            dw_ref[g] += jnp.where(tril, _dot_nt(d_mixed_b, vn_g), 0.0)
            d_bt = d_bt + jnp.where(lane == g, jnp.sum(d_mixed, axis=-1, keepdims=True), 0.0)
            d_vn.append(_dot(wmt, d_mixed_b))
        dbt_acc[...] += d_bt
        d_vn = jnp.concatenate(d_vn, axis=1)
        dlng_ref[...] += jnp.sum(d_vn * vhat, axis=0, keepdims=True)
        dlnb_ref[...] += jnp.sum(d_vn, axis=0, keepdims=True)
        d_vhat = d_vn * lng
        d_v = rstd * (d_vhat - jnp.mean(d_vhat, axis=-1, keepdims=True)
                      - vhat * jnp.mean(d_vhat * vhat, axis=-1, keepdims=True))
        dz_ref[:, SEG_VS:SEG_VS + D_SGU] = d_v.astype(BF16)
        return d_kv[:BLOCK]

    rows = MIXER_BLOCKS * BLOCK
    ns = t // rows
    rev = lambda i: ns - 1 - i
    return pl.pallas_call(
        body, name="mixer_bwd", grid=(ns,),
        in_specs=[pl.BlockSpec((rows, D_IN), lambda i: (rev(i), 0)), _kv_before_spec(rev),
                  pl.BlockSpec((rows, D_MODEL), lambda i: (rev(i), 0)),
                  pl.BlockSpec((MIXER_BLOCKS, 2, 2 * BLOCK, 8 * BLOCK), lambda i: (rev(i), 0, 0, 0)),
                  pl.BlockSpec((MIXER_BLOCKS, 2, 1, 8 * BLOCK), lambda i: (rev(i), 0, 0, 0)),
                  _const_spec((1, D_SGU)), _const_spec((1, D_SGU)),
                  _const_spec((SGU_GROUPS, BLOCK, BLOCK)), _const_spec((SGU_GROUPS, BLOCK, BLOCK)),
                  _const_spec((BLOCK, SGU_GROUPS))],
        out_specs=(pl.BlockSpec((rows, D_IN), lambda i: (rev(i), 0)), _const_spec((1, 128)),
                   _const_spec((SGU_GROUPS, BLOCK, BLOCK)), _const_spec((SGU_GROUPS, BLOCK)),
                   _const_spec((1, D_SGU)), _const_spec((1, D_SGU))),
        out_shape=(jax.ShapeDtypeStruct((t, D_IN), BF16), jax.ShapeDtypeStruct((1, 128), F32),
                   jax.ShapeDtypeStruct((SGU_GROUPS, BLOCK, BLOCK), F32), jax.ShapeDtypeStruct((SGU_GROUPS, BLOCK), F32),
                   jax.ShapeDtypeStruct((1, D_SGU), F32), jax.ShapeDtypeStruct((1, D_SGU), F32)),
        scratch_shapes=[pltpu.VMEM((BLOCK, 2 * D_KV), F32), pltpu.VMEM((2, 1, 8 * BLOCK), F32),
                        pltpu.VMEM((BLOCK, 128), F32)],
        compiler_params=_params(dimension_semantics=("arbitrary",)),
    )(z, z, da, probs, sink_probs, ln_g, ln_b, sgu_w, sgu_wt, sgu_bt)


def _out_proj_head(a, w_out_full, x, target, mod, final_g, tm=256):
    t, d = x.shape

    def body(a_ref, w_ref, x_ref, tg_ref, gate_ref, fg_ref, dx2_ref, dy_ref, loss_ref, dfg_ref, dgate_ref):
        @pl.when(pl.program_id(0) == 0)
        def _():
            loss_ref[...] = jnp.zeros_like(loss_ref)
            dfg_ref[...] = jnp.zeros_like(dfg_ref)
            dgate_ref[...] = jnp.zeros_like(dgate_ref)

        yv, gate, fg = _dot(a_ref[...], w_ref[...]), gate_ref[...], fg_ref[...]
        x2 = x_ref[...] + gate * yv
        r2 = lax.rsqrt(jnp.mean(x2 * x2, axis=-1, keepdims=True) + EPS)
        nrm = x2 * r2
        err = nrm * fg - tg_ref[...]
        loss_ref[...] += 0.5 * jnp.sum(jnp.mean(err * err, axis=-1, keepdims=True), axis=0, keepdims=True)
        fg_d = fg * (1.0 / d)
        err_nrm = err * nrm
        dfg_ref[...] += jnp.sum(err_nrm, axis=0, keepdims=True) * (1.0 / d)
        d_nrm = err * fg_d
        dx2 = r2 * (d_nrm - nrm * jnp.mean(err_nrm * fg_d, axis=-1, keepdims=True))
        dx2_ref[...] = dx2
        dgate_ref[...] += jnp.sum(dx2 * yv, axis=0, keepdims=True)
        dy_ref[...] = (dx2 * gate).astype(BF16)

    blk = pl.BlockSpec((tm, d), lambda i: (i, 0))
    row = _const_spec((1, d))
    whole = pl.BlockSpec(w_out_full.shape, lambda i: (0, 0), pipeline_mode=pl.Buffered(1))
    return pl.pallas_call(
        body, name="out_proj_head", grid=(t // tm,),
        in_specs=[pl.BlockSpec((tm, a.shape[1]), lambda i: (i, 0)), whole, blk, blk, _mod_spec(MOD_GATE, d), row],
        out_specs=(blk, blk, _const_spec((1, 128)), row, row),
        out_shape=(jax.ShapeDtypeStruct((t, d), F32), jax.ShapeDtypeStruct((t, d), BF16),
                   jax.ShapeDtypeStruct((1, 128), F32), jax.ShapeDtypeStruct((1, d), F32),
                   jax.ShapeDtypeStruct((1, d), F32)),
        compiler_params=_params(dimension_semantics=("arbitrary",)),
    )(a, w_out_full, x, target, mod, final_g)


def _z_proj_bwd_norm(dz, w_in_t, x, dx2, norm_g, mod, dep, tm=256):
    t, d = x.shape

    def body(dz_ref, w_ref, x_ref, dx2_ref, g_ref, sc_ref, dep_ref, gx_ref, dshift_ref, dscale_ref, dg_ref):
        @pl.when(pl.program_id(0) == 0)
        def _():
            dshift_ref[...] = jnp.zeros_like(dshift_ref)
            dscale_ref[...] = jnp.zeros_like(dscale_ref)
            dg_ref[...] = jnp.zeros_like(dg_ref)

        dh, xv, g = _dot(dz_ref[...], w_ref[...]), x_ref[...], g_ref[...]
        one_plus = 1.0 + sc_ref[...]
        r = lax.rsqrt(jnp.mean(xv * xv, axis=-1, keepdims=True) + EPS)
        xn = xv * r
        gain = one_plus * g
        dh_xn = dh * xn
        dh_xn_sum = jnp.sum(dh_xn, axis=0, keepdims=True)
        dshift_ref[...] += jnp.sum(dh, axis=0, keepdims=True)
        dscale_ref[...] += dh_xn_sum * g
        dg_ref[...] += dh_xn_sum * one_plus
        d_xn = dh * gain
        gx_ref[...] = dx2_ref[...] + r * (d_xn - xn * jnp.mean(dh_xn * gain, axis=-1, keepdims=True))

    blk = pl.BlockSpec((tm, d), lambda i: (i, 0))
    row = _const_spec((1, d))
    whole = pl.BlockSpec(w_in_t.shape, lambda i: (0, 0), pipeline_mode=pl.Buffered(1))
    return pl.pallas_call(
        body, name="z_proj_bwd_norm", grid=(t // tm,),
        in_specs=[pl.BlockSpec((tm, dz.shape[1]), lambda i: (i, 0)), whole, blk, blk, row, _mod_spec(MOD_SCALE, d),
                  _const_spec((8, 128))],
        out_specs=(blk, row, row, row),
        out_shape=(jax.ShapeDtypeStruct((t, d), F32),) + (jax.ShapeDtypeStruct((1, d), F32),) * 3,
        compiler_params=_params(dimension_semantics=("arbitrary",)),
    )(dz, w_in_t, x, dx2, norm_g, mod, dep)


def _adamw(w, g, m, v):
    m = ADAM_B1 * m + (1.0 - ADAM_B1) * g
    v = ADAM_B2 * v + (1.0 - ADAM_B2) * (g * g)
    m_hat = m / (1.0 - ADAM_B1 ** ADAM_STEP)
    v_hat = v / (1.0 - ADAM_B2 ** ADAM_STEP)
    delta = -ADAM_LR * (m_hat / (jnp.sqrt(v_hat) + ADAM_EPS) + ADAM_WD * w)
    return delta, m, v


def _relay_sum(device, blocks, land_pair, land_first, tr):
    _, r, c = blocks.shape

    def body(device_ref, a_ref, b_ref, c_ref, o_ref):
        o_ref[...] = (a_ref[...].astype(F32) + b_ref[...].astype(F32) + c_ref[...].astype(F32)).astype(BF16)

    second = pl.BlockSpec((None, tr, c), lambda i, device_ref: (1, i, 0))
    return pl.pallas_call(
        body, name="w_in_grad_relay_sum",
        grid_spec=pltpu.PrefetchScalarGridSpec(
            num_scalar_prefetch=1, grid=(r // tr,),
            in_specs=[pl.BlockSpec((None, tr, c), lambda i, device_ref: (device_ref[0], i, 0)), second, second],
            out_specs=pl.BlockSpec((tr, c), lambda i, device_ref: (i, 0))),
        out_shape=jax.ShapeDtypeStruct((r, c), BF16),
        compiler_params=_params(dimension_semantics=("arbitrary",)),
    )(device, blocks, land_pair, land_first)


def _adam_from_chips(chip, pair, landed, w, m, v, name, tc):
    _, r, c = pair.shape
    n = len(landed)

    def body(chip_ref, own_ref, *refs):
        w_ref, m_ref, v_ref, g_ref, d_ref, nm_ref, nv_ref = refs[n:]
        g = own_ref[...].astype(F32)
        for k in range(n):
            g = g + refs[k][...].astype(F32)
        g_ref[...] = g
        d_ref[...], nm_ref[...], nv_ref[...] = _adamw(w_ref[...], g, m_ref[...], v_ref[...])

    def landed_spec(index):
        return pl.BlockSpec((None, r, tc), lambda i, chip_ref: (index, 0, i))

    blk = pl.BlockSpec((r, tc), lambda i, chip_ref: (0, i))
    return pl.pallas_call(
        body, name=name,
        grid_spec=pltpu.PrefetchScalarGridSpec(
            num_scalar_prefetch=1, grid=(c // tc,),
            in_specs=[pl.BlockSpec((None, r, tc), lambda i, chip_ref: (chip_ref[0], 0, i))]
            + [landed_spec(index) for _, index in landed] + [blk, blk, blk],
            out_specs=(blk,) * 4),
        out_shape=(jax.ShapeDtypeStruct((r, c), F32),) * 4,
        compiler_params=_params(dimension_semantics=("arbitrary",)),
    )(chip, pair, *[array for array, _ in landed], w, m, v)


def _adam_w_ada(device, act_t, dmod_all, w, m, v, tr=512):
    r, c = w.shape

    def body(device_ref, a_ref, dm_ref, w_ref, m_ref, v_ref, g_ref, d_ref, nm_ref, nv_ref):
        g = _dot(a_ref[...].astype(BF16), dm_ref[...].astype(BF16))
        g_ref[...] = g
        d_ref[...], nm_ref[...], nv_ref[...] = _adamw(w_ref[...], g, m_ref[...], v_ref[...])

    blk = pl.BlockSpec((tr, c), lambda i, device_ref: (i, 0))
    return pl.pallas_call(
        body, name="adam_w_ada",
        grid_spec=pltpu.PrefetchScalarGridSpec(
            num_scalar_prefetch=1, grid=(r // tr,),
            in_specs=[pl.BlockSpec((tr, N_DEV), lambda i, device_ref: (i, 0)),
                      pl.BlockSpec((N_DEV, c), lambda i, device_ref: (0, device_ref[0])), blk, blk, blk],
            out_specs=(blk,) * 4),
        out_shape=(jax.ShapeDtypeStruct((r, c), F32),) * 4,
        compiler_params=_params(dimension_semantics=("arbitrary",)),
    )(device, act_t, dmod_all, w, m, v)


def _pack_small(d_shift, d_scale, d_gate, d_norm_g, d_final_g, d_ln_g, d_ln_b, loss, d_sinks, d_sgu_b):
    def body(shift_ref, scale_ref, gate_ref, ng_ref, fg_ref, lng_ref, lnb_ref, loss_ref, sink_ref, b_ref, o_ref):
        o_ref[...] = jnp.zeros_like(o_ref)
        o_ref[ROW_SHIFT:ROW_SHIFT + 1, :] = shift_ref[...]
        o_ref[ROW_SCALE:ROW_SCALE + 1, :] = scale_ref[...]
        o_ref[ROW_GATE:ROW_GATE + 1, :] = gate_ref[...]
        o_ref[ROW_NORM_G:ROW_NORM_G + 1, :] = ng_ref[...]
        o_ref[ROW_FINAL_G:ROW_FINAL_G + 1, :] = fg_ref[...]
        o_ref[ROW_LN:ROW_LN + 1, 0:D_SGU] = lng_ref[...]
        o_ref[ROW_LN:ROW_LN + 1, D_SGU:2 * D_SGU] = lnb_ref[...]
        o_ref[ROW_MISC:ROW_MISC + 1, 0:128] = loss_ref[...]
        o_ref[ROW_MISC:ROW_MISC + 1, 128:256] = sink_ref[...]
        o_ref[ROW_SGU_B:ROW_SGU_B + SGU_GROUPS, 0:BLOCK] = b_ref[...]

    return pl.pallas_call(
        body, name="pack_small", out_shape=jax.ShapeDtypeStruct((SMALL_ROWS, D_MODEL), F32),
        compiler_params=_params(),
    )(d_shift, d_scale, d_gate, d_norm_g, d_final_g, d_ln_g, d_ln_b, loss, d_sinks, d_sgu_b)


_SMALL_NAMES = ("norm_g", "b_ada", "attn_sinks", "sgu_ln_g", "sgu_ln_b", "sgu_w", "sgu_b", "final_g")


def _adam_small(partials, d_sgu_w_all, weights, moments_m, moments_v):
    names = _SMALL_NAMES
    k = len(names)

    def body(*refs):
        p_ref, sw_ref = refs[0], refs[1]
        w_refs, m_refs, v_refs = refs[2:2 + k], refs[2 + k:2 + 2 * k], refs[2 + 2 * k:2 + 3 * k]
        loss_ref, dmod_ref = refs[2 + 3 * k], refs[3 + 3 * k]
        out_refs = refs[4 + 3 * k:4 + 7 * k]
        sum_ref = refs[4 + 7 * k]
        total = p_ref[0]
        for j in range(1, N_DEV):
            total = total + p_ref[j]
        sum_ref[...] = total
        for j in range(N_DEV):
            for part, row in enumerate((ROW_SHIFT, ROW_SCALE, ROW_GATE)):
                dmod_ref[j:j + 1, part * D_MODEL:(part + 1) * D_MODEL] = p_ref[j, row:row + 1, :]
        loss_ref[...] = sum_ref[ROW_MISC:ROW_MISC + 1, 0:1]
        d_sgu_w = sw_ref[0]
        for j in range(1, N_DEV):
            d_sgu_w = d_sgu_w + sw_ref[j]
        grads = {
            "norm_g": sum_ref[ROW_NORM_G:ROW_NORM_G + 1, :],
            "b_ada": jnp.concatenate([sum_ref[r:r + 1, :] for r in (ROW_SHIFT, ROW_SCALE, ROW_GATE)], axis=1),
            "attn_sinks": sum_ref[ROW_MISC:ROW_MISC + 1, 128:128 + N_Q_HEADS],
            "sgu_ln_g": sum_ref[ROW_LN:ROW_LN + 1, 0:D_SGU],
            "sgu_ln_b": sum_ref[ROW_LN:ROW_LN + 1, D_SGU:2 * D_SGU],
            "sgu_w": d_sgu_w[None],
            "sgu_b": sum_ref[ROW_SGU_B:ROW_SGU_B + SGU_GROUPS, 0:BLOCK][None],
            "final_g": sum_ref[ROW_FINAL_G:ROW_FINAL_G + 1, :],
        }
        for i, name in enumerate(names):
            g = grads[name]
            delta, m, v = _adamw(w_refs[i][...], g, m_refs[i][...], v_refs[i][...])
            out_refs[4 * i][...] = g
            out_refs[4 * i + 1][...] = delta
            out_refs[4 * i + 2][...] = m
            out_refs[4 * i + 3][...] = v

    shapes = [jax.ShapeDtypeStruct((1, 1), F32), jax.ShapeDtypeStruct((N_DEV, 3 * D_MODEL), F32)]
    for name in names:
        shapes += [jax.ShapeDtypeStruct(weights[name].shape, F32)] * 4
    outs = pl.pallas_call(
        body, name="adam_small", out_shape=tuple(shapes),
        scratch_shapes=[pltpu.VMEM((SMALL_ROWS, D_MODEL), F32)],
        compiler_params=_params(),
    )(partials, d_sgu_w_all, *[weights[n] for n in names], *[moments_m[n] for n in names],
      *[moments_v[n] for n in names])
    return outs[0], outs[1], {name: outs[2 + 4 * i:6 + 4 * i] for i, name in enumerate(names)}


def kernel(x, c, norm_g, w_ada, b_ada, w_in, attn_sinks, sgu_ln_g, sgu_ln_b, sgu_w, sgu_b, w_out, final_g, loss_target, m_norm_g, m_w_ada, m_b_ada, m_w_in, m_attn_sinks, m_sgu_ln_g, m_sgu_ln_b, m_sgu_w, m_sgu_b, m_w_out, m_final_g, v_norm_g, v_w_ada, v_b_ada, v_w_in, v_attn_sinks, v_sgu_ln_g, v_sgu_ln_b, v_sgu_w, v_sgu_b, v_w_out, v_final_g):
    xi, yi, ci = _place()
    me = 4 * xi + 2 * yi + ci
    x2d, target = x[0], loss_target[0]
    t = x2d.shape[0]

    core = ci.astype(jnp.int32).reshape(1)
    chip = (2 * xi + yi).astype(jnp.int32).reshape(1)

    first = _own_block_copies(_first_targets)
    first_flight = _start_copies([_with_own_slot(w_in[0].T.astype(BF16), me)], first, 2, core, "gather_w_in_start")

    c_all = _all_gather_small(c.reshape(8, 256) + first_flight[3][0, 0], "gather_c").reshape(N_DEV, D_MODEL)
    device = me.astype(jnp.int32).reshape(1)
    c_act, mod_part = _modulation(device, c_all, w_ada[0], b_ada)
    mod_all = _all_gather_small(mod_part, "gather_mod")

    across = _wait_then_start(first_flight, lambda *a: first(*a)[1:], _second_axis_stage_copies, 3, mod_all,
                              "gather_w_in_second_axis_stage")
    mod = lax.dynamic_index_in_dim(mod_all, me, axis=1, keepdims=False).reshape(1, 3 * D_MODEL)
    mod = mod + across[3][0, 0]
    h = _modulated_norm(x2d, norm_g, mod)

    w_in_pair = _wait_copies((first_flight[0], first_flight[1], across[2], None), lambda *a: first(*a)[:1], h,
                             "gather_w_in_sibling_wait")
    z_own = _z_proj(h, w_in_pair[0].reshape(D_IN, D_MODEL), chip, 0, 1, None, "z_proj_own")
    w_out_early = _own_block_copies(lambda x, y, c: [(x, y, 1 - c), (*_second_axis_chip(x, y, c), c)])
    w_out_late = _own_block_copies(lambda x, y, c: [(*_first_axis_chip(x, y, c), c), (1 - x, 1 - y, c)])
    forward = _wait_then_start(
        (across[0], across[1], w_in_pair, None), lambda *a: _second_axis_stage_copies(*a)[:1],
        lambda refs, s, r: _second_axis_forward_copies(refs[:1], s, r) + _group(w_out_early, 1, 1, 1)(refs, s, r),
        3, z_own, "gather_w_in_second_axis_forward", more_bufs=[_with_own_slot(w_out[0].astype(BF16), me)])
    w_in_most = _wait_copies((across[0], across[1], forward[2][:1], None),
                             lambda *a: _second_axis_stage_copies(*a)[1:2], z_own, "gather_w_in_first_forward_wait")
    w_in_most = _wait_copies((forward[0], forward[1], w_in_most, None), _second_axis_forward_copies, z_own,
                             "gather_w_in_second_forward_wait")
    z_early = _z_proj(h, w_in_most[0].reshape(D_IN, D_MODEL), chip, 1, _Z_EARLY_TILES - 1, z_own, "z_proj_early")
    last = _wait_then_start(
        (across[0], across[1], [w_in_most[0], forward[2][1]], None), lambda *a: _second_axis_stage_copies(*a)[2:],
        lambda refs, s, r: _diagonal_forward_copies(refs[:1], s, r) + _group(w_out_late, 1, 1, 1)(refs, s, r),
        3, z_early, "gather_w_in_last_stage")
    w_in_all = _wait_copies((last[0], last[1], last[2][:1], None), _diagonal_forward_copies, z_early,
                            "gather_w_in_last_wait")[0]
    w_in_t = w_in_all.reshape(D_IN, D_MODEL)
    z = _z_proj(h, w_in_t, chip, _Z_EARLY_TILES, 7 - _Z_EARLY_TILES, z_early, "z_proj_late")
    w_out_half = _wait_copies((forward[0], forward[1], last[2][1:], None), _group(w_out_early, 0, 1, 1), z,
                              "gather_w_out_early_wait")
    w_out_flight = _wait_then_start((last[0], last[1], w_out_half, None), _group(w_out_late, 0, 1, 1),
                                    _forward_copies, 3, z, "gather_w_out_forward_stage")
    sink_rows = jnp.repeat(attn_sinks.reshape(N_Q_HEADS), BLOCK).reshape(2, 1, 8 * BLOCK)
    sgu_bt = sgu_b[0].T
    a, probs, sink_probs = _mixer_fwd(z, sink_rows + w_out_flight[3][0, 0], sgu_ln_g, sgu_ln_b, sgu_w[0], sgu_bt)
    w_out_all = _wait_copies(w_out_flight, _forward_copies, a, "gather_w_out_forward_wait")[0]
    w_out_full = w_out_all.reshape(D_MODEL, D_MODEL)
    final_g_row = final_g.reshape(1, D_MODEL)
    dx2, dy, loss_part, d_final_g, d_gate = _out_proj_head(a, w_out_full, x2d, target, mod, final_g_row)

    da = _matmul(dy, w_out_full, "nt", F32, min(t, 1024), 1024, "out_proj_bwd")
    dw_out = _matmul(a, dy, "tn", BF16, 1024, 1024, "w_out_grad").reshape(4, 2, W_OUT_SHARD, D_MODEL)
    pair_out = _pair_reduce(dw_out, _every_chip, "w_out_grad_pair_reduce", W_OUT_SHARD // 2)
    dz, d_sinks, d_sgu_w, d_sgu_b, d_ln_g, d_ln_b = _mixer_bwd(
        z, da, probs, sink_probs, sgu_ln_g, sgu_ln_b, sgu_w[0], jnp.swapaxes(sgu_w[0], 1, 2), sgu_bt)
    sgu_w_to_all = _group(_own_block_copies(_all_others), 2, 1, 3)
    both = _start_copies(
        [pair_out, lax.empty((3, W_OUT_SHARD, D_MODEL), BF16), _with_own_slot(d_sgu_w, me)],
        lambda refs, s, r: _chip_copies(refs[:2], s, r) + sgu_w_to_all(refs, s, r), 3 + N_DEV - 1, core,
        "w_out_grad_chip_and_sgu_w_gather_start")
    out_flight, sgu_w_flight = (both[0], both[1], both[2][:2], None), (both[0], both[1], both[2][2:], None)
    dw_in_t = _matmul(dz, h, "tn", BF16, 768, D_MODEL, "w_in_grad", dep=both[3])
    pair_in = _pair_reduce(dw_in_t.reshape(4, 2, W_IN_SHARD, D_MODEL), _first_hop_chips, "w_in_grad_pair_reduce",
                           W_IN_SHARD // 3)
    first_hop = lambda refs, s, r: _first_hop_copies(refs[:2], s, r) + _group(_late_pair_copies, 2, 2, 2)(refs, s, r)
    hop1 = _start_copies(
        [pair_in, lax.empty((2, W_IN_SHARD, D_MODEL), BF16), dw_in_t.reshape(N_DEV, W_IN_SHARD, D_MODEL),
         lax.empty((2, W_IN_SHARD, D_MODEL), BF16)], first_hop, 4, core, "w_in_grad_first_hop_start")
    grad_x, d_shift, d_scale, d_norm_g = _z_proj_bwd_norm(dz, w_in_t, x2d, dx2, norm_g, mod, hop1[3])

    partial = _pack_small(d_shift, d_scale, d_gate, d_norm_g, d_final_g, d_ln_g, d_ln_b, loss_part, d_sinks, d_sgu_b)
    small_flight = _start_copies([_with_own_slot(partial, me)], _own_block_copies(_all_others), N_DEV - 1, core,
                                 "small_grad_gather_start")
    _, land_first, dw_in_t, land_pair = _wait_copies(hop1, first_hop, small_flight[3], "w_in_grad_first_hop_wait")
    second_device = (4 * ((xi + ci) % 2) + 2 * ((yi + 1 - ci) % 2) + ci).astype(jnp.int32).reshape(1)
    relay = _relay_sum(second_device, dw_in_t, land_pair, land_first, W_IN_SHARD // 3)
    hop2 = _start_copies([relay, lax.empty((1, W_IN_SHARD, D_MODEL), BF16)], _second_hop_copies, 1, core,
                         "w_in_grad_second_hop_start")
    pair_out, land_out = _wait_copies(out_flight, _chip_copies, hop2[3], "w_out_grad_chip_wait")
    big = {"w_out": _adam_from_chips(chip, pair_out, [(land_out, k) for k in range(3)], w_out[0], m_w_out[0],
                                     v_w_out[0], "adam_w_out", 1024)}
    partial_all = _wait_copies(small_flight, _own_block_copies(_all_others), big["w_out"][0],
                               "small_grad_gather_wait")[0]
    d_sgu_w_all = _wait_copies(sgu_w_flight, _group(_own_block_copies(_all_others), 0, 1, 3), partial_all,
                               "sgu_w_grad_gather_wait")[0]
    weights = {"norm_g": norm_g, "b_ada": b_ada, "attn_sinks": attn_sinks, "sgu_ln_g": sgu_ln_g,
               "sgu_ln_b": sgu_ln_b, "sgu_w": sgu_w, "sgu_b": sgu_b, "final_g": final_g_row}
    moments_m = {"norm_g": m_norm_g, "b_ada": m_b_ada, "attn_sinks": m_attn_sinks, "sgu_ln_g": m_sgu_ln_g,
                 "sgu_ln_b": m_sgu_ln_b, "sgu_w": m_sgu_w, "sgu_b": m_sgu_b,
                 "final_g": m_final_g.reshape(1, D_MODEL)}
    moments_v = {"norm_g": v_norm_g, "b_ada": v_b_ada, "attn_sinks": v_attn_sinks, "sgu_ln_g": v_sgu_ln_g,
                 "sgu_ln_b": v_sgu_ln_b, "sgu_w": v_sgu_w, "sgu_b": v_sgu_b,
                 "final_g": v_final_g.reshape(1, D_MODEL)}
    loss, dmod_all, small = _adam_small(partial_all, d_sgu_w_all, weights, moments_m, moments_v)
    small["final_g"] = tuple(o.reshape(D_MODEL) for o in small["final_g"])

    big["w_ada"] = _adam_w_ada(device, c_act.T, dmod_all, w_ada[0], m_w_ada[0], v_w_ada[0])
    _, land_second = _wait_copies(hop2, _second_hop_copies, big["w_ada"][0], "w_in_grad_second_hop_wait")
    big["w_in"] = tuple(o.T for o in _adam_from_chips(
        device, dw_in_t, [(land_pair, 0), (land_first, 0), (land_second, 0)], w_in[0].T, m_w_in[0].T, v_w_in[0].T,
        "adam_w_in", 512))
    order = ["norm_g", "w_ada", "b_ada", "w_in", "attn_sinks", "sgu_ln_g", "sgu_ln_b", "sgu_w", "sgu_b", "w_out",
             "final_g"]
    outs = [loss.reshape(()), grad_x[None]]
    for k in range(4):
        for name in order:
            outs.append(big[name][k][None] if name in big else small[name][k])
    return tuple(outs)
```

```python
import jax
import jax.numpy as jnp
from jax import lax
from jax.experimental import pallas as pl
from jax.experimental.pallas import tpu as pltpu

F32 = jnp.float32
BF16 = jnp.bfloat16
MESH = pl.DeviceIdType.MESH

N_DEV = 8
D_MODEL = 2048
HEAD_DIM = 64
D_ATTN = 1024
N_Q_HEADS = 16
D_KV = 128
BLOCK = 128
D_SGU = 1024
SGU_GROUPS = 8
D_IN = 5376
W_IN_SHARD = D_IN // N_DEV
W_OUT_SHARD = D_MODEL // N_DEV
W_ADA_SHARD = 3 * D_MODEL // N_DEV
EPS = 1e-6
ATTN_SCALE = 0.125

ADAM_LR = 0.001
ADAM_B1 = 0.9
ADAM_B2 = 0.999
ADAM_EPS = 1e-08
ADAM_WD = 0.01
ADAM_STEP = 10

SEG_Q, SEG_KV, SEG_GA, SEG_U, SEG_VS, SEG_GS = 0, 1024, 1280, 2304, 3328, 4352

VMEM_LIMIT = 56 * 1024 * 1024

ROW_SHIFT, ROW_SCALE, ROW_GATE, ROW_NORM_G, ROW_FINAL_G, ROW_LN, ROW_MISC, ROW_SGU_B = 0, 1, 2, 3, 4, 5, 6, 8
SMALL_ROWS = 16


def _params(**kw):
    return pltpu.CompilerParams(vmem_limit_bytes=VMEM_LIMIT, **kw)


def _sigmoid(x):
    return 0.5 * (jnp.tanh(0.5 * x) + 1.0)


def _place():
    return lax.axis_index("x"), lax.axis_index("y"), lax.axis_index("c")


def _every_chip(x, y, c):
    return [0, 1, 2, 3]


def _first_hop_chips(x, y, c):
    first = _first_axis_chip(x, y, c)
    return [2 * first[0] + first[1], 2 * (1 - x) + (1 - y)]


def _pair_reduce(blocks, chips, name, row_chunk):
    _, _, r, cols = blocks.shape
    n = len(chips(0, 0, 0))
    assert r % row_chunk == 0

    def body(in_ref, out_ref, land, own, summed, send_sems, recv_sems, own_sems, out_sems):
        x, y, c = _place()
        sends, loads, stores = [], [], []
        for m in range(n):
            cp = pltpu.make_async_remote_copy(
                src_ref=in_ref.at[chips(x, y, 1 - c)[m], 1 - c], dst_ref=land.at[m], send_sem=send_sems.at[m],
                recv_sem=recv_sems.at[m], device_id=(x, y, 1 - c), device_id_type=MESH)
            cp.start()
            sends.append(cp)
            ld = pltpu.make_async_copy(in_ref.at[chips(x, y, c)[m], c], own.at[m], own_sems.at[m])
            ld.start()
            loads.append(ld)
        for m in range(n):
            sends[m].wait_recv()
            loads[m].wait()
            for k in range(r // row_chunk):
                rows = slice(k * row_chunk, (k + 1) * row_chunk)
                summed[m, rows, :] = (own[m, rows, :].astype(F32) + land[m, rows, :].astype(F32)).astype(BF16)
            st = pltpu.make_async_copy(summed.at[m], out_ref.at[m], out_sems.at[m])
            st.start()
            stores.append(st)
        for m in range(n):
            sends[m].wait_send()
            stores[m].wait()

    spec = pl.BlockSpec(memory_space=pl.ANY)
    return pl.pallas_call(
        body, name=name, out_shape=jax.ShapeDtypeStruct((n, r, cols), BF16),
        in_specs=[spec], out_specs=spec,
        scratch_shapes=[pltpu.VMEM((n, r, cols), BF16), pltpu.VMEM((n, r, cols), BF16), pltpu.VMEM((n, r, cols), BF16),
                        pltpu.SemaphoreType.DMA((n,)), pltpu.SemaphoreType.DMA((n,)), pltpu.SemaphoreType.DMA((n,)),
                        pltpu.SemaphoreType.DMA((n,))],
        compiler_params=_params(),
    )(blocks)


_HBM = pl.BlockSpec(memory_space=pltpu.HBM)
_SEM = pl.BlockSpec(memory_space=pltpu.SEMAPHORE)
_EFFECT = pltpu.SideEffectType.DATAFLOW_SIDE_EFFECTING


def _start_copies(bufs, copies, n_copies, after, name):
    nb = len(bufs)

    def body(*refs):
        for cp in copies(refs[:nb], refs[nb + 1], refs[nb + 2]):
            cp.start()
        refs[-1][...] = jnp.zeros_like(refs[-1])

    out = pl.pallas_call(
        body, name=name,
        out_shape=(pltpu.SemaphoreType.DMA((n_copies,)), pltpu.SemaphoreType.DMA((n_copies,)),
                   *[pltpu.HBM(b.shape, b.dtype) for b in bufs], jax.ShapeDtypeStruct((8, 128), F32)),
        in_specs=(_HBM,) * nb + (pl.BlockSpec(memory_space=pl.ANY),),
        out_specs=(_SEM, _SEM) + (_HBM,) * nb + (pl.BlockSpec(memory_space=pltpu.VMEM),),
        input_output_aliases={i: 2 + i for i in range(nb)},
        compiler_params=pltpu.CompilerParams(has_side_effects=_EFFECT),
    )(*[pltpu.with_memory_space_constraint(b, pltpu.HBM) for b in bufs], after)
    return out[0], out[1], list(out[2:2 + nb]), out[-1]


def _wait_copies(flight, copies, after, name):
    send_sems, recv_sems, bufs, _ = flight
    nb = len(bufs)

    def body(*refs):
        for cp in copies(refs[:nb], refs[nb], refs[nb + 1]):
            cp.wait_send()
            cp.wait_recv()

    return pl.pallas_call(
        body, name=name,
        out_shape=tuple(pltpu.HBM(b.shape, b.dtype) for b in bufs),
        in_specs=(_HBM,) * nb + (_SEM, _SEM, pl.BlockSpec(memory_space=pl.ANY)), out_specs=(_HBM,) * nb,
        input_output_aliases={i: i for i in range(nb)},
        compiler_params=pltpu.CompilerParams(has_side_effects=_EFFECT),
    )(*bufs, send_sems, recv_sems, after)


class _From:
    def __init__(self, sems, offset):
        self.sems, self.offset = sems, offset

    @property
    def at(self):
        return self

    def __getitem__(self, k):
        return self.sems.at[k + self.offset]


def _group(copies, first_buf, n_bufs, offset):
    def grouped(refs, send_sems, recv_sems):
        return copies(refs[first_buf:first_buf + n_bufs], _From(send_sems, offset), _From(recv_sems, offset))
    return grouped


def _wait_then_start(flight, waited, started, n_started, after, name, more_bufs=()):
    old_send, old_recv, bufs, _ = flight
    bufs = list(bufs) + [pltpu.with_memory_space_constraint(b, pltpu.HBM) for b in more_bufs]
    nb = len(bufs)

    def body(*refs):
        for cp in waited(refs[:nb], refs[nb], refs[nb + 1]):
            cp.wait_send()
            cp.wait_recv()
        for cp in started(refs[:nb], refs[nb + 3], refs[nb + 4]):
            cp.start()
        refs[-1][...] = jnp.zeros_like(refs[-1])

    out = pl.pallas_call(
        body, name=name,
        out_shape=(pltpu.SemaphoreType.DMA((n_started,)), pltpu.SemaphoreType.DMA((n_started,)),
                   *[pltpu.HBM(b.shape, b.dtype) for b in bufs], jax.ShapeDtypeStruct((8, 128), F32)),
        in_specs=(_HBM,) * nb + (_SEM, _SEM, pl.BlockSpec(memory_space=pl.ANY)),
        out_specs=(_SEM, _SEM) + (_HBM,) * nb + (pl.BlockSpec(memory_space=pltpu.VMEM),),
        input_output_aliases={i: 2 + i for i in range(nb)},
        compiler_params=pltpu.CompilerParams(has_side_effects=_EFFECT),
    )(*bufs, old_send, old_recv, after)
    return out[0], out[1], list(out[2:2 + nb]), out[-1]


def _late_pair_copies(refs, send_sems, recv_sems):
    blocks_ref, land_ref = refs
    x, y, c = _place()
    first = _first_axis_chip(x, y, c)
    devices = [4 * x + 2 * y + 1 - c, 4 * first[0] + 2 * first[1] + 1 - c]
    return [pltpu.make_async_remote_copy(
        src_ref=blocks_ref.at[devices[k]], dst_ref=land_ref.at[k], send_sem=send_sems.at[k], recv_sem=recv_sems.at[k],
        device_id=(x, y, 1 - c), device_id_type=MESH) for k in range(2)]


def _chip_copies(refs, send_sems, recv_sems):
    pair_ref, land_ref = refs
    x, y, c = _place()
    chips = [(1 - x, y), (x, 1 - y), (1 - x, 1 - y)]
    return [pltpu.make_async_remote_copy(
        src_ref=pair_ref.at[2 * chip[0] + chip[1]], dst_ref=land_ref.at[k],
        send_sem=send_sems.at[k], recv_sem=recv_sems.at[k],
        device_id=(*chip, c), device_id_type=MESH) for k, chip in enumerate(chips)]


def _first_hop_copies(refs, send_sems, recv_sems):
    pair_ref, land_ref = refs
    x, y, c = _place()
    return [pltpu.make_async_remote_copy(
        src_ref=pair_ref.at[k], dst_ref=land_ref.at[k], send_sem=send_sems.at[k], recv_sem=recv_sems.at[k],
        device_id=(*_first_axis_chip(x, y, c), c), device_id_type=MESH) for k in range(2)]


def _second_hop_copies(refs, send_sems, recv_sems):
    relay_ref, land_ref = refs
    x, y, c = _place()
    second = ((x + c) % 2, (y + 1 - c) % 2)
    return [pltpu.make_async_remote_copy(
        src_ref=relay_ref, dst_ref=land_ref.at[0], send_sem=send_sems.at[0], recv_sem=recv_sems.at[0],
        device_id=(*second, c), device_id_type=MESH)]


def _own_block_copies(targets):
    def copies(refs, send_sems, recv_sems):
        x, y, c = _place()
        mine = refs[0].at[4 * x + 2 * y + c]
        return [pltpu.make_async_remote_copy(
            src_ref=mine, dst_ref=mine, send_sem=send_sems.at[k], recv_sem=recv_sems.at[k],
            device_id=to, device_id_type=MESH) for k, to in enumerate(targets(x, y, c))]
    return copies


def _all_others(x, y, c):
    flip = lambda v, f: 1 - v if f else v
    return [(flip(x, r & 4), flip(y, r & 2), flip(c, r & 1)) for r in range(1, N_DEV)]


def _forward_copies(refs, send_sems, recv_sems):
    x, y, c = _place()
    chips = [(1 - x, y), (x, 1 - y), (1 - x, 1 - y)]
    return [pltpu.make_async_remote_copy(
        src_ref=refs[0].at[4 * chip[0] + 2 * chip[1] + c], dst_ref=refs[0].at[4 * chip[0] + 2 * chip[1] + c],
        send_sem=send_sems.at[k], recv_sem=recv_sems.at[k],
        device_id=(x, y, 1 - c), device_id_type=MESH) for k, chip in enumerate(chips)]


def _first_axis_chip(x, y, c):
    return (x + 1 - c) % 2, (y + c) % 2


def _second_axis_chip(x, y, c):
    return (x + c) % 2, (y + 1 - c) % 2


def _first_targets(x, y, c):
    return [(x, y, 1 - c), (*_first_axis_chip(x, y, c), c)]


def _all_gather_small(shard, name):
    def body(in_ref, out_ref, send_sems, recv_sems, local_sem):
        x, y, c = _place()
        me, sibling = 4 * x + 2 * y + c, (x, y, 1 - c)
        first, second = _first_axis_chip(x, y, c), _second_axis_chip(x, y, c)

        def pair(chip):
            return out_ref.at[pl.ds(2 * (2 * chip[0] + chip[1]), 2)]

        def exchange(k, src, dst, to):
            cp = pltpu.make_async_remote_copy(src_ref=src, dst_ref=dst, send_sem=send_sems.at[k],
                                              recv_sem=recv_sems.at[k], device_id=to, device_id_type=MESH)
            cp.start()
            cp.wait()

        own = pltpu.make_async_copy(in_ref, out_ref.at[me], local_sem)
        own.start()
        exchange(0, in_ref, out_ref.at[me], sibling)
        own.wait()
        exchange(1, pair((x, y)), pair((x, y)), (*second, c))
        exchange(2, pair(second), pair(second), sibling)
        exchange(3, pair(first), pair(first), (*second, c))

    spec = pl.BlockSpec(memory_space=pltpu.VMEM)
    return pl.pallas_call(
        body, name=name, out_shape=jax.ShapeDtypeStruct((N_DEV,) + shard.shape, shard.dtype),
        in_specs=[spec], out_specs=spec,
        scratch_shapes=[pltpu.SemaphoreType.DMA((4,)), pltpu.SemaphoreType.DMA((4,)), pltpu.SemaphoreType.DMA],
        compiler_params=_params(),
    )(shard)


def _slot_copies(refs, send_sems, recv_sems, plan):
    copies = []
    for k, ((px, py, pc), to) in enumerate(plan):
        blk = refs[0].at[4 * px + 2 * py + pc]
        copies.append(pltpu.make_async_remote_copy(
            src_ref=blk, dst_ref=blk, send_sem=send_sems.at[k], recv_sem=recv_sems.at[k],
            device_id=to, device_id_type=MESH))
    return copies


def _second_axis_stage_copies(refs, send_sems, recv_sems):
    x, y, c = _place()
    first, second = (*_first_axis_chip(x, y, c), c), (*_second_axis_chip(x, y, c), c)
    return _slot_copies(refs, send_sems, recv_sems, [((x, y, c), second), (first, (x, y, 1 - c)), (first, second)])


def _second_axis_forward_copies(refs, send_sems, recv_sems):
    x, y, c = _place()
    return _slot_copies(refs, send_sems, recv_sems, [((*_second_axis_chip(x, y, c), c), (x, y, 1 - c))])


def _diagonal_forward_copies(refs, send_sems, recv_sems):
    x, y, c = _place()
    blk = refs[0].at[4 * (1 - x) + 2 * (1 - y) + c]
    return [pltpu.make_async_remote_copy(
        src_ref=blk, dst_ref=blk, send_sem=send_sems.at[0], recv_sem=recv_sems.at[0],
        device_id=(x, y, 1 - c), device_id_type=MESH)]


def _with_own_slot(block, me):
    return lax.dynamic_update_index_in_dim(lax.empty((N_DEV,) + block.shape, block.dtype), block, me, 0)


def _matmul(a, b, dims, out_dtype, tm, tn, name, dep=None):
    if dims == "nn":
        (m, k), n = a.shape, b.shape[1]
        a_spec = pl.BlockSpec((tm, k), lambda i, j: (i, 0))
        b_spec = pl.BlockSpec((k, tn), lambda i, j: (0, j))
        contract = ((1,), (0,))
    elif dims == "nt":
        (m, k), n = a.shape, b.shape[0]
        a_spec = pl.BlockSpec((tm, k), lambda i, j: (i, 0))
        b_spec = pl.BlockSpec((tn, k), lambda i, j: (j, 0))
        contract = ((1,), (1,))
    else:
        (k, m), n = a.shape, b.shape[1]
        a_spec = pl.BlockSpec((k, tm), lambda i, j: (0, i))
        b_spec = pl.BlockSpec((k, tn), lambda i, j: (0, j))
        contract = ((0,), (0,))
    assert m % tm == 0 and n % tn == 0 and a.dtype == BF16 and b.dtype == BF16

    def body(a_ref, b_ref, *rest):
        rest[-1][...] = lax.dot_general(a_ref[...], b_ref[...], (contract, ((), ())),
                                        preferred_element_type=F32).astype(out_dtype)

    deps = [] if dep is None else [dep]
    return pl.pallas_call(
        body, name=name, grid=(m // tm, n // tn),
        in_specs=[a_spec, b_spec] + [pl.BlockSpec((8, 128), lambda i, j: (0, 0))] * len(deps),
        out_specs=pl.BlockSpec((tm, tn), lambda i, j: (i, j)),
        out_shape=jax.ShapeDtypeStruct((m, n), out_dtype),
        compiler_params=_params(dimension_semantics=("arbitrary", "arbitrary")),
    )(a, b, *deps)


Z_TILE = 768
_Z_TILE_ORDER = ((0, 1, 2, 3, 4, 5, 6), (2, 0, 1, 6, 3, 4, 5), (4, 0, 5, 6, 1, 2, 3), (6, 2, 3, 4, 0, 1, 5))
_Z_EARLY_TILES = 4


def _z_proj(h, w_in_t, chip, first, count, z_prev, name, dep=None, tr=1024):
    t = h.shape[0]

    def body(chip_ref, h_ref, w_ref, *rest):
        rest[-1][...] = _dot_nt(h_ref[...], w_ref[...])

    def tile(j, chip_ref):
        picked = 0
        for c, order in enumerate(_Z_TILE_ORDER):
            for k in range(count):
                picked = picked + jnp.where((chip_ref[0] == c) & (j == k), order[first + k], 0)
        return picked

    prev = [] if z_prev is None else [z_prev]
    deps = [] if dep is None else [dep]
    return pl.pallas_call(
        body, name=name,
        grid_spec=pltpu.PrefetchScalarGridSpec(
            num_scalar_prefetch=1, grid=(count, t // tr),
            in_specs=[pl.BlockSpec((tr, D_MODEL), lambda j, i, o: (i, 0)),
                      pl.BlockSpec((Z_TILE, D_MODEL), lambda j, i, o: (tile(j, o), 0))]
            + [pl.BlockSpec(memory_space=pl.ANY)] * len(prev)
            + [pl.BlockSpec((8, 128), lambda j, i, o: (0, 0))] * len(deps),
            out_specs=pl.BlockSpec((tr, Z_TILE), lambda j, i, o: (i, tile(j, o)))),
        out_shape=jax.ShapeDtypeStruct((t, D_IN), F32),
        input_output_aliases={3: 0} if prev else {},
        compiler_params=_params(dimension_semantics=("arbitrary", "arbitrary")),
    )(chip, h, w_in_t, *prev, *deps)


def _modulation(device, c_all, w_ada, b_ada):
    def body(device_ref, c_ref, w_ref, b_ref, act_ref, mod_ref):
        cv = c_ref[...]
        act = cv * _sigmoid(cv)
        act_ref[...] = act
        mod_ref[...] = jnp.dot(act.astype(BF16), w_ref[...].astype(BF16), preferred_element_type=F32) + b_ref[...]

    whole = lambda a: pl.BlockSpec(a.shape, lambda i, device_ref: (0,) * a.ndim)
    return pl.pallas_call(
        body, name="modulation",
        grid_spec=pltpu.PrefetchScalarGridSpec(
            num_scalar_prefetch=1, grid=(1,),
            in_specs=[whole(c_all), whole(w_ada), pl.BlockSpec((1, W_ADA_SHARD), lambda i, device_ref: (0, device_ref[0]))],
            out_specs=(whole(c_all), pl.BlockSpec((N_DEV, W_ADA_SHARD), lambda i, device_ref: (0, 0)))),
        out_shape=(jax.ShapeDtypeStruct(c_all.shape, F32), jax.ShapeDtypeStruct((N_DEV, W_ADA_SHARD), F32)),
        compiler_params=_params(dimension_semantics=("arbitrary",)),
    )(device, c_all, w_ada, b_ada)


MOD_SHIFT, MOD_SCALE, MOD_GATE = 0, 1, 2


def _mod_spec(part, d):
    return pl.BlockSpec((1, d), lambda i: (0, part))


def _modulated_norm(x, norm_g, mod, tm=512):
    t, d = x.shape

    def body(x_ref, g_ref, sc_ref, sh_ref, h_ref):
        xv = x_ref[...]
        r = lax.rsqrt(jnp.mean(xv * xv, axis=-1, keepdims=True) + EPS)
        h = (xv * r) * g_ref[...] * (1.0 + sc_ref[...]) + sh_ref[...]
        h_ref[...] = h.astype(BF16)

    row = pl.BlockSpec((1, d), lambda i: (0, 0))
    return pl.pallas_call(
        body, name="modulated_norm", grid=(t // tm,),
        in_specs=[pl.BlockSpec((tm, d), lambda i: (i, 0)), row, _mod_spec(MOD_SCALE, d), _mod_spec(MOD_SHIFT, d)],
        out_specs=pl.BlockSpec((tm, d), lambda i: (i, 0)),
        out_shape=jax.ShapeDtypeStruct((t, d), BF16),
        compiler_params=_params(dimension_semantics=("arbitrary",)),
    )(x, norm_g, mod, mod)


def _window_bias(block_index):
    s = lax.broadcasted_iota(jnp.int32, (2 * BLOCK, BLOCK), 0)
    t = lax.broadcasted_iota(jnp.int32, (2 * BLOCK, BLOCK), 1)
    valid = ((s < BLOCK) & (s > t) & (block_index > 0)) | ((s >= BLOCK) & ((s - BLOCK) <= t))
    bias = jnp.where(valid, 0.0, -jnp.inf).astype(F32)
    return jnp.concatenate([bias] * 8, axis=1)


def _heads_t(pair_blocks, g):
    top = lax.broadcasted_iota(jnp.int32, (BLOCK, BLOCK), 0) < HEAD_DIM
    zeros = jnp.zeros((HEAD_DIM, BLOCK), F32)
    tiles = []
    for blk in pair_blocks:
        tp = blk.T
        if g == 0:
            tiles += [jnp.where(top, tp, 0.0), jnp.concatenate([tp[HEAD_DIM:], zeros], axis=0)]
        else:
            tiles += [jnp.concatenate([zeros, tp[:HEAD_DIM]], axis=0), jnp.where(top, 0.0, tp)]
    return jnp.concatenate(tiles, axis=1)


def _pair_block(xt, p, g):
    r0 = HEAD_DIM * g
    even = xt[r0:r0 + HEAD_DIM, (2 * p) * BLOCK:(2 * p + 1) * BLOCK]
    odd = xt[r0:r0 + HEAD_DIM, (2 * p + 1) * BLOCK:(2 * p + 2) * BLOCK]
    return jnp.concatenate([even, odd], axis=0).T


def _softmax_t(scores_t, bias, sink):
    st = scores_t + bias
    m = jnp.maximum(jnp.max(st, axis=0, keepdims=True), sink)
    e = jnp.exp(st - m)
    es = jnp.exp(sink - m)
    inv = 1.0 / (jnp.sum(e, axis=0, keepdims=True) + es)
    return e * inv, es * inv


def _dot(a, b):
    return jnp.dot(a, b, preferred_element_type=F32)


def _dot_nt(a, b):
    return lax.dot_general(a, b, (((1,), (1,)), ((), ())), preferred_element_type=F32)


def _layer_norm_fwd(v):
    mu = jnp.mean(v, axis=-1, keepdims=True)
    xc = v - mu
    rstd = lax.rsqrt(jnp.mean(xc * xc, axis=-1, keepdims=True) + EPS)
    return xc * rstd, rstd


def _tril(transposed=False):
    t = lax.broadcasted_iota(jnp.int32, (BLOCK, BLOCK), 0)
    s = lax.broadcasted_iota(jnp.int32, (BLOCK, BLOCK), 1)
    return s >= t if transposed else t >= s


def _const_spec(shape):
    return pl.BlockSpec(shape, lambda i: (0,) * len(shape))


def _keys_values(z_ref, kvp):
    kvc = z_ref[:, SEG_KV:SEG_KV + 2 * D_KV]
    kk = jnp.concatenate([kvp[:, :D_KV], kvc[:, :D_KV]], axis=0)
    vv = jnp.concatenate([kvp[:, D_KV:], kvc[:, D_KV:]], axis=0)
    return kk, vv


MIXER_BLOCKS = 2


class _Rows:
    def __init__(self, ref, sub):
        self.ref, self.rows = ref, slice(sub * BLOCK, (sub + 1) * BLOCK)

    def __getitem__(self, idx):
        return self.ref[self.rows, idx[1]]

    def __setitem__(self, idx, value):
        self.ref[self.rows, idx[1]] = value


def _kv_before_spec(index):
    return pl.BlockSpec((BLOCK, 2 * D_KV),
                        lambda i: (jnp.maximum(MIXER_BLOCKS * index(i) - 1, 0), SEG_KV // (2 * D_KV)))


def _pair_cols(g, p, base=0):
    return slice(base + (4 * g + p) * 128, base + (4 * g + p + 1) * 128)


def _mixer_fwd(z, sink_rows, ln_g, ln_b, sgu_w, sgu_bt):
    t = z.shape[0]

    def body(z_all, kvp_ref, sink_ref, lng_ref, lnb_ref, w_ref, bt_ref, a_all, prob_ref, sink_prob_ref):
        kv_before = kvp_ref[...]
        for sub in range(MIXER_BLOCKS):
            z_ref, a_ref = _Rows(z_all, sub), _Rows(a_all, sub)
            one_block(z_ref, kv_before, MIXER_BLOCKS * pl.program_id(0) + sub, sink_ref, lng_ref, lnb_ref, w_ref,
                      bt_ref, a_ref, prob_ref.at[sub], sink_prob_ref.at[sub])
            kv_before = z_ref[:, SEG_KV:SEG_KV + 2 * D_KV]

    def one_block(z_ref, kv_before, block_index, sink_ref, lng_ref, lnb_ref, w_ref, bt_ref, a_ref, prob_ref,
                  sink_prob_ref):
        bias = _window_bias(block_index)
        kk, vv = _keys_values(z_ref, kv_before)
        kk_b, vvt_b = kk.astype(BF16), vv.T.astype(BF16)
        for g in range(2):
            qt = _heads_t([z_ref[:, _pair_cols(g, p, SEG_Q)] * ATTN_SCALE for p in range(4)], g).astype(BF16)
            prob, sink_prob = _softmax_t(_dot(kk_b, qt), bias, sink_ref[g])
            prob_b = prob.astype(BF16)
            prob_ref[g] = prob_b
            sink_prob_ref[g] = sink_prob
            ot = _dot(vvt_b, prob_b)
            for p in range(4):
                gate = z_ref[:, _pair_cols(g, p, SEG_GA)]
                a_ref[:, _pair_cols(g, p)] = (_pair_block(ot, p, g) * (gate * _sigmoid(gate))).astype(BF16)

        vhat, _ = _layer_norm_fwd(z_ref[:, SEG_VS:SEG_VS + D_SGU])
        vn = vhat * lng_ref[...] + lnb_ref[...]
        tril = _tril()
        for g in range(SGU_GROUPS):
            cols = slice(g * 128, (g + 1) * 128)
            wm = jnp.where(tril, w_ref[g], 0.0).astype(BF16)
            mixed = _dot(wm, vn[:, cols].astype(BF16)) + bt_ref[:, g:g + 1]
            gate = z_ref[:, SEG_GS + g * 128:SEG_GS + (g + 1) * 128]
            a_ref[:, D_ATTN + g * 128:D_ATTN + (g + 1) * 128] = (
                (z_ref[:, SEG_U + g * 128:SEG_U + (g + 1) * 128] * mixed) * (gate * _sigmoid(gate))).astype(BF16)

    rows = MIXER_BLOCKS * BLOCK
    return pl.pallas_call(
        body, name="mixer_fwd", grid=(t // rows,),
        in_specs=[pl.BlockSpec((rows, D_IN), lambda i: (i, 0)), _kv_before_spec(lambda i: i),
                  _const_spec((2, 1, 8 * BLOCK)), _const_spec((1, D_SGU)), _const_spec((1, D_SGU)),
                  _const_spec((SGU_GROUPS, BLOCK, BLOCK)), _const_spec((BLOCK, SGU_GROUPS))],
        out_specs=(pl.BlockSpec((rows, D_MODEL), lambda i: (i, 0)),
                   pl.BlockSpec((MIXER_BLOCKS, 2, 2 * BLOCK, 8 * BLOCK), lambda i: (i, 0, 0, 0)),
                   pl.BlockSpec((MIXER_BLOCKS, 2, 1, 8 * BLOCK), lambda i: (i, 0, 0, 0))),
        out_shape=(jax.ShapeDtypeStruct((t, D_MODEL), BF16),
                   jax.ShapeDtypeStruct((t // BLOCK, 2, 2 * BLOCK, 8 * BLOCK), BF16),
                   jax.ShapeDtypeStruct((t // BLOCK, 2, 1, 8 * BLOCK), F32)),
        compiler_params=_params(dimension_semantics=("arbitrary",)),
    )(z, z, sink_rows, ln_g, ln_b, sgu_w, sgu_bt)


def _mixer_bwd(z, da, probs, sink_probs, ln_g, ln_b, sgu_w, sgu_wt, sgu_bt):
    t = z.shape[0]

    def body(z_all, kvp_ref, da_all, prob_ref, sink_prob_ref, lng_ref, lnb_ref, w_ref, wt_ref, bt_ref,
             dz_all, dsink_ref, dw_ref, db_ref, dlng_ref, dlnb_ref, carry_ref, dsink_acc, dbt_acc):
        step = pl.program_id(0)

        @pl.when(step == 0)
        def _():
            carry_ref[...] = jnp.zeros_like(carry_ref)
            dsink_acc[...] = jnp.zeros_like(dsink_acc)
            dbt_acc[...] = jnp.zeros_like(dbt_acc)
            dw_ref[...] = jnp.zeros_like(dw_ref)
            dlng_ref[...] = jnp.zeros_like(dlng_ref)
            dlnb_ref[...] = jnp.zeros_like(dlnb_ref)

        carry = carry_ref[...]
        for sub in reversed(range(MIXER_BLOCKS)):
            kv_before = kvp_ref[...] if sub == 0 else _Rows(z_all, sub - 1)[:, SEG_KV:SEG_KV + 2 * D_KV]
            carry = one_block(_Rows(z_all, sub), kv_before, _Rows(da_all, sub), prob_ref.at[sub], sink_prob_ref.at[sub],
                              carry, lng_ref, lnb_ref, w_ref, wt_ref, bt_ref, _Rows(dz_all, sub),
                              dw_ref, dlng_ref, dlnb_ref, dsink_acc, dbt_acc)
        carry_ref[...] = carry

        @pl.when(step == ns - 1)
        def _():
            db_ref[...] = dbt_acc[...].T[:SGU_GROUPS]
            lane_row = lax.broadcasted_iota(jnp.int32, (1, 128), 1)
            d_sink = jnp.zeros((1, 128), F32)
            for g in range(2):
                acc = dsink_acc[g]
                for j in range(8):
                    head_sum = jnp.sum(acc[:, j * BLOCK:(j + 1) * BLOCK], axis=-1, keepdims=True)
                    d_sink = d_sink + jnp.where(lane_row == 8 * g + j, head_sum, 0.0)
            dsink_ref[...] = d_sink

    def one_block(z_ref, kv_before, da_ref, prob_ref, sink_prob_ref, carry, lng_ref, lnb_ref, w_ref, wt_ref, bt_ref,
                  dz_ref, dw_ref, dlng_ref, dlnb_ref, dsink_acc, dbt_acc):
        kk, vv = _keys_values(z_ref, kv_before)
        vv_b = vv.astype(BF16)
        kkt_b, vvt_b = kk.T.astype(BF16), vv.T.astype(BF16)
        dkk = jnp.zeros((2 * BLOCK, D_KV), F32)
        dvv = jnp.zeros((2 * BLOCK, D_KV), F32)
        for g in range(2):
            qt = _heads_t([z_ref[:, _pair_cols(g, p, SEG_Q)] * ATTN_SCALE for p in range(4)], g).astype(BF16)
            prob_b, sink_prob = prob_ref[g], sink_prob_ref[g]
            prob = prob_b.astype(F32)
            ot = _dot(vvt_b, prob_b)
            gates = [z_ref[:, _pair_cols(g, p, SEG_GA)] for p in range(4)]
            sig = [_sigmoid(gt) for gt in gates]
            d_attn = [da_ref[:, _pair_cols(g, p)] for p in range(4)]
            d_ot = _heads_t([d_attn[p] * (gates[p] * sig[p]) for p in range(4)], g).astype(BF16)
            d_prob = _dot(vv_b, d_ot)
            delta = jnp.sum(prob * d_prob, axis=0, keepdims=True)
            d_scores = (prob * (d_prob - delta)).astype(BF16)
            dsink_acc[g] -= sink_prob * delta
            d_qt = _dot(kkt_b, d_scores)
            dkk = dkk + _dot_nt(d_scores, qt)
            dvv = dvv + _dot_nt(prob_b, d_ot)
            for p in range(4):
                dz_ref[:, _pair_cols(g, p, SEG_Q)] = (_pair_block(d_qt, p, g) * ATTN_SCALE).astype(BF16)
                d_silu = sig[p] * (1.0 + gates[p] * (1.0 - sig[p]))
                dz_ref[:, _pair_cols(g, p, SEG_GA)] = (d_attn[p] * _pair_block(ot, p, g) * d_silu).astype(BF16)
        d_kv = jnp.concatenate([dkk, dvv], axis=1)
        dz_ref[:, SEG_KV:SEG_KV + 2 * D_KV] = (d_kv[BLOCK:] + carry).astype(BF16)

        vhat, rstd = _layer_norm_fwd(z_ref[:, SEG_VS:SEG_VS + D_SGU])
        lng = lng_ref[...]
        vn = vhat * lng + lnb_ref[...]
        tril, triu = _tril(), _tril(transposed=True)
        lane = lax.broadcasted_iota(jnp.int32, (BLOCK, 128), 1)
        d_bt = jnp.zeros((BLOCK, 128), F32)
        d_vn = []
        for g in range(SGU_GROUPS):
            cols = slice(g * 128, (g + 1) * 128)
            wm = jnp.where(tril, w_ref[g], 0.0).astype(BF16)
            wmt = jnp.where(triu, wt_ref[g], 0.0).astype(BF16)
            vn_g = vn[:, cols].astype(BF16)
            mixed = _dot(wm, vn_g) + bt_ref[:, g:g + 1]
            gate = z_ref[:, SEG_GS + g * 128:SEG_GS + (g + 1) * 128]
            u = z_ref[:, SEG_U + g * 128:SEG_U + (g + 1) * 128]
            d_out = da_ref[:, D_ATTN + g * 128:D_ATTN + (g + 1) * 128]
            sg = _sigmoid(gate)
            d_um = d_out * (gate * sg)
            dz_ref[:, SEG_U + g * 128:SEG_U + (g + 1) * 128] = (d_um * mixed).astype(BF16)
            dz_ref[:, SEG_GS + g * 128:SEG_GS + (g + 1) * 128] = (
                d_out * (u * mixed) * (sg * (1.0 + gate * (1.0 - sg)))).astype(BF16)
            d_mixed = d_um * u
            d_mixed_b = d_mixed.astype(BF16)
            dw_ref[g] += jnp.where(tril, _dot_nt(d_mixed_b, vn_g), 0.0)
            d_bt = d_bt + jnp.where(lane == g, jnp.sum(d_mixed, axis=-1, keepdims=True), 0.0)
            d_vn.append(_dot(wmt, d_mixed_b))
        dbt_acc[...] += d_bt
        d_vn = jnp.concatenate(d_vn, axis=1)
        dlng_ref[...] += jnp.sum(d_vn * vhat, axis=0, keepdims=True)
        dlnb_ref[...] += jnp.sum(d_vn, axis=0, keepdims=True)
        d_vhat = d_vn * lng
        d_v = rstd * (d_vhat - jnp.mean(d_vhat, axis=-1, keepdims=True)
                      - vhat * jnp.mean(d_vhat * vhat, axis=-1, keepdims=True))
        dz_ref[:, SEG_VS:SEG_VS + D_SGU] = d_v.astype(BF16)
        return d_kv[:BLOCK]

    rows = MIXER_BLOCKS * BLOCK
    ns = t // rows
    rev = lambda i: ns - 1 - i
    return pl.pallas_call(
        body, name="mixer_bwd", grid=(ns,),
        in_specs=[pl.BlockSpec((rows, D_IN), lambda i: (rev(i), 0)), _kv_before_spec(rev),
                  pl.BlockSpec((rows, D_MODEL), lambda i: (rev(i), 0)),
                  pl.BlockSpec((MIXER_BLOCKS, 2, 2 * BLOCK, 8 * BLOCK), lambda i: (rev(i), 0, 0, 0)),
                  pl.BlockSpec((MIXER_BLOCKS, 2, 1, 8 * BLOCK), lambda i: (rev(i), 0, 0, 0)),
                  _const_spec((1, D_SGU)), _const_spec((1, D_SGU)),
                  _const_spec((SGU_GROUPS, BLOCK, BLOCK)), _const_spec((SGU_GROUPS, BLOCK, BLOCK)),
                  _const_spec((BLOCK, SGU_GROUPS))],
        out_specs=(pl.BlockSpec((rows, D_IN), lambda i: (rev(i), 0)), _const_spec((1, 128)),
                   _const_spec((SGU_GROUPS, BLOCK, BLOCK)), _const_spec((SGU_GROUPS, BLOCK)),
                   _const_spec((1, D_SGU)), _const_spec((1, D_SGU))),
        out_shape=(jax.ShapeDtypeStruct((t, D_IN), BF16), jax.ShapeDtypeStruct((1, 128), F32),
                   jax.ShapeDtypeStruct((SGU_GROUPS, BLOCK, BLOCK), F32), jax.ShapeDtypeStruct((SGU_GROUPS, BLOCK), F32),
                   jax.ShapeDtypeStruct((1, D_SGU), F32), jax.ShapeDtypeStruct((1, D_SGU), F32)),
        scratch_shapes=[pltpu.VMEM((BLOCK, 2 * D_KV), F32), pltpu.VMEM((2, 1, 8 * BLOCK), F32),
                        pltpu.VMEM((BLOCK, 128), F32)],
        compiler_params=_params(dimension_semantics=("arbitrary",)),
    )(z, z, da, probs, sink_probs, ln_g, ln_b, sgu_w, sgu_wt, sgu_bt)


def _out_proj_head(a, w_out_full, x, target, mod, final_g, tm=256):
    t, d = x.shape

    def body(a_ref, w_ref, x_ref, tg_ref, gate_ref, fg_ref, dx2_ref, dy_ref, loss_ref, dfg_ref, dgate_ref):
        @pl.when(pl.program_id(0) == 0)
        def _():
            loss_ref[...] = jnp.zeros_like(loss_ref)
            dfg_ref[...] = jnp.zeros_like(dfg_ref)
            dgate_ref[...] = jnp.zeros_like(dgate_ref)

        yv, gate, fg = _dot(a_ref[...], w_ref[...]), gate_ref[...], fg_ref[...]
        x2 = x_ref[...] + gate * yv
        r2 = lax.rsqrt(jnp.mean(x2 * x2, axis=-1, keepdims=True) + EPS)
        nrm = x2 * r2
        err = nrm * fg - tg_ref[...]
        loss_ref[...] += 0.5 * jnp.sum(jnp.mean(err * err, axis=-1, keepdims=True), axis=0, keepdims=True)
        fg_d = fg * (1.0 / d)
        err_nrm = err * nrm
        dfg_ref[...] += jnp.sum(err_nrm, axis=0, keepdims=True) * (1.0 / d)
        d_nrm = err * fg_d
        dx2 = r2 * (d_nrm - nrm * jnp.mean(err_nrm * fg_d, axis=-1, keepdims=True))
        dx2_ref[...] = dx2
        dgate_ref[...] += jnp.sum(dx2 * yv, axis=0, keepdims=True)
        dy_ref[...] = (dx2 * gate).astype(BF16)

    blk = pl.BlockSpec((tm, d), lambda i: (i, 0))
    row = _const_spec((1, d))
    whole = pl.BlockSpec(w_out_full.shape, lambda i: (0, 0), pipeline_mode=pl.Buffered(1))
    return pl.pallas_call(
        body, name="out_proj_head", grid=(t // tm,),
        in_specs=[pl.BlockSpec((tm, a.shape[1]), lambda i: (i, 0)), whole, blk, blk, _mod_spec(MOD_GATE, d), row],
        out_specs=(blk, blk, _const_spec((1, 128)), row, row),
        out_shape=(jax.ShapeDtypeStruct((t, d), F32), jax.ShapeDtypeStruct((t, d), BF16),
                   jax.ShapeDtypeStruct((1, 128), F32), jax.ShapeDtypeStruct((1, d), F32),
                   jax.ShapeDtypeStruct((1, d), F32)),
        compiler_params=_params(dimension_semantics=("arbitrary",)),
    )(a, w_out_full, x, target, mod, final_g)


def _z_proj_bwd_norm(dz, w_in_t, x, dx2, norm_g, mod, dep, tm=256):
    t, d = x.shape

    def body(dz_ref, w_ref, x_ref, dx2_ref, g_ref, sc_ref, dep_ref, gx_ref, dshift_ref, dscale_ref, dg_ref):
        @pl.when(pl.program_id(0) == 0)
        def _():
            dshift_ref[...] = jnp.zeros_like(dshift_ref)
            dscale_ref[...] = jnp.zeros_like(dscale_ref)
            dg_ref[...] = jnp.zeros_like(dg_ref)

        dh, xv, g = _dot(dz_ref[...], w_ref[...]), x_ref[...], g_ref[...]
        one_plus = 1.0 + sc_ref[...]
        r = lax.rsqrt(jnp.mean(xv * xv, axis=-1, keepdims=True) + EPS)
        xn = xv * r
        gain = one_plus * g
        dh_xn = dh * xn
        dh_xn_sum = jnp.sum(dh_xn, axis=0, keepdims=True)
        dshift_ref[...] += jnp.sum(dh, axis=0, keepdims=True)
        dscale_ref[...] += dh_xn_sum * g
        dg_ref[...] += dh_xn_sum * one_plus
        d_xn = dh * gain
        gx_ref[...] = dx2_ref[...] + r * (d_xn - xn * jnp.mean(dh_xn * gain, axis=-1, keepdims=True))

    blk = pl.BlockSpec((tm, d), lambda i: (i, 0))
    row = _const_spec((1, d))
    whole = pl.BlockSpec(w_in_t.shape, lambda i: (0, 0), pipeline_mode=pl.Buffered(1))
    return pl.pallas_call(
        body, name="z_proj_bwd_norm", grid=(t // tm,),
        in_specs=[pl.BlockSpec((tm, dz.shape[1]), lambda i: (i, 0)), whole, blk, blk, row, _mod_spec(MOD_SCALE, d),
                  _const_spec((8, 128))],
        out_specs=(blk, row, row, row),
        out_shape=(jax.ShapeDtypeStruct((t, d), F32),) + (jax.ShapeDtypeStruct((1, d), F32),) * 3,
        compiler_params=_params(dimension_semantics=("arbitrary",)),
    )(dz, w_in_t, x, dx2, norm_g, mod, dep)


def _adamw(w, g, m, v):
    m = ADAM_B1 * m + (1.0 - ADAM_B1) * g
    v = ADAM_B2 * v + (1.0 - ADAM_B2) * (g * g)
    m_hat = m / (1.0 - ADAM_B1 ** ADAM_STEP)
    v_hat = v / (1.0 - ADAM_B2 ** ADAM_STEP)
    delta = -ADAM_LR * (m_hat / (jnp.sqrt(v_hat) + ADAM_EPS) + ADAM_WD * w)
    return delta, m, v


def _relay_sum(device, blocks, land_pair, land_first, tr):
    _, r, c = blocks.shape

    def body(device_ref, a_ref, b_ref, c_ref, o_ref):
        o_ref[...] = (a_ref[...].astype(F32) + b_ref[...].astype(F32) + c_ref[...].astype(F32)).astype(BF16)

    second = pl.BlockSpec((None, tr, c), lambda i, device_ref: (1, i, 0))
    return pl.pallas_call(
        body, name="w_in_grad_relay_sum",
        grid_spec=pltpu.PrefetchScalarGridSpec(
            num_scalar_prefetch=1, grid=(r // tr,),
            in_specs=[pl.BlockSpec((None, tr, c), lambda i, device_ref: (device_ref[0], i, 0)), second, second],
            out_specs=pl.BlockSpec((tr, c), lambda i, device_ref: (i, 0))),
        out_shape=jax.ShapeDtypeStruct((r, c), BF16),
        compiler_params=_params(dimension_semantics=("arbitrary",)),
    )(device, blocks, land_pair, land_first)


def _adam_from_chips(chip, pair, landed, w, m, v, name, tc):
    _, r, c = pair.shape
    n = len(landed)

    def body(chip_ref, own_ref, *refs):
        w_ref, m_ref, v_ref, g_ref, d_ref, nm_ref, nv_ref = refs[n:]
        g = own_ref[...].astype(F32)
        for k in range(n):
            g = g + refs[k][...].astype(F32)
        g_ref[...] = g
        d_ref[...], nm_ref[...], nv_ref[...] = _adamw(w_ref[...], g, m_ref[...], v_ref[...])

    def landed_spec(index):
        return pl.BlockSpec((None, r, tc), lambda i, chip_ref: (index, 0, i))

    blk = pl.BlockSpec((r, tc), lambda i, chip_ref: (0, i))
    return pl.pallas_call(
        body, name=name,
        grid_spec=pltpu.PrefetchScalarGridSpec(
            num_scalar_prefetch=1, grid=(c // tc,),
            in_specs=[pl.BlockSpec((None, r, tc), lambda i, chip_ref: (chip_ref[0], 0, i))]
            + [landed_spec(index) for _, index in landed] + [blk, blk, blk],
            out_specs=(blk,) * 4),
        out_shape=(jax.ShapeDtypeStruct((r, c), F32),) * 4,
        compiler_params=_params(dimension_semantics=("arbitrary",)),
    )(chip, pair, *[array for array, _ in landed], w, m, v)


def _adam_w_ada(device, act_t, dmod_all, w, m, v, tr=512):
    r, c = w.shape

    def body(device_ref, a_ref, dm_ref, w_ref, m_ref, v_ref, g_ref, d_ref, nm_ref, nv_ref):
        g = _dot(a_ref[...].astype(BF16), dm_ref[...].astype(BF16))
        g_ref[...] = g
        d_ref[...], nm_ref[...], nv_ref[...] = _adamw(w_ref[...], g, m_ref[...], v_ref[...])

    blk = pl.BlockSpec((tr, c), lambda i, device_ref: (i, 0))
    return pl.pallas_call(
        body, name="adam_w_ada",
        grid_spec=pltpu.PrefetchScalarGridSpec(
            num_scalar_prefetch=1, grid=(r // tr,),
            in_specs=[pl.BlockSpec((tr, N_DEV), lambda i, device_ref: (i, 0)),
                      pl.BlockSpec((N_DEV, c), lambda i, device_ref: (0, device_ref[0])), blk, blk, blk],
            out_specs=(blk,) * 4),
        out_shape=(jax.ShapeDtypeStruct((r, c), F32),) * 4,
        compiler_params=_params(dimension_semantics=("arbitrary",)),
    )(device, act_t, dmod_all, w, m, v)


def _pack_small(d_shift, d_scale, d_gate, d_norm_g, d_final_g, d_ln_g, d_ln_b, loss, d_sinks, d_sgu_b):
    def body(shift_ref, scale_ref, gate_ref, ng_ref, fg_ref, lng_ref, lnb_ref, loss_ref, sink_ref, b_ref, o_ref):
        o_ref[...] = jnp.zeros_like(o_ref)
        o_ref[ROW_SHIFT:ROW_SHIFT + 1, :] = shift_ref[...]
        o_ref[ROW_SCALE:ROW_SCALE + 1, :] = scale_ref[...]
        o_ref[ROW_GATE:ROW_GATE + 1, :] = gate_ref[...]
        o_ref[ROW_NORM_G:ROW_NORM_G + 1, :] = ng_ref[...]
        o_ref[ROW_FINAL_G:ROW_FINAL_G + 1, :] = fg_ref[...]
        o_ref[ROW_LN:ROW_LN + 1, 0:D_SGU] = lng_ref[...]
        o_ref[ROW_LN:ROW_LN + 1, D_SGU:2 * D_SGU] = lnb_ref[...]
        o_ref[ROW_MISC:ROW_MISC + 1, 0:128] = loss_ref[...]
        o_ref[ROW_MISC:ROW_MISC + 1, 128:256] = sink_ref[...]
        o_ref[ROW_SGU_B:ROW_SGU_B + SGU_GROUPS, 0:BLOCK] = b_ref[...]

    return pl.pallas_call(
        body, name="pack_small", out_shape=jax.ShapeDtypeStruct((SMALL_ROWS, D_MODEL), F32),
        compiler_params=_params(),
    )(d_shift, d_scale, d_gate, d_norm_g, d_final_g, d_ln_g, d_ln_b, loss, d_sinks, d_sgu_b)


_SMALL_NAMES = ("norm_g", "b_ada", "attn_sinks", "sgu_ln_g", "sgu_ln_b", "sgu_w", "sgu_b", "final_g")


def _adam_small(partials, d_sgu_w_all, weights, moments_m, moments_v):
    names = _SMALL_NAMES
    k = len(names)

    def body(*refs):
        p_ref, sw_ref = refs[0], refs[1]
        w_refs, m_refs, v_refs = refs[2:2 + k], refs[2 + k:2 + 2 * k], refs[2 + 2 * k:2 + 3 * k]
        loss_ref, dmod_ref = refs[2 + 3 * k], refs[3 + 3 * k]
        out_refs = refs[4 + 3 * k:4 + 7 * k]
        sum_ref = refs[4 + 7 * k]
        total = p_ref[0]
        for j in range(1, N_DEV):
            total = total + p_ref[j]
        sum_ref[...] = total
        for j in range(N_DEV):
            for part, row in enumerate((ROW_SHIFT, ROW_SCALE, ROW_GATE)):
                dmod_ref[j:j + 1, part * D_MODEL:(part + 1) * D_MODEL] = p_ref[j, row:row + 1, :]
        loss_ref[...] = sum_ref[ROW_MISC:ROW_MISC + 1, 0:1]
        d_sgu_w = sw_ref[0]
        for j in range(1, N_DEV):
            d_sgu_w = d_sgu_w + sw_ref[j]
        grads = {
            "norm_g": sum_ref[ROW_NORM_G:ROW_NORM_G + 1, :],
            "b_ada": jnp.concatenate([sum_ref[r:r + 1, :] for r in (ROW_SHIFT, ROW_SCALE, ROW_GATE)], axis=1),
            "attn_sinks": sum_ref[ROW_MISC:ROW_MISC + 1, 128:128 + N_Q_HEADS],
            "sgu_ln_g": sum_ref[ROW_LN:ROW_LN + 1, 0:D_SGU],
            "sgu_ln_b": sum_ref[ROW_LN:ROW_LN + 1, D_SGU:2 * D_SGU],
            "sgu_w": d_sgu_w[None],
            "sgu_b": sum_ref[ROW_SGU_B:ROW_SGU_B + SGU_GROUPS, 0:BLOCK][None],
            "final_g": sum_ref[ROW_FINAL_G:ROW_FINAL_G + 1, :],
        }
        for i, name in enumerate(names):
            g = grads[name]
            delta, m, v = _adamw(w_refs[i][...], g, m_refs[i][...], v_refs[i][...])
            out_refs[4 * i][...] = g
            out_refs[4 * i + 1][...] = delta
            out_refs[4 * i + 2][...] = m
            out_refs[4 * i + 3][...] = v

    shapes = [jax.ShapeDtypeStruct((1, 1), F32), jax.ShapeDtypeStruct((N_DEV, 3 * D_MODEL), F32)]
    for name in names:
        shapes += [jax.ShapeDtypeStruct(weights[name].shape, F32)] * 4
    outs = pl.pallas_call(
        body, name="adam_small", out_shape=tuple(shapes),
        scratch_shapes=[pltpu.VMEM((SMALL_ROWS, D_MODEL), F32)],
        compiler_params=_params(),
    )(partials, d_sgu_w_all, *[weights[n] for n in names], *[moments_m[n] for n in names],
      *[moments_v[n] for n in names])
    return outs[0], outs[1], {name: outs[2 + 4 * i:6 + 4 * i] for i, name in enumerate(names)}


def kernel(x, c, norm_g, w_ada, b_ada, w_in, attn_sinks, sgu_ln_g, sgu_ln_b, sgu_w, sgu_b, w_out, final_g, loss_target, m_norm_g, m_w_ada, m_b_ada, m_w_in, m_attn_sinks, m_sgu_ln_g, m_sgu_ln_b, m_sgu_w, m_sgu_b, m_w_out, m_final_g, v_norm_g, v_w_ada, v_b_ada, v_w_in, v_attn_sinks, v_sgu_ln_g, v_sgu_ln_b, v_sgu_w, v_sgu_b, v_w_out, v_final_g):
    xi, yi, ci = _place()
    me = 4 * xi + 2 * yi + ci
    x2d, target = x[0], loss_target[0]
    t = x2d.shape[0]

    core = ci.astype(jnp.int32).reshape(1)
    chip = (2 * xi + yi).astype(jnp.int32).reshape(1)

    first = _own_block_copies(_first_targets)
    first_flight = _start_copies([_with_own_slot(w_in[0].T.astype(BF16), me)], first, 2, core, "gather_w_in_start")

    c_all = _all_gather_small(c.reshape(8, 256) + first_flight[3][0, 0], "gather_c").reshape(N_DEV, D_MODEL)
    device = me.astype(jnp.int32).reshape(1)
    c_act, mod_part = _modulation(device, c_all, w_ada[0], b_ada)
    mod_all = _all_gather_small(mod_part, "gather_mod")

    across = _wait_then_start(first_flight, lambda *a: first(*a)[1:], _second_axis_stage_copies, 3, mod_all,
                              "gather_w_in_second_axis_stage")
    mod = lax.dynamic_index_in_dim(mod_all, me, axis=1, keepdims=False).reshape(1, 3 * D_MODEL)
    mod = mod + across[3][0, 0]
    h = _modulated_norm(x2d, norm_g, mod)

    w_in_pair = _wait_copies((first_flight[0], first_flight[1], across[2], None), lambda *a: first(*a)[:1], h,
                             "gather_w_in_sibling_wait")
    z_own = _z_proj(h, w_in_pair[0].reshape(D_IN, D_MODEL), chip, 0, 1, None, "z_proj_own")
    w_out_early = _own_block_copies(lambda x, y, c: [(x, y, 1 - c), (*_second_axis_chip(x, y, c), c)])
    w_out_late = _own_block_copies(lambda x, y, c: [(*_first_axis_chip(x, y, c), c), (1 - x, 1 - y, c)])
    forward = _wait_then_start(
        (across[0], across[1], w_in_pair, None), lambda *a: _second_axis_stage_copies(*a)[:1],
        lambda refs, s, r: _second_axis_forward_copies(refs[:1], s, r) + _group(w_out_early, 1, 1, 1)(refs, s, r),
        3, z_own, "gather_w_in_second_axis_forward", more_bufs=[_with_own_slot(w_out[0].astype(BF16), me)])
    w_in_most = _wait_copies((across[0], across[1], forward[2][:1], None),
                             lambda *a: _second_axis_stage_copies(*a)[1:2], z_own, "gather_w_in_first_forward_wait")
    w_in_most = _wait_copies((forward[0], forward[1], w_in_most, None), _second_axis_forward_copies, z_own,
                             "gather_w_in_second_forward_wait")
    z_early = _z_proj(h, w_in_most[0].reshape(D_IN, D_MODEL), chip, 1, _Z_EARLY_TILES - 1, z_own, "z_proj_early")
    last = _wait_then_start(
        (across[0], across[1], [w_in_most[0], forward[2][1]], None), lambda *a: _second_axis_stage_copies(*a)[2:],
        lambda refs, s, r: _diagonal_forward_copies(refs[:1], s, r) + _group(w_out_late, 1, 1, 1)(refs, s, r),
        3, z_early, "gather_w_in_last_stage")
    w_in_all = _wait_copies((last[0], last[1], last[2][:1], None), _diagonal_forward_copies, z_early,
                            "gather_w_in_last_wait")[0]
    w_in_t = w_in_all.reshape(D_IN, D_MODEL)
    z = _z_proj(h, w_in_t, chip, _Z_EARLY_TILES, 7 - _Z_EARLY_TILES, z_early, "z_proj_late")
    w_out_half = _wait_copies((forward[0], forward[1], last[2][1:], None), _group(w_out_early, 0, 1, 1), z,
                              "gather_w_out_early_wait")
    w_out_flight = _wait_then_start((last[0], last[1], w_out_half, None), _group(w_out_late, 0, 1, 1),
                                    _forward_copies, 3, z, "gather_w_out_forward_stage")
    sink_rows = jnp.repeat(attn_sinks.reshape(N_Q_HEADS), BLOCK).reshape(2, 1, 8 * BLOCK)
    sgu_bt = sgu_b[0].T
    a, probs, sink_probs = _mixer_fwd(z, sink_rows + w_out_flight[3][0, 0], sgu_ln_g, sgu_ln_b, sgu_w[0], sgu_bt)
    w_out_all = _wait_copies(w_out_flight, _forward_copies, a, "gather_w_out_forward_wait")[0]
    w_out_full = w_out_all.reshape(D_MODEL, D_MODEL)
    final_g_row = final_g.reshape(1, D_MODEL)
    dx2, dy, loss_part, d_final_g, d_gate = _out_proj_head(a, w_out_full, x2d, target, mod, final_g_row)

    da = _matmul(dy, w_out_full, "nt", F32, min(t, 1024), 1024, "out_proj_bwd")
    dw_out = _matmul(a, dy, "tn", BF16, 1024, 1024, "w_out_grad").reshape(4, 2, W_OUT_SHARD, D_MODEL)
    pair_out = _pair_reduce(dw_out, _every_chip, "w_out_grad_pair_reduce", W_OUT_SHARD // 2)
    dz, d_sinks, d_sgu_w, d_sgu_b, d_ln_g, d_ln_b = _mixer_bwd(
        z, da, probs, sink_probs, sgu_ln_g, sgu_ln_b, sgu_w[0], jnp.swapaxes(sgu_w[0], 1, 2), sgu_bt)
    sgu_w_to_all = _group(_own_block_copies(_all_others), 2, 1, 3)
    both = _start_copies(
        [pair_out, lax.empty((3, W_OUT_SHARD, D_MODEL), BF16), _with_own_slot(d_sgu_w, me)],
        lambda refs, s, r: _chip_copies(refs[:2], s, r) + sgu_w_to_all(refs, s, r), 3 + N_DEV - 1, core,
        "w_out_grad_chip_and_sgu_w_gather_start")
    out_flight, sgu_w_flight = (both[0], both[1], both[2][:2], None), (both[0], both[1], both[2][2:], None)
    dw_in_t = _matmul(dz, h, "tn", BF16, 768, D_MODEL, "w_in_grad", dep=both[3])
    pair_in = _pair_reduce(dw_in_t.reshape(4, 2, W_IN_SHARD, D_MODEL), _first_hop_chips, "w_in_grad_pair_reduce",
                           W_IN_SHARD // 3)
    first_hop = lambda refs, s, r: _first_hop_copies(refs[:2], s, r) + _group(_late_pair_copies, 2, 2, 2)(refs, s, r)
    hop1 = _start_copies(
        [pair_in, lax.empty((2, W_IN_SHARD, D_MODEL), BF16), dw_in_t.reshape(N_DEV, W_IN_SHARD, D_MODEL),
         lax.empty((2, W_IN_SHARD, D_MODEL), BF16)], first_hop, 4, core, "w_in_grad_first_hop_start")
    grad_x, d_shift, d_scale, d_norm_g = _z_proj_bwd_norm(dz, w_in_t, x2d, dx2, norm_g, mod, hop1[3])

    partial = _pack_small(d_shift, d_scale, d_gate, d_norm_g, d_final_g, d_ln_g, d_ln_b, loss_part, d_sinks, d_sgu_b)
    small_flight = _start_copies([_with_own_slot(partial, me)], _own_block_copies(_all_others), N_DEV - 1, core,
                                 "small_grad_gather_start")
    _, land_first, dw_in_t, land_pair = _wait_copies(hop1, first_hop, small_flight[3], "w_in_grad_first_hop_wait")
    second_device = (4 * ((xi + ci) % 2) + 2 * ((yi + 1 - ci) % 2) + ci).astype(jnp.int32).reshape(1)
    relay = _relay_sum(second_device, dw_in_t, land_pair, land_first, W_IN_SHARD // 3)
    hop2 = _start_copies([relay, lax.empty((1, W_IN_SHARD, D_MODEL), BF16)], _second_hop_copies, 1, core,
                         "w_in_grad_second_hop_start")
    pair_out, land_out = _wait_copies(out_flight, _chip_copies, hop2[3], "w_out_grad_chip_wait")
    big = {"w_out": _adam_from_chips(chip, pair_out, [(land_out, k) for k in range(3)], w_out[0], m_w_out[0],
                                     v_w_out[0], "adam_w_out", 1024)}
    partial_all = _wait_copies(small_flight, _own_block_copies(_all_others), big["w_out"][0],
                               "small_grad_gather_wait")[0]
    d_sgu_w_all = _wait_copies(sgu_w_flight, _group(_own_block_copies(_all_others), 0, 1, 3), partial_all,
                               "sgu_w_grad_gather_wait")[0]
    weights = {"norm_g": norm_g, "b_ada": b_ada, "attn_sinks": attn_sinks, "sgu_ln_g": sgu_ln_g,
               "sgu_ln_b": sgu_ln_b, "sgu_w": sgu_w, "sgu_b": sgu_b, "final_g": final_g_row}
    moments_m = {"norm_g": m_norm_g, "b_ada": m_b_ada, "attn_sinks": m_attn_sinks, "sgu_ln_g": m_sgu_ln_g,
                 "sgu_ln_b": m_sgu_ln_b, "sgu_w": m_sgu_w, "sgu_b": m_sgu_b,
                 "final_g": m_final_g.reshape(1, D_MODEL)}
    moments_v = {"norm_g": v_norm_g, "b_ada": v_b_ada, "attn_sinks": v_attn_sinks, "sgu_ln_g": v_sgu_ln_g,
                 "sgu_ln_b": v_sgu_ln_b, "sgu_w": v_sgu_w, "sgu_b": v_sgu_b,
                 "final_g": v_final_g.reshape(1, D_MODEL)}
    loss, dmod_all, small = _adam_small(partial_all, d_sgu_w_all, weights, moments_m, moments_v)
    small["final_g"] = tuple(o.reshape(D_MODEL) for o in small["final_g"])

    big["w_ada"] = _adam_w_ada(device, c_act.T, dmod_all, w_ada[0], m_w_ada[0], v_w_ada[0])
    _, land_second = _wait_copies(hop2, _second_hop_copies, big["w_ada"][0], "w_in_grad_second_hop_wait")
    big["w_in"] = tuple(o.T for o in _adam_from_chips(
        device, dw_in_t, [(land_pair, 0), (land_first, 0), (land_second, 0)], w_in[0].T, m_w_in[0].T, v_w_in[0].T,
        "adam_w_in", 512))
    order = ["norm_g", "w_ada", "b_ada", "w_in", "attn_sinks", "sgu_ln_g", "sgu_ln_b", "sgu_w", "sgu_b", "w_out",
             "final_g"]
    outs = [loss.reshape(()), grad_x[None]]
    for k in range(4):
        for name in order:
            outs.append(big[name][k][None] if name in big else small[name][k])
    return tuple(outs)
```

```python
import jax
import jax.numpy as jnp
from jax import lax
from jax.experimental import pallas as pl
from jax.experimental.pallas import tpu as pltpu

F32 = jnp.float32
BF16 = jnp.bfloat16
MESH = pl.DeviceIdType.MESH

N_DEV = 8
D_MODEL = 2048
HEAD_DIM = 64
D_ATTN = 1024
N_Q_HEADS = 16
D_KV = 128
BLOCK = 128
D_SGU = 1024
SGU_GROUPS = 8
D_IN = 5376
W_IN_SHARD = D_IN // N_DEV
W_OUT_SHARD = D_MODEL // N_DEV
W_ADA_SHARD = 3 * D_MODEL // N_DEV
EPS = 1e-6
ATTN_SCALE = 0.125

ADAM_LR = 0.001
ADAM_B1 = 0.9
ADAM_B2 = 0.999
ADAM_EPS = 1e-08
ADAM_WD = 0.01
ADAM_STEP = 10

SEG_Q, SEG_KV, SEG_GA, SEG_U, SEG_VS, SEG_GS = 0, 1024, 1280, 2304, 3328, 4352

VMEM_LIMIT = 56 * 1024 * 1024

ROW_SHIFT, ROW_SCALE, ROW_GATE, ROW_NORM_G, ROW_FINAL_G, ROW_LN, ROW_MISC, ROW_SGU_B = 0, 1, 2, 3, 4, 5, 6, 8
SMALL_ROWS = 16


def _params(**kw):
    return pltpu.CompilerParams(vmem_limit_bytes=VMEM_LIMIT, **kw)


def _sigmoid(x):
    return 0.5 * (jnp.tanh(0.5 * x) + 1.0)


def _place():
    return lax.axis_index("x"), lax.axis_index("y"), lax.axis_index("c")


def _every_chip(x, y, c):
    return [0, 1, 2, 3]


def _first_hop_chips(x, y, c):
    first = _first_axis_chip(x, y, c)
    return [2 * first[0] + first[1], 2 * (1 - x) + (1 - y)]


def _pair_reduce(blocks, chips, name, row_chunk):
    _, _, r, cols = blocks.shape
    n = len(chips(0, 0, 0))
    assert r % row_chunk == 0

    def body(in_ref, out_ref, land, own, summed, send_sems, recv_sems, own_sems, out_sems):
        x, y, c = _place()
        sends, loads, stores = [], [], []
        for m in range(n):
            cp = pltpu.make_async_remote_copy(
                src_ref=in_ref.at[chips(x, y, 1 - c)[m], 1 - c], dst_ref=land.at[m], send_sem=send_sems.at[m],
                recv_sem=recv_sems.at[m], device_id=(x, y, 1 - c), device_id_type=MESH)
            cp.start()
            sends.append(cp)
            ld = pltpu.make_async_copy(in_ref.at[chips(x, y, c)[m], c], own.at[m], own_sems.at[m])
            ld.start()
            loads.append(ld)
        for m in range(n):
            sends[m].wait_recv()
            loads[m].wait()
            for k in range(r // row_chunk):
                rows = slice(k * row_chunk, (k + 1) * row_chunk)
                summed[m, rows, :] = (own[m, rows, :].astype(F32) + land[m, rows, :].astype(F32)).astype(BF16)
            st = pltpu.make_async_copy(summed.at[m], out_ref.at[m], out_sems.at[m])
            st.start()
            stores.append(st)
        for m in range(n):
            sends[m].wait_send()
            stores[m].wait()

    spec = pl.BlockSpec(memory_space=pl.ANY)
    return pl.pallas_call(
        body, name=name, out_shape=jax.ShapeDtypeStruct((n, r, cols), BF16),
        in_specs=[spec], out_specs=spec,
        scratch_shapes=[pltpu.VMEM((n, r, cols), BF16), pltpu.VMEM((n, r, cols), BF16), pltpu.VMEM((n, r, cols), BF16),
                        pltpu.SemaphoreType.DMA((n,)), pltpu.SemaphoreType.DMA((n,)), pltpu.SemaphoreType.DMA((n,)),
                        pltpu.SemaphoreType.DMA((n,))],
        compiler_params=_params(),
    )(blocks)


_HBM = pl.BlockSpec(memory_space=pltpu.HBM)
_SEM = pl.BlockSpec(memory_space=pltpu.SEMAPHORE)
_EFFECT = pltpu.SideEffectType.DATAFLOW_SIDE_EFFECTING


def _start_copies(bufs, copies, n_copies, after, name):
    nb = len(bufs)

    def body(*refs):
        for cp in copies(refs[:nb], refs[nb + 1], refs[nb + 2]):
            cp.start()
        refs[-1][...] = jnp.zeros_like(refs[-1])

    out = pl.pallas_call(
        body, name=name,
        out_shape=(pltpu.SemaphoreType.DMA((n_copies,)), pltpu.SemaphoreType.DMA((n_copies,)),
                   *[pltpu.HBM(b.shape, b.dtype) for b in bufs], jax.ShapeDtypeStruct((8, 128), F32)),
        in_specs=(_HBM,) * nb + (pl.BlockSpec(memory_space=pl.ANY),),
        out_specs=(_SEM, _SEM) + (_HBM,) * nb + (pl.BlockSpec(memory_space=pltpu.VMEM),),
        input_output_aliases={i: 2 + i for i in range(nb)},
        compiler_params=pltpu.CompilerParams(has_side_effects=_EFFECT),
    )(*[pltpu.with_memory_space_constraint(b, pltpu.HBM) for b in bufs], after)
    return out[0], out[1], list(out[2:2 + nb]), out[-1]


def _wait_copies(flight, copies, after, name):
    send_sems, recv_sems, bufs, _ = flight
    nb = len(bufs)

    def body(*refs):
        for cp in copies(refs[:nb], refs[nb], refs[nb + 1]):
            cp.wait_send()
            cp.wait_recv()

    return pl.pallas_call(
        body, name=name,
        out_shape=tuple(pltpu.HBM(b.shape, b.dtype) for b in bufs),
        in_specs=(_HBM,) * nb + (_SEM, _SEM, pl.BlockSpec(memory_space=pl.ANY)), out_specs=(_HBM,) * nb,
        input_output_aliases={i: i for i in range(nb)},
        compiler_params=pltpu.CompilerParams(has_side_effects=_EFFECT),
    )(*bufs, send_sems, recv_sems, after)


class _From:
    def __init__(self, sems, offset):
        self.sems, self.offset = sems, offset

    @property
    def at(self):
        return self

    def __getitem__(self, k):
        return self.sems.at[k + self.offset]


def _group(copies, first_buf, n_bufs, offset):
    def grouped(refs, send_sems, recv_sems):
        return copies(refs[first_buf:first_buf + n_bufs], _From(send_sems, offset), _From(recv_sems, offset))
    return grouped


def _wait_then_start(flight, waited, started, n_started, after, name, more_bufs=()):
    old_send, old_recv, bufs, _ = flight
    bufs = list(bufs) + [pltpu.with_memory_space_constraint(b, pltpu.HBM) for b in more_bufs]
    nb = len(bufs)

    def body(*refs):
        for cp in waited(refs[:nb], refs[nb], refs[nb + 1]):
            cp.wait_send()
            cp.wait_recv()
        for cp in started(refs[:nb], refs[nb + 3], refs[nb + 4]):
            cp.start()
        refs[-1][...] = jnp.zeros_like(refs[-1])

    out = pl.pallas_call(
        body, name=name,
        out_shape=(pltpu.SemaphoreType.DMA((n_started,)), pltpu.SemaphoreType.DMA((n_started,)),
                   *[pltpu.HBM(b.shape, b.dtype) for b in bufs], jax.ShapeDtypeStruct((8, 128), F32)),
        in_specs=(_HBM,) * nb + (_SEM, _SEM, pl.BlockSpec(memory_space=pl.ANY)),
        out_specs=(_SEM, _SEM) + (_HBM,) * nb + (pl.BlockSpec(memory_space=pltpu.VMEM),),
        input_output_aliases={i: 2 + i for i in range(nb)},
        compiler_params=pltpu.CompilerParams(has_side_effects=_EFFECT),
    )(*bufs, old_send, old_recv, after)
    return out[0], out[1], list(out[2:2 + nb]), out[-1]


def _late_pair_copies(refs, send_sems, recv_sems):
    blocks_ref, land_ref = refs
    x, y, c = _place()
    first = _first_axis_chip(x, y, c)
    devices = [4 * x + 2 * y + 1 - c, 4 * first[0] + 2 * first[1] + 1 - c]
    return [pltpu.make_async_remote_copy(
        src_ref=blocks_ref.at[devices[k]], dst_ref=land_ref.at[k], send_sem=send_sems.at[k], recv_sem=recv_sems.at[k],
        device_id=(x, y, 1 - c), device_id_type=MESH) for k in range(2)]


def _chip_copies(refs, send_sems, recv_sems):
    pair_ref, land_ref = refs
    x, y, c = _place()
    chips = [(1 - x, y), (x, 1 - y), (1 - x, 1 - y)]
    return [pltpu.make_async_remote_copy(
        src_ref=pair_ref.at[2 * chip[0] + chip[1]], dst_ref=land_ref.at[k],
        send_sem=send_sems.at[k], recv_sem=recv_sems.at[k],
        device_id=(*chip, c), device_id_type=MESH) for k, chip in enumerate(chips)]


def _first_hop_copies(refs, send_sems, recv_sems):
    pair_ref, land_ref = refs
    x, y, c = _place()
    return [pltpu.make_async_remote_copy(
        src_ref=pair_ref.at[k], dst_ref=land_ref.at[k], send_sem=send_sems.at[k], recv_sem=recv_sems.at[k],
        device_id=(*_first_axis_chip(x, y, c), c), device_id_type=MESH) for k in range(2)]


def _second_hop_copies(refs, send_sems, recv_sems):
    relay_ref, land_ref = refs
    x, y, c = _place()
    second = ((x + c) % 2, (y + 1 - c) % 2)
    return [pltpu.make_async_remote_copy(
        src_ref=relay_ref, dst_ref=land_ref.at[0], send_sem=send_sems.at[0], recv_sem=recv_sems.at[0],
        device_id=(*second, c), device_id_type=MESH)]


def _own_block_copies(targets):
    def copies(refs, send_sems, recv_sems):
        x, y, c = _place()
        mine = refs[0].at[4 * x + 2 * y + c]
        return [pltpu.make_async_remote_copy(
            src_ref=mine, dst_ref=mine, send_sem=send_sems.at[k], recv_sem=recv_sems.at[k],
            device_id=to, device_id_type=MESH) for k, to in enumerate(targets(x, y, c))]
    return copies


def _all_others(x, y, c):
    flip = lambda v, f: 1 - v if f else v
    return [(flip(x, r & 4), flip(y, r & 2), flip(c, r & 1)) for r in range(1, N_DEV)]


def _forward_copies(refs, send_sems, recv_sems):
    x, y, c = _place()
    chips = [(1 - x, y), (x, 1 - y), (1 - x, 1 - y)]
    return [pltpu.make_async_remote_copy(
        src_ref=refs[0].at[4 * chip[0] + 2 * chip[1] + c], dst_ref=refs[0].at[4 * chip[0] + 2 * chip[1] + c],
        send_sem=send_sems.at[k], recv_sem=recv_sems.at[k],
        device_id=(x, y, 1 - c), device_id_type=MESH) for k, chip in enumerate(chips)]


def _first_axis_chip(x, y, c):
    return (x + 1 - c) % 2, (y + c) % 2


def _second_axis_chip(x, y, c):
    return (x + c) % 2, (y + 1 - c) % 2


def _first_targets(x, y, c):
    return [(x, y, 1 - c), (*_first_axis_chip(x, y, c), c)]


def _all_gather_small(shard, name):
    def body(in_ref, out_ref, send_sems, recv_sems, local_sem):
        x, y, c = _place()
        me, sibling = 4 * x + 2 * y + c, (x, y, 1 - c)
        first, second = _first_axis_chip(x, y, c), _second_axis_chip(x, y, c)

        def pair(chip):
            return out_ref.at[pl.ds(2 * (2 * chip[0] + chip[1]), 2)]

        def exchange(k, src, dst, to):
            cp = pltpu.make_async_remote_copy(src_ref=src, dst_ref=dst, send_sem=send_sems.at[k],
                                              recv_sem=recv_sems.at[k], device_id=to, device_id_type=MESH)
            cp.start()
            cp.wait()

        own = pltpu.make_async_copy(in_ref, out_ref.at[me], local_sem)
        own.start()
        exchange(0, in_ref, out_ref.at[me], sibling)
        own.wait()
        exchange(1, pair((x, y)), pair((x, y)), (*second, c))
        exchange(2, pair(second), pair(second), sibling)
        exchange(3, pair(first), pair(first), (*second, c))

    spec = pl.BlockSpec(memory_space=pltpu.VMEM)
    return pl.pallas_call(
        body, name=name, out_shape=jax.ShapeDtypeStruct((N_DEV,) + shard.shape, shard.dtype),
        in_specs=[spec], out_specs=spec,
        scratch_shapes=[pltpu.SemaphoreType.DMA((4,)), pltpu.SemaphoreType.DMA((4,)), pltpu.SemaphoreType.DMA],
        compiler_params=_params(),
    )(shard)


def _slot_copies(refs, send_sems, recv_sems, plan):
    copies = []
    for k, ((px, py, pc), to) in enumerate(plan):
        blk = refs[0].at[4 * px + 2 * py + pc]
        copies.append(pltpu.make_async_remote_copy(
            src_ref=blk, dst_ref=blk, send_sem=send_sems.at[k], recv_sem=recv_sems.at[k],
            device_id=to, device_id_type=MESH))
    return copies


def _second_axis_stage_copies(refs, send_sems, recv_sems):
    x, y, c = _place()
    first, second = (*_first_axis_chip(x, y, c), c), (*_second_axis_chip(x, y, c), c)
    return _slot_copies(refs, send_sems, recv_sems, [((x, y, c), second), (first, (x, y, 1 - c)), (first, second)])


def _second_axis_forward_copies(refs, send_sems, recv_sems):
    x, y, c = _place()
    return _slot_copies(refs, send_sems, recv_sems, [((*_second_axis_chip(x, y, c), c), (x, y, 1 - c))])


def _diagonal_forward_copies(refs, send_sems, recv_sems):
    x, y, c = _place()
    blk = refs[0].at[4 * (1 - x) + 2 * (1 - y) + c]
    return [pltpu.make_async_remote_copy(
        src_ref=blk, dst_ref=blk, send_sem=send_sems.at[0], recv_sem=recv_sems.at[0],
        device_id=(x, y, 1 - c), device_id_type=MESH)]


def _with_own_slot(block, me):
    return lax.dynamic_update_index_in_dim(lax.empty((N_DEV,) + block.shape, block.dtype), block, me, 0)


def _matmul(a, b, dims, out_dtype, tm, tn, name, dep=None):
    if dims == "nn":
        (m, k), n = a.shape, b.shape[1]
        a_spec = pl.BlockSpec((tm, k), lambda i, j: (i, 0))
        b_spec = pl.BlockSpec((k, tn), lambda i, j: (0, j))
        contract = ((1,), (0,))
    elif dims == "nt":
        (m, k), n = a.shape, b.shape[0]
        a_spec = pl.BlockSpec((tm, k), lambda i, j: (i, 0))
        b_spec = pl.BlockSpec((tn, k), lambda i, j: (j, 0))
        contract = ((1,), (1,))
    else:
        (k, m), n = a.shape, b.shape[1]
        a_spec = pl.BlockSpec((k, tm), lambda i, j: (0, i))
        b_spec = pl.BlockSpec((k, tn), lambda i, j: (0, j))
        contract = ((0,), (0,))
    assert m % tm == 0 and n % tn == 0 and a.dtype == BF16 and b.dtype == BF16

    def body(a_ref, b_ref, *rest):
        rest[-1][...] = lax.dot_general(a_ref[...], b_ref[...], (contract, ((), ())),
                                        preferred_element_type=F32).astype(out_dtype)

    deps = [] if dep is None else [dep]
    return pl.pallas_call(
        body, name=name, grid=(m // tm, n // tn),
        in_specs=[a_spec, b_spec] + [pl.BlockSpec((8, 128), lambda i, j: (0, 0))] * len(deps),
        out_specs=pl.BlockSpec((tm, tn), lambda i, j: (i, j)),
        out_shape=jax.ShapeDtypeStruct((m, n), out_dtype),
        compiler_params=_params(dimension_semantics=("arbitrary", "arbitrary")),
    )(a, b, *deps)


Z_TILE = 768
_Z_TILE_ORDER = ((0, 1, 2, 3, 4, 5, 6), (2, 0, 1, 6, 3, 4, 5), (4, 0, 5, 6, 1, 2, 3), (6, 2, 3, 4, 0, 1, 5))
_Z_EARLY_TILES = 4


def _z_proj(h, w_in_t, chip, first, count, z_prev, name, tr=1024):
    t = h.shape[0]

    def body(chip_ref, h_ref, w_ref, z_prev_ref, z_ref):
        z_ref[...] = _dot_nt(h_ref[...], w_ref[...])

    def tile(j, chip_ref):
        picked = 0
        for c, order in enumerate(_Z_TILE_ORDER):
            for k in range(count):
                picked = picked + jnp.where((chip_ref[0] == c) & (j == k), order[first + k], 0)
        return picked

    return pl.pallas_call(
        body, name=name,
        grid_spec=pltpu.PrefetchScalarGridSpec(
            num_scalar_prefetch=1, grid=(count, t // tr),
            in_specs=[pl.BlockSpec((tr, D_MODEL), lambda j, i, o: (i, 0)),
                      pl.BlockSpec((Z_TILE, D_MODEL), lambda j, i, o: (tile(j, o), 0)),
                      pl.BlockSpec(memory_space=pl.ANY)],
            out_specs=pl.BlockSpec((tr, Z_TILE), lambda j, i, o: (i, tile(j, o)))),
        out_shape=jax.ShapeDtypeStruct((t, D_IN), F32),
        input_output_aliases={3: 0},
        compiler_params=_params(dimension_semantics=("arbitrary", "arbitrary")),
    )(chip, h, w_in_t, z_prev)


def _modulation(device, c_all, w_ada, b_ada):
    def body(device_ref, c_ref, w_ref, b_ref, act_ref, mod_ref):
        cv = c_ref[...]
        act = cv * _sigmoid(cv)
        act_ref[...] = act
        mod_ref[...] = jnp.dot(act.astype(BF16), w_ref[...].astype(BF16), preferred_element_type=F32) + b_ref[...]

    whole = lambda a: pl.BlockSpec(a.shape, lambda i, device_ref: (0,) * a.ndim)
    return pl.pallas_call(
        body, name="modulation",
        grid_spec=pltpu.PrefetchScalarGridSpec(
            num_scalar_prefetch=1, grid=(1,),
            in_specs=[whole(c_all), whole(w_ada), pl.BlockSpec((1, W_ADA_SHARD), lambda i, device_ref: (0, device_ref[0]))],
            out_specs=(whole(c_all), pl.BlockSpec((N_DEV, W_ADA_SHARD), lambda i, device_ref: (0, 0)))),
        out_shape=(jax.ShapeDtypeStruct(c_all.shape, F32), jax.ShapeDtypeStruct((N_DEV, W_ADA_SHARD), F32)),
        compiler_params=_params(dimension_semantics=("arbitrary",)),
    )(device, c_all, w_ada, b_ada)


MOD_SHIFT, MOD_SCALE, MOD_GATE = 0, 1, 2


def _mod_spec(part, d):
    return pl.BlockSpec((1, d), lambda i: (0, part))


def _norm_z_proj_own(x, norm_g, mod, w_in_t, chip, tm=512):
    t, d = x.shape

    def body(chip_ref, x_ref, g_ref, sc_ref, sh_ref, w_ref, h_ref, z_ref):
        xv = x_ref[...]
        r = lax.rsqrt(jnp.mean(xv * xv, axis=-1, keepdims=True) + EPS)
        h = ((xv * r) * g_ref[...] * (1.0 + sc_ref[...]) + sh_ref[...]).astype(BF16)
        h_ref[...] = h
        z_ref[...] = _dot_nt(h, w_ref[...])

    def own_tile(chip_ref):
        picked = 0
        for c, order in enumerate(_Z_TILE_ORDER):
            picked = picked + jnp.where(chip_ref[0] == c, order[0], 0)
        return picked

    def row(part):
        return pl.BlockSpec((1, d), lambda i, o: (0, part))

    return pl.pallas_call(
        body, name="norm_z_proj_own",
        grid_spec=pltpu.PrefetchScalarGridSpec(
            num_scalar_prefetch=1, grid=(t // tm,),
            in_specs=[pl.BlockSpec((tm, d), lambda i, o: (i, 0)), row(0), row(MOD_SCALE), row(MOD_SHIFT),
                      pl.BlockSpec((Z_TILE, d), lambda i, o: (own_tile(o), 0))],
            out_specs=(pl.BlockSpec((tm, d), lambda i, o: (i, 0)),
                       pl.BlockSpec((tm, Z_TILE), lambda i, o: (i, own_tile(o))))),
        out_shape=(jax.ShapeDtypeStruct((t, d), BF16), jax.ShapeDtypeStruct((t, D_IN), F32)),
        compiler_params=_params(dimension_semantics=("arbitrary",)),
    )(chip, x, norm_g, mod, mod, w_in_t)


def _window_bias(block_index):
    s = lax.broadcasted_iota(jnp.int32, (2 * BLOCK, BLOCK), 0)
    t = lax.broadcasted_iota(jnp.int32, (2 * BLOCK, BLOCK), 1)
    valid = ((s < BLOCK) & (s > t) & (block_index > 0)) | ((s >= BLOCK) & ((s - BLOCK) <= t))
    bias = jnp.where(valid, 0.0, -jnp.inf).astype(F32)
    return jnp.concatenate([bias] * 8, axis=1)


def _heads_t(pair_blocks, g):
    top = lax.broadcasted_iota(jnp.int32, (BLOCK, BLOCK), 0) < HEAD_DIM
    zeros = jnp.zeros((HEAD_DIM, BLOCK), F32)
    tiles = []
    for blk in pair_blocks:
        tp = blk.T
        if g == 0:
            tiles += [jnp.where(top, tp, 0.0), jnp.concatenate([tp[HEAD_DIM:], zeros], axis=0)]
        else:
            tiles += [jnp.concatenate([zeros, tp[:HEAD_DIM]], axis=0), jnp.where(top, 0.0, tp)]
    return jnp.concatenate(tiles, axis=1)


def _pair_block(xt, p, g):
    r0 = HEAD_DIM * g
    even = xt[r0:r0 + HEAD_DIM, (2 * p) * BLOCK:(2 * p + 1) * BLOCK]
    odd = xt[r0:r0 + HEAD_DIM, (2 * p + 1) * BLOCK:(2 * p + 2) * BLOCK]
    return jnp.concatenate([even, odd], axis=0).T


def _softmax_t(scores_t, bias, sink):
    st = scores_t + bias
    m = jnp.maximum(jnp.max(st, axis=0, keepdims=True), sink)
    e = jnp.exp(st - m)
    es = jnp.exp(sink - m)
    inv = 1.0 / (jnp.sum(e, axis=0, keepdims=True) + es)
    return e * inv, es * inv


def _dot(a, b):
    return jnp.dot(a, b, preferred_element_type=F32)


def _dot_nt(a, b):
    return lax.dot_general(a, b, (((1,), (1,)), ((), ())), preferred_element_type=F32)


def _layer_norm_fwd(v):
    mu = jnp.mean(v, axis=-1, keepdims=True)
    xc = v - mu
    rstd = lax.rsqrt(jnp.mean(xc * xc, axis=-1, keepdims=True) + EPS)
    return xc * rstd, rstd


def _tril(transposed=False):
    t = lax.broadcasted_iota(jnp.int32, (BLOCK, BLOCK), 0)
    s = lax.broadcasted_iota(jnp.int32, (BLOCK, BLOCK), 1)
    return s >= t if transposed else t >= s


def _const_spec(shape):
    return pl.BlockSpec(shape, lambda i: (0,) * len(shape))


def _keys_values(z_ref, kvp):
    kvc = z_ref[:, SEG_KV:SEG_KV + 2 * D_KV]
    kk = jnp.concatenate([kvp[:, :D_KV], kvc[:, :D_KV]], axis=0)
    vv = jnp.concatenate([kvp[:, D_KV:], kvc[:, D_KV:]], axis=0)
    return kk, vv


MIXER_BLOCKS = 2


class _Rows:
    def __init__(self, ref, sub):
        self.ref, self.rows = ref, slice(sub * BLOCK, (sub + 1) * BLOCK)

    def __getitem__(self, idx):
        return self.ref[self.rows, idx[1]]

    def __setitem__(self, idx, value):
        self.ref[self.rows, idx[1]] = value


def _kv_before_spec(index):
    return pl.BlockSpec((BLOCK, 2 * D_KV),
                        lambda i: (jnp.maximum(MIXER_BLOCKS * index(i) - 1, 0), SEG_KV // (2 * D_KV)))


def _pair_cols(g, p, base=0):
    return slice(base + (4 * g + p) * 128, base + (4 * g + p + 1) * 128)


def _mixer_fwd(z, sink_rows, ln_g, ln_b, sgu_w, sgu_bt):
    t = z.shape[0]

    def body(z_all, kvp_ref, sink_ref, lng_ref, lnb_ref, w_ref, bt_ref, a_all, prob_ref, sink_prob_ref):
        kv_before = kvp_ref[...]
        for sub in range(MIXER_BLOCKS):
            z_ref, a_ref = _Rows(z_all, sub), _Rows(a_all, sub)
            one_block(z_ref, kv_before, MIXER_BLOCKS * pl.program_id(0) + sub, sink_ref, lng_ref, lnb_ref, w_ref,
                      bt_ref, a_ref, prob_ref.at[sub], sink_prob_ref.at[sub])
            kv_before = z_ref[:, SEG_KV:SEG_KV + 2 * D_KV]

    def one_block(z_ref, kv_before, block_index, sink_ref, lng_ref, lnb_ref, w_ref, bt_ref, a_ref, prob_ref,
                  sink_prob_ref):
        bias = _window_bias(block_index)
        kk, vv = _keys_values(z_ref, kv_before)
        kk_b, vvt_b = kk.astype(BF16), vv.T.astype(BF16)
        for g in range(2):
            qt = _heads_t([z_ref[:, _pair_cols(g, p, SEG_Q)] * ATTN_SCALE for p in range(4)], g).astype(BF16)
            prob, sink_prob = _softmax_t(_dot(kk_b, qt), bias, sink_ref[g])
            prob_b = prob.astype(BF16)
            prob_ref[g] = prob_b
            sink_prob_ref[g] = sink_prob
            ot = _dot(vvt_b, prob_b)
            for p in range(4):
                gate = z_ref[:, _pair_cols(g, p, SEG_GA)]
                a_ref[:, _pair_cols(g, p)] = (_pair_block(ot, p, g) * (gate * _sigmoid(gate))).astype(BF16)

        vhat, _ = _layer_norm_fwd(z_ref[:, SEG_VS:SEG_VS + D_SGU])
        vn = vhat * lng_ref[...] + lnb_ref[...]
        tril = _tril()
        for g in range(SGU_GROUPS):
            cols = slice(g * 128, (g + 1) * 128)
            wm = jnp.where(tril, w_ref[g], 0.0).astype(BF16)
            mixed = _dot(wm, vn[:, cols].astype(BF16)) + bt_ref[:, g:g + 1]
            gate = z_ref[:, SEG_GS + g * 128:SEG_GS + (g + 1) * 128]
            a_ref[:, D_ATTN + g * 128:D_ATTN + (g + 1) * 128] = (
                (z_ref[:, SEG_U + g * 128:SEG_U + (g + 1) * 128] * mixed) * (gate * _sigmoid(gate))).astype(BF16)

    rows = MIXER_BLOCKS * BLOCK
    return pl.pallas_call(
        body, name="mixer_fwd", grid=(t // rows,),
        in_specs=[pl.BlockSpec((rows, D_IN), lambda i: (i, 0)), _kv_before_spec(lambda i: i),
                  _const_spec((2, 1, 8 * BLOCK)), _const_spec((1, D_SGU)), _const_spec((1, D_SGU)),
                  _const_spec((SGU_GROUPS, BLOCK, BLOCK)), _const_spec((BLOCK, SGU_GROUPS))],
        out_specs=(pl.BlockSpec((rows, D_MODEL), lambda i: (i, 0)),
                   pl.BlockSpec((MIXER_BLOCKS, 2, 2 * BLOCK, 8 * BLOCK), lambda i: (i, 0, 0, 0)),
                   pl.BlockSpec((MIXER_BLOCKS, 2, 1, 8 * BLOCK), lambda i: (i, 0, 0, 0))),
        out_shape=(jax.ShapeDtypeStruct((t, D_MODEL), BF16),
                   jax.ShapeDtypeStruct((t // BLOCK, 2, 2 * BLOCK, 8 * BLOCK), BF16),
                   jax.ShapeDtypeStruct((t // BLOCK, 2, 1, 8 * BLOCK), F32)),
        compiler_params=_params(dimension_semantics=("arbitrary",)),
    )(z, z, sink_rows, ln_g, ln_b, sgu_w, sgu_bt)


def _mixer_bwd(z, da, probs, sink_probs, ln_g, ln_b, sgu_w, sgu_wt, sgu_bt):
    t = z.shape[0]

    def body(z_all, kvp_ref, da_all, prob_ref, sink_prob_ref, lng_ref, lnb_ref, w_ref, wt_ref, bt_ref,
             dz_all, dsink_ref, dw_ref, db_ref, dlng_ref, dlnb_ref, carry_ref, dsink_acc, dbt_acc):
        step = pl.program_id(0)

        @pl.when(step == 0)
        def _():
            carry_ref[...] = jnp.zeros_like(carry_ref)
            dsink_acc[...] = jnp.zeros_like(dsink_acc)
            dbt_acc[...] = jnp.zeros_like(dbt_acc)
            dw_ref[...] = jnp.zeros_like(dw_ref)
            dlng_ref[...] = jnp.zeros_like(dlng_ref)
            dlnb_ref[...] = jnp.zeros_like(dlnb_ref)

        carry = carry_ref[...]
        for sub in reversed(range(MIXER_BLOCKS)):
            kv_before = kvp_ref[...] if sub == 0 else _Rows(z_all, sub - 1)[:, SEG_KV:SEG_KV + 2 * D_KV]
            carry = one_block(_Rows(z_all, sub), kv_before, _Rows(da_all, sub), prob_ref.at[sub], sink_prob_ref.at[sub],
                              carry, lng_ref, lnb_ref, w_ref, wt_ref, bt_ref, _Rows(dz_all, sub),
                              dw_ref, dlng_ref, dlnb_ref, dsink_acc, dbt_acc)
        carry_ref[...] = carry

        @pl.when(step == ns - 1)
        def _():
            db_ref[...] = dbt_acc[...].T[:SGU_GROUPS]
            lane_row = lax.broadcasted_iota(jnp.int32, (1, 128), 1)
            d_sink = jnp.zeros((1, 128), F32)
            for g in range(2):
                acc = dsink_acc[g]
                for j in range(8):
                    head_sum = jnp.sum(acc[:, j * BLOCK:(j + 1) * BLOCK], axis=-1, keepdims=True)
                    d_sink = d_sink + jnp.where(lane_row == 8 * g + j, head_sum, 0.0)
            dsink_ref[...] = d_sink

    def one_block(z_ref, kv_before, da_ref, prob_ref, sink_prob_ref, carry, lng_ref, lnb_ref, w_ref, wt_ref, bt_ref,
                  dz_ref, dw_ref, dlng_ref, dlnb_ref, dsink_acc, dbt_acc):
        kk, vv = _keys_values(z_ref, kv_before)
        vv_b = vv.astype(BF16)
        kkt_b, vvt_b = kk.T.astype(BF16), vv.T.astype(BF16)
        dkk = jnp.zeros((2 * BLOCK, D_KV), F32)
        dvv = jnp.zeros((2 * BLOCK, D_KV), F32)
        for g in range(2):
            qt = _heads_t([z_ref[:, _pair_cols(g, p, SEG_Q)] * ATTN_SCALE for p in range(4)], g).astype(BF16)
            prob_b, sink_prob = prob_ref[g], sink_prob_ref[g]
            prob = prob_b.astype(F32)
            ot = _dot(vvt_b, prob_b)
            gates = [z_ref[:, _pair_cols(g, p, SEG_GA)] for p in range(4)]
            sig = [_sigmoid(gt) for gt in gates]
            d_attn = [da_ref[:, _pair_cols(g, p)] for p in range(4)]
            d_ot = _heads_t([d_attn[p] * (gates[p] * sig[p]) for p in range(4)], g).astype(BF16)
            d_prob = _dot(vv_b, d_ot)
            delta = jnp.sum(prob * d_prob, axis=0, keepdims=True)
            d_scores = (prob * (d_prob - delta)).astype(BF16)
            dsink_acc[g] -= sink_prob * delta
            d_qt = _dot(kkt_b, d_scores)
            dkk = dkk + _dot_nt(d_scores, qt)
            dvv = dvv + _dot_nt(prob_b, d_ot)
            for p in range(4):
                dz_ref[:, _pair_cols(g, p, SEG_Q)] = (_pair_block(d_qt, p, g) * ATTN_SCALE).astype(BF16)
                d_silu = sig[p] * (1.0 + gates[p] * (1.0 - sig[p]))
                dz_ref[:, _pair_cols(g, p, SEG_GA)] = (d_attn[p] * _pair_block(ot, p, g) * d_silu).astype(BF16)
        d_kv = jnp.concatenate([dkk, dvv], axis=1)
        dz_ref[:, SEG_KV:SEG_KV + 2 * D_KV] = (d_kv[BLOCK:] + carry).astype(BF16)

        vhat, rstd = _layer_norm_fwd(z_ref[:, SEG_VS:SEG_VS + D_SGU])
        lng = lng_ref[...]
        vn = vhat * lng + lnb_ref[...]
        tril, triu = _tril(), _tril(transposed=True)
        lane = lax.broadcasted_iota(jnp.int32, (BLOCK, 128), 1)
        d_bt = jnp.zeros((BLOCK, 128), F32)
        d_vn = []
        for g in range(SGU_GROUPS):
            cols = slice(g * 128, (g + 1) * 128)
            wm = jnp.where(tril, w_ref[g], 0.0).astype(BF16)
            wmt = jnp.where(triu, wt_ref[g], 0.0).astype(BF16)
            vn_g = vn[:, cols].astype(BF16)
            mixed = _dot(wm, vn_g) + bt_ref[:, g:g + 1]
            gate = z_ref[:, SEG_GS + g * 128:SEG_GS + (g + 1) * 128]
            u = z_ref[:, SEG_U + g * 128:SEG_U + (g + 1) * 128]
            d_out = da_ref[:, D_ATTN + g * 128:D_ATTN + (g + 1) * 128]
            sg = _sigmoid(gate)
            d_um = d_out * (gate * sg)
            dz_ref[:, SEG_U + g * 128:SEG_U + (g + 1) * 128] = (d_um * mixed).astype(BF16)
            dz_ref[:, SEG_GS + g * 128:SEG_GS + (g + 1) * 128] = (
                d_out * (u * mixed) * (sg * (1.0 + gate * (1.0 - sg)))).astype(BF16)
            d_mixed = d_um * u
            d_mixed_b = d_mixed.astype(BF16)
            dw_ref[g] += jnp.where(tril, _dot_nt(d_mixed_b, vn_g), 0.0)
            d_bt = d_bt + jnp.where(lane == g, jnp.sum(d_mixed, axis=-1, keepdims=True), 0.0)
            d_vn.append(_dot(wmt, d_mixed_b))
        dbt_acc[...] += d_bt
        d_vn = jnp.concatenate(d_vn, axis=1)
        dlng_ref[...] += jnp.sum(d_vn * vhat, axis=0, keepdims=True)
        dlnb_ref[...] += jnp.sum(d_vn, axis=0, keepdims=True)
        d_vhat = d_vn * lng
        d_v = rstd * (d_vhat - jnp.mean(d_vhat, axis=-1, keepdims=True)
                      - vhat * jnp.mean(d_vhat * vhat, axis=-1, keepdims=True))
        dz_ref[:, SEG_VS:SEG_VS + D_SGU] = d_v.astype(BF16)
        return d_kv[:BLOCK]

    rows = MIXER_BLOCKS * BLOCK
    ns = t // rows
    rev = lambda i: ns - 1 - i
    return pl.pallas_call(
        body, name="mixer_bwd", grid=(ns,),
        in_specs=[pl.BlockSpec((rows, D_IN), lambda i: (rev(i), 0)), _kv_before_spec(rev),
                  pl.BlockSpec((rows, D_MODEL), lambda i: (rev(i), 0)),
                  pl.BlockSpec((MIXER_BLOCKS, 2, 2 * BLOCK, 8 * BLOCK), lambda i: (rev(i), 0, 0, 0)),
                  pl.BlockSpec((MIXER_BLOCKS, 2, 1, 8 * BLOCK), lambda i: (rev(i), 0, 0, 0)),
                  _const_spec((1, D_SGU)), _const_spec((1, D_SGU)),
                  _const_spec((SGU_GROUPS, BLOCK, BLOCK)), _const_spec((SGU_GROUPS, BLOCK, BLOCK)),
                  _const_spec((BLOCK, SGU_GROUPS))],
        out_specs=(pl.BlockSpec((rows, D_IN), lambda i: (rev(i), 0)), _const_spec((1, 128)),
                   _const_spec((SGU_GROUPS, BLOCK, BLOCK)), _const_spec((SGU_GROUPS, BLOCK)),
                   _const_spec((1, D_SGU)), _const_spec((1, D_SGU))),
        out_shape=(jax.ShapeDtypeStruct((t, D_IN), BF16), jax.ShapeDtypeStruct((1, 128), F32),
                   jax.ShapeDtypeStruct((SGU_GROUPS, BLOCK, BLOCK), F32), jax.ShapeDtypeStruct((SGU_GROUPS, BLOCK), F32),
                   jax.ShapeDtypeStruct((1, D_SGU), F32), jax.ShapeDtypeStruct((1, D_SGU), F32)),
        scratch_shapes=[pltpu.VMEM((BLOCK, 2 * D_KV), F32), pltpu.VMEM((2, 1, 8 * BLOCK), F32),
                        pltpu.VMEM((BLOCK, 128), F32)],
        compiler_params=_params(dimension_semantics=("arbitrary",)),
    )(z, z, da, probs, sink_probs, ln_g, ln_b, sgu_w, sgu_wt, sgu_bt)


def _out_proj_head(a, w_out_full, x, target, mod, final_g, tm=256):
    t, d = x.shape

    def body(a_ref, w_ref, x_ref, tg_ref, gate_ref, fg_ref, dx2_ref, dy_ref, loss_ref, dfg_ref, dgate_ref):
        @pl.when(pl.program_id(0) == 0)
        def _():
            loss_ref[...] = jnp.zeros_like(loss_ref)
            dfg_ref[...] = jnp.zeros_like(dfg_ref)
            dgate_ref[...] = jnp.zeros_like(dgate_ref)

        yv, gate, fg = _dot(a_ref[...], w_ref[...]), gate_ref[...], fg_ref[...]
        x2 = x_ref[...] + gate * yv
        r2 = lax.rsqrt(jnp.mean(x2 * x2, axis=-1, keepdims=True) + EPS)
        nrm = x2 * r2
        err = nrm * fg - tg_ref[...]
        loss_ref[...] += 0.5 * jnp.sum(jnp.mean(err * err, axis=-1, keepdims=True), axis=0, keepdims=True)
        fg_d = fg * (1.0 / d)
        err_nrm = err * nrm
        dfg_ref[...] += jnp.sum(err_nrm, axis=0, keepdims=True) * (1.0 / d)
        d_nrm = err * fg_d
        dx2 = r2 * (d_nrm - nrm * jnp.mean(err_nrm * fg_d, axis=-1, keepdims=True))
        dx2_ref[...] = dx2
        dgate_ref[...] += jnp.sum(dx2 * yv, axis=0, keepdims=True)
        dy_ref[...] = (dx2 * gate).astype(BF16)

    blk = pl.BlockSpec((tm, d), lambda i: (i, 0))
    row = _const_spec((1, d))
    whole = pl.BlockSpec(w_out_full.shape, lambda i: (0, 0), pipeline_mode=pl.Buffered(1))
    return pl.pallas_call(
        body, name="out_proj_head", grid=(t // tm,),
        in_specs=[pl.BlockSpec((tm, a.shape[1]), lambda i: (i, 0)), whole, blk, blk, _mod_spec(MOD_GATE, d), row],
        out_specs=(blk, blk, _const_spec((1, 128)), row, row),
        out_shape=(jax.ShapeDtypeStruct((t, d), F32), jax.ShapeDtypeStruct((t, d), BF16),
                   jax.ShapeDtypeStruct((1, 128), F32), jax.ShapeDtypeStruct((1, d), F32),
                   jax.ShapeDtypeStruct((1, d), F32)),
        compiler_params=_params(dimension_semantics=("arbitrary",)),
    )(a, w_out_full, x, target, mod, final_g)


def _z_proj_bwd_norm(dz, w_in_t, x, dx2, norm_g, mod, dep, tm=256):
    t, d = x.shape

    def body(dz_ref, w_ref, x_ref, dx2_ref, g_ref, sc_ref, dep_ref, gx_ref, dshift_ref, dscale_ref, dg_ref):
        @pl.when(pl.program_id(0) == 0)
        def _():
            dshift_ref[...] = jnp.zeros_like(dshift_ref)
            dscale_ref[...] = jnp.zeros_like(dscale_ref)
            dg_ref[...] = jnp.zeros_like(dg_ref)

        dh, xv, g = _dot(dz_ref[...], w_ref[...]), x_ref[...], g_ref[...]
        one_plus = 1.0 + sc_ref[...]
        r = lax.rsqrt(jnp.mean(xv * xv, axis=-1, keepdims=True) + EPS)
        xn = xv * r
        gain = one_plus * g
        dh_xn = dh * xn
        dh_xn_sum = jnp.sum(dh_xn, axis=0, keepdims=True)
        dshift_ref[...] += jnp.sum(dh, axis=0, keepdims=True)
        dscale_ref[...] += dh_xn_sum * g
        dg_ref[...] += dh_xn_sum * one_plus
        d_xn = dh * gain
        gx_ref[...] = dx2_ref[...] + r * (d_xn - xn * jnp.mean(dh_xn * gain, axis=-1, keepdims=True))

    blk = pl.BlockSpec((tm, d), lambda i: (i, 0))
    row = _const_spec((1, d))
    whole = pl.BlockSpec(w_in_t.shape, lambda i: (0, 0), pipeline_mode=pl.Buffered(1))
    return pl.pallas_call(
        body, name="z_proj_bwd_norm", grid=(t // tm,),
        in_specs=[pl.BlockSpec((tm, dz.shape[1]), lambda i: (i, 0)), whole, blk, blk, row, _mod_spec(MOD_SCALE, d),
                  _const_spec((8, 128))],
        out_specs=(blk, row, row, row),
        out_shape=(jax.ShapeDtypeStruct((t, d), F32),) + (jax.ShapeDtypeStruct((1, d), F32),) * 3,
        compiler_params=_params(dimension_semantics=("arbitrary",)),
    )(dz, w_in_t, x, dx2, norm_g, mod, dep)


def _adamw(w, g, m, v):
    m = ADAM_B1 * m + (1.0 - ADAM_B1) * g
    v = ADAM_B2 * v + (1.0 - ADAM_B2) * (g * g)
    m_hat = m / (1.0 - ADAM_B1 ** ADAM_STEP)
    v_hat = v / (1.0 - ADAM_B2 ** ADAM_STEP)
    delta = -ADAM_LR * (m_hat / (jnp.sqrt(v_hat) + ADAM_EPS) + ADAM_WD * w)
    return delta, m, v


def _relay_sum(device, blocks, land_pair, land_first, tr):
    _, r, c = blocks.shape

    def body(device_ref, a_ref, b_ref, c_ref, o_ref):
        o_ref[...] = (a_ref[...].astype(F32) + b_ref[...].astype(F32) + c_ref[...].astype(F32)).astype(BF16)

    second = pl.BlockSpec((None, tr, c), lambda i, device_ref: (1, i, 0))
    return pl.pallas_call(
        body, name="w_in_grad_relay_sum",
        grid_spec=pltpu.PrefetchScalarGridSpec(
            num_scalar_prefetch=1, grid=(r // tr,),
            in_specs=[pl.BlockSpec((None, tr, c), lambda i, device_ref: (device_ref[0], i, 0)), second, second],
            out_specs=pl.BlockSpec((tr, c), lambda i, device_ref: (i, 0))),
        out_shape=jax.ShapeDtypeStruct((r, c), BF16),
        compiler_params=_params(dimension_semantics=("arbitrary",)),
    )(device, blocks, land_pair, land_first)


def _adam_from_chips(chip, pair, landed, w, m, v, name, tc):
    _, r, c = pair.shape
    n = len(landed)

    def body(chip_ref, own_ref, *refs):
        w_ref, m_ref, v_ref, g_ref, d_ref, nm_ref, nv_ref = refs[n:]
        g = own_ref[...].astype(F32)
        for k in range(n):
            g = g + refs[k][...].astype(F32)
        g_ref[...] = g
        d_ref[...], nm_ref[...], nv_ref[...] = _adamw(w_ref[...], g, m_ref[...], v_ref[...])

    def landed_spec(index):
        return pl.BlockSpec((None, r, tc), lambda i, chip_ref: (index, 0, i))

    blk = pl.BlockSpec((r, tc), lambda i, chip_ref: (0, i))
    return pl.pallas_call(
        body, name=name,
        grid_spec=pltpu.PrefetchScalarGridSpec(
            num_scalar_prefetch=1, grid=(c // tc,),
            in_specs=[pl.BlockSpec((None, r, tc), lambda i, chip_ref: (chip_ref[0], 0, i))]
            + [landed_spec(index) for _, index in landed] + [blk, blk, blk],
            out_specs=(blk,) * 4),
        out_shape=(jax.ShapeDtypeStruct((r, c), F32),) * 4,
        compiler_params=_params(dimension_semantics=("arbitrary",)),
    )(chip, pair, *[array for array, _ in landed], w, m, v)


def _adam_w_ada(device, act_t, dmod_all, w, m, v, tr=256):
    r, c = w.shape

    def body(device_ref, a_ref, dm_ref, w_ref, m_ref, v_ref, g_ref, d_ref, nm_ref, nv_ref):
        g = _dot(a_ref[...].astype(BF16), dm_ref[...].astype(BF16))
        g_ref[...] = g
        d_ref[...], nm_ref[...], nv_ref[...] = _adamw(w_ref[...], g, m_ref[...], v_ref[...])

    blk = pl.BlockSpec((tr, c), lambda i, device_ref: (i, 0))
    return pl.pallas_call(
        body, name="adam_w_ada",
        grid_spec=pltpu.PrefetchScalarGridSpec(
            num_scalar_prefetch=1, grid=(r // tr,),
            in_specs=[pl.BlockSpec((tr, N_DEV), lambda i, device_ref: (i, 0)),
                      pl.BlockSpec((N_DEV, c), lambda i, device_ref: (0, device_ref[0])), blk, blk, blk],
            out_specs=(blk,) * 4),
        out_shape=(jax.ShapeDtypeStruct((r, c), F32),) * 4,
        compiler_params=_params(dimension_semantics=("arbitrary",)),
    )(device, act_t, dmod_all, w, m, v)


def _pack_small(d_shift, d_scale, d_gate, d_norm_g, d_final_g, d_ln_g, d_ln_b, loss, d_sinks, d_sgu_b):
    def body(shift_ref, scale_ref, gate_ref, ng_ref, fg_ref, lng_ref, lnb_ref, loss_ref, sink_ref, b_ref, o_ref):
        o_ref[...] = jnp.zeros_like(o_ref)
        o_ref[ROW_SHIFT:ROW_SHIFT + 1, :] = shift_ref[...]
        o_ref[ROW_SCALE:ROW_SCALE + 1, :] = scale_ref[...]
        o_ref[ROW_GATE:ROW_GATE + 1, :] = gate_ref[...]
        o_ref[ROW_NORM_G:ROW_NORM_G + 1, :] = ng_ref[...]
        o_ref[ROW_FINAL_G:ROW_FINAL_G + 1, :] = fg_ref[...]
        o_ref[ROW_LN:ROW_LN + 1, 0:D_SGU] = lng_ref[...]
        o_ref[ROW_LN:ROW_LN + 1, D_SGU:2 * D_SGU] = lnb_ref[...]
        o_ref[ROW_MISC:ROW_MISC + 1, 0:128] = loss_ref[...]
        o_ref[ROW_MISC:ROW_MISC + 1, 128:256] = sink_ref[...]
        o_ref[ROW_SGU_B:ROW_SGU_B + SGU_GROUPS, 0:BLOCK] = b_ref[...]

    return pl.pallas_call(
        body, name="pack_small", out_shape=jax.ShapeDtypeStruct((SMALL_ROWS, D_MODEL), F32),
        compiler_params=_params(),
    )(d_shift, d_scale, d_gate, d_norm_g, d_final_g, d_ln_g, d_ln_b, loss, d_sinks, d_sgu_b)


_SMALL_NAMES = ("norm_g", "b_ada", "attn_sinks", "sgu_ln_g", "sgu_ln_b", "sgu_w", "sgu_b", "final_g")


def _adam_small(partials, d_sgu_w_all, weights, moments_m, moments_v):
    names = _SMALL_NAMES
    k = len(names)

    def body(*refs):
        p_ref, sw_ref = refs[0], refs[1]
        w_refs, m_refs, v_refs = refs[2:2 + k], refs[2 + k:2 + 2 * k], refs[2 + 2 * k:2 + 3 * k]
        loss_ref, dmod_ref = refs[2 + 3 * k], refs[3 + 3 * k]
        out_refs = refs[4 + 3 * k:4 + 7 * k]
        sum_ref = refs[4 + 7 * k]
        total = p_ref[0]
        for j in range(1, N_DEV):
            total = total + p_ref[j]
        sum_ref[...] = total
        for j in range(N_DEV):
            for part, row in enumerate((ROW_SHIFT, ROW_SCALE, ROW_GATE)):
                dmod_ref[j:j + 1, part * D_MODEL:(part + 1) * D_MODEL] = p_ref[j, row:row + 1, :]
        loss_ref[...] = sum_ref[ROW_MISC:ROW_MISC + 1, 0:1]
        d_sgu_w = sw_ref[0]
        for j in range(1, N_DEV):
            d_sgu_w = d_sgu_w + sw_ref[j]
        grads = {
            "norm_g": sum_ref[ROW_NORM_G:ROW_NORM_G + 1, :],
            "b_ada": jnp.concatenate([sum_ref[r:r + 1, :] for r in (ROW_SHIFT, ROW_SCALE, ROW_GATE)], axis=1),
            "attn_sinks": sum_ref[ROW_MISC:ROW_MISC + 1, 128:128 + N_Q_HEADS],
            "sgu_ln_g": sum_ref[ROW_LN:ROW_LN + 1, 0:D_SGU],
            "sgu_ln_b": sum_ref[ROW_LN:ROW_LN + 1, D_SGU:2 * D_SGU],
            "sgu_w": d_sgu_w[None],
            "sgu_b": sum_ref[ROW_SGU_B:ROW_SGU_B + SGU_GROUPS, 0:BLOCK][None],
            "final_g": sum_ref[ROW_FINAL_G:ROW_FINAL_G + 1, :],
        }
        for i, name in enumerate(names):
            g = grads[name]
            delta, m, v = _adamw(w_refs[i][...], g, m_refs[i][...], v_refs[i][...])
            out_refs[4 * i][...] = g
            out_refs[4 * i + 1][...] = delta
            out_refs[4 * i + 2][...] = m
            out_refs[4 * i + 3][...] = v

    shapes = [jax.ShapeDtypeStruct((1, 1), F32), jax.ShapeDtypeStruct((N_DEV, 3 * D_MODEL), F32)]
    for name in names:
        shapes += [jax.ShapeDtypeStruct(weights[name].shape, F32)] * 4
    outs = pl.pallas_call(
        body, name="adam_small", out_shape=tuple(shapes),
        scratch_shapes=[pltpu.VMEM((SMALL_ROWS, D_MODEL), F32)],
        compiler_params=_params(),
    )(partials, d_sgu_w_all, *[weights[n] for n in names], *[moments_m[n] for n in names],
      *[moments_v[n] for n in names])
    return outs[0], outs[1], {name: outs[2 + 4 * i:6 + 4 * i] for i, name in enumerate(names)}


def kernel(x, c, norm_g, w_ada, b_ada, w_in, attn_sinks, sgu_ln_g, sgu_ln_b, sgu_w, sgu_b, w_out, final_g, loss_target, m_norm_g, m_w_ada, m_b_ada, m_w_in, m_attn_sinks, m_sgu_ln_g, m_sgu_ln_b, m_sgu_w, m_sgu_b, m_w_out, m_final_g, v_norm_g, v_w_ada, v_b_ada, v_w_in, v_attn_sinks, v_sgu_ln_g, v_sgu_ln_b, v_sgu_w, v_sgu_b, v_w_out, v_final_g):
    xi, yi, ci = _place()
    me = 4 * xi + 2 * yi + ci
    x2d, target = x[0], loss_target[0]
    t = x2d.shape[0]

    core = ci.astype(jnp.int32).reshape(1)
    chip = (2 * xi + yi).astype(jnp.int32).reshape(1)

    first = _own_block_copies(_first_targets)
    first_flight = _start_copies([_with_own_slot(w_in[0].T.astype(BF16), me)], first, 2, core, "gather_w_in_start")

    c_all = _all_gather_small(c.reshape(8, 256) + first_flight[3][0, 0], "gather_c").reshape(N_DEV, D_MODEL)
    device = me.astype(jnp.int32).reshape(1)
    c_act, mod_part = _modulation(device, c_all, w_ada[0], b_ada)
    mod_all = _all_gather_small(mod_part, "gather_mod")

    across = _wait_then_start(first_flight, lambda *a: first(*a)[1:], _second_axis_stage_copies, 3, mod_all,
                              "gather_w_in_second_axis_stage")
    mod = lax.dynamic_index_in_dim(mod_all, me, axis=1, keepdims=False).reshape(1, 3 * D_MODEL)
    mod = mod + across[3][0, 0]

    w_in_pair = _wait_copies((first_flight[0], first_flight[1], across[2], None), lambda *a: first(*a)[:1], mod,
                             "gather_w_in_sibling_wait")
    h, z_own = _norm_z_proj_own(x2d, norm_g, mod, w_in_pair[0].reshape(D_IN, D_MODEL), chip)
    w_out_early = _own_block_copies(lambda x, y, c: [(x, y, 1 - c), (*_second_axis_chip(x, y, c), c)])
    w_out_late = _own_block_copies(lambda x, y, c: [(*_first_axis_chip(x, y, c), c), (1 - x, 1 - y, c)])
    forward = _wait_then_start(
        (across[0], across[1], w_in_pair, None), lambda *a: _second_axis_stage_copies(*a)[:1],
        lambda refs, s, r: _second_axis_forward_copies(refs[:1], s, r) + _group(w_out_early, 1, 1, 1)(refs, s, r),
        3, z_own, "gather_w_in_second_axis_forward", more_bufs=[_with_own_slot(w_out[0].astype(BF16), me)])
    w_in_most = _wait_copies((across[0], across[1], forward[2][:1], None),
                             lambda *a: _second_axis_stage_copies(*a)[1:2], z_own, "gather_w_in_first_forward_wait")
    w_in_most = _wait_copies((forward[0], forward[1], w_in_most, None), _second_axis_forward_copies, z_own,
                             "gather_w_in_second_forward_wait")
    z_early = _z_proj(h, w_in_most[0].reshape(D_IN, D_MODEL), chip, 1, _Z_EARLY_TILES - 1, z_own, "z_proj_early")
    last = _wait_then_start(
        (across[0], across[1], [w_in_most[0], forward[2][1]], None), lambda *a: _second_axis_stage_copies(*a)[2:],
        lambda refs, s, r: _diagonal_forward_copies(refs[:1], s, r) + _group(w_out_late, 1, 1, 1)(refs, s, r),
        3, z_early, "gather_w_in_last_stage")
    w_in_all = _wait_copies((last[0], last[1], last[2][:1], None), _diagonal_forward_copies, z_early,
                            "gather_w_in_last_wait")[0]
    w_in_t = w_in_all.reshape(D_IN, D_MODEL)
    z = _z_proj(h, w_in_t, chip, _Z_EARLY_TILES, 7 - _Z_EARLY_TILES, z_early, "z_proj_late")
    w_out_half = _wait_copies((forward[0], forward[1], last[2][1:], None), _group(w_out_early, 0, 1, 1), z,
                              "gather_w_out_early_wait")
    w_out_flight = _wait_then_start((last[0], last[1], w_out_half, None), _group(w_out_late, 0, 1, 1),
                                    _forward_copies, 3, z, "gather_w_out_forward_stage")
    sink_rows = jnp.repeat(attn_sinks.reshape(N_Q_HEADS), BLOCK).reshape(2, 1, 8 * BLOCK)
    sgu_bt = sgu_b[0].T
    a, probs, sink_probs = _mixer_fwd(z, sink_rows + w_out_flight[3][0, 0], sgu_ln_g, sgu_ln_b, sgu_w[0], sgu_bt)
    w_out_all = _wait_copies(w_out_flight, _forward_copies, a, "gather_w_out_forward_wait")[0]
    w_out_full = w_out_all.reshape(D_MODEL, D_MODEL)
    final_g_row = final_g.reshape(1, D_MODEL)
    dx2, dy, loss_part, d_final_g, d_gate = _out_proj_head(a, w_out_full, x2d, target, mod, final_g_row)

    da = _matmul(dy, w_out_full, "nt", F32, min(t, 1024), 1024, "out_proj_bwd")
    dw_out = _matmul(a, dy, "tn", BF16, 1024, 1024, "w_out_grad").reshape(4, 2, W_OUT_SHARD, D_MODEL)
    pair_out = _pair_reduce(dw_out, _every_chip, "w_out_grad_pair_reduce", W_OUT_SHARD // 2)
    dz, d_sinks, d_sgu_w, d_sgu_b, d_ln_g, d_ln_b = _mixer_bwd(
        z, da, probs, sink_probs, sgu_ln_g, sgu_ln_b, sgu_w[0], jnp.swapaxes(sgu_w[0], 1, 2), sgu_bt)
    sgu_w_to_all = _group(_own_block_copies(_all_others), 2, 1, 3)
    both = _start_copies(
        [pair_out, lax.empty((3, W_OUT_SHARD, D_MODEL), BF16), _with_own_slot(d_sgu_w, me)],
        lambda refs, s, r: _chip_copies(refs[:2], s, r) + sgu_w_to_all(refs, s, r), 3 + N_DEV - 1, core,
        "w_out_grad_chip_and_sgu_w_gather_start")
    out_flight, sgu_w_flight = (both[0], both[1], both[2][:2], None), (both[0], both[1], both[2][2:], None)
    dw_in_t = _matmul(dz, h, "tn", BF16, 768, D_MODEL, "w_in_grad", dep=both[3])
    pair_in = _pair_reduce(dw_in_t.reshape(4, 2, W_IN_SHARD, D_MODEL), _first_hop_chips, "w_in_grad_pair_reduce",
                           W_IN_SHARD // 3)
    first_hop = lambda refs, s, r: _first_hop_copies(refs[:2], s, r) + _group(_late_pair_copies, 2, 2, 2)(refs, s, r)
    hop1 = _start_copies(
        [pair_in, lax.empty((2, W_IN_SHARD, D_MODEL), BF16), dw_in_t.reshape(N_DEV, W_IN_SHARD, D_MODEL),
         lax.empty((2, W_IN_SHARD, D_MODEL), BF16)], first_hop, 4, core, "w_in_grad_first_hop_start")
    grad_x, d_shift, d_scale, d_norm_g = _z_proj_bwd_norm(dz, w_in_t, x2d, dx2, norm_g, mod, hop1[3])

    partial = _pack_small(d_shift, d_scale, d_gate, d_norm_g, d_final_g, d_ln_g, d_ln_b, loss_part, d_sinks, d_sgu_b)
    small_flight = _start_copies([_with_own_slot(partial, me)], _own_block_copies(_all_others), N_DEV - 1, core,
                                 "small_grad_gather_start")
    _, land_first, dw_in_t, land_pair = _wait_copies(hop1, first_hop, small_flight[3], "w_in_grad_first_hop_wait")
    second_device = (4 * ((xi + ci) % 2) + 2 * ((yi + 1 - ci) % 2) + ci).astype(jnp.int32).reshape(1)
    relay = _relay_sum(second_device, dw_in_t, land_pair, land_first, W_IN_SHARD // 3)
    hop2 = _start_copies([relay, lax.empty((1, W_IN_SHARD, D_MODEL), BF16)], _second_hop_copies, 1, core,
                         "w_in_grad_second_hop_start")
    pair_out, land_out = _wait_copies(out_flight, _chip_copies, hop2[3], "w_out_grad_chip_wait")
    big = {"w_out": _adam_from_chips(chip, pair_out, [(land_out, k) for k in range(3)], w_out[0], m_w_out[0],
                                     v_w_out[0], "adam_w_out", 512)}
    partial_all = _wait_copies(small_flight, _own_block_copies(_all_others), big["w_out"][0],
                               "small_grad_gather_wait")[0]
    d_sgu_w_all = _wait_copies(sgu_w_flight, _group(_own_block_copies(_all_others), 0, 1, 3), partial_all,
                               "sgu_w_grad_gather_wait")[0]
    weights = {"norm_g": norm_g, "b_ada": b_ada, "attn_sinks": attn_sinks, "sgu_ln_g": sgu_ln_g,
               "sgu_ln_b": sgu_ln_b, "sgu_w": sgu_w, "sgu_b": sgu_b, "final_g": final_g_row}
    moments_m = {"norm_g": m_norm_g, "b_ada": m_b_ada, "attn_sinks": m_attn_sinks, "sgu_ln_g": m_sgu_ln_g,
                 "sgu_ln_b": m_sgu_ln_b, "sgu_w": m_sgu_w, "sgu_b": m_sgu_b,
                 "final_g": m_final_g.reshape(1, D_MODEL)}
    moments_v = {"norm_g": v_norm_g, "b_ada": v_b_ada, "attn_sinks": v_attn_sinks, "sgu_ln_g": v_sgu_ln_g,
                 "sgu_ln_b": v_sgu_ln_b, "sgu_w": v_sgu_w, "sgu_b": v_sgu_b,
                 "final_g": v_final_g.reshape(1, D_MODEL)}
    loss, dmod_all, small = _adam_small(partial_all, d_sgu_w_all, weights, moments_m, moments_v)
    small["final_g"] = tuple(o.reshape(D_MODEL) for o in small["final_g"])

    big["w_ada"] = _adam_w_ada(device, c_act.T, dmod_all, w_ada[0], m_w_ada[0], v_w_ada[0])
    _, land_second = _wait_copies(hop2, _second_hop_copies, big["w_ada"][0], "w_in_grad_second_hop_wait")
    big["w_in"] = tuple(o.T for o in _adam_from_chips(
        device, dw_in_t, [(land_pair, 0), (land_first, 0), (land_second, 0)], w_in[0].T, m_w_in[0].T, v_w_in[0].T,
        "adam_w_in", 256))
    order = ["norm_g", "w_ada", "b_ada", "w_in", "attn_sinks", "sgu_ln_g", "sgu_ln_b", "sgu_w", "sgu_b", "w_out",
             "final_g"]
    outs = [loss.reshape(()), grad_x[None]]
    for k in range(4):
        for name in order:
            outs.append(big[name][k][None] if name in big else small[name][k])
    return tuple(outs)
```

```python
import jax
import jax.numpy as jnp
from jax import lax
from jax.experimental import pallas as pl
from jax.experimental.pallas import tpu as pltpu

F32 = jnp.float32
BF16 = jnp.bfloat16
MESH = pl.DeviceIdType.MESH

N_DEV = 8
D_MODEL = 2048
HEAD_DIM = 64
D_ATTN = 1024
N_Q_HEADS = 16
D_KV = 128
BLOCK = 128
D_SGU = 1024
SGU_GROUPS = 8
D_IN = 5376
W_IN_SHARD = D_IN // N_DEV
W_OUT_SHARD = D_MODEL // N_DEV
W_ADA_SHARD = 3 * D_MODEL // N_DEV
EPS = 1e-6
ATTN_SCALE = 0.125

ADAM_LR = 0.001
ADAM_B1 = 0.9
ADAM_B2 = 0.999
ADAM_EPS = 1e-08
ADAM_WD = 0.01
ADAM_STEP = 10

SEG_Q, SEG_KV, SEG_GA, SEG_U, SEG_VS, SEG_GS = 0, 1024, 1280, 2304, 3328, 4352

VMEM_LIMIT = 56 * 1024 * 1024

ROW_SHIFT, ROW_SCALE, ROW_GATE, ROW_NORM_G, ROW_FINAL_G, ROW_LN, ROW_MISC, ROW_SGU_B = 0, 1, 2, 3, 4, 5, 6, 8
SMALL_ROWS = 16


def _params(**kw):
    return pltpu.CompilerParams(vmem_limit_bytes=VMEM_LIMIT, **kw)


def _sigmoid(x):
    return 0.5 * (jnp.tanh(0.5 * x) + 1.0)


def _place():
    return lax.axis_index("x"), lax.axis_index("y"), lax.axis_index("c")


def _every_chip(x, y, c):
    return [0, 1, 2, 3]


def _first_hop_chips(x, y, c):
    first = _first_axis_chip(x, y, c)
    return [2 * first[0] + first[1], 2 * (1 - x) + (1 - y)]


def _pair_reduce(blocks, chips, name, row_chunk):
    _, _, r, cols = blocks.shape
    n = len(chips(0, 0, 0))
    assert r % row_chunk == 0

    def body(in_ref, out_ref, land, own, summed, send_sems, recv_sems, own_sems, out_sems):
        x, y, c = _place()
        sends, loads, stores = [], [], []
        for m in range(n):
            cp = pltpu.make_async_remote_copy(
                src_ref=in_ref.at[chips(x, y, 1 - c)[m], 1 - c], dst_ref=land.at[m], send_sem=send_sems.at[m],
                recv_sem=recv_sems.at[m], device_id=(x, y, 1 - c), device_id_type=MESH)
            cp.start()
            sends.append(cp)
            ld = pltpu.make_async_copy(in_ref.at[chips(x, y, c)[m], c], own.at[m], own_sems.at[m])
            ld.start()
            loads.append(ld)
        for m in range(n):
            sends[m].wait_recv()
            loads[m].wait()
            for k in range(r // row_chunk):
                rows = slice(k * row_chunk, (k + 1) * row_chunk)
                summed[m, rows, :] = (own[m, rows, :].astype(F32) + land[m, rows, :].astype(F32)).astype(BF16)
            st = pltpu.make_async_copy(summed.at[m], out_ref.at[m], out_sems.at[m])
            st.start()
            stores.append(st)
        for m in range(n):
            sends[m].wait_send()
            stores[m].wait()

    spec = pl.BlockSpec(memory_space=pl.ANY)
    return pl.pallas_call(
        body, name=name, out_shape=jax.ShapeDtypeStruct((n, r, cols), BF16),
        in_specs=[spec], out_specs=spec,
        scratch_shapes=[pltpu.VMEM((n, r, cols), BF16), pltpu.VMEM((n, r, cols), BF16), pltpu.VMEM((n, r, cols), BF16),
                        pltpu.SemaphoreType.DMA((n,)), pltpu.SemaphoreType.DMA((n,)), pltpu.SemaphoreType.DMA((n,)),
                        pltpu.SemaphoreType.DMA((n,))],
        compiler_params=_params(),
    )(blocks)


_HBM = pl.BlockSpec(memory_space=pltpu.HBM)
_SEM = pl.BlockSpec(memory_space=pltpu.SEMAPHORE)
_EFFECT = pltpu.SideEffectType.DATAFLOW_SIDE_EFFECTING


def _start_copies(bufs, copies, n_copies, after, name):
    nb = len(bufs)

    def body(*refs):
        for cp in copies(refs[:nb], refs[nb + 1], refs[nb + 2]):
            cp.start()
        refs[-1][...] = jnp.zeros_like(refs[-1])

    out = pl.pallas_call(
        body, name=name,
        out_shape=(pltpu.SemaphoreType.DMA((n_copies,)), pltpu.SemaphoreType.DMA((n_copies,)),
                   *[pltpu.HBM(b.shape, b.dtype) for b in bufs], jax.ShapeDtypeStruct((8, 128), F32)),
        in_specs=(_HBM,) * nb + (pl.BlockSpec(memory_space=pl.ANY),),
        out_specs=(_SEM, _SEM) + (_HBM,) * nb + (pl.BlockSpec(memory_space=pltpu.VMEM),),
        input_output_aliases={i: 2 + i for i in range(nb)},
        compiler_params=pltpu.CompilerParams(has_side_effects=_EFFECT),
    )(*[pltpu.with_memory_space_constraint(b, pltpu.HBM) for b in bufs], after)
    return out[0], out[1], list(out[2:2 + nb]), out[-1]


def _wait_copies(flight, copies, after, name):
    send_sems, recv_sems, bufs, _ = flight
    nb = len(bufs)

    def body(*refs):
        for cp in copies(refs[:nb], refs[nb], refs[nb + 1]):
            cp.wait_send()
            cp.wait_recv()

    return pl.pallas_call(
        body, name=name,
        out_shape=tuple(pltpu.HBM(b.shape, b.dtype) for b in bufs),
        in_specs=(_HBM,) * nb + (_SEM, _SEM, pl.BlockSpec(memory_space=pl.ANY)), out_specs=(_HBM,) * nb,
        input_output_aliases={i: i for i in range(nb)},
        compiler_params=pltpu.CompilerParams(has_side_effects=_EFFECT),
    )(*bufs, send_sems, recv_sems, after)


class _From:
    def __init__(self, sems, offset):
        self.sems, self.offset = sems, offset

    @property
    def at(self):
        return self

    def __getitem__(self, k):
        return self.sems.at[k + self.offset]


def _group(copies, first_buf, n_bufs, offset):
    def grouped(refs, send_sems, recv_sems):
        return copies(refs[first_buf:first_buf + n_bufs], _From(send_sems, offset), _From(recv_sems, offset))
    return grouped


def _wait_then_start(flight, waited, started, n_started, after, name, more_bufs=()):
    old_send, old_recv, bufs, _ = flight
    bufs = list(bufs) + [pltpu.with_memory_space_constraint(b, pltpu.HBM) for b in more_bufs]
    nb = len(bufs)

    def body(*refs):
        for cp in waited(refs[:nb], refs[nb], refs[nb + 1]):
            cp.wait_send()
            cp.wait_recv()
        for cp in started(refs[:nb], refs[nb + 3], refs[nb + 4]):
            cp.start()
        refs[-1][...] = jnp.zeros_like(refs[-1])

    out = pl.pallas_call(
        body, name=name,
        out_shape=(pltpu.SemaphoreType.DMA((n_started,)), pltpu.SemaphoreType.DMA((n_started,)),
                   *[pltpu.HBM(b.shape, b.dtype) for b in bufs], jax.ShapeDtypeStruct((8, 128), F32)),
        in_specs=(_HBM,) * nb + (_SEM, _SEM, pl.BlockSpec(memory_space=pl.ANY)),
        out_specs=(_SEM, _SEM) + (_HBM,) * nb + (pl.BlockSpec(memory_space=pltpu.VMEM),),
        input_output_aliases={i: 2 + i for i in range(nb)},
        compiler_params=pltpu.CompilerParams(has_side_effects=_EFFECT),
    )(*bufs, old_send, old_recv, after)
    return out[0], out[1], list(out[2:2 + nb]), out[-1]


def _late_pair_copies(refs, send_sems, recv_sems):
    blocks_ref, land_ref = refs
    x, y, c = _place()
    first = _first_axis_chip(x, y, c)
    devices = [4 * x + 2 * y + 1 - c, 4 * first[0] + 2 * first[1] + 1 - c]
    return [pltpu.make_async_remote_copy(
        src_ref=blocks_ref.at[devices[k]], dst_ref=land_ref.at[k], send_sem=send_sems.at[k], recv_sem=recv_sems.at[k],
        device_id=(x, y, 1 - c), device_id_type=MESH) for k in range(2)]


def _chip_copies(refs, send_sems, recv_sems):
    pair_ref, land_ref = refs
    x, y, c = _place()
    chips = [(1 - x, y), (x, 1 - y), (1 - x, 1 - y)]
    return [pltpu.make_async_remote_copy(
        src_ref=pair_ref.at[2 * chip[0] + chip[1]], dst_ref=land_ref.at[k],
        send_sem=send_sems.at[k], recv_sem=recv_sems.at[k],
        device_id=(*chip, c), device_id_type=MESH) for k, chip in enumerate(chips)]


def _first_hop_copies(refs, send_sems, recv_sems):
    pair_ref, land_ref = refs
    x, y, c = _place()
    return [pltpu.make_async_remote_copy(
        src_ref=pair_ref.at[k], dst_ref=land_ref.at[k], send_sem=send_sems.at[k], recv_sem=recv_sems.at[k],
        device_id=(*_first_axis_chip(x, y, c), c), device_id_type=MESH) for k in range(2)]


def _second_hop_copies(refs, send_sems, recv_sems):
    relay_ref, land_ref = refs
    x, y, c = _place()
    second = ((x + c) % 2, (y + 1 - c) % 2)
    return [pltpu.make_async_remote_copy(
        src_ref=relay_ref, dst_ref=land_ref.at[0], send_sem=send_sems.at[0], recv_sem=recv_sems.at[0],
        device_id=(*second, c), device_id_type=MESH)]


def _own_block_copies(targets):
    def copies(refs, send_sems, recv_sems):
        x, y, c = _place()
        mine = refs[0].at[4 * x + 2 * y + c]
        return [pltpu.make_async_remote_copy(
            src_ref=mine, dst_ref=mine, send_sem=send_sems.at[k], recv_sem=recv_sems.at[k],
            device_id=to, device_id_type=MESH) for k, to in enumerate(targets(x, y, c))]
    return copies


def _all_others(x, y, c):
    flip = lambda v, f: 1 - v if f else v
    return [(flip(x, r & 4), flip(y, r & 2), flip(c, r & 1)) for r in range(1, N_DEV)]


def _forward_copies(refs, send_sems, recv_sems):
    x, y, c = _place()
    chips = [(1 - x, y), (x, 1 - y), (1 - x, 1 - y)]
    return [pltpu.make_async_remote_copy(
        src_ref=refs[0].at[4 * chip[0] + 2 * chip[1] + c], dst_ref=refs[0].at[4 * chip[0] + 2 * chip[1] + c],
        send_sem=send_sems.at[k], recv_sem=recv_sems.at[k],
        device_id=(x, y, 1 - c), device_id_type=MESH) for k, chip in enumerate(chips)]


def _first_axis_chip(x, y, c):
    return (x + 1 - c) % 2, (y + c) % 2


def _second_axis_chip(x, y, c):
    return (x + c) % 2, (y + 1 - c) % 2


def _first_targets(x, y, c):
    return [(x, y, 1 - c), (*_first_axis_chip(x, y, c), c)]


def _all_gather_small(shard, name):
    def body(in_ref, out_ref, send_sems, recv_sems, local_sem):
        x, y, c = _place()
        me, sibling = 4 * x + 2 * y + c, (x, y, 1 - c)
        first, second = _first_axis_chip(x, y, c), _second_axis_chip(x, y, c)

        def pair(chip):
            return out_ref.at[pl.ds(2 * (2 * chip[0] + chip[1]), 2)]

        def exchange(k, src, dst, to):
            cp = pltpu.make_async_remote_copy(src_ref=src, dst_ref=dst, send_sem=send_sems.at[k],
                                              recv_sem=recv_sems.at[k], device_id=to, device_id_type=MESH)
            cp.start()
            cp.wait()

        own = pltpu.make_async_copy(in_ref, out_ref.at[me], local_sem)
        own.start()
        exchange(0, in_ref, out_ref.at[me], sibling)
        own.wait()
        exchange(1, pair((x, y)), pair((x, y)), (*second, c))
        exchange(2, pair(second), pair(second), sibling)
        exchange(3, pair(first), pair(first), (*second, c))

    spec = pl.BlockSpec(memory_space=pltpu.VMEM)
    return pl.pallas_call(
        body, name=name, out_shape=jax.ShapeDtypeStruct((N_DEV,) + shard.shape, shard.dtype),
        in_specs=[spec], out_specs=spec,
        scratch_shapes=[pltpu.SemaphoreType.DMA((4,)), pltpu.SemaphoreType.DMA((4,)), pltpu.SemaphoreType.DMA],
        compiler_params=_params(),
    )(shard)


def _slot_copies(refs, send_sems, recv_sems, plan):
    copies = []
    for k, ((px, py, pc), to) in enumerate(plan):
        blk = refs[0].at[4 * px + 2 * py + pc]
        copies.append(pltpu.make_async_remote_copy(
            src_ref=blk, dst_ref=blk, send_sem=send_sems.at[k], recv_sem=recv_sems.at[k],
            device_id=to, device_id_type=MESH))
    return copies


def _second_axis_stage_copies(refs, send_sems, recv_sems):
    x, y, c = _place()
    first, second = (*_first_axis_chip(x, y, c), c), (*_second_axis_chip(x, y, c), c)
    return _slot_copies(refs, send_sems, recv_sems, [((x, y, c), second), (first, (x, y, 1 - c)), (first, second)])


def _second_axis_forward_copies(refs, send_sems, recv_sems):
    x, y, c = _place()
    return _slot_copies(refs, send_sems, recv_sems, [((*_second_axis_chip(x, y, c), c), (x, y, 1 - c))])


def _diagonal_forward_copies(refs, send_sems, recv_sems):
    x, y, c = _place()
    blk = refs[0].at[4 * (1 - x) + 2 * (1 - y) + c]
    return [pltpu.make_async_remote_copy(
        src_ref=blk, dst_ref=blk, send_sem=send_sems.at[0], recv_sem=recv_sems.at[0],
        device_id=(x, y, 1 - c), device_id_type=MESH)]


def _with_own_slot(block, me):
    return lax.dynamic_update_index_in_dim(lax.empty((N_DEV,) + block.shape, block.dtype), block, me, 0)


def _matmul(a, b, dims, out_dtype, tm, tn, name, dep=None):
    if dims == "nn":
        (m, k), n = a.shape, b.shape[1]
        a_spec = pl.BlockSpec((tm, k), lambda i, j: (i, 0))
        b_spec = pl.BlockSpec((k, tn), lambda i, j: (0, j))
        contract = ((1,), (0,))
    elif dims == "nt":
        (m, k), n = a.shape, b.shape[0]
        a_spec = pl.BlockSpec((tm, k), lambda i, j: (i, 0))
        b_spec = pl.BlockSpec((tn, k), lambda i, j: (j, 0))
        contract = ((1,), (1,))
    else:
        (k, m), n = a.shape, b.shape[1]
        a_spec = pl.BlockSpec((k, tm), lambda i, j: (0, i))
        b_spec = pl.BlockSpec((k, tn), lambda i, j: (0, j))
        contract = ((0,), (0,))
    assert m % tm == 0 and n % tn == 0 and a.dtype == BF16 and b.dtype == BF16

    def body(a_ref, b_ref, *rest):
        rest[-1][...] = lax.dot_general(a_ref[...], b_ref[...], (contract, ((), ())),
                                        preferred_element_type=F32).astype(out_dtype)

    deps = [] if dep is None else [dep]
    return pl.pallas_call(
        body, name=name, grid=(m // tm, n // tn),
        in_specs=[a_spec, b_spec] + [pl.BlockSpec((8, 128), lambda i, j: (0, 0))] * len(deps),
        out_specs=pl.BlockSpec((tm, tn), lambda i, j: (i, j)),
        out_shape=jax.ShapeDtypeStruct((m, n), out_dtype),
        compiler_params=_params(dimension_semantics=("arbitrary", "arbitrary")),
    )(a, b, *deps)


Z_TILE = 768
_Z_TILE_ORDER = ((0, 1, 2, 3, 4, 5, 6), (2, 0, 1, 6, 3, 4, 5), (4, 0, 5, 6, 1, 2, 3), (6, 2, 3, 4, 0, 1, 5))
_Z_EARLY_TILES = 4


def _z_proj(h, w_in_t, chip, first, count, z_prev, name, tr=1024):
    t = h.shape[0]

    def body(chip_ref, h_ref, w_ref, z_prev_ref, z_ref):
        z_ref[...] = _dot_nt(h_ref[...], w_ref[...])

    def tile(j, chip_ref):
        picked = 0
        for c, order in enumerate(_Z_TILE_ORDER):
            for k in range(count):
                picked = picked + jnp.where((chip_ref[0] == c) & (j == k), order[first + k], 0)
        return picked

    return pl.pallas_call(
        body, name=name,
        grid_spec=pltpu.PrefetchScalarGridSpec(
            num_scalar_prefetch=1, grid=(count, t // tr),
            in_specs=[pl.BlockSpec((tr, D_MODEL), lambda j, i, o: (i, 0)),
                      pl.BlockSpec((Z_TILE, D_MODEL), lambda j, i, o: (tile(j, o), 0)),
                      pl.BlockSpec(memory_space=pl.ANY)],
            out_specs=pl.BlockSpec((tr, Z_TILE), lambda j, i, o: (i, tile(j, o)))),
        out_shape=jax.ShapeDtypeStruct((t, D_IN), F32),
        input_output_aliases={3: 0},
        compiler_params=_params(dimension_semantics=("arbitrary", "arbitrary")),
    )(chip, h, w_in_t, z_prev)


def _modulation(device, c_all, w_ada, b_ada):
    def body(device_ref, c_ref, w_ref, b_ref, act_ref, mod_ref):
        cv = c_ref[...]
        act = cv * _sigmoid(cv)
        act_ref[...] = act
        mod_ref[...] = jnp.dot(act.astype(BF16), w_ref[...].astype(BF16), preferred_element_type=F32) + b_ref[...]

    whole = lambda a: pl.BlockSpec(a.shape, lambda i, device_ref: (0,) * a.ndim)
    return pl.pallas_call(
        body, name="modulation",
        grid_spec=pltpu.PrefetchScalarGridSpec(
            num_scalar_prefetch=1, grid=(1,),
            in_specs=[whole(c_all), whole(w_ada), pl.BlockSpec((1, W_ADA_SHARD), lambda i, device_ref: (0, device_ref[0]))],
            out_specs=(whole(c_all), pl.BlockSpec((N_DEV, W_ADA_SHARD), lambda i, device_ref: (0, 0)))),
        out_shape=(jax.ShapeDtypeStruct(c_all.shape, F32), jax.ShapeDtypeStruct((N_DEV, W_ADA_SHARD), F32)),
        compiler_params=_params(dimension_semantics=("arbitrary",)),
    )(device, c_all, w_ada, b_ada)


MOD_SHIFT, MOD_SCALE, MOD_GATE = 0, 1, 2


def _mod_spec(part, d):
    return pl.BlockSpec((1, d), lambda i: (0, part))


def _norm_z_proj_own(x, norm_g, mod, w_in_t, chip, tm=512):
    t, d = x.shape

    def body(chip_ref, x_ref, g_ref, sc_ref, sh_ref, w_ref, h_ref, z_ref):
        xv = x_ref[...]
        r = lax.rsqrt(jnp.mean(xv * xv, axis=-1, keepdims=True) + EPS)
        h = ((xv * r) * g_ref[...] * (1.0 + sc_ref[...]) + sh_ref[...]).astype(BF16)
        h_ref[...] = h
        z_ref[...] = _dot_nt(h, w_ref[...])

    def own_tile(chip_ref):
        picked = 0
        for c, order in enumerate(_Z_TILE_ORDER):
            picked = picked + jnp.where(chip_ref[0] == c, order[0], 0)
        return picked

    def row(part):
        return pl.BlockSpec((1, d), lambda i, o: (0, part))

    return pl.pallas_call(
        body, name="norm_z_proj_own",
        grid_spec=pltpu.PrefetchScalarGridSpec(
            num_scalar_prefetch=1, grid=(t // tm,),
            in_specs=[pl.BlockSpec((tm, d), lambda i, o: (i, 0)), row(0), row(MOD_SCALE), row(MOD_SHIFT),
                      pl.BlockSpec((Z_TILE, d), lambda i, o: (own_tile(o), 0))],
            out_specs=(pl.BlockSpec((tm, d), lambda i, o: (i, 0)),
                       pl.BlockSpec((tm, Z_TILE), lambda i, o: (i, own_tile(o))))),
        out_shape=(jax.ShapeDtypeStruct((t, d), BF16), jax.ShapeDtypeStruct((t, D_IN), F32)),
        compiler_params=_params(dimension_semantics=("arbitrary",)),
    )(chip, x, norm_g, mod, mod, w_in_t)


def _window_bias(block_index):
    s = lax.broadcasted_iota(jnp.int32, (2 * BLOCK, BLOCK), 0)
    t = lax.broadcasted_iota(jnp.int32, (2 * BLOCK, BLOCK), 1)
    valid = ((s < BLOCK) & (s > t) & (block_index > 0)) | ((s >= BLOCK) & ((s - BLOCK) <= t))
    bias = jnp.where(valid, 0.0, -jnp.inf).astype(F32)
    return jnp.concatenate([bias] * 8, axis=1)


def _heads_t(pair_blocks, g):
    top = lax.broadcasted_iota(jnp.int32, (BLOCK, BLOCK), 0) < HEAD_DIM
    zeros = jnp.zeros((HEAD_DIM, BLOCK), F32)
    tiles = []
    for blk in pair_blocks:
        tp = blk.T
        if g == 0:
            tiles += [jnp.where(top, tp, 0.0), jnp.concatenate([tp[HEAD_DIM:], zeros], axis=0)]
        else:
            tiles += [jnp.concatenate([zeros, tp[:HEAD_DIM]], axis=0), jnp.where(top, 0.0, tp)]
    return jnp.concatenate(tiles, axis=1)


def _pair_block(xt, p, g):
    r0 = HEAD_DIM * g
    even = xt[r0:r0 + HEAD_DIM, (2 * p) * BLOCK:(2 * p + 1) * BLOCK]
    odd = xt[r0:r0 + HEAD_DIM, (2 * p + 1) * BLOCK:(2 * p + 2) * BLOCK]
    return jnp.concatenate([even, odd], axis=0).T


def _softmax_t(scores_t, bias, sink):
    st = scores_t + bias
    m = jnp.maximum(jnp.max(st, axis=0, keepdims=True), sink)
    e = jnp.exp(st - m)
    es = jnp.exp(sink - m)
    inv = 1.0 / (jnp.sum(e, axis=0, keepdims=True) + es)
    return e * inv, es * inv


def _dot(a, b):
    return jnp.dot(a, b, preferred_element_type=F32)


def _dot_nt(a, b):
    return lax.dot_general(a, b, (((1,), (1,)), ((), ())), preferred_element_type=F32)


def _layer_norm_fwd(v):
    mu = jnp.mean(v, axis=-1, keepdims=True)
    xc = v - mu
    rstd = lax.rsqrt(jnp.mean(xc * xc, axis=-1, keepdims=True) + EPS)
    return xc * rstd, rstd


def _tril(transposed=False):
    t = lax.broadcasted_iota(jnp.int32, (BLOCK, BLOCK), 0)
    s = lax.broadcasted_iota(jnp.int32, (BLOCK, BLOCK), 1)
    return s >= t if transposed else t >= s


def _const_spec(shape):
    return pl.BlockSpec(shape, lambda i: (0,) * len(shape))


def _keys_values(z_ref, kvp):
    kvc = z_ref[:, SEG_KV:SEG_KV + 2 * D_KV]
    kk = jnp.concatenate([kvp[:, :D_KV], kvc[:, :D_KV]], axis=0)
    vv = jnp.concatenate([kvp[:, D_KV:], kvc[:, D_KV:]], axis=0)
    return kk, vv


MIXER_BLOCKS = 2


class _Rows:
    def __init__(self, ref, sub):
        self.ref, self.rows = ref, slice(sub * BLOCK, (sub + 1) * BLOCK)

    def __getitem__(self, idx):
        return self.ref[self.rows, idx[1]]

    def __setitem__(self, idx, value):
        self.ref[self.rows, idx[1]] = value


def _kv_before_spec(index):
    return pl.BlockSpec((BLOCK, 2 * D_KV),
                        lambda i: (jnp.maximum(MIXER_BLOCKS * index(i) - 1, 0), SEG_KV // (2 * D_KV)))


def _pair_cols(g, p, base=0):
    return slice(base + (4 * g + p) * 128, base + (4 * g + p + 1) * 128)


def _mixer_fwd(z, sink_rows, ln_g, ln_b, sgu_w, sgu_bt):
    t = z.shape[0]

    def body(z_all, kvp_ref, sink_ref, lng_ref, lnb_ref, w_ref, bt_ref, a_all, prob_ref, sink_prob_ref):
        kv_before = kvp_ref[...]
        for sub in range(MIXER_BLOCKS):
            z_ref, a_ref = _Rows(z_all, sub), _Rows(a_all, sub)
            one_block(z_ref, kv_before, MIXER_BLOCKS * pl.program_id(0) + sub, sink_ref, lng_ref, lnb_ref, w_ref,
                      bt_ref, a_ref, prob_ref.at[sub], sink_prob_ref.at[sub])
            kv_before = z_ref[:, SEG_KV:SEG_KV + 2 * D_KV]

    def one_block(z_ref, kv_before, block_index, sink_ref, lng_ref, lnb_ref, w_ref, bt_ref, a_ref, prob_ref,
                  sink_prob_ref):
        bias = _window_bias(block_index)
        kk, vv = _keys_values(z_ref, kv_before)
        kk_b, vvt_b = kk.astype(BF16), vv.T.astype(BF16)
        for g in range(2):
            qt = _heads_t([z_ref[:, _pair_cols(g, p, SEG_Q)] * ATTN_SCALE for p in range(4)], g).astype(BF16)
            prob, sink_prob = _softmax_t(_dot(kk_b, qt), bias, sink_ref[g])
            prob_b = prob.astype(BF16)
            prob_ref[g] = prob_b
            sink_prob_ref[g] = sink_prob
            ot = _dot(vvt_b, prob_b)
            for p in range(4):
                gate = z_ref[:, _pair_cols(g, p, SEG_GA)]
                a_ref[:, _pair_cols(g, p)] = (_pair_block(ot, p, g) * (gate * _sigmoid(gate))).astype(BF16)

        vhat, _ = _layer_norm_fwd(z_ref[:, SEG_VS:SEG_VS + D_SGU])
        vn = vhat * lng_ref[...] + lnb_ref[...]
        tril = _tril()
        for g in range(SGU_GROUPS):
            cols = slice(g * 128, (g + 1) * 128)
            wm = jnp.where(tril, w_ref[g], 0.0).astype(BF16)
            mixed = _dot(wm, vn[:, cols].astype(BF16)) + bt_ref[:, g:g + 1]
            gate = z_ref[:, SEG_GS + g * 128:SEG_GS + (g + 1) * 128]
            a_ref[:, D_ATTN + g * 128:D_ATTN + (g + 1) * 128] = (
                (z_ref[:, SEG_U + g * 128:SEG_U + (g + 1) * 128] * mixed) * (gate * _sigmoid(gate))).astype(BF16)

    rows = MIXER_BLOCKS * BLOCK
    return pl.pallas_call(
        body, name="mixer_fwd", grid=(t // rows,),
        in_specs=[pl.BlockSpec((rows, D_IN), lambda i: (i, 0)), _kv_before_spec(lambda i: i),
                  _const_spec((2, 1, 8 * BLOCK)), _const_spec((1, D_SGU)), _const_spec((1, D_SGU)),
                  _const_spec((SGU_GROUPS, BLOCK, BLOCK)), _const_spec((BLOCK, SGU_GROUPS))],
        out_specs=(pl.BlockSpec((rows, D_MODEL), lambda i: (i, 0)),
                   pl.BlockSpec((MIXER_BLOCKS, 2, 2 * BLOCK, 8 * BLOCK), lambda i: (i, 0, 0, 0)),
                   pl.BlockSpec((MIXER_BLOCKS, 2, 1, 8 * BLOCK), lambda i: (i, 0, 0, 0))),
        out_shape=(jax.ShapeDtypeStruct((t, D_MODEL), BF16),
                   jax.ShapeDtypeStruct((t // BLOCK, 2, 2 * BLOCK, 8 * BLOCK), BF16),
                   jax.ShapeDtypeStruct((t // BLOCK, 2, 1, 8 * BLOCK), F32)),
        compiler_params=_params(dimension_semantics=("arbitrary",)),
    )(z, z, sink_rows, ln_g, ln_b, sgu_w, sgu_bt)


def _mixer_bwd(z, da, probs, sink_probs, ln_g, ln_b, sgu_w, sgu_wt, sgu_bt):
    t = z.shape[0]

    def body(z_all, kvp_ref, da_all, prob_ref, sink_prob_ref, lng_ref, lnb_ref, w_ref, wt_ref, bt_ref,
             dz_all, dsink_ref, dw_ref, db_ref, dlng_ref, dlnb_ref, carry_ref, dsink_acc, dbt_acc):
        step = pl.program_id(0)

        @pl.when(step == 0)
        def _():
            carry_ref[...] = jnp.zeros_like(carry_ref)
            dsink_acc[...] = jnp.zeros_like(dsink_acc)
            dbt_acc[...] = jnp.zeros_like(dbt_acc)
            dw_ref[...] = jnp.zeros_like(dw_ref)
            dlng_ref[...] = jnp.zeros_like(dlng_ref)
            dlnb_ref[...] = jnp.zeros_like(dlnb_ref)

        carry = carry_ref[...]
        for sub in reversed(range(MIXER_BLOCKS)):
            kv_before = kvp_ref[...] if sub == 0 else _Rows(z_all, sub - 1)[:, SEG_KV:SEG_KV + 2 * D_KV]
            carry = one_block(_Rows(z_all, sub), kv_before, _Rows(da_all, sub), prob_ref.at[sub], sink_prob_ref.at[sub],
                              carry, lng_ref, lnb_ref, w_ref, wt_ref, bt_ref, _Rows(dz_all, sub),
                              dw_ref, dlng_ref, dlnb_ref, dsink_acc, dbt_acc)
        carry_ref[...] = carry

        @pl.when(step == ns - 1)
        def _():
            db_ref[...] = dbt_acc[...].T[:SGU_GROUPS]
            lane_row = lax.broadcasted_iota(jnp.int32, (1, 128), 1)
            d_sink = jnp.zeros((1, 128), F32)
            for g in range(2):
                acc = dsink_acc[g]
                for j in range(8):
                    head_sum = jnp.sum(acc[:, j * BLOCK:(j + 1) * BLOCK], axis=-1, keepdims=True)
                    d_sink = d_sink + jnp.where(lane_row == 8 * g + j, head_sum, 0.0)
            dsink_ref[...] = d_sink

    def one_block(z_ref, kv_before, da_ref, prob_ref, sink_prob_ref, carry, lng_ref, lnb_ref, w_ref, wt_ref, bt_ref,
                  dz_ref, dw_ref, dlng_ref, dlnb_ref, dsink_acc, dbt_acc):
        kk, vv = _keys_values(z_ref, kv_before)
        vv_b = vv.astype(BF16)
        kkt_b, vvt_b = kk.T.astype(BF16), vv.T.astype(BF16)
        dkk = jnp.zeros((2 * BLOCK, D_KV), F32)
        dvv = jnp.zeros((2 * BLOCK, D_KV), F32)
        for g in range(2):
            qt = _heads_t([z_ref[:, _pair_cols(g, p, SEG_Q)] * ATTN_SCALE for p in range(4)], g).astype(BF16)
            prob_b, sink_prob = prob_ref[g], sink_prob_ref[g]
            prob = prob_b.astype(F32)
            ot = _dot(vvt_b, prob_b)
            gates = [z_ref[:, _pair_cols(g, p, SEG_GA)] for p in range(4)]
            sig = [_sigmoid(gt) for gt in gates]
            d_attn = [da_ref[:, _pair_cols(g, p)] for p in range(4)]
            d_ot = _heads_t([d_attn[p] * (gates[p] * sig[p]) for p in range(4)], g).astype(BF16)
            d_prob = _dot(vv_b, d_ot)
            delta = jnp.sum(prob * d_prob, axis=0, keepdims=True)
            d_scores = (prob * (d_prob - delta)).astype(BF16)
            dsink_acc[g] -= sink_prob * delta
            d_qt = _dot(kkt_b, d_scores)
            dkk = dkk + _dot_nt(d_scores, qt)
            dvv = dvv + _dot_nt(prob_b, d_ot)
            for p in range(4):
                dz_ref[:, _pair_cols(g, p, SEG_Q)] = (_pair_block(d_qt, p, g) * ATTN_SCALE).astype(BF16)
                d_silu = sig[p] * (1.0 + gates[p] * (1.0 - sig[p]))
                dz_ref[:, _pair_cols(g, p, SEG_GA)] = (d_attn[p] * _pair_block(ot, p, g) * d_silu).astype(BF16)
        d_kv = jnp.concatenate([dkk, dvv], axis=1)
        dz_ref[:, SEG_KV:SEG_KV + 2 * D_KV] = (d_kv[BLOCK:] + carry).astype(BF16)

        vhat, rstd = _layer_norm_fwd(z_ref[:, SEG_VS:SEG_VS + D_SGU])
        lng = lng_ref[...]
        vn = vhat * lng + lnb_ref[...]
        tril, triu = _tril(), _tril(transposed=True)
        lane = lax.broadcasted_iota(jnp.int32, (BLOCK, 128), 1)
        d_bt = jnp.zeros((BLOCK, 128), F32)
        d_vn = []
        for g in range(SGU_GROUPS):
            cols = slice(g * 128, (g + 1) * 128)
            wm = jnp.where(tril, w_ref[g], 0.0).astype(BF16)
            wmt = jnp.where(triu, wt_ref[g], 0.0).astype(BF16)
            vn_g = vn[:, cols].astype(BF16)
            mixed = _dot(wm, vn_g) + bt_ref[:, g:g + 1]
            gate = z_ref[:, SEG_GS + g * 128:SEG_GS + (g + 1) * 128]
            u = z_ref[:, SEG_U + g * 128:SEG_U + (g + 1) * 128]
            d_out = da_ref[:, D_ATTN + g * 128:D_ATTN + (g + 1) * 128]
            sg = _sigmoid(gate)
            d_um = d_out * (gate * sg)
            dz_ref[:, SEG_U + g * 128:SEG_U + (g + 1) * 128] = (d_um * mixed).astype(BF16)
            dz_ref[:, SEG_GS + g * 128:SEG_GS + (g + 1) * 128] = (
                d_out * (u * mixed) * (sg * (1.0 + gate * (1.0 - sg)))).astype(BF16)
            d_mixed = d_um * u
            d_mixed_b = d_mixed.astype(BF16)
            dw_ref[g] += jnp.where(tril, _dot_nt(d_mixed_b, vn_g), 0.0)
            d_bt = d_bt + jnp.where(lane == g, jnp.sum(d_mixed, axis=-1, keepdims=True), 0.0)
            d_vn.append(_dot(wmt, d_mixed_b))
        dbt_acc[...] += d_bt
        d_vn = jnp.concatenate(d_vn, axis=1)
        dlng_ref[...] += jnp.sum(d_vn * vhat, axis=0, keepdims=True)
        dlnb_ref[...] += jnp.sum(d_vn, axis=0, keepdims=True)
        d_vhat = d_vn * lng
        d_v = rstd * (d_vhat - jnp.mean(d_vhat, axis=-1, keepdims=True)
                      - vhat * jnp.mean(d_vhat * vhat, axis=-1, keepdims=True))
        dz_ref[:, SEG_VS:SEG_VS + D_SGU] = d_v.astype(BF16)
        return d_kv[:BLOCK]

    rows = MIXER_BLOCKS * BLOCK
    ns = t // rows
    rev = lambda i: ns - 1 - i
    return pl.pallas_call(
        body, name="mixer_bwd", grid=(ns,),
        in_specs=[pl.BlockSpec((rows, D_IN), lambda i: (rev(i), 0)), _kv_before_spec(rev),
                  pl.BlockSpec((rows, D_MODEL), lambda i: (rev(i), 0)),
                  pl.BlockSpec((MIXER_BLOCKS, 2, 2 * BLOCK, 8 * BLOCK), lambda i: (rev(i), 0, 0, 0)),
                  pl.BlockSpec((MIXER_BLOCKS, 2, 1, 8 * BLOCK), lambda i: (rev(i), 0, 0, 0)),
                  _const_spec((1, D_SGU)), _const_spec((1, D_SGU)),
                  _const_spec((SGU_GROUPS, BLOCK, BLOCK)), _const_spec((SGU_GROUPS, BLOCK, BLOCK)),
                  _const_spec((BLOCK, SGU_GROUPS))],
        out_specs=(pl.BlockSpec((rows, D_IN), lambda i: (rev(i), 0)), _const_spec((1, 128)),
                   _const_spec((SGU_GROUPS, BLOCK, BLOCK)), _const_spec((SGU_GROUPS, BLOCK)),
                   _const_spec((1, D_SGU)), _const_spec((1, D_SGU))),
        out_shape=(jax.ShapeDtypeStruct((t, D_IN), BF16), jax.ShapeDtypeStruct((1, 128), F32),
                   jax.ShapeDtypeStruct((SGU_GROUPS, BLOCK, BLOCK), F32), jax.ShapeDtypeStruct((SGU_GROUPS, BLOCK), F32),
                   jax.ShapeDtypeStruct((1, D_SGU), F32), jax.ShapeDtypeStruct((1, D_SGU), F32)),
        scratch_shapes=[pltpu.VMEM((BLOCK, 2 * D_KV), F32), pltpu.VMEM((2, 1, 8 * BLOCK), F32),
                        pltpu.VMEM((BLOCK, 128), F32)],
        compiler_params=_params(dimension_semantics=("arbitrary",)),
    )(z, z, da, probs, sink_probs, ln_g, ln_b, sgu_w, sgu_wt, sgu_bt)


def _out_proj_head(a, w_out_full, x, target, mod, final_g, tm=256):
    t, d = x.shape

    def body(a_ref, w_ref, x_ref, tg_ref, gate_ref, fg_ref, dx2_ref, dy_ref, loss_ref, dfg_ref, dgate_ref):
        @pl.when(pl.program_id(0) == 0)
        def _():
            loss_ref[...] = jnp.zeros_like(loss_ref)
            dfg_ref[...] = jnp.zeros_like(dfg_ref)
            dgate_ref[...] = jnp.zeros_like(dgate_ref)

        yv, gate, fg = _dot(a_ref[...], w_ref[...]), gate_ref[...], fg_ref[...]
        x2 = x_ref[...] + gate * yv
        r2 = lax.rsqrt(jnp.mean(x2 * x2, axis=-1, keepdims=True) + EPS)
        nrm = x2 * r2
        err = nrm * fg - tg_ref[...]
        loss_ref[...] += 0.5 * jnp.sum(jnp.mean(err * err, axis=-1, keepdims=True), axis=0, keepdims=True)
        fg_d = fg * (1.0 / d)
        err_nrm = err * nrm
        dfg_ref[...] += jnp.sum(err_nrm, axis=0, keepdims=True) * (1.0 / d)
        d_nrm = err * fg_d
        dx2 = r2 * (d_nrm - nrm * jnp.mean(err_nrm * fg_d, axis=-1, keepdims=True))
        dx2_ref[...] = dx2
        dgate_ref[...] += jnp.sum(dx2 * yv, axis=0, keepdims=True)
        dy_ref[...] = (dx2 * gate).astype(BF16)

    blk = pl.BlockSpec((tm, d), lambda i: (i, 0))
    row = _const_spec((1, d))
    whole = pl.BlockSpec(w_out_full.shape, lambda i: (0, 0), pipeline_mode=pl.Buffered(1))
    return pl.pallas_call(
        body, name="out_proj_head", grid=(t // tm,),
        in_specs=[pl.BlockSpec((tm, a.shape[1]), lambda i: (i, 0)), whole, blk, blk, _mod_spec(MOD_GATE, d), row],
        out_specs=(blk, blk, _const_spec((1, 128)), row, row),
        out_shape=(jax.ShapeDtypeStruct((t, d), F32), jax.ShapeDtypeStruct((t, d), BF16),
                   jax.ShapeDtypeStruct((1, 128), F32), jax.ShapeDtypeStruct((1, d), F32),
                   jax.ShapeDtypeStruct((1, d), F32)),
        compiler_params=_params(dimension_semantics=("arbitrary",)),
    )(a, w_out_full, x, target, mod, final_g)


def _z_proj_bwd_norm(dz, w_in_t, x, dx2, norm_g, mod, dep, tm=256):
    t, d = x.shape

    def body(dz_ref, w_ref, x_ref, dx2_ref, g_ref, sc_ref, dep_ref, gx_ref, dshift_ref, dscale_ref, dg_ref):
        @pl.when(pl.program_id(0) == 0)
        def _():
            dshift_ref[...] = jnp.zeros_like(dshift_ref)
            dscale_ref[...] = jnp.zeros_like(dscale_ref)
            dg_ref[...] = jnp.zeros_like(dg_ref)

        dh, xv, g = _dot(dz_ref[...], w_ref[...]), x_ref[...], g_ref[...]
        one_plus = 1.0 + sc_ref[...]
        r = lax.rsqrt(jnp.mean(xv * xv, axis=-1, keepdims=True) + EPS)
        xn = xv * r
        gain = one_plus * g
        dh_xn = dh * xn
        dh_xn_sum = jnp.sum(dh_xn, axis=0, keepdims=True)
        dshift_ref[...] += jnp.sum(dh, axis=0, keepdims=True)
        dscale_ref[...] += dh_xn_sum * g
        dg_ref[...] += dh_xn_sum * one_plus
        d_xn = dh * gain
        gx_ref[...] = dx2_ref[...] + r * (d_xn - xn * jnp.mean(dh_xn * gain, axis=-1, keepdims=True))

    blk = pl.BlockSpec((tm, d), lambda i: (i, 0))
    row = _const_spec((1, d))
    whole = pl.BlockSpec(w_in_t.shape, lambda i: (0, 0), pipeline_mode=pl.Buffered(1))
    return pl.pallas_call(
        body, name="z_proj_bwd_norm", grid=(t // tm,),
        in_specs=[pl.BlockSpec((tm, dz.shape[1]), lambda i: (i, 0)), whole, blk, blk, row, _mod_spec(MOD_SCALE, d),
                  _const_spec((8, 128))],
        out_specs=(blk, row, row, row),
        out_shape=(jax.ShapeDtypeStruct((t, d), F32),) + (jax.ShapeDtypeStruct((1, d), F32),) * 3,
        compiler_params=_params(dimension_semantics=("arbitrary",)),
    )(dz, w_in_t, x, dx2, norm_g, mod, dep)


def _adamw(w, g, m, v):
    m = ADAM_B1 * m + (1.0 - ADAM_B1) * g
    v = ADAM_B2 * v + (1.0 - ADAM_B2) * (g * g)
    m_hat = m / (1.0 - ADAM_B1 ** ADAM_STEP)
    v_hat = v / (1.0 - ADAM_B2 ** ADAM_STEP)
    delta = -ADAM_LR * (m_hat / (jnp.sqrt(v_hat) + ADAM_EPS) + ADAM_WD * w)
    return delta, m, v


def _relay_sum(device, blocks, land_pair, land_first, tr):
    _, r, c = blocks.shape

    def body(device_ref, a_ref, b_ref, c_ref, o_ref):
        o_ref[...] = (a_ref[...].astype(F32) + b_ref[...].astype(F32) + c_ref[...].astype(F32)).astype(BF16)

    second = pl.BlockSpec((None, tr, c), lambda i, device_ref: (1, i, 0))
    return pl.pallas_call(
        body, name="w_in_grad_relay_sum",
        grid_spec=pltpu.PrefetchScalarGridSpec(
            num_scalar_prefetch=1, grid=(r // tr,),
            in_specs=[pl.BlockSpec((None, tr, c), lambda i, device_ref: (device_ref[0], i, 0)), second, second],
            out_specs=pl.BlockSpec((tr, c), lambda i, device_ref: (i, 0))),
        out_shape=jax.ShapeDtypeStruct((r, c), BF16),
        compiler_params=_params(dimension_semantics=("arbitrary",)),
    )(device, blocks, land_pair, land_first)


def _adam_from_chips(chip, pair, landed, w, m, v, name, tc):
    _, r, c = pair.shape
    n = len(landed)

    def body(chip_ref, own_ref, *refs):
        w_ref, m_ref, v_ref, g_ref, d_ref, nm_ref, nv_ref = refs[n:]
        g = own_ref[...].astype(F32)
        for k in range(n):
            g = g + refs[k][...].astype(F32)
        g_ref[...] = g
        d_ref[...], nm_ref[...], nv_ref[...] = _adamw(w_ref[...], g, m_ref[...], v_ref[...])

    def landed_spec(index):
        return pl.BlockSpec((None, r, tc), lambda i, chip_ref: (index, 0, i))

    blk = pl.BlockSpec((r, tc), lambda i, chip_ref: (0, i))
    return pl.pallas_call(
        body, name=name,
        grid_spec=pltpu.PrefetchScalarGridSpec(
            num_scalar_prefetch=1, grid=(c // tc,),
            in_specs=[pl.BlockSpec((None, r, tc), lambda i, chip_ref: (chip_ref[0], 0, i))]
            + [landed_spec(index) for _, index in landed] + [blk, blk, blk],
            out_specs=(blk,) * 4),
        out_shape=(jax.ShapeDtypeStruct((r, c), F32),) * 4,
        compiler_params=_params(dimension_semantics=("arbitrary",)),
    )(chip, pair, *[array for array, _ in landed], w, m, v)


def _adam_w_ada(device, act_t, dmod_all, w, m, v, tr=512):
    r, c = w.shape

    def body(device_ref, a_ref, dm_ref, w_ref, m_ref, v_ref, g_ref, d_ref, nm_ref, nv_ref):
        g = _dot(a_ref[...].astype(BF16), dm_ref[...].astype(BF16))
        g_ref[...] = g
        d_ref[...], nm_ref[...], nv_ref[...] = _adamw(w_ref[...], g, m_ref[...], v_ref[...])

    blk = pl.BlockSpec((tr, c), lambda i, device_ref: (i, 0))
    return pl.pallas_call(
        body, name="adam_w_ada",
        grid_spec=pltpu.PrefetchScalarGridSpec(
            num_scalar_prefetch=1, grid=(r // tr,),
            in_specs=[pl.BlockSpec((tr, N_DEV), lambda i, device_ref: (i, 0)),
                      pl.BlockSpec((N_DEV, c), lambda i, device_ref: (0, device_ref[0])), blk, blk, blk],
            out_specs=(blk,) * 4),
        out_shape=(jax.ShapeDtypeStruct((r, c), F32),) * 4,
        compiler_params=_params(dimension_semantics=("arbitrary",)),
    )(device, act_t, dmod_all, w, m, v)


def _pack_small(d_shift, d_scale, d_gate, d_norm_g, d_final_g, d_ln_g, d_ln_b, loss, d_sinks, d_sgu_b):
    def body(shift_ref, scale_ref, gate_ref, ng_ref, fg_ref, lng_ref, lnb_ref, loss_ref, sink_ref, b_ref, o_ref):
        o_ref[...] = jnp.zeros_like(o_ref)
        o_ref[ROW_SHIFT:ROW_SHIFT + 1, :] = shift_ref[...]
        o_ref[ROW_SCALE:ROW_SCALE + 1, :] = scale_ref[...]
        o_ref[ROW_GATE:ROW_GATE + 1, :] = gate_ref[...]
        o_ref[ROW_NORM_G:ROW_NORM_G + 1, :] = ng_ref[...]
        o_ref[ROW_FINAL_G:ROW_FINAL_G + 1, :] = fg_ref[...]
        o_ref[ROW_LN:ROW_LN + 1, 0:D_SGU] = lng_ref[...]
        o_ref[ROW_LN:ROW_LN + 1, D_SGU:2 * D_SGU] = lnb_ref[...]
        o_ref[ROW_MISC:ROW_MISC + 1, 0:128] = loss_ref[...]
        o_ref[ROW_MISC:ROW_MISC + 1, 128:256] = sink_ref[...]
        o_ref[ROW_SGU_B:ROW_SGU_B + SGU_GROUPS, 0:BLOCK] = b_ref[...]

    return pl.pallas_call(
        body, name="pack_small", out_shape=jax.ShapeDtypeStruct((SMALL_ROWS, D_MODEL), F32),
        compiler_params=_params(),
    )(d_shift, d_scale, d_gate, d_norm_g, d_final_g, d_ln_g, d_ln_b, loss, d_sinks, d_sgu_b)


_SMALL_NAMES = ("norm_g", "b_ada", "attn_sinks", "sgu_ln_g", "sgu_ln_b", "sgu_w", "sgu_b", "final_g")


def _adam_small(partials, d_sgu_w_all, weights, moments_m, moments_v):
    names = _SMALL_NAMES
    k = len(names)

    def body(*refs):
        p_ref, sw_ref = refs[0], refs[1]
        w_refs, m_refs, v_refs = refs[2:2 + k], refs[2 + k:2 + 2 * k], refs[2 + 2 * k:2 + 3 * k]
        loss_ref, dmod_ref = refs[2 + 3 * k], refs[3 + 3 * k]
        out_refs = refs[4 + 3 * k:4 + 7 * k]
        sum_ref = refs[4 + 7 * k]
        total = p_ref[0]
        for j in range(1, N_DEV):
            total = total + p_ref[j]
        sum_ref[...] = total
        for j in range(N_DEV):
            for part, row in enumerate((ROW_SHIFT, ROW_SCALE, ROW_GATE)):
                dmod_ref[j:j + 1, part * D_MODEL:(part + 1) * D_MODEL] = p_ref[j, row:row + 1, :]
        loss_ref[...] = sum_ref[ROW_MISC:ROW_MISC + 1, 0:1]
        d_sgu_w = sw_ref[0]
        for j in range(1, N_DEV):
            d_sgu_w = d_sgu_w + sw_ref[j]
        grads = {
            "norm_g": sum_ref[ROW_NORM_G:ROW_NORM_G + 1, :],
            "b_ada": jnp.concatenate([sum_ref[r:r + 1, :] for r in (ROW_SHIFT, ROW_SCALE, ROW_GATE)], axis=1),
            "attn_sinks": sum_ref[ROW_MISC:ROW_MISC + 1, 128:128 + N_Q_HEADS],
            "sgu_ln_g": sum_ref[ROW_LN:ROW_LN + 1, 0:D_SGU],
            "sgu_ln_b": sum_ref[ROW_LN:ROW_LN + 1, D_SGU:2 * D_SGU],
            "sgu_w": d_sgu_w[None],
            "sgu_b": sum_ref[ROW_SGU_B:ROW_SGU_B + SGU_GROUPS, 0:BLOCK][None],
            "final_g": sum_ref[ROW_FINAL_G:ROW_FINAL_G + 1, :],
        }
        for i, name in enumerate(names):
            g = grads[name]
            delta, m, v = _adamw(w_refs[i][...], g, m_refs[i][...], v_refs[i][...])
            out_refs[4 * i][...] = g
            out_refs[4 * i + 1][...] = delta
            out_refs[4 * i + 2][...] = m
            out_refs[4 * i + 3][...] = v

    shapes = [jax.ShapeDtypeStruct((1, 1), F32), jax.ShapeDtypeStruct((N_DEV, 3 * D_MODEL), F32)]
    for name in names:
        shapes += [jax.ShapeDtypeStruct(weights[name].shape, F32)] * 4
    outs = pl.pallas_call(
        body, name="adam_small", out_shape=tuple(shapes),
        scratch_shapes=[pltpu.VMEM((SMALL_ROWS, D_MODEL), F32)],
        compiler_params=_params(),
    )(partials, d_sgu_w_all, *[weights[n] for n in names], *[moments_m[n] for n in names],
      *[moments_v[n] for n in names])
    return outs[0], outs[1], {name: outs[2 + 4 * i:6 + 4 * i] for i, name in enumerate(names)}


def kernel(x, c, norm_g, w_ada, b_ada, w_in, attn_sinks, sgu_ln_g, sgu_ln_b, sgu_w, sgu_b, w_out, final_g, loss_target, m_norm_g, m_w_ada, m_b_ada, m_w_in, m_attn_sinks, m_sgu_ln_g, m_sgu_ln_b, m_sgu_w, m_sgu_b, m_w_out, m_final_g, v_norm_g, v_w_ada, v_b_ada, v_w_in, v_attn_sinks, v_sgu_ln_g, v_sgu_ln_b, v_sgu_w, v_sgu_b, v_w_out, v_final_g):
    xi, yi, ci = _place()
    me = 4 * xi + 2 * yi + ci
    x2d, target = x[0], loss_target[0]
    t = x2d.shape[0]

    core = ci.astype(jnp.int32).reshape(1)
    chip = (2 * xi + yi).astype(jnp.int32).reshape(1)

    first = _own_block_copies(_first_targets)
    first_flight = _start_copies([_with_own_slot(w_in[0].T.astype(BF16), me)], first, 2, core, "gather_w_in_start")

    c_all = _all_gather_small(c.reshape(8, 256) + first_flight[3][0, 0], "gather_c").reshape(N_DEV, D_MODEL)
    device = me.astype(jnp.int32).reshape(1)
    c_act, mod_part = _modulation(device, c_all, w_ada[0], b_ada)
    mod_all = _all_gather_small(mod_part, "gather_mod")

    across = _wait_then_start(first_flight, lambda *a: first(*a)[1:], _second_axis_stage_copies, 3, mod_all,
                              "gather_w_in_second_axis_stage")
    mod = lax.dynamic_index_in_dim(mod_all, me, axis=1, keepdims=False).reshape(1, 3 * D_MODEL)
    mod = mod + across[3][0, 0]

    w_in_pair = _wait_copies((first_flight[0], first_flight[1], across[2], None), lambda *a: first(*a)[:1], mod,
                             "gather_w_in_sibling_wait")
    h, z_own = _norm_z_proj_own(x2d, norm_g, mod, w_in_pair[0].reshape(D_IN, D_MODEL), chip)
    w_out_early = _own_block_copies(lambda x, y, c: [(x, y, 1 - c), (*_second_axis_chip(x, y, c), c)])
    w_out_late = _own_block_copies(lambda x, y, c: [(*_first_axis_chip(x, y, c), c), (1 - x, 1 - y, c)])
    forward = _wait_then_start(
        (across[0], across[1], w_in_pair, None), lambda *a: _second_axis_stage_copies(*a)[:1],
        lambda refs, s, r: _second_axis_forward_copies(refs[:1], s, r) + _group(w_out_early, 1, 1, 1)(refs, s, r),
        3, z_own, "gather_w_in_second_axis_forward", more_bufs=[_with_own_slot(w_out[0].astype(BF16), me)])
    w_in_most = _wait_copies((across[0], across[1], forward[2][:1], None),
                             lambda *a: _second_axis_stage_copies(*a)[1:2], z_own, "gather_w_in_first_forward_wait")
    w_in_most = _wait_copies((forward[0], forward[1], w_in_most, None), _second_axis_forward_copies, z_own,
                             "gather_w_in_second_forward_wait")
    z_early = _z_proj(h, w_in_most[0].reshape(D_IN, D_MODEL), chip, 1, _Z_EARLY_TILES - 1, z_own, "z_proj_early")
    last = _wait_then_start(
        (across[0], across[1], [w_in_most[0], forward[2][1]], None), lambda *a: _second_axis_stage_copies(*a)[2:],
        lambda refs, s, r: _diagonal_forward_copies(refs[:1], s, r) + _group(w_out_late, 1, 1, 1)(refs, s, r),
        3, z_early, "gather_w_in_last_stage")
    w_in_all = _wait_copies((last[0], last[1], last[2][:1], None), _diagonal_forward_copies, z_early,
                            "gather_w_in_last_wait")[0]
    w_in_t = w_in_all.reshape(D_IN, D_MODEL)
    z = _z_proj(h, w_in_t, chip, _Z_EARLY_TILES, 7 - _Z_EARLY_TILES, z_early, "z_proj_late")
    w_out_half = _wait_copies((forward[0], forward[1], last[2][1:], None), _group(w_out_early, 0, 1, 1), z,
                              "gather_w_out_early_wait")
    w_out_flight = _wait_then_start((last[0], last[1], w_out_half, None), _group(w_out_late, 0, 1, 1),
                                    _forward_copies, 3, z, "gather_w_out_forward_stage")
    sink_rows = jnp.repeat(attn_sinks.reshape(N_Q_HEADS), BLOCK).reshape(2, 1, 8 * BLOCK)
    sgu_bt = sgu_b[0].T
    a, probs, sink_probs = _mixer_fwd(z, sink_rows + w_out_flight[3][0, 0], sgu_ln_g, sgu_ln_b, sgu_w[0], sgu_bt)
    w_out_all = _wait_copies(w_out_flight, _forward_copies, a, "gather_w_out_forward_wait")[0]
    w_out_full = w_out_all.reshape(D_MODEL, D_MODEL)
    final_g_row = final_g.reshape(1, D_MODEL)
    dx2, dy, loss_part, d_final_g, d_gate = _out_proj_head(a, w_out_full, x2d, target, mod, final_g_row)

    da = _matmul(dy, w_out_full, "nt", F32, min(t, 1024), 1024, "out_proj_bwd")
    dw_out = _matmul(a, dy, "tn", BF16, 1024, 1024, "w_out_grad").reshape(4, 2, W_OUT_SHARD, D_MODEL)
    pair_out = _pair_reduce(dw_out, _every_chip, "w_out_grad_pair_reduce", W_OUT_SHARD // 2)
    dz, d_sinks, d_sgu_w, d_sgu_b, d_ln_g, d_ln_b = _mixer_bwd(
        z, da, probs, sink_probs, sgu_ln_g, sgu_ln_b, sgu_w[0], jnp.swapaxes(sgu_w[0], 1, 2), sgu_bt)
    sgu_w_to_all = _group(_own_block_copies(_all_others), 2, 1, 3)
    both = _start_copies(
        [pair_out, lax.empty((3, W_OUT_SHARD, D_MODEL), BF16), _with_own_slot(d_sgu_w, me)],
        lambda refs, s, r: _chip_copies(refs[:2], s, r) + sgu_w_to_all(refs, s, r), 3 + N_DEV - 1, core,
        "w_out_grad_chip_and_sgu_w_gather_start")
    out_flight, sgu_w_flight = (both[0], both[1], both[2][:2], None), (both[0], both[1], both[2][2:], None)
    dw_in_t = _matmul(dz, h, "tn", BF16, 768, D_MODEL, "w_in_grad", dep=both[3])
    pair_in = _pair_reduce(dw_in_t.reshape(4, 2, W_IN_SHARD, D_MODEL), _first_hop_chips, "w_in_grad_pair_reduce",
                           W_IN_SHARD // 3)
    first_hop = lambda refs, s, r: _first_hop_copies(refs[:2], s, r) + _group(_late_pair_copies, 2, 2, 2)(refs, s, r)
    hop1 = _start_copies(
        [pair_in, lax.empty((2, W_IN_SHARD, D_MODEL), BF16), dw_in_t.reshape(N_DEV, W_IN_SHARD, D_MODEL),
         lax.empty((2, W_IN_SHARD, D_MODEL), BF16)], first_hop, 4, core, "w_in_grad_first_hop_start")
    grad_x, d_shift, d_scale, d_norm_g = _z_proj_bwd_norm(dz, w_in_t, x2d, dx2, norm_g, mod, hop1[3])

    partial = _pack_small(d_shift, d_scale, d_gate, d_norm_g, d_final_g, d_ln_g, d_ln_b, loss_part, d_sinks, d_sgu_b)
    small_flight = _start_copies([_with_own_slot(partial, me)], _own_block_copies(_all_others), N_DEV - 1, core,
                                 "small_grad_gather_start")
    _, land_first, dw_in_t, land_pair = _wait_copies(hop1, first_hop, small_flight[3], "w_in_grad_first_hop_wait")
    second_device = (4 * ((xi + ci) % 2) + 2 * ((yi + 1 - ci) % 2) + ci).astype(jnp.int32).reshape(1)
    relay = _relay_sum(second_device, dw_in_t, land_pair, land_first, W_IN_SHARD // 3)
    hop2 = _start_copies([relay, lax.empty((1, W_IN_SHARD, D_MODEL), BF16)], _second_hop_copies, 1, core,
                         "w_in_grad_second_hop_start")
    pair_out, land_out = _wait_copies(out_flight, _chip_copies, hop2[3], "w_out_grad_chip_wait")
    big = {"w_out": _adam_from_chips(chip, pair_out, [(land_out, k) for k in range(3)], w_out[0], m_w_out[0],
                                     v_w_out[0], "adam_w_out", 1024)}
    partial_all = _wait_copies(small_flight, _own_block_copies(_all_others), big["w_out"][0],
                               "small_grad_gather_wait")[0]
    d_sgu_w_all = _wait_copies(sgu_w_flight, _group(_own_block_copies(_all_others), 0, 1, 3), partial_all,
                               "sgu_w_grad_gather_wait")[0]
    weights = {"norm_g": norm_g, "b_ada": b_ada, "attn_sinks": attn_sinks, "sgu_ln_g": sgu_ln_g,
               "sgu_ln_b": sgu_ln_b, "sgu_w": sgu_w, "sgu_b": sgu_b, "final_g": final_g_row}
    moments_m = {"norm_g": m_norm_g, "b_ada": m_b_ada, "attn_sinks": m_attn_sinks, "sgu_ln_g": m_sgu_ln_g,
                 "sgu_ln_b": m_sgu_ln_b, "sgu_w": m_sgu_w, "sgu_b": m_sgu_b,
                 "final_g": m_final_g.reshape(1, D_MODEL)}
    moments_v = {"norm_g": v_norm_g, "b_ada": v_b_ada, "attn_sinks": v_attn_sinks, "sgu_ln_g": v_sgu_ln_g,
                 "sgu_ln_b": v_sgu_ln_b, "sgu_w": v_sgu_w, "sgu_b": v_sgu_b,
                 "final_g": v_final_g.reshape(1, D_MODEL)}
    loss, dmod_all, small = _adam_small(partial_all, d_sgu_w_all, weights, moments_m, moments_v)
    small["final_g"] = tuple(o.reshape(D_MODEL) for o in small["final_g"])

    big["w_ada"] = _adam_w_ada(device, c_act.T, dmod_all, w_ada[0], m_w_ada[0], v_w_ada[0])
    _, land_second = _wait_copies(hop2, _second_hop_copies, big["w_ada"][0], "w_in_grad_second_hop_wait")
    big["w_in"] = tuple(o.T for o in _adam_from_chips(
        device, dw_in_t, [(land_pair, 0), (land_first, 0), (land_second, 0)], w_in[0].T, m_w_in[0].T, v_w_in[0].T,
        "adam_w_in", 512))
    order = ["norm_g", "w_ada", "b_ada", "w_in", "attn_sinks", "sgu_ln_g", "sgu_ln_b", "sgu_w", "sgu_b", "w_out",
             "final_g"]
    outs = [loss.reshape(()), grad_x[None]]
    for k in range(4):
        for name in order:
            outs.append(big[name][k][None] if name in big else small[name][k])
    return tuple(outs)
```

```python
import jax
import jax.numpy as jnp
from jax import lax
from jax.experimental import pallas as pl
from jax.experimental.pallas import tpu as pltpu

F32 = jnp.float32
BF16 = jnp.bfloat16
MESH = pl.DeviceIdType.MESH

N_DEV = 8
D_MODEL = 2048
HEAD_DIM = 64
D_ATTN = 1024
N_Q_HEADS = 16
D_KV = 128
BLOCK = 128
D_SGU = 1024
SGU_GROUPS = 8
D_IN = 5376
W_IN_SHARD = D_IN // N_DEV
W_OUT_SHARD = D_MODEL // N_DEV
W_ADA_SHARD = 3 * D_MODEL // N_DEV
EPS = 1e-6
ATTN_SCALE = 0.125

ADAM_LR = 0.001
ADAM_B1 = 0.9
ADAM_B2 = 0.999
ADAM_EPS = 1e-08
ADAM_WD = 0.01
ADAM_STEP = 10

SEG_Q, SEG_KV, SEG_GA, SEG_U, SEG_VS, SEG_GS = 0, 1024, 1280, 2304, 3328, 4352

VMEM_LIMIT = 56 * 1024 * 1024

ROW_SHIFT, ROW_SCALE, ROW_GATE, ROW_NORM_G, ROW_FINAL_G, ROW_LN, ROW_MISC, ROW_SGU_B = 0, 1, 2, 3, 4, 5, 6, 8
SMALL_ROWS = 16


def _params(**kw):
    return pltpu.CompilerParams(vmem_limit_bytes=VMEM_LIMIT, **kw)


def _sigmoid(x):
    return 0.5 * (jnp.tanh(0.5 * x) + 1.0)


def _place():
    return lax.axis_index("x"), lax.axis_index("y"), lax.axis_index("c")


def _every_chip(x, y, c):
    return [0, 1, 2, 3]


def _first_hop_chips(x, y, c):
    first = _first_axis_chip(x, y, c)
    return [2 * first[0] + first[1], 2 * (1 - x) + (1 - y)]


def _pair_reduce(blocks, chips, name, row_chunk):
    _, _, r, cols = blocks.shape
    n = len(chips(0, 0, 0))
    assert r % row_chunk == 0

    def body(in_ref, out_ref, land, own, summed, send_sems, recv_sems, own_sems, out_sems):
        x, y, c = _place()
        sends, loads, stores = [], [], []
        for m in range(n):
            cp = pltpu.make_async_remote_copy(
                src_ref=in_ref.at[chips(x, y, 1 - c)[m], 1 - c], dst_ref=land.at[m], send_sem=send_sems.at[m],
                recv_sem=recv_sems.at[m], device_id=(x, y, 1 - c), device_id_type=MESH)
            cp.start()
            sends.append(cp)
            ld = pltpu.make_async_copy(in_ref.at[chips(x, y, c)[m], c], own.at[m], own_sems.at[m])
            ld.start()
            loads.append(ld)
        for m in range(n):
            sends[m].wait_recv()
            loads[m].wait()
            for k in range(r // row_chunk):
                rows = slice(k * row_chunk, (k + 1) * row_chunk)
                summed[m, rows, :] = (own[m, rows, :].astype(F32) + land[m, rows, :].astype(F32)).astype(BF16)
            st = pltpu.make_async_copy(summed.at[m], out_ref.at[m], out_sems.at[m])
            st.start()
            stores.append(st)
        for m in range(n):
            sends[m].wait_send()
            stores[m].wait()

    spec = pl.BlockSpec(memory_space=pl.ANY)
    return pl.pallas_call(
        body, name=name, out_shape=jax.ShapeDtypeStruct((n, r, cols), BF16),
        in_specs=[spec], out_specs=spec,
        scratch_shapes=[pltpu.VMEM((n, r, cols), BF16), pltpu.VMEM((n, r, cols), BF16), pltpu.VMEM((n, r, cols), BF16),
                        pltpu.SemaphoreType.DMA((n,)), pltpu.SemaphoreType.DMA((n,)), pltpu.SemaphoreType.DMA((n,)),
                        pltpu.SemaphoreType.DMA((n,))],
        compiler_params=_params(),
    )(blocks)


_HBM = pl.BlockSpec(memory_space=pltpu.HBM)
_SEM = pl.BlockSpec(memory_space=pltpu.SEMAPHORE)
_EFFECT = pltpu.SideEffectType.DATAFLOW_SIDE_EFFECTING


def _start_copies(bufs, copies, n_copies, after, name):
    nb = len(bufs)

    def body(*refs):
        for cp in copies(refs[:nb], refs[nb + 1], refs[nb + 2]):
            cp.start()
        refs[-1][...] = jnp.zeros_like(refs[-1])

    out = pl.pallas_call(
        body, name=name,
        out_shape=(pltpu.SemaphoreType.DMA((n_copies,)), pltpu.SemaphoreType.DMA((n_copies,)),
                   *[pltpu.HBM(b.shape, b.dtype) for b in bufs], jax.ShapeDtypeStruct((8, 128), F32)),
        in_specs=(_HBM,) * nb + (pl.BlockSpec(memory_space=pl.ANY),),
        out_specs=(_SEM, _SEM) + (_HBM,) * nb + (pl.BlockSpec(memory_space=pltpu.VMEM),),
        input_output_aliases={i: 2 + i for i in range(nb)},
        compiler_params=pltpu.CompilerParams(has_side_effects=_EFFECT),
    )(*[pltpu.with_memory_space_constraint(b, pltpu.HBM) for b in bufs], after)
    return out[0], out[1], list(out[2:2 + nb]), out[-1]


def _wait_copies(flight, copies, after, name):
    send_sems, recv_sems, bufs, _ = flight
    nb = len(bufs)

    def body(*refs):
        for cp in copies(refs[:nb], refs[nb], refs[nb + 1]):
            cp.wait_send()
            cp.wait_recv()

    return pl.pallas_call(
        body, name=name,
        out_shape=tuple(pltpu.HBM(b.shape, b.dtype) for b in bufs),
        in_specs=(_HBM,) * nb + (_SEM, _SEM, pl.BlockSpec(memory_space=pl.ANY)), out_specs=(_HBM,) * nb,
        input_output_aliases={i: i for i in range(nb)},
        compiler_params=pltpu.CompilerParams(has_side_effects=_EFFECT),
    )(*bufs, send_sems, recv_sems, after)


class _From:
    def __init__(self, sems, offset):
        self.sems, self.offset = sems, offset

    @property
    def at(self):
        return self

    def __getitem__(self, k):
        return self.sems.at[k + self.offset]


def _group(copies, first_buf, n_bufs, offset):
    def grouped(refs, send_sems, recv_sems):
        return copies(refs[first_buf:first_buf + n_bufs], _From(send_sems, offset), _From(recv_sems, offset))
    return grouped


def _wait_then_start(flight, waited, started, n_started, after, name, more_bufs=()):
    old_send, old_recv, bufs, _ = flight
    bufs = list(bufs) + [pltpu.with_memory_space_constraint(b, pltpu.HBM) for b in more_bufs]
    nb = len(bufs)

    def body(*refs):
        for cp in waited(refs[:nb], refs[nb], refs[nb + 1]):
            cp.wait_send()
            cp.wait_recv()
        for cp in started(refs[:nb], refs[nb + 3], refs[nb + 4]):
            cp.start()
        refs[-1][...] = jnp.zeros_like(refs[-1])

    out = pl.pallas_call(
        body, name=name,
        out_shape=(pltpu.SemaphoreType.DMA((n_started,)), pltpu.SemaphoreType.DMA((n_started,)),
                   *[pltpu.HBM(b.shape, b.dtype) for b in bufs], jax.ShapeDtypeStruct((8, 128), F32)),
        in_specs=(_HBM,) * nb + (_SEM, _SEM, pl.BlockSpec(memory_space=pl.ANY)),
        out_specs=(_SEM, _SEM) + (_HBM,) * nb + (pl.BlockSpec(memory_space=pltpu.VMEM),),
        input_output_aliases={i: 2 + i for i in range(nb)},
        compiler_params=pltpu.CompilerParams(has_side_effects=_EFFECT),
    )(*bufs, old_send, old_recv, after)
    return out[0], out[1], list(out[2:2 + nb]), out[-1]


def _late_pair_copies(refs, send_sems, recv_sems):
    blocks_ref, land_ref = refs
    x, y, c = _place()
    first = _first_axis_chip(x, y, c)
    devices = [4 * x + 2 * y + 1 - c, 4 * first[0] + 2 * first[1] + 1 - c]
    return [pltpu.make_async_remote_copy(
        src_ref=blocks_ref.at[devices[k]], dst_ref=land_ref.at[k], send_sem=send_sems.at[k], recv_sem=recv_sems.at[k],
        device_id=(x, y, 1 - c), device_id_type=MESH) for k in range(2)]


def _chip_copies(refs, send_sems, recv_sems):
    pair_ref, land_ref = refs
    x, y, c = _place()
    chips = [(1 - x, y), (x, 1 - y), (1 - x, 1 - y)]
    return [pltpu.make_async_remote_copy(
        src_ref=pair_ref.at[2 * chip[0] + chip[1]], dst_ref=land_ref.at[k],
        send_sem=send_sems.at[k], recv_sem=recv_sems.at[k],
        device_id=(*chip, c), device_id_type=MESH) for k, chip in enumerate(chips)]


def _first_hop_copies(refs, send_sems, recv_sems):
    pair_ref, land_ref = refs
    x, y, c = _place()
    return [pltpu.make_async_remote_copy(
        src_ref=pair_ref.at[k], dst_ref=land_ref.at[k], send_sem=send_sems.at[k], recv_sem=recv_sems.at[k],
        device_id=(*_first_axis_chip(x, y, c), c), device_id_type=MESH) for k in range(2)]


def _second_hop_copies(refs, send_sems, recv_sems):
    relay_ref, land_ref = refs
    x, y, c = _place()
    second = ((x + c) % 2, (y + 1 - c) % 2)
    return [pltpu.make_async_remote_copy(
        src_ref=relay_ref, dst_ref=land_ref.at[0], send_sem=send_sems.at[0], recv_sem=recv_sems.at[0],
        device_id=(*second, c), device_id_type=MESH)]


def _own_block_copies(targets):
    def copies(refs, send_sems, recv_sems):
        x, y, c = _place()
        mine = refs[0].at[4 * x + 2 * y + c]
        return [pltpu.make_async_remote_copy(
            src_ref=mine, dst_ref=mine, send_sem=send_sems.at[k], recv_sem=recv_sems.at[k],
            device_id=to, device_id_type=MESH) for k, to in enumerate(targets(x, y, c))]
    return copies


def _all_others(x, y, c):
    flip = lambda v, f: 1 - v if f else v
    return [(flip(x, r & 4), flip(y, r & 2), flip(c, r & 1)) for r in range(1, N_DEV)]


def _forward_copies(refs, send_sems, recv_sems):
    x, y, c = _place()
    chips = [(1 - x, y), (x, 1 - y), (1 - x, 1 - y)]
    return [pltpu.make_async_remote_copy(
        src_ref=refs[0].at[4 * chip[0] + 2 * chip[1] + c], dst_ref=refs[0].at[4 * chip[0] + 2 * chip[1] + c],
        send_sem=send_sems.at[k], recv_sem=recv_sems.at[k],
        device_id=(x, y, 1 - c), device_id_type=MESH) for k, chip in enumerate(chips)]


def _first_axis_chip(x, y, c):
    return (x + 1 - c) % 2, (y + c) % 2


def _second_axis_chip(x, y, c):
    return (x + c) % 2, (y + 1 - c) % 2


def _first_targets(x, y, c):
    return [(x, y, 1 - c), (*_first_axis_chip(x, y, c), c)]


def _all_gather_small(shard, name):
    def body(in_ref, out_ref, send_sems, recv_sems, local_sem):
        x, y, c = _place()
        me, sibling = 4 * x + 2 * y + c, (x, y, 1 - c)
        first, second = _first_axis_chip(x, y, c), _second_axis_chip(x, y, c)

        def pair(chip):
            return out_ref.at[pl.ds(2 * (2 * chip[0] + chip[1]), 2)]

        def exchange(k, src, dst, to):
            cp = pltpu.make_async_remote_copy(src_ref=src, dst_ref=dst, send_sem=send_sems.at[k],
                                              recv_sem=recv_sems.at[k], device_id=to, device_id_type=MESH)
            cp.start()
            cp.wait()

        own = pltpu.make_async_copy(in_ref, out_ref.at[me], local_sem)
        own.start()
        exchange(0, in_ref, out_ref.at[me], sibling)
        own.wait()
        exchange(1, pair((x, y)), pair((x, y)), (*second, c))
        exchange(2, pair(second), pair(second), sibling)
        exchange(3, pair(first), pair(first), (*second, c))

    spec = pl.BlockSpec(memory_space=pltpu.VMEM)
    return pl.pallas_call(
        body, name=name, out_shape=jax.ShapeDtypeStruct((N_DEV,) + shard.shape, shard.dtype),
        in_specs=[spec], out_specs=spec,
        scratch_shapes=[pltpu.SemaphoreType.DMA((4,)), pltpu.SemaphoreType.DMA((4,)), pltpu.SemaphoreType.DMA],
        compiler_params=_params(),
    )(shard)


def _slot_copies(refs, send_sems, recv_sems, plan):
    copies = []
    for k, ((px, py, pc), to) in enumerate(plan):
        blk = refs[0].at[4 * px + 2 * py + pc]
        copies.append(pltpu.make_async_remote_copy(
            src_ref=blk, dst_ref=blk, send_sem=send_sems.at[k], recv_sem=recv_sems.at[k],
            device_id=to, device_id_type=MESH))
    return copies


def _second_axis_stage_copies(refs, send_sems, recv_sems):
    x, y, c = _place()
    first, second = (*_first_axis_chip(x, y, c), c), (*_second_axis_chip(x, y, c), c)
    return _slot_copies(refs, send_sems, recv_sems, [((x, y, c), second), (first, (x, y, 1 - c)), (first, second)])


def _second_axis_forward_copies(refs, send_sems, recv_sems):
    x, y, c = _place()
    return _slot_copies(refs, send_sems, recv_sems, [((*_second_axis_chip(x, y, c), c), (x, y, 1 - c))])


def _diagonal_forward_copies(refs, send_sems, recv_sems):
    x, y, c = _place()
    blk = refs[0].at[4 * (1 - x) + 2 * (1 - y) + c]
    return [pltpu.make_async_remote_copy(
        src_ref=blk, dst_ref=blk, send_sem=send_sems.at[0], recv_sem=recv_sems.at[0],
        device_id=(x, y, 1 - c), device_id_type=MESH)]


def _with_own_slot(block, me):
    return lax.dynamic_update_index_in_dim(lax.empty((N_DEV,) + block.shape, block.dtype), block, me, 0)


def _matmul(a, b, dims, out_dtype, tm, tn, name, dep=None):
    if dims == "nn":
        (m, k), n = a.shape, b.shape[1]
        a_spec = pl.BlockSpec((tm, k), lambda i, j: (i, 0))
        b_spec = pl.BlockSpec((k, tn), lambda i, j: (0, j))
        contract = ((1,), (0,))
    elif dims == "nt":
        (m, k), n = a.shape, b.shape[0]
        a_spec = pl.BlockSpec((tm, k), lambda i, j: (i, 0))
        b_spec = pl.BlockSpec((tn, k), lambda i, j: (j, 0))
        contract = ((1,), (1,))
    else:
        (k, m), n = a.shape, b.shape[1]
        a_spec = pl.BlockSpec((k, tm), lambda i, j: (0, i))
        b_spec = pl.BlockSpec((k, tn), lambda i, j: (0, j))
        contract = ((0,), (0,))
    assert m % tm == 0 and n % tn == 0 and a.dtype == BF16 and b.dtype == BF16

    def body(a_ref, b_ref, *rest):
        rest[-1][...] = lax.dot_general(a_ref[...], b_ref[...], (contract, ((), ())),
                                        preferred_element_type=F32).astype(out_dtype)

    deps = [] if dep is None else [dep]
    return pl.pallas_call(
        body, name=name, grid=(m // tm, n // tn),
        in_specs=[a_spec, b_spec] + [pl.BlockSpec((8, 128), lambda i, j: (0, 0))] * len(deps),
        out_specs=pl.BlockSpec((tm, tn), lambda i, j: (i, j)),
        out_shape=jax.ShapeDtypeStruct((m, n), out_dtype),
        compiler_params=_params(dimension_semantics=("arbitrary", "arbitrary")),
    )(a, b, *deps)


Z_TILE = 768
_Z_TILE_ORDER = ((0, 1, 2, 3, 4, 5, 6), (2, 0, 1, 6, 3, 4, 5), (4, 0, 5, 6, 1, 2, 3), (6, 2, 3, 4, 0, 1, 5))
_Z_EARLY_TILES = 4


def _z_proj(h, w_in_t, chip, first, count, z_prev, name, tr=1024):
    t = h.shape[0]

    def body(chip_ref, h_ref, w_ref, z_prev_ref, z_ref):
        z_ref[...] = _dot_nt(h_ref[...], w_ref[...])

    def tile(j, chip_ref):
        picked = 0
        for c, order in enumerate(_Z_TILE_ORDER):
            for k in range(count):
                picked = picked + jnp.where((chip_ref[0] == c) & (j == k), order[first + k], 0)
        return picked

    return pl.pallas_call(
        body, name=name,
        grid_spec=pltpu.PrefetchScalarGridSpec(
            num_scalar_prefetch=1, grid=(count, t // tr),
            in_specs=[pl.BlockSpec((tr, D_MODEL), lambda j, i, o: (i, 0)),
                      pl.BlockSpec((Z_TILE, D_MODEL), lambda j, i, o: (tile(j, o), 0)),
                      pl.BlockSpec(memory_space=pl.ANY)],
            out_specs=pl.BlockSpec((tr, Z_TILE), lambda j, i, o: (i, tile(j, o)))),
        out_shape=jax.ShapeDtypeStruct((t, D_IN), F32),
        input_output_aliases={3: 0},
        compiler_params=_params(dimension_semantics=("arbitrary", "arbitrary")),
    )(chip, h, w_in_t, z_prev)


def _modulation(device, c_all, w_ada, b_ada):
    def body(device_ref, c_ref, w_ref, b_ref, act_ref, mod_ref):
        cv = c_ref[...]
        act = cv * _sigmoid(cv)
        act_ref[...] = act
        mod_ref[...] = jnp.dot(act.astype(BF16), w_ref[...].astype(BF16), preferred_element_type=F32) + b_ref[...]

    whole = lambda a: pl.BlockSpec(a.shape, lambda i, device_ref: (0,) * a.ndim)
    return pl.pallas_call(
        body, name="modulation",
        grid_spec=pltpu.PrefetchScalarGridSpec(
            num_scalar_prefetch=1, grid=(1,),
            in_specs=[whole(c_all), whole(w_ada), pl.BlockSpec((1, W_ADA_SHARD), lambda i, device_ref: (0, device_ref[0]))],
            out_specs=(whole(c_all), pl.BlockSpec((N_DEV, W_ADA_SHARD), lambda i, device_ref: (0, 0)))),
        out_shape=(jax.ShapeDtypeStruct(c_all.shape, F32), jax.ShapeDtypeStruct((N_DEV, W_ADA_SHARD), F32)),
        compiler_params=_params(dimension_semantics=("arbitrary",)),
    )(device, c_all, w_ada, b_ada)


MOD_SHIFT, MOD_SCALE, MOD_GATE = 0, 1, 2


def _mod_spec(part, d):
    return pl.BlockSpec((1, d), lambda i: (0, part))


def _norm_z_proj_own(x, norm_g, mod, w_in_t, chip, tm=512):
    t, d = x.shape

    def body(chip_ref, x_ref, g_ref, sc_ref, sh_ref, w_ref, h_ref, z_ref):
        xv = x_ref[...]
        r = lax.rsqrt(jnp.mean(xv * xv, axis=-1, keepdims=True) + EPS)
        h = ((xv * r) * g_ref[...] * (1.0 + sc_ref[...]) + sh_ref[...]).astype(BF16)
        h_ref[...] = h
        z_ref[...] = _dot_nt(h, w_ref[...])

    def own_tile(chip_ref):
        picked = 0
        for c, order in enumerate(_Z_TILE_ORDER):
            picked = picked + jnp.where(chip_ref[0] == c, order[0], 0)
        return picked

    def row(part):
        return pl.BlockSpec((1, d), lambda i, o: (0, part))

    return pl.pallas_call(
        body, name="norm_z_proj_own",
        grid_spec=pltpu.PrefetchScalarGridSpec(
            num_scalar_prefetch=1, grid=(t // tm,),
            in_specs=[pl.BlockSpec((tm, d), lambda i, o: (i, 0)), row(0), row(MOD_SCALE), row(MOD_SHIFT),
                      pl.BlockSpec((Z_TILE, d), lambda i, o: (own_tile(o), 0))],
            out_specs=(pl.BlockSpec((tm, d), lambda i, o: (i, 0)),
                       pl.BlockSpec((tm, Z_TILE), lambda i, o: (i, own_tile(o))))),
        out_shape=(jax.ShapeDtypeStruct((t, d), BF16), jax.ShapeDtypeStruct((t, D_IN), F32)),
        compiler_params=_params(dimension_semantics=("arbitrary",)),
    )(chip, x, norm_g, mod, mod, w_in_t)


def _window_bias(block_index):
    s = lax.broadcasted_iota(jnp.int32, (2 * BLOCK, BLOCK), 0)
    t = lax.broadcasted_iota(jnp.int32, (2 * BLOCK, BLOCK), 1)
    valid = ((s < BLOCK) & (s > t) & (block_index > 0)) | ((s >= BLOCK) & ((s - BLOCK) <= t))
    bias = jnp.where(valid, 0.0, -jnp.inf).astype(F32)
    return jnp.concatenate([bias] * 8, axis=1)


def _heads_t(pair_blocks, g):
    top = lax.broadcasted_iota(jnp.int32, (BLOCK, BLOCK), 0) < HEAD_DIM
    zeros = jnp.zeros((HEAD_DIM, BLOCK), F32)
    tiles = []
    for blk in pair_blocks:
        tp = blk.T
        if g == 0:
            tiles += [jnp.where(top, tp, 0.0), jnp.concatenate([tp[HEAD_DIM:], zeros], axis=0)]
        else:
            tiles += [jnp.concatenate([zeros, tp[:HEAD_DIM]], axis=0), jnp.where(top, 0.0, tp)]
    return jnp.concatenate(tiles, axis=1)


def _pair_block(xt, p, g):
    r0 = HEAD_DIM * g
    even = xt[r0:r0 + HEAD_DIM, (2 * p) * BLOCK:(2 * p + 1) * BLOCK]
    odd = xt[r0:r0 + HEAD_DIM, (2 * p + 1) * BLOCK:(2 * p + 2) * BLOCK]
    return jnp.concatenate([even, odd], axis=0).T


def _softmax_t(scores_t, bias, sink):
    st = scores_t + bias
    m = jnp.maximum(jnp.max(st, axis=0, keepdims=True), sink)
    e = jnp.exp(st - m)
    es = jnp.exp(sink - m)
    inv = 1.0 / (jnp.sum(e, axis=0, keepdims=True) + es)
    return e * inv, es * inv


def _dot(a, b):
    return jnp.dot(a, b, preferred_element_type=F32)


def _dot_nt(a, b):
    return lax.dot_general(a, b, (((1,), (1,)), ((), ())), preferred_element_type=F32)


def _layer_norm_fwd(v):
    mu = jnp.mean(v, axis=-1, keepdims=True)
    xc = v - mu
    rstd = lax.rsqrt(jnp.mean(xc * xc, axis=-1, keepdims=True) + EPS)
    return xc * rstd, rstd


def _tril(transposed=False):
    t = lax.broadcasted_iota(jnp.int32, (BLOCK, BLOCK), 0)
    s = lax.broadcasted_iota(jnp.int32, (BLOCK, BLOCK), 1)
    return s >= t if transposed else t >= s


def _const_spec(shape):
    return pl.BlockSpec(shape, lambda i: (0,) * len(shape))


def _keys_values(z_ref, kvp):
    kvc = z_ref[:, SEG_KV:SEG_KV + 2 * D_KV]
    kk = jnp.concatenate([kvp[:, :D_KV], kvc[:, :D_KV]], axis=0)
    vv = jnp.concatenate([kvp[:, D_KV:], kvc[:, D_KV:]], axis=0)
    return kk, vv


MIXER_BLOCKS = 2


class _Rows:
    def __init__(self, ref, sub):
        self.ref, self.rows = ref, slice(sub * BLOCK, (sub + 1) * BLOCK)

    def __getitem__(self, idx):
        return self.ref[self.rows, idx[1]]

    def __setitem__(self, idx, value):
        self.ref[self.rows, idx[1]] = value


def _kv_before_spec(index):
    return pl.BlockSpec((BLOCK, 2 * D_KV),
                        lambda i: (jnp.maximum(MIXER_BLOCKS * index(i) - 1, 0), SEG_KV // (2 * D_KV)))


def _pair_cols(g, p, base=0):
    return slice(base + (4 * g + p) * 128, base + (4 * g + p + 1) * 128)


def _mixer_fwd(z, sink_rows, ln_g, ln_b, sgu_w, sgu_bt, dep):
    t = z.shape[0]

    def body(z_all, kvp_ref, sink_ref, lng_ref, lnb_ref, w_ref, bt_ref, dep_ref, a_all, prob_ref, sink_prob_ref):
        kv_before = kvp_ref[...]
        for sub in range(MIXER_BLOCKS):
            z_ref, a_ref = _Rows(z_all, sub), _Rows(a_all, sub)
            one_block(z_ref, kv_before, MIXER_BLOCKS * pl.program_id(0) + sub, sink_ref, lng_ref, lnb_ref, w_ref,
                      bt_ref, a_ref, prob_ref.at[sub], sink_prob_ref.at[sub])
            kv_before = z_ref[:, SEG_KV:SEG_KV + 2 * D_KV]

    def one_block(z_ref, kv_before, block_index, sink_ref, lng_ref, lnb_ref, w_ref, bt_ref, a_ref, prob_ref,
                  sink_prob_ref):
        bias = _window_bias(block_index)
        kk, vv = _keys_values(z_ref, kv_before)
        kk_b, vvt_b = kk.astype(BF16), vv.T.astype(BF16)
        for g in range(2):
            qt = _heads_t([z_ref[:, _pair_cols(g, p, SEG_Q)] * ATTN_SCALE for p in range(4)], g).astype(BF16)
            prob, sink_prob = _softmax_t(_dot(kk_b, qt), bias, sink_ref[g])
            prob_b = prob.astype(BF16)
            prob_ref[g] = prob_b
            sink_prob_ref[g] = sink_prob
            ot = _dot(vvt_b, prob_b)
            for p in range(4):
                gate = z_ref[:, _pair_cols(g, p, SEG_GA)]
                a_ref[:, _pair_cols(g, p)] = (_pair_block(ot, p, g) * (gate * _sigmoid(gate))).astype(BF16)

        vhat, _ = _layer_norm_fwd(z_ref[:, SEG_VS:SEG_VS + D_SGU])
        vn = vhat * lng_ref[...] + lnb_ref[...]
        tril = _tril()
        for g in range(SGU_GROUPS):
            cols = slice(g * 128, (g + 1) * 128)
            wm = jnp.where(tril, w_ref[g], 0.0).astype(BF16)
            mixed = _dot(wm, vn[:, cols].astype(BF16)) + bt_ref[:, g:g + 1]
            gate = z_ref[:, SEG_GS + g * 128:SEG_GS + (g + 1) * 128]
            a_ref[:, D_ATTN + g * 128:D_ATTN + (g + 1) * 128] = (
                (z_ref[:, SEG_U + g * 128:SEG_U + (g + 1) * 128] * mixed) * (gate * _sigmoid(gate))).astype(BF16)

    rows = MIXER_BLOCKS * BLOCK
    return pl.pallas_call(
        body, name="mixer_fwd", grid=(t // rows,),
        in_specs=[pl.BlockSpec((rows, D_IN), lambda i: (i, 0)), _kv_before_spec(lambda i: i),
                  _const_spec((2, 1, 8 * BLOCK)), _const_spec((1, D_SGU)), _const_spec((1, D_SGU)),
                  _const_spec((SGU_GROUPS, BLOCK, BLOCK)), _const_spec((BLOCK, SGU_GROUPS)), _const_spec((8, 128))],
        out_specs=(pl.BlockSpec((rows, D_MODEL), lambda i: (i, 0)),
                   pl.BlockSpec((MIXER_BLOCKS, 2, 2 * BLOCK, 8 * BLOCK), lambda i: (i, 0, 0, 0)),
                   pl.BlockSpec((MIXER_BLOCKS, 2, 1, 8 * BLOCK), lambda i: (i, 0, 0, 0))),
        out_shape=(jax.ShapeDtypeStruct((t, D_MODEL), BF16),
                   jax.ShapeDtypeStruct((t // BLOCK, 2, 2 * BLOCK, 8 * BLOCK), BF16),
                   jax.ShapeDtypeStruct((t // BLOCK, 2, 1, 8 * BLOCK), F32)),
        compiler_params=_params(dimension_semantics=("arbitrary",)),
    )(z, z, sink_rows, ln_g, ln_b, sgu_w, sgu_bt, dep)


def _mixer_bwd(z, da, probs, sink_probs, ln_g, ln_b, sgu_w, sgu_wt, sgu_bt):
    t = z.shape[0]

    def body(z_all, kvp_ref, da_all, prob_ref, sink_prob_ref, lng_ref, lnb_ref, w_ref, wt_ref, bt_ref,
             dz_all, dsink_ref, dw_ref, db_ref, dlng_ref, dlnb_ref, carry_ref, dsink_acc, dbt_acc):
        step = pl.program_id(0)

        @pl.when(step == 0)
        def _():
            carry_ref[...] = jnp.zeros_like(carry_ref)
            dsink_acc[...] = jnp.zeros_like(dsink_acc)
            dbt_acc[...] = jnp.zeros_like(dbt_acc)
            dw_ref[...] = jnp.zeros_like(dw_ref)
            dlng_ref[...] = jnp.zeros_like(dlng_ref)
            dlnb_ref[...] = jnp.zeros_like(dlnb_ref)

        carry = carry_ref[...]
        for sub in reversed(range(MIXER_BLOCKS)):
            kv_before = kvp_ref[...] if sub == 0 else _Rows(z_all, sub - 1)[:, SEG_KV:SEG_KV + 2 * D_KV]
            carry = one_block(_Rows(z_all, sub), kv_before, _Rows(da_all, sub), prob_ref.at[sub], sink_prob_ref.at[sub],
                              carry, lng_ref, lnb_ref, w_ref, wt_ref, bt_ref, _Rows(dz_all, sub),
                              dw_ref, dlng_ref, dlnb_ref, dsink_acc, dbt_acc)
        carry_ref[...] = carry

        @pl.when(step == ns - 1)
        def _():
            db_ref[...] = dbt_acc[...].T[:SGU_GROUPS]
            lane_row = lax.broadcasted_iota(jnp.int32, (1, 128), 1)
            d_sink = jnp.zeros((1, 128), F32)
            for g in range(2):
                acc = dsink_acc[g]
                for j in range(8):
                    head_sum = jnp.sum(acc[:, j * BLOCK:(j + 1) * BLOCK], axis=-1, keepdims=True)
                    d_sink = d_sink + jnp.where(lane_row == 8 * g + j, head_sum, 0.0)
            dsink_ref[...] = d_sink

    def one_block(z_ref, kv_before, da_ref, prob_ref, sink_prob_ref, carry, lng_ref, lnb_ref, w_ref, wt_ref, bt_ref,
                  dz_ref, dw_ref, dlng_ref, dlnb_ref, dsink_acc, dbt_acc):
        kk, vv = _keys_values(z_ref, kv_before)
        vv_b = vv.astype(BF16)
        kkt_b, vvt_b = kk.T.astype(BF16), vv.T.astype(BF16)
        dkk = jnp.zeros((2 * BLOCK, D_KV), F32)
        dvv = jnp.zeros((2 * BLOCK, D_KV), F32)
        for g in range(2):
            qt = _heads_t([z_ref[:, _pair_cols(g, p, SEG_Q)] * ATTN_SCALE for p in range(4)], g).astype(BF16)
            prob_b, sink_prob = prob_ref[g], sink_prob_ref[g]
            prob = prob_b.astype(F32)
            ot = _dot(vvt_b, prob_b)
            gates = [z_ref[:, _pair_cols(g, p, SEG_GA)] for p in range(4)]
            sig = [_sigmoid(gt) for gt in gates]
            d_attn = [da_ref[:, _pair_cols(g, p)] for p in range(4)]
            d_ot = _heads_t([d_attn[p] * (gates[p] * sig[p]) for p in range(4)], g).astype(BF16)
            d_prob = _dot(vv_b, d_ot)
            delta = jnp.sum(prob * d_prob, axis=0, keepdims=True)
            d_scores = (prob * (d_prob - delta)).astype(BF16)
            dsink_acc[g] -= sink_prob * delta
            d_qt = _dot(kkt_b, d_scores)
            dkk = dkk + _dot_nt(d_scores, qt)
            dvv = dvv + _dot_nt(prob_b, d_ot)
            for p in range(4):
                dz_ref[:, _pair_cols(g, p, SEG_Q)] = (_pair_block(d_qt, p, g) * ATTN_SCALE).astype(BF16)
                d_silu = sig[p] * (1.0 + gates[p] * (1.0 - sig[p]))
                dz_ref[:, _pair_cols(g, p, SEG_GA)] = (d_attn[p] * _pair_block(ot, p, g) * d_silu).astype(BF16)
        d_kv = jnp.concatenate([dkk, dvv], axis=1)
        dz_ref[:, SEG_KV:SEG_KV + 2 * D_KV] = (d_kv[BLOCK:] + carry).astype(BF16)

        vhat, rstd = _layer_norm_fwd(z_ref[:, SEG_VS:SEG_VS + D_SGU])
        lng = lng_ref[...]
        vn = vhat * lng + lnb_ref[...]
        tril, triu = _tril(), _tril(transposed=True)
        lane = lax.broadcasted_iota(jnp.int32, (BLOCK, 128), 1)
        d_bt = jnp.zeros((BLOCK, 128), F32)
        d_vn = []
        for g in range(SGU_GROUPS):
            cols = slice(g * 128, (g + 1) * 128)
            wm = jnp.where(tril, w_ref[g], 0.0).astype(BF16)
            wmt = jnp.where(triu, wt_ref[g], 0.0).astype(BF16)
            vn_g = vn[:, cols].astype(BF16)
            mixed = _dot(wm, vn_g) + bt_ref[:, g:g + 1]
            gate = z_ref[:, SEG_GS + g * 128:SEG_GS + (g + 1) * 128]
            u = z_ref[:, SEG_U + g * 128:SEG_U + (g + 1) * 128]
            d_out = da_ref[:, D_ATTN + g * 128:D_ATTN + (g + 1) * 128]
            sg = _sigmoid(gate)
            d_um = d_out * (gate * sg)
            dz_ref[:, SEG_U + g * 128:SEG_U + (g + 1) * 128] = (d_um * mixed).astype(BF16)
            dz_ref[:, SEG_GS + g * 128:SEG_GS + (g + 1) * 128] = (
                d_out * (u * mixed) * (sg * (1.0 + gate * (1.0 - sg)))).astype(BF16)
            d_mixed = d_um * u
            d_mixed_b = d_mixed.astype(BF16)
            dw_ref[g] += jnp.where(tril, _dot_nt(d_mixed_b, vn_g), 0.0)
            d_bt = d_bt + jnp.where(lane == g, jnp.sum(d_mixed, axis=-1, keepdims=True), 0.0)
            d_vn.append(_dot(wmt, d_mixed_b))
        dbt_acc[...] += d_bt
        d_vn = jnp.concatenate(d_vn, axis=1)
        dlng_ref[...] += jnp.sum(d_vn * vhat, axis=0, keepdims=True)
        dlnb_ref[...] += jnp.sum(d_vn, axis=0, keepdims=True)
        d_vhat = d_vn * lng
        d_v = rstd * (d_vhat - jnp.mean(d_vhat, axis=-1, keepdims=True)
                      - vhat * jnp.mean(d_vhat * vhat, axis=-1, keepdims=True))
        dz_ref[:, SEG_VS:SEG_VS + D_SGU] = d_v.astype(BF16)
        return d_kv[:BLOCK]

    rows = MIXER_BLOCKS * BLOCK
    ns = t // rows
    rev = lambda i: ns - 1 - i
    return pl.pallas_call(
        body, name="mixer_bwd", grid=(ns,),
        in_specs=[pl.BlockSpec((rows, D_IN), lambda i: (rev(i), 0)), _kv_before_spec(rev),
                  pl.BlockSpec((rows, D_MODEL), lambda i: (rev(i), 0)),
                  pl.BlockSpec((MIXER_BLOCKS, 2, 2 * BLOCK, 8 * BLOCK), lambda i: (rev(i), 0, 0, 0)),
                  pl.BlockSpec((MIXER_BLOCKS, 2, 1, 8 * BLOCK), lambda i: (rev(i), 0, 0, 0)),
                  _const_spec((1, D_SGU)), _const_spec((1, D_SGU)),
                  _const_spec((SGU_GROUPS, BLOCK, BLOCK)), _const_spec((SGU_GROUPS, BLOCK, BLOCK)),
                  _const_spec((BLOCK, SGU_GROUPS))],
        out_specs=(pl.BlockSpec((rows, D_IN), lambda i: (rev(i), 0)), _const_spec((1, 128)),
                   _const_spec((SGU_GROUPS, BLOCK, BLOCK)), _const_spec((SGU_GROUPS, BLOCK)),
                   _const_spec((1, D_SGU)), _const_spec((1, D_SGU))),
        out_shape=(jax.ShapeDtypeStruct((t, D_IN), BF16), jax.ShapeDtypeStruct((1, 128), F32),
                   jax.ShapeDtypeStruct((SGU_GROUPS, BLOCK, BLOCK), F32), jax.ShapeDtypeStruct((SGU_GROUPS, BLOCK), F32),
                   jax.ShapeDtypeStruct((1, D_SGU), F32), jax.ShapeDtypeStruct((1, D_SGU), F32)),
        scratch_shapes=[pltpu.VMEM((BLOCK, 2 * D_KV), F32), pltpu.VMEM((2, 1, 8 * BLOCK), F32),
                        pltpu.VMEM((BLOCK, 128), F32)],
        compiler_params=_params(dimension_semantics=("arbitrary",)),
    )(z, z, da, probs, sink_probs, ln_g, ln_b, sgu_w, sgu_wt, sgu_bt)


def _out_proj_head(a, w_out_full, x, target, mod, final_g, tm=256):
    t, d = x.shape

    def body(a_ref, w_ref, x_ref, tg_ref, gate_ref, fg_ref, dx2_ref, dy_ref, loss_ref, dfg_ref, dgate_ref):
        @pl.when(pl.program_id(0) == 0)
        def _():
            loss_ref[...] = jnp.zeros_like(loss_ref)
            dfg_ref[...] = jnp.zeros_like(dfg_ref)
            dgate_ref[...] = jnp.zeros_like(dgate_ref)

        yv, gate, fg = _dot(a_ref[...], w_ref[...]), gate_ref[...], fg_ref[...]
        x2 = x_ref[...] + gate * yv
        r2 = lax.rsqrt(jnp.mean(x2 * x2, axis=-1, keepdims=True) + EPS)
        nrm = x2 * r2
        err = nrm * fg - tg_ref[...]
        loss_ref[...] += 0.5 * jnp.sum(jnp.mean(err * err, axis=-1, keepdims=True), axis=0, keepdims=True)
        fg_d = fg * (1.0 / d)
        err_nrm = err * nrm
        dfg_ref[...] += jnp.sum(err_nrm, axis=0, keepdims=True) * (1.0 / d)
        d_nrm = err * fg_d
        dx2 = r2 * (d_nrm - nrm * jnp.mean(err_nrm * fg_d, axis=-1, keepdims=True))
        dx2_ref[...] = dx2
        dgate_ref[...] += jnp.sum(dx2 * yv, axis=0, keepdims=True)
        dy_ref[...] = (dx2 * gate).astype(BF16)

    blk = pl.BlockSpec((tm, d), lambda i: (i, 0))
    row = _const_spec((1, d))
    whole = pl.BlockSpec(w_out_full.shape, lambda i: (0, 0), pipeline_mode=pl.Buffered(1))
    return pl.pallas_call(
        body, name="out_proj_head", grid=(t // tm,),
        in_specs=[pl.BlockSpec((tm, a.shape[1]), lambda i: (i, 0)), whole, blk, blk, _mod_spec(MOD_GATE, d), row],
        out_specs=(blk, blk, _const_spec((1, 128)), row, row),
        out_shape=(jax.ShapeDtypeStruct((t, d), F32), jax.ShapeDtypeStruct((t, d), BF16),
                   jax.ShapeDtypeStruct((1, 128), F32), jax.ShapeDtypeStruct((1, d), F32),
                   jax.ShapeDtypeStruct((1, d), F32)),
        compiler_params=_params(dimension_semantics=("arbitrary",)),
    )(a, w_out_full, x, target, mod, final_g)


def _z_proj_bwd_norm(dz, w_in_t, x, dx2, norm_g, mod, dep, tm=256):
    t, d = x.shape

    def body(dz_ref, w_ref, x_ref, dx2_ref, g_ref, sc_ref, dep_ref, gx_ref, dshift_ref, dscale_ref, dg_ref):
        @pl.when(pl.program_id(0) == 0)
        def _():
            dshift_ref[...] = jnp.zeros_like(dshift_ref)
            dscale_ref[...] = jnp.zeros_like(dscale_ref)
            dg_ref[...] = jnp.zeros_like(dg_ref)

        dh, xv, g = _dot(dz_ref[...], w_ref[...]), x_ref[...], g_ref[...]
        one_plus = 1.0 + sc_ref[...]
        r = lax.rsqrt(jnp.mean(xv * xv, axis=-1, keepdims=True) + EPS)
        xn = xv * r
        gain = one_plus * g
        dh_xn = dh * xn
        dh_xn_sum = jnp.sum(dh_xn, axis=0, keepdims=True)
        dshift_ref[...] += jnp.sum(dh, axis=0, keepdims=True)
        dscale_ref[...] += dh_xn_sum * g
        dg_ref[...] += dh_xn_sum * one_plus
        d_xn = dh * gain
        gx_ref[...] = dx2_ref[...] + r * (d_xn - xn * jnp.mean(dh_xn * gain, axis=-1, keepdims=True))

    blk = pl.BlockSpec((tm, d), lambda i: (i, 0))
    row = _const_spec((1, d))
    whole = pl.BlockSpec(w_in_t.shape, lambda i: (0, 0), pipeline_mode=pl.Buffered(1))
    return pl.pallas_call(
        body, name="z_proj_bwd_norm", grid=(t // tm,),
        in_specs=[pl.BlockSpec((tm, dz.shape[1]), lambda i: (i, 0)), whole, blk, blk, row, _mod_spec(MOD_SCALE, d),
                  _const_spec((8, 128))],
        out_specs=(blk, row, row, row),
        out_shape=(jax.ShapeDtypeStruct((t, d), F32),) + (jax.ShapeDtypeStruct((1, d), F32),) * 3,
        compiler_params=_params(dimension_semantics=("arbitrary",)),
    )(dz, w_in_t, x, dx2, norm_g, mod, dep)


def _adamw(w, g, m, v):
    m = ADAM_B1 * m + (1.0 - ADAM_B1) * g
    v = ADAM_B2 * v + (1.0 - ADAM_B2) * (g * g)
    m_hat = m / (1.0 - ADAM_B1 ** ADAM_STEP)
    v_hat = v / (1.0 - ADAM_B2 ** ADAM_STEP)
    delta = -ADAM_LR * (m_hat / (jnp.sqrt(v_hat) + ADAM_EPS) + ADAM_WD * w)
    return delta, m, v


def _relay_sum(device, blocks, land_pair, land_first, tr):
    _, r, c = blocks.shape

    def body(device_ref, a_ref, b_ref, c_ref, o_ref):
        o_ref[...] = (a_ref[...].astype(F32) + b_ref[...].astype(F32) + c_ref[...].astype(F32)).astype(BF16)

    second = pl.BlockSpec((None, tr, c), lambda i, device_ref: (1, i, 0))
    return pl.pallas_call(
        body, name="w_in_grad_relay_sum",
        grid_spec=pltpu.PrefetchScalarGridSpec(
            num_scalar_prefetch=1, grid=(r // tr,),
            in_specs=[pl.BlockSpec((None, tr, c), lambda i, device_ref: (device_ref[0], i, 0)), second, second],
            out_specs=pl.BlockSpec((tr, c), lambda i, device_ref: (i, 0))),
        out_shape=jax.ShapeDtypeStruct((r, c), BF16),
        compiler_params=_params(dimension_semantics=("arbitrary",)),
    )(device, blocks, land_pair, land_first)


def _adam_from_chips(chip, pair, landed, w, m, v, name, tc):
    _, r, c = pair.shape
    n = len(landed)

    def body(chip_ref, own_ref, *refs):
        w_ref, m_ref, v_ref, g_ref, d_ref, nm_ref, nv_ref = refs[n:]
        g = own_ref[...].astype(F32)
        for k in range(n):
            g = g + refs[k][...].astype(F32)
        g_ref[...] = g
        d_ref[...], nm_ref[...], nv_ref[...] = _adamw(w_ref[...], g, m_ref[...], v_ref[...])

    def landed_spec(index):
        return pl.BlockSpec((None, r, tc), lambda i, chip_ref: (index, 0, i))

    blk = pl.BlockSpec((r, tc), lambda i, chip_ref: (0, i))
    return pl.pallas_call(
        body, name=name,
        grid_spec=pltpu.PrefetchScalarGridSpec(
            num_scalar_prefetch=1, grid=(c // tc,),
            in_specs=[pl.BlockSpec((None, r, tc), lambda i, chip_ref: (chip_ref[0], 0, i))]
            + [landed_spec(index) for _, index in landed] + [blk, blk, blk],
            out_specs=(blk,) * 4),
        out_shape=(jax.ShapeDtypeStruct((r, c), F32),) * 4,
        compiler_params=_params(dimension_semantics=("arbitrary",)),
    )(chip, pair, *[array for array, _ in landed], w, m, v)


def _adam_w_ada(device, act, dmod_all, w, m, v, tr=512):
    r, c = w.shape

    def body(device_ref, a_ref, dm_ref, w_ref, m_ref, v_ref, g_ref, d_ref, nm_ref, nv_ref):
        g = lax.dot_general(a_ref[...].astype(BF16), dm_ref[...].astype(BF16), (((0,), (0,)), ((), ())),
                            preferred_element_type=F32)
        g_ref[...] = g
        d_ref[...], nm_ref[...], nv_ref[...] = _adamw(w_ref[...], g, m_ref[...], v_ref[...])

    blk = pl.BlockSpec((tr, c), lambda i, device_ref: (i, 0))
    return pl.pallas_call(
        body, name="adam_w_ada",
        grid_spec=pltpu.PrefetchScalarGridSpec(
            num_scalar_prefetch=1, grid=(r // tr,),
            in_specs=[pl.BlockSpec((N_DEV, tr), lambda i, device_ref: (0, i)),
                      pl.BlockSpec((N_DEV, c), lambda i, device_ref: (0, device_ref[0])), blk, blk, blk],
            out_specs=(blk,) * 4),
        out_shape=(jax.ShapeDtypeStruct((r, c), F32),) * 4,
        compiler_params=_params(dimension_semantics=("arbitrary",)),
    )(device, act, dmod_all, w, m, v)


def _pack_small(d_shift, d_scale, d_gate, d_norm_g, d_final_g, d_ln_g, d_ln_b, loss, d_sinks, d_sgu_b):
    def body(shift_ref, scale_ref, gate_ref, ng_ref, fg_ref, lng_ref, lnb_ref, loss_ref, sink_ref, b_ref, o_ref):
        o_ref[...] = jnp.zeros_like(o_ref)
        o_ref[ROW_SHIFT:ROW_SHIFT + 1, :] = shift_ref[...]
        o_ref[ROW_SCALE:ROW_SCALE + 1, :] = scale_ref[...]
        o_ref[ROW_GATE:ROW_GATE + 1, :] = gate_ref[...]
        o_ref[ROW_NORM_G:ROW_NORM_G + 1, :] = ng_ref[...]
        o_ref[ROW_FINAL_G:ROW_FINAL_G + 1, :] = fg_ref[...]
        o_ref[ROW_LN:ROW_LN + 1, 0:D_SGU] = lng_ref[...]
        o_ref[ROW_LN:ROW_LN + 1, D_SGU:2 * D_SGU] = lnb_ref[...]
        o_ref[ROW_MISC:ROW_MISC + 1, 0:128] = loss_ref[...]
        o_ref[ROW_MISC:ROW_MISC + 1, 128:256] = sink_ref[...]
        o_ref[ROW_SGU_B:ROW_SGU_B + SGU_GROUPS, 0:BLOCK] = b_ref[...]

    return pl.pallas_call(
        body, name="pack_small", out_shape=jax.ShapeDtypeStruct((SMALL_ROWS, D_MODEL), F32),
        compiler_params=_params(),
    )(d_shift, d_scale, d_gate, d_norm_g, d_final_g, d_ln_g, d_ln_b, loss, d_sinks, d_sgu_b)


_SMALL_NAMES = ("norm_g", "b_ada", "attn_sinks", "sgu_ln_g", "sgu_ln_b", "sgu_w", "sgu_b", "final_g")


def _adam_small(partials, d_sgu_w_all, weights, moments_m, moments_v):
    names = _SMALL_NAMES
    k = len(names)

    def body(*refs):
        p_ref, sw_ref = refs[0], refs[1]
        w_refs, m_refs, v_refs = refs[2:2 + k], refs[2 + k:2 + 2 * k], refs[2 + 2 * k:2 + 3 * k]
        loss_ref, dmod_ref = refs[2 + 3 * k], refs[3 + 3 * k]
        out_refs = refs[4 + 3 * k:4 + 7 * k]
        sum_ref = refs[4 + 7 * k]
        total = p_ref[0]
        for j in range(1, N_DEV):
            total = total + p_ref[j]
        sum_ref[...] = total
        for j in range(N_DEV):
            for part, row in enumerate((ROW_SHIFT, ROW_SCALE, ROW_GATE)):
                dmod_ref[j:j + 1, part * D_MODEL:(part + 1) * D_MODEL] = p_ref[j, row:row + 1, :]
        loss_ref[...] = sum_ref[ROW_MISC:ROW_MISC + 1, 0:1]
        d_sgu_w = sw_ref[0]
        for j in range(1, N_DEV):
            d_sgu_w = d_sgu_w + sw_ref[j]
        grads = {
            "norm_g": sum_ref[ROW_NORM_G:ROW_NORM_G + 1, :],
            "b_ada": jnp.concatenate([sum_ref[r:r + 1, :] for r in (ROW_SHIFT, ROW_SCALE, ROW_GATE)], axis=1),
            "attn_sinks": sum_ref[ROW_MISC:ROW_MISC + 1, 128:128 + N_Q_HEADS],
            "sgu_ln_g": sum_ref[ROW_LN:ROW_LN + 1, 0:D_SGU],
            "sgu_ln_b": sum_ref[ROW_LN:ROW_LN + 1, D_SGU:2 * D_SGU],
            "sgu_w": d_sgu_w[None],
            "sgu_b": sum_ref[ROW_SGU_B:ROW_SGU_B + SGU_GROUPS, 0:BLOCK][None],
            "final_g": sum_ref[ROW_FINAL_G:ROW_FINAL_G + 1, :],
        }
        for i, name in enumerate(names):
            g = grads[name]
            delta, m, v = _adamw(w_refs[i][...], g, m_refs[i][...], v_refs[i][...])
            out_refs[4 * i][...] = g
            out_refs[4 * i + 1][...] = delta
            out_refs[4 * i + 2][...] = m
            out_refs[4 * i + 3][...] = v

    shapes = [jax.ShapeDtypeStruct((1, 1), F32), jax.ShapeDtypeStruct((N_DEV, 3 * D_MODEL), F32)]
    for name in names:
        shapes += [jax.ShapeDtypeStruct(weights[name].shape, F32)] * 4
    outs = pl.pallas_call(
        body, name="adam_small", out_shape=tuple(shapes),
        scratch_shapes=[pltpu.VMEM((SMALL_ROWS, D_MODEL), F32)],
        compiler_params=_params(),
    )(partials, d_sgu_w_all, *[weights[n] for n in names], *[moments_m[n] for n in names],
      *[moments_v[n] for n in names])
    return outs[0], outs[1], {name: outs[2 + 4 * i:6 + 4 * i] for i, name in enumerate(names)}


def kernel(x, c, norm_g, w_ada, b_ada, w_in, attn_sinks, sgu_ln_g, sgu_ln_b, sgu_w, sgu_b, w_out, final_g, loss_target, m_norm_g, m_w_ada, m_b_ada, m_w_in, m_attn_sinks, m_sgu_ln_g, m_sgu_ln_b, m_sgu_w, m_sgu_b, m_w_out, m_final_g, v_norm_g, v_w_ada, v_b_ada, v_w_in, v_attn_sinks, v_sgu_ln_g, v_sgu_ln_b, v_sgu_w, v_sgu_b, v_w_out, v_final_g):
    xi, yi, ci = _place()
    me = 4 * xi + 2 * yi + ci
    x2d, target = x[0], loss_target[0]
    t = x2d.shape[0]

    core = ci.astype(jnp.int32).reshape(1)
    chip = (2 * xi + yi).astype(jnp.int32).reshape(1)

    first = _own_block_copies(_first_targets)
    first_flight = _start_copies([_with_own_slot(w_in[0].T.astype(BF16), me)], first, 2, core, "gather_w_in_start")

    c_all = _all_gather_small(c.reshape(8, 256) + first_flight[3][0, 0], "gather_c").reshape(N_DEV, D_MODEL)
    device = me.astype(jnp.int32).reshape(1)
    c_act, mod_part = _modulation(device, c_all, w_ada[0], b_ada)
    mod_all = _all_gather_small(mod_part, "gather_mod")

    across = _wait_then_start(first_flight, lambda *a: first(*a)[1:], _second_axis_stage_copies, 3, mod_all,
                              "gather_w_in_second_axis_stage")
    mod = lax.dynamic_index_in_dim(mod_all, me, axis=1, keepdims=False).reshape(1, 3 * D_MODEL)

    w_in_pair = _wait_copies((first_flight[0], first_flight[1], across[2], None), lambda *a: first(*a)[:1], mod,
                             "gather_w_in_sibling_wait")
    h, z_own = _norm_z_proj_own(x2d, norm_g, mod, w_in_pair[0].reshape(D_IN, D_MODEL), chip)
    w_out_early = _own_block_copies(lambda x, y, c: [(x, y, 1 - c), (*_second_axis_chip(x, y, c), c)])
    w_out_late = _own_block_copies(lambda x, y, c: [(*_first_axis_chip(x, y, c), c), (1 - x, 1 - y, c)])
    forward = _wait_then_start(
        (across[0], across[1], w_in_pair, None), lambda *a: _second_axis_stage_copies(*a)[:1],
        lambda refs, s, r: _second_axis_forward_copies(refs[:1], s, r) + _group(w_out_early, 1, 1, 1)(refs, s, r),
        3, z_own, "gather_w_in_second_axis_forward", more_bufs=[_with_own_slot(w_out[0].astype(BF16), me)])
    w_in_most = _wait_copies((across[0], across[1], forward[2][:1], None),
                             lambda *a: _second_axis_stage_copies(*a)[1:2], z_own, "gather_w_in_first_forward_wait")
    w_in_most = _wait_copies((forward[0], forward[1], w_in_most, None), _second_axis_forward_copies, z_own,
                             "gather_w_in_second_forward_wait")
    z_early = _z_proj(h, w_in_most[0].reshape(D_IN, D_MODEL), chip, 1, _Z_EARLY_TILES - 1, z_own, "z_proj_early")
    last = _wait_then_start(
        (across[0], across[1], [w_in_most[0], forward[2][1]], None), lambda *a: _second_axis_stage_copies(*a)[2:],
        lambda refs, s, r: _diagonal_forward_copies(refs[:1], s, r) + _group(w_out_late, 1, 1, 1)(refs, s, r),
        3, z_early, "gather_w_in_last_stage")
    w_in_all = _wait_copies((last[0], last[1], last[2][:1], None), _diagonal_forward_copies, z_early,
                            "gather_w_in_last_wait")[0]
    w_in_t = w_in_all.reshape(D_IN, D_MODEL)
    z = _z_proj(h, w_in_t, chip, _Z_EARLY_TILES, 7 - _Z_EARLY_TILES, z_early, "z_proj_late")
    w_out_half = _wait_copies((forward[0], forward[1], last[2][1:], None), _group(w_out_early, 0, 1, 1), z,
                              "gather_w_out_early_wait")
    w_out_flight = _wait_then_start((last[0], last[1], w_out_half, None), _group(w_out_late, 0, 1, 1),
                                    _forward_copies, 3, z, "gather_w_out_forward_stage")
    sink_rows = jnp.repeat(attn_sinks.reshape(N_Q_HEADS), BLOCK).reshape(2, 1, 8 * BLOCK)
    sgu_bt = sgu_b[0].T
    a, probs, sink_probs = _mixer_fwd(z, sink_rows, sgu_ln_g, sgu_ln_b, sgu_w[0], sgu_bt, w_out_flight[3])
    w_out_all = _wait_copies(w_out_flight, _forward_copies, a, "gather_w_out_forward_wait")[0]
    w_out_full = w_out_all.reshape(D_MODEL, D_MODEL)
    final_g_row = final_g.reshape(1, D_MODEL)
    dx2, dy, loss_part, d_final_g, d_gate = _out_proj_head(a, w_out_full, x2d, target, mod, final_g_row)

    da = _matmul(dy, w_out_full, "nt", F32, min(t, 1024), 1024, "out_proj_bwd")
    dw_out = _matmul(a, dy, "tn", BF16, 1024, 1024, "w_out_grad").reshape(4, 2, W_OUT_SHARD, D_MODEL)
    pair_out = _pair_reduce(dw_out, _every_chip, "w_out_grad_pair_reduce", W_OUT_SHARD // 2)
    dz, d_sinks, d_sgu_w, d_sgu_b, d_ln_g, d_ln_b = _mixer_bwd(
        z, da, probs, sink_probs, sgu_ln_g, sgu_ln_b, sgu_w[0], jnp.swapaxes(sgu_w[0], 1, 2), sgu_bt)
    sgu_w_to_all = _group(_own_block_copies(_all_others), 2, 1, 3)
    both = _start_copies(
        [pair_out, lax.empty((3, W_OUT_SHARD, D_MODEL), BF16), _with_own_slot(d_sgu_w, me)],
        lambda refs, s, r: _chip_copies(refs[:2], s, r) + sgu_w_to_all(refs, s, r), 3 + N_DEV - 1, core,
        "w_out_grad_chip_and_sgu_w_gather_start")
    out_flight, sgu_w_flight = (both[0], both[1], both[2][:2], None), (both[0], both[1], both[2][2:], None)
    dw_in_t = _matmul(dz, h, "tn", BF16, 768, D_MODEL, "w_in_grad", dep=both[3])
    pair_in = _pair_reduce(dw_in_t.reshape(4, 2, W_IN_SHARD, D_MODEL), _first_hop_chips, "w_in_grad_pair_reduce",
                           W_IN_SHARD // 3)
    first_hop = lambda refs, s, r: _first_hop_copies(refs[:2], s, r) + _group(_late_pair_copies, 2, 2, 2)(refs, s, r)
    hop1 = _start_copies(
        [pair_in, lax.empty((2, W_IN_SHARD, D_MODEL), BF16), dw_in_t.reshape(N_DEV, W_IN_SHARD, D_MODEL),
         lax.empty((2, W_IN_SHARD, D_MODEL), BF16)], first_hop, 4, core, "w_in_grad_first_hop_start")
    grad_x, d_shift, d_scale, d_norm_g = _z_proj_bwd_norm(dz, w_in_t, x2d, dx2, norm_g, mod, hop1[3])

    partial = _pack_small(d_shift, d_scale, d_gate, d_norm_g, d_final_g, d_ln_g, d_ln_b, loss_part, d_sinks, d_sgu_b)
    small_flight = _start_copies([_with_own_slot(partial, me)], _own_block_copies(_all_others), N_DEV - 1, core,
                                 "small_grad_gather_start")
    _, land_first, dw_in_t, land_pair = _wait_copies(hop1, first_hop, small_flight[3], "w_in_grad_first_hop_wait")
    second_device = (4 * ((xi + ci) % 2) + 2 * ((yi + 1 - ci) % 2) + ci).astype(jnp.int32).reshape(1)
    relay = _relay_sum(second_device, dw_in_t, land_pair, land_first, W_IN_SHARD // 3)
    hop2 = _start_copies([relay, lax.empty((1, W_IN_SHARD, D_MODEL), BF16)], _second_hop_copies, 1, core,
                         "w_in_grad_second_hop_start")
    pair_out, land_out = _wait_copies(out_flight, _chip_copies, hop2[3], "w_out_grad_chip_wait")
    big = {"w_out": _adam_from_chips(chip, pair_out, [(land_out, k) for k in range(3)], w_out[0], m_w_out[0],
                                     v_w_out[0], "adam_w_out", 1024)}
    partial_all = _wait_copies(small_flight, _own_block_copies(_all_others), big["w_out"][0],
                               "small_grad_gather_wait")[0]
    d_sgu_w_all = _wait_copies(sgu_w_flight, _group(_own_block_copies(_all_others), 0, 1, 3), partial_all,
                               "sgu_w_grad_gather_wait")[0]
    weights = {"norm_g": norm_g, "b_ada": b_ada, "attn_sinks": attn_sinks, "sgu_ln_g": sgu_ln_g,
               "sgu_ln_b": sgu_ln_b, "sgu_w": sgu_w, "sgu_b": sgu_b, "final_g": final_g_row}
    moments_m = {"norm_g": m_norm_g, "b_ada": m_b_ada, "attn_sinks": m_attn_sinks, "sgu_ln_g": m_sgu_ln_g,
                 "sgu_ln_b": m_sgu_ln_b, "sgu_w": m_sgu_w, "sgu_b": m_sgu_b,
                 "final_g": m_final_g.reshape(1, D_MODEL)}
    moments_v = {"norm_g": v_norm_g, "b_ada": v_b_ada, "attn_sinks": v_attn_sinks, "sgu_ln_g": v_sgu_ln_g,
                 "sgu_ln_b": v_sgu_ln_b, "sgu_w": v_sgu_w, "sgu_b": v_sgu_b,
                 "final_g": v_final_g.reshape(1, D_MODEL)}
    loss, dmod_all, small = _adam_small(partial_all, d_sgu_w_all, weights, moments_m, moments_v)
    small["final_g"] = tuple(o.reshape(D_MODEL) for o in small["final_g"])

    big["w_ada"] = _adam_w_ada(device, c_act, dmod_all, w_ada[0], m_w_ada[0], v_w_ada[0])
    _, land_second = _wait_copies(hop2, _second_hop_copies, big["w_ada"][0], "w_in_grad_second_hop_wait")
    big["w_in"] = tuple(o.T for o in _adam_from_chips(
        device, dw_in_t, [(land_pair, 0), (land_first, 0), (land_second, 0)], w_in[0].T, m_w_in[0].T, v_w_in[0].T,
        "adam_w_in", 512))
    order = ["norm_g", "w_ada", "b_ada", "w_in", "attn_sinks", "sgu_ln_g", "sgu_ln_b", "sgu_w", "sgu_b", "w_out",
             "final_g"]
    outs = [loss.reshape(()), grad_x[None]]
    for k in range(4):
        for name in order:
            outs.append(big[name][k][None] if name in big else small[name][k])
    return tuple(outs)
```

```python
import jax
import jax.numpy as jnp
from jax import lax
from jax.experimental import pallas as pl
from jax.experimental.pallas import tpu as pltpu

F32 = jnp.float32
BF16 = jnp.bfloat16
MESH = pl.DeviceIdType.MESH

N_DEV = 8
D_MODEL = 2048
HEAD_DIM = 64
D_ATTN = 1024
N_Q_HEADS = 16
D_KV = 128
BLOCK = 128
D_SGU = 1024
SGU_GROUPS = 8
D_IN = 5376
W_IN_SHARD = D_IN // N_DEV
W_OUT_SHARD = D_MODEL // N_DEV
W_ADA_SHARD = 3 * D_MODEL // N_DEV
EPS = 1e-6
ATTN_SCALE = 0.125

ADAM_LR = 0.001
ADAM_B1 = 0.9
ADAM_B2 = 0.999
ADAM_EPS = 1e-08
ADAM_WD = 0.01
ADAM_STEP = 10

SEG_Q, SEG_KV, SEG_GA, SEG_U, SEG_VS, SEG_GS = 0, 1024, 1280, 2304, 3328, 4352

VMEM_LIMIT = 56 * 1024 * 1024

ROW_SHIFT, ROW_SCALE, ROW_GATE, ROW_NORM_G, ROW_FINAL_G, ROW_LN, ROW_MISC, ROW_SGU_B = 0, 1, 2, 3, 4, 5, 6, 8
SMALL_ROWS = 16


def _params(**kw):
    return pltpu.CompilerParams(vmem_limit_bytes=VMEM_LIMIT, **kw)


def _sigmoid(x):
    return 0.5 * (jnp.tanh(0.5 * x) + 1.0)


def _place():
    return lax.axis_index("x"), lax.axis_index("y"), lax.axis_index("c")


def _every_chip(x, y, c):
    return [0, 1, 2, 3]


def _first_hop_chips(x, y, c):
    first = _first_axis_chip(x, y, c)
    return [2 * first[0] + first[1], 2 * (1 - x) + (1 - y)]


def _pair_reduce(blocks, chips, name, row_chunk):
    _, _, r, cols = blocks.shape
    n = len(chips(0, 0, 0))
    assert r % row_chunk == 0

    def body(in_ref, out_ref, land, own, summed, send_sems, recv_sems, own_sems, out_sems):
        x, y, c = _place()
        sends, loads, stores = [], [], []
        for m in range(n):
            cp = pltpu.make_async_remote_copy(
                src_ref=in_ref.at[chips(x, y, 1 - c)[m], 1 - c], dst_ref=land.at[m], send_sem=send_sems.at[m],
                recv_sem=recv_sems.at[m], device_id=(x, y, 1 - c), device_id_type=MESH)
            cp.start()
            sends.append(cp)
            ld = pltpu.make_async_copy(in_ref.at[chips(x, y, c)[m], c], own.at[m], own_sems.at[m])
            ld.start()
            loads.append(ld)
        for m in range(n):
            sends[m].wait_recv()
            loads[m].wait()
            for k in range(r // row_chunk):
                rows = slice(k * row_chunk, (k + 1) * row_chunk)
                summed[m, rows, :] = (own[m, rows, :].astype(F32) + land[m, rows, :].astype(F32)).astype(BF16)
            st = pltpu.make_async_copy(summed.at[m], out_ref.at[m], out_sems.at[m])
            st.start()
            stores.append(st)
        for m in range(n):
            sends[m].wait_send()
            stores[m].wait()

    spec = pl.BlockSpec(memory_space=pl.ANY)
    return pl.pallas_call(
        body, name=name, out_shape=jax.ShapeDtypeStruct((n, r, cols), BF16),
        in_specs=[spec], out_specs=spec,
        scratch_shapes=[pltpu.VMEM((n, r, cols), BF16), pltpu.VMEM((n, r, cols), BF16), pltpu.VMEM((n, r, cols), BF16),
                        pltpu.SemaphoreType.DMA((n,)), pltpu.SemaphoreType.DMA((n,)), pltpu.SemaphoreType.DMA((n,)),
                        pltpu.SemaphoreType.DMA((n,))],
        compiler_params=_params(),
    )(blocks)


_HBM = pl.BlockSpec(memory_space=pltpu.HBM)
_SEM = pl.BlockSpec(memory_space=pltpu.SEMAPHORE)
_EFFECT = pltpu.SideEffectType.DATAFLOW_SIDE_EFFECTING


def _start_copies(bufs, copies, n_copies, after, name):
    nb = len(bufs)

    def body(*refs):
        for cp in copies(refs[:nb], refs[nb + 1], refs[nb + 2]):
            cp.start()
        refs[-1][...] = jnp.zeros_like(refs[-1])

    out = pl.pallas_call(
        body, name=name,
        out_shape=(pltpu.SemaphoreType.DMA((n_copies,)), pltpu.SemaphoreType.DMA((n_copies,)),
                   *[pltpu.HBM(b.shape, b.dtype) for b in bufs], jax.ShapeDtypeStruct((8, 128), F32)),
        in_specs=(_HBM,) * nb + (pl.BlockSpec(memory_space=pl.ANY),),
        out_specs=(_SEM, _SEM) + (_HBM,) * nb + (pl.BlockSpec(memory_space=pltpu.VMEM),),
        input_output_aliases={i: 2 + i for i in range(nb)},
        compiler_params=pltpu.CompilerParams(has_side_effects=_EFFECT),
    )(*[pltpu.with_memory_space_constraint(b, pltpu.HBM) for b in bufs], after)
    return out[0], out[1], list(out[2:2 + nb]), out[-1]


def _wait_copies(flight, copies, after, name):
    send_sems, recv_sems, bufs, _ = flight
    nb = len(bufs)

    def body(*refs):
        for cp in copies(refs[:nb], refs[nb], refs[nb + 1]):
            cp.wait_send()
            cp.wait_recv()

    return pl.pallas_call(
        body, name=name,
        out_shape=tuple(pltpu.HBM(b.shape, b.dtype) for b in bufs),
        in_specs=(_HBM,) * nb + (_SEM, _SEM, pl.BlockSpec(memory_space=pl.ANY)), out_specs=(_HBM,) * nb,
        input_output_aliases={i: i for i in range(nb)},
        compiler_params=pltpu.CompilerParams(has_side_effects=_EFFECT),
    )(*bufs, send_sems, recv_sems, after)


class _From:
    def __init__(self, sems, offset):
        self.sems, self.offset = sems, offset

    @property
    def at(self):
        return self

    def __getitem__(self, k):
        return self.sems.at[k + self.offset]


def _group(copies, first_buf, n_bufs, offset):
    def grouped(refs, send_sems, recv_sems):
        return copies(refs[first_buf:first_buf + n_bufs], _From(send_sems, offset), _From(recv_sems, offset))
    return grouped


def _wait_then_start(flight, waited, started, n_started, after, name, more_bufs=()):
    old_send, old_recv, bufs, _ = flight
    bufs = list(bufs) + [pltpu.with_memory_space_constraint(b, pltpu.HBM) for b in more_bufs]
    nb = len(bufs)

    def body(*refs):
        for cp in waited(refs[:nb], refs[nb], refs[nb + 1]):
            cp.wait_send()
            cp.wait_recv()
        for cp in started(refs[:nb], refs[nb + 3], refs[nb + 4]):
            cp.start()
        refs[-1][...] = jnp.zeros_like(refs[-1])

    out = pl.pallas_call(
        body, name=name,
        out_shape=(pltpu.SemaphoreType.DMA((n_started,)), pltpu.SemaphoreType.DMA((n_started,)),
                   *[pltpu.HBM(b.shape, b.dtype) for b in bufs], jax.ShapeDtypeStruct((8, 128), F32)),
        in_specs=(_HBM,) * nb + (_SEM, _SEM, pl.BlockSpec(memory_space=pl.ANY)),
        out_specs=(_SEM, _SEM) + (_HBM,) * nb + (pl.BlockSpec(memory_space=pltpu.VMEM),),
        input_output_aliases={i: 2 + i for i in range(nb)},
        compiler_params=pltpu.CompilerParams(has_side_effects=_EFFECT),
    )(*bufs, old_send, old_recv, after)
    return out[0], out[1], list(out[2:2 + nb]), out[-1]


def _late_pair_copies(refs, send_sems, recv_sems):
    blocks_ref, land_ref = refs
    x, y, c = _place()
    first = _first_axis_chip(x, y, c)
    devices = [4 * x + 2 * y + 1 - c, 4 * first[0] + 2 * first[1] + 1 - c]
    return [pltpu.make_async_remote_copy(
        src_ref=blocks_ref.at[devices[k]], dst_ref=land_ref.at[k], send_sem=send_sems.at[k], recv_sem=recv_sems.at[k],
        device_id=(x, y, 1 - c), device_id_type=MESH) for k in range(2)]


def _chip_copies(refs, send_sems, recv_sems):
    pair_ref, land_ref = refs
    x, y, c = _place()
    chips = [(1 - x, y), (x, 1 - y), (1 - x, 1 - y)]
    return [pltpu.make_async_remote_copy(
        src_ref=pair_ref.at[2 * chip[0] + chip[1]], dst_ref=land_ref.at[k],
        send_sem=send_sems.at[k], recv_sem=recv_sems.at[k],
        device_id=(*chip, c), device_id_type=MESH) for k, chip in enumerate(chips)]


def _first_hop_copies(refs, send_sems, recv_sems):
    pair_ref, land_ref = refs
    x, y, c = _place()
    return [pltpu.make_async_remote_copy(
        src_ref=pair_ref.at[k], dst_ref=land_ref.at[k], send_sem=send_sems.at[k], recv_sem=recv_sems.at[k],
        device_id=(*_first_axis_chip(x, y, c), c), device_id_type=MESH) for k in range(2)]


def _second_hop_copies(refs, send_sems, recv_sems):
    relay_ref, land_ref = refs
    x, y, c = _place()
    second = ((x + c) % 2, (y + 1 - c) % 2)
    return [pltpu.make_async_remote_copy(
        src_ref=relay_ref, dst_ref=land_ref.at[0], send_sem=send_sems.at[0], recv_sem=recv_sems.at[0],
        device_id=(*second, c), device_id_type=MESH)]


def _own_block_copies(targets):
    def copies(refs, send_sems, recv_sems):
        x, y, c = _place()
        mine = refs[0].at[4 * x + 2 * y + c]
        return [pltpu.make_async_remote_copy(
            src_ref=mine, dst_ref=mine, send_sem=send_sems.at[k], recv_sem=recv_sems.at[k],
            device_id=to, device_id_type=MESH) for k, to in enumerate(targets(x, y, c))]
    return copies


def _all_others(x, y, c):
    flip = lambda v, f: 1 - v if f else v
    return [(flip(x, r & 4), flip(y, r & 2), flip(c, r & 1)) for r in range(1, N_DEV)]


def _forward_copies(refs, send_sems, recv_sems):
    x, y, c = _place()
    chips = [(1 - x, y), (x, 1 - y), (1 - x, 1 - y)]
    return [pltpu.make_async_remote_copy(
        src_ref=refs[0].at[4 * chip[0] + 2 * chip[1] + c], dst_ref=refs[0].at[4 * chip[0] + 2 * chip[1] + c],
        send_sem=send_sems.at[k], recv_sem=recv_sems.at[k],
        device_id=(x, y, 1 - c), device_id_type=MESH) for k, chip in enumerate(chips)]


def _first_axis_chip(x, y, c):
    return (x + 1 - c) % 2, (y + c) % 2


def _second_axis_chip(x, y, c):
    return (x + c) % 2, (y + 1 - c) % 2


def _first_targets(x, y, c):
    return [(x, y, 1 - c), (*_first_axis_chip(x, y, c), c)]


def _all_gather_small(shard, name):
    def body(in_ref, out_ref, send_sems, recv_sems, local_sem):
        x, y, c = _place()
        me, sibling = 4 * x + 2 * y + c, (x, y, 1 - c)
        first, second = _first_axis_chip(x, y, c), _second_axis_chip(x, y, c)

        def pair(chip):
            return out_ref.at[pl.ds(2 * (2 * chip[0] + chip[1]), 2)]

        def exchange(k, src, dst, to):
            cp = pltpu.make_async_remote_copy(src_ref=src, dst_ref=dst, send_sem=send_sems.at[k],
                                              recv_sem=recv_sems.at[k], device_id=to, device_id_type=MESH)
            cp.start()
            cp.wait()

        own = pltpu.make_async_copy(in_ref, out_ref.at[me], local_sem)
        own.start()
        exchange(0, in_ref, out_ref.at[me], sibling)
        own.wait()
        exchange(1, pair((x, y)), pair((x, y)), (*second, c))
        exchange(2, pair(second), pair(second), sibling)
        exchange(3, pair(first), pair(first), (*second, c))

    spec = pl.BlockSpec(memory_space=pltpu.VMEM)
    return pl.pallas_call(
        body, name=name, out_shape=jax.ShapeDtypeStruct((N_DEV,) + shard.shape, shard.dtype),
        in_specs=[spec], out_specs=spec,
        scratch_shapes=[pltpu.SemaphoreType.DMA((4,)), pltpu.SemaphoreType.DMA((4,)), pltpu.SemaphoreType.DMA],
        compiler_params=_params(),
    )(shard)


def _slot_copies(refs, send_sems, recv_sems, plan):
    copies = []
    for k, ((px, py, pc), to) in enumerate(plan):
        blk = refs[0].at[4 * px + 2 * py + pc]
        copies.append(pltpu.make_async_remote_copy(
            src_ref=blk, dst_ref=blk, send_sem=send_sems.at[k], recv_sem=recv_sems.at[k],
            device_id=to, device_id_type=MESH))
    return copies


def _second_axis_stage_copies(refs, send_sems, recv_sems):
    x, y, c = _place()
    first, second = (*_first_axis_chip(x, y, c), c), (*_second_axis_chip(x, y, c), c)
    return _slot_copies(refs, send_sems, recv_sems, [((x, y, c), second), (first, (x, y, 1 - c)), (first, second)])


def _second_axis_forward_copies(refs, send_sems, recv_sems):
    x, y, c = _place()
    return _slot_copies(refs, send_sems, recv_sems, [((*_second_axis_chip(x, y, c), c), (x, y, 1 - c))])


def _diagonal_forward_copies(refs, send_sems, recv_sems):
    x, y, c = _place()
    blk = refs[0].at[4 * (1 - x) + 2 * (1 - y) + c]
    return [pltpu.make_async_remote_copy(
        src_ref=blk, dst_ref=blk, send_sem=send_sems.at[0], recv_sem=recv_sems.at[0],
        device_id=(x, y, 1 - c), device_id_type=MESH)]


def _with_own_slot(block, me):
    return lax.dynamic_update_index_in_dim(lax.empty((N_DEV,) + block.shape, block.dtype), block, me, 0)


def _matmul(a, b, dims, out_dtype, tm, tn, name, dep=None):
    if dims == "nn":
        (m, k), n = a.shape, b.shape[1]
        a_spec = pl.BlockSpec((tm, k), lambda i, j: (i, 0))
        b_spec = pl.BlockSpec((k, tn), lambda i, j: (0, j))
        contract = ((1,), (0,))
    elif dims == "nt":
        (m, k), n = a.shape, b.shape[0]
        a_spec = pl.BlockSpec((tm, k), lambda i, j: (i, 0))
        b_spec = pl.BlockSpec((tn, k), lambda i, j: (j, 0))
        contract = ((1,), (1,))
    else:
        (k, m), n = a.shape, b.shape[1]
        a_spec = pl.BlockSpec((k, tm), lambda i, j: (0, i))
        b_spec = pl.BlockSpec((k, tn), lambda i, j: (0, j))
        contract = ((0,), (0,))
    assert m % tm == 0 and n % tn == 0 and a.dtype == BF16 and b.dtype == BF16

    def body(a_ref, b_ref, *rest):
        rest[-1][...] = lax.dot_general(a_ref[...], b_ref[...], (contract, ((), ())),
                                        preferred_element_type=F32).astype(out_dtype)

    deps = [] if dep is None else [dep]
    return pl.pallas_call(
        body, name=name, grid=(m // tm, n // tn),
        in_specs=[a_spec, b_spec] + [pl.BlockSpec((8, 128), lambda i, j: (0, 0))] * len(deps),
        out_specs=pl.BlockSpec((tm, tn), lambda i, j: (i, j)),
        out_shape=jax.ShapeDtypeStruct((m, n), out_dtype),
        compiler_params=_params(dimension_semantics=("arbitrary", "arbitrary")),
    )(a, b, *deps)


Z_TILE = 768
_Z_TILE_ORDER = ((0, 1, 2, 3, 4, 5, 6), (2, 0, 1, 6, 3, 4, 5), (4, 0, 5, 6, 1, 2, 3), (6, 2, 3, 4, 0, 1, 5))
_Z_EARLY_TILES = 4


def _z_proj(h, w_in_t, chip, first, count, z_prev, name, tr=1024):
    t = h.shape[0]

    def body(chip_ref, h_ref, w_ref, z_prev_ref, z_ref):
        z_ref[...] = _dot_nt(h_ref[...], w_ref[...])

    def tile(j, chip_ref):
        picked = 0
        for c, order in enumerate(_Z_TILE_ORDER):
            for k in range(count):
                picked = picked + jnp.where((chip_ref[0] == c) & (j == k), order[first + k], 0)
        return picked

    return pl.pallas_call(
        body, name=name,
        grid_spec=pltpu.PrefetchScalarGridSpec(
            num_scalar_prefetch=1, grid=(count, t // tr),
            in_specs=[pl.BlockSpec((tr, D_MODEL), lambda j, i, o: (i, 0)),
                      pl.BlockSpec((Z_TILE, D_MODEL), lambda j, i, o: (tile(j, o), 0)),
                      pl.BlockSpec(memory_space=pl.ANY)],
            out_specs=pl.BlockSpec((tr, Z_TILE), lambda j, i, o: (i, tile(j, o)))),
        out_shape=jax.ShapeDtypeStruct((t, D_IN), F32),
        input_output_aliases={3: 0},
        compiler_params=_params(dimension_semantics=("arbitrary", "arbitrary")),
    )(chip, h, w_in_t, z_prev)


def _modulation(device, c_all, w_ada, b_ada):
    def body(device_ref, c_ref, w_ref, b_ref, act_ref, mod_ref):
        cv = c_ref[...]
        act = cv * _sigmoid(cv)
        act_ref[...] = act
        mod_ref[...] = jnp.dot(act.astype(BF16), w_ref[...].astype(BF16), preferred_element_type=F32) + b_ref[...]

    whole = lambda a: pl.BlockSpec(a.shape, lambda i, device_ref: (0,) * a.ndim)
    return pl.pallas_call(
        body, name="modulation",
        grid_spec=pltpu.PrefetchScalarGridSpec(
            num_scalar_prefetch=1, grid=(1,),
            in_specs=[whole(c_all), whole(w_ada), pl.BlockSpec((1, W_ADA_SHARD), lambda i, device_ref: (0, device_ref[0]))],
            out_specs=(whole(c_all), pl.BlockSpec((N_DEV, W_ADA_SHARD), lambda i, device_ref: (0, 0)))),
        out_shape=(jax.ShapeDtypeStruct(c_all.shape, F32), jax.ShapeDtypeStruct((N_DEV, W_ADA_SHARD), F32)),
        compiler_params=_params(dimension_semantics=("arbitrary",)),
    )(device, c_all, w_ada, b_ada)


MOD_SHIFT, MOD_SCALE, MOD_GATE = 0, 1, 2


def _mod_spec(part, d):
    return pl.BlockSpec((1, d), lambda i: (0, part))


def _norm_z_proj_own(x, norm_g, mod, w_in_t, chip, tm=512):
    t, d = x.shape

    def body(chip_ref, x_ref, g_ref, sc_ref, sh_ref, w_ref, h_ref, z_ref):
        xv = x_ref[...]
        r = lax.rsqrt(jnp.mean(xv * xv, axis=-1, keepdims=True) + EPS)
        h = ((xv * r) * g_ref[...] * (1.0 + sc_ref[...]) + sh_ref[...]).astype(BF16)
        h_ref[...] = h
        z_ref[...] = _dot_nt(h, w_ref[...])

    def own_tile(chip_ref):
        picked = 0
        for c, order in enumerate(_Z_TILE_ORDER):
            picked = picked + jnp.where(chip_ref[0] == c, order[0], 0)
        return picked

    def row(part):
        return pl.BlockSpec((1, d), lambda i, o: (0, part))

    return pl.pallas_call(
        body, name="norm_z_proj_own",
        grid_spec=pltpu.PrefetchScalarGridSpec(
            num_scalar_prefetch=1, grid=(t // tm,),
            in_specs=[pl.BlockSpec((tm, d), lambda i, o: (i, 0)), row(0), row(MOD_SCALE), row(MOD_SHIFT),
                      pl.BlockSpec((Z_TILE, d), lambda i, o: (own_tile(o), 0))],
            out_specs=(pl.BlockSpec((tm, d), lambda i, o: (i, 0)),
                       pl.BlockSpec((tm, Z_TILE), lambda i, o: (i, own_tile(o))))),
        out_shape=(jax.ShapeDtypeStruct((t, d), BF16), jax.ShapeDtypeStruct((t, D_IN), F32)),
        compiler_params=_params(dimension_semantics=("arbitrary",)),
    )(chip, x, norm_g, mod, mod, w_in_t)


def _window_bias(block_index):
    s = lax.broadcasted_iota(jnp.int32, (2 * BLOCK, BLOCK), 0)
    t = lax.broadcasted_iota(jnp.int32, (2 * BLOCK, BLOCK), 1)
    valid = ((s < BLOCK) & (s > t) & (block_index > 0)) | ((s >= BLOCK) & ((s - BLOCK) <= t))
    bias = jnp.where(valid, 0.0, -jnp.inf).astype(F32)
    return jnp.concatenate([bias] * 8, axis=1)


def _heads_t(pair_blocks, g):
    top = lax.broadcasted_iota(jnp.int32, (BLOCK, BLOCK), 0) < HEAD_DIM
    zeros = jnp.zeros((HEAD_DIM, BLOCK), F32)
    tiles = []
    for blk in pair_blocks:
        tp = blk.T
        if g == 0:
            tiles += [jnp.where(top, tp, 0.0), jnp.concatenate([tp[HEAD_DIM:], zeros], axis=0)]
        else:
            tiles += [jnp.concatenate([zeros, tp[:HEAD_DIM]], axis=0), jnp.where(top, 0.0, tp)]
    return jnp.concatenate(tiles, axis=1)


def _pair_block(xt, p, g):
    r0 = HEAD_DIM * g
    even = xt[r0:r0 + HEAD_DIM, (2 * p) * BLOCK:(2 * p + 1) * BLOCK]
    odd = xt[r0:r0 + HEAD_DIM, (2 * p + 1) * BLOCK:(2 * p + 2) * BLOCK]
    return jnp.concatenate([even, odd], axis=0).T


def _softmax_t(scores_t, bias, sink):
    st = scores_t + bias
    m = jnp.maximum(jnp.max(st, axis=0, keepdims=True), sink)
    e = jnp.exp(st - m)
    es = jnp.exp(sink - m)
    inv = 1.0 / (jnp.sum(e, axis=0, keepdims=True) + es)
    return e * inv, es * inv


def _dot(a, b):
    return jnp.dot(a, b, preferred_element_type=F32)


def _dot_nt(a, b):
    return lax.dot_general(a, b, (((1,), (1,)), ((), ())), preferred_element_type=F32)


def _layer_norm_fwd(v):
    mu = jnp.mean(v, axis=-1, keepdims=True)
    xc = v - mu
    rstd = lax.rsqrt(jnp.mean(xc * xc, axis=-1, keepdims=True) + EPS)
    return xc * rstd, rstd


def _tril(transposed=False):
    t = lax.broadcasted_iota(jnp.int32, (BLOCK, BLOCK), 0)
    s = lax.broadcasted_iota(jnp.int32, (BLOCK, BLOCK), 1)
    return s >= t if transposed else t >= s


def _const_spec(shape):
    return pl.BlockSpec(shape, lambda i: (0,) * len(shape))


def _keys_values(z_ref, kvp):
    kvc = z_ref[:, SEG_KV:SEG_KV + 2 * D_KV]
    kk = jnp.concatenate([kvp[:, :D_KV], kvc[:, :D_KV]], axis=0)
    vv = jnp.concatenate([kvp[:, D_KV:], kvc[:, D_KV:]], axis=0)
    return kk, vv


MIXER_BLOCKS = 2


class _Rows:
    def __init__(self, ref, sub):
        self.ref, self.rows = ref, slice(sub * BLOCK, (sub + 1) * BLOCK)

    def __getitem__(self, idx):
        return self.ref[self.rows, idx[1]]

    def __setitem__(self, idx, value):
        self.ref[self.rows, idx[1]] = value


def _kv_before_spec(index):
    return pl.BlockSpec((BLOCK, 2 * D_KV),
                        lambda i: (jnp.maximum(MIXER_BLOCKS * index(i) - 1, 0), SEG_KV // (2 * D_KV)))


def _pair_cols(g, p, base=0):
    return slice(base + (4 * g + p) * 128, base + (4 * g + p + 1) * 128)


def _mixer_fwd(z, sink_rows, ln_g, ln_b, sgu_w, sgu_bt, dep):
    t = z.shape[0]

    def body(z_all, kvp_ref, sink_ref, lng_ref, lnb_ref, w_ref, bt_ref, dep_ref, a_all, prob_ref, sink_prob_ref):
        kv_before = kvp_ref[...]
        for sub in range(MIXER_BLOCKS):
            z_ref, a_ref = _Rows(z_all, sub), _Rows(a_all, sub)
            one_block(z_ref, kv_before, MIXER_BLOCKS * pl.program_id(0) + sub, sink_ref, lng_ref, lnb_ref, w_ref,
                      bt_ref, a_ref, prob_ref.at[sub], sink_prob_ref.at[sub])
            kv_before = z_ref[:, SEG_KV:SEG_KV + 2 * D_KV]

    def one_block(z_ref, kv_before, block_index, sink_ref, lng_ref, lnb_ref, w_ref, bt_ref, a_ref, prob_ref,
                  sink_prob_ref):
        bias = _window_bias(block_index)
        kk, vv = _keys_values(z_ref, kv_before)
        kk_b, vvt_b = kk.astype(BF16), vv.T.astype(BF16)
        for g in range(2):
            qt = _heads_t([z_ref[:, _pair_cols(g, p, SEG_Q)] * ATTN_SCALE for p in range(4)], g).astype(BF16)
            prob, sink_prob = _softmax_t(_dot(kk_b, qt), bias, sink_ref[g])
            prob_b = prob.astype(BF16)
            prob_ref[g] = prob_b
            sink_prob_ref[g] = sink_prob
            ot = _dot(vvt_b, prob_b)
            for p in range(4):
                gate = z_ref[:, _pair_cols(g, p, SEG_GA)]
                a_ref[:, _pair_cols(g, p)] = (_pair_block(ot, p, g) * (gate * _sigmoid(gate))).astype(BF16)

        vhat, _ = _layer_norm_fwd(z_ref[:, SEG_VS:SEG_VS + D_SGU])
        vn = vhat * lng_ref[...] + lnb_ref[...]
        tril = _tril()
        for g in range(SGU_GROUPS):
            cols = slice(g * 128, (g + 1) * 128)
            wm = jnp.where(tril, w_ref[g], 0.0).astype(BF16)
            mixed = _dot(wm, vn[:, cols].astype(BF16)) + bt_ref[:, g:g + 1]
            gate = z_ref[:, SEG_GS + g * 128:SEG_GS + (g + 1) * 128]
            a_ref[:, D_ATTN + g * 128:D_ATTN + (g + 1) * 128] = (
                (z_ref[:, SEG_U + g * 128:SEG_U + (g + 1) * 128] * mixed) * (gate * _sigmoid(gate))).astype(BF16)

    rows = MIXER_BLOCKS * BLOCK
    return pl.pallas_call(
        body, name="mixer_fwd", grid=(t // rows,),
        in_specs=[pl.BlockSpec((rows, D_IN), lambda i: (i, 0)), _kv_before_spec(lambda i: i),
                  _const_spec((2, 1, 8 * BLOCK)), _const_spec((1, D_SGU)), _const_spec((1, D_SGU)),
                  _const_spec((SGU_GROUPS, BLOCK, BLOCK)), _const_spec((BLOCK, SGU_GROUPS)), _const_spec((8, 128))],
        out_specs=(pl.BlockSpec((rows, D_MODEL), lambda i: (i, 0)),
                   pl.BlockSpec((MIXER_BLOCKS, 2, 2 * BLOCK, 8 * BLOCK), lambda i: (i, 0, 0, 0)),
                   pl.BlockSpec((MIXER_BLOCKS, 2, 1, 8 * BLOCK), lambda i: (i, 0, 0, 0))),
        out_shape=(jax.ShapeDtypeStruct((t, D_MODEL), BF16),
                   jax.ShapeDtypeStruct((t // BLOCK, 2, 2 * BLOCK, 8 * BLOCK), BF16),
                   jax.ShapeDtypeStruct((t // BLOCK, 2, 1, 8 * BLOCK), F32)),
        compiler_params=_params(dimension_semantics=("arbitrary",)),
    )(z, z, sink_rows, ln_g, ln_b, sgu_w, sgu_bt, dep)


def _mixer_bwd(z, da, probs, sink_probs, ln_g, ln_b, sgu_w, sgu_wt, sgu_bt):
    t = z.shape[0]

    def body(z_all, kvp_ref, da_all, prob_ref, sink_prob_ref, lng_ref, lnb_ref, w_ref, wt_ref, bt_ref,
             dz_all, dsink_ref, dw_ref, db_ref, dlng_ref, dlnb_ref, carry_ref, dsink_acc, dbt_acc):
        step = pl.program_id(0)

        @pl.when(step == 0)
        def _():
            carry_ref[...] = jnp.zeros_like(carry_ref)
            dsink_acc[...] = jnp.zeros_like(dsink_acc)
            dbt_acc[...] = jnp.zeros_like(dbt_acc)
            dw_ref[...] = jnp.zeros_like(dw_ref)
            dlng_ref[...] = jnp.zeros_like(dlng_ref)
            dlnb_ref[...] = jnp.zeros_like(dlnb_ref)

        carry = carry_ref[...]
        for sub in reversed(range(MIXER_BLOCKS)):
            kv_before = kvp_ref[...] if sub == 0 else _Rows(z_all, sub - 1)[:, SEG_KV:SEG_KV + 2 * D_KV]
            carry = one_block(_Rows(z_all, sub), kv_before, _Rows(da_all, sub), prob_ref.at[sub], sink_prob_ref.at[sub],
                              carry, lng_ref, lnb_ref, w_ref, wt_ref, bt_ref, _Rows(dz_all, sub),
                              dw_ref, dlng_ref, dlnb_ref, dsink_acc, dbt_acc)
        carry_ref[...] = carry

        @pl.when(step == ns - 1)
        def _():
            db_ref[...] = dbt_acc[...].T[:SGU_GROUPS]
            lane_row = lax.broadcasted_iota(jnp.int32, (1, 128), 1)
            d_sink = jnp.zeros((1, 128), F32)
            for g in range(2):
                acc = dsink_acc[g]
                for j in range(8):
                    head_sum = jnp.sum(acc[:, j * BLOCK:(j + 1) * BLOCK], axis=-1, keepdims=True)
                    d_sink = d_sink + jnp.where(lane_row == 8 * g + j, head_sum, 0.0)
            dsink_ref[...] = d_sink

    def one_block(z_ref, kv_before, da_ref, prob_ref, sink_prob_ref, carry, lng_ref, lnb_ref, w_ref, wt_ref, bt_ref,
                  dz_ref, dw_ref, dlng_ref, dlnb_ref, dsink_acc, dbt_acc):
        kk, vv = _keys_values(z_ref, kv_before)
        vv_b = vv.astype(BF16)
        kkt_b, vvt_b = kk.T.astype(BF16), vv.T.astype(BF16)
        dkk = jnp.zeros((2 * BLOCK, D_KV), F32)
        dvv = jnp.zeros((2 * BLOCK, D_KV), F32)
        for g in range(2):
            qt = _heads_t([z_ref[:, _pair_cols(g, p, SEG_Q)] * ATTN_SCALE for p in range(4)], g).astype(BF16)
            prob_b, sink_prob = prob_ref[g], sink_prob_ref[g]
            prob = prob_b.astype(F32)
            ot = _dot(vvt_b, prob_b)
            gates = [z_ref[:, _pair_cols(g, p, SEG_GA)] for p in range(4)]
            sig = [_sigmoid(gt) for gt in gates]
            d_attn = [da_ref[:, _pair_cols(g, p)] for p in range(4)]
            d_ot = _heads_t([d_attn[p] * (gates[p] * sig[p]) for p in range(4)], g).astype(BF16)
            d_prob = _dot(vv_b, d_ot)
            delta = jnp.sum(prob * d_prob, axis=0, keepdims=True)
            d_scores = (prob * (d_prob - delta)).astype(BF16)
            dsink_acc[g] -= sink_prob * delta
            d_qt = _dot(kkt_b, d_scores)
            dkk = dkk + _dot_nt(d_scores, qt)
            dvv = dvv + _dot_nt(prob_b, d_ot)
            for p in range(4):
                dz_ref[:, _pair_cols(g, p, SEG_Q)] = (_pair_block(d_qt, p, g) * ATTN_SCALE).astype(BF16)
                d_silu = sig[p] * (1.0 + gates[p] * (1.0 - sig[p]))
                dz_ref[:, _pair_cols(g, p, SEG_GA)] = (d_attn[p] * _pair_block(ot, p, g) * d_silu).astype(BF16)
        d_kv = jnp.concatenate([dkk, dvv], axis=1)
        dz_ref[:, SEG_KV:SEG_KV + 2 * D_KV] = (d_kv[BLOCK:] + carry).astype(BF16)

        vhat, rstd = _layer_norm_fwd(z_ref[:, SEG_VS:SEG_VS + D_SGU])
        lng = lng_ref[...]
        vn = vhat * lng + lnb_ref[...]
        tril, triu = _tril(), _tril(transposed=True)
        lane = lax.broadcasted_iota(jnp.int32, (BLOCK, 128), 1)
        d_bt = jnp.zeros((BLOCK, 128), F32)
        d_vn = []
        for g in range(SGU_GROUPS):
            cols = slice(g * 128, (g + 1) * 128)
            wm = jnp.where(tril, w_ref[g], 0.0).astype(BF16)
            wmt = jnp.where(triu, wt_ref[g], 0.0).astype(BF16)
            vn_g = vn[:, cols].astype(BF16)
            mixed = _dot(wm, vn_g) + bt_ref[:, g:g + 1]
            gate = z_ref[:, SEG_GS + g * 128:SEG_GS + (g + 1) * 128]
            u = z_ref[:, SEG_U + g * 128:SEG_U + (g + 1) * 128]
            d_out = da_ref[:, D_ATTN + g * 128:D_ATTN + (g + 1) * 128]
            sg = _sigmoid(gate)
            d_um = d_out * (gate * sg)
            dz_ref[:, SEG_U + g * 128:SEG_U + (g + 1) * 128] = (d_um * mixed).astype(BF16)
            dz_ref[:, SEG_GS + g * 128:SEG_GS + (g + 1) * 128] = (
                d_out * (u * mixed) * (sg * (1.0 + gate * (1.0 - sg)))).astype(BF16)
            d_mixed = d_um * u
            d_mixed_b = d_mixed.astype(BF16)
            dw_ref[g] += jnp.where(tril, _dot_nt(d_mixed_b, vn_g), 0.0)
            d_bt = d_bt + jnp.where(lane == g, jnp.sum(d_mixed, axis=-1, keepdims=True), 0.0)
            d_vn.append(_dot(wmt, d_mixed_b))
        dbt_acc[...] += d_bt
        d_vn = jnp.concatenate(d_vn, axis=1)
        dlng_ref[...] += jnp.sum(d_vn * vhat, axis=0, keepdims=True)
        dlnb_ref[...] += jnp.sum(d_vn, axis=0, keepdims=True)
        d_vhat = d_vn * lng
        d_v = rstd * (d_vhat - jnp.mean(d_vhat, axis=-1, keepdims=True)
                      - vhat * jnp.mean(d_vhat * vhat, axis=-1, keepdims=True))
        dz_ref[:, SEG_VS:SEG_VS + D_SGU] = d_v.astype(BF16)
        return d_kv[:BLOCK]

    rows = MIXER_BLOCKS * BLOCK
    ns = t // rows
    rev = lambda i: ns - 1 - i
    return pl.pallas_call(
        body, name="mixer_bwd", grid=(ns,),
        in_specs=[pl.BlockSpec((rows, D_IN), lambda i: (rev(i), 0)), _kv_before_spec(rev),
                  pl.BlockSpec((rows, D_MODEL), lambda i: (rev(i), 0)),
                  pl.BlockSpec((MIXER_BLOCKS, 2, 2 * BLOCK, 8 * BLOCK), lambda i: (rev(i), 0, 0, 0)),
                  pl.BlockSpec((MIXER_BLOCKS, 2, 1, 8 * BLOCK), lambda i: (rev(i), 0, 0, 0)),
                  _const_spec((1, D_SGU)), _const_spec((1, D_SGU)),
                  _const_spec((SGU_GROUPS, BLOCK, BLOCK)), _const_spec((SGU_GROUPS, BLOCK, BLOCK)),
                  _const_spec((BLOCK, SGU_GROUPS))],
        out_specs=(pl.BlockSpec((rows, D_IN), lambda i: (rev(i), 0)), _const_spec((1, 128)),
                   _const_spec((SGU_GROUPS, BLOCK, BLOCK)), _const_spec((SGU_GROUPS, BLOCK)),
                   _const_spec((1, D_SGU)), _const_spec((1, D_SGU))),
        out_shape=(jax.ShapeDtypeStruct((t, D_IN), BF16), jax.ShapeDtypeStruct((1, 128), F32),
                   jax.ShapeDtypeStruct((SGU_GROUPS, BLOCK, BLOCK), F32), jax.ShapeDtypeStruct((SGU_GROUPS, BLOCK), F32),
                   jax.ShapeDtypeStruct((1, D_SGU), F32), jax.ShapeDtypeStruct((1, D_SGU), F32)),
        scratch_shapes=[pltpu.VMEM((BLOCK, 2 * D_KV), F32), pltpu.VMEM((2, 1, 8 * BLOCK), F32),
                        pltpu.VMEM((BLOCK, 128), F32)],
        compiler_params=_params(dimension_semantics=("arbitrary",)),
    )(z, z, da, probs, sink_probs, ln_g, ln_b, sgu_w, sgu_wt, sgu_bt)


def _out_proj_head(a, w_out_full, x, target, mod, final_g, tm=256):
    t, d = x.shape

    def body(a_ref, w_ref, x_ref, tg_ref, gate_ref, fg_ref, dx2_ref, dy_ref, loss_ref, dfg_ref, dgate_ref):
        @pl.when(pl.program_id(0) == 0)
        def _():
            loss_ref[...] = jnp.zeros_like(loss_ref)
            dfg_ref[...] = jnp.zeros_like(dfg_ref)
            dgate_ref[...] = jnp.zeros_like(dgate_ref)

        yv, gate, fg = _dot(a_ref[...], w_ref[...]), gate_ref[...], fg_ref[...]
        x2 = x_ref[...] + gate * yv
        r2 = lax.rsqrt(jnp.mean(x2 * x2, axis=-1, keepdims=True) + EPS)
        nrm = x2 * r2
        err = nrm * fg - tg_ref[...]
        loss_ref[...] += 0.5 * jnp.sum(jnp.mean(err * err, axis=-1, keepdims=True), axis=0, keepdims=True)
        fg_d = fg * (1.0 / d)
        err_nrm = err * nrm
        dfg_ref[...] += jnp.sum(err_nrm, axis=0, keepdims=True) * (1.0 / d)
        d_nrm = err * fg_d
        dx2 = r2 * (d_nrm - nrm * jnp.mean(err_nrm * fg_d, axis=-1, keepdims=True))
        dx2_ref[...] = dx2
        dgate_ref[...] += jnp.sum(dx2 * yv, axis=0, keepdims=True)
        dy_ref[...] = (dx2 * gate).astype(BF16)

    blk = pl.BlockSpec((tm, d), lambda i: (i, 0))
    row = _const_spec((1, d))
    whole = pl.BlockSpec(w_out_full.shape, lambda i: (0, 0), pipeline_mode=pl.Buffered(1))
    return pl.pallas_call(
        body, name="out_proj_head", grid=(t // tm,),
        in_specs=[pl.BlockSpec((tm, a.shape[1]), lambda i: (i, 0)), whole, blk, blk, _mod_spec(MOD_GATE, d), row],
        out_specs=(blk, blk, _const_spec((1, 128)), row, row),
        out_shape=(jax.ShapeDtypeStruct((t, d), F32), jax.ShapeDtypeStruct((t, d), BF16),
                   jax.ShapeDtypeStruct((1, 128), F32), jax.ShapeDtypeStruct((1, d), F32),
                   jax.ShapeDtypeStruct((1, d), F32)),
        compiler_params=_params(dimension_semantics=("arbitrary",)),
    )(a, w_out_full, x, target, mod, final_g)


def _z_proj_bwd_norm(dz, w_in_t, x, dx2, norm_g, mod, dep, tm=256):
    t, d = x.shape

    def body(dz_ref, w_ref, x_ref, dx2_ref, g_ref, sc_ref, dep_ref, gx_ref, dshift_ref, dscale_ref, dg_ref):
        @pl.when(pl.program_id(0) == 0)
        def _():
            dshift_ref[...] = jnp.zeros_like(dshift_ref)
            dscale_ref[...] = jnp.zeros_like(dscale_ref)
            dg_ref[...] = jnp.zeros_like(dg_ref)

        dh, xv, g = _dot(dz_ref[...], w_ref[...]), x_ref[...], g_ref[...]
        one_plus = 1.0 + sc_ref[...]
        r = lax.rsqrt(jnp.mean(xv * xv, axis=-1, keepdims=True) + EPS)
        xn = xv * r
        gain = one_plus * g
        dh_xn = dh * xn
        dh_xn_sum = jnp.sum(dh_xn, axis=0, keepdims=True)
        dshift_ref[...] += jnp.sum(dh, axis=0, keepdims=True)
        dscale_ref[...] += dh_xn_sum * g
        dg_ref[...] += dh_xn_sum * one_plus
        d_xn = dh * gain
        gx_ref[...] = dx2_ref[...] + r * (d_xn - xn * jnp.mean(dh_xn * gain, axis=-1, keepdims=True))

    blk = pl.BlockSpec((tm, d), lambda i: (i, 0))
    row = _const_spec((1, d))
    whole = pl.BlockSpec(w_in_t.shape, lambda i: (0, 0), pipeline_mode=pl.Buffered(1))
    return pl.pallas_call(
        body, name="z_proj_bwd_norm", grid=(t // tm,),
        in_specs=[pl.BlockSpec((tm, dz.shape[1]), lambda i: (i, 0)), whole, blk, blk, row, _mod_spec(MOD_SCALE, d),
                  _const_spec((8, 128))],
        out_specs=(blk, row, row, row),
        out_shape=(jax.ShapeDtypeStruct((t, d), F32),) + (jax.ShapeDtypeStruct((1, d), F32),) * 3,
        compiler_params=_params(dimension_semantics=("arbitrary",)),
    )(dz, w_in_t, x, dx2, norm_g, mod, dep)


def _adamw(w, g, m, v):
    m = ADAM_B1 * m + (1.0 - ADAM_B1) * g
    v = ADAM_B2 * v + (1.0 - ADAM_B2) * (g * g)
    m_hat = m / (1.0 - ADAM_B1 ** ADAM_STEP)
    v_hat = v / (1.0 - ADAM_B2 ** ADAM_STEP)
    delta = -ADAM_LR * (m_hat / (jnp.sqrt(v_hat) + ADAM_EPS) + ADAM_WD * w)
    return delta, m, v


def _relay_sum(device, blocks, land_pair, land_first, tr):
    _, r, c = blocks.shape

    def body(device_ref, a_ref, b_ref, c_ref, o_ref):
        o_ref[...] = (a_ref[...].astype(F32) + b_ref[...].astype(F32) + c_ref[...].astype(F32)).astype(BF16)

    second = pl.BlockSpec((None, tr, c), lambda i, device_ref: (1, i, 0))
    return pl.pallas_call(
        body, name="w_in_grad_relay_sum",
        grid_spec=pltpu.PrefetchScalarGridSpec(
            num_scalar_prefetch=1, grid=(r // tr,),
            in_specs=[pl.BlockSpec((None, tr, c), lambda i, device_ref: (device_ref[0], i, 0)), second, second],
            out_specs=pl.BlockSpec((tr, c), lambda i, device_ref: (i, 0))),
        out_shape=jax.ShapeDtypeStruct((r, c), BF16),
        compiler_params=_params(dimension_semantics=("arbitrary",)),
    )(device, blocks, land_pair, land_first)


def _adam_from_chips(chip, pair, landed, w, m, v, name, tc):
    _, r, c = pair.shape
    n = len(landed)

    def body(chip_ref, own_ref, *refs):
        w_ref, m_ref, v_ref, g_ref, d_ref, nm_ref, nv_ref = refs[n:]
        g = own_ref[...].astype(F32)
        for k in range(n):
            g = g + refs[k][...].astype(F32)
        g_ref[...] = g
        d_ref[...], nm_ref[...], nv_ref[...] = _adamw(w_ref[...], g, m_ref[...], v_ref[...])

    def landed_spec(index):
        return pl.BlockSpec((None, r, tc), lambda i, chip_ref: (index, 0, i))

    blk = pl.BlockSpec((r, tc), lambda i, chip_ref: (0, i))
    return pl.pallas_call(
        body, name=name,
        grid_spec=pltpu.PrefetchScalarGridSpec(
            num_scalar_prefetch=1, grid=(c // tc,),
            in_specs=[pl.BlockSpec((None, r, tc), lambda i, chip_ref: (chip_ref[0], 0, i))]
            + [landed_spec(index) for _, index in landed] + [blk, blk, blk],
            out_specs=(blk,) * 4),
        out_shape=(jax.ShapeDtypeStruct((r, c), F32),) * 4,
        compiler_params=_params(dimension_semantics=("arbitrary",)),
    )(chip, pair, *[array for array, _ in landed], w, m, v)


def _adam_w_ada(device, act, dmod_all, w, m, v, tr=512):
    r, c = w.shape

    def body(device_ref, a_ref, dm_ref, w_ref, m_ref, v_ref, g_ref, d_ref, nm_ref, nv_ref):
        g = lax.dot_general(a_ref[...].astype(BF16), dm_ref[...].astype(BF16), (((0,), (0,)), ((), ())),
                            preferred_element_type=F32)
        g_ref[...] = g
        d_ref[...], nm_ref[...], nv_ref[...] = _adamw(w_ref[...], g, m_ref[...], v_ref[...])

    blk = pl.BlockSpec((tr, c), lambda i, device_ref: (i, 0))
    return pl.pallas_call(
        body, name="adam_w_ada",
        grid_spec=pltpu.PrefetchScalarGridSpec(
            num_scalar_prefetch=1, grid=(r // tr,),
            in_specs=[pl.BlockSpec((N_DEV, tr), lambda i, device_ref: (0, i)),
                      pl.BlockSpec((N_DEV, c), lambda i, device_ref: (0, device_ref[0])), blk, blk, blk],
            out_specs=(blk,) * 4),
        out_shape=(jax.ShapeDtypeStruct((r, c), F32),) * 4,
        compiler_params=_params(dimension_semantics=("arbitrary",)),
    )(device, act, dmod_all, w, m, v)


def _pack_small(d_shift, d_scale, d_gate, d_norm_g, d_final_g, d_ln_g, d_ln_b, loss, d_sinks, d_sgu_b):
    def body(shift_ref, scale_ref, gate_ref, ng_ref, fg_ref, lng_ref, lnb_ref, loss_ref, sink_ref, b_ref, o_ref):
        o_ref[...] = jnp.zeros_like(o_ref)
        o_ref[ROW_SHIFT:ROW_SHIFT + 1, :] = shift_ref[...]
        o_ref[ROW_SCALE:ROW_SCALE + 1, :] = scale_ref[...]
        o_ref[ROW_GATE:ROW_GATE + 1, :] = gate_ref[...]
        o_ref[ROW_NORM_G:ROW_NORM_G + 1, :] = ng_ref[...]
        o_ref[ROW_FINAL_G:ROW_FINAL_G + 1, :] = fg_ref[...]
        o_ref[ROW_LN:ROW_LN + 1, 0:D_SGU] = lng_ref[...]
        o_ref[ROW_LN:ROW_LN + 1, D_SGU:2 * D_SGU] = lnb_ref[...]
        o_ref[ROW_MISC:ROW_MISC + 1, 0:128] = loss_ref[...]
        o_ref[ROW_MISC:ROW_MISC + 1, 128:256] = sink_ref[...]
        o_ref[ROW_SGU_B:ROW_SGU_B + SGU_GROUPS, 0:BLOCK] = b_ref[...]

    return pl.pallas_call(
        body, name="pack_small", out_shape=jax.ShapeDtypeStruct((SMALL_ROWS, D_MODEL), F32),
        compiler_params=_params(),
    )(d_shift, d_scale, d_gate, d_norm_g, d_final_g, d_ln_g, d_ln_b, loss, d_sinks, d_sgu_b)


_SMALL_NAMES = ("norm_g", "b_ada", "attn_sinks", "sgu_ln_g", "sgu_ln_b", "sgu_w", "sgu_b", "final_g")


def _adam_small(partials, d_sgu_w_all, weights, moments_m, moments_v):
    names = _SMALL_NAMES
    k = len(names)

    def body(*refs):
        p_ref, sw_ref = refs[0], refs[1]
        w_refs, m_refs, v_refs = refs[2:2 + k], refs[2 + k:2 + 2 * k], refs[2 + 2 * k:2 + 3 * k]
        loss_ref, dmod_ref = refs[2 + 3 * k], refs[3 + 3 * k]
        out_refs = refs[4 + 3 * k:4 + 7 * k]
        sum_ref = refs[4 + 7 * k]
        total = p_ref[0]
        for j in range(1, N_DEV):
            total = total + p_ref[j]
        sum_ref[...] = total
        for j in range(N_DEV):
            for part, row in enumerate((ROW_SHIFT, ROW_SCALE, ROW_GATE)):
                dmod_ref[j:j + 1, part * D_MODEL:(part + 1) * D_MODEL] = p_ref[j, row:row + 1, :]
        loss_ref[...] = sum_ref[ROW_MISC:ROW_MISC + 1, 0:1]
        d_sgu_w = sw_ref[0]
        for j in range(1, N_DEV):
            d_sgu_w = d_sgu_w + sw_ref[j]
        grads = {
            "norm_g": sum_ref[ROW_NORM_G:ROW_NORM_G + 1, :],
            "b_ada": jnp.concatenate([sum_ref[r:r + 1, :] for r in (ROW_SHIFT, ROW_SCALE, ROW_GATE)], axis=1),
            "attn_sinks": sum_ref[ROW_MISC:ROW_MISC + 1, 128:128 + N_Q_HEADS],
            "sgu_ln_g": sum_ref[ROW_LN:ROW_LN + 1, 0:D_SGU],
            "sgu_ln_b": sum_ref[ROW_LN:ROW_LN + 1, D_SGU:2 * D_SGU],
            "sgu_w": d_sgu_w[None],
            "sgu_b": sum_ref[ROW_SGU_B:ROW_SGU_B + SGU_GROUPS, 0:BLOCK][None],
            "final_g": sum_ref[ROW_FINAL_G:ROW_FINAL_G + 1, :],
        }
        for i, name in enumerate(names):
            g = grads[name]
            delta, m, v = _adamw(w_refs[i][...], g, m_refs[i][...], v_refs[i][...])
            out_refs[4 * i][...] = g
            out_refs[4 * i + 1][...] = delta
            out_refs[4 * i + 2][...] = m
            out_refs[4 * i + 3][...] = v

    shapes = [jax.ShapeDtypeStruct((1, 1), F32), jax.ShapeDtypeStruct((N_DEV, 3 * D_MODEL), F32)]
    for name in names:
        shapes += [jax.ShapeDtypeStruct(weights[name].shape, F32)] * 4
    outs = pl.pallas_call(
        body, name="adam_small", out_shape=tuple(shapes),
        scratch_shapes=[pltpu.VMEM((SMALL_ROWS, D_MODEL), F32)],
        compiler_params=_params(),
    )(partials, d_sgu_w_all, *[weights[n] for n in names], *[moments_m[n] for n in names],
      *[moments_v[n] for n in names])
    return outs[0], outs[1], {name: outs[2 + 4 * i:6 + 4 * i] for i, name in enumerate(names)}


def kernel(x, c, norm_g, w_ada, b_ada, w_in, attn_sinks, sgu_ln_g, sgu_ln_b, sgu_w, sgu_b, w_out, final_g, loss_target, m_norm_g, m_w_ada, m_b_ada, m_w_in, m_attn_sinks, m_sgu_ln_g, m_sgu_ln_b, m_sgu_w, m_sgu_b, m_w_out, m_final_g, v_norm_g, v_w_ada, v_b_ada, v_w_in, v_attn_sinks, v_sgu_ln_g, v_sgu_ln_b, v_sgu_w, v_sgu_b, v_w_out, v_final_g):
    xi, yi, ci = _place()
    me = 4 * xi + 2 * yi + ci
    x2d, target = x[0], loss_target[0]
    t = x2d.shape[0]

    core = ci.astype(jnp.int32).reshape(1)
    chip = (2 * xi + yi).astype(jnp.int32).reshape(1)

    first = _own_block_copies(_first_targets)
    first_flight = _start_copies([_with_own_slot(w_in[0].T.astype(BF16), me)], first, 2, core, "gather_w_in_start")

    c_all = _all_gather_small(c.reshape(8, 256) + first_flight[3][0, 0], "gather_c").reshape(N_DEV, D_MODEL)
    device = me.astype(jnp.int32).reshape(1)
    c_act, mod_part = _modulation(device, c_all, w_ada[0], b_ada)
    mod_all = _all_gather_small(mod_part, "gather_mod")

    across = _wait_then_start(first_flight, lambda *a: first(*a)[1:], _second_axis_stage_copies, 3, mod_all,
                              "gather_w_in_second_axis_stage")
    mod = lax.dynamic_index_in_dim(mod_all, me, axis=1, keepdims=False).reshape(1, 3 * D_MODEL)

    w_in_pair = _wait_copies((first_flight[0], first_flight[1], across[2], None), lambda *a: first(*a)[:1], mod,
                             "gather_w_in_sibling_wait")
    h, z_own = _norm_z_proj_own(x2d, norm_g, mod, w_in_pair[0].reshape(D_IN, D_MODEL), chip)
    w_out_early = _own_block_copies(lambda x, y, c: [(x, y, 1 - c), (*_second_axis_chip(x, y, c), c)])
    w_out_late = _own_block_copies(lambda x, y, c: [(*_first_axis_chip(x, y, c), c), (1 - x, 1 - y, c)])
    forward = _wait_then_start(
        (across[0], across[1], w_in_pair, None), lambda *a: _second_axis_stage_copies(*a)[:1],
        lambda refs, s, r: _second_axis_forward_copies(refs[:1], s, r) + _group(w_out_early, 1, 1, 1)(refs, s, r),
        3, z_own, "gather_w_in_second_axis_forward", more_bufs=[_with_own_slot(w_out[0].astype(BF16), me)])
    w_in_most = _wait_copies((across[0], across[1], forward[2][:1], None),
                             lambda *a: _second_axis_stage_copies(*a)[1:2], z_own, "gather_w_in_first_forward_wait")
    w_in_most = _wait_copies((forward[0], forward[1], w_in_most, None), _second_axis_forward_copies, z_own,
                             "gather_w_in_second_forward_wait")
    z_early = _z_proj(h, w_in_most[0].reshape(D_IN, D_MODEL), chip, 1, _Z_EARLY_TILES - 1, z_own, "z_proj_early")
    last = _wait_then_start(
        (across[0], across[1], [w_in_most[0], forward[2][1]], None), lambda *a: _second_axis_stage_copies(*a)[2:],
        lambda refs, s, r: _diagonal_forward_copies(refs[:1], s, r) + _group(w_out_late, 1, 1, 1)(refs, s, r),
        3, z_early, "gather_w_in_last_stage")
    w_in_all = _wait_copies((last[0], last[1], last[2][:1], None), _diagonal_forward_copies, z_early,
                            "gather_w_in_last_wait")[0]
    w_in_t = w_in_all.reshape(D_IN, D_MODEL)
    z = _z_proj(h, w_in_t, chip, _Z_EARLY_TILES, 7 - _Z_EARLY_TILES, z_early, "z_proj_late")
    w_out_half = _wait_copies((forward[0], forward[1], last[2][1:], None), _group(w_out_early, 0, 1, 1), z,
                              "gather_w_out_early_wait")
    w_out_flight = _wait_then_start((last[0], last[1], w_out_half, None), _group(w_out_late, 0, 1, 1),
                                    _forward_copies, 3, z, "gather_w_out_forward_stage")
    sink_rows = jnp.repeat(attn_sinks.reshape(N_Q_HEADS), BLOCK).reshape(2, 1, 8 * BLOCK)
    sgu_bt = sgu_b[0].T
    a, probs, sink_probs = _mixer_fwd(z, sink_rows, sgu_ln_g, sgu_ln_b, sgu_w[0], sgu_bt, w_out_flight[3])
    w_out_all = _wait_copies(w_out_flight, _forward_copies, a, "gather_w_out_forward_wait")[0]
    w_out_full = w_out_all.reshape(D_MODEL, D_MODEL)
    final_g_row = final_g.reshape(1, D_MODEL)
    dx2, dy, loss_part, d_final_g, d_gate = _out_proj_head(a, w_out_full, x2d, target, mod, final_g_row)

    da = _matmul(dy, w_out_full, "nt", F32, min(t, 1024), 1024, "out_proj_bwd")
    dw_out = _matmul(a, dy, "tn", BF16, 1024, 1024, "w_out_grad").reshape(4, 2, W_OUT_SHARD, D_MODEL)
    pair_out = _pair_reduce(dw_out, _every_chip, "w_out_grad_pair_reduce", W_OUT_SHARD // 2)
    dz, d_sinks, d_sgu_w, d_sgu_b, d_ln_g, d_ln_b = _mixer_bwd(
        z, da, probs, sink_probs, sgu_ln_g, sgu_ln_b, sgu_w[0], jnp.swapaxes(sgu_w[0], 1, 2), sgu_bt)
    sgu_w_to_all = _group(_own_block_copies(_all_others), 2, 1, 3)
    both = _start_copies(
        [pair_out, lax.empty((3, W_OUT_SHARD, D_MODEL), BF16), _with_own_slot(d_sgu_w, me)],
        lambda refs, s, r: _chip_copies(refs[:2], s, r) + sgu_w_to_all(refs, s, r), 3 + N_DEV - 1, core,
        "w_out_grad_chip_and_sgu_w_gather_start")
    out_flight, sgu_w_flight = (both[0], both[1], both[2][:2], None), (both[0], both[1], both[2][2:], None)
    dw_in_t = _matmul(dz, h, "tn", BF16, 768, D_MODEL, "w_in_grad", dep=both[3])
    pair_in = _pair_reduce(dw_in_t.reshape(4, 2, W_IN_SHARD, D_MODEL), _first_hop_chips, "w_in_grad_pair_reduce",
                           W_IN_SHARD // 3)
    first_hop = lambda refs, s, r: _first_hop_copies(refs[:2], s, r) + _group(_late_pair_copies, 2, 2, 2)(refs, s, r)
    hop1 = _start_copies(
        [pair_in, lax.empty((2, W_IN_SHARD, D_MODEL), BF16), dw_in_t.reshape(N_DEV, W_IN_SHARD, D_MODEL),
         lax.empty((2, W_IN_SHARD, D_MODEL), BF16)], first_hop, 4, core, "w_in_grad_first_hop_start")
    grad_x, d_shift, d_scale, d_norm_g = _z_proj_bwd_norm(dz, w_in_t, x2d, dx2, norm_g, mod, hop1[3])

    partial = _pack_small(d_shift, d_scale, d_gate, d_norm_g, d_final_g, d_ln_g, d_ln_b, loss_part, d_sinks, d_sgu_b)
    _, land_first, dw_in_t, land_pair = _wait_copies(hop1, first_hop, partial, "w_in_grad_first_hop_wait")
    second_device = (4 * ((xi + ci) % 2) + 2 * ((yi + 1 - ci) % 2) + ci).astype(jnp.int32).reshape(1)
    relay = _relay_sum(second_device, dw_in_t, land_pair, land_first, W_IN_SHARD // 3)
    small_to_all, second_hop = _own_block_copies(_all_others), _group(_second_hop_copies, 0, 2, N_DEV - 1)
    tail = _start_copies(
        [_with_own_slot(partial, me), relay, lax.empty((1, W_IN_SHARD, D_MODEL), BF16)],
        lambda refs, s, r: small_to_all(refs[:1], s, r) + second_hop(refs[1:], s, r), N_DEV, core,
        "small_grad_gather_and_w_in_grad_second_hop_start")
    small_flight, hop2 = (tail[0], tail[1], tail[2][:1], None), (tail[0], tail[1], tail[2][1:], tail[3])
    pair_out, land_out = _wait_copies(out_flight, _chip_copies, hop2[3], "w_out_grad_chip_wait")
    big = {"w_out": _adam_from_chips(chip, pair_out, [(land_out, k) for k in range(3)], w_out[0], m_w_out[0],
                                     v_w_out[0], "adam_w_out", 1024)}
    partial_all = _wait_copies(small_flight, small_to_all, big["w_out"][0],
                               "small_grad_gather_wait")[0]
    d_sgu_w_all = _wait_copies(sgu_w_flight, _group(_own_block_copies(_all_others), 0, 1, 3), partial_all,
                               "sgu_w_grad_gather_wait")[0]
    weights = {"norm_g": norm_g, "b_ada": b_ada, "attn_sinks": attn_sinks, "sgu_ln_g": sgu_ln_g,
               "sgu_ln_b": sgu_ln_b, "sgu_w": sgu_w, "sgu_b": sgu_b, "final_g": final_g_row}
    moments_m = {"norm_g": m_norm_g, "b_ada": m_b_ada, "attn_sinks": m_attn_sinks, "sgu_ln_g": m_sgu_ln_g,
                 "sgu_ln_b": m_sgu_ln_b, "sgu_w": m_sgu_w, "sgu_b": m_sgu_b,
                 "final_g": m_final_g.reshape(1, D_MODEL)}
    moments_v = {"norm_g": v_norm_g, "b_ada": v_b_ada, "attn_sinks": v_attn_sinks, "sgu_ln_g": v_sgu_ln_g,
                 "sgu_ln_b": v_sgu_ln_b, "sgu_w": v_sgu_w, "sgu_b": v_sgu_b,
                 "final_g": v_final_g.reshape(1, D_MODEL)}
    loss, dmod_all, small = _adam_small(partial_all, d_sgu_w_all, weights, moments_m, moments_v)
    small["final_g"] = tuple(o.reshape(D_MODEL) for o in small["final_g"])

    big["w_ada"] = _adam_w_ada(device, c_act, dmod_all, w_ada[0], m_w_ada[0], v_w_ada[0])
    _, land_second = _wait_copies(hop2, second_hop, big["w_ada"][0], "w_in_grad_second_hop_wait")
    big["w_in"] = tuple(o.T for o in _adam_from_chips(
        device, dw_in_t, [(land_pair, 0), (land_first, 0), (land_second, 0)], w_in[0].T, m_w_in[0].T, v_w_in[0].T,
        "adam_w_in", 512))
    order = ["norm_g", "w_ada", "b_ada", "w_in", "attn_sinks", "sgu_ln_g", "sgu_ln_b", "sgu_w", "sgu_b", "w_out",
             "final_g"]
    outs = [loss.reshape(()), grad_x[None]]
    for k in range(4):
        for name in order:
            outs.append(big[name][k][None] if name in big else small[name][k])
    return tuple(outs)
```

```python
import jax
import jax.numpy as jnp
from jax import lax
from jax.experimental import pallas as pl
from jax.experimental.pallas import tpu as pltpu

F32 = jnp.float32
BF16 = jnp.bfloat16
MESH = pl.DeviceIdType.MESH

N_DEV = 8
D_MODEL = 2048
HEAD_DIM = 64
D_ATTN = 1024
N_Q_HEADS = 16
D_KV = 128
BLOCK = 128
D_SGU = 1024
SGU_GROUPS = 8
D_IN = 5376
W_IN_SHARD = D_IN // N_DEV
W_OUT_SHARD = D_MODEL // N_DEV
W_ADA_SHARD = 3 * D_MODEL // N_DEV
EPS = 1e-6
ATTN_SCALE = 0.125

ADAM_LR = 0.001
ADAM_B1 = 0.9
ADAM_B2 = 0.999
ADAM_EPS = 1e-08
ADAM_WD = 0.01
ADAM_STEP = 10

SEG_Q, SEG_KV, SEG_GA, SEG_U, SEG_VS, SEG_GS = 0, 1024, 1280, 2304, 3328, 4352

VMEM_LIMIT = 56 * 1024 * 1024

ROW_SHIFT, ROW_SCALE, ROW_GATE, ROW_NORM_G, ROW_FINAL_G, ROW_LN, ROW_MISC, ROW_SGU_B = 0, 1, 2, 3, 4, 5, 6, 8
SMALL_ROWS = 16


def _params(**kw):
    return pltpu.CompilerParams(vmem_limit_bytes=VMEM_LIMIT, **kw)


def _sigmoid(x):
    return 0.5 * (jnp.tanh(0.5 * x) + 1.0)


def _place():
    return lax.axis_index("x"), lax.axis_index("y"), lax.axis_index("c")


def _other_chips(x, y, c):
    return [2 * (1 - x) + y, 2 * x + (1 - y), 2 * (1 - x) + (1 - y)]


def _first_hop_chips(x, y, c):
    first = _first_axis_chip(x, y, c)
    return [2 * first[0] + first[1], 2 * (1 - x) + (1 - y)]


def _pair_reduce(blocks, chips, name, row_chunk):
    _, _, r, cols = blocks.shape
    n = len(chips(0, 0, 0))
    assert r % row_chunk == 0

    def body(in_ref, out_ref, land, own, summed, send_sems, recv_sems, own_sems, out_sems):
        x, y, c = _place()
        sends, loads, stores = [], [], []
        for m in range(n):
            cp = pltpu.make_async_remote_copy(
                src_ref=in_ref.at[chips(x, y, 1 - c)[m], 1 - c], dst_ref=land.at[m], send_sem=send_sems.at[m],
                recv_sem=recv_sems.at[m], device_id=(x, y, 1 - c), device_id_type=MESH)
            cp.start()
            sends.append(cp)
            ld = pltpu.make_async_copy(in_ref.at[chips(x, y, c)[m], c], own.at[m], own_sems.at[m])
            ld.start()
            loads.append(ld)
        for m in range(n):
            sends[m].wait_recv()
            loads[m].wait()
            for k in range(r // row_chunk):
                rows = slice(k * row_chunk, (k + 1) * row_chunk)
                summed[m, rows, :] = (own[m, rows, :].astype(F32) + land[m, rows, :].astype(F32)).astype(BF16)
            st = pltpu.make_async_copy(summed.at[m], out_ref.at[m], out_sems.at[m])
            st.start()
            stores.append(st)
        for m in range(n):
            sends[m].wait_send()
            stores[m].wait()

    spec = pl.BlockSpec(memory_space=pl.ANY)
    return pl.pallas_call(
        body, name=name, out_shape=jax.ShapeDtypeStruct((n, r, cols), BF16),
        in_specs=[spec], out_specs=spec,
        scratch_shapes=[pltpu.VMEM((n, r, cols), BF16), pltpu.VMEM((n, r, cols), BF16), pltpu.VMEM((n, r, cols), BF16),
                        pltpu.SemaphoreType.DMA((n,)), pltpu.SemaphoreType.DMA((n,)), pltpu.SemaphoreType.DMA((n,)),
                        pltpu.SemaphoreType.DMA((n,))],
        compiler_params=_params(),
    )(blocks)


_HBM = pl.BlockSpec(memory_space=pltpu.HBM)
_SEM = pl.BlockSpec(memory_space=pltpu.SEMAPHORE)
_EFFECT = pltpu.SideEffectType.DATAFLOW_SIDE_EFFECTING


def _start_copies(bufs, copies, n_copies, after, name):
    nb = len(bufs)

    def body(*refs):
        for cp in copies(refs[:nb], refs[nb + 1], refs[nb + 2]):
            cp.start()
        refs[-1][...] = jnp.zeros_like(refs[-1])

    out = pl.pallas_call(
        body, name=name,
        out_shape=(pltpu.SemaphoreType.DMA((n_copies,)), pltpu.SemaphoreType.DMA((n_copies,)),
                   *[pltpu.HBM(b.shape, b.dtype) for b in bufs], jax.ShapeDtypeStruct((8, 128), F32)),
        in_specs=(_HBM,) * nb + (pl.BlockSpec(memory_space=pl.ANY),),
        out_specs=(_SEM, _SEM) + (_HBM,) * nb + (pl.BlockSpec(memory_space=pltpu.VMEM),),
        input_output_aliases={i: 2 + i for i in range(nb)},
        compiler_params=pltpu.CompilerParams(has_side_effects=_EFFECT),
    )(*[pltpu.with_memory_space_constraint(b, pltpu.HBM) for b in bufs], after)
    return out[0], out[1], list(out[2:2 + nb]), out[-1]


def _wait_copies(flight, copies, after, name):
    send_sems, recv_sems, bufs, _ = flight
    nb = len(bufs)

    def body(*refs):
        for cp in copies(refs[:nb], refs[nb], refs[nb + 1]):
            cp.wait_send()
            cp.wait_recv()

    return pl.pallas_call(
        body, name=name,
        out_shape=tuple(pltpu.HBM(b.shape, b.dtype) for b in bufs),
        in_specs=(_HBM,) * nb + (_SEM, _SEM, pl.BlockSpec(memory_space=pl.ANY)), out_specs=(_HBM,) * nb,
        input_output_aliases={i: i for i in range(nb)},
        compiler_params=pltpu.CompilerParams(has_side_effects=_EFFECT),
    )(*bufs, send_sems, recv_sems, after)


class _From:
    def __init__(self, sems, offset):
        self.sems, self.offset = sems, offset

    @property
    def at(self):
        return self

    def __getitem__(self, k):
        return self.sems.at[k + self.offset]


def _group(copies, first_buf, n_bufs, offset):
    def grouped(refs, send_sems, recv_sems):
        return copies(refs[first_buf:first_buf + n_bufs], _From(send_sems, offset), _From(recv_sems, offset))
    return grouped


def _wait_then_start(flight, waited, started, n_started, after, name, more_bufs=()):
    old_send, old_recv, bufs, _ = flight
    bufs = list(bufs) + [pltpu.with_memory_space_constraint(b, pltpu.HBM) for b in more_bufs]
    nb = len(bufs)

    def body(*refs):
        for cp in waited(refs[:nb], refs[nb], refs[nb + 1]):
            cp.wait_send()
            cp.wait_recv()
        for cp in started(refs[:nb], refs[nb + 3], refs[nb + 4]):
            cp.start()
        refs[-1][...] = jnp.zeros_like(refs[-1])

    out = pl.pallas_call(
        body, name=name,
        out_shape=(pltpu.SemaphoreType.DMA((n_started,)), pltpu.SemaphoreType.DMA((n_started,)),
                   *[pltpu.HBM(b.shape, b.dtype) for b in bufs], jax.ShapeDtypeStruct((8, 128), F32)),
        in_specs=(_HBM,) * nb + (_SEM, _SEM, pl.BlockSpec(memory_space=pl.ANY)),
        out_specs=(_SEM, _SEM) + (_HBM,) * nb + (pl.BlockSpec(memory_space=pltpu.VMEM),),
        input_output_aliases={i: 2 + i for i in range(nb)},
        compiler_params=pltpu.CompilerParams(has_side_effects=_EFFECT),
    )(*bufs, old_send, old_recv, after)
    return out[0], out[1], list(out[2:2 + nb]), out[-1]


def _late_pair_copies(refs, send_sems, recv_sems):
    blocks_ref, land_ref = refs
    x, y, c = _place()
    first = _first_axis_chip(x, y, c)
    devices = [4 * x + 2 * y + 1 - c, 4 * first[0] + 2 * first[1] + 1 - c]
    return [pltpu.make_async_remote_copy(
        src_ref=blocks_ref.at[devices[k]], dst_ref=land_ref.at[k], send_sem=send_sems.at[k], recv_sem=recv_sems.at[k],
        device_id=(x, y, 1 - c), device_id_type=MESH) for k in range(2)]


def _chip_copies(refs, send_sems, recv_sems):
    pair_ref, land_ref = refs
    x, y, c = _place()
    chips = [(1 - x, y), (x, 1 - y), (1 - x, 1 - y)]
    return [pltpu.make_async_remote_copy(
        src_ref=pair_ref.at[k], dst_ref=land_ref.at[k], send_sem=send_sems.at[k], recv_sem=recv_sems.at[k],
        device_id=(*chip, c), device_id_type=MESH) for k, chip in enumerate(chips)]


def _own_chip_pair_copy(refs, send_sems, recv_sems):
    blocks_ref, land_ref = refs
    x, y, c = _place()
    return [pltpu.make_async_remote_copy(
        src_ref=blocks_ref.at[4 * x + 2 * y + 1 - c], dst_ref=land_ref.at[0], send_sem=send_sems.at[0],
        recv_sem=recv_sems.at[0], device_id=(x, y, 1 - c), device_id_type=MESH)]


def _first_hop_copies(refs, send_sems, recv_sems):
    pair_ref, land_ref = refs
    x, y, c = _place()
    return [pltpu.make_async_remote_copy(
        src_ref=pair_ref.at[k], dst_ref=land_ref.at[k], send_sem=send_sems.at[k], recv_sem=recv_sems.at[k],
        device_id=(*_first_axis_chip(x, y, c), c), device_id_type=MESH) for k in range(2)]


def _second_hop_copies(refs, send_sems, recv_sems):
    relay_ref, land_ref = refs
    x, y, c = _place()
    second = ((x + c) % 2, (y + 1 - c) % 2)
    return [pltpu.make_async_remote_copy(
        src_ref=relay_ref, dst_ref=land_ref.at[0], send_sem=send_sems.at[0], recv_sem=recv_sems.at[0],
        device_id=(*second, c), device_id_type=MESH)]


def _own_block_copies(targets):
    def copies(refs, send_sems, recv_sems):
        x, y, c = _place()
        mine = refs[0].at[4 * x + 2 * y + c]
        return [pltpu.make_async_remote_copy(
            src_ref=mine, dst_ref=mine, send_sem=send_sems.at[k], recv_sem=recv_sems.at[k],
            device_id=to, device_id_type=MESH) for k, to in enumerate(targets(x, y, c))]
    return copies


def _all_others(x, y, c):
    flip = lambda v, f: 1 - v if f else v
    return [(flip(x, r & 4), flip(y, r & 2), flip(c, r & 1)) for r in range(1, N_DEV)]


def _forward_copies(refs, send_sems, recv_sems):
    x, y, c = _place()
    chips = [(1 - x, y), (x, 1 - y), (1 - x, 1 - y)]
    return [pltpu.make_async_remote_copy(
        src_ref=refs[0].at[4 * chip[0] + 2 * chip[1] + c], dst_ref=refs[0].at[4 * chip[0] + 2 * chip[1] + c],
        send_sem=send_sems.at[k], recv_sem=recv_sems.at[k],
        device_id=(x, y, 1 - c), device_id_type=MESH) for k, chip in enumerate(chips)]


def _first_axis_chip(x, y, c):
    return (x + 1 - c) % 2, (y + c) % 2


def _second_axis_chip(x, y, c):
    return (x + c) % 2, (y + 1 - c) % 2


def _first_targets(x, y, c):
    return [(x, y, 1 - c), (*_first_axis_chip(x, y, c), c)]


def _all_gather_small(shard, name):
    def body(in_ref, out_ref, send_sems, recv_sems, local_sem):
        x, y, c = _place()
        me, sibling = 4 * x + 2 * y + c, (x, y, 1 - c)
        first, second = _first_axis_chip(x, y, c), _second_axis_chip(x, y, c)

        def pair(chip):
            return out_ref.at[pl.ds(2 * (2 * chip[0] + chip[1]), 2)]

        def exchange(k, src, dst, to):
            cp = pltpu.make_async_remote_copy(src_ref=src, dst_ref=dst, send_sem=send_sems.at[k],
                                              recv_sem=recv_sems.at[k], device_id=to, device_id_type=MESH)
            cp.start()
            cp.wait()

        own = pltpu.make_async_copy(in_ref, out_ref.at[me], local_sem)
        own.start()
        exchange(0, in_ref, out_ref.at[me], sibling)
        own.wait()
        exchange(1, pair((x, y)), pair((x, y)), (*second, c))
        exchange(2, pair(second), pair(second), sibling)
        exchange(3, pair(first), pair(first), (*second, c))

    spec = pl.BlockSpec(memory_space=pltpu.VMEM)
    return pl.pallas_call(
        body, name=name, out_shape=jax.ShapeDtypeStruct((N_DEV,) + shard.shape, shard.dtype),
        in_specs=[spec], out_specs=spec,
        scratch_shapes=[pltpu.SemaphoreType.DMA((4,)), pltpu.SemaphoreType.DMA((4,)), pltpu.SemaphoreType.DMA],
        compiler_params=_params(),
    )(shard)


def _slot_copies(refs, send_sems, recv_sems, plan):
    copies = []
    for k, ((px, py, pc), to) in enumerate(plan):
        blk = refs[0].at[4 * px + 2 * py + pc]
        copies.append(pltpu.make_async_remote_copy(
            src_ref=blk, dst_ref=blk, send_sem=send_sems.at[k], recv_sem=recv_sems.at[k],
            device_id=to, device_id_type=MESH))
    return copies


def _second_axis_stage_copies(refs, send_sems, recv_sems):
    x, y, c = _place()
    first, second = (*_first_axis_chip(x, y, c), c), (*_second_axis_chip(x, y, c), c)
    return _slot_copies(refs, send_sems, recv_sems, [((x, y, c), second), (first, (x, y, 1 - c)), (first, second)])


def _second_axis_forward_copies(refs, send_sems, recv_sems):
    x, y, c = _place()
    return _slot_copies(refs, send_sems, recv_sems, [((*_second_axis_chip(x, y, c), c), (x, y, 1 - c))])


def _diagonal_forward_copies(refs, send_sems, recv_sems):
    x, y, c = _place()
    blk = refs[0].at[4 * (1 - x) + 2 * (1 - y) + c]
    return [pltpu.make_async_remote_copy(
        src_ref=blk, dst_ref=blk, send_sem=send_sems.at[0], recv_sem=recv_sems.at[0],
        device_id=(x, y, 1 - c), device_id_type=MESH)]


def _with_own_slot(block, me):
    return lax.dynamic_update_index_in_dim(lax.empty((N_DEV,) + block.shape, block.dtype), block, me, 0)


def _matmul(a, b, dims, out_dtype, tm, tn, name, dep=None):
    if dims == "nn":
        (m, k), n = a.shape, b.shape[1]
        a_spec = pl.BlockSpec((tm, k), lambda i, j: (i, 0))
        b_spec = pl.BlockSpec((k, tn), lambda i, j: (0, j))
        contract = ((1,), (0,))
    elif dims == "nt":
        (m, k), n = a.shape, b.shape[0]
        a_spec = pl.BlockSpec((tm, k), lambda i, j: (i, 0))
        b_spec = pl.BlockSpec((tn, k), lambda i, j: (j, 0))
        contract = ((1,), (1,))
    else:
        (k, m), n = a.shape, b.shape[1]
        a_spec = pl.BlockSpec((k, tm), lambda i, j: (0, i))
        b_spec = pl.BlockSpec((k, tn), lambda i, j: (0, j))
        contract = ((0,), (0,))
    assert m % tm == 0 and n % tn == 0 and a.dtype == BF16 and b.dtype == BF16

    def body(a_ref, b_ref, *rest):
        rest[-1][...] = lax.dot_general(a_ref[...], b_ref[...], (contract, ((), ())),
                                        preferred_element_type=F32).astype(out_dtype)

    deps = [] if dep is None else [dep]
    return pl.pallas_call(
        body, name=name, grid=(m // tm, n // tn),
        in_specs=[a_spec, b_spec] + [pl.BlockSpec((8, 128), lambda i, j: (0, 0))] * len(deps),
        out_specs=pl.BlockSpec((tm, tn), lambda i, j: (i, j)),
        out_shape=jax.ShapeDtypeStruct((m, n), out_dtype),
        compiler_params=_params(dimension_semantics=("arbitrary", "arbitrary")),
    )(a, b, *deps)


Z_TILE = 768
_Z_TILE_ORDER = ((0, 1, 2, 3, 4, 5, 6), (2, 0, 1, 6, 3, 4, 5), (4, 0, 5, 6, 1, 2, 3), (6, 2, 3, 4, 0, 1, 5))
_Z_EARLY_TILES = 4


def _z_proj(h, w_in_t, chip, first, count, z_prev, name, tr=1024):
    t = h.shape[0]

    def body(chip_ref, h_ref, w_ref, z_prev_ref, z_ref):
        z_ref[...] = _dot_nt(h_ref[...], w_ref[...])

    def tile(j, chip_ref):
        picked = 0
        for c, order in enumerate(_Z_TILE_ORDER):
            for k in range(count):
                picked = picked + jnp.where((chip_ref[0] == c) & (j == k), order[first + k], 0)
        return picked

    return pl.pallas_call(
        body, name=name,
        grid_spec=pltpu.PrefetchScalarGridSpec(
            num_scalar_prefetch=1, grid=(count, t // tr),
            in_specs=[pl.BlockSpec((tr, D_MODEL), lambda j, i, o: (i, 0)),
                      pl.BlockSpec((Z_TILE, D_MODEL), lambda j, i, o: (tile(j, o), 0)),
                      pl.BlockSpec(memory_space=pl.ANY)],
            out_specs=pl.BlockSpec((tr, Z_TILE), lambda j, i, o: (i, tile(j, o)))),
        out_shape=jax.ShapeDtypeStruct((t, D_IN), F32),
        input_output_aliases={3: 0},
        compiler_params=_params(dimension_semantics=("arbitrary", "arbitrary")),
    )(chip, h, w_in_t, z_prev)


def _modulation(device, c_all, w_ada, b_ada):
    def body(device_ref, c_ref, w_ref, b_ref, act_ref, mod_ref):
        cv = c_ref[...]
        act = cv * _sigmoid(cv)
        act_ref[...] = act
        mod_ref[...] = jnp.dot(act.astype(BF16), w_ref[...].astype(BF16), preferred_element_type=F32) + b_ref[...]

    whole = lambda a: pl.BlockSpec(a.shape, lambda i, device_ref: (0,) * a.ndim)
    return pl.pallas_call(
        body, name="modulation",
        grid_spec=pltpu.PrefetchScalarGridSpec(
            num_scalar_prefetch=1, grid=(1,),
            in_specs=[whole(c_all), whole(w_ada), pl.BlockSpec((1, W_ADA_SHARD), lambda i, device_ref: (0, device_ref[0]))],
            out_specs=(whole(c_all), pl.BlockSpec((N_DEV, W_ADA_SHARD), lambda i, device_ref: (0, 0)))),
        out_shape=(jax.ShapeDtypeStruct(c_all.shape, F32), jax.ShapeDtypeStruct((N_DEV, W_ADA_SHARD), F32)),
        compiler_params=_params(dimension_semantics=("arbitrary",)),
    )(device, c_all, w_ada, b_ada)


MOD_SHIFT, MOD_SCALE, MOD_GATE = 0, 1, 2


def _mod_spec(part, d):
    return pl.BlockSpec((1, d), lambda i: (0, part))


def _norm_z_proj_own(x, norm_g, mod, w_in_t, chip, tm=512):
    t, d = x.shape

    def body(chip_ref, x_ref, g_ref, sc_ref, sh_ref, w_ref, h_ref, z_ref):
        xv = x_ref[...]
        r = lax.rsqrt(jnp.mean(xv * xv, axis=-1, keepdims=True) + EPS)
        h = ((xv * r) * g_ref[...] * (1.0 + sc_ref[...]) + sh_ref[...]).astype(BF16)
        h_ref[...] = h
        z_ref[...] = _dot_nt(h, w_ref[...])

    def own_tile(chip_ref):
        picked = 0
        for c, order in enumerate(_Z_TILE_ORDER):
            picked = picked + jnp.where(chip_ref[0] == c, order[0], 0)
        return picked

    def row(part):
        return pl.BlockSpec((1, d), lambda i, o: (0, part))

    return pl.pallas_call(
        body, name="norm_z_proj_own",
        grid_spec=pltpu.PrefetchScalarGridSpec(
            num_scalar_prefetch=1, grid=(t // tm,),
            in_specs=[pl.BlockSpec((tm, d), lambda i, o: (i, 0)), row(0), row(MOD_SCALE), row(MOD_SHIFT),
                      pl.BlockSpec((Z_TILE, d), lambda i, o: (own_tile(o), 0))],
            out_specs=(pl.BlockSpec((tm, d), lambda i, o: (i, 0)),
                       pl.BlockSpec((tm, Z_TILE), lambda i, o: (i, own_tile(o))))),
        out_shape=(jax.ShapeDtypeStruct((t, d), BF16), jax.ShapeDtypeStruct((t, D_IN), F32)),
        compiler_params=_params(dimension_semantics=("arbitrary",)),
    )(chip, x, norm_g, mod, mod, w_in_t)


def _window_bias(block_index):
    s = lax.broadcasted_iota(jnp.int32, (2 * BLOCK, BLOCK), 0)
    t = lax.broadcasted_iota(jnp.int32, (2 * BLOCK, BLOCK), 1)
    valid = ((s < BLOCK) & (s > t) & (block_index > 0)) | ((s >= BLOCK) & ((s - BLOCK) <= t))
    bias = jnp.where(valid, 0.0, -jnp.inf).astype(F32)
    return jnp.concatenate([bias] * 8, axis=1)


def _heads_t(pair_blocks, g):
    top = lax.broadcasted_iota(jnp.int32, (BLOCK, BLOCK), 0) < HEAD_DIM
    zeros = jnp.zeros((HEAD_DIM, BLOCK), F32)
    tiles = []
    for blk in pair_blocks:
        tp = blk.T
        if g == 0:
            tiles += [jnp.where(top, tp, 0.0), jnp.concatenate([tp[HEAD_DIM:], zeros], axis=0)]
        else:
            tiles += [jnp.concatenate([zeros, tp[:HEAD_DIM]], axis=0), jnp.where(top, 0.0, tp)]
    return jnp.concatenate(tiles, axis=1)


def _pair_block(xt, p, g):
    r0 = HEAD_DIM * g
    even = xt[r0:r0 + HEAD_DIM, (2 * p) * BLOCK:(2 * p + 1) * BLOCK]
    odd = xt[r0:r0 + HEAD_DIM, (2 * p + 1) * BLOCK:(2 * p + 2) * BLOCK]
    return jnp.concatenate([even, odd], axis=0).T


def _softmax_t(scores_t, bias, sink):
    st = scores_t + bias
    m = jnp.maximum(jnp.max(st, axis=0, keepdims=True), sink)
    e = jnp.exp(st - m)
    es = jnp.exp(sink - m)
    inv = 1.0 / (jnp.sum(e, axis=0, keepdims=True) + es)
    return e * inv, es * inv


def _dot(a, b):
    return jnp.dot(a, b, preferred_element_type=F32)


def _dot_nt(a, b):
    return lax.dot_general(a, b, (((1,), (1,)), ((), ())), preferred_element_type=F32)


def _layer_norm_fwd(v):
    mu = jnp.mean(v, axis=-1, keepdims=True)
    xc = v - mu
    rstd = lax.rsqrt(jnp.mean(xc * xc, axis=-1, keepdims=True) + EPS)
    return xc * rstd, rstd


def _tril(transposed=False):
    t = lax.broadcasted_iota(jnp.int32, (BLOCK, BLOCK), 0)
    s = lax.broadcasted_iota(jnp.int32, (BLOCK, BLOCK), 1)
    return s >= t if transposed else t >= s


def _const_spec(shape):
    return pl.BlockSpec(shape, lambda i: (0,) * len(shape))


def _keys_values(z_ref, kvp):
    kvc = z_ref[:, SEG_KV:SEG_KV + 2 * D_KV]
    kk = jnp.concatenate([kvp[:, :D_KV], kvc[:, :D_KV]], axis=0)
    vv = jnp.concatenate([kvp[:, D_KV:], kvc[:, D_KV:]], axis=0)
    return kk, vv


MIXER_BLOCKS = 2


class _Rows:
    def __init__(self, ref, sub):
        self.ref, self.rows = ref, slice(sub * BLOCK, (sub + 1) * BLOCK)

    def __getitem__(self, idx):
        return self.ref[self.rows, idx[1]]

    def __setitem__(self, idx, value):
        self.ref[self.rows, idx[1]] = value


def _kv_before_spec(index):
    return pl.BlockSpec((BLOCK, 2 * D_KV),
                        lambda i: (jnp.maximum(MIXER_BLOCKS * index(i) - 1, 0), SEG_KV // (2 * D_KV)))


def _pair_cols(g, p, base=0):
    return slice(base + (4 * g + p) * 128, base + (4 * g + p + 1) * 128)


def _mixer_fwd(z, sink_rows, ln_g, ln_b, sgu_w, sgu_bt):
    t = z.shape[0]

    def body(z_all, kvp_ref, sink_ref, lng_ref, lnb_ref, w_ref, bt_ref, a_all, prob_ref, sink_prob_ref):
        kv_before = kvp_ref[...]
        for sub in range(MIXER_BLOCKS):
            z_ref, a_ref = _Rows(z_all, sub), _Rows(a_all, sub)
            one_block(z_ref, kv_before, MIXER_BLOCKS * pl.program_id(0) + sub, sink_ref, lng_ref, lnb_ref, w_ref,
                      bt_ref, a_ref, prob_ref.at[sub], sink_prob_ref.at[sub])
            kv_before = z_ref[:, SEG_KV:SEG_KV + 2 * D_KV]

    def one_block(z_ref, kv_before, block_index, sink_ref, lng_ref, lnb_ref, w_ref, bt_ref, a_ref, prob_ref,
                  sink_prob_ref):
        bias = _window_bias(block_index)
        kk, vv = _keys_values(z_ref, kv_before)
        kk_b, vvt_b = kk.astype(BF16), vv.T.astype(BF16)
        for g in range(2):
            qt = _heads_t([z_ref[:, _pair_cols(g, p, SEG_Q)] * ATTN_SCALE for p in range(4)], g).astype(BF16)
            prob, sink_prob = _softmax_t(_dot(kk_b, qt), bias, sink_ref[g])
            prob_b = prob.astype(BF16)
            prob_ref[g] = prob_b
            sink_prob_ref[g] = sink_prob
            ot = _dot(vvt_b, prob_b)
            for p in range(4):
                gate = z_ref[:, _pair_cols(g, p, SEG_GA)]
                a_ref[:, _pair_cols(g, p)] = (_pair_block(ot, p, g) * (gate * _sigmoid(gate))).astype(BF16)

        vhat, _ = _layer_norm_fwd(z_ref[:, SEG_VS:SEG_VS + D_SGU])
        vn = vhat * lng_ref[...] + lnb_ref[...]
        tril = _tril()
        for g in range(SGU_GROUPS):
            cols = slice(g * 128, (g + 1) * 128)
            wm = jnp.where(tril, w_ref[g], 0.0).astype(BF16)
            mixed = _dot(wm, vn[:, cols].astype(BF16)) + bt_ref[:, g:g + 1]
            gate = z_ref[:, SEG_GS + g * 128:SEG_GS + (g + 1) * 128]
            a_ref[:, D_ATTN + g * 128:D_ATTN + (g + 1) * 128] = (
                (z_ref[:, SEG_U + g * 128:SEG_U + (g + 1) * 128] * mixed) * (gate * _sigmoid(gate))).astype(BF16)

    rows = MIXER_BLOCKS * BLOCK
    return pl.pallas_call(
        body, name="mixer_fwd", grid=(t // rows,),
        in_specs=[pl.BlockSpec((rows, D_IN), lambda i: (i, 0)), _kv_before_spec(lambda i: i),
                  _const_spec((2, 1, 8 * BLOCK)), _const_spec((1, D_SGU)), _const_spec((1, D_SGU)),
                  _const_spec((SGU_GROUPS, BLOCK, BLOCK)), _const_spec((BLOCK, SGU_GROUPS))],
        out_specs=(pl.BlockSpec((rows, D_MODEL), lambda i: (i, 0)),
                   pl.BlockSpec((MIXER_BLOCKS, 2, 2 * BLOCK, 8 * BLOCK), lambda i: (i, 0, 0, 0)),
                   pl.BlockSpec((MIXER_BLOCKS, 2, 1, 8 * BLOCK), lambda i: (i, 0, 0, 0))),
        out_shape=(jax.ShapeDtypeStruct((t, D_MODEL), BF16),
                   jax.ShapeDtypeStruct((t // BLOCK, 2, 2 * BLOCK, 8 * BLOCK), BF16),
                   jax.ShapeDtypeStruct((t // BLOCK, 2, 1, 8 * BLOCK), F32)),
        compiler_params=_params(dimension_semantics=("arbitrary",)),
    )(z, z, sink_rows, ln_g, ln_b, sgu_w, sgu_bt)


def _mixer_bwd(z, da, probs, sink_probs, ln_g, ln_b, sgu_w, sgu_wt, sgu_bt):
    t = z.shape[0]

    def body(z_all, kvp_ref, da_all, prob_ref, sink_prob_ref, lng_ref, lnb_ref, w_ref, wt_ref, bt_ref,
             dz_all, dsink_ref, dw_ref, db_ref, dlng_ref, dlnb_ref, carry_ref, dsink_acc, dbt_acc):
        step = pl.program_id(0)

        @pl.when(step == 0)
        def _():
            carry_ref[...] = jnp.zeros_like(carry_ref)
            dsink_acc[...] = jnp.zeros_like(dsink_acc)
            dbt_acc[...] = jnp.zeros_like(dbt_acc)
            dw_ref[...] = jnp.zeros_like(dw_ref)
            dlng_ref[...] = jnp.zeros_like(dlng_ref)
            dlnb_ref[...] = jnp.zeros_like(dlnb_ref)

        carry = carry_ref[...]
        for sub in reversed(range(MIXER_BLOCKS)):
            kv_before = kvp_ref[...] if sub == 0 else _Rows(z_all, sub - 1)[:, SEG_KV:SEG_KV + 2 * D_KV]
            carry = one_block(_Rows(z_all, sub), kv_before, _Rows(da_all, sub), prob_ref.at[sub], sink_prob_ref.at[sub],
                              carry, lng_ref, lnb_ref, w_ref, wt_ref, bt_ref, _Rows(dz_all, sub),
                              dw_ref, dlng_ref, dlnb_ref, dsink_acc, dbt_acc)
        carry_ref[...] = carry

        @pl.when(step == ns - 1)
        def _():
            db_ref[...] = dbt_acc[...].T[:SGU_GROUPS]
            lane_row = lax.broadcasted_iota(jnp.int32, (1, 128), 1)
            d_sink = jnp.zeros((1, 128), F32)
            for g in range(2):
                acc = dsink_acc[g]
                for j in range(8):
                    head_sum = jnp.sum(acc[:, j * BLOCK:(j + 1) * BLOCK], axis=-1, keepdims=True)
                    d_sink = d_sink + jnp.where(lane_row == 8 * g + j, head_sum, 0.0)
            dsink_ref[...] = d_sink

    def one_block(z_ref, kv_before, da_ref, prob_ref, sink_prob_ref, carry, lng_ref, lnb_ref, w_ref, wt_ref, bt_ref,
                  dz_ref, dw_ref, dlng_ref, dlnb_ref, dsink_acc, dbt_acc):
        kk, vv = _keys_values(z_ref, kv_before)
        vv_b = vv.astype(BF16)
        kkt_b, vvt_b = kk.T.astype(BF16), vv.T.astype(BF16)
        dkk = jnp.zeros((2 * BLOCK, D_KV), F32)
        dvv = jnp.zeros((2 * BLOCK, D_KV), F32)
        for g in range(2):
            qt = _heads_t([z_ref[:, _pair_cols(g, p, SEG_Q)] * ATTN_SCALE for p in range(4)], g).astype(BF16)
            prob_b, sink_prob = prob_ref[g], sink_prob_ref[g]
            prob = prob_b.astype(F32)
            ot = _dot(vvt_b, prob_b)
            gates = [z_ref[:, _pair_cols(g, p, SEG_GA)] for p in range(4)]
            sig = [_sigmoid(gt) for gt in gates]
            d_attn = [da_ref[:, _pair_cols(g, p)] for p in range(4)]
            d_ot = _heads_t([d_attn[p] * (gates[p] * sig[p]) for p in range(4)], g).astype(BF16)
            d_prob = _dot(vv_b, d_ot)
            delta = jnp.sum(prob * d_prob, axis=0, keepdims=True)
            d_scores = (prob * (d_prob - delta)).astype(BF16)
            dsink_acc[g] -= sink_prob * delta
            d_qt = _dot(kkt_b, d_scores)
            dkk = dkk + _dot_nt(d_scores, qt)
            dvv = dvv + _dot_nt(prob_b, d_ot)
            for p in range(4):
                dz_ref[:, _pair_cols(g, p, SEG_Q)] = (_pair_block(d_qt, p, g) * ATTN_SCALE).astype(BF16)
                d_silu = sig[p] * (1.0 + gates[p] * (1.0 - sig[p]))
                dz_ref[:, _pair_cols(g, p, SEG_GA)] = (d_attn[p] * _pair_block(ot, p, g) * d_silu).astype(BF16)
        d_kv = jnp.concatenate([dkk, dvv], axis=1)
        dz_ref[:, SEG_KV:SEG_KV + 2 * D_KV] = (d_kv[BLOCK:] + carry).astype(BF16)

        vhat, rstd = _layer_norm_fwd(z_ref[:, SEG_VS:SEG_VS + D_SGU])
        lng = lng_ref[...]
        vn = vhat * lng + lnb_ref[...]
        tril, triu = _tril(), _tril(transposed=True)
        lane = lax.broadcasted_iota(jnp.int32, (BLOCK, 128), 1)
        d_bt = jnp.zeros((BLOCK, 128), F32)
        d_vn = []
        for g in range(SGU_GROUPS):
            cols = slice(g * 128, (g + 1) * 128)
            wm = jnp.where(tril, w_ref[g], 0.0).astype(BF16)
            wmt = jnp.where(triu, wt_ref[g], 0.0).astype(BF16)
            vn_g = vn[:, cols].astype(BF16)
            mixed = _dot(wm, vn_g) + bt_ref[:, g:g + 1]
            gate = z_ref[:, SEG_GS + g * 128:SEG_GS + (g + 1) * 128]
            u = z_ref[:, SEG_U + g * 128:SEG_U + (g + 1) * 128]
            d_out = da_ref[:, D_ATTN + g * 128:D_ATTN + (g + 1) * 128]
            sg = _sigmoid(gate)
            d_um = d_out * (gate * sg)
            dz_ref[:, SEG_U + g * 128:SEG_U + (g + 1) * 128] = (d_um * mixed).astype(BF16)
            dz_ref[:, SEG_GS + g * 128:SEG_GS + (g + 1) * 128] = (
                d_out * (u * mixed) * (sg * (1.0 + gate * (1.0 - sg)))).astype(BF16)
            d_mixed = d_um * u
            d_mixed_b = d_mixed.astype(BF16)
            dw_ref[g] += jnp.where(tril, _dot_nt(d_mixed_b, vn_g), 0.0)
            d_bt = d_bt + jnp.where(lane == g, jnp.sum(d_mixed, axis=-1, keepdims=True), 0.0)
            d_vn.append(_dot(wmt, d_mixed_b))
        dbt_acc[...] += d_bt
        d_vn = jnp.concatenate(d_vn, axis=1)
        dlng_ref[...] += jnp.sum(d_vn * vhat, axis=0, keepdims=True)
        dlnb_ref[...] += jnp.sum(d_vn, axis=0, keepdims=True)
        d_vhat = d_vn * lng
        d_v = rstd * (d_vhat - jnp.mean(d_vhat, axis=-1, keepdims=True)
                      - vhat * jnp.mean(d_vhat * vhat, axis=-1, keepdims=True))
        dz_ref[:, SEG_VS:SEG_VS + D_SGU] = d_v.astype(BF16)
        return d_kv[:BLOCK]

    rows = MIXER_BLOCKS * BLOCK
    ns = t // rows
    rev = lambda i: ns - 1 - i
    return pl.pallas_call(
        body, name="mixer_bwd", grid=(ns,),
        in_specs=[pl.BlockSpec((rows, D_IN), lambda i: (rev(i), 0)), _kv_before_spec(rev),
                  pl.BlockSpec((rows, D_MODEL), lambda i: (rev(i), 0)),
                  pl.BlockSpec((MIXER_BLOCKS, 2, 2 * BLOCK, 8 * BLOCK), lambda i: (rev(i), 0, 0, 0)),
                  pl.BlockSpec((MIXER_BLOCKS, 2, 1, 8 * BLOCK), lambda i: (rev(i), 0, 0, 0)),
                  _const_spec((1, D_SGU)), _const_spec((1, D_SGU)),
                  _const_spec((SGU_GROUPS, BLOCK, BLOCK)), _const_spec((SGU_GROUPS, BLOCK, BLOCK)),
                  _const_spec((BLOCK, SGU_GROUPS))],
        out_specs=(pl.BlockSpec((rows, D_IN), lambda i: (rev(i), 0)), _const_spec((1, 128)),
                   _const_spec((SGU_GROUPS, BLOCK, BLOCK)), _const_spec((SGU_GROUPS, BLOCK)),
                   _const_spec((1, D_SGU)), _const_spec((1, D_SGU))),
        out_shape=(jax.ShapeDtypeStruct((t, D_IN), BF16), jax.ShapeDtypeStruct((1, 128), F32),
                   jax.ShapeDtypeStruct((SGU_GROUPS, BLOCK, BLOCK), F32), jax.ShapeDtypeStruct((SGU_GROUPS, BLOCK), F32),
                   jax.ShapeDtypeStruct((1, D_SGU), F32), jax.ShapeDtypeStruct((1, D_SGU), F32)),
        scratch_shapes=[pltpu.VMEM((BLOCK, 2 * D_KV), F32), pltpu.VMEM((2, 1, 8 * BLOCK), F32),
                        pltpu.VMEM((BLOCK, 128), F32)],
        compiler_params=_params(dimension_semantics=("arbitrary",)),
    )(z, z, da, probs, sink_probs, ln_g, ln_b, sgu_w, sgu_wt, sgu_bt)


def _out_proj_head(a, w_out_full, x, target, mod, final_g, tm=256):
    t, d = x.shape

    def body(a_ref, w_ref, x_ref, tg_ref, gate_ref, fg_ref, dx2_ref, dy_ref, loss_ref, dfg_ref, dgate_ref):
        @pl.when(pl.program_id(0) == 0)
        def _():
            loss_ref[...] = jnp.zeros_like(loss_ref)
            dfg_ref[...] = jnp.zeros_like(dfg_ref)
            dgate_ref[...] = jnp.zeros_like(dgate_ref)

        yv, gate, fg = _dot(a_ref[...], w_ref[...]), gate_ref[...], fg_ref[...]
        x2 = x_ref[...] + gate * yv
        r2 = lax.rsqrt(jnp.mean(x2 * x2, axis=-1, keepdims=True) + EPS)
        nrm = x2 * r2
        err = nrm * fg - tg_ref[...]
        loss_ref[...] += 0.5 * jnp.sum(jnp.mean(err * err, axis=-1, keepdims=True), axis=0, keepdims=True)
        fg_d = fg * (1.0 / d)
        err_nrm = err * nrm
        dfg_ref[...] += jnp.sum(err_nrm, axis=0, keepdims=True) * (1.0 / d)
        d_nrm = err * fg_d
        dx2 = r2 * (d_nrm - nrm * jnp.mean(err_nrm * fg_d, axis=-1, keepdims=True))
        dx2_ref[...] = dx2
        dgate_ref[...] += jnp.sum(dx2 * yv, axis=0, keepdims=True)
        dy_ref[...] = (dx2 * gate).astype(BF16)

    blk = pl.BlockSpec((tm, d), lambda i: (i, 0))
    row = _const_spec((1, d))
    whole = pl.BlockSpec(w_out_full.shape, lambda i: (0, 0), pipeline_mode=pl.Buffered(1))
    return pl.pallas_call(
        body, name="out_proj_head", grid=(t // tm,),
        in_specs=[pl.BlockSpec((tm, a.shape[1]), lambda i: (i, 0)), whole, blk, blk, _mod_spec(MOD_GATE, d), row],
        out_specs=(blk, blk, _const_spec((1, 128)), row, row),
        out_shape=(jax.ShapeDtypeStruct((t, d), F32), jax.ShapeDtypeStruct((t, d), BF16),
                   jax.ShapeDtypeStruct((1, 128), F32), jax.ShapeDtypeStruct((1, d), F32),
                   jax.ShapeDtypeStruct((1, d), F32)),
        compiler_params=_params(dimension_semantics=("arbitrary",)),
    )(a, w_out_full, x, target, mod, final_g)


def _z_proj_bwd_norm(dz, w_in_t, x, dx2, norm_g, mod, dep, tm=256):
    t, d = x.shape

    def body(dz_ref, w_ref, x_ref, dx2_ref, g_ref, sc_ref, dep_ref, gx_ref, dshift_ref, dscale_ref, dg_ref):
        @pl.when(pl.program_id(0) == 0)
        def _():
            dshift_ref[...] = jnp.zeros_like(dshift_ref)
            dscale_ref[...] = jnp.zeros_like(dscale_ref)
            dg_ref[...] = jnp.zeros_like(dg_ref)

        dh, xv, g = _dot(dz_ref[...], w_ref[...]), x_ref[...], g_ref[...]
        one_plus = 1.0 + sc_ref[...]
        r = lax.rsqrt(jnp.mean(xv * xv, axis=-1, keepdims=True) + EPS)
        xn = xv * r
        gain = one_plus * g
        dh_xn = dh * xn
        dh_xn_sum = jnp.sum(dh_xn, axis=0, keepdims=True)
        dshift_ref[...] += jnp.sum(dh, axis=0, keepdims=True)
        dscale_ref[...] += dh_xn_sum * g
        dg_ref[...] += dh_xn_sum * one_plus
        d_xn = dh * gain
        gx_ref[...] = dx2_ref[...] + r * (d_xn - xn * jnp.mean(dh_xn * gain, axis=-1, keepdims=True))

    blk = pl.BlockSpec((tm, d), lambda i: (i, 0))
    row = _const_spec((1, d))
    whole = pl.BlockSpec(w_in_t.shape, lambda i: (0, 0), pipeline_mode=pl.Buffered(1))
    return pl.pallas_call(
        body, name="z_proj_bwd_norm", grid=(t // tm,),
        in_specs=[pl.BlockSpec((tm, dz.shape[1]), lambda i: (i, 0)), whole, blk, blk, row, _mod_spec(MOD_SCALE, d),
                  _const_spec((8, 128))],
        out_specs=(blk, row, row, row),
        out_shape=(jax.ShapeDtypeStruct((t, d), F32),) + (jax.ShapeDtypeStruct((1, d), F32),) * 3,
        compiler_params=_params(dimension_semantics=("arbitrary",)),
    )(dz, w_in_t, x, dx2, norm_g, mod, dep)


def _adamw(w, g, m, v):
    m = ADAM_B1 * m + (1.0 - ADAM_B1) * g
    v = ADAM_B2 * v + (1.0 - ADAM_B2) * (g * g)
    m_hat = m / (1.0 - ADAM_B1 ** ADAM_STEP)
    v_hat = v / (1.0 - ADAM_B2 ** ADAM_STEP)
    delta = -ADAM_LR * (m_hat / (jnp.sqrt(v_hat) + ADAM_EPS) + ADAM_WD * w)
    return delta, m, v


def _relay_sum(device, blocks, land_pair, land_first, tr):
    _, r, c = blocks.shape

    def body(device_ref, a_ref, b_ref, c_ref, o_ref):
        o_ref[...] = (a_ref[...].astype(F32) + b_ref[...].astype(F32) + c_ref[...].astype(F32)).astype(BF16)

    second = pl.BlockSpec((None, tr, c), lambda i, device_ref: (1, i, 0))
    return pl.pallas_call(
        body, name="w_in_grad_relay_sum",
        grid_spec=pltpu.PrefetchScalarGridSpec(
            num_scalar_prefetch=1, grid=(r // tr,),
            in_specs=[pl.BlockSpec((None, tr, c), lambda i, device_ref: (device_ref[0], i, 0)), second, second],
            out_specs=pl.BlockSpec((tr, c), lambda i, device_ref: (i, 0))),
        out_shape=jax.ShapeDtypeStruct((r, c), BF16),
        compiler_params=_params(dimension_semantics=("arbitrary",)),
    )(device, blocks, land_pair, land_first)


def _adam_from_chips(chip, pair, landed, w, m, v, name, tc):
    _, r, c = pair.shape
    n = len(landed)

    def body(chip_ref, own_ref, *refs):
        w_ref, m_ref, v_ref, g_ref, d_ref, nm_ref, nv_ref = refs[n:]
        g = own_ref[...].astype(F32)
        for k in range(n):
            g = g + refs[k][...].astype(F32)
        g_ref[...] = g
        d_ref[...], nm_ref[...], nv_ref[...] = _adamw(w_ref[...], g, m_ref[...], v_ref[...])

    def landed_spec(index):
        return pl.BlockSpec((None, r, tc), lambda i, chip_ref: (index, 0, i))

    blk = pl.BlockSpec((r, tc), lambda i, chip_ref: (0, i))
    return pl.pallas_call(
        body, name=name,
        grid_spec=pltpu.PrefetchScalarGridSpec(
            num_scalar_prefetch=1, grid=(c // tc,),
            in_specs=[pl.BlockSpec((None, r, tc), lambda i, chip_ref: (chip_ref[0], 0, i))]
            + [landed_spec(index) for _, index in landed] + [blk, blk, blk],
            out_specs=(blk,) * 4),
        out_shape=(jax.ShapeDtypeStruct((r, c), F32),) * 4,
        compiler_params=_params(dimension_semantics=("arbitrary",)),
    )(chip, pair, *[array for array, _ in landed], w, m, v)


def _adam_w_ada(device, act_t, dmod_all, w, m, v, tr=512):
    r, c = w.shape

    def body(device_ref, a_ref, dm_ref, w_ref, m_ref, v_ref, g_ref, d_ref, nm_ref, nv_ref):
        g = _dot(a_ref[...].astype(BF16), dm_ref[...].astype(BF16))
        g_ref[...] = g
        d_ref[...], nm_ref[...], nv_ref[...] = _adamw(w_ref[...], g, m_ref[...], v_ref[...])

    blk = pl.BlockSpec((tr, c), lambda i, device_ref: (i, 0))
    return pl.pallas_call(
        body, name="adam_w_ada",
        grid_spec=pltpu.PrefetchScalarGridSpec(
            num_scalar_prefetch=1, grid=(r // tr,),
            in_specs=[pl.BlockSpec((tr, N_DEV), lambda i, device_ref: (i, 0)),
                      pl.BlockSpec((N_DEV, c), lambda i, device_ref: (0, device_ref[0])), blk, blk, blk],
            out_specs=(blk,) * 4),
        out_shape=(jax.ShapeDtypeStruct((r, c), F32),) * 4,
        compiler_params=_params(dimension_semantics=("arbitrary",)),
    )(device, act_t, dmod_all, w, m, v)


def _pack_small(d_shift, d_scale, d_gate, d_norm_g, d_final_g, d_ln_g, d_ln_b, loss, d_sinks, d_sgu_b):
    def body(shift_ref, scale_ref, gate_ref, ng_ref, fg_ref, lng_ref, lnb_ref, loss_ref, sink_ref, b_ref, o_ref):
        o_ref[...] = jnp.zeros_like(o_ref)
        o_ref[ROW_SHIFT:ROW_SHIFT + 1, :] = shift_ref[...]
        o_ref[ROW_SCALE:ROW_SCALE + 1, :] = scale_ref[...]
        o_ref[ROW_GATE:ROW_GATE + 1, :] = gate_ref[...]
        o_ref[ROW_NORM_G:ROW_NORM_G + 1, :] = ng_ref[...]
        o_ref[ROW_FINAL_G:ROW_FINAL_G + 1, :] = fg_ref[...]
        o_ref[ROW_LN:ROW_LN + 1, 0:D_SGU] = lng_ref[...]
        o_ref[ROW_LN:ROW_LN + 1, D_SGU:2 * D_SGU] = lnb_ref[...]
        o_ref[ROW_MISC:ROW_MISC + 1, 0:128] = loss_ref[...]
        o_ref[ROW_MISC:ROW_MISC + 1, 128:256] = sink_ref[...]
        o_ref[ROW_SGU_B:ROW_SGU_B + SGU_GROUPS, 0:BLOCK] = b_ref[...]

    return pl.pallas_call(
        body, name="pack_small", out_shape=jax.ShapeDtypeStruct((SMALL_ROWS, D_MODEL), F32),
        compiler_params=_params(),
    )(d_shift, d_scale, d_gate, d_norm_g, d_final_g, d_ln_g, d_ln_b, loss, d_sinks, d_sgu_b)


_SMALL_NAMES = ("norm_g", "b_ada", "attn_sinks", "sgu_ln_g", "sgu_ln_b", "sgu_w", "sgu_b", "final_g")


def _adam_small(partials, d_sgu_w_all, weights, moments_m, moments_v):
    names = _SMALL_NAMES
    k = len(names)

    def body(*refs):
        p_ref, sw_ref = refs[0], refs[1]
        w_refs, m_refs, v_refs = refs[2:2 + k], refs[2 + k:2 + 2 * k], refs[2 + 2 * k:2 + 3 * k]
        loss_ref, dmod_ref = refs[2 + 3 * k], refs[3 + 3 * k]
        out_refs = refs[4 + 3 * k:4 + 7 * k]
        sum_ref = refs[4 + 7 * k]
        total = p_ref[0]
        for j in range(1, N_DEV):
            total = total + p_ref[j]
        sum_ref[...] = total
        for j in range(N_DEV):
            for part, row in enumerate((ROW_SHIFT, ROW_SCALE, ROW_GATE)):
                dmod_ref[j:j + 1, part * D_MODEL:(part + 1) * D_MODEL] = p_ref[j, row:row + 1, :]
        loss_ref[...] = sum_ref[ROW_MISC:ROW_MISC + 1, 0:1]
        d_sgu_w = sw_ref[0]
        for j in range(1, N_DEV):
            d_sgu_w = d_sgu_w + sw_ref[j]
        grads = {
            "norm_g": sum_ref[ROW_NORM_G:ROW_NORM_G + 1, :],
            "b_ada": jnp.concatenate([sum_ref[r:r + 1, :] for r in (ROW_SHIFT, ROW_SCALE, ROW_GATE)], axis=1),
            "attn_sinks": sum_ref[ROW_MISC:ROW_MISC + 1, 128:128 + N_Q_HEADS],
            "sgu_ln_g": sum_ref[ROW_LN:ROW_LN + 1, 0:D_SGU],
            "sgu_ln_b": sum_ref[ROW_LN:ROW_LN + 1, D_SGU:2 * D_SGU],
            "sgu_w": d_sgu_w[None],
            "sgu_b": sum_ref[ROW_SGU_B:ROW_SGU_B + SGU_GROUPS, 0:BLOCK][None],
            "final_g": sum_ref[ROW_FINAL_G:ROW_FINAL_G + 1, :],
        }
        for i, name in enumerate(names):
            g = grads[name]
            delta, m, v = _adamw(w_refs[i][...], g, m_refs[i][...], v_refs[i][...])
            out_refs[4 * i][...] = g
            out_refs[4 * i + 1][...] = delta
            out_refs[4 * i + 2][...] = m
            out_refs[4 * i + 3][...] = v

    shapes = [jax.ShapeDtypeStruct((1, 1), F32), jax.ShapeDtypeStruct((N_DEV, 3 * D_MODEL), F32)]
    for name in names:
        shapes += [jax.ShapeDtypeStruct(weights[name].shape, F32)] * 4
    outs = pl.pallas_call(
        body, name="adam_small", out_shape=tuple(shapes),
        scratch_shapes=[pltpu.VMEM((SMALL_ROWS, D_MODEL), F32)],
        compiler_params=_params(),
    )(partials, d_sgu_w_all, *[weights[n] for n in names], *[moments_m[n] for n in names],
      *[moments_v[n] for n in names])
    return outs[0], outs[1], {name: outs[2 + 4 * i:6 + 4 * i] for i, name in enumerate(names)}


def kernel(x, c, norm_g, w_ada, b_ada, w_in, attn_sinks, sgu_ln_g, sgu_ln_b, sgu_w, sgu_b, w_out, final_g, loss_target, m_norm_g, m_w_ada, m_b_ada, m_w_in, m_attn_sinks, m_sgu_ln_g, m_sgu_ln_b, m_sgu_w, m_sgu_b, m_w_out, m_final_g, v_norm_g, v_w_ada, v_b_ada, v_w_in, v_attn_sinks, v_sgu_ln_g, v_sgu_ln_b, v_sgu_w, v_sgu_b, v_w_out, v_final_g):
    xi, yi, ci = _place()
    me = 4 * xi + 2 * yi + ci
    x2d, target = x[0], loss_target[0]
    t = x2d.shape[0]

    core = ci.astype(jnp.int32).reshape(1)
    chip = (2 * xi + yi).astype(jnp.int32).reshape(1)

    first = _own_block_copies(_first_targets)
    first_flight = _start_copies([_with_own_slot(w_in[0].T.astype(BF16), me)], first, 2, core, "gather_w_in_start")

    c_all = _all_gather_small(c.reshape(8, 256) + first_flight[3][0, 0], "gather_c").reshape(N_DEV, D_MODEL)
    device = me.astype(jnp.int32).reshape(1)
    c_act, mod_part = _modulation(device, c_all, w_ada[0], b_ada)
    mod_all = _all_gather_small(mod_part, "gather_mod")

    across = _wait_then_start(first_flight, lambda *a: first(*a)[1:], _second_axis_stage_copies, 3, mod_all,
                              "gather_w_in_second_axis_stage")
    mod = lax.dynamic_index_in_dim(mod_all, me, axis=1, keepdims=False).reshape(1, 3 * D_MODEL)
    mod = mod + across[3][0, 0]

    w_in_pair = _wait_copies((first_flight[0], first_flight[1], across[2], None), lambda *a: first(*a)[:1], mod,
                             "gather_w_in_sibling_wait")
    h, z_own = _norm_z_proj_own(x2d, norm_g, mod, w_in_pair[0].reshape(D_IN, D_MODEL), chip)
    w_out_early = _own_block_copies(lambda x, y, c: [(x, y, 1 - c), (*_second_axis_chip(x, y, c), c)])
    w_out_late = _own_block_copies(lambda x, y, c: [(*_first_axis_chip(x, y, c), c), (1 - x, 1 - y, c)])
    forward = _wait_then_start(
        (across[0], across[1], w_in_pair, None), lambda *a: _second_axis_stage_copies(*a)[:1],
        lambda refs, s, r: _second_axis_forward_copies(refs[:1], s, r) + _group(w_out_early, 1, 1, 1)(refs, s, r),
        3, z_own, "gather_w_in_second_axis_forward", more_bufs=[_with_own_slot(w_out[0].astype(BF16), me)])
    w_in_most = _wait_copies((across[0], across[1], forward[2][:1], None),
                             lambda *a: _second_axis_stage_copies(*a)[1:2], z_own, "gather_w_in_first_forward_wait")
    w_in_most = _wait_copies((forward[0], forward[1], w_in_most, None), _second_axis_forward_copies, z_own,
                             "gather_w_in_second_forward_wait")
    z_early = _z_proj(h, w_in_most[0].reshape(D_IN, D_MODEL), chip, 1, _Z_EARLY_TILES - 1, z_own, "z_proj_early")
    last = _wait_then_start(
        (across[0], across[1], [w_in_most[0], forward[2][1]], None), lambda *a: _second_axis_stage_copies(*a)[2:],
        lambda refs, s, r: _diagonal_forward_copies(refs[:1], s, r) + _group(w_out_late, 1, 1, 1)(refs, s, r),
        3, z_early, "gather_w_in_last_stage")
    w_in_all = _wait_copies((last[0], last[1], last[2][:1], None), _diagonal_forward_copies, z_early,
                            "gather_w_in_last_wait")[0]
    w_in_t = w_in_all.reshape(D_IN, D_MODEL)
    z = _z_proj(h, w_in_t, chip, _Z_EARLY_TILES, 7 - _Z_EARLY_TILES, z_early, "z_proj_late")
    w_out_half = _wait_copies((forward[0], forward[1], last[2][1:], None), _group(w_out_early, 0, 1, 1), z,
                              "gather_w_out_early_wait")
    w_out_flight = _wait_then_start((last[0], last[1], w_out_half, None), _group(w_out_late, 0, 1, 1),
                                    _forward_copies, 3, z, "gather_w_out_forward_stage")
    sink_rows = jnp.repeat(attn_sinks.reshape(N_Q_HEADS), BLOCK).reshape(2, 1, 8 * BLOCK)
    sgu_bt = sgu_b[0].T
    a, probs, sink_probs = _mixer_fwd(z, sink_rows + w_out_flight[3][0, 0], sgu_ln_g, sgu_ln_b, sgu_w[0], sgu_bt)
    w_out_all = _wait_copies(w_out_flight, _forward_copies, a, "gather_w_out_forward_wait")[0]
    w_out_full = w_out_all.reshape(D_MODEL, D_MODEL)
    final_g_row = final_g.reshape(1, D_MODEL)
    dx2, dy, loss_part, d_final_g, d_gate = _out_proj_head(a, w_out_full, x2d, target, mod, final_g_row)

    da = _matmul(dy, w_out_full, "nt", F32, min(t, 1024), 1024, "out_proj_bwd")
    dw_out = _matmul(a, dy, "tn", BF16, 1024, 1024, "w_out_grad")
    pair_out = _pair_reduce(dw_out.reshape(4, 2, W_OUT_SHARD, D_MODEL), _other_chips, "w_out_grad_pair_reduce",
                            W_OUT_SHARD // 2)
    dz, d_sinks, d_sgu_w, d_sgu_b, d_ln_g, d_ln_b = _mixer_bwd(
        z, da, probs, sink_probs, sgu_ln_g, sgu_ln_b, sgu_w[0], jnp.swapaxes(sgu_w[0], 1, 2), sgu_bt)
    w_out_copies = lambda refs, s, r: _chip_copies(refs[:2], s, r) + _group(_own_chip_pair_copy, 2, 2, 3)(refs, s, r)
    sgu_w_to_all = _group(_own_block_copies(_all_others), 4, 1, 4)
    both = _start_copies(
        [pair_out, lax.empty((3, W_OUT_SHARD, D_MODEL), BF16), dw_out.reshape(N_DEV, W_OUT_SHARD, D_MODEL),
         lax.empty((1, W_OUT_SHARD, D_MODEL), BF16), _with_own_slot(d_sgu_w, me)],
        lambda refs, s, r: w_out_copies(refs, s, r) + sgu_w_to_all(refs, s, r), 4 + N_DEV - 1, core,
        "w_out_grad_chip_and_sgu_w_gather_start")
    out_flight, sgu_w_flight = (both[0], both[1], both[2][:4], None), (both[0], both[1], both[2][4:], None)
    dw_in_t = _matmul(dz, h, "tn", BF16, 768, D_MODEL, "w_in_grad", dep=both[3])
    pair_in = _pair_reduce(dw_in_t.reshape(4, 2, W_IN_SHARD, D_MODEL), _first_hop_chips, "w_in_grad_pair_reduce",
                           W_IN_SHARD // 3)
    first_hop = lambda refs, s, r: _first_hop_copies(refs[:2], s, r) + _group(_late_pair_copies, 2, 2, 2)(refs, s, r)
    hop1 = _start_copies(
        [pair_in, lax.empty((2, W_IN_SHARD, D_MODEL), BF16), dw_in_t.reshape(N_DEV, W_IN_SHARD, D_MODEL),
         lax.empty((2, W_IN_SHARD, D_MODEL), BF16)], first_hop, 4, core, "w_in_grad_first_hop_start")
    grad_x, d_shift, d_scale, d_norm_g = _z_proj_bwd_norm(dz, w_in_t, x2d, dx2, norm_g, mod, hop1[3])

    partial = _pack_small(d_shift, d_scale, d_gate, d_norm_g, d_final_g, d_ln_g, d_ln_b, loss_part, d_sinks, d_sgu_b)
    small_flight = _start_copies([_with_own_slot(partial, me)], _own_block_copies(_all_others), N_DEV - 1, core,
                                 "small_grad_gather_start")
    _, land_first, dw_in_t, land_pair = _wait_copies(hop1, first_hop, small_flight[3], "w_in_grad_first_hop_wait")
    second_device = (4 * ((xi + ci) % 2) + 2 * ((yi + 1 - ci) % 2) + ci).astype(jnp.int32).reshape(1)
    relay = _relay_sum(second_device, dw_in_t, land_pair, land_first, W_IN_SHARD // 3)
    hop2 = _start_copies([relay, lax.empty((1, W_IN_SHARD, D_MODEL), BF16)], _second_hop_copies, 1, core,
                         "w_in_grad_second_hop_start")
    _, land_out, dw_out, land_own = _wait_copies(out_flight, w_out_copies, hop2[3], "w_out_grad_chip_wait")
    big = {"w_out": _adam_from_chips(device, dw_out, [(land_own, 0)] + [(land_out, k) for k in range(3)], w_out[0],
                                     m_w_out[0], v_w_out[0], "adam_w_out", 1024)}
    partial_all = _wait_copies(small_flight, _own_block_copies(_all_others), big["w_out"][0],
                               "small_grad_gather_wait")[0]
    d_sgu_w_all = _wait_copies(sgu_w_flight, _group(_own_block_copies(_all_others), 0, 1, 4), partial_all,
                               "sgu_w_grad_gather_wait")[0]
    weights = {"norm_g": norm_g, "b_ada": b_ada, "attn_sinks": attn_sinks, "sgu_ln_g": sgu_ln_g,
               "sgu_ln_b": sgu_ln_b, "sgu_w": sgu_w, "sgu_b": sgu_b, "final_g": final_g_row}
    moments_m = {"norm_g": m_norm_g, "b_ada": m_b_ada, "attn_sinks": m_attn_sinks, "sgu_ln_g": m_sgu_ln_g,
                 "sgu_ln_b": m_sgu_ln_b, "sgu_w": m_sgu_w, "sgu_b": m_sgu_b,
                 "final_g": m_final_g.reshape(1, D_MODEL)}
    moments_v = {"norm_g": v_norm_g, "b_ada": v_b_ada, "attn_sinks": v_attn_sinks, "sgu_ln_g": v_sgu_ln_g,
                 "sgu_ln_b": v_sgu_ln_b, "sgu_w": v_sgu_w, "sgu_b": v_sgu_b,
                 "final_g": v_final_g.reshape(1, D_MODEL)}
    loss, dmod_all, small = _adam_small(partial_all, d_sgu_w_all, weights, moments_m, moments_v)
    small["final_g"] = tuple(o.reshape(D_MODEL) for o in small["final_g"])

    big["w_ada"] = _adam_w_ada(device, c_act.T, dmod_all, w_ada[0], m_w_ada[0], v_w_ada[0])
    _, land_second = _wait_copies(hop2, _second_hop_copies, big["w_ada"][0], "w_in_grad_second_hop_wait")
    big["w_in"] = tuple(o.T for o in _adam_from_chips(
        device, dw_in_t, [(land_pair, 0), (land_first, 0), (land_second, 0)], w_in[0].T, m_w_in[0].T, v_w_in[0].T,
        "adam_w_in", 512))
    order = ["norm_g", "w_ada", "b_ada", "w_in", "attn_sinks", "sgu_ln_g", "sgu_ln_b", "sgu_w", "sgu_b", "w_out",
             "final_g"]
    outs = [loss.reshape(()), grad_x[None]]
    for k in range(4):
        for name in order:
            outs.append(big[name][k][None] if name in big else small[name][k])
    return tuple(outs)
```

```python
import jax
import jax.numpy as jnp
from jax import lax
from jax.experimental import pallas as pl
from jax.experimental.pallas import tpu as pltpu

F32 = jnp.float32
BF16 = jnp.bfloat16
MESH = pl.DeviceIdType.MESH

N_DEV = 8
D_MODEL = 2048
HEAD_DIM = 64
D_ATTN = 1024
N_Q_HEADS = 16
D_KV = 128
BLOCK = 128
D_SGU = 1024
SGU_GROUPS = 8
D_IN = 5376
W_IN_SHARD = D_IN // N_DEV
W_OUT_SHARD = D_MODEL // N_DEV
W_ADA_SHARD = 3 * D_MODEL // N_DEV
EPS = 1e-6
ATTN_SCALE = 0.125

ADAM_LR = 0.001
ADAM_B1 = 0.9
ADAM_B2 = 0.999
ADAM_EPS = 1e-08
ADAM_WD = 0.01
ADAM_STEP = 10

SEG_Q, SEG_KV, SEG_GA, SEG_U, SEG_VS, SEG_GS = 0, 1024, 1280, 2304, 3328, 4352

VMEM_LIMIT = 56 * 1024 * 1024

ROW_SHIFT, ROW_SCALE, ROW_GATE, ROW_NORM_G, ROW_FINAL_G, ROW_LN, ROW_MISC, ROW_SGU_B = 0, 1, 2, 3, 4, 5, 6, 8
SMALL_ROWS = 16


def _params(**kw):
    return pltpu.CompilerParams(vmem_limit_bytes=VMEM_LIMIT, **kw)


def _sigmoid(x):
    return 0.5 * (jnp.tanh(0.5 * x) + 1.0)


def _place():
    return lax.axis_index("x"), lax.axis_index("y"), lax.axis_index("c")


def _every_chip(x, y, c):
    return [0, 1, 2, 3]


def _first_hop_chips(x, y, c):
    first = _first_axis_chip(x, y, c)
    return [2 * first[0] + first[1], 2 * (1 - x) + (1 - y)]


def _pair_reduce(blocks, chips, name, row_chunk):
    _, _, r, cols = blocks.shape
    n = len(chips(0, 0, 0))
    assert r % row_chunk == 0

    def body(in_ref, out_ref, land, own, summed, send_sems, recv_sems, own_sems, out_sems):
        x, y, c = _place()
        sends, loads, stores = [], [], []
        for m in range(n):
            cp = pltpu.make_async_remote_copy(
                src_ref=in_ref.at[chips(x, y, 1 - c)[m], 1 - c], dst_ref=land.at[m], send_sem=send_sems.at[m],
                recv_sem=recv_sems.at[m], device_id=(x, y, 1 - c), device_id_type=MESH)
            cp.start()
            sends.append(cp)
            ld = pltpu.make_async_copy(in_ref.at[chips(x, y, c)[m], c], own.at[m], own_sems.at[m])
            ld.start()
            loads.append(ld)
        for m in range(n):
            sends[m].wait_recv()
            loads[m].wait()
            for k in range(r // row_chunk):
                rows = slice(k * row_chunk, (k + 1) * row_chunk)
                summed[m, rows, :] = (own[m, rows, :].astype(F32) + land[m, rows, :].astype(F32)).astype(BF16)
            st = pltpu.make_async_copy(summed.at[m], out_ref.at[m], out_sems.at[m])
            st.start()
            stores.append(st)
        for m in range(n):
            sends[m].wait_send()
            stores[m].wait()

    spec = pl.BlockSpec(memory_space=pl.ANY)
    return pl.pallas_call(
        body, name=name, out_shape=jax.ShapeDtypeStruct((n, r, cols), BF16),
        in_specs=[spec], out_specs=spec,
        scratch_shapes=[pltpu.VMEM((n, r, cols), BF16), pltpu.VMEM((n, r, cols), BF16), pltpu.VMEM((n, r, cols), BF16),
                        pltpu.SemaphoreType.DMA((n,)), pltpu.SemaphoreType.DMA((n,)), pltpu.SemaphoreType.DMA((n,)),
                        pltpu.SemaphoreType.DMA((n,))],
        compiler_params=_params(),
    )(blocks)


_HBM = pl.BlockSpec(memory_space=pltpu.HBM)
_SEM = pl.BlockSpec(memory_space=pltpu.SEMAPHORE)
_EFFECT = pltpu.SideEffectType.DATAFLOW_SIDE_EFFECTING


def _start_copies(bufs, copies, n_copies, after, name):
    nb = len(bufs)

    def body(*refs):
        for cp in copies(refs[:nb], refs[nb + 1], refs[nb + 2]):
            cp.start()
        refs[-1][...] = jnp.zeros_like(refs[-1])

    out = pl.pallas_call(
        body, name=name,
        out_shape=(pltpu.SemaphoreType.DMA((n_copies,)), pltpu.SemaphoreType.DMA((n_copies,)),
                   *[pltpu.HBM(b.shape, b.dtype) for b in bufs], jax.ShapeDtypeStruct((8, 128), F32)),
        in_specs=(_HBM,) * nb + (pl.BlockSpec(memory_space=pl.ANY),),
        out_specs=(_SEM, _SEM) + (_HBM,) * nb + (pl.BlockSpec(memory_space=pltpu.VMEM),),
        input_output_aliases={i: 2 + i for i in range(nb)},
        compiler_params=pltpu.CompilerParams(has_side_effects=_EFFECT),
    )(*[pltpu.with_memory_space_constraint(b, pltpu.HBM) for b in bufs], after)
    return out[0], out[1], list(out[2:2 + nb]), out[-1]


def _wait_copies(flight, copies, after, name):
    send_sems, recv_sems, bufs, _ = flight
    nb = len(bufs)

    def body(*refs):
        for cp in copies(refs[:nb], refs[nb], refs[nb + 1]):
            cp.wait_send()
            cp.wait_recv()

    return pl.pallas_call(
        body, name=name,
        out_shape=tuple(pltpu.HBM(b.shape, b.dtype) for b in bufs),
        in_specs=(_HBM,) * nb + (_SEM, _SEM, pl.BlockSpec(memory_space=pl.ANY)), out_specs=(_HBM,) * nb,
        input_output_aliases={i: i for i in range(nb)},
        compiler_params=pltpu.CompilerParams(has_side_effects=_EFFECT),
    )(*bufs, send_sems, recv_sems, after)


class _From:
    def __init__(self, sems, offset):
        self.sems, self.offset = sems, offset

    @property
    def at(self):
        return self

    def __getitem__(self, k):
        return self.sems.at[k + self.offset]


def _group(copies, first_buf, n_bufs, offset):
    def grouped(refs, send_sems, recv_sems):
        return copies(refs[first_buf:first_buf + n_bufs], _From(send_sems, offset), _From(recv_sems, offset))
    return grouped


def _wait_then_start(flight, waited, started, n_started, after, name, more_bufs=()):
    old_send, old_recv, bufs, _ = flight
    bufs = list(bufs) + [pltpu.with_memory_space_constraint(b, pltpu.HBM) for b in more_bufs]
    nb = len(bufs)

    def body(*refs):
        for cp in waited(refs[:nb], refs[nb], refs[nb + 1]):
            cp.wait_send()
            cp.wait_recv()
        for cp in started(refs[:nb], refs[nb + 3], refs[nb + 4]):
            cp.start()
        refs[-1][...] = jnp.zeros_like(refs[-1])

    out = pl.pallas_call(
        body, name=name,
        out_shape=(pltpu.SemaphoreType.DMA((n_started,)), pltpu.SemaphoreType.DMA((n_started,)),
                   *[pltpu.HBM(b.shape, b.dtype) for b in bufs], jax.ShapeDtypeStruct((8, 128), F32)),
        in_specs=(_HBM,) * nb + (_SEM, _SEM, pl.BlockSpec(memory_space=pl.ANY)),
        out_specs=(_SEM, _SEM) + (_HBM,) * nb + (pl.BlockSpec(memory_space=pltpu.VMEM),),
        input_output_aliases={i: 2 + i for i in range(nb)},
        compiler_params=pltpu.CompilerParams(has_side_effects=_EFFECT),
    )(*bufs, old_send, old_recv, after)
    return out[0], out[1], list(out[2:2 + nb]), out[-1]


def _late_pair_copies(refs, send_sems, recv_sems):
    blocks_ref, land_ref = refs
    x, y, c = _place()
    first = _first_axis_chip(x, y, c)
    devices = [4 * x + 2 * y + 1 - c, 4 * first[0] + 2 * first[1] + 1 - c]
    return [pltpu.make_async_remote_copy(
        src_ref=blocks_ref.at[devices[k]], dst_ref=land_ref.at[k], send_sem=send_sems.at[k], recv_sem=recv_sems.at[k],
        device_id=(x, y, 1 - c), device_id_type=MESH) for k in range(2)]


def _chip_copies(refs, send_sems, recv_sems):
    pair_ref, land_ref = refs
    x, y, c = _place()
    chips = [(1 - x, y), (x, 1 - y), (1 - x, 1 - y)]
    return [pltpu.make_async_remote_copy(
        src_ref=pair_ref.at[2 * chip[0] + chip[1]], dst_ref=land_ref.at[k],
        send_sem=send_sems.at[k], recv_sem=recv_sems.at[k],
        device_id=(*chip, c), device_id_type=MESH) for k, chip in enumerate(chips)]


def _first_hop_copies(refs, send_sems, recv_sems):
    pair_ref, land_ref = refs
    x, y, c = _place()
    return [pltpu.make_async_remote_copy(
        src_ref=pair_ref.at[k], dst_ref=land_ref.at[k], send_sem=send_sems.at[k], recv_sem=recv_sems.at[k],
        device_id=(*_first_axis_chip(x, y, c), c), device_id_type=MESH) for k in range(2)]


def _second_hop_copies(refs, send_sems, recv_sems):
    relay_ref, land_ref = refs
    x, y, c = _place()
    second = ((x + c) % 2, (y + 1 - c) % 2)
    return [pltpu.make_async_remote_copy(
        src_ref=relay_ref, dst_ref=land_ref.at[0], send_sem=send_sems.at[0], recv_sem=recv_sems.at[0],
        device_id=(*second, c), device_id_type=MESH)]


def _own_block_copies(targets):
    def copies(refs, send_sems, recv_sems):
        x, y, c = _place()
        mine = refs[0].at[4 * x + 2 * y + c]
        return [pltpu.make_async_remote_copy(
            src_ref=mine, dst_ref=mine, send_sem=send_sems.at[k], recv_sem=recv_sems.at[k],
            device_id=to, device_id_type=MESH) for k, to in enumerate(targets(x, y, c))]
    return copies


def _all_others(x, y, c):
    flip = lambda v, f: 1 - v if f else v
    return [(flip(x, r & 4), flip(y, r & 2), flip(c, r & 1)) for r in range(1, N_DEV)]


def _forward_copies(refs, send_sems, recv_sems):
    x, y, c = _place()
    chips = [(1 - x, y), (x, 1 - y), (1 - x, 1 - y)]
    return [pltpu.make_async_remote_copy(
        src_ref=refs[0].at[4 * chip[0] + 2 * chip[1] + c], dst_ref=refs[0].at[4 * chip[0] + 2 * chip[1] + c],
        send_sem=send_sems.at[k], recv_sem=recv_sems.at[k],
        device_id=(x, y, 1 - c), device_id_type=MESH) for k, chip in enumerate(chips)]


def _first_axis_chip(x, y, c):
    return (x + 1 - c) % 2, (y + c) % 2


def _second_axis_chip(x, y, c):
    return (x + c) % 2, (y + 1 - c) % 2


def _first_targets(x, y, c):
    return [(x, y, 1 - c), (*_first_axis_chip(x, y, c), c)]


def _all_gather_small(shard, name, dep=None):
    def body(in_ref, *refs):
        out_ref, send_sems, recv_sems, local_sem = refs[-4:]
        x, y, c = _place()
        me, sibling = 4 * x + 2 * y + c, (x, y, 1 - c)
        first, second = _first_axis_chip(x, y, c), _second_axis_chip(x, y, c)

        def pair(chip):
            return out_ref.at[pl.ds(2 * (2 * chip[0] + chip[1]), 2)]

        def exchange(k, src, dst, to):
            cp = pltpu.make_async_remote_copy(src_ref=src, dst_ref=dst, send_sem=send_sems.at[k],
                                              recv_sem=recv_sems.at[k], device_id=to, device_id_type=MESH)
            cp.start()
            cp.wait()

        own = pltpu.make_async_copy(in_ref, out_ref.at[me], local_sem)
        own.start()
        exchange(0, in_ref, out_ref.at[me], sibling)
        own.wait()
        exchange(1, pair((x, y)), pair((x, y)), (*second, c))
        exchange(2, pair(second), pair(second), sibling)
        exchange(3, pair(first), pair(first), (*second, c))

    spec = pl.BlockSpec(memory_space=pltpu.VMEM)
    deps = [] if dep is None else [dep]
    return pl.pallas_call(
        body, name=name, out_shape=jax.ShapeDtypeStruct((N_DEV,) + shard.shape, shard.dtype),
        in_specs=[spec] * (1 + len(deps)), out_specs=spec,
        scratch_shapes=[pltpu.SemaphoreType.DMA((4,)), pltpu.SemaphoreType.DMA((4,)), pltpu.SemaphoreType.DMA],
        compiler_params=_params(),
    )(shard, *deps)


def _slot_copies(refs, send_sems, recv_sems, plan):
    copies = []
    for k, ((px, py, pc), to) in enumerate(plan):
        blk = refs[0].at[4 * px + 2 * py + pc]
        copies.append(pltpu.make_async_remote_copy(
            src_ref=blk, dst_ref=blk, send_sem=send_sems.at[k], recv_sem=recv_sems.at[k],
            device_id=to, device_id_type=MESH))
    return copies


def _second_axis_stage_copies(refs, send_sems, recv_sems):
    x, y, c = _place()
    first, second = (*_first_axis_chip(x, y, c), c), (*_second_axis_chip(x, y, c), c)
    return _slot_copies(refs, send_sems, recv_sems, [((x, y, c), second), (first, (x, y, 1 - c)), (first, second)])


def _second_axis_forward_copies(refs, send_sems, recv_sems):
    x, y, c = _place()
    return _slot_copies(refs, send_sems, recv_sems, [((*_second_axis_chip(x, y, c), c), (x, y, 1 - c))])


def _diagonal_forward_copies(refs, send_sems, recv_sems):
    x, y, c = _place()
    blk = refs[0].at[4 * (1 - x) + 2 * (1 - y) + c]
    return [pltpu.make_async_remote_copy(
        src_ref=blk, dst_ref=blk, send_sem=send_sems.at[0], recv_sem=recv_sems.at[0],
        device_id=(x, y, 1 - c), device_id_type=MESH)]


def _with_own_slot(block, me):
    return lax.dynamic_update_index_in_dim(lax.empty((N_DEV,) + block.shape, block.dtype), block, me, 0)


def _matmul(a, b, dims, out_dtype, tm, tn, name, dep=None):
    if dims == "nn":
        (m, k), n = a.shape, b.shape[1]
        a_spec = pl.BlockSpec((tm, k), lambda i, j: (i, 0))
        b_spec = pl.BlockSpec((k, tn), lambda i, j: (0, j))
        contract = ((1,), (0,))
    elif dims == "nt":
        (m, k), n = a.shape, b.shape[0]
        a_spec = pl.BlockSpec((tm, k), lambda i, j: (i, 0))
        b_spec = pl.BlockSpec((tn, k), lambda i, j: (j, 0))
        contract = ((1,), (1,))
    else:
        (k, m), n = a.shape, b.shape[1]
        a_spec = pl.BlockSpec((k, tm), lambda i, j: (0, i))
        b_spec = pl.BlockSpec((k, tn), lambda i, j: (0, j))
        contract = ((0,), (0,))
    assert m % tm == 0 and n % tn == 0 and a.dtype == BF16 and b.dtype == BF16

    def body(a_ref, b_ref, *rest):
        rest[-1][...] = lax.dot_general(a_ref[...], b_ref[...], (contract, ((), ())),
                                        preferred_element_type=F32).astype(out_dtype)

    deps = [] if dep is None else [dep]
    return pl.pallas_call(
        body, name=name, grid=(m // tm, n // tn),
        in_specs=[a_spec, b_spec] + [pl.BlockSpec((8, 128), lambda i, j: (0, 0))] * len(deps),
        out_specs=pl.BlockSpec((tm, tn), lambda i, j: (i, j)),
        out_shape=jax.ShapeDtypeStruct((m, n), out_dtype),
        compiler_params=_params(dimension_semantics=("arbitrary", "arbitrary")),
    )(a, b, *deps)


Z_TILE = 768
_Z_TILE_ORDER = ((0, 1, 2, 3, 4, 5, 6), (2, 0, 1, 6, 3, 4, 5), (4, 0, 5, 6, 1, 2, 3), (6, 2, 3, 4, 0, 1, 5))
_Z_EARLY_TILES = 4


def _z_proj(h, w_in_t, chip, first, count, z_prev, name, tr=1024):
    t = h.shape[0]

    def body(chip_ref, h_ref, w_ref, z_prev_ref, z_ref):
        z_ref[...] = _dot_nt(h_ref[...], w_ref[...])

    def tile(j, chip_ref):
        picked = 0
        for c, order in enumerate(_Z_TILE_ORDER):
            for k in range(count):
                picked = picked + jnp.where((chip_ref[0] == c) & (j == k), order[first + k], 0)
        return picked

    return pl.pallas_call(
        body, name=name,
        grid_spec=pltpu.PrefetchScalarGridSpec(
            num_scalar_prefetch=1, grid=(count, t // tr),
            in_specs=[pl.BlockSpec((tr, D_MODEL), lambda j, i, o: (i, 0)),
                      pl.BlockSpec((Z_TILE, D_MODEL), lambda j, i, o: (tile(j, o), 0)),
                      pl.BlockSpec(memory_space=pl.ANY)],
            out_specs=pl.BlockSpec((tr, Z_TILE), lambda j, i, o: (i, tile(j, o)))),
        out_shape=jax.ShapeDtypeStruct((t, D_IN), F32),
        input_output_aliases={3: 0},
        compiler_params=_params(dimension_semantics=("arbitrary", "arbitrary")),
    )(chip, h, w_in_t, z_prev)


def _modulation(device, c_all, w_ada, b_ada):
    def body(device_ref, c_ref, w_ref, b_ref, act_ref, mod_ref):
        cv = c_ref[...]
        act = cv * _sigmoid(cv)
        act_ref[...] = act
        mod_ref[...] = jnp.dot(act.astype(BF16), w_ref[...].astype(BF16), preferred_element_type=F32) + b_ref[...]

    whole = lambda a: pl.BlockSpec(a.shape, lambda i, device_ref: (0,) * a.ndim)
    return pl.pallas_call(
        body, name="modulation",
        grid_spec=pltpu.PrefetchScalarGridSpec(
            num_scalar_prefetch=1, grid=(1,),
            in_specs=[whole(c_all), whole(w_ada), pl.BlockSpec((1, W_ADA_SHARD), lambda i, device_ref: (0, device_ref[0]))],
            out_specs=(whole(c_all), pl.BlockSpec((N_DEV, W_ADA_SHARD), lambda i, device_ref: (0, 0)))),
        out_shape=(jax.ShapeDtypeStruct(c_all.shape, F32), jax.ShapeDtypeStruct((N_DEV, W_ADA_SHARD), F32)),
        compiler_params=_params(dimension_semantics=("arbitrary",)),
    )(device, c_all, w_ada, b_ada)


MOD_SHIFT, MOD_SCALE, MOD_GATE = 0, 1, 2


def _mod_spec(part, d):
    return pl.BlockSpec((1, d), lambda i: (0, part))


def _norm_z_proj_own(x, norm_g, mod, w_in_t, chip, tm=512):
    t, d = x.shape

    def body(chip_ref, x_ref, g_ref, sc_ref, sh_ref, w_ref, h_ref, z_ref):
        xv = x_ref[...]
        r = lax.rsqrt(jnp.mean(xv * xv, axis=-1, keepdims=True) + EPS)
        h = ((xv * r) * g_ref[...] * (1.0 + sc_ref[...]) + sh_ref[...]).astype(BF16)
        h_ref[...] = h
        z_ref[...] = _dot_nt(h, w_ref[...])

    def own_tile(chip_ref):
        picked = 0
        for c, order in enumerate(_Z_TILE_ORDER):
            picked = picked + jnp.where(chip_ref[0] == c, order[0], 0)
        return picked

    def row(part):
        return pl.BlockSpec((1, d), lambda i, o: (0, part))

    return pl.pallas_call(
        body, name="norm_z_proj_own",
        grid_spec=pltpu.PrefetchScalarGridSpec(
            num_scalar_prefetch=1, grid=(t // tm,),
            in_specs=[pl.BlockSpec((tm, d), lambda i, o: (i, 0)), row(0), row(MOD_SCALE), row(MOD_SHIFT),
                      pl.BlockSpec((Z_TILE, d), lambda i, o: (own_tile(o), 0))],
            out_specs=(pl.BlockSpec((tm, d), lambda i, o: (i, 0)),
                       pl.BlockSpec((tm, Z_TILE), lambda i, o: (i, own_tile(o))))),
        out_shape=(jax.ShapeDtypeStruct((t, d), BF16), jax.ShapeDtypeStruct((t, D_IN), F32)),
        compiler_params=_params(dimension_semantics=("arbitrary",)),
    )(chip, x, norm_g, mod, mod, w_in_t)


def _window_bias(block_index):
    s = lax.broadcasted_iota(jnp.int32, (2 * BLOCK, BLOCK), 0)
    t = lax.broadcasted_iota(jnp.int32, (2 * BLOCK, BLOCK), 1)
    valid = ((s < BLOCK) & (s > t) & (block_index > 0)) | ((s >= BLOCK) & ((s - BLOCK) <= t))
    bias = jnp.where(valid, 0.0, -jnp.inf).astype(F32)
    return jnp.concatenate([bias] * 8, axis=1)


def _heads_t(pair_blocks, g):
    top = lax.broadcasted_iota(jnp.int32, (BLOCK, BLOCK), 0) < HEAD_DIM
    zeros = jnp.zeros((HEAD_DIM, BLOCK), F32)
    tiles = []
    for blk in pair_blocks:
        tp = blk.T
        if g == 0:
            tiles += [jnp.where(top, tp, 0.0), jnp.concatenate([tp[HEAD_DIM:], zeros], axis=0)]
        else:
            tiles += [jnp.concatenate([zeros, tp[:HEAD_DIM]], axis=0), jnp.where(top, 0.0, tp)]
    return jnp.concatenate(tiles, axis=1)


def _pair_block(xt, p, g):
    r0 = HEAD_DIM * g
    even = xt[r0:r0 + HEAD_DIM, (2 * p) * BLOCK:(2 * p + 1) * BLOCK]
    odd = xt[r0:r0 + HEAD_DIM, (2 * p + 1) * BLOCK:(2 * p + 2) * BLOCK]
    return jnp.concatenate([even, odd], axis=0).T


def _softmax_t(scores_t, bias, sink):
    st = scores_t + bias
    m = jnp.maximum(jnp.max(st, axis=0, keepdims=True), sink)
    e = jnp.exp(st - m)
    es = jnp.exp(sink - m)
    inv = 1.0 / (jnp.sum(e, axis=0, keepdims=True) + es)
    return e * inv, es * inv


def _dot(a, b):
    return jnp.dot(a, b, preferred_element_type=F32)


def _dot_nt(a, b):
    return lax.dot_general(a, b, (((1,), (1,)), ((), ())), preferred_element_type=F32)


def _layer_norm_fwd(v):
    mu = jnp.mean(v, axis=-1, keepdims=True)
    xc = v - mu
    rstd = lax.rsqrt(jnp.mean(xc * xc, axis=-1, keepdims=True) + EPS)
    return xc * rstd, rstd


def _tril(transposed=False):
    t = lax.broadcasted_iota(jnp.int32, (BLOCK, BLOCK), 0)
    s = lax.broadcasted_iota(jnp.int32, (BLOCK, BLOCK), 1)
    return s >= t if transposed else t >= s


def _const_spec(shape):
    return pl.BlockSpec(shape, lambda i: (0,) * len(shape))


def _keys_values(z_ref, kvp):
    kvc = z_ref[:, SEG_KV:SEG_KV + 2 * D_KV]
    kk = jnp.concatenate([kvp[:, :D_KV], kvc[:, :D_KV]], axis=0)
    vv = jnp.concatenate([kvp[:, D_KV:], kvc[:, D_KV:]], axis=0)
    return kk, vv


MIXER_BLOCKS = 2


class _Rows:
    def __init__(self, ref, sub):
        self.ref, self.rows = ref, slice(sub * BLOCK, (sub + 1) * BLOCK)

    def __getitem__(self, idx):
        return self.ref[self.rows, idx[1]]

    def __setitem__(self, idx, value):
        self.ref[self.rows, idx[1]] = value


def _kv_before_spec(index):
    return pl.BlockSpec((BLOCK, 2 * D_KV),
                        lambda i: (jnp.maximum(MIXER_BLOCKS * index(i) - 1, 0), SEG_KV // (2 * D_KV)))


def _pair_cols(g, p, base=0):
    return slice(base + (4 * g + p) * 128, base + (4 * g + p + 1) * 128)


def _mixer_fwd(z, sink_rows, ln_g, ln_b, sgu_w, sgu_bt):
    t = z.shape[0]

    def body(z_all, kvp_ref, sink_ref, lng_ref, lnb_ref, w_ref, bt_ref, a_all, prob_ref, sink_prob_ref):
        kv_before = kvp_ref[...]
        for sub in range(MIXER_BLOCKS):
            z_ref, a_ref = _Rows(z_all, sub), _Rows(a_all, sub)
            one_block(z_ref, kv_before, MIXER_BLOCKS * pl.program_id(0) + sub, sink_ref, lng_ref, lnb_ref, w_ref,
                      bt_ref, a_ref, prob_ref.at[sub], sink_prob_ref.at[sub])
            kv_before = z_ref[:, SEG_KV:SEG_KV + 2 * D_KV]

    def one_block(z_ref, kv_before, block_index, sink_ref, lng_ref, lnb_ref, w_ref, bt_ref, a_ref, prob_ref,
                  sink_prob_ref):
        bias = _window_bias(block_index)
        kk, vv = _keys_values(z_ref, kv_before)
        kk_b, vvt_b = kk.astype(BF16), vv.T.astype(BF16)
        for g in range(2):
            qt = _heads_t([z_ref[:, _pair_cols(g, p, SEG_Q)] * ATTN_SCALE for p in range(4)], g).astype(BF16)
            prob, sink_prob = _softmax_t(_dot(kk_b, qt), bias, sink_ref[g])
            prob_b = prob.astype(BF16)
            prob_ref[g] = prob_b
            sink_prob_ref[g] = sink_prob
            ot = _dot(vvt_b, prob_b)
            for p in range(4):
                gate = z_ref[:, _pair_cols(g, p, SEG_GA)]
                a_ref[:, _pair_cols(g, p)] = (_pair_block(ot, p, g) * (gate * _sigmoid(gate))).astype(BF16)

        vhat, _ = _layer_norm_fwd(z_ref[:, SEG_VS:SEG_VS + D_SGU])
        vn = vhat * lng_ref[...] + lnb_ref[...]
        tril = _tril()
        for g in range(SGU_GROUPS):
            cols = slice(g * 128, (g + 1) * 128)
            wm = jnp.where(tril, w_ref[g], 0.0).astype(BF16)
            mixed = _dot(wm, vn[:, cols].astype(BF16)) + bt_ref[:, g:g + 1]
            gate = z_ref[:, SEG_GS + g * 128:SEG_GS + (g + 1) * 128]
            a_ref[:, D_ATTN + g * 128:D_ATTN + (g + 1) * 128] = (
                (z_ref[:, SEG_U + g * 128:SEG_U + (g + 1) * 128] * mixed) * (gate * _sigmoid(gate))).astype(BF16)

    rows = MIXER_BLOCKS * BLOCK
    return pl.pallas_call(
        body, name="mixer_fwd", grid=(t // rows,),
        in_specs=[pl.BlockSpec((rows, D_IN), lambda i: (i, 0)), _kv_before_spec(lambda i: i),
                  _const_spec((2, 1, 8 * BLOCK)), _const_spec((1, D_SGU)), _const_spec((1, D_SGU)),
                  _const_spec((SGU_GROUPS, BLOCK, BLOCK)), _const_spec((BLOCK, SGU_GROUPS))],
        out_specs=(pl.BlockSpec((rows, D_MODEL), lambda i: (i, 0)),
                   pl.BlockSpec((MIXER_BLOCKS, 2, 2 * BLOCK, 8 * BLOCK), lambda i: (i, 0, 0, 0)),
                   pl.BlockSpec((MIXER_BLOCKS, 2, 1, 8 * BLOCK), lambda i: (i, 0, 0, 0))),
        out_shape=(jax.ShapeDtypeStruct((t, D_MODEL), BF16),
                   jax.ShapeDtypeStruct((t // BLOCK, 2, 2 * BLOCK, 8 * BLOCK), BF16),
                   jax.ShapeDtypeStruct((t // BLOCK, 2, 1, 8 * BLOCK), F32)),
        compiler_params=_params(dimension_semantics=("arbitrary",)),
    )(z, z, sink_rows, ln_g, ln_b, sgu_w, sgu_bt)


def _mixer_bwd(z, da, probs, sink_probs, ln_g, ln_b, sgu_w, sgu_wt, sgu_bt):
    t = z.shape[0]

    def body(z_all, kvp_ref, da_all, prob_ref, sink_prob_ref, lng_ref, lnb_ref, w_ref, wt_ref, bt_ref,
             dz_all, dsink_ref, dw_ref, db_ref, dlng_ref, dlnb_ref, carry_ref, dsink_acc, dbt_acc):
        step = pl.program_id(0)

        @pl.when(step == 0)
        def _():
            carry_ref[...] = jnp.zeros_like(carry_ref)
            dsink_acc[...] = jnp.zeros_like(dsink_acc)
            dbt_acc[...] = jnp.zeros_like(dbt_acc)
            dw_ref[...] = jnp.zeros_like(dw_ref)
            dlng_ref[...] = jnp.zeros_like(dlng_ref)
            dlnb_ref[...] = jnp.zeros_like(dlnb_ref)

        carry = carry_ref[...]
        for sub in reversed(range(MIXER_BLOCKS)):
            kv_before = kvp_ref[...] if sub == 0 else _Rows(z_all, sub - 1)[:, SEG_KV:SEG_KV + 2 * D_KV]
            carry = one_block(_Rows(z_all, sub), kv_before, _Rows(da_all, sub), prob_ref.at[sub], sink_prob_ref.at[sub],
                              carry, lng_ref, lnb_ref, w_ref, wt_ref, bt_ref, _Rows(dz_all, sub),
                              dw_ref, dlng_ref, dlnb_ref, dsink_acc, dbt_acc)
        carry_ref[...] = carry

        @pl.when(step == ns - 1)
        def _():
            db_ref[...] = dbt_acc[...].T[:SGU_GROUPS]
            lane_row = lax.broadcasted_iota(jnp.int32, (1, 128), 1)
            d_sink = jnp.zeros((1, 128), F32)
            for g in range(2):
                acc = dsink_acc[g]
                for j in range(8):
                    head_sum = jnp.sum(acc[:, j * BLOCK:(j + 1) * BLOCK], axis=-1, keepdims=True)
                    d_sink = d_sink + jnp.where(lane_row == 8 * g + j, head_sum, 0.0)
            dsink_ref[...] = d_sink

    def one_block(z_ref, kv_before, da_ref, prob_ref, sink_prob_ref, carry, lng_ref, lnb_ref, w_ref, wt_ref, bt_ref,
                  dz_ref, dw_ref, dlng_ref, dlnb_ref, dsink_acc, dbt_acc):
        kk, vv = _keys_values(z_ref, kv_before)
        vv_b = vv.astype(BF16)
        kkt_b, vvt_b = kk.T.astype(BF16), vv.T.astype(BF16)
        dkk = jnp.zeros((2 * BLOCK, D_KV), F32)
        dvv = jnp.zeros((2 * BLOCK, D_KV), F32)
        for g in range(2):
            qt = _heads_t([z_ref[:, _pair_cols(g, p, SEG_Q)] * ATTN_SCALE for p in range(4)], g).astype(BF16)
            prob_b, sink_prob = prob_ref[g], sink_prob_ref[g]
            prob = prob_b.astype(F32)
            ot = _dot(vvt_b, prob_b)
            gates = [z_ref[:, _pair_cols(g, p, SEG_GA)] for p in range(4)]
            sig = [_sigmoid(gt) for gt in gates]
            d_attn = [da_ref[:, _pair_cols(g, p)] for p in range(4)]
            d_ot = _heads_t([d_attn[p] * (gates[p] * sig[p]) for p in range(4)], g).astype(BF16)
            d_prob = _dot(vv_b, d_ot)
            delta = jnp.sum(prob * d_prob, axis=0, keepdims=True)
            d_scores = (prob * (d_prob - delta)).astype(BF16)
            dsink_acc[g] -= sink_prob * delta
            d_qt = _dot(kkt_b, d_scores)
            dkk = dkk + _dot_nt(d_scores, qt)
            dvv = dvv + _dot_nt(prob_b, d_ot)
            for p in range(4):
                dz_ref[:, _pair_cols(g, p, SEG_Q)] = (_pair_block(d_qt, p, g) * ATTN_SCALE).astype(BF16)
                d_silu = sig[p] * (1.0 + gates[p] * (1.0 - sig[p]))
                dz_ref[:, _pair_cols(g, p, SEG_GA)] = (d_attn[p] * _pair_block(ot, p, g) * d_silu).astype(BF16)
        d_kv = jnp.concatenate([dkk, dvv], axis=1)
        dz_ref[:, SEG_KV:SEG_KV + 2 * D_KV] = (d_kv[BLOCK:] + carry).astype(BF16)

        vhat, rstd = _layer_norm_fwd(z_ref[:, SEG_VS:SEG_VS + D_SGU])
        lng = lng_ref[...]
        vn = vhat * lng + lnb_ref[...]
        tril, triu = _tril(), _tril(transposed=True)
        lane = lax.broadcasted_iota(jnp.int32, (BLOCK, 128), 1)
        d_bt = jnp.zeros((BLOCK, 128), F32)
        d_vn = []
        for g in range(SGU_GROUPS):
            cols = slice(g * 128, (g + 1) * 128)
            wm = jnp.where(tril, w_ref[g], 0.0).astype(BF16)
            wmt = jnp.where(triu, wt_ref[g], 0.0).astype(BF16)
            vn_g = vn[:, cols].astype(BF16)
            mixed = _dot(wm, vn_g) + bt_ref[:, g:g + 1]
            gate = z_ref[:, SEG_GS + g * 128:SEG_GS + (g + 1) * 128]
            u = z_ref[:, SEG_U + g * 128:SEG_U + (g + 1) * 128]
            d_out = da_ref[:, D_ATTN + g * 128:D_ATTN + (g + 1) * 128]
            sg = _sigmoid(gate)
            d_um = d_out * (gate * sg)
            dz_ref[:, SEG_U + g * 128:SEG_U + (g + 1) * 128] = (d_um * mixed).astype(BF16)
            dz_ref[:, SEG_GS + g * 128:SEG_GS + (g + 1) * 128] = (
                d_out * (u * mixed) * (sg * (1.0 + gate * (1.0 - sg)))).astype(BF16)
            d_mixed = d_um * u
            d_mixed_b = d_mixed.astype(BF16)
            dw_ref[g] += jnp.where(tril, _dot_nt(d_mixed_b, vn_g), 0.0)
            d_bt = d_bt + jnp.where(lane == g, jnp.sum(d_mixed, axis=-1, keepdims=True), 0.0)
            d_vn.append(_dot(wmt, d_mixed_b))
        dbt_acc[...] += d_bt
        d_vn = jnp.concatenate(d_vn, axis=1)
        dlng_ref[...] += jnp.sum(d_vn * vhat, axis=0, keepdims=True)
        dlnb_ref[...] += jnp.sum(d_vn, axis=0, keepdims=True)
        d_vhat = d_vn * lng
        d_v = rstd * (d_vhat - jnp.mean(d_vhat, axis=-1, keepdims=True)
                      - vhat * jnp.mean(d_vhat * vhat, axis=-1, keepdims=True))
        dz_ref[:, SEG_VS:SEG_VS + D_SGU] = d_v.astype(BF16)
        return d_kv[:BLOCK]

    rows = MIXER_BLOCKS * BLOCK
    ns = t // rows
    rev = lambda i: ns - 1 - i
    return pl.pallas_call(
        body, name="mixer_bwd", grid=(ns,),
        in_specs=[pl.BlockSpec((rows, D_IN), lambda i: (rev(i), 0)), _kv_before_spec(rev),
                  pl.BlockSpec((rows, D_MODEL), lambda i: (rev(i), 0)),
                  pl.BlockSpec((MIXER_BLOCKS, 2, 2 * BLOCK, 8 * BLOCK), lambda i: (rev(i), 0, 0, 0)),
                  pl.BlockSpec((MIXER_BLOCKS, 2, 1, 8 * BLOCK), lambda i: (rev(i), 0, 0, 0)),
                  _const_spec((1, D_SGU)), _const_spec((1, D_SGU)),
                  _const_spec((SGU_GROUPS, BLOCK, BLOCK)), _const_spec((SGU_GROUPS, BLOCK, BLOCK)),
                  _const_spec((BLOCK, SGU_GROUPS))],
        out_specs=(pl.BlockSpec((rows, D_IN), lambda i: (rev(i), 0)), _const_spec((1, 128)),
                   _const_spec((SGU_GROUPS, BLOCK, BLOCK)), _const_spec((SGU_GROUPS, BLOCK)),
                   _const_spec((1, D_SGU)), _const_spec((1, D_SGU))),
        out_shape=(jax.ShapeDtypeStruct((t, D_IN), BF16), jax.ShapeDtypeStruct((1, 128), F32),
                   jax.ShapeDtypeStruct((SGU_GROUPS, BLOCK, BLOCK), F32), jax.ShapeDtypeStruct((SGU_GROUPS, BLOCK), F32),
                   jax.ShapeDtypeStruct((1, D_SGU), F32), jax.ShapeDtypeStruct((1, D_SGU), F32)),
        scratch_shapes=[pltpu.VMEM((BLOCK, 2 * D_KV), F32), pltpu.VMEM((2, 1, 8 * BLOCK), F32),
                        pltpu.VMEM((BLOCK, 128), F32)],
        compiler_params=_params(dimension_semantics=("arbitrary",)),
    )(z, z, da, probs, sink_probs, ln_g, ln_b, sgu_w, sgu_wt, sgu_bt)


def _out_proj_head(a, w_out_full, x, target, mod, final_g, tm=256):
    t, d = x.shape

    def body(a_ref, w_ref, x_ref, tg_ref, gate_ref, fg_ref, dx2_ref, dy_ref, loss_ref, dfg_ref, dgate_ref):
        @pl.when(pl.program_id(0) == 0)
        def _():
            loss_ref[...] = jnp.zeros_like(loss_ref)
            dfg_ref[...] = jnp.zeros_like(dfg_ref)
            dgate_ref[...] = jnp.zeros_like(dgate_ref)

        yv, gate, fg = _dot(a_ref[...], w_ref[...]), gate_ref[...], fg_ref[...]
        x2 = x_ref[...] + gate * yv
        r2 = lax.rsqrt(jnp.mean(x2 * x2, axis=-1, keepdims=True) + EPS)
        nrm = x2 * r2
        err = nrm * fg - tg_ref[...]
        loss_ref[...] += 0.5 * jnp.sum(jnp.mean(err * err, axis=-1, keepdims=True), axis=0, keepdims=True)
        fg_d = fg * (1.0 / d)
        err_nrm = err * nrm
        dfg_ref[...] += jnp.sum(err_nrm, axis=0, keepdims=True) * (1.0 / d)
        d_nrm = err * fg_d
        dx2 = r2 * (d_nrm - nrm * jnp.mean(err_nrm * fg_d, axis=-1, keepdims=True))
        dx2_ref[...] = dx2
        dgate_ref[...] += jnp.sum(dx2 * yv, axis=0, keepdims=True)
        dy_ref[...] = (dx2 * gate).astype(BF16)

    blk = pl.BlockSpec((tm, d), lambda i: (i, 0))
    row = _const_spec((1, d))
    whole = pl.BlockSpec(w_out_full.shape, lambda i: (0, 0), pipeline_mode=pl.Buffered(1))
    return pl.pallas_call(
        body, name="out_proj_head", grid=(t // tm,),
        in_specs=[pl.BlockSpec((tm, a.shape[1]), lambda i: (i, 0)), whole, blk, blk, _mod_spec(MOD_GATE, d), row],
        out_specs=(blk, blk, _const_spec((1, 128)), row, row),
        out_shape=(jax.ShapeDtypeStruct((t, d), F32), jax.ShapeDtypeStruct((t, d), BF16),
                   jax.ShapeDtypeStruct((1, 128), F32), jax.ShapeDtypeStruct((1, d), F32),
                   jax.ShapeDtypeStruct((1, d), F32)),
        compiler_params=_params(dimension_semantics=("arbitrary",)),
    )(a, w_out_full, x, target, mod, final_g)


def _z_proj_bwd_norm(dz, w_in_t, x, dx2, norm_g, mod, dep, tm=256):
    t, d = x.shape

    def body(dz_ref, w_ref, x_ref, dx2_ref, g_ref, sc_ref, dep_ref, gx_ref, dshift_ref, dscale_ref, dg_ref):
        @pl.when(pl.program_id(0) == 0)
        def _():
            dshift_ref[...] = jnp.zeros_like(dshift_ref)
            dscale_ref[...] = jnp.zeros_like(dscale_ref)
            dg_ref[...] = jnp.zeros_like(dg_ref)

        dh, xv, g = _dot(dz_ref[...], w_ref[...]), x_ref[...], g_ref[...]
        one_plus = 1.0 + sc_ref[...]
        r = lax.rsqrt(jnp.mean(xv * xv, axis=-1, keepdims=True) + EPS)
        xn = xv * r
        gain = one_plus * g
        dh_xn = dh * xn
        dh_xn_sum = jnp.sum(dh_xn, axis=0, keepdims=True)
        dshift_ref[...] += jnp.sum(dh, axis=0, keepdims=True)
        dscale_ref[...] += dh_xn_sum * g
        dg_ref[...] += dh_xn_sum * one_plus
        d_xn = dh * gain
        gx_ref[...] = dx2_ref[...] + r * (d_xn - xn * jnp.mean(dh_xn * gain, axis=-1, keepdims=True))

    blk = pl.BlockSpec((tm, d), lambda i: (i, 0))
    row = _const_spec((1, d))
    whole = pl.BlockSpec(w_in_t.shape, lambda i: (0, 0), pipeline_mode=pl.Buffered(1))
    return pl.pallas_call(
        body, name="z_proj_bwd_norm", grid=(t // tm,),
        in_specs=[pl.BlockSpec((tm, dz.shape[1]), lambda i: (i, 0)), whole, blk, blk, row, _mod_spec(MOD_SCALE, d),
                  _const_spec((8, 128))],
        out_specs=(blk, row, row, row),
        out_shape=(jax.ShapeDtypeStruct((t, d), F32),) + (jax.ShapeDtypeStruct((1, d), F32),) * 3,
        compiler_params=_params(dimension_semantics=("arbitrary",)),
    )(dz, w_in_t, x, dx2, norm_g, mod, dep)


def _adamw(w, g, m, v):
    m = ADAM_B1 * m + (1.0 - ADAM_B1) * g
    v = ADAM_B2 * v + (1.0 - ADAM_B2) * (g * g)
    m_hat = m / (1.0 - ADAM_B1 ** ADAM_STEP)
    v_hat = v / (1.0 - ADAM_B2 ** ADAM_STEP)
    delta = -ADAM_LR * (m_hat / (jnp.sqrt(v_hat) + ADAM_EPS) + ADAM_WD * w)
    return delta, m, v


def _relay_sum(device, blocks, land_pair, land_first, tr):
    _, r, c = blocks.shape

    def body(device_ref, a_ref, b_ref, c_ref, o_ref):
        o_ref[...] = (a_ref[...].astype(F32) + b_ref[...].astype(F32) + c_ref[...].astype(F32)).astype(BF16)

    second = pl.BlockSpec((None, tr, c), lambda i, device_ref: (1, i, 0))
    return pl.pallas_call(
        body, name="w_in_grad_relay_sum",
        grid_spec=pltpu.PrefetchScalarGridSpec(
            num_scalar_prefetch=1, grid=(r // tr,),
            in_specs=[pl.BlockSpec((None, tr, c), lambda i, device_ref: (device_ref[0], i, 0)), second, second],
            out_specs=pl.BlockSpec((tr, c), lambda i, device_ref: (i, 0))),
        out_shape=jax.ShapeDtypeStruct((r, c), BF16),
        compiler_params=_params(dimension_semantics=("arbitrary",)),
    )(device, blocks, land_pair, land_first)


def _adam_from_chips(chip, pair, landed, w, m, v, name, tc):
    _, r, c = pair.shape
    n = len(landed)

    def body(chip_ref, own_ref, *refs):
        w_ref, m_ref, v_ref, g_ref, d_ref, nm_ref, nv_ref = refs[n:]
        g = own_ref[...].astype(F32)
        for k in range(n):
            g = g + refs[k][...].astype(F32)
        g_ref[...] = g
        d_ref[...], nm_ref[...], nv_ref[...] = _adamw(w_ref[...], g, m_ref[...], v_ref[...])

    def landed_spec(index):
        return pl.BlockSpec((None, r, tc), lambda i, chip_ref: (index, 0, i))

    blk = pl.BlockSpec((r, tc), lambda i, chip_ref: (0, i))
    return pl.pallas_call(
        body, name=name,
        grid_spec=pltpu.PrefetchScalarGridSpec(
            num_scalar_prefetch=1, grid=(c // tc,),
            in_specs=[pl.BlockSpec((None, r, tc), lambda i, chip_ref: (chip_ref[0], 0, i))]
            + [landed_spec(index) for _, index in landed] + [blk, blk, blk],
            out_specs=(blk,) * 4),
        out_shape=(jax.ShapeDtypeStruct((r, c), F32),) * 4,
        compiler_params=_params(dimension_semantics=("arbitrary",)),
    )(chip, pair, *[array for array, _ in landed], w, m, v)


def _adam_w_ada(device, act_t, dmod_all, w, m, v, tr=512):
    r, c = w.shape

    def body(device_ref, a_ref, dm_ref, w_ref, m_ref, v_ref, g_ref, d_ref, nm_ref, nv_ref):
        g = _dot(a_ref[...].astype(BF16), dm_ref[...].astype(BF16))
        g_ref[...] = g
        d_ref[...], nm_ref[...], nv_ref[...] = _adamw(w_ref[...], g, m_ref[...], v_ref[...])

    blk = pl.BlockSpec((tr, c), lambda i, device_ref: (i, 0))
    return pl.pallas_call(
        body, name="adam_w_ada",
        grid_spec=pltpu.PrefetchScalarGridSpec(
            num_scalar_prefetch=1, grid=(r // tr,),
            in_specs=[pl.BlockSpec((tr, N_DEV), lambda i, device_ref: (i, 0)),
                      pl.BlockSpec((N_DEV, c), lambda i, device_ref: (0, device_ref[0])), blk, blk, blk],
            out_specs=(blk,) * 4),
        out_shape=(jax.ShapeDtypeStruct((r, c), F32),) * 4,
        compiler_params=_params(dimension_semantics=("arbitrary",)),
    )(device, act_t, dmod_all, w, m, v)


def _pack_small(d_shift, d_scale, d_gate, d_norm_g, d_final_g, d_ln_g, d_ln_b, loss, d_sinks, d_sgu_b):
    def body(shift_ref, scale_ref, gate_ref, ng_ref, fg_ref, lng_ref, lnb_ref, loss_ref, sink_ref, b_ref, o_ref):
        o_ref[...] = jnp.zeros_like(o_ref)
        o_ref[ROW_SHIFT:ROW_SHIFT + 1, :] = shift_ref[...]
        o_ref[ROW_SCALE:ROW_SCALE + 1, :] = scale_ref[...]
        o_ref[ROW_GATE:ROW_GATE + 1, :] = gate_ref[...]
        o_ref[ROW_NORM_G:ROW_NORM_G + 1, :] = ng_ref[...]
        o_ref[ROW_FINAL_G:ROW_FINAL_G + 1, :] = fg_ref[...]
        o_ref[ROW_LN:ROW_LN + 1, 0:D_SGU] = lng_ref[...]
        o_ref[ROW_LN:ROW_LN + 1, D_SGU:2 * D_SGU] = lnb_ref[...]
        o_ref[ROW_MISC:ROW_MISC + 1, 0:128] = loss_ref[...]
        o_ref[ROW_MISC:ROW_MISC + 1, 128:256] = sink_ref[...]
        o_ref[ROW_SGU_B:ROW_SGU_B + SGU_GROUPS, 0:BLOCK] = b_ref[...]

    return pl.pallas_call(
        body, name="pack_small", out_shape=jax.ShapeDtypeStruct((SMALL_ROWS, D_MODEL), F32),
        compiler_params=_params(),
    )(d_shift, d_scale, d_gate, d_norm_g, d_final_g, d_ln_g, d_ln_b, loss, d_sinks, d_sgu_b)


_SMALL_NAMES = ("norm_g", "b_ada", "attn_sinks", "sgu_ln_g", "sgu_ln_b", "sgu_w", "sgu_b", "final_g")


def _adam_small(partials, d_sgu_w_all, weights, moments_m, moments_v):
    names = _SMALL_NAMES
    k = len(names)

    def body(*refs):
        p_ref, sw_ref = refs[0], refs[1]
        w_refs, m_refs, v_refs = refs[2:2 + k], refs[2 + k:2 + 2 * k], refs[2 + 2 * k:2 + 3 * k]
        loss_ref, dmod_ref = refs[2 + 3 * k], refs[3 + 3 * k]
        out_refs = refs[4 + 3 * k:4 + 7 * k]
        sum_ref = refs[4 + 7 * k]
        total = p_ref[0]
        for j in range(1, N_DEV):
            total = total + p_ref[j]
        sum_ref[...] = total
        for j in range(N_DEV):
            for part, row in enumerate((ROW_SHIFT, ROW_SCALE, ROW_GATE)):
                dmod_ref[j:j + 1, part * D_MODEL:(part + 1) * D_MODEL] = p_ref[j, row:row + 1, :]
        loss_ref[...] = sum_ref[ROW_MISC:ROW_MISC + 1, 0:1]
        d_sgu_w = sw_ref[0]
        for j in range(1, N_DEV):
            d_sgu_w = d_sgu_w + sw_ref[j]
        grads = {
            "norm_g": sum_ref[ROW_NORM_G:ROW_NORM_G + 1, :],
            "b_ada": jnp.concatenate([sum_ref[r:r + 1, :] for r in (ROW_SHIFT, ROW_SCALE, ROW_GATE)], axis=1),
            "attn_sinks": sum_ref[ROW_MISC:ROW_MISC + 1, 128:128 + N_Q_HEADS],
            "sgu_ln_g": sum_ref[ROW_LN:ROW_LN + 1, 0:D_SGU],
            "sgu_ln_b": sum_ref[ROW_LN:ROW_LN + 1, D_SGU:2 * D_SGU],
            "sgu_w": d_sgu_w[None],
            "sgu_b": sum_ref[ROW_SGU_B:ROW_SGU_B + SGU_GROUPS, 0:BLOCK][None],
            "final_g": sum_ref[ROW_FINAL_G:ROW_FINAL_G + 1, :],
        }
        for i, name in enumerate(names):
            g = grads[name]
            delta, m, v = _adamw(w_refs[i][...], g, m_refs[i][...], v_refs[i][...])
            out_refs[4 * i][...] = g
            out_refs[4 * i + 1][...] = delta
            out_refs[4 * i + 2][...] = m
            out_refs[4 * i + 3][...] = v

    shapes = [jax.ShapeDtypeStruct((1, 1), F32), jax.ShapeDtypeStruct((N_DEV, 3 * D_MODEL), F32)]
    for name in names:
        shapes += [jax.ShapeDtypeStruct(weights[name].shape, F32)] * 4
    outs = pl.pallas_call(
        body, name="adam_small", out_shape=tuple(shapes),
        scratch_shapes=[pltpu.VMEM((SMALL_ROWS, D_MODEL), F32)],
        compiler_params=_params(),
    )(partials, d_sgu_w_all, *[weights[n] for n in names], *[moments_m[n] for n in names],
      *[moments_v[n] for n in names])
    return outs[0], outs[1], {name: outs[2 + 4 * i:6 + 4 * i] for i, name in enumerate(names)}


def kernel(x, c, norm_g, w_ada, b_ada, w_in, attn_sinks, sgu_ln_g, sgu_ln_b, sgu_w, sgu_b, w_out, final_g, loss_target, m_norm_g, m_w_ada, m_b_ada, m_w_in, m_attn_sinks, m_sgu_ln_g, m_sgu_ln_b, m_sgu_w, m_sgu_b, m_w_out, m_final_g, v_norm_g, v_w_ada, v_b_ada, v_w_in, v_attn_sinks, v_sgu_ln_g, v_sgu_ln_b, v_sgu_w, v_sgu_b, v_w_out, v_final_g):
    xi, yi, ci = _place()
    me = 4 * xi + 2 * yi + ci
    x2d, target = x[0], loss_target[0]
    t = x2d.shape[0]

    core = ci.astype(jnp.int32).reshape(1)
    chip = (2 * xi + yi).astype(jnp.int32).reshape(1)

    first = _own_block_copies(_first_targets)
    first_flight = _start_copies([_with_own_slot(w_in[0].T.astype(BF16), me)], first, 2, core, "gather_w_in_start")

    c_all = _all_gather_small(c.reshape(8, 256), "gather_c", first_flight[3]).reshape(N_DEV, D_MODEL)
    device = me.astype(jnp.int32).reshape(1)
    c_act, mod_part = _modulation(device, c_all, w_ada[0], b_ada)
    mod_all = _all_gather_small(mod_part, "gather_mod")

    across = _wait_then_start(first_flight, lambda *a: first(*a)[1:], _second_axis_stage_copies, 3, mod_all,
                              "gather_w_in_second_axis_stage")
    mod = lax.dynamic_index_in_dim(mod_all, me, axis=1, keepdims=False).reshape(1, 3 * D_MODEL)
    mod = mod + across[3][0, 0]

    w_in_pair = _wait_copies((first_flight[0], first_flight[1], across[2], None), lambda *a: first(*a)[:1], mod,
                             "gather_w_in_sibling_wait")
    h, z_own = _norm_z_proj_own(x2d, norm_g, mod, w_in_pair[0].reshape(D_IN, D_MODEL), chip)
    w_out_early = _own_block_copies(lambda x, y, c: [(x, y, 1 - c), (*_second_axis_chip(x, y, c), c)])
    w_out_late = _own_block_copies(lambda x, y, c: [(*_first_axis_chip(x, y, c), c), (1 - x, 1 - y, c)])
    forward = _wait_then_start(
        (across[0], across[1], w_in_pair, None), lambda *a: _second_axis_stage_copies(*a)[:1],
        lambda refs, s, r: _second_axis_forward_copies(refs[:1], s, r) + _group(w_out_early, 1, 1, 1)(refs, s, r),
        3, z_own, "gather_w_in_second_axis_forward", more_bufs=[_with_own_slot(w_out[0].astype(BF16), me)])
    w_in_most = _wait_copies((across[0], across[1], forward[2][:1], None),
                             lambda *a: _second_axis_stage_copies(*a)[1:2], z_own, "gather_w_in_first_forward_wait")
    w_in_most = _wait_copies((forward[0], forward[1], w_in_most, None), _second_axis_forward_copies, z_own,
                             "gather_w_in_second_forward_wait")
    z_early = _z_proj(h, w_in_most[0].reshape(D_IN, D_MODEL), chip, 1, _Z_EARLY_TILES - 1, z_own, "z_proj_early")
    last = _wait_then_start(
        (across[0], across[1], [w_in_most[0], forward[2][1]], None), lambda *a: _second_axis_stage_copies(*a)[2:],
        lambda refs, s, r: _diagonal_forward_copies(refs[:1], s, r) + _group(w_out_late, 1, 1, 1)(refs, s, r),
        3, z_early, "gather_w_in_last_stage")
    w_in_all = _wait_copies((last[0], last[1], last[2][:1], None), _diagonal_forward_copies, z_early,
                            "gather_w_in_last_wait")[0]
    w_in_t = w_in_all.reshape(D_IN, D_MODEL)
    z = _z_proj(h, w_in_t, chip, _Z_EARLY_TILES, 7 - _Z_EARLY_TILES, z_early, "z_proj_late")
    w_out_half = _wait_copies((forward[0], forward[1], last[2][1:], None), _group(w_out_early, 0, 1, 1), z,
                              "gather_w_out_early_wait")
    w_out_flight = _wait_then_start((last[0], last[1], w_out_half, None), _group(w_out_late, 0, 1, 1),
                                    _forward_copies, 3, z, "gather_w_out_forward_stage")
    sink_rows = jnp.repeat(attn_sinks.reshape(N_Q_HEADS), BLOCK).reshape(2, 1, 8 * BLOCK)
    sgu_bt = sgu_b[0].T
    a, probs, sink_probs = _mixer_fwd(z, sink_rows + w_out_flight[3][0, 0], sgu_ln_g, sgu_ln_b, sgu_w[0], sgu_bt)
    w_out_all = _wait_copies(w_out_flight, _forward_copies, a, "gather_w_out_forward_wait")[0]
    w_out_full = w_out_all.reshape(D_MODEL, D_MODEL)
    final_g_row = final_g.reshape(1, D_MODEL)
    dx2, dy, loss_part, d_final_g, d_gate = _out_proj_head(a, w_out_full, x2d, target, mod, final_g_row)

    da = _matmul(dy, w_out_full, "nt", F32, min(t, 1024), 1024, "out_proj_bwd")
    dw_out = _matmul(a, dy, "tn", BF16, 1024, 1024, "w_out_grad").reshape(4, 2, W_OUT_SHARD, D_MODEL)
    pair_out = _pair_reduce(dw_out, _every_chip, "w_out_grad_pair_reduce", W_OUT_SHARD // 2)
    dz, d_sinks, d_sgu_w, d_sgu_b, d_ln_g, d_ln_b = _mixer_bwd(
        z, da, probs, sink_probs, sgu_ln_g, sgu_ln_b, sgu_w[0], jnp.swapaxes(sgu_w[0], 1, 2), sgu_bt)
    sgu_w_to_all = _group(_own_block_copies(_all_others), 2, 1, 3)
    both = _start_copies(
        [pair_out, lax.empty((3, W_OUT_SHARD, D_MODEL), BF16), _with_own_slot(d_sgu_w, me)],
        lambda refs, s, r: _chip_copies(refs[:2], s, r) + sgu_w_to_all(refs, s, r), 3 + N_DEV - 1, core,
        "w_out_grad_chip_and_sgu_w_gather_start")
    out_flight, sgu_w_flight = (both[0], both[1], both[2][:2], None), (both[0], both[1], both[2][2:], None)
    dw_in_t = _matmul(dz, h, "tn", BF16, 768, D_MODEL, "w_in_grad", dep=both[3])
    pair_in = _pair_reduce(dw_in_t.reshape(4, 2, W_IN_SHARD, D_MODEL), _first_hop_chips, "w_in_grad_pair_reduce",
                           W_IN_SHARD // 3)
    first_hop = lambda refs, s, r: _first_hop_copies(refs[:2], s, r) + _group(_late_pair_copies, 2, 2, 2)(refs, s, r)
    hop1 = _start_copies(
        [pair_in, lax.empty((2, W_IN_SHARD, D_MODEL), BF16), dw_in_t.reshape(N_DEV, W_IN_SHARD, D_MODEL),
         lax.empty((2, W_IN_SHARD, D_MODEL), BF16)], first_hop, 4, core, "w_in_grad_first_hop_start")
    grad_x, d_shift, d_scale, d_norm_g = _z_proj_bwd_norm(dz, w_in_t, x2d, dx2, norm_g, mod, hop1[3])

    partial = _pack_small(d_shift, d_scale, d_gate, d_norm_g, d_final_g, d_ln_g, d_ln_b, loss_part, d_sinks, d_sgu_b)
    small_flight = _start_copies([_with_own_slot(partial, me)], _own_block_copies(_all_others), N_DEV - 1, core,
                                 "small_grad_gather_start")
    _, land_first, dw_in_t, land_pair = _wait_copies(hop1, first_hop, small_flight[3], "w_in_grad_first_hop_wait")
    second_device = (4 * ((xi + ci) % 2) + 2 * ((yi + 1 - ci) % 2) + ci).astype(jnp.int32).reshape(1)
    relay = _relay_sum(second_device, dw_in_t, land_pair, land_first, W_IN_SHARD // 3)
    hop2 = _start_copies([relay, lax.empty((1, W_IN_SHARD, D_MODEL), BF16)], _second_hop_copies, 1, core,
                         "w_in_grad_second_hop_start")
    pair_out, land_out = _wait_copies(out_flight, _chip_copies, hop2[3], "w_out_grad_chip_wait")
    big = {"w_out": _adam_from_chips(chip, pair_out, [(land_out, k) for k in range(3)], w_out[0], m_w_out[0],
                                     v_w_out[0], "adam_w_out", 1024)}
    partial_all = _wait_copies(small_flight, _own_block_copies(_all_others), big["w_out"][0],
                               "small_grad_gather_wait")[0]
    d_sgu_w_all = _wait_copies(sgu_w_flight, _group(_own_block_copies(_all_others), 0, 1, 3), partial_all,
                               "sgu_w_grad_gather_wait")[0]
    weights = {"norm_g": norm_g, "b_ada": b_ada, "attn_sinks": attn_sinks, "sgu_ln_g": sgu_ln_g,
               "sgu_ln_b": sgu_ln_b, "sgu_w": sgu_w, "sgu_b": sgu_b, "final_g": final_g_row}
    moments_m = {"norm_g": m_norm_g, "b_ada": m_b_ada, "attn_sinks": m_attn_sinks, "sgu_ln_g": m_sgu_ln_g,
                 "sgu_ln_b": m_sgu_ln_b, "sgu_w": m_sgu_w, "sgu_b": m_sgu_b,
                 "final_g": m_final_g.reshape(1, D_MODEL)}
    moments_v = {"norm_g": v_norm_g, "b_ada": v_b_ada, "attn_sinks": v_attn_sinks, "sgu_ln_g": v_sgu_ln_g,
                 "sgu_ln_b": v_sgu_ln_b, "sgu_w": v_sgu_w, "sgu_b": v_sgu_b,
                 "final_g": v_final_g.reshape(1, D_MODEL)}
    loss, dmod_all, small = _adam_small(partial_all, d_sgu_w_all, weights, moments_m, moments_v)
    small["final_g"] = tuple(o.reshape(D_MODEL) for o in small["final_g"])

    big["w_ada"] = _adam_w_ada(device, c_act.T, dmod_all, w_ada[0], m_w_ada[0], v_w_ada[0])
    _, land_second = _wait_copies(hop2, _second_hop_copies, big["w_ada"][0], "w_in_grad_second_hop_wait")
    big["w_in"] = tuple(o.T for o in _adam_from_chips(
        device, dw_in_t, [(land_pair, 0), (land_first, 0), (land_second, 0)], w_in[0].T, m_w_in[0].T, v_w_in[0].T,
        "adam_w_in", 512))
    order = ["norm_g", "w_ada", "b_ada", "w_in", "attn_sinks", "sgu_ln_g", "sgu_ln_b", "sgu_w", "sgu_b", "w_out",
             "final_g"]
    outs = [loss.reshape(()), grad_x[None]]
    for k in range(4):
        for name in order:
            outs.append(big[name][k][None] if name in big else small[name][k])
    return tuple(outs)
```

```python
import jax
import jax.numpy as jnp
from jax import lax
from jax.experimental import pallas as pl
from jax.experimental.pallas import tpu as pltpu

F32 = jnp.float32
BF16 = jnp.bfloat16
MESH = pl.DeviceIdType.MESH

N_DEV = 8
D_MODEL = 2048
HEAD_DIM = 64
D_ATTN = 1024
N_Q_HEADS = 16
D_KV = 128
BLOCK = 128
D_SGU = 1024
SGU_GROUPS = 8
D_IN = 5376
W_IN_SHARD = D_IN // N_DEV
W_OUT_SHARD = D_MODEL // N_DEV
W_ADA_SHARD = 3 * D_MODEL // N_DEV
EPS = 1e-6
ATTN_SCALE = 0.125

ADAM_LR = 0.001
ADAM_B1 = 0.9
ADAM_B2 = 0.999
ADAM_EPS = 1e-08
ADAM_WD = 0.01
ADAM_STEP = 10

SEG_Q, SEG_KV, SEG_GA, SEG_U, SEG_VS, SEG_GS = 0, 1024, 1280, 2304, 3328, 4352

VMEM_LIMIT = 56 * 1024 * 1024

ROW_SHIFT, ROW_SCALE, ROW_GATE, ROW_NORM_G, ROW_FINAL_G, ROW_LN, ROW_MISC, ROW_SGU_B = 0, 1, 2, 3, 4, 5, 6, 8
SMALL_ROWS = 16


def _params(**kw):
    return pltpu.CompilerParams(vmem_limit_bytes=VMEM_LIMIT, **kw)


def _sigmoid(x):
    return 0.5 * (jnp.tanh(0.5 * x) + 1.0)


def _place():
    return lax.axis_index("x"), lax.axis_index("y"), lax.axis_index("c")


def _every_chip(x, y, c):
    return [0, 1, 2, 3]


def _first_hop_chips(x, y, c):
    first = _first_axis_chip(x, y, c)
    return [2 * first[0] + first[1], 2 * (1 - x) + (1 - y)]


def _pair_reduce(blocks, chips, name, row_chunk):
    _, _, r, cols = blocks.shape
    n = len(chips(0, 0, 0))
    assert r % row_chunk == 0

    def body(in_ref, out_ref, land, own, summed, send_sems, recv_sems, own_sems, out_sems):
        x, y, c = _place()
        sends, loads, stores = [], [], []
        for m in range(n):
            cp = pltpu.make_async_remote_copy(
                src_ref=in_ref.at[chips(x, y, 1 - c)[m], 1 - c], dst_ref=land.at[m], send_sem=send_sems.at[m],
                recv_sem=recv_sems.at[m], device_id=(x, y, 1 - c), device_id_type=MESH)
            cp.start()
            sends.append(cp)
            ld = pltpu.make_async_copy(in_ref.at[chips(x, y, c)[m], c], own.at[m], own_sems.at[m])
            ld.start()
            loads.append(ld)
        for m in range(n):
            sends[m].wait_recv()
            loads[m].wait()
            for k in range(r // row_chunk):
                rows = slice(k * row_chunk, (k + 1) * row_chunk)
                summed[m, rows, :] = (own[m, rows, :].astype(F32) + land[m, rows, :].astype(F32)).astype(BF16)
            st = pltpu.make_async_copy(summed.at[m], out_ref.at[m], out_sems.at[m])
            st.start()
            stores.append(st)
        for m in range(n):
            sends[m].wait_send()
            stores[m].wait()

    spec = pl.BlockSpec(memory_space=pl.ANY)
    return pl.pallas_call(
        body, name=name, out_shape=jax.ShapeDtypeStruct((n, r, cols), BF16),
        in_specs=[spec], out_specs=spec,
        scratch_shapes=[pltpu.VMEM((n, r, cols), BF16), pltpu.VMEM((n, r, cols), BF16), pltpu.VMEM((n, r, cols), BF16),
                        pltpu.SemaphoreType.DMA((n,)), pltpu.SemaphoreType.DMA((n,)), pltpu.SemaphoreType.DMA((n,)),
                        pltpu.SemaphoreType.DMA((n,))],
        compiler_params=_params(),
    )(blocks)


_HBM = pl.BlockSpec(memory_space=pltpu.HBM)
_SEM = pl.BlockSpec(memory_space=pltpu.SEMAPHORE)
_EFFECT = pltpu.SideEffectType.DATAFLOW_SIDE_EFFECTING


def _start_copies(bufs, copies, n_copies, after, name):
    nb = len(bufs)

    def body(*refs):
        for cp in copies(refs[:nb], refs[nb + 1], refs[nb + 2]):
            cp.start()
        refs[-1][...] = jnp.zeros_like(refs[-1])

    out = pl.pallas_call(
        body, name=name,
        out_shape=(pltpu.SemaphoreType.DMA((n_copies,)), pltpu.SemaphoreType.DMA((n_copies,)),
                   *[pltpu.HBM(b.shape, b.dtype) for b in bufs], jax.ShapeDtypeStruct((8, 128), F32)),
        in_specs=(_HBM,) * nb + (pl.BlockSpec(memory_space=pl.ANY),),
        out_specs=(_SEM, _SEM) + (_HBM,) * nb + (pl.BlockSpec(memory_space=pltpu.VMEM),),
        input_output_aliases={i: 2 + i for i in range(nb)},
        compiler_params=pltpu.CompilerParams(has_side_effects=_EFFECT),
    )(*[pltpu.with_memory_space_constraint(b, pltpu.HBM) for b in bufs], after)
    return out[0], out[1], list(out[2:2 + nb]), out[-1]


def _wait_copies(flight, copies, after, name):
    send_sems, recv_sems, bufs, _ = flight
    nb = len(bufs)

    def body(*refs):
        for cp in copies(refs[:nb], refs[nb], refs[nb + 1]):
            cp.wait_send()
            cp.wait_recv()

    return pl.pallas_call(
        body, name=name,
        out_shape=tuple(pltpu.HBM(b.shape, b.dtype) for b in bufs),
        in_specs=(_HBM,) * nb + (_SEM, _SEM, pl.BlockSpec(memory_space=pl.ANY)), out_specs=(_HBM,) * nb,
        input_output_aliases={i: i for i in range(nb)},
        compiler_params=pltpu.CompilerParams(has_side_effects=_EFFECT),
    )(*bufs, send_sems, recv_sems, after)


class _From:
    def __init__(self, sems, offset):
        self.sems, self.offset = sems, offset

    @property
    def at(self):
        return self

    def __getitem__(self, k):
        return self.sems.at[k + self.offset]


def _group(copies, first_buf, n_bufs, offset):
    def grouped(refs, send_sems, recv_sems):
        return copies(refs[first_buf:first_buf + n_bufs], _From(send_sems, offset), _From(recv_sems, offset))
    return grouped


def _wait_then_start(flight, waited, started, n_started, after, name, more_bufs=()):
    old_send, old_recv, bufs, _ = flight
    bufs = list(bufs) + [pltpu.with_memory_space_constraint(b, pltpu.HBM) for b in more_bufs]
    nb = len(bufs)

    def body(*refs):
        for cp in waited(refs[:nb], refs[nb], refs[nb + 1]):
            cp.wait_send()
            cp.wait_recv()
        for cp in started(refs[:nb], refs[nb + 3], refs[nb + 4]):
            cp.start()
        refs[-1][...] = jnp.zeros_like(refs[-1])

    out = pl.pallas_call(
        body, name=name,
        out_shape=(pltpu.SemaphoreType.DMA((n_started,)), pltpu.SemaphoreType.DMA((n_started,)),
                   *[pltpu.HBM(b.shape, b.dtype) for b in bufs], jax.ShapeDtypeStruct((8, 128), F32)),
        in_specs=(_HBM,) * nb + (_SEM, _SEM, pl.BlockSpec(memory_space=pl.ANY)),
        out_specs=(_SEM, _SEM) + (_HBM,) * nb + (pl.BlockSpec(memory_space=pltpu.VMEM),),
        input_output_aliases={i: 2 + i for i in range(nb)},
        compiler_params=pltpu.CompilerParams(has_side_effects=_EFFECT),
    )(*bufs, old_send, old_recv, after)
    return out[0], out[1], list(out[2:2 + nb]), out[-1]


def _late_pair_copies(refs, send_sems, recv_sems):
    blocks_ref, land_ref = refs
    x, y, c = _place()
    first = _first_axis_chip(x, y, c)
    devices = [4 * x + 2 * y + 1 - c, 4 * first[0] + 2 * first[1] + 1 - c]
    return [pltpu.make_async_remote_copy(
        src_ref=blocks_ref.at[devices[k]], dst_ref=land_ref.at[k], send_sem=send_sems.at[k], recv_sem=recv_sems.at[k],
        device_id=(x, y, 1 - c), device_id_type=MESH) for k in range(2)]


def _chip_copies(refs, send_sems, recv_sems):
    pair_ref, land_ref = refs
    x, y, c = _place()
    chips = [(1 - x, y), (x, 1 - y), (1 - x, 1 - y)]
    return [pltpu.make_async_remote_copy(
        src_ref=pair_ref.at[2 * chip[0] + chip[1]], dst_ref=land_ref.at[k],
        send_sem=send_sems.at[k], recv_sem=recv_sems.at[k],
        device_id=(*chip, c), device_id_type=MESH) for k, chip in enumerate(chips)]


def _first_hop_copies(refs, send_sems, recv_sems):
    pair_ref, land_ref = refs
    x, y, c = _place()
    return [pltpu.make_async_remote_copy(
        src_ref=pair_ref.at[k], dst_ref=land_ref.at[k], send_sem=send_sems.at[k], recv_sem=recv_sems.at[k],
        device_id=(*_first_axis_chip(x, y, c), c), device_id_type=MESH) for k in range(2)]


def _second_hop_copies(refs, send_sems, recv_sems):
    relay_ref, land_ref = refs
    x, y, c = _place()
    second = ((x + c) % 2, (y + 1 - c) % 2)
    return [pltpu.make_async_remote_copy(
        src_ref=relay_ref, dst_ref=land_ref.at[0], send_sem=send_sems.at[0], recv_sem=recv_sems.at[0],
        device_id=(*second, c), device_id_type=MESH)]


def _own_block_copies(targets):
    def copies(refs, send_sems, recv_sems):
        x, y, c = _place()
        mine = refs[0].at[4 * x + 2 * y + c]
        return [pltpu.make_async_remote_copy(
            src_ref=mine, dst_ref=mine, send_sem=send_sems.at[k], recv_sem=recv_sems.at[k],
            device_id=to, device_id_type=MESH) for k, to in enumerate(targets(x, y, c))]
    return copies


def _all_others(x, y, c):
    flip = lambda v, f: 1 - v if f else v
    return [(flip(x, r & 4), flip(y, r & 2), flip(c, r & 1)) for r in range(1, N_DEV)]


def _forward_copies(refs, send_sems, recv_sems):
    x, y, c = _place()
    chips = [(1 - x, y), (x, 1 - y), (1 - x, 1 - y)]
    return [pltpu.make_async_remote_copy(
        src_ref=refs[0].at[4 * chip[0] + 2 * chip[1] + c], dst_ref=refs[0].at[4 * chip[0] + 2 * chip[1] + c],
        send_sem=send_sems.at[k], recv_sem=recv_sems.at[k],
        device_id=(x, y, 1 - c), device_id_type=MESH) for k, chip in enumerate(chips)]


def _first_axis_chip(x, y, c):
    return (x + 1 - c) % 2, (y + c) % 2


def _second_axis_chip(x, y, c):
    return (x + c) % 2, (y + 1 - c) % 2


def _first_targets(x, y, c):
    return [(x, y, 1 - c), (*_first_axis_chip(x, y, c), c)]


def _all_gather_small(shard, name, dep=None):
    def body(in_ref, *refs):
        out_ref, send_sems, recv_sems, local_sem = refs[-4:]
        x, y, c = _place()
        me, sibling = 4 * x + 2 * y + c, (x, y, 1 - c)
        first, second = _first_axis_chip(x, y, c), _second_axis_chip(x, y, c)

        def pair(chip):
            return out_ref.at[pl.ds(2 * (2 * chip[0] + chip[1]), 2)]

        def exchange(k, src, dst, to):
            cp = pltpu.make_async_remote_copy(src_ref=src, dst_ref=dst, send_sem=send_sems.at[k],
                                              recv_sem=recv_sems.at[k], device_id=to, device_id_type=MESH)
            cp.start()
            cp.wait()

        own = pltpu.make_async_copy(in_ref, out_ref.at[me], local_sem)
        own.start()
        exchange(0, in_ref, out_ref.at[me], sibling)
        own.wait()
        exchange(1, pair((x, y)), pair((x, y)), (*second, c))
        exchange(2, pair(second), pair(second), sibling)
        exchange(3, pair(first), pair(first), (*second, c))

    spec = pl.BlockSpec(memory_space=pltpu.VMEM)
    deps = [] if dep is None else [dep]
    return pl.pallas_call(
        body, name=name, out_shape=jax.ShapeDtypeStruct((N_DEV,) + shard.shape, shard.dtype),
        in_specs=[spec] * (1 + len(deps)), out_specs=spec,
        scratch_shapes=[pltpu.SemaphoreType.DMA((4,)), pltpu.SemaphoreType.DMA((4,)), pltpu.SemaphoreType.DMA],
        compiler_params=_params(),
    )(shard, *deps)


def _slot_copies(refs, send_sems, recv_sems, plan):
    copies = []
    for k, ((px, py, pc), to) in enumerate(plan):
        blk = refs[0].at[4 * px + 2 * py + pc]
        copies.append(pltpu.make_async_remote_copy(
            src_ref=blk, dst_ref=blk, send_sem=send_sems.at[k], recv_sem=recv_sems.at[k],
            device_id=to, device_id_type=MESH))
    return copies


def _second_axis_stage_copies(refs, send_sems, recv_sems):
    x, y, c = _place()
    first, second = (*_first_axis_chip(x, y, c), c), (*_second_axis_chip(x, y, c), c)
    return _slot_copies(refs, send_sems, recv_sems, [((x, y, c), second), (first, (x, y, 1 - c)), (first, second)])


def _second_axis_forward_copies(refs, send_sems, recv_sems):
    x, y, c = _place()
    return _slot_copies(refs, send_sems, recv_sems, [((*_second_axis_chip(x, y, c), c), (x, y, 1 - c))])


def _diagonal_forward_copies(refs, send_sems, recv_sems):
    x, y, c = _place()
    blk = refs[0].at[4 * (1 - x) + 2 * (1 - y) + c]
    return [pltpu.make_async_remote_copy(
        src_ref=blk, dst_ref=blk, send_sem=send_sems.at[0], recv_sem=recv_sems.at[0],
        device_id=(x, y, 1 - c), device_id_type=MESH)]


def _with_own_slot(block, me):
    return lax.dynamic_update_index_in_dim(lax.empty((N_DEV,) + block.shape, block.dtype), block, me, 0)


def _matmul(a, b, dims, out_dtype, tm, tn, name, dep=None):
    if dims == "nn":
        (m, k), n = a.shape, b.shape[1]
        a_spec = pl.BlockSpec((tm, k), lambda i, j: (i, 0))
        b_spec = pl.BlockSpec((k, tn), lambda i, j: (0, j))
        contract = ((1,), (0,))
    elif dims == "nt":
        (m, k), n = a.shape, b.shape[0]
        a_spec = pl.BlockSpec((tm, k), lambda i, j: (i, 0))
        b_spec = pl.BlockSpec((tn, k), lambda i, j: (j, 0))
        contract = ((1,), (1,))
    else:
        (k, m), n = a.shape, b.shape[1]
        a_spec = pl.BlockSpec((k, tm), lambda i, j: (0, i))
        b_spec = pl.BlockSpec((k, tn), lambda i, j: (0, j))
        contract = ((0,), (0,))
    assert m % tm == 0 and n % tn == 0 and a.dtype == BF16 and b.dtype == BF16

    def body(a_ref, b_ref, *rest):
        rest[-1][...] = lax.dot_general(a_ref[...], b_ref[...], (contract, ((), ())),
                                        preferred_element_type=F32).astype(out_dtype)

    deps = [] if dep is None else [dep]
    return pl.pallas_call(
        body, name=name, grid=(m // tm, n // tn),
        in_specs=[a_spec, b_spec] + [pl.BlockSpec((8, 128), lambda i, j: (0, 0))] * len(deps),
        out_specs=pl.BlockSpec((tm, tn), lambda i, j: (i, j)),
        out_shape=jax.ShapeDtypeStruct((m, n), out_dtype),
        compiler_params=_params(dimension_semantics=("arbitrary", "arbitrary")),
    )(a, b, *deps)


Z_TILE = 768
_Z_TILE_ORDER = ((0, 1, 2, 3, 4, 5, 6), (2, 0, 1, 6, 3, 4, 5), (4, 0, 5, 6, 1, 2, 3), (6, 2, 3, 4, 0, 1, 5))
_Z_EARLY_TILES = 4


def _z_proj(h, w_in_t, chip, first, count, z_prev, name, tr=1024):
    t = h.shape[0]

    def body(chip_ref, h_ref, w_ref, z_prev_ref, z_ref):
        z_ref[...] = _dot_nt(h_ref[...], w_ref[...])

    def tile(j, chip_ref):
        picked = 0
        for c, order in enumerate(_Z_TILE_ORDER):
            for k in range(count):
                picked = picked + jnp.where((chip_ref[0] == c) & (j == k), order[first + k], 0)
        return picked

    return pl.pallas_call(
        body, name=name,
        grid_spec=pltpu.PrefetchScalarGridSpec(
            num_scalar_prefetch=1, grid=(count, t // tr),
            in_specs=[pl.BlockSpec((tr, D_MODEL), lambda j, i, o: (i, 0)),
                      pl.BlockSpec((Z_TILE, D_MODEL), lambda j, i, o: (tile(j, o), 0)),
                      pl.BlockSpec(memory_space=pl.ANY)],
            out_specs=pl.BlockSpec((tr, Z_TILE), lambda j, i, o: (i, tile(j, o)))),
        out_shape=jax.ShapeDtypeStruct((t, D_IN), F32),
        input_output_aliases={3: 0},
        compiler_params=_params(dimension_semantics=("arbitrary", "arbitrary")),
    )(chip, h, w_in_t, z_prev)


def _modulation(device, c_all, w_ada, b_ada):
    def body(device_ref, c_ref, w_ref, b_ref, act_ref, mod_ref):
        cv = c_ref[...]
        act = cv * _sigmoid(cv)
        act_ref[...] = act
        mod_ref[...] = jnp.dot(act.astype(BF16), w_ref[...].astype(BF16), preferred_element_type=F32) + b_ref[...]

    whole = lambda a: pl.BlockSpec(a.shape, lambda i, device_ref: (0,) * a.ndim)
    return pl.pallas_call(
        body, name="modulation",
        grid_spec=pltpu.PrefetchScalarGridSpec(
            num_scalar_prefetch=1, grid=(1,),
            in_specs=[whole(c_all), whole(w_ada), pl.BlockSpec((1, W_ADA_SHARD), lambda i, device_ref: (0, device_ref[0]))],
            out_specs=(whole(c_all), pl.BlockSpec((N_DEV, W_ADA_SHARD), lambda i, device_ref: (0, 0)))),
        out_shape=(jax.ShapeDtypeStruct(c_all.shape, F32), jax.ShapeDtypeStruct((N_DEV, W_ADA_SHARD), F32)),
        compiler_params=_params(dimension_semantics=("arbitrary",)),
    )(device, c_all, w_ada, b_ada)


MOD_SHIFT, MOD_SCALE, MOD_GATE = 0, 1, 2


def _mod_spec(part, d):
    return pl.BlockSpec((1, d), lambda i: (0, part))


def _norm_z_proj_own(x, norm_g, mod, w_in_t, chip, tm=512):
    t, d = x.shape

    def body(chip_ref, x_ref, g_ref, sc_ref, sh_ref, w_ref, h_ref, z_ref):
        xv = x_ref[...]
        r = lax.rsqrt(jnp.mean(xv * xv, axis=-1, keepdims=True) + EPS)
        h = ((xv * r) * g_ref[...] * (1.0 + sc_ref[...]) + sh_ref[...]).astype(BF16)
        h_ref[...] = h
        z_ref[...] = _dot_nt(h, w_ref[...])

    def own_tile(chip_ref):
        picked = 0
        for c, order in enumerate(_Z_TILE_ORDER):
            picked = picked + jnp.where(chip_ref[0] == c, order[0], 0)
        return picked

    def row(part):
        return pl.BlockSpec((1, d), lambda i, o: (0, part))

    return pl.pallas_call(
        body, name="norm_z_proj_own",
        grid_spec=pltpu.PrefetchScalarGridSpec(
            num_scalar_prefetch=1, grid=(t // tm,),
            in_specs=[pl.BlockSpec((tm, d), lambda i, o: (i, 0)), row(0), row(MOD_SCALE), row(MOD_SHIFT),
                      pl.BlockSpec((Z_TILE, d), lambda i, o: (own_tile(o), 0))],
            out_specs=(pl.BlockSpec((tm, d), lambda i, o: (i, 0)),
                       pl.BlockSpec((tm, Z_TILE), lambda i, o: (i, own_tile(o))))),
        out_shape=(jax.ShapeDtypeStruct((t, d), BF16), jax.ShapeDtypeStruct((t, D_IN), F32)),
        compiler_params=_params(dimension_semantics=("arbitrary",)),
    )(chip, x, norm_g, mod, mod, w_in_t)


def _window_bias(block_index):
    s = lax.broadcasted_iota(jnp.int32, (2 * BLOCK, BLOCK), 0)
    t = lax.broadcasted_iota(jnp.int32, (2 * BLOCK, BLOCK), 1)
    valid = ((s < BLOCK) & (s > t) & (block_index > 0)) | ((s >= BLOCK) & ((s - BLOCK) <= t))
    bias = jnp.where(valid, 0.0, -jnp.inf).astype(F32)
    return jnp.concatenate([bias] * 8, axis=1)


def _heads_t(pair_blocks, g):
    top = lax.broadcasted_iota(jnp.int32, (BLOCK, BLOCK), 0) < HEAD_DIM
    zeros = jnp.zeros((HEAD_DIM, BLOCK), F32)
    tiles = []
    for blk in pair_blocks:
        tp = blk.T
        if g == 0:
            tiles += [jnp.where(top, tp, 0.0), jnp.concatenate([tp[HEAD_DIM:], zeros], axis=0)]
        else:
            tiles += [jnp.concatenate([zeros, tp[:HEAD_DIM]], axis=0), jnp.where(top, 0.0, tp)]
    return jnp.concatenate(tiles, axis=1)


def _pair_block(xt, p, g):
    r0 = HEAD_DIM * g
    even = xt[r0:r0 + HEAD_DIM, (2 * p) * BLOCK:(2 * p + 1) * BLOCK]
    odd = xt[r0:r0 + HEAD_DIM, (2 * p + 1) * BLOCK:(2 * p + 2) * BLOCK]
    return jnp.concatenate([even, odd], axis=0).T


def _softmax_t(scores_t, bias, sink):
    st = scores_t + bias
    m = jnp.maximum(jnp.max(st, axis=0, keepdims=True), sink)
    e = jnp.exp(st - m)
    es = jnp.exp(sink - m)
    inv = 1.0 / (jnp.sum(e, axis=0, keepdims=True) + es)
    return e * inv, es * inv


def _dot(a, b):
    return jnp.dot(a, b, preferred_element_type=F32)


def _dot_nt(a, b):
    return lax.dot_general(a, b, (((1,), (1,)), ((), ())), preferred_element_type=F32)


def _layer_norm_fwd(v):
    mu = jnp.mean(v, axis=-1, keepdims=True)
    xc = v - mu
    rstd = lax.rsqrt(jnp.mean(xc * xc, axis=-1, keepdims=True) + EPS)
    return xc * rstd, rstd


def _tril(transposed=False):
    t = lax.broadcasted_iota(jnp.int32, (BLOCK, BLOCK), 0)
    s = lax.broadcasted_iota(jnp.int32, (BLOCK, BLOCK), 1)
    return s >= t if transposed else t >= s


def _const_spec(shape):
    return pl.BlockSpec(shape, lambda i: (0,) * len(shape))


def _keys_values(z_ref, kvp):
    kvc = z_ref[:, SEG_KV:SEG_KV + 2 * D_KV]
    kk = jnp.concatenate([kvp[:, :D_KV], kvc[:, :D_KV]], axis=0)
    vv = jnp.concatenate([kvp[:, D_KV:], kvc[:, D_KV:]], axis=0)
    return kk, vv


MIXER_BLOCKS = 2


class _Rows:
    def __init__(self, ref, sub):
        self.ref, self.rows = ref, slice(sub * BLOCK, (sub + 1) * BLOCK)

    def __getitem__(self, idx):
        return self.ref[self.rows, idx[1]]

    def __setitem__(self, idx, value):
        self.ref[self.rows, idx[1]] = value


def _kv_before_spec(index):
    return pl.BlockSpec((BLOCK, 2 * D_KV),
                        lambda i: (jnp.maximum(MIXER_BLOCKS * index(i) - 1, 0), SEG_KV // (2 * D_KV)))


def _pair_cols(g, p, base=0):
    return slice(base + (4 * g + p) * 128, base + (4 * g + p + 1) * 128)


def _mixer_fwd(z, sink_rows, ln_g, ln_b, sgu_w, sgu_bt, dep):
    t = z.shape[0]

    def body(z_all, kvp_ref, sink_ref, lng_ref, lnb_ref, w_ref, bt_ref, dep_ref, a_all, prob_ref, sink_prob_ref):
        kv_before = kvp_ref[...]
        for sub in range(MIXER_BLOCKS):
            z_ref, a_ref = _Rows(z_all, sub), _Rows(a_all, sub)
            one_block(z_ref, kv_before, MIXER_BLOCKS * pl.program_id(0) + sub, sink_ref, lng_ref, lnb_ref, w_ref,
                      bt_ref, a_ref, prob_ref.at[sub], sink_prob_ref.at[sub])
            kv_before = z_ref[:, SEG_KV:SEG_KV + 2 * D_KV]

    def one_block(z_ref, kv_before, block_index, sink_ref, lng_ref, lnb_ref, w_ref, bt_ref, a_ref, prob_ref,
                  sink_prob_ref):
        bias = _window_bias(block_index)
        kk, vv = _keys_values(z_ref, kv_before)
        kk_b, vvt_b = kk.astype(BF16), vv.T.astype(BF16)
        for g in range(2):
            qt = _heads_t([z_ref[:, _pair_cols(g, p, SEG_Q)] * ATTN_SCALE for p in range(4)], g).astype(BF16)
            prob, sink_prob = _softmax_t(_dot(kk_b, qt), bias, sink_ref[g])
            prob_b = prob.astype(BF16)
            prob_ref[g] = prob_b
            sink_prob_ref[g] = sink_prob
            ot = _dot(vvt_b, prob_b)
            for p in range(4):
                gate = z_ref[:, _pair_cols(g, p, SEG_GA)]
                a_ref[:, _pair_cols(g, p)] = (_pair_block(ot, p, g) * (gate * _sigmoid(gate))).astype(BF16)

        vhat, _ = _layer_norm_fwd(z_ref[:, SEG_VS:SEG_VS + D_SGU])
        vn = vhat * lng_ref[...] + lnb_ref[...]
        tril = _tril()
        for g in range(SGU_GROUPS):
            cols = slice(g * 128, (g + 1) * 128)
            wm = jnp.where(tril, w_ref[g], 0.0).astype(BF16)
            mixed = _dot(wm, vn[:, cols].astype(BF16)) + bt_ref[:, g:g + 1]
            gate = z_ref[:, SEG_GS + g * 128:SEG_GS + (g + 1) * 128]
            a_ref[:, D_ATTN + g * 128:D_ATTN + (g + 1) * 128] = (
                (z_ref[:, SEG_U + g * 128:SEG_U + (g + 1) * 128] * mixed) * (gate * _sigmoid(gate))).astype(BF16)

    rows = MIXER_BLOCKS * BLOCK
    return pl.pallas_call(
        body, name="mixer_fwd", grid=(t // rows,),
        in_specs=[pl.BlockSpec((rows, D_IN), lambda i: (i, 0)), _kv_before_spec(lambda i: i),
                  _const_spec((2, 1, 8 * BLOCK)), _const_spec((1, D_SGU)), _const_spec((1, D_SGU)),
                  _const_spec((SGU_GROUPS, BLOCK, BLOCK)), _const_spec((BLOCK, SGU_GROUPS)), _const_spec((8, 128))],
        out_specs=(pl.BlockSpec((rows, D_MODEL), lambda i: (i, 0)),
                   pl.BlockSpec((MIXER_BLOCKS, 2, 2 * BLOCK, 8 * BLOCK), lambda i: (i, 0, 0, 0)),
                   pl.BlockSpec((MIXER_BLOCKS, 2, 1, 8 * BLOCK), lambda i: (i, 0, 0, 0))),
        out_shape=(jax.ShapeDtypeStruct((t, D_MODEL), BF16),
                   jax.ShapeDtypeStruct((t // BLOCK, 2, 2 * BLOCK, 8 * BLOCK), BF16),
                   jax.ShapeDtypeStruct((t // BLOCK, 2, 1, 8 * BLOCK), F32)),
        compiler_params=_params(dimension_semantics=("arbitrary",)),
    )(z, z, sink_rows, ln_g, ln_b, sgu_w, sgu_bt, dep)


def _mixer_bwd(z, da, probs, sink_probs, ln_g, ln_b, sgu_w, sgu_wt, sgu_bt):
    t = z.shape[0]

    def body(z_all, kvp_ref, da_all, prob_ref, sink_prob_ref, lng_ref, lnb_ref, w_ref, wt_ref, bt_ref,
             dz_all, dsink_ref, dw_ref, db_ref, dlng_ref, dlnb_ref, carry_ref, dsink_acc, dbt_acc):
        step = pl.program_id(0)

        @pl.when(step == 0)
        def _():
            carry_ref[...] = jnp.zeros_like(carry_ref)
            dsink_acc[...] = jnp.zeros_like(dsink_acc)
            dbt_acc[...] = jnp.zeros_like(dbt_acc)
            dw_ref[...] = jnp.zeros_like(dw_ref)
            dlng_ref[...] = jnp.zeros_like(dlng_ref)
            dlnb_ref[...] = jnp.zeros_like(dlnb_ref)

        carry = carry_ref[...]
        for sub in reversed(range(MIXER_BLOCKS)):
            kv_before = kvp_ref[...] if sub == 0 else _Rows(z_all, sub - 1)[:, SEG_KV:SEG_KV + 2 * D_KV]
            carry = one_block(_Rows(z_all, sub), kv_before, _Rows(da_all, sub), prob_ref.at[sub], sink_prob_ref.at[sub],
                              carry, lng_ref, lnb_ref, w_ref, wt_ref, bt_ref, _Rows(dz_all, sub),
                              dw_ref, dlng_ref, dlnb_ref, dsink_acc, dbt_acc)
        carry_ref[...] = carry

        @pl.when(step == ns - 1)
        def _():
            db_ref[...] = dbt_acc[...].T[:SGU_GROUPS]
            lane_row = lax.broadcasted_iota(jnp.int32, (1, 128), 1)
            d_sink = jnp.zeros((1, 128), F32)
            for g in range(2):
                acc = dsink_acc[g]
                for j in range(8):
                    head_sum = jnp.sum(acc[:, j * BLOCK:(j + 1) * BLOCK], axis=-1, keepdims=True)
                    d_sink = d_sink + jnp.where(lane_row == 8 * g + j, head_sum, 0.0)
            dsink_ref[...] = d_sink

    def one_block(z_ref, kv_before, da_ref, prob_ref, sink_prob_ref, carry, lng_ref, lnb_ref, w_ref, wt_ref, bt_ref,
                  dz_ref, dw_ref, dlng_ref, dlnb_ref, dsink_acc, dbt_acc):
        kk, vv = _keys_values(z_ref, kv_before)
        vv_b = vv.astype(BF16)
        kkt_b, vvt_b = kk.T.astype(BF16), vv.T.astype(BF16)
        dkk = jnp.zeros((2 * BLOCK, D_KV), F32)
        dvv = jnp.zeros((2 * BLOCK, D_KV), F32)
        for g in range(2):
            qt = _heads_t([z_ref[:, _pair_cols(g, p, SEG_Q)] * ATTN_SCALE for p in range(4)], g).astype(BF16)
            prob_b, sink_prob = prob_ref[g], sink_prob_ref[g]
            prob = prob_b.astype(F32)
            ot = _dot(vvt_b, prob_b)
            gates = [z_ref[:, _pair_cols(g, p, SEG_GA)] for p in range(4)]
            sig = [_sigmoid(gt) for gt in gates]
            d_attn = [da_ref[:, _pair_cols(g, p)] for p in range(4)]
            d_ot = _heads_t([d_attn[p] * (gates[p] * sig[p]) for p in range(4)], g).astype(BF16)
            d_prob = _dot(vv_b, d_ot)
            delta = jnp.sum(prob * d_prob, axis=0, keepdims=True)
            d_scores = (prob * (d_prob - delta)).astype(BF16)
            dsink_acc[g] -= sink_prob * delta
            d_qt = _dot(kkt_b, d_scores)
            dkk = dkk + _dot_nt(d_scores, qt)
            dvv = dvv + _dot_nt(prob_b, d_ot)
            for p in range(4):
                dz_ref[:, _pair_cols(g, p, SEG_Q)] = (_pair_block(d_qt, p, g) * ATTN_SCALE).astype(BF16)
                d_silu = sig[p] * (1.0 + gates[p] * (1.0 - sig[p]))
                dz_ref[:, _pair_cols(g, p, SEG_GA)] = (d_attn[p] * _pair_block(ot, p, g) * d_silu).astype(BF16)
        d_kv = jnp.concatenate([dkk, dvv], axis=1)
        dz_ref[:, SEG_KV:SEG_KV + 2 * D_KV] = (d_kv[BLOCK:] + carry).astype(BF16)

        vhat, rstd = _layer_norm_fwd(z_ref[:, SEG_VS:SEG_VS + D_SGU])
        lng = lng_ref[...]
        vn = vhat * lng + lnb_ref[...]
        tril, triu = _tril(), _tril(transposed=True)
        lane = lax.broadcasted_iota(jnp.int32, (BLOCK, 128), 1)
        d_bt = jnp.zeros((BLOCK, 128), F32)
        d_vn = []
        for g in range(SGU_GROUPS):
            cols = slice(g * 128, (g + 1) * 128)
            wm = jnp.where(tril, w_ref[g], 0.0).astype(BF16)
            wmt = jnp.where(triu, wt_ref[g], 0.0).astype(BF16)
            vn_g = vn[:, cols].astype(BF16)
            mixed = _dot(wm, vn_g) + bt_ref[:, g:g + 1]
            gate = z_ref[:, SEG_GS + g * 128:SEG_GS + (g + 1) * 128]
            u = z_ref[:, SEG_U + g * 128:SEG_U + (g + 1) * 128]
            d_out = da_ref[:, D_ATTN + g * 128:D_ATTN + (g + 1) * 128]
            sg = _sigmoid(gate)
            d_um = d_out * (gate * sg)
            dz_ref[:, SEG_U + g * 128:SEG_U + (g + 1) * 128] = (d_um * mixed).astype(BF16)
            dz_ref[:, SEG_GS + g * 128:SEG_GS + (g + 1) * 128] = (
                d_out * (u * mixed) * (sg * (1.0 + gate * (1.0 - sg)))).astype(BF16)
            d_mixed = d_um * u
            d_mixed_b = d_mixed.astype(BF16)
            dw_ref[g] += jnp.where(tril, _dot_nt(d_mixed_b, vn_g), 0.0)
            d_bt = d_bt + jnp.where(lane == g, jnp.sum(d_mixed, axis=-1, keepdims=True), 0.0)
            d_vn.append(_dot(wmt, d_mixed_b))
        dbt_acc[...] += d_bt
        d_vn = jnp.concatenate(d_vn, axis=1)
        dlng_ref[...] += jnp.sum(d_vn * vhat, axis=0, keepdims=True)
        dlnb_ref[...] += jnp.sum(d_vn, axis=0, keepdims=True)
        d_vhat = d_vn * lng
        d_v = rstd * (d_vhat - jnp.mean(d_vhat, axis=-1, keepdims=True)
                      - vhat * jnp.mean(d_vhat * vhat, axis=-1, keepdims=True))
        dz_ref[:, SEG_VS:SEG_VS + D_SGU] = d_v.astype(BF16)
        return d_kv[:BLOCK]

    rows = MIXER_BLOCKS * BLOCK
    ns = t // rows
    rev = lambda i: ns - 1 - i
    return pl.pallas_call(
        body, name="mixer_bwd", grid=(ns,),
        in_specs=[pl.BlockSpec((rows, D_IN), lambda i: (rev(i), 0)), _kv_before_spec(rev),
                  pl.BlockSpec((rows, D_MODEL), lambda i: (rev(i), 0)),
                  pl.BlockSpec((MIXER_BLOCKS, 2, 2 * BLOCK, 8 * BLOCK), lambda i: (rev(i), 0, 0, 0)),
                  pl.BlockSpec((MIXER_BLOCKS, 2, 1, 8 * BLOCK), lambda i: (rev(i), 0, 0, 0)),
                  _const_spec((1, D_SGU)), _const_spec((1, D_SGU)),
                  _const_spec((SGU_GROUPS, BLOCK, BLOCK)), _const_spec((SGU_GROUPS, BLOCK, BLOCK)),
                  _const_spec((BLOCK, SGU_GROUPS))],
        out_specs=(pl.BlockSpec((rows, D_IN), lambda i: (rev(i), 0)), _const_spec((1, 128)),
                   _const_spec((SGU_GROUPS, BLOCK, BLOCK)), _const_spec((SGU_GROUPS, BLOCK)),
                   _const_spec((1, D_SGU)), _const_spec((1, D_SGU))),
        out_shape=(jax.ShapeDtypeStruct((t, D_IN), BF16), jax.ShapeDtypeStruct((1, 128), F32),
                   jax.ShapeDtypeStruct((SGU_GROUPS, BLOCK, BLOCK), F32), jax.ShapeDtypeStruct((SGU_GROUPS, BLOCK), F32),
                   jax.ShapeDtypeStruct((1, D_SGU), F32), jax.ShapeDtypeStruct((1, D_SGU), F32)),
        scratch_shapes=[pltpu.VMEM((BLOCK, 2 * D_KV), F32), pltpu.VMEM((2, 1, 8 * BLOCK), F32),
                        pltpu.VMEM((BLOCK, 128), F32)],
        compiler_params=_params(dimension_semantics=("arbitrary",)),
    )(z, z, da, probs, sink_probs, ln_g, ln_b, sgu_w, sgu_wt, sgu_bt)


def _out_proj_head(a, w_out_full, x, target, mod, final_g, tm=256):
    t, d = x.shape

    def body(a_ref, w_ref, x_ref, tg_ref, gate_ref, fg_ref, dx2_ref, dy_ref, loss_ref, dfg_ref, dgate_ref):
        @pl.when(pl.program_id(0) == 0)
        def _():
            loss_ref[...] = jnp.zeros_like(loss_ref)
            dfg_ref[...] = jnp.zeros_like(dfg_ref)
            dgate_ref[...] = jnp.zeros_like(dgate_ref)

        yv, gate, fg = _dot(a_ref[...], w_ref[...]), gate_ref[...], fg_ref[...]
        x2 = x_ref[...] + gate * yv
        r2 = lax.rsqrt(jnp.mean(x2 * x2, axis=-1, keepdims=True) + EPS)
        nrm = x2 * r2
        err = nrm * fg - tg_ref[...]
        loss_ref[...] += 0.5 * jnp.sum(jnp.mean(err * err, axis=-1, keepdims=True), axis=0, keepdims=True)
        fg_d = fg * (1.0 / d)
        err_nrm = err * nrm
        dfg_ref[...] += jnp.sum(err_nrm, axis=0, keepdims=True) * (1.0 / d)
        d_nrm = err * fg_d
        dx2 = r2 * (d_nrm - nrm * jnp.mean(err_nrm * fg_d, axis=-1, keepdims=True))
        dx2_ref[...] = dx2
        dgate_ref[...] += jnp.sum(dx2 * yv, axis=0, keepdims=True)
        dy_ref[...] = (dx2 * gate).astype(BF16)

    blk = pl.BlockSpec((tm, d), lambda i: (i, 0))
    row = _const_spec((1, d))
    whole = pl.BlockSpec(w_out_full.shape, lambda i: (0, 0), pipeline_mode=pl.Buffered(1))
    return pl.pallas_call(
        body, name="out_proj_head", grid=(t // tm,),
        in_specs=[pl.BlockSpec((tm, a.shape[1]), lambda i: (i, 0)), whole, blk, blk, _mod_spec(MOD_GATE, d), row],
        out_specs=(blk, blk, _const_spec((1, 128)), row, row),
        out_shape=(jax.ShapeDtypeStruct((t, d), F32), jax.ShapeDtypeStruct((t, d), BF16),
                   jax.ShapeDtypeStruct((1, 128), F32), jax.ShapeDtypeStruct((1, d), F32),
                   jax.ShapeDtypeStruct((1, d), F32)),
        compiler_params=_params(dimension_semantics=("arbitrary",)),
    )(a, w_out_full, x, target, mod, final_g)


def _z_proj_bwd_norm(dz, w_in_t, x, dx2, norm_g, mod, dep, tm=256):
    t, d = x.shape

    def body(dz_ref, w_ref, x_ref, dx2_ref, g_ref, sc_ref, dep_ref, gx_ref, dshift_ref, dscale_ref, dg_ref):
        @pl.when(pl.program_id(0) == 0)
        def _():
            dshift_ref[...] = jnp.zeros_like(dshift_ref)
            dscale_ref[...] = jnp.zeros_like(dscale_ref)
            dg_ref[...] = jnp.zeros_like(dg_ref)

        dh, xv, g = _dot(dz_ref[...], w_ref[...]), x_ref[...], g_ref[...]
        one_plus = 1.0 + sc_ref[...]
        r = lax.rsqrt(jnp.mean(xv * xv, axis=-1, keepdims=True) + EPS)
        xn = xv * r
        gain = one_plus * g
        dh_xn = dh * xn
        dh_xn_sum = jnp.sum(dh_xn, axis=0, keepdims=True)
        dshift_ref[...] += jnp.sum(dh, axis=0, keepdims=True)
        dscale_ref[...] += dh_xn_sum * g
        dg_ref[...] += dh_xn_sum * one_plus
        d_xn = dh * gain
        gx_ref[...] = dx2_ref[...] + r * (d_xn - xn * jnp.mean(dh_xn * gain, axis=-1, keepdims=True))

    blk = pl.BlockSpec((tm, d), lambda i: (i, 0))
    row = _const_spec((1, d))
    whole = pl.BlockSpec(w_in_t.shape, lambda i: (0, 0), pipeline_mode=pl.Buffered(1))
    return pl.pallas_call(
        body, name="z_proj_bwd_norm", grid=(t // tm,),
        in_specs=[pl.BlockSpec((tm, dz.shape[1]), lambda i: (i, 0)), whole, blk, blk, row, _mod_spec(MOD_SCALE, d),
                  _const_spec((8, 128))],
        out_specs=(blk, row, row, row),
        out_shape=(jax.ShapeDtypeStruct((t, d), F32),) + (jax.ShapeDtypeStruct((1, d), F32),) * 3,
        compiler_params=_params(dimension_semantics=("arbitrary",)),
    )(dz, w_in_t, x, dx2, norm_g, mod, dep)


def _adamw(w, g, m, v):
    m = ADAM_B1 * m + (1.0 - ADAM_B1) * g
    v = ADAM_B2 * v + (1.0 - ADAM_B2) * (g * g)
    m_hat = m / (1.0 - ADAM_B1 ** ADAM_STEP)
    v_hat = v / (1.0 - ADAM_B2 ** ADAM_STEP)
    delta = -ADAM_LR * (m_hat / (jnp.sqrt(v_hat) + ADAM_EPS) + ADAM_WD * w)
    return delta, m, v


def _relay_sum(device, blocks, land_pair, land_first, tr):
    _, r, c = blocks.shape

    def body(device_ref, a_ref, b_ref, c_ref, o_ref):
        o_ref[...] = (a_ref[...].astype(F32) + b_ref[...].astype(F32) + c_ref[...].astype(F32)).astype(BF16)

    second = pl.BlockSpec((None, tr, c), lambda i, device_ref: (1, i, 0))
    return pl.pallas_call(
        body, name="w_in_grad_relay_sum",
        grid_spec=pltpu.PrefetchScalarGridSpec(
            num_scalar_prefetch=1, grid=(r // tr,),
            in_specs=[pl.BlockSpec((None, tr, c), lambda i, device_ref: (device_ref[0], i, 0)), second, second],
            out_specs=pl.BlockSpec((tr, c), lambda i, device_ref: (i, 0))),
        out_shape=jax.ShapeDtypeStruct((r, c), BF16),
        compiler_params=_params(dimension_semantics=("arbitrary",)),
    )(device, blocks, land_pair, land_first)


def _adam_from_chips(chip, pair, landed, w, m, v, name, tc):
    _, r, c = pair.shape
    n = len(landed)

    def body(chip_ref, own_ref, *refs):
        w_ref, m_ref, v_ref, g_ref, d_ref, nm_ref, nv_ref = refs[n:]
        g = own_ref[...].astype(F32)
        for k in range(n):
            g = g + refs[k][...].astype(F32)
        g_ref[...] = g
        d_ref[...], nm_ref[...], nv_ref[...] = _adamw(w_ref[...], g, m_ref[...], v_ref[...])

    def landed_spec(index):
        return pl.BlockSpec((None, r, tc), lambda i, chip_ref: (index, 0, i))

    blk = pl.BlockSpec((r, tc), lambda i, chip_ref: (0, i))
    return pl.pallas_call(
        body, name=name,
        grid_spec=pltpu.PrefetchScalarGridSpec(
            num_scalar_prefetch=1, grid=(c // tc,),
            in_specs=[pl.BlockSpec((None, r, tc), lambda i, chip_ref: (chip_ref[0], 0, i))]
            + [landed_spec(index) for _, index in landed] + [blk, blk, blk],
            out_specs=(blk,) * 4),
        out_shape=(jax.ShapeDtypeStruct((r, c), F32),) * 4,
        compiler_params=_params(dimension_semantics=("arbitrary",)),
    )(chip, pair, *[array for array, _ in landed], w, m, v)


def _adam_w_ada(device, act_t, dmod_all, w, m, v, tr=512):
    r, c = w.shape

    def body(device_ref, a_ref, dm_ref, w_ref, m_ref, v_ref, g_ref, d_ref, nm_ref, nv_ref):
        g = _dot(a_ref[...].astype(BF16), dm_ref[...].astype(BF16))
        g_ref[...] = g
        d_ref[...], nm_ref[...], nv_ref[...] = _adamw(w_ref[...], g, m_ref[...], v_ref[...])

    blk = pl.BlockSpec((tr, c), lambda i, device_ref: (i, 0))
    return pl.pallas_call(
        body, name="adam_w_ada",
        grid_spec=pltpu.PrefetchScalarGridSpec(
            num_scalar_prefetch=1, grid=(r // tr,),
            in_specs=[pl.BlockSpec((tr, N_DEV), lambda i, device_ref: (i, 0)),
                      pl.BlockSpec((N_DEV, c), lambda i, device_ref: (0, device_ref[0])), blk, blk, blk],
            out_specs=(blk,) * 4),
        out_shape=(jax.ShapeDtypeStruct((r, c), F32),) * 4,
        compiler_params=_params(dimension_semantics=("arbitrary",)),
    )(device, act_t, dmod_all, w, m, v)


def _pack_small(d_shift, d_scale, d_gate, d_norm_g, d_final_g, d_ln_g, d_ln_b, loss, d_sinks, d_sgu_b):
    def body(shift_ref, scale_ref, gate_ref, ng_ref, fg_ref, lng_ref, lnb_ref, loss_ref, sink_ref, b_ref, o_ref):
        o_ref[...] = jnp.zeros_like(o_ref)
        o_ref[ROW_SHIFT:ROW_SHIFT + 1, :] = shift_ref[...]
        o_ref[ROW_SCALE:ROW_SCALE + 1, :] = scale_ref[...]
        o_ref[ROW_GATE:ROW_GATE + 1, :] = gate_ref[...]
        o_ref[ROW_NORM_G:ROW_NORM_G + 1, :] = ng_ref[...]
        o_ref[ROW_FINAL_G:ROW_FINAL_G + 1, :] = fg_ref[...]
        o_ref[ROW_LN:ROW_LN + 1, 0:D_SGU] = lng_ref[...]
        o_ref[ROW_LN:ROW_LN + 1, D_SGU:2 * D_SGU] = lnb_ref[...]
        o_ref[ROW_MISC:ROW_MISC + 1, 0:128] = loss_ref[...]
        o_ref[ROW_MISC:ROW_MISC + 1, 128:256] = sink_ref[...]
        o_ref[ROW_SGU_B:ROW_SGU_B + SGU_GROUPS, 0:BLOCK] = b_ref[...]

    return pl.pallas_call(
        body, name="pack_small", out_shape=jax.ShapeDtypeStruct((SMALL_ROWS, D_MODEL), F32),
        compiler_params=_params(),
    )(d_shift, d_scale, d_gate, d_norm_g, d_final_g, d_ln_g, d_ln_b, loss, d_sinks, d_sgu_b)


_SMALL_NAMES = ("norm_g", "b_ada", "attn_sinks", "sgu_ln_g", "sgu_ln_b", "sgu_w", "sgu_b", "final_g")


def _adam_small(partials, d_sgu_w_all, weights, moments_m, moments_v):
    names = _SMALL_NAMES
    k = len(names)

    def body(*refs):
        p_ref, sw_ref = refs[0], refs[1]
        w_refs, m_refs, v_refs = refs[2:2 + k], refs[2 + k:2 + 2 * k], refs[2 + 2 * k:2 + 3 * k]
        loss_ref, dmod_ref = refs[2 + 3 * k], refs[3 + 3 * k]
        out_refs = refs[4 + 3 * k:4 + 7 * k]
        sum_ref = refs[4 + 7 * k]
        total = p_ref[0]
        for j in range(1, N_DEV):
            total = total + p_ref[j]
        sum_ref[...] = total
        for j in range(N_DEV):
            for part, row in enumerate((ROW_SHIFT, ROW_SCALE, ROW_GATE)):
                dmod_ref[j:j + 1, part * D_MODEL:(part + 1) * D_MODEL] = p_ref[j, row:row + 1, :]
        loss_ref[...] = sum_ref[ROW_MISC:ROW_MISC + 1, 0:1]
        d_sgu_w = sw_ref[0]
        for j in range(1, N_DEV):
            d_sgu_w = d_sgu_w + sw_ref[j]
        grads = {
            "norm_g": sum_ref[ROW_NORM_G:ROW_NORM_G + 1, :],
            "b_ada": jnp.concatenate([sum_ref[r:r + 1, :] for r in (ROW_SHIFT, ROW_SCALE, ROW_GATE)], axis=1),
            "attn_sinks": sum_ref[ROW_MISC:ROW_MISC + 1, 128:128 + N_Q_HEADS],
            "sgu_ln_g": sum_ref[ROW_LN:ROW_LN + 1, 0:D_SGU],
            "sgu_ln_b": sum_ref[ROW_LN:ROW_LN + 1, D_SGU:2 * D_SGU],
            "sgu_w": d_sgu_w[None],
            "sgu_b": sum_ref[ROW_SGU_B:ROW_SGU_B + SGU_GROUPS, 0:BLOCK][None],
            "final_g": sum_ref[ROW_FINAL_G:ROW_FINAL_G + 1, :],
        }
        for i, name in enumerate(names):
            g = grads[name]
            delta, m, v = _adamw(w_refs[i][...], g, m_refs[i][...], v_refs[i][...])
            out_refs[4 * i][...] = g
            out_refs[4 * i + 1][...] = delta
            out_refs[4 * i + 2][...] = m
            out_refs[4 * i + 3][...] = v

    shapes = [jax.ShapeDtypeStruct((1, 1), F32), jax.ShapeDtypeStruct((N_DEV, 3 * D_MODEL), F32)]
    for name in names:
        shapes += [jax.ShapeDtypeStruct(weights[name].shape, F32)] * 4
    outs = pl.pallas_call(
        body, name="adam_small", out_shape=tuple(shapes),
        scratch_shapes=[pltpu.VMEM((SMALL_ROWS, D_MODEL), F32)],
        compiler_params=_params(),
    )(partials, d_sgu_w_all, *[weights[n] for n in names], *[moments_m[n] for n in names],
      *[moments_v[n] for n in names])
    return outs[0], outs[1], {name: outs[2 + 4 * i:6 + 4 * i] for i, name in enumerate(names)}


def kernel(x, c, norm_g, w_ada, b_ada, w_in, attn_sinks, sgu_ln_g, sgu_ln_b, sgu_w, sgu_b, w_out, final_g, loss_target, m_norm_g, m_w_ada, m_b_ada, m_w_in, m_attn_sinks, m_sgu_ln_g, m_sgu_ln_b, m_sgu_w, m_sgu_b, m_w_out, m_final_g, v_norm_g, v_w_ada, v_b_ada, v_w_in, v_attn_sinks, v_sgu_ln_g, v_sgu_ln_b, v_sgu_w, v_sgu_b, v_w_out, v_final_g):
    xi, yi, ci = _place()
    me = 4 * xi + 2 * yi + ci
    x2d, target = x[0], loss_target[0]
    t = x2d.shape[0]

    core = ci.astype(jnp.int32).reshape(1)
    chip = (2 * xi + yi).astype(jnp.int32).reshape(1)

    first = _own_block_copies(_first_targets)
    first_flight = _start_copies([_with_own_slot(w_in[0].T.astype(BF16), me)], first, 2, core, "gather_w_in_start")

    c_all = _all_gather_small(c.reshape(8, 256), "gather_c", first_flight[3]).reshape(N_DEV, D_MODEL)
    device = me.astype(jnp.int32).reshape(1)
    c_act, mod_part = _modulation(device, c_all, w_ada[0], b_ada)
    mod_all = _all_gather_small(mod_part, "gather_mod")

    across = _wait_then_start(first_flight, lambda *a: first(*a)[1:], _second_axis_stage_copies, 3, mod_all,
                              "gather_w_in_second_axis_stage")
    mod = lax.dynamic_index_in_dim(mod_all, me, axis=1, keepdims=False).reshape(1, 3 * D_MODEL)
    mod = mod + across[3][0, 0]

    w_in_pair = _wait_copies((first_flight[0], first_flight[1], across[2], None), lambda *a: first(*a)[:1], mod,
                             "gather_w_in_sibling_wait")
    h, z_own = _norm_z_proj_own(x2d, norm_g, mod, w_in_pair[0].reshape(D_IN, D_MODEL), chip)
    w_out_early = _own_block_copies(lambda x, y, c: [(x, y, 1 - c), (*_second_axis_chip(x, y, c), c)])
    w_out_late = _own_block_copies(lambda x, y, c: [(*_first_axis_chip(x, y, c), c), (1 - x, 1 - y, c)])
    forward = _wait_then_start(
        (across[0], across[1], w_in_pair, None), lambda *a: _second_axis_stage_copies(*a)[:1],
        lambda refs, s, r: _second_axis_forward_copies(refs[:1], s, r) + _group(w_out_early, 1, 1, 1)(refs, s, r),
        3, z_own, "gather_w_in_second_axis_forward", more_bufs=[_with_own_slot(w_out[0].astype(BF16), me)])
    w_in_most = _wait_copies((across[0], across[1], forward[2][:1], None),
                             lambda *a: _second_axis_stage_copies(*a)[1:2], z_own, "gather_w_in_first_forward_wait")
    w_in_most = _wait_copies((forward[0], forward[1], w_in_most, None), _second_axis_forward_copies, z_own,
                             "gather_w_in_second_forward_wait")
    z_early = _z_proj(h, w_in_most[0].reshape(D_IN, D_MODEL), chip, 1, _Z_EARLY_TILES - 1, z_own, "z_proj_early")
    last = _wait_then_start(
        (across[0], across[1], [w_in_most[0], forward[2][1]], None), lambda *a: _second_axis_stage_copies(*a)[2:],
        lambda refs, s, r: _diagonal_forward_copies(refs[:1], s, r) + _group(w_out_late, 1, 1, 1)(refs, s, r),
        3, z_early, "gather_w_in_last_stage")
    w_in_all = _wait_copies((last[0], last[1], last[2][:1], None), _diagonal_forward_copies, z_early,
                            "gather_w_in_last_wait")[0]
    w_in_t = w_in_all.reshape(D_IN, D_MODEL)
    z = _z_proj(h, w_in_t, chip, _Z_EARLY_TILES, 7 - _Z_EARLY_TILES, z_early, "z_proj_late")
    w_out_half = _wait_copies((forward[0], forward[1], last[2][1:], None), _group(w_out_early, 0, 1, 1), z,
                              "gather_w_out_early_wait")
    w_out_flight = _wait_then_start((last[0], last[1], w_out_half, None), _group(w_out_late, 0, 1, 1),
                                    _forward_copies, 3, z, "gather_w_out_forward_stage")
    sink_rows = jnp.repeat(attn_sinks.reshape(N_Q_HEADS), BLOCK).reshape(2, 1, 8 * BLOCK)
    sgu_bt = sgu_b[0].T
    a, probs, sink_probs = _mixer_fwd(z, sink_rows, sgu_ln_g, sgu_ln_b, sgu_w[0], sgu_bt, w_out_flight[3])
    w_out_all = _wait_copies(w_out_flight, _forward_copies, a, "gather_w_out_forward_wait")[0]
    w_out_full = w_out_all.reshape(D_MODEL, D_MODEL)
    final_g_row = final_g.reshape(1, D_MODEL)
    dx2, dy, loss_part, d_final_g, d_gate = _out_proj_head(a, w_out_full, x2d, target, mod, final_g_row)

    da = _matmul(dy, w_out_full, "nt", F32, min(t, 1024), 1024, "out_proj_bwd")
    dw_out = _matmul(a, dy, "tn", BF16, 1024, 1024, "w_out_grad").reshape(4, 2, W_OUT_SHARD, D_MODEL)
    pair_out = _pair_reduce(dw_out, _every_chip, "w_out_grad_pair_reduce", W_OUT_SHARD // 2)
    dz, d_sinks, d_sgu_w, d_sgu_b, d_ln_g, d_ln_b = _mixer_bwd(
        z, da, probs, sink_probs, sgu_ln_g, sgu_ln_b, sgu_w[0], jnp.swapaxes(sgu_w[0], 1, 2), sgu_bt)
    sgu_w_to_all = _group(_own_block_copies(_all_others), 2, 1, 3)
    both = _start_copies(
        [pair_out, lax.empty((3, W_OUT_SHARD, D_MODEL), BF16), _with_own_slot(d_sgu_w, me)],
        lambda refs, s, r: _chip_copies(refs[:2], s, r) + sgu_w_to_all(refs, s, r), 3 + N_DEV - 1, core,
        "w_out_grad_chip_and_sgu_w_gather_start")
    out_flight, sgu_w_flight = (both[0], both[1], both[2][:2], None), (both[0], both[1], both[2][2:], None)
    dw_in_t = _matmul(dz, h, "tn", BF16, 768, D_MODEL, "w_in_grad", dep=both[3])
    pair_in = _pair_reduce(dw_in_t.reshape(4, 2, W_IN_SHARD, D_MODEL), _first_hop_chips, "w_in_grad_pair_reduce",
                           W_IN_SHARD // 3)
    first_hop = lambda refs, s, r: _first_hop_copies(refs[:2], s, r) + _group(_late_pair_copies, 2, 2, 2)(refs, s, r)
    hop1 = _start_copies(
        [pair_in, lax.empty((2, W_IN_SHARD, D_MODEL), BF16), dw_in_t.reshape(N_DEV, W_IN_SHARD, D_MODEL),
         lax.empty((2, W_IN_SHARD, D_MODEL), BF16)], first_hop, 4, core, "w_in_grad_first_hop_start")
    grad_x, d_shift, d_scale, d_norm_g = _z_proj_bwd_norm(dz, w_in_t, x2d, dx2, norm_g, mod, hop1[3])

    partial = _pack_small(d_shift, d_scale, d_gate, d_norm_g, d_final_g, d_ln_g, d_ln_b, loss_part, d_sinks, d_sgu_b)
    small_flight = _start_copies([_with_own_slot(partial, me)], _own_block_copies(_all_others), N_DEV - 1, core,
                                 "small_grad_gather_start")
    _, land_first, dw_in_t, land_pair = _wait_copies(hop1, first_hop, small_flight[3], "w_in_grad_first_hop_wait")
    second_device = (4 * ((xi + ci) % 2) + 2 * ((yi + 1 - ci) % 2) + ci).astype(jnp.int32).reshape(1)
    relay = _relay_sum(second_device, dw_in_t, land_pair, land_first, W_IN_SHARD // 3)
    hop2 = _start_copies([relay, lax.empty((1, W_IN_SHARD, D_MODEL), BF16)], _second_hop_copies, 1, core,
                         "w_in_grad_second_hop_start")
    pair_out, land_out = _wait_copies(out_flight, _chip_copies, hop2[3], "w_out_grad_chip_wait")
    big = {"w_out": _adam_from_chips(chip, pair_out, [(land_out, k) for k in range(3)], w_out[0], m_w_out[0],
                                     v_w_out[0], "adam_w_out", 1024)}
    partial_all = _wait_copies(small_flight, _own_block_copies(_all_others), big["w_out"][0],
                               "small_grad_gather_wait")[0]
    d_sgu_w_all = _wait_copies(sgu_w_flight, _group(_own_block_copies(_all_others), 0, 1, 3), partial_all,
                               "sgu_w_grad_gather_wait")[0]
    weights = {"norm_g": norm_g, "b_ada": b_ada, "attn_sinks": attn_sinks, "sgu_ln_g": sgu_ln_g,
               "sgu_ln_b": sgu_ln_b, "sgu_w": sgu_w, "sgu_b": sgu_b, "final_g": final_g_row}
    moments_m = {"norm_g": m_norm_g, "b_ada": m_b_ada, "attn_sinks": m_attn_sinks, "sgu_ln_g": m_sgu_ln_g,
                 "sgu_ln_b": m_sgu_ln_b, "sgu_w": m_sgu_w, "sgu_b": m_sgu_b,
                 "final_g": m_final_g.reshape(1, D_MODEL)}
    moments_v = {"norm_g": v_norm_g, "b_ada": v_b_ada, "attn_sinks": v_attn_sinks, "sgu_ln_g": v_sgu_ln_g,
                 "sgu_ln_b": v_sgu_ln_b, "sgu_w": v_sgu_w, "sgu_b": v_sgu_b,
                 "final_g": v_final_g.reshape(1, D_MODEL)}
    loss, dmod_all, small = _adam_small(partial_all, d_sgu_w_all, weights, moments_m, moments_v)
    small["final_g"] = tuple(o.reshape(D_MODEL) for o in small["final_g"])

    big["w_ada"] = _adam_w_ada(device, c_act.T, dmod_all, w_ada[0], m_w_ada[0], v_w_ada[0])
    _, land_second = _wait_copies(hop2, _second_hop_copies, big["w_ada"][0], "w_in_grad_second_hop_wait")
    big["w_in"] = tuple(o.T for o in _adam_from_chips(
        device, dw_in_t, [(land_pair, 0), (land_first, 0), (land_second, 0)], w_in[0].T, m_w_in[0].T, v_w_in[0].T,
        "adam_w_in", 512))
    order = ["norm_g", "w_ada", "b_ada", "w_in", "attn_sinks", "sgu_ln_g", "sgu_ln_b", "sgu_w", "sgu_b", "w_out",
             "final_g"]
    outs = [loss.reshape(()), grad_x[None]]
    for k in range(4):
        for name in order:
            outs.append(big[name][k][None] if name in big else small[name][k])
    return tuple(outs)
```

```python
import jax
import jax.numpy as jnp
from jax import lax
from jax.experimental import pallas as pl
from jax.experimental.pallas import tpu as pltpu

F32 = jnp.float32
BF16 = jnp.bfloat16
MESH = pl.DeviceIdType.MESH

N_DEV = 8
D_MODEL = 2048
HEAD_DIM = 64
D_ATTN = 1024
N_Q_HEADS = 16
D_KV = 128
BLOCK = 128
D_SGU = 1024
SGU_GROUPS = 8
D_IN = 5376
W_IN_SHARD = D_IN // N_DEV
W_OUT_SHARD = D_MODEL // N_DEV
W_ADA_SHARD = 3 * D_MODEL // N_DEV
EPS = 1e-6
ATTN_SCALE = 0.125

ADAM_LR = 0.001
ADAM_B1 = 0.9
ADAM_B2 = 0.999
ADAM_EPS = 1e-08
ADAM_WD = 0.01
ADAM_STEP = 10

SEG_Q, SEG_KV, SEG_GA, SEG_U, SEG_VS, SEG_GS = 0, 1024, 1280, 2304, 3328, 4352

VMEM_LIMIT = 56 * 1024 * 1024

ROW_SHIFT, ROW_SCALE, ROW_GATE, ROW_NORM_G, ROW_FINAL_G, ROW_LN, ROW_MISC, ROW_SGU_B = 0, 1, 2, 3, 4, 5, 6, 8
SMALL_ROWS = 16


def _params(**kw):
    return pltpu.CompilerParams(vmem_limit_bytes=VMEM_LIMIT, **kw)


def _sigmoid(x):
    return 0.5 * (jnp.tanh(0.5 * x) + 1.0)


def _place():
    return lax.axis_index("x"), lax.axis_index("y"), lax.axis_index("c")


def _every_chip(x, y, c):
    return [0, 1, 2, 3]


def _first_hop_chips(x, y, c):
    first = _first_axis_chip(x, y, c)
    return [2 * first[0] + first[1], 2 * (1 - x) + (1 - y)]


def _pair_reduce(blocks, chips, name, row_chunk):
    _, _, r, cols = blocks.shape
    n = len(chips(0, 0, 0))
    assert r % row_chunk == 0

    def body(in_ref, out_ref, land, own, summed, send_sems, recv_sems, own_sems, out_sems):
        x, y, c = _place()
        sends, loads, stores = [], [], []
        for m in range(n):
            cp = pltpu.make_async_remote_copy(
                src_ref=in_ref.at[chips(x, y, 1 - c)[m], 1 - c], dst_ref=land.at[m], send_sem=send_sems.at[m],
                recv_sem=recv_sems.at[m], device_id=(x, y, 1 - c), device_id_type=MESH)
            cp.start()
            sends.append(cp)
            ld = pltpu.make_async_copy(in_ref.at[chips(x, y, c)[m], c], own.at[m], own_sems.at[m])
            ld.start()
            loads.append(ld)
        for m in range(n):
            sends[m].wait_recv()
            loads[m].wait()
            for k in range(r // row_chunk):
                rows = slice(k * row_chunk, (k + 1) * row_chunk)
                summed[m, rows, :] = (own[m, rows, :].astype(F32) + land[m, rows, :].astype(F32)).astype(BF16)
            st = pltpu.make_async_copy(summed.at[m], out_ref.at[m], out_sems.at[m])
            st.start()
            stores.append(st)
        for m in range(n):
            sends[m].wait_send()
            stores[m].wait()

    spec = pl.BlockSpec(memory_space=pl.ANY)
    return pl.pallas_call(
        body, name=name, out_shape=jax.ShapeDtypeStruct((n, r, cols), BF16),
        in_specs=[spec], out_specs=spec,
        scratch_shapes=[pltpu.VMEM((n, r, cols), BF16), pltpu.VMEM((n, r, cols), BF16), pltpu.VMEM((n, r, cols), BF16),
                        pltpu.SemaphoreType.DMA((n,)), pltpu.SemaphoreType.DMA((n,)), pltpu.SemaphoreType.DMA((n,)),
                        pltpu.SemaphoreType.DMA((n,))],
        compiler_params=_params(),
    )(blocks)


_HBM = pl.BlockSpec(memory_space=pltpu.HBM)
_SEM = pl.BlockSpec(memory_space=pltpu.SEMAPHORE)
_EFFECT = pltpu.SideEffectType.DATAFLOW_SIDE_EFFECTING


def _start_copies(bufs, copies, n_copies, after, name):
    nb = len(bufs)

    def body(*refs):
        for cp in copies(refs[:nb], refs[nb + 1], refs[nb + 2]):
            cp.start()
        refs[-1][...] = jnp.zeros_like(refs[-1])

    out = pl.pallas_call(
        body, name=name,
        out_shape=(pltpu.SemaphoreType.DMA((n_copies,)), pltpu.SemaphoreType.DMA((n_copies,)),
                   *[pltpu.HBM(b.shape, b.dtype) for b in bufs], jax.ShapeDtypeStruct((8, 128), F32)),
        in_specs=(_HBM,) * nb + (pl.BlockSpec(memory_space=pl.ANY),),
        out_specs=(_SEM, _SEM) + (_HBM,) * nb + (pl.BlockSpec(memory_space=pltpu.VMEM),),
        input_output_aliases={i: 2 + i for i in range(nb)},
        compiler_params=pltpu.CompilerParams(has_side_effects=_EFFECT),
    )(*[pltpu.with_memory_space_constraint(b, pltpu.HBM) for b in bufs], after)
    return out[0], out[1], list(out[2:2 + nb]), out[-1]


def _wait_copies(flight, copies, after, name):
    send_sems, recv_sems, bufs, _ = flight
    nb = len(bufs)

    def body(*refs):
        for cp in copies(refs[:nb], refs[nb], refs[nb + 1]):
            cp.wait_send()
            cp.wait_recv()

    return pl.pallas_call(
        body, name=name,
        out_shape=tuple(pltpu.HBM(b.shape, b.dtype) for b in bufs),
        in_specs=(_HBM,) * nb + (_SEM, _SEM, pl.BlockSpec(memory_space=pl.ANY)), out_specs=(_HBM,) * nb,
        input_output_aliases={i: i for i in range(nb)},
        compiler_params=pltpu.CompilerParams(has_side_effects=_EFFECT),
    )(*bufs, send_sems, recv_sems, after)


class _From:
    def __init__(self, sems, offset):
        self.sems, self.offset = sems, offset

    @property
    def at(self):
        return self

    def __getitem__(self, k):
        return self.sems.at[k + self.offset]


def _group(copies, first_buf, n_bufs, offset):
    def grouped(refs, send_sems, recv_sems):
        return copies(refs[first_buf:first_buf + n_bufs], _From(send_sems, offset), _From(recv_sems, offset))
    return grouped


def _wait_then_start(flight, waited, started, n_started, after, name, more_bufs=()):
    old_send, old_recv, bufs, _ = flight
    bufs = list(bufs) + [pltpu.with_memory_space_constraint(b, pltpu.HBM) for b in more_bufs]
    nb = len(bufs)

    def body(*refs):
        for cp in waited(refs[:nb], refs[nb], refs[nb + 1]):
            cp.wait_send()
            cp.wait_recv()
        for cp in started(refs[:nb], refs[nb + 3], refs[nb + 4]):
            cp.start()
        refs[-1][...] = jnp.zeros_like(refs[-1])

    out = pl.pallas_call(
        body, name=name,
        out_shape=(pltpu.SemaphoreType.DMA((n_started,)), pltpu.SemaphoreType.DMA((n_started,)),
                   *[pltpu.HBM(b.shape, b.dtype) for b in bufs], jax.ShapeDtypeStruct((8, 128), F32)),
        in_specs=(_HBM,) * nb + (_SEM, _SEM, pl.BlockSpec(memory_space=pl.ANY)),
        out_specs=(_SEM, _SEM) + (_HBM,) * nb + (pl.BlockSpec(memory_space=pltpu.VMEM),),
        input_output_aliases={i: 2 + i for i in range(nb)},
        compiler_params=pltpu.CompilerParams(has_side_effects=_EFFECT),
    )(*bufs, old_send, old_recv, after)
    return out[0], out[1], list(out[2:2 + nb]), out[-1]


def _late_pair_copies(refs, send_sems, recv_sems):
    blocks_ref, land_ref = refs
    x, y, c = _place()
    first = _first_axis_chip(x, y, c)
    devices = [4 * x + 2 * y + 1 - c, 4 * first[0] + 2 * first[1] + 1 - c]
    return [pltpu.make_async_remote_copy(
        src_ref=blocks_ref.at[devices[k]], dst_ref=land_ref.at[k], send_sem=send_sems.at[k], recv_sem=recv_sems.at[k],
        device_id=(x, y, 1 - c), device_id_type=MESH) for k in range(2)]


def _chip_copies(refs, send_sems, recv_sems):
    pair_ref, land_ref = refs
    x, y, c = _place()
    chips = [(1 - x, y), (x, 1 - y), (1 - x, 1 - y)]
    return [pltpu.make_async_remote_copy(
        src_ref=pair_ref.at[2 * chip[0] + chip[1]], dst_ref=land_ref.at[k],
        send_sem=send_sems.at[k], recv_sem=recv_sems.at[k],
        device_id=(*chip, c), device_id_type=MESH) for k, chip in enumerate(chips)]


def _first_hop_copies(refs, send_sems, recv_sems):
    pair_ref, land_ref = refs
    x, y, c = _place()
    return [pltpu.make_async_remote_copy(
        src_ref=pair_ref.at[k], dst_ref=land_ref.at[k], send_sem=send_sems.at[k], recv_sem=recv_sems.at[k],
        device_id=(*_first_axis_chip(x, y, c), c), device_id_type=MESH) for k in range(2)]


def _second_hop_copies(refs, send_sems, recv_sems):
    relay_ref, land_ref = refs
    x, y, c = _place()
    second = ((x + c) % 2, (y + 1 - c) % 2)
    return [pltpu.make_async_remote_copy(
        src_ref=relay_ref, dst_ref=land_ref.at[0], send_sem=send_sems.at[0], recv_sem=recv_sems.at[0],
        device_id=(*second, c), device_id_type=MESH)]


def _own_block_copies(targets):
    def copies(refs, send_sems, recv_sems):
        x, y, c = _place()
        mine = refs[0].at[4 * x + 2 * y + c]
        return [pltpu.make_async_remote_copy(
            src_ref=mine, dst_ref=mine, send_sem=send_sems.at[k], recv_sem=recv_sems.at[k],
            device_id=to, device_id_type=MESH) for k, to in enumerate(targets(x, y, c))]
    return copies


def _all_others(x, y, c):
    flip = lambda v, f: 1 - v if f else v
    return [(flip(x, r & 4), flip(y, r & 2), flip(c, r & 1)) for r in range(1, N_DEV)]


def _forward_copies(refs, send_sems, recv_sems):
    x, y, c = _place()
    chips = [(1 - x, y), (x, 1 - y), (1 - x, 1 - y)]
    return [pltpu.make_async_remote_copy(
        src_ref=refs[0].at[4 * chip[0] + 2 * chip[1] + c], dst_ref=refs[0].at[4 * chip[0] + 2 * chip[1] + c],
        send_sem=send_sems.at[k], recv_sem=recv_sems.at[k],
        device_id=(x, y, 1 - c), device_id_type=MESH) for k, chip in enumerate(chips)]


def _first_axis_chip(x, y, c):
    return (x + 1 - c) % 2, (y + c) % 2


def _second_axis_chip(x, y, c):
    return (x + c) % 2, (y + 1 - c) % 2


def _first_targets(x, y, c):
    return [(x, y, 1 - c), (*_first_axis_chip(x, y, c), c)]


def _all_gather_small(shard, name, dep=None):
    def body(in_ref, *refs):
        out_ref, send_sems, recv_sems, local_sem = refs[-4:]
        x, y, c = _place()
        me, sibling = 4 * x + 2 * y + c, (x, y, 1 - c)
        first, second = _first_axis_chip(x, y, c), _second_axis_chip(x, y, c)

        def pair(chip):
            return out_ref.at[pl.ds(2 * (2 * chip[0] + chip[1]), 2)]

        def exchange(k, src, dst, to):
            cp = pltpu.make_async_remote_copy(src_ref=src, dst_ref=dst, send_sem=send_sems.at[k],
                                              recv_sem=recv_sems.at[k], device_id=to, device_id_type=MESH)
            cp.start()
            cp.wait()

        own = pltpu.make_async_copy(in_ref, out_ref.at[me], local_sem)
        own.start()
        exchange(0, in_ref, out_ref.at[me], sibling)
        own.wait()
        exchange(1, pair((x, y)), pair((x, y)), (*second, c))
        exchange(2, pair(second), pair(second), sibling)
        exchange(3, pair(first), pair(first), (*second, c))

    spec = pl.BlockSpec(memory_space=pltpu.VMEM)
    deps = [] if dep is None else [dep]
    return pl.pallas_call(
        body, name=name, out_shape=jax.ShapeDtypeStruct((N_DEV,) + shard.shape, shard.dtype),
        in_specs=[spec] * (1 + len(deps)), out_specs=spec,
        scratch_shapes=[pltpu.SemaphoreType.DMA((4,)), pltpu.SemaphoreType.DMA((4,)), pltpu.SemaphoreType.DMA],
        compiler_params=_params(),
    )(shard, *deps)


def _slot_copies(refs, send_sems, recv_sems, plan):
    copies = []
    for k, ((px, py, pc), to) in enumerate(plan):
        blk = refs[0].at[4 * px + 2 * py + pc]
        copies.append(pltpu.make_async_remote_copy(
            src_ref=blk, dst_ref=blk, send_sem=send_sems.at[k], recv_sem=recv_sems.at[k],
            device_id=to, device_id_type=MESH))
    return copies


def _second_axis_stage_copies(refs, send_sems, recv_sems):
    x, y, c = _place()
    first, second = (*_first_axis_chip(x, y, c), c), (*_second_axis_chip(x, y, c), c)
    return _slot_copies(refs, send_sems, recv_sems, [((x, y, c), second), (first, (x, y, 1 - c)), (first, second)])


def _second_axis_forward_copies(refs, send_sems, recv_sems):
    x, y, c = _place()
    return _slot_copies(refs, send_sems, recv_sems, [((*_second_axis_chip(x, y, c), c), (x, y, 1 - c))])


def _diagonal_forward_copies(refs, send_sems, recv_sems):
    x, y, c = _place()
    blk = refs[0].at[4 * (1 - x) + 2 * (1 - y) + c]
    return [pltpu.make_async_remote_copy(
        src_ref=blk, dst_ref=blk, send_sem=send_sems.at[0], recv_sem=recv_sems.at[0],
        device_id=(x, y, 1 - c), device_id_type=MESH)]


def _with_own_slot(block, me):
    return lax.dynamic_update_index_in_dim(lax.empty((N_DEV,) + block.shape, block.dtype), block, me, 0)


def _matmul(a, b, dims, out_dtype, tm, tn, name, dep=None):
    if dims == "nn":
        (m, k), n = a.shape, b.shape[1]
        a_spec = pl.BlockSpec((tm, k), lambda i, j: (i, 0))
        b_spec = pl.BlockSpec((k, tn), lambda i, j: (0, j))
        contract = ((1,), (0,))
    elif dims == "nt":
        (m, k), n = a.shape, b.shape[0]
        a_spec = pl.BlockSpec((tm, k), lambda i, j: (i, 0))
        b_spec = pl.BlockSpec((tn, k), lambda i, j: (j, 0))
        contract = ((1,), (1,))
    else:
        (k, m), n = a.shape, b.shape[1]
        a_spec = pl.BlockSpec((k, tm), lambda i, j: (0, i))
        b_spec = pl.BlockSpec((k, tn), lambda i, j: (0, j))
        contract = ((0,), (0,))
    assert m % tm == 0 and n % tn == 0 and a.dtype == BF16 and b.dtype == BF16

    def body(a_ref, b_ref, *rest):
        rest[-1][...] = lax.dot_general(a_ref[...], b_ref[...], (contract, ((), ())),
                                        preferred_element_type=F32).astype(out_dtype)

    deps = [] if dep is None else [dep]
    return pl.pallas_call(
        body, name=name, grid=(m // tm, n // tn),
        in_specs=[a_spec, b_spec] + [pl.BlockSpec((8, 128), lambda i, j: (0, 0))] * len(deps),
        out_specs=pl.BlockSpec((tm, tn), lambda i, j: (i, j)),
        out_shape=jax.ShapeDtypeStruct((m, n), out_dtype),
        compiler_params=_params(dimension_semantics=("arbitrary", "arbitrary")),
    )(a, b, *deps)


Z_TILE = 768
_Z_TILE_ORDER = ((0, 1, 2, 3, 4, 5, 6), (2, 0, 1, 6, 3, 4, 5), (4, 0, 5, 6, 1, 2, 3), (6, 2, 3, 4, 0, 1, 5))
_Z_EARLY_TILES = 4


def _z_proj(h, w_in_t, chip, first, count, z_prev, name, tr=1024):
    t = h.shape[0]

    def body(chip_ref, h_ref, w_ref, z_prev_ref, z_ref):
        z_ref[...] = _dot_nt(h_ref[...], w_ref[...])

    def tile(j, chip_ref):
        picked = 0
        for c, order in enumerate(_Z_TILE_ORDER):
            for k in range(count):
                picked = picked + jnp.where((chip_ref[0] == c) & (j == k), order[first + k], 0)
        return picked

    return pl.pallas_call(
        body, name=name,
        grid_spec=pltpu.PrefetchScalarGridSpec(
            num_scalar_prefetch=1, grid=(count, t // tr),
            in_specs=[pl.BlockSpec((tr, D_MODEL), lambda j, i, o: (i, 0)),
                      pl.BlockSpec((Z_TILE, D_MODEL), lambda j, i, o: (tile(j, o), 0)),
                      pl.BlockSpec(memory_space=pl.ANY)],
            out_specs=pl.BlockSpec((tr, Z_TILE), lambda j, i, o: (i, tile(j, o)))),
        out_shape=jax.ShapeDtypeStruct((t, D_IN), F32),
        input_output_aliases={3: 0},
        compiler_params=_params(dimension_semantics=("arbitrary", "arbitrary")),
    )(chip, h, w_in_t, z_prev)


def _modulation(device, c_all, w_ada, b_ada):
    def body(device_ref, c_ref, w_ref, b_ref, act_ref, mod_ref):
        cv = c_ref[...]
        act = cv * _sigmoid(cv)
        act_ref[...] = act
        mod_ref[...] = jnp.dot(act.astype(BF16), w_ref[...].astype(BF16), preferred_element_type=F32) + b_ref[...]

    whole = lambda a: pl.BlockSpec(a.shape, lambda i, device_ref: (0,) * a.ndim)
    return pl.pallas_call(
        body, name="modulation",
        grid_spec=pltpu.PrefetchScalarGridSpec(
            num_scalar_prefetch=1, grid=(1,),
            in_specs=[whole(c_all), whole(w_ada), pl.BlockSpec((1, W_ADA_SHARD), lambda i, device_ref: (0, device_ref[0]))],
            out_specs=(whole(c_all), pl.BlockSpec((N_DEV, W_ADA_SHARD), lambda i, device_ref: (0, 0)))),
        out_shape=(jax.ShapeDtypeStruct(c_all.shape, F32), jax.ShapeDtypeStruct((N_DEV, W_ADA_SHARD), F32)),
        compiler_params=_params(dimension_semantics=("arbitrary",)),
    )(device, c_all, w_ada, b_ada)


MOD_SHIFT, MOD_SCALE, MOD_GATE = 0, 1, 2


def _mod_spec(part, d):
    return pl.BlockSpec((1, d), lambda i: (0, part))


def _norm_z_proj_own(x, norm_g, mod, w_in_t, chip, tm=512):
    t, d = x.shape

    def body(chip_ref, x_ref, g_ref, sc_ref, sh_ref, w_ref, h_ref, z_ref):
        xv = x_ref[...]
        r = lax.rsqrt(jnp.mean(xv * xv, axis=-1, keepdims=True) + EPS)
        h = ((xv * r) * g_ref[...] * (1.0 + sc_ref[...]) + sh_ref[...]).astype(BF16)
        h_ref[...] = h
        z_ref[...] = _dot_nt(h, w_ref[...])

    def own_tile(chip_ref):
        picked = 0
        for c, order in enumerate(_Z_TILE_ORDER):
            picked = picked + jnp.where(chip_ref[0] == c, order[0], 0)
        return picked

    def row(part):
        return pl.BlockSpec((1, d), lambda i, o: (0, part))

    return pl.pallas_call(
        body, name="norm_z_proj_own",
        grid_spec=pltpu.PrefetchScalarGridSpec(
            num_scalar_prefetch=1, grid=(t // tm,),
            in_specs=[pl.BlockSpec((tm, d), lambda i, o: (i, 0)), row(0), row(MOD_SCALE), row(MOD_SHIFT),
                      pl.BlockSpec((Z_TILE, d), lambda i, o: (own_tile(o), 0))],
            out_specs=(pl.BlockSpec((tm, d), lambda i, o: (i, 0)),
                       pl.BlockSpec((tm, Z_TILE), lambda i, o: (i, own_tile(o))))),
        out_shape=(jax.ShapeDtypeStruct((t, d), BF16), jax.ShapeDtypeStruct((t, D_IN), F32)),
        compiler_params=_params(dimension_semantics=("arbitrary",)),
    )(chip, x, norm_g, mod, mod, w_in_t)


def _window_bias(block_index):
    s = lax.broadcasted_iota(jnp.int32, (2 * BLOCK, BLOCK), 0)
    t = lax.broadcasted_iota(jnp.int32, (2 * BLOCK, BLOCK), 1)
    valid = ((s < BLOCK) & (s > t) & (block_index > 0)) | ((s >= BLOCK) & ((s - BLOCK) <= t))
    bias = jnp.where(valid, 0.0, -jnp.inf).astype(F32)
    return jnp.concatenate([bias] * 8, axis=1)


def _heads_t(pair_blocks, g):
    top = lax.broadcasted_iota(jnp.int32, (BLOCK, BLOCK), 0) < HEAD_DIM
    zeros = jnp.zeros((HEAD_DIM, BLOCK), F32)
    tiles = []
    for blk in pair_blocks:
        tp = blk.T
        if g == 0:
            tiles += [jnp.where(top, tp, 0.0), jnp.concatenate([tp[HEAD_DIM:], zeros], axis=0)]
        else:
            tiles += [jnp.concatenate([zeros, tp[:HEAD_DIM]], axis=0), jnp.where(top, 0.0, tp)]
    return jnp.concatenate(tiles, axis=1)


def _pair_block(xt, p, g):
    r0 = HEAD_DIM * g
    even = xt[r0:r0 + HEAD_DIM, (2 * p) * BLOCK:(2 * p + 1) * BLOCK]
    odd = xt[r0:r0 + HEAD_DIM, (2 * p + 1) * BLOCK:(2 * p + 2) * BLOCK]
    return jnp.concatenate([even, odd], axis=0).T


def _softmax_t(scores_t, bias, sink):
    st = scores_t + bias
    m = jnp.maximum(jnp.max(st, axis=0, keepdims=True), sink)
    e = jnp.exp(st - m)
    es = jnp.exp(sink - m)
    inv = 1.0 / (jnp.sum(e, axis=0, keepdims=True) + es)
    return e * inv, es * inv


def _dot(a, b):
    return jnp.dot(a, b, preferred_element_type=F32)


def _dot_nt(a, b):
    return lax.dot_general(a, b, (((1,), (1,)), ((), ())), preferred_element_type=F32)


def _layer_norm_fwd(v):
    mu = jnp.mean(v, axis=-1, keepdims=True)
    xc = v - mu
    rstd = lax.rsqrt(jnp.mean(xc * xc, axis=-1, keepdims=True) + EPS)
    return xc * rstd, rstd


def _tril(transposed=False):
    t = lax.broadcasted_iota(jnp.int32, (BLOCK, BLOCK), 0)
    s = lax.broadcasted_iota(jnp.int32, (BLOCK, BLOCK), 1)
    return s >= t if transposed else t >= s


def _const_spec(shape):
    return pl.BlockSpec(shape, lambda i: (0,) * len(shape))


def _keys_values(z_ref, kvp):
    kvc = z_ref[:, SEG_KV:SEG_KV + 2 * D_KV]
    kk = jnp.concatenate([kvp[:, :D_KV], kvc[:, :D_KV]], axis=0)
    vv = jnp.concatenate([kvp[:, D_KV:], kvc[:, D_KV:]], axis=0)
    return kk, vv


MIXER_BLOCKS = 2


class _Rows:
    def __init__(self, ref, sub):
        self.ref, self.rows = ref, slice(sub * BLOCK, (sub + 1) * BLOCK)

    def __getitem__(self, idx):
        return self.ref[self.rows, idx[1]]

    def __setitem__(self, idx, value):
        self.ref[self.rows, idx[1]] = value


def _kv_before_spec(index):
    return pl.BlockSpec((BLOCK, 2 * D_KV),
                        lambda i: (jnp.maximum(MIXER_BLOCKS * index(i) - 1, 0), SEG_KV // (2 * D_KV)))


def _pair_cols(g, p, base=0):
    return slice(base + (4 * g + p) * 128, base + (4 * g + p + 1) * 128)


def _mixer_fwd(z, sink_rows, ln_g, ln_b, sgu_w, sgu_bt):
    t = z.shape[0]

    def body(z_all, kvp_ref, sink_ref, lng_ref, lnb_ref, w_ref, bt_ref, a_all, prob_ref, sink_prob_ref):
        kv_before = kvp_ref[...]
        for sub in range(MIXER_BLOCKS):
            z_ref, a_ref = _Rows(z_all, sub), _Rows(a_all, sub)
            one_block(z_ref, kv_before, MIXER_BLOCKS * pl.program_id(0) + sub, sink_ref, lng_ref, lnb_ref, w_ref,
                      bt_ref, a_ref, prob_ref.at[sub], sink_prob_ref.at[sub])
            kv_before = z_ref[:, SEG_KV:SEG_KV + 2 * D_KV]

    def one_block(z_ref, kv_before, block_index, sink_ref, lng_ref, lnb_ref, w_ref, bt_ref, a_ref, prob_ref,
                  sink_prob_ref):
        bias = _window_bias(block_index)
        kk, vv = _keys_values(z_ref, kv_before)
        kk_b, vvt_b = kk.astype(BF16), vv.T.astype(BF16)
        for g in range(2):
            qt = _heads_t([z_ref[:, _pair_cols(g, p, SEG_Q)] * ATTN_SCALE for p in range(4)], g).astype(BF16)
            prob, sink_prob = _softmax_t(_dot(kk_b, qt), bias, sink_ref[g])
            prob_b = prob.astype(BF16)
            prob_ref[g] = prob_b
            sink_prob_ref[g] = sink_prob
            ot = _dot(vvt_b, prob_b)
            for p in range(4):
                gate = z_ref[:, _pair_cols(g, p, SEG_GA)]
                a_ref[:, _pair_cols(g, p)] = (_pair_block(ot, p, g) * (gate * _sigmoid(gate))).astype(BF16)

        vhat, _ = _layer_norm_fwd(z_ref[:, SEG_VS:SEG_VS + D_SGU])
        vn = vhat * lng_ref[...] + lnb_ref[...]
        tril = _tril()
        for g in range(SGU_GROUPS):
            cols = slice(g * 128, (g + 1) * 128)
            wm = jnp.where(tril, w_ref[g], 0.0).astype(BF16)
            mixed = _dot(wm, vn[:, cols].astype(BF16)) + bt_ref[:, g:g + 1]
            gate = z_ref[:, SEG_GS + g * 128:SEG_GS + (g + 1) * 128]
            a_ref[:, D_ATTN + g * 128:D_ATTN + (g + 1) * 128] = (
                (z_ref[:, SEG_U + g * 128:SEG_U + (g + 1) * 128] * mixed) * (gate * _sigmoid(gate))).astype(BF16)

    rows = MIXER_BLOCKS * BLOCK
    return pl.pallas_call(
        body, name="mixer_fwd", grid=(t // rows,),
        in_specs=[pl.BlockSpec((rows, D_IN), lambda i: (i, 0)), _kv_before_spec(lambda i: i),
                  _const_spec((2, 1, 8 * BLOCK)), _const_spec((1, D_SGU)), _const_spec((1, D_SGU)),
                  _const_spec((SGU_GROUPS, BLOCK, BLOCK)), _const_spec((BLOCK, SGU_GROUPS))],
        out_specs=(pl.BlockSpec((rows, D_MODEL), lambda i: (i, 0)),
                   pl.BlockSpec((MIXER_BLOCKS, 2, 2 * BLOCK, 8 * BLOCK), lambda i: (i, 0, 0, 0)),
                   pl.BlockSpec((MIXER_BLOCKS, 2, 1, 8 * BLOCK), lambda i: (i, 0, 0, 0))),
        out_shape=(jax.ShapeDtypeStruct((t, D_MODEL), BF16),
                   jax.ShapeDtypeStruct((t // BLOCK, 2, 2 * BLOCK, 8 * BLOCK), BF16),
                   jax.ShapeDtypeStruct((t // BLOCK, 2, 1, 8 * BLOCK), F32)),
        compiler_params=_params(dimension_semantics=("arbitrary",)),
    )(z, z, sink_rows, ln_g, ln_b, sgu_w, sgu_bt)


def _mixer_bwd(z, da, probs, sink_probs, ln_g, ln_b, sgu_w, sgu_wt, sgu_bt):
    t = z.shape[0]

    def body(z_all, kvp_ref, da_all, prob_ref, sink_prob_ref, lng_ref, lnb_ref, w_ref, wt_ref, bt_ref,
             dz_all, dsink_ref, dw_ref, db_ref, dlng_ref, dlnb_ref, carry_ref, dsink_acc, dbt_acc):
        step = pl.program_id(0)

        @pl.when(step == 0)
        def _():
            carry_ref[...] = jnp.zeros_like(carry_ref)
            dsink_acc[...] = jnp.zeros_like(dsink_acc)
            dbt_acc[...] = jnp.zeros_like(dbt_acc)
            dw_ref[...] = jnp.zeros_like(dw_ref)
            dlng_ref[...] = jnp.zeros_like(dlng_ref)
            dlnb_ref[...] = jnp.zeros_like(dlnb_ref)

        carry = carry_ref[...]
        for sub in reversed(range(MIXER_BLOCKS)):
            kv_before = kvp_ref[...] if sub == 0 else _Rows(z_all, sub - 1)[:, SEG_KV:SEG_KV + 2 * D_KV]
            carry = one_block(_Rows(z_all, sub), kv_before, _Rows(da_all, sub), prob_ref.at[sub], sink_prob_ref.at[sub],
                              carry, lng_ref, lnb_ref, w_ref, wt_ref, bt_ref, _Rows(dz_all, sub),
                              dw_ref, dlng_ref, dlnb_ref, dsink_acc, dbt_acc)
        carry_ref[...] = carry

        @pl.when(step == ns - 1)
        def _():
            db_ref[...] = dbt_acc[...].T[:SGU_GROUPS]
            lane_row = lax.broadcasted_iota(jnp.int32, (1, 128), 1)
            d_sink = jnp.zeros((1, 128), F32)
            for g in range(2):
                acc = dsink_acc[g]
                for j in range(8):
                    head_sum = jnp.sum(acc[:, j * BLOCK:(j + 1) * BLOCK], axis=-1, keepdims=True)
                    d_sink = d_sink + jnp.where(lane_row == 8 * g + j, head_sum, 0.0)
            dsink_ref[...] = d_sink

    def one_block(z_ref, kv_before, da_ref, prob_ref, sink_prob_ref, carry, lng_ref, lnb_ref, w_ref, wt_ref, bt_ref,
                  dz_ref, dw_ref, dlng_ref, dlnb_ref, dsink_acc, dbt_acc):
        kk, vv = _keys_values(z_ref, kv_before)
        vv_b = vv.astype(BF16)
        kkt_b, vvt_b = kk.T.astype(BF16), vv.T.astype(BF16)
        dkk = jnp.zeros((2 * BLOCK, D_KV), F32)
        dvv = jnp.zeros((2 * BLOCK, D_KV), F32)
        for g in range(2):
            qt = _heads_t([z_ref[:, _pair_cols(g, p, SEG_Q)] * ATTN_SCALE for p in range(4)], g).astype(BF16)
            prob_b, sink_prob = prob_ref[g], sink_prob_ref[g]
            prob = prob_b.astype(F32)
            ot = _dot(vvt_b, prob_b)
            gates = [z_ref[:, _pair_cols(g, p, SEG_GA)] for p in range(4)]
            sig = [_sigmoid(gt) for gt in gates]
            d_attn = [da_ref[:, _pair_cols(g, p)] for p in range(4)]
            d_ot = _heads_t([d_attn[p] * (gates[p] * sig[p]) for p in range(4)], g).astype(BF16)
            d_prob = _dot(vv_b, d_ot)
            delta = jnp.sum(prob * d_prob, axis=0, keepdims=True)
            d_scores = (prob * (d_prob - delta)).astype(BF16)
            dsink_acc[g] -= sink_prob * delta
            d_qt = _dot(kkt_b, d_scores)
            dkk = dkk + _dot_nt(d_scores, qt)
            dvv = dvv + _dot_nt(prob_b, d_ot)
            for p in range(4):
                dz_ref[:, _pair_cols(g, p, SEG_Q)] = (_pair_block(d_qt, p, g) * ATTN_SCALE).astype(BF16)
                d_silu = sig[p] * (1.0 + gates[p] * (1.0 - sig[p]))
                dz_ref[:, _pair_cols(g, p, SEG_GA)] = (d_attn[p] * _pair_block(ot, p, g) * d_silu).astype(BF16)
        d_kv = jnp.concatenate([dkk, dvv], axis=1)
        dz_ref[:, SEG_KV:SEG_KV + 2 * D_KV] = (d_kv[BLOCK:] + carry).astype(BF16)

        vhat, rstd = _layer_norm_fwd(z_ref[:, SEG_VS:SEG_VS + D_SGU])
        lng = lng_ref[...]
        vn = vhat * lng + lnb_ref[...]
        tril, triu = _tril(), _tril(transposed=True)
        lane = lax.broadcasted_iota(jnp.int32, (BLOCK, 128), 1)
        d_bt = jnp.zeros((BLOCK, 128), F32)
        d_vn = []
        for g in range(SGU_GROUPS):
            cols = slice(g * 128, (g + 1) * 128)
            wm = jnp.where(tril, w_ref[g], 0.0).astype(BF16)
            wmt = jnp.where(triu, wt_ref[g], 0.0).astype(BF16)
            vn_g = vn[:, cols].astype(BF16)
            mixed = _dot(wm, vn_g) + bt_ref[:, g:g + 1]
            gate = z_ref[:, SEG_GS + g * 128:SEG_GS + (g + 1) * 128]
            u = z_ref[:, SEG_U + g * 128:SEG_U + (g + 1) * 128]
            d_out = da_ref[:, D_ATTN + g * 128:D_ATTN + (g + 1) * 128]
            sg = _sigmoid(gate)
            d_um = d_out * (gate * sg)
            dz_ref[:, SEG_U + g * 128:SEG_U + (g + 1) * 128] = (d_um * mixed).astype(BF16)
            dz_ref[:, SEG_GS + g * 128:SEG_GS + (g + 1) * 128] = (
                d_out * (u * mixed) * (sg * (1.0 + gate * (1.0 - sg)))).astype(BF16)
            d_mixed = d_um * u
            d_mixed_b = d_mixed.astype(BF16)
            dw_ref[g] += jnp.where(tril, _dot_nt(d_mixed_b, vn_g), 0.0)
            d_bt = d_bt + jnp.where(lane == g, jnp.sum(d_mixed, axis=-1, keepdims=True), 0.0)
            d_vn.append(_dot(wmt, d_mixed_b))
        dbt_acc[...] += d_bt
        d_vn = jnp.concatenate(d_vn, axis=1)
        dlng_ref[...] += jnp.sum(d_vn * vhat, axis=0, keepdims=True)
        dlnb_ref[...] += jnp.sum(d_vn, axis=0, keepdims=True)
        d_vhat = d_vn * lng
        d_v = rstd * (d_vhat - jnp.mean(d_vhat, axis=-1, keepdims=True)
                      - vhat * jnp.mean(d_vhat * vhat, axis=-1, keepdims=True))
        dz_ref[:, SEG_VS:SEG_VS + D_SGU] = d_v.astype(BF16)
        return d_kv[:BLOCK]

    rows = MIXER_BLOCKS * BLOCK
    ns = t // rows
    rev = lambda i: ns - 1 - i
    return pl.pallas_call(
        body, name="mixer_bwd", grid=(ns,),
        in_specs=[pl.BlockSpec((rows, D_IN), lambda i: (rev(i), 0)), _kv_before_spec(rev),
                  pl.BlockSpec((rows, D_MODEL), lambda i: (rev(i), 0)),
                  pl.BlockSpec((MIXER_BLOCKS, 2, 2 * BLOCK, 8 * BLOCK), lambda i: (rev(i), 0, 0, 0)),
                  pl.BlockSpec((MIXER_BLOCKS, 2, 1, 8 * BLOCK), lambda i: (rev(i), 0, 0, 0)),
                  _const_spec((1, D_SGU)), _const_spec((1, D_SGU)),
                  _const_spec((SGU_GROUPS, BLOCK, BLOCK)), _const_spec((SGU_GROUPS, BLOCK, BLOCK)),
                  _const_spec((BLOCK, SGU_GROUPS))],
        out_specs=(pl.BlockSpec((rows, D_IN), lambda i: (rev(i), 0)), _const_spec((1, 128)),
                   _const_spec((SGU_GROUPS, BLOCK, BLOCK)), _const_spec((SGU_GROUPS, BLOCK)),
                   _const_spec((1, D_SGU)), _const_spec((1, D_SGU))),
        out_shape=(jax.ShapeDtypeStruct((t, D_IN), BF16), jax.ShapeDtypeStruct((1, 128), F32),
                   jax.ShapeDtypeStruct((SGU_GROUPS, BLOCK, BLOCK), F32), jax.ShapeDtypeStruct((SGU_GROUPS, BLOCK), F32),
                   jax.ShapeDtypeStruct((1, D_SGU), F32), jax.ShapeDtypeStruct((1, D_SGU), F32)),
        scratch_shapes=[pltpu.VMEM((BLOCK, 2 * D_KV), F32), pltpu.VMEM((2, 1, 8 * BLOCK), F32),
                        pltpu.VMEM((BLOCK, 128), F32)],
        compiler_params=_params(dimension_semantics=("arbitrary",)),
    )(z, z, da, probs, sink_probs, ln_g, ln_b, sgu_w, sgu_wt, sgu_bt)


def _out_proj_head(a, w_out_full, x, target, mod, final_g, tm=256):
    t, d = x.shape

    def body(a_ref, w_ref, x_ref, tg_ref, gate_ref, fg_ref, dx2_ref, dy_ref, da_ref, loss_ref, dfg_ref, dgate_ref):
        @pl.when(pl.program_id(0) == 0)
        def _():
            loss_ref[...] = jnp.zeros_like(loss_ref)
            dfg_ref[...] = jnp.zeros_like(dfg_ref)
            dgate_ref[...] = jnp.zeros_like(dgate_ref)

        yv, gate, fg = _dot(a_ref[...], w_ref[...]), gate_ref[...], fg_ref[...]
        x2 = x_ref[...] + gate * yv
        r2 = lax.rsqrt(jnp.mean(x2 * x2, axis=-1, keepdims=True) + EPS)
        nrm = x2 * r2
        err = nrm * fg - tg_ref[...]
        loss_ref[...] += 0.5 * jnp.sum(jnp.mean(err * err, axis=-1, keepdims=True), axis=0, keepdims=True)
        fg_d = fg * (1.0 / d)
        err_nrm = err * nrm
        dfg_ref[...] += jnp.sum(err_nrm, axis=0, keepdims=True) * (1.0 / d)
        d_nrm = err * fg_d
        dx2 = r2 * (d_nrm - nrm * jnp.mean(err_nrm * fg_d, axis=-1, keepdims=True))
        dx2_ref[...] = dx2
        dgate_ref[...] += jnp.sum(dx2 * yv, axis=0, keepdims=True)
        dy = (dx2 * gate).astype(BF16)
        dy_ref[...] = dy
        da_ref[...] = _dot_nt(dy, w_ref[...])

    blk = pl.BlockSpec((tm, d), lambda i: (i, 0))
    a_blk = pl.BlockSpec((tm, a.shape[1]), lambda i: (i, 0))
    row = _const_spec((1, d))
    whole = pl.BlockSpec(w_out_full.shape, lambda i: (0, 0), pipeline_mode=pl.Buffered(1))
    return pl.pallas_call(
        body, name="out_proj_head", grid=(t // tm,),
        in_specs=[a_blk, whole, blk, blk, _mod_spec(MOD_GATE, d), row],
        out_specs=(blk, blk, a_blk, _const_spec((1, 128)), row, row),
        out_shape=(jax.ShapeDtypeStruct((t, d), F32), jax.ShapeDtypeStruct((t, d), BF16),
                   jax.ShapeDtypeStruct(a.shape, F32), jax.ShapeDtypeStruct((1, 128), F32),
                   jax.ShapeDtypeStruct((1, d), F32), jax.ShapeDtypeStruct((1, d), F32)),
        compiler_params=_params(dimension_semantics=("arbitrary",)),
    )(a, w_out_full, x, target, mod, final_g)


def _z_proj_bwd_norm(dz, w_in_t, x, dx2, norm_g, mod, dep, tm=256):
    t, d = x.shape

    def body(dz_ref, w_ref, x_ref, dx2_ref, g_ref, sc_ref, dep_ref, gx_ref, dshift_ref, dscale_ref, dg_ref):
        @pl.when(pl.program_id(0) == 0)
        def _():
            dshift_ref[...] = jnp.zeros_like(dshift_ref)
            dscale_ref[...] = jnp.zeros_like(dscale_ref)
            dg_ref[...] = jnp.zeros_like(dg_ref)

        dh, xv, g = _dot(dz_ref[...], w_ref[...]), x_ref[...], g_ref[...]
        one_plus = 1.0 + sc_ref[...]
        r = lax.rsqrt(jnp.mean(xv * xv, axis=-1, keepdims=True) + EPS)
        xn = xv * r
        gain = one_plus * g
        dh_xn = dh * xn
        dh_xn_sum = jnp.sum(dh_xn, axis=0, keepdims=True)
        dshift_ref[...] += jnp.sum(dh, axis=0, keepdims=True)
        dscale_ref[...] += dh_xn_sum * g
        dg_ref[...] += dh_xn_sum * one_plus
        d_xn = dh * gain
        gx_ref[...] = dx2_ref[...] + r * (d_xn - xn * jnp.mean(dh_xn * gain, axis=-1, keepdims=True))

    blk = pl.BlockSpec((tm, d), lambda i: (i, 0))
    row = _const_spec((1, d))
    whole = pl.BlockSpec(w_in_t.shape, lambda i: (0, 0), pipeline_mode=pl.Buffered(1))
    return pl.pallas_call(
        body, name="z_proj_bwd_norm", grid=(t // tm,),
        in_specs=[pl.BlockSpec((tm, dz.shape[1]), lambda i: (i, 0)), whole, blk, blk, row, _mod_spec(MOD_SCALE, d),
                  _const_spec((8, 128))],
        out_specs=(blk, row, row, row),
        out_shape=(jax.ShapeDtypeStruct((t, d), F32),) + (jax.ShapeDtypeStruct((1, d), F32),) * 3,
        compiler_params=_params(dimension_semantics=("arbitrary",)),
    )(dz, w_in_t, x, dx2, norm_g, mod, dep)


def _adamw(w, g, m, v):
    m = ADAM_B1 * m + (1.0 - ADAM_B1) * g
    v = ADAM_B2 * v + (1.0 - ADAM_B2) * (g * g)
    m_hat = m / (1.0 - ADAM_B1 ** ADAM_STEP)
    v_hat = v / (1.0 - ADAM_B2 ** ADAM_STEP)
    delta = -ADAM_LR * (m_hat / (jnp.sqrt(v_hat) + ADAM_EPS) + ADAM_WD * w)
    return delta, m, v


def _relay_sum(device, blocks, land_pair, land_first, tr):
    _, r, c = blocks.shape

    def body(device_ref, a_ref, b_ref, c_ref, o_ref):
        o_ref[...] = (a_ref[...].astype(F32) + b_ref[...].astype(F32) + c_ref[...].astype(F32)).astype(BF16)

    second = pl.BlockSpec((None, tr, c), lambda i, device_ref: (1, i, 0))
    return pl.pallas_call(
        body, name="w_in_grad_relay_sum",
        grid_spec=pltpu.PrefetchScalarGridSpec(
            num_scalar_prefetch=1, grid=(r // tr,),
            in_specs=[pl.BlockSpec((None, tr, c), lambda i, device_ref: (device_ref[0], i, 0)), second, second],
            out_specs=pl.BlockSpec((tr, c), lambda i, device_ref: (i, 0))),
        out_shape=jax.ShapeDtypeStruct((r, c), BF16),
        compiler_params=_params(dimension_semantics=("arbitrary",)),
    )(device, blocks, land_pair, land_first)


def _adam_from_chips(chip, pair, landed, w, m, v, name, tc):
    _, r, c = pair.shape
    n = len(landed)

    def body(chip_ref, own_ref, *refs):
        w_ref, m_ref, v_ref, g_ref, d_ref, nm_ref, nv_ref = refs[n:]
        g = own_ref[...].astype(F32)
        for k in range(n):
            g = g + refs[k][...].astype(F32)
        g_ref[...] = g
        d_ref[...], nm_ref[...], nv_ref[...] = _adamw(w_ref[...], g, m_ref[...], v_ref[...])

    def landed_spec(index):
        return pl.BlockSpec((None, r, tc), lambda i, chip_ref: (index, 0, i))

    blk = pl.BlockSpec((r, tc), lambda i, chip_ref: (0, i))
    return pl.pallas_call(
        body, name=name,
        grid_spec=pltpu.PrefetchScalarGridSpec(
            num_scalar_prefetch=1, grid=(c // tc,),
            in_specs=[pl.BlockSpec((None, r, tc), lambda i, chip_ref: (chip_ref[0], 0, i))]
            + [landed_spec(index) for _, index in landed] + [blk, blk, blk],
            out_specs=(blk,) * 4),
        out_shape=(jax.ShapeDtypeStruct((r, c), F32),) * 4,
        compiler_params=_params(dimension_semantics=("arbitrary",)),
    )(chip, pair, *[array for array, _ in landed], w, m, v)


def _adam_w_ada(device, act_t, dmod_all, w, m, v, tr=512):
    r, c = w.shape

    def body(device_ref, a_ref, dm_ref, w_ref, m_ref, v_ref, g_ref, d_ref, nm_ref, nv_ref):
        g = _dot(a_ref[...].astype(BF16), dm_ref[...].astype(BF16))
        g_ref[...] = g
        d_ref[...], nm_ref[...], nv_ref[...] = _adamw(w_ref[...], g, m_ref[...], v_ref[...])

    blk = pl.BlockSpec((tr, c), lambda i, device_ref: (i, 0))
    return pl.pallas_call(
        body, name="adam_w_ada",
        grid_spec=pltpu.PrefetchScalarGridSpec(
            num_scalar_prefetch=1, grid=(r // tr,),
            in_specs=[pl.BlockSpec((tr, N_DEV), lambda i, device_ref: (i, 0)),
                      pl.BlockSpec((N_DEV, c), lambda i, device_ref: (0, device_ref[0])), blk, blk, blk],
            out_specs=(blk,) * 4),
        out_shape=(jax.ShapeDtypeStruct((r, c), F32),) * 4,
        compiler_params=_params(dimension_semantics=("arbitrary",)),
    )(device, act_t, dmod_all, w, m, v)


def _pack_small(d_shift, d_scale, d_gate, d_norm_g, d_final_g, d_ln_g, d_ln_b, loss, d_sinks, d_sgu_b):
    def body(shift_ref, scale_ref, gate_ref, ng_ref, fg_ref, lng_ref, lnb_ref, loss_ref, sink_ref, b_ref, o_ref):
        o_ref[...] = jnp.zeros_like(o_ref)
        o_ref[ROW_SHIFT:ROW_SHIFT + 1, :] = shift_ref[...]
        o_ref[ROW_SCALE:ROW_SCALE + 1, :] = scale_ref[...]
        o_ref[ROW_GATE:ROW_GATE + 1, :] = gate_ref[...]
        o_ref[ROW_NORM_G:ROW_NORM_G + 1, :] = ng_ref[...]
        o_ref[ROW_FINAL_G:ROW_FINAL_G + 1, :] = fg_ref[...]
        o_ref[ROW_LN:ROW_LN + 1, 0:D_SGU] = lng_ref[...]
        o_ref[ROW_LN:ROW_LN + 1, D_SGU:2 * D_SGU] = lnb_ref[...]
        o_ref[ROW_MISC:ROW_MISC + 1, 0:128] = loss_ref[...]
        o_ref[ROW_MISC:ROW_MISC + 1, 128:256] = sink_ref[...]
        o_ref[ROW_SGU_B:ROW_SGU_B + SGU_GROUPS, 0:BLOCK] = b_ref[...]

    return pl.pallas_call(
        body, name="pack_small", out_shape=jax.ShapeDtypeStruct((SMALL_ROWS, D_MODEL), F32),
        compiler_params=_params(),
    )(d_shift, d_scale, d_gate, d_norm_g, d_final_g, d_ln_g, d_ln_b, loss, d_sinks, d_sgu_b)


_SMALL_NAMES = ("norm_g", "b_ada", "attn_sinks", "sgu_ln_g", "sgu_ln_b", "sgu_w", "sgu_b", "final_g")


def _adam_small(partials, d_sgu_w_all, weights, moments_m, moments_v):
    names = _SMALL_NAMES
    k = len(names)

    def body(*refs):
        p_ref, sw_ref = refs[0], refs[1]
        w_refs, m_refs, v_refs = refs[2:2 + k], refs[2 + k:2 + 2 * k], refs[2 + 2 * k:2 + 3 * k]
        loss_ref, dmod_ref = refs[2 + 3 * k], refs[3 + 3 * k]
        out_refs = refs[4 + 3 * k:4 + 7 * k]
        sum_ref = refs[4 + 7 * k]
        total = p_ref[0]
        for j in range(1, N_DEV):
            total = total + p_ref[j]
        sum_ref[...] = total
        for j in range(N_DEV):
            for part, row in enumerate((ROW_SHIFT, ROW_SCALE, ROW_GATE)):
                dmod_ref[j:j + 1, part * D_MODEL:(part + 1) * D_MODEL] = p_ref[j, row:row + 1, :]
        loss_ref[...] = sum_ref[ROW_MISC:ROW_MISC + 1, 0:1]
        d_sgu_w = sw_ref[0]
        for j in range(1, N_DEV):
            d_sgu_w = d_sgu_w + sw_ref[j]
        grads = {
            "norm_g": sum_ref[ROW_NORM_G:ROW_NORM_G + 1, :],
            "b_ada": jnp.concatenate([sum_ref[r:r + 1, :] for r in (ROW_SHIFT, ROW_SCALE, ROW_GATE)], axis=1),
            "attn_sinks": sum_ref[ROW_MISC:ROW_MISC + 1, 128:128 + N_Q_HEADS],
            "sgu_ln_g": sum_ref[ROW_LN:ROW_LN + 1, 0:D_SGU],
            "sgu_ln_b": sum_ref[ROW_LN:ROW_LN + 1, D_SGU:2 * D_SGU],
            "sgu_w": d_sgu_w[None],
            "sgu_b": sum_ref[ROW_SGU_B:ROW_SGU_B + SGU_GROUPS, 0:BLOCK][None],
            "final_g": sum_ref[ROW_FINAL_G:ROW_FINAL_G + 1, :],
        }
        for i, name in enumerate(names):
            g = grads[name]
            delta, m, v = _adamw(w_refs[i][...], g, m_refs[i][...], v_refs[i][...])
            out_refs[4 * i][...] = g
            out_refs[4 * i + 1][...] = delta
            out_refs[4 * i + 2][...] = m
            out_refs[4 * i + 3][...] = v

    shapes = [jax.ShapeDtypeStruct((1, 1), F32), jax.ShapeDtypeStruct((N_DEV, 3 * D_MODEL), F32)]
    for name in names:
        shapes += [jax.ShapeDtypeStruct(weights[name].shape, F32)] * 4
    outs = pl.pallas_call(
        body, name="adam_small", out_shape=tuple(shapes),
        scratch_shapes=[pltpu.VMEM((SMALL_ROWS, D_MODEL), F32)],
        compiler_params=_params(),
    )(partials, d_sgu_w_all, *[weights[n] for n in names], *[moments_m[n] for n in names],
      *[moments_v[n] for n in names])
    return outs[0], outs[1], {name: outs[2 + 4 * i:6 + 4 * i] for i, name in enumerate(names)}


def kernel(x, c, norm_g, w_ada, b_ada, w_in, attn_sinks, sgu_ln_g, sgu_ln_b, sgu_w, sgu_b, w_out, final_g, loss_target, m_norm_g, m_w_ada, m_b_ada, m_w_in, m_attn_sinks, m_sgu_ln_g, m_sgu_ln_b, m_sgu_w, m_sgu_b, m_w_out, m_final_g, v_norm_g, v_w_ada, v_b_ada, v_w_in, v_attn_sinks, v_sgu_ln_g, v_sgu_ln_b, v_sgu_w, v_sgu_b, v_w_out, v_final_g):
    xi, yi, ci = _place()
    me = 4 * xi + 2 * yi + ci
    x2d, target = x[0], loss_target[0]
    t = x2d.shape[0]

    core = ci.astype(jnp.int32).reshape(1)
    chip = (2 * xi + yi).astype(jnp.int32).reshape(1)

    first = _own_block_copies(_first_targets)
    first_flight = _start_copies([_with_own_slot(w_in[0].T.astype(BF16), me)], first, 2, core, "gather_w_in_start")

    c_all = _all_gather_small(c.reshape(8, 256), "gather_c", first_flight[3]).reshape(N_DEV, D_MODEL)
    device = me.astype(jnp.int32).reshape(1)
    c_act, mod_part = _modulation(device, c_all, w_ada[0], b_ada)
    mod_all = _all_gather_small(mod_part, "gather_mod")

    across = _wait_then_start(first_flight, lambda *a: first(*a)[1:], _second_axis_stage_copies, 3, mod_all,
                              "gather_w_in_second_axis_stage")
    mod = lax.dynamic_index_in_dim(mod_all, me, axis=1, keepdims=False).reshape(1, 3 * D_MODEL)
    mod = mod + across[3][0, 0]

    w_in_pair = _wait_copies((first_flight[0], first_flight[1], across[2], None), lambda *a: first(*a)[:1], mod,
                             "gather_w_in_sibling_wait")
    h, z_own = _norm_z_proj_own(x2d, norm_g, mod, w_in_pair[0].reshape(D_IN, D_MODEL), chip)
    w_out_early = _own_block_copies(lambda x, y, c: [(x, y, 1 - c), (*_second_axis_chip(x, y, c), c)])
    w_out_late = _own_block_copies(lambda x, y, c: [(*_first_axis_chip(x, y, c), c), (1 - x, 1 - y, c)])
    forward = _wait_then_start(
        (across[0], across[1], w_in_pair, None), lambda *a: _second_axis_stage_copies(*a)[:1],
        lambda refs, s, r: _second_axis_forward_copies(refs[:1], s, r) + _group(w_out_early, 1, 1, 1)(refs, s, r),
        3, z_own, "gather_w_in_second_axis_forward", more_bufs=[_with_own_slot(w_out[0].astype(BF16), me)])
    w_in_most = _wait_copies((across[0], across[1], forward[2][:1], None),
                             lambda *a: _second_axis_stage_copies(*a)[1:2], z_own, "gather_w_in_first_forward_wait")
    w_in_most = _wait_copies((forward[0], forward[1], w_in_most, None), _second_axis_forward_copies, z_own,
                             "gather_w_in_second_forward_wait")
    z_early = _z_proj(h, w_in_most[0].reshape(D_IN, D_MODEL), chip, 1, _Z_EARLY_TILES - 1, z_own, "z_proj_early")
    last = _wait_then_start(
        (across[0], across[1], [w_in_most[0], forward[2][1]], None), lambda *a: _second_axis_stage_copies(*a)[2:],
        lambda refs, s, r: _diagonal_forward_copies(refs[:1], s, r) + _group(w_out_late, 1, 1, 1)(refs, s, r),
        3, z_early, "gather_w_in_last_stage")
    w_in_all = _wait_copies((last[0], last[1], last[2][:1], None), _diagonal_forward_copies, z_early,
                            "gather_w_in_last_wait")[0]
    w_in_t = w_in_all.reshape(D_IN, D_MODEL)
    z = _z_proj(h, w_in_t, chip, _Z_EARLY_TILES, 7 - _Z_EARLY_TILES, z_early, "z_proj_late")
    w_out_half = _wait_copies((forward[0], forward[1], last[2][1:], None), _group(w_out_early, 0, 1, 1), z,
                              "gather_w_out_early_wait")
    w_out_flight = _wait_then_start((last[0], last[1], w_out_half, None), _group(w_out_late, 0, 1, 1),
                                    _forward_copies, 3, z, "gather_w_out_forward_stage")
    sink_rows = jnp.repeat(attn_sinks.reshape(N_Q_HEADS), BLOCK).reshape(2, 1, 8 * BLOCK)
    sgu_bt = sgu_b[0].T
    a, probs, sink_probs = _mixer_fwd(z, sink_rows + w_out_flight[3][0, 0], sgu_ln_g, sgu_ln_b, sgu_w[0], sgu_bt)
    w_out_all = _wait_copies(w_out_flight, _forward_copies, a, "gather_w_out_forward_wait")[0]
    w_out_full = w_out_all.reshape(D_MODEL, D_MODEL)
    final_g_row = final_g.reshape(1, D_MODEL)
    dx2, dy, da, loss_part, d_final_g, d_gate = _out_proj_head(a, w_out_full, x2d, target, mod, final_g_row)

    dw_out = _matmul(a, dy, "tn", BF16, 1024, 1024, "w_out_grad").reshape(4, 2, W_OUT_SHARD, D_MODEL)
    pair_out = _pair_reduce(dw_out, _every_chip, "w_out_grad_pair_reduce", W_OUT_SHARD // 2)
    dz, d_sinks, d_sgu_w, d_sgu_b, d_ln_g, d_ln_b = _mixer_bwd(
        z, da, probs, sink_probs, sgu_ln_g, sgu_ln_b, sgu_w[0], jnp.swapaxes(sgu_w[0], 1, 2), sgu_bt)
    sgu_w_to_all = _group(_own_block_copies(_all_others), 2, 1, 3)
    both = _start_copies(
        [pair_out, lax.empty((3, W_OUT_SHARD, D_MODEL), BF16), _with_own_slot(d_sgu_w, me)],
        lambda refs, s, r: _chip_copies(refs[:2], s, r) + sgu_w_to_all(refs, s, r), 3 + N_DEV - 1, core,
        "w_out_grad_chip_and_sgu_w_gather_start")
    out_flight, sgu_w_flight = (both[0], both[1], both[2][:2], None), (both[0], both[1], both[2][2:], None)
    dw_in_t = _matmul(dz, h, "tn", BF16, 768, D_MODEL, "w_in_grad", dep=both[3])
    pair_in = _pair_reduce(dw_in_t.reshape(4, 2, W_IN_SHARD, D_MODEL), _first_hop_chips, "w_in_grad_pair_reduce",
                           W_IN_SHARD // 3)
    first_hop = lambda refs, s, r: _first_hop_copies(refs[:2], s, r) + _group(_late_pair_copies, 2, 2, 2)(refs, s, r)
    hop1 = _start_copies(
        [pair_in, lax.empty((2, W_IN_SHARD, D_MODEL), BF16), dw_in_t.reshape(N_DEV, W_IN_SHARD, D_MODEL),
         lax.empty((2, W_IN_SHARD, D_MODEL), BF16)], first_hop, 4, core, "w_in_grad_first_hop_start")
    grad_x, d_shift, d_scale, d_norm_g = _z_proj_bwd_norm(dz, w_in_t, x2d, dx2, norm_g, mod, hop1[3])

    partial = _pack_small(d_shift, d_scale, d_gate, d_norm_g, d_final_g, d_ln_g, d_ln_b, loss_part, d_sinks, d_sgu_b)
    small_flight = _start_copies([_with_own_slot(partial, me)], _own_block_copies(_all_others), N_DEV - 1, core,
                                 "small_grad_gather_start")
    _, land_first, dw_in_t, land_pair = _wait_copies(hop1, first_hop, small_flight[3], "w_in_grad_first_hop_wait")
    second_device = (4 * ((xi + ci) % 2) + 2 * ((yi + 1 - ci) % 2) + ci).astype(jnp.int32).reshape(1)
    relay = _relay_sum(second_device, dw_in_t, land_pair, land_first, W_IN_SHARD // 3)
    hop2 = _start_copies([relay, lax.empty((1, W_IN_SHARD, D_MODEL), BF16)], _second_hop_copies, 1, core,
                         "w_in_grad_second_hop_start")
    pair_out, land_out = _wait_copies(out_flight, _chip_copies, hop2[3], "w_out_grad_chip_wait")
    big = {"w_out": _adam_from_chips(chip, pair_out, [(land_out, k) for k in range(3)], w_out[0], m_w_out[0],
                                     v_w_out[0], "adam_w_out", 1024)}
    partial_all = _wait_copies(small_flight, _own_block_copies(_all_others), big["w_out"][0],
                               "small_grad_gather_wait")[0]
    d_sgu_w_all = _wait_copies(sgu_w_flight, _group(_own_block_copies(_all_others), 0, 1, 3), partial_all,
                               "sgu_w_grad_gather_wait")[0]
    weights = {"norm_g": norm_g, "b_ada": b_ada, "attn_sinks": attn_sinks, "sgu_ln_g": sgu_ln_g,
               "sgu_ln_b": sgu_ln_b, "sgu_w": sgu_w, "sgu_b": sgu_b, "final_g": final_g_row}
    moments_m = {"norm_g": m_norm_g, "b_ada": m_b_ada, "attn_sinks": m_attn_sinks, "sgu_ln_g": m_sgu_ln_g,
                 "sgu_ln_b": m_sgu_ln_b, "sgu_w": m_sgu_w, "sgu_b": m_sgu_b,
                 "final_g": m_final_g.reshape(1, D_MODEL)}
    moments_v = {"norm_g": v_norm_g, "b_ada": v_b_ada, "attn_sinks": v_attn_sinks, "sgu_ln_g": v_sgu_ln_g,
                 "sgu_ln_b": v_sgu_ln_b, "sgu_w": v_sgu_w, "sgu_b": v_sgu_b,
                 "final_g": v_final_g.reshape(1, D_MODEL)}
    loss, dmod_all, small = _adam_small(partial_all, d_sgu_w_all, weights, moments_m, moments_v)
    small["final_g"] = tuple(o.reshape(D_MODEL) for o in small["final_g"])

    big["w_ada"] = _adam_w_ada(device, c_act.T, dmod_all, w_ada[0], m_w_ada[0], v_w_ada[0])
    _, land_second = _wait_copies(hop2, _second_hop_copies, big["w_ada"][0], "w_in_grad_second_hop_wait")
    big["w_in"] = tuple(o.T for o in _adam_from_chips(
        device, dw_in_t, [(land_pair, 0), (land_first, 0), (land_second, 0)], w_in[0].T, m_w_in[0].T, v_w_in[0].T,
        "adam_w_in", 512))
    order = ["norm_g", "w_ada", "b_ada", "w_in", "attn_sinks", "sgu_ln_g", "sgu_ln_b", "sgu_w", "sgu_b", "w_out",
             "final_g"]
    outs = [loss.reshape(()), grad_x[None]]
    for k in range(4):
        for name in order:
            outs.append(big[name][k][None] if name in big else small[name][k])
    return tuple(outs)
```

```python
import jax
import jax.numpy as jnp
from jax import lax
from jax.experimental import pallas as pl
from jax.experimental.pallas import tpu as pltpu

F32 = jnp.float32
BF16 = jnp.bfloat16
MESH = pl.DeviceIdType.MESH

N_DEV = 8
D_MODEL = 2048
HEAD_DIM = 64
D_ATTN = 1024
N_Q_HEADS = 16
D_KV = 128
BLOCK = 128
D_SGU = 1024
SGU_GROUPS = 8
D_IN = 5376
W_IN_SHARD = D_IN // N_DEV
W_OUT_SHARD = D_MODEL // N_DEV
W_ADA_SHARD = 3 * D_MODEL // N_DEV
EPS = 1e-6
ATTN_SCALE = 0.125

ADAM_LR = 0.001
ADAM_B1 = 0.9
ADAM_B2 = 0.999
ADAM_EPS = 1e-08
ADAM_WD = 0.01
ADAM_STEP = 10

SEG_Q, SEG_KV, SEG_GA, SEG_U, SEG_VS, SEG_GS = 0, 1024, 1280, 2304, 3328, 4352

VMEM_LIMIT = 56 * 1024 * 1024

ROW_SHIFT, ROW_SCALE, ROW_GATE, ROW_NORM_G, ROW_FINAL_G, ROW_LN, ROW_MISC, ROW_SGU_B = 0, 1, 2, 3, 4, 5, 6, 8
SMALL_ROWS = 16


def _params(**kw):
    return pltpu.CompilerParams(vmem_limit_bytes=VMEM_LIMIT, **kw)


def _sigmoid(x):
    return 0.5 * (jnp.tanh(0.5 * x) + 1.0)


def _place():
    return lax.axis_index("x"), lax.axis_index("y"), lax.axis_index("c")


def _every_chip(x, y, c):
    return [0, 1, 2, 3]


def _first_hop_chips(x, y, c):
    first = _first_axis_chip(x, y, c)
    return [2 * first[0] + first[1], 2 * (1 - x) + (1 - y)]


def _pair_reduce(blocks, chips, name, row_chunk):
    _, _, r, cols = blocks.shape
    n = len(chips(0, 0, 0))
    assert r % row_chunk == 0

    def body(in_ref, out_ref, land, own, summed, send_sems, recv_sems, own_sems, out_sems):
        x, y, c = _place()
        sends, loads, stores = [], [], []
        for m in range(n):
            cp = pltpu.make_async_remote_copy(
                src_ref=in_ref.at[chips(x, y, 1 - c)[m], 1 - c], dst_ref=land.at[m], send_sem=send_sems.at[m],
                recv_sem=recv_sems.at[m], device_id=(x, y, 1 - c), device_id_type=MESH)
            cp.start()
            sends.append(cp)
            ld = pltpu.make_async_copy(in_ref.at[chips(x, y, c)[m], c], own.at[m], own_sems.at[m])
            ld.start()
            loads.append(ld)
        for m in range(n):
            sends[m].wait_recv()
            loads[m].wait()
            for k in range(r // row_chunk):
                rows = slice(k * row_chunk, (k + 1) * row_chunk)
                summed[m, rows, :] = (own[m, rows, :].astype(F32) + land[m, rows, :].astype(F32)).astype(BF16)
            st = pltpu.make_async_copy(summed.at[m], out_ref.at[m], out_sems.at[m])
            st.start()
            stores.append(st)
        for m in range(n):
            sends[m].wait_send()
            stores[m].wait()

    spec = pl.BlockSpec(memory_space=pl.ANY)
    return pl.pallas_call(
        body, name=name, out_shape=jax.ShapeDtypeStruct((n, r, cols), BF16),
        in_specs=[spec], out_specs=spec,
        scratch_shapes=[pltpu.VMEM((n, r, cols), BF16), pltpu.VMEM((n, r, cols), BF16), pltpu.VMEM((n, r, cols), BF16),
                        pltpu.SemaphoreType.DMA((n,)), pltpu.SemaphoreType.DMA((n,)), pltpu.SemaphoreType.DMA((n,)),
                        pltpu.SemaphoreType.DMA((n,))],
        compiler_params=_params(),
    )(blocks)


_HBM = pl.BlockSpec(memory_space=pltpu.HBM)
_SEM = pl.BlockSpec(memory_space=pltpu.SEMAPHORE)
_EFFECT = pltpu.SideEffectType.DATAFLOW_SIDE_EFFECTING


def _start_copies(bufs, copies, n_copies, after, name):
    nb = len(bufs)

    def body(*refs):
        for cp in copies(refs[:nb], refs[nb + 1], refs[nb + 2]):
            cp.start()
        refs[-1][...] = jnp.zeros_like(refs[-1])

    out = pl.pallas_call(
        body, name=name,
        out_shape=(pltpu.SemaphoreType.DMA((n_copies,)), pltpu.SemaphoreType.DMA((n_copies,)),
                   *[pltpu.HBM(b.shape, b.dtype) for b in bufs], jax.ShapeDtypeStruct((8, 128), F32)),
        in_specs=(_HBM,) * nb + (pl.BlockSpec(memory_space=pl.ANY),),
        out_specs=(_SEM, _SEM) + (_HBM,) * nb + (pl.BlockSpec(memory_space=pltpu.VMEM),),
        input_output_aliases={i: 2 + i for i in range(nb)},
        compiler_params=pltpu.CompilerParams(has_side_effects=_EFFECT),
    )(*[pltpu.with_memory_space_constraint(b, pltpu.HBM) for b in bufs], after)
    return out[0], out[1], list(out[2:2 + nb]), out[-1]


def _wait_copies(flight, copies, after, name):
    send_sems, recv_sems, bufs, _ = flight
    nb = len(bufs)

    def body(*refs):
        for cp in copies(refs[:nb], refs[nb], refs[nb + 1]):
            cp.wait_send()
            cp.wait_recv()

    return pl.pallas_call(
        body, name=name,
        out_shape=tuple(pltpu.HBM(b.shape, b.dtype) for b in bufs),
        in_specs=(_HBM,) * nb + (_SEM, _SEM, pl.BlockSpec(memory_space=pl.ANY)), out_specs=(_HBM,) * nb,
        input_output_aliases={i: i for i in range(nb)},
        compiler_params=pltpu.CompilerParams(has_side_effects=_EFFECT),
    )(*bufs, send_sems, recv_sems, after)


class _From:
    def __init__(self, sems, offset):
        self.sems, self.offset = sems, offset

    @property
    def at(self):
        return self

    def __getitem__(self, k):
        return self.sems.at[k + self.offset]


def _group(copies, first_buf, n_bufs, offset):
    def grouped(refs, send_sems, recv_sems):
        return copies(refs[first_buf:first_buf + n_bufs], _From(send_sems, offset), _From(recv_sems, offset))
    return grouped


def _wait_then_start(flight, waited, started, n_started, after, name, more_bufs=()):
    old_send, old_recv, bufs, _ = flight
    bufs = list(bufs) + [pltpu.with_memory_space_constraint(b, pltpu.HBM) for b in more_bufs]
    nb = len(bufs)

    def body(*refs):
        for cp in waited(refs[:nb], refs[nb], refs[nb + 1]):
            cp.wait_send()
            cp.wait_recv()
        for cp in started(refs[:nb], refs[nb + 3], refs[nb + 4]):
            cp.start()
        refs[-1][...] = jnp.zeros_like(refs[-1])

    out = pl.pallas_call(
        body, name=name,
        out_shape=(pltpu.SemaphoreType.DMA((n_started,)), pltpu.SemaphoreType.DMA((n_started,)),
                   *[pltpu.HBM(b.shape, b.dtype) for b in bufs], jax.ShapeDtypeStruct((8, 128), F32)),
        in_specs=(_HBM,) * nb + (_SEM, _SEM, pl.BlockSpec(memory_space=pl.ANY)),
        out_specs=(_SEM, _SEM) + (_HBM,) * nb + (pl.BlockSpec(memory_space=pltpu.VMEM),),
        input_output_aliases={i: 2 + i for i in range(nb)},
        compiler_params=pltpu.CompilerParams(has_side_effects=_EFFECT),
    )(*bufs, old_send, old_recv, after)
    return out[0], out[1], list(out[2:2 + nb]), out[-1]


def _late_pair_copies(refs, send_sems, recv_sems):
    blocks_ref, land_ref = refs
    x, y, c = _place()
    first = _first_axis_chip(x, y, c)
    devices = [4 * x + 2 * y + 1 - c, 4 * first[0] + 2 * first[1] + 1 - c]
    return [pltpu.make_async_remote_copy(
        src_ref=blocks_ref.at[devices[k]], dst_ref=land_ref.at[k], send_sem=send_sems.at[k], recv_sem=recv_sems.at[k],
        device_id=(x, y, 1 - c), device_id_type=MESH) for k in range(2)]


def _chip_copies(refs, send_sems, recv_sems):
    pair_ref, land_ref = refs
    x, y, c = _place()
    chips = [(1 - x, y), (x, 1 - y), (1 - x, 1 - y)]
    return [pltpu.make_async_remote_copy(
        src_ref=pair_ref.at[2 * chip[0] + chip[1]], dst_ref=land_ref.at[k],
        send_sem=send_sems.at[k], recv_sem=recv_sems.at[k],
        device_id=(*chip, c), device_id_type=MESH) for k, chip in enumerate(chips)]


def _first_hop_copies(refs, send_sems, recv_sems):
    pair_ref, land_ref = refs
    x, y, c = _place()
    return [pltpu.make_async_remote_copy(
        src_ref=pair_ref.at[k], dst_ref=land_ref.at[k], send_sem=send_sems.at[k], recv_sem=recv_sems.at[k],
        device_id=(*_first_axis_chip(x, y, c), c), device_id_type=MESH) for k in range(2)]


def _second_hop_copies(refs, send_sems, recv_sems):
    relay_ref, land_ref = refs
    x, y, c = _place()
    second = ((x + c) % 2, (y + 1 - c) % 2)
    return [pltpu.make_async_remote_copy(
        src_ref=relay_ref, dst_ref=land_ref.at[0], send_sem=send_sems.at[0], recv_sem=recv_sems.at[0],
        device_id=(*second, c), device_id_type=MESH)]


def _own_block_copies(targets):
    def copies(refs, send_sems, recv_sems):
        x, y, c = _place()
        mine = refs[0].at[4 * x + 2 * y + c]
        return [pltpu.make_async_remote_copy(
            src_ref=mine, dst_ref=mine, send_sem=send_sems.at[k], recv_sem=recv_sems.at[k],
            device_id=to, device_id_type=MESH) for k, to in enumerate(targets(x, y, c))]
    return copies


def _all_others(x, y, c):
    flip = lambda v, f: 1 - v if f else v
    return [(flip(x, r & 4), flip(y, r & 2), flip(c, r & 1)) for r in range(1, N_DEV)]


def _forward_copies(refs, send_sems, recv_sems):
    x, y, c = _place()
    chips = [(1 - x, y), (x, 1 - y), (1 - x, 1 - y)]
    return [pltpu.make_async_remote_copy(
        src_ref=refs[0].at[4 * chip[0] + 2 * chip[1] + c], dst_ref=refs[0].at[4 * chip[0] + 2 * chip[1] + c],
        send_sem=send_sems.at[k], recv_sem=recv_sems.at[k],
        device_id=(x, y, 1 - c), device_id_type=MESH) for k, chip in enumerate(chips)]


def _first_axis_chip(x, y, c):
    return (x + 1 - c) % 2, (y + c) % 2


def _second_axis_chip(x, y, c):
    return (x + c) % 2, (y + 1 - c) % 2


def _first_targets(x, y, c):
    return [(x, y, 1 - c), (*_first_axis_chip(x, y, c), c)]


def _all_gather_small(shard, name, dep=None):
    def body(in_ref, *refs):
        out_ref, send_sems, recv_sems, local_sem = refs[-4:]
        x, y, c = _place()
        me, sibling = 4 * x + 2 * y + c, (x, y, 1 - c)
        first, second = _first_axis_chip(x, y, c), _second_axis_chip(x, y, c)

        def pair(chip):
            return out_ref.at[pl.ds(2 * (2 * chip[0] + chip[1]), 2)]

        def exchange(k, src, dst, to):
            cp = pltpu.make_async_remote_copy(src_ref=src, dst_ref=dst, send_sem=send_sems.at[k],
                                              recv_sem=recv_sems.at[k], device_id=to, device_id_type=MESH)
            cp.start()
            cp.wait()

        own = pltpu.make_async_copy(in_ref, out_ref.at[me], local_sem)
        own.start()
        exchange(0, in_ref, out_ref.at[me], sibling)
        own.wait()
        exchange(1, pair((x, y)), pair((x, y)), (*second, c))
        exchange(2, pair(second), pair(second), sibling)
        exchange(3, pair(first), pair(first), (*second, c))

    spec = pl.BlockSpec(memory_space=pltpu.VMEM)
    deps = [] if dep is None else [dep]
    return pl.pallas_call(
        body, name=name, out_shape=jax.ShapeDtypeStruct((N_DEV,) + shard.shape, shard.dtype),
        in_specs=[spec] * (1 + len(deps)), out_specs=spec,
        scratch_shapes=[pltpu.SemaphoreType.DMA((4,)), pltpu.SemaphoreType.DMA((4,)), pltpu.SemaphoreType.DMA],
        compiler_params=_params(),
    )(shard, *deps)


def _slot_copies(refs, send_sems, recv_sems, plan):
    copies = []
    for k, ((px, py, pc), to) in enumerate(plan):
        blk = refs[0].at[4 * px + 2 * py + pc]
        copies.append(pltpu.make_async_remote_copy(
            src_ref=blk, dst_ref=blk, send_sem=send_sems.at[k], recv_sem=recv_sems.at[k],
            device_id=to, device_id_type=MESH))
    return copies


def _second_axis_stage_copies(refs, send_sems, recv_sems):
    x, y, c = _place()
    first, second = (*_first_axis_chip(x, y, c), c), (*_second_axis_chip(x, y, c), c)
    return _slot_copies(refs, send_sems, recv_sems, [((x, y, c), second), (first, (x, y, 1 - c)), (first, second)])


def _second_axis_forward_copies(refs, send_sems, recv_sems):
    x, y, c = _place()
    return _slot_copies(refs, send_sems, recv_sems, [((*_second_axis_chip(x, y, c), c), (x, y, 1 - c))])


def _diagonal_forward_copies(refs, send_sems, recv_sems):
    x, y, c = _place()
    blk = refs[0].at[4 * (1 - x) + 2 * (1 - y) + c]
    return [pltpu.make_async_remote_copy(
        src_ref=blk, dst_ref=blk, send_sem=send_sems.at[0], recv_sem=recv_sems.at[0],
        device_id=(x, y, 1 - c), device_id_type=MESH)]


def _with_own_slot(block, me):
    return lax.dynamic_update_index_in_dim(lax.empty((N_DEV,) + block.shape, block.dtype), block, me, 0)


def _matmul(a, b, dims, out_dtype, tm, tn, name, dep=None):
    if dims == "nn":
        (m, k), n = a.shape, b.shape[1]
        a_spec = pl.BlockSpec((tm, k), lambda i, j: (i, 0))
        b_spec = pl.BlockSpec((k, tn), lambda i, j: (0, j))
        contract = ((1,), (0,))
    elif dims == "nt":
        (m, k), n = a.shape, b.shape[0]
        a_spec = pl.BlockSpec((tm, k), lambda i, j: (i, 0))
        b_spec = pl.BlockSpec((tn, k), lambda i, j: (j, 0))
        contract = ((1,), (1,))
    else:
        (k, m), n = a.shape, b.shape[1]
        a_spec = pl.BlockSpec((k, tm), lambda i, j: (0, i))
        b_spec = pl.BlockSpec((k, tn), lambda i, j: (0, j))
        contract = ((0,), (0,))
    assert m % tm == 0 and n % tn == 0 and a.dtype == BF16 and b.dtype == BF16

    def body(a_ref, b_ref, *rest):
        rest[-1][...] = lax.dot_general(a_ref[...], b_ref[...], (contract, ((), ())),
                                        preferred_element_type=F32).astype(out_dtype)

    deps = [] if dep is None else [dep]
    return pl.pallas_call(
        body, name=name, grid=(m // tm, n // tn),
        in_specs=[a_spec, b_spec] + [pl.BlockSpec((8, 128), lambda i, j: (0, 0))] * len(deps),
        out_specs=pl.BlockSpec((tm, tn), lambda i, j: (i, j)),
        out_shape=jax.ShapeDtypeStruct((m, n), out_dtype),
        compiler_params=_params(dimension_semantics=("arbitrary", "arbitrary")),
    )(a, b, *deps)


Z_TILE = 768
_Z_TILE_ORDER = ((0, 1, 2, 3, 4, 5, 6), (2, 0, 1, 6, 3, 4, 5), (4, 0, 5, 6, 1, 2, 3), (6, 2, 3, 4, 0, 1, 5))
_Z_EARLY_TILES = 4


def _z_proj(h, w_in_t, chip, first, count, z_prev, name, tr=1024):
    t = h.shape[0]

    def body(chip_ref, h_ref, w_ref, z_prev_ref, z_ref):
        z_ref[...] = _dot_nt(h_ref[...], w_ref[...])

    def tile(j, chip_ref):
        picked = 0
        for c, order in enumerate(_Z_TILE_ORDER):
            for k in range(count):
                picked = picked + jnp.where((chip_ref[0] == c) & (j == k), order[first + k], 0)
        return picked

    return pl.pallas_call(
        body, name=name,
        grid_spec=pltpu.PrefetchScalarGridSpec(
            num_scalar_prefetch=1, grid=(count, t // tr),
            in_specs=[pl.BlockSpec((tr, D_MODEL), lambda j, i, o: (i, 0)),
                      pl.BlockSpec((Z_TILE, D_MODEL), lambda j, i, o: (tile(j, o), 0)),
                      pl.BlockSpec(memory_space=pl.ANY)],
            out_specs=pl.BlockSpec((tr, Z_TILE), lambda j, i, o: (i, tile(j, o)))),
        out_shape=jax.ShapeDtypeStruct((t, D_IN), F32),
        input_output_aliases={3: 0},
        compiler_params=_params(dimension_semantics=("arbitrary", "arbitrary")),
    )(chip, h, w_in_t, z_prev)


def _modulation(device, c_all, w_ada, b_ada):
    def body(device_ref, c_ref, w_ref, b_ref, act_ref, mod_ref):
        cv = c_ref[...]
        act = cv * _sigmoid(cv)
        act_ref[...] = act
        mod_ref[...] = jnp.dot(act.astype(BF16), w_ref[...].astype(BF16), preferred_element_type=F32) + b_ref[...]

    whole = lambda a: pl.BlockSpec(a.shape, lambda i, device_ref: (0,) * a.ndim)
    return pl.pallas_call(
        body, name="modulation",
        grid_spec=pltpu.PrefetchScalarGridSpec(
            num_scalar_prefetch=1, grid=(1,),
            in_specs=[whole(c_all), whole(w_ada), pl.BlockSpec((1, W_ADA_SHARD), lambda i, device_ref: (0, device_ref[0]))],
            out_specs=(whole(c_all), pl.BlockSpec((N_DEV, W_ADA_SHARD), lambda i, device_ref: (0, 0)))),
        out_shape=(jax.ShapeDtypeStruct(c_all.shape, F32), jax.ShapeDtypeStruct((N_DEV, W_ADA_SHARD), F32)),
        compiler_params=_params(dimension_semantics=("arbitrary",)),
    )(device, c_all, w_ada, b_ada)


MOD_SHIFT, MOD_SCALE, MOD_GATE = 0, 1, 2


def _mod_spec(part, d):
    return pl.BlockSpec((1, d), lambda i: (0, part))


def _norm_z_proj_own(x, norm_g, mod, w_in_t, chip, tm=512):
    t, d = x.shape

    def body(chip_ref, x_ref, g_ref, sc_ref, sh_ref, w_ref, h_ref, z_ref):
        xv = x_ref[...]
        r = lax.rsqrt(jnp.mean(xv * xv, axis=-1, keepdims=True) + EPS)
        h = ((xv * r) * g_ref[...] * (1.0 + sc_ref[...]) + sh_ref[...]).astype(BF16)
        h_ref[...] = h
        z_ref[...] = _dot_nt(h, w_ref[...])

    def own_tile(chip_ref):
        picked = 0
        for c, order in enumerate(_Z_TILE_ORDER):
            picked = picked + jnp.where(chip_ref[0] == c, order[0], 0)
        return picked

    def row(part):
        return pl.BlockSpec((1, d), lambda i, o: (0, part))

    return pl.pallas_call(
        body, name="norm_z_proj_own",
        grid_spec=pltpu.PrefetchScalarGridSpec(
            num_scalar_prefetch=1, grid=(t // tm,),
            in_specs=[pl.BlockSpec((tm, d), lambda i, o: (i, 0)), row(0), row(MOD_SCALE), row(MOD_SHIFT),
                      pl.BlockSpec((Z_TILE, d), lambda i, o: (own_tile(o), 0))],
            out_specs=(pl.BlockSpec((tm, d), lambda i, o: (i, 0)),
                       pl.BlockSpec((tm, Z_TILE), lambda i, o: (i, own_tile(o))))),
        out_shape=(jax.ShapeDtypeStruct((t, d), BF16), jax.ShapeDtypeStruct((t, D_IN), F32)),
        compiler_params=_params(dimension_semantics=("arbitrary",)),
    )(chip, x, norm_g, mod, mod, w_in_t)


def _window_bias(block_index):
    s = lax.broadcasted_iota(jnp.int32, (2 * BLOCK, BLOCK), 0)
    t = lax.broadcasted_iota(jnp.int32, (2 * BLOCK, BLOCK), 1)
    valid = ((s < BLOCK) & (s > t) & (block_index > 0)) | ((s >= BLOCK) & ((s - BLOCK) <= t))
    bias = jnp.where(valid, 0.0, -jnp.inf).astype(F32)
    return jnp.concatenate([bias] * 8, axis=1)


def _heads_t(pair_blocks, g):
    top = lax.broadcasted_iota(jnp.int32, (BLOCK, BLOCK), 0) < HEAD_DIM
    zeros = jnp.zeros((HEAD_DIM, BLOCK), F32)
    tiles = []
    for blk in pair_blocks:
        tp = blk.T
        if g == 0:
            tiles += [jnp.where(top, tp, 0.0), jnp.concatenate([tp[HEAD_DIM:], zeros], axis=0)]
        else:
            tiles += [jnp.concatenate([zeros, tp[:HEAD_DIM]], axis=0), jnp.where(top, 0.0, tp)]
    return jnp.concatenate(tiles, axis=1)


def _pair_block(xt, p, g):
    r0 = HEAD_DIM * g
    even = xt[r0:r0 + HEAD_DIM, (2 * p) * BLOCK:(2 * p + 1) * BLOCK]
    odd = xt[r0:r0 + HEAD_DIM, (2 * p + 1) * BLOCK:(2 * p + 2) * BLOCK]
    return jnp.concatenate([even, odd], axis=0).T


def _softmax_t(scores_t, bias, sink):
    st = scores_t + bias
    m = jnp.maximum(jnp.max(st, axis=0, keepdims=True), sink)
    e = jnp.exp(st - m)
    es = jnp.exp(sink - m)
    inv = 1.0 / (jnp.sum(e, axis=0, keepdims=True) + es)
    return e * inv, es * inv


def _dot(a, b):
    return jnp.dot(a, b, preferred_element_type=F32)


def _dot_nt(a, b):
    return lax.dot_general(a, b, (((1,), (1,)), ((), ())), preferred_element_type=F32)


def _layer_norm_fwd(v):
    mu = jnp.mean(v, axis=-1, keepdims=True)
    xc = v - mu
    rstd = lax.rsqrt(jnp.mean(xc * xc, axis=-1, keepdims=True) + EPS)
    return xc * rstd, rstd


def _tril(transposed=False):
    t = lax.broadcasted_iota(jnp.int32, (BLOCK, BLOCK), 0)
    s = lax.broadcasted_iota(jnp.int32, (BLOCK, BLOCK), 1)
    return s >= t if transposed else t >= s


def _const_spec(shape):
    return pl.BlockSpec(shape, lambda i: (0,) * len(shape))


def _keys_values(z_ref, kvp):
    kvc = z_ref[:, SEG_KV:SEG_KV + 2 * D_KV]
    kk = jnp.concatenate([kvp[:, :D_KV], kvc[:, :D_KV]], axis=0)
    vv = jnp.concatenate([kvp[:, D_KV:], kvc[:, D_KV:]], axis=0)
    return kk, vv


MIXER_BLOCKS = 2


class _Rows:
    def __init__(self, ref, sub):
        self.ref, self.rows = ref, slice(sub * BLOCK, (sub + 1) * BLOCK)

    def __getitem__(self, idx):
        return self.ref[self.rows, idx[1]]

    def __setitem__(self, idx, value):
        self.ref[self.rows, idx[1]] = value


def _kv_before_spec(index):
    return pl.BlockSpec((BLOCK, 2 * D_KV),
                        lambda i: (jnp.maximum(MIXER_BLOCKS * index(i) - 1, 0), SEG_KV // (2 * D_KV)))


def _pair_cols(g, p, base=0):
    return slice(base + (4 * g + p) * 128, base + (4 * g + p + 1) * 128)


def _mixer_fwd(z, sink_rows, ln_g, ln_b, sgu_w, sgu_bt):
    t = z.shape[0]

    def body(z_all, kvp_ref, sink_ref, lng_ref, lnb_ref, w_ref, bt_ref, a_all, prob_ref, sink_prob_ref):
        kv_before = kvp_ref[...]
        for sub in range(MIXER_BLOCKS):
            z_ref, a_ref = _Rows(z_all, sub), _Rows(a_all, sub)
            one_block(z_ref, kv_before, MIXER_BLOCKS * pl.program_id(0) + sub, sink_ref, lng_ref, lnb_ref, w_ref,
                      bt_ref, a_ref, prob_ref.at[sub], sink_prob_ref.at[sub])
            kv_before = z_ref[:, SEG_KV:SEG_KV + 2 * D_KV]

    def one_block(z_ref, kv_before, block_index, sink_ref, lng_ref, lnb_ref, w_ref, bt_ref, a_ref, prob_ref,
                  sink_prob_ref):
        bias = _window_bias(block_index)
        kk, vv = _keys_values(z_ref, kv_before)
        kk_b, vvt_b = kk.astype(BF16), vv.T.astype(BF16)
        for g in range(2):
            qt = _heads_t([z_ref[:, _pair_cols(g, p, SEG_Q)] * ATTN_SCALE for p in range(4)], g).astype(BF16)
            prob, sink_prob = _softmax_t(_dot(kk_b, qt), bias, sink_ref[g])
            prob_b = prob.astype(BF16)
            prob_ref[g] = prob_b
            sink_prob_ref[g] = sink_prob
            ot = _dot(vvt_b, prob_b)
            for p in range(4):
                gate = z_ref[:, _pair_cols(g, p, SEG_GA)]
                a_ref[:, _pair_cols(g, p)] = (_pair_block(ot, p, g) * (gate * _sigmoid(gate))).astype(BF16)

        vhat, _ = _layer_norm_fwd(z_ref[:, SEG_VS:SEG_VS + D_SGU])
        vn = vhat * lng_ref[...] + lnb_ref[...]
        tril = _tril()
        for g in range(SGU_GROUPS):
            cols = slice(g * 128, (g + 1) * 128)
            wm = jnp.where(tril, w_ref[g], 0.0).astype(BF16)
            mixed = _dot(wm, vn[:, cols].astype(BF16)) + bt_ref[:, g:g + 1]
            gate = z_ref[:, SEG_GS + g * 128:SEG_GS + (g + 1) * 128]
            a_ref[:, D_ATTN + g * 128:D_ATTN + (g + 1) * 128] = (
                (z_ref[:, SEG_U + g * 128:SEG_U + (g + 1) * 128] * mixed) * (gate * _sigmoid(gate))).astype(BF16)

    rows = MIXER_BLOCKS * BLOCK
    return pl.pallas_call(
        body, name="mixer_fwd", grid=(t // rows,),
        in_specs=[pl.BlockSpec((rows, D_IN), lambda i: (i, 0)), _kv_before_spec(lambda i: i),
                  _const_spec((2, 1, 8 * BLOCK)), _const_spec((1, D_SGU)), _const_spec((1, D_SGU)),
                  _const_spec((SGU_GROUPS, BLOCK, BLOCK)), _const_spec((BLOCK, SGU_GROUPS))],
        out_specs=(pl.BlockSpec((rows, D_MODEL), lambda i: (i, 0)),
                   pl.BlockSpec((MIXER_BLOCKS, 2, 2 * BLOCK, 8 * BLOCK), lambda i: (i, 0, 0, 0)),
                   pl.BlockSpec((MIXER_BLOCKS, 2, 1, 8 * BLOCK), lambda i: (i, 0, 0, 0))),
        out_shape=(jax.ShapeDtypeStruct((t, D_MODEL), BF16),
                   jax.ShapeDtypeStruct((t // BLOCK, 2, 2 * BLOCK, 8 * BLOCK), BF16),
                   jax.ShapeDtypeStruct((t // BLOCK, 2, 1, 8 * BLOCK), F32)),
        compiler_params=_params(dimension_semantics=("arbitrary",)),
    )(z, z, sink_rows, ln_g, ln_b, sgu_w, sgu_bt)


def _mixer_bwd(z, da, probs, sink_probs, ln_g, ln_b, sgu_w, sgu_wt, sgu_bt, a, dy):
    t = z.shape[0]

    def body(z_all, kvp_ref, da_all, prob_ref, sink_prob_ref, lng_ref, lnb_ref, w_ref, wt_ref, bt_ref, a_ref, dy_ref,
             dz_all, dsink_ref, dw_ref, db_ref, dlng_ref, dlnb_ref, dw_out_ref, carry_ref, dsink_acc, dbt_acc):
        step = pl.program_id(0)
        dw_out_ref[...] = lax.dot_general(a_ref[...], dy_ref[...], (((0,), (0,)), ((), ())),
                                          preferred_element_type=F32).astype(BF16)

        @pl.when(step == 0)
        def _():
            carry_ref[...] = jnp.zeros_like(carry_ref)
            dsink_acc[...] = jnp.zeros_like(dsink_acc)
            dbt_acc[...] = jnp.zeros_like(dbt_acc)
            dw_ref[...] = jnp.zeros_like(dw_ref)
            dlng_ref[...] = jnp.zeros_like(dlng_ref)
            dlnb_ref[...] = jnp.zeros_like(dlnb_ref)

        carry = carry_ref[...]
        for sub in reversed(range(MIXER_BLOCKS)):
            kv_before = kvp_ref[...] if sub == 0 else _Rows(z_all, sub - 1)[:, SEG_KV:SEG_KV + 2 * D_KV]
            carry = one_block(_Rows(z_all, sub), kv_before, _Rows(da_all, sub), prob_ref.at[sub], sink_prob_ref.at[sub],
                              carry, lng_ref, lnb_ref, w_ref, wt_ref, bt_ref, _Rows(dz_all, sub),
                              dw_ref, dlng_ref, dlnb_ref, dsink_acc, dbt_acc)
        carry_ref[...] = carry

        @pl.when(step == ns - 1)
        def _():
            db_ref[...] = dbt_acc[...].T[:SGU_GROUPS]
            lane_row = lax.broadcasted_iota(jnp.int32, (1, 128), 1)
            d_sink = jnp.zeros((1, 128), F32)
            for g in range(2):
                acc = dsink_acc[g]
                for j in range(8):
                    head_sum = jnp.sum(acc[:, j * BLOCK:(j + 1) * BLOCK], axis=-1, keepdims=True)
                    d_sink = d_sink + jnp.where(lane_row == 8 * g + j, head_sum, 0.0)
            dsink_ref[...] = d_sink

    def one_block(z_ref, kv_before, da_ref, prob_ref, sink_prob_ref, carry, lng_ref, lnb_ref, w_ref, wt_ref, bt_ref,
                  dz_ref, dw_ref, dlng_ref, dlnb_ref, dsink_acc, dbt_acc):
        kk, vv = _keys_values(z_ref, kv_before)
        vv_b = vv.astype(BF16)
        kkt_b, vvt_b = kk.T.astype(BF16), vv.T.astype(BF16)
        dkk = jnp.zeros((2 * BLOCK, D_KV), F32)
        dvv = jnp.zeros((2 * BLOCK, D_KV), F32)
        for g in range(2):
            qt = _heads_t([z_ref[:, _pair_cols(g, p, SEG_Q)] * ATTN_SCALE for p in range(4)], g).astype(BF16)
            prob_b, sink_prob = prob_ref[g], sink_prob_ref[g]
            prob = prob_b.astype(F32)
            ot = _dot(vvt_b, prob_b)
            gates = [z_ref[:, _pair_cols(g, p, SEG_GA)] for p in range(4)]
            sig = [_sigmoid(gt) for gt in gates]
            d_attn = [da_ref[:, _pair_cols(g, p)] for p in range(4)]
            d_ot = _heads_t([d_attn[p] * (gates[p] * sig[p]) for p in range(4)], g).astype(BF16)
            d_prob = _dot(vv_b, d_ot)
            delta = jnp.sum(prob * d_prob, axis=0, keepdims=True)
            d_scores = (prob * (d_prob - delta)).astype(BF16)
            dsink_acc[g] -= sink_prob * delta
            d_qt = _dot(kkt_b, d_scores)
            dkk = dkk + _dot_nt(d_scores, qt)
            dvv = dvv + _dot_nt(prob_b, d_ot)
            for p in range(4):
                dz_ref[:, _pair_cols(g, p, SEG_Q)] = (_pair_block(d_qt, p, g) * ATTN_SCALE).astype(BF16)
                d_silu = sig[p] * (1.0 + gates[p] * (1.0 - sig[p]))
                dz_ref[:, _pair_cols(g, p, SEG_GA)] = (d_attn[p] * _pair_block(ot, p, g) * d_silu).astype(BF16)
        d_kv = jnp.concatenate([dkk, dvv], axis=1)
        dz_ref[:, SEG_KV:SEG_KV + 2 * D_KV] = (d_kv[BLOCK:] + carry).astype(BF16)

        vhat, rstd = _layer_norm_fwd(z_ref[:, SEG_VS:SEG_VS + D_SGU])
        lng = lng_ref[...]
        vn = vhat * lng + lnb_ref[...]
        tril, triu = _tril(), _tril(transposed=True)
        lane = lax.broadcasted_iota(jnp.int32, (BLOCK, 128), 1)
        d_bt = jnp.zeros((BLOCK, 128), F32)
        d_vn = []
        for g in range(SGU_GROUPS):
            cols = slice(g * 128, (g + 1) * 128)
            wm = jnp.where(tril, w_ref[g], 0.0).astype(BF16)
            wmt = jnp.where(triu, wt_ref[g], 0.0).astype(BF16)
            vn_g = vn[:, cols].astype(BF16)
            mixed = _dot(wm, vn_g) + bt_ref[:, g:g + 1]
            gate = z_ref[:, SEG_GS + g * 128:SEG_GS + (g + 1) * 128]
            u = z_ref[:, SEG_U + g * 128:SEG_U + (g + 1) * 128]
            d_out = da_ref[:, D_ATTN + g * 128:D_ATTN + (g + 1) * 128]
            sg = _sigmoid(gate)
            d_um = d_out * (gate * sg)
            dz_ref[:, SEG_U + g * 128:SEG_U + (g + 1) * 128] = (d_um * mixed).astype(BF16)
            dz_ref[:, SEG_GS + g * 128:SEG_GS + (g + 1) * 128] = (
                d_out * (u * mixed) * (sg * (1.0 + gate * (1.0 - sg)))).astype(BF16)
            d_mixed = d_um * u
            d_mixed_b = d_mixed.astype(BF16)
            dw_ref[g] += jnp.where(tril, _dot_nt(d_mixed_b, vn_g), 0.0)
            d_bt = d_bt + jnp.where(lane == g, jnp.sum(d_mixed, axis=-1, keepdims=True), 0.0)
            d_vn.append(_dot(wmt, d_mixed_b))
        dbt_acc[...] += d_bt
        d_vn = jnp.concatenate(d_vn, axis=1)
        dlng_ref[...] += jnp.sum(d_vn * vhat, axis=0, keepdims=True)
        dlnb_ref[...] += jnp.sum(d_vn, axis=0, keepdims=True)
        d_vhat = d_vn * lng
        d_v = rstd * (d_vhat - jnp.mean(d_vhat, axis=-1, keepdims=True)
                      - vhat * jnp.mean(d_vhat * vhat, axis=-1, keepdims=True))
        dz_ref[:, SEG_VS:SEG_VS + D_SGU] = d_v.astype(BF16)
        return d_kv[:BLOCK]

    rows = MIXER_BLOCKS * BLOCK
    ns = t // rows
    rev = lambda i: ns - 1 - i
    tile = a.shape[1] // ns
    assert tile % 128 == 0
    return pl.pallas_call(
        body, name="mixer_bwd", grid=(ns,),
        in_specs=[pl.BlockSpec((rows, D_IN), lambda i: (rev(i), 0)), _kv_before_spec(rev),
                  pl.BlockSpec((rows, D_MODEL), lambda i: (rev(i), 0)),
                  pl.BlockSpec((MIXER_BLOCKS, 2, 2 * BLOCK, 8 * BLOCK), lambda i: (rev(i), 0, 0, 0)),
                  pl.BlockSpec((MIXER_BLOCKS, 2, 1, 8 * BLOCK), lambda i: (rev(i), 0, 0, 0)),
                  _const_spec((1, D_SGU)), _const_spec((1, D_SGU)),
                  _const_spec((SGU_GROUPS, BLOCK, BLOCK)), _const_spec((SGU_GROUPS, BLOCK, BLOCK)),
                  _const_spec((BLOCK, SGU_GROUPS)), pl.BlockSpec((t, tile), lambda i: (0, i)),
                  pl.BlockSpec(dy.shape, lambda i: (0, 0), pipeline_mode=pl.Buffered(1))],
        out_specs=(pl.BlockSpec((rows, D_IN), lambda i: (rev(i), 0)), _const_spec((1, 128)),
                   _const_spec((SGU_GROUPS, BLOCK, BLOCK)), _const_spec((SGU_GROUPS, BLOCK)),
                   _const_spec((1, D_SGU)), _const_spec((1, D_SGU)),
                   pl.BlockSpec((tile, dy.shape[1]), lambda i: (i, 0))),
        out_shape=(jax.ShapeDtypeStruct((t, D_IN), BF16), jax.ShapeDtypeStruct((1, 128), F32),
                   jax.ShapeDtypeStruct((SGU_GROUPS, BLOCK, BLOCK), F32), jax.ShapeDtypeStruct((SGU_GROUPS, BLOCK), F32),
                   jax.ShapeDtypeStruct((1, D_SGU), F32), jax.ShapeDtypeStruct((1, D_SGU), F32),
                   jax.ShapeDtypeStruct((a.shape[1], dy.shape[1]), BF16)),
        scratch_shapes=[pltpu.VMEM((BLOCK, 2 * D_KV), F32), pltpu.VMEM((2, 1, 8 * BLOCK), F32),
                        pltpu.VMEM((BLOCK, 128), F32)],
        compiler_params=_params(dimension_semantics=("arbitrary",)),
    )(z, z, da, probs, sink_probs, ln_g, ln_b, sgu_w, sgu_wt, sgu_bt, a, dy)


def _out_proj_head(a, w_out_full, x, target, mod, final_g, tm=256):
    t, d = x.shape

    def body(a_ref, w_ref, x_ref, tg_ref, gate_ref, fg_ref, dx2_ref, dy_ref, da_ref, loss_ref, dfg_ref, dgate_ref):
        @pl.when(pl.program_id(0) == 0)
        def _():
            loss_ref[...] = jnp.zeros_like(loss_ref)
            dfg_ref[...] = jnp.zeros_like(dfg_ref)
            dgate_ref[...] = jnp.zeros_like(dgate_ref)

        yv, gate, fg = _dot(a_ref[...], w_ref[...]), gate_ref[...], fg_ref[...]
        x2 = x_ref[...] + gate * yv
        r2 = lax.rsqrt(jnp.mean(x2 * x2, axis=-1, keepdims=True) + EPS)
        nrm = x2 * r2
        err = nrm * fg - tg_ref[...]
        loss_ref[...] += 0.5 * jnp.sum(jnp.mean(err * err, axis=-1, keepdims=True), axis=0, keepdims=True)
        fg_d = fg * (1.0 / d)
        err_nrm = err * nrm
        dfg_ref[...] += jnp.sum(err_nrm, axis=0, keepdims=True) * (1.0 / d)
        d_nrm = err * fg_d
        dx2 = r2 * (d_nrm - nrm * jnp.mean(err_nrm * fg_d, axis=-1, keepdims=True))
        dx2_ref[...] = dx2
        dgate_ref[...] += jnp.sum(dx2 * yv, axis=0, keepdims=True)
        dy = (dx2 * gate).astype(BF16)
        dy_ref[...] = dy
        da_ref[...] = _dot_nt(dy, w_ref[...])

    blk = pl.BlockSpec((tm, d), lambda i: (i, 0))
    a_blk = pl.BlockSpec((tm, a.shape[1]), lambda i: (i, 0))
    row = _const_spec((1, d))
    whole = pl.BlockSpec(w_out_full.shape, lambda i: (0, 0), pipeline_mode=pl.Buffered(1))
    return pl.pallas_call(
        body, name="out_proj_head", grid=(t // tm,),
        in_specs=[a_blk, whole, blk, blk, _mod_spec(MOD_GATE, d), row],
        out_specs=(blk, blk, a_blk, _const_spec((1, 128)), row, row),
        out_shape=(jax.ShapeDtypeStruct((t, d), F32), jax.ShapeDtypeStruct((t, d), BF16),
                   jax.ShapeDtypeStruct(a.shape, F32), jax.ShapeDtypeStruct((1, 128), F32),
                   jax.ShapeDtypeStruct((1, d), F32), jax.ShapeDtypeStruct((1, d), F32)),
        compiler_params=_params(dimension_semantics=("arbitrary",)),
    )(a, w_out_full, x, target, mod, final_g)


def _z_proj_bwd_norm(dz, w_in_t, x, dx2, norm_g, mod, dep, tm=256):
    t, d = x.shape

    def body(dz_ref, w_ref, x_ref, dx2_ref, g_ref, sc_ref, dep_ref, gx_ref, dshift_ref, dscale_ref, dg_ref):
        @pl.when(pl.program_id(0) == 0)
        def _():
            dshift_ref[...] = jnp.zeros_like(dshift_ref)
            dscale_ref[...] = jnp.zeros_like(dscale_ref)
            dg_ref[...] = jnp.zeros_like(dg_ref)

        dh, xv, g = _dot(dz_ref[...], w_ref[...]), x_ref[...], g_ref[...]
        one_plus = 1.0 + sc_ref[...]
        r = lax.rsqrt(jnp.mean(xv * xv, axis=-1, keepdims=True) + EPS)
        xn = xv * r
        gain = one_plus * g
        dh_xn = dh * xn
        dh_xn_sum = jnp.sum(dh_xn, axis=0, keepdims=True)
        dshift_ref[...] += jnp.sum(dh, axis=0, keepdims=True)
        dscale_ref[...] += dh_xn_sum * g
        dg_ref[...] += dh_xn_sum * one_plus
        d_xn = dh * gain
        gx_ref[...] = dx2_ref[...] + r * (d_xn - xn * jnp.mean(dh_xn * gain, axis=-1, keepdims=True))

    blk = pl.BlockSpec((tm, d), lambda i: (i, 0))
    row = _const_spec((1, d))
    whole = pl.BlockSpec(w_in_t.shape, lambda i: (0, 0), pipeline_mode=pl.Buffered(1))
    return pl.pallas_call(
        body, name="z_proj_bwd_norm", grid=(t // tm,),
        in_specs=[pl.BlockSpec((tm, dz.shape[1]), lambda i: (i, 0)), whole, blk, blk, row, _mod_spec(MOD_SCALE, d),
                  _const_spec((8, 128))],
        out_specs=(blk, row, row, row),
        out_shape=(jax.ShapeDtypeStruct((t, d), F32),) + (jax.ShapeDtypeStruct((1, d), F32),) * 3,
        compiler_params=_params(dimension_semantics=("arbitrary",)),
    )(dz, w_in_t, x, dx2, norm_g, mod, dep)


def _adamw(w, g, m, v):
    m = ADAM_B1 * m + (1.0 - ADAM_B1) * g
    v = ADAM_B2 * v + (1.0 - ADAM_B2) * (g * g)
    m_hat = m / (1.0 - ADAM_B1 ** ADAM_STEP)
    v_hat = v / (1.0 - ADAM_B2 ** ADAM_STEP)
    delta = -ADAM_LR * (m_hat / (jnp.sqrt(v_hat) + ADAM_EPS) + ADAM_WD * w)
    return delta, m, v


def _relay_sum(device, blocks, land_pair, land_first, tr):
    _, r, c = blocks.shape

    def body(device_ref, a_ref, b_ref, c_ref, o_ref):
        o_ref[...] = (a_ref[...].astype(F32) + b_ref[...].astype(F32) + c_ref[...].astype(F32)).astype(BF16)

    second = pl.BlockSpec((None, tr, c), lambda i, device_ref: (1, i, 0))
    return pl.pallas_call(
        body, name="w_in_grad_relay_sum",
        grid_spec=pltpu.PrefetchScalarGridSpec(
            num_scalar_prefetch=1, grid=(r // tr,),
            in_specs=[pl.BlockSpec((None, tr, c), lambda i, device_ref: (device_ref[0], i, 0)), second, second],
            out_specs=pl.BlockSpec((tr, c), lambda i, device_ref: (i, 0))),
        out_shape=jax.ShapeDtypeStruct((r, c), BF16),
        compiler_params=_params(dimension_semantics=("arbitrary",)),
    )(device, blocks, land_pair, land_first)


def _adam_from_chips(chip, pair, landed, w, m, v, name, tc):
    _, r, c = pair.shape
    n = len(landed)

    def body(chip_ref, own_ref, *refs):
        w_ref, m_ref, v_ref, g_ref, d_ref, nm_ref, nv_ref = refs[n:]
        g = own_ref[...].astype(F32)
        for k in range(n):
            g = g + refs[k][...].astype(F32)
        g_ref[...] = g
        d_ref[...], nm_ref[...], nv_ref[...] = _adamw(w_ref[...], g, m_ref[...], v_ref[...])

    def landed_spec(index):
        return pl.BlockSpec((None, r, tc), lambda i, chip_ref: (index, 0, i))

    blk = pl.BlockSpec((r, tc), lambda i, chip_ref: (0, i))
    return pl.pallas_call(
        body, name=name,
        grid_spec=pltpu.PrefetchScalarGridSpec(
            num_scalar_prefetch=1, grid=(c // tc,),
            in_specs=[pl.BlockSpec((None, r, tc), lambda i, chip_ref: (chip_ref[0], 0, i))]
            + [landed_spec(index) for _, index in landed] + [blk, blk, blk],
            out_specs=(blk,) * 4),
        out_shape=(jax.ShapeDtypeStruct((r, c), F32),) * 4,
        compiler_params=_params(dimension_semantics=("arbitrary",)),
    )(chip, pair, *[array for array, _ in landed], w, m, v)


def _adam_w_ada(device, act_t, dmod_all, w, m, v, tr=512):
    r, c = w.shape

    def body(device_ref, a_ref, dm_ref, w_ref, m_ref, v_ref, g_ref, d_ref, nm_ref, nv_ref):
        g = _dot(a_ref[...].astype(BF16), dm_ref[...].astype(BF16))
        g_ref[...] = g
        d_ref[...], nm_ref[...], nv_ref[...] = _adamw(w_ref[...], g, m_ref[...], v_ref[...])

    blk = pl.BlockSpec((tr, c), lambda i, device_ref: (i, 0))
    return pl.pallas_call(
        body, name="adam_w_ada",
        grid_spec=pltpu.PrefetchScalarGridSpec(
            num_scalar_prefetch=1, grid=(r // tr,),
            in_specs=[pl.BlockSpec((tr, N_DEV), lambda i, device_ref: (i, 0)),
                      pl.BlockSpec((N_DEV, c), lambda i, device_ref: (0, device_ref[0])), blk, blk, blk],
            out_specs=(blk,) * 4),
        out_shape=(jax.ShapeDtypeStruct((r, c), F32),) * 4,
        compiler_params=_params(dimension_semantics=("arbitrary",)),
    )(device, act_t, dmod_all, w, m, v)


def _pack_small(d_shift, d_scale, d_gate, d_norm_g, d_final_g, d_ln_g, d_ln_b, loss, d_sinks, d_sgu_b):
    def body(shift_ref, scale_ref, gate_ref, ng_ref, fg_ref, lng_ref, lnb_ref, loss_ref, sink_ref, b_ref, o_ref):
        o_ref[...] = jnp.zeros_like(o_ref)
        o_ref[ROW_SHIFT:ROW_SHIFT + 1, :] = shift_ref[...]
        o_ref[ROW_SCALE:ROW_SCALE + 1, :] = scale_ref[...]
        o_ref[ROW_GATE:ROW_GATE + 1, :] = gate_ref[...]
        o_ref[ROW_NORM_G:ROW_NORM_G + 1, :] = ng_ref[...]
        o_ref[ROW_FINAL_G:ROW_FINAL_G + 1, :] = fg_ref[...]
        o_ref[ROW_LN:ROW_LN + 1, 0:D_SGU] = lng_ref[...]
        o_ref[ROW_LN:ROW_LN + 1, D_SGU:2 * D_SGU] = lnb_ref[...]
        o_ref[ROW_MISC:ROW_MISC + 1, 0:128] = loss_ref[...]
        o_ref[ROW_MISC:ROW_MISC + 1, 128:256] = sink_ref[...]
        o_ref[ROW_SGU_B:ROW_SGU_B + SGU_GROUPS, 0:BLOCK] = b_ref[...]

    return pl.pallas_call(
        body, name="pack_small", out_shape=jax.ShapeDtypeStruct((SMALL_ROWS, D_MODEL), F32),
        compiler_params=_params(),
    )(d_shift, d_scale, d_gate, d_norm_g, d_final_g, d_ln_g, d_ln_b, loss, d_sinks, d_sgu_b)


_SMALL_NAMES = ("norm_g", "b_ada", "attn_sinks", "sgu_ln_g", "sgu_ln_b", "sgu_w", "sgu_b", "final_g")


def _adam_small(partials, d_sgu_w_all, weights, moments_m, moments_v):
    names = _SMALL_NAMES
    k = len(names)

    def body(*refs):
        p_ref, sw_ref = refs[0], refs[1]
        w_refs, m_refs, v_refs = refs[2:2 + k], refs[2 + k:2 + 2 * k], refs[2 + 2 * k:2 + 3 * k]
        loss_ref, dmod_ref = refs[2 + 3 * k], refs[3 + 3 * k]
        out_refs = refs[4 + 3 * k:4 + 7 * k]
        sum_ref = refs[4 + 7 * k]
        total = p_ref[0]
        for j in range(1, N_DEV):
            total = total + p_ref[j]
        sum_ref[...] = total
        for j in range(N_DEV):
            for part, row in enumerate((ROW_SHIFT, ROW_SCALE, ROW_GATE)):
                dmod_ref[j:j + 1, part * D_MODEL:(part + 1) * D_MODEL] = p_ref[j, row:row + 1, :]
        loss_ref[...] = sum_ref[ROW_MISC:ROW_MISC + 1, 0:1]
        d_sgu_w = sw_ref[0]
        for j in range(1, N_DEV):
            d_sgu_w = d_sgu_w + sw_ref[j]
        grads = {
            "norm_g": sum_ref[ROW_NORM_G:ROW_NORM_G + 1, :],
            "b_ada": jnp.concatenate([sum_ref[r:r + 1, :] for r in (ROW_SHIFT, ROW_SCALE, ROW_GATE)], axis=1),
            "attn_sinks": sum_ref[ROW_MISC:ROW_MISC + 1, 128:128 + N_Q_HEADS],
            "sgu_ln_g": sum_ref[ROW_LN:ROW_LN + 1, 0:D_SGU],
            "sgu_ln_b": sum_ref[ROW_LN:ROW_LN + 1, D_SGU:2 * D_SGU],
            "sgu_w": d_sgu_w[None],
            "sgu_b": sum_ref[ROW_SGU_B:ROW_SGU_B + SGU_GROUPS, 0:BLOCK][None],
            "final_g": sum_ref[ROW_FINAL_G:ROW_FINAL_G + 1, :],
        }
        for i, name in enumerate(names):
            g = grads[name]
            delta, m, v = _adamw(w_refs[i][...], g, m_refs[i][...], v_refs[i][...])
            out_refs[4 * i][...] = g
            out_refs[4 * i + 1][...] = delta
            out_refs[4 * i + 2][...] = m
            out_refs[4 * i + 3][...] = v

    shapes = [jax.ShapeDtypeStruct((1, 1), F32), jax.ShapeDtypeStruct((N_DEV, 3 * D_MODEL), F32)]
    for name in names:
        shapes += [jax.ShapeDtypeStruct(weights[name].shape, F32)] * 4
    outs = pl.pallas_call(
        body, name="adam_small", out_shape=tuple(shapes),
        scratch_shapes=[pltpu.VMEM((SMALL_ROWS, D_MODEL), F32)],
        compiler_params=_params(),
    )(partials, d_sgu_w_all, *[weights[n] for n in names], *[moments_m[n] for n in names],
      *[moments_v[n] for n in names])
    return outs[0], outs[1], {name: outs[2 + 4 * i:6 + 4 * i] for i, name in enumerate(names)}


def kernel(x, c, norm_g, w_ada, b_ada, w_in, attn_sinks, sgu_ln_g, sgu_ln_b, sgu_w, sgu_b, w_out, final_g, loss_target, m_norm_g, m_w_ada, m_b_ada, m_w_in, m_attn_sinks, m_sgu_ln_g, m_sgu_ln_b, m_sgu_w, m_sgu_b, m_w_out, m_final_g, v_norm_g, v_w_ada, v_b_ada, v_w_in, v_attn_sinks, v_sgu_ln_g, v_sgu_ln_b, v_sgu_w, v_sgu_b, v_w_out, v_final_g):
    xi, yi, ci = _place()
    me = 4 * xi + 2 * yi + ci
    x2d, target = x[0], loss_target[0]
    t = x2d.shape[0]

    core = ci.astype(jnp.int32).reshape(1)
    chip = (2 * xi + yi).astype(jnp.int32).reshape(1)

    first = _own_block_copies(_first_targets)
    first_flight = _start_copies([_with_own_slot(w_in[0].T.astype(BF16), me)], first, 2, core, "gather_w_in_start")

    c_all = _all_gather_small(c.reshape(8, 256), "gather_c", first_flight[3]).reshape(N_DEV, D_MODEL)
    device = me.astype(jnp.int32).reshape(1)
    c_act, mod_part = _modulation(device, c_all, w_ada[0], b_ada)
    mod_all = _all_gather_small(mod_part, "gather_mod")

    across = _wait_then_start(first_flight, lambda *a: first(*a)[1:], _second_axis_stage_copies, 3, mod_all,
                              "gather_w_in_second_axis_stage")
    mod = lax.dynamic_index_in_dim(mod_all, me, axis=1, keepdims=False).reshape(1, 3 * D_MODEL)
    mod = mod + across[3][0, 0]

    w_in_pair = _wait_copies((first_flight[0], first_flight[1], across[2], None), lambda *a: first(*a)[:1], mod,
                             "gather_w_in_sibling_wait")
    h, z_own = _norm_z_proj_own(x2d, norm_g, mod, w_in_pair[0].reshape(D_IN, D_MODEL), chip)
    w_out_early = _own_block_copies(lambda x, y, c: [(x, y, 1 - c), (*_second_axis_chip(x, y, c), c)])
    w_out_late = _own_block_copies(lambda x, y, c: [(*_first_axis_chip(x, y, c), c), (1 - x, 1 - y, c)])
    forward = _wait_then_start(
        (across[0], across[1], w_in_pair, None), lambda *a: _second_axis_stage_copies(*a)[:1],
        lambda refs, s, r: _second_axis_forward_copies(refs[:1], s, r) + _group(w_out_early, 1, 1, 1)(refs, s, r),
        3, z_own, "gather_w_in_second_axis_forward", more_bufs=[_with_own_slot(w_out[0].astype(BF16), me)])
    w_in_most = _wait_copies((across[0], across[1], forward[2][:1], None),
                             lambda *a: _second_axis_stage_copies(*a)[1:2], z_own, "gather_w_in_first_forward_wait")
    w_in_most = _wait_copies((forward[0], forward[1], w_in_most, None), _second_axis_forward_copies, z_own,
                             "gather_w_in_second_forward_wait")
    z_early = _z_proj(h, w_in_most[0].reshape(D_IN, D_MODEL), chip, 1, _Z_EARLY_TILES - 1, z_own, "z_proj_early")
    last = _wait_then_start(
        (across[0], across[1], [w_in_most[0], forward[2][1]], None), lambda *a: _second_axis_stage_copies(*a)[2:],
        lambda refs, s, r: _diagonal_forward_copies(refs[:1], s, r) + _group(w_out_late, 1, 1, 1)(refs, s, r),
        3, z_early, "gather_w_in_last_stage")
    w_in_all = _wait_copies((last[0], last[1], last[2][:1], None), _diagonal_forward_copies, z_early,
                            "gather_w_in_last_wait")[0]
    w_in_t = w_in_all.reshape(D_IN, D_MODEL)
    z = _z_proj(h, w_in_t, chip, _Z_EARLY_TILES, 7 - _Z_EARLY_TILES, z_early, "z_proj_late")
    w_out_half = _wait_copies((forward[0], forward[1], last[2][1:], None), _group(w_out_early, 0, 1, 1), z,
                              "gather_w_out_early_wait")
    w_out_flight = _wait_then_start((last[0], last[1], w_out_half, None), _group(w_out_late, 0, 1, 1),
                                    _forward_copies, 3, z, "gather_w_out_forward_stage")
    sink_rows = jnp.repeat(attn_sinks.reshape(N_Q_HEADS), BLOCK).reshape(2, 1, 8 * BLOCK)
    sgu_bt = sgu_b[0].T
    a, probs, sink_probs = _mixer_fwd(z, sink_rows + w_out_flight[3][0, 0], sgu_ln_g, sgu_ln_b, sgu_w[0], sgu_bt)
    w_out_all = _wait_copies(w_out_flight, _forward_copies, a, "gather_w_out_forward_wait")[0]
    w_out_full = w_out_all.reshape(D_MODEL, D_MODEL)
    final_g_row = final_g.reshape(1, D_MODEL)
    dx2, dy, da, loss_part, d_final_g, d_gate = _out_proj_head(a, w_out_full, x2d, target, mod, final_g_row)

    dz, d_sinks, d_sgu_w, d_sgu_b, d_ln_g, d_ln_b, dw_out = _mixer_bwd(
        z, da, probs, sink_probs, sgu_ln_g, sgu_ln_b, sgu_w[0], jnp.swapaxes(sgu_w[0], 1, 2), sgu_bt, a, dy)
    pair_out = _pair_reduce(dw_out.reshape(4, 2, W_OUT_SHARD, D_MODEL), _every_chip, "w_out_grad_pair_reduce",
                            W_OUT_SHARD // 2)
    sgu_w_to_all = _group(_own_block_copies(_all_others), 2, 1, 3)
    both = _start_copies(
        [pair_out, lax.empty((3, W_OUT_SHARD, D_MODEL), BF16), _with_own_slot(d_sgu_w, me)],
        lambda refs, s, r: _chip_copies(refs[:2], s, r) + sgu_w_to_all(refs, s, r), 3 + N_DEV - 1, core,
        "w_out_grad_chip_and_sgu_w_gather_start")
    out_flight, sgu_w_flight = (both[0], both[1], both[2][:2], None), (both[0], both[1], both[2][2:], None)
    dw_in_t = _matmul(dz, h, "tn", BF16, 768, D_MODEL, "w_in_grad", dep=both[3])
    pair_in = _pair_reduce(dw_in_t.reshape(4, 2, W_IN_SHARD, D_MODEL), _first_hop_chips, "w_in_grad_pair_reduce",
                           W_IN_SHARD // 3)
    first_hop = lambda refs, s, r: _first_hop_copies(refs[:2], s, r) + _group(_late_pair_copies, 2, 2, 2)(refs, s, r)
    hop1 = _start_copies(
        [pair_in, lax.empty((2, W_IN_SHARD, D_MODEL), BF16), dw_in_t.reshape(N_DEV, W_IN_SHARD, D_MODEL),
         lax.empty((2, W_IN_SHARD, D_MODEL), BF16)], first_hop, 4, core, "w_in_grad_first_hop_start")
    grad_x, d_shift, d_scale, d_norm_g = _z_proj_bwd_norm(dz, w_in_t, x2d, dx2, norm_g, mod, hop1[3])

    partial = _pack_small(d_shift, d_scale, d_gate, d_norm_g, d_final_g, d_ln_g, d_ln_b, loss_part, d_sinks, d_sgu_b)
    small_flight = _start_copies([_with_own_slot(partial, me)], _own_block_copies(_all_others), N_DEV - 1, core,
                                 "small_grad_gather_start")
    _, land_first, dw_in_t, land_pair = _wait_copies(hop1, first_hop, small_flight[3], "w_in_grad_first_hop_wait")
    second_device = (4 * ((xi + ci) % 2) + 2 * ((yi + 1 - ci) % 2) + ci).astype(jnp.int32).reshape(1)
    relay = _relay_sum(second_device, dw_in_t, land_pair, land_first, W_IN_SHARD // 3)
    hop2 = _start_copies([relay, lax.empty((1, W_IN_SHARD, D_MODEL), BF16)], _second_hop_copies, 1, core,
                         "w_in_grad_second_hop_start")
    pair_out, land_out = _wait_copies(out_flight, _chip_copies, hop2[3], "w_out_grad_chip_wait")
    big = {"w_out": _adam_from_chips(chip, pair_out, [(land_out, k) for k in range(3)], w_out[0], m_w_out[0],
                                     v_w_out[0], "adam_w_out", 1024)}
    partial_all = _wait_copies(small_flight, _own_block_copies(_all_others), big["w_out"][0],
                               "small_grad_gather_wait")[0]
    d_sgu_w_all = _wait_copies(sgu_w_flight, _group(_own_block_copies(_all_others), 0, 1, 3), partial_all,
                               "sgu_w_grad_gather_wait")[0]
    weights = {"norm_g": norm_g, "b_ada": b_ada, "attn_sinks": attn_sinks, "sgu_ln_g": sgu_ln_g,
               "sgu_ln_b": sgu_ln_b, "sgu_w": sgu_w, "sgu_b": sgu_b, "final_g": final_g_row}
    moments_m = {"norm_g": m_norm_g, "b_ada": m_b_ada, "attn_sinks": m_attn_sinks, "sgu_ln_g": m_sgu_ln_g,
                 "sgu_ln_b": m_sgu_ln_b, "sgu_w": m_sgu_w, "sgu_b": m_sgu_b,
                 "final_g": m_final_g.reshape(1, D_MODEL)}
    moments_v = {"norm_g": v_norm_g, "b_ada": v_b_ada, "attn_sinks": v_attn_sinks, "sgu_ln_g": v_sgu_ln_g,
                 "sgu_ln_b": v_sgu_ln_b, "sgu_w": v_sgu_w, "sgu_b": v_sgu_b,
                 "final_g": v_final_g.reshape(1, D_MODEL)}
    loss, dmod_all, small = _adam_small(partial_all, d_sgu_w_all, weights, moments_m, moments_v)
    small["final_g"] = tuple(o.reshape(D_MODEL) for o in small["final_g"])

    big["w_ada"] = _adam_w_ada(device, c_act.T, dmod_all, w_ada[0], m_w_ada[0], v_w_ada[0])
    _, land_second = _wait_copies(hop2, _second_hop_copies, big["w_ada"][0], "w_in_grad_second_hop_wait")
    big["w_in"] = tuple(o.T for o in _adam_from_chips(
        device, dw_in_t, [(land_pair, 0), (land_first, 0), (land_second, 0)], w_in[0].T, m_w_in[0].T, v_w_in[0].T,
        "adam_w_in", 512))
    order = ["norm_g", "w_ada", "b_ada", "w_in", "attn_sinks", "sgu_ln_g", "sgu_ln_b", "sgu_w", "sgu_b", "w_out",
             "final_g"]
    outs = [loss.reshape(()), grad_x[None]]
    for k in range(4):
        for name in order:
            outs.append(big[name][k][None] if name in big else small[name][k])
    return tuple(outs)
```

```python
import jax
import jax.numpy as jnp
from jax import lax
from jax.experimental import pallas as pl
from jax.experimental.pallas import tpu as pltpu

F32 = jnp.float32
BF16 = jnp.bfloat16
MESH = pl.DeviceIdType.MESH

N_DEV = 8
D_MODEL = 2048
HEAD_DIM = 64
D_ATTN = 1024
N_Q_HEADS = 16
D_KV = 128
BLOCK = 128
D_SGU = 1024
SGU_GROUPS = 8
D_IN = 5376
W_IN_SHARD = D_IN // N_DEV
W_OUT_SHARD = D_MODEL // N_DEV
W_ADA_SHARD = 3 * D_MODEL // N_DEV
EPS = 1e-6
ATTN_SCALE = 0.125

ADAM_LR = 0.001
ADAM_B1 = 0.9
ADAM_B2 = 0.999
ADAM_EPS = 1e-08
ADAM_WD = 0.01
ADAM_STEP = 10

SEG_Q, SEG_KV, SEG_GA, SEG_U, SEG_VS, SEG_GS = 0, 1024, 1280, 2304, 3328, 4352

VMEM_LIMIT = 56 * 1024 * 1024

ROW_SHIFT, ROW_SCALE, ROW_GATE, ROW_NORM_G, ROW_FINAL_G, ROW_LN, ROW_MISC, ROW_SGU_B = 0, 1, 2, 3, 4, 5, 6, 8
SMALL_ROWS = 16


def _params(**kw):
    return pltpu.CompilerParams(vmem_limit_bytes=VMEM_LIMIT, **kw)


def _sigmoid(x):
    return 0.5 * (jnp.tanh(0.5 * x) + 1.0)


def _place():
    return lax.axis_index("x"), lax.axis_index("y"), lax.axis_index("c")


def _every_chip(x, y, c):
    return [0, 1, 2, 3]


def _first_hop_chips(x, y, c):
    first = _first_axis_chip(x, y, c)
    return [2 * first[0] + first[1], 2 * (1 - x) + (1 - y)]


def _pair_reduce(blocks, chips, name, row_chunk):
    _, _, r, cols = blocks.shape
    n = len(chips(0, 0, 0))
    assert r % row_chunk == 0

    def body(in_ref, out_ref, land, own, summed, send_sems, recv_sems, own_sems, out_sems):
        x, y, c = _place()
        sends, loads, stores = [], [], []
        for m in range(n):
            cp = pltpu.make_async_remote_copy(
                src_ref=in_ref.at[chips(x, y, 1 - c)[m], 1 - c], dst_ref=land.at[m], send_sem=send_sems.at[m],
                recv_sem=recv_sems.at[m], device_id=(x, y, 1 - c), device_id_type=MESH)
            cp.start()
            sends.append(cp)
            ld = pltpu.make_async_copy(in_ref.at[chips(x, y, c)[m], c], own.at[m], own_sems.at[m])
            ld.start()
            loads.append(ld)
        for m in range(n):
            sends[m].wait_recv()
            loads[m].wait()
            for k in range(r // row_chunk):
                rows = slice(k * row_chunk, (k + 1) * row_chunk)
                summed[m, rows, :] = (own[m, rows, :].astype(F32) + land[m, rows, :].astype(F32)).astype(BF16)
            st = pltpu.make_async_copy(summed.at[m], out_ref.at[m], out_sems.at[m])
            st.start()
            stores.append(st)
        for m in range(n):
            sends[m].wait_send()
            stores[m].wait()

    spec = pl.BlockSpec(memory_space=pl.ANY)
    return pl.pallas_call(
        body, name=name, out_shape=jax.ShapeDtypeStruct((n, r, cols), BF16),
        in_specs=[spec], out_specs=spec,
        scratch_shapes=[pltpu.VMEM((n, r, cols), BF16), pltpu.VMEM((n, r, cols), BF16), pltpu.VMEM((n, r, cols), BF16),
                        pltpu.SemaphoreType.DMA((n,)), pltpu.SemaphoreType.DMA((n,)), pltpu.SemaphoreType.DMA((n,)),
                        pltpu.SemaphoreType.DMA((n,))],
        compiler_params=_params(),
    )(blocks)


_HBM = pl.BlockSpec(memory_space=pltpu.HBM)
_SEM = pl.BlockSpec(memory_space=pltpu.SEMAPHORE)
_EFFECT = pltpu.SideEffectType.DATAFLOW_SIDE_EFFECTING


def _start_copies(bufs, copies, n_copies, after, name):
    nb = len(bufs)

    def body(*refs):
        for cp in copies(refs[:nb], refs[nb + 1], refs[nb + 2]):
            cp.start()
        refs[-1][...] = jnp.zeros_like(refs[-1])

    out = pl.pallas_call(
        body, name=name,
        out_shape=(pltpu.SemaphoreType.DMA((n_copies,)), pltpu.SemaphoreType.DMA((n_copies,)),
                   *[pltpu.HBM(b.shape, b.dtype) for b in bufs], jax.ShapeDtypeStruct((8, 128), F32)),
        in_specs=(_HBM,) * nb + (pl.BlockSpec(memory_space=pl.ANY),),
        out_specs=(_SEM, _SEM) + (_HBM,) * nb + (pl.BlockSpec(memory_space=pltpu.VMEM),),
        input_output_aliases={i: 2 + i for i in range(nb)},
        compiler_params=pltpu.CompilerParams(has_side_effects=_EFFECT),
    )(*[pltpu.with_memory_space_constraint(b, pltpu.HBM) for b in bufs], after)
    return out[0], out[1], list(out[2:2 + nb]), out[-1]


def _wait_copies(flight, copies, after, name):
    send_sems, recv_sems, bufs, _ = flight
    nb = len(bufs)

    def body(*refs):
        for cp in copies(refs[:nb], refs[nb], refs[nb + 1]):
            cp.wait_send()
            cp.wait_recv()

    return pl.pallas_call(
        body, name=name,
        out_shape=tuple(pltpu.HBM(b.shape, b.dtype) for b in bufs),
        in_specs=(_HBM,) * nb + (_SEM, _SEM, pl.BlockSpec(memory_space=pl.ANY)), out_specs=(_HBM,) * nb,
        input_output_aliases={i: i for i in range(nb)},
        compiler_params=pltpu.CompilerParams(has_side_effects=_EFFECT),
    )(*bufs, send_sems, recv_sems, after)


class _From:
    def __init__(self, sems, offset):
        self.sems, self.offset = sems, offset

    @property
    def at(self):
        return self

    def __getitem__(self, k):
        return self.sems.at[k + self.offset]


def _group(copies, first_buf, n_bufs, offset):
    def grouped(refs, send_sems, recv_sems):
        return copies(refs[first_buf:first_buf + n_bufs], _From(send_sems, offset), _From(recv_sems, offset))
    return grouped


def _wait_then_start(flight, waited, started, n_started, after, name, more_bufs=()):
    old_send, old_recv, bufs, _ = flight
    bufs = list(bufs) + [pltpu.with_memory_space_constraint(b, pltpu.HBM) for b in more_bufs]
    nb = len(bufs)

    def body(*refs):
        for cp in waited(refs[:nb], refs[nb], refs[nb + 1]):
            cp.wait_send()
            cp.wait_recv()
        for cp in started(refs[:nb], refs[nb + 3], refs[nb + 4]):
            cp.start()
        refs[-1][...] = jnp.zeros_like(refs[-1])

    out = pl.pallas_call(
        body, name=name,
        out_shape=(pltpu.SemaphoreType.DMA((n_started,)), pltpu.SemaphoreType.DMA((n_started,)),
                   *[pltpu.HBM(b.shape, b.dtype) for b in bufs], jax.ShapeDtypeStruct((8, 128), F32)),
        in_specs=(_HBM,) * nb + (_SEM, _SEM, pl.BlockSpec(memory_space=pl.ANY)),
        out_specs=(_SEM, _SEM) + (_HBM,) * nb + (pl.BlockSpec(memory_space=pltpu.VMEM),),
        input_output_aliases={i: 2 + i for i in range(nb)},
        compiler_params=pltpu.CompilerParams(has_side_effects=_EFFECT),
    )(*bufs, old_send, old_recv, after)
    return out[0], out[1], list(out[2:2 + nb]), out[-1]


def _late_pair_copies(refs, send_sems, recv_sems):
    blocks_ref, land_ref = refs
    x, y, c = _place()
    first = _first_axis_chip(x, y, c)
    devices = [4 * x + 2 * y + 1 - c, 4 * first[0] + 2 * first[1] + 1 - c]
    return [pltpu.make_async_remote_copy(
        src_ref=blocks_ref.at[devices[k]], dst_ref=land_ref.at[k], send_sem=send_sems.at[k], recv_sem=recv_sems.at[k],
        device_id=(x, y, 1 - c), device_id_type=MESH) for k in range(2)]


def _chip_copies(refs, send_sems, recv_sems):
    pair_ref, land_ref = refs
    x, y, c = _place()
    chips = [(1 - x, y), (x, 1 - y), (1 - x, 1 - y)]
    return [pltpu.make_async_remote_copy(
        src_ref=pair_ref.at[2 * chip[0] + chip[1]], dst_ref=land_ref.at[k],
        send_sem=send_sems.at[k], recv_sem=recv_sems.at[k],
        device_id=(*chip, c), device_id_type=MESH) for k, chip in enumerate(chips)]


def _first_hop_copies(refs, send_sems, recv_sems):
    pair_ref, land_ref = refs
    x, y, c = _place()
    return [pltpu.make_async_remote_copy(
        src_ref=pair_ref.at[k], dst_ref=land_ref.at[k], send_sem=send_sems.at[k], recv_sem=recv_sems.at[k],
        device_id=(*_first_axis_chip(x, y, c), c), device_id_type=MESH) for k in range(2)]


def _second_hop_copies(refs, send_sems, recv_sems):
    relay_ref, land_ref = refs
    x, y, c = _place()
    second = ((x + c) % 2, (y + 1 - c) % 2)
    return [pltpu.make_async_remote_copy(
        src_ref=relay_ref, dst_ref=land_ref.at[0], send_sem=send_sems.at[0], recv_sem=recv_sems.at[0],
        device_id=(*second, c), device_id_type=MESH)]


def _own_block_copies(targets):
    def copies(refs, send_sems, recv_sems):
        x, y, c = _place()
        mine = refs[0].at[4 * x + 2 * y + c]
        return [pltpu.make_async_remote_copy(
            src_ref=mine, dst_ref=mine, send_sem=send_sems.at[k], recv_sem=recv_sems.at[k],
            device_id=to, device_id_type=MESH) for k, to in enumerate(targets(x, y, c))]
    return copies


def _all_others(x, y, c):
    flip = lambda v, f: 1 - v if f else v
    return [(flip(x, r & 4), flip(y, r & 2), flip(c, r & 1)) for r in range(1, N_DEV)]


def _forward_copies(refs, send_sems, recv_sems):
    x, y, c = _place()
    chips = [(1 - x, y), (x, 1 - y), (1 - x, 1 - y)]
    return [pltpu.make_async_remote_copy(
        src_ref=refs[0].at[4 * chip[0] + 2 * chip[1] + c], dst_ref=refs[0].at[4 * chip[0] + 2 * chip[1] + c],
        send_sem=send_sems.at[k], recv_sem=recv_sems.at[k],
        device_id=(x, y, 1 - c), device_id_type=MESH) for k, chip in enumerate(chips)]


def _first_axis_chip(x, y, c):
    return (x + 1 - c) % 2, (y + c) % 2


def _second_axis_chip(x, y, c):
    return (x + c) % 2, (y + 1 - c) % 2


def _first_targets(x, y, c):
    return [(x, y, 1 - c), (*_first_axis_chip(x, y, c), c)]


def _all_gather_small(shard, name, dep=None):
    def body(in_ref, *refs):
        out_ref, send_sems, recv_sems, local_sem = refs[-4:]
        x, y, c = _place()
        me, sibling = 4 * x + 2 * y + c, (x, y, 1 - c)
        first, second = _first_axis_chip(x, y, c), _second_axis_chip(x, y, c)

        def pair(chip):
            return out_ref.at[pl.ds(2 * (2 * chip[0] + chip[1]), 2)]

        def exchange(k, src, dst, to):
            cp = pltpu.make_async_remote_copy(src_ref=src, dst_ref=dst, send_sem=send_sems.at[k],
                                              recv_sem=recv_sems.at[k], device_id=to, device_id_type=MESH)
            cp.start()
            cp.wait()

        own = pltpu.make_async_copy(in_ref, out_ref.at[me], local_sem)
        own.start()
        exchange(0, in_ref, out_ref.at[me], sibling)
        own.wait()
        exchange(1, pair((x, y)), pair((x, y)), (*second, c))
        exchange(2, pair(second), pair(second), sibling)
        exchange(3, pair(first), pair(first), (*second, c))

    spec = pl.BlockSpec(memory_space=pltpu.VMEM)
    deps = [] if dep is None else [dep]
    return pl.pallas_call(
        body, name=name, out_shape=jax.ShapeDtypeStruct((N_DEV,) + shard.shape, shard.dtype),
        in_specs=[spec] * (1 + len(deps)), out_specs=spec,
        scratch_shapes=[pltpu.SemaphoreType.DMA((4,)), pltpu.SemaphoreType.DMA((4,)), pltpu.SemaphoreType.DMA],
        compiler_params=_params(),
    )(shard, *deps)


def _slot_copies(refs, send_sems, recv_sems, plan):
    copies = []
    for k, ((px, py, pc), to) in enumerate(plan):
        blk = refs[0].at[4 * px + 2 * py + pc]
        copies.append(pltpu.make_async_remote_copy(
            src_ref=blk, dst_ref=blk, send_sem=send_sems.at[k], recv_sem=recv_sems.at[k],
            device_id=to, device_id_type=MESH))
    return copies


def _second_axis_stage_copies(refs, send_sems, recv_sems):
    x, y, c = _place()
    first, second = (*_first_axis_chip(x, y, c), c), (*_second_axis_chip(x, y, c), c)
    return _slot_copies(refs, send_sems, recv_sems, [((x, y, c), second), (first, (x, y, 1 - c)), (first, second)])


def _second_axis_forward_copies(refs, send_sems, recv_sems):
    x, y, c = _place()
    return _slot_copies(refs, send_sems, recv_sems, [((*_second_axis_chip(x, y, c), c), (x, y, 1 - c))])


def _diagonal_forward_copies(refs, send_sems, recv_sems):
    x, y, c = _place()
    blk = refs[0].at[4 * (1 - x) + 2 * (1 - y) + c]
    return [pltpu.make_async_remote_copy(
        src_ref=blk, dst_ref=blk, send_sem=send_sems.at[0], recv_sem=recv_sems.at[0],
        device_id=(x, y, 1 - c), device_id_type=MESH)]


def _with_own_slot(block, me):
    return lax.dynamic_update_index_in_dim(lax.empty((N_DEV,) + block.shape, block.dtype), block, me, 0)


def _matmul(a, b, dims, out_dtype, tm, tn, name, dep=None):
    if dims == "nn":
        (m, k), n = a.shape, b.shape[1]
        a_spec = pl.BlockSpec((tm, k), lambda i, j: (i, 0))
        b_spec = pl.BlockSpec((k, tn), lambda i, j: (0, j))
        contract = ((1,), (0,))
    elif dims == "nt":
        (m, k), n = a.shape, b.shape[0]
        a_spec = pl.BlockSpec((tm, k), lambda i, j: (i, 0))
        b_spec = pl.BlockSpec((tn, k), lambda i, j: (j, 0))
        contract = ((1,), (1,))
    else:
        (k, m), n = a.shape, b.shape[1]
        a_spec = pl.BlockSpec((k, tm), lambda i, j: (0, i))
        b_spec = pl.BlockSpec((k, tn), lambda i, j: (0, j))
        contract = ((0,), (0,))
    assert m % tm == 0 and n % tn == 0 and a.dtype == BF16 and b.dtype == BF16

    def body(a_ref, b_ref, *rest):
        rest[-1][...] = lax.dot_general(a_ref[...], b_ref[...], (contract, ((), ())),
                                        preferred_element_type=F32).astype(out_dtype)

    deps = [] if dep is None else [dep]
    return pl.pallas_call(
        body, name=name, grid=(m // tm, n // tn),
        in_specs=[a_spec, b_spec] + [pl.BlockSpec((8, 128), lambda i, j: (0, 0))] * len(deps),
        out_specs=pl.BlockSpec((tm, tn), lambda i, j: (i, j)),
        out_shape=jax.ShapeDtypeStruct((m, n), out_dtype),
        compiler_params=_params(dimension_semantics=("arbitrary", "arbitrary")),
    )(a, b, *deps)


Z_TILE = 768
_Z_TILE_ORDER = ((0, 1, 2, 3, 4, 5, 6), (2, 0, 1, 6, 3, 4, 5), (4, 0, 5, 6, 1, 2, 3), (6, 2, 3, 4, 0, 1, 5))
_Z_EARLY_TILES = 4


def _z_proj(h, w_in_t, chip, first, count, z_prev, name, tr=1024):
    t = h.shape[0]

    def body(chip_ref, h_ref, w_ref, z_prev_ref, z_ref):
        z_ref[...] = _dot_nt(h_ref[...], w_ref[...])

    def tile(j, chip_ref):
        picked = 0
        for c, order in enumerate(_Z_TILE_ORDER):
            for k in range(count):
                picked = picked + jnp.where((chip_ref[0] == c) & (j == k), order[first + k], 0)
        return picked

    return pl.pallas_call(
        body, name=name,
        grid_spec=pltpu.PrefetchScalarGridSpec(
            num_scalar_prefetch=1, grid=(count, t // tr),
            in_specs=[pl.BlockSpec((tr, D_MODEL), lambda j, i, o: (i, 0)),
                      pl.BlockSpec((Z_TILE, D_MODEL), lambda j, i, o: (tile(j, o), 0)),
                      pl.BlockSpec(memory_space=pl.ANY)],
            out_specs=pl.BlockSpec((tr, Z_TILE), lambda j, i, o: (i, tile(j, o)))),
        out_shape=jax.ShapeDtypeStruct((t, D_IN), F32),
        input_output_aliases={3: 0},
        compiler_params=_params(dimension_semantics=("arbitrary", "arbitrary")),
    )(chip, h, w_in_t, z_prev)


def _modulation(device, c_all, w_ada, b_ada):
    def body(device_ref, c_ref, w_ref, b_ref, act_ref, mod_ref):
        cv = c_ref[...]
        act = cv * _sigmoid(cv)
        act_ref[...] = act
        mod_ref[...] = jnp.dot(act.astype(BF16), w_ref[...].astype(BF16), preferred_element_type=F32) + b_ref[...]

    whole = lambda a: pl.BlockSpec(a.shape, lambda i, device_ref: (0,) * a.ndim)
    return pl.pallas_call(
        body, name="modulation",
        grid_spec=pltpu.PrefetchScalarGridSpec(
            num_scalar_prefetch=1, grid=(1,),
            in_specs=[whole(c_all), whole(w_ada), pl.BlockSpec((1, W_ADA_SHARD), lambda i, device_ref: (0, device_ref[0]))],
            out_specs=(whole(c_all), pl.BlockSpec((N_DEV, W_ADA_SHARD), lambda i, device_ref: (0, 0)))),
        out_shape=(jax.ShapeDtypeStruct(c_all.shape, F32), jax.ShapeDtypeStruct((N_DEV, W_ADA_SHARD), F32)),
        compiler_params=_params(dimension_semantics=("arbitrary",)),
    )(device, c_all, w_ada, b_ada)


MOD_SHIFT, MOD_SCALE, MOD_GATE = 0, 1, 2


def _mod_spec(part, d):
    return pl.BlockSpec((1, d), lambda i: (0, part))


def _norm_z_proj_own(x, norm_g, mod, w_in_t, chip, tm=512):
    t, d = x.shape

    def body(chip_ref, x_ref, g_ref, sc_ref, sh_ref, w_ref, h_ref, z_ref):
        xv = x_ref[...]
        r = lax.rsqrt(jnp.mean(xv * xv, axis=-1, keepdims=True) + EPS)
        h = ((xv * r) * g_ref[...] * (1.0 + sc_ref[...]) + sh_ref[...]).astype(BF16)
        h_ref[...] = h
        z_ref[...] = _dot_nt(h, w_ref[...])

    def own_tile(chip_ref):
        picked = 0
        for c, order in enumerate(_Z_TILE_ORDER):
            picked = picked + jnp.where(chip_ref[0] == c, order[0], 0)
        return picked

    def row(part):
        return pl.BlockSpec((1, d), lambda i, o: (0, part))

    return pl.pallas_call(
        body, name="norm_z_proj_own",
        grid_spec=pltpu.PrefetchScalarGridSpec(
            num_scalar_prefetch=1, grid=(t // tm,),
            in_specs=[pl.BlockSpec((tm, d), lambda i, o: (i, 0)), row(0), row(MOD_SCALE), row(MOD_SHIFT),
                      pl.BlockSpec((Z_TILE, d), lambda i, o: (own_tile(o), 0))],
            out_specs=(pl.BlockSpec((tm, d), lambda i, o: (i, 0)),
                       pl.BlockSpec((tm, Z_TILE), lambda i, o: (i, own_tile(o))))),
        out_shape=(jax.ShapeDtypeStruct((t, d), BF16), jax.ShapeDtypeStruct((t, D_IN), F32)),
        compiler_params=_params(dimension_semantics=("arbitrary",)),
    )(chip, x, norm_g, mod, mod, w_in_t)


def _window_bias(block_index):
    s = lax.broadcasted_iota(jnp.int32, (2 * BLOCK, BLOCK), 0)
    t = lax.broadcasted_iota(jnp.int32, (2 * BLOCK, BLOCK), 1)
    valid = ((s < BLOCK) & (s > t) & (block_index > 0)) | ((s >= BLOCK) & ((s - BLOCK) <= t))
    bias = jnp.where(valid, 0.0, -jnp.inf).astype(F32)
    return jnp.concatenate([bias] * 8, axis=1)


def _heads_t(pair_blocks, g):
    top = lax.broadcasted_iota(jnp.int32, (BLOCK, BLOCK), 0) < HEAD_DIM
    zeros = jnp.zeros((HEAD_DIM, BLOCK), F32)
    tiles = []
    for blk in pair_blocks:
        tp = blk.T
        if g == 0:
            tiles += [jnp.where(top, tp, 0.0), jnp.concatenate([tp[HEAD_DIM:], zeros], axis=0)]
        else:
            tiles += [jnp.concatenate([zeros, tp[:HEAD_DIM]], axis=0), jnp.where(top, 0.0, tp)]
    return jnp.concatenate(tiles, axis=1)


def _pair_block(xt, p, g):
    r0 = HEAD_DIM * g
    even = xt[r0:r0 + HEAD_DIM, (2 * p) * BLOCK:(2 * p + 1) * BLOCK]
    odd = xt[r0:r0 + HEAD_DIM, (2 * p + 1) * BLOCK:(2 * p + 2) * BLOCK]
    return jnp.concatenate([even, odd], axis=0).T


def _softmax_t(scores_t, bias, sink):
    st = scores_t + bias
    m = jnp.maximum(jnp.max(st, axis=0, keepdims=True), sink)
    e = jnp.exp(st - m)
    es = jnp.exp(sink - m)
    inv = 1.0 / (jnp.sum(e, axis=0, keepdims=True) + es)
    return e * inv, es * inv


def _dot(a, b):
    return jnp.dot(a, b, preferred_element_type=F32)


def _dot_nt(a, b):
    return lax.dot_general(a, b, (((1,), (1,)), ((), ())), preferred_element_type=F32)


def _layer_norm_fwd(v):
    mu = jnp.mean(v, axis=-1, keepdims=True)
    xc = v - mu
    rstd = lax.rsqrt(jnp.mean(xc * xc, axis=-1, keepdims=True) + EPS)
    return xc * rstd, rstd


def _tril(transposed=False):
    t = lax.broadcasted_iota(jnp.int32, (BLOCK, BLOCK), 0)
    s = lax.broadcasted_iota(jnp.int32, (BLOCK, BLOCK), 1)
    return s >= t if transposed else t >= s


def _const_spec(shape):
    return pl.BlockSpec(shape, lambda i: (0,) * len(shape))


def _keys_values(z_ref, kvp):
    kvc = z_ref[:, SEG_KV:SEG_KV + 2 * D_KV]
    kk = jnp.concatenate([kvp[:, :D_KV], kvc[:, :D_KV]], axis=0)
    vv = jnp.concatenate([kvp[:, D_KV:], kvc[:, D_KV:]], axis=0)
    return kk, vv


MIXER_BLOCKS = 2


class _Rows:
    def __init__(self, ref, sub):
        self.ref, self.rows = ref, slice(sub * BLOCK, (sub + 1) * BLOCK)

    def __getitem__(self, idx):
        return self.ref[self.rows, idx[1]]

    def __setitem__(self, idx, value):
        self.ref[self.rows, idx[1]] = value


def _kv_before_spec(index):
    return pl.BlockSpec((BLOCK, 2 * D_KV),
                        lambda i: (jnp.maximum(MIXER_BLOCKS * index(i) - 1, 0), SEG_KV // (2 * D_KV)))


def _pair_cols(g, p, base=0):
    return slice(base + (4 * g + p) * 128, base + (4 * g + p + 1) * 128)


def _mixer_fwd(z, sink_rows, ln_g, ln_b, sgu_w, sgu_bt):
    t = z.shape[0]

    def body(z_all, kvp_ref, sink_ref, lng_ref, lnb_ref, w_ref, bt_ref, a_all, prob_ref, sink_prob_ref):
        kv_before = kvp_ref[...]
        for sub in range(MIXER_BLOCKS):
            z_ref, a_ref = _Rows(z_all, sub), _Rows(a_all, sub)
            one_block(z_ref, kv_before, MIXER_BLOCKS * pl.program_id(0) + sub, sink_ref, lng_ref, lnb_ref, w_ref,
                      bt_ref, a_ref, prob_ref.at[sub], sink_prob_ref.at[sub])
            kv_before = z_ref[:, SEG_KV:SEG_KV + 2 * D_KV]

    def one_block(z_ref, kv_before, block_index, sink_ref, lng_ref, lnb_ref, w_ref, bt_ref, a_ref, prob_ref,
                  sink_prob_ref):
        bias = _window_bias(block_index)
        kk, vv = _keys_values(z_ref, kv_before)
        kk_b, vvt_b = kk.astype(BF16), vv.T.astype(BF16)
        for g in range(2):
            qt = _heads_t([z_ref[:, _pair_cols(g, p, SEG_Q)] * ATTN_SCALE for p in range(4)], g).astype(BF16)
            prob, sink_prob = _softmax_t(_dot(kk_b, qt), bias, sink_ref[g])
            prob_b = prob.astype(BF16)
            prob_ref[g] = prob_b
            sink_prob_ref[g] = sink_prob
            ot = _dot(vvt_b, prob_b)
            for p in range(4):
                gate = z_ref[:, _pair_cols(g, p, SEG_GA)]
                a_ref[:, _pair_cols(g, p)] = (_pair_block(ot, p, g) * (gate * _sigmoid(gate))).astype(BF16)

        vhat, _ = _layer_norm_fwd(z_ref[:, SEG_VS:SEG_VS + D_SGU])
        vn = vhat * lng_ref[...] + lnb_ref[...]
        tril = _tril()
        for g in range(SGU_GROUPS):
            cols = slice(g * 128, (g + 1) * 128)
            wm = jnp.where(tril, w_ref[g], 0.0).astype(BF16)
            mixed = _dot(wm, vn[:, cols].astype(BF16)) + bt_ref[:, g:g + 1]
            gate = z_ref[:, SEG_GS + g * 128:SEG_GS + (g + 1) * 128]
            a_ref[:, D_ATTN + g * 128:D_ATTN + (g + 1) * 128] = (
                (z_ref[:, SEG_U + g * 128:SEG_U + (g + 1) * 128] * mixed) * (gate * _sigmoid(gate))).astype(BF16)

    rows = MIXER_BLOCKS * BLOCK
    return pl.pallas_call(
        body, name="mixer_fwd", grid=(t // rows,),
        in_specs=[pl.BlockSpec((rows, D_IN), lambda i: (i, 0)), _kv_before_spec(lambda i: i),
                  _const_spec((2, 1, 8 * BLOCK)), _const_spec((1, D_SGU)), _const_spec((1, D_SGU)),
                  _const_spec((SGU_GROUPS, BLOCK, BLOCK)), _const_spec((BLOCK, SGU_GROUPS))],
        out_specs=(pl.BlockSpec((rows, D_MODEL), lambda i: (i, 0)),
                   pl.BlockSpec((MIXER_BLOCKS, 2, 2 * BLOCK, 8 * BLOCK), lambda i: (i, 0, 0, 0)),
                   pl.BlockSpec((MIXER_BLOCKS, 2, 1, 8 * BLOCK), lambda i: (i, 0, 0, 0))),
        out_shape=(jax.ShapeDtypeStruct((t, D_MODEL), BF16),
                   jax.ShapeDtypeStruct((t // BLOCK, 2, 2 * BLOCK, 8 * BLOCK), BF16),
                   jax.ShapeDtypeStruct((t // BLOCK, 2, 1, 8 * BLOCK), F32)),
        compiler_params=_params(dimension_semantics=("arbitrary",)),
    )(z, z, sink_rows, ln_g, ln_b, sgu_w, sgu_bt)


def _mixer_bwd(z, da, probs, sink_probs, ln_g, ln_b, sgu_w, sgu_wt, sgu_bt, a, dy):
    t = z.shape[0]

    def body(z_all, kvp_ref, da_all, prob_ref, sink_prob_ref, lng_ref, lnb_ref, w_ref, wt_ref, bt_ref, a_ref, dy_ref,
             dz_all, dsink_ref, dw_ref, db_ref, dlng_ref, dlnb_ref, dw_out_ref, carry_ref, dsink_acc, dbt_acc):
        step = pl.program_id(0)

        @pl.when(step == 0)
        def _():
            carry_ref[...] = jnp.zeros_like(carry_ref)
            dsink_acc[...] = jnp.zeros_like(dsink_acc)
            dbt_acc[...] = jnp.zeros_like(dbt_acc)
            dw_ref[...] = jnp.zeros_like(dw_ref)
            dlng_ref[...] = jnp.zeros_like(dlng_ref)
            dlnb_ref[...] = jnp.zeros_like(dlnb_ref)

        carry = carry_ref[...]
        dw_out_ref[...] = lax.dot_general(a_ref[...], dy_ref[...], (((0,), (0,)), ((), ())),
                                          preferred_element_type=F32).astype(BF16)
        for sub in reversed(range(MIXER_BLOCKS)):
            kv_before = kvp_ref[...] if sub == 0 else _Rows(z_all, sub - 1)[:, SEG_KV:SEG_KV + 2 * D_KV]
            carry = one_block(_Rows(z_all, sub), kv_before, _Rows(da_all, sub), prob_ref.at[sub], sink_prob_ref.at[sub],
                              carry, lng_ref, lnb_ref, w_ref, wt_ref, bt_ref, _Rows(dz_all, sub),
                              dw_ref, dlng_ref, dlnb_ref, dsink_acc, dbt_acc)
        carry_ref[...] = carry

        @pl.when(step == ns - 1)
        def _():
            db_ref[...] = dbt_acc[...].T[:SGU_GROUPS]
            lane_row = lax.broadcasted_iota(jnp.int32, (1, 128), 1)
            d_sink = jnp.zeros((1, 128), F32)
            for g in range(2):
                acc = dsink_acc[g]
                for j in range(8):
                    head_sum = jnp.sum(acc[:, j * BLOCK:(j + 1) * BLOCK], axis=-1, keepdims=True)
                    d_sink = d_sink + jnp.where(lane_row == 8 * g + j, head_sum, 0.0)
            dsink_ref[...] = d_sink

    def one_block(z_ref, kv_before, da_ref, prob_ref, sink_prob_ref, carry, lng_ref, lnb_ref, w_ref, wt_ref, bt_ref,
                  dz_ref, dw_ref, dlng_ref, dlnb_ref, dsink_acc, dbt_acc):
        kk, vv = _keys_values(z_ref, kv_before)
        vv_b = vv.astype(BF16)
        kkt_b, vvt_b = kk.T.astype(BF16), vv.T.astype(BF16)
        dkk = jnp.zeros((2 * BLOCK, D_KV), F32)
        dvv = jnp.zeros((2 * BLOCK, D_KV), F32)
        for g in range(2):
            qt = _heads_t([z_ref[:, _pair_cols(g, p, SEG_Q)] * ATTN_SCALE for p in range(4)], g).astype(BF16)
            prob_b, sink_prob = prob_ref[g], sink_prob_ref[g]
            prob = prob_b.astype(F32)
            ot = _dot(vvt_b, prob_b)
            gates = [z_ref[:, _pair_cols(g, p, SEG_GA)] for p in range(4)]
            sig = [_sigmoid(gt) for gt in gates]
            d_attn = [da_ref[:, _pair_cols(g, p)] for p in range(4)]
            d_ot = _heads_t([d_attn[p] * (gates[p] * sig[p]) for p in range(4)], g).astype(BF16)
            d_prob = _dot(vv_b, d_ot)
            delta = jnp.sum(prob * d_prob, axis=0, keepdims=True)
            d_scores = (prob * (d_prob - delta)).astype(BF16)
            dsink_acc[g] -= sink_prob * delta
            d_qt = _dot(kkt_b, d_scores)
            dkk = dkk + _dot_nt(d_scores, qt)
            dvv = dvv + _dot_nt(prob_b, d_ot)
            for p in range(4):
                dz_ref[:, _pair_cols(g, p, SEG_Q)] = (_pair_block(d_qt, p, g) * ATTN_SCALE).astype(BF16)
                d_silu = sig[p] * (1.0 + gates[p] * (1.0 - sig[p]))
                dz_ref[:, _pair_cols(g, p, SEG_GA)] = (d_attn[p] * _pair_block(ot, p, g) * d_silu).astype(BF16)
        d_kv = jnp.concatenate([dkk, dvv], axis=1)
        dz_ref[:, SEG_KV:SEG_KV + 2 * D_KV] = (d_kv[BLOCK:] + carry).astype(BF16)

        vhat, rstd = _layer_norm_fwd(z_ref[:, SEG_VS:SEG_VS + D_SGU])
        lng = lng_ref[...]
        vn = vhat * lng + lnb_ref[...]
        tril, triu = _tril(), _tril(transposed=True)
        lane = lax.broadcasted_iota(jnp.int32, (BLOCK, 128), 1)
        d_bt = jnp.zeros((BLOCK, 128), F32)
        d_vn = []
        for g in range(SGU_GROUPS):
            cols = slice(g * 128, (g + 1) * 128)
            wm = jnp.where(tril, w_ref[g], 0.0).astype(BF16)
            wmt = jnp.where(triu, wt_ref[g], 0.0).astype(BF16)
            vn_g = vn[:, cols].astype(BF16)
            mixed = _dot(wm, vn_g) + bt_ref[:, g:g + 1]
            gate = z_ref[:, SEG_GS + g * 128:SEG_GS + (g + 1) * 128]
            u = z_ref[:, SEG_U + g * 128:SEG_U + (g + 1) * 128]
            d_out = da_ref[:, D_ATTN + g * 128:D_ATTN + (g + 1) * 128]
            sg = _sigmoid(gate)
            d_um = d_out * (gate * sg)
            dz_ref[:, SEG_U + g * 128:SEG_U + (g + 1) * 128] = (d_um * mixed).astype(BF16)
            dz_ref[:, SEG_GS + g * 128:SEG_GS + (g + 1) * 128] = (
                d_out * (u * mixed) * (sg * (1.0 + gate * (1.0 - sg)))).astype(BF16)
            d_mixed = d_um * u
            d_mixed_b = d_mixed.astype(BF16)
            dw_ref[g] += jnp.where(tril, _dot_nt(d_mixed_b, vn_g), 0.0)
            d_bt = d_bt + jnp.where(lane == g, jnp.sum(d_mixed, axis=-1, keepdims=True), 0.0)
            d_vn.append(_dot(wmt, d_mixed_b))
        dbt_acc[...] += d_bt
        d_vn = jnp.concatenate(d_vn, axis=1)
        dlng_ref[...] += jnp.sum(d_vn * vhat, axis=0, keepdims=True)
        dlnb_ref[...] += jnp.sum(d_vn, axis=0, keepdims=True)
        d_vhat = d_vn * lng
        d_v = rstd * (d_vhat - jnp.mean(d_vhat, axis=-1, keepdims=True)
                      - vhat * jnp.mean(d_vhat * vhat, axis=-1, keepdims=True))
        dz_ref[:, SEG_VS:SEG_VS + D_SGU] = d_v.astype(BF16)
        return d_kv[:BLOCK]

    rows = MIXER_BLOCKS * BLOCK
    ns = t // rows
    rev = lambda i: ns - 1 - i
    tile = a.shape[1] // ns
    assert tile % 128 == 0
    return pl.pallas_call(
        body, name="mixer_bwd", grid=(ns,),
        in_specs=[pl.BlockSpec((rows, D_IN), lambda i: (rev(i), 0)), _kv_before_spec(rev),
                  pl.BlockSpec((rows, D_MODEL), lambda i: (rev(i), 0)),
                  pl.BlockSpec((MIXER_BLOCKS, 2, 2 * BLOCK, 8 * BLOCK), lambda i: (rev(i), 0, 0, 0)),
                  pl.BlockSpec((MIXER_BLOCKS, 2, 1, 8 * BLOCK), lambda i: (rev(i), 0, 0, 0)),
                  _const_spec((1, D_SGU)), _const_spec((1, D_SGU)),
                  _const_spec((SGU_GROUPS, BLOCK, BLOCK)), _const_spec((SGU_GROUPS, BLOCK, BLOCK)),
                  _const_spec((BLOCK, SGU_GROUPS)), pl.BlockSpec((t, tile), lambda i: (0, i)),
                  pl.BlockSpec(dy.shape, lambda i: (0, 0), pipeline_mode=pl.Buffered(1))],
        out_specs=(pl.BlockSpec((rows, D_IN), lambda i: (rev(i), 0)), _const_spec((1, 128)),
                   _const_spec((SGU_GROUPS, BLOCK, BLOCK)), _const_spec((SGU_GROUPS, BLOCK)),
                   _const_spec((1, D_SGU)), _const_spec((1, D_SGU)),
                   pl.BlockSpec((tile, dy.shape[1]), lambda i: (i, 0))),
        out_shape=(jax.ShapeDtypeStruct((t, D_IN), BF16), jax.ShapeDtypeStruct((1, 128), F32),
                   jax.ShapeDtypeStruct((SGU_GROUPS, BLOCK, BLOCK), F32), jax.ShapeDtypeStruct((SGU_GROUPS, BLOCK), F32),
                   jax.ShapeDtypeStruct((1, D_SGU), F32), jax.ShapeDtypeStruct((1, D_SGU), F32),
                   jax.ShapeDtypeStruct((a.shape[1], dy.shape[1]), BF16)),
        scratch_shapes=[pltpu.VMEM((BLOCK, 2 * D_KV), F32), pltpu.VMEM((2, 1, 8 * BLOCK), F32),
                        pltpu.VMEM((BLOCK, 128), F32)],
        compiler_params=_params(dimension_semantics=("arbitrary",)),
    )(z, z, da, probs, sink_probs, ln_g, ln_b, sgu_w, sgu_wt, sgu_bt, a, dy)


def _out_proj_head(a, w_out_full, x, target, mod, final_g, tm=256):
    t, d = x.shape

    def body(a_ref, w_ref, x_ref, tg_ref, gate_ref, fg_ref, dx2_ref, dy_ref, da_ref, loss_ref, dfg_ref, dgate_ref):
        @pl.when(pl.program_id(0) == 0)
        def _():
            loss_ref[...] = jnp.zeros_like(loss_ref)
            dfg_ref[...] = jnp.zeros_like(dfg_ref)
            dgate_ref[...] = jnp.zeros_like(dgate_ref)

        yv, gate, fg = _dot(a_ref[...], w_ref[...]), gate_ref[...], fg_ref[...]
        x2 = x_ref[...] + gate * yv
        r2 = lax.rsqrt(jnp.mean(x2 * x2, axis=-1, keepdims=True) + EPS)
        nrm = x2 * r2
        err = nrm * fg - tg_ref[...]
        loss_ref[...] += 0.5 * jnp.sum(jnp.mean(err * err, axis=-1, keepdims=True), axis=0, keepdims=True)
        fg_d = fg * (1.0 / d)
        err_nrm = err * nrm
        dfg_ref[...] += jnp.sum(err_nrm, axis=0, keepdims=True) * (1.0 / d)
        d_nrm = err * fg_d
        dx2 = r2 * (d_nrm - nrm * jnp.mean(err_nrm * fg_d, axis=-1, keepdims=True))
        dx2_ref[...] = dx2
        dgate_ref[...] += jnp.sum(dx2 * yv, axis=0, keepdims=True)
        dy = (dx2 * gate).astype(BF16)
        dy_ref[...] = dy
        da_ref[...] = _dot_nt(dy, w_ref[...])

    blk = pl.BlockSpec((tm, d), lambda i: (i, 0))
    a_blk = pl.BlockSpec((tm, a.shape[1]), lambda i: (i, 0))
    row = _const_spec((1, d))
    whole = pl.BlockSpec(w_out_full.shape, lambda i: (0, 0), pipeline_mode=pl.Buffered(1))
    return pl.pallas_call(
        body, name="out_proj_head", grid=(t // tm,),
        in_specs=[a_blk, whole, blk, blk, _mod_spec(MOD_GATE, d), row],
        out_specs=(blk, blk, a_blk, _const_spec((1, 128)), row, row),
        out_shape=(jax.ShapeDtypeStruct((t, d), F32), jax.ShapeDtypeStruct((t, d), BF16),
                   jax.ShapeDtypeStruct(a.shape, F32), jax.ShapeDtypeStruct((1, 128), F32),
                   jax.ShapeDtypeStruct((1, d), F32), jax.ShapeDtypeStruct((1, d), F32)),
        compiler_params=_params(dimension_semantics=("arbitrary",)),
    )(a, w_out_full, x, target, mod, final_g)


def _z_proj_bwd_norm(dz, w_in_t, x, dx2, norm_g, mod, dep, tm=256):
    t, d = x.shape

    def body(dz_ref, w_ref, x_ref, dx2_ref, g_ref, sc_ref, dep_ref, gx_ref, dshift_ref, dscale_ref, dg_ref):
        @pl.when(pl.program_id(0) == 0)
        def _():
            dshift_ref[...] = jnp.zeros_like(dshift_ref)
            dscale_ref[...] = jnp.zeros_like(dscale_ref)
            dg_ref[...] = jnp.zeros_like(dg_ref)

        dh, xv, g = _dot(dz_ref[...], w_ref[...]), x_ref[...], g_ref[...]
        one_plus = 1.0 + sc_ref[...]
        r = lax.rsqrt(jnp.mean(xv * xv, axis=-1, keepdims=True) + EPS)
        xn = xv * r
        gain = one_plus * g
        dh_xn = dh * xn
        dh_xn_sum = jnp.sum(dh_xn, axis=0, keepdims=True)
        dshift_ref[...] += jnp.sum(dh, axis=0, keepdims=True)
        dscale_ref[...] += dh_xn_sum * g
        dg_ref[...] += dh_xn_sum * one_plus
        d_xn = dh * gain
        gx_ref[...] = dx2_ref[...] + r * (d_xn - xn * jnp.mean(dh_xn * gain, axis=-1, keepdims=True))

    blk = pl.BlockSpec((tm, d), lambda i: (i, 0))
    row = _const_spec((1, d))
    whole = pl.BlockSpec(w_in_t.shape, lambda i: (0, 0), pipeline_mode=pl.Buffered(1))
    return pl.pallas_call(
        body, name="z_proj_bwd_norm", grid=(t // tm,),
        in_specs=[pl.BlockSpec((tm, dz.shape[1]), lambda i: (i, 0)), whole, blk, blk, row, _mod_spec(MOD_SCALE, d),
                  _const_spec((8, 128))],
        out_specs=(blk, row, row, row),
        out_shape=(jax.ShapeDtypeStruct((t, d), F32),) + (jax.ShapeDtypeStruct((1, d), F32),) * 3,
        compiler_params=_params(dimension_semantics=("arbitrary",)),
    )(dz, w_in_t, x, dx2, norm_g, mod, dep)


def _adamw(w, g, m, v):
    m = ADAM_B1 * m + (1.0 - ADAM_B1) * g
    v = ADAM_B2 * v + (1.0 - ADAM_B2) * (g * g)
    m_hat = m / (1.0 - ADAM_B1 ** ADAM_STEP)
    v_hat = v / (1.0 - ADAM_B2 ** ADAM_STEP)
    delta = -ADAM_LR * (m_hat / (jnp.sqrt(v_hat) + ADAM_EPS) + ADAM_WD * w)
    return delta, m, v


def _relay_sum(device, blocks, land_pair, land_first, tr):
    _, r, c = blocks.shape

    def body(device_ref, a_ref, b_ref, c_ref, o_ref):
        o_ref[...] = (a_ref[...].astype(F32) + b_ref[...].astype(F32) + c_ref[...].astype(F32)).astype(BF16)

    second = pl.BlockSpec((None, tr, c), lambda i, device_ref: (1, i, 0))
    return pl.pallas_call(
        body, name="w_in_grad_relay_sum",
        grid_spec=pltpu.PrefetchScalarGridSpec(
            num_scalar_prefetch=1, grid=(r // tr,),
            in_specs=[pl.BlockSpec((None, tr, c), lambda i, device_ref: (device_ref[0], i, 0)), second, second],
            out_specs=pl.BlockSpec((tr, c), lambda i, device_ref: (i, 0))),
        out_shape=jax.ShapeDtypeStruct((r, c), BF16),
        compiler_params=_params(dimension_semantics=("arbitrary",)),
    )(device, blocks, land_pair, land_first)


def _adam_from_chips(chip, pair, landed, w, m, v, name, tc):
    _, r, c = pair.shape
    n = len(landed)

    def body(chip_ref, own_ref, *refs):
        w_ref, m_ref, v_ref, g_ref, d_ref, nm_ref, nv_ref = refs[n:]
        g = own_ref[...].astype(F32)
        for k in range(n):
            g = g + refs[k][...].astype(F32)
        g_ref[...] = g
        d_ref[...], nm_ref[...], nv_ref[...] = _adamw(w_ref[...], g, m_ref[...], v_ref[...])

    def landed_spec(index):
        return pl.BlockSpec((None, r, tc), lambda i, chip_ref: (index, 0, i))

    blk = pl.BlockSpec((r, tc), lambda i, chip_ref: (0, i))
    return pl.pallas_call(
        body, name=name,
        grid_spec=pltpu.PrefetchScalarGridSpec(
            num_scalar_prefetch=1, grid=(c // tc,),
            in_specs=[pl.BlockSpec((None, r, tc), lambda i, chip_ref: (chip_ref[0], 0, i))]
            + [landed_spec(index) for _, index in landed] + [blk, blk, blk],
            out_specs=(blk,) * 4),
        out_shape=(jax.ShapeDtypeStruct((r, c), F32),) * 4,
        compiler_params=_params(dimension_semantics=("arbitrary",)),
    )(chip, pair, *[array for array, _ in landed], w, m, v)


def _adam_w_ada(device, act_t, dmod_all, w, m, v, tr=512):
    r, c = w.shape

    def body(device_ref, a_ref, dm_ref, w_ref, m_ref, v_ref, g_ref, d_ref, nm_ref, nv_ref):
        g = _dot(a_ref[...].astype(BF16), dm_ref[...].astype(BF16))
        g_ref[...] = g
        d_ref[...], nm_ref[...], nv_ref[...] = _adamw(w_ref[...], g, m_ref[...], v_ref[...])

    blk = pl.BlockSpec((tr, c), lambda i, device_ref: (i, 0))
    return pl.pallas_call(
        body, name="adam_w_ada",
        grid_spec=pltpu.PrefetchScalarGridSpec(
            num_scalar_prefetch=1, grid=(r // tr,),
            in_specs=[pl.BlockSpec((tr, N_DEV), lambda i, device_ref: (i, 0)),
                      pl.BlockSpec((N_DEV, c), lambda i, device_ref: (0, device_ref[0])), blk, blk, blk],
            out_specs=(blk,) * 4),
        out_shape=(jax.ShapeDtypeStruct((r, c), F32),) * 4,
        compiler_params=_params(dimension_semantics=("arbitrary",)),
    )(device, act_t, dmod_all, w, m, v)


def _pack_small(d_shift, d_scale, d_gate, d_norm_g, d_final_g, d_ln_g, d_ln_b, loss, d_sinks, d_sgu_b):
    def body(shift_ref, scale_ref, gate_ref, ng_ref, fg_ref, lng_ref, lnb_ref, loss_ref, sink_ref, b_ref, o_ref):
        o_ref[...] = jnp.zeros_like(o_ref)
        o_ref[ROW_SHIFT:ROW_SHIFT + 1, :] = shift_ref[...]
        o_ref[ROW_SCALE:ROW_SCALE + 1, :] = scale_ref[...]
        o_ref[ROW_GATE:ROW_GATE + 1, :] = gate_ref[...]
        o_ref[ROW_NORM_G:ROW_NORM_G + 1, :] = ng_ref[...]
        o_ref[ROW_FINAL_G:ROW_FINAL_G + 1, :] = fg_ref[...]
        o_ref[ROW_LN:ROW_LN + 1, 0:D_SGU] = lng_ref[...]
        o_ref[ROW_LN:ROW_LN + 1, D_SGU:2 * D_SGU] = lnb_ref[...]
        o_ref[ROW_MISC:ROW_MISC + 1, 0:128] = loss_ref[...]
        o_ref[ROW_MISC:ROW_MISC + 1, 128:256] = sink_ref[...]
        o_ref[ROW_SGU_B:ROW_SGU_B + SGU_GROUPS, 0:BLOCK] = b_ref[...]

    return pl.pallas_call(
        body, name="pack_small", out_shape=jax.ShapeDtypeStruct((SMALL_ROWS, D_MODEL), F32),
        compiler_params=_params(),
    )(d_shift, d_scale, d_gate, d_norm_g, d_final_g, d_ln_g, d_ln_b, loss, d_sinks, d_sgu_b)


_SMALL_NAMES = ("norm_g", "b_ada", "attn_sinks", "sgu_ln_g", "sgu_ln_b", "sgu_w", "sgu_b", "final_g")


def _adam_small(partials, d_sgu_w_all, weights, moments_m, moments_v):
    names = _SMALL_NAMES
    k = len(names)

    def body(*refs):
        p_ref, sw_ref = refs[0], refs[1]
        w_refs, m_refs, v_refs = refs[2:2 + k], refs[2 + k:2 + 2 * k], refs[2 + 2 * k:2 + 3 * k]
        loss_ref, dmod_ref = refs[2 + 3 * k], refs[3 + 3 * k]
        out_refs = refs[4 + 3 * k:4 + 7 * k]
        sum_ref = refs[4 + 7 * k]
        total = p_ref[0]
        for j in range(1, N_DEV):
            total = total + p_ref[j]
        sum_ref[...] = total
        for j in range(N_DEV):
            for part, row in enumerate((ROW_SHIFT, ROW_SCALE, ROW_GATE)):
                dmod_ref[j:j + 1, part * D_MODEL:(part + 1) * D_MODEL] = p_ref[j, row:row + 1, :]
        loss_ref[...] = sum_ref[ROW_MISC:ROW_MISC + 1, 0:1]
        d_sgu_w = sw_ref[0]
        for j in range(1, N_DEV):
            d_sgu_w = d_sgu_w + sw_ref[j]
        grads = {
            "norm_g": sum_ref[ROW_NORM_G:ROW_NORM_G + 1, :],
            "b_ada": jnp.concatenate([sum_ref[r:r + 1, :] for r in (ROW_SHIFT, ROW_SCALE, ROW_GATE)], axis=1),
            "attn_sinks": sum_ref[ROW_MISC:ROW_MISC + 1, 128:128 + N_Q_HEADS],
            "sgu_ln_g": sum_ref[ROW_LN:ROW_LN + 1, 0:D_SGU],
            "sgu_ln_b": sum_ref[ROW_LN:ROW_LN + 1, D_SGU:2 * D_SGU],
            "sgu_w": d_sgu_w[None],
            "sgu_b": sum_ref[ROW_SGU_B:ROW_SGU_B + SGU_GROUPS, 0:BLOCK][None],
            "final_g": sum_ref[ROW_FINAL_G:ROW_FINAL_G + 1, :],
        }
        for i, name in enumerate(names):
            g = grads[name]
            delta, m, v = _adamw(w_refs[i][...], g, m_refs[i][...], v_refs[i][...])
            out_refs[4 * i][...] = g
            out_refs[4 * i + 1][...] = delta
            out_refs[4 * i + 2][...] = m
            out_refs[4 * i + 3][...] = v

    shapes = [jax.ShapeDtypeStruct((1, 1), F32), jax.ShapeDtypeStruct((N_DEV, 3 * D_MODEL), F32)]
    for name in names:
        shapes += [jax.ShapeDtypeStruct(weights[name].shape, F32)] * 4
    outs = pl.pallas_call(
        body, name="adam_small", out_shape=tuple(shapes),
        scratch_shapes=[pltpu.VMEM((SMALL_ROWS, D_MODEL), F32)],
        compiler_params=_params(),
    )(partials, d_sgu_w_all, *[weights[n] for n in names], *[moments_m[n] for n in names],
      *[moments_v[n] for n in names])
    return outs[0], outs[1], {name: outs[2 + 4 * i:6 + 4 * i] for i, name in enumerate(names)}


def kernel(x, c, norm_g, w_ada, b_ada, w_in, attn_sinks, sgu_ln_g, sgu_ln_b, sgu_w, sgu_b, w_out, final_g, loss_target, m_norm_g, m_w_ada, m_b_ada, m_w_in, m_attn_sinks, m_sgu_ln_g, m_sgu_ln_b, m_sgu_w, m_sgu_b, m_w_out, m_final_g, v_norm_g, v_w_ada, v_b_ada, v_w_in, v_attn_sinks, v_sgu_ln_g, v_sgu_ln_b, v_sgu_w, v_sgu_b, v_w_out, v_final_g):
    xi, yi, ci = _place()
    me = 4 * xi + 2 * yi + ci
    x2d, target = x[0], loss_target[0]
    t = x2d.shape[0]

    core = ci.astype(jnp.int32).reshape(1)
    chip = (2 * xi + yi).astype(jnp.int32).reshape(1)

    first = _own_block_copies(_first_targets)
    first_flight = _start_copies([_with_own_slot(w_in[0].T.astype(BF16), me)], first, 2, core, "gather_w_in_start")

    c_all = _all_gather_small(c.reshape(8, 256), "gather_c", first_flight[3]).reshape(N_DEV, D_MODEL)
    device = me.astype(jnp.int32).reshape(1)
    c_act, mod_part = _modulation(device, c_all, w_ada[0], b_ada)
    mod_all = _all_gather_small(mod_part, "gather_mod")

    across = _wait_then_start(first_flight, lambda *a: first(*a)[1:], _second_axis_stage_copies, 3, mod_all,
                              "gather_w_in_second_axis_stage")
    mod = lax.dynamic_index_in_dim(mod_all, me, axis=1, keepdims=False).reshape(1, 3 * D_MODEL)
    mod = mod + across[3][0, 0]

    w_in_pair = _wait_copies((first_flight[0], first_flight[1], across[2], None), lambda *a: first(*a)[:1], mod,
                             "gather_w_in_sibling_wait")
    h, z_own = _norm_z_proj_own(x2d, norm_g, mod, w_in_pair[0].reshape(D_IN, D_MODEL), chip)
    w_out_early = _own_block_copies(lambda x, y, c: [(x, y, 1 - c), (*_second_axis_chip(x, y, c), c)])
    w_out_late = _own_block_copies(lambda x, y, c: [(*_first_axis_chip(x, y, c), c), (1 - x, 1 - y, c)])
    forward = _wait_then_start(
        (across[0], across[1], w_in_pair, None), lambda *a: _second_axis_stage_copies(*a)[:1],
        lambda refs, s, r: _second_axis_forward_copies(refs[:1], s, r) + _group(w_out_early, 1, 1, 1)(refs, s, r),
        3, z_own, "gather_w_in_second_axis_forward", more_bufs=[_with_own_slot(w_out[0].astype(BF16), me)])
    w_in_most = _wait_copies((across[0], across[1], forward[2][:1], None),
                             lambda *a: _second_axis_stage_copies(*a)[1:2], z_own, "gather_w_in_first_forward_wait")
    w_in_most = _wait_copies((forward[0], forward[1], w_in_most, None), _second_axis_forward_copies, z_own,
                             "gather_w_in_second_forward_wait")
    z_early = _z_proj(h, w_in_most[0].reshape(D_IN, D_MODEL), chip, 1, _Z_EARLY_TILES - 1, z_own, "z_proj_early")
    last = _wait_then_start(
        (across[0], across[1], [w_in_most[0], forward[2][1]], None), lambda *a: _second_axis_stage_copies(*a)[2:],
        lambda refs, s, r: _diagonal_forward_copies(refs[:1], s, r) + _group(w_out_late, 1, 1, 1)(refs, s, r),
        3, z_early, "gather_w_in_last_stage")
    w_in_all = _wait_copies((last[0], last[1], last[2][:1], None), _diagonal_forward_copies, z_early,
                            "gather_w_in_last_wait")[0]
    w_in_t = w_in_all.reshape(D_IN, D_MODEL)
    z = _z_proj(h, w_in_t, chip, _Z_EARLY_TILES, 7 - _Z_EARLY_TILES, z_early, "z_proj_late")
    w_out_half = _wait_copies((forward[0], forward[1], last[2][1:], None), _group(w_out_early, 0, 1, 1), z,
                              "gather_w_out_early_wait")
    w_out_flight = _wait_then_start((last[0], last[1], w_out_half, None), _group(w_out_late, 0, 1, 1),
                                    _forward_copies, 3, z, "gather_w_out_forward_stage")
    sink_rows = jnp.repeat(attn_sinks.reshape(N_Q_HEADS), BLOCK).reshape(2, 1, 8 * BLOCK)
    sgu_bt = sgu_b[0].T
    a, probs, sink_probs = _mixer_fwd(z, sink_rows + w_out_flight[3][0, 0], sgu_ln_g, sgu_ln_b, sgu_w[0], sgu_bt)
    w_out_all = _wait_copies(w_out_flight, _forward_copies, a, "gather_w_out_forward_wait")[0]
    w_out_full = w_out_all.reshape(D_MODEL, D_MODEL)
    final_g_row = final_g.reshape(1, D_MODEL)
    dx2, dy, da, loss_part, d_final_g, d_gate = _out_proj_head(a, w_out_full, x2d, target, mod, final_g_row)

    dz, d_sinks, d_sgu_w, d_sgu_b, d_ln_g, d_ln_b, dw_out = _mixer_bwd(
        z, da, probs, sink_probs, sgu_ln_g, sgu_ln_b, sgu_w[0], jnp.swapaxes(sgu_w[0], 1, 2), sgu_bt, a, dy)
    pair_out = _pair_reduce(dw_out.reshape(4, 2, W_OUT_SHARD, D_MODEL), _every_chip, "w_out_grad_pair_reduce",
                            W_OUT_SHARD // 2)
    sgu_w_to_all = _group(_own_block_copies(_all_others), 2, 1, 3)
    both = _start_copies(
        [pair_out, lax.empty((3, W_OUT_SHARD, D_MODEL), BF16), _with_own_slot(d_sgu_w, me)],
        lambda refs, s, r: _chip_copies(refs[:2], s, r) + sgu_w_to_all(refs, s, r), 3 + N_DEV - 1, core,
        "w_out_grad_chip_and_sgu_w_gather_start")
    out_flight, sgu_w_flight = (both[0], both[1], both[2][:2], None), (both[0], both[1], both[2][2:], None)
    dw_in_t = _matmul(dz, h, "tn", BF16, 768, D_MODEL, "w_in_grad", dep=both[3])
    pair_in = _pair_reduce(dw_in_t.reshape(4, 2, W_IN_SHARD, D_MODEL), _first_hop_chips, "w_in_grad_pair_reduce",
                           W_IN_SHARD // 3)
    first_hop = lambda refs, s, r: _first_hop_copies(refs[:2], s, r) + _group(_late_pair_copies, 2, 2, 2)(refs, s, r)
    hop1 = _start_copies(
        [pair_in, lax.empty((2, W_IN_SHARD, D_MODEL), BF16), dw_in_t.reshape(N_DEV, W_IN_SHARD, D_MODEL),
         lax.empty((2, W_IN_SHARD, D_MODEL), BF16)], first_hop, 4, core, "w_in_grad_first_hop_start")
    grad_x, d_shift, d_scale, d_norm_g = _z_proj_bwd_norm(dz, w_in_t, x2d, dx2, norm_g, mod, hop1[3])

    partial = _pack_small(d_shift, d_scale, d_gate, d_norm_g, d_final_g, d_ln_g, d_ln_b, loss_part, d_sinks, d_sgu_b)
    small_flight = _start_copies([_with_own_slot(partial, me)], _own_block_copies(_all_others), N_DEV - 1, core,
                                 "small_grad_gather_start")
    _, land_first, dw_in_t, land_pair = _wait_copies(hop1, first_hop, small_flight[3], "w_in_grad_first_hop_wait")
    second_device = (4 * ((xi + ci) % 2) + 2 * ((yi + 1 - ci) % 2) + ci).astype(jnp.int32).reshape(1)
    relay = _relay_sum(second_device, dw_in_t, land_pair, land_first, W_IN_SHARD // 3)
    hop2 = _start_copies([relay, lax.empty((1, W_IN_SHARD, D_MODEL), BF16)], _second_hop_copies, 1, core,
                         "w_in_grad_second_hop_start")
    pair_out, land_out = _wait_copies(out_flight, _chip_copies, hop2[3], "w_out_grad_chip_wait")
    big = {"w_out": _adam_from_chips(chip, pair_out, [(land_out, k) for k in range(3)], w_out[0], m_w_out[0],
                                     v_w_out[0], "adam_w_out", 1024)}
    partial_all = _wait_copies(small_flight, _own_block_copies(_all_others), big["w_out"][0],
                               "small_grad_gather_wait")[0]
    d_sgu_w_all = _wait_copies(sgu_w_flight, _group(_own_block_copies(_all_others), 0, 1, 3), partial_all,
                               "sgu_w_grad_gather_wait")[0]
    weights = {"norm_g": norm_g, "b_ada": b_ada, "attn_sinks": attn_sinks, "sgu_ln_g": sgu_ln_g,
               "sgu_ln_b": sgu_ln_b, "sgu_w": sgu_w, "sgu_b": sgu_b, "final_g": final_g_row}
    moments_m = {"norm_g": m_norm_g, "b_ada": m_b_ada, "attn_sinks": m_attn_sinks, "sgu_ln_g": m_sgu_ln_g,
                 "sgu_ln_b": m_sgu_ln_b, "sgu_w": m_sgu_w, "sgu_b": m_sgu_b,
                 "final_g": m_final_g.reshape(1, D_MODEL)}
    moments_v = {"norm_g": v_norm_g, "b_ada": v_b_ada, "attn_sinks": v_attn_sinks, "sgu_ln_g": v_sgu_ln_g,
                 "sgu_ln_b": v_sgu_ln_b, "sgu_w": v_sgu_w, "sgu_b": v_sgu_b,
                 "final_g": v_final_g.reshape(1, D_MODEL)}
    loss, dmod_all, small = _adam_small(partial_all, d_sgu_w_all, weights, moments_m, moments_v)
    small["final_g"] = tuple(o.reshape(D_MODEL) for o in small["final_g"])

    big["w_ada"] = _adam_w_ada(device, c_act.T, dmod_all, w_ada[0], m_w_ada[0], v_w_ada[0])
    _, land_second = _wait_copies(hop2, _second_hop_copies, big["w_ada"][0], "w_in_grad_second_hop_wait")
    big["w_in"] = tuple(o.T for o in _adam_from_chips(
        device, dw_in_t, [(land_pair, 0), (land_first, 0), (land_second, 0)], w_in[0].T, m_w_in[0].T, v_w_in[0].T,
        "adam_w_in", 512))
    order = ["norm_g", "w_ada", "b_ada", "w_in", "attn_sinks", "sgu_ln_g", "sgu_ln_b", "sgu_w", "sgu_b", "w_out",
             "final_g"]
    outs = [loss.reshape(()), grad_x[None]]
    for k in range(4):
        for name in order:
            outs.append(big[name][k][None] if name in big else small[name][k])
    return tuple(outs)
```

```python
import jax
import jax.numpy as jnp
from jax import lax
from jax.experimental import pallas as pl
from jax.experimental.pallas import tpu as pltpu

F32 = jnp.float32
BF16 = jnp.bfloat16
MESH = pl.DeviceIdType.MESH

N_DEV = 8
D_MODEL = 2048
HEAD_DIM = 64
D_ATTN = 1024
N_Q_HEADS = 16
D_KV = 128
BLOCK = 128
D_SGU = 1024
SGU_GROUPS = 8
D_IN = 5376
W_IN_SHARD = D_IN // N_DEV
W_OUT_SHARD = D_MODEL // N_DEV
W_ADA_SHARD = 3 * D_MODEL // N_DEV
EPS = 1e-6
ATTN_SCALE = 0.125

ADAM_LR = 0.001
ADAM_B1 = 0.9
ADAM_B2 = 0.999
ADAM_EPS = 1e-08
ADAM_WD = 0.01
ADAM_STEP = 10

SEG_Q, SEG_KV, SEG_GA, SEG_U, SEG_VS, SEG_GS = 0, 1024, 1280, 2304, 3328, 4352

VMEM_LIMIT = 56 * 1024 * 1024

ROW_SHIFT, ROW_SCALE, ROW_GATE, ROW_NORM_G, ROW_FINAL_G, ROW_LN, ROW_MISC, ROW_SGU_B = 0, 1, 2, 3, 4, 5, 6, 8
SMALL_ROWS = 16


def _params(**kw):
    return pltpu.CompilerParams(vmem_limit_bytes=VMEM_LIMIT, **kw)


def _sigmoid(x):
    return 0.5 * (jnp.tanh(0.5 * x) + 1.0)


def _place():
    return lax.axis_index("x"), lax.axis_index("y"), lax.axis_index("c")


def _every_chip(x, y, c):
    return [0, 1, 2, 3]


def _first_hop_chips(x, y, c):
    first = _first_axis_chip(x, y, c)
    return [2 * first[0] + first[1], 2 * (1 - x) + (1 - y)]


def _pair_reduce(blocks, chips, name, row_chunk):
    _, _, r, cols = blocks.shape
    n = len(chips(0, 0, 0))
    assert r % row_chunk == 0

    def body(in_ref, out_ref, land, own, summed, send_sems, recv_sems, own_sems, out_sems):
        x, y, c = _place()
        sends, loads, stores = [], [], []
        for m in range(n):
            cp = pltpu.make_async_remote_copy(
                src_ref=in_ref.at[chips(x, y, 1 - c)[m], 1 - c], dst_ref=land.at[m], send_sem=send_sems.at[m],
                recv_sem=recv_sems.at[m], device_id=(x, y, 1 - c), device_id_type=MESH)
            cp.start()
            sends.append(cp)
            ld = pltpu.make_async_copy(in_ref.at[chips(x, y, c)[m], c], own.at[m], own_sems.at[m])
            ld.start()
            loads.append(ld)
        for m in range(n):
            sends[m].wait_recv()
            loads[m].wait()
            for k in range(r // row_chunk):
                rows = slice(k * row_chunk, (k + 1) * row_chunk)
                summed[m, rows, :] = (own[m, rows, :].astype(F32) + land[m, rows, :].astype(F32)).astype(BF16)
            st = pltpu.make_async_copy(summed.at[m], out_ref.at[m], out_sems.at[m])
            st.start()
            stores.append(st)
        for m in range(n):
            sends[m].wait_send()
            stores[m].wait()

    spec = pl.BlockSpec(memory_space=pl.ANY)
    return pl.pallas_call(
        body, name=name, out_shape=jax.ShapeDtypeStruct((n, r, cols), BF16),
        in_specs=[spec], out_specs=spec,
        scratch_shapes=[pltpu.VMEM((n, r, cols), BF16), pltpu.VMEM((n, r, cols), BF16), pltpu.VMEM((n, r, cols), BF16),
                        pltpu.SemaphoreType.DMA((n,)), pltpu.SemaphoreType.DMA((n,)), pltpu.SemaphoreType.DMA((n,)),
                        pltpu.SemaphoreType.DMA((n,))],
        compiler_params=_params(),
    )(blocks)


_HBM = pl.BlockSpec(memory_space=pltpu.HBM)
_SEM = pl.BlockSpec(memory_space=pltpu.SEMAPHORE)
_EFFECT = pltpu.SideEffectType.DATAFLOW_SIDE_EFFECTING


def _start_copies(bufs, copies, n_copies, after, name):
    nb = len(bufs)

    def body(*refs):
        for cp in copies(refs[:nb], refs[nb + 1], refs[nb + 2]):
            cp.start()
        refs[-1][...] = jnp.zeros_like(refs[-1])

    out = pl.pallas_call(
        body, name=name,
        out_shape=(pltpu.SemaphoreType.DMA((n_copies,)), pltpu.SemaphoreType.DMA((n_copies,)),
                   *[pltpu.HBM(b.shape, b.dtype) for b in bufs], jax.ShapeDtypeStruct((8, 128), F32)),
        in_specs=(_HBM,) * nb + (pl.BlockSpec(memory_space=pl.ANY),),
        out_specs=(_SEM, _SEM) + (_HBM,) * nb + (pl.BlockSpec(memory_space=pltpu.VMEM),),
        input_output_aliases={i: 2 + i for i in range(nb)},
        compiler_params=pltpu.CompilerParams(has_side_effects=_EFFECT),
    )(*[pltpu.with_memory_space_constraint(b, pltpu.HBM) for b in bufs], after)
    return out[0], out[1], list(out[2:2 + nb]), out[-1]


def _wait_copies(flight, copies, after, name):
    send_sems, recv_sems, bufs, _ = flight
    nb = len(bufs)

    def body(*refs):
        for cp in copies(refs[:nb], refs[nb], refs[nb + 1]):
            cp.wait_send()
            cp.wait_recv()

    return pl.pallas_call(
        body, name=name,
        out_shape=tuple(pltpu.HBM(b.shape, b.dtype) for b in bufs),
        in_specs=(_HBM,) * nb + (_SEM, _SEM, pl.BlockSpec(memory_space=pl.ANY)), out_specs=(_HBM,) * nb,
        input_output_aliases={i: i for i in range(nb)},
        compiler_params=pltpu.CompilerParams(has_side_effects=_EFFECT),
    )(*bufs, send_sems, recv_sems, after)


class _From:
    def __init__(self, sems, offset):
        self.sems, self.offset = sems, offset

    @property
    def at(self):
        return self

    def __getitem__(self, k):
        return self.sems.at[k + self.offset]


def _group(copies, first_buf, n_bufs, offset):
    def grouped(refs, send_sems, recv_sems):
        return copies(refs[first_buf:first_buf + n_bufs], _From(send_sems, offset), _From(recv_sems, offset))
    return grouped


def _wait_then_start(flight, waited, started, n_started, after, name, more_bufs=()):
    old_send, old_recv, bufs, _ = flight
    bufs = list(bufs) + [pltpu.with_memory_space_constraint(b, pltpu.HBM) for b in more_bufs]
    nb = len(bufs)

    def body(*refs):
        for cp in waited(refs[:nb], refs[nb], refs[nb + 1]):
            cp.wait_send()
            cp.wait_recv()
        for cp in started(refs[:nb], refs[nb + 3], refs[nb + 4]):
            cp.start()
        refs[-1][...] = jnp.zeros_like(refs[-1])

    out = pl.pallas_call(
        body, name=name,
        out_shape=(pltpu.SemaphoreType.DMA((n_started,)), pltpu.SemaphoreType.DMA((n_started,)),
                   *[pltpu.HBM(b.shape, b.dtype) for b in bufs], jax.ShapeDtypeStruct((8, 128), F32)),
        in_specs=(_HBM,) * nb + (_SEM, _SEM, pl.BlockSpec(memory_space=pl.ANY)),
        out_specs=(_SEM, _SEM) + (_HBM,) * nb + (pl.BlockSpec(memory_space=pltpu.VMEM),),
        input_output_aliases={i: 2 + i for i in range(nb)},
        compiler_params=pltpu.CompilerParams(has_side_effects=_EFFECT),
    )(*bufs, old_send, old_recv, after)
    return out[0], out[1], list(out[2:2 + nb]), out[-1]


def _late_pair_copies(refs, send_sems, recv_sems):
    blocks_ref, land_ref = refs
    x, y, c = _place()
    first = _first_axis_chip(x, y, c)
    devices = [4 * x + 2 * y + 1 - c, 4 * first[0] + 2 * first[1] + 1 - c]
    return [pltpu.make_async_remote_copy(
        src_ref=blocks_ref.at[devices[k]], dst_ref=land_ref.at[k], send_sem=send_sems.at[k], recv_sem=recv_sems.at[k],
        device_id=(x, y, 1 - c), device_id_type=MESH) for k in range(2)]


def _chip_copies(refs, send_sems, recv_sems):
    pair_ref, land_ref = refs
    x, y, c = _place()
    chips = [(1 - x, y), (x, 1 - y), (1 - x, 1 - y)]
    return [pltpu.make_async_remote_copy(
        src_ref=pair_ref.at[2 * chip[0] + chip[1]], dst_ref=land_ref.at[k],
        send_sem=send_sems.at[k], recv_sem=recv_sems.at[k],
        device_id=(*chip, c), device_id_type=MESH) for k, chip in enumerate(chips)]


def _first_hop_copies(refs, send_sems, recv_sems):
    pair_ref, land_ref = refs
    x, y, c = _place()
    return [pltpu.make_async_remote_copy(
        src_ref=pair_ref.at[k], dst_ref=land_ref.at[k], send_sem=send_sems.at[k], recv_sem=recv_sems.at[k],
        device_id=(*_first_axis_chip(x, y, c), c), device_id_type=MESH) for k in range(2)]


def _second_hop_copies(refs, send_sems, recv_sems):
    relay_ref, land_ref = refs
    x, y, c = _place()
    second = ((x + c) % 2, (y + 1 - c) % 2)
    return [pltpu.make_async_remote_copy(
        src_ref=relay_ref, dst_ref=land_ref.at[0], send_sem=send_sems.at[0], recv_sem=recv_sems.at[0],
        device_id=(*second, c), device_id_type=MESH)]


def _own_block_copies(targets):
    def copies(refs, send_sems, recv_sems):
        x, y, c = _place()
        mine = refs[0].at[4 * x + 2 * y + c]
        return [pltpu.make_async_remote_copy(
            src_ref=mine, dst_ref=mine, send_sem=send_sems.at[k], recv_sem=recv_sems.at[k],
            device_id=to, device_id_type=MESH) for k, to in enumerate(targets(x, y, c))]
    return copies


def _all_others(x, y, c):
    flip = lambda v, f: 1 - v if f else v
    return [(flip(x, r & 4), flip(y, r & 2), flip(c, r & 1)) for r in range(1, N_DEV)]


def _forward_copies(refs, send_sems, recv_sems):
    x, y, c = _place()
    chips = [(1 - x, y), (x, 1 - y), (1 - x, 1 - y)]
    return [pltpu.make_async_remote_copy(
        src_ref=refs[0].at[4 * chip[0] + 2 * chip[1] + c], dst_ref=refs[0].at[4 * chip[0] + 2 * chip[1] + c],
        send_sem=send_sems.at[k], recv_sem=recv_sems.at[k],
        device_id=(x, y, 1 - c), device_id_type=MESH) for k, chip in enumerate(chips)]


def _first_axis_chip(x, y, c):
    return (x + 1 - c) % 2, (y + c) % 2


def _second_axis_chip(x, y, c):
    return (x + c) % 2, (y + 1 - c) % 2


def _first_targets(x, y, c):
    return [(x, y, 1 - c), (*_first_axis_chip(x, y, c), c)]


def _all_gather_small(shard, name, dep=None):
    def body(in_ref, *refs):
        out_ref, send_sems, recv_sems, local_sem = refs[-4:]
        x, y, c = _place()
        me, sibling = 4 * x + 2 * y + c, (x, y, 1 - c)
        first, second = _first_axis_chip(x, y, c), _second_axis_chip(x, y, c)

        def pair(chip):
            return out_ref.at[pl.ds(2 * (2 * chip[0] + chip[1]), 2)]

        def exchange(k, src, dst, to):
            cp = pltpu.make_async_remote_copy(src_ref=src, dst_ref=dst, send_sem=send_sems.at[k],
                                              recv_sem=recv_sems.at[k], device_id=to, device_id_type=MESH)
            cp.start()
            cp.wait()

        own = pltpu.make_async_copy(in_ref, out_ref.at[me], local_sem)
        own.start()
        exchange(0, in_ref, out_ref.at[me], sibling)
        own.wait()
        exchange(1, pair((x, y)), pair((x, y)), (*second, c))
        exchange(2, pair(second), pair(second), sibling)
        exchange(3, pair(first), pair(first), (*second, c))

    spec = pl.BlockSpec(memory_space=pltpu.VMEM)
    deps = [] if dep is None else [dep]
    return pl.pallas_call(
        body, name=name, out_shape=jax.ShapeDtypeStruct((N_DEV,) + shard.shape, shard.dtype),
        in_specs=[spec] * (1 + len(deps)), out_specs=spec,
        scratch_shapes=[pltpu.SemaphoreType.DMA((4,)), pltpu.SemaphoreType.DMA((4,)), pltpu.SemaphoreType.DMA],
        compiler_params=_params(),
    )(shard, *deps)


def _slot_copies(refs, send_sems, recv_sems, plan):
    copies = []
    for k, ((px, py, pc), to) in enumerate(plan):
        blk = refs[0].at[4 * px + 2 * py + pc]
        copies.append(pltpu.make_async_remote_copy(
            src_ref=blk, dst_ref=blk, send_sem=send_sems.at[k], recv_sem=recv_sems.at[k],
            device_id=to, device_id_type=MESH))
    return copies


def _second_axis_stage_copies(refs, send_sems, recv_sems):
    x, y, c = _place()
    first, second = (*_first_axis_chip(x, y, c), c), (*_second_axis_chip(x, y, c), c)
    return _slot_copies(refs, send_sems, recv_sems, [((x, y, c), second), (first, (x, y, 1 - c)), (first, second)])


def _second_axis_forward_copies(refs, send_sems, recv_sems):
    x, y, c = _place()
    return _slot_copies(refs, send_sems, recv_sems, [((*_second_axis_chip(x, y, c), c), (x, y, 1 - c))])


def _diagonal_forward_copies(refs, send_sems, recv_sems):
    x, y, c = _place()
    blk = refs[0].at[4 * (1 - x) + 2 * (1 - y) + c]
    return [pltpu.make_async_remote_copy(
        src_ref=blk, dst_ref=blk, send_sem=send_sems.at[0], recv_sem=recv_sems.at[0],
        device_id=(x, y, 1 - c), device_id_type=MESH)]


def _with_own_slot(block, me):
    return lax.dynamic_update_index_in_dim(lax.empty((N_DEV,) + block.shape, block.dtype), block, me, 0)


def _w_in_grad(dz, h, dep, tm):
    (t, m), n = dz.shape, h.shape[1]
    assert m % tm == 0 and dz.dtype == BF16 and h.dtype == BF16

    def body(dz_ref, h_ref, dep_ref, o_ref):
        o_ref[...] = _dot_tn(dz_ref[...], h_ref[...]).astype(BF16)

    return pl.pallas_call(
        body, name="w_in_grad", grid=(m // tm,),
        in_specs=[pl.BlockSpec((t, tm), lambda i: (0, i)), pl.BlockSpec((t, n), lambda i: (0, 0)),
                  pl.BlockSpec((8, 128), lambda i: (0, 0))],
        out_specs=pl.BlockSpec((tm, n), lambda i: (i, 0)),
        out_shape=jax.ShapeDtypeStruct((m, n), BF16),
        compiler_params=_params(dimension_semantics=("arbitrary",)),
    )(dz, h, dep)


Z_TILE = 768
_Z_TILE_ORDER = ((0, 1, 2, 3, 4, 5, 6), (2, 0, 1, 6, 3, 4, 5), (4, 0, 5, 6, 1, 2, 3), (6, 2, 3, 4, 0, 1, 5))
_Z_EARLY_TILES = 4


def _z_proj(h, w_in_t, chip, first, count, z_prev, name, tr=1024):
    t = h.shape[0]

    def body(chip_ref, h_ref, w_ref, z_prev_ref, z_ref):
        z_ref[...] = _dot_nt(h_ref[...], w_ref[...])

    def tile(j, chip_ref):
        picked = 0
        for c, order in enumerate(_Z_TILE_ORDER):
            for k in range(count):
                picked = picked + jnp.where((chip_ref[0] == c) & (j == k), order[first + k], 0)
        return picked

    return pl.pallas_call(
        body, name=name,
        grid_spec=pltpu.PrefetchScalarGridSpec(
            num_scalar_prefetch=1, grid=(count, t // tr),
            in_specs=[pl.BlockSpec((tr, D_MODEL), lambda j, i, o: (i, 0)),
                      pl.BlockSpec((Z_TILE, D_MODEL), lambda j, i, o: (tile(j, o), 0)),
                      pl.BlockSpec(memory_space=pl.ANY)],
            out_specs=pl.BlockSpec((tr, Z_TILE), lambda j, i, o: (i, tile(j, o)))),
        out_shape=jax.ShapeDtypeStruct((t, D_IN), F32),
        input_output_aliases={3: 0},
        compiler_params=_params(dimension_semantics=("arbitrary", "arbitrary")),
    )(chip, h, w_in_t, z_prev)


def _modulation(device, c_all, w_ada, b_ada):
    def body(device_ref, c_ref, w_ref, b_ref, act_ref, mod_ref):
        cv = c_ref[...]
        act = cv * _sigmoid(cv)
        act_ref[...] = act
        mod_ref[...] = jnp.dot(act.astype(BF16), w_ref[...].astype(BF16), preferred_element_type=F32) + b_ref[...]

    whole = lambda a: pl.BlockSpec(a.shape, lambda i, device_ref: (0,) * a.ndim)
    return pl.pallas_call(
        body, name="modulation",
        grid_spec=pltpu.PrefetchScalarGridSpec(
            num_scalar_prefetch=1, grid=(1,),
            in_specs=[whole(c_all), whole(w_ada), pl.BlockSpec((1, W_ADA_SHARD), lambda i, device_ref: (0, device_ref[0]))],
            out_specs=(whole(c_all), pl.BlockSpec((N_DEV, W_ADA_SHARD), lambda i, device_ref: (0, 0)))),
        out_shape=(jax.ShapeDtypeStruct(c_all.shape, F32), jax.ShapeDtypeStruct((N_DEV, W_ADA_SHARD), F32)),
        compiler_params=_params(dimension_semantics=("arbitrary",)),
    )(device, c_all, w_ada, b_ada)


MOD_SHIFT, MOD_SCALE, MOD_GATE = 0, 1, 2


def _mod_spec(part, d):
    return pl.BlockSpec((1, d), lambda i: (0, part))


def _norm_z_proj_own(x, norm_g, mod, w_in_t, chip, tm=512):
    t, d = x.shape

    def body(chip_ref, x_ref, g_ref, sc_ref, sh_ref, w_ref, h_ref, z_ref):
        xv = x_ref[...]
        r = lax.rsqrt(jnp.mean(xv * xv, axis=-1, keepdims=True) + EPS)
        h = ((xv * r) * g_ref[...] * (1.0 + sc_ref[...]) + sh_ref[...]).astype(BF16)
        h_ref[...] = h
        z_ref[...] = _dot_nt(h, w_ref[...])

    def own_tile(chip_ref):
        picked = 0
        for c, order in enumerate(_Z_TILE_ORDER):
            picked = picked + jnp.where(chip_ref[0] == c, order[0], 0)
        return picked

    def row(part):
        return pl.BlockSpec((1, d), lambda i, o: (0, part))

    return pl.pallas_call(
        body, name="norm_z_proj_own",
        grid_spec=pltpu.PrefetchScalarGridSpec(
            num_scalar_prefetch=1, grid=(t // tm,),
            in_specs=[pl.BlockSpec((tm, d), lambda i, o: (i, 0)), row(0), row(MOD_SCALE), row(MOD_SHIFT),
                      pl.BlockSpec((Z_TILE, d), lambda i, o: (own_tile(o), 0))],
            out_specs=(pl.BlockSpec((tm, d), lambda i, o: (i, 0)),
                       pl.BlockSpec((tm, Z_TILE), lambda i, o: (i, own_tile(o))))),
        out_shape=(jax.ShapeDtypeStruct((t, d), BF16), jax.ShapeDtypeStruct((t, D_IN), F32)),
        compiler_params=_params(dimension_semantics=("arbitrary",)),
    )(chip, x, norm_g, mod, mod, w_in_t)


def _window_bias(block_index):
    s = lax.broadcasted_iota(jnp.int32, (2 * BLOCK, BLOCK), 0)
    t = lax.broadcasted_iota(jnp.int32, (2 * BLOCK, BLOCK), 1)
    valid = ((s < BLOCK) & (s > t) & (block_index > 0)) | ((s >= BLOCK) & ((s - BLOCK) <= t))
    bias = jnp.where(valid, 0.0, -jnp.inf).astype(F32)
    return jnp.concatenate([bias] * 8, axis=1)


def _heads_t(pair_blocks, g):
    top = lax.broadcasted_iota(jnp.int32, (BLOCK, BLOCK), 0) < HEAD_DIM
    zeros = jnp.zeros((HEAD_DIM, BLOCK), F32)
    tiles = []
    for blk in pair_blocks:
        tp = blk.T
        if g == 0:
            tiles += [jnp.where(top, tp, 0.0), jnp.concatenate([tp[HEAD_DIM:], zeros], axis=0)]
        else:
            tiles += [jnp.concatenate([zeros, tp[:HEAD_DIM]], axis=0), jnp.where(top, 0.0, tp)]
    return jnp.concatenate(tiles, axis=1)


def _pair_block(xt, p, g):
    r0 = HEAD_DIM * g
    even = xt[r0:r0 + HEAD_DIM, (2 * p) * BLOCK:(2 * p + 1) * BLOCK]
    odd = xt[r0:r0 + HEAD_DIM, (2 * p + 1) * BLOCK:(2 * p + 2) * BLOCK]
    return jnp.concatenate([even, odd], axis=0).T


def _softmax_t(scores_t, bias, sink):
    st = scores_t + bias
    m = jnp.maximum(jnp.max(st, axis=0, keepdims=True), sink)
    e = jnp.exp(st - m)
    es = jnp.exp(sink - m)
    inv = 1.0 / (jnp.sum(e, axis=0, keepdims=True) + es)
    return e * inv, es * inv


def _dot(a, b):
    return jnp.dot(a, b, preferred_element_type=F32)


def _dot_nt(a, b):
    return lax.dot_general(a, b, (((1,), (1,)), ((), ())), preferred_element_type=F32)


def _dot_tn(a, b):
    return lax.dot_general(a, b, (((0,), (0,)), ((), ())), preferred_element_type=F32)


def _layer_norm_fwd(v):
    mu = jnp.mean(v, axis=-1, keepdims=True)
    xc = v - mu
    rstd = lax.rsqrt(jnp.mean(xc * xc, axis=-1, keepdims=True) + EPS)
    return xc * rstd, rstd


def _tril(transposed=False):
    t = lax.broadcasted_iota(jnp.int32, (BLOCK, BLOCK), 0)
    s = lax.broadcasted_iota(jnp.int32, (BLOCK, BLOCK), 1)
    return s >= t if transposed else t >= s


def _const_spec(shape):
    return pl.BlockSpec(shape, lambda i: (0,) * len(shape))


def _keys_values(z_ref, kvp):
    kvc = z_ref[:, SEG_KV:SEG_KV + 2 * D_KV]
    kk = jnp.concatenate([kvp[:, :D_KV], kvc[:, :D_KV]], axis=0)
    vv = jnp.concatenate([kvp[:, D_KV:], kvc[:, D_KV:]], axis=0)
    return kk, vv


MIXER_BLOCKS = 2


class _Rows:
    def __init__(self, ref, sub):
        self.ref, self.rows = ref, slice(sub * BLOCK, (sub + 1) * BLOCK)

    def __getitem__(self, idx):
        return self.ref[self.rows, idx[1]]

    def __setitem__(self, idx, value):
        self.ref[self.rows, idx[1]] = value


def _kv_before_spec(index):
    return pl.BlockSpec((BLOCK, 2 * D_KV),
                        lambda i: (jnp.maximum(MIXER_BLOCKS * index(i) - 1, 0), SEG_KV // (2 * D_KV)))


def _pair_cols(g, p, base=0):
    return slice(base + (4 * g + p) * 128, base + (4 * g + p + 1) * 128)


def _mixer_fwd(z, sink_rows, ln_g, ln_b, sgu_w, sgu_bt):
    t = z.shape[0]

    def body(z_all, kvp_ref, sink_ref, lng_ref, lnb_ref, w_ref, bt_ref, a_all, prob_ref, sink_prob_ref):
        kv_before = kvp_ref[...]
        for sub in range(MIXER_BLOCKS):
            z_ref, a_ref = _Rows(z_all, sub), _Rows(a_all, sub)
            one_block(z_ref, kv_before, MIXER_BLOCKS * pl.program_id(0) + sub, sink_ref, lng_ref, lnb_ref, w_ref,
                      bt_ref, a_ref, prob_ref.at[sub], sink_prob_ref.at[sub])
            kv_before = z_ref[:, SEG_KV:SEG_KV + 2 * D_KV]

    def one_block(z_ref, kv_before, block_index, sink_ref, lng_ref, lnb_ref, w_ref, bt_ref, a_ref, prob_ref,
                  sink_prob_ref):
        bias = _window_bias(block_index)
        kk, vv = _keys_values(z_ref, kv_before)
        kk_b, vvt_b = kk.astype(BF16), vv.T.astype(BF16)
        for g in range(2):
            qt = _heads_t([z_ref[:, _pair_cols(g, p, SEG_Q)] * ATTN_SCALE for p in range(4)], g).astype(BF16)
            prob, sink_prob = _softmax_t(_dot(kk_b, qt), bias, sink_ref[g])
            prob_b = prob.astype(BF16)
            prob_ref[g] = prob_b
            sink_prob_ref[g] = sink_prob
            ot = _dot(vvt_b, prob_b)
            for p in range(4):
                gate = z_ref[:, _pair_cols(g, p, SEG_GA)]
                a_ref[:, _pair_cols(g, p)] = (_pair_block(ot, p, g) * (gate * _sigmoid(gate))).astype(BF16)

        vhat, _ = _layer_norm_fwd(z_ref[:, SEG_VS:SEG_VS + D_SGU])
        vn = vhat * lng_ref[...] + lnb_ref[...]
        tril = _tril()
        for g in range(SGU_GROUPS):
            cols = slice(g * 128, (g + 1) * 128)
            wm = jnp.where(tril, w_ref[g], 0.0).astype(BF16)
            mixed = _dot(wm, vn[:, cols].astype(BF16)) + bt_ref[:, g:g + 1]
            gate = z_ref[:, SEG_GS + g * 128:SEG_GS + (g + 1) * 128]
            a_ref[:, D_ATTN + g * 128:D_ATTN + (g + 1) * 128] = (
                (z_ref[:, SEG_U + g * 128:SEG_U + (g + 1) * 128] * mixed) * (gate * _sigmoid(gate))).astype(BF16)

    rows = MIXER_BLOCKS * BLOCK
    return pl.pallas_call(
        body, name="mixer_fwd", grid=(t // rows,),
        in_specs=[pl.BlockSpec((rows, D_IN), lambda i: (i, 0)), _kv_before_spec(lambda i: i),
                  _const_spec((2, 1, 8 * BLOCK)), _const_spec((1, D_SGU)), _const_spec((1, D_SGU)),
                  _const_spec((SGU_GROUPS, BLOCK, BLOCK)), _const_spec((BLOCK, SGU_GROUPS))],
        out_specs=(pl.BlockSpec((rows, D_MODEL), lambda i: (i, 0)),
                   pl.BlockSpec((MIXER_BLOCKS, 2, 2 * BLOCK, 8 * BLOCK), lambda i: (i, 0, 0, 0)),
                   pl.BlockSpec((MIXER_BLOCKS, 2, 1, 8 * BLOCK), lambda i: (i, 0, 0, 0))),
        out_shape=(jax.ShapeDtypeStruct((t, D_MODEL), BF16),
                   jax.ShapeDtypeStruct((t // BLOCK, 2, 2 * BLOCK, 8 * BLOCK), BF16),
                   jax.ShapeDtypeStruct((t // BLOCK, 2, 1, 8 * BLOCK), F32)),
        compiler_params=_params(dimension_semantics=("arbitrary",)),
    )(z, z, sink_rows, ln_g, ln_b, sgu_w, sgu_bt)


def _mixer_bwd(z, da, probs, sink_probs, ln_g, ln_b, sgu_w, sgu_wt, sgu_bt, a, dy):
    t = z.shape[0]

    def body(z_all, kvp_ref, da_all, prob_ref, sink_prob_ref, lng_ref, lnb_ref, w_ref, wt_ref, bt_ref, a_ref, dy_ref,
             dz_all, dsink_ref, dw_ref, db_ref, dlng_ref, dlnb_ref, dw_out_ref, carry_ref, dsink_acc, dbt_acc):
        step = pl.program_id(0)

        @pl.when(step == 0)
        def _():
            carry_ref[...] = jnp.zeros_like(carry_ref)
            dsink_acc[...] = jnp.zeros_like(dsink_acc)
            dbt_acc[...] = jnp.zeros_like(dbt_acc)
            dw_ref[...] = jnp.zeros_like(dw_ref)
            dlng_ref[...] = jnp.zeros_like(dlng_ref)
            dlnb_ref[...] = jnp.zeros_like(dlnb_ref)

        carry = carry_ref[...]
        dw_out_ref[...] = _dot_tn(a_ref[...], dy_ref[...]).astype(BF16)
        for sub in reversed(range(MIXER_BLOCKS)):
            kv_before = kvp_ref[...] if sub == 0 else _Rows(z_all, sub - 1)[:, SEG_KV:SEG_KV + 2 * D_KV]
            carry = one_block(_Rows(z_all, sub), kv_before, _Rows(da_all, sub), prob_ref.at[sub], sink_prob_ref.at[sub],
                              carry, lng_ref, lnb_ref, w_ref, wt_ref, bt_ref, _Rows(dz_all, sub),
                              dw_ref, dlng_ref, dlnb_ref, dsink_acc, dbt_acc)
        carry_ref[...] = carry

        @pl.when(step == ns - 1)
        def _():
            db_ref[...] = dbt_acc[...].T[:SGU_GROUPS]
            lane_row = lax.broadcasted_iota(jnp.int32, (1, 128), 1)
            d_sink = jnp.zeros((1, 128), F32)
            for g in range(2):
                acc = dsink_acc[g]
                for j in range(8):
                    head_sum = jnp.sum(acc[:, j * BLOCK:(j + 1) * BLOCK], axis=-1, keepdims=True)
                    d_sink = d_sink + jnp.where(lane_row == 8 * g + j, head_sum, 0.0)
            dsink_ref[...] = d_sink

    def one_block(z_ref, kv_before, da_ref, prob_ref, sink_prob_ref, carry, lng_ref, lnb_ref, w_ref, wt_ref, bt_ref,
                  dz_ref, dw_ref, dlng_ref, dlnb_ref, dsink_acc, dbt_acc):
        kk, vv = _keys_values(z_ref, kv_before)
        vv_b = vv.astype(BF16)
        kkt_b, vvt_b = kk.T.astype(BF16), vv.T.astype(BF16)
        dkk = jnp.zeros((2 * BLOCK, D_KV), F32)
        dvv = jnp.zeros((2 * BLOCK, D_KV), F32)
        for g in range(2):
            qt = _heads_t([z_ref[:, _pair_cols(g, p, SEG_Q)] * ATTN_SCALE for p in range(4)], g).astype(BF16)
            prob_b, sink_prob = prob_ref[g], sink_prob_ref[g]
            prob = prob_b.astype(F32)
            ot = _dot(vvt_b, prob_b)
            gates = [z_ref[:, _pair_cols(g, p, SEG_GA)] for p in range(4)]
            sig = [_sigmoid(gt) for gt in gates]
            d_attn = [da_ref[:, _pair_cols(g, p)] for p in range(4)]
            d_ot = _heads_t([d_attn[p] * (gates[p] * sig[p]) for p in range(4)], g).astype(BF16)
            d_prob = _dot(vv_b, d_ot)
            delta = jnp.sum(prob * d_prob, axis=0, keepdims=True)
            d_scores = (prob * (d_prob - delta)).astype(BF16)
            dsink_acc[g] -= sink_prob * delta
            d_qt = _dot(kkt_b, d_scores)
            dkk = dkk + _dot_nt(d_scores, qt)
            dvv = dvv + _dot_nt(prob_b, d_ot)
            for p in range(4):
                dz_ref[:, _pair_cols(g, p, SEG_Q)] = (_pair_block(d_qt, p, g) * ATTN_SCALE).astype(BF16)
                d_silu = sig[p] * (1.0 + gates[p] * (1.0 - sig[p]))
                dz_ref[:, _pair_cols(g, p, SEG_GA)] = (d_attn[p] * _pair_block(ot, p, g) * d_silu).astype(BF16)
        d_kv = jnp.concatenate([dkk, dvv], axis=1)
        dz_ref[:, SEG_KV:SEG_KV + 2 * D_KV] = (d_kv[BLOCK:] + carry).astype(BF16)

        vhat, rstd = _layer_norm_fwd(z_ref[:, SEG_VS:SEG_VS + D_SGU])
        lng = lng_ref[...]
        vn = vhat * lng + lnb_ref[...]
        tril, triu = _tril(), _tril(transposed=True)
        lane = lax.broadcasted_iota(jnp.int32, (BLOCK, 128), 1)
        d_bt = jnp.zeros((BLOCK, 128), F32)
        d_vn = []
        for g in range(SGU_GROUPS):
            cols = slice(g * 128, (g + 1) * 128)
            wm = jnp.where(tril, w_ref[g], 0.0).astype(BF16)
            wmt = jnp.where(triu, wt_ref[g], 0.0).astype(BF16)
            vn_g = vn[:, cols].astype(BF16)
            mixed = _dot(wm, vn_g) + bt_ref[:, g:g + 1]
            gate = z_ref[:, SEG_GS + g * 128:SEG_GS + (g + 1) * 128]
            u = z_ref[:, SEG_U + g * 128:SEG_U + (g + 1) * 128]
            d_out = da_ref[:, D_ATTN + g * 128:D_ATTN + (g + 1) * 128]
            sg = _sigmoid(gate)
            d_um = d_out * (gate * sg)
            dz_ref[:, SEG_U + g * 128:SEG_U + (g + 1) * 128] = (d_um * mixed).astype(BF16)
            dz_ref[:, SEG_GS + g * 128:SEG_GS + (g + 1) * 128] = (
                d_out * (u * mixed) * (sg * (1.0 + gate * (1.0 - sg)))).astype(BF16)
            d_mixed = d_um * u
            d_mixed_b = d_mixed.astype(BF16)
            dw_ref[g] += jnp.where(tril, _dot_nt(d_mixed_b, vn_g), 0.0)
            d_bt = d_bt + jnp.where(lane == g, jnp.sum(d_mixed, axis=-1, keepdims=True), 0.0)
            d_vn.append(_dot(wmt, d_mixed_b))
        dbt_acc[...] += d_bt
        d_vn = jnp.concatenate(d_vn, axis=1)
        dlng_ref[...] += jnp.sum(d_vn * vhat, axis=0, keepdims=True)
        dlnb_ref[...] += jnp.sum(d_vn, axis=0, keepdims=True)
        d_vhat = d_vn * lng
        d_v = rstd * (d_vhat - jnp.mean(d_vhat, axis=-1, keepdims=True)
                      - vhat * jnp.mean(d_vhat * vhat, axis=-1, keepdims=True))
        dz_ref[:, SEG_VS:SEG_VS + D_SGU] = d_v.astype(BF16)
        return d_kv[:BLOCK]

    rows = MIXER_BLOCKS * BLOCK
    ns = t // rows
    rev = lambda i: ns - 1 - i
    tile = a.shape[1] // ns
    assert tile % 128 == 0
    return pl.pallas_call(
        body, name="mixer_bwd", grid=(ns,),
        in_specs=[pl.BlockSpec((rows, D_IN), lambda i: (rev(i), 0)), _kv_before_spec(rev),
                  pl.BlockSpec((rows, D_MODEL), lambda i: (rev(i), 0)),
                  pl.BlockSpec((MIXER_BLOCKS, 2, 2 * BLOCK, 8 * BLOCK), lambda i: (rev(i), 0, 0, 0)),
                  pl.BlockSpec((MIXER_BLOCKS, 2, 1, 8 * BLOCK), lambda i: (rev(i), 0, 0, 0)),
                  _const_spec((1, D_SGU)), _const_spec((1, D_SGU)),
                  _const_spec((SGU_GROUPS, BLOCK, BLOCK)), _const_spec((SGU_GROUPS, BLOCK, BLOCK)),
                  _const_spec((BLOCK, SGU_GROUPS)), pl.BlockSpec((t, tile), lambda i: (0, i)),
                  pl.BlockSpec(dy.shape, lambda i: (0, 0), pipeline_mode=pl.Buffered(1))],
        out_specs=(pl.BlockSpec((rows, D_IN), lambda i: (rev(i), 0)), _const_spec((1, 128)),
                   _const_spec((SGU_GROUPS, BLOCK, BLOCK)), _const_spec((SGU_GROUPS, BLOCK)),
                   _const_spec((1, D_SGU)), _const_spec((1, D_SGU)),
                   pl.BlockSpec((tile, dy.shape[1]), lambda i: (i, 0))),
        out_shape=(jax.ShapeDtypeStruct((t, D_IN), BF16), jax.ShapeDtypeStruct((1, 128), F32),
                   jax.ShapeDtypeStruct((SGU_GROUPS, BLOCK, BLOCK), F32), jax.ShapeDtypeStruct((SGU_GROUPS, BLOCK), F32),
                   jax.ShapeDtypeStruct((1, D_SGU), F32), jax.ShapeDtypeStruct((1, D_SGU), F32),
                   jax.ShapeDtypeStruct((a.shape[1], dy.shape[1]), BF16)),
        scratch_shapes=[pltpu.VMEM((BLOCK, 2 * D_KV), F32), pltpu.VMEM((2, 1, 8 * BLOCK), F32),
                        pltpu.VMEM((BLOCK, 128), F32)],
        compiler_params=_params(dimension_semantics=("arbitrary",)),
    )(z, z, da, probs, sink_probs, ln_g, ln_b, sgu_w, sgu_wt, sgu_bt, a, dy)


def _out_proj_head(a, w_out_full, x, target, mod, final_g, tm=256):
    t, d = x.shape

    def body(a_ref, w_ref, x_ref, tg_ref, gate_ref, fg_ref, dx2_ref, dy_ref, da_ref, loss_ref, dfg_ref, dgate_ref):
        @pl.when(pl.program_id(0) == 0)
        def _():
            loss_ref[...] = jnp.zeros_like(loss_ref)
            dfg_ref[...] = jnp.zeros_like(dfg_ref)
            dgate_ref[...] = jnp.zeros_like(dgate_ref)

        yv, gate, fg = _dot(a_ref[...], w_ref[...]), gate_ref[...], fg_ref[...]
        x2 = x_ref[...] + gate * yv
        r2 = lax.rsqrt(jnp.mean(x2 * x2, axis=-1, keepdims=True) + EPS)
        nrm = x2 * r2
        err = nrm * fg - tg_ref[...]
        loss_ref[...] += 0.5 * jnp.sum(jnp.mean(err * err, axis=-1, keepdims=True), axis=0, keepdims=True)
        fg_d = fg * (1.0 / d)
        err_nrm = err * nrm
        dfg_ref[...] += jnp.sum(err_nrm, axis=0, keepdims=True) * (1.0 / d)
        d_nrm = err * fg_d
        dx2 = r2 * (d_nrm - nrm * jnp.mean(err_nrm * fg_d, axis=-1, keepdims=True))
        dx2_ref[...] = dx2
        dgate_ref[...] += jnp.sum(dx2 * yv, axis=0, keepdims=True)
        dy = (dx2 * gate).astype(BF16)
        dy_ref[...] = dy
        da_ref[...] = _dot_nt(dy, w_ref[...])

    blk = pl.BlockSpec((tm, d), lambda i: (i, 0))
    a_blk = pl.BlockSpec((tm, a.shape[1]), lambda i: (i, 0))
    row = _const_spec((1, d))
    whole = pl.BlockSpec(w_out_full.shape, lambda i: (0, 0), pipeline_mode=pl.Buffered(1))
    return pl.pallas_call(
        body, name="out_proj_head", grid=(t // tm,),
        in_specs=[a_blk, whole, blk, blk, _mod_spec(MOD_GATE, d), row],
        out_specs=(blk, blk, a_blk, _const_spec((1, 128)), row, row),
        out_shape=(jax.ShapeDtypeStruct((t, d), F32), jax.ShapeDtypeStruct((t, d), BF16),
                   jax.ShapeDtypeStruct(a.shape, F32), jax.ShapeDtypeStruct((1, 128), F32),
                   jax.ShapeDtypeStruct((1, d), F32), jax.ShapeDtypeStruct((1, d), F32)),
        compiler_params=_params(dimension_semantics=("arbitrary",)),
    )(a, w_out_full, x, target, mod, final_g)


def _z_proj_bwd_norm(dz, w_in_t, x, dx2, norm_g, mod, dep, tm=256):
    t, d = x.shape

    def body(dz_ref, w_ref, x_ref, dx2_ref, g_ref, sc_ref, dep_ref, gx_ref, dshift_ref, dscale_ref, dg_ref):
        @pl.when(pl.program_id(0) == 0)
        def _():
            dshift_ref[...] = jnp.zeros_like(dshift_ref)
            dscale_ref[...] = jnp.zeros_like(dscale_ref)
            dg_ref[...] = jnp.zeros_like(dg_ref)

        dh, xv, g = _dot(dz_ref[...], w_ref[...]), x_ref[...], g_ref[...]
        one_plus = 1.0 + sc_ref[...]
        r = lax.rsqrt(jnp.mean(xv * xv, axis=-1, keepdims=True) + EPS)
        xn = xv * r
        gain = one_plus * g
        dh_xn = dh * xn
        dh_xn_sum = jnp.sum(dh_xn, axis=0, keepdims=True)
        dshift_ref[...] += jnp.sum(dh, axis=0, keepdims=True)
        dscale_ref[...] += dh_xn_sum * g
        dg_ref[...] += dh_xn_sum * one_plus
        d_xn = dh * gain
        gx_ref[...] = dx2_ref[...] + r * (d_xn - xn * jnp.mean(dh_xn * gain, axis=-1, keepdims=True))

    blk = pl.BlockSpec((tm, d), lambda i: (i, 0))
    row = _const_spec((1, d))
    whole = pl.BlockSpec(w_in_t.shape, lambda i: (0, 0), pipeline_mode=pl.Buffered(1))
    return pl.pallas_call(
        body, name="z_proj_bwd_norm", grid=(t // tm,),
        in_specs=[pl.BlockSpec((tm, dz.shape[1]), lambda i: (i, 0)), whole, blk, blk, row, _mod_spec(MOD_SCALE, d),
                  _const_spec((8, 128))],
        out_specs=(blk, row, row, row),
        out_shape=(jax.ShapeDtypeStruct((t, d), F32),) + (jax.ShapeDtypeStruct((1, d), F32),) * 3,
        compiler_params=_params(dimension_semantics=("arbitrary",)),
    )(dz, w_in_t, x, dx2, norm_g, mod, dep)


def _adamw(w, g, m, v):
    m = ADAM_B1 * m + (1.0 - ADAM_B1) * g
    v = ADAM_B2 * v + (1.0 - ADAM_B2) * (g * g)
    m_hat = m / (1.0 - ADAM_B1 ** ADAM_STEP)
    v_hat = v / (1.0 - ADAM_B2 ** ADAM_STEP)
    delta = -ADAM_LR * (m_hat / (jnp.sqrt(v_hat) + ADAM_EPS) + ADAM_WD * w)
    return delta, m, v


def _relay_sum(device, blocks, land_pair, land_first, tr):
    _, r, c = blocks.shape

    def body(device_ref, a_ref, b_ref, c_ref, o_ref):
        o_ref[...] = (a_ref[...].astype(F32) + b_ref[...].astype(F32) + c_ref[...].astype(F32)).astype(BF16)

    second = pl.BlockSpec((None, tr, c), lambda i, device_ref: (1, i, 0))
    return pl.pallas_call(
        body, name="w_in_grad_relay_sum",
        grid_spec=pltpu.PrefetchScalarGridSpec(
            num_scalar_prefetch=1, grid=(r // tr,),
            in_specs=[pl.BlockSpec((None, tr, c), lambda i, device_ref: (device_ref[0], i, 0)), second, second],
            out_specs=pl.BlockSpec((tr, c), lambda i, device_ref: (i, 0))),
        out_shape=jax.ShapeDtypeStruct((r, c), BF16),
        compiler_params=_params(dimension_semantics=("arbitrary",)),
    )(device, blocks, land_pair, land_first)


def _adam_from_chips(chip, pair, landed, w, m, v, name, tc):
    _, r, c = pair.shape
    n = len(landed)

    def body(chip_ref, own_ref, *refs):
        w_ref, m_ref, v_ref, g_ref, d_ref, nm_ref, nv_ref = refs[n:]
        g = own_ref[...].astype(F32)
        for k in range(n):
            g = g + refs[k][...].astype(F32)
        g_ref[...] = g
        d_ref[...], nm_ref[...], nv_ref[...] = _adamw(w_ref[...], g, m_ref[...], v_ref[...])

    def landed_spec(index):
        return pl.BlockSpec((None, r, tc), lambda i, chip_ref: (index, 0, i))

    blk = pl.BlockSpec((r, tc), lambda i, chip_ref: (0, i))
    return pl.pallas_call(
        body, name=name,
        grid_spec=pltpu.PrefetchScalarGridSpec(
            num_scalar_prefetch=1, grid=(c // tc,),
            in_specs=[pl.BlockSpec((None, r, tc), lambda i, chip_ref: (chip_ref[0], 0, i))]
            + [landed_spec(index) for _, index in landed] + [blk, blk, blk],
            out_specs=(blk,) * 4),
        out_shape=(jax.ShapeDtypeStruct((r, c), F32),) * 4,
        compiler_params=_params(dimension_semantics=("arbitrary",)),
    )(chip, pair, *[array for array, _ in landed], w, m, v)


def _adam_w_ada(device, act_t, dmod_all, w, m, v, tr=512):
    r, c = w.shape

    def body(device_ref, a_ref, dm_ref, w_ref, m_ref, v_ref, g_ref, d_ref, nm_ref, nv_ref):
        g = _dot(a_ref[...].astype(BF16), dm_ref[...].astype(BF16))
        g_ref[...] = g
        d_ref[...], nm_ref[...], nv_ref[...] = _adamw(w_ref[...], g, m_ref[...], v_ref[...])

    blk = pl.BlockSpec((tr, c), lambda i, device_ref: (i, 0))
    return pl.pallas_call(
        body, name="adam_w_ada",
        grid_spec=pltpu.PrefetchScalarGridSpec(
            num_scalar_prefetch=1, grid=(r // tr,),
            in_specs=[pl.BlockSpec((tr, N_DEV), lambda i, device_ref: (i, 0)),
                      pl.BlockSpec((N_DEV, c), lambda i, device_ref: (0, device_ref[0])), blk, blk, blk],
            out_specs=(blk,) * 4),
        out_shape=(jax.ShapeDtypeStruct((r, c), F32),) * 4,
        compiler_params=_params(dimension_semantics=("arbitrary",)),
    )(device, act_t, dmod_all, w, m, v)


def _pack_small(d_shift, d_scale, d_gate, d_norm_g, d_final_g, d_ln_g, d_ln_b, loss, d_sinks, d_sgu_b):
    def body(shift_ref, scale_ref, gate_ref, ng_ref, fg_ref, lng_ref, lnb_ref, loss_ref, sink_ref, b_ref, o_ref):
        o_ref[...] = jnp.zeros_like(o_ref)
        o_ref[ROW_SHIFT:ROW_SHIFT + 1, :] = shift_ref[...]
        o_ref[ROW_SCALE:ROW_SCALE + 1, :] = scale_ref[...]
        o_ref[ROW_GATE:ROW_GATE + 1, :] = gate_ref[...]
        o_ref[ROW_NORM_G:ROW_NORM_G + 1, :] = ng_ref[...]
        o_ref[ROW_FINAL_G:ROW_FINAL_G + 1, :] = fg_ref[...]
        o_ref[ROW_LN:ROW_LN + 1, 0:D_SGU] = lng_ref[...]
        o_ref[ROW_LN:ROW_LN + 1, D_SGU:2 * D_SGU] = lnb_ref[...]
        o_ref[ROW_MISC:ROW_MISC + 1, 0:128] = loss_ref[...]
        o_ref[ROW_MISC:ROW_MISC + 1, 128:256] = sink_ref[...]
        o_ref[ROW_SGU_B:ROW_SGU_B + SGU_GROUPS, 0:BLOCK] = b_ref[...]

    return pl.pallas_call(
        body, name="pack_small", out_shape=jax.ShapeDtypeStruct((SMALL_ROWS, D_MODEL), F32),
        compiler_params=_params(),
    )(d_shift, d_scale, d_gate, d_norm_g, d_final_g, d_ln_g, d_ln_b, loss, d_sinks, d_sgu_b)


_SMALL_NAMES = ("norm_g", "b_ada", "attn_sinks", "sgu_ln_g", "sgu_ln_b", "sgu_w", "sgu_b", "final_g")


def _adam_small(partials, d_sgu_w_all, weights, moments_m, moments_v):
    names = _SMALL_NAMES
    k = len(names)

    def body(*refs):
        p_ref, sw_ref = refs[0], refs[1]
        w_refs, m_refs, v_refs = refs[2:2 + k], refs[2 + k:2 + 2 * k], refs[2 + 2 * k:2 + 3 * k]
        loss_ref, dmod_ref = refs[2 + 3 * k], refs[3 + 3 * k]
        out_refs = refs[4 + 3 * k:4 + 7 * k]
        sum_ref = refs[4 + 7 * k]
        total = p_ref[0]
        for j in range(1, N_DEV):
            total = total + p_ref[j]
        sum_ref[...] = total
        for j in range(N_DEV):
            for part, row in enumerate((ROW_SHIFT, ROW_SCALE, ROW_GATE)):
                dmod_ref[j:j + 1, part * D_MODEL:(part + 1) * D_MODEL] = p_ref[j, row:row + 1, :]
        loss_ref[...] = sum_ref[ROW_MISC:ROW_MISC + 1, 0:1]
        d_sgu_w = sw_ref[0]
        for j in range(1, N_DEV):
            d_sgu_w = d_sgu_w + sw_ref[j]
        grads = {
            "norm_g": sum_ref[ROW_NORM_G:ROW_NORM_G + 1, :],
            "b_ada": jnp.concatenate([sum_ref[r:r + 1, :] for r in (ROW_SHIFT, ROW_SCALE, ROW_GATE)], axis=1),
            "attn_sinks": sum_ref[ROW_MISC:ROW_MISC + 1, 128:128 + N_Q_HEADS],
            "sgu_ln_g": sum_ref[ROW_LN:ROW_LN + 1, 0:D_SGU],
            "sgu_ln_b": sum_ref[ROW_LN:ROW_LN + 1, D_SGU:2 * D_SGU],
            "sgu_w": d_sgu_w[None],
            "sgu_b": sum_ref[ROW_SGU_B:ROW_SGU_B + SGU_GROUPS, 0:BLOCK][None],
            "final_g": sum_ref[ROW_FINAL_G:ROW_FINAL_G + 1, :],
        }
        for i, name in enumerate(names):
            g = grads[name]
            delta, m, v = _adamw(w_refs[i][...], g, m_refs[i][...], v_refs[i][...])
            out_refs[4 * i][...] = g
            out_refs[4 * i + 1][...] = delta
            out_refs[4 * i + 2][...] = m
            out_refs[4 * i + 3][...] = v

    shapes = [jax.ShapeDtypeStruct((1, 1), F32), jax.ShapeDtypeStruct((N_DEV, 3 * D_MODEL), F32)]
    for name in names:
        shapes += [jax.ShapeDtypeStruct(weights[name].shape, F32)] * 4
    outs = pl.pallas_call(
        body, name="adam_small", out_shape=tuple(shapes),
        scratch_shapes=[pltpu.VMEM((SMALL_ROWS, D_MODEL), F32)],
        compiler_params=_params(),
    )(partials, d_sgu_w_all, *[weights[n] for n in names], *[moments_m[n] for n in names],
      *[moments_v[n] for n in names])
    return outs[0], outs[1], {name: outs[2 + 4 * i:6 + 4 * i] for i, name in enumerate(names)}


def kernel(x, c, norm_g, w_ada, b_ada, w_in, attn_sinks, sgu_ln_g, sgu_ln_b, sgu_w, sgu_b, w_out, final_g, loss_target, m_norm_g, m_w_ada, m_b_ada, m_w_in, m_attn_sinks, m_sgu_ln_g, m_sgu_ln_b, m_sgu_w, m_sgu_b, m_w_out, m_final_g, v_norm_g, v_w_ada, v_b_ada, v_w_in, v_attn_sinks, v_sgu_ln_g, v_sgu_ln_b, v_sgu_w, v_sgu_b, v_w_out, v_final_g):
    xi, yi, ci = _place()
    me = 4 * xi + 2 * yi + ci
    x2d, target = x[0], loss_target[0]

    core = ci.astype(jnp.int32).reshape(1)
    chip = (2 * xi + yi).astype(jnp.int32).reshape(1)

    first = _own_block_copies(_first_targets)
    first_flight = _start_copies([_with_own_slot(w_in[0].T.astype(BF16), me)], first, 2, core, "gather_w_in_start")

    c_all = _all_gather_small(c.reshape(8, 256), "gather_c", first_flight[3]).reshape(N_DEV, D_MODEL)
    device = me.astype(jnp.int32).reshape(1)
    c_act, mod_part = _modulation(device, c_all, w_ada[0], b_ada)
    mod_all = _all_gather_small(mod_part, "gather_mod")

    across = _wait_then_start(first_flight, lambda *a: first(*a)[1:], _second_axis_stage_copies, 3, mod_all,
                              "gather_w_in_second_axis_stage")
    mod = lax.dynamic_index_in_dim(mod_all, me, axis=1, keepdims=False).reshape(1, 3 * D_MODEL)
    mod = mod + across[3][0, 0]

    w_in_pair = _wait_copies((first_flight[0], first_flight[1], across[2], None), lambda *a: first(*a)[:1], mod,
                             "gather_w_in_sibling_wait")
    h, z_own = _norm_z_proj_own(x2d, norm_g, mod, w_in_pair[0].reshape(D_IN, D_MODEL), chip)
    w_out_early = _own_block_copies(lambda x, y, c: [(x, y, 1 - c), (*_second_axis_chip(x, y, c), c)])
    w_out_late = _own_block_copies(lambda x, y, c: [(*_first_axis_chip(x, y, c), c), (1 - x, 1 - y, c)])
    forward = _wait_then_start(
        (across[0], across[1], w_in_pair, None), lambda *a: _second_axis_stage_copies(*a)[:1],
        lambda refs, s, r: _second_axis_forward_copies(refs[:1], s, r) + _group(w_out_early, 1, 1, 1)(refs, s, r),
        3, z_own, "gather_w_in_second_axis_forward", more_bufs=[_with_own_slot(w_out[0].astype(BF16), me)])
    w_in_most = _wait_copies((across[0], across[1], forward[2][:1], None),
                             lambda *a: _second_axis_stage_copies(*a)[1:2], z_own, "gather_w_in_first_forward_wait")
    w_in_most = _wait_copies((forward[0], forward[1], w_in_most, None), _second_axis_forward_copies, z_own,
                             "gather_w_in_second_forward_wait")
    z_early = _z_proj(h, w_in_most[0].reshape(D_IN, D_MODEL), chip, 1, _Z_EARLY_TILES - 1, z_own, "z_proj_early")
    last = _wait_then_start(
        (across[0], across[1], [w_in_most[0], forward[2][1]], None), lambda *a: _second_axis_stage_copies(*a)[2:],
        lambda refs, s, r: _diagonal_forward_copies(refs[:1], s, r) + _group(w_out_late, 1, 1, 1)(refs, s, r),
        3, z_early, "gather_w_in_last_stage")
    w_in_all = _wait_copies((last[0], last[1], last[2][:1], None), _diagonal_forward_copies, z_early,
                            "gather_w_in_last_wait")[0]
    w_in_t = w_in_all.reshape(D_IN, D_MODEL)
    z = _z_proj(h, w_in_t, chip, _Z_EARLY_TILES, 7 - _Z_EARLY_TILES, z_early, "z_proj_late")
    w_out_half = _wait_copies((forward[0], forward[1], last[2][1:], None), _group(w_out_early, 0, 1, 1), z,
                              "gather_w_out_early_wait")
    w_out_flight = _wait_then_start((last[0], last[1], w_out_half, None), _group(w_out_late, 0, 1, 1),
                                    _forward_copies, 3, z, "gather_w_out_forward_stage")
    sink_rows = jnp.repeat(attn_sinks.reshape(N_Q_HEADS), BLOCK).reshape(2, 1, 8 * BLOCK)
    sgu_bt = sgu_b[0].T
    a, probs, sink_probs = _mixer_fwd(z, sink_rows + w_out_flight[3][0, 0], sgu_ln_g, sgu_ln_b, sgu_w[0], sgu_bt)
    w_out_all = _wait_copies(w_out_flight, _forward_copies, a, "gather_w_out_forward_wait")[0]
    w_out_full = w_out_all.reshape(D_MODEL, D_MODEL)
    final_g_row = final_g.reshape(1, D_MODEL)
    dx2, dy, da, loss_part, d_final_g, d_gate = _out_proj_head(a, w_out_full, x2d, target, mod, final_g_row)

    dz, d_sinks, d_sgu_w, d_sgu_b, d_ln_g, d_ln_b, dw_out = _mixer_bwd(
        z, da, probs, sink_probs, sgu_ln_g, sgu_ln_b, sgu_w[0], jnp.swapaxes(sgu_w[0], 1, 2), sgu_bt, a, dy)
    pair_out = _pair_reduce(dw_out.reshape(4, 2, W_OUT_SHARD, D_MODEL), _every_chip, "w_out_grad_pair_reduce",
                            W_OUT_SHARD // 2)
    sgu_w_to_all = _group(_own_block_copies(_all_others), 2, 1, 3)
    both = _start_copies(
        [pair_out, lax.empty((3, W_OUT_SHARD, D_MODEL), BF16), _with_own_slot(d_sgu_w, me)],
        lambda refs, s, r: _chip_copies(refs[:2], s, r) + sgu_w_to_all(refs, s, r), 3 + N_DEV - 1, core,
        "w_out_grad_chip_and_sgu_w_gather_start")
    out_flight, sgu_w_flight = (both[0], both[1], both[2][:2], None), (both[0], both[1], both[2][2:], None)
    dw_in_t = _w_in_grad(dz, h, both[3], Z_TILE)
    pair_in = _pair_reduce(dw_in_t.reshape(4, 2, W_IN_SHARD, D_MODEL), _first_hop_chips, "w_in_grad_pair_reduce",
                           W_IN_SHARD // 3)
    first_hop = lambda refs, s, r: _first_hop_copies(refs[:2], s, r) + _group(_late_pair_copies, 2, 2, 2)(refs, s, r)
    hop1 = _start_copies(
        [pair_in, lax.empty((2, W_IN_SHARD, D_MODEL), BF16), dw_in_t.reshape(N_DEV, W_IN_SHARD, D_MODEL),
         lax.empty((2, W_IN_SHARD, D_MODEL), BF16)], first_hop, 4, core, "w_in_grad_first_hop_start")
    grad_x, d_shift, d_scale, d_norm_g = _z_proj_bwd_norm(dz, w_in_t, x2d, dx2, norm_g, mod, hop1[3])

    partial = _pack_small(d_shift, d_scale, d_gate, d_norm_g, d_final_g, d_ln_g, d_ln_b, loss_part, d_sinks, d_sgu_b)
    small_flight = _start_copies([_with_own_slot(partial, me)], _own_block_copies(_all_others), N_DEV - 1, core,
                                 "small_grad_gather_start")
    _, land_first, dw_in_t, land_pair = _wait_copies(hop1, first_hop, small_flight[3], "w_in_grad_first_hop_wait")
    second_device = (4 * ((xi + ci) % 2) + 2 * ((yi + 1 - ci) % 2) + ci).astype(jnp.int32).reshape(1)
    relay = _relay_sum(second_device, dw_in_t, land_pair, land_first, W_IN_SHARD // 3)
    hop2 = _start_copies([relay, lax.empty((1, W_IN_SHARD, D_MODEL), BF16)], _second_hop_copies, 1, core,
                         "w_in_grad_second_hop_start")
    pair_out, land_out = _wait_copies(out_flight, _chip_copies, hop2[3], "w_out_grad_chip_wait")
    big = {"w_out": _adam_from_chips(chip, pair_out, [(land_out, k) for k in range(3)], w_out[0], m_w_out[0],
                                     v_w_out[0], "adam_w_out", 1024)}
    partial_all = _wait_copies(small_flight, _own_block_copies(_all_others), big["w_out"][0],
                               "small_grad_gather_wait")[0]
    d_sgu_w_all = _wait_copies(sgu_w_flight, _group(_own_block_copies(_all_others), 0, 1, 3), partial_all,
                               "sgu_w_grad_gather_wait")[0]
    weights = {"norm_g": norm_g, "b_ada": b_ada, "attn_sinks": attn_sinks, "sgu_ln_g": sgu_ln_g,
               "sgu_ln_b": sgu_ln_b, "sgu_w": sgu_w, "sgu_b": sgu_b, "final_g": final_g_row}
    moments_m = {"norm_g": m_norm_g, "b_ada": m_b_ada, "attn_sinks": m_attn_sinks, "sgu_ln_g": m_sgu_ln_g,
                 "sgu_ln_b": m_sgu_ln_b, "sgu_w": m_sgu_w, "sgu_b": m_sgu_b,
                 "final_g": m_final_g.reshape(1, D_MODEL)}
    moments_v = {"norm_g": v_norm_g, "b_ada": v_b_ada, "attn_sinks": v_attn_sinks, "sgu_ln_g": v_sgu_ln_g,
                 "sgu_ln_b": v_sgu_ln_b, "sgu_w": v_sgu_w, "sgu_b": v_sgu_b,
                 "final_g": v_final_g.reshape(1, D_MODEL)}
    loss, dmod_all, small = _adam_small(partial_all, d_sgu_w_all, weights, moments_m, moments_v)
    small["final_g"] = tuple(o.reshape(D_MODEL) for o in small["final_g"])

    big["w_ada"] = _adam_w_ada(device, c_act.T, dmod_all, w_ada[0], m_w_ada[0], v_w_ada[0])
    _, land_second = _wait_copies(hop2, _second_hop_copies, big["w_ada"][0], "w_in_grad_second_hop_wait")
    big["w_in"] = tuple(o.T for o in _adam_from_chips(
        device, dw_in_t, [(land_pair, 0), (land_first, 0), (land_second, 0)], w_in[0].T, m_w_in[0].T, v_w_in[0].T,
        "adam_w_in", 512))
    order = ["norm_g", "w_ada", "b_ada", "w_in", "attn_sinks", "sgu_ln_g", "sgu_ln_b", "sgu_w", "sgu_b", "w_out",
             "final_g"]
    outs = [loss.reshape(()), grad_x[None]]
    for k in range(4):
        for name in order:
            outs.append(big[name][k][None] if name in big else small[name][k])
    return tuple(outs)
```

```python
import jax
import jax.numpy as jnp
from jax import lax
from jax.experimental import pallas as pl
from jax.experimental.pallas import tpu as pltpu

F32 = jnp.float32
BF16 = jnp.bfloat16
MESH = pl.DeviceIdType.MESH

N_DEV = 8
D_MODEL = 2048
HEAD_DIM = 64
D_ATTN = 1024
N_Q_HEADS = 16
D_KV = 128
BLOCK = 128
D_SGU = 1024
SGU_GROUPS = 8
D_IN = 5376
W_IN_SHARD = D_IN // N_DEV
W_OUT_SHARD = D_MODEL // N_DEV
W_ADA_SHARD = 3 * D_MODEL // N_DEV
EPS = 1e-6
ATTN_SCALE = 0.125

ADAM_LR = 0.001
ADAM_B1 = 0.9
ADAM_B2 = 0.999
ADAM_EPS = 1e-08
ADAM_WD = 0.01
ADAM_STEP = 10

SEG_Q, SEG_KV, SEG_GA, SEG_U, SEG_VS, SEG_GS = 0, 1024, 1280, 2304, 3328, 4352

VMEM_LIMIT = 56 * 1024 * 1024

ROW_SHIFT, ROW_SCALE, ROW_GATE, ROW_NORM_G, ROW_FINAL_G, ROW_LN, ROW_MISC, ROW_SGU_B = 0, 1, 2, 3, 4, 5, 6, 8
SMALL_ROWS = 16


def _params(**kw):
    return pltpu.CompilerParams(vmem_limit_bytes=VMEM_LIMIT, **kw)


def _sigmoid(x):
    return 0.5 * (jnp.tanh(0.5 * x) + 1.0)


def _place():
    return lax.axis_index("x"), lax.axis_index("y"), lax.axis_index("c")


def _every_chip(x, y, c):
    return [0, 1, 2, 3]


def _first_hop_chips(x, y, c):
    first = _first_axis_chip(x, y, c)
    return [2 * first[0] + first[1], 2 * (1 - x) + (1 - y)]


def _pair_reduce(blocks, chips, name, row_chunk):
    _, _, r, cols = blocks.shape
    n = len(chips(0, 0, 0))
    assert r % row_chunk == 0

    def body(in_ref, out_ref, land, own, summed, send_sems, recv_sems, own_sems, out_sems):
        x, y, c = _place()
        sends, loads, stores = [], [], []
        for m in range(n):
            cp = pltpu.make_async_remote_copy(
                src_ref=in_ref.at[chips(x, y, 1 - c)[m], 1 - c], dst_ref=land.at[m], send_sem=send_sems.at[m],
                recv_sem=recv_sems.at[m], device_id=(x, y, 1 - c), device_id_type=MESH)
            cp.start()
            sends.append(cp)
            ld = pltpu.make_async_copy(in_ref.at[chips(x, y, c)[m], c], own.at[m], own_sems.at[m])
            ld.start()
            loads.append(ld)
        for m in range(n):
            sends[m].wait_recv()
            loads[m].wait()
            for k in range(r // row_chunk):
                rows = slice(k * row_chunk, (k + 1) * row_chunk)
                summed[m, rows, :] = (own[m, rows, :].astype(F32) + land[m, rows, :].astype(F32)).astype(BF16)
            st = pltpu.make_async_copy(summed.at[m], out_ref.at[m], out_sems.at[m])
            st.start()
            stores.append(st)
        for m in range(n):
            sends[m].wait_send()
            stores[m].wait()

    spec = pl.BlockSpec(memory_space=pl.ANY)
    return pl.pallas_call(
        body, name=name, out_shape=jax.ShapeDtypeStruct((n, r, cols), BF16),
        in_specs=[spec], out_specs=spec,
        scratch_shapes=[pltpu.VMEM((n, r, cols), BF16), pltpu.VMEM((n, r, cols), BF16), pltpu.VMEM((n, r, cols), BF16),
                        pltpu.SemaphoreType.DMA((n,)), pltpu.SemaphoreType.DMA((n,)), pltpu.SemaphoreType.DMA((n,)),
                        pltpu.SemaphoreType.DMA((n,))],
        compiler_params=_params(),
    )(blocks)


_HBM = pl.BlockSpec(memory_space=pltpu.HBM)
_SEM = pl.BlockSpec(memory_space=pltpu.SEMAPHORE)
_EFFECT = pltpu.SideEffectType.DATAFLOW_SIDE_EFFECTING


def _start_copies(bufs, copies, n_copies, after, name):
    nb = len(bufs)

    def body(*refs):
        for cp in copies(refs[:nb], refs[nb + 1], refs[nb + 2]):
            cp.start()
        refs[-1][...] = jnp.zeros_like(refs[-1])

    out = pl.pallas_call(
        body, name=name,
        out_shape=(pltpu.SemaphoreType.DMA((n_copies,)), pltpu.SemaphoreType.DMA((n_copies,)),
                   *[pltpu.HBM(b.shape, b.dtype) for b in bufs], jax.ShapeDtypeStruct((8, 128), F32)),
        in_specs=(_HBM,) * nb + (pl.BlockSpec(memory_space=pl.ANY),),
        out_specs=(_SEM, _SEM) + (_HBM,) * nb + (pl.BlockSpec(memory_space=pltpu.VMEM),),
        input_output_aliases={i: 2 + i for i in range(nb)},
        compiler_params=pltpu.CompilerParams(has_side_effects=_EFFECT),
    )(*[pltpu.with_memory_space_constraint(b, pltpu.HBM) for b in bufs], after)
    return out[0], out[1], list(out[2:2 + nb]), out[-1]


def _wait_copies(flight, copies, after, name):
    send_sems, recv_sems, bufs, _ = flight
    nb = len(bufs)

    def body(*refs):
        for cp in copies(refs[:nb], refs[nb], refs[nb + 1]):
            cp.wait_send()
            cp.wait_recv()

    return pl.pallas_call(
        body, name=name,
        out_shape=tuple(pltpu.HBM(b.shape, b.dtype) for b in bufs),
        in_specs=(_HBM,) * nb + (_SEM, _SEM, pl.BlockSpec(memory_space=pl.ANY)), out_specs=(_HBM,) * nb,
        input_output_aliases={i: i for i in range(nb)},
        compiler_params=pltpu.CompilerParams(has_side_effects=_EFFECT),
    )(*bufs, send_sems, recv_sems, after)


class _From:
    def __init__(self, sems, offset):
        self.sems, self.offset = sems, offset

    @property
    def at(self):
        return self

    def __getitem__(self, k):
        return self.sems.at[k + self.offset]


def _group(copies, first_buf, n_bufs, offset):
    def grouped(refs, send_sems, recv_sems):
        return copies(refs[first_buf:first_buf + n_bufs], _From(send_sems, offset), _From(recv_sems, offset))
    return grouped


def _wait_then_start(flight, waited, started, n_started, after, name, more_bufs=()):
    old_send, old_recv, bufs, _ = flight
    bufs = list(bufs) + [pltpu.with_memory_space_constraint(b, pltpu.HBM) for b in more_bufs]
    nb = len(bufs)

    def body(*refs):
        for cp in waited(refs[:nb], refs[nb], refs[nb + 1]):
            cp.wait_send()
            cp.wait_recv()
        for cp in started(refs[:nb], refs[nb + 3], refs[nb + 4]):
            cp.start()
        refs[-1][...] = jnp.zeros_like(refs[-1])

    out = pl.pallas_call(
        body, name=name,
        out_shape=(pltpu.SemaphoreType.DMA((n_started,)), pltpu.SemaphoreType.DMA((n_started,)),
                   *[pltpu.HBM(b.shape, b.dtype) for b in bufs], jax.ShapeDtypeStruct((8, 128), F32)),
        in_specs=(_HBM,) * nb + (_SEM, _SEM, pl.BlockSpec(memory_space=pl.ANY)),
        out_specs=(_SEM, _SEM) + (_HBM,) * nb + (pl.BlockSpec(memory_space=pltpu.VMEM),),
        input_output_aliases={i: 2 + i for i in range(nb)},
        compiler_params=pltpu.CompilerParams(has_side_effects=_EFFECT),
    )(*bufs, old_send, old_recv, after)
    return out[0], out[1], list(out[2:2 + nb]), out[-1]


def _late_pair_copies(refs, send_sems, recv_sems):
    blocks_ref, land_ref = refs
    x, y, c = _place()
    first = _first_axis_chip(x, y, c)
    devices = [4 * x + 2 * y + 1 - c, 4 * first[0] + 2 * first[1] + 1 - c]
    return [pltpu.make_async_remote_copy(
        src_ref=blocks_ref.at[devices[k]], dst_ref=land_ref.at[k], send_sem=send_sems.at[k], recv_sem=recv_sems.at[k],
        device_id=(x, y, 1 - c), device_id_type=MESH) for k in range(2)]


def _chip_copies(refs, send_sems, recv_sems):
    pair_ref, land_ref = refs
    x, y, c = _place()
    chips = [(1 - x, y), (x, 1 - y), (1 - x, 1 - y)]
    return [pltpu.make_async_remote_copy(
        src_ref=pair_ref.at[2 * chip[0] + chip[1]], dst_ref=land_ref.at[k],
        send_sem=send_sems.at[k], recv_sem=recv_sems.at[k],
        device_id=(*chip, c), device_id_type=MESH) for k, chip in enumerate(chips)]


def _first_hop_copies(refs, send_sems, recv_sems):
    pair_ref, land_ref = refs
    x, y, c = _place()
    return [pltpu.make_async_remote_copy(
        src_ref=pair_ref.at[k], dst_ref=land_ref.at[k], send_sem=send_sems.at[k], recv_sem=recv_sems.at[k],
        device_id=(*_first_axis_chip(x, y, c), c), device_id_type=MESH) for k in range(2)]


def _second_hop_copies(refs, send_sems, recv_sems):
    relay_ref, land_ref = refs
    x, y, c = _place()
    second = ((x + c) % 2, (y + 1 - c) % 2)
    return [pltpu.make_async_remote_copy(
        src_ref=relay_ref, dst_ref=land_ref.at[0], send_sem=send_sems.at[0], recv_sem=recv_sems.at[0],
        device_id=(*second, c), device_id_type=MESH)]


def _own_block_copies(targets):
    def copies(refs, send_sems, recv_sems):
        x, y, c = _place()
        mine = refs[0].at[4 * x + 2 * y + c]
        return [pltpu.make_async_remote_copy(
            src_ref=mine, dst_ref=mine, send_sem=send_sems.at[k], recv_sem=recv_sems.at[k],
            device_id=to, device_id_type=MESH) for k, to in enumerate(targets(x, y, c))]
    return copies


def _all_others(x, y, c):
    flip = lambda v, f: 1 - v if f else v
    return [(flip(x, r & 4), flip(y, r & 2), flip(c, r & 1)) for r in range(1, N_DEV)]


def _forward_copies(refs, send_sems, recv_sems):
    x, y, c = _place()
    chips = [(1 - x, y), (x, 1 - y), (1 - x, 1 - y)]
    return [pltpu.make_async_remote_copy(
        src_ref=refs[0].at[4 * chip[0] + 2 * chip[1] + c], dst_ref=refs[0].at[4 * chip[0] + 2 * chip[1] + c],
        send_sem=send_sems.at[k], recv_sem=recv_sems.at[k],
        device_id=(x, y, 1 - c), device_id_type=MESH) for k, chip in enumerate(chips)]


def _first_axis_chip(x, y, c):
    return (x + 1 - c) % 2, (y + c) % 2


def _second_axis_chip(x, y, c):
    return (x + c) % 2, (y + 1 - c) % 2


def _first_targets(x, y, c):
    return [(x, y, 1 - c), (*_first_axis_chip(x, y, c), c)]


def _all_gather_small(shard, name, dep=None):
    def body(in_ref, *refs):
        out_ref, send_sems, recv_sems, local_sem = refs[-4:]
        x, y, c = _place()
        me, sibling = 4 * x + 2 * y + c, (x, y, 1 - c)
        first, second = _first_axis_chip(x, y, c), _second_axis_chip(x, y, c)

        def pair(chip):
            return out_ref.at[pl.ds(2 * (2 * chip[0] + chip[1]), 2)]

        def exchange(k, src, dst, to):
            cp = pltpu.make_async_remote_copy(src_ref=src, dst_ref=dst, send_sem=send_sems.at[k],
                                              recv_sem=recv_sems.at[k], device_id=to, device_id_type=MESH)
            cp.start()
            cp.wait()

        own = pltpu.make_async_copy(in_ref, out_ref.at[me], local_sem)
        own.start()
        exchange(0, in_ref, out_ref.at[me], sibling)
        own.wait()
        exchange(1, pair((x, y)), pair((x, y)), (*second, c))
        exchange(2, pair(second), pair(second), sibling)
        exchange(3, pair(first), pair(first), (*second, c))

    spec = pl.BlockSpec(memory_space=pltpu.VMEM)
    deps = [] if dep is None else [dep]
    return pl.pallas_call(
        body, name=name, out_shape=jax.ShapeDtypeStruct((N_DEV,) + shard.shape, shard.dtype),
        in_specs=[spec] * (1 + len(deps)), out_specs=spec,
        scratch_shapes=[pltpu.SemaphoreType.DMA((4,)), pltpu.SemaphoreType.DMA((4,)), pltpu.SemaphoreType.DMA],
        compiler_params=_params(),
    )(shard, *deps)


def _slot_copies(refs, send_sems, recv_sems, plan):
    copies = []
    for k, ((px, py, pc), to) in enumerate(plan):
        blk = refs[0].at[4 * px + 2 * py + pc]
        copies.append(pltpu.make_async_remote_copy(
            src_ref=blk, dst_ref=blk, send_sem=send_sems.at[k], recv_sem=recv_sems.at[k],
            device_id=to, device_id_type=MESH))
    return copies


def _second_axis_stage_copies(refs, send_sems, recv_sems):
    x, y, c = _place()
    first, second = (*_first_axis_chip(x, y, c), c), (*_second_axis_chip(x, y, c), c)
    return _slot_copies(refs, send_sems, recv_sems, [((x, y, c), second), (first, (x, y, 1 - c)), (first, second)])


def _second_axis_forward_copies(refs, send_sems, recv_sems):
    x, y, c = _place()
    return _slot_copies(refs, send_sems, recv_sems, [((*_second_axis_chip(x, y, c), c), (x, y, 1 - c))])


def _diagonal_forward_copies(refs, send_sems, recv_sems):
    x, y, c = _place()
    blk = refs[0].at[4 * (1 - x) + 2 * (1 - y) + c]
    return [pltpu.make_async_remote_copy(
        src_ref=blk, dst_ref=blk, send_sem=send_sems.at[0], recv_sem=recv_sems.at[0],
        device_id=(x, y, 1 - c), device_id_type=MESH)]


def _with_own_slot(block, me):
    return lax.dynamic_update_index_in_dim(lax.empty((N_DEV,) + block.shape, block.dtype), block, me, 0)


def _w_in_grad(dz, h, dep, tm):
    (t, m), n = dz.shape, h.shape[1]
    assert m % tm == 0 and dz.dtype == BF16 and h.dtype == BF16

    def body(dz_ref, h_ref, dep_ref, o_ref):
        o_ref[...] = _dot_tn(dz_ref[...], h_ref[...]).astype(BF16)

    return pl.pallas_call(
        body, name="w_in_grad", grid=(m // tm,),
        in_specs=[pl.BlockSpec((t, tm), lambda i: (0, i)), pl.BlockSpec((t, n), lambda i: (0, 0)),
                  pl.BlockSpec((8, 128), lambda i: (0, 0))],
        out_specs=pl.BlockSpec((tm, n), lambda i: (i, 0)),
        out_shape=jax.ShapeDtypeStruct((m, n), BF16),
        compiler_params=_params(dimension_semantics=("arbitrary",)),
    )(dz, h, dep)


Z_TILE = 768
_Z_TILE_ORDER = ((0, 1, 2, 3, 4, 5, 6), (2, 0, 1, 6, 3, 4, 5), (4, 0, 5, 6, 1, 2, 3), (6, 2, 3, 4, 0, 1, 5))
_Z_EARLY_TILES = 4


def _z_proj(h, w_in_t, chip, first, count, z_prev, name, tr=1024):
    t = h.shape[0]

    def body(chip_ref, h_ref, w_ref, z_prev_ref, z_ref):
        z_ref[...] = _dot_nt(h_ref[...], w_ref[...])

    def tile(j, chip_ref):
        picked = 0
        for c, order in enumerate(_Z_TILE_ORDER):
            for k in range(count):
                picked = picked + jnp.where((chip_ref[0] == c) & (j == k), order[first + k], 0)
        return picked

    return pl.pallas_call(
        body, name=name,
        grid_spec=pltpu.PrefetchScalarGridSpec(
            num_scalar_prefetch=1, grid=(count, t // tr),
            in_specs=[pl.BlockSpec((tr, D_MODEL), lambda j, i, o: (i, 0)),
                      pl.BlockSpec((Z_TILE, D_MODEL), lambda j, i, o: (tile(j, o), 0)),
                      pl.BlockSpec(memory_space=pl.ANY)],
            out_specs=pl.BlockSpec((tr, Z_TILE), lambda j, i, o: (i, tile(j, o)))),
        out_shape=jax.ShapeDtypeStruct((t, D_IN), F32),
        input_output_aliases={3: 0},
        compiler_params=_params(dimension_semantics=("arbitrary", "arbitrary")),
    )(chip, h, w_in_t, z_prev)


def _modulation(device, c_all, w_ada, b_ada):
    def body(device_ref, c_ref, w_ref, b_ref, act_ref, mod_ref):
        cv = c_ref[...]
        act = cv * _sigmoid(cv)
        act_ref[...] = act
        mod_ref[...] = jnp.dot(act.astype(BF16), w_ref[...].astype(BF16), preferred_element_type=F32) + b_ref[...]

    whole = lambda a: pl.BlockSpec(a.shape, lambda i, device_ref: (0,) * a.ndim)
    return pl.pallas_call(
        body, name="modulation",
        grid_spec=pltpu.PrefetchScalarGridSpec(
            num_scalar_prefetch=1, grid=(1,),
            in_specs=[whole(c_all), whole(w_ada), pl.BlockSpec((1, W_ADA_SHARD), lambda i, device_ref: (0, device_ref[0]))],
            out_specs=(whole(c_all), pl.BlockSpec((N_DEV, W_ADA_SHARD), lambda i, device_ref: (0, 0)))),
        out_shape=(jax.ShapeDtypeStruct(c_all.shape, F32), jax.ShapeDtypeStruct((N_DEV, W_ADA_SHARD), F32)),
        compiler_params=_params(dimension_semantics=("arbitrary",)),
    )(device, c_all, w_ada, b_ada)


MOD_SHIFT, MOD_SCALE, MOD_GATE = 0, 1, 2


def _mod_spec(part, d):
    return pl.BlockSpec((1, d), lambda i: (0, part))


def _norm_z_proj_own(x, norm_g, mod, w_in_t, chip, tm=512):
    t, d = x.shape

    def body(chip_ref, x_ref, g_ref, sc_ref, sh_ref, w_ref, h_ref, z_ref):
        xv = x_ref[...]
        r = lax.rsqrt(jnp.mean(xv * xv, axis=-1, keepdims=True) + EPS)
        h = ((xv * r) * g_ref[...] * (1.0 + sc_ref[...]) + sh_ref[...]).astype(BF16)
        h_ref[...] = h
        z_ref[...] = _dot_nt(h, w_ref[...])

    def own_tile(chip_ref):
        picked = 0
        for c, order in enumerate(_Z_TILE_ORDER):
            picked = picked + jnp.where(chip_ref[0] == c, order[0], 0)
        return picked

    def row(part):
        return pl.BlockSpec((1, d), lambda i, o: (0, part))

    return pl.pallas_call(
        body, name="norm_z_proj_own",
        grid_spec=pltpu.PrefetchScalarGridSpec(
            num_scalar_prefetch=1, grid=(t // tm,),
            in_specs=[pl.BlockSpec((tm, d), lambda i, o: (i, 0)), row(0), row(MOD_SCALE), row(MOD_SHIFT),
                      pl.BlockSpec((Z_TILE, d), lambda i, o: (own_tile(o), 0))],
            out_specs=(pl.BlockSpec((tm, d), lambda i, o: (i, 0)),
                       pl.BlockSpec((tm, Z_TILE), lambda i, o: (i, own_tile(o))))),
        out_shape=(jax.ShapeDtypeStruct((t, d), BF16), jax.ShapeDtypeStruct((t, D_IN), F32)),
        compiler_params=_params(dimension_semantics=("arbitrary",)),
    )(chip, x, norm_g, mod, mod, w_in_t)


def _window_bias(block_index):
    s = lax.broadcasted_iota(jnp.int32, (2 * BLOCK, BLOCK), 0)
    t = lax.broadcasted_iota(jnp.int32, (2 * BLOCK, BLOCK), 1)
    valid = ((s < BLOCK) & (s > t) & (block_index > 0)) | ((s >= BLOCK) & ((s - BLOCK) <= t))
    bias = jnp.where(valid, 0.0, -jnp.inf).astype(F32)
    return jnp.concatenate([bias] * 8, axis=1)


def _heads_t(pair_blocks, g):
    top = lax.broadcasted_iota(jnp.int32, (BLOCK, BLOCK), 0) < HEAD_DIM
    zeros = jnp.zeros((HEAD_DIM, BLOCK), F32)
    tiles = []
    for blk in pair_blocks:
        tp = blk.T
        if g == 0:
            tiles += [jnp.where(top, tp, 0.0), jnp.concatenate([tp[HEAD_DIM:], zeros], axis=0)]
        else:
            tiles += [jnp.concatenate([zeros, tp[:HEAD_DIM]], axis=0), jnp.where(top, 0.0, tp)]
    return jnp.concatenate(tiles, axis=1)


def _pair_block(xt, p, g):
    r0 = HEAD_DIM * g
    even = xt[r0:r0 + HEAD_DIM, (2 * p) * BLOCK:(2 * p + 1) * BLOCK]
    odd = xt[r0:r0 + HEAD_DIM, (2 * p + 1) * BLOCK:(2 * p + 2) * BLOCK]
    return jnp.concatenate([even, odd], axis=0).T


def _softmax_t(scores_t, bias, sink):
    st = scores_t + bias
    m = jnp.maximum(jnp.max(st, axis=0, keepdims=True), sink)
    e = jnp.exp(st - m)
    es = jnp.exp(sink - m)
    inv = 1.0 / (jnp.sum(e, axis=0, keepdims=True) + es)
    return e * inv, es * inv


def _dot(a, b):
    return jnp.dot(a, b, preferred_element_type=F32)


def _dot_nt(a, b):
    return lax.dot_general(a, b, (((1,), (1,)), ((), ())), preferred_element_type=F32)


def _dot_tn(a, b):
    return lax.dot_general(a, b, (((0,), (0,)), ((), ())), preferred_element_type=F32)


def _layer_norm_fwd(v):
    mu = jnp.mean(v, axis=-1, keepdims=True)
    xc = v - mu
    rstd = lax.rsqrt(jnp.mean(xc * xc, axis=-1, keepdims=True) + EPS)
    return xc * rstd, rstd


def _tril(transposed=False):
    t = lax.broadcasted_iota(jnp.int32, (BLOCK, BLOCK), 0)
    s = lax.broadcasted_iota(jnp.int32, (BLOCK, BLOCK), 1)
    return s >= t if transposed else t >= s


def _const_spec(shape):
    return pl.BlockSpec(shape, lambda i: (0,) * len(shape))


def _keys_values(z_ref, kvp):
    kvc = z_ref[:, SEG_KV:SEG_KV + 2 * D_KV]
    kk = jnp.concatenate([kvp[:, :D_KV], kvc[:, :D_KV]], axis=0)
    vv = jnp.concatenate([kvp[:, D_KV:], kvc[:, D_KV:]], axis=0)
    return kk, vv


MIXER_BLOCKS = 2


class _Rows:
    def __init__(self, ref, sub):
        self.ref, self.rows = ref, slice(sub * BLOCK, (sub + 1) * BLOCK)

    def __getitem__(self, idx):
        return self.ref[self.rows, idx[1]]

    def __setitem__(self, idx, value):
        self.ref[self.rows, idx[1]] = value


def _kv_before_spec(index):
    return pl.BlockSpec((BLOCK, 2 * D_KV),
                        lambda i: (jnp.maximum(MIXER_BLOCKS * index(i) - 1, 0), SEG_KV // (2 * D_KV)))


def _pair_cols(g, p, base=0):
    return slice(base + (4 * g + p) * 128, base + (4 * g + p + 1) * 128)


def _mixer_fwd(z, sink_rows, ln_g, ln_b, sgu_w, sgu_bt):
    t = z.shape[0]

    def body(z_all, kvp_ref, sink_ref, lng_ref, lnb_ref, w_ref, bt_ref, a_all, prob_ref, sink_prob_ref):
        kv_before = kvp_ref[...]
        for sub in range(MIXER_BLOCKS):
            z_ref, a_ref = _Rows(z_all, sub), _Rows(a_all, sub)
            one_block(z_ref, kv_before, MIXER_BLOCKS * pl.program_id(0) + sub, sink_ref, lng_ref, lnb_ref, w_ref,
                      bt_ref, a_ref, prob_ref.at[sub], sink_prob_ref.at[sub])
            kv_before = z_ref[:, SEG_KV:SEG_KV + 2 * D_KV]

    def one_block(z_ref, kv_before, block_index, sink_ref, lng_ref, lnb_ref, w_ref, bt_ref, a_ref, prob_ref,
                  sink_prob_ref):
        bias = _window_bias(block_index)
        kk, vv = _keys_values(z_ref, kv_before)
        kk_b, vvt_b = kk.astype(BF16), vv.T.astype(BF16)
        for g in range(2):
            qt = _heads_t([z_ref[:, _pair_cols(g, p, SEG_Q)] * ATTN_SCALE for p in range(4)], g).astype(BF16)
            prob, sink_prob = _softmax_t(_dot(kk_b, qt), bias, sink_ref[g])
            prob_b = prob.astype(BF16)
            prob_ref[g] = prob_b
            sink_prob_ref[g] = sink_prob
            ot = _dot(vvt_b, prob_b)
            for p in range(4):
                gate = z_ref[:, _pair_cols(g, p, SEG_GA)]
                a_ref[:, _pair_cols(g, p)] = (_pair_block(ot, p, g) * (gate * _sigmoid(gate))).astype(BF16)

        vhat, _ = _layer_norm_fwd(z_ref[:, SEG_VS:SEG_VS + D_SGU])
        vn = vhat * lng_ref[...] + lnb_ref[...]
        tril = _tril()
        for g in range(SGU_GROUPS):
            cols = slice(g * 128, (g + 1) * 128)
            wm = jnp.where(tril, w_ref[g], 0.0).astype(BF16)
            mixed = _dot(wm, vn[:, cols].astype(BF16)) + bt_ref[:, g:g + 1]
            gate = z_ref[:, SEG_GS + g * 128:SEG_GS + (g + 1) * 128]
            a_ref[:, D_ATTN + g * 128:D_ATTN + (g + 1) * 128] = (
                (z_ref[:, SEG_U + g * 128:SEG_U + (g + 1) * 128] * mixed) * (gate * _sigmoid(gate))).astype(BF16)

    rows = MIXER_BLOCKS * BLOCK
    return pl.pallas_call(
        body, name="mixer_fwd", grid=(t // rows,),
        in_specs=[pl.BlockSpec((rows, D_IN), lambda i: (i, 0)), _kv_before_spec(lambda i: i),
                  _const_spec((2, 1, 8 * BLOCK)), _const_spec((1, D_SGU)), _const_spec((1, D_SGU)),
                  _const_spec((SGU_GROUPS, BLOCK, BLOCK)), _const_spec((BLOCK, SGU_GROUPS))],
        out_specs=(pl.BlockSpec((rows, D_MODEL), lambda i: (i, 0)),
                   pl.BlockSpec((MIXER_BLOCKS, 2, 2 * BLOCK, 8 * BLOCK), lambda i: (i, 0, 0, 0)),
                   pl.BlockSpec((MIXER_BLOCKS, 2, 1, 8 * BLOCK), lambda i: (i, 0, 0, 0))),
        out_shape=(jax.ShapeDtypeStruct((t, D_MODEL), BF16),
                   jax.ShapeDtypeStruct((t // BLOCK, 2, 2 * BLOCK, 8 * BLOCK), BF16),
                   jax.ShapeDtypeStruct((t // BLOCK, 2, 1, 8 * BLOCK), F32)),
        compiler_params=_params(dimension_semantics=("arbitrary",)),
    )(z, z, sink_rows, ln_g, ln_b, sgu_w, sgu_bt)


def _mixer_bwd(z, da, probs, sink_probs, ln_g, ln_b, sgu_w, sgu_wt, sgu_bt, a, dy):
    t = z.shape[0]

    def body(z_all, kvp_ref, da_all, prob_ref, sink_prob_ref, lng_ref, lnb_ref, w_ref, wt_ref, bt_ref, a_ref, dy_ref,
             dz_all, dsink_ref, dw_ref, db_ref, dlng_ref, dlnb_ref, dw_out_ref, carry_ref, dsink_acc, dbt_acc):
        step = pl.program_id(0)

        @pl.when(step == 0)
        def _():
            carry_ref[...] = jnp.zeros_like(carry_ref)
            dsink_acc[...] = jnp.zeros_like(dsink_acc)
            dbt_acc[...] = jnp.zeros_like(dbt_acc)
            dw_ref[...] = jnp.zeros_like(dw_ref)
            dlng_ref[...] = jnp.zeros_like(dlng_ref)
            dlnb_ref[...] = jnp.zeros_like(dlnb_ref)

        carry = carry_ref[...]
        dw_out_ref[...] = _dot_tn(a_ref[...], dy_ref[...]).astype(BF16)
        for sub in reversed(range(MIXER_BLOCKS)):
            kv_before = kvp_ref[...] if sub == 0 else _Rows(z_all, sub - 1)[:, SEG_KV:SEG_KV + 2 * D_KV]
            carry = one_block(_Rows(z_all, sub), kv_before, _Rows(da_all, sub), prob_ref.at[sub], sink_prob_ref.at[sub],
                              carry, lng_ref, lnb_ref, w_ref, wt_ref, bt_ref, _Rows(dz_all, sub),
                              dw_ref, dlng_ref, dlnb_ref, dsink_acc, dbt_acc)
        carry_ref[...] = carry

        @pl.when(step == ns - 1)
        def _():
            db_ref[...] = dbt_acc[...].T[:SGU_GROUPS]
            lane_row = lax.broadcasted_iota(jnp.int32, (1, 128), 1)
            d_sink = jnp.zeros((1, 128), F32)
            for g in range(2):
                acc = dsink_acc[g]
                for j in range(8):
                    head_sum = jnp.sum(acc[:, j * BLOCK:(j + 1) * BLOCK], axis=-1, keepdims=True)
                    d_sink = d_sink + jnp.where(lane_row == 8 * g + j, head_sum, 0.0)
            dsink_ref[...] = d_sink

    def one_block(z_ref, kv_before, da_ref, prob_ref, sink_prob_ref, carry, lng_ref, lnb_ref, w_ref, wt_ref, bt_ref,
                  dz_ref, dw_ref, dlng_ref, dlnb_ref, dsink_acc, dbt_acc):
        kk, vv = _keys_values(z_ref, kv_before)
        vv_b = vv.astype(BF16)
        kkt_b, vvt_b = kk.T.astype(BF16), vv.T.astype(BF16)
        dkk = jnp.zeros((2 * BLOCK, D_KV), F32)
        dvv = jnp.zeros((2 * BLOCK, D_KV), F32)
        for g in range(2):
            qt = _heads_t([z_ref[:, _pair_cols(g, p, SEG_Q)] * ATTN_SCALE for p in range(4)], g).astype(BF16)
            prob_b, sink_prob = prob_ref[g], sink_prob_ref[g]
            prob = prob_b.astype(F32)
            ot = _dot(vvt_b, prob_b)
            gates = [z_ref[:, _pair_cols(g, p, SEG_GA)] for p in range(4)]
            sig = [_sigmoid(gt) for gt in gates]
            d_attn = [da_ref[:, _pair_cols(g, p)] for p in range(4)]
            d_ot = _heads_t([d_attn[p] * (gates[p] * sig[p]) for p in range(4)], g).astype(BF16)
            d_prob = _dot(vv_b, d_ot)
            delta = jnp.sum(prob * d_prob, axis=0, keepdims=True)
            d_scores = (prob * (d_prob - delta)).astype(BF16)
            dsink_acc[g] -= sink_prob * delta
            d_qt = _dot(kkt_b, d_scores)
            dkk = dkk + _dot_nt(d_scores, qt)
            dvv = dvv + _dot_nt(prob_b, d_ot)
            for p in range(4):
                dz_ref[:, _pair_cols(g, p, SEG_Q)] = (_pair_block(d_qt, p, g) * ATTN_SCALE).astype(BF16)
                d_silu = sig[p] * (1.0 + gates[p] * (1.0 - sig[p]))
                dz_ref[:, _pair_cols(g, p, SEG_GA)] = (d_attn[p] * _pair_block(ot, p, g) * d_silu).astype(BF16)
        d_kv = jnp.concatenate([dkk, dvv], axis=1)
        dz_ref[:, SEG_KV:SEG_KV + 2 * D_KV] = (d_kv[BLOCK:] + carry).astype(BF16)

        vhat, rstd = _layer_norm_fwd(z_ref[:, SEG_VS:SEG_VS + D_SGU])
        lng = lng_ref[...]
        vn = vhat * lng + lnb_ref[...]
        tril, triu = _tril(), _tril(transposed=True)
        lane = lax.broadcasted_iota(jnp.int32, (BLOCK, 128), 1)
        d_bt = jnp.zeros((BLOCK, 128), F32)
        d_vn = []
        for g in range(SGU_GROUPS):
            cols = slice(g * 128, (g + 1) * 128)
            wm = jnp.where(tril, w_ref[g], 0.0).astype(BF16)
            wmt = jnp.where(triu, wt_ref[g], 0.0).astype(BF16)
            vn_g = vn[:, cols].astype(BF16)
            mixed = _dot(wm, vn_g) + bt_ref[:, g:g + 1]
            gate = z_ref[:, SEG_GS + g * 128:SEG_GS + (g + 1) * 128]
            u = z_ref[:, SEG_U + g * 128:SEG_U + (g + 1) * 128]
            d_out = da_ref[:, D_ATTN + g * 128:D_ATTN + (g + 1) * 128]
            sg = _sigmoid(gate)
            d_um = d_out * (gate * sg)
            dz_ref[:, SEG_U + g * 128:SEG_U + (g + 1) * 128] = (d_um * mixed).astype(BF16)
            dz_ref[:, SEG_GS + g * 128:SEG_GS + (g + 1) * 128] = (
                d_out * (u * mixed) * (sg * (1.0 + gate * (1.0 - sg)))).astype(BF16)
            d_mixed = d_um * u
            d_mixed_b = d_mixed.astype(BF16)
            dw_ref[g] += jnp.where(tril, _dot_nt(d_mixed_b, vn_g), 0.0)
            d_bt = d_bt + jnp.where(lane == g, jnp.sum(d_mixed, axis=-1, keepdims=True), 0.0)
            d_vn.append(_dot(wmt, d_mixed_b))
        dbt_acc[...] += d_bt
        d_vn = jnp.concatenate(d_vn, axis=1)
        dlng_ref[...] += jnp.sum(d_vn * vhat, axis=0, keepdims=True)
        dlnb_ref[...] += jnp.sum(d_vn, axis=0, keepdims=True)
        d_vhat = d_vn * lng
        d_v = rstd * (d_vhat - jnp.mean(d_vhat, axis=-1, keepdims=True)
                      - vhat * jnp.mean(d_vhat * vhat, axis=-1, keepdims=True))
        dz_ref[:, SEG_VS:SEG_VS + D_SGU] = d_v.astype(BF16)
        return d_kv[:BLOCK]

    rows = MIXER_BLOCKS * BLOCK
    ns = t // rows
    rev = lambda i: ns - 1 - i
    tile = a.shape[1] // ns
    assert tile % 128 == 0
    return pl.pallas_call(
        body, name="mixer_bwd", grid=(ns,),
        in_specs=[pl.BlockSpec((rows, D_IN), lambda i: (rev(i), 0)), _kv_before_spec(rev),
                  pl.BlockSpec((rows, D_MODEL), lambda i: (rev(i), 0)),
                  pl.BlockSpec((MIXER_BLOCKS, 2, 2 * BLOCK, 8 * BLOCK), lambda i: (rev(i), 0, 0, 0)),
                  pl.BlockSpec((MIXER_BLOCKS, 2, 1, 8 * BLOCK), lambda i: (rev(i), 0, 0, 0)),
                  _const_spec((1, D_SGU)), _const_spec((1, D_SGU)),
                  _const_spec((SGU_GROUPS, BLOCK, BLOCK)), _const_spec((SGU_GROUPS, BLOCK, BLOCK)),
                  _const_spec((BLOCK, SGU_GROUPS)), pl.BlockSpec((t, tile), lambda i: (0, i)),
                  pl.BlockSpec(dy.shape, lambda i: (0, 0), pipeline_mode=pl.Buffered(1))],
        out_specs=(pl.BlockSpec((rows, D_IN), lambda i: (rev(i), 0)), _const_spec((1, 128)),
                   _const_spec((SGU_GROUPS, BLOCK, BLOCK)), _const_spec((SGU_GROUPS, BLOCK)),
                   _const_spec((1, D_SGU)), _const_spec((1, D_SGU)),
                   pl.BlockSpec((tile, dy.shape[1]), lambda i: (i, 0))),
        out_shape=(jax.ShapeDtypeStruct((t, D_IN), BF16), jax.ShapeDtypeStruct((1, 128), F32),
                   jax.ShapeDtypeStruct((SGU_GROUPS, BLOCK, BLOCK), F32), jax.ShapeDtypeStruct((SGU_GROUPS, BLOCK), F32),
                   jax.ShapeDtypeStruct((1, D_SGU), F32), jax.ShapeDtypeStruct((1, D_SGU), F32),
                   jax.ShapeDtypeStruct((a.shape[1], dy.shape[1]), BF16)),
        scratch_shapes=[pltpu.VMEM((BLOCK, 2 * D_KV), F32), pltpu.VMEM((2, 1, 8 * BLOCK), F32),
                        pltpu.VMEM((BLOCK, 128), F32)],
        compiler_params=_params(dimension_semantics=("arbitrary",)),
    )(z, z, da, probs, sink_probs, ln_g, ln_b, sgu_w, sgu_wt, sgu_bt, a, dy)


def _out_proj_head(a, w_out_full, x, target, mod, final_g, tm=256):
    t, d = x.shape

    def body(a_ref, w_ref, x_ref, tg_ref, gate_ref, fg_ref, dx2_ref, dy_ref, da_ref, loss_ref, dfg_ref, dgate_ref):
        @pl.when(pl.program_id(0) == 0)
        def _():
            loss_ref[...] = jnp.zeros_like(loss_ref)
            dfg_ref[...] = jnp.zeros_like(dfg_ref)
            dgate_ref[...] = jnp.zeros_like(dgate_ref)

        yv, gate, fg = _dot(a_ref[...], w_ref[...]), gate_ref[...], fg_ref[...]
        x2 = x_ref[...] + gate * yv
        r2 = lax.rsqrt(jnp.mean(x2 * x2, axis=-1, keepdims=True) + EPS)
        nrm = x2 * r2
        err = nrm * fg - tg_ref[...]
        loss_ref[...] += 0.5 * jnp.sum(jnp.mean(err * err, axis=-1, keepdims=True), axis=0, keepdims=True)
        fg_d = fg * (1.0 / d)
        err_nrm = err * nrm
        dfg_ref[...] += jnp.sum(err_nrm, axis=0, keepdims=True) * (1.0 / d)
        d_nrm = err * fg_d
        dx2 = r2 * (d_nrm - nrm * jnp.mean(err_nrm * fg_d, axis=-1, keepdims=True))
        dx2_ref[...] = dx2
        dgate_ref[...] += jnp.sum(dx2 * yv, axis=0, keepdims=True)
        dy = (dx2 * gate).astype(BF16)
        dy_ref[...] = dy
        da_ref[...] = _dot_nt(dy, w_ref[...])

    blk = pl.BlockSpec((tm, d), lambda i: (i, 0))
    a_blk = pl.BlockSpec((tm, a.shape[1]), lambda i: (i, 0))
    row = _const_spec((1, d))
    whole = pl.BlockSpec(w_out_full.shape, lambda i: (0, 0), pipeline_mode=pl.Buffered(1))
    return pl.pallas_call(
        body, name="out_proj_head", grid=(t // tm,),
        in_specs=[a_blk, whole, blk, blk, _mod_spec(MOD_GATE, d), row],
        out_specs=(blk, blk, a_blk, _const_spec((1, 128)), row, row),
        out_shape=(jax.ShapeDtypeStruct((t, d), F32), jax.ShapeDtypeStruct((t, d), BF16),
                   jax.ShapeDtypeStruct(a.shape, F32), jax.ShapeDtypeStruct((1, 128), F32),
                   jax.ShapeDtypeStruct((1, d), F32), jax.ShapeDtypeStruct((1, d), F32)),
        compiler_params=_params(dimension_semantics=("arbitrary",)),
    )(a, w_out_full, x, target, mod, final_g)


def _z_proj_bwd_norm(dz, w_in_t, x, dx2, norm_g, mod, dep, tm=256):
    t, d = x.shape

    def body(dz_ref, w_ref, x_ref, dx2_ref, g_ref, sc_ref, dep_ref, gx_ref, dshift_ref, dscale_ref, dg_ref):
        @pl.when(pl.program_id(0) == 0)
        def _():
            dshift_ref[...] = jnp.zeros_like(dshift_ref)
            dscale_ref[...] = jnp.zeros_like(dscale_ref)
            dg_ref[...] = jnp.zeros_like(dg_ref)

        dh, xv, g = _dot(dz_ref[...], w_ref[...]), x_ref[...], g_ref[...]
        one_plus = 1.0 + sc_ref[...]
        r = lax.rsqrt(jnp.mean(xv * xv, axis=-1, keepdims=True) + EPS)
        xn = xv * r
        gain = one_plus * g
        dh_xn = dh * xn
        dh_xn_sum = jnp.sum(dh_xn, axis=0, keepdims=True)
        dshift_ref[...] += jnp.sum(dh, axis=0, keepdims=True)
        dscale_ref[...] += dh_xn_sum * g
        dg_ref[...] += dh_xn_sum * one_plus
        d_xn = dh * gain
        gx_ref[...] = dx2_ref[...] + r * (d_xn - xn * jnp.mean(dh_xn * gain, axis=-1, keepdims=True))

    blk = pl.BlockSpec((tm, d), lambda i: (i, 0))
    row = _const_spec((1, d))
    whole = pl.BlockSpec(w_in_t.shape, lambda i: (0, 0), pipeline_mode=pl.Buffered(1))
    return pl.pallas_call(
        body, name="z_proj_bwd_norm", grid=(t // tm,),
        in_specs=[pl.BlockSpec((tm, dz.shape[1]), lambda i: (i, 0)), whole, blk, blk, row, _mod_spec(MOD_SCALE, d),
                  _const_spec((8, 128))],
        out_specs=(blk, row, row, row),
        out_shape=(jax.ShapeDtypeStruct((t, d), F32),) + (jax.ShapeDtypeStruct((1, d), F32),) * 3,
        compiler_params=_params(dimension_semantics=("arbitrary",)),
    )(dz, w_in_t, x, dx2, norm_g, mod, dep)


def _adamw(w, g, m, v):
    m = ADAM_B1 * m + (1.0 - ADAM_B1) * g
    v = ADAM_B2 * v + (1.0 - ADAM_B2) * (g * g)
    m_hat = m / (1.0 - ADAM_B1 ** ADAM_STEP)
    v_hat = v / (1.0 - ADAM_B2 ** ADAM_STEP)
    delta = -ADAM_LR * (m_hat / (jnp.sqrt(v_hat) + ADAM_EPS) + ADAM_WD * w)
    return delta, m, v


def _relay_sum(device, blocks, land_pair, land_first, tr):
    _, r, c = blocks.shape

    def body(device_ref, a_ref, b_ref, c_ref, o_ref):
        o_ref[...] = (a_ref[...].astype(F32) + b_ref[...].astype(F32) + c_ref[...].astype(F32)).astype(BF16)

    second = pl.BlockSpec((None, tr, c), lambda i, device_ref: (1, i, 0))
    return pl.pallas_call(
        body, name="w_in_grad_relay_sum",
        grid_spec=pltpu.PrefetchScalarGridSpec(
            num_scalar_prefetch=1, grid=(r // tr,),
            in_specs=[pl.BlockSpec((None, tr, c), lambda i, device_ref: (device_ref[0], i, 0)), second, second],
            out_specs=pl.BlockSpec((tr, c), lambda i, device_ref: (i, 0))),
        out_shape=jax.ShapeDtypeStruct((r, c), BF16),
        compiler_params=_params(dimension_semantics=("arbitrary",)),
    )(device, blocks, land_pair, land_first)


def _adam_from_chips(chip, pair, landed, w, m, v, name, tr):
    _, r, c = pair.shape
    n = len(landed)
    assert r % tr == 0

    def body(chip_ref, own_ref, *refs):
        w_ref, m_ref, v_ref, g_ref, d_ref, nm_ref, nv_ref = refs[n:]
        g = own_ref[...].astype(F32)
        for k in range(n):
            g = g + refs[k][...].astype(F32)
        g_ref[...] = g
        d_ref[...], nm_ref[...], nv_ref[...] = _adamw(w_ref[...], g, m_ref[...], v_ref[...])

    def landed_spec(index):
        return pl.BlockSpec((None, tr, c), lambda i, chip_ref: (index, i, 0))

    blk = pl.BlockSpec((tr, c), lambda i, chip_ref: (i, 0))
    return pl.pallas_call(
        body, name=name,
        grid_spec=pltpu.PrefetchScalarGridSpec(
            num_scalar_prefetch=1, grid=(r // tr,),
            in_specs=[pl.BlockSpec((None, tr, c), lambda i, chip_ref: (chip_ref[0], i, 0))]
            + [landed_spec(index) for _, index in landed] + [blk, blk, blk],
            out_specs=(blk,) * 4),
        out_shape=(jax.ShapeDtypeStruct((r, c), F32),) * 4,
        compiler_params=_params(dimension_semantics=("arbitrary",)),
    )(chip, pair, *[array for array, _ in landed], w, m, v)


def _adam_w_ada(device, act_t, dmod_all, w, m, v, tr=512):
    r, c = w.shape

    def body(device_ref, a_ref, dm_ref, w_ref, m_ref, v_ref, g_ref, d_ref, nm_ref, nv_ref):
        g = _dot(a_ref[...].astype(BF16), dm_ref[...].astype(BF16))
        g_ref[...] = g
        d_ref[...], nm_ref[...], nv_ref[...] = _adamw(w_ref[...], g, m_ref[...], v_ref[...])

    blk = pl.BlockSpec((tr, c), lambda i, device_ref: (i, 0))
    return pl.pallas_call(
        body, name="adam_w_ada",
        grid_spec=pltpu.PrefetchScalarGridSpec(
            num_scalar_prefetch=1, grid=(r // tr,),
            in_specs=[pl.BlockSpec((tr, N_DEV), lambda i, device_ref: (i, 0)),
                      pl.BlockSpec((N_DEV, c), lambda i, device_ref: (0, device_ref[0])), blk, blk, blk],
            out_specs=(blk,) * 4),
        out_shape=(jax.ShapeDtypeStruct((r, c), F32),) * 4,
        compiler_params=_params(dimension_semantics=("arbitrary",)),
    )(device, act_t, dmod_all, w, m, v)


def _pack_small(d_shift, d_scale, d_gate, d_norm_g, d_final_g, d_ln_g, d_ln_b, loss, d_sinks, d_sgu_b):
    def body(shift_ref, scale_ref, gate_ref, ng_ref, fg_ref, lng_ref, lnb_ref, loss_ref, sink_ref, b_ref, o_ref):
        o_ref[...] = jnp.zeros_like(o_ref)
        o_ref[ROW_SHIFT:ROW_SHIFT + 1, :] = shift_ref[...]
        o_ref[ROW_SCALE:ROW_SCALE + 1, :] = scale_ref[...]
        o_ref[ROW_GATE:ROW_GATE + 1, :] = gate_ref[...]
        o_ref[ROW_NORM_G:ROW_NORM_G + 1, :] = ng_ref[...]
        o_ref[ROW_FINAL_G:ROW_FINAL_G + 1, :] = fg_ref[...]
        o_ref[ROW_LN:ROW_LN + 1, 0:D_SGU] = lng_ref[...]
        o_ref[ROW_LN:ROW_LN + 1, D_SGU:2 * D_SGU] = lnb_ref[...]
        o_ref[ROW_MISC:ROW_MISC + 1, 0:128] = loss_ref[...]
        o_ref[ROW_MISC:ROW_MISC + 1, 128:256] = sink_ref[...]
        o_ref[ROW_SGU_B:ROW_SGU_B + SGU_GROUPS, 0:BLOCK] = b_ref[...]

    return pl.pallas_call(
        body, name="pack_small", out_shape=jax.ShapeDtypeStruct((SMALL_ROWS, D_MODEL), F32),
        compiler_params=_params(),
    )(d_shift, d_scale, d_gate, d_norm_g, d_final_g, d_ln_g, d_ln_b, loss, d_sinks, d_sgu_b)


_SMALL_NAMES = ("norm_g", "b_ada", "attn_sinks", "sgu_ln_g", "sgu_ln_b", "sgu_w", "sgu_b", "final_g")


def _adam_small(partials, d_sgu_w_all, weights, moments_m, moments_v):
    names = _SMALL_NAMES
    k = len(names)

    def body(*refs):
        p_ref, sw_ref = refs[0], refs[1]
        w_refs, m_refs, v_refs = refs[2:2 + k], refs[2 + k:2 + 2 * k], refs[2 + 2 * k:2 + 3 * k]
        loss_ref, dmod_ref = refs[2 + 3 * k], refs[3 + 3 * k]
        out_refs = refs[4 + 3 * k:4 + 7 * k]
        sum_ref = refs[4 + 7 * k]
        total = p_ref[0]
        for j in range(1, N_DEV):
            total = total + p_ref[j]
        sum_ref[...] = total
        for j in range(N_DEV):
            for part, row in enumerate((ROW_SHIFT, ROW_SCALE, ROW_GATE)):
                dmod_ref[j:j + 1, part * D_MODEL:(part + 1) * D_MODEL] = p_ref[j, row:row + 1, :]
        loss_ref[...] = sum_ref[ROW_MISC:ROW_MISC + 1, 0:1]
        d_sgu_w = sw_ref[0]
        for j in range(1, N_DEV):
            d_sgu_w = d_sgu_w + sw_ref[j]
        grads = {
            "norm_g": sum_ref[ROW_NORM_G:ROW_NORM_G + 1, :],
            "b_ada": jnp.concatenate([sum_ref[r:r + 1, :] for r in (ROW_SHIFT, ROW_SCALE, ROW_GATE)], axis=1),
            "attn_sinks": sum_ref[ROW_MISC:ROW_MISC + 1, 128:128 + N_Q_HEADS],
            "sgu_ln_g": sum_ref[ROW_LN:ROW_LN + 1, 0:D_SGU],
            "sgu_ln_b": sum_ref[ROW_LN:ROW_LN + 1, D_SGU:2 * D_SGU],
            "sgu_w": d_sgu_w[None],
            "sgu_b": sum_ref[ROW_SGU_B:ROW_SGU_B + SGU_GROUPS, 0:BLOCK][None],
            "final_g": sum_ref[ROW_FINAL_G:ROW_FINAL_G + 1, :],
        }
        for i, name in enumerate(names):
            g = grads[name]
            delta, m, v = _adamw(w_refs[i][...], g, m_refs[i][...], v_refs[i][...])
            out_refs[4 * i][...] = g
            out_refs[4 * i + 1][...] = delta
            out_refs[4 * i + 2][...] = m
            out_refs[4 * i + 3][...] = v

    shapes = [jax.ShapeDtypeStruct((1, 1), F32), jax.ShapeDtypeStruct((N_DEV, 3 * D_MODEL), F32)]
    for name in names:
        shapes += [jax.ShapeDtypeStruct(weights[name].shape, F32)] * 4
    outs = pl.pallas_call(
        body, name="adam_small", out_shape=tuple(shapes),
        scratch_shapes=[pltpu.VMEM((SMALL_ROWS, D_MODEL), F32)],
        compiler_params=_params(),
    )(partials, d_sgu_w_all, *[weights[n] for n in names], *[moments_m[n] for n in names],
      *[moments_v[n] for n in names])
    return outs[0], outs[1], {name: outs[2 + 4 * i:6 + 4 * i] for i, name in enumerate(names)}


def kernel(x, c, norm_g, w_ada, b_ada, w_in, attn_sinks, sgu_ln_g, sgu_ln_b, sgu_w, sgu_b, w_out, final_g, loss_target, m_norm_g, m_w_ada, m_b_ada, m_w_in, m_attn_sinks, m_sgu_ln_g, m_sgu_ln_b, m_sgu_w, m_sgu_b, m_w_out, m_final_g, v_norm_g, v_w_ada, v_b_ada, v_w_in, v_attn_sinks, v_sgu_ln_g, v_sgu_ln_b, v_sgu_w, v_sgu_b, v_w_out, v_final_g):
    xi, yi, ci = _place()
    me = 4 * xi + 2 * yi + ci
    x2d, target = x[0], loss_target[0]

    core = ci.astype(jnp.int32).reshape(1)
    chip = (2 * xi + yi).astype(jnp.int32).reshape(1)

    first = _own_block_copies(_first_targets)
    first_flight = _start_copies([_with_own_slot(w_in[0].T.astype(BF16), me)], first, 2, core, "gather_w_in_start")

    c_all = _all_gather_small(c.reshape(8, 256), "gather_c", first_flight[3]).reshape(N_DEV, D_MODEL)
    device = me.astype(jnp.int32).reshape(1)
    c_act, mod_part = _modulation(device, c_all, w_ada[0], b_ada)
    mod_all = _all_gather_small(mod_part, "gather_mod")

    across = _wait_then_start(first_flight, lambda *a: first(*a)[1:], _second_axis_stage_copies, 3, mod_all,
                              "gather_w_in_second_axis_stage")
    mod = lax.dynamic_index_in_dim(mod_all, me, axis=1, keepdims=False).reshape(1, 3 * D_MODEL)
    mod = mod + across[3][0, 0]

    w_in_pair = _wait_copies((first_flight[0], first_flight[1], across[2], None), lambda *a: first(*a)[:1], mod,
                             "gather_w_in_sibling_wait")
    h, z_own = _norm_z_proj_own(x2d, norm_g, mod, w_in_pair[0].reshape(D_IN, D_MODEL), chip)
    w_out_early = _own_block_copies(lambda x, y, c: [(x, y, 1 - c), (*_second_axis_chip(x, y, c), c)])
    w_out_late = _own_block_copies(lambda x, y, c: [(*_first_axis_chip(x, y, c), c), (1 - x, 1 - y, c)])
    forward = _wait_then_start(
        (across[0], across[1], w_in_pair, None), lambda *a: _second_axis_stage_copies(*a)[:1],
        lambda refs, s, r: _second_axis_forward_copies(refs[:1], s, r) + _group(w_out_early, 1, 1, 1)(refs, s, r),
        3, z_own, "gather_w_in_second_axis_forward", more_bufs=[_with_own_slot(w_out[0].astype(BF16), me)])
    w_in_most = _wait_copies((across[0], across[1], forward[2][:1], None),
                             lambda *a: _second_axis_stage_copies(*a)[1:2], z_own, "gather_w_in_first_forward_wait")
    w_in_most = _wait_copies((forward[0], forward[1], w_in_most, None), _second_axis_forward_copies, z_own,
                             "gather_w_in_second_forward_wait")
    z_early = _z_proj(h, w_in_most[0].reshape(D_IN, D_MODEL), chip, 1, _Z_EARLY_TILES - 1, z_own, "z_proj_early")
    last = _wait_then_start(
        (across[0], across[1], [w_in_most[0], forward[2][1]], None), lambda *a: _second_axis_stage_copies(*a)[2:],
        lambda refs, s, r: _diagonal_forward_copies(refs[:1], s, r) + _group(w_out_late, 1, 1, 1)(refs, s, r),
        3, z_early, "gather_w_in_last_stage")
    w_in_all = _wait_copies((last[0], last[1], last[2][:1], None), _diagonal_forward_copies, z_early,
                            "gather_w_in_last_wait")[0]
    w_in_t = w_in_all.reshape(D_IN, D_MODEL)
    z = _z_proj(h, w_in_t, chip, _Z_EARLY_TILES, 7 - _Z_EARLY_TILES, z_early, "z_proj_late")
    w_out_half = _wait_copies((forward[0], forward[1], last[2][1:], None), _group(w_out_early, 0, 1, 1), z,
                              "gather_w_out_early_wait")
    w_out_flight = _wait_then_start((last[0], last[1], w_out_half, None), _group(w_out_late, 0, 1, 1),
                                    _forward_copies, 3, z, "gather_w_out_forward_stage")
    sink_rows = jnp.repeat(attn_sinks.reshape(N_Q_HEADS), BLOCK).reshape(2, 1, 8 * BLOCK)
    sgu_bt = sgu_b[0].T
    a, probs, sink_probs = _mixer_fwd(z, sink_rows + w_out_flight[3][0, 0], sgu_ln_g, sgu_ln_b, sgu_w[0], sgu_bt)
    w_out_all = _wait_copies(w_out_flight, _forward_copies, a, "gather_w_out_forward_wait")[0]
    w_out_full = w_out_all.reshape(D_MODEL, D_MODEL)
    final_g_row = final_g.reshape(1, D_MODEL)
    dx2, dy, da, loss_part, d_final_g, d_gate = _out_proj_head(a, w_out_full, x2d, target, mod, final_g_row)

    dz, d_sinks, d_sgu_w, d_sgu_b, d_ln_g, d_ln_b, dw_out = _mixer_bwd(
        z, da, probs, sink_probs, sgu_ln_g, sgu_ln_b, sgu_w[0], jnp.swapaxes(sgu_w[0], 1, 2), sgu_bt, a, dy)
    pair_out = _pair_reduce(dw_out.reshape(4, 2, W_OUT_SHARD, D_MODEL), _every_chip, "w_out_grad_pair_reduce",
                            W_OUT_SHARD // 2)
    sgu_w_to_all = _group(_own_block_copies(_all_others), 2, 1, 3)
    both = _start_copies(
        [pair_out, lax.empty((3, W_OUT_SHARD, D_MODEL), BF16), _with_own_slot(d_sgu_w, me)],
        lambda refs, s, r: _chip_copies(refs[:2], s, r) + sgu_w_to_all(refs, s, r), 3 + N_DEV - 1, core,
        "w_out_grad_chip_and_sgu_w_gather_start")
    out_flight, sgu_w_flight = (both[0], both[1], both[2][:2], None), (both[0], both[1], both[2][2:], None)
    dw_in_t = _w_in_grad(dz, h, both[3], Z_TILE)
    pair_in = _pair_reduce(dw_in_t.reshape(4, 2, W_IN_SHARD, D_MODEL), _first_hop_chips, "w_in_grad_pair_reduce",
                           W_IN_SHARD // 3)
    first_hop = lambda refs, s, r: _first_hop_copies(refs[:2], s, r) + _group(_late_pair_copies, 2, 2, 2)(refs, s, r)
    hop1 = _start_copies(
        [pair_in, lax.empty((2, W_IN_SHARD, D_MODEL), BF16), dw_in_t.reshape(N_DEV, W_IN_SHARD, D_MODEL),
         lax.empty((2, W_IN_SHARD, D_MODEL), BF16)], first_hop, 4, core, "w_in_grad_first_hop_start")
    grad_x, d_shift, d_scale, d_norm_g = _z_proj_bwd_norm(dz, w_in_t, x2d, dx2, norm_g, mod, hop1[3])

    partial = _pack_small(d_shift, d_scale, d_gate, d_norm_g, d_final_g, d_ln_g, d_ln_b, loss_part, d_sinks, d_sgu_b)
    small_flight = _start_copies([_with_own_slot(partial, me)], _own_block_copies(_all_others), N_DEV - 1, core,
                                 "small_grad_gather_start")
    _, land_first, dw_in_t, land_pair = _wait_copies(hop1, first_hop, small_flight[3], "w_in_grad_first_hop_wait")
    second_device = (4 * ((xi + ci) % 2) + 2 * ((yi + 1 - ci) % 2) + ci).astype(jnp.int32).reshape(1)
    relay = _relay_sum(second_device, dw_in_t, land_pair, land_first, W_IN_SHARD // 3)
    hop2 = _start_copies([relay, lax.empty((1, W_IN_SHARD, D_MODEL), BF16)], _second_hop_copies, 1, core,
                         "w_in_grad_second_hop_start")
    pair_out, land_out = _wait_copies(out_flight, _chip_copies, hop2[3], "w_out_grad_chip_wait")
    big = {"w_out": _adam_from_chips(chip, pair_out, [(land_out, k) for k in range(3)], w_out[0], m_w_out[0],
                                     v_w_out[0], "adam_w_out", W_OUT_SHARD // 2)}
    partial_all = _wait_copies(small_flight, _own_block_copies(_all_others), big["w_out"][0],
                               "small_grad_gather_wait")[0]
    d_sgu_w_all = _wait_copies(sgu_w_flight, _group(_own_block_copies(_all_others), 0, 1, 3), partial_all,
                               "sgu_w_grad_gather_wait")[0]
    weights = {"norm_g": norm_g, "b_ada": b_ada, "attn_sinks": attn_sinks, "sgu_ln_g": sgu_ln_g,
               "sgu_ln_b": sgu_ln_b, "sgu_w": sgu_w, "sgu_b": sgu_b, "final_g": final_g_row}
    moments_m = {"norm_g": m_norm_g, "b_ada": m_b_ada, "attn_sinks": m_attn_sinks, "sgu_ln_g": m_sgu_ln_g,
                 "sgu_ln_b": m_sgu_ln_b, "sgu_w": m_sgu_w, "sgu_b": m_sgu_b,
                 "final_g": m_final_g.reshape(1, D_MODEL)}
    moments_v = {"norm_g": v_norm_g, "b_ada": v_b_ada, "attn_sinks": v_attn_sinks, "sgu_ln_g": v_sgu_ln_g,
                 "sgu_ln_b": v_sgu_ln_b, "sgu_w": v_sgu_w, "sgu_b": v_sgu_b,
                 "final_g": v_final_g.reshape(1, D_MODEL)}
    loss, dmod_all, small = _adam_small(partial_all, d_sgu_w_all, weights, moments_m, moments_v)
    small["final_g"] = tuple(o.reshape(D_MODEL) for o in small["final_g"])

    big["w_ada"] = _adam_w_ada(device, c_act.T, dmod_all, w_ada[0], m_w_ada[0], v_w_ada[0])
    _, land_second = _wait_copies(hop2, _second_hop_copies, big["w_ada"][0], "w_in_grad_second_hop_wait")
    big["w_in"] = tuple(o.T for o in _adam_from_chips(
        device, dw_in_t, [(land_pair, 0), (land_first, 0), (land_second, 0)], w_in[0].T, m_w_in[0].T, v_w_in[0].T,
        "adam_w_in", W_IN_SHARD // 3))
    order = ["norm_g", "w_ada", "b_ada", "w_in", "attn_sinks", "sgu_ln_g", "sgu_ln_b", "sgu_w", "sgu_b", "w_out",
             "final_g"]
    outs = [loss.reshape(()), grad_x[None]]
    for k in range(4):
        for name in order:
            outs.append(big[name][k][None] if name in big else small[name][k])
    return tuple(outs)
```

```python
import jax
import jax.numpy as jnp
from jax import lax
from jax.experimental import pallas as pl
from jax.experimental.pallas import tpu as pltpu

F32 = jnp.float32
BF16 = jnp.bfloat16
MESH = pl.DeviceIdType.MESH

N_DEV = 8
D_MODEL = 2048
HEAD_DIM = 64
D_ATTN = 1024
N_Q_HEADS = 16
D_KV = 128
BLOCK = 128
D_SGU = 1024
SGU_GROUPS = 8
D_IN = 5376
W_IN_SHARD = D_IN // N_DEV
W_OUT_SHARD = D_MODEL // N_DEV
W_ADA_SHARD = 3 * D_MODEL // N_DEV
EPS = 1e-6
ATTN_SCALE = 0.125

ADAM_LR = 0.001
ADAM_B1 = 0.9
ADAM_B2 = 0.999
ADAM_EPS = 1e-08
ADAM_WD = 0.01
ADAM_STEP = 10

SEG_Q, SEG_KV, SEG_GA, SEG_U, SEG_VS, SEG_GS = 0, 1024, 1280, 2304, 3328, 4352

VMEM_LIMIT = 56 * 1024 * 1024

ROW_SHIFT, ROW_SCALE, ROW_GATE, ROW_NORM_G, ROW_FINAL_G, ROW_LN, ROW_MISC, ROW_SGU_B = 0, 1, 2, 3, 4, 5, 6, 8
SMALL_ROWS = 16


def _params(**kw):
    return pltpu.CompilerParams(vmem_limit_bytes=VMEM_LIMIT, **kw)


def _sigmoid(x):
    return 0.5 * (jnp.tanh(0.5 * x) + 1.0)


def _place():
    return lax.axis_index("x"), lax.axis_index("y"), lax.axis_index("c")


def _every_chip(x, y, c):
    return [0, 1, 2, 3]


def _first_hop_chips(x, y, c):
    first = _first_axis_chip(x, y, c)
    return [2 * first[0] + first[1], 2 * (1 - x) + (1 - y)]


def _pair_reduce(blocks, chips, name, row_chunk):
    _, _, r, cols = blocks.shape
    n = len(chips(0, 0, 0))
    assert r % row_chunk == 0

    def body(in_ref, out_ref, land, own, summed, send_sems, recv_sems, own_sems, out_sems):
        x, y, c = _place()
        sends, loads, stores = [], [], []
        for m in range(n):
            cp = pltpu.make_async_remote_copy(
                src_ref=in_ref.at[chips(x, y, 1 - c)[m], 1 - c], dst_ref=land.at[m], send_sem=send_sems.at[m],
                recv_sem=recv_sems.at[m], device_id=(x, y, 1 - c), device_id_type=MESH)
            cp.start()
            sends.append(cp)
            ld = pltpu.make_async_copy(in_ref.at[chips(x, y, c)[m], c], own.at[m], own_sems.at[m])
            ld.start()
            loads.append(ld)
        for m in range(n):
            sends[m].wait_recv()
            loads[m].wait()
            for k in range(r // row_chunk):
                rows = slice(k * row_chunk, (k + 1) * row_chunk)
                summed[m, rows, :] = (own[m, rows, :].astype(F32) + land[m, rows, :].astype(F32)).astype(BF16)
            st = pltpu.make_async_copy(summed.at[m], out_ref.at[m], out_sems.at[m])
            st.start()
            stores.append(st)
        for m in range(n):
            sends[m].wait_send()
            stores[m].wait()

    spec = pl.BlockSpec(memory_space=pl.ANY)
    return pl.pallas_call(
        body, name=name, out_shape=jax.ShapeDtypeStruct((n, r, cols), BF16),
        in_specs=[spec], out_specs=spec,
        scratch_shapes=[pltpu.VMEM((n, r, cols), BF16), pltpu.VMEM((n, r, cols), BF16), pltpu.VMEM((n, r, cols), BF16),
                        pltpu.SemaphoreType.DMA((n,)), pltpu.SemaphoreType.DMA((n,)), pltpu.SemaphoreType.DMA((n,)),
                        pltpu.SemaphoreType.DMA((n,))],
        compiler_params=_params(),
    )(blocks)


_HBM = pl.BlockSpec(memory_space=pltpu.HBM)
_SEM = pl.BlockSpec(memory_space=pltpu.SEMAPHORE)
_EFFECT = pltpu.SideEffectType.DATAFLOW_SIDE_EFFECTING


def _start_copies(bufs, copies, n_copies, after, name):
    nb = len(bufs)

    def body(*refs):
        for cp in copies(refs[:nb], refs[nb + 1], refs[nb + 2]):
            cp.start()
        refs[-1][...] = jnp.zeros_like(refs[-1])

    out = pl.pallas_call(
        body, name=name,
        out_shape=(pltpu.SemaphoreType.DMA((n_copies,)), pltpu.SemaphoreType.DMA((n_copies,)),
                   *[pltpu.HBM(b.shape, b.dtype) for b in bufs], jax.ShapeDtypeStruct((8, 128), F32)),
        in_specs=(_HBM,) * nb + (pl.BlockSpec(memory_space=pl.ANY),),
        out_specs=(_SEM, _SEM) + (_HBM,) * nb + (pl.BlockSpec(memory_space=pltpu.VMEM),),
        input_output_aliases={i: 2 + i for i in range(nb)},
        compiler_params=pltpu.CompilerParams(has_side_effects=_EFFECT),
    )(*[pltpu.with_memory_space_constraint(b, pltpu.HBM) for b in bufs], after)
    return out[0], out[1], list(out[2:2 + nb]), out[-1]


def _wait_copies(flight, copies, after, name):
    send_sems, recv_sems, bufs, _ = flight
    nb = len(bufs)

    def body(*refs):
        for cp in copies(refs[:nb], refs[nb], refs[nb + 1]):
            cp.wait_send()
            cp.wait_recv()

    return pl.pallas_call(
        body, name=name,
        out_shape=tuple(pltpu.HBM(b.shape, b.dtype) for b in bufs),
        in_specs=(_HBM,) * nb + (_SEM, _SEM, pl.BlockSpec(memory_space=pl.ANY)), out_specs=(_HBM,) * nb,
        input_output_aliases={i: i for i in range(nb)},
        compiler_params=pltpu.CompilerParams(has_side_effects=_EFFECT),
    )(*bufs, send_sems, recv_sems, after)


class _From:
    def __init__(self, sems, offset):
        self.sems, self.offset = sems, offset

    @property
    def at(self):
        return self

    def __getitem__(self, k):
        return self.sems.at[k + self.offset]


def _group(copies, first_buf, n_bufs, offset):
    def grouped(refs, send_sems, recv_sems):
        return copies(refs[first_buf:first_buf + n_bufs], _From(send_sems, offset), _From(recv_sems, offset))
    return grouped


def _wait_then_start(flight, waited, started, n_started, after, name, more_bufs=()):
    old_send, old_recv, bufs, _ = flight
    bufs = list(bufs) + [pltpu.with_memory_space_constraint(b, pltpu.HBM) for b in more_bufs]
    nb = len(bufs)

    def body(*refs):
        for cp in waited(refs[:nb], refs[nb], refs[nb + 1]):
            cp.wait_send()
            cp.wait_recv()
        for cp in started(refs[:nb], refs[nb + 3], refs[nb + 4]):
            cp.start()
        refs[-1][...] = jnp.zeros_like(refs[-1])

    out = pl.pallas_call(
        body, name=name,
        out_shape=(pltpu.SemaphoreType.DMA((n_started,)), pltpu.SemaphoreType.DMA((n_started,)),
                   *[pltpu.HBM(b.shape, b.dtype) for b in bufs], jax.ShapeDtypeStruct((8, 128), F32)),
        in_specs=(_HBM,) * nb + (_SEM, _SEM, pl.BlockSpec(memory_space=pl.ANY)),
        out_specs=(_SEM, _SEM) + (_HBM,) * nb + (pl.BlockSpec(memory_space=pltpu.VMEM),),
        input_output_aliases={i: 2 + i for i in range(nb)},
        compiler_params=pltpu.CompilerParams(has_side_effects=_EFFECT),
    )(*bufs, old_send, old_recv, after)
    return out[0], out[1], list(out[2:2 + nb]), out[-1]


def _late_pair_copies(refs, send_sems, recv_sems):
    blocks_ref, land_ref = refs
    x, y, c = _place()
    first = _first_axis_chip(x, y, c)
    devices = [4 * x + 2 * y + 1 - c, 4 * first[0] + 2 * first[1] + 1 - c]
    return [pltpu.make_async_remote_copy(
        src_ref=blocks_ref.at[devices[k]], dst_ref=land_ref.at[k], send_sem=send_sems.at[k], recv_sem=recv_sems.at[k],
        device_id=(x, y, 1 - c), device_id_type=MESH) for k in range(2)]


def _chip_copies(refs, send_sems, recv_sems):
    pair_ref, land_ref = refs
    x, y, c = _place()
    chips = [(1 - x, y), (x, 1 - y), (1 - x, 1 - y)]
    return [pltpu.make_async_remote_copy(
        src_ref=pair_ref.at[2 * chip[0] + chip[1]], dst_ref=land_ref.at[k],
        send_sem=send_sems.at[k], recv_sem=recv_sems.at[k],
        device_id=(*chip, c), device_id_type=MESH) for k, chip in enumerate(chips)]


def _first_hop_copies(refs, send_sems, recv_sems):
    pair_ref, land_ref = refs
    x, y, c = _place()
    return [pltpu.make_async_remote_copy(
        src_ref=pair_ref.at[k], dst_ref=land_ref.at[k], send_sem=send_sems.at[k], recv_sem=recv_sems.at[k],
        device_id=(*_first_axis_chip(x, y, c), c), device_id_type=MESH) for k in range(2)]


def _second_hop_copies(refs, send_sems, recv_sems):
    relay_ref, land_ref = refs
    x, y, c = _place()
    second = ((x + c) % 2, (y + 1 - c) % 2)
    return [pltpu.make_async_remote_copy(
        src_ref=relay_ref, dst_ref=land_ref.at[0], send_sem=send_sems.at[0], recv_sem=recv_sems.at[0],
        device_id=(*second, c), device_id_type=MESH)]


def _own_block_copies(targets):
    def copies(refs, send_sems, recv_sems):
        x, y, c = _place()
        mine = refs[0].at[4 * x + 2 * y + c]
        return [pltpu.make_async_remote_copy(
            src_ref=mine, dst_ref=mine, send_sem=send_sems.at[k], recv_sem=recv_sems.at[k],
            device_id=to, device_id_type=MESH) for k, to in enumerate(targets(x, y, c))]
    return copies


def _all_others(x, y, c):
    flip = lambda v, f: 1 - v if f else v
    return [(flip(x, r & 4), flip(y, r & 2), flip(c, r & 1)) for r in range(1, N_DEV)]


def _forward_copies(refs, send_sems, recv_sems):
    x, y, c = _place()
    chips = [(1 - x, y), (x, 1 - y), (1 - x, 1 - y)]
    return [pltpu.make_async_remote_copy(
        src_ref=refs[0].at[4 * chip[0] + 2 * chip[1] + c], dst_ref=refs[0].at[4 * chip[0] + 2 * chip[1] + c],
        send_sem=send_sems.at[k], recv_sem=recv_sems.at[k],
        device_id=(x, y, 1 - c), device_id_type=MESH) for k, chip in enumerate(chips)]


def _first_axis_chip(x, y, c):
    return (x + 1 - c) % 2, (y + c) % 2


def _second_axis_chip(x, y, c):
    return (x + c) % 2, (y + 1 - c) % 2


def _first_targets(x, y, c):
    return [(x, y, 1 - c), (*_first_axis_chip(x, y, c), c)]


def _all_gather_small(shard, name, dep=None):
    def body(in_ref, *refs):
        out_ref, send_sems, recv_sems, local_sem = refs[-4:]
        x, y, c = _place()
        me, sibling = 4 * x + 2 * y + c, (x, y, 1 - c)
        first, second = _first_axis_chip(x, y, c), _second_axis_chip(x, y, c)

        def pair(chip):
            return out_ref.at[pl.ds(2 * (2 * chip[0] + chip[1]), 2)]

        def exchange(k, src, dst, to):
            cp = pltpu.make_async_remote_copy(src_ref=src, dst_ref=dst, send_sem=send_sems.at[k],
                                              recv_sem=recv_sems.at[k], device_id=to, device_id_type=MESH)
            cp.start()
            cp.wait()

        own = pltpu.make_async_copy(in_ref, out_ref.at[me], local_sem)
        own.start()
        exchange(0, in_ref, out_ref.at[me], sibling)
        own.wait()
        exchange(1, pair((x, y)), pair((x, y)), (*second, c))
        exchange(2, pair(second), pair(second), sibling)
        exchange(3, pair(first), pair(first), (*second, c))

    spec = pl.BlockSpec(memory_space=pltpu.VMEM)
    deps = [] if dep is None else [dep]
    return pl.pallas_call(
        body, name=name, out_shape=jax.ShapeDtypeStruct((N_DEV,) + shard.shape, shard.dtype),
        in_specs=[spec] * (1 + len(deps)), out_specs=spec,
        scratch_shapes=[pltpu.SemaphoreType.DMA((4,)), pltpu.SemaphoreType.DMA((4,)), pltpu.SemaphoreType.DMA],
        compiler_params=_params(),
    )(shard, *deps)


def _slot_copies(refs, send_sems, recv_sems, plan):
    copies = []
    for k, ((px, py, pc), to) in enumerate(plan):
        blk = refs[0].at[4 * px + 2 * py + pc]
        copies.append(pltpu.make_async_remote_copy(
            src_ref=blk, dst_ref=blk, send_sem=send_sems.at[k], recv_sem=recv_sems.at[k],
            device_id=to, device_id_type=MESH))
    return copies


def _second_axis_stage_copies(refs, send_sems, recv_sems):
    x, y, c = _place()
    first, second = (*_first_axis_chip(x, y, c), c), (*_second_axis_chip(x, y, c), c)
    return _slot_copies(refs, send_sems, recv_sems, [((x, y, c), second), (first, (x, y, 1 - c)), (first, second)])


def _second_axis_forward_copies(refs, send_sems, recv_sems):
    x, y, c = _place()
    return _slot_copies(refs, send_sems, recv_sems, [((*_second_axis_chip(x, y, c), c), (x, y, 1 - c))])


def _diagonal_forward_copies(refs, send_sems, recv_sems):
    x, y, c = _place()
    blk = refs[0].at[4 * (1 - x) + 2 * (1 - y) + c]
    return [pltpu.make_async_remote_copy(
        src_ref=blk, dst_ref=blk, send_sem=send_sems.at[0], recv_sem=recv_sems.at[0],
        device_id=(x, y, 1 - c), device_id_type=MESH)]


def _with_own_slot(block, me):
    return lax.dynamic_update_index_in_dim(lax.empty((N_DEV,) + block.shape, block.dtype), block, me, 0)


def _w_in_grad(dz, h, dep, tm):
    (t, m), n = dz.shape, h.shape[1]
    assert m % tm == 0 and dz.dtype == BF16 and h.dtype == BF16

    def body(dz_ref, h_ref, dep_ref, o_ref):
        o_ref[...] = _dot_tn(dz_ref[...], h_ref[...]).astype(BF16)

    return pl.pallas_call(
        body, name="w_in_grad", grid=(m // tm,),
        in_specs=[pl.BlockSpec((t, tm), lambda i: (0, i)), pl.BlockSpec((t, n), lambda i: (0, 0)),
                  pl.BlockSpec((8, 128), lambda i: (0, 0))],
        out_specs=pl.BlockSpec((tm, n), lambda i: (i, 0)),
        out_shape=jax.ShapeDtypeStruct((m, n), BF16),
        compiler_params=_params(dimension_semantics=("arbitrary",)),
    )(dz, h, dep)


Z_TILE = 768
_Z_TILE_ORDER = ((0, 1, 2, 3, 4, 5, 6), (2, 0, 1, 6, 3, 4, 5), (4, 0, 5, 6, 1, 2, 3), (6, 2, 3, 4, 0, 1, 5))
_Z_EARLY_TILES = 4


def _z_proj(h, w_in_t, chip, first, count, z_prev, name, tr=1024):
    t = h.shape[0]

    def body(chip_ref, h_ref, w_ref, z_prev_ref, z_ref):
        z_ref[...] = _dot_nt(h_ref[...], w_ref[...])

    def tile(j, chip_ref):
        picked = 0
        for c, order in enumerate(_Z_TILE_ORDER):
            for k in range(count):
                picked = picked + jnp.where((chip_ref[0] == c) & (j == k), order[first + k], 0)
        return picked

    return pl.pallas_call(
        body, name=name,
        grid_spec=pltpu.PrefetchScalarGridSpec(
            num_scalar_prefetch=1, grid=(count, t // tr),
            in_specs=[pl.BlockSpec((tr, D_MODEL), lambda j, i, o: (i, 0)),
                      pl.BlockSpec((Z_TILE, D_MODEL), lambda j, i, o: (tile(j, o), 0)),
                      pl.BlockSpec(memory_space=pl.ANY)],
            out_specs=pl.BlockSpec((tr, Z_TILE), lambda j, i, o: (i, tile(j, o)))),
        out_shape=jax.ShapeDtypeStruct((t, D_IN), F32),
        input_output_aliases={3: 0},
        compiler_params=_params(dimension_semantics=("arbitrary", "arbitrary")),
    )(chip, h, w_in_t, z_prev)


def _modulation(device, c_all, w_ada, b_ada):
    def body(device_ref, c_ref, w_ref, b_ref, act_ref, mod_ref):
        cv = c_ref[...]
        act = cv * _sigmoid(cv)
        act_ref[...] = act
        mod_ref[...] = jnp.dot(act.astype(BF16), w_ref[...].astype(BF16), preferred_element_type=F32) + b_ref[...]

    whole = lambda a: pl.BlockSpec(a.shape, lambda i, device_ref: (0,) * a.ndim)
    return pl.pallas_call(
        body, name="modulation",
        grid_spec=pltpu.PrefetchScalarGridSpec(
            num_scalar_prefetch=1, grid=(1,),
            in_specs=[whole(c_all), whole(w_ada), pl.BlockSpec((1, W_ADA_SHARD), lambda i, device_ref: (0, device_ref[0]))],
            out_specs=(whole(c_all), pl.BlockSpec((N_DEV, W_ADA_SHARD), lambda i, device_ref: (0, 0)))),
        out_shape=(jax.ShapeDtypeStruct(c_all.shape, F32), jax.ShapeDtypeStruct((N_DEV, W_ADA_SHARD), F32)),
        compiler_params=_params(dimension_semantics=("arbitrary",)),
    )(device, c_all, w_ada, b_ada)


MOD_SHIFT, MOD_SCALE, MOD_GATE = 0, 1, 2


def _mod_spec(part, d):
    return pl.BlockSpec((1, d), lambda i: (0, part))


def _norm_z_proj_own(x, norm_g, mod, w_in_t, chip, tm=512):
    t, d = x.shape

    def body(chip_ref, x_ref, g_ref, sc_ref, sh_ref, w_ref, h_ref, z_ref):
        xv = x_ref[...]
        r = lax.rsqrt(jnp.mean(xv * xv, axis=-1, keepdims=True) + EPS)
        h = ((xv * r) * g_ref[...] * (1.0 + sc_ref[...]) + sh_ref[...]).astype(BF16)
        h_ref[...] = h
        z_ref[...] = _dot_nt(h, w_ref[...])

    def own_tile(chip_ref):
        picked = 0
        for c, order in enumerate(_Z_TILE_ORDER):
            picked = picked + jnp.where(chip_ref[0] == c, order[0], 0)
        return picked

    def row(part):
        return pl.BlockSpec((1, d), lambda i, o: (0, part))

    return pl.pallas_call(
        body, name="norm_z_proj_own",
        grid_spec=pltpu.PrefetchScalarGridSpec(
            num_scalar_prefetch=1, grid=(t // tm,),
            in_specs=[pl.BlockSpec((tm, d), lambda i, o: (i, 0)), row(0), row(MOD_SCALE), row(MOD_SHIFT),
                      pl.BlockSpec((Z_TILE, d), lambda i, o: (own_tile(o), 0))],
            out_specs=(pl.BlockSpec((tm, d), lambda i, o: (i, 0)),
                       pl.BlockSpec((tm, Z_TILE), lambda i, o: (i, own_tile(o))))),
        out_shape=(jax.ShapeDtypeStruct((t, d), BF16), jax.ShapeDtypeStruct((t, D_IN), F32)),
        compiler_params=_params(dimension_semantics=("arbitrary",)),
    )(chip, x, norm_g, mod, mod, w_in_t)


def _window_bias(block_index):
    s = lax.broadcasted_iota(jnp.int32, (2 * BLOCK, BLOCK), 0)
    t = lax.broadcasted_iota(jnp.int32, (2 * BLOCK, BLOCK), 1)
    valid = ((s < BLOCK) & (s > t) & (block_index > 0)) | ((s >= BLOCK) & ((s - BLOCK) <= t))
    bias = jnp.where(valid, 0.0, -jnp.inf).astype(F32)
    return jnp.concatenate([bias] * 8, axis=1)


def _heads_t(pair_blocks, g):
    top = lax.broadcasted_iota(jnp.int32, (BLOCK, BLOCK), 0) < HEAD_DIM
    zeros = jnp.zeros((HEAD_DIM, BLOCK), F32)
    tiles = []
    for blk in pair_blocks:
        tp = blk.T
        if g == 0:
            tiles += [jnp.where(top, tp, 0.0), jnp.concatenate([tp[HEAD_DIM:], zeros], axis=0)]
        else:
            tiles += [jnp.concatenate([zeros, tp[:HEAD_DIM]], axis=0), jnp.where(top, 0.0, tp)]
    return jnp.concatenate(tiles, axis=1)


def _pair_block(xt, p, g):
    r0 = HEAD_DIM * g
    even = xt[r0:r0 + HEAD_DIM, (2 * p) * BLOCK:(2 * p + 1) * BLOCK]
    odd = xt[r0:r0 + HEAD_DIM, (2 * p + 1) * BLOCK:(2 * p + 2) * BLOCK]
    return jnp.concatenate([even, odd], axis=0).T


def _softmax_t(scores_t, bias, sink):
    st = scores_t + bias
    m = jnp.maximum(jnp.max(st, axis=0, keepdims=True), sink)
    e = jnp.exp(st - m)
    es = jnp.exp(sink - m)
    inv = 1.0 / (jnp.sum(e, axis=0, keepdims=True) + es)
    return e * inv, es * inv


def _dot(a, b):
    return jnp.dot(a, b, preferred_element_type=F32)


def _dot_nt(a, b):
    return lax.dot_general(a, b, (((1,), (1,)), ((), ())), preferred_element_type=F32)


def _dot_tn(a, b):
    return lax.dot_general(a, b, (((0,), (0,)), ((), ())), preferred_element_type=F32)


def _layer_norm_fwd(v):
    mu = jnp.mean(v, axis=-1, keepdims=True)
    xc = v - mu
    rstd = lax.rsqrt(jnp.mean(xc * xc, axis=-1, keepdims=True) + EPS)
    return xc * rstd, rstd


def _tril(transposed=False):
    t = lax.broadcasted_iota(jnp.int32, (BLOCK, BLOCK), 0)
    s = lax.broadcasted_iota(jnp.int32, (BLOCK, BLOCK), 1)
    return s >= t if transposed else t >= s


def _const_spec(shape):
    return pl.BlockSpec(shape, lambda i: (0,) * len(shape))


def _keys_values(z_ref, kvp):
    kvc = z_ref[:, SEG_KV:SEG_KV + 2 * D_KV]
    kk = jnp.concatenate([kvp[:, :D_KV], kvc[:, :D_KV]], axis=0)
    vv = jnp.concatenate([kvp[:, D_KV:], kvc[:, D_KV:]], axis=0)
    return kk, vv


MIXER_BLOCKS = 2


class _Rows:
    def __init__(self, ref, sub):
        self.ref, self.rows = ref, slice(sub * BLOCK, (sub + 1) * BLOCK)

    def __getitem__(self, idx):
        return self.ref[self.rows, idx[1]]

    def __setitem__(self, idx, value):
        self.ref[self.rows, idx[1]] = value


def _kv_before_spec(index):
    return pl.BlockSpec((BLOCK, 2 * D_KV),
                        lambda i: (jnp.maximum(MIXER_BLOCKS * index(i) - 1, 0), SEG_KV // (2 * D_KV)))


def _pair_cols(g, p, base=0):
    return slice(base + (4 * g + p) * 128, base + (4 * g + p + 1) * 128)


def _mixer_fwd(z, sink_rows, ln_g, ln_b, sgu_w, sgu_bt):
    t = z.shape[0]

    def body(z_all, kvp_ref, sink_ref, lng_ref, lnb_ref, w_ref, bt_ref, a_all, prob_ref, sink_prob_ref):
        kv_before = kvp_ref[...]
        for sub in range(MIXER_BLOCKS):
            z_ref, a_ref = _Rows(z_all, sub), _Rows(a_all, sub)
            one_block(z_ref, kv_before, MIXER_BLOCKS * pl.program_id(0) + sub, sink_ref, lng_ref, lnb_ref, w_ref,
                      bt_ref, a_ref, prob_ref.at[sub], sink_prob_ref.at[sub])
            kv_before = z_ref[:, SEG_KV:SEG_KV + 2 * D_KV]

    def one_block(z_ref, kv_before, block_index, sink_ref, lng_ref, lnb_ref, w_ref, bt_ref, a_ref, prob_ref,
                  sink_prob_ref):
        bias = _window_bias(block_index)
        kk, vv = _keys_values(z_ref, kv_before)
        kk_b, vvt_b = kk.astype(BF16), vv.T.astype(BF16)
        for g in range(2):
            qt = _heads_t([z_ref[:, _pair_cols(g, p, SEG_Q)] * ATTN_SCALE for p in range(4)], g).astype(BF16)
            prob, sink_prob = _softmax_t(_dot(kk_b, qt), bias, sink_ref[g])
            prob_b = prob.astype(BF16)
            prob_ref[g] = prob_b
            sink_prob_ref[g] = sink_prob
            ot = _dot(vvt_b, prob_b)
            for p in range(4):
                gate = z_ref[:, _pair_cols(g, p, SEG_GA)]
                a_ref[:, _pair_cols(g, p)] = (_pair_block(ot, p, g) * (gate * _sigmoid(gate))).astype(BF16)

        vhat, _ = _layer_norm_fwd(z_ref[:, SEG_VS:SEG_VS + D_SGU])
        vn = vhat * lng_ref[...] + lnb_ref[...]
        tril = _tril()
        for g in range(SGU_GROUPS):
            cols = slice(g * 128, (g + 1) * 128)
            wm = jnp.where(tril, w_ref[g], 0.0).astype(BF16)
            mixed = _dot(wm, vn[:, cols].astype(BF16)) + bt_ref[:, g:g + 1]
            gate = z_ref[:, SEG_GS + g * 128:SEG_GS + (g + 1) * 128]
            a_ref[:, D_ATTN + g * 128:D_ATTN + (g + 1) * 128] = (
                (z_ref[:, SEG_U + g * 128:SEG_U + (g + 1) * 128] * mixed) * (gate * _sigmoid(gate))).astype(BF16)

    rows = MIXER_BLOCKS * BLOCK
    return pl.pallas_call(
        body, name="mixer_fwd", grid=(t // rows,),
        in_specs=[pl.BlockSpec((rows, D_IN), lambda i: (i, 0)), _kv_before_spec(lambda i: i),
                  _const_spec((2, 1, 8 * BLOCK)), _const_spec((1, D_SGU)), _const_spec((1, D_SGU)),
                  _const_spec((SGU_GROUPS, BLOCK, BLOCK)), _const_spec((BLOCK, SGU_GROUPS))],
        out_specs=(pl.BlockSpec((rows, D_MODEL), lambda i: (i, 0)),
                   pl.BlockSpec((MIXER_BLOCKS, 2, 2 * BLOCK, 8 * BLOCK), lambda i: (i, 0, 0, 0)),
                   pl.BlockSpec((MIXER_BLOCKS, 2, 1, 8 * BLOCK), lambda i: (i, 0, 0, 0))),
        out_shape=(jax.ShapeDtypeStruct((t, D_MODEL), BF16),
                   jax.ShapeDtypeStruct((t // BLOCK, 2, 2 * BLOCK, 8 * BLOCK), BF16),
                   jax.ShapeDtypeStruct((t // BLOCK, 2, 1, 8 * BLOCK), F32)),
        compiler_params=_params(dimension_semantics=("arbitrary",)),
    )(z, z, sink_rows, ln_g, ln_b, sgu_w, sgu_bt)


def _mixer_bwd(z, da, probs, sink_probs, ln_g, ln_b, sgu_w, sgu_wt, sgu_bt, a, dy):
    t = z.shape[0]

    def body(z_all, kvp_ref, da_all, prob_ref, sink_prob_ref, lng_ref, lnb_ref, w_ref, wt_ref, bt_ref, a_ref, dy_ref,
             dz_all, dsink_ref, dw_ref, db_ref, dlng_ref, dlnb_ref, dw_out_ref, carry_ref, dsink_acc, dbt_acc):
        step = pl.program_id(0)

        @pl.when(step == 0)
        def _():
            carry_ref[...] = jnp.zeros_like(carry_ref)
            dsink_acc[...] = jnp.zeros_like(dsink_acc)
            dbt_acc[...] = jnp.zeros_like(dbt_acc)
            dw_ref[...] = jnp.zeros_like(dw_ref)
            dlng_ref[...] = jnp.zeros_like(dlng_ref)
            dlnb_ref[...] = jnp.zeros_like(dlnb_ref)

        carry = carry_ref[...]
        dw_out_ref[...] = _dot_tn(a_ref[...], dy_ref[...]).astype(BF16)
        for sub in reversed(range(MIXER_BLOCKS)):
            kv_before = kvp_ref[...] if sub == 0 else _Rows(z_all, sub - 1)[:, SEG_KV:SEG_KV + 2 * D_KV]
            carry = one_block(_Rows(z_all, sub), kv_before, _Rows(da_all, sub), prob_ref.at[sub], sink_prob_ref.at[sub],
                              carry, lng_ref, lnb_ref, w_ref, wt_ref, bt_ref, _Rows(dz_all, sub),
                              dw_ref, dlng_ref, dlnb_ref, dsink_acc, dbt_acc)
        carry_ref[...] = carry

        @pl.when(step == ns - 1)
        def _():
            db_ref[...] = dbt_acc[...].T[:SGU_GROUPS]
            lane_row = lax.broadcasted_iota(jnp.int32, (1, 128), 1)
            d_sink = jnp.zeros((1, 128), F32)
            for g in range(2):
                acc = dsink_acc[g]
                for j in range(8):
                    head_sum = jnp.sum(acc[:, j * BLOCK:(j + 1) * BLOCK], axis=-1, keepdims=True)
                    d_sink = d_sink + jnp.where(lane_row == 8 * g + j, head_sum, 0.0)
            dsink_ref[...] = d_sink

    def one_block(z_ref, kv_before, da_ref, prob_ref, sink_prob_ref, carry, lng_ref, lnb_ref, w_ref, wt_ref, bt_ref,
                  dz_ref, dw_ref, dlng_ref, dlnb_ref, dsink_acc, dbt_acc):
        kk, vv = _keys_values(z_ref, kv_before)
        vv_b = vv.astype(BF16)
        kkt_b, vvt_b = kk.T.astype(BF16), vv.T.astype(BF16)
        dkk = jnp.zeros((2 * BLOCK, D_KV), F32)
        dvv = jnp.zeros((2 * BLOCK, D_KV), F32)
        for g in range(2):
            qt = _heads_t([z_ref[:, _pair_cols(g, p, SEG_Q)] * ATTN_SCALE for p in range(4)], g).astype(BF16)
            prob_b, sink_prob = prob_ref[g], sink_prob_ref[g]
            prob = prob_b.astype(F32)
            ot = _dot(vvt_b, prob_b)
            gates = [z_ref[:, _pair_cols(g, p, SEG_GA)] for p in range(4)]
            sig = [_sigmoid(gt) for gt in gates]
            d_attn = [da_ref[:, _pair_cols(g, p)] for p in range(4)]
            d_ot = _heads_t([d_attn[p] * (gates[p] * sig[p]) for p in range(4)], g).astype(BF16)
            d_prob = _dot(vv_b, d_ot)
            delta = jnp.sum(prob * d_prob, axis=0, keepdims=True)
            d_scores = (prob * (d_prob - delta)).astype(BF16)
            dsink_acc[g] -= sink_prob * delta
            d_qt = _dot(kkt_b, d_scores)
            dkk = dkk + _dot_nt(d_scores, qt)
            dvv = dvv + _dot_nt(prob_b, d_ot)
            for p in range(4):
                dz_ref[:, _pair_cols(g, p, SEG_Q)] = (_pair_block(d_qt, p, g) * ATTN_SCALE).astype(BF16)
                d_silu = sig[p] * (1.0 + gates[p] * (1.0 - sig[p]))
                dz_ref[:, _pair_cols(g, p, SEG_GA)] = (d_attn[p] * _pair_block(ot, p, g) * d_silu).astype(BF16)
        d_kv = jnp.concatenate([dkk, dvv], axis=1)
        dz_ref[:, SEG_KV:SEG_KV + 2 * D_KV] = (d_kv[BLOCK:] + carry).astype(BF16)

        vhat, rstd = _layer_norm_fwd(z_ref[:, SEG_VS:SEG_VS + D_SGU])
        lng = lng_ref[...]
        vn = vhat * lng + lnb_ref[...]
        tril, triu = _tril(), _tril(transposed=True)
        lane = lax.broadcasted_iota(jnp.int32, (BLOCK, 128), 1)
        d_bt = jnp.zeros((BLOCK, 128), F32)
        d_vn = []
        for g in range(SGU_GROUPS):
            cols = slice(g * 128, (g + 1) * 128)
            wm = jnp.where(tril, w_ref[g], 0.0).astype(BF16)
            wmt = jnp.where(triu, wt_ref[g], 0.0).astype(BF16)
            vn_g = vn[:, cols].astype(BF16)
            mixed = _dot(wm, vn_g) + bt_ref[:, g:g + 1]
            gate = z_ref[:, SEG_GS + g * 128:SEG_GS + (g + 1) * 128]
            u = z_ref[:, SEG_U + g * 128:SEG_U + (g + 1) * 128]
            d_out = da_ref[:, D_ATTN + g * 128:D_ATTN + (g + 1) * 128]
            sg = _sigmoid(gate)
            d_um = d_out * (gate * sg)
            dz_ref[:, SEG_U + g * 128:SEG_U + (g + 1) * 128] = (d_um * mixed).astype(BF16)
            dz_ref[:, SEG_GS + g * 128:SEG_GS + (g + 1) * 128] = (
                d_out * (u * mixed) * (sg * (1.0 + gate * (1.0 - sg)))).astype(BF16)
            d_mixed = d_um * u
            d_mixed_b = d_mixed.astype(BF16)
            dw_ref[g] += jnp.where(tril, _dot_nt(d_mixed_b, vn_g), 0.0)
            d_bt = d_bt + jnp.where(lane == g, jnp.sum(d_mixed, axis=-1, keepdims=True), 0.0)
            d_vn.append(_dot(wmt, d_mixed_b))
        dbt_acc[...] += d_bt
        d_vn = jnp.concatenate(d_vn, axis=1)
        dlng_ref[...] += jnp.sum(d_vn * vhat, axis=0, keepdims=True)
        dlnb_ref[...] += jnp.sum(d_vn, axis=0, keepdims=True)
        d_vhat = d_vn * lng
        d_v = rstd * (d_vhat - jnp.mean(d_vhat, axis=-1, keepdims=True)
                      - vhat * jnp.mean(d_vhat * vhat, axis=-1, keepdims=True))
        dz_ref[:, SEG_VS:SEG_VS + D_SGU] = d_v.astype(BF16)
        return d_kv[:BLOCK]

    rows = MIXER_BLOCKS * BLOCK
    ns = t // rows
    rev = lambda i: ns - 1 - i
    tile = a.shape[1] // ns
    assert tile % 128 == 0
    return pl.pallas_call(
        body, name="mixer_bwd", grid=(ns,),
        in_specs=[pl.BlockSpec((rows, D_IN), lambda i: (rev(i), 0)), _kv_before_spec(rev),
                  pl.BlockSpec((rows, D_MODEL), lambda i: (rev(i), 0)),
                  pl.BlockSpec((MIXER_BLOCKS, 2, 2 * BLOCK, 8 * BLOCK), lambda i: (rev(i), 0, 0, 0)),
                  pl.BlockSpec((MIXER_BLOCKS, 2, 1, 8 * BLOCK), lambda i: (rev(i), 0, 0, 0)),
                  _const_spec((1, D_SGU)), _const_spec((1, D_SGU)),
                  _const_spec((SGU_GROUPS, BLOCK, BLOCK)), _const_spec((SGU_GROUPS, BLOCK, BLOCK)),
                  _const_spec((BLOCK, SGU_GROUPS)), pl.BlockSpec((t, tile), lambda i: (0, i)),
                  pl.BlockSpec(dy.shape, lambda i: (0, 0), pipeline_mode=pl.Buffered(1))],
        out_specs=(pl.BlockSpec((rows, D_IN), lambda i: (rev(i), 0)), _const_spec((1, 128)),
                   _const_spec((SGU_GROUPS, BLOCK, BLOCK)), _const_spec((SGU_GROUPS, BLOCK)),
                   _const_spec((1, D_SGU)), _const_spec((1, D_SGU)),
                   pl.BlockSpec((tile, dy.shape[1]), lambda i: (i, 0))),
        out_shape=(jax.ShapeDtypeStruct((t, D_IN), BF16), jax.ShapeDtypeStruct((1, 128), F32),
                   jax.ShapeDtypeStruct((SGU_GROUPS, BLOCK, BLOCK), F32), jax.ShapeDtypeStruct((SGU_GROUPS, BLOCK), F32),
                   jax.ShapeDtypeStruct((1, D_SGU), F32), jax.ShapeDtypeStruct((1, D_SGU), F32),
                   jax.ShapeDtypeStruct((a.shape[1], dy.shape[1]), BF16)),
        scratch_shapes=[pltpu.VMEM((BLOCK, 2 * D_KV), F32), pltpu.VMEM((2, 1, 8 * BLOCK), F32),
                        pltpu.VMEM((BLOCK, 128), F32)],
        compiler_params=_params(dimension_semantics=("arbitrary",)),
    )(z, z, da, probs, sink_probs, ln_g, ln_b, sgu_w, sgu_wt, sgu_bt, a, dy)


def _out_proj_head(a, w_out_full, x, target, mod, final_g, tm=256):
    t, d = x.shape

    def body(a_ref, w_ref, x_ref, tg_ref, gate_ref, fg_ref, dx2_ref, dy_ref, da_ref, loss_ref, dfg_ref, dgate_ref):
        @pl.when(pl.program_id(0) == 0)
        def _():
            loss_ref[...] = jnp.zeros_like(loss_ref)
            dfg_ref[...] = jnp.zeros_like(dfg_ref)
            dgate_ref[...] = jnp.zeros_like(dgate_ref)

        yv, gate, fg = _dot(a_ref[...], w_ref[...]), gate_ref[...], fg_ref[...]
        x2 = x_ref[...] + gate * yv
        r2 = lax.rsqrt(jnp.mean(x2 * x2, axis=-1, keepdims=True) + EPS)
        nrm = x2 * r2
        err = nrm * fg - tg_ref[...]
        loss_ref[...] += 0.5 * jnp.sum(jnp.mean(err * err, axis=-1, keepdims=True), axis=0, keepdims=True)
        fg_d = fg * (1.0 / d)
        err_nrm = err * nrm
        dfg_ref[...] += jnp.sum(err_nrm, axis=0, keepdims=True) * (1.0 / d)
        d_nrm = err * fg_d
        dx2 = r2 * (d_nrm - nrm * jnp.mean(err_nrm * fg_d, axis=-1, keepdims=True))
        dx2_ref[...] = dx2
        dgate_ref[...] += jnp.sum(dx2 * yv, axis=0, keepdims=True)
        dy = (dx2 * gate).astype(BF16)
        dy_ref[...] = dy
        da_ref[...] = _dot_nt(dy, w_ref[...])

    blk = pl.BlockSpec((tm, d), lambda i: (i, 0))
    a_blk = pl.BlockSpec((tm, a.shape[1]), lambda i: (i, 0))
    row = _const_spec((1, d))
    whole = pl.BlockSpec(w_out_full.shape, lambda i: (0, 0), pipeline_mode=pl.Buffered(1))
    return pl.pallas_call(
        body, name="out_proj_head", grid=(t // tm,),
        in_specs=[a_blk, whole, blk, blk, _mod_spec(MOD_GATE, d), row],
        out_specs=(blk, blk, a_blk, _const_spec((1, 128)), row, row),
        out_shape=(jax.ShapeDtypeStruct((t, d), F32), jax.ShapeDtypeStruct((t, d), BF16),
                   jax.ShapeDtypeStruct(a.shape, F32), jax.ShapeDtypeStruct((1, 128), F32),
                   jax.ShapeDtypeStruct((1, d), F32), jax.ShapeDtypeStruct((1, d), F32)),
        compiler_params=_params(dimension_semantics=("arbitrary",)),
    )(a, w_out_full, x, target, mod, final_g)


def _z_proj_bwd_norm(dz, w_in_t, x, dx2, norm_g, mod, dep, tm=256):
    t, d = x.shape
    k_all = dz.shape[1]
    nk = k_all // Z_TILE

    def body(dz_hbm, w_hbm, x_hbm, dx2_hbm, g_ref, sc_ref, dep_ref, gx_hbm, dshift_ref, dscale_ref, dg_ref,
             w_ref, dz0, x0, dx20, sems):
        g = g_ref[...]
        one_plus = 1.0 + sc_ref[...]
        gain = one_plus * g

        def norm_bwd(dh, xv, dx2v):
            r = lax.rsqrt(jnp.mean(xv * xv, axis=-1, keepdims=True) + EPS)
            xn = xv * r
            dh_xn = dh * xn
            dh_xn_sum = jnp.sum(dh_xn, axis=0, keepdims=True)
            dshift_ref[...] += jnp.sum(dh, axis=0, keepdims=True)
            dscale_ref[...] += dh_xn_sum * g
            dg_ref[...] += dh_xn_sum * one_plus
            d_xn = dh * gain
            return dx2v + r * (d_xn - xn * jnp.mean(dh_xn * gain, axis=-1, keepdims=True))

        first = pl.ds(0, tm)
        loads = [pltpu.make_async_copy(dz_hbm.at[first], dz0, sems.at[nk]),
                 pltpu.make_async_copy(x_hbm.at[first], x0, sems.at[nk + 1]),
                 pltpu.make_async_copy(dx2_hbm.at[first], dx20, sems.at[nk + 2])]
        chunks = [pltpu.make_async_copy(w_hbm.at[pl.ds(k * Z_TILE, Z_TILE)], w_ref.at[pl.ds(k * Z_TILE, Z_TILE)],
                                        sems.at[k]) for k in range(nk)]
        loads[0].start()
        for cp in chunks[:1] + loads[1:] + chunks[1:]:
            cp.start()
        dshift_ref[...] = jnp.zeros_like(dshift_ref)
        dscale_ref[...] = jnp.zeros_like(dscale_ref)
        dg_ref[...] = jnp.zeros_like(dg_ref)
        loads[0].wait()
        dh = None
        for k in range(nk):
            chunks[k].wait()
            cols = slice(k * Z_TILE, (k + 1) * Z_TILE)
            part = _dot(dz0[:, cols], w_ref[cols, :])
            dh = part if dh is None else dh + part
        loads[1].wait()
        loads[2].wait()
        dx20[...] = norm_bwd(dh, x0[...], dx20[...])
        store = pltpu.make_async_copy(dx20, gx_hbm.at[first], sems.at[nk + 3])
        store.start()

        def rest(dz_ref, x_ref, dx2_ref, gx_ref):
            gx_ref[...] = norm_bwd(_dot(dz_ref[...], w_ref[...]), x_ref[...], dx2_ref[...])

        blk = pl.BlockSpec((tm, d), lambda i: (i + 1, 0))
        pltpu.emit_pipeline(rest, grid=(t // tm - 1,),
                            in_specs=[pl.BlockSpec((tm, k_all), lambda i: (i + 1, 0)), blk, blk],
                            out_specs=[blk])(dz_hbm, x_hbm, dx2_hbm, gx_hbm)
        store.wait()

    any_spec = pl.BlockSpec(memory_space=pl.ANY)
    row = _const_spec((1, d))
    return pl.pallas_call(
        body, name="z_proj_bwd_norm", grid=(1,),
        in_specs=[any_spec, any_spec, any_spec, any_spec, row, _mod_spec(MOD_SCALE, d), _const_spec((8, 128))],
        out_specs=(any_spec, row, row, row),
        out_shape=(jax.ShapeDtypeStruct((t, d), F32),) + (jax.ShapeDtypeStruct((1, d), F32),) * 3,
        scratch_shapes=[pltpu.VMEM(w_in_t.shape, BF16), pltpu.VMEM((tm, k_all), BF16), pltpu.VMEM((tm, d), F32),
                        pltpu.VMEM((tm, d), F32), pltpu.SemaphoreType.DMA((nk + 4,))],
        compiler_params=_params(dimension_semantics=("arbitrary",)),
    )(dz, w_in_t, x, dx2, norm_g, mod, dep)


def _adamw(w, g, m, v):
    m = ADAM_B1 * m + (1.0 - ADAM_B1) * g
    v = ADAM_B2 * v + (1.0 - ADAM_B2) * (g * g)
    m_hat = m / (1.0 - ADAM_B1 ** ADAM_STEP)
    v_hat = v / (1.0 - ADAM_B2 ** ADAM_STEP)
    delta = -ADAM_LR * (m_hat / (jnp.sqrt(v_hat) + ADAM_EPS) + ADAM_WD * w)
    return delta, m, v


def _relay_sum(device, blocks, land_pair, land_first, tr):
    _, r, c = blocks.shape

    def body(device_ref, a_ref, b_ref, c_ref, o_ref):
        o_ref[...] = (a_ref[...].astype(F32) + b_ref[...].astype(F32) + c_ref[...].astype(F32)).astype(BF16)

    second = pl.BlockSpec((None, tr, c), lambda i, device_ref: (1, i, 0))
    return pl.pallas_call(
        body, name="w_in_grad_relay_sum",
        grid_spec=pltpu.PrefetchScalarGridSpec(
            num_scalar_prefetch=1, grid=(r // tr,),
            in_specs=[pl.BlockSpec((None, tr, c), lambda i, device_ref: (device_ref[0], i, 0)), second, second],
            out_specs=pl.BlockSpec((tr, c), lambda i, device_ref: (i, 0))),
        out_shape=jax.ShapeDtypeStruct((r, c), BF16),
        compiler_params=_params(dimension_semantics=("arbitrary",)),
    )(device, blocks, land_pair, land_first)


def _adam_from_chips(chip, pair, landed, w, m, v, name, tc):
    _, r, c = pair.shape
    n = len(landed)

    def body(chip_ref, own_ref, *refs):
        w_ref, m_ref, v_ref, g_ref, d_ref, nm_ref, nv_ref = refs[n:]
        g = own_ref[...].astype(F32)
        for k in range(n):
            g = g + refs[k][...].astype(F32)
        g_ref[...] = g
        d_ref[...], nm_ref[...], nv_ref[...] = _adamw(w_ref[...], g, m_ref[...], v_ref[...])

    def landed_spec(index):
        return pl.BlockSpec((None, r, tc), lambda i, chip_ref: (index, 0, i))

    blk = pl.BlockSpec((r, tc), lambda i, chip_ref: (0, i))
    return pl.pallas_call(
        body, name=name,
        grid_spec=pltpu.PrefetchScalarGridSpec(
            num_scalar_prefetch=1, grid=(c // tc,),
            in_specs=[pl.BlockSpec((None, r, tc), lambda i, chip_ref: (chip_ref[0], 0, i))]
            + [landed_spec(index) for _, index in landed] + [blk, blk, blk],
            out_specs=(blk,) * 4),
        out_shape=(jax.ShapeDtypeStruct((r, c), F32),) * 4,
        compiler_params=_params(dimension_semantics=("arbitrary",)),
    )(chip, pair, *[array for array, _ in landed], w, m, v)


def _adam_w_ada(device, act_t, dmod_all, w, m, v, tr=512):
    r, c = w.shape

    def body(device_ref, a_ref, dm_ref, w_ref, m_ref, v_ref, g_ref, d_ref, nm_ref, nv_ref):
        g = _dot(a_ref[...].astype(BF16), dm_ref[...].astype(BF16))
        g_ref[...] = g
        d_ref[...], nm_ref[...], nv_ref[...] = _adamw(w_ref[...], g, m_ref[...], v_ref[...])

    blk = pl.BlockSpec((tr, c), lambda i, device_ref: (i, 0))
    return pl.pallas_call(
        body, name="adam_w_ada",
        grid_spec=pltpu.PrefetchScalarGridSpec(
            num_scalar_prefetch=1, grid=(r // tr,),
            in_specs=[pl.BlockSpec((tr, N_DEV), lambda i, device_ref: (i, 0)),
                      pl.BlockSpec((N_DEV, c), lambda i, device_ref: (0, device_ref[0])), blk, blk, blk],
            out_specs=(blk,) * 4),
        out_shape=(jax.ShapeDtypeStruct((r, c), F32),) * 4,
        compiler_params=_params(dimension_semantics=("arbitrary",)),
    )(device, act_t, dmod_all, w, m, v)


def _pack_small(d_shift, d_scale, d_gate, d_norm_g, d_final_g, d_ln_g, d_ln_b, loss, d_sinks, d_sgu_b):
    def body(shift_ref, scale_ref, gate_ref, ng_ref, fg_ref, lng_ref, lnb_ref, loss_ref, sink_ref, b_ref, o_ref):
        o_ref[...] = jnp.zeros_like(o_ref)
        o_ref[ROW_SHIFT:ROW_SHIFT + 1, :] = shift_ref[...]
        o_ref[ROW_SCALE:ROW_SCALE + 1, :] = scale_ref[...]
        o_ref[ROW_GATE:ROW_GATE + 1, :] = gate_ref[...]
        o_ref[ROW_NORM_G:ROW_NORM_G + 1, :] = ng_ref[...]
        o_ref[ROW_FINAL_G:ROW_FINAL_G + 1, :] = fg_ref[...]
        o_ref[ROW_LN:ROW_LN + 1, 0:D_SGU] = lng_ref[...]
        o_ref[ROW_LN:ROW_LN + 1, D_SGU:2 * D_SGU] = lnb_ref[...]
        o_ref[ROW_MISC:ROW_MISC + 1, 0:128] = loss_ref[...]
        o_ref[ROW_MISC:ROW_MISC + 1, 128:256] = sink_ref[...]
        o_ref[ROW_SGU_B:ROW_SGU_B + SGU_GROUPS, 0:BLOCK] = b_ref[...]

    return pl.pallas_call(
        body, name="pack_small", out_shape=jax.ShapeDtypeStruct((SMALL_ROWS, D_MODEL), F32),
        compiler_params=_params(),
    )(d_shift, d_scale, d_gate, d_norm_g, d_final_g, d_ln_g, d_ln_b, loss, d_sinks, d_sgu_b)


_SMALL_NAMES = ("norm_g", "b_ada", "attn_sinks", "sgu_ln_g", "sgu_ln_b", "sgu_w", "sgu_b", "final_g")


def _adam_small(partials, d_sgu_w_all, weights, moments_m, moments_v):
    names = _SMALL_NAMES
    k = len(names)

    def body(*refs):
        p_ref, sw_ref = refs[0], refs[1]
        w_refs, m_refs, v_refs = refs[2:2 + k], refs[2 + k:2 + 2 * k], refs[2 + 2 * k:2 + 3 * k]
        loss_ref, dmod_ref = refs[2 + 3 * k], refs[3 + 3 * k]
        out_refs = refs[4 + 3 * k:4 + 7 * k]
        sum_ref = refs[4 + 7 * k]
        total = p_ref[0]
        for j in range(1, N_DEV):
            total = total + p_ref[j]
        sum_ref[...] = total
        for j in range(N_DEV):
            for part, row in enumerate((ROW_SHIFT, ROW_SCALE, ROW_GATE)):
                dmod_ref[j:j + 1, part * D_MODEL:(part + 1) * D_MODEL] = p_ref[j, row:row + 1, :]
        loss_ref[...] = sum_ref[ROW_MISC:ROW_MISC + 1, 0:1]
        d_sgu_w = sw_ref[0]
        for j in range(1, N_DEV):
            d_sgu_w = d_sgu_w + sw_ref[j]
        grads = {
            "norm_g": sum_ref[ROW_NORM_G:ROW_NORM_G + 1, :],
            "b_ada": jnp.concatenate([sum_ref[r:r + 1, :] for r in (ROW_SHIFT, ROW_SCALE, ROW_GATE)], axis=1),
            "attn_sinks": sum_ref[ROW_MISC:ROW_MISC + 1, 128:128 + N_Q_HEADS],
            "sgu_ln_g": sum_ref[ROW_LN:ROW_LN + 1, 0:D_SGU],
            "sgu_ln_b": sum_ref[ROW_LN:ROW_LN + 1, D_SGU:2 * D_SGU],
            "sgu_w": d_sgu_w[None],
            "sgu_b": sum_ref[ROW_SGU_B:ROW_SGU_B + SGU_GROUPS, 0:BLOCK][None],
            "final_g": sum_ref[ROW_FINAL_G:ROW_FINAL_G + 1, :],
        }
        for i, name in enumerate(names):
            g = grads[name]
            delta, m, v = _adamw(w_refs[i][...], g, m_refs[i][...], v_refs[i][...])
            out_refs[4 * i][...] = g
            out_refs[4 * i + 1][...] = delta
            out_refs[4 * i + 2][...] = m
            out_refs[4 * i + 3][...] = v

    shapes = [jax.ShapeDtypeStruct((1, 1), F32), jax.ShapeDtypeStruct((N_DEV, 3 * D_MODEL), F32)]
    for name in names:
        shapes += [jax.ShapeDtypeStruct(weights[name].shape, F32)] * 4
    outs = pl.pallas_call(
        body, name="adam_small", out_shape=tuple(shapes),
        scratch_shapes=[pltpu.VMEM((SMALL_ROWS, D_MODEL), F32)],
        compiler_params=_params(),
    )(partials, d_sgu_w_all, *[weights[n] for n in names], *[moments_m[n] for n in names],
      *[moments_v[n] for n in names])
    return outs[0], outs[1], {name: outs[2 + 4 * i:6 + 4 * i] for i, name in enumerate(names)}


def kernel(x, c, norm_g, w_ada, b_ada, w_in, attn_sinks, sgu_ln_g, sgu_ln_b, sgu_w, sgu_b, w_out, final_g, loss_target, m_norm_g, m_w_ada, m_b_ada, m_w_in, m_attn_sinks, m_sgu_ln_g, m_sgu_ln_b, m_sgu_w, m_sgu_b, m_w_out, m_final_g, v_norm_g, v_w_ada, v_b_ada, v_w_in, v_attn_sinks, v_sgu_ln_g, v_sgu_ln_b, v_sgu_w, v_sgu_b, v_w_out, v_final_g):
    xi, yi, ci = _place()
    me = 4 * xi + 2 * yi + ci
    x2d, target = x[0], loss_target[0]

    core = ci.astype(jnp.int32).reshape(1)
    chip = (2 * xi + yi).astype(jnp.int32).reshape(1)

    first = _own_block_copies(_first_targets)
    first_flight = _start_copies([_with_own_slot(w_in[0].T.astype(BF16), me)], first, 2, core, "gather_w_in_start")

    c_all = _all_gather_small(c.reshape(8, 256), "gather_c", first_flight[3]).reshape(N_DEV, D_MODEL)
    device = me.astype(jnp.int32).reshape(1)
    c_act, mod_part = _modulation(device, c_all, w_ada[0], b_ada)
    mod_all = _all_gather_small(mod_part, "gather_mod")

    across = _wait_then_start(first_flight, lambda *a: first(*a)[1:], _second_axis_stage_copies, 3, mod_all,
                              "gather_w_in_second_axis_stage")
    mod = lax.dynamic_index_in_dim(mod_all, me, axis=1, keepdims=False).reshape(1, 3 * D_MODEL)
    mod = mod + across[3][0, 0]

    w_in_pair = _wait_copies((first_flight[0], first_flight[1], across[2], None), lambda *a: first(*a)[:1], mod,
                             "gather_w_in_sibling_wait")
    h, z_own = _norm_z_proj_own(x2d, norm_g, mod, w_in_pair[0].reshape(D_IN, D_MODEL), chip)
    w_out_early = _own_block_copies(lambda x, y, c: [(x, y, 1 - c), (*_second_axis_chip(x, y, c), c)])
    w_out_late = _own_block_copies(lambda x, y, c: [(*_first_axis_chip(x, y, c), c), (1 - x, 1 - y, c)])
    forward = _wait_then_start(
        (across[0], across[1], w_in_pair, None), lambda *a: _second_axis_stage_copies(*a)[:1],
        lambda refs, s, r: _second_axis_forward_copies(refs[:1], s, r) + _group(w_out_early, 1, 1, 1)(refs, s, r),
        3, z_own, "gather_w_in_second_axis_forward", more_bufs=[_with_own_slot(w_out[0].astype(BF16), me)])
    w_in_most = _wait_copies((across[0], across[1], forward[2][:1], None),
                             lambda *a: _second_axis_stage_copies(*a)[1:2], z_own, "gather_w_in_first_forward_wait")
    w_in_most = _wait_copies((forward[0], forward[1], w_in_most, None), _second_axis_forward_copies, z_own,
                             "gather_w_in_second_forward_wait")
    z_early = _z_proj(h, w_in_most[0].reshape(D_IN, D_MODEL), chip, 1, _Z_EARLY_TILES - 1, z_own, "z_proj_early")
    last = _wait_then_start(
        (across[0], across[1], [w_in_most[0], forward[2][1]], None), lambda *a: _second_axis_stage_copies(*a)[2:],
        lambda refs, s, r: _diagonal_forward_copies(refs[:1], s, r) + _group(w_out_late, 1, 1, 1)(refs, s, r),
        3, z_early, "gather_w_in_last_stage")
    w_in_all = _wait_copies((last[0], last[1], last[2][:1], None), _diagonal_forward_copies, z_early,
                            "gather_w_in_last_wait")[0]
    w_in_t = w_in_all.reshape(D_IN, D_MODEL)
    z = _z_proj(h, w_in_t, chip, _Z_EARLY_TILES, 7 - _Z_EARLY_TILES, z_early, "z_proj_late")
    w_out_half = _wait_copies((forward[0], forward[1], last[2][1:], None), _group(w_out_early, 0, 1, 1), z,
                              "gather_w_out_early_wait")
    w_out_flight = _wait_then_start((last[0], last[1], w_out_half, None), _group(w_out_late, 0, 1, 1),
                                    _forward_copies, 3, z, "gather_w_out_forward_stage")
    sink_rows = jnp.repeat(attn_sinks.reshape(N_Q_HEADS), BLOCK).reshape(2, 1, 8 * BLOCK)
    sgu_bt = sgu_b[0].T
    a, probs, sink_probs = _mixer_fwd(z, sink_rows + w_out_flight[3][0, 0], sgu_ln_g, sgu_ln_b, sgu_w[0], sgu_bt)
    w_out_all = _wait_copies(w_out_flight, _forward_copies, a, "gather_w_out_forward_wait")[0]
    w_out_full = w_out_all.reshape(D_MODEL, D_MODEL)
    final_g_row = final_g.reshape(1, D_MODEL)
    dx2, dy, da, loss_part, d_final_g, d_gate = _out_proj_head(a, w_out_full, x2d, target, mod, final_g_row)

    dz, d_sinks, d_sgu_w, d_sgu_b, d_ln_g, d_ln_b, dw_out = _mixer_bwd(
        z, da, probs, sink_probs, sgu_ln_g, sgu_ln_b, sgu_w[0], jnp.swapaxes(sgu_w[0], 1, 2), sgu_bt, a, dy)
    pair_out = _pair_reduce(dw_out.reshape(4, 2, W_OUT_SHARD, D_MODEL), _every_chip, "w_out_grad_pair_reduce",
                            W_OUT_SHARD // 2)
    sgu_w_to_all = _group(_own_block_copies(_all_others), 2, 1, 3)
    both = _start_copies(
        [pair_out, lax.empty((3, W_OUT_SHARD, D_MODEL), BF16), _with_own_slot(d_sgu_w, me)],
        lambda refs, s, r: _chip_copies(refs[:2], s, r) + sgu_w_to_all(refs, s, r), 3 + N_DEV - 1, core,
        "w_out_grad_chip_and_sgu_w_gather_start")
    out_flight, sgu_w_flight = (both[0], both[1], both[2][:2], None), (both[0], both[1], both[2][2:], None)
    dw_in_t = _w_in_grad(dz, h, both[3], Z_TILE)
    pair_in = _pair_reduce(dw_in_t.reshape(4, 2, W_IN_SHARD, D_MODEL), _first_hop_chips, "w_in_grad_pair_reduce",
                           W_IN_SHARD // 3)
    first_hop = lambda refs, s, r: _first_hop_copies(refs[:2], s, r) + _group(_late_pair_copies, 2, 2, 2)(refs, s, r)
    hop1 = _start_copies(
        [pair_in, lax.empty((2, W_IN_SHARD, D_MODEL), BF16), dw_in_t.reshape(N_DEV, W_IN_SHARD, D_MODEL),
         lax.empty((2, W_IN_SHARD, D_MODEL), BF16)], first_hop, 4, core, "w_in_grad_first_hop_start")
    grad_x, d_shift, d_scale, d_norm_g = _z_proj_bwd_norm(dz, w_in_t, x2d, dx2, norm_g, mod, hop1[3])

    partial = _pack_small(d_shift, d_scale, d_gate, d_norm_g, d_final_g, d_ln_g, d_ln_b, loss_part, d_sinks, d_sgu_b)
    small_flight = _start_copies([_with_own_slot(partial, me)], _own_block_copies(_all_others), N_DEV - 1, core,
                                 "small_grad_gather_start")
    _, land_first, dw_in_t, land_pair = _wait_copies(hop1, first_hop, small_flight[3], "w_in_grad_first_hop_wait")
    second_device = (4 * ((xi + ci) % 2) + 2 * ((yi + 1 - ci) % 2) + ci).astype(jnp.int32).reshape(1)
    relay = _relay_sum(second_device, dw_in_t, land_pair, land_first, W_IN_SHARD // 3)
    hop2 = _start_copies([relay, lax.empty((1, W_IN_SHARD, D_MODEL), BF16)], _second_hop_copies, 1, core,
                         "w_in_grad_second_hop_start")
    pair_out, land_out = _wait_copies(out_flight, _chip_copies, hop2[3], "w_out_grad_chip_wait")
    big = {"w_out": _adam_from_chips(chip, pair_out, [(land_out, k) for k in range(3)], w_out[0], m_w_out[0],
                                     v_w_out[0], "adam_w_out", 1024)}
    partial_all = _wait_copies(small_flight, _own_block_copies(_all_others), big["w_out"][0],
                               "small_grad_gather_wait")[0]
    d_sgu_w_all = _wait_copies(sgu_w_flight, _group(_own_block_copies(_all_others), 0, 1, 3), partial_all,
                               "sgu_w_grad_gather_wait")[0]
    weights = {"norm_g": norm_g, "b_ada": b_ada, "attn_sinks": attn_sinks, "sgu_ln_g": sgu_ln_g,
               "sgu_ln_b": sgu_ln_b, "sgu_w": sgu_w, "sgu_b": sgu_b, "final_g": final_g_row}
    moments_m = {"norm_g": m_norm_g, "b_ada": m_b_ada, "attn_sinks": m_attn_sinks, "sgu_ln_g": m_sgu_ln_g,
                 "sgu_ln_b": m_sgu_ln_b, "sgu_w": m_sgu_w, "sgu_b": m_sgu_b,
                 "final_g": m_final_g.reshape(1, D_MODEL)}
    moments_v = {"norm_g": v_norm_g, "b_ada": v_b_ada, "attn_sinks": v_attn_sinks, "sgu_ln_g": v_sgu_ln_g,
                 "sgu_ln_b": v_sgu_ln_b, "sgu_w": v_sgu_w, "sgu_b": v_sgu_b,
                 "final_g": v_final_g.reshape(1, D_MODEL)}
    loss, dmod_all, small = _adam_small(partial_all, d_sgu_w_all, weights, moments_m, moments_v)
    small["final_g"] = tuple(o.reshape(D_MODEL) for o in small["final_g"])

    big["w_ada"] = _adam_w_ada(device, c_act.T, dmod_all, w_ada[0], m_w_ada[0], v_w_ada[0])
    _, land_second = _wait_copies(hop2, _second_hop_copies, big["w_ada"][0], "w_in_grad_second_hop_wait")
    big["w_in"] = tuple(o.T for o in _adam_from_chips(
        device, dw_in_t, [(land_pair, 0), (land_first, 0), (land_second, 0)], w_in[0].T, m_w_in[0].T, v_w_in[0].T,
        "adam_w_in", 512))
    order = ["norm_g", "w_ada", "b_ada", "w_in", "attn_sinks", "sgu_ln_g", "sgu_ln_b", "sgu_w", "sgu_b", "w_out",
             "final_g"]
    outs = [loss.reshape(()), grad_x[None]]
    for k in range(4):
        for name in order:
            outs.append(big[name][k][None] if name in big else small[name][k])
    return tuple(outs)
```

```python
import jax
import jax.numpy as jnp
from jax import lax
from jax.experimental import pallas as pl
from jax.experimental.pallas import tpu as pltpu

F32 = jnp.float32
BF16 = jnp.bfloat16
MESH = pl.DeviceIdType.MESH

N_DEV = 8
D_MODEL = 2048
HEAD_DIM = 64
D_ATTN = 1024
N_Q_HEADS = 16
D_KV = 128
BLOCK = 128
D_SGU = 1024
SGU_GROUPS = 8
D_IN = 5376
W_IN_SHARD = D_IN // N_DEV
W_OUT_SHARD = D_MODEL // N_DEV
W_ADA_SHARD = 3 * D_MODEL // N_DEV
EPS = 1e-6
ATTN_SCALE = 0.125

ADAM_LR = 0.001
ADAM_B1 = 0.9
ADAM_B2 = 0.999
ADAM_EPS = 1e-08
ADAM_WD = 0.01
ADAM_STEP = 10

SEG_Q, SEG_KV, SEG_GA, SEG_U, SEG_VS, SEG_GS = 0, 1024, 1280, 2304, 3328, 4352

VMEM_LIMIT = 56 * 1024 * 1024

ROW_SHIFT, ROW_SCALE, ROW_GATE, ROW_NORM_G, ROW_FINAL_G, ROW_LN, ROW_MISC, ROW_SGU_B = 0, 1, 2, 3, 4, 5, 6, 8
SMALL_ROWS = 16


def _params(**kw):
    return pltpu.CompilerParams(vmem_limit_bytes=VMEM_LIMIT, **kw)


def _sigmoid(x):
    return 0.5 * (jnp.tanh(0.5 * x) + 1.0)


def _place():
    return lax.axis_index("x"), lax.axis_index("y"), lax.axis_index("c")


def _every_chip(x, y, c):
    return [0, 1, 2, 3]


def _first_hop_chips(x, y, c):
    first = _first_axis_chip(x, y, c)
    return [2 * first[0] + first[1], 2 * (1 - x) + (1 - y)]


def _pair_reduce(blocks, chips, name, row_chunk):
    _, _, r, cols = blocks.shape
    n = len(chips(0, 0, 0))
    assert r % row_chunk == 0

    def body(in_ref, out_ref, land, own, summed, send_sems, recv_sems, own_sems, out_sems):
        x, y, c = _place()
        sends, loads, stores = [], [], []
        for m in range(n):
            cp = pltpu.make_async_remote_copy(
                src_ref=in_ref.at[chips(x, y, 1 - c)[m], 1 - c], dst_ref=land.at[m], send_sem=send_sems.at[m],
                recv_sem=recv_sems.at[m], device_id=(x, y, 1 - c), device_id_type=MESH)
            cp.start()
            sends.append(cp)
            ld = pltpu.make_async_copy(in_ref.at[chips(x, y, c)[m], c], own.at[m], own_sems.at[m])
            ld.start()
            loads.append(ld)
        for m in range(n):
            sends[m].wait_recv()
            loads[m].wait()
            for k in range(r // row_chunk):
                rows = slice(k * row_chunk, (k + 1) * row_chunk)
                summed[m, rows, :] = (own[m, rows, :].astype(F32) + land[m, rows, :].astype(F32)).astype(BF16)
            st = pltpu.make_async_copy(summed.at[m], out_ref.at[m], out_sems.at[m])
            st.start()
            stores.append(st)
        for m in range(n):
            sends[m].wait_send()
            stores[m].wait()

    spec = pl.BlockSpec(memory_space=pl.ANY)
    return pl.pallas_call(
        body, name=name, out_shape=jax.ShapeDtypeStruct((n, r, cols), BF16),
        in_specs=[spec], out_specs=spec,
        scratch_shapes=[pltpu.VMEM((n, r, cols), BF16), pltpu.VMEM((n, r, cols), BF16), pltpu.VMEM((n, r, cols), BF16),
                        pltpu.SemaphoreType.DMA((n,)), pltpu.SemaphoreType.DMA((n,)), pltpu.SemaphoreType.DMA((n,)),
                        pltpu.SemaphoreType.DMA((n,))],
        compiler_params=_params(),
    )(blocks)


_HBM = pl.BlockSpec(memory_space=pltpu.HBM)
_SEM = pl.BlockSpec(memory_space=pltpu.SEMAPHORE)
_EFFECT = pltpu.SideEffectType.DATAFLOW_SIDE_EFFECTING


def _start_copies(bufs, copies, n_copies, after, name):
    nb = len(bufs)

    def body(*refs):
        for cp in copies(refs[:nb], refs[nb + 1], refs[nb + 2]):
            cp.start()
        refs[-1][...] = jnp.zeros_like(refs[-1])

    out = pl.pallas_call(
        body, name=name,
        out_shape=(pltpu.SemaphoreType.DMA((n_copies,)), pltpu.SemaphoreType.DMA((n_copies,)),
                   *[pltpu.HBM(b.shape, b.dtype) for b in bufs], jax.ShapeDtypeStruct((8, 128), F32)),
        in_specs=(_HBM,) * nb + (pl.BlockSpec(memory_space=pl.ANY),),
        out_specs=(_SEM, _SEM) + (_HBM,) * nb + (pl.BlockSpec(memory_space=pltpu.VMEM),),
        input_output_aliases={i: 2 + i for i in range(nb)},
        compiler_params=pltpu.CompilerParams(has_side_effects=_EFFECT),
    )(*[pltpu.with_memory_space_constraint(b, pltpu.HBM) for b in bufs], after)
    return out[0], out[1], list(out[2:2 + nb]), out[-1]


def _wait_copies(flight, copies, after, name):
    send_sems, recv_sems, bufs, _ = flight
    nb = len(bufs)

    def body(*refs):
        for cp in copies(refs[:nb], refs[nb], refs[nb + 1]):
            cp.wait_send()
            cp.wait_recv()

    return pl.pallas_call(
        body, name=name,
        out_shape=tuple(pltpu.HBM(b.shape, b.dtype) for b in bufs),
        in_specs=(_HBM,) * nb + (_SEM, _SEM, pl.BlockSpec(memory_space=pl.ANY)), out_specs=(_HBM,) * nb,
        input_output_aliases={i: i for i in range(nb)},
        compiler_params=pltpu.CompilerParams(has_side_effects=_EFFECT),
    )(*bufs, send_sems, recv_sems, after)


class _From:
    def __init__(self, sems, offset):
        self.sems, self.offset = sems, offset

    @property
    def at(self):
        return self

    def __getitem__(self, k):
        return self.sems.at[k + self.offset]


def _group(copies, first_buf, n_bufs, offset):
    def grouped(refs, send_sems, recv_sems):
        return copies(refs[first_buf:first_buf + n_bufs], _From(send_sems, offset), _From(recv_sems, offset))
    return grouped


def _wait_then_start(flight, waited, started, n_started, after, name, more_bufs=()):
    old_send, old_recv, bufs, _ = flight
    bufs = list(bufs) + [pltpu.with_memory_space_constraint(b, pltpu.HBM) for b in more_bufs]
    nb = len(bufs)

    def body(*refs):
        for cp in waited(refs[:nb], refs[nb], refs[nb + 1]):
            cp.wait_send()
            cp.wait_recv()
        for cp in started(refs[:nb], refs[nb + 3], refs[nb + 4]):
            cp.start()
        refs[-1][...] = jnp.zeros_like(refs[-1])

    out = pl.pallas_call(
        body, name=name,
        out_shape=(pltpu.SemaphoreType.DMA((n_started,)), pltpu.SemaphoreType.DMA((n_started,)),
                   *[pltpu.HBM(b.shape, b.dtype) for b in bufs], jax.ShapeDtypeStruct((8, 128), F32)),
        in_specs=(_HBM,) * nb + (_SEM, _SEM, pl.BlockSpec(memory_space=pl.ANY)),
        out_specs=(_SEM, _SEM) + (_HBM,) * nb + (pl.BlockSpec(memory_space=pltpu.VMEM),),
        input_output_aliases={i: 2 + i for i in range(nb)},
        compiler_params=pltpu.CompilerParams(has_side_effects=_EFFECT),
    )(*bufs, old_send, old_recv, after)
    return out[0], out[1], list(out[2:2 + nb]), out[-1]


def _late_pair_copies(refs, send_sems, recv_sems):
    blocks_ref, land_ref = refs
    x, y, c = _place()
    first = _first_axis_chip(x, y, c)
    devices = [4 * x + 2 * y + 1 - c, 4 * first[0] + 2 * first[1] + 1 - c]
    return [pltpu.make_async_remote_copy(
        src_ref=blocks_ref.at[devices[k]], dst_ref=land_ref.at[k], send_sem=send_sems.at[k], recv_sem=recv_sems.at[k],
        device_id=(x, y, 1 - c), device_id_type=MESH) for k in range(2)]


def _chip_copies(refs, send_sems, recv_sems):
    pair_ref, land_ref = refs
    x, y, c = _place()
    chips = [(1 - x, y), (x, 1 - y), (1 - x, 1 - y)]
    return [pltpu.make_async_remote_copy(
        src_ref=pair_ref.at[2 * chip[0] + chip[1]], dst_ref=land_ref.at[k],
        send_sem=send_sems.at[k], recv_sem=recv_sems.at[k],
        device_id=(*chip, c), device_id_type=MESH) for k, chip in enumerate(chips)]


def _first_hop_copies(refs, send_sems, recv_sems):
    pair_ref, land_ref = refs
    x, y, c = _place()
    return [pltpu.make_async_remote_copy(
        src_ref=pair_ref.at[k], dst_ref=land_ref.at[k], send_sem=send_sems.at[k], recv_sem=recv_sems.at[k],
        device_id=(*_first_axis_chip(x, y, c), c), device_id_type=MESH) for k in range(2)]


def _second_hop_copies(refs, send_sems, recv_sems):
    relay_ref, land_ref = refs
    x, y, c = _place()
    second = ((x + c) % 2, (y + 1 - c) % 2)
    return [pltpu.make_async_remote_copy(
        src_ref=relay_ref, dst_ref=land_ref.at[0], send_sem=send_sems.at[0], recv_sem=recv_sems.at[0],
        device_id=(*second, c), device_id_type=MESH)]


def _own_block_copies(targets):
    def copies(refs, send_sems, recv_sems):
        x, y, c = _place()
        mine = refs[0].at[4 * x + 2 * y + c]
        return [pltpu.make_async_remote_copy(
            src_ref=mine, dst_ref=mine, send_sem=send_sems.at[k], recv_sem=recv_sems.at[k],
            device_id=to, device_id_type=MESH) for k, to in enumerate(targets(x, y, c))]
    return copies


def _all_others(x, y, c):
    flip = lambda v, f: 1 - v if f else v
    return [(flip(x, r & 4), flip(y, r & 2), flip(c, r & 1)) for r in range(1, N_DEV)]


def _forward_copies(refs, send_sems, recv_sems):
    x, y, c = _place()
    chips = [(1 - x, y), (x, 1 - y), (1 - x, 1 - y)]
    return [pltpu.make_async_remote_copy(
        src_ref=refs[0].at[4 * chip[0] + 2 * chip[1] + c], dst_ref=refs[0].at[4 * chip[0] + 2 * chip[1] + c],
        send_sem=send_sems.at[k], recv_sem=recv_sems.at[k],
        device_id=(x, y, 1 - c), device_id_type=MESH) for k, chip in enumerate(chips)]


def _first_axis_chip(x, y, c):
    return (x + 1 - c) % 2, (y + c) % 2


def _second_axis_chip(x, y, c):
    return (x + c) % 2, (y + 1 - c) % 2


def _first_targets(x, y, c):
    return [(x, y, 1 - c), (*_first_axis_chip(x, y, c), c)]


def _all_gather_small(shard, name, dep=None):
    def body(in_ref, *refs):
        out_ref, send_sems, recv_sems, local_sem = refs[-4:]
        x, y, c = _place()
        me, sibling = 4 * x + 2 * y + c, (x, y, 1 - c)
        first, second = _first_axis_chip(x, y, c), _second_axis_chip(x, y, c)

        def pair(chip):
            return out_ref.at[pl.ds(2 * (2 * chip[0] + chip[1]), 2)]

        def exchange(k, src, dst, to):
            cp = pltpu.make_async_remote_copy(src_ref=src, dst_ref=dst, send_sem=send_sems.at[k],
                                              recv_sem=recv_sems.at[k], device_id=to, device_id_type=MESH)
            cp.start()
            cp.wait()

        own = pltpu.make_async_copy(in_ref, out_ref.at[me], local_sem)
        own.start()
        exchange(0, in_ref, out_ref.at[me], sibling)
        own.wait()
        exchange(1, pair((x, y)), pair((x, y)), (*second, c))
        exchange(2, pair(second), pair(second), sibling)
        exchange(3, pair(first), pair(first), (*second, c))

    spec = pl.BlockSpec(memory_space=pltpu.VMEM)
    deps = [] if dep is None else [dep]
    return pl.pallas_call(
        body, name=name, out_shape=jax.ShapeDtypeStruct((N_DEV,) + shard.shape, shard.dtype),
        in_specs=[spec] * (1 + len(deps)), out_specs=spec,
        scratch_shapes=[pltpu.SemaphoreType.DMA((4,)), pltpu.SemaphoreType.DMA((4,)), pltpu.SemaphoreType.DMA],
        compiler_params=_params(),
    )(shard, *deps)


def _slot_copies(refs, send_sems, recv_sems, plan):
    copies = []
    for k, ((px, py, pc), to) in enumerate(plan):
        blk = refs[0].at[4 * px + 2 * py + pc]
        copies.append(pltpu.make_async_remote_copy(
            src_ref=blk, dst_ref=blk, send_sem=send_sems.at[k], recv_sem=recv_sems.at[k],
            device_id=to, device_id_type=MESH))
    return copies


def _second_axis_stage_copies(refs, send_sems, recv_sems):
    x, y, c = _place()
    first, second = (*_first_axis_chip(x, y, c), c), (*_second_axis_chip(x, y, c), c)
    return _slot_copies(refs, send_sems, recv_sems, [((x, y, c), second), (first, (x, y, 1 - c)), (first, second)])


def _second_axis_forward_copies(refs, send_sems, recv_sems):
    x, y, c = _place()
    return _slot_copies(refs, send_sems, recv_sems, [((*_second_axis_chip(x, y, c), c), (x, y, 1 - c))])


def _diagonal_forward_copies(refs, send_sems, recv_sems):
    x, y, c = _place()
    blk = refs[0].at[4 * (1 - x) + 2 * (1 - y) + c]
    return [pltpu.make_async_remote_copy(
        src_ref=blk, dst_ref=blk, send_sem=send_sems.at[0], recv_sem=recv_sems.at[0],
        device_id=(x, y, 1 - c), device_id_type=MESH)]


def _with_own_slot(block, me):
    return lax.dynamic_update_index_in_dim(lax.empty((N_DEV,) + block.shape, block.dtype), block, me, 0)


def _w_in_grad(dz, h, dep, tm):
    (t, m), n = dz.shape, h.shape[1]
    assert m % tm == 0 and dz.dtype == BF16 and h.dtype == BF16

    def body(dz_ref, h_ref, dep_ref, o_ref):
        o_ref[...] = _dot_tn(dz_ref[...], h_ref[...]).astype(BF16)

    return pl.pallas_call(
        body, name="w_in_grad", grid=(m // tm,),
        in_specs=[pl.BlockSpec((t, tm), lambda i: (0, i)), pl.BlockSpec((t, n), lambda i: (0, 0)),
                  pl.BlockSpec((8, 128), lambda i: (0, 0))],
        out_specs=pl.BlockSpec((tm, n), lambda i: (i, 0)),
        out_shape=jax.ShapeDtypeStruct((m, n), BF16),
        compiler_params=_params(dimension_semantics=("arbitrary",)),
    )(dz, h, dep)


Z_TILE = 768
_Z_TILE_ORDER = ((0, 1, 2, 3, 4, 5, 6), (2, 0, 1, 6, 3, 4, 5), (4, 0, 5, 6, 1, 2, 3), (6, 2, 3, 4, 0, 1, 5))
_Z_EARLY_TILES = 4


def _z_proj(h, w_in_t, chip, first, count, z_prev, name, tr=1024):
    t = h.shape[0]

    def body(chip_ref, h_ref, w_ref, z_prev_ref, z_ref):
        z_ref[...] = _dot_nt(h_ref[...], w_ref[...])

    def tile(j, chip_ref):
        picked = 0
        for c, order in enumerate(_Z_TILE_ORDER):
            for k in range(count):
                picked = picked + jnp.where((chip_ref[0] == c) & (j == k), order[first + k], 0)
        return picked

    return pl.pallas_call(
        body, name=name,
        grid_spec=pltpu.PrefetchScalarGridSpec(
            num_scalar_prefetch=1, grid=(count, t // tr),
            in_specs=[pl.BlockSpec((tr, D_MODEL), lambda j, i, o: (i, 0)),
                      pl.BlockSpec((Z_TILE, D_MODEL), lambda j, i, o: (tile(j, o), 0)),
                      pl.BlockSpec(memory_space=pl.ANY)],
            out_specs=pl.BlockSpec((tr, Z_TILE), lambda j, i, o: (i, tile(j, o)))),
        out_shape=jax.ShapeDtypeStruct((t, D_IN), F32),
        input_output_aliases={3: 0},
        compiler_params=_params(dimension_semantics=("arbitrary", "arbitrary")),
    )(chip, h, w_in_t, z_prev)


def _modulation(device, c_all, w_ada, b_ada):
    def body(device_ref, c_ref, w_ref, b_ref, act_ref, mod_ref):
        cv = c_ref[...]
        act = cv * _sigmoid(cv)
        act_ref[...] = act
        mod_ref[...] = jnp.dot(act.astype(BF16), w_ref[...].astype(BF16), preferred_element_type=F32) + b_ref[...]

    whole = lambda a: pl.BlockSpec(a.shape, lambda i, device_ref: (0,) * a.ndim)
    return pl.pallas_call(
        body, name="modulation",
        grid_spec=pltpu.PrefetchScalarGridSpec(
            num_scalar_prefetch=1, grid=(1,),
            in_specs=[whole(c_all), whole(w_ada), pl.BlockSpec((1, W_ADA_SHARD), lambda i, device_ref: (0, device_ref[0]))],
            out_specs=(whole(c_all), pl.BlockSpec((N_DEV, W_ADA_SHARD), lambda i, device_ref: (0, 0)))),
        out_shape=(jax.ShapeDtypeStruct(c_all.shape, F32), jax.ShapeDtypeStruct((N_DEV, W_ADA_SHARD), F32)),
        compiler_params=_params(dimension_semantics=("arbitrary",)),
    )(device, c_all, w_ada, b_ada)


MOD_SHIFT, MOD_SCALE, MOD_GATE = 0, 1, 2


def _mod_spec(part, d):
    return pl.BlockSpec((1, d), lambda i: (0, part))


def _norm_z_proj_own(x, norm_g, mod, w_in_t, chip, tm=512):
    t, d = x.shape

    def body(chip_ref, x_ref, g_ref, sc_ref, sh_ref, w_ref, h_ref, z_ref):
        xv = x_ref[...]
        r = lax.rsqrt(jnp.mean(xv * xv, axis=-1, keepdims=True) + EPS)
        h = ((xv * r) * g_ref[...] * (1.0 + sc_ref[...]) + sh_ref[...]).astype(BF16)
        h_ref[...] = h
        z_ref[...] = _dot_nt(h, w_ref[...])

    def own_tile(chip_ref):
        picked = 0
        for c, order in enumerate(_Z_TILE_ORDER):
            picked = picked + jnp.where(chip_ref[0] == c, order[0], 0)
        return picked

    def row(part):
        return pl.BlockSpec((1, d), lambda i, o: (0, part))

    return pl.pallas_call(
        body, name="norm_z_proj_own",
        grid_spec=pltpu.PrefetchScalarGridSpec(
            num_scalar_prefetch=1, grid=(t // tm,),
            in_specs=[pl.BlockSpec((tm, d), lambda i, o: (i, 0)), row(0), row(MOD_SCALE), row(MOD_SHIFT),
                      pl.BlockSpec((Z_TILE, d), lambda i, o: (own_tile(o), 0))],
            out_specs=(pl.BlockSpec((tm, d), lambda i, o: (i, 0)),
                       pl.BlockSpec((tm, Z_TILE), lambda i, o: (i, own_tile(o))))),
        out_shape=(jax.ShapeDtypeStruct((t, d), BF16), jax.ShapeDtypeStruct((t, D_IN), F32)),
        compiler_params=_params(dimension_semantics=("arbitrary",)),
    )(chip, x, norm_g, mod, mod, w_in_t)


def _window_bias(block_index):
    s = lax.broadcasted_iota(jnp.int32, (2 * BLOCK, BLOCK), 0)
    t = lax.broadcasted_iota(jnp.int32, (2 * BLOCK, BLOCK), 1)
    valid = ((s < BLOCK) & (s > t) & (block_index > 0)) | ((s >= BLOCK) & ((s - BLOCK) <= t))
    bias = jnp.where(valid, 0.0, -jnp.inf).astype(F32)
    return jnp.concatenate([bias] * 8, axis=1)


def _heads_t(pair_blocks, g):
    top = lax.broadcasted_iota(jnp.int32, (BLOCK, BLOCK), 0) < HEAD_DIM
    zeros = jnp.zeros((HEAD_DIM, BLOCK), F32)
    tiles = []
    for blk in pair_blocks:
        tp = blk.T
        if g == 0:
            tiles += [jnp.where(top, tp, 0.0), jnp.concatenate([tp[HEAD_DIM:], zeros], axis=0)]
        else:
            tiles += [jnp.concatenate([zeros, tp[:HEAD_DIM]], axis=0), jnp.where(top, 0.0, tp)]
    return jnp.concatenate(tiles, axis=1)


def _pair_block(xt, p, g):
    r0 = HEAD_DIM * g
    even = xt[r0:r0 + HEAD_DIM, (2 * p) * BLOCK:(2 * p + 1) * BLOCK]
    odd = xt[r0:r0 + HEAD_DIM, (2 * p + 1) * BLOCK:(2 * p + 2) * BLOCK]
    return jnp.concatenate([even, odd], axis=0).T


def _softmax_t(scores_t, bias, sink):
    st = scores_t + bias
    m = jnp.maximum(jnp.max(st, axis=0, keepdims=True), sink)
    e = jnp.exp(st - m)
    es = jnp.exp(sink - m)
    inv = 1.0 / (jnp.sum(e, axis=0, keepdims=True) + es)
    return e * inv, es * inv


def _dot(a, b):
    return jnp.dot(a, b, preferred_element_type=F32)


def _dot_nt(a, b):
    return lax.dot_general(a, b, (((1,), (1,)), ((), ())), preferred_element_type=F32)


def _dot_tn(a, b):
    return lax.dot_general(a, b, (((0,), (0,)), ((), ())), preferred_element_type=F32)


def _layer_norm_fwd(v):
    mu = jnp.mean(v, axis=-1, keepdims=True)
    xc = v - mu
    rstd = lax.rsqrt(jnp.mean(xc * xc, axis=-1, keepdims=True) + EPS)
    return xc * rstd, rstd


def _tril(transposed=False):
    t = lax.broadcasted_iota(jnp.int32, (BLOCK, BLOCK), 0)
    s = lax.broadcasted_iota(jnp.int32, (BLOCK, BLOCK), 1)
    return s >= t if transposed else t >= s


def _const_spec(shape):
    return pl.BlockSpec(shape, lambda i: (0,) * len(shape))


def _keys_values(z_ref, kvp):
    kvc = z_ref[:, SEG_KV:SEG_KV + 2 * D_KV]
    kk = jnp.concatenate([kvp[:, :D_KV], kvc[:, :D_KV]], axis=0)
    vv = jnp.concatenate([kvp[:, D_KV:], kvc[:, D_KV:]], axis=0)
    return kk, vv


MIXER_BLOCKS = 2


class _Rows:
    def __init__(self, ref, sub):
        self.ref, self.rows = ref, slice(sub * BLOCK, (sub + 1) * BLOCK)

    def __getitem__(self, idx):
        return self.ref[self.rows, idx[1]]

    def __setitem__(self, idx, value):
        self.ref[self.rows, idx[1]] = value


def _kv_before_spec(index):
    return pl.BlockSpec((BLOCK, 2 * D_KV),
                        lambda i: (jnp.maximum(MIXER_BLOCKS * index(i) - 1, 0), SEG_KV // (2 * D_KV)))


def _pair_cols(g, p, base=0):
    return slice(base + (4 * g + p) * 128, base + (4 * g + p + 1) * 128)


def _mixer_fwd(z, sink_rows, ln_g, ln_b, sgu_w, sgu_bt):
    t = z.shape[0]

    def body(z_all, kvp_ref, sink_ref, lng_ref, lnb_ref, w_ref, bt_ref, a_all, prob_ref, sink_prob_ref):
        kv_before = kvp_ref[...]
        for sub in range(MIXER_BLOCKS):
            z_ref, a_ref = _Rows(z_all, sub), _Rows(a_all, sub)
            one_block(z_ref, kv_before, MIXER_BLOCKS * pl.program_id(0) + sub, sink_ref, lng_ref, lnb_ref, w_ref,
                      bt_ref, a_ref, prob_ref.at[sub], sink_prob_ref.at[sub])
            kv_before = z_ref[:, SEG_KV:SEG_KV + 2 * D_KV]

    def one_block(z_ref, kv_before, block_index, sink_ref, lng_ref, lnb_ref, w_ref, bt_ref, a_ref, prob_ref,
                  sink_prob_ref):
        bias = _window_bias(block_index)
        kk, vv = _keys_values(z_ref, kv_before)
        kk_b, vvt_b = kk.astype(BF16), vv.T.astype(BF16)
        for g in range(2):
            qt = _heads_t([z_ref[:, _pair_cols(g, p, SEG_Q)] * ATTN_SCALE for p in range(4)], g).astype(BF16)
            prob, sink_prob = _softmax_t(_dot(kk_b, qt), bias, sink_ref[g])
            prob_b = prob.astype(BF16)
            prob_ref[g] = prob_b
            sink_prob_ref[g] = sink_prob
            ot = _dot(vvt_b, prob_b)
            for p in range(4):
                gate = z_ref[:, _pair_cols(g, p, SEG_GA)]
                a_ref[:, _pair_cols(g, p)] = (_pair_block(ot, p, g) * (gate * _sigmoid(gate))).astype(BF16)

        vhat, _ = _layer_norm_fwd(z_ref[:, SEG_VS:SEG_VS + D_SGU])
        vn = vhat * lng_ref[...] + lnb_ref[...]
        tril = _tril()
        for g in range(SGU_GROUPS):
            cols = slice(g * 128, (g + 1) * 128)
            wm = jnp.where(tril, w_ref[g], 0.0).astype(BF16)
            mixed = _dot(wm, vn[:, cols].astype(BF16)) + bt_ref[:, g:g + 1]
            gate = z_ref[:, SEG_GS + g * 128:SEG_GS + (g + 1) * 128]
            a_ref[:, D_ATTN + g * 128:D_ATTN + (g + 1) * 128] = (
                (z_ref[:, SEG_U + g * 128:SEG_U + (g + 1) * 128] * mixed) * (gate * _sigmoid(gate))).astype(BF16)

    rows = MIXER_BLOCKS * BLOCK
    return pl.pallas_call(
        body, name="mixer_fwd", grid=(t // rows,),
        in_specs=[pl.BlockSpec((rows, D_IN), lambda i: (i, 0)), _kv_before_spec(lambda i: i),
                  _const_spec((2, 1, 8 * BLOCK)), _const_spec((1, D_SGU)), _const_spec((1, D_SGU)),
                  _const_spec((SGU_GROUPS, BLOCK, BLOCK)), _const_spec((BLOCK, SGU_GROUPS))],
        out_specs=(pl.BlockSpec((rows, D_MODEL), lambda i: (i, 0)),
                   pl.BlockSpec((MIXER_BLOCKS, 2, 2 * BLOCK, 8 * BLOCK), lambda i: (i, 0, 0, 0)),
                   pl.BlockSpec((MIXER_BLOCKS, 2, 1, 8 * BLOCK), lambda i: (i, 0, 0, 0))),
        out_shape=(jax.ShapeDtypeStruct((t, D_MODEL), BF16),
                   jax.ShapeDtypeStruct((t // BLOCK, 2, 2 * BLOCK, 8 * BLOCK), BF16),
                   jax.ShapeDtypeStruct((t // BLOCK, 2, 1, 8 * BLOCK), F32)),
        compiler_params=_params(dimension_semantics=("arbitrary",)),
    )(z, z, sink_rows, ln_g, ln_b, sgu_w, sgu_bt)


def _mixer_bwd(z, da, probs, sink_probs, ln_g, ln_b, sgu_w, sgu_wt, sgu_bt, a, dy):
    t = z.shape[0]

    def body(z_all, kvp_ref, da_all, prob_ref, sink_prob_ref, lng_ref, lnb_ref, w_ref, wt_ref, bt_ref, a_ref, dy_ref,
             dz_all, dsink_ref, dw_ref, db_ref, dlng_ref, dlnb_ref, dw_out_ref, carry_ref, dsink_acc, dbt_acc):
        step = pl.program_id(0)

        @pl.when(step == 0)
        def _():
            carry_ref[...] = jnp.zeros_like(carry_ref)
            dsink_acc[...] = jnp.zeros_like(dsink_acc)
            dbt_acc[...] = jnp.zeros_like(dbt_acc)
            dw_ref[...] = jnp.zeros_like(dw_ref)
            dlng_ref[...] = jnp.zeros_like(dlng_ref)
            dlnb_ref[...] = jnp.zeros_like(dlnb_ref)

        carry = carry_ref[...]
        dw_out_ref[...] = _dot_tn(a_ref[...], dy_ref[...]).astype(BF16)
        for sub in reversed(range(MIXER_BLOCKS)):
            kv_before = kvp_ref[...] if sub == 0 else _Rows(z_all, sub - 1)[:, SEG_KV:SEG_KV + 2 * D_KV]
            carry = one_block(_Rows(z_all, sub), kv_before, _Rows(da_all, sub), prob_ref.at[sub], sink_prob_ref.at[sub],
                              carry, lng_ref, lnb_ref, w_ref, wt_ref, bt_ref, _Rows(dz_all, sub),
                              dw_ref, dlng_ref, dlnb_ref, dsink_acc, dbt_acc)
        carry_ref[...] = carry

        @pl.when(step == ns - 1)
        def _():
            db_ref[...] = dbt_acc[...].T[:SGU_GROUPS]
            lane_row = lax.broadcasted_iota(jnp.int32, (1, 128), 1)
            d_sink = jnp.zeros((1, 128), F32)
            for g in range(2):
                acc = dsink_acc[g]
                for j in range(8):
                    head_sum = jnp.sum(acc[:, j * BLOCK:(j + 1) * BLOCK], axis=-1, keepdims=True)
                    d_sink = d_sink + jnp.where(lane_row == 8 * g + j, head_sum, 0.0)
            dsink_ref[...] = d_sink

    def one_block(z_ref, kv_before, da_ref, prob_ref, sink_prob_ref, carry, lng_ref, lnb_ref, w_ref, wt_ref, bt_ref,
                  dz_ref, dw_ref, dlng_ref, dlnb_ref, dsink_acc, dbt_acc):
        kk, vv = _keys_values(z_ref, kv_before)
        vv_b = vv.astype(BF16)
        kkt_b, vvt_b = kk.T.astype(BF16), vv.T.astype(BF16)
        dkk = jnp.zeros((2 * BLOCK, D_KV), F32)
        dvv = jnp.zeros((2 * BLOCK, D_KV), F32)
        for g in range(2):
            qt = _heads_t([z_ref[:, _pair_cols(g, p, SEG_Q)] * ATTN_SCALE for p in range(4)], g).astype(BF16)
            prob_b, sink_prob = prob_ref[g], sink_prob_ref[g]
            prob = prob_b.astype(F32)
            ot = _dot(vvt_b, prob_b)
            gates = [z_ref[:, _pair_cols(g, p, SEG_GA)] for p in range(4)]
            sig = [_sigmoid(gt) for gt in gates]
            d_attn = [da_ref[:, _pair_cols(g, p)] for p in range(4)]
            d_ot = _heads_t([d_attn[p] * (gates[p] * sig[p]) for p in range(4)], g).astype(BF16)
            d_prob = _dot(vv_b, d_ot)
            delta = jnp.sum(prob * d_prob, axis=0, keepdims=True)
            d_scores = (prob * (d_prob - delta)).astype(BF16)
            dsink_acc[g] -= sink_prob * delta
            d_qt = _dot(kkt_b, d_scores)
            dkk = dkk + _dot_nt(d_scores, qt)
            dvv = dvv + _dot_nt(prob_b, d_ot)
            for p in range(4):
                dz_ref[:, _pair_cols(g, p, SEG_Q)] = (_pair_block(d_qt, p, g) * ATTN_SCALE).astype(BF16)
                d_silu = sig[p] * (1.0 + gates[p] * (1.0 - sig[p]))
                dz_ref[:, _pair_cols(g, p, SEG_GA)] = (d_attn[p] * _pair_block(ot, p, g) * d_silu).astype(BF16)
        d_kv = jnp.concatenate([dkk, dvv], axis=1)
        dz_ref[:, SEG_KV:SEG_KV + 2 * D_KV] = (d_kv[BLOCK:] + carry).astype(BF16)

        vhat, rstd = _layer_norm_fwd(z_ref[:, SEG_VS:SEG_VS + D_SGU])
        lng = lng_ref[...]
        vn = vhat * lng + lnb_ref[...]
        tril, triu = _tril(), _tril(transposed=True)
        lane = lax.broadcasted_iota(jnp.int32, (BLOCK, 128), 1)
        d_bt = jnp.zeros((BLOCK, 128), F32)
        d_vn = []
        for g in range(SGU_GROUPS):
            cols = slice(g * 128, (g + 1) * 128)
            wm = jnp.where(tril, w_ref[g], 0.0).astype(BF16)
            wmt = jnp.where(triu, wt_ref[g], 0.0).astype(BF16)
            vn_g = vn[:, cols].astype(BF16)
            mixed = _dot(wm, vn_g) + bt_ref[:, g:g + 1]
            gate = z_ref[:, SEG_GS + g * 128:SEG_GS + (g + 1) * 128]
            u = z_ref[:, SEG_U + g * 128:SEG_U + (g + 1) * 128]
            d_out = da_ref[:, D_ATTN + g * 128:D_ATTN + (g + 1) * 128]
            sg = _sigmoid(gate)
            d_um = d_out * (gate * sg)
            dz_ref[:, SEG_U + g * 128:SEG_U + (g + 1) * 128] = (d_um * mixed).astype(BF16)
            dz_ref[:, SEG_GS + g * 128:SEG_GS + (g + 1) * 128] = (
                d_out * (u * mixed) * (sg * (1.0 + gate * (1.0 - sg)))).astype(BF16)
            d_mixed = d_um * u
            d_mixed_b = d_mixed.astype(BF16)
            dw_ref[g] += jnp.where(tril, _dot_nt(d_mixed_b, vn_g), 0.0)
            d_bt = d_bt + jnp.where(lane == g, jnp.sum(d_mixed, axis=-1, keepdims=True), 0.0)
            d_vn.append(_dot(wmt, d_mixed_b))
        dbt_acc[...] += d_bt
        d_vn = jnp.concatenate(d_vn, axis=1)
        dlng_ref[...] += jnp.sum(d_vn * vhat, axis=0, keepdims=True)
        dlnb_ref[...] += jnp.sum(d_vn, axis=0, keepdims=True)
        d_vhat = d_vn * lng
        d_v = rstd * (d_vhat - jnp.mean(d_vhat, axis=-1, keepdims=True)
                      - vhat * jnp.mean(d_vhat * vhat, axis=-1, keepdims=True))
        dz_ref[:, SEG_VS:SEG_VS + D_SGU] = d_v.astype(BF16)
        return d_kv[:BLOCK]

    rows = MIXER_BLOCKS * BLOCK
    ns = t // rows
    rev = lambda i: ns - 1 - i
    tile = a.shape[1] // ns
    assert tile % 128 == 0
    return pl.pallas_call(
        body, name="mixer_bwd", grid=(ns,),
        in_specs=[pl.BlockSpec((rows, D_IN), lambda i: (rev(i), 0)), _kv_before_spec(rev),
                  pl.BlockSpec((rows, D_MODEL), lambda i: (rev(i), 0)),
                  pl.BlockSpec((MIXER_BLOCKS, 2, 2 * BLOCK, 8 * BLOCK), lambda i: (rev(i), 0, 0, 0)),
                  pl.BlockSpec((MIXER_BLOCKS, 2, 1, 8 * BLOCK), lambda i: (rev(i), 0, 0, 0)),
                  _const_spec((1, D_SGU)), _const_spec((1, D_SGU)),
                  _const_spec((SGU_GROUPS, BLOCK, BLOCK)), _const_spec((SGU_GROUPS, BLOCK, BLOCK)),
                  _const_spec((BLOCK, SGU_GROUPS)), pl.BlockSpec((t, tile), lambda i: (0, i)),
                  pl.BlockSpec(dy.shape, lambda i: (0, 0), pipeline_mode=pl.Buffered(1))],
        out_specs=(pl.BlockSpec((rows, D_IN), lambda i: (rev(i), 0)), _const_spec((1, 128)),
                   _const_spec((SGU_GROUPS, BLOCK, BLOCK)), _const_spec((SGU_GROUPS, BLOCK)),
                   _const_spec((1, D_SGU)), _const_spec((1, D_SGU)),
                   pl.BlockSpec((tile, dy.shape[1]), lambda i: (i, 0))),
        out_shape=(jax.ShapeDtypeStruct((t, D_IN), BF16), jax.ShapeDtypeStruct((1, 128), F32),
                   jax.ShapeDtypeStruct((SGU_GROUPS, BLOCK, BLOCK), F32), jax.ShapeDtypeStruct((SGU_GROUPS, BLOCK), F32),
                   jax.ShapeDtypeStruct((1, D_SGU), F32), jax.ShapeDtypeStruct((1, D_SGU), F32),
                   jax.ShapeDtypeStruct((a.shape[1], dy.shape[1]), BF16)),
        scratch_shapes=[pltpu.VMEM((BLOCK, 2 * D_KV), F32), pltpu.VMEM((2, 1, 8 * BLOCK), F32),
                        pltpu.VMEM((BLOCK, 128), F32)],
        compiler_params=_params(dimension_semantics=("arbitrary",)),
    )(z, z, da, probs, sink_probs, ln_g, ln_b, sgu_w, sgu_wt, sgu_bt, a, dy)


def _out_proj_head(a, w_out_full, x, target, mod, final_g, tm=256):
    t, d = x.shape

    def body(a_ref, w_ref, x_ref, tg_ref, gate_ref, fg_ref, dx2_ref, dy_ref, da_ref, loss_ref, dfg_ref, dgate_ref):
        @pl.when(pl.program_id(0) == 0)
        def _():
            loss_ref[...] = jnp.zeros_like(loss_ref)
            dfg_ref[...] = jnp.zeros_like(dfg_ref)
            dgate_ref[...] = jnp.zeros_like(dgate_ref)

        yv, gate, fg = _dot(a_ref[...], w_ref[...]), gate_ref[...], fg_ref[...]
        x2 = x_ref[...] + gate * yv
        r2 = lax.rsqrt(jnp.mean(x2 * x2, axis=-1, keepdims=True) + EPS)
        nrm = x2 * r2
        err = nrm * fg - tg_ref[...]
        loss_ref[...] += 0.5 * jnp.sum(jnp.mean(err * err, axis=-1, keepdims=True), axis=0, keepdims=True)
        fg_d = fg * (1.0 / d)
        err_nrm = err * nrm
        dfg_ref[...] += jnp.sum(err_nrm, axis=0, keepdims=True) * (1.0 / d)
        d_nrm = err * fg_d
        dx2 = r2 * (d_nrm - nrm * jnp.mean(err_nrm * fg_d, axis=-1, keepdims=True))
        dx2_ref[...] = dx2
        dgate_ref[...] += jnp.sum(dx2 * yv, axis=0, keepdims=True)
        dy = (dx2 * gate).astype(BF16)
        dy_ref[...] = dy
        da_ref[...] = _dot_nt(dy, w_ref[...])

    blk = pl.BlockSpec((tm, d), lambda i: (i, 0))
    a_blk = pl.BlockSpec((tm, a.shape[1]), lambda i: (i, 0))
    row = _const_spec((1, d))
    whole = pl.BlockSpec(w_out_full.shape, lambda i: (0, 0), pipeline_mode=pl.Buffered(1))
    return pl.pallas_call(
        body, name="out_proj_head", grid=(t // tm,),
        in_specs=[a_blk, whole, blk, blk, _mod_spec(MOD_GATE, d), row],
        out_specs=(blk, blk, a_blk, _const_spec((1, 128)), row, row),
        out_shape=(jax.ShapeDtypeStruct((t, d), F32), jax.ShapeDtypeStruct((t, d), BF16),
                   jax.ShapeDtypeStruct(a.shape, F32), jax.ShapeDtypeStruct((1, 128), F32),
                   jax.ShapeDtypeStruct((1, d), F32), jax.ShapeDtypeStruct((1, d), F32)),
        compiler_params=_params(dimension_semantics=("arbitrary",)),
    )(a, w_out_full, x, target, mod, final_g)


def _z_proj_bwd_norm(dz, w_in_t, x, dx2, norm_g, mod, dep, tm=256):
    t, d = x.shape

    def body(dz_ref, w_ref, x_ref, dx2_ref, g_ref, sc_ref, dep_ref, gx_ref, dshift_ref, dscale_ref, dg_ref):
        @pl.when(pl.program_id(0) == 0)
        def _():
            dshift_ref[...] = jnp.zeros_like(dshift_ref)
            dscale_ref[...] = jnp.zeros_like(dscale_ref)
            dg_ref[...] = jnp.zeros_like(dg_ref)

        dh, xv, g = _dot(dz_ref[...], w_ref[...]), x_ref[...], g_ref[...]
        one_plus = 1.0 + sc_ref[...]
        r = lax.rsqrt(jnp.mean(xv * xv, axis=-1, keepdims=True) + EPS)
        xn = xv * r
        gain = one_plus * g
        dh_xn = dh * xn
        dh_xn_sum = jnp.sum(dh_xn, axis=0, keepdims=True)
        dshift_ref[...] += jnp.sum(dh, axis=0, keepdims=True)
        dscale_ref[...] += dh_xn_sum * g
        dg_ref[...] += dh_xn_sum * one_plus
        d_xn = dh * gain
        gx_ref[...] = dx2_ref[...] + r * (d_xn - xn * jnp.mean(dh_xn * gain, axis=-1, keepdims=True))

    blk = pl.BlockSpec((tm, d), lambda i: (i, 0))
    row = _const_spec((1, d))
    whole = pl.BlockSpec(w_in_t.shape, lambda i: (0, 0), pipeline_mode=pl.Buffered(1))
    return pl.pallas_call(
        body, name="z_proj_bwd_norm", grid=(t // tm,),
        in_specs=[pl.BlockSpec((tm, dz.shape[1]), lambda i: (i, 0)), whole, blk, blk, row, _mod_spec(MOD_SCALE, d),
                  _const_spec((8, 128))],
        out_specs=(blk, row, row, row),
        out_shape=(jax.ShapeDtypeStruct((t, d), F32),) + (jax.ShapeDtypeStruct((1, d), F32),) * 3,
        compiler_params=_params(dimension_semantics=("arbitrary",)),
    )(dz, w_in_t, x, dx2, norm_g, mod, dep)


def _adamw(w, g, m, v):
    m = ADAM_B1 * m + (1.0 - ADAM_B1) * g
    v = ADAM_B2 * v + (1.0 - ADAM_B2) * (g * g)
    m_hat = m / (1.0 - ADAM_B1 ** ADAM_STEP)
    v_hat = v / (1.0 - ADAM_B2 ** ADAM_STEP)
    delta = -ADAM_LR * (m_hat / (jnp.sqrt(v_hat) + ADAM_EPS) + ADAM_WD * w)
    return delta, m, v


def _relay_sum(device, blocks, land_pair, land_first, tr):
    _, r, c = blocks.shape

    def body(device_ref, a_ref, b_ref, c_ref, o_ref):
        o_ref[...] = (a_ref[...].astype(F32) + b_ref[...].astype(F32) + c_ref[...].astype(F32)).astype(BF16)

    second = pl.BlockSpec((None, tr, c), lambda i, device_ref: (1, i, 0))
    return pl.pallas_call(
        body, name="w_in_grad_relay_sum",
        grid_spec=pltpu.PrefetchScalarGridSpec(
            num_scalar_prefetch=1, grid=(r // tr,),
            in_specs=[pl.BlockSpec((None, tr, c), lambda i, device_ref: (device_ref[0], i, 0)), second, second],
            out_specs=pl.BlockSpec((tr, c), lambda i, device_ref: (i, 0))),
        out_shape=jax.ShapeDtypeStruct((r, c), BF16),
        compiler_params=_params(dimension_semantics=("arbitrary",)),
    )(device, blocks, land_pair, land_first)


def _adam_from_chips(chip, pair, landed, w, m, v, name, tc):
    _, r, c = pair.shape
    n = len(landed)
    tiles = c // tc
    n_in = n + 4

    def body(chip_ref, pair_hbm, *refs):
        landed_hbm, (w_hbm, m_hbm, v_hbm) = refs[:n], refs[n:n + 3]
        outs_hbm = refs[n + 3:n + 7]
        grads, w_in, m_in, v_in, staged, in_sems, out_sems = refs[n + 7:]
        sources = [pair_hbm.at[chip_ref[0]]] + [landed_hbm[k].at[index] for k, (_, index) in enumerate(landed)]

        def reads(i):
            cols = pl.ds(i * tc, tc)
            cps = [pltpu.make_async_copy(src.at[:, cols], grads.at[k, :, cols], in_sems.at[i * n_in + k])
                   for k, src in enumerate(sources)]
            for k, (src, dst) in enumerate([(w_hbm, w_in), (m_hbm, m_in), (v_hbm, v_in)]):
                cps.append(pltpu.make_async_copy(src.at[:, cols], dst.at[:, cols], in_sems.at[i * n_in + n + 1 + k]))
            return cps

        def writes(i):
            cols = pl.ds(i * tc, tc)
            return [pltpu.make_async_copy(staged.at[i % 2, k], outs_hbm[k].at[:, cols], out_sems.at[i * 4 + k])
                    for k in range(4)]

        for i in range(tiles):
            for cp in reads(i):
                cp.start()
        for i in range(tiles):
            cols = pl.ds(i * tc, tc)
            for cp in reads(i):
                cp.wait()
            if i >= 2:
                for cp in writes(i - 2):
                    cp.wait()
            g = grads[0, :, cols].astype(F32)
            for k in range(1, n + 1):
                g = g + grads[k, :, cols].astype(F32)
            delta, new_m, new_v = _adamw(w_in[:, cols], g, m_in[:, cols], v_in[:, cols])
            for k, value in enumerate([g, delta, new_m, new_v]):
                staged[i % 2, k] = value
            for cp in writes(i):
                cp.start()
        for i in range(max(tiles - 2, 0), tiles):
            for cp in writes(i):
                cp.wait()

    any_spec = pl.BlockSpec(memory_space=pl.ANY)
    return pl.pallas_call(
        body, name=name,
        grid_spec=pltpu.PrefetchScalarGridSpec(
            num_scalar_prefetch=1, grid=(1,), in_specs=[any_spec] * (n + 4), out_specs=(any_spec,) * 4,
            scratch_shapes=[pltpu.VMEM((n + 1, r, c), BF16), pltpu.VMEM((r, c), F32), pltpu.VMEM((r, c), F32),
                            pltpu.VMEM((r, c), F32), pltpu.VMEM((2, 4, r, tc), F32),
                            pltpu.SemaphoreType.DMA((tiles * n_in,)), pltpu.SemaphoreType.DMA((tiles * 4,))]),
        out_shape=(jax.ShapeDtypeStruct((r, c), F32),) * 4,
        compiler_params=_params(dimension_semantics=("arbitrary",)),
    )(chip, pair, *[array for array, _ in landed], w, m, v)


def _adam_w_ada(device, act_t, dmod_all, w, m, v, tr=512):
    r, c = w.shape

    def body(device_ref, a_ref, dm_ref, w_ref, m_ref, v_ref, g_ref, d_ref, nm_ref, nv_ref):
        g = _dot(a_ref[...].astype(BF16), dm_ref[...].astype(BF16))
        g_ref[...] = g
        d_ref[...], nm_ref[...], nv_ref[...] = _adamw(w_ref[...], g, m_ref[...], v_ref[...])

    blk = pl.BlockSpec((tr, c), lambda i, device_ref: (i, 0))
    return pl.pallas_call(
        body, name="adam_w_ada",
        grid_spec=pltpu.PrefetchScalarGridSpec(
            num_scalar_prefetch=1, grid=(r // tr,),
            in_specs=[pl.BlockSpec((tr, N_DEV), lambda i, device_ref: (i, 0)),
                      pl.BlockSpec((N_DEV, c), lambda i, device_ref: (0, device_ref[0])), blk, blk, blk],
            out_specs=(blk,) * 4),
        out_shape=(jax.ShapeDtypeStruct((r, c), F32),) * 4,
        compiler_params=_params(dimension_semantics=("arbitrary",)),
    )(device, act_t, dmod_all, w, m, v)


def _pack_small(d_shift, d_scale, d_gate, d_norm_g, d_final_g, d_ln_g, d_ln_b, loss, d_sinks, d_sgu_b):
    def body(shift_ref, scale_ref, gate_ref, ng_ref, fg_ref, lng_ref, lnb_ref, loss_ref, sink_ref, b_ref, o_ref):
        o_ref[...] = jnp.zeros_like(o_ref)
        o_ref[ROW_SHIFT:ROW_SHIFT + 1, :] = shift_ref[...]
        o_ref[ROW_SCALE:ROW_SCALE + 1, :] = scale_ref[...]
        o_ref[ROW_GATE:ROW_GATE + 1, :] = gate_ref[...]
        o_ref[ROW_NORM_G:ROW_NORM_G + 1, :] = ng_ref[...]
        o_ref[ROW_FINAL_G:ROW_FINAL_G + 1, :] = fg_ref[...]
        o_ref[ROW_LN:ROW_LN + 1, 0:D_SGU] = lng_ref[...]
        o_ref[ROW_LN:ROW_LN + 1, D_SGU:2 * D_SGU] = lnb_ref[...]
        o_ref[ROW_MISC:ROW_MISC + 1, 0:128] = loss_ref[...]
        o_ref[ROW_MISC:ROW_MISC + 1, 128:256] = sink_ref[...]
        o_ref[ROW_SGU_B:ROW_SGU_B + SGU_GROUPS, 0:BLOCK] = b_ref[...]

    return pl.pallas_call(
        body, name="pack_small", out_shape=jax.ShapeDtypeStruct((SMALL_ROWS, D_MODEL), F32),
        compiler_params=_params(),
    )(d_shift, d_scale, d_gate, d_norm_g, d_final_g, d_ln_g, d_ln_b, loss, d_sinks, d_sgu_b)


_SMALL_NAMES = ("norm_g", "b_ada", "attn_sinks", "sgu_ln_g", "sgu_ln_b", "sgu_w", "sgu_b", "final_g")


def _adam_small(partials, d_sgu_w_all, weights, moments_m, moments_v):
    names = _SMALL_NAMES
    k = len(names)

    def body(*refs):
        p_ref, sw_ref = refs[0], refs[1]
        w_refs, m_refs, v_refs = refs[2:2 + k], refs[2 + k:2 + 2 * k], refs[2 + 2 * k:2 + 3 * k]
        loss_ref, dmod_ref = refs[2 + 3 * k], refs[3 + 3 * k]
        out_refs = refs[4 + 3 * k:4 + 7 * k]
        sum_ref = refs[4 + 7 * k]
        total = p_ref[0]
        for j in range(1, N_DEV):
            total = total + p_ref[j]
        sum_ref[...] = total
        for j in range(N_DEV):
            for part, row in enumerate((ROW_SHIFT, ROW_SCALE, ROW_GATE)):
                dmod_ref[j:j + 1, part * D_MODEL:(part + 1) * D_MODEL] = p_ref[j, row:row + 1, :]
        loss_ref[...] = sum_ref[ROW_MISC:ROW_MISC + 1, 0:1]
        d_sgu_w = sw_ref[0]
        for j in range(1, N_DEV):
            d_sgu_w = d_sgu_w + sw_ref[j]
        grads = {
            "norm_g": sum_ref[ROW_NORM_G:ROW_NORM_G + 1, :],
            "b_ada": jnp.concatenate([sum_ref[r:r + 1, :] for r in (ROW_SHIFT, ROW_SCALE, ROW_GATE)], axis=1),
            "attn_sinks": sum_ref[ROW_MISC:ROW_MISC + 1, 128:128 + N_Q_HEADS],
            "sgu_ln_g": sum_ref[ROW_LN:ROW_LN + 1, 0:D_SGU],
            "sgu_ln_b": sum_ref[ROW_LN:ROW_LN + 1, D_SGU:2 * D_SGU],
            "sgu_w": d_sgu_w[None],
            "sgu_b": sum_ref[ROW_SGU_B:ROW_SGU_B + SGU_GROUPS, 0:BLOCK][None],
            "final_g": sum_ref[ROW_FINAL_G:ROW_FINAL_G + 1, :],
        }
        for i, name in enumerate(names):
            g = grads[name]
            delta, m, v = _adamw(w_refs[i][...], g, m_refs[i][...], v_refs[i][...])
            out_refs[4 * i][...] = g
            out_refs[4 * i + 1][...] = delta
            out_refs[4 * i + 2][...] = m
            out_refs[4 * i + 3][...] = v

    shapes = [jax.ShapeDtypeStruct((1, 1), F32), jax.ShapeDtypeStruct((N_DEV, 3 * D_MODEL), F32)]
    for name in names:
        shapes += [jax.ShapeDtypeStruct(weights[name].shape, F32)] * 4
    outs = pl.pallas_call(
        body, name="adam_small", out_shape=tuple(shapes),
        scratch_shapes=[pltpu.VMEM((SMALL_ROWS, D_MODEL), F32)],
        compiler_params=_params(),
    )(partials, d_sgu_w_all, *[weights[n] for n in names], *[moments_m[n] for n in names],
      *[moments_v[n] for n in names])
    return outs[0], outs[1], {name: outs[2 + 4 * i:6 + 4 * i] for i, name in enumerate(names)}


def kernel(x, c, norm_g, w_ada, b_ada, w_in, attn_sinks, sgu_ln_g, sgu_ln_b, sgu_w, sgu_b, w_out, final_g, loss_target, m_norm_g, m_w_ada, m_b_ada, m_w_in, m_attn_sinks, m_sgu_ln_g, m_sgu_ln_b, m_sgu_w, m_sgu_b, m_w_out, m_final_g, v_norm_g, v_w_ada, v_b_ada, v_w_in, v_attn_sinks, v_sgu_ln_g, v_sgu_ln_b, v_sgu_w, v_sgu_b, v_w_out, v_final_g):
    xi, yi, ci = _place()
    me = 4 * xi + 2 * yi + ci
    x2d, target = x[0], loss_target[0]

    core = ci.astype(jnp.int32).reshape(1)
    chip = (2 * xi + yi).astype(jnp.int32).reshape(1)

    first = _own_block_copies(_first_targets)
    first_flight = _start_copies([_with_own_slot(w_in[0].T.astype(BF16), me)], first, 2, core, "gather_w_in_start")

    c_all = _all_gather_small(c.reshape(8, 256), "gather_c", first_flight[3]).reshape(N_DEV, D_MODEL)
    device = me.astype(jnp.int32).reshape(1)
    c_act, mod_part = _modulation(device, c_all, w_ada[0], b_ada)
    mod_all = _all_gather_small(mod_part, "gather_mod")

    across = _wait_then_start(first_flight, lambda *a: first(*a)[1:], _second_axis_stage_copies, 3, mod_all,
                              "gather_w_in_second_axis_stage")
    mod = lax.dynamic_index_in_dim(mod_all, me, axis=1, keepdims=False).reshape(1, 3 * D_MODEL)
    mod = mod + across[3][0, 0]

    w_in_pair = _wait_copies((first_flight[0], first_flight[1], across[2], None), lambda *a: first(*a)[:1], mod,
                             "gather_w_in_sibling_wait")
    h, z_own = _norm_z_proj_own(x2d, norm_g, mod, w_in_pair[0].reshape(D_IN, D_MODEL), chip)
    w_out_early = _own_block_copies(lambda x, y, c: [(x, y, 1 - c), (*_second_axis_chip(x, y, c), c)])
    w_out_late = _own_block_copies(lambda x, y, c: [(*_first_axis_chip(x, y, c), c), (1 - x, 1 - y, c)])
    forward = _wait_then_start(
        (across[0], across[1], w_in_pair, None), lambda *a: _second_axis_stage_copies(*a)[:1],
        lambda refs, s, r: _second_axis_forward_copies(refs[:1], s, r) + _group(w_out_early, 1, 1, 1)(refs, s, r),
        3, z_own, "gather_w_in_second_axis_forward", more_bufs=[_with_own_slot(w_out[0].astype(BF16), me)])
    w_in_most = _wait_copies((across[0], across[1], forward[2][:1], None),
                             lambda *a: _second_axis_stage_copies(*a)[1:2], z_own, "gather_w_in_first_forward_wait")
    w_in_most = _wait_copies((forward[0], forward[1], w_in_most, None), _second_axis_forward_copies, z_own,
                             "gather_w_in_second_forward_wait")
    z_early = _z_proj(h, w_in_most[0].reshape(D_IN, D_MODEL), chip, 1, _Z_EARLY_TILES - 1, z_own, "z_proj_early")
    last = _wait_then_start(
        (across[0], across[1], [w_in_most[0], forward[2][1]], None), lambda *a: _second_axis_stage_copies(*a)[2:],
        lambda refs, s, r: _diagonal_forward_copies(refs[:1], s, r) + _group(w_out_late, 1, 1, 1)(refs, s, r),
        3, z_early, "gather_w_in_last_stage")
    w_in_all = _wait_copies((last[0], last[1], last[2][:1], None), _diagonal_forward_copies, z_early,
                            "gather_w_in_last_wait")[0]
    w_in_t = w_in_all.reshape(D_IN, D_MODEL)
    z = _z_proj(h, w_in_t, chip, _Z_EARLY_TILES, 7 - _Z_EARLY_TILES, z_early, "z_proj_late")
    w_out_half = _wait_copies((forward[0], forward[1], last[2][1:], None), _group(w_out_early, 0, 1, 1), z,
                              "gather_w_out_early_wait")
    w_out_flight = _wait_then_start((last[0], last[1], w_out_half, None), _group(w_out_late, 0, 1, 1),
                                    _forward_copies, 3, z, "gather_w_out_forward_stage")
    sink_rows = jnp.repeat(attn_sinks.reshape(N_Q_HEADS), BLOCK).reshape(2, 1, 8 * BLOCK)
    sgu_bt = sgu_b[0].T
    a, probs, sink_probs = _mixer_fwd(z, sink_rows + w_out_flight[3][0, 0], sgu_ln_g, sgu_ln_b, sgu_w[0], sgu_bt)
    w_out_all = _wait_copies(w_out_flight, _forward_copies, a, "gather_w_out_forward_wait")[0]
    w_out_full = w_out_all.reshape(D_MODEL, D_MODEL)
    final_g_row = final_g.reshape(1, D_MODEL)
    dx2, dy, da, loss_part, d_final_g, d_gate = _out_proj_head(a, w_out_full, x2d, target, mod, final_g_row)

    dz, d_sinks, d_sgu_w, d_sgu_b, d_ln_g, d_ln_b, dw_out = _mixer_bwd(
        z, da, probs, sink_probs, sgu_ln_g, sgu_ln_b, sgu_w[0], jnp.swapaxes(sgu_w[0], 1, 2), sgu_bt, a, dy)
    pair_out = _pair_reduce(dw_out.reshape(4, 2, W_OUT_SHARD, D_MODEL), _every_chip, "w_out_grad_pair_reduce",
                            W_OUT_SHARD // 2)
    sgu_w_to_all = _group(_own_block_copies(_all_others), 2, 1, 3)
    both = _start_copies(
        [pair_out, lax.empty((3, W_OUT_SHARD, D_MODEL), BF16), _with_own_slot(d_sgu_w, me)],
        lambda refs, s, r: _chip_copies(refs[:2], s, r) + sgu_w_to_all(refs, s, r), 3 + N_DEV - 1, core,
        "w_out_grad_chip_and_sgu_w_gather_start")
    out_flight, sgu_w_flight = (both[0], both[1], both[2][:2], None), (both[0], both[1], both[2][2:], None)
    dw_in_t = _w_in_grad(dz, h, both[3], Z_TILE)
    pair_in = _pair_reduce(dw_in_t.reshape(4, 2, W_IN_SHARD, D_MODEL), _first_hop_chips, "w_in_grad_pair_reduce",
                           W_IN_SHARD // 3)
    first_hop = lambda refs, s, r: _first_hop_copies(refs[:2], s, r) + _group(_late_pair_copies, 2, 2, 2)(refs, s, r)
    hop1 = _start_copies(
        [pair_in, lax.empty((2, W_IN_SHARD, D_MODEL), BF16), dw_in_t.reshape(N_DEV, W_IN_SHARD, D_MODEL),
         lax.empty((2, W_IN_SHARD, D_MODEL), BF16)], first_hop, 4, core, "w_in_grad_first_hop_start")
    grad_x, d_shift, d_scale, d_norm_g = _z_proj_bwd_norm(dz, w_in_t, x2d, dx2, norm_g, mod, hop1[3])

    partial = _pack_small(d_shift, d_scale, d_gate, d_norm_g, d_final_g, d_ln_g, d_ln_b, loss_part, d_sinks, d_sgu_b)
    small_flight = _start_copies([_with_own_slot(partial, me)], _own_block_copies(_all_others), N_DEV - 1, core,
                                 "small_grad_gather_start")
    _, land_first, dw_in_t, land_pair = _wait_copies(hop1, first_hop, small_flight[3], "w_in_grad_first_hop_wait")
    second_device = (4 * ((xi + ci) % 2) + 2 * ((yi + 1 - ci) % 2) + ci).astype(jnp.int32).reshape(1)
    relay = _relay_sum(second_device, dw_in_t, land_pair, land_first, W_IN_SHARD // 3)
    hop2 = _start_copies([relay, lax.empty((1, W_IN_SHARD, D_MODEL), BF16)], _second_hop_copies, 1, core,
                         "w_in_grad_second_hop_start")
    pair_out, land_out = _wait_copies(out_flight, _chip_copies, hop2[3], "w_out_grad_chip_wait")
    big = {"w_out": _adam_from_chips(chip, pair_out, [(land_out, k) for k in range(3)], w_out[0], m_w_out[0],
                                     v_w_out[0], "adam_w_out", 1024)}
    partial_all = _wait_copies(small_flight, _own_block_copies(_all_others), big["w_out"][0],
                               "small_grad_gather_wait")[0]
    d_sgu_w_all = _wait_copies(sgu_w_flight, _group(_own_block_copies(_all_others), 0, 1, 3), partial_all,
                               "sgu_w_grad_gather_wait")[0]
    weights = {"norm_g": norm_g, "b_ada": b_ada, "attn_sinks": attn_sinks, "sgu_ln_g": sgu_ln_g,
               "sgu_ln_b": sgu_ln_b, "sgu_w": sgu_w, "sgu_b": sgu_b, "final_g": final_g_row}
    moments_m = {"norm_g": m_norm_g, "b_ada": m_b_ada, "attn_sinks": m_attn_sinks, "sgu_ln_g": m_sgu_ln_g,
                 "sgu_ln_b": m_sgu_ln_b, "sgu_w": m_sgu_w, "sgu_b": m_sgu_b,
                 "final_g": m_final_g.reshape(1, D_MODEL)}
    moments_v = {"norm_g": v_norm_g, "b_ada": v_b_ada, "attn_sinks": v_attn_sinks, "sgu_ln_g": v_sgu_ln_g,
                 "sgu_ln_b": v_sgu_ln_b, "sgu_w": v_sgu_w, "sgu_b": v_sgu_b,
                 "final_g": v_final_g.reshape(1, D_MODEL)}
    loss, dmod_all, small = _adam_small(partial_all, d_sgu_w_all, weights, moments_m, moments_v)
    small["final_g"] = tuple(o.reshape(D_MODEL) for o in small["final_g"])

    big["w_ada"] = _adam_w_ada(device, c_act.T, dmod_all, w_ada[0], m_w_ada[0], v_w_ada[0])
    _, land_second = _wait_copies(hop2, _second_hop_copies, big["w_ada"][0], "w_in_grad_second_hop_wait")
    big["w_in"] = tuple(o.T for o in _adam_from_chips(
        device, dw_in_t, [(land_pair, 0), (land_first, 0), (land_second, 0)], w_in[0].T, m_w_in[0].T, v_w_in[0].T,
        "adam_w_in", 512))
    order = ["norm_g", "w_ada", "b_ada", "w_in", "attn_sinks", "sgu_ln_g", "sgu_ln_b", "sgu_w", "sgu_b", "w_out",
             "final_g"]
    outs = [loss.reshape(()), grad_x[None]]
    for k in range(4):
        for name in order:
            outs.append(big[name][k][None] if name in big else small[name][k])
    return tuple(outs)
```

```python
import jax
import jax.numpy as jnp
from jax import lax
from jax.experimental import pallas as pl
from jax.experimental.pallas import tpu as pltpu

F32 = jnp.float32
BF16 = jnp.bfloat16
MESH = pl.DeviceIdType.MESH

N_DEV = 8
D_MODEL = 2048
HEAD_DIM = 64
D_ATTN = 1024
N_Q_HEADS = 16
D_KV = 128
BLOCK = 128
D_SGU = 1024
SGU_GROUPS = 8
D_IN = 5376
W_IN_SHARD = D_IN // N_DEV
W_OUT_SHARD = D_MODEL // N_DEV
W_ADA_SHARD = 3 * D_MODEL // N_DEV
EPS = 1e-6
ATTN_SCALE = 0.125

ADAM_LR = 0.001
ADAM_B1 = 0.9
ADAM_B2 = 0.999
ADAM_EPS = 1e-08
ADAM_WD = 0.01
ADAM_STEP = 10

SEG_Q, SEG_KV, SEG_GA, SEG_U, SEG_VS, SEG_GS = 0, 1024, 1280, 2304, 3328, 4352

VMEM_LIMIT = 56 * 1024 * 1024

ROW_SHIFT, ROW_SCALE, ROW_GATE, ROW_NORM_G, ROW_FINAL_G, ROW_LN, ROW_MISC, ROW_SGU_B = 0, 1, 2, 3, 4, 5, 6, 8
SMALL_ROWS = 16


def _params(**kw):
    return pltpu.CompilerParams(vmem_limit_bytes=VMEM_LIMIT, **kw)


def _sigmoid(x):
    return 0.5 * (jnp.tanh(0.5 * x) + 1.0)


def _place():
    return lax.axis_index("x"), lax.axis_index("y"), lax.axis_index("c")


def _every_chip(x, y, c):
    return [0, 1, 2, 3]


def _first_hop_chips(x, y, c):
    first = _first_axis_chip(x, y, c)
    return [2 * first[0] + first[1], 2 * (1 - x) + (1 - y)]


def _pair_reduce(blocks, chips, name, row_chunk):
    _, _, r, cols = blocks.shape
    n = len(chips(0, 0, 0))
    assert r % row_chunk == 0

    def body(in_ref, out_ref, land, own, summed, send_sems, recv_sems, own_sems, out_sems):
        x, y, c = _place()
        sends, loads, stores = [], [], []
        for m in range(n):
            cp = pltpu.make_async_remote_copy(
                src_ref=in_ref.at[chips(x, y, 1 - c)[m], 1 - c], dst_ref=land.at[m], send_sem=send_sems.at[m],
                recv_sem=recv_sems.at[m], device_id=(x, y, 1 - c), device_id_type=MESH)
            cp.start()
            sends.append(cp)
            ld = pltpu.make_async_copy(in_ref.at[chips(x, y, c)[m], c], own.at[m], own_sems.at[m])
            ld.start()
            loads.append(ld)
        for m in range(n):
            sends[m].wait_recv()
            loads[m].wait()
            for k in range(r // row_chunk):
                rows = slice(k * row_chunk, (k + 1) * row_chunk)
                summed[m, rows, :] = (own[m, rows, :].astype(F32) + land[m, rows, :].astype(F32)).astype(BF16)
            st = pltpu.make_async_copy(summed.at[m], out_ref.at[m], out_sems.at[m])
            st.start()
            stores.append(st)
        for m in range(n):
            sends[m].wait_send()
            stores[m].wait()

    spec = pl.BlockSpec(memory_space=pl.ANY)
    return pl.pallas_call(
        body, name=name, out_shape=jax.ShapeDtypeStruct((n, r, cols), BF16),
        in_specs=[spec], out_specs=spec,
        scratch_shapes=[pltpu.VMEM((n, r, cols), BF16), pltpu.VMEM((n, r, cols), BF16), pltpu.VMEM((n, r, cols), BF16),
                        pltpu.SemaphoreType.DMA((n,)), pltpu.SemaphoreType.DMA((n,)), pltpu.SemaphoreType.DMA((n,)),
                        pltpu.SemaphoreType.DMA((n,))],
        compiler_params=_params(),
    )(blocks)


_HBM = pl.BlockSpec(memory_space=pltpu.HBM)
_SEM = pl.BlockSpec(memory_space=pltpu.SEMAPHORE)
_EFFECT = pltpu.SideEffectType.DATAFLOW_SIDE_EFFECTING


def _start_copies(bufs, copies, n_copies, after, name):
    nb = len(bufs)

    def body(*refs):
        for cp in copies(refs[:nb], refs[nb + 1], refs[nb + 2]):
            cp.start()
        refs[-1][...] = jnp.zeros_like(refs[-1])

    out = pl.pallas_call(
        body, name=name,
        out_shape=(pltpu.SemaphoreType.DMA((n_copies,)), pltpu.SemaphoreType.DMA((n_copies,)),
                   *[pltpu.HBM(b.shape, b.dtype) for b in bufs], jax.ShapeDtypeStruct((8, 128), F32)),
        in_specs=(_HBM,) * nb + (pl.BlockSpec(memory_space=pl.ANY),),
        out_specs=(_SEM, _SEM) + (_HBM,) * nb + (pl.BlockSpec(memory_space=pltpu.VMEM),),
        input_output_aliases={i: 2 + i for i in range(nb)},
        compiler_params=pltpu.CompilerParams(has_side_effects=_EFFECT),
    )(*[pltpu.with_memory_space_constraint(b, pltpu.HBM) for b in bufs], after)
    return out[0], out[1], list(out[2:2 + nb]), out[-1]


def _wait_copies(flight, copies, after, name):
    send_sems, recv_sems, bufs, _ = flight
    nb = len(bufs)

    def body(*refs):
        for cp in copies(refs[:nb], refs[nb], refs[nb + 1]):
            cp.wait_send()
            cp.wait_recv()

    return pl.pallas_call(
        body, name=name,
        out_shape=tuple(pltpu.HBM(b.shape, b.dtype) for b in bufs),
        in_specs=(_HBM,) * nb + (_SEM, _SEM, pl.BlockSpec(memory_space=pl.ANY)), out_specs=(_HBM,) * nb,
        input_output_aliases={i: i for i in range(nb)},
        compiler_params=pltpu.CompilerParams(has_side_effects=_EFFECT),
    )(*bufs, send_sems, recv_sems, after)


class _From:
    def __init__(self, sems, offset):
        self.sems, self.offset = sems, offset

    @property
    def at(self):
        return self

    def __getitem__(self, k):
        return self.sems.at[k + self.offset]


def _group(copies, first_buf, n_bufs, offset):
    def grouped(refs, send_sems, recv_sems):
        return copies(refs[first_buf:first_buf + n_bufs], _From(send_sems, offset), _From(recv_sems, offset))
    return grouped


def _wait_then_start(flight, waited, started, n_started, after, name, more_bufs=()):
    old_send, old_recv, bufs, _ = flight
    bufs = list(bufs) + [pltpu.with_memory_space_constraint(b, pltpu.HBM) for b in more_bufs]
    nb = len(bufs)

    def body(*refs):
        for cp in waited(refs[:nb], refs[nb], refs[nb + 1]):
            cp.wait_send()
            cp.wait_recv()
        for cp in started(refs[:nb], refs[nb + 3], refs[nb + 4]):
            cp.start()
        refs[-1][...] = jnp.zeros_like(refs[-1])

    out = pl.pallas_call(
        body, name=name,
        out_shape=(pltpu.SemaphoreType.DMA((n_started,)), pltpu.SemaphoreType.DMA((n_started,)),
                   *[pltpu.HBM(b.shape, b.dtype) for b in bufs], jax.ShapeDtypeStruct((8, 128), F32)),
        in_specs=(_HBM,) * nb + (_SEM, _SEM, pl.BlockSpec(memory_space=pl.ANY)),
        out_specs=(_SEM, _SEM) + (_HBM,) * nb + (pl.BlockSpec(memory_space=pltpu.VMEM),),
        input_output_aliases={i: 2 + i for i in range(nb)},
        compiler_params=pltpu.CompilerParams(has_side_effects=_EFFECT),
    )(*bufs, old_send, old_recv, after)
    return out[0], out[1], list(out[2:2 + nb]), out[-1]


def _late_pair_copies(refs, send_sems, recv_sems):
    blocks_ref, land_ref = refs
    x, y, c = _place()
    first = _first_axis_chip(x, y, c)
    devices = [4 * x + 2 * y + 1 - c, 4 * first[0] + 2 * first[1] + 1 - c]
    return [pltpu.make_async_remote_copy(
        src_ref=blocks_ref.at[devices[k]], dst_ref=land_ref.at[k], send_sem=send_sems.at[k], recv_sem=recv_sems.at[k],
        device_id=(x, y, 1 - c), device_id_type=MESH) for k in range(2)]


def _chip_copies(refs, send_sems, recv_sems):
    pair_ref, land_ref = refs
    x, y, c = _place()
    chips = [(1 - x, y), (x, 1 - y), (1 - x, 1 - y)]
    return [pltpu.make_async_remote_copy(
        src_ref=pair_ref.at[2 * chip[0] + chip[1]], dst_ref=land_ref.at[k],
        send_sem=send_sems.at[k], recv_sem=recv_sems.at[k],
        device_id=(*chip, c), device_id_type=MESH) for k, chip in enumerate(chips)]


def _first_hop_copies(refs, send_sems, recv_sems):
    pair_ref, land_ref = refs
    x, y, c = _place()
    return [pltpu.make_async_remote_copy(
        src_ref=pair_ref.at[k], dst_ref=land_ref.at[k], send_sem=send_sems.at[k], recv_sem=recv_sems.at[k],
        device_id=(*_first_axis_chip(x, y, c), c), device_id_type=MESH) for k in range(2)]


def _second_hop_copies(refs, send_sems, recv_sems):
    relay_ref, land_ref = refs
    x, y, c = _place()
    second = ((x + c) % 2, (y + 1 - c) % 2)
    return [pltpu.make_async_remote_copy(
        src_ref=relay_ref, dst_ref=land_ref.at[0], send_sem=send_sems.at[0], recv_sem=recv_sems.at[0],
        device_id=(*second, c), device_id_type=MESH)]


def _own_block_copies(targets):
    def copies(refs, send_sems, recv_sems):
        x, y, c = _place()
        mine = refs[0].at[4 * x + 2 * y + c]
        return [pltpu.make_async_remote_copy(
            src_ref=mine, dst_ref=mine, send_sem=send_sems.at[k], recv_sem=recv_sems.at[k],
            device_id=to, device_id_type=MESH) for k, to in enumerate(targets(x, y, c))]
    return copies


def _all_others(x, y, c):
    flip = lambda v, f: 1 - v if f else v
    return [(flip(x, r & 4), flip(y, r & 2), flip(c, r & 1)) for r in range(1, N_DEV)]


def _forward_copies(refs, send_sems, recv_sems):
    x, y, c = _place()
    chips = [(1 - x, y), (x, 1 - y), (1 - x, 1 - y)]
    return [pltpu.make_async_remote_copy(
        src_ref=refs[0].at[4 * chip[0] + 2 * chip[1] + c], dst_ref=refs[0].at[4 * chip[0] + 2 * chip[1] + c],
        send_sem=send_sems.at[k], recv_sem=recv_sems.at[k],
        device_id=(x, y, 1 - c), device_id_type=MESH) for k, chip in enumerate(chips)]


def _first_axis_chip(x, y, c):
    return (x + 1 - c) % 2, (y + c) % 2


def _second_axis_chip(x, y, c):
    return (x + c) % 2, (y + 1 - c) % 2


def _first_targets(x, y, c):
    return [(x, y, 1 - c), (*_first_axis_chip(x, y, c), c)]


def _all_gather_small(shard, name, dep=None):
    def body(in_ref, *refs):
        out_ref, send_sems, recv_sems, local_sem = refs[-4:]
        x, y, c = _place()
        me, sibling = 4 * x + 2 * y + c, (x, y, 1 - c)
        first, second = _first_axis_chip(x, y, c), _second_axis_chip(x, y, c)

        def pair(chip):
            return out_ref.at[pl.ds(2 * (2 * chip[0] + chip[1]), 2)]

        def exchange(k, src, dst, to):
            cp = pltpu.make_async_remote_copy(src_ref=src, dst_ref=dst, send_sem=send_sems.at[k],
                                              recv_sem=recv_sems.at[k], device_id=to, device_id_type=MESH)
            cp.start()
            cp.wait()

        own = pltpu.make_async_copy(in_ref, out_ref.at[me], local_sem)
        own.start()
        exchange(0, in_ref, out_ref.at[me], sibling)
        own.wait()
        exchange(1, pair((x, y)), pair((x, y)), (*second, c))
        exchange(2, pair(second), pair(second), sibling)
        exchange(3, pair(first), pair(first), (*second, c))

    spec = pl.BlockSpec(memory_space=pltpu.VMEM)
    deps = [] if dep is None else [dep]
    return pl.pallas_call(
        body, name=name, out_shape=jax.ShapeDtypeStruct((N_DEV,) + shard.shape, shard.dtype),
        in_specs=[spec] * (1 + len(deps)), out_specs=spec,
        scratch_shapes=[pltpu.SemaphoreType.DMA((4,)), pltpu.SemaphoreType.DMA((4,)), pltpu.SemaphoreType.DMA],
        compiler_params=_params(),
    )(shard, *deps)


def _slot_copies(refs, send_sems, recv_sems, plan):
    copies = []
    for k, ((px, py, pc), to) in enumerate(plan):
        blk = refs[0].at[4 * px + 2 * py + pc]
        copies.append(pltpu.make_async_remote_copy(
            src_ref=blk, dst_ref=blk, send_sem=send_sems.at[k], recv_sem=recv_sems.at[k],
            device_id=to, device_id_type=MESH))
    return copies


def _second_axis_stage_copies(refs, send_sems, recv_sems):
    x, y, c = _place()
    first, second = (*_first_axis_chip(x, y, c), c), (*_second_axis_chip(x, y, c), c)
    return _slot_copies(refs, send_sems, recv_sems, [((x, y, c), second), (first, (x, y, 1 - c)), (first, second)])


def _second_axis_forward_copies(refs, send_sems, recv_sems):
    x, y, c = _place()
    return _slot_copies(refs, send_sems, recv_sems, [((*_second_axis_chip(x, y, c), c), (x, y, 1 - c))])


def _diagonal_forward_copies(refs, send_sems, recv_sems):
    x, y, c = _place()
    blk = refs[0].at[4 * (1 - x) + 2 * (1 - y) + c]
    return [pltpu.make_async_remote_copy(
        src_ref=blk, dst_ref=blk, send_sem=send_sems.at[0], recv_sem=recv_sems.at[0],
        device_id=(x, y, 1 - c), device_id_type=MESH)]


def _with_own_slot(block, me):
    return lax.dynamic_update_index_in_dim(lax.empty((N_DEV,) + block.shape, block.dtype), block, me, 0)


def _w_in_grad(dz, h, dep, tm):
    (t, m), n = dz.shape, h.shape[1]
    assert m % tm == 0 and dz.dtype == BF16 and h.dtype == BF16

    def body(dz_ref, h_ref, dep_ref, o_ref):
        o_ref[...] = _dot_tn(dz_ref[...], h_ref[...]).astype(BF16)

    return pl.pallas_call(
        body, name="w_in_grad", grid=(m // tm,),
        in_specs=[pl.BlockSpec((t, tm), lambda i: (0, i)), pl.BlockSpec((t, n), lambda i: (0, 0)),
                  pl.BlockSpec((8, 128), lambda i: (0, 0))],
        out_specs=pl.BlockSpec((tm, n), lambda i: (i, 0)),
        out_shape=jax.ShapeDtypeStruct((m, n), BF16),
        compiler_params=_params(dimension_semantics=("arbitrary",)),
    )(dz, h, dep)


Z_TILE = 768
_Z_TILE_ORDER = ((0, 1, 2, 3, 4, 5, 6), (2, 0, 1, 6, 3, 4, 5), (4, 0, 5, 6, 1, 2, 3), (6, 2, 3, 4, 0, 1, 5))
_Z_EARLY_TILES = 4


def _z_proj(h, w_in_t, chip, first, count, z_prev, name, tr=1024):
    t = h.shape[0]

    def body(chip_ref, h_ref, w_ref, z_prev_ref, z_ref):
        z_ref[...] = _dot_nt(h_ref[...], w_ref[...])

    def tile(j, chip_ref):
        picked = 0
        for c, order in enumerate(_Z_TILE_ORDER):
            for k in range(count):
                picked = picked + jnp.where((chip_ref[0] == c) & (j == k), order[first + k], 0)
        return picked

    return pl.pallas_call(
        body, name=name,
        grid_spec=pltpu.PrefetchScalarGridSpec(
            num_scalar_prefetch=1, grid=(count, t // tr),
            in_specs=[pl.BlockSpec((tr, D_MODEL), lambda j, i, o: (i, 0)),
                      pl.BlockSpec((Z_TILE, D_MODEL), lambda j, i, o: (tile(j, o), 0)),
                      pl.BlockSpec(memory_space=pl.ANY)],
            out_specs=pl.BlockSpec((tr, Z_TILE), lambda j, i, o: (i, tile(j, o)))),
        out_shape=jax.ShapeDtypeStruct((t, D_IN), F32),
        input_output_aliases={3: 0},
        compiler_params=_params(dimension_semantics=("arbitrary", "arbitrary")),
    )(chip, h, w_in_t, z_prev)


def _modulation(device, c_all, w_ada, b_ada):
    def body(device_ref, c_ref, w_ref, b_ref, act_ref, mod_ref):
        cv = c_ref[...]
        act = cv * _sigmoid(cv)
        act_ref[...] = act
        mod_ref[...] = jnp.dot(act.astype(BF16), w_ref[...].astype(BF16), preferred_element_type=F32) + b_ref[...]

    whole = lambda a: pl.BlockSpec(a.shape, lambda i, device_ref: (0,) * a.ndim)
    return pl.pallas_call(
        body, name="modulation",
        grid_spec=pltpu.PrefetchScalarGridSpec(
            num_scalar_prefetch=1, grid=(1,),
            in_specs=[whole(c_all), whole(w_ada), pl.BlockSpec((1, W_ADA_SHARD), lambda i, device_ref: (0, device_ref[0]))],
            out_specs=(whole(c_all), pl.BlockSpec((N_DEV, W_ADA_SHARD), lambda i, device_ref: (0, 0)))),
        out_shape=(jax.ShapeDtypeStruct(c_all.shape, F32), jax.ShapeDtypeStruct((N_DEV, W_ADA_SHARD), F32)),
        compiler_params=_params(dimension_semantics=("arbitrary",)),
    )(device, c_all, w_ada, b_ada)


MOD_SHIFT, MOD_SCALE, MOD_GATE = 0, 1, 2


def _mod_spec(part, d):
    return pl.BlockSpec((1, d), lambda i: (0, part))


def _norm_z_proj_own(x, norm_g, mod, w_in_t, chip, tm=512):
    t, d = x.shape

    def body(chip_ref, x_ref, g_ref, sc_ref, sh_ref, w_ref, h_ref, z_ref):
        xv = x_ref[...]
        r = lax.rsqrt(jnp.mean(xv * xv, axis=-1, keepdims=True) + EPS)
        h = ((xv * r) * g_ref[...] * (1.0 + sc_ref[...]) + sh_ref[...]).astype(BF16)
        h_ref[...] = h
        z_ref[...] = _dot_nt(h, w_ref[...])

    def own_tile(chip_ref):
        picked = 0
        for c, order in enumerate(_Z_TILE_ORDER):
            picked = picked + jnp.where(chip_ref[0] == c, order[0], 0)
        return picked

    def row(part):
        return pl.BlockSpec((1, d), lambda i, o: (0, part))

    return pl.pallas_call(
        body, name="norm_z_proj_own",
        grid_spec=pltpu.PrefetchScalarGridSpec(
            num_scalar_prefetch=1, grid=(t // tm,),
            in_specs=[pl.BlockSpec((tm, d), lambda i, o: (i, 0)), row(0), row(MOD_SCALE), row(MOD_SHIFT),
                      pl.BlockSpec((Z_TILE, d), lambda i, o: (own_tile(o), 0))],
            out_specs=(pl.BlockSpec((tm, d), lambda i, o: (i, 0)),
                       pl.BlockSpec((tm, Z_TILE), lambda i, o: (i, own_tile(o))))),
        out_shape=(jax.ShapeDtypeStruct((t, d), BF16), jax.ShapeDtypeStruct((t, D_IN), F32)),
        compiler_params=_params(dimension_semantics=("arbitrary",)),
    )(chip, x, norm_g, mod, mod, w_in_t)


def _window_bias(block_index):
    s = lax.broadcasted_iota(jnp.int32, (2 * BLOCK, BLOCK), 0)
    t = lax.broadcasted_iota(jnp.int32, (2 * BLOCK, BLOCK), 1)
    valid = ((s < BLOCK) & (s > t) & (block_index > 0)) | ((s >= BLOCK) & ((s - BLOCK) <= t))
    bias = jnp.where(valid, 0.0, -jnp.inf).astype(F32)
    return jnp.concatenate([bias] * 8, axis=1)


def _heads_t(pair_blocks, g):
    top = lax.broadcasted_iota(jnp.int32, (BLOCK, BLOCK), 0) < HEAD_DIM
    zeros = jnp.zeros((HEAD_DIM, BLOCK), F32)
    tiles = []
    for blk in pair_blocks:
        tp = blk.T
        if g == 0:
            tiles += [jnp.where(top, tp, 0.0), jnp.concatenate([tp[HEAD_DIM:], zeros], axis=0)]
        else:
            tiles += [jnp.concatenate([zeros, tp[:HEAD_DIM]], axis=0), jnp.where(top, 0.0, tp)]
    return jnp.concatenate(tiles, axis=1)


def _pair_block(xt, p, g):
    r0 = HEAD_DIM * g
    even = xt[r0:r0 + HEAD_DIM, (2 * p) * BLOCK:(2 * p + 1) * BLOCK]
    odd = xt[r0:r0 + HEAD_DIM, (2 * p + 1) * BLOCK:(2 * p + 2) * BLOCK]
    return jnp.concatenate([even, odd], axis=0).T


def _softmax_t(scores_t, bias, sink):
    st = scores_t + bias
    m = jnp.maximum(jnp.max(st, axis=0, keepdims=True), sink)
    e = jnp.exp(st - m)
    es = jnp.exp(sink - m)
    inv = 1.0 / (jnp.sum(e, axis=0, keepdims=True) + es)
    return e * inv, es * inv


def _dot(a, b):
    return jnp.dot(a, b, preferred_element_type=F32)


def _dot_nt(a, b):
    return lax.dot_general(a, b, (((1,), (1,)), ((), ())), preferred_element_type=F32)


def _dot_tn(a, b):
    return lax.dot_general(a, b, (((0,), (0,)), ((), ())), preferred_element_type=F32)


def _layer_norm_fwd(v):
    mu = jnp.mean(v, axis=-1, keepdims=True)
    xc = v - mu
    rstd = lax.rsqrt(jnp.mean(xc * xc, axis=-1, keepdims=True) + EPS)
    return xc * rstd, rstd


def _tril(transposed=False):
    t = lax.broadcasted_iota(jnp.int32, (BLOCK, BLOCK), 0)
    s = lax.broadcasted_iota(jnp.int32, (BLOCK, BLOCK), 1)
    return s >= t if transposed else t >= s


def _const_spec(shape):
    return pl.BlockSpec(shape, lambda i: (0,) * len(shape))


def _keys_values(z_ref, kvp):
    kvc = z_ref[:, SEG_KV:SEG_KV + 2 * D_KV]
    kk = jnp.concatenate([kvp[:, :D_KV], kvc[:, :D_KV]], axis=0)
    vv = jnp.concatenate([kvp[:, D_KV:], kvc[:, D_KV:]], axis=0)
    return kk, vv


MIXER_BLOCKS = 2


class _Rows:
    def __init__(self, ref, sub):
        self.ref, self.rows = ref, slice(sub * BLOCK, (sub + 1) * BLOCK)

    def __getitem__(self, idx):
        return self.ref[self.rows, idx[1]]

    def __setitem__(self, idx, value):
        self.ref[self.rows, idx[1]] = value


def _kv_before_spec(index):
    return pl.BlockSpec((BLOCK, 2 * D_KV),
                        lambda i: (jnp.maximum(MIXER_BLOCKS * index(i) - 1, 0), SEG_KV // (2 * D_KV)))


def _pair_cols(g, p, base=0):
    return slice(base + (4 * g + p) * 128, base + (4 * g + p + 1) * 128)


def _mixer_fwd(z, sink_rows, ln_g, ln_b, sgu_w, sgu_bt):
    t = z.shape[0]

    def body(z_all, kvp_ref, sink_ref, lng_ref, lnb_ref, w_ref, bt_ref, a_all, prob_ref, sink_prob_ref):
        kv_before = kvp_ref[...]
        for sub in range(MIXER_BLOCKS):
            z_ref, a_ref = _Rows(z_all, sub), _Rows(a_all, sub)
            one_block(z_ref, kv_before, MIXER_BLOCKS * pl.program_id(0) + sub, sink_ref, lng_ref, lnb_ref, w_ref,
                      bt_ref, a_ref, prob_ref.at[sub], sink_prob_ref.at[sub])
            kv_before = z_ref[:, SEG_KV:SEG_KV + 2 * D_KV]

    def one_block(z_ref, kv_before, block_index, sink_ref, lng_ref, lnb_ref, w_ref, bt_ref, a_ref, prob_ref,
                  sink_prob_ref):
        bias = _window_bias(block_index)
        kk, vv = _keys_values(z_ref, kv_before)
        kk_b, vvt_b = kk.astype(BF16), vv.T.astype(BF16)
        for g in range(2):
            qt = _heads_t([z_ref[:, _pair_cols(g, p, SEG_Q)] * ATTN_SCALE for p in range(4)], g).astype(BF16)
            prob, sink_prob = _softmax_t(_dot(kk_b, qt), bias, sink_ref[g])
            prob_b = prob.astype(BF16)
            prob_ref[g] = prob_b
            sink_prob_ref[g] = sink_prob
            ot = _dot(vvt_b, prob_b)
            for p in range(4):
                gate = z_ref[:, _pair_cols(g, p, SEG_GA)]
                a_ref[:, _pair_cols(g, p)] = (_pair_block(ot, p, g) * (gate * _sigmoid(gate))).astype(BF16)

        vhat, _ = _layer_norm_fwd(z_ref[:, SEG_VS:SEG_VS + D_SGU])
        vn = vhat * lng_ref[...] + lnb_ref[...]
        tril = _tril()
        for g in range(SGU_GROUPS):
            cols = slice(g * 128, (g + 1) * 128)
            wm = jnp.where(tril, w_ref[g], 0.0).astype(BF16)
            mixed = _dot(wm, vn[:, cols].astype(BF16)) + bt_ref[:, g:g + 1]
            gate = z_ref[:, SEG_GS + g * 128:SEG_GS + (g + 1) * 128]
            a_ref[:, D_ATTN + g * 128:D_ATTN + (g + 1) * 128] = (
                (z_ref[:, SEG_U + g * 128:SEG_U + (g + 1) * 128] * mixed) * (gate * _sigmoid(gate))).astype(BF16)

    rows = MIXER_BLOCKS * BLOCK
    return pl.pallas_call(
        body, name="mixer_fwd", grid=(t // rows,),
        in_specs=[pl.BlockSpec((rows, D_IN), lambda i: (i, 0)), _kv_before_spec(lambda i: i),
                  _const_spec((2, 1, 8 * BLOCK)), _const_spec((1, D_SGU)), _const_spec((1, D_SGU)),
                  _const_spec((SGU_GROUPS, BLOCK, BLOCK)), _const_spec((BLOCK, SGU_GROUPS))],
        out_specs=(pl.BlockSpec((rows, D_MODEL), lambda i: (i, 0)),
                   pl.BlockSpec((MIXER_BLOCKS, 2, 2 * BLOCK, 8 * BLOCK), lambda i: (i, 0, 0, 0)),
                   pl.BlockSpec((MIXER_BLOCKS, 2, 1, 8 * BLOCK), lambda i: (i, 0, 0, 0))),
        out_shape=(jax.ShapeDtypeStruct((t, D_MODEL), BF16),
                   jax.ShapeDtypeStruct((t // BLOCK, 2, 2 * BLOCK, 8 * BLOCK), BF16),
                   jax.ShapeDtypeStruct((t // BLOCK, 2, 1, 8 * BLOCK), F32)),
        compiler_params=_params(dimension_semantics=("arbitrary",)),
    )(z, z, sink_rows, ln_g, ln_b, sgu_w, sgu_bt)


def _mixer_bwd(z, da, probs, sink_probs, ln_g, ln_b, sgu_w, sgu_wt, sgu_bt, a, dy):
    t = z.shape[0]

    def body(z_all, kvp_ref, da_all, prob_ref, sink_prob_ref, lng_ref, lnb_ref, w_ref, wt_ref, bt_ref, a_ref, dy_ref,
             dz_all, dsink_ref, dw_ref, db_ref, dlng_ref, dlnb_ref, dw_out_ref, carry_ref, dsink_acc, dbt_acc):
        step = pl.program_id(0)

        @pl.when(step == 0)
        def _():
            carry_ref[...] = jnp.zeros_like(carry_ref)
            dsink_acc[...] = jnp.zeros_like(dsink_acc)
            dbt_acc[...] = jnp.zeros_like(dbt_acc)
            dw_ref[...] = jnp.zeros_like(dw_ref)
            dlng_ref[...] = jnp.zeros_like(dlng_ref)
            dlnb_ref[...] = jnp.zeros_like(dlnb_ref)

        carry = carry_ref[...]
        dw_out_ref[...] = _dot_tn(a_ref[...], dy_ref[...]).astype(BF16)
        for sub in reversed(range(MIXER_BLOCKS)):
            kv_before = kvp_ref[...] if sub == 0 else _Rows(z_all, sub - 1)[:, SEG_KV:SEG_KV + 2 * D_KV]
            carry = one_block(_Rows(z_all, sub), kv_before, _Rows(da_all, sub), prob_ref.at[sub], sink_prob_ref.at[sub],
                              carry, lng_ref, lnb_ref, w_ref, wt_ref, bt_ref, _Rows(dz_all, sub),
                              dw_ref, dlng_ref, dlnb_ref, dsink_acc, dbt_acc)
        carry_ref[...] = carry

        @pl.when(step == ns - 1)
        def _():
            db_ref[...] = dbt_acc[...].T[:SGU_GROUPS]
            lane_row = lax.broadcasted_iota(jnp.int32, (1, 128), 1)
            d_sink = jnp.zeros((1, 128), F32)
            for g in range(2):
                acc = dsink_acc[g]
                for j in range(8):
                    head_sum = jnp.sum(acc[:, j * BLOCK:(j + 1) * BLOCK], axis=-1, keepdims=True)
                    d_sink = d_sink + jnp.where(lane_row == 8 * g + j, head_sum, 0.0)
            dsink_ref[...] = d_sink

    def one_block(z_ref, kv_before, da_ref, prob_ref, sink_prob_ref, carry, lng_ref, lnb_ref, w_ref, wt_ref, bt_ref,
                  dz_ref, dw_ref, dlng_ref, dlnb_ref, dsink_acc, dbt_acc):
        kk, vv = _keys_values(z_ref, kv_before)
        vv_b = vv.astype(BF16)
        kkt_b, vvt_b = kk.T.astype(BF16), vv.T.astype(BF16)
        dkk = jnp.zeros((2 * BLOCK, D_KV), F32)
        dvv = jnp.zeros((2 * BLOCK, D_KV), F32)
        for g in range(2):
            qt = _heads_t([z_ref[:, _pair_cols(g, p, SEG_Q)] * ATTN_SCALE for p in range(4)], g).astype(BF16)
            prob_b, sink_prob = prob_ref[g], sink_prob_ref[g]
            prob = prob_b.astype(F32)
            ot = _dot(vvt_b, prob_b)
            gates = [z_ref[:, _pair_cols(g, p, SEG_GA)] for p in range(4)]
            sig = [_sigmoid(gt) for gt in gates]
            d_attn = [da_ref[:, _pair_cols(g, p)] for p in range(4)]
            d_ot = _heads_t([d_attn[p] * (gates[p] * sig[p]) for p in range(4)], g).astype(BF16)
            d_prob = _dot(vv_b, d_ot)
            delta = jnp.sum(prob * d_prob, axis=0, keepdims=True)
            d_scores = (prob * (d_prob - delta)).astype(BF16)
            dsink_acc[g] -= sink_prob * delta
            d_qt = _dot(kkt_b, d_scores)
            dkk = dkk + _dot_nt(d_scores, qt)
            dvv = dvv + _dot_nt(prob_b, d_ot)
            for p in range(4):
                dz_ref[:, _pair_cols(g, p, SEG_Q)] = (_pair_block(d_qt, p, g) * ATTN_SCALE).astype(BF16)
                d_silu = sig[p] * (1.0 + gates[p] * (1.0 - sig[p]))
                dz_ref[:, _pair_cols(g, p, SEG_GA)] = (d_attn[p] * _pair_block(ot, p, g) * d_silu).astype(BF16)
        d_kv = jnp.concatenate([dkk, dvv], axis=1)
        dz_ref[:, SEG_KV:SEG_KV + 2 * D_KV] = (d_kv[BLOCK:] + carry).astype(BF16)

        vhat, rstd = _layer_norm_fwd(z_ref[:, SEG_VS:SEG_VS + D_SGU])
        lng = lng_ref[...]
        vn = vhat * lng + lnb_ref[...]
        tril, triu = _tril(), _tril(transposed=True)
        lane = lax.broadcasted_iota(jnp.int32, (BLOCK, 128), 1)
        d_bt = jnp.zeros((BLOCK, 128), F32)
        d_vn = []
        for g in range(SGU_GROUPS):
            cols = slice(g * 128, (g + 1) * 128)
            wm = jnp.where(tril, w_ref[g], 0.0).astype(BF16)
            wmt = jnp.where(triu, wt_ref[g], 0.0).astype(BF16)
            vn_g = vn[:, cols].astype(BF16)
            mixed = _dot(wm, vn_g) + bt_ref[:, g:g + 1]
            gate = z_ref[:, SEG_GS + g * 128:SEG_GS + (g + 1) * 128]
            u = z_ref[:, SEG_U + g * 128:SEG_U + (g + 1) * 128]
            d_out = da_ref[:, D_ATTN + g * 128:D_ATTN + (g + 1) * 128]
            sg = _sigmoid(gate)
            d_um = d_out * (gate * sg)
            dz_ref[:, SEG_U + g * 128:SEG_U + (g + 1) * 128] = (d_um * mixed).astype(BF16)
            dz_ref[:, SEG_GS + g * 128:SEG_GS + (g + 1) * 128] = (
                d_out * (u * mixed) * (sg * (1.0 + gate * (1.0 - sg)))).astype(BF16)
            d_mixed = d_um * u
            d_mixed_b = d_mixed.astype(BF16)
            dw_ref[g] += jnp.where(tril, _dot_nt(d_mixed_b, vn_g), 0.0)
            d_bt = d_bt + jnp.where(lane == g, jnp.sum(d_mixed, axis=-1, keepdims=True), 0.0)
            d_vn.append(_dot(wmt, d_mixed_b))
        dbt_acc[...] += d_bt
        d_vn = jnp.concatenate(d_vn, axis=1)
        dlng_ref[...] += jnp.sum(d_vn * vhat, axis=0, keepdims=True)
        dlnb_ref[...] += jnp.sum(d_vn, axis=0, keepdims=True)
        d_vhat = d_vn * lng
        d_v = rstd * (d_vhat - jnp.mean(d_vhat, axis=-1, keepdims=True)
                      - vhat * jnp.mean(d_vhat * vhat, axis=-1, keepdims=True))
        dz_ref[:, SEG_VS:SEG_VS + D_SGU] = d_v.astype(BF16)
        return d_kv[:BLOCK]

    rows = MIXER_BLOCKS * BLOCK
    ns = t // rows
    rev = lambda i: ns - 1 - i
    tile = a.shape[1] // ns
    assert tile % 128 == 0
    return pl.pallas_call(
        body, name="mixer_bwd", grid=(ns,),
        in_specs=[pl.BlockSpec((rows, D_IN), lambda i: (rev(i), 0)), _kv_before_spec(rev),
                  pl.BlockSpec((rows, D_MODEL), lambda i: (rev(i), 0)),
                  pl.BlockSpec((MIXER_BLOCKS, 2, 2 * BLOCK, 8 * BLOCK), lambda i: (rev(i), 0, 0, 0)),
                  pl.BlockSpec((MIXER_BLOCKS, 2, 1, 8 * BLOCK), lambda i: (rev(i), 0, 0, 0)),
                  _const_spec((1, D_SGU)), _const_spec((1, D_SGU)),
                  _const_spec((SGU_GROUPS, BLOCK, BLOCK)), _const_spec((SGU_GROUPS, BLOCK, BLOCK)),
                  _const_spec((BLOCK, SGU_GROUPS)), pl.BlockSpec((t, tile), lambda i: (0, i)),
                  pl.BlockSpec(dy.shape, lambda i: (0, 0), pipeline_mode=pl.Buffered(1))],
        out_specs=(pl.BlockSpec((rows, D_IN), lambda i: (rev(i), 0)), _const_spec((1, 128)),
                   _const_spec((SGU_GROUPS, BLOCK, BLOCK)), _const_spec((SGU_GROUPS, BLOCK)),
                   _const_spec((1, D_SGU)), _const_spec((1, D_SGU)),
                   pl.BlockSpec((tile, dy.shape[1]), lambda i: (i, 0))),
        out_shape=(jax.ShapeDtypeStruct((t, D_IN), BF16), jax.ShapeDtypeStruct((1, 128), F32),
                   jax.ShapeDtypeStruct((SGU_GROUPS, BLOCK, BLOCK), F32), jax.ShapeDtypeStruct((SGU_GROUPS, BLOCK), F32),
                   jax.ShapeDtypeStruct((1, D_SGU), F32), jax.ShapeDtypeStruct((1, D_SGU), F32),
                   jax.ShapeDtypeStruct((a.shape[1], dy.shape[1]), BF16)),
        scratch_shapes=[pltpu.VMEM((BLOCK, 2 * D_KV), F32), pltpu.VMEM((2, 1, 8 * BLOCK), F32),
                        pltpu.VMEM((BLOCK, 128), F32)],
        compiler_params=_params(dimension_semantics=("arbitrary",)),
    )(z, z, da, probs, sink_probs, ln_g, ln_b, sgu_w, sgu_wt, sgu_bt, a, dy)


def _out_proj_head(a, w_out_full, x, target, mod, final_g, tm=256):
    t, d = x.shape

    def body(a_ref, w_ref, x_ref, tg_ref, gate_ref, fg_ref, dx2_ref, dy_ref, da_ref, loss_ref, dfg_ref, dgate_ref):
        @pl.when(pl.program_id(0) == 0)
        def _():
            loss_ref[...] = jnp.zeros_like(loss_ref)
            dfg_ref[...] = jnp.zeros_like(dfg_ref)
            dgate_ref[...] = jnp.zeros_like(dgate_ref)

        yv, gate, fg = _dot(a_ref[...], w_ref[...]), gate_ref[...], fg_ref[...]
        x2 = x_ref[...] + gate * yv
        r2 = lax.rsqrt(jnp.mean(x2 * x2, axis=-1, keepdims=True) + EPS)
        nrm = x2 * r2
        err = nrm * fg - tg_ref[...]
        loss_ref[...] += 0.5 * jnp.sum(jnp.mean(err * err, axis=-1, keepdims=True), axis=0, keepdims=True)
        fg_d = fg * (1.0 / d)
        err_nrm = err * nrm
        dfg_ref[...] += jnp.sum(err_nrm, axis=0, keepdims=True) * (1.0 / d)
        d_nrm = err * fg_d
        dx2 = r2 * (d_nrm - nrm * jnp.mean(err_nrm * fg_d, axis=-1, keepdims=True))
        dx2_ref[...] = dx2
        dgate_ref[...] += jnp.sum(dx2 * yv, axis=0, keepdims=True)
        dy = (dx2 * gate).astype(BF16)
        dy_ref[...] = dy
        da_ref[...] = _dot_nt(dy, w_ref[...])

    blk = pl.BlockSpec((tm, d), lambda i: (i, 0))
    a_blk = pl.BlockSpec((tm, a.shape[1]), lambda i: (i, 0))
    row = _const_spec((1, d))
    whole = pl.BlockSpec(w_out_full.shape, lambda i: (0, 0), pipeline_mode=pl.Buffered(1))
    return pl.pallas_call(
        body, name="out_proj_head", grid=(t // tm,),
        in_specs=[a_blk, whole, blk, blk, _mod_spec(MOD_GATE, d), row],
        out_specs=(blk, blk, a_blk, _const_spec((1, 128)), row, row),
        out_shape=(jax.ShapeDtypeStruct((t, d), F32), jax.ShapeDtypeStruct((t, d), BF16),
                   jax.ShapeDtypeStruct(a.shape, F32), jax.ShapeDtypeStruct((1, 128), F32),
                   jax.ShapeDtypeStruct((1, d), F32), jax.ShapeDtypeStruct((1, d), F32)),
        compiler_params=_params(dimension_semantics=("arbitrary",)),
    )(a, w_out_full, x, target, mod, final_g)


def _z_proj_bwd_norm(dz, w_in_t, x, dx2, norm_g, mod, dep, tm=256):
    t, d = x.shape

    def body(dz_ref, w_ref, x_ref, dx2_ref, g_ref, sc_ref, dep_ref, gx_ref, dshift_ref, dscale_ref, dg_ref):
        @pl.when(pl.program_id(0) == 0)
        def _():
            dshift_ref[...] = jnp.zeros_like(dshift_ref)
            dscale_ref[...] = jnp.zeros_like(dscale_ref)
            dg_ref[...] = jnp.zeros_like(dg_ref)

        dh, xv, g = _dot(dz_ref[...], w_ref[...]), x_ref[...], g_ref[...]
        one_plus = 1.0 + sc_ref[...]
        r = lax.rsqrt(jnp.mean(xv * xv, axis=-1, keepdims=True) + EPS)
        xn = xv * r
        gain = one_plus * g
        dh_xn = dh * xn
        dh_xn_sum = jnp.sum(dh_xn, axis=0, keepdims=True)
        dshift_ref[...] += jnp.sum(dh, axis=0, keepdims=True)
        dscale_ref[...] += dh_xn_sum * g
        dg_ref[...] += dh_xn_sum * one_plus
        d_xn = dh * gain
        gx_ref[...] = dx2_ref[...] + r * (d_xn - xn * jnp.mean(dh_xn * gain, axis=-1, keepdims=True))

    blk = pl.BlockSpec((tm, d), lambda i: (i, 0))
    row = _const_spec((1, d))
    whole = pl.BlockSpec(w_in_t.shape, lambda i: (0, 0), pipeline_mode=pl.Buffered(1))
    return pl.pallas_call(
        body, name="z_proj_bwd_norm", grid=(t // tm,),
        in_specs=[pl.BlockSpec((tm, dz.shape[1]), lambda i: (i, 0)), whole, blk, blk, row, _mod_spec(MOD_SCALE, d),
                  _const_spec((8, 128))],
        out_specs=(blk, row, row, row),
        out_shape=(jax.ShapeDtypeStruct((t, d), F32),) + (jax.ShapeDtypeStruct((1, d), F32),) * 3,
        compiler_params=_params(dimension_semantics=("arbitrary",)),
    )(dz, w_in_t, x, dx2, norm_g, mod, dep)


def _adamw(w, g, m, v):
    m = ADAM_B1 * m + (1.0 - ADAM_B1) * g
    v = ADAM_B2 * v + (1.0 - ADAM_B2) * (g * g)
    m_hat = m / (1.0 - ADAM_B1 ** ADAM_STEP)
    v_hat = v / (1.0 - ADAM_B2 ** ADAM_STEP)
    delta = -ADAM_LR * (m_hat / (jnp.sqrt(v_hat) + ADAM_EPS) + ADAM_WD * w)
    return delta, m, v


def _relay_sum(device, blocks, land_pair, land_first, tr):
    _, r, c = blocks.shape

    def body(device_ref, a_ref, b_ref, c_ref, o_ref):
        o_ref[...] = (a_ref[...].astype(F32) + b_ref[...].astype(F32) + c_ref[...].astype(F32)).astype(BF16)

    second = pl.BlockSpec((None, tr, c), lambda i, device_ref: (1, i, 0))
    return pl.pallas_call(
        body, name="w_in_grad_relay_sum",
        grid_spec=pltpu.PrefetchScalarGridSpec(
            num_scalar_prefetch=1, grid=(r // tr,),
            in_specs=[pl.BlockSpec((None, tr, c), lambda i, device_ref: (device_ref[0], i, 0)), second, second],
            out_specs=pl.BlockSpec((tr, c), lambda i, device_ref: (i, 0))),
        out_shape=jax.ShapeDtypeStruct((r, c), BF16),
        compiler_params=_params(dimension_semantics=("arbitrary",)),
    )(device, blocks, land_pair, land_first)


def _adam_from_chips(chip, pair, landed, w, m, v, name, tc):
    _, r, c = pair.shape
    n = len(landed)
    tiles = c // tc
    n_in = n + 4

    def body(chip_ref, pair_hbm, *refs):
        landed_hbm, (w_hbm, m_hbm, v_hbm) = refs[:n], refs[n:n + 3]
        outs_hbm = refs[n + 3:n + 7]
        grads, w_in, m_in, v_in, staged, in_sems, out_sems = refs[n + 7:]
        sources = [pair_hbm.at[chip_ref[0]]] + [landed_hbm[k].at[index] for k, (_, index) in enumerate(landed)]

        def reads(i):
            cols = pl.ds(i * tc, tc)
            cps = [pltpu.make_async_copy(src.at[:, cols], grads.at[k, :, cols], in_sems.at[i * n_in + k])
                   for k, src in enumerate(sources)]
            for k, (src, dst) in enumerate([(w_hbm, w_in), (m_hbm, m_in), (v_hbm, v_in)]):
                cps.append(pltpu.make_async_copy(src.at[:, cols], dst.at[:, cols], in_sems.at[i * n_in + n + 1 + k]))
            return cps

        def writes(i):
            cols = pl.ds(i * tc, tc)
            return [pltpu.make_async_copy(staged.at[i % 2, k], outs_hbm[k].at[:, cols], out_sems.at[i * 4 + k])
                    for k in range(4)]

        for i in range(tiles):
            for cp in reads(i):
                cp.start()
        for i in range(tiles):
            cols = pl.ds(i * tc, tc)
            for cp in reads(i):
                cp.wait()
            if i >= 2:
                for cp in writes(i - 2):
                    cp.wait()
            g = grads[0, :, cols].astype(F32)
            for k in range(1, n + 1):
                g = g + grads[k, :, cols].astype(F32)
            delta, new_m, new_v = _adamw(w_in[:, cols], g, m_in[:, cols], v_in[:, cols])
            for k, value in enumerate([g, delta, new_m, new_v]):
                staged[i % 2, k] = value
            for cp in writes(i):
                cp.start()
        for i in range(max(tiles - 2, 0), tiles):
            for cp in writes(i):
                cp.wait()

    any_spec = pl.BlockSpec(memory_space=pl.ANY)
    return pl.pallas_call(
        body, name=name,
        grid_spec=pltpu.PrefetchScalarGridSpec(
            num_scalar_prefetch=1, grid=(1,), in_specs=[any_spec] * (n + 4), out_specs=(any_spec,) * 4,
            scratch_shapes=[pltpu.VMEM((n + 1, r, c), BF16), pltpu.VMEM((r, c), F32), pltpu.VMEM((r, c), F32),
                            pltpu.VMEM((r, c), F32), pltpu.VMEM((2, 4, r, tc), F32),
                            pltpu.SemaphoreType.DMA((tiles * n_in,)), pltpu.SemaphoreType.DMA((tiles * 4,))]),
        out_shape=(jax.ShapeDtypeStruct((r, c), F32),) * 4,
        compiler_params=_params(dimension_semantics=("arbitrary",)),
    )(chip, pair, *[array for array, _ in landed], w, m, v)


def _adam_w_ada(device, act_t, dmod_all, w, m, v, tr=512):
    r, c = w.shape

    def body(device_ref, a_ref, dm_ref, w_ref, m_ref, v_ref, g_ref, d_ref, nm_ref, nv_ref):
        g = _dot(a_ref[...].astype(BF16), dm_ref[...].astype(BF16))
        g_ref[...] = g
        d_ref[...], nm_ref[...], nv_ref[...] = _adamw(w_ref[...], g, m_ref[...], v_ref[...])

    blk = pl.BlockSpec((tr, c), lambda i, device_ref: (i, 0))
    return pl.pallas_call(
        body, name="adam_w_ada",
        grid_spec=pltpu.PrefetchScalarGridSpec(
            num_scalar_prefetch=1, grid=(r // tr,),
            in_specs=[pl.BlockSpec((tr, N_DEV), lambda i, device_ref: (i, 0)),
                      pl.BlockSpec((N_DEV, c), lambda i, device_ref: (0, device_ref[0])), blk, blk, blk],
            out_specs=(blk,) * 4),
        out_shape=(jax.ShapeDtypeStruct((r, c), F32),) * 4,
        compiler_params=_params(dimension_semantics=("arbitrary",)),
    )(device, act_t, dmod_all, w, m, v)


def _pack_small(d_shift, d_scale, d_gate, d_norm_g, d_final_g, d_ln_g, d_ln_b, loss, d_sinks, d_sgu_b):
    def body(shift_ref, scale_ref, gate_ref, ng_ref, fg_ref, lng_ref, lnb_ref, loss_ref, sink_ref, b_ref, o_ref):
        o_ref[...] = jnp.zeros_like(o_ref)
        o_ref[ROW_SHIFT:ROW_SHIFT + 1, :] = shift_ref[...]
        o_ref[ROW_SCALE:ROW_SCALE + 1, :] = scale_ref[...]
        o_ref[ROW_GATE:ROW_GATE + 1, :] = gate_ref[...]
        o_ref[ROW_NORM_G:ROW_NORM_G + 1, :] = ng_ref[...]
        o_ref[ROW_FINAL_G:ROW_FINAL_G + 1, :] = fg_ref[...]
        o_ref[ROW_LN:ROW_LN + 1, 0:D_SGU] = lng_ref[...]
        o_ref[ROW_LN:ROW_LN + 1, D_SGU:2 * D_SGU] = lnb_ref[...]
        o_ref[ROW_MISC:ROW_MISC + 1, 0:128] = loss_ref[...]
        o_ref[ROW_MISC:ROW_MISC + 1, 128:256] = sink_ref[...]
        o_ref[ROW_SGU_B:ROW_SGU_B + SGU_GROUPS, 0:BLOCK] = b_ref[...]

    return pl.pallas_call(
        body, name="pack_small", out_shape=jax.ShapeDtypeStruct((SMALL_ROWS, D_MODEL), F32),
        compiler_params=_params(),
    )(d_shift, d_scale, d_gate, d_norm_g, d_final_g, d_ln_g, d_ln_b, loss, d_sinks, d_sgu_b)


_SMALL_NAMES = ("norm_g", "b_ada", "attn_sinks", "sgu_ln_g", "sgu_ln_b", "sgu_w", "sgu_b", "final_g")


def _adam_small(partials, d_sgu_w_all, weights, moments_m, moments_v):
    names = _SMALL_NAMES
    k = len(names)

    def body(*refs):
        p_ref, sw_ref = refs[0], refs[1]
        w_refs, m_refs, v_refs = refs[2:2 + k], refs[2 + k:2 + 2 * k], refs[2 + 2 * k:2 + 3 * k]
        loss_ref, dmod_ref = refs[2 + 3 * k], refs[3 + 3 * k]
        out_refs = refs[4 + 3 * k:4 + 7 * k]
        sum_ref = refs[4 + 7 * k]
        total = p_ref[0]
        for j in range(1, N_DEV):
            total = total + p_ref[j]
        sum_ref[...] = total
        for j in range(N_DEV):
            for part, row in enumerate((ROW_SHIFT, ROW_SCALE, ROW_GATE)):
                dmod_ref[j:j + 1, part * D_MODEL:(part + 1) * D_MODEL] = p_ref[j, row:row + 1, :]
        loss_ref[...] = sum_ref[ROW_MISC:ROW_MISC + 1, 0:1]
        d_sgu_w = sw_ref[0]
        for j in range(1, N_DEV):
            d_sgu_w = d_sgu_w + sw_ref[j]
        grads = {
            "norm_g": sum_ref[ROW_NORM_G:ROW_NORM_G + 1, :],
            "b_ada": jnp.concatenate([sum_ref[r:r + 1, :] for r in (ROW_SHIFT, ROW_SCALE, ROW_GATE)], axis=1),
            "attn_sinks": sum_ref[ROW_MISC:ROW_MISC + 1, 128:128 + N_Q_HEADS],
            "sgu_ln_g": sum_ref[ROW_LN:ROW_LN + 1, 0:D_SGU],
            "sgu_ln_b": sum_ref[ROW_LN:ROW_LN + 1, D_SGU:2 * D_SGU],
            "sgu_w": d_sgu_w[None],
            "sgu_b": sum_ref[ROW_SGU_B:ROW_SGU_B + SGU_GROUPS, 0:BLOCK][None],
            "final_g": sum_ref[ROW_FINAL_G:ROW_FINAL_G + 1, :],
        }
        for i, name in enumerate(names):
            g = grads[name]
            delta, m, v = _adamw(w_refs[i][...], g, m_refs[i][...], v_refs[i][...])
            out_refs[4 * i][...] = g
            out_refs[4 * i + 1][...] = delta
            out_refs[4 * i + 2][...] = m
            out_refs[4 * i + 3][...] = v

    shapes = [jax.ShapeDtypeStruct((1, 1), F32), jax.ShapeDtypeStruct((N_DEV, 3 * D_MODEL), F32)]
    for name in names:
        shapes += [jax.ShapeDtypeStruct(weights[name].shape, F32)] * 4
    outs = pl.pallas_call(
        body, name="adam_small", out_shape=tuple(shapes),
        scratch_shapes=[pltpu.VMEM((SMALL_ROWS, D_MODEL), F32)],
        compiler_params=_params(),
    )(partials, d_sgu_w_all, *[weights[n] for n in names], *[moments_m[n] for n in names],
      *[moments_v[n] for n in names])
    return outs[0], outs[1], {name: outs[2 + 4 * i:6 + 4 * i] for i, name in enumerate(names)}


def kernel(x, c, norm_g, w_ada, b_ada, w_in, attn_sinks, sgu_ln_g, sgu_ln_b, sgu_w, sgu_b, w_out, final_g, loss_target, m_norm_g, m_w_ada, m_b_ada, m_w_in, m_attn_sinks, m_sgu_ln_g, m_sgu_ln_b, m_sgu_w, m_sgu_b, m_w_out, m_final_g, v_norm_g, v_w_ada, v_b_ada, v_w_in, v_attn_sinks, v_sgu_ln_g, v_sgu_ln_b, v_sgu_w, v_sgu_b, v_w_out, v_final_g):
    xi, yi, ci = _place()
    me = 4 * xi + 2 * yi + ci
    x2d, target = x[0], loss_target[0]

    core = ci.astype(jnp.int32).reshape(1)
    chip = (2 * xi + yi).astype(jnp.int32).reshape(1)

    first = _own_block_copies(_first_targets)
    first_flight = _start_copies([_with_own_slot(w_in[0].T.astype(BF16), me)], first, 2, core, "gather_w_in_start")

    c_all = _all_gather_small(c.reshape(8, 256), "gather_c", first_flight[3]).reshape(N_DEV, D_MODEL)
    device = me.astype(jnp.int32).reshape(1)
    c_act, mod_part = _modulation(device, c_all, w_ada[0], b_ada)
    mod_all = _all_gather_small(mod_part, "gather_mod")

    across = _wait_then_start(first_flight, lambda *a: first(*a)[1:], _second_axis_stage_copies, 3, mod_all,
                              "gather_w_in_second_axis_stage")
    mod = lax.dynamic_index_in_dim(mod_all, me, axis=1, keepdims=False).reshape(1, 3 * D_MODEL)
    mod = mod + across[3][0, 0]

    w_in_pair = _wait_copies((first_flight[0], first_flight[1], across[2], None), lambda *a: first(*a)[:1], mod,
                             "gather_w_in_sibling_wait")
    h, z_own = _norm_z_proj_own(x2d, norm_g, mod, w_in_pair[0].reshape(D_IN, D_MODEL), chip)
    w_out_early = _own_block_copies(lambda x, y, c: [(x, y, 1 - c), (*_second_axis_chip(x, y, c), c)])
    w_out_late = _own_block_copies(lambda x, y, c: [(*_first_axis_chip(x, y, c), c), (1 - x, 1 - y, c)])
    forward = _wait_then_start(
        (across[0], across[1], w_in_pair, None), lambda *a: _second_axis_stage_copies(*a)[:1],
        lambda refs, s, r: _second_axis_forward_copies(refs[:1], s, r) + _group(w_out_early, 1, 1, 1)(refs, s, r),
        3, z_own, "gather_w_in_second_axis_forward", more_bufs=[_with_own_slot(w_out[0].astype(BF16), me)])
    w_in_most = _wait_copies((across[0], across[1], forward[2][:1], None),
                             lambda *a: _second_axis_stage_copies(*a)[1:2], z_own, "gather_w_in_first_forward_wait")
    w_in_most = _wait_copies((forward[0], forward[1], w_in_most, None), _second_axis_forward_copies, z_own,
                             "gather_w_in_second_forward_wait")
    z_early = _z_proj(h, w_in_most[0].reshape(D_IN, D_MODEL), chip, 1, _Z_EARLY_TILES - 1, z_own, "z_proj_early")
    last = _wait_then_start(
        (across[0], across[1], [w_in_most[0], forward[2][1]], None), lambda *a: _second_axis_stage_copies(*a)[2:],
        lambda refs, s, r: _diagonal_forward_copies(refs[:1], s, r) + _group(w_out_late, 1, 1, 1)(refs, s, r),
        3, z_early, "gather_w_in_last_stage")
    w_in_all = _wait_copies((last[0], last[1], last[2][:1], None), _diagonal_forward_copies, z_early,
                            "gather_w_in_last_wait")[0]
    w_in_t = w_in_all.reshape(D_IN, D_MODEL)
    z = _z_proj(h, w_in_t, chip, _Z_EARLY_TILES, 7 - _Z_EARLY_TILES, z_early, "z_proj_late")
    w_out_half = _wait_copies((forward[0], forward[1], last[2][1:], None), _group(w_out_early, 0, 1, 1), z,
                              "gather_w_out_early_wait")
    w_out_flight = _wait_then_start((last[0], last[1], w_out_half, None), _group(w_out_late, 0, 1, 1),
                                    _forward_copies, 3, z, "gather_w_out_forward_stage")
    sink_rows = jnp.repeat(attn_sinks.reshape(N_Q_HEADS), BLOCK).reshape(2, 1, 8 * BLOCK)
    sgu_bt = sgu_b[0].T
    a, probs, sink_probs = _mixer_fwd(z, sink_rows + w_out_flight[3][0, 0], sgu_ln_g, sgu_ln_b, sgu_w[0], sgu_bt)
    w_out_all = _wait_copies(w_out_flight, _forward_copies, a, "gather_w_out_forward_wait")[0]
    w_out_full = w_out_all.reshape(D_MODEL, D_MODEL)
    final_g_row = final_g.reshape(1, D_MODEL)
    dx2, dy, da, loss_part, d_final_g, d_gate = _out_proj_head(a, w_out_full, x2d, target, mod, final_g_row)

    dz, d_sinks, d_sgu_w, d_sgu_b, d_ln_g, d_ln_b, dw_out = _mixer_bwd(
        z, da, probs, sink_probs, sgu_ln_g, sgu_ln_b, sgu_w[0], jnp.swapaxes(sgu_w[0], 1, 2), sgu_bt, a, dy)
    pair_out = _pair_reduce(dw_out.reshape(4, 2, W_OUT_SHARD, D_MODEL), _every_chip, "w_out_grad_pair_reduce",
                            W_OUT_SHARD // 2)
    sgu_w_to_all = _group(_own_block_copies(_all_others), 2, 1, 3)
    both = _start_copies(
        [pair_out, lax.empty((3, W_OUT_SHARD, D_MODEL), BF16), _with_own_slot(d_sgu_w, me)],
        lambda refs, s, r: _chip_copies(refs[:2], s, r) + sgu_w_to_all(refs, s, r), 3 + N_DEV - 1, core,
        "w_out_grad_chip_and_sgu_w_gather_start")
    out_flight, sgu_w_flight = (both[0], both[1], both[2][:2], None), (both[0], both[1], both[2][2:], None)
    dw_in_t = _w_in_grad(dz, h, both[3], Z_TILE)
    pair_in = _pair_reduce(dw_in_t.reshape(4, 2, W_IN_SHARD, D_MODEL), _first_hop_chips, "w_in_grad_pair_reduce",
                           W_IN_SHARD // 3)
    first_hop = lambda refs, s, r: _first_hop_copies(refs[:2], s, r) + _group(_late_pair_copies, 2, 2, 2)(refs, s, r)
    hop1 = _start_copies(
        [pair_in, lax.empty((2, W_IN_SHARD, D_MODEL), BF16), dw_in_t.reshape(N_DEV, W_IN_SHARD, D_MODEL),
         lax.empty((2, W_IN_SHARD, D_MODEL), BF16)], first_hop, 4, core, "w_in_grad_first_hop_start")
    grad_x, d_shift, d_scale, d_norm_g = _z_proj_bwd_norm(dz, w_in_t, x2d, dx2, norm_g, mod, hop1[3])

    partial = _pack_small(d_shift, d_scale, d_gate, d_norm_g, d_final_g, d_ln_g, d_ln_b, loss_part, d_sinks, d_sgu_b)
    small_flight = _start_copies([_with_own_slot(partial, me)], _own_block_copies(_all_others), N_DEV - 1, core,
                                 "small_grad_gather_start")
    _, land_first, dw_in_t, land_pair = _wait_copies(hop1, first_hop, small_flight[3], "w_in_grad_first_hop_wait")
    second_device = (4 * ((xi + ci) % 2) + 2 * ((yi + 1 - ci) % 2) + ci).astype(jnp.int32).reshape(1)
    relay = _relay_sum(second_device, dw_in_t, land_pair, land_first, W_IN_SHARD // 3)
    hop2 = _start_copies([relay, lax.empty((1, W_IN_SHARD, D_MODEL), BF16)], _second_hop_copies, 1, core,
                         "w_in_grad_second_hop_start")
    pair_out, land_out = _wait_copies(out_flight, _chip_copies, hop2[3], "w_out_grad_chip_wait")
    big = {"w_out": _adam_from_chips(chip, pair_out, [(land_out, k) for k in range(3)], w_out[0], m_w_out[0],
                                     v_w_out[0], "adam_w_out", 1024)}
    partial_all = _wait_copies(small_flight, _own_block_copies(_all_others), big["w_out"][0],
                               "small_grad_gather_wait")[0]
    d_sgu_w_all = _wait_copies(sgu_w_flight, _group(_own_block_copies(_all_others), 0, 1, 3), partial_all,
                               "sgu_w_grad_gather_wait")[0]
    weights = {"norm_g": norm_g, "b_ada": b_ada, "attn_sinks": attn_sinks, "sgu_ln_g": sgu_ln_g,
               "sgu_ln_b": sgu_ln_b, "sgu_w": sgu_w, "sgu_b": sgu_b, "final_g": final_g_row}
    moments_m = {"norm_g": m_norm_g, "b_ada": m_b_ada, "attn_sinks": m_attn_sinks, "sgu_ln_g": m_sgu_ln_g,
                 "sgu_ln_b": m_sgu_ln_b, "sgu_w": m_sgu_w, "sgu_b": m_sgu_b,
                 "final_g": m_final_g.reshape(1, D_MODEL)}
    moments_v = {"norm_g": v_norm_g, "b_ada": v_b_ada, "attn_sinks": v_attn_sinks, "sgu_ln_g": v_sgu_ln_g,
                 "sgu_ln_b": v_sgu_ln_b, "sgu_w": v_sgu_w, "sgu_b": v_sgu_b,
                 "final_g": v_final_g.reshape(1, D_MODEL)}
    loss, dmod_all, small = _adam_small(partial_all, d_sgu_w_all, weights, moments_m, moments_v)
    small["final_g"] = tuple(o.reshape(D_MODEL) for o in small["final_g"])

    big["w_ada"] = _adam_w_ada(device, c_act.T, dmod_all, w_ada[0], m_w_ada[0], v_w_ada[0])
    _, land_second = _wait_copies(hop2, _second_hop_copies, big["w_ada"][0], "w_in_grad_second_hop_wait")
    big["w_in"] = tuple(o.T for o in _adam_from_chips(
        device, dw_in_t, [(land_pair, 0), (land_first, 0), (land_second, 0)], w_in[0].T, m_w_in[0].T, v_w_in[0].T,
        "adam_w_in", 256))
    order = ["norm_g", "w_ada", "b_ada", "w_in", "attn_sinks", "sgu_ln_g", "sgu_ln_b", "sgu_w", "sgu_b", "w_out",
             "final_g"]
    outs = [loss.reshape(()), grad_x[None]]
    for k in range(4):
        for name in order:
            outs.append(big[name][k][None] if name in big else small[name][k])
    return tuple(outs)
```

```python
import jax
import jax.numpy as jnp
from jax import lax
from jax.experimental import pallas as pl
from jax.experimental.pallas import tpu as pltpu

F32 = jnp.float32
BF16 = jnp.bfloat16
MESH = pl.DeviceIdType.MESH

N_DEV = 8
D_MODEL = 2048
HEAD_DIM = 64
D_ATTN = 1024
N_Q_HEADS = 16
D_KV = 128
BLOCK = 128
D_SGU = 1024
SGU_GROUPS = 8
D_IN = 5376
W_IN_SHARD = D_IN // N_DEV
W_OUT_SHARD = D_MODEL // N_DEV
W_ADA_SHARD = 3 * D_MODEL // N_DEV
EPS = 1e-6
ATTN_SCALE = 0.125

ADAM_LR = 0.001
ADAM_B1 = 0.9
ADAM_B2 = 0.999
ADAM_EPS = 1e-08
ADAM_WD = 0.01
ADAM_STEP = 10

SEG_Q, SEG_KV, SEG_GA, SEG_U, SEG_VS, SEG_GS = 0, 1024, 1280, 2304, 3328, 4352

VMEM_LIMIT = 56 * 1024 * 1024

ROW_SHIFT, ROW_SCALE, ROW_GATE, ROW_NORM_G, ROW_FINAL_G, ROW_LN, ROW_MISC, ROW_SGU_B = 0, 1, 2, 3, 4, 5, 6, 8
SMALL_ROWS = 16


def _params(**kw):
    return pltpu.CompilerParams(vmem_limit_bytes=VMEM_LIMIT, **kw)


def _sigmoid(x):
    return 0.5 * (jnp.tanh(0.5 * x) + 1.0)


def _place():
    return lax.axis_index("x"), lax.axis_index("y"), lax.axis_index("c")


def _every_chip(x, y, c):
    return [0, 1, 2, 3]


def _first_hop_chips(x, y, c):
    first = _first_axis_chip(x, y, c)
    return [2 * first[0] + first[1], 2 * (1 - x) + (1 - y)]


def _pair_reduce(blocks, chips, name, row_chunk):
    _, _, r, cols = blocks.shape
    n = len(chips(0, 0, 0))
    assert r % row_chunk == 0

    def body(in_ref, out_ref, land, own, summed, send_sems, recv_sems, own_sems, out_sems):
        x, y, c = _place()
        sends, loads, stores = [], [], []
        for m in range(n):
            cp = pltpu.make_async_remote_copy(
                src_ref=in_ref.at[chips(x, y, 1 - c)[m], 1 - c], dst_ref=land.at[m], send_sem=send_sems.at[m],
                recv_sem=recv_sems.at[m], device_id=(x, y, 1 - c), device_id_type=MESH)
            cp.start()
            sends.append(cp)
            ld = pltpu.make_async_copy(in_ref.at[chips(x, y, c)[m], c], own.at[m], own_sems.at[m])
            ld.start()
            loads.append(ld)
        for m in range(n):
            sends[m].wait_recv()
            loads[m].wait()
            for k in range(r // row_chunk):
                rows = slice(k * row_chunk, (k + 1) * row_chunk)
                summed[m, rows, :] = (own[m, rows, :].astype(F32) + land[m, rows, :].astype(F32)).astype(BF16)
            st = pltpu.make_async_copy(summed.at[m], out_ref.at[m], out_sems.at[m])
            st.start()
            stores.append(st)
        for m in range(n):
            sends[m].wait_send()
            stores[m].wait()

    spec = pl.BlockSpec(memory_space=pl.ANY)
    return pl.pallas_call(
        body, name=name, out_shape=jax.ShapeDtypeStruct((n, r, cols), BF16),
        in_specs=[spec], out_specs=spec,
        scratch_shapes=[pltpu.VMEM((n, r, cols), BF16), pltpu.VMEM((n, r, cols), BF16), pltpu.VMEM((n, r, cols), BF16),
                        pltpu.SemaphoreType.DMA((n,)), pltpu.SemaphoreType.DMA((n,)), pltpu.SemaphoreType.DMA((n,)),
                        pltpu.SemaphoreType.DMA((n,))],
        compiler_params=_params(),
    )(blocks)


_HBM = pl.BlockSpec(memory_space=pltpu.HBM)
_SEM = pl.BlockSpec(memory_space=pltpu.SEMAPHORE)
_EFFECT = pltpu.SideEffectType.DATAFLOW_SIDE_EFFECTING


def _start_copies(bufs, copies, n_copies, after, name):
    nb = len(bufs)

    def body(*refs):
        for cp in copies(refs[:nb], refs[nb + 1], refs[nb + 2]):
            cp.start()
        refs[-1][...] = jnp.zeros_like(refs[-1])

    out = pl.pallas_call(
        body, name=name,
        out_shape=(pltpu.SemaphoreType.DMA((n_copies,)), pltpu.SemaphoreType.DMA((n_copies,)),
                   *[pltpu.HBM(b.shape, b.dtype) for b in bufs], jax.ShapeDtypeStruct((8, 128), F32)),
        in_specs=(_HBM,) * nb + (pl.BlockSpec(memory_space=pl.ANY),),
        out_specs=(_SEM, _SEM) + (_HBM,) * nb + (pl.BlockSpec(memory_space=pltpu.VMEM),),
        input_output_aliases={i: 2 + i for i in range(nb)},
        compiler_params=pltpu.CompilerParams(has_side_effects=_EFFECT),
    )(*[pltpu.with_memory_space_constraint(b, pltpu.HBM) for b in bufs], after)
    return out[0], out[1], list(out[2:2 + nb]), out[-1]


def _wait_copies(flight, copies, after, name):
    send_sems, recv_sems, bufs, _ = flight
    nb = len(bufs)

    def body(*refs):
        for cp in copies(refs[:nb], refs[nb], refs[nb + 1]):
            cp.wait_send()
            cp.wait_recv()

    return pl.pallas_call(
        body, name=name,
        out_shape=tuple(pltpu.HBM(b.shape, b.dtype) for b in bufs),
        in_specs=(_HBM,) * nb + (_SEM, _SEM, pl.BlockSpec(memory_space=pl.ANY)), out_specs=(_HBM,) * nb,
        input_output_aliases={i: i for i in range(nb)},
        compiler_params=pltpu.CompilerParams(has_side_effects=_EFFECT),
    )(*bufs, send_sems, recv_sems, after)


class _From:
    def __init__(self, sems, offset):
        self.sems, self.offset = sems, offset

    @property
    def at(self):
        return self

    def __getitem__(self, k):
        return self.sems.at[k + self.offset]


def _group(copies, first_buf, n_bufs, offset):
    def grouped(refs, send_sems, recv_sems):
        return copies(refs[first_buf:first_buf + n_bufs], _From(send_sems, offset), _From(recv_sems, offset))
    return grouped


def _wait_then_start(flight, waited, started, n_started, after, name, more_bufs=()):
    old_send, old_recv, bufs, _ = flight
    bufs = list(bufs) + [pltpu.with_memory_space_constraint(b, pltpu.HBM) for b in more_bufs]
    nb = len(bufs)

    def body(*refs):
        for cp in waited(refs[:nb], refs[nb], refs[nb + 1]):
            cp.wait_send()
            cp.wait_recv()
        for cp in started(refs[:nb], refs[nb + 3], refs[nb + 4]):
            cp.start()
        refs[-1][...] = jnp.zeros_like(refs[-1])

    out = pl.pallas_call(
        body, name=name,
        out_shape=(pltpu.SemaphoreType.DMA((n_started,)), pltpu.SemaphoreType.DMA((n_started,)),
                   *[pltpu.HBM(b.shape, b.dtype) for b in bufs], jax.ShapeDtypeStruct((8, 128), F32)),
        in_specs=(_HBM,) * nb + (_SEM, _SEM, pl.BlockSpec(memory_space=pl.ANY)),
        out_specs=(_SEM, _SEM) + (_HBM,) * nb + (pl.BlockSpec(memory_space=pltpu.VMEM),),
        input_output_aliases={i: 2 + i for i in range(nb)},
        compiler_params=pltpu.CompilerParams(has_side_effects=_EFFECT),
    )(*bufs, old_send, old_recv, after)
    return out[0], out[1], list(out[2:2 + nb]), out[-1]


def _late_pair_copies(refs, send_sems, recv_sems):
    blocks_ref, land_ref = refs
    x, y, c = _place()
    first = _first_axis_chip(x, y, c)
    devices = [4 * x + 2 * y + 1 - c, 4 * first[0] + 2 * first[1] + 1 - c]
    return [pltpu.make_async_remote_copy(
        src_ref=blocks_ref.at[devices[k]], dst_ref=land_ref.at[k], send_sem=send_sems.at[k], recv_sem=recv_sems.at[k],
        device_id=(x, y, 1 - c), device_id_type=MESH) for k in range(2)]


def _chip_copies(refs, send_sems, recv_sems):
    pair_ref, land_ref = refs
    x, y, c = _place()
    chips = [(1 - x, y), (x, 1 - y), (1 - x, 1 - y)]
    return [pltpu.make_async_remote_copy(
        src_ref=pair_ref.at[2 * chip[0] + chip[1]], dst_ref=land_ref.at[k],
        send_sem=send_sems.at[k], recv_sem=recv_sems.at[k],
        device_id=(*chip, c), device_id_type=MESH) for k, chip in enumerate(chips)]


def _first_hop_copies(refs, send_sems, recv_sems):
    pair_ref, land_ref = refs
    x, y, c = _place()
    return [pltpu.make_async_remote_copy(
        src_ref=pair_ref.at[k], dst_ref=land_ref.at[k], send_sem=send_sems.at[k], recv_sem=recv_sems.at[k],
        device_id=(*_first_axis_chip(x, y, c), c), device_id_type=MESH) for k in range(2)]


def _second_hop_copies(refs, send_sems, recv_sems):
    relay_ref, land_ref = refs
    x, y, c = _place()
    second = ((x + c) % 2, (y + 1 - c) % 2)
    return [pltpu.make_async_remote_copy(
        src_ref=relay_ref, dst_ref=land_ref.at[0], send_sem=send_sems.at[0], recv_sem=recv_sems.at[0],
        device_id=(*second, c), device_id_type=MESH)]


def _own_block_copies(targets):
    def copies(refs, send_sems, recv_sems):
        x, y, c = _place()
        mine = refs[0].at[4 * x + 2 * y + c]
        return [pltpu.make_async_remote_copy(
            src_ref=mine, dst_ref=mine, send_sem=send_sems.at[k], recv_sem=recv_sems.at[k],
            device_id=to, device_id_type=MESH) for k, to in enumerate(targets(x, y, c))]
    return copies


def _all_others(x, y, c):
    flip = lambda v, f: 1 - v if f else v
    return [(flip(x, r & 4), flip(y, r & 2), flip(c, r & 1)) for r in range(1, N_DEV)]


def _forward_copies(refs, send_sems, recv_sems):
    x, y, c = _place()
    chips = [(1 - x, y), (x, 1 - y), (1 - x, 1 - y)]
    return [pltpu.make_async_remote_copy(
        src_ref=refs[0].at[4 * chip[0] + 2 * chip[1] + c], dst_ref=refs[0].at[4 * chip[0] + 2 * chip[1] + c],
        send_sem=send_sems.at[k], recv_sem=recv_sems.at[k],
        device_id=(x, y, 1 - c), device_id_type=MESH) for k, chip in enumerate(chips)]


def _first_axis_chip(x, y, c):
    return (x + 1 - c) % 2, (y + c) % 2


def _second_axis_chip(x, y, c):
    return (x + c) % 2, (y + 1 - c) % 2


def _first_targets(x, y, c):
    return [(x, y, 1 - c), (*_first_axis_chip(x, y, c), c)]


def _all_gather_small(shard, name, dep=None):
    def body(in_ref, *refs):
        out_ref, send_sems, recv_sems, local_sem = refs[-4:]
        x, y, c = _place()
        me, sibling = 4 * x + 2 * y + c, (x, y, 1 - c)
        first, second = _first_axis_chip(x, y, c), _second_axis_chip(x, y, c)

        def pair(chip):
            return out_ref.at[pl.ds(2 * (2 * chip[0] + chip[1]), 2)]

        def exchange(k, src, dst, to):
            cp = pltpu.make_async_remote_copy(src_ref=src, dst_ref=dst, send_sem=send_sems.at[k],
                                              recv_sem=recv_sems.at[k], device_id=to, device_id_type=MESH)
            cp.start()
            cp.wait()

        own = pltpu.make_async_copy(in_ref, out_ref.at[me], local_sem)
        own.start()
        exchange(0, in_ref, out_ref.at[me], sibling)
        own.wait()
        exchange(1, pair((x, y)), pair((x, y)), (*second, c))
        exchange(2, pair(second), pair(second), sibling)
        exchange(3, pair(first), pair(first), (*second, c))

    spec = pl.BlockSpec(memory_space=pltpu.VMEM)
    deps = [] if dep is None else [dep]
    return pl.pallas_call(
        body, name=name, out_shape=jax.ShapeDtypeStruct((N_DEV,) + shard.shape, shard.dtype),
        in_specs=[spec] * (1 + len(deps)), out_specs=spec,
        scratch_shapes=[pltpu.SemaphoreType.DMA((4,)), pltpu.SemaphoreType.DMA((4,)), pltpu.SemaphoreType.DMA],
        compiler_params=_params(),
    )(shard, *deps)


def _slot_copies(refs, send_sems, recv_sems, plan):
    copies = []
    for k, ((px, py, pc), to) in enumerate(plan):
        blk = refs[0].at[4 * px + 2 * py + pc]
        copies.append(pltpu.make_async_remote_copy(
            src_ref=blk, dst_ref=blk, send_sem=send_sems.at[k], recv_sem=recv_sems.at[k],
            device_id=to, device_id_type=MESH))
    return copies


def _second_axis_stage_copies(refs, send_sems, recv_sems):
    x, y, c = _place()
    first, second = (*_first_axis_chip(x, y, c), c), (*_second_axis_chip(x, y, c), c)
    return _slot_copies(refs, send_sems, recv_sems, [((x, y, c), second), (first, (x, y, 1 - c)), (first, second)])


def _second_axis_forward_copies(refs, send_sems, recv_sems):
    x, y, c = _place()
    return _slot_copies(refs, send_sems, recv_sems, [((*_second_axis_chip(x, y, c), c), (x, y, 1 - c))])


def _diagonal_forward_copies(refs, send_sems, recv_sems):
    x, y, c = _place()
    blk = refs[0].at[4 * (1 - x) + 2 * (1 - y) + c]
    return [pltpu.make_async_remote_copy(
        src_ref=blk, dst_ref=blk, send_sem=send_sems.at[0], recv_sem=recv_sems.at[0],
        device_id=(x, y, 1 - c), device_id_type=MESH)]


def _with_own_slot(block, me):
    return lax.dynamic_update_index_in_dim(lax.empty((N_DEV,) + block.shape, block.dtype), block, me, 0)


def _w_in_grad(dz, h, dep, tm):
    (t, m), n = dz.shape, h.shape[1]
    assert m % tm == 0 and dz.dtype == BF16 and h.dtype == BF16

    def body(dz_ref, h_ref, dep_ref, o_ref):
        o_ref[...] = _dot_tn(dz_ref[...], h_ref[...]).astype(BF16)

    return pl.pallas_call(
        body, name="w_in_grad", grid=(m // tm,),
        in_specs=[pl.BlockSpec((t, tm), lambda i: (0, i)), pl.BlockSpec((t, n), lambda i: (0, 0)),
                  pl.BlockSpec((8, 128), lambda i: (0, 0))],
        out_specs=pl.BlockSpec((tm, n), lambda i: (i, 0)),
        out_shape=jax.ShapeDtypeStruct((m, n), BF16),
        compiler_params=_params(dimension_semantics=("arbitrary",)),
    )(dz, h, dep)


Z_TILE = 768
_Z_TILE_ORDER = ((0, 1, 2, 3, 4, 5, 6), (2, 0, 1, 6, 3, 4, 5), (4, 0, 5, 6, 1, 2, 3), (6, 2, 3, 4, 0, 1, 5))
_Z_EARLY_TILES = 4


def _z_proj(h, w_in_t, chip, first, count, z_prev, name, tr=1024):
    t = h.shape[0]

    def body(chip_ref, h_ref, w_ref, z_prev_ref, z_ref):
        z_ref[...] = _dot_nt(h_ref[...], w_ref[...])

    def tile(j, chip_ref):
        picked = 0
        for c, order in enumerate(_Z_TILE_ORDER):
            for k in range(count):
                picked = picked + jnp.where((chip_ref[0] == c) & (j == k), order[first + k], 0)
        return picked

    return pl.pallas_call(
        body, name=name,
        grid_spec=pltpu.PrefetchScalarGridSpec(
            num_scalar_prefetch=1, grid=(count, t // tr),
            in_specs=[pl.BlockSpec((tr, D_MODEL), lambda j, i, o: (i, 0)),
                      pl.BlockSpec((Z_TILE, D_MODEL), lambda j, i, o: (tile(j, o), 0)),
                      pl.BlockSpec(memory_space=pl.ANY)],
            out_specs=pl.BlockSpec((tr, Z_TILE), lambda j, i, o: (i, tile(j, o)))),
        out_shape=jax.ShapeDtypeStruct((t, D_IN), F32),
        input_output_aliases={3: 0},
        compiler_params=_params(dimension_semantics=("arbitrary", "arbitrary")),
    )(chip, h, w_in_t, z_prev)


def _modulation(device, c_all, w_ada, b_ada):
    def body(device_ref, c_ref, w_ref, b_ref, act_ref, mod_ref):
        cv = c_ref[...]
        act = cv * _sigmoid(cv)
        act_ref[...] = act
        mod_ref[...] = jnp.dot(act.astype(BF16), w_ref[...].astype(BF16), preferred_element_type=F32) + b_ref[...]

    whole = lambda a: pl.BlockSpec(a.shape, lambda i, device_ref: (0,) * a.ndim)
    return pl.pallas_call(
        body, name="modulation",
        grid_spec=pltpu.PrefetchScalarGridSpec(
            num_scalar_prefetch=1, grid=(1,),
            in_specs=[whole(c_all), whole(w_ada), pl.BlockSpec((1, W_ADA_SHARD), lambda i, device_ref: (0, device_ref[0]))],
            out_specs=(whole(c_all), pl.BlockSpec((N_DEV, W_ADA_SHARD), lambda i, device_ref: (0, 0)))),
        out_shape=(jax.ShapeDtypeStruct(c_all.shape, F32), jax.ShapeDtypeStruct((N_DEV, W_ADA_SHARD), F32)),
        compiler_params=_params(dimension_semantics=("arbitrary",)),
    )(device, c_all, w_ada, b_ada)


MOD_SHIFT, MOD_SCALE, MOD_GATE = 0, 1, 2


def _mod_spec(part, d):
    return pl.BlockSpec((1, d), lambda i: (0, part))


def _norm_z_proj_own(x, norm_g, mod, w_in_t, chip, tm=512):
    t, d = x.shape

    def body(chip_ref, x_ref, g_ref, sc_ref, sh_ref, w_ref, h_ref, z_ref):
        xv = x_ref[...]
        r = lax.rsqrt(jnp.mean(xv * xv, axis=-1, keepdims=True) + EPS)
        h = ((xv * r) * g_ref[...] * (1.0 + sc_ref[...]) + sh_ref[...]).astype(BF16)
        h_ref[...] = h
        z_ref[...] = _dot_nt(h, w_ref[...])

    def own_tile(chip_ref):
        picked = 0
        for c, order in enumerate(_Z_TILE_ORDER):
            picked = picked + jnp.where(chip_ref[0] == c, order[0], 0)
        return picked

    def row(part):
        return pl.BlockSpec((1, d), lambda i, o: (0, part))

    return pl.pallas_call(
        body, name="norm_z_proj_own",
        grid_spec=pltpu.PrefetchScalarGridSpec(
            num_scalar_prefetch=1, grid=(t // tm,),
            in_specs=[pl.BlockSpec((tm, d), lambda i, o: (i, 0)), row(0), row(MOD_SCALE), row(MOD_SHIFT),
                      pl.BlockSpec((Z_TILE, d), lambda i, o: (own_tile(o), 0))],
            out_specs=(pl.BlockSpec((tm, d), lambda i, o: (i, 0)),
                       pl.BlockSpec((tm, Z_TILE), lambda i, o: (i, own_tile(o))))),
        out_shape=(jax.ShapeDtypeStruct((t, d), BF16), jax.ShapeDtypeStruct((t, D_IN), F32)),
        compiler_params=_params(dimension_semantics=("arbitrary",)),
    )(chip, x, norm_g, mod, mod, w_in_t)


def _window_bias(block_index):
    s = lax.broadcasted_iota(jnp.int32, (2 * BLOCK, BLOCK), 0)
    t = lax.broadcasted_iota(jnp.int32, (2 * BLOCK, BLOCK), 1)
    valid = ((s < BLOCK) & (s > t) & (block_index > 0)) | ((s >= BLOCK) & ((s - BLOCK) <= t))
    bias = jnp.where(valid, 0.0, -jnp.inf).astype(F32)
    return jnp.concatenate([bias] * 8, axis=1)


def _heads_t(pair_blocks, g):
    top = lax.broadcasted_iota(jnp.int32, (BLOCK, BLOCK), 0) < HEAD_DIM
    zeros = jnp.zeros((HEAD_DIM, BLOCK), F32)
    tiles = []
    for blk in pair_blocks:
        tp = blk.T
        if g == 0:
            tiles += [jnp.where(top, tp, 0.0), jnp.concatenate([tp[HEAD_DIM:], zeros], axis=0)]
        else:
            tiles += [jnp.concatenate([zeros, tp[:HEAD_DIM]], axis=0), jnp.where(top, 0.0, tp)]
    return jnp.concatenate(tiles, axis=1)


def _pair_block(xt, p, g):
    r0 = HEAD_DIM * g
    even = xt[r0:r0 + HEAD_DIM, (2 * p) * BLOCK:(2 * p + 1) * BLOCK]
    odd = xt[r0:r0 + HEAD_DIM, (2 * p + 1) * BLOCK:(2 * p + 2) * BLOCK]
    return jnp.concatenate([even, odd], axis=0).T


def _softmax_t(scores_t, bias, sink):
    st = scores_t + bias
    m = jnp.maximum(jnp.max(st, axis=0, keepdims=True), sink)
    e = jnp.exp(st - m)
    es = jnp.exp(sink - m)
    inv = 1.0 / (jnp.sum(e, axis=0, keepdims=True) + es)
    return e * inv, es * inv


def _dot(a, b):
    return jnp.dot(a, b, preferred_element_type=F32)


def _dot_nt(a, b):
    return lax.dot_general(a, b, (((1,), (1,)), ((), ())), preferred_element_type=F32)


def _dot_tn(a, b):
    return lax.dot_general(a, b, (((0,), (0,)), ((), ())), preferred_element_type=F32)


def _layer_norm_fwd(v):
    mu = jnp.mean(v, axis=-1, keepdims=True)
    xc = v - mu
    rstd = lax.rsqrt(jnp.mean(xc * xc, axis=-1, keepdims=True) + EPS)
    return xc * rstd, rstd


def _tril(transposed=False):
    t = lax.broadcasted_iota(jnp.int32, (BLOCK, BLOCK), 0)
    s = lax.broadcasted_iota(jnp.int32, (BLOCK, BLOCK), 1)
    return s >= t if transposed else t >= s


def _const_spec(shape):
    return pl.BlockSpec(shape, lambda i: (0,) * len(shape))


def _keys_values(z_ref, kvp):
    kvc = z_ref[:, SEG_KV:SEG_KV + 2 * D_KV]
    kk = jnp.concatenate([kvp[:, :D_KV], kvc[:, :D_KV]], axis=0)
    vv = jnp.concatenate([kvp[:, D_KV:], kvc[:, D_KV:]], axis=0)
    return kk, vv


MIXER_BLOCKS = 2


class _Rows:
    def __init__(self, ref, sub):
        self.ref, self.rows = ref, slice(sub * BLOCK, (sub + 1) * BLOCK)

    def __getitem__(self, idx):
        return self.ref[self.rows, idx[1]]

    def __setitem__(self, idx, value):
        self.ref[self.rows, idx[1]] = value


def _kv_before_spec(index):
    return pl.BlockSpec((BLOCK, 2 * D_KV),
                        lambda i: (jnp.maximum(MIXER_BLOCKS * index(i) - 1, 0), SEG_KV // (2 * D_KV)))


def _pair_cols(g, p, base=0):
    return slice(base + (4 * g + p) * 128, base + (4 * g + p + 1) * 128)


def _mixer_fwd(z, sink_rows, ln_g, ln_b, sgu_w, sgu_bt):
    t = z.shape[0]

    def body(z_all, kvp_ref, sink_ref, lng_ref, lnb_ref, w_ref, bt_ref, a_all, prob_ref, sink_prob_ref):
        kv_before = kvp_ref[...]
        for sub in range(MIXER_BLOCKS):
            z_ref, a_ref = _Rows(z_all, sub), _Rows(a_all, sub)
            one_block(z_ref, kv_before, MIXER_BLOCKS * pl.program_id(0) + sub, sink_ref, lng_ref, lnb_ref, w_ref,
                      bt_ref, a_ref, prob_ref.at[sub], sink_prob_ref.at[sub])
            kv_before = z_ref[:, SEG_KV:SEG_KV + 2 * D_KV]

    def one_block(z_ref, kv_before, block_index, sink_ref, lng_ref, lnb_ref, w_ref, bt_ref, a_ref, prob_ref,
                  sink_prob_ref):
        bias = _window_bias(block_index)
        kk, vv = _keys_values(z_ref, kv_before)
        kk_b, vvt_b = kk.astype(BF16), vv.T.astype(BF16)
        for g in range(2):
            qt = _heads_t([z_ref[:, _pair_cols(g, p, SEG_Q)] * ATTN_SCALE for p in range(4)], g).astype(BF16)
            prob, sink_prob = _softmax_t(_dot(kk_b, qt), bias, sink_ref[g])
            prob_b = prob.astype(BF16)
            prob_ref[g] = prob_b
            sink_prob_ref[g] = sink_prob
            ot = _dot(vvt_b, prob_b)
            for p in range(4):
                gate = z_ref[:, _pair_cols(g, p, SEG_GA)]
                a_ref[:, _pair_cols(g, p)] = (_pair_block(ot, p, g) * (gate * _sigmoid(gate))).astype(BF16)

        vhat, _ = _layer_norm_fwd(z_ref[:, SEG_VS:SEG_VS + D_SGU])
        vn = vhat * lng_ref[...] + lnb_ref[...]
        tril = _tril()
        for g in range(SGU_GROUPS):
            cols = slice(g * 128, (g + 1) * 128)
            wm = jnp.where(tril, w_ref[g], 0.0).astype(BF16)
            mixed = _dot(wm, vn[:, cols].astype(BF16)) + bt_ref[:, g:g + 1]
            gate = z_ref[:, SEG_GS + g * 128:SEG_GS + (g + 1) * 128]
            a_ref[:, D_ATTN + g * 128:D_ATTN + (g + 1) * 128] = (
                (z_ref[:, SEG_U + g * 128:SEG_U + (g + 1) * 128] * mixed) * (gate * _sigmoid(gate))).astype(BF16)

    rows = MIXER_BLOCKS * BLOCK
    return pl.pallas_call(
        body, name="mixer_fwd", grid=(t // rows,),
        in_specs=[pl.BlockSpec((rows, D_IN), lambda i: (i, 0)), _kv_before_spec(lambda i: i),
                  _const_spec((2, 1, 8 * BLOCK)), _const_spec((1, D_SGU)), _const_spec((1, D_SGU)),
                  _const_spec((SGU_GROUPS, BLOCK, BLOCK)), _const_spec((BLOCK, SGU_GROUPS))],
        out_specs=(pl.BlockSpec((rows, D_MODEL), lambda i: (i, 0)),
                   pl.BlockSpec((MIXER_BLOCKS, 2, 2 * BLOCK, 8 * BLOCK), lambda i: (i, 0, 0, 0)),
                   pl.BlockSpec((MIXER_BLOCKS, 2, 1, 8 * BLOCK), lambda i: (i, 0, 0, 0))),
        out_shape=(jax.ShapeDtypeStruct((t, D_MODEL), BF16),
                   jax.ShapeDtypeStruct((t // BLOCK, 2, 2 * BLOCK, 8 * BLOCK), BF16),
                   jax.ShapeDtypeStruct((t // BLOCK, 2, 1, 8 * BLOCK), F32)),
        compiler_params=_params(dimension_semantics=("arbitrary",)),
    )(z, z, sink_rows, ln_g, ln_b, sgu_w, sgu_bt)


def _mixer_bwd(z, da, probs, sink_probs, ln_g, ln_b, sgu_w, sgu_wt, sgu_bt, a, dy):
    t = z.shape[0]

    def body(z_all, kvp_ref, da_all, prob_ref, sink_prob_ref, lng_ref, lnb_ref, w_ref, wt_ref, bt_ref, a_ref, dy_ref,
             dz_all, dsink_ref, dw_ref, db_ref, dlng_ref, dlnb_ref, dw_out_ref, carry_ref, dsink_acc, dbt_acc):
        step = pl.program_id(0)

        @pl.when(step == 0)
        def _():
            carry_ref[...] = jnp.zeros_like(carry_ref)
            dsink_acc[...] = jnp.zeros_like(dsink_acc)
            dbt_acc[...] = jnp.zeros_like(dbt_acc)
            dw_ref[...] = jnp.zeros_like(dw_ref)
            dlng_ref[...] = jnp.zeros_like(dlng_ref)
            dlnb_ref[...] = jnp.zeros_like(dlnb_ref)

        carry = carry_ref[...]
        dw_out_ref[...] = _dot_tn(a_ref[...], dy_ref[...]).astype(BF16)
        for sub in reversed(range(MIXER_BLOCKS)):
            kv_before = kvp_ref[...] if sub == 0 else _Rows(z_all, sub - 1)[:, SEG_KV:SEG_KV + 2 * D_KV]
            carry = one_block(_Rows(z_all, sub), kv_before, _Rows(da_all, sub), prob_ref.at[sub], sink_prob_ref.at[sub],
                              carry, lng_ref, lnb_ref, w_ref, wt_ref, bt_ref, _Rows(dz_all, sub),
                              dw_ref, dlng_ref, dlnb_ref, dsink_acc, dbt_acc)
        carry_ref[...] = carry

        @pl.when(step == ns - 1)
        def _():
            db_ref[...] = dbt_acc[...].T[:SGU_GROUPS]
            lane_row = lax.broadcasted_iota(jnp.int32, (1, 128), 1)
            d_sink = jnp.zeros((1, 128), F32)
            for g in range(2):
                acc = dsink_acc[g]
                for j in range(8):
                    head_sum = jnp.sum(acc[:, j * BLOCK:(j + 1) * BLOCK], axis=-1, keepdims=True)
                    d_sink = d_sink + jnp.where(lane_row == 8 * g + j, head_sum, 0.0)
            dsink_ref[...] = d_sink

    def one_block(z_ref, kv_before, da_ref, prob_ref, sink_prob_ref, carry, lng_ref, lnb_ref, w_ref, wt_ref, bt_ref,
                  dz_ref, dw_ref, dlng_ref, dlnb_ref, dsink_acc, dbt_acc):
        kk, vv = _keys_values(z_ref, kv_before)
        vv_b = vv.astype(BF16)
        kkt_b, vvt_b = kk.T.astype(BF16), vv.T.astype(BF16)
        dkk = jnp.zeros((2 * BLOCK, D_KV), F32)
        dvv = jnp.zeros((2 * BLOCK, D_KV), F32)
        for g in range(2):
            qt = _heads_t([z_ref[:, _pair_cols(g, p, SEG_Q)] * ATTN_SCALE for p in range(4)], g).astype(BF16)
            prob_b, sink_prob = prob_ref[g], sink_prob_ref[g]
            prob = prob_b.astype(F32)
            ot = _dot(vvt_b, prob_b)
            gates = [z_ref[:, _pair_cols(g, p, SEG_GA)] for p in range(4)]
            sig = [_sigmoid(gt) for gt in gates]
            d_attn = [da_ref[:, _pair_cols(g, p)] for p in range(4)]
            d_ot = _heads_t([d_attn[p] * (gates[p] * sig[p]) for p in range(4)], g).astype(BF16)
            d_prob = _dot(vv_b, d_ot)
            delta = jnp.sum(prob * d_prob, axis=0, keepdims=True)
            d_scores = (prob * (d_prob - delta)).astype(BF16)
            dsink_acc[g] -= sink_prob * delta
            d_qt = _dot(kkt_b, d_scores)
            dkk = dkk + _dot_nt(d_scores, qt)
            dvv = dvv + _dot_nt(prob_b, d_ot)
            for p in range(4):
                dz_ref[:, _pair_cols(g, p, SEG_Q)] = (_pair_block(d_qt, p, g) * ATTN_SCALE).astype(BF16)
                d_silu = sig[p] * (1.0 + gates[p] * (1.0 - sig[p]))
                dz_ref[:, _pair_cols(g, p, SEG_GA)] = (d_attn[p] * _pair_block(ot, p, g) * d_silu).astype(BF16)
        d_kv = jnp.concatenate([dkk, dvv], axis=1)
        dz_ref[:, SEG_KV:SEG_KV + 2 * D_KV] = (d_kv[BLOCK:] + carry).astype(BF16)

        vhat, rstd = _layer_norm_fwd(z_ref[:, SEG_VS:SEG_VS + D_SGU])
        lng = lng_ref[...]
        vn = vhat * lng + lnb_ref[...]
        tril, triu = _tril(), _tril(transposed=True)
        lane = lax.broadcasted_iota(jnp.int32, (BLOCK, 128), 1)
        d_bt = jnp.zeros((BLOCK, 128), F32)
        d_vn = []
        for g in range(SGU_GROUPS):
            cols = slice(g * 128, (g + 1) * 128)
            wm = jnp.where(tril, w_ref[g], 0.0).astype(BF16)
            wmt = jnp.where(triu, wt_ref[g], 0.0).astype(BF16)
            vn_g = vn[:, cols].astype(BF16)
            mixed = _dot(wm, vn_g) + bt_ref[:, g:g + 1]
            gate = z_ref[:, SEG_GS + g * 128:SEG_GS + (g + 1) * 128]
            u = z_ref[:, SEG_U + g * 128:SEG_U + (g + 1) * 128]
            d_out = da_ref[:, D_ATTN + g * 128:D_ATTN + (g + 1) * 128]
            sg = _sigmoid(gate)
            d_um = d_out * (gate * sg)
            dz_ref[:, SEG_U + g * 128:SEG_U + (g + 1) * 128] = (d_um * mixed).astype(BF16)
            dz_ref[:, SEG_GS + g * 128:SEG_GS + (g + 1) * 128] = (
                d_out * (u * mixed) * (sg * (1.0 + gate * (1.0 - sg)))).astype(BF16)
            d_mixed = d_um * u
            d_mixed_b = d_mixed.astype(BF16)
            dw_ref[g] += jnp.where(tril, _dot_nt(d_mixed_b, vn_g), 0.0)
            d_bt = d_bt + jnp.where(lane == g, jnp.sum(d_mixed, axis=-1, keepdims=True), 0.0)
            d_vn.append(_dot(wmt, d_mixed_b))
        dbt_acc[...] += d_bt
        d_vn = jnp.concatenate(d_vn, axis=1)
        dlng_ref[...] += jnp.sum(d_vn * vhat, axis=0, keepdims=True)
        dlnb_ref[...] += jnp.sum(d_vn, axis=0, keepdims=True)
        d_vhat = d_vn * lng
        d_v = rstd * (d_vhat - jnp.mean(d_vhat, axis=-1, keepdims=True)
                      - vhat * jnp.mean(d_vhat * vhat, axis=-1, keepdims=True))
        dz_ref[:, SEG_VS:SEG_VS + D_SGU] = d_v.astype(BF16)
        return d_kv[:BLOCK]

    rows = MIXER_BLOCKS * BLOCK
    ns = t // rows
    rev = lambda i: ns - 1 - i
    tile = a.shape[1] // ns
    assert tile % 128 == 0
    return pl.pallas_call(
        body, name="mixer_bwd", grid=(ns,),
        in_specs=[pl.BlockSpec((rows, D_IN), lambda i: (rev(i), 0)), _kv_before_spec(rev),
                  pl.BlockSpec((rows, D_MODEL), lambda i: (rev(i), 0)),
                  pl.BlockSpec((MIXER_BLOCKS, 2, 2 * BLOCK, 8 * BLOCK), lambda i: (rev(i), 0, 0, 0)),
                  pl.BlockSpec((MIXER_BLOCKS, 2, 1, 8 * BLOCK), lambda i: (rev(i), 0, 0, 0)),
                  _const_spec((1, D_SGU)), _const_spec((1, D_SGU)),
                  _const_spec((SGU_GROUPS, BLOCK, BLOCK)), _const_spec((SGU_GROUPS, BLOCK, BLOCK)),
                  _const_spec((BLOCK, SGU_GROUPS)), pl.BlockSpec((t, tile), lambda i: (0, i)),
                  pl.BlockSpec(dy.shape, lambda i: (0, 0), pipeline_mode=pl.Buffered(1))],
        out_specs=(pl.BlockSpec((rows, D_IN), lambda i: (rev(i), 0)), _const_spec((1, 128)),
                   _const_spec((SGU_GROUPS, BLOCK, BLOCK)), _const_spec((SGU_GROUPS, BLOCK)),
                   _const_spec((1, D_SGU)), _const_spec((1, D_SGU)),
                   pl.BlockSpec((tile, dy.shape[1]), lambda i: (i, 0))),
        out_shape=(jax.ShapeDtypeStruct((t, D_IN), BF16), jax.ShapeDtypeStruct((1, 128), F32),
                   jax.ShapeDtypeStruct((SGU_GROUPS, BLOCK, BLOCK), F32), jax.ShapeDtypeStruct((SGU_GROUPS, BLOCK), F32),
                   jax.ShapeDtypeStruct((1, D_SGU), F32), jax.ShapeDtypeStruct((1, D_SGU), F32),
                   jax.ShapeDtypeStruct((a.shape[1], dy.shape[1]), BF16)),
        scratch_shapes=[pltpu.VMEM((BLOCK, 2 * D_KV), F32), pltpu.VMEM((2, 1, 8 * BLOCK), F32),
                        pltpu.VMEM((BLOCK, 128), F32)],
        compiler_params=_params(dimension_semantics=("arbitrary",)),
    )(z, z, da, probs, sink_probs, ln_g, ln_b, sgu_w, sgu_wt, sgu_bt, a, dy)


def _out_proj_head(a, w_out_full, x, target, mod, final_g, tm=256):
    t, d = x.shape

    def body(a_ref, w_ref, x_ref, tg_ref, gate_ref, fg_ref, dx2_ref, dy_ref, da_ref, loss_ref, dfg_ref, dgate_ref):
        @pl.when(pl.program_id(0) == 0)
        def _():
            loss_ref[...] = jnp.zeros_like(loss_ref)
            dfg_ref[...] = jnp.zeros_like(dfg_ref)
            dgate_ref[...] = jnp.zeros_like(dgate_ref)

        yv, gate, fg = _dot(a_ref[...], w_ref[...]), gate_ref[...], fg_ref[...]
        x2 = x_ref[...] + gate * yv
        r2 = lax.rsqrt(jnp.mean(x2 * x2, axis=-1, keepdims=True) + EPS)
        nrm = x2 * r2
        err = nrm * fg - tg_ref[...]
        loss_ref[...] += 0.5 * jnp.sum(jnp.mean(err * err, axis=-1, keepdims=True), axis=0, keepdims=True)
        fg_d = fg * (1.0 / d)
        err_nrm = err * nrm
        dfg_ref[...] += jnp.sum(err_nrm, axis=0, keepdims=True) * (1.0 / d)
        d_nrm = err * fg_d
        dx2 = r2 * (d_nrm - nrm * jnp.mean(err_nrm * fg_d, axis=-1, keepdims=True))
        dx2_ref[...] = dx2
        dgate_ref[...] += jnp.sum(dx2 * yv, axis=0, keepdims=True)
        dy = (dx2 * gate).astype(BF16)
        dy_ref[...] = dy
        da_ref[...] = _dot_nt(dy, w_ref[...])

    blk = pl.BlockSpec((tm, d), lambda i: (i, 0))
    a_blk = pl.BlockSpec((tm, a.shape[1]), lambda i: (i, 0))
    row = _const_spec((1, d))
    whole = pl.BlockSpec(w_out_full.shape, lambda i: (0, 0), pipeline_mode=pl.Buffered(1))
    return pl.pallas_call(
        body, name="out_proj_head", grid=(t // tm,),
        in_specs=[a_blk, whole, blk, blk, _mod_spec(MOD_GATE, d), row],
        out_specs=(blk, blk, a_blk, _const_spec((1, 128)), row, row),
        out_shape=(jax.ShapeDtypeStruct((t, d), F32), jax.ShapeDtypeStruct((t, d), BF16),
                   jax.ShapeDtypeStruct(a.shape, F32), jax.ShapeDtypeStruct((1, 128), F32),
                   jax.ShapeDtypeStruct((1, d), F32), jax.ShapeDtypeStruct((1, d), F32)),
        compiler_params=_params(dimension_semantics=("arbitrary",)),
    )(a, w_out_full, x, target, mod, final_g)


def _z_proj_bwd_norm(dz, w_in_t, x, dx2, norm_g, mod, dep, tm=256):
    t, d = x.shape

    def body(dz_ref, w_ref, x_ref, dx2_ref, g_ref, sc_ref, dep_ref, gx_ref, dshift_ref, dscale_ref, dg_ref):
        @pl.when(pl.program_id(0) == 0)
        def _():
            dshift_ref[...] = jnp.zeros_like(dshift_ref)
            dscale_ref[...] = jnp.zeros_like(dscale_ref)
            dg_ref[...] = jnp.zeros_like(dg_ref)

        dh, xv, g = _dot(dz_ref[...], w_ref[...]), x_ref[...], g_ref[...]
        one_plus = 1.0 + sc_ref[...]
        r = lax.rsqrt(jnp.mean(xv * xv, axis=-1, keepdims=True) + EPS)
        xn = xv * r
        gain = one_plus * g
        dh_xn = dh * xn
        dh_xn_sum = jnp.sum(dh_xn, axis=0, keepdims=True)
        dshift_ref[...] += jnp.sum(dh, axis=0, keepdims=True)
        dscale_ref[...] += dh_xn_sum * g
        dg_ref[...] += dh_xn_sum * one_plus
        d_xn = dh * gain
        gx_ref[...] = dx2_ref[...] + r * (d_xn - xn * jnp.mean(dh_xn * gain, axis=-1, keepdims=True))

    blk = pl.BlockSpec((tm, d), lambda i: (i, 0))
    row = _const_spec((1, d))
    whole = pl.BlockSpec(w_in_t.shape, lambda i: (0, 0), pipeline_mode=pl.Buffered(1))
    return pl.pallas_call(
        body, name="z_proj_bwd_norm", grid=(t // tm,),
        in_specs=[pl.BlockSpec((tm, dz.shape[1]), lambda i: (i, 0)), whole, blk, blk, row, _mod_spec(MOD_SCALE, d),
                  _const_spec((8, 128))],
        out_specs=(blk, row, row, row),
        out_shape=(jax.ShapeDtypeStruct((t, d), F32),) + (jax.ShapeDtypeStruct((1, d), F32),) * 3,
        compiler_params=_params(dimension_semantics=("arbitrary",)),
    )(dz, w_in_t, x, dx2, norm_g, mod, dep)


def _adamw(w, g, m, v):
    m = ADAM_B1 * m + (1.0 - ADAM_B1) * g
    v = ADAM_B2 * v + (1.0 - ADAM_B2) * (g * g)
    m_hat = m / (1.0 - ADAM_B1 ** ADAM_STEP)
    v_hat = v / (1.0 - ADAM_B2 ** ADAM_STEP)
    delta = -ADAM_LR * (m_hat / (jnp.sqrt(v_hat) + ADAM_EPS) + ADAM_WD * w)
    return delta, m, v


def _relay_sum(device, blocks, land_pair, land_first, tr):
    _, r, c = blocks.shape
    tiles = r // tr

    def body(device_ref, a_hbm, b_hbm, c_hbm, o_hbm, terms, summed, in_sems, out_sems):
        sources = [a_hbm.at[device_ref[0]], b_hbm.at[1], c_hbm.at[1]]

        def reads(i):
            rows = pl.ds(i * tr, tr)
            return [pltpu.make_async_copy(src.at[rows], terms.at[k, rows], in_sems.at[3 * i + k])
                    for k, src in enumerate(sources)]

        def write(i):
            rows = pl.ds(i * tr, tr)
            return pltpu.make_async_copy(summed.at[rows], o_hbm.at[rows], out_sems.at[i])

        for i in range(tiles):
            for cp in reads(i):
                cp.start()
        for i in range(tiles):
            rows = pl.ds(i * tr, tr)
            for cp in reads(i):
                cp.wait()
            summed[rows, :] = (terms[0, rows, :].astype(F32) + terms[1, rows, :].astype(F32)
                               + terms[2, rows, :].astype(F32)).astype(BF16)
            write(i).start()
        for i in range(tiles):
            write(i).wait()

    any_spec = pl.BlockSpec(memory_space=pl.ANY)
    return pl.pallas_call(
        body, name="w_in_grad_relay_sum",
        grid_spec=pltpu.PrefetchScalarGridSpec(
            num_scalar_prefetch=1, grid=(1,), in_specs=[any_spec] * 3, out_specs=any_spec,
            scratch_shapes=[pltpu.VMEM((3, r, c), BF16), pltpu.VMEM((r, c), BF16),
                            pltpu.SemaphoreType.DMA((3 * tiles,)), pltpu.SemaphoreType.DMA((tiles,))]),
        out_shape=jax.ShapeDtypeStruct((r, c), BF16),
        compiler_params=_params(dimension_semantics=("arbitrary",)),
    )(device, blocks, land_pair, land_first)


def _adam_from_chips(chip, pair, landed, w, m, v, name, tc):
    _, r, c = pair.shape
    n = len(landed)
    tiles = c // tc
    n_in = n + 4

    def body(chip_ref, pair_hbm, *refs):
        landed_hbm, (w_hbm, m_hbm, v_hbm) = refs[:n], refs[n:n + 3]
        outs_hbm = refs[n + 3:n + 7]
        grads, w_in, m_in, v_in, staged, in_sems, out_sems = refs[n + 7:]
        sources = [pair_hbm.at[chip_ref[0]]] + [landed_hbm[k].at[index] for k, (_, index) in enumerate(landed)]

        def reads(i):
            cols = pl.ds(i * tc, tc)
            cps = [pltpu.make_async_copy(src.at[:, cols], grads.at[k, :, cols], in_sems.at[i * n_in + k])
                   for k, src in enumerate(sources)]
            for k, (src, dst) in enumerate([(w_hbm, w_in), (m_hbm, m_in), (v_hbm, v_in)]):
                cps.append(pltpu.make_async_copy(src.at[:, cols], dst.at[:, cols], in_sems.at[i * n_in + n + 1 + k]))
            return cps

        def writes(i):
            cols = pl.ds(i * tc, tc)
            return [pltpu.make_async_copy(staged.at[i % 2, k], outs_hbm[k].at[:, cols], out_sems.at[i * 4 + k])
                    for k in range(4)]

        for i in range(tiles):
            for cp in reads(i):
                cp.start()
        for i in range(tiles):
            cols = pl.ds(i * tc, tc)
            for cp in reads(i):
                cp.wait()
            if i >= 2:
                for cp in writes(i - 2):
                    cp.wait()
            g = grads[0, :, cols].astype(F32)
            for k in range(1, n + 1):
                g = g + grads[k, :, cols].astype(F32)
            delta, new_m, new_v = _adamw(w_in[:, cols], g, m_in[:, cols], v_in[:, cols])
            for k, value in enumerate([g, delta, new_m, new_v]):
                staged[i % 2, k] = value
            for cp in writes(i):
                cp.start()
        for i in range(max(tiles - 2, 0), tiles):
            for cp in writes(i):
                cp.wait()

    any_spec = pl.BlockSpec(memory_space=pl.ANY)
    return pl.pallas_call(
        body, name=name,
        grid_spec=pltpu.PrefetchScalarGridSpec(
            num_scalar_prefetch=1, grid=(1,), in_specs=[any_spec] * (n + 4), out_specs=(any_spec,) * 4,
            scratch_shapes=[pltpu.VMEM((n + 1, r, c), BF16), pltpu.VMEM((r, c), F32), pltpu.VMEM((r, c), F32),
                            pltpu.VMEM((r, c), F32), pltpu.VMEM((2, 4, r, tc), F32),
                            pltpu.SemaphoreType.DMA((tiles * n_in,)), pltpu.SemaphoreType.DMA((tiles * 4,))]),
        out_shape=(jax.ShapeDtypeStruct((r, c), F32),) * 4,
        compiler_params=_params(dimension_semantics=("arbitrary",)),
    )(chip, pair, *[array for array, _ in landed], w, m, v)


def _adam_w_ada(device, act_t, dmod_all, w, m, v, tr=512):
    r, c = w.shape

    def body(device_ref, a_ref, dm_ref, w_ref, m_ref, v_ref, g_ref, d_ref, nm_ref, nv_ref):
        g = _dot(a_ref[...].astype(BF16), dm_ref[...].astype(BF16))
        g_ref[...] = g
        d_ref[...], nm_ref[...], nv_ref[...] = _adamw(w_ref[...], g, m_ref[...], v_ref[...])

    blk = pl.BlockSpec((tr, c), lambda i, device_ref: (i, 0))
    return pl.pallas_call(
        body, name="adam_w_ada",
        grid_spec=pltpu.PrefetchScalarGridSpec(
            num_scalar_prefetch=1, grid=(r // tr,),
            in_specs=[pl.BlockSpec((tr, N_DEV), lambda i, device_ref: (i, 0)),
                      pl.BlockSpec((N_DEV, c), lambda i, device_ref: (0, device_ref[0])), blk, blk, blk],
            out_specs=(blk,) * 4),
        out_shape=(jax.ShapeDtypeStruct((r, c), F32),) * 4,
        compiler_params=_params(dimension_semantics=("arbitrary",)),
    )(device, act_t, dmod_all, w, m, v)


def _pack_small(d_shift, d_scale, d_gate, d_norm_g, d_final_g, d_ln_g, d_ln_b, loss, d_sinks, d_sgu_b):
    def body(shift_ref, scale_ref, gate_ref, ng_ref, fg_ref, lng_ref, lnb_ref, loss_ref, sink_ref, b_ref, o_ref):
        o_ref[...] = jnp.zeros_like(o_ref)
        o_ref[ROW_SHIFT:ROW_SHIFT + 1, :] = shift_ref[...]
        o_ref[ROW_SCALE:ROW_SCALE + 1, :] = scale_ref[...]
        o_ref[ROW_GATE:ROW_GATE + 1, :] = gate_ref[...]
        o_ref[ROW_NORM_G:ROW_NORM_G + 1, :] = ng_ref[...]
        o_ref[ROW_FINAL_G:ROW_FINAL_G + 1, :] = fg_ref[...]
        o_ref[ROW_LN:ROW_LN + 1, 0:D_SGU] = lng_ref[...]
        o_ref[ROW_LN:ROW_LN + 1, D_SGU:2 * D_SGU] = lnb_ref[...]
        o_ref[ROW_MISC:ROW_MISC + 1, 0:128] = loss_ref[...]
        o_ref[ROW_MISC:ROW_MISC + 1, 128:256] = sink_ref[...]
        o_ref[ROW_SGU_B:ROW_SGU_B + SGU_GROUPS, 0:BLOCK] = b_ref[...]

    return pl.pallas_call(
        body, name="pack_small", out_shape=jax.ShapeDtypeStruct((SMALL_ROWS, D_MODEL), F32),
        compiler_params=_params(),
    )(d_shift, d_scale, d_gate, d_norm_g, d_final_g, d_ln_g, d_ln_b, loss, d_sinks, d_sgu_b)


_SMALL_NAMES = ("norm_g", "b_ada", "attn_sinks", "sgu_ln_g", "sgu_ln_b", "sgu_w", "sgu_b", "final_g")


def _adam_small(partials, d_sgu_w_all, weights, moments_m, moments_v):
    names = _SMALL_NAMES
    k = len(names)

    def body(*refs):
        p_ref, sw_ref = refs[0], refs[1]
        w_refs, m_refs, v_refs = refs[2:2 + k], refs[2 + k:2 + 2 * k], refs[2 + 2 * k:2 + 3 * k]
        loss_ref, dmod_ref = refs[2 + 3 * k], refs[3 + 3 * k]
        out_refs = refs[4 + 3 * k:4 + 7 * k]
        sum_ref = refs[4 + 7 * k]
        total = p_ref[0]
        for j in range(1, N_DEV):
            total = total + p_ref[j]
        sum_ref[...] = total
        for j in range(N_DEV):
            for part, row in enumerate((ROW_SHIFT, ROW_SCALE, ROW_GATE)):
                dmod_ref[j:j + 1, part * D_MODEL:(part + 1) * D_MODEL] = p_ref[j, row:row + 1, :]
        loss_ref[...] = sum_ref[ROW_MISC:ROW_MISC + 1, 0:1]
        d_sgu_w = sw_ref[0]
        for j in range(1, N_DEV):
            d_sgu_w = d_sgu_w + sw_ref[j]
        grads = {
            "norm_g": sum_ref[ROW_NORM_G:ROW_NORM_G + 1, :],
            "b_ada": jnp.concatenate([sum_ref[r:r + 1, :] for r in (ROW_SHIFT, ROW_SCALE, ROW_GATE)], axis=1),
            "attn_sinks": sum_ref[ROW_MISC:ROW_MISC + 1, 128:128 + N_Q_HEADS],
            "sgu_ln_g": sum_ref[ROW_LN:ROW_LN + 1, 0:D_SGU],
            "sgu_ln_b": sum_ref[ROW_LN:ROW_LN + 1, D_SGU:2 * D_SGU],
            "sgu_w": d_sgu_w[None],
            "sgu_b": sum_ref[ROW_SGU_B:ROW_SGU_B + SGU_GROUPS, 0:BLOCK][None],
            "final_g": sum_ref[ROW_FINAL_G:ROW_FINAL_G + 1, :],
        }
        for i, name in enumerate(names):
            g = grads[name]
            delta, m, v = _adamw(w_refs[i][...], g, m_refs[i][...], v_refs[i][...])
            out_refs[4 * i][...] = g
            out_refs[4 * i + 1][...] = delta
            out_refs[4 * i + 2][...] = m
            out_refs[4 * i + 3][...] = v

    shapes = [jax.ShapeDtypeStruct((1, 1), F32), jax.ShapeDtypeStruct((N_DEV, 3 * D_MODEL), F32)]
    for name in names:
        shapes += [jax.ShapeDtypeStruct(weights[name].shape, F32)] * 4
    outs = pl.pallas_call(
        body, name="adam_small", out_shape=tuple(shapes),
        scratch_shapes=[pltpu.VMEM((SMALL_ROWS, D_MODEL), F32)],
        compiler_params=_params(),
    )(partials, d_sgu_w_all, *[weights[n] for n in names], *[moments_m[n] for n in names],
      *[moments_v[n] for n in names])
    return outs[0], outs[1], {name: outs[2 + 4 * i:6 + 4 * i] for i, name in enumerate(names)}


def kernel(x, c, norm_g, w_ada, b_ada, w_in, attn_sinks, sgu_ln_g, sgu_ln_b, sgu_w, sgu_b, w_out, final_g, loss_target, m_norm_g, m_w_ada, m_b_ada, m_w_in, m_attn_sinks, m_sgu_ln_g, m_sgu_ln_b, m_sgu_w, m_sgu_b, m_w_out, m_final_g, v_norm_g, v_w_ada, v_b_ada, v_w_in, v_attn_sinks, v_sgu_ln_g, v_sgu_ln_b, v_sgu_w, v_sgu_b, v_w_out, v_final_g):
    xi, yi, ci = _place()
    me = 4 * xi + 2 * yi + ci
    x2d, target = x[0], loss_target[0]

    core = ci.astype(jnp.int32).reshape(1)
    chip = (2 * xi + yi).astype(jnp.int32).reshape(1)

    first = _own_block_copies(_first_targets)
    first_flight = _start_copies([_with_own_slot(w_in[0].T.astype(BF16), me)], first, 2, core, "gather_w_in_start")

    c_all = _all_gather_small(c.reshape(8, 256), "gather_c", first_flight[3]).reshape(N_DEV, D_MODEL)
    device = me.astype(jnp.int32).reshape(1)
    c_act, mod_part = _modulation(device, c_all, w_ada[0], b_ada)
    mod_all = _all_gather_small(mod_part, "gather_mod")

    across = _wait_then_start(first_flight, lambda *a: first(*a)[1:], _second_axis_stage_copies, 3, mod_all,
                              "gather_w_in_second_axis_stage")
    mod = lax.dynamic_index_in_dim(mod_all, me, axis=1, keepdims=False).reshape(1, 3 * D_MODEL)
    mod = mod + across[3][0, 0]

    w_in_pair = _wait_copies((first_flight[0], first_flight[1], across[2], None), lambda *a: first(*a)[:1], mod,
                             "gather_w_in_sibling_wait")
    h, z_own = _norm_z_proj_own(x2d, norm_g, mod, w_in_pair[0].reshape(D_IN, D_MODEL), chip)
    w_out_early = _own_block_copies(lambda x, y, c: [(x, y, 1 - c), (*_second_axis_chip(x, y, c), c)])
    w_out_late = _own_block_copies(lambda x, y, c: [(*_first_axis_chip(x, y, c), c), (1 - x, 1 - y, c)])
    forward = _wait_then_start(
        (across[0], across[1], w_in_pair, None), lambda *a: _second_axis_stage_copies(*a)[:1],
        lambda refs, s, r: _second_axis_forward_copies(refs[:1], s, r) + _group(w_out_early, 1, 1, 1)(refs, s, r),
        3, z_own, "gather_w_in_second_axis_forward", more_bufs=[_with_own_slot(w_out[0].astype(BF16), me)])
    w_in_most = _wait_copies((across[0], across[1], forward[2][:1], None),
                             lambda *a: _second_axis_stage_copies(*a)[1:2], z_own, "gather_w_in_first_forward_wait")
    w_in_most = _wait_copies((forward[0], forward[1], w_in_most, None), _second_axis_forward_copies, z_own,
                             "gather_w_in_second_forward_wait")
    z_early = _z_proj(h, w_in_most[0].reshape(D_IN, D_MODEL), chip, 1, _Z_EARLY_TILES - 1, z_own, "z_proj_early")
    last = _wait_then_start(
        (across[0], across[1], [w_in_most[0], forward[2][1]], None), lambda *a: _second_axis_stage_copies(*a)[2:],
        lambda refs, s, r: _diagonal_forward_copies(refs[:1], s, r) + _group(w_out_late, 1, 1, 1)(refs, s, r),
        3, z_early, "gather_w_in_last_stage")
    w_in_all = _wait_copies((last[0], last[1], last[2][:1], None), _diagonal_forward_copies, z_early,
                            "gather_w_in_last_wait")[0]
    w_in_t = w_in_all.reshape(D_IN, D_MODEL)
    z = _z_proj(h, w_in_t, chip, _Z_EARLY_TILES, 7 - _Z_EARLY_TILES, z_early, "z_proj_late")
    w_out_half = _wait_copies((forward[0], forward[1], last[2][1:], None), _group(w_out_early, 0, 1, 1), z,
                              "gather_w_out_early_wait")
    w_out_flight = _wait_then_start((last[0], last[1], w_out_half, None), _group(w_out_late, 0, 1, 1),
                                    _forward_copies, 3, z, "gather_w_out_forward_stage")
    sink_rows = jnp.repeat(attn_sinks.reshape(N_Q_HEADS), BLOCK).reshape(2, 1, 8 * BLOCK)
    sgu_bt = sgu_b[0].T
    a, probs, sink_probs = _mixer_fwd(z, sink_rows + w_out_flight[3][0, 0], sgu_ln_g, sgu_ln_b, sgu_w[0], sgu_bt)
    w_out_all = _wait_copies(w_out_flight, _forward_copies, a, "gather_w_out_forward_wait")[0]
    w_out_full = w_out_all.reshape(D_MODEL, D_MODEL)
    final_g_row = final_g.reshape(1, D_MODEL)
    dx2, dy, da, loss_part, d_final_g, d_gate = _out_proj_head(a, w_out_full, x2d, target, mod, final_g_row)

    dz, d_sinks, d_sgu_w, d_sgu_b, d_ln_g, d_ln_b, dw_out = _mixer_bwd(
        z, da, probs, sink_probs, sgu_ln_g, sgu_ln_b, sgu_w[0], jnp.swapaxes(sgu_w[0], 1, 2), sgu_bt, a, dy)
    pair_out = _pair_reduce(dw_out.reshape(4, 2, W_OUT_SHARD, D_MODEL), _every_chip, "w_out_grad_pair_reduce",
                            W_OUT_SHARD // 2)
    sgu_w_to_all = _group(_own_block_copies(_all_others), 2, 1, 3)
    both = _start_copies(
        [pair_out, lax.empty((3, W_OUT_SHARD, D_MODEL), BF16), _with_own_slot(d_sgu_w, me)],
        lambda refs, s, r: _chip_copies(refs[:2], s, r) + sgu_w_to_all(refs, s, r), 3 + N_DEV - 1, core,
        "w_out_grad_chip_and_sgu_w_gather_start")
    out_flight, sgu_w_flight = (both[0], both[1], both[2][:2], None), (both[0], both[1], both[2][2:], None)
    dw_in_t = _w_in_grad(dz, h, both[3], Z_TILE)
    pair_in = _pair_reduce(dw_in_t.reshape(4, 2, W_IN_SHARD, D_MODEL), _first_hop_chips, "w_in_grad_pair_reduce",
                           W_IN_SHARD // 3)
    first_hop = lambda refs, s, r: _first_hop_copies(refs[:2], s, r) + _group(_late_pair_copies, 2, 2, 2)(refs, s, r)
    hop1 = _start_copies(
        [pair_in, lax.empty((2, W_IN_SHARD, D_MODEL), BF16), dw_in_t.reshape(N_DEV, W_IN_SHARD, D_MODEL),
         lax.empty((2, W_IN_SHARD, D_MODEL), BF16)], first_hop, 4, core, "w_in_grad_first_hop_start")
    grad_x, d_shift, d_scale, d_norm_g = _z_proj_bwd_norm(dz, w_in_t, x2d, dx2, norm_g, mod, hop1[3])

    partial = _pack_small(d_shift, d_scale, d_gate, d_norm_g, d_final_g, d_ln_g, d_ln_b, loss_part, d_sinks, d_sgu_b)
    small_flight = _start_copies([_with_own_slot(partial, me)], _own_block_copies(_all_others), N_DEV - 1, core,
                                 "small_grad_gather_start")
    _, land_first, dw_in_t, land_pair = _wait_copies(hop1, first_hop, small_flight[3], "w_in_grad_first_hop_wait")
    second_device = (4 * ((xi + ci) % 2) + 2 * ((yi + 1 - ci) % 2) + ci).astype(jnp.int32).reshape(1)
    relay = _relay_sum(second_device, dw_in_t, land_pair, land_first, W_IN_SHARD // 3)
    hop2 = _start_copies([relay, lax.empty((1, W_IN_SHARD, D_MODEL), BF16)], _second_hop_copies, 1, core,
                         "w_in_grad_second_hop_start")
    pair_out, land_out = _wait_copies(out_flight, _chip_copies, hop2[3], "w_out_grad_chip_wait")
    big = {"w_out": _adam_from_chips(chip, pair_out, [(land_out, k) for k in range(3)], w_out[0], m_w_out[0],
                                     v_w_out[0], "adam_w_out", 1024)}
    partial_all = _wait_copies(small_flight, _own_block_copies(_all_others), big["w_out"][0],
                               "small_grad_gather_wait")[0]
    d_sgu_w_all = _wait_copies(sgu_w_flight, _group(_own_block_copies(_all_others), 0, 1, 3), partial_all,
                               "sgu_w_grad_gather_wait")[0]
    weights = {"norm_g": norm_g, "b_ada": b_ada, "attn_sinks": attn_sinks, "sgu_ln_g": sgu_ln_g,
               "sgu_ln_b": sgu_ln_b, "sgu_w": sgu_w, "sgu_b": sgu_b, "final_g": final_g_row}
    moments_m = {"norm_g": m_norm_g, "b_ada": m_b_ada, "attn_sinks": m_attn_sinks, "sgu_ln_g": m_sgu_ln_g,
                 "sgu_ln_b": m_sgu_ln_b, "sgu_w": m_sgu_w, "sgu_b": m_sgu_b,
                 "final_g": m_final_g.reshape(1, D_MODEL)}
    moments_v = {"norm_g": v_norm_g, "b_ada": v_b_ada, "attn_sinks": v_attn_sinks, "sgu_ln_g": v_sgu_ln_g,
                 "sgu_ln_b": v_sgu_ln_b, "sgu_w": v_sgu_w, "sgu_b": v_sgu_b,
                 "final_g": v_final_g.reshape(1, D_MODEL)}
    loss, dmod_all, small = _adam_small(partial_all, d_sgu_w_all, weights, moments_m, moments_v)
    small["final_g"] = tuple(o.reshape(D_MODEL) for o in small["final_g"])

    big["w_ada"] = _adam_w_ada(device, c_act.T, dmod_all, w_ada[0], m_w_ada[0], v_w_ada[0])
    _, land_second = _wait_copies(hop2, _second_hop_copies, big["w_ada"][0], "w_in_grad_second_hop_wait")
    big["w_in"] = tuple(o.T for o in _adam_from_chips(
        device, dw_in_t, [(land_pair, 0), (land_first, 0), (land_second, 0)], w_in[0].T, m_w_in[0].T, v_w_in[0].T,
        "adam_w_in", 512))
    order = ["norm_g", "w_ada", "b_ada", "w_in", "attn_sinks", "sgu_ln_g", "sgu_ln_b", "sgu_w", "sgu_b", "w_out",
             "final_g"]
    outs = [loss.reshape(()), grad_x[None]]
    for k in range(4):
        for name in order:
            outs.append(big[name][k][None] if name in big else small[name][k])
    return tuple(outs)
```
